```python
import jax, jax.numpy as jnp
from jax import lax
import numpy as np

D_MODEL = 1024
BATCH = 8
SEQ = 8192
DEPTH = 1

RET_HEADS = 8
RET_HEAD_DIM = 64
RET_WIDTH = RET_HEADS * RET_HEAD_DIM
RET_CHUNK = 128
MLA_HEADS = 8
MLA_NOPE_DIM = 64
MLA_ROPE_DIM = 32
MLA_V_DIM = 64
MLA_Q_RANK = 256
MLA_KV_RANK = 128
MLA_WIDTH = MLA_HEADS * MLA_V_DIM
MIX_WIDTH = RET_WIDTH + MLA_WIDTH
IN_WIDTH = 4 * RET_WIDTH + MLA_Q_RANK + MLA_KV_RANK + MLA_ROPE_DIM
D_FF = 2816
CONV_WIDTH = 3
Q_BLOCK = 128
ROPE_BASE = 10000.0
EPS = 1e-6

kernel_name = "hybrid_retention_mla_convffn"


def rms_norm(x, w):
    xf = x.astype(jnp.float32)
    y = xf * lax.rsqrt(jnp.mean(xf * xf, axis=-1, keepdims=True) + EPS)
    return (y * w.astype(jnp.float32)).astype(x.dtype)


def rope(x, positions):
    d = x.shape[-1]
    inv_freq = ROPE_BASE ** (-jnp.arange(0, d, 2, dtype=jnp.float32) / d)
    ang = positions.astype(jnp.float32)[..., None] * inv_freq
    if x.ndim == 4:
        ang = ang[:, :, None, :]
    cos, sin = jnp.cos(ang), jnp.sin(ang)
    xf = x.astype(jnp.float32)
    x1, x2 = xf[..., : d // 2], xf[..., d // 2:]
    return jnp.concatenate([x1 * cos - x2 * sin, x1 * sin + x2 * cos], axis=-1).astype(x.dtype)


def retention(q, k, v):
    B, S, H, dk = q.shape
    dv = v.shape[-1]
    C = RET_CHUNK
    N = S // C
    log_gamma = jnp.log1p(-jnp.power(2.0, -5.0 - jnp.arange(H, dtype=jnp.float32)))
    qc = q.astype(jnp.float32).reshape(B, N, C, H, dk)
    kc = k.astype(jnp.float32).reshape(B, N, C, H, dk)
    vc = v.astype(jnp.float32).reshape(B, N, C, H, dv)
    idx = jnp.arange(C, dtype=jnp.float32)
    diff = idx[:, None] - idx[None, :]
    decay_mask = jnp.where(diff >= 0, jnp.exp(log_gamma[:, None, None] * jnp.maximum(diff, 0.0)), 0.0)
    scores = jnp.einsum('bnihd,bnjhd->bnhij', qc, kc) * decay_mask
    o_inner = jnp.einsum('bnhij,bnjhe->bnihe', scores, vc)
    zeta = jnp.exp(log_gamma[:, None] * (C - 1.0 - idx))
    chunk_states = jnp.einsum('bnjhd,hj,bnjhe->nbhde', kc, zeta, vc)
    chunk_decay = jnp.exp(log_gamma * C)[None, :, None, None]

    def step(R, s_n):
        return chunk_decay * R + s_n, R

    _, r_prev = lax.scan(step, jnp.zeros((B, H, dk, dv), jnp.float32), chunk_states)
    r_prev = jnp.moveaxis(r_prev, 0, 1)
    xi = jnp.exp(log_gamma[:, None] * (idx + 1.0))
    o_cross = jnp.einsum('bnihd,bnhde,hi->bnihe', qc, r_prev, xi)
    return (o_inner + o_cross).reshape(B, S, H, dv)


def retention_group(q, k, v, g, positions, gn_w):
    B, S, _ = q.shape
    q = rope(q.reshape(B, S, RET_HEADS, RET_HEAD_DIM), positions)
    k = rope(k.reshape(B, S, RET_HEADS, RET_HEAD_DIM), positions) * (RET_HEAD_DIM ** -0.5)
    v = v.reshape(B, S, RET_HEADS, RET_HEAD_DIM)
    o = retention(q, k, v)
    mu = jnp.mean(o, axis=-1, keepdims=True)
    var = jnp.mean(jnp.square(o - mu), axis=-1, keepdims=True)
    o = ((o - mu) * lax.rsqrt(var + EPS)).reshape(B, S, RET_WIDTH) * gn_w.astype(jnp.float32)
    return (jax.nn.silu(g.astype(jnp.float32)) * o).astype(g.dtype)


def mla_group(c_q, c_kv, k_pe, positions, q_norm_w, w_uq, kv_norm_w, w_ukv):
    B, S, _ = c_q.shape
    H = MLA_HEADS
    q = jnp.einsum('bsr,rf->bsf', rms_norm(c_q, q_norm_w), w_uq).reshape(B, S, H, MLA_NOPE_DIM + MLA_ROPE_DIM)
    q_nope = q[..., :MLA_NOPE_DIM]
    q_pe = rope(q[..., MLA_NOPE_DIM:], positions)
    kv = jnp.einsum('bsr,rf->bsf', rms_norm(c_kv, kv_norm_w), w_ukv).reshape(B, S, H, MLA_NOPE_DIM + MLA_V_DIM)
    k_nope = kv[..., :MLA_NOPE_DIM]
    v = kv[..., MLA_NOPE_DIM:]
    k_pe = rope(k_pe, positions)
    scale = (MLA_NOPE_DIM + MLA_ROPE_DIM) ** -0.5
    N = S // Q_BLOCK
    qn_b = jnp.moveaxis(q_nope.reshape(B, N, Q_BLOCK, H, MLA_NOPE_DIM), 1, 0)
    qp_b = jnp.moveaxis(q_pe.reshape(B, N, Q_BLOCK, H, MLA_ROPE_DIM), 1, 0)
    key_pos = jnp.arange(S)
    neg = jnp.finfo(jnp.float32).min

    def block(args):
        qn, qp, blk = args
        s = (jnp.einsum('bqhd,bkhd->bhqk', qn, k_nope)
             + jnp.einsum('bqhr,bkr->bhqk', qp, k_pe)).astype(jnp.float32) * scale
        q_pos = blk * Q_BLOCK + jnp.arange(Q_BLOCK)
        s = jnp.where(key_pos[None, :] <= q_pos[:, None], s, neg)
        p = jax.nn.softmax(s, axis=-1).astype(v.dtype)
        return jnp.einsum('bhqk,bkhd->bqhd', p, v)

    o = lax.map(block, (qn_b, qp_b, jnp.arange(N)))
    return jnp.moveaxis(o, 0, 1).reshape(B, S, MLA_WIDTH)


def conv_ffn(h, w_up, conv_w, conv_b, w_down):
    S = h.shape[1]
    u = jnp.einsum('bsd,df->bsf', h, w_up)
    up = jnp.pad(u, ((0, 0), (CONV_WIDTH - 1, 0), (0, 0)))
    u = conv_b + sum(conv_w[j] * up[:, j:j + S] for j in range(CONV_WIDTH))
    gate, val = u[..., :D_FF], u[..., D_FF:]
    return jnp.einsum('bsf,fd->bsd', jax.nn.silu(gate) * val, w_down)


def _fwd_setup_inputs(seed: int = 0) -> dict:
    key = jax.random.key(seed)
    ks = jax.random.split(key, 20)
    f32 = jnp.float32

    def nrm(k, shape, fan_in):
        return jax.random.normal(k, shape, f32) * (fan_in ** -0.5)

    def gain(k, shape):
        return 1.0 + 0.02 * jax.random.normal(k, shape, f32)

    x = jax.random.normal(ks[0], (BATCH, SEQ, D_MODEL), f32)
    offset = jax.random.randint(ks[1], (BATCH, 1), 0, 4096, dtype=jnp.int32)
    positions = (offset + jnp.arange(SEQ, dtype=jnp.int32)[None, :]).astype(jnp.int32)
    return {
        "x": x,
        "positions": positions,
        "attn_norm_w": gain(ks[2], (DEPTH, D_MODEL)),
        "w_in": nrm(ks[3], (DEPTH, D_MODEL, IN_WIDTH), D_MODEL),
        "ret_gn_w": gain(ks[4], (DEPTH, RET_WIDTH)),
        "mla_q_norm_w": gain(ks[5], (DEPTH, MLA_Q_RANK)),
        "w_uq": nrm(ks[6], (DEPTH, MLA_Q_RANK, MLA_HEADS * (MLA_NOPE_DIM + MLA_ROPE_DIM)), MLA_Q_RANK),
        "mla_kv_norm_w": gain(ks[7], (DEPTH, MLA_KV_RANK)),
        "w_ukv": nrm(ks[8], (DEPTH, MLA_KV_RANK, MLA_HEADS * (MLA_NOPE_DIM + MLA_V_DIM)), MLA_KV_RANK),
        "w_out": nrm(ks[9], (DEPTH, MIX_WIDTH, D_MODEL), MIX_WIDTH),
        "ffn_norm_w": gain(ks[10], (DEPTH, D_MODEL)),
        "w_up": nrm(ks[11], (DEPTH, D_MODEL, 2 * D_FF), D_MODEL),
        "conv_w": nrm(ks[12], (DEPTH, CONV_WIDTH, 2 * D_FF), CONV_WIDTH),
        "conv_b": 0.01 * jax.random.normal(ks[13], (DEPTH, 2 * D_FF), f32),
        "w_down": nrm(ks[14], (DEPTH, D_FF, D_MODEL), D_FF),
        "final_norm_w": gain(ks[15], (D_MODEL,)),
    }


def _fwd_reference(x, positions, attn_norm_w, w_in, ret_gn_w, mla_q_norm_w, w_uq, mla_kv_norm_w, w_ukv,
              w_out, ffn_norm_w, w_up, conv_w, conv_b, w_down, final_norm_w):
    splits = np.cumsum([RET_WIDTH, RET_WIDTH, RET_WIDTH, RET_WIDTH, MLA_Q_RANK, MLA_KV_RANK]).tolist()
    for l in range(DEPTH):
        h = rms_norm(x, attn_norm_w[l])
        proj = jnp.einsum('bsd,df->bsf', h, w_in[l])
        r_q, r_k, r_v, r_g, c_q, c_kv, k_pe = jnp.split(proj, splits, axis=-1)
        y_ret = retention_group(r_q, r_k, r_v, r_g, positions, ret_gn_w[l])
        y_mla = mla_group(c_q, c_kv, k_pe, positions, mla_q_norm_w[l], w_uq[l],
                          mla_kv_norm_w[l], w_ukv[l])
        mixed = jnp.concatenate([y_ret, y_mla.astype(y_ret.dtype)], axis=-1)
        x = x + jnp.einsum('bsm,md->bsd', mixed, w_out[l])
        x = x + conv_ffn(rms_norm(x, ffn_norm_w[l]), w_up[l], conv_w[l], conv_b[l], w_down[l])
    return rms_norm(x, final_norm_w)


import jax as _jax
import jax.numpy as _jnp

TWIN_FORMAT = 'train_step'
FWD_PARAMS = ['x', 'positions', 'attn_norm_w', 'w_in', 'ret_gn_w', 'mla_q_norm_w', 'w_uq', 'mla_kv_norm_w', 'w_ukv', 'w_out', 'ffn_norm_w', 'w_up', 'conv_w', 'conv_b', 'w_down', 'final_norm_w']
TWIN_WEIGHTS = ['attn_norm_w', 'w_in', 'ret_gn_w', 'mla_q_norm_w', 'w_uq', 'mla_kv_norm_w', 'w_ukv', 'w_out', 'ffn_norm_w', 'w_up', 'conv_w', 'conv_b', 'w_down', 'final_norm_w']
TWIN_DIFF_INPUT = 'x'
TWIN_INPUTS = ['x', 'positions', 'attn_norm_w', 'w_in', 'ret_gn_w', 'mla_q_norm_w', 'w_uq', 'mla_kv_norm_w', 'w_ukv', 'w_out', 'ffn_norm_w', 'w_up', 'conv_w', 'conv_b', 'w_down', 'final_norm_w', 'loss_target', 'm_attn_norm_w', 'm_w_in', 'm_ret_gn_w', 'm_mla_q_norm_w', 'm_w_uq', 'm_mla_kv_norm_w', 'm_w_ukv', 'm_w_out', 'm_ffn_norm_w', 'm_w_up', 'm_conv_w', 'm_conv_b', 'm_w_down', 'm_final_norm_w', 'v_attn_norm_w', 'v_w_in', 'v_ret_gn_w', 'v_mla_q_norm_w', 'v_w_uq', 'v_mla_kv_norm_w', 'v_w_ukv', 'v_w_out', 'v_ffn_norm_w', 'v_w_up', 'v_conv_w', 'v_conv_b', 'v_w_down', 'v_final_norm_w']
TWIN_OUTPUTS = ['loss', 'grad_x', 'grad_attn_norm_w', 'grad_w_in', 'grad_ret_gn_w', 'grad_mla_q_norm_w', 'grad_w_uq', 'grad_mla_kv_norm_w', 'grad_w_ukv', 'grad_w_out', 'grad_ffn_norm_w', 'grad_w_up', 'grad_conv_w', 'grad_conv_b', 'grad_w_down', 'grad_final_norm_w', 'delta_attn_norm_w', 'delta_w_in', 'delta_ret_gn_w', 'delta_mla_q_norm_w', 'delta_w_uq', 'delta_mla_kv_norm_w', 'delta_w_ukv', 'delta_w_out', 'delta_ffn_norm_w', 'delta_w_up', 'delta_conv_w', 'delta_conv_b', 'delta_w_down', 'delta_final_norm_w', 'new_m_attn_norm_w', 'new_m_w_in', 'new_m_ret_gn_w', 'new_m_mla_q_norm_w', 'new_m_w_uq', 'new_m_mla_kv_norm_w', 'new_m_w_ukv', 'new_m_w_out', 'new_m_ffn_norm_w', 'new_m_w_up', 'new_m_conv_w', 'new_m_conv_b', 'new_m_w_down', 'new_m_final_norm_w', 'new_v_attn_norm_w', 'new_v_w_in', 'new_v_ret_gn_w', 'new_v_mla_q_norm_w', 'new_v_w_uq', 'new_v_mla_kv_norm_w', 'new_v_w_ukv', 'new_v_w_out', 'new_v_ffn_norm_w', 'new_v_w_up', 'new_v_conv_w', 'new_v_conv_b', 'new_v_w_down', 'new_v_final_norm_w']
TWIN_LEAF_KINDS = {'loss': 'loss', 'grad_x': 'grad_x', 'grad_attn_norm_w': 'grad_w', 'grad_w_in': 'grad_w', 'grad_ret_gn_w': 'grad_w', 'grad_mla_q_norm_w': 'grad_w', 'grad_w_uq': 'grad_w', 'grad_mla_kv_norm_w': 'grad_w', 'grad_w_ukv': 'grad_w', 'grad_w_out': 'grad_w', 'grad_ffn_norm_w': 'grad_w', 'grad_w_up': 'grad_w', 'grad_conv_w': 'grad_w', 'grad_conv_b': 'grad_w', 'grad_w_down': 'grad_w', 'grad_final_norm_w': 'grad_w', 'delta_attn_norm_w': 'delta_w', 'delta_w_in': 'delta_w', 'delta_ret_gn_w': 'delta_w', 'delta_mla_q_norm_w': 'delta_w', 'delta_w_uq': 'delta_w', 'delta_mla_kv_norm_w': 'delta_w', 'delta_w_ukv': 'delta_w', 'delta_w_out': 'delta_w', 'delta_ffn_norm_w': 'delta_w', 'delta_w_up': 'delta_w', 'delta_conv_w': 'delta_w', 'delta_conv_b': 'delta_w', 'delta_w_down': 'delta_w', 'delta_final_norm_w': 'delta_w', 'new_m_attn_norm_w': 'new_m', 'new_m_w_in': 'new_m', 'new_m_ret_gn_w': 'new_m', 'new_m_mla_q_norm_w': 'new_m', 'new_m_w_uq': 'new_m', 'new_m_mla_kv_norm_w': 'new_m', 'new_m_w_ukv': 'new_m', 'new_m_w_out': 'new_m', 'new_m_ffn_norm_w': 'new_m', 'new_m_w_up': 'new_m', 'new_m_conv_w': 'new_m', 'new_m_conv_b': 'new_m', 'new_m_w_down': 'new_m', 'new_m_final_norm_w': 'new_m', 'new_v_attn_norm_w': 'new_v', 'new_v_w_in': 'new_v', 'new_v_ret_gn_w': 'new_v', 'new_v_mla_q_norm_w': 'new_v', 'new_v_w_uq': 'new_v', 'new_v_mla_kv_norm_w': 'new_v', 'new_v_w_ukv': 'new_v', 'new_v_w_out': 'new_v', 'new_v_ffn_norm_w': 'new_v', 'new_v_w_up': 'new_v', 'new_v_conv_w': 'new_v', 'new_v_conv_b': 'new_v', 'new_v_w_down': 'new_v', 'new_v_final_norm_w': 'new_v'}


def _forward(args):
    return _fwd_reference(*[args[k] for k in FWD_PARAMS])


def _output_shape():
    def fwd():
        inp = _fwd_setup_inputs(0)
        return _fwd_reference(*[inp[k] for k in FWD_PARAMS])
    out = _jax.eval_shape(fwd)
    return out.shape, out.dtype

N_MICROBATCH = 1
ADAM_LR = 0.001
ADAM_B1 = 0.9
ADAM_B2 = 0.999
ADAM_EPS = 1e-08
ADAM_WD = 0.01
ADAM_STEP = 10
PER_EXAMPLE_BATCH_AXIS = {'x': 0, 'positions': 0, 'loss_target': 0}
SHARED_INPUTS = []
_WEIGHT_DTYPES = {'attn_norm_w': _jnp.float32, 'w_in': _jnp.float32, 'ret_gn_w': _jnp.float32, 'mla_q_norm_w': _jnp.float32, 'w_uq': _jnp.float32, 'mla_kv_norm_w': _jnp.float32, 'w_ukv': _jnp.float32, 'w_out': _jnp.float32, 'ffn_norm_w': _jnp.float32, 'w_up': _jnp.float32, 'conv_w': _jnp.float32, 'conv_b': _jnp.float32, 'w_down': _jnp.float32, 'final_norm_w': _jnp.float32}
MOMENT_SCALE = {'attn_norm_w': 2.372195e-01, 'w_in': 1.469731e-01, 'ret_gn_w': 1.766269e-01, 'mla_q_norm_w': 6.277568e-02, 'w_uq': 3.854965e-02, 'mla_kv_norm_w': 1.440009e-01, 'w_ukv': 4.796977e-02, 'w_out': 1.147093e-01, 'ffn_norm_w': 1.769606e-01, 'w_up': 7.556273e-02, 'conv_w': 7.752227e-02, 'conv_b': 7.726549e-02, 'w_down': 1.231143e-01, 'final_norm_w': 6.393243e+01}


def _to_microbatches(a, axis):
    t = _jnp.moveaxis(a, axis, 0)
    t = t.reshape((N_MICROBATCH, t.shape[0] // N_MICROBATCH) + t.shape[1:])
    return _jnp.moveaxis(t, 1, axis + 1)


def setup_inputs(seed: int = 0) -> dict:
    inp = _fwd_setup_inputs(seed)
    key = _jax.random.fold_in(_jax.random.key(seed), 7919)
    shape, _ = _output_shape()
    out = dict(inp)
    out["loss_target"] = _jax.random.normal(_jax.random.fold_in(key, 0), shape, _jnp.float32)
    for i, name in enumerate(TWIN_WEIGHTS):
        w = inp[name].astype(_jnp.float32)
        if MOMENT_SCALE is None:
            s = _jnp.sqrt(_jnp.mean(_jnp.square(w)) + 1e-30)
        else:
            s = MOMENT_SCALE[name]
        km, kv = _jax.random.split(_jax.random.fold_in(key, i + 1))
        out[name] = w
        out["m_" + name] = s * _jax.random.normal(km, w.shape, _jnp.float32)
        out["v_" + name] = (s * s) * _jax.random.uniform(kv, w.shape, _jnp.float32, 0.5, 1.5)
    if N_MICROBATCH > 1:
        for name, axis in PER_EXAMPLE_BATCH_AXIS.items():
            out[name] = _to_microbatches(out[name], axis)
    return {'x': out['x'], 'positions': out['positions'], 'attn_norm_w': out['attn_norm_w'], 'w_in': out['w_in'], 'ret_gn_w': out['ret_gn_w'], 'mla_q_norm_w': out['mla_q_norm_w'], 'w_uq': out['w_uq'], 'mla_kv_norm_w': out['mla_kv_norm_w'], 'w_ukv': out['w_ukv'], 'w_out': out['w_out'], 'ffn_norm_w': out['ffn_norm_w'], 'w_up': out['w_up'], 'conv_w': out['conv_w'], 'conv_b': out['conv_b'], 'w_down': out['w_down'], 'final_norm_w': out['final_norm_w'], 'loss_target': out['loss_target'], 'm_attn_norm_w': out['m_attn_norm_w'], 'm_w_in': out['m_w_in'], 'm_ret_gn_w': out['m_ret_gn_w'], 'm_mla_q_norm_w': out['m_mla_q_norm_w'], 'm_w_uq': out['m_w_uq'], 'm_mla_kv_norm_w': out['m_mla_kv_norm_w'], 'm_w_ukv': out['m_w_ukv'], 'm_w_out': out['m_w_out'], 'm_ffn_norm_w': out['m_ffn_norm_w'], 'm_w_up': out['m_w_up'], 'm_conv_w': out['m_conv_w'], 'm_conv_b': out['m_conv_b'], 'm_w_down': out['m_w_down'], 'm_final_norm_w': out['m_final_norm_w'], 'v_attn_norm_w': out['v_attn_norm_w'], 'v_w_in': out['v_w_in'], 'v_ret_gn_w': out['v_ret_gn_w'], 'v_mla_q_norm_w': out['v_mla_q_norm_w'], 'v_w_uq': out['v_w_uq'], 'v_mla_kv_norm_w': out['v_mla_kv_norm_w'], 'v_w_ukv': out['v_w_ukv'], 'v_w_out': out['v_w_out'], 'v_ffn_norm_w': out['v_ffn_norm_w'], 'v_w_up': out['v_w_up'], 'v_conv_w': out['v_conv_w'], 'v_conv_b': out['v_conv_b'], 'v_w_down': out['v_w_down'], 'v_final_norm_w': out['v_final_norm_w']}


def _loss(weights, diff, rest, loss_target):
    with _jax.named_scope("forward"):
        args = {**rest, TWIN_DIFF_INPUT: diff, **{k: w.astype(_WEIGHT_DTYPES[k]) for k, w in weights.items()}}
        y = _forward(args)
    with _jax.named_scope("loss_head"):
        err = _jnp.square(y.astype(_jnp.float32) - loss_target)
        return 0.5 * _jnp.sum(_jnp.mean(err, axis=-1)) if err.ndim else 0.5 * err


def _adamw(w, g, m, v):
    m = ADAM_B1 * m + (1.0 - ADAM_B1) * g
    v = ADAM_B2 * v + (1.0 - ADAM_B2) * _jnp.square(g)
    m_hat = m / (1.0 - ADAM_B1 ** ADAM_STEP)
    v_hat = v / (1.0 - ADAM_B2 ** ADAM_STEP)
    delta = -ADAM_LR * (m_hat / (_jnp.sqrt(v_hat) + ADAM_EPS) + ADAM_WD * w)
    return delta, m, v


def reference(x, positions, attn_norm_w, w_in, ret_gn_w, mla_q_norm_w, w_uq, mla_kv_norm_w, w_ukv, w_out, ffn_norm_w, w_up, conv_w, conv_b, w_down, final_norm_w, loss_target, m_attn_norm_w, m_w_in, m_ret_gn_w, m_mla_q_norm_w, m_w_uq, m_mla_kv_norm_w, m_w_ukv, m_w_out, m_ffn_norm_w, m_w_up, m_conv_w, m_conv_b, m_w_down, m_final_norm_w, v_attn_norm_w, v_w_in, v_ret_gn_w, v_mla_q_norm_w, v_w_uq, v_mla_kv_norm_w, v_w_ukv, v_w_out, v_ffn_norm_w, v_w_up, v_conv_w, v_conv_b, v_w_down, v_final_norm_w):
    given = dict(x=x, positions=positions, attn_norm_w=attn_norm_w, w_in=w_in, ret_gn_w=ret_gn_w, mla_q_norm_w=mla_q_norm_w, w_uq=w_uq, mla_kv_norm_w=mla_kv_norm_w, w_ukv=w_ukv, w_out=w_out, ffn_norm_w=ffn_norm_w, w_up=w_up, conv_w=conv_w, conv_b=conv_b, w_down=w_down, final_norm_w=final_norm_w, loss_target=loss_target, m_attn_norm_w=m_attn_norm_w, m_w_in=m_w_in, m_ret_gn_w=m_ret_gn_w, m_mla_q_norm_w=m_mla_q_norm_w, m_w_uq=m_w_uq, m_mla_kv_norm_w=m_mla_kv_norm_w, m_w_ukv=m_w_ukv, m_w_out=m_w_out, m_ffn_norm_w=m_ffn_norm_w, m_w_up=m_w_up, m_conv_w=m_conv_w, m_conv_b=m_conv_b, m_w_down=m_w_down, m_final_norm_w=m_final_norm_w, v_attn_norm_w=v_attn_norm_w, v_w_in=v_w_in, v_ret_gn_w=v_ret_gn_w, v_mla_q_norm_w=v_mla_q_norm_w, v_w_uq=v_w_uq, v_mla_kv_norm_w=v_mla_kv_norm_w, v_w_ukv=v_w_ukv, v_w_out=v_w_out, v_ffn_norm_w=v_ffn_norm_w, v_w_up=v_w_up, v_conv_w=v_conv_w, v_conv_b=v_conv_b, v_w_down=v_w_down, v_final_norm_w=v_final_norm_w)
    weights = {n: given[n] for n in TWIN_WEIGHTS}
    shared = {n: given[n] for n in SHARED_INPUTS}
    per_example = {n: given[n] for n in ['x', 'positions']}
    grad_fn = _jax.value_and_grad(_loss, argnums=(0, 1))

    def one_microbatch(ex, loss_target):
        ex = dict(ex)
        diff = ex.pop(TWIN_DIFF_INPUT)
        return grad_fn(weights, diff, {**shared, **ex}, loss_target)

    if N_MICROBATCH == 1:
        loss, (grad_w, grad_x) = one_microbatch(per_example, given["loss_target"])
    else:
        def body(carry, xs):
            loss_sum, grad_sum = carry
            l_k, (gw_k, gx_k) = one_microbatch(xs[0], xs[1])
            with _jax.named_scope("update"):
                return (loss_sum + l_k, _jax.tree.map(_jnp.add, grad_sum, gw_k)), gx_k

        init = (_jnp.zeros((), _jnp.float32), _jax.tree.map(_jnp.zeros_like, weights))
        (loss, grad_w), grad_x = _jax.lax.scan(body, init, (per_example, given["loss_target"]))
    with _jax.named_scope("update"):
        delta_w, new_m, new_v = {}, {}, {}
        for n in TWIN_WEIGHTS:
            delta_w[n], new_m[n], new_v[n] = _adamw(weights[n], grad_w[n], given["m_" + n], given["v_" + n])
    return (loss, grad_x, *[grad_w[n] for n in TWIN_WEIGHTS], *[delta_w[n] for n in TWIN_WEIGHTS],
            *[new_m[n] for n in TWIN_WEIGHTS], *[new_v[n] for n in TWIN_WEIGHTS])
```

```python
import functools
import math

import numpy as np
import jax
import jax.numpy as jnp
from jax import lax
from jax.experimental import pallas as pl
from jax.experimental.pallas import tpu as pltpu

F32 = jnp.float32
BF16 = jnp.bfloat16

D_MODEL = 1024
N_HEADS = 8
HEAD = 64
RET_W = N_HEADS * HEAD
MLA_W = N_HEADS * HEAD
ROPE = 32
Q_RANK = 256
KV_RANK = 128
D_FF = 2816
F2 = 2 * D_FF
IN_W = 4 * RET_W + Q_RANK + KV_RANK + ROPE
IN_EXT = 4 * RET_W + Q_RANK + KV_RANK + 128
KPE_LO = 64
ROPE_BASE = 10000.0
EPS = 1e-6
RET_CHUNK = 128
SM_SCALE = (HEAD + ROPE) ** -0.5
NEG = -1e30
LANES = 128
VMEM_LIMIT = 56 * 1024 * 1024

ADAM_LR = 0.001
ADAM_B1 = 0.9
ADAM_B2 = 0.999
ADAM_EPS = 1e-08
ADAM_WD = 0.01
ADAM_STEP = 10


def _cp(*sem):
    return pltpu.CompilerParams(dimension_semantics=sem, vmem_limit_bytes=VMEM_LIMIT)


def _full(shape):
    n = len(shape)
    return pl.BlockSpec(tuple(shape), lambda *_: (0,) * n)


def _row(ts, c):
    return pl.BlockSpec((ts, c), lambda i: (i, 0))


def _hrow(h, ts, c):
    return pl.BlockSpec((h, ts, c), lambda i: (0, i, 0))


def _dot(a, b):
    return jnp.dot(a, b, preferred_element_type=F32)


def _dot_nt(a, b):
    return lax.dot_general(a, b, (((1,), (1,)), ((), ())), preferred_element_type=F32)


def _dot_tn(a, b):
    return lax.dot_general(a, b, (((0,), (0,)), ((), ())), preferred_element_type=F32)


def _dot_hi(a, b):
    return jnp.dot(a, b, preferred_element_type=F32, precision=lax.Precision.HIGHEST)


def _rot_half(x, half):
    w = x.shape[-1]
    lane = lax.broadcasted_iota(jnp.int32, x.shape, x.ndim - 1)
    first = (lane % (2 * half)) < half
    return jnp.where(first, -pltpu.roll(x, w - half, x.ndim - 1), pltpu.roll(x, half, x.ndim - 1))


def _rope(x, cos, sin, half):
    return x * cos + _rot_half(x, half) * sin


def _unrope(dy, cos, sin, half):
    return dy * cos - _rot_half(dy, half) * sin


def _silu(g):
    return g / (1.0 + jnp.exp(-g))


def _rstd(x):
    return lax.rsqrt(jnp.mean(x * x, axis=-1, keepdims=True) + EPS)


def _rope_tables(positions):
    pos = positions.astype(F32)[:, None]
    s = pos.shape[0]
    inv = ROPE_BASE ** (-jnp.arange(0, HEAD, 2, dtype=F32) / HEAD)
    ang = pos * inv
    c, sn = jnp.cos(ang), jnp.sin(ang)
    cos_r = jnp.tile(jnp.concatenate([c, c], -1), (1, 2))
    sin_r = jnp.tile(jnp.concatenate([sn, sn], -1), (1, 2))
    inv = ROPE_BASE ** (-jnp.arange(0, ROPE, 2, dtype=F32) / ROPE)
    ang = pos * inv
    c, sn = jnp.cos(ang), jnp.sin(ang)
    one, zero = jnp.ones((s, KPE_LO), F32), jnp.zeros((s, KPE_LO), F32)
    cos_m = jnp.concatenate([one, c, c, one[:, :LANES - KPE_LO - ROPE]], -1)
    sin_m = jnp.concatenate([zero, sn, sn, zero[:, :LANES - KPE_LO - ROPE]], -1)
    return cos_r, sin_r, cos_m, sin_m


def _ret_consts():
    c = RET_CHUNK
    lg = np.log1p(-np.power(2.0, -5.0 - np.arange(N_HEADS, dtype=np.float64)))
    idx = np.arange(c, dtype=np.float64)
    diff = idx[:, None] - idx[None, :]
    lane_head = np.arange(LANES) // HEAD
    dmask = np.zeros((4, 2, c, c))
    zeta = np.zeros((4, c, LANES))
    xi = np.zeros((4, c, LANES))
    cd = np.zeros((4, LANES, LANES))
    bd = (lane_head[:, None] == lane_head[None, :]).astype(np.float64)
    for j in range(4):
        for hh in range(2):
            dmask[j, hh] = np.where(diff >= 0, np.exp(lg[2 * j + hh] * np.maximum(diff, 0.0)), 0.0)
        lgl = lg[2 * j + lane_head]
        zeta[j] = np.exp(lgl[None, :] * (c - 1.0 - idx[:, None]))
        xi[j] = np.exp(lgl[None, :] * (idx[:, None] + 1.0))
        cd[j] = np.exp(lgl * c)[:, None] * bd
    f = lambda a: jnp.asarray(a, F32)
    return dict(dmask=f(dmask), dmask_t=f(np.swapaxes(dmask, 2, 3)), zeta=f(zeta), xi=f(xi), cd=f(cd), bd=f(bd))


def _f1_call(x, anw, win, cos_r, sin_r, cos_m, sin_m, ts):
    s = x.shape[0]

    def body(x_ref, anw_ref, w_ref, cr_ref, sr_ref, cm_ref, sm_ref,
             q_ref, k_ref, v_ref, g_ref, cq_ref, ckv_ref, kpe_ref, r_ref):
        xv = x_ref[...]
        r = _rstd(xv)
        r_ref[...] = r
        h = (xv * r * anw_ref[...]).astype(BF16)
        cr, sr = cr_ref[...], sr_ref[...]
        qk = _dot(h, w_ref[:, 0:2 * RET_W])
        for j in range(4):
            sl = slice(j * LANES, (j + 1) * LANES)
            q_ref[:, sl] = _rope(qk[:, sl], cr, sr, HEAD // 2).astype(BF16)
            kk = qk[:, RET_W + j * LANES:RET_W + (j + 1) * LANES]
            k_ref[:, sl] = (_rope(kk, cr, sr, HEAD // 2) * (HEAD ** -0.5)).astype(BF16)
        v_ref[...] = _dot(h, w_ref[:, 2 * RET_W:3 * RET_W]).astype(BF16)
        g_ref[...] = _dot(h, w_ref[:, 3 * RET_W:4 * RET_W])
        o = 4 * RET_W
        cq_ref[...] = _dot(h, w_ref[:, o:o + Q_RANK])
        ckv_ref[...] = _dot(h, w_ref[:, o + Q_RANK:o + Q_RANK + KV_RANK])
        kp = _dot(h, w_ref[:, o + Q_RANK + KV_RANK:IN_EXT])
        kpe_ref[...] = _rope(kp, cm_ref[...], sm_ref[...], ROPE // 2)

    sd = jax.ShapeDtypeStruct
    return pl.pallas_call(
        body, name="f1_in_proj", grid=(s // ts,),
        in_specs=[_row(ts, D_MODEL), _full((1, D_MODEL)), _full((D_MODEL, IN_EXT)),
                  _row(ts, LANES), _row(ts, LANES), _row(ts, LANES), _row(ts, LANES)],
        out_specs=[_row(ts, RET_W), _row(ts, RET_W), _row(ts, RET_W), _row(ts, RET_W),
                   _row(ts, Q_RANK), _row(ts, KV_RANK), _row(ts, LANES), _row(ts, 1)],
        out_shape=[sd((s, RET_W), BF16), sd((s, RET_W), BF16), sd((s, RET_W), BF16), sd((s, RET_W), F32),
                   sd((s, Q_RANK), F32), sd((s, KV_RANK), F32), sd((s, LANES), F32), sd((s, 1), F32)],
        compiler_params=_cp("parallel"),
    )(x, anw, win, cos_r, sin_r, cos_m, sin_m)


def _ret_fwd_call(q, k, v, g, gnw, rc, tr):
    s = q.shape[0]
    c = RET_CHUNK
    nc = tr // c

    def body(q_ref, k_ref, v_ref, g_ref, gnw_ref, dm_ref, zeta_ref, xi_ref, cd_ref, bd_ref, o_ref, y_ref, st_ref):
        @pl.when(pl.program_id(1) == 0)
        def _():
            st_ref[...] = jnp.zeros_like(st_ref)

        lane = lax.broadcasted_iota(jnp.int32, (c, LANES), 1)
        bd = bd_ref[...]
        for ci in range(nc):
            rows = slice(ci * c, (ci + 1) * c)
            qc, kc, vc = q_ref[rows, :], k_ref[rows, :], v_ref[rows, :]
            st = st_ref[...]
            o = _dot(qc, st.astype(BF16)) * xi_ref[0]
            for hh in range(2):
                m = (lane >= HEAD) if hh else (lane < HEAD)
                sc = _dot_nt(jnp.where(m, qc, jnp.zeros_like(qc)), kc) * dm_ref[0, hh]
                o = o + jnp.where(m, _dot(sc.astype(BF16), vc), 0.0)
            kz = (kc.astype(F32) * zeta_ref[0]).astype(BF16)
            st_ref[...] = st * cd_ref[0] + _dot_tn(kz, vc) * bd
            o_ref[rows, :] = o
        o = o_ref[...]
        avg = bd * (1.0 / HEAD)
        ctr = o - _dot_hi(o, avg)
        var = _dot_hi(ctr * ctr, avg)
        y_ref[...] = (_silu(g_ref[...]) * (ctr * lax.rsqrt(var + EPS) * gnw_ref[...])).astype(BF16)

    slab = pl.BlockSpec((tr, LANES), lambda j, i: (i, j))
    sd = jax.ShapeDtypeStruct
    return pl.pallas_call(
        body, name="ret_fwd", grid=(4, s // tr),
        in_specs=[slab, slab, slab, slab, pl.BlockSpec((1, LANES), lambda j, i: (0, j)),
                  pl.BlockSpec((1, 2, c, c), lambda j, i: (j, 0, 0, 0)),
                  pl.BlockSpec((1, c, LANES), lambda j, i: (j, 0, 0)),
                  pl.BlockSpec((1, c, LANES), lambda j, i: (j, 0, 0)),
                  pl.BlockSpec((1, LANES, LANES), lambda j, i: (j, 0, 0)),
                  pl.BlockSpec((LANES, LANES), lambda j, i: (0, 0))],
        out_specs=[slab, slab],
        out_shape=[sd((s, RET_W), F32), sd((s, RET_W), BF16)],
        scratch_shapes=[pltpu.VMEM((LANES, LANES), F32)],
        compiler_params=_cp("parallel", "arbitrary"),
    )(q, k, v, g, gnw, rc["dmask"], rc["zeta"], rc["xi"], rc["cd"], rc["bd"])


def _mla_pre_call(cq, ckv, kpe, qnw, kvnw, wq, wk, wv, cos_m, sin_m, ts):
    s = cq.shape[0]

    def body(cq_ref, ckv_ref, kpe_ref, qnw_ref, kvnw_ref, wq_ref, wk_ref, wv_ref, cm_ref, sm_ref, q_ref, k_ref, v_ref):
        cqv, ckvv = cq_ref[...], ckv_ref[...]
        cqn = (cqv * _rstd(cqv) * qnw_ref[...]).astype(BF16)
        ckvn = (ckvv * _rstd(ckvv) * kvnw_ref[...]).astype(BF16)
        cm, sm, kp = cm_ref[...], sm_ref[...], kpe_ref[...]
        for h in range(N_HEADS):
            qh = _rope(_dot(cqn, wq_ref[h]), cm, sm, ROPE // 2)
            q_ref[h] = (qh * SM_SCALE).astype(BF16)
            k_ref[h] = (_dot(ckvn, wk_ref[h]) + kp).astype(BF16)
            v_ref[h] = _dot(ckvn, wv_ref[h]).astype(BF16)

    sd = jax.ShapeDtypeStruct
    hm = sd((N_HEADS, s, LANES), BF16)
    return pl.pallas_call(
        body, name="mla_pre", grid=(s // ts,),
        in_specs=[_row(ts, Q_RANK), _row(ts, KV_RANK), _row(ts, LANES), _full((1, Q_RANK)), _full((1, KV_RANK)),
                  _full((N_HEADS, Q_RANK, LANES)), _full((N_HEADS, KV_RANK, LANES)), _full((N_HEADS, KV_RANK, LANES)),
                  _row(ts, LANES), _row(ts, LANES)],
        out_specs=[_hrow(N_HEADS, ts, LANES)] * 3,
        out_shape=[hm, hm, hm],
        compiler_params=_cp("parallel"),
    )(cq, ckv, kpe, qnw, kvnw, wq, wk, wv, cos_m, sin_m)


def _head_lane_mask(shape, hh):
    lane = lax.broadcasted_iota(jnp.int32, shape, 1)
    return (lane >= HEAD) if hh else (lane < HEAD)


def _flash_fwd_call(q, k, v, tb):
    s = q.shape[1]
    nb = s // tb

    def body(q_ref, k_ref, v_ref, o_ref, lse_ref, m_ref, l_ref, acc_ref):
        qi, ki = pl.program_id(0), pl.program_id(1)

        @pl.when(ki == 0)
        def _():
            m_ref[...] = jnp.full_like(m_ref, NEG)
            l_ref[...] = jnp.zeros_like(l_ref)
            acc_ref[...] = jnp.zeros_like(acc_ref)

        @pl.when(ki <= qi)
        def _():
            row = qi * tb + lax.broadcasted_iota(jnp.int32, (tb, tb), 0)
            col = ki * tb + lax.broadcasted_iota(jnp.int32, (tb, tb), 1)
            keep = col <= row
            for p in range(N_HEADS // 2):
                sl = slice(p * LANES, (p + 1) * LANES)
                accp = acc_ref[:, sl]
                for hh in range(2):
                    h = 2 * p + hh
                    sc = jnp.where(keep, _dot_nt(q_ref[h], k_ref[h]), NEG)
                    m_prev = m_ref[h]
                    m_new = jnp.maximum(m_prev, jnp.max(sc, axis=1, keepdims=True))
                    pe = jnp.exp(sc - m_new)
                    alpha = jnp.exp(m_prev - m_new)
                    l_ref[h] = alpha * l_ref[h] + jnp.sum(pe, axis=1, keepdims=True)
                    m_ref[h] = m_new
                    hm = _head_lane_mask((tb, LANES), hh)
                    accp = jnp.where(hm, accp * alpha, accp) + _dot(pe.astype(BF16), v_ref[h])
                acc_ref[:, sl] = accp

        @pl.when(ki == qi)
        def _():
            lane = lax.broadcasted_iota(jnp.int32, (tb, LANES), 1)
            lse = jnp.zeros((tb, LANES), F32)
            for p in range(N_HEADS // 2):
                sl = slice(p * LANES, (p + 1) * LANES)
                accp = acc_ref[:, sl]
                inv = jnp.where(_head_lane_mask((tb, LANES), 1), 1.0 / l_ref[2 * p + 1], 1.0 / l_ref[2 * p])
                o_ref[:, sl] = (accp * inv).astype(BF16)
                for hh in range(2):
                    h = 2 * p + hh
                    lse = jnp.where(lane // 16 == h, m_ref[h] + jnp.log(l_ref[h]), lse)
            lse_ref[...] = lse

    sd = jax.ShapeDtypeStruct
    qspec = pl.BlockSpec((N_HEADS, tb, LANES), lambda qi, ki: (0, qi, 0))
    kspec = pl.BlockSpec((N_HEADS, tb, LANES), lambda qi, ki: (0, jnp.minimum(ki, qi), 0))
    return pl.pallas_call(
        body, name="mla_flash_fwd", grid=(nb, nb),
        in_specs=[qspec, kspec, kspec],
        out_specs=[pl.BlockSpec((tb, MLA_W), lambda qi, ki: (qi, 0)), pl.BlockSpec((tb, LANES), lambda qi, ki: (qi, 0))],
        out_shape=[sd((s, MLA_W), BF16), sd((s, LANES), F32)],
        scratch_shapes=[pltpu.VMEM((N_HEADS, tb, 1), F32), pltpu.VMEM((N_HEADS, tb, 1), F32), pltpu.VMEM((tb, MLA_W), F32)],
        compiler_params=_cp("parallel", "arbitrary"),
    )(q, k, v)


def _out_proj_call(x, yret, ymla, wout, ts):
    s = x.shape[0]

    def body(x_ref, yr_ref, ym_ref, w_ref, x1_ref, r_ref):
        x1 = x_ref[...] + _dot(yr_ref[...], w_ref[0:RET_W, :]) + _dot(ym_ref[...], w_ref[RET_W:, :])
        x1_ref[...] = x1
        r_ref[...] = _rstd(x1)

    sd = jax.ShapeDtypeStruct
    return pl.pallas_call(
        body, name="out_proj", grid=(s // ts,),
        in_specs=[_row(ts, D_MODEL), _row(ts, RET_W), _row(ts, MLA_W), _full((D_MODEL, D_MODEL))],
        out_specs=[_row(ts, D_MODEL), _row(ts, 1)],
        out_shape=[sd((s, D_MODEL), F32), sd((s, 1), F32)],
        compiler_params=_cp("parallel"),
    )(x, yret, ymla, wout)


def _up_proj_call(x1, r2, fnw, wup, ts):
    s = x1.shape[0]
    nchunk = 4
    cw = F2 // nchunk

    def body(x_ref, r_ref, fnw_ref, w_ref, u_ref):
        h = (x_ref[...] * r_ref[...] * fnw_ref[...]).astype(BF16)
        for j in range(nchunk):
            u_ref[:, j * cw:(j + 1) * cw] = _dot(h, w_ref[:, j * cw:(j + 1) * cw]).astype(BF16)

    return pl.pallas_call(
        body, name="up_proj", grid=(s // ts,),
        in_specs=[_row(ts, D_MODEL), _row(ts, 1), _full((1, D_MODEL)), _full((D_MODEL, F2))],
        out_specs=_row(ts, F2),
        out_shape=jax.ShapeDtypeStruct((s, F2), BF16),
        compiler_params=_cp("parallel"),
    )(x1, r2, fnw, wup)


def _shifted(u, hal):
    row = lax.broadcasted_iota(jnp.int32, u.shape, 0)
    u1 = jnp.where(row == 0, hal[7:8, :], pltpu.roll(u, 1, 0))
    u2 = jnp.where(row == 0, hal[6:7, :], jnp.where(row == 1, hal[7:8, :], pltpu.roll(u, 2, 0)))
    return u1, u2


def _conv_tile(u_ref, hal_ref, w_ref, b_ref, first):
    u = u_ref[...].astype(F32)
    hal = jnp.where(first, 0.0, hal_ref[...].astype(F32))
    u1, u2 = _shifted(u, hal)
    w = w_ref[...]
    return b_ref[...] + w[0:1, :] * u2 + w[1:2, :] * u1 + w[2:3, :] * u, u1, u2, u


def _gate_specs(ts, tf, rows_inner=False):
    nf = D_FF // tf
    hb = ts // 8

    def spec(shape, fn):
        return pl.BlockSpec(shape, (lambda j, i: fn(i, j)) if rows_inner else fn)

    return [
        spec((ts, tf), lambda i, j: (i, j)),
        spec((8, tf), lambda i, j: (jnp.maximum(i * hb - 1, 0), j)),
        spec((ts, tf), lambda i, j: (i, j + nf)),
        spec((8, tf), lambda i, j: (jnp.maximum(i * hb - 1, 0), j + nf)),
        spec((3, tf), lambda i, j: (0, j)),
        spec((3, tf), lambda i, j: (0, j + nf)),
        spec((1, tf), lambda i, j: (0, j)),
        spec((1, tf), lambda i, j: (0, j + nf)),
    ]


def _gate_call(u, cw, cb, ts, tf):
    s = u.shape[0]

    def body(ug_ref, hg_ref, uv_ref, hv_ref, wg_ref, wv_ref, bg_ref, bv_ref, a_ref):
        first = pl.program_id(0) == 0
        gate = _conv_tile(ug_ref, hg_ref, wg_ref, bg_ref, first)[0]
        val = _conv_tile(uv_ref, hv_ref, wv_ref, bv_ref, first)[0]
        a_ref[...] = (_silu(gate) * val).astype(BF16)

    return pl.pallas_call(
        body, name="conv_gate", grid=(s // ts, D_FF // tf),
        in_specs=_gate_specs(ts, tf),
        out_specs=pl.BlockSpec((ts, tf), lambda i, j: (i, j)),
        out_shape=jax.ShapeDtypeStruct((s, D_FF), BF16),
        compiler_params=_cp("parallel", "parallel"),
    )(u, u, u, u, cw, cw, cb, cb)


def _down_proj_call(x1, a, wdown, ts):
    s = x1.shape[0]

    def body(x_ref, a_ref, w_ref, x2_ref):
        x2_ref[...] = x_ref[...] + _dot(a_ref[...], w_ref[...])

    return pl.pallas_call(
        body, name="down_proj", grid=(s // ts,),
        in_specs=[_row(ts, D_MODEL), _row(ts, D_FF), _full((D_FF, D_MODEL))],
        out_specs=_row(ts, D_MODEL),
        out_shape=jax.ShapeDtypeStruct((s, D_MODEL), F32),
        compiler_params=_cp("parallel"),
    )(x1, a, wdown)


def _prep_weights(w):
    win = w["w_in"]
    pad = lambda n: jnp.zeros((D_MODEL, n), win.dtype)
    win_ext = jnp.concatenate([win[:, :IN_W - ROPE], pad(KPE_LO), win[:, IN_W - ROPE:], pad(LANES - KPE_LO - ROPE)], -1)
    wuq = w["w_uq"].reshape(Q_RANK, N_HEADS, HEAD + ROPE)
    wq = jnp.concatenate([wuq, jnp.zeros((Q_RANK, N_HEADS, LANES - HEAD - ROPE), wuq.dtype)], -1).transpose(1, 0, 2)
    wukv = w["w_ukv"].reshape(KV_RANK, N_HEADS, 2 * HEAD)
    zk = jnp.zeros((KV_RANK, N_HEADS, HEAD), wukv.dtype)
    wk = jnp.concatenate([wukv[:, :, :HEAD], zk], -1).transpose(1, 0, 2)
    wv_e = jnp.concatenate([wukv[:, :, HEAD:], zk], -1)
    wv_o = jnp.concatenate([zk, wukv[:, :, HEAD:]], -1)
    odd = (jnp.arange(N_HEADS) % 2 == 1)[None, :, None]
    wv = jnp.where(odd, wv_o, wv_e).transpose(1, 0, 2)
    c = lambda a: a.astype(BF16)
    return dict(win=c(win_ext), wq=c(wq), wk=c(wk), wv=c(wv), wout=c(w["w_out"]), wup=c(w["w_up"]), wdown=c(w["w_down"]))


def _tiles(s):
    return dict(ts=min(s, 512), tr=min(s, 512), tb=min(s, 512), tg=min(s, 512), tf=D_FF // 2, t2=min(s, 256))


def _forward(x, positions, w, small):
    s = x.shape[0]
    t = _tiles(s)
    pw = _prep_weights(w)
    cos_r, sin_r, cos_m, sin_m = _rope_tables(positions)
    rc = _ret_consts()
    q, k, v, g, cq, ckv, kpe, r1 = _f1_call(x, small["attn_norm_w"], pw["win"], cos_r, sin_r, cos_m, sin_m, t["ts"])
    o_ret, y_ret = _ret_fwd_call(q, k, v, g, small["ret_gn_w"], rc, t["tr"])
    mq, mk, mv = _mla_pre_call(cq, ckv, kpe, small["mla_q_norm_w"], small["mla_kv_norm_w"],
                               pw["wq"], pw["wk"], pw["wv"], cos_m, sin_m, t["ts"])
    y_mla, lse = _flash_fwd_call(mq, mk, mv, t["tb"])
    x1, r2 = _out_proj_call(x, y_ret, y_mla, pw["wout"], t["ts"])
    u = _up_proj_call(x1, r2, small["ffn_norm_w"], pw["wup"], t["ts"])
    a = _gate_call(u, w["conv_w"], small["conv_b"], t["tg"], t["tf"])
    x2 = _down_proj_call(x1, a, pw["wdown"], t["ts"])
    return dict(pw=pw, tabs=(cos_r, sin_r, cos_m, sin_m), rc=rc, q=q, k=k, v=v, g=g, cq=cq, ckv=ckv, kpe=kpe, r1=r1,
                o_ret=o_ret, y_ret=y_ret, mq=mq, mk=mk, mv=mv, y_mla=y_mla, lse=lse, x1=x1, r2=r2, u=u, a=a, x2=x2)


def _norm_bwd(dh, xh, r, nw):
    dxn = dh * nw
    return r * (dxn - xh * jnp.mean(dxn * xh, axis=-1, keepdims=True))


def _zero_first(first, *refs):
    @pl.when(first)
    def _():
        for ref in refs:
            ref[...] = jnp.zeros_like(ref)


def _colsum(v):
    return jnp.sum(v, axis=0, keepdims=True)


def _dsilu(g, sg):
    return sg * (1.0 + g * (1.0 - sg))


def _loss_call(x2, tgt, fw, ts):
    s = x2.shape[0]

    def body(x_ref, t_ref, fw_ref, dx_ref, loss_ref, gfw_ref):
        _zero_first(pl.program_id(0) == 0, loss_ref, gfw_ref)
        xv = x_ref[...]
        r = _rstd(xv)
        xh = xv * r
        fwv = fw_ref[...]
        e = xh * fwv - t_ref[...]
        loss_ref[...] += (0.5 / D_MODEL) * _colsum(jnp.sum(e * e, axis=1, keepdims=True))
        dy = e * (1.0 / D_MODEL)
        gfw_ref[...] += _colsum(dy * xh)
        dx_ref[...] = _norm_bwd(dy, xh, r, fwv)

    sd = jax.ShapeDtypeStruct
    return pl.pallas_call(
        body, name="loss_bwd", grid=(s // ts,),
        in_specs=[_row(ts, D_MODEL), _row(ts, D_MODEL), _full((1, D_MODEL))],
        out_specs=[_row(ts, D_MODEL), _full((1, 1)), _full((1, D_MODEL))],
        out_shape=[sd((s, D_MODEL), F32), sd((1, 1), F32), sd((1, D_MODEL), F32)],
        compiler_params=_cp("arbitrary"),
    )(x2, tgt, fw)


def _ffn_bwd1_call(dx2, u, cw, cb, wdown, ts, tf):
    s = dx2.shape[0]
    nf = D_FF // tf

    def body(dx_ref, wd_ref, ug_ref, hg_ref, uv_ref, hv_ref, wg_ref, wv_ref, bg_ref, bv_ref,
             dgate_ref, dval_ref, dwd_ref, dcwg_ref, dcwv_ref, dcbg_ref, dcbv_ref):
        first = pl.program_id(1) == 0
        _zero_first(first, dwd_ref, dcwg_ref, dcwv_ref, dcbg_ref, dcbv_ref)
        gate, g1, g2, g0 = _conv_tile(ug_ref, hg_ref, wg_ref, bg_ref, first)
        val, v1, v2, v0 = _conv_tile(uv_ref, hv_ref, wv_ref, bv_ref, first)
        dxb = dx_ref[...].astype(BF16)
        da = _dot_nt(dxb, wd_ref[...])
        sg = 1.0 / (1.0 + jnp.exp(-gate))
        sl = gate * sg
        dgate = da * val * _dsilu(gate, sg)
        dval = da * sl
        dgate_ref[...] = dgate.astype(BF16)
        dval_ref[...] = dval.astype(BF16)
        dwd_ref[...] += _dot_tn((sl * val).astype(BF16), dxb)
        for ref, d, taps in ((dcwg_ref, dgate, (g2, g1, g0)), (dcwv_ref, dval, (v2, v1, v0))):
            for t in range(3):
                ref[t:t + 1, :] += _colsum(d * taps[t])
        dcbg_ref[...] += _colsum(dgate)
        dcbv_ref[...] += _colsum(dval)

    sd = jax.ShapeDtypeStruct
    colacc = lambda r: pl.BlockSpec((r, tf), lambda j, i: (0, j))
    return pl.pallas_call(
        body, name="ffn_bwd_gate", grid=(nf, s // ts),
        in_specs=[pl.BlockSpec((ts, D_MODEL), lambda j, i: (i, 0)), pl.BlockSpec((tf, D_MODEL), lambda j, i: (j, 0))]
        + _gate_specs(ts, tf, rows_inner=True),
        out_specs=[pl.BlockSpec((ts, tf), lambda j, i: (i, j)), pl.BlockSpec((ts, tf), lambda j, i: (i, j)),
                   pl.BlockSpec((tf, D_MODEL), lambda j, i: (j, 0)), colacc(3), colacc(3), colacc(1), colacc(1)],
        out_shape=[sd((s, D_FF), BF16), sd((s, D_FF), BF16), sd((D_FF, D_MODEL), F32),
                   sd((3, D_FF), F32), sd((3, D_FF), F32), sd((1, D_FF), F32), sd((1, D_FF), F32)],
        compiler_params=_cp("parallel", "arbitrary"),
    )(dx2, wdown, u, u, u, u, cw, cw, cb, cb)


def _shifted_up(d, hal):
    n = d.shape[0]
    row = lax.broadcasted_iota(jnp.int32, d.shape, 0)
    d1 = jnp.where(row == n - 1, hal[0:1, :], pltpu.roll(d, n - 1, 0))
    d2 = jnp.where(row == n - 2, hal[0:1, :], jnp.where(row == n - 1, hal[1:2, :], pltpu.roll(d, n - 2, 0)))
    return d1, d2


def _ffn_bwd2_call(dgate, dval, cw, wup, x1, r2, fnw, dx2, ts):
    s = dx2.shape[0]
    nt = s // ts
    hb = ts // 8
    nxt = pl.BlockSpec((8, D_FF), lambda i: (jnp.minimum((i + 1) * hb, s // 8 - 1), 0))

    def body(dg_ref, hg_ref, dv_ref, hv_ref, cw_ref, wup_ref, x_ref, r_ref, fnw_ref, dx2_ref, du_ref, dx1_ref, dfnw_ref):
        i = pl.program_id(0)
        _zero_first(i == 0, dfnw_ref)
        dh = jnp.zeros((ts, D_MODEL), F32)
        for d_ref, h_ref, off in ((dg_ref, hg_ref, 0), (dv_ref, hv_ref, D_FF)):
            d = d_ref[...].astype(F32)
            hal = jnp.where(i == nt - 1, 0.0, h_ref[...].astype(F32))
            d1, d2 = _shifted_up(d, hal)
            w = cw_ref[:, off:off + D_FF]
            du = (w[2:3, :] * d + w[1:2, :] * d1 + w[0:1, :] * d2).astype(BF16)
            du_ref[:, off:off + D_FF] = du
            dh = dh + _dot_nt(du, wup_ref[:, off:off + D_FF])
        r = r_ref[...]
        xh = x_ref[...] * r
        dfnw_ref[...] += _colsum(dh * xh)
        dx1_ref[...] = dx2_ref[...] + _norm_bwd(dh, xh, r, fnw_ref[...])

    sd = jax.ShapeDtypeStruct
    return pl.pallas_call(
        body, name="ffn_bwd_up", grid=(nt,),
        in_specs=[_row(ts, D_FF), nxt, _row(ts, D_FF), nxt, _full((3, F2)), _full((D_MODEL, F2)),
                  _row(ts, D_MODEL), _row(ts, 1), _full((1, D_MODEL)), _row(ts, D_MODEL)],
        out_specs=[_row(ts, F2), _row(ts, D_MODEL), _full((1, D_MODEL))],
        out_shape=[sd((s, F2), BF16), sd((s, D_MODEL), F32), sd((1, D_MODEL), F32)],
        compiler_params=_cp("arbitrary"),
    )(dgate, dgate, dval, dval, cw, wup, x1, r2, fnw, dx2)


def _dw_norm_call(x, r, nw, b, ts, tn, name):
    s, n = b.shape
    k = x.shape[1]

    def body(x_ref, r_ref, nw_ref, b_ref, dw_ref):
        _zero_first(pl.program_id(1) == 0, dw_ref)
        h = (x_ref[...] * r_ref[...] * nw_ref[...]).astype(BF16)
        dw_ref[...] += _dot_tn(h, b_ref[...])

    return pl.pallas_call(
        body, name=name, grid=(n // tn, s // ts),
        in_specs=[pl.BlockSpec((ts, k), lambda j, i: (i, 0)), pl.BlockSpec((ts, 1), lambda j, i: (i, 0)),
                  pl.BlockSpec((1, k), lambda j, i: (0, 0)), pl.BlockSpec((ts, tn), lambda j, i: (i, j))],
        out_specs=pl.BlockSpec((k, tn), lambda j, i: (0, j)),
        out_shape=jax.ShapeDtypeStruct((k, n), F32),
        compiler_params=_cp("parallel", "arbitrary"),
    )(x, r, nw, b)


def _head_sum_matrix():
    l = np.arange(MLA_W)[:, None] // HEAD
    m = np.arange(LANES)[None, :] // 16
    return jnp.asarray((l == m).astype(np.float32))


def _out_bwd_call(dx1, yret, ymla, wout, ts):
    s = dx1.shape[0]

    def body(dx_ref, yr_ref, ym_ref, w_ref, e_ref, dyr_ref, do_ref, delta_ref, dwo_ref):
        _zero_first(pl.program_id(0) == 0, dwo_ref)
        dxb = dx_ref[...].astype(BF16)
        dmix = _dot_nt(dxb, w_ref[...])
        dyr_ref[...] = dmix[:, :RET_W]
        dom = dmix[:, RET_W:]
        do_ref[...] = dom.astype(BF16)
        ym = ym_ref[...]
        delta_ref[...] = _dot_hi(dom * ym.astype(F32), e_ref[...])
        dwo_ref[0:RET_W, :] += _dot_tn(yr_ref[...], dxb)
        dwo_ref[RET_W:, :] += _dot_tn(ym, dxb)

    sd = jax.ShapeDtypeStruct
    return pl.pallas_call(
        body, name="out_proj_bwd", grid=(s // ts,),
        in_specs=[_row(ts, D_MODEL), _row(ts, RET_W), _row(ts, MLA_W), _full((D_MODEL, D_MODEL)), _full((MLA_W, LANES))],
        out_specs=[_row(ts, RET_W), _row(ts, MLA_W), _row(ts, LANES), _full((D_MODEL, D_MODEL))],
        out_shape=[sd((s, RET_W), F32), sd((s, MLA_W), BF16), sd((s, LANES), F32), sd((D_MODEL, D_MODEL), F32)],
        compiler_params=_cp("arbitrary"),
    )(dx1, yret, ymla, wout, _head_sum_matrix())


def _ret_bwd_q_call(q, k, v, o, g, dy, gnw, rc, cos_r, sin_r, tr):
    s = q.shape[0]
    c = RET_CHUNK
    nc = tr // c

    def body(q_ref, k_ref, v_ref, o_ref, g_ref, dy_ref, gnw_ref, dm_ref, zeta_ref, xi_ref, cd_ref, bd_ref, cr_ref, sr_ref,
             dq_ref, dg_ref, do_ref, dgnw_ref, st_ref):
        _zero_first(pl.program_id(1) == 0, st_ref, dgnw_ref)
        bd = bd_ref[...]
        avg = bd * (1.0 / HEAD)
        ov = o_ref[...]
        ctr = ov - _dot_hi(ov, avg)
        rs = lax.rsqrt(_dot_hi(ctr * ctr, avg) + EPS)
        oh = ctr * rs
        gg, dyv, gnw_v = g_ref[...], dy_ref[...], gnw_ref[...]
        sg = 1.0 / (1.0 + jnp.exp(-gg))
        sl = gg * sg
        dg_ref[...] = (dyv * oh * gnw_v * _dsilu(gg, sg)).astype(BF16)
        dgnw_ref[...] += _colsum(dyv * sl * oh)
        doh = dyv * sl * gnw_v
        dov = (rs * (doh - _dot_hi(doh, avg) - oh * _dot_hi(doh * oh, avg))).astype(BF16)
        do_ref[...] = dov
        lane = lax.broadcasted_iota(jnp.int32, (c, LANES), 1)
        for ci in range(nc):
            rows = slice(ci * c, (ci + 1) * c)
            kc, vc, doc = k_ref[rows, :], v_ref[rows, :], dov[rows, :]
            st = st_ref[...]
            dq = _dot_nt(doc, st.astype(BF16)) * xi_ref[0]
            for hh in range(2):
                m = (lane >= HEAD) if hh else (lane < HEAD)
                a = _dot_nt(jnp.where(m, doc, jnp.zeros_like(doc)), vc) * dm_ref[0, hh]
                dq = dq + jnp.where(m, _dot(a.astype(BF16), kc), 0.0)
            kz = (kc.astype(F32) * zeta_ref[0]).astype(BF16)
            st_ref[...] = st * cd_ref[0] + _dot_tn(kz, vc) * bd
            dq_ref[rows, :] = _unrope(dq, cr_ref[rows, :], sr_ref[rows, :], HEAD // 2).astype(BF16)

    slab = pl.BlockSpec((tr, LANES), lambda j, i: (i, j))
    tab = pl.BlockSpec((tr, LANES), lambda j, i: (i, 0))
    vec = pl.BlockSpec((1, LANES), lambda j, i: (0, j))
    sd = jax.ShapeDtypeStruct
    return pl.pallas_call(
        body, name="ret_bwd_q", grid=(4, s // tr),
        in_specs=[slab, slab, slab, slab, slab, slab, vec,
                  pl.BlockSpec((1, 2, c, c), lambda j, i: (j, 0, 0, 0)),
                  pl.BlockSpec((1, c, LANES), lambda j, i: (j, 0, 0)),
                  pl.BlockSpec((1, c, LANES), lambda j, i: (j, 0, 0)),
                  pl.BlockSpec((1, LANES, LANES), lambda j, i: (j, 0, 0)),
                  pl.BlockSpec((LANES, LANES), lambda j, i: (0, 0)), tab, tab],
        out_specs=[slab, slab, slab, vec],
        out_shape=[sd((s, RET_W), BF16), sd((s, RET_W), BF16), sd((s, RET_W), BF16), sd((1, RET_W), F32)],
        scratch_shapes=[pltpu.VMEM((LANES, LANES), F32)],
        compiler_params=_cp("parallel", "arbitrary"),
    )(q, k, v, o, g, dy, gnw, rc["dmask"], rc["zeta"], rc["xi"], rc["cd"], rc["bd"], cos_r, sin_r)


def _ret_bwd_kv_call(q, k, v, do, rc, cos_r, sin_r, tr):
    s = q.shape[0]
    c = RET_CHUNK
    nc = tr // c
    nt = s // tr

    def body(q_ref, k_ref, v_ref, do_ref, dm_ref, zeta_ref, xi_ref, cd_ref, bd_ref, cr_ref, sr_ref, dk_ref, dv_ref, gs_ref):
        _zero_first(pl.program_id(1) == 0, gs_ref)
        bd = bd_ref[...]
        lane = lax.broadcasted_iota(jnp.int32, (c, LANES), 1)
        for ci in reversed(range(nc)):
            rows = slice(ci * c, (ci + 1) * c)
            qc, kc, vc, doc = q_ref[rows, :], k_ref[rows, :], v_ref[rows, :], do_ref[rows, :]
            gs = gs_ref[...]
            gb = gs.astype(BF16)
            dk = _dot_nt(vc, gb) * zeta_ref[0]
            dv = _dot(kc, gb) * zeta_ref[0]
            for hh in range(2):
                m = (lane >= HEAD) if hh else (lane < HEAD)
                a = _dot_nt(jnp.where(m, doc, jnp.zeros_like(doc)), vc) * dm_ref[0, hh]
                dk = dk + jnp.where(m, _dot_tn(a.astype(BF16), qc), 0.0)
                p = _dot_nt(jnp.where(m, qc, jnp.zeros_like(qc)), kc) * dm_ref[0, hh]
                dv = dv + jnp.where(m, _dot_tn(p.astype(BF16), doc), 0.0)
            qx = (qc.astype(F32) * xi_ref[0]).astype(BF16)
            gs_ref[...] = gs * cd_ref[0] + _dot_tn(qx, doc) * bd
            dk_ref[rows, :] = (_unrope(dk, cr_ref[rows, :], sr_ref[rows, :], HEAD // 2) * (HEAD ** -0.5)).astype(BF16)
            dv_ref[rows, :] = dv.astype(BF16)

    slab = pl.BlockSpec((tr, LANES), lambda j, i: (nt - 1 - i, j))
    tab = pl.BlockSpec((tr, LANES), lambda j, i: (nt - 1 - i, 0))
    sd = jax.ShapeDtypeStruct
    return pl.pallas_call(
        body, name="ret_bwd_kv", grid=(4, nt),
        in_specs=[slab, slab, slab, slab,
                  pl.BlockSpec((1, 2, c, c), lambda j, i: (j, 0, 0, 0)),
                  pl.BlockSpec((1, c, LANES), lambda j, i: (j, 0, 0)),
                  pl.BlockSpec((1, c, LANES), lambda j, i: (j, 0, 0)),
                  pl.BlockSpec((1, LANES, LANES), lambda j, i: (j, 0, 0)),
                  pl.BlockSpec((LANES, LANES), lambda j, i: (0, 0)), tab, tab],
        out_specs=[slab, slab],
        out_shape=[sd((s, RET_W), BF16), sd((s, RET_W), BF16)],
        scratch_shapes=[pltpu.VMEM((LANES, LANES), F32)],
        compiler_params=_cp("parallel", "arbitrary"),
    )(q, k, v, do, rc["dmask"], rc["zeta"], rc["xi"], rc["cd"], rc["bd"], cos_r, sin_r)


def _head_col(slab, h):
    lane = lax.broadcasted_iota(jnp.int32, slab.shape, 1)
    return jnp.max(jnp.where(lane // 16 == h, slab, NEG), axis=1, keepdims=True)


def _flash_bwd_dq_call(q, k, v, do, lse, delta, tb):
    s = q.shape[1]
    nb = s // tb

    def body(q_ref, k_ref, v_ref, do_ref, lse_ref, dl_ref, dq_ref, acc_ref):
        qi, ki = pl.program_id(0), pl.program_id(1)
        _zero_first(ki == 0, acc_ref)

        @pl.when(ki <= qi)
        def _():
            row = qi * tb + lax.broadcasted_iota(jnp.int32, (tb, tb), 0)
            col = ki * tb + lax.broadcasted_iota(jnp.int32, (tb, tb), 1)
            keep = col <= row
            lsev, dlv = lse_ref[...], dl_ref[...]
            for h in range(N_HEADS):
                dob = do_ref[:, (h // 2) * LANES:(h // 2 + 1) * LANES]
                sc = jnp.where(keep, _dot_nt(q_ref[h], k_ref[h]), NEG)
                p = jnp.exp(sc - _head_col(lsev, h))
                dp = _dot_nt(dob, v_ref[h])
                ds = (p * (dp - _head_col(dlv, h))).astype(BF16)
                acc_ref[h] += _dot(ds, k_ref[h])

        @pl.when(ki == qi)
        def _():
            dq_ref[...] = (acc_ref[...] * SM_SCALE).astype(BF16)

    qspec = pl.BlockSpec((N_HEADS, tb, LANES), lambda qi, ki: (0, qi, 0))
    kspec = pl.BlockSpec((N_HEADS, tb, LANES), lambda qi, ki: (0, jnp.minimum(ki, qi), 0))
    rspec = lambda w: pl.BlockSpec((tb, w), lambda qi, ki: (qi, 0))
    return pl.pallas_call(
        body, name="mla_flash_bwd_dq", grid=(nb, nb),
        in_specs=[qspec, kspec, kspec, rspec(MLA_W), rspec(LANES), rspec(LANES)],
        out_specs=qspec,
        out_shape=jax.ShapeDtypeStruct((N_HEADS, s, LANES), BF16),
        scratch_shapes=[pltpu.VMEM((N_HEADS, tb, LANES), F32)],
        compiler_params=_cp("parallel", "arbitrary"),
    )(q, k, v, do, lse, delta)


def _flash_bwd_dkv_call(q, k, v, do, lse, delta, tb):
    s = q.shape[1]
    nb = s // tb

    def body(q_ref, k_ref, v_ref, do_ref, lse_ref, dl_ref, dk_ref, dv_ref, dka_ref, dva_ref):
        ki, qi = pl.program_id(0), pl.program_id(1)
        _zero_first(qi == 0, dka_ref, dva_ref)

        @pl.when(qi >= ki)
        def _():
            row = qi * tb + lax.broadcasted_iota(jnp.int32, (tb, tb), 0)
            col = ki * tb + lax.broadcasted_iota(jnp.int32, (tb, tb), 1)
            keep = col <= row
            lsev, dlv = lse_ref[...], dl_ref[...]
            for h in range(N_HEADS):
                dob = do_ref[:, (h // 2) * LANES:(h // 2 + 1) * LANES]
                qh = q_ref[h]
                sc = jnp.where(keep, _dot_nt(qh, k_ref[h]), NEG)
                p = jnp.exp(sc - _head_col(lsev, h))
                dva_ref[h] += _dot_tn(p.astype(BF16), dob)
                dp = _dot_nt(dob, v_ref[h])
                ds = (p * (dp - _head_col(dlv, h))).astype(BF16)
                dka_ref[h] += _dot_tn(ds, qh)

        @pl.when(qi == nb - 1)
        def _():
            dk_ref[...] = dka_ref[...].astype(BF16)
            dv_ref[...] = dva_ref[...].astype(BF16)

    kspec = pl.BlockSpec((N_HEADS, tb, LANES), lambda ki, qi: (0, ki, 0))
    qspec = pl.BlockSpec((N_HEADS, tb, LANES), lambda ki, qi: (0, jnp.maximum(qi, ki), 0))
    rspec = lambda w: pl.BlockSpec((tb, w), lambda ki, qi: (jnp.maximum(qi, ki), 0))
    sd = jax.ShapeDtypeStruct((N_HEADS, s, LANES), BF16)
    return pl.pallas_call(
        body, name="mla_flash_bwd_dkv", grid=(nb, nb),
        in_specs=[qspec, kspec, kspec, rspec(MLA_W), rspec(LANES), rspec(LANES)],
        out_specs=[kspec, kspec],
        out_shape=[sd, sd],
        scratch_shapes=[pltpu.VMEM((N_HEADS, tb, LANES), F32), pltpu.VMEM((N_HEADS, tb, LANES), F32)],
        compiler_params=_cp("parallel", "arbitrary"),
    )(q, k, v, do, lse, delta)


def _mla_post_call(dq, dk, dv, cq, ckv, qnw, kvnw, wq, wk, wv, cos_m, sin_m, ts):
    s = cq.shape[0]

    def body(dq_ref, dk_ref, dv_ref, cq_ref, ckv_ref, qnw_ref, kvnw_ref, wq_ref, wk_ref, wv_ref, cm_ref, sm_ref,
             dcq_ref, dckv_ref, dkpe_ref, dwq_ref, dwk_ref, dwv_ref, dqnw_ref, dkvnw_ref):
        _zero_first(pl.program_id(0) == 0, dwq_ref, dwk_ref, dwv_ref, dqnw_ref, dkvnw_ref)
        cqv, ckvv = cq_ref[...], ckv_ref[...]
        rq, rkv = _rstd(cqv), _rstd(ckvv)
        qh_, kvh_ = cqv * rq, ckvv * rkv
        qnw_v, kvnw_v = qnw_ref[...], kvnw_ref[...]
        cqn = (qh_ * qnw_v).astype(BF16)
        ckvn = (kvh_ * kvnw_v).astype(BF16)
        cm, sm = cm_ref[...], sm_ref[...]
        dcqn = jnp.zeros((ts, Q_RANK), F32)
        dckvn = jnp.zeros((ts, KV_RANK), F32)
        dkpe = jnp.zeros((ts, LANES), F32)
        for h in range(N_HEADS):
            dqu = _unrope(dq_ref[h].astype(F32), cm, sm, ROPE // 2).astype(BF16)
            dwq_ref[h] += _dot_tn(cqn, dqu)
            dcqn = dcqn + _dot_nt(dqu, wq_ref[h])
            dkb, dvb = dk_ref[h], dv_ref[h]
            dkpe = dkpe + dkb.astype(F32)
            dwk_ref[h] += _dot_tn(ckvn, dkb)
            dwv_ref[h] += _dot_tn(ckvn, dvb)
            dckvn = dckvn + _dot_nt(dkb, wk_ref[h]) + _dot_nt(dvb, wv_ref[h])
        lane = lax.broadcasted_iota(jnp.int32, (ts, LANES), 1)
        dkpe = jnp.where((lane >= KPE_LO) & (lane < KPE_LO + ROPE), dkpe, 0.0)
        dkpe_ref[...] = _unrope(dkpe, cm, sm, ROPE // 2).astype(BF16)
        dqnw_ref[...] += _colsum(dcqn * qh_)
        dkvnw_ref[...] += _colsum(dckvn * kvh_)
        dcq_ref[...] = _norm_bwd(dcqn, qh_, rq, qnw_v).astype(BF16)
        dckv_ref[...] = _norm_bwd(dckvn, kvh_, rkv, kvnw_v).astype(BF16)

    sd = jax.ShapeDtypeStruct
    hm = _hrow(N_HEADS, ts, LANES)
    return pl.pallas_call(
        body, name="mla_post", grid=(s // ts,),
        in_specs=[hm, hm, hm, _row(ts, Q_RANK), _row(ts, KV_RANK), _full((1, Q_RANK)), _full((1, KV_RANK)),
                  _full((N_HEADS, Q_RANK, LANES)), _full((N_HEADS, KV_RANK, LANES)), _full((N_HEADS, KV_RANK, LANES)),
                  _row(ts, LANES), _row(ts, LANES)],
        out_specs=[_row(ts, Q_RANK), _row(ts, KV_RANK), _row(ts, LANES),
                   _full((N_HEADS, Q_RANK, LANES)), _full((N_HEADS, KV_RANK, LANES)), _full((N_HEADS, KV_RANK, LANES)),
                   _full((1, Q_RANK)), _full((1, KV_RANK))],
        out_shape=[sd((s, Q_RANK), BF16), sd((s, KV_RANK), BF16), sd((s, LANES), BF16),
                   sd((N_HEADS, Q_RANK, LANES), F32), sd((N_HEADS, KV_RANK, LANES), F32), sd((N_HEADS, KV_RANK, LANES), F32),
                   sd((1, Q_RANK), F32), sd((1, KV_RANK), F32)],
        compiler_params=_cp("arbitrary"),
    )(dq, dk, dv, cq, ckv, qnw, kvnw, wq, wk, wv, cos_m, sin_m)


def _in_bwd_call(parts, x, r1, anw, dx1, win, ts):
    s = x.shape[0]
    widths = [p.shape[1] for p in parts]
    np_ = len(parts)

    def body(*refs):
        p_refs = refs[:np_]
        x_ref, r_ref, anw_ref, dx1_ref, w_ref, dx_ref, dw_ref, danw_ref = refs[np_:]
        _zero_first(pl.program_id(0) == 0, dw_ref, danw_ref)
        dproj = jnp.concatenate([p[...] for p in p_refs], axis=-1)
        r, anw_v = r_ref[...], anw_ref[...]
        xh = x_ref[...] * r
        dw_ref[...] += _dot_tn((xh * anw_v).astype(BF16), dproj)
        dh = _dot_nt(dproj, w_ref[...])
        danw_ref[...] += _colsum(dh * xh)
        dx_ref[...] = dx1_ref[...] + _norm_bwd(dh, xh, r, anw_v)

    sd = jax.ShapeDtypeStruct
    return pl.pallas_call(
        body, name="in_proj_bwd", grid=(s // ts,),
        in_specs=[_row(ts, w) for w in widths]
        + [_row(ts, D_MODEL), _row(ts, 1), _full((1, D_MODEL)), _row(ts, D_MODEL), _full((D_MODEL, IN_EXT))],
        out_specs=[_row(ts, D_MODEL), _full((D_MODEL, IN_EXT)), _full((1, D_MODEL))],
        out_shape=[sd((s, D_MODEL), F32), sd((D_MODEL, IN_EXT), F32), sd((1, D_MODEL), F32)],
        compiler_params=_cp("arbitrary"),
    )(*parts, x, r1, anw, dx1, win)


def _local_step(x, positions, tgt, w, small):
    s = x.shape[0]
    t = _tiles(s)
    f = _forward(x, positions, w, small)
    pw, rc = f["pw"], f["rc"]
    cos_r, sin_r, cos_m, sin_m = f["tabs"]
    dx2, loss, g_fw = _loss_call(f["x2"], tgt, small["final_norm_w"], t["ts"])
    dgate, dval, g_wd, dcw_g, dcw_v, dcb_g, dcb_v = _ffn_bwd1_call(dx2, f["u"], w["conv_w"], small["conv_b"], pw["wdown"], t["tg"], t["tf"])
    du, dx1, g_fnw = _ffn_bwd2_call(dgate, dval, w["conv_w"], pw["wup"], f["x1"], f["r2"], small["ffn_norm_w"], dx2, t["t2"])
    g_wup = _dw_norm_call(f["x1"], f["r2"], small["ffn_norm_w"], du, t["ts"], F2 // 4, "dw_up")
    dy_ret, do, delta, g_wout = _out_bwd_call(dx1, f["y_ret"], f["y_mla"], pw["wout"], t["ts"])
    drq, dg, do_ret, g_gnw = _ret_bwd_q_call(f["q"], f["k"], f["v"], f["o_ret"], f["g"], dy_ret, small["ret_gn_w"], rc, cos_r, sin_r, t["tr"])
    drk, drv = _ret_bwd_kv_call(f["q"], f["k"], f["v"], do_ret, rc, cos_r, sin_r, t["tr"])
    dmq = _flash_bwd_dq_call(f["mq"], f["mk"], f["mv"], do, f["lse"], delta, t["tb"])
    dmk, dmv = _flash_bwd_dkv_call(f["mq"], f["mk"], f["mv"], do, f["lse"], delta, t["tb"])
    dcq, dckv, dkpe, g_wq, g_wk, g_wv, g_qnw, g_kvnw = _mla_post_call(
        dmq, dmk, dmv, f["cq"], f["ckv"], small["mla_q_norm_w"], small["mla_kv_norm_w"], pw["wq"], pw["wk"], pw["wv"], cos_m, sin_m, t["ts"])
    gx, g_win_ext, g_anw = _in_bwd_call([drq, drk, drv, dg, dcq, dckv, dkpe], x, f["r1"], small["attn_norm_w"], dx1, pw["win"], t["ts"])
    lo = IN_W - ROPE
    g_win = jnp.concatenate([g_win_ext[:, :lo], g_win_ext[:, lo + KPE_LO:lo + KPE_LO + ROPE]], -1)
    g_wuq = g_wq.transpose(1, 0, 2)[:, :, :HEAD + ROPE].reshape(Q_RANK, N_HEADS * (HEAD + ROPE))
    gv_e, gv_o = g_wv[:, :, :HEAD], g_wv[:, :, HEAD:]
    odd = (jnp.arange(N_HEADS) % 2 == 1)[:, None, None]
    g_wukv = jnp.concatenate([g_wk[:, :, :HEAD], jnp.where(odd, gv_o, gv_e)], -1).transpose(1, 0, 2).reshape(KV_RANK, 2 * MLA_W)
    gw = dict(w_in=g_win, w_uq=g_wuq, w_ukv=g_wukv, w_out=g_wout, w_up=g_wup,
              conv_w=jnp.concatenate([dcw_g, dcw_v], -1), w_down=g_wd)
    gs = dict(attn_norm_w=g_anw, ret_gn_w=g_gnw, mla_q_norm_w=g_qnw, mla_kv_norm_w=g_kvnw, ffn_norm_w=g_fnw,
              conv_b=jnp.concatenate([dcb_g, dcb_v], -1), final_norm_w=g_fw)
    return loss, gx, gw, gs


MESH_ID = pl.DeviceIdType.MESH
ANY = pl.BlockSpec(memory_space=pl.ANY)
VMEM_SPEC = pl.BlockSpec(memory_space=pltpu.VMEM)
N_DEV = 8
PACKED = (("w_in", (D_MODEL, IN_W // 4), 1), ("w_uq", (Q_RANK, 192), 1), ("w_ukv", (KV_RANK, 256), 1),
          ("w_out", (D_MODEL // 4, D_MODEL), 0), ("w_up", (D_MODEL, F2 // 4), 1), ("w_down", (D_FF // 4, D_MODEL), 0))
PACK_ROWS = 24576
HALF_ROWS = PACK_ROWS // 2
ADD_ROWS = 2048


def _mesh_pos():
    return lax.axis_index("x"), lax.axis_index("y"), lax.axis_index("c")


def _other_chips(x, y):
    return [(1 - x, y), (x, 1 - y), (1 - x, 1 - y)]


def _pack(parts, dtype):
    flat = jnp.concatenate([p.reshape(-1).astype(dtype) for p in parts])
    flat = jnp.concatenate([flat, jnp.zeros((PACK_ROWS * LANES - flat.shape[0],), dtype)])
    return flat.reshape(2, HALF_ROWS, LANES)


def _unpack(flat):
    out, off = [], 0
    for _, (r, c), _ in PACKED:
        out.append(flat[..., off:off + r * c].reshape(flat.shape[:-1] + (r, c)))
        off += r * c
    return out


def _all_gather_call(packed):
    _, h, _ = packed.shape

    def body(src_ref, out_ref, send_sems, recv_sems, local_sem):
        x, y, c = _mesh_pos()
        sm = 2 * x + y
        chips = _other_chips(x, y)

        def rcopy(k, src, dst, dev):
            return pltpu.make_async_remote_copy(src_ref=src, dst_ref=dst, send_sem=send_sems.at[k], recv_sem=recv_sems.at[k],
                                                device_id=dev, device_id_type=MESH_ID)

        mine = pltpu.make_async_copy(src_ref, out_ref.at[sm], local_sem)
        mine.start()
        first = [rcopy(j, src_ref.at[c], out_ref.at[sm, c], (cx, cy, c)) for j, (cx, cy) in enumerate(chips)]
        for cp in first:
            cp.start()
        passed = []
        for j, (cx, cy) in enumerate(chips):
            sj = 2 * cx + cy
            rcopy(j, src_ref.at[c], out_ref.at[sj, c], (cx, cy, c)).wait_recv()
            cp = rcopy(3 + j, out_ref.at[sj, c], out_ref.at[sj, c], (x, y, 1 - c))
            cp.start()
            passed.append(cp)
        for j, (cx, cy) in enumerate(chips):
            rcopy(3 + j, src_ref.at[c], out_ref.at[2 * cx + cy, 1 - c], (x, y, 1 - c)).wait_recv()
        for cp in first + passed:
            cp.wait_send()
        mine.wait()

    return pl.pallas_call(
        body, name="weights_all_gather",
        in_specs=[ANY], out_specs=ANY,
        out_shape=jax.ShapeDtypeStruct((4, 2, h, LANES), packed.dtype),
        scratch_shapes=[pltpu.SemaphoreType.DMA((6,)), pltpu.SemaphoreType.DMA((6,)), pltpu.SemaphoreType.DMA],
    )(packed)


def _rs_sibling_call(g):
    _, _, h, _ = g.shape

    def body(g_ref, buf_ref, send_sem, recv_sem):
        x, y, c = _mesh_pos()
        cp = pltpu.make_async_remote_copy(src_ref=g_ref.at[1 - c], dst_ref=buf_ref, send_sem=send_sem, recv_sem=recv_sem,
                                          device_id=(x, y, 1 - c), device_id_type=MESH_ID)
        cp.start()
        cp.wait()

    return pl.pallas_call(
        body, name="grads_rs_sibling",
        in_specs=[ANY], out_specs=ANY,
        out_shape=jax.ShapeDtypeStruct((4, h, LANES), g.dtype),
        scratch_shapes=[pltpu.SemaphoreType.DMA, pltpu.SemaphoreType.DMA],
    )(g)


def _rs_add1_call(g, buf, c):
    _, _, h, _ = g.shape

    def body(c_ref, g_ref, b_ref, p_ref, pb_ref):
        p = g_ref[...] + b_ref[...]
        p_ref[...] = p
        pb_ref[...] = p.astype(BF16)

    blk = pl.BlockSpec((None, ADD_ROWS, LANES), lambda s, i, c_ref: (s, i, 0))
    return pl.pallas_call(
        body, name="grads_rs_add_sibling",
        grid_spec=pltpu.PrefetchScalarGridSpec(
            num_scalar_prefetch=1, grid=(4, h // ADD_ROWS),
            in_specs=[pl.BlockSpec((None, None, ADD_ROWS, LANES), lambda s, i, c_ref: (c_ref[0], s, i, 0)), blk],
            out_specs=[blk, blk]),
        out_shape=[jax.ShapeDtypeStruct((4, h, LANES), F32), jax.ShapeDtypeStruct((4, h, LANES), BF16)],
        compiler_params=_cp("parallel", "parallel"),
    )(c, g, buf)


def _rs_chips_call(pb):
    _, h, _ = pb.shape

    def body(pb_ref, buf_ref, send_sems, recv_sems):
        x, y, c = _mesh_pos()
        cps = [pltpu.make_async_remote_copy(src_ref=pb_ref.at[2 * cx + cy], dst_ref=buf_ref.at[j], send_sem=send_sems.at[j],
                                            recv_sem=recv_sems.at[j], device_id=(cx, cy, c), device_id_type=MESH_ID)
               for j, (cx, cy) in enumerate(_other_chips(x, y))]
        for cp in cps:
            cp.start()
        for cp in cps:
            cp.wait()

    return pl.pallas_call(
        body, name="grads_rs_chips",
        in_specs=[ANY], out_specs=ANY,
        out_shape=jax.ShapeDtypeStruct((3, h, LANES), pb.dtype),
        scratch_shapes=[pltpu.SemaphoreType.DMA((3,)), pltpu.SemaphoreType.DMA((3,))],
    )(pb)


def _rs_add2_call(p, buf, sm):
    _, h, _ = p.shape

    def body(sm_ref, p_ref, b_ref, f_ref):
        f_ref[...] = ((p_ref[...] + b_ref[0].astype(F32)) + b_ref[1].astype(F32)) + b_ref[2].astype(F32)

    return pl.pallas_call(
        body, name="grads_rs_add_chips",
        grid_spec=pltpu.PrefetchScalarGridSpec(
            num_scalar_prefetch=1, grid=(h // ADD_ROWS,),
            in_specs=[pl.BlockSpec((None, ADD_ROWS, LANES), lambda i, sm_ref: (sm_ref[0], i, 0)),
                      pl.BlockSpec((3, ADD_ROWS, LANES), lambda i, sm_ref: (0, i, 0))],
            out_specs=pl.BlockSpec((ADD_ROWS, LANES), lambda i, sm_ref: (i, 0))),
        out_shape=jax.ShapeDtypeStruct((h, LANES), F32),
        compiler_params=_cp("parallel"),
    )(sm, p, buf)


def _rs_share_call(f):
    h, _ = f.shape

    def body(f_ref, out_ref, send_sem, recv_sem, local_sem):
        x, y, c = _mesh_pos()
        loc = pltpu.make_async_copy(f_ref, out_ref.at[c], local_sem)
        loc.start()
        cp = pltpu.make_async_remote_copy(src_ref=f_ref, dst_ref=out_ref.at[c], send_sem=send_sem, recv_sem=recv_sem,
                                          device_id=(x, y, 1 - c), device_id_type=MESH_ID)
        cp.start()
        cp.wait_send()
        pltpu.make_async_remote_copy(src_ref=f_ref, dst_ref=out_ref.at[1 - c], send_sem=send_sem, recv_sem=recv_sem,
                                     device_id=(x, y, 1 - c), device_id_type=MESH_ID).wait_recv()
        loc.wait()

    return pl.pallas_call(
        body, name="grads_rs_share",
        in_specs=[ANY], out_specs=ANY,
        out_shape=jax.ShapeDtypeStruct((2, h, LANES), f.dtype),
        scratch_shapes=[pltpu.SemaphoreType.DMA, pltpu.SemaphoreType.DMA, pltpu.SemaphoreType.DMA],
    )(f)


def _exchange8_call(vec, reduce, name):
    rows = vec.shape[0]

    def body(v_ref, out_ref, *rest):
        slots, send_sems, recv_sems = (rest if reduce else (out_ref,) + rest)
        x, y, c = _mesh_pos()
        me = 4 * x + 2 * y + c
        slots[me] = v_ref[...]

        def rcopy(k, to_me):
            bx, by, bc = (k >> 2) & 1, (k >> 1) & 1, k & 1
            px, py, pc = (1 - x if bx else x), (1 - y if by else y), (1 - c if bc else c)
            slot = 4 * px + 2 * py + pc if to_me else me
            return pltpu.make_async_remote_copy(src_ref=v_ref, dst_ref=slots.at[slot], send_sem=send_sems.at[k - 1],
                                                recv_sem=recv_sems.at[k - 1], device_id=(px, py, pc), device_id_type=MESH_ID)

        for k in range(1, N_DEV):
            rcopy(k, False).start()
        for k in range(1, N_DEV):
            rcopy(k, True).wait_recv()
        for k in range(1, N_DEV):
            rcopy(k, False).wait_send()
        if reduce:
            tot = slots[0]
            for d in range(1, N_DEV):
                tot = tot + slots[d]
            out_ref[...] = tot

    stack = jax.ShapeDtypeStruct((N_DEV, rows, LANES), F32)
    return pl.pallas_call(
        body, name=name,
        in_specs=[VMEM_SPEC], out_specs=VMEM_SPEC,
        out_shape=jax.ShapeDtypeStruct((rows, LANES), F32) if reduce else stack,
        scratch_shapes=([pltpu.VMEM((N_DEV, rows, LANES), F32)] if reduce else [])
        + [pltpu.SemaphoreType.DMA((N_DEV - 1,)), pltpu.SemaphoreType.DMA((N_DEV - 1,))],
    )(vec)


def _adamw_call(w, g, m, v, name):
    r, c = w.shape
    rb = r if r <= 256 else (256 if r % 256 == 0 else 352)
    assert r % rb == 0

    def body(w_ref, g_ref, m_ref, v_ref, d_ref, nm_ref, nv_ref):
        gv = g_ref[...]
        nm = ADAM_B1 * m_ref[...] + (1.0 - ADAM_B1) * gv
        nv = ADAM_B2 * v_ref[...] + (1.0 - ADAM_B2) * jnp.square(gv)
        m_hat = nm / (1.0 - ADAM_B1 ** ADAM_STEP)
        v_hat = nv / (1.0 - ADAM_B2 ** ADAM_STEP)
        d_ref[...] = -ADAM_LR * (m_hat / (jnp.sqrt(v_hat) + ADAM_EPS) + ADAM_WD * w_ref[...])
        nm_ref[...] = nm
        nv_ref[...] = nv

    spec = pl.BlockSpec((rb, c), lambda i: (i, 0))
    sd = jax.ShapeDtypeStruct((r, c), F32)
    return pl.pallas_call(
        body, name=name, grid=(r // rb,),
        in_specs=[spec] * 4, out_specs=[spec] * 3, out_shape=[sd, sd, sd],
        compiler_params=_cp("parallel"),
    )(w, g, m, v)


SMALL = (("attn_norm_w", D_MODEL), ("ret_gn_w", RET_W), ("mla_q_norm_w", Q_RANK), ("mla_kv_norm_w", KV_RANK),
         ("ffn_norm_w", D_MODEL), ("conv_b", F2), ("final_norm_w", D_MODEL))
WEIGHT_ORDER = ("attn_norm_w", "w_in", "ret_gn_w", "mla_q_norm_w", "w_uq", "mla_kv_norm_w", "w_ukv", "w_out",
                "ffn_norm_w", "w_up", "conv_w", "conv_b", "w_down", "final_norm_w")


def _pad_rows(flat, rows):
    return jnp.concatenate([flat, jnp.zeros((rows * LANES - flat.shape[0],), flat.dtype)]).reshape(rows, LANES)


def kernel(x, positions, attn_norm_w, w_in, ret_gn_w, mla_q_norm_w, w_uq, mla_kv_norm_w, w_ukv, w_out, ffn_norm_w, w_up, conv_w, conv_b, w_down, final_norm_w, loss_target, m_attn_norm_w, m_w_in, m_ret_gn_w, m_mla_q_norm_w, m_w_uq, m_mla_kv_norm_w, m_w_ukv, m_w_out, m_ffn_norm_w, m_w_up, m_conv_w, m_conv_b, m_w_down, m_final_norm_w, v_attn_norm_w, v_w_in, v_ret_gn_w, v_mla_q_norm_w, v_w_uq, v_mla_kv_norm_w, v_w_ukv, v_w_out, v_ffn_norm_w, v_w_up, v_conv_w, v_conv_b, v_w_down, v_final_norm_w):
    args = dict(locals())
    cx, cy, cc = _mesh_pos()
    sm = 2 * cx + cy

    gathered = _all_gather_call(_pack([args[n][0] for n, _, _ in PACKED], BF16))
    full = {}
    for (n, (r, c), axis), piece in zip(PACKED, _unpack(gathered.reshape(4, PACK_ROWS * LANES))):
        full[n] = piece.transpose(1, 0, 2).reshape(r, 4 * c) if axis == 1 else piece.reshape(4 * r, c)
    cw_rows = 40
    cw_all = _exchange8_call(_pad_rows(conv_w[0].reshape(-1), cw_rows), False, "conv_w_all_gather")
    cw_all = cw_all[0::2].reshape(4, cw_rows * LANES)[:, :3 * F2 // 4].reshape(4, 3, F2 // 4)
    full["conv_w"] = cw_all.transpose(1, 0, 2).reshape(3, F2)
    small = {n: args[n].reshape(1, d) for n, d in SMALL}

    loss, gx, gw, gs = _local_step(x[0], positions[0], loss_target[0], full, small)

    shards = []
    for n, (r, c), axis in PACKED:
        g = gw[n]
        shards.append(g.reshape(r, 4, c).transpose(1, 0, 2).reshape(4, r * c) if axis == 1 else g.reshape(4, r * c))
    gflat = jnp.concatenate(shards + [jnp.zeros((4, PACK_ROWS * LANES - sum(s.shape[1] for s in shards)), F32)], axis=1)
    gpk = gflat.reshape(4, 2, HALF_ROWS, LANES).transpose(1, 0, 2, 3)
    p, pb = _rs_add1_call(gpk, _rs_sibling_call(gpk), cc.reshape(1).astype(jnp.int32))
    fin = _rs_add2_call(p, _rs_chips_call(pb), sm.reshape(1).astype(jnp.int32))
    red = dict(zip([n for n, _, _ in PACKED], _unpack(_rs_share_call(fin).reshape(PACK_ROWS * LANES))))

    vec = jnp.concatenate([gs[n].reshape(-1) for n, _ in SMALL] + [gw["conv_w"].reshape(-1), loss.reshape(-1)])
    n_small = sum(d for _, d in SMALL)
    tot = _exchange8_call(_pad_rows(vec, 216), True, "small_all_reduce").reshape(-1)
    off = 0
    for n, d in SMALL:
        red[n] = tot[off:off + d].reshape(1, d)
        off += d
    red["conv_w"] = lax.dynamic_slice(tot[off:off + 3 * F2].reshape(3, F2), (0, sm * (F2 // 4)), (3, F2 // 4))
    loss_tot = tot[off + 3 * F2]

    grads, deltas, new_m, new_v = [], [], [], []
    for n in WEIGHT_ORDER:
        shape = args[n].shape
        two_d = (1, shape[0]) if len(shape) == 1 else shape[-2:]
        g = red[n].reshape(two_d)
        d, nm, nv = _adamw_call(args[n].reshape(two_d), g, args["m_" + n].reshape(two_d), args["v_" + n].reshape(two_d), "adamw_" + n)
        grads.append(g.reshape(shape))
        deltas.append(d.reshape(shape))
        new_m.append(nm.reshape(shape))
        new_v.append(nv.reshape(shape))
    return (loss_tot, gx[None], *grads, *deltas, *new_m, *new_v)
```

```python
import functools
import math

import numpy as np
import jax
import jax.numpy as jnp
from jax import lax
from jax.experimental import pallas as pl
from jax.experimental.pallas import tpu as pltpu

F32 = jnp.float32
BF16 = jnp.bfloat16

D_MODEL = 1024
N_HEADS = 8
HEAD = 64
RET_W = N_HEADS * HEAD
MLA_W = N_HEADS * HEAD
ROPE = 32
Q_RANK = 256
KV_RANK = 128
D_FF = 2816
F2 = 2 * D_FF
IN_W = 4 * RET_W + Q_RANK + KV_RANK + ROPE
IN_EXT = 4 * RET_W + Q_RANK + KV_RANK + 128
KPE_LO = 64
ROPE_BASE = 10000.0
EPS = 1e-6
RET_CHUNK = 128
SM_SCALE = (HEAD + ROPE) ** -0.5
LOG2E = math.log2(math.e)
LN2 = math.log(2.0)
NEG = -1e30
LANES = 128
VMEM_LIMIT = 56 * 1024 * 1024

ADAM_LR = 0.001
ADAM_B1 = 0.9
ADAM_B2 = 0.999
ADAM_EPS = 1e-08
ADAM_WD = 0.01
ADAM_STEP = 10


def _cp(*sem):
    return pltpu.CompilerParams(dimension_semantics=sem, vmem_limit_bytes=VMEM_LIMIT)


def _full(shape):
    n = len(shape)
    return pl.BlockSpec(tuple(shape), lambda *_: (0,) * n)


def _row(ts, c):
    return pl.BlockSpec((ts, c), lambda i: (i, 0))


def _hrow(h, ts, c):
    return pl.BlockSpec((h, ts, c), lambda i: (0, i, 0))


def _dot(a, b):
    return jnp.dot(a, b, preferred_element_type=F32)


def _dot_nt(a, b):
    return lax.dot_general(a, b, (((1,), (1,)), ((), ())), preferred_element_type=F32)


def _dot_tn(a, b):
    return lax.dot_general(a, b, (((0,), (0,)), ((), ())), preferred_element_type=F32)


def _dot_hi(a, b):
    return jnp.dot(a, b, preferred_element_type=F32, precision=lax.Precision.HIGHEST)


def _rot_half(x, half):
    w = x.shape[-1]
    lane = lax.broadcasted_iota(jnp.int32, x.shape, x.ndim - 1)
    first = (lane % (2 * half)) < half
    return jnp.where(first, -pltpu.roll(x, w - half, x.ndim - 1), pltpu.roll(x, half, x.ndim - 1))


def _rope(x, cos, sin, half):
    return x * cos + _rot_half(x, half) * sin


def _unrope(dy, cos, sin, half):
    return dy * cos - _rot_half(dy, half) * sin


def _silu(g):
    return g / (1.0 + jnp.exp(-g))


def _rstd(x):
    return lax.rsqrt(jnp.mean(x * x, axis=-1, keepdims=True) + EPS)


def _rope_tables(positions):
    pos = positions.astype(F32)[:, None]
    s = pos.shape[0]
    inv = ROPE_BASE ** (-jnp.arange(0, HEAD, 2, dtype=F32) / HEAD)
    ang = pos * inv
    c, sn = jnp.cos(ang), jnp.sin(ang)
    cos_r = jnp.tile(jnp.concatenate([c, c], -1), (1, 2))
    sin_r = jnp.tile(jnp.concatenate([sn, sn], -1), (1, 2))
    inv = ROPE_BASE ** (-jnp.arange(0, ROPE, 2, dtype=F32) / ROPE)
    ang = pos * inv
    c, sn = jnp.cos(ang), jnp.sin(ang)
    one, zero = jnp.ones((s, KPE_LO), F32), jnp.zeros((s, KPE_LO), F32)
    cos_m = jnp.concatenate([one, c, c, one[:, :LANES - KPE_LO - ROPE]], -1)
    sin_m = jnp.concatenate([zero, sn, sn, zero[:, :LANES - KPE_LO - ROPE]], -1)
    return cos_r, sin_r, cos_m, sin_m


def _ret_consts():
    c = RET_CHUNK
    lg = np.log1p(-np.power(2.0, -5.0 - np.arange(N_HEADS, dtype=np.float64)))
    idx = np.arange(c, dtype=np.float64)
    diff = idx[:, None] - idx[None, :]
    lane_head = np.arange(LANES) // HEAD
    dmask = np.zeros((4, 2, c, c))
    zeta = np.zeros((4, c, LANES))
    xi = np.zeros((4, c, LANES))
    cd = np.zeros((4, LANES, LANES))
    bd = (lane_head[:, None] == lane_head[None, :]).astype(np.float64)
    for j in range(4):
        for hh in range(2):
            dmask[j, hh] = np.where(diff >= 0, np.exp(lg[2 * j + hh] * np.maximum(diff, 0.0)), 0.0)
        lgl = lg[2 * j + lane_head]
        zeta[j] = np.exp(lgl[None, :] * (c - 1.0 - idx[:, None]))
        xi[j] = np.exp(lgl[None, :] * (idx[:, None] + 1.0))
        cd[j] = np.exp(lgl * c)[:, None] * bd
    f = lambda a: jnp.asarray(a, F32)
    return dict(dmask=f(dmask), dmask_t=f(np.swapaxes(dmask, 2, 3)), zeta=f(zeta), xi=f(xi), cd=f(cd), bd=f(bd))


def _f1_call(x, anw, win, cos_r, sin_r, cos_m, sin_m, ts):
    s = x.shape[0]

    def body(x_ref, anw_ref, w_ref, cr_ref, sr_ref, cm_ref, sm_ref,
             q_ref, k_ref, v_ref, g_ref, cq_ref, ckv_ref, kpe_ref, r_ref):
        xv = x_ref[...]
        r = _rstd(xv)
        r_ref[...] = r
        h = (xv * r * anw_ref[...]).astype(BF16)
        cr, sr = cr_ref[...], sr_ref[...]
        qk = _dot(h, w_ref[:, 0:2 * RET_W])
        for j in range(4):
            sl = slice(j * LANES, (j + 1) * LANES)
            q_ref[:, sl] = _rope(qk[:, sl], cr, sr, HEAD // 2).astype(BF16)
            kk = qk[:, RET_W + j * LANES:RET_W + (j + 1) * LANES]
            k_ref[:, sl] = (_rope(kk, cr, sr, HEAD // 2) * (HEAD ** -0.5)).astype(BF16)
        v_ref[...] = _dot(h, w_ref[:, 2 * RET_W:3 * RET_W]).astype(BF16)
        g_ref[...] = _dot(h, w_ref[:, 3 * RET_W:4 * RET_W])
        o = 4 * RET_W
        cq_ref[...] = _dot(h, w_ref[:, o:o + Q_RANK])
        ckv_ref[...] = _dot(h, w_ref[:, o + Q_RANK:o + Q_RANK + KV_RANK])
        kp = _dot(h, w_ref[:, o + Q_RANK + KV_RANK:IN_EXT])
        kpe_ref[...] = _rope(kp, cm_ref[...], sm_ref[...], ROPE // 2)

    sd = jax.ShapeDtypeStruct
    return pl.pallas_call(
        body, name="f1_in_proj", grid=(s // ts,),
        in_specs=[_row(ts, D_MODEL), _full((1, D_MODEL)), _full((D_MODEL, IN_EXT)),
                  _row(ts, LANES), _row(ts, LANES), _row(ts, LANES), _row(ts, LANES)],
        out_specs=[_row(ts, RET_W), _row(ts, RET_W), _row(ts, RET_W), _row(ts, RET_W),
                   _row(ts, Q_RANK), _row(ts, KV_RANK), _row(ts, LANES), _row(ts, 1)],
        out_shape=[sd((s, RET_W), BF16), sd((s, RET_W), BF16), sd((s, RET_W), BF16), sd((s, RET_W), F32),
                   sd((s, Q_RANK), F32), sd((s, KV_RANK), F32), sd((s, LANES), F32), sd((s, 1), F32)],
        compiler_params=_cp("parallel"),
    )(x, anw, win, cos_r, sin_r, cos_m, sin_m)


def _ret_fwd_call(q, k, v, g, gnw, rc, tr):
    s = q.shape[0]
    c = RET_CHUNK
    nc = tr // c

    def body(q_ref, k_ref, v_ref, g_ref, gnw_ref, dm_ref, zeta_ref, xi_ref, cd_ref, bd_ref, o_ref, y_ref, st_ref):
        @pl.when(pl.program_id(1) == 0)
        def _():
            st_ref[...] = jnp.zeros_like(st_ref)

        lane = lax.broadcasted_iota(jnp.int32, (c, LANES), 1)
        bd = bd_ref[...]
        for ci in range(nc):
            rows = slice(ci * c, (ci + 1) * c)
            qc, kc, vc = q_ref[rows, :], k_ref[rows, :], v_ref[rows, :]
            st = st_ref[...]
            o = _dot(qc, st.astype(BF16)) * xi_ref[0]
            for hh in range(2):
                m = (lane >= HEAD) if hh else (lane < HEAD)
                sc = _dot_nt(jnp.where(m, qc, jnp.zeros_like(qc)), kc) * dm_ref[0, hh]
                o = o + jnp.where(m, _dot(sc.astype(BF16), vc), 0.0)
            kz = (kc.astype(F32) * zeta_ref[0]).astype(BF16)
            st_ref[...] = st * cd_ref[0] + _dot_tn(kz, vc) * bd
            o_ref[rows, :] = o
        o = o_ref[...]
        avg = bd * (1.0 / HEAD)
        ctr = o - _dot_hi(o, avg)
        var = _dot_hi(ctr * ctr, avg)
        y_ref[...] = (_silu(g_ref[...]) * (ctr * lax.rsqrt(var + EPS) * gnw_ref[...])).astype(BF16)

    slab = pl.BlockSpec((tr, LANES), lambda j, i: (i, j))
    sd = jax.ShapeDtypeStruct
    return pl.pallas_call(
        body, name="ret_fwd", grid=(4, s // tr),
        in_specs=[slab, slab, slab, slab, pl.BlockSpec((1, LANES), lambda j, i: (0, j)),
                  pl.BlockSpec((1, 2, c, c), lambda j, i: (j, 0, 0, 0)),
                  pl.BlockSpec((1, c, LANES), lambda j, i: (j, 0, 0)),
                  pl.BlockSpec((1, c, LANES), lambda j, i: (j, 0, 0)),
                  pl.BlockSpec((1, LANES, LANES), lambda j, i: (j, 0, 0)),
                  pl.BlockSpec((LANES, LANES), lambda j, i: (0, 0))],
        out_specs=[slab, slab],
        out_shape=[sd((s, RET_W), F32), sd((s, RET_W), BF16)],
        scratch_shapes=[pltpu.VMEM((LANES, LANES), F32)],
        compiler_params=_cp("parallel", "arbitrary"),
    )(q, k, v, g, gnw, rc["dmask"], rc["zeta"], rc["xi"], rc["cd"], rc["bd"])


QK_AUX = HEAD + ROPE
V_AUX = HEAD


def _lane_pair(shape, lo, a, b, rest):
    lane = lax.broadcasted_iota(jnp.int32, shape, len(shape) - 1)
    return jnp.where(lane == lo, a, jnp.where(lane == lo + 1, b, rest))


def _hi_lo(v):
    hi = v.astype(BF16).astype(F32)
    return hi, v - hi


def _mla_pre_call(cq, ckv, kpe, qnw, kvnw, wq, wk, wv, cos_m, sin_m, ts):
    s = cq.shape[0]

    def body(cq_ref, ckv_ref, kpe_ref, qnw_ref, kvnw_ref, wq_ref, wk_ref, wv_ref, cm_ref, sm_ref, q_ref, k_ref, v_ref):
        cqv, ckvv = cq_ref[...], ckv_ref[...]
        cqn = (cqv * _rstd(cqv) * qnw_ref[...]).astype(BF16)
        ckvn = (ckvv * _rstd(ckvv) * kvnw_ref[...]).astype(BF16)
        cm, sm = cm_ref[...], sm_ref[...]
        kp = _lane_pair((ts, LANES), QK_AUX, -1.0, -1.0, kpe_ref[...])
        for h in range(N_HEADS):
            qh = _rope(_dot(cqn, wq_ref[h]), cm, sm, ROPE // 2)
            q_ref[h] = (qh * (SM_SCALE * LOG2E)).astype(BF16)
            k_ref[h] = (_dot(ckvn, wk_ref[h]) + kp).astype(BF16)
            v_ref[h] = _lane_pair((ts, LANES), V_AUX, 1.0, 1.0, _dot(ckvn, wv_ref[h])).astype(BF16)

    sd = jax.ShapeDtypeStruct
    hm = sd((N_HEADS, s, LANES), BF16)
    return pl.pallas_call(
        body, name="mla_pre", grid=(s // ts,),
        in_specs=[_row(ts, Q_RANK), _row(ts, KV_RANK), _row(ts, LANES), _full((1, Q_RANK)), _full((1, KV_RANK)),
                  _full((N_HEADS, Q_RANK, LANES)), _full((N_HEADS, KV_RANK, LANES)), _full((N_HEADS, KV_RANK, LANES)),
                  _row(ts, LANES), _row(ts, LANES)],
        out_specs=[_hrow(N_HEADS, ts, LANES)] * 3,
        out_shape=[hm, hm, hm],
        compiler_params=_cp("parallel"),
    )(cq, ckv, kpe, qnw, kvnw, wq, wk, wv, cos_m, sin_m)


def _flash_fwd_call(q, k, v, tb):
    s = q.shape[1]
    nb = s // tb

    def body(q_ref, k_ref, v_ref, o_ref, qb_ref, m_ref, acc_ref):
        qi, ki = pl.program_id(0), pl.program_id(1)

        @pl.when(ki == 0)
        def _():
            m_ref[...] = jnp.full_like(m_ref, NEG)
            acc_ref[...] = jnp.zeros_like(acc_ref)

        def step(masked):
            if masked:
                keep = lax.broadcasted_iota(jnp.int32, (tb, tb), 1) <= lax.broadcasted_iota(jnp.int32, (tb, tb), 0)
            for h in range(N_HEADS):
                sc = _dot_nt(q_ref[h], k_ref[h])
                if masked:
                    sc = jnp.where(keep, sc, NEG)
                m_prev = m_ref[h]
                m_new = jnp.maximum(m_prev, jnp.max(sc, axis=1, keepdims=True))
                pe = jnp.exp2(sc - jnp.tile(m_new, (1, tb // LANES)))
                m_ref[h] = m_new
                acc_ref[h] = acc_ref[h] * jnp.exp2(m_prev - m_new) + _dot(pe.astype(BF16), v_ref[h])

        @pl.when(ki < qi)
        def _():
            step(False)

        @pl.when(ki == qi)
        def _():
            step(True)
            lane = lax.broadcasted_iota(jnp.int32, (tb, LANES), 1)
            for p in range(N_HEADS // 2):
                outs = []
                for h in (2 * p, 2 * p + 1):
                    acc = acc_ref[h]
                    l = acc[:, V_AUX:V_AUX + 1]
                    outs.append(acc * (1.0 / l))
                    hi, lo = _hi_lo(m_ref[h][:, 0:1] + jnp.log(l) * LOG2E)
                    qb_ref[h] = _lane_pair((tb, LANES), QK_AUX, hi, lo, q_ref[h].astype(F32)).astype(BF16)
                o_ref[:, p * LANES:(p + 1) * LANES] = jnp.where(lane < HEAD, outs[0], pltpu.roll(outs[1], HEAD, 1)).astype(BF16)

    sd = jax.ShapeDtypeStruct
    qspec = pl.BlockSpec((N_HEADS, tb, LANES), lambda qi, ki: (0, qi, 0))
    kspec = pl.BlockSpec((N_HEADS, tb, LANES), lambda qi, ki: (0, jnp.minimum(ki, qi), 0))
    return pl.pallas_call(
        body, name="mla_flash_fwd", grid=(nb, nb),
        in_specs=[qspec, kspec, kspec],
        out_specs=[pl.BlockSpec((tb, MLA_W), lambda qi, ki: (qi, 0)), qspec],
        out_shape=[sd((s, MLA_W), BF16), sd((N_HEADS, s, LANES), BF16)],
        scratch_shapes=[pltpu.VMEM((N_HEADS, tb, LANES), F32), pltpu.VMEM((N_HEADS, tb, LANES), F32)],
        compiler_params=_cp("parallel", "arbitrary"),
    )(q, k, v)


def _out_proj_call(x, yret, ymla, wout, ts):
    s = x.shape[0]

    def body(x_ref, yr_ref, ym_ref, w_ref, x1_ref, r_ref):
        x1 = x_ref[...] + _dot(yr_ref[...], w_ref[0:RET_W, :]) + _dot(ym_ref[...], w_ref[RET_W:, :])
        x1_ref[...] = x1
        r_ref[...] = _rstd(x1)

    sd = jax.ShapeDtypeStruct
    return pl.pallas_call(
        body, name="out_proj", grid=(s // ts,),
        in_specs=[_row(ts, D_MODEL), _row(ts, RET_W), _row(ts, MLA_W), _full((D_MODEL, D_MODEL))],
        out_specs=[_row(ts, D_MODEL), _row(ts, 1)],
        out_shape=[sd((s, D_MODEL), F32), sd((s, 1), F32)],
        compiler_params=_cp("parallel"),
    )(x, yret, ymla, wout)


def _up_proj_call(x1, r2, fnw, wup, ts):
    s = x1.shape[0]
    nchunk = 4
    cw = F2 // nchunk

    def body(x_ref, r_ref, fnw_ref, w_ref, u_ref):
        h = (x_ref[...] * r_ref[...] * fnw_ref[...]).astype(BF16)
        for j in range(nchunk):
            u_ref[:, j * cw:(j + 1) * cw] = _dot(h, w_ref[:, j * cw:(j + 1) * cw]).astype(BF16)

    return pl.pallas_call(
        body, name="up_proj", grid=(s // ts,),
        in_specs=[_row(ts, D_MODEL), _row(ts, 1), _full((1, D_MODEL)), _full((D_MODEL, F2))],
        out_specs=_row(ts, F2),
        out_shape=jax.ShapeDtypeStruct((s, F2), BF16),
        compiler_params=_cp("parallel"),
    )(x1, r2, fnw, wup)


def _shifted(u, hal):
    row = lax.broadcasted_iota(jnp.int32, u.shape, 0)
    u1 = jnp.where(row == 0, hal[7:8, :], pltpu.roll(u, 1, 0))
    u2 = jnp.where(row == 0, hal[6:7, :], jnp.where(row == 1, hal[7:8, :], pltpu.roll(u, 2, 0)))
    return u1, u2


def _conv_tile(u_ref, hal_ref, w_ref, b_ref, first):
    u = u_ref[...].astype(F32)
    hal = jnp.where(first, 0.0, hal_ref[...].astype(F32))
    u1, u2 = _shifted(u, hal)
    w = w_ref[...]
    return b_ref[...] + w[0:1, :] * u2 + w[1:2, :] * u1 + w[2:3, :] * u, u1, u2, u


def _gate_specs(ts, tf, rows_inner=False):
    nf = D_FF // tf
    hb = ts // 8

    def spec(shape, fn):
        return pl.BlockSpec(shape, (lambda j, i: fn(i, j)) if rows_inner else fn)

    return [
        spec((ts, tf), lambda i, j: (i, j)),
        spec((8, tf), lambda i, j: (jnp.maximum(i * hb - 1, 0), j)),
        spec((ts, tf), lambda i, j: (i, j + nf)),
        spec((8, tf), lambda i, j: (jnp.maximum(i * hb - 1, 0), j + nf)),
        spec((3, tf), lambda i, j: (0, j)),
        spec((3, tf), lambda i, j: (0, j + nf)),
        spec((1, tf), lambda i, j: (0, j)),
        spec((1, tf), lambda i, j: (0, j + nf)),
    ]


def _gate_call(u, cw, cb, ts, tf):
    s = u.shape[0]

    def body(ug_ref, hg_ref, uv_ref, hv_ref, wg_ref, wv_ref, bg_ref, bv_ref, a_ref):
        first = pl.program_id(0) == 0
        gate = _conv_tile(ug_ref, hg_ref, wg_ref, bg_ref, first)[0]
        val = _conv_tile(uv_ref, hv_ref, wv_ref, bv_ref, first)[0]
        a_ref[...] = (_silu(gate) * val).astype(BF16)

    return pl.pallas_call(
        body, name="conv_gate", grid=(s // ts, D_FF // tf),
        in_specs=_gate_specs(ts, tf),
        out_specs=pl.BlockSpec((ts, tf), lambda i, j: (i, j)),
        out_shape=jax.ShapeDtypeStruct((s, D_FF), BF16),
        compiler_params=_cp("parallel", "parallel"),
    )(u, u, u, u, cw, cw, cb, cb)


def _down_proj_call(x1, a, wdown, ts):
    s = x1.shape[0]

    def body(x_ref, a_ref, w_ref, x2_ref):
        x2_ref[...] = x_ref[...] + _dot(a_ref[...], w_ref[...])

    return pl.pallas_call(
        body, name="down_proj", grid=(s // ts,),
        in_specs=[_row(ts, D_MODEL), _row(ts, D_FF), _full((D_FF, D_MODEL))],
        out_specs=_row(ts, D_MODEL),
        out_shape=jax.ShapeDtypeStruct((s, D_MODEL), F32),
        compiler_params=_cp("parallel"),
    )(x1, a, wdown)


def _prep_weights(w):
    win = w["w_in"]
    pad = lambda n: jnp.zeros((D_MODEL, n), win.dtype)
    win_ext = jnp.concatenate([win[:, :IN_W - ROPE], pad(KPE_LO), win[:, IN_W - ROPE:], pad(LANES - KPE_LO - ROPE)], -1)
    wuq = w["w_uq"].reshape(Q_RANK, N_HEADS, HEAD + ROPE)
    wq = jnp.concatenate([wuq, jnp.zeros((Q_RANK, N_HEADS, LANES - HEAD - ROPE), wuq.dtype)], -1).transpose(1, 0, 2)
    wukv = w["w_ukv"].reshape(KV_RANK, N_HEADS, 2 * HEAD)
    zk = jnp.zeros((KV_RANK, N_HEADS, HEAD), wukv.dtype)
    wk = jnp.concatenate([wukv[:, :, :HEAD], zk], -1).transpose(1, 0, 2)
    wv = jnp.concatenate([wukv[:, :, HEAD:], zk], -1).transpose(1, 0, 2)
    c = lambda a: a.astype(BF16)
    return dict(win=c(win_ext), wq=c(wq), wk=c(wk), wv=c(wv), wout=c(w["w_out"]), wup=c(w["w_up"]), wdown=c(w["w_down"]))


def _tiles(s):
    return dict(ts=min(s, 512), tr=min(s, 512), tb=min(s, 512), tg=min(s, 512), tf=D_FF // 2, t2=min(s, 256))


def _forward(x, positions, w, small):
    s = x.shape[0]
    t = _tiles(s)
    pw = _prep_weights(w)
    cos_r, sin_r, cos_m, sin_m = _rope_tables(positions)
    rc = _ret_consts()
    q, k, v, g, cq, ckv, kpe, r1 = _f1_call(x, small["attn_norm_w"], pw["win"], cos_r, sin_r, cos_m, sin_m, t["ts"])
    o_ret, y_ret = _ret_fwd_call(q, k, v, g, small["ret_gn_w"], rc, t["tr"])
    mq, mk, mv = _mla_pre_call(cq, ckv, kpe, small["mla_q_norm_w"], small["mla_kv_norm_w"],
                               pw["wq"], pw["wk"], pw["wv"], cos_m, sin_m, t["ts"])
    y_mla, mqb = _flash_fwd_call(mq, mk, mv, t["tb"])
    x1, r2 = _out_proj_call(x, y_ret, y_mla, pw["wout"], t["ts"])
    u = _up_proj_call(x1, r2, small["ffn_norm_w"], pw["wup"], t["ts"])
    a = _gate_call(u, w["conv_w"], small["conv_b"], t["tg"], t["tf"])
    x2 = _down_proj_call(x1, a, pw["wdown"], t["ts"])
    return dict(pw=pw, tabs=(cos_r, sin_r, cos_m, sin_m), rc=rc, q=q, k=k, v=v, g=g, cq=cq, ckv=ckv, kpe=kpe, r1=r1,
                o_ret=o_ret, y_ret=y_ret, mqb=mqb, mk=mk, mv=mv, y_mla=y_mla, x1=x1, r2=r2, u=u, a=a, x2=x2)


def _norm_bwd(dh, xh, r, nw):
    dxn = dh * nw
    return r * (dxn - xh * jnp.mean(dxn * xh, axis=-1, keepdims=True))


def _zero_first(first, *refs):
    @pl.when(first)
    def _():
        for ref in refs:
            ref[...] = jnp.zeros_like(ref)


def _colsum(v):
    return jnp.sum(v, axis=0, keepdims=True)


def _dsilu(g, sg):
    return sg * (1.0 + g * (1.0 - sg))


def _loss_call(x2, tgt, fw, ts):
    s = x2.shape[0]

    def body(x_ref, t_ref, fw_ref, dx_ref, loss_ref, gfw_ref):
        _zero_first(pl.program_id(0) == 0, loss_ref, gfw_ref)
        xv = x_ref[...]
        r = _rstd(xv)
        xh = xv * r
        fwv = fw_ref[...]
        e = xh * fwv - t_ref[...]
        loss_ref[...] += (0.5 / D_MODEL) * _colsum(jnp.sum(e * e, axis=1, keepdims=True))
        dy = e * (1.0 / D_MODEL)
        gfw_ref[...] += _colsum(dy * xh)
        dx_ref[...] = _norm_bwd(dy, xh, r, fwv)

    sd = jax.ShapeDtypeStruct
    return pl.pallas_call(
        body, name="loss_bwd", grid=(s // ts,),
        in_specs=[_row(ts, D_MODEL), _row(ts, D_MODEL), _full((1, D_MODEL))],
        out_specs=[_row(ts, D_MODEL), _full((1, 1)), _full((1, D_MODEL))],
        out_shape=[sd((s, D_MODEL), F32), sd((1, 1), F32), sd((1, D_MODEL), F32)],
        compiler_params=_cp("arbitrary"),
    )(x2, tgt, fw)


def _ffn_bwd1_call(dx2, u, cw, cb, wdown, ts, tf):
    s = dx2.shape[0]
    nf = D_FF // tf

    def body(dx_ref, wd_ref, ug_ref, hg_ref, uv_ref, hv_ref, wg_ref, wv_ref, bg_ref, bv_ref,
             dgate_ref, dval_ref, dwd_ref, dcwg_ref, dcwv_ref, dcbg_ref, dcbv_ref):
        first = pl.program_id(1) == 0
        _zero_first(first, dwd_ref, dcwg_ref, dcwv_ref, dcbg_ref, dcbv_ref)
        gate, g1, g2, g0 = _conv_tile(ug_ref, hg_ref, wg_ref, bg_ref, first)
        val, v1, v2, v0 = _conv_tile(uv_ref, hv_ref, wv_ref, bv_ref, first)
        dxb = dx_ref[...].astype(BF16)
        da = _dot_nt(dxb, wd_ref[...])
        sg = 1.0 / (1.0 + jnp.exp(-gate))
        sl = gate * sg
        dgate = da * val * _dsilu(gate, sg)
        dval = da * sl
        dgate_ref[...] = dgate.astype(BF16)
        dval_ref[...] = dval.astype(BF16)
        dwd_ref[...] += _dot_tn((sl * val).astype(BF16), dxb)
        for ref, d, taps in ((dcwg_ref, dgate, (g2, g1, g0)), (dcwv_ref, dval, (v2, v1, v0))):
            for t in range(3):
                ref[t:t + 1, :] += _colsum(d * taps[t])
        dcbg_ref[...] += _colsum(dgate)
        dcbv_ref[...] += _colsum(dval)

    sd = jax.ShapeDtypeStruct
    colacc = lambda r: pl.BlockSpec((r, tf), lambda j, i: (0, j))
    return pl.pallas_call(
        body, name="ffn_bwd_gate", grid=(nf, s // ts),
        in_specs=[pl.BlockSpec((ts, D_MODEL), lambda j, i: (i, 0)), pl.BlockSpec((tf, D_MODEL), lambda j, i: (j, 0))]
        + _gate_specs(ts, tf, rows_inner=True),
        out_specs=[pl.BlockSpec((ts, tf), lambda j, i: (i, j)), pl.BlockSpec((ts, tf), lambda j, i: (i, j)),
                   pl.BlockSpec((tf, D_MODEL), lambda j, i: (j, 0)), colacc(3), colacc(3), colacc(1), colacc(1)],
        out_shape=[sd((s, D_FF), BF16), sd((s, D_FF), BF16), sd((D_FF, D_MODEL), F32),
                   sd((3, D_FF), F32), sd((3, D_FF), F32), sd((1, D_FF), F32), sd((1, D_FF), F32)],
        compiler_params=_cp("parallel", "arbitrary"),
    )(dx2, wdown, u, u, u, u, cw, cw, cb, cb)


def _shifted_up(d, hal):
    n = d.shape[0]
    row = lax.broadcasted_iota(jnp.int32, d.shape, 0)
    d1 = jnp.where(row == n - 1, hal[0:1, :], pltpu.roll(d, n - 1, 0))
    d2 = jnp.where(row == n - 2, hal[0:1, :], jnp.where(row == n - 1, hal[1:2, :], pltpu.roll(d, n - 2, 0)))
    return d1, d2


def _ffn_bwd2_call(dgate, dval, cw, wup, x1, r2, fnw, dx2, ts):
    s = dx2.shape[0]
    nt = s // ts
    hb = ts // 8
    nxt = pl.BlockSpec((8, D_FF), lambda i: (jnp.minimum((i + 1) * hb, s // 8 - 1), 0))

    def body(dg_ref, hg_ref, dv_ref, hv_ref, cw_ref, wup_ref, x_ref, r_ref, fnw_ref, dx2_ref, du_ref, dx1_ref, dfnw_ref):
        i = pl.program_id(0)
        _zero_first(i == 0, dfnw_ref)
        dh = jnp.zeros((ts, D_MODEL), F32)
        for d_ref, h_ref, off in ((dg_ref, hg_ref, 0), (dv_ref, hv_ref, D_FF)):
            d = d_ref[...].astype(F32)
            hal = jnp.where(i == nt - 1, 0.0, h_ref[...].astype(F32))
            d1, d2 = _shifted_up(d, hal)
            w = cw_ref[:, off:off + D_FF]
            du = (w[2:3, :] * d + w[1:2, :] * d1 + w[0:1, :] * d2).astype(BF16)
            du_ref[:, off:off + D_FF] = du
            dh = dh + _dot_nt(du, wup_ref[:, off:off + D_FF])
        r = r_ref[...]
        xh = x_ref[...] * r
        dfnw_ref[...] += _colsum(dh * xh)
        dx1_ref[...] = dx2_ref[...] + _norm_bwd(dh, xh, r, fnw_ref[...])

    sd = jax.ShapeDtypeStruct
    return pl.pallas_call(
        body, name="ffn_bwd_up", grid=(nt,),
        in_specs=[_row(ts, D_FF), nxt, _row(ts, D_FF), nxt, _full((3, F2)), _full((D_MODEL, F2)),
                  _row(ts, D_MODEL), _row(ts, 1), _full((1, D_MODEL)), _row(ts, D_MODEL)],
        out_specs=[_row(ts, F2), _row(ts, D_MODEL), _full((1, D_MODEL))],
        out_shape=[sd((s, F2), BF16), sd((s, D_MODEL), F32), sd((1, D_MODEL), F32)],
        compiler_params=_cp("arbitrary"),
    )(dgate, dgate, dval, dval, cw, wup, x1, r2, fnw, dx2)


def _dw_norm_call(x, r, nw, b, ts, tn, name):
    s, n = b.shape
    k = x.shape[1]

    def body(x_ref, r_ref, nw_ref, b_ref, dw_ref):
        _zero_first(pl.program_id(1) == 0, dw_ref)
        h = (x_ref[...] * r_ref[...] * nw_ref[...]).astype(BF16)
        dw_ref[...] += _dot_tn(h, b_ref[...])

    return pl.pallas_call(
        body, name=name, grid=(n // tn, s // ts),
        in_specs=[pl.BlockSpec((ts, k), lambda j, i: (i, 0)), pl.BlockSpec((ts, 1), lambda j, i: (i, 0)),
                  pl.BlockSpec((1, k), lambda j, i: (0, 0)), pl.BlockSpec((ts, tn), lambda j, i: (i, j))],
        out_specs=pl.BlockSpec((k, tn), lambda j, i: (0, j)),
        out_shape=jax.ShapeDtypeStruct((k, n), F32),
        compiler_params=_cp("parallel", "arbitrary"),
    )(x, r, nw, b)


def _out_bwd_call(dx1, yret, ymla, wout, ts):
    s = dx1.shape[0]

    def body(dx_ref, yr_ref, ym_ref, w_ref, dyr_ref, do_ref, dwo_ref):
        _zero_first(pl.program_id(0) == 0, dwo_ref)
        dxb = dx_ref[...].astype(BF16)
        dmix = _dot_nt(dxb, w_ref[...])
        dyr_ref[...] = dmix[:, :RET_W]
        ym = ym_ref[...]
        lane = lax.broadcasted_iota(jnp.int32, (ts, LANES), 1)
        for p in range(N_HEADS // 2):
            dom = dmix[:, RET_W + p * LANES:RET_W + (p + 1) * LANES]
            prod = dom * ym[:, p * LANES:(p + 1) * LANES].astype(F32)
            for hh in range(2):
                mine = (lane >= HEAD) if hh else (lane < HEAD)
                hi, lo = _hi_lo(jnp.sum(jnp.where(mine, prod, 0.0), axis=1, keepdims=True))
                base = jnp.where(lane < HEAD, pltpu.roll(dom, HEAD, 1) if hh else dom, 0.0)
                do_ref[2 * p + hh] = _lane_pair((ts, LANES), V_AUX, -hi, -lo, base).astype(BF16)
        dwo_ref[0:RET_W, :] += _dot_tn(yr_ref[...], dxb)
        dwo_ref[RET_W:, :] += _dot_tn(ym, dxb)

    sd = jax.ShapeDtypeStruct
    return pl.pallas_call(
        body, name="out_proj_bwd", grid=(s // ts,),
        in_specs=[_row(ts, D_MODEL), _row(ts, RET_W), _row(ts, MLA_W), _full((D_MODEL, D_MODEL))],
        out_specs=[_row(ts, RET_W), _hrow(N_HEADS, ts, LANES), _full((D_MODEL, D_MODEL))],
        out_shape=[sd((s, RET_W), F32), sd((N_HEADS, s, LANES), BF16), sd((D_MODEL, D_MODEL), F32)],
        compiler_params=_cp("arbitrary"),
    )(dx1, yret, ymla, wout)


def _ret_bwd_q_call(q, k, v, o, g, dy, gnw, rc, cos_r, sin_r, tr):
    s = q.shape[0]
    c = RET_CHUNK
    nc = tr // c

    def body(q_ref, k_ref, v_ref, o_ref, g_ref, dy_ref, gnw_ref, dm_ref, zeta_ref, xi_ref, cd_ref, bd_ref, cr_ref, sr_ref,
             dq_ref, dg_ref, do_ref, dgnw_ref, st_ref):
        _zero_first(pl.program_id(1) == 0, st_ref, dgnw_ref)
        bd = bd_ref[...]
        avg = bd * (1.0 / HEAD)
        ov = o_ref[...]
        ctr = ov - _dot_hi(ov, avg)
        rs = lax.rsqrt(_dot_hi(ctr * ctr, avg) + EPS)
        oh = ctr * rs
        gg, dyv, gnw_v = g_ref[...], dy_ref[...], gnw_ref[...]
        sg = 1.0 / (1.0 + jnp.exp(-gg))
        sl = gg * sg
        dg_ref[...] = (dyv * oh * gnw_v * _dsilu(gg, sg)).astype(BF16)
        dgnw_ref[...] += _colsum(dyv * sl * oh)
        doh = dyv * sl * gnw_v
        dov = (rs * (doh - _dot_hi(doh, avg) - oh * _dot_hi(doh * oh, avg))).astype(BF16)
        do_ref[...] = dov
        lane = lax.broadcasted_iota(jnp.int32, (c, LANES), 1)
        for ci in range(nc):
            rows = slice(ci * c, (ci + 1) * c)
            kc, vc, doc = k_ref[rows, :], v_ref[rows, :], dov[rows, :]
            st = st_ref[...]
            dq = _dot_nt(doc, st.astype(BF16)) * xi_ref[0]
            for hh in range(2):
                m = (lane >= HEAD) if hh else (lane < HEAD)
                a = _dot_nt(jnp.where(m, doc, jnp.zeros_like(doc)), vc) * dm_ref[0, hh]
                dq = dq + jnp.where(m, _dot(a.astype(BF16), kc), 0.0)
            kz = (kc.astype(F32) * zeta_ref[0]).astype(BF16)
            st_ref[...] = st * cd_ref[0] + _dot_tn(kz, vc) * bd
            dq_ref[rows, :] = _unrope(dq, cr_ref[rows, :], sr_ref[rows, :], HEAD // 2).astype(BF16)

    slab = pl.BlockSpec((tr, LANES), lambda j, i: (i, j))
    tab = pl.BlockSpec((tr, LANES), lambda j, i: (i, 0))
    vec = pl.BlockSpec((1, LANES), lambda j, i: (0, j))
    sd = jax.ShapeDtypeStruct
    return pl.pallas_call(
        body, name="ret_bwd_q", grid=(4, s // tr),
        in_specs=[slab, slab, slab, slab, slab, slab, vec,
                  pl.BlockSpec((1, 2, c, c), lambda j, i: (j, 0, 0, 0)),
                  pl.BlockSpec((1, c, LANES), lambda j, i: (j, 0, 0)),
                  pl.BlockSpec((1, c, LANES), lambda j, i: (j, 0, 0)),
                  pl.BlockSpec((1, LANES, LANES), lambda j, i: (j, 0, 0)),
                  pl.BlockSpec((LANES, LANES), lambda j, i: (0, 0)), tab, tab],
        out_specs=[slab, slab, slab, vec],
        out_shape=[sd((s, RET_W), BF16), sd((s, RET_W), BF16), sd((s, RET_W), BF16), sd((1, RET_W), F32)],
        scratch_shapes=[pltpu.VMEM((LANES, LANES), F32)],
        compiler_params=_cp("parallel", "arbitrary"),
    )(q, k, v, o, g, dy, gnw, rc["dmask"], rc["zeta"], rc["xi"], rc["cd"], rc["bd"], cos_r, sin_r)


def _ret_bwd_kv_call(q, k, v, do, rc, cos_r, sin_r, tr):
    s = q.shape[0]
    c = RET_CHUNK
    nc = tr // c
    nt = s // tr

    def body(q_ref, k_ref, v_ref, do_ref, dm_ref, zeta_ref, xi_ref, cd_ref, bd_ref, cr_ref, sr_ref, dk_ref, dv_ref, gs_ref):
        _zero_first(pl.program_id(1) == 0, gs_ref)
        bd = bd_ref[...]
        lane = lax.broadcasted_iota(jnp.int32, (c, LANES), 1)
        for ci in reversed(range(nc)):
            rows = slice(ci * c, (ci + 1) * c)
            qc, kc, vc, doc = q_ref[rows, :], k_ref[rows, :], v_ref[rows, :], do_ref[rows, :]
            gs = gs_ref[...]
            gb = gs.astype(BF16)
            dk = _dot_nt(vc, gb) * zeta_ref[0]
            dv = _dot(kc, gb) * zeta_ref[0]
            for hh in range(2):
                m = (lane >= HEAD) if hh else (lane < HEAD)
                a = _dot_nt(jnp.where(m, doc, jnp.zeros_like(doc)), vc) * dm_ref[0, hh]
                dk = dk + jnp.where(m, _dot_tn(a.astype(BF16), qc), 0.0)
                p = _dot_nt(jnp.where(m, qc, jnp.zeros_like(qc)), kc) * dm_ref[0, hh]
                dv = dv + jnp.where(m, _dot_tn(p.astype(BF16), doc), 0.0)
            qx = (qc.astype(F32) * xi_ref[0]).astype(BF16)
            gs_ref[...] = gs * cd_ref[0] + _dot_tn(qx, doc) * bd
            dk_ref[rows, :] = (_unrope(dk, cr_ref[rows, :], sr_ref[rows, :], HEAD // 2) * (HEAD ** -0.5)).astype(BF16)
            dv_ref[rows, :] = dv.astype(BF16)

    slab = pl.BlockSpec((tr, LANES), lambda j, i: (nt - 1 - i, j))
    tab = pl.BlockSpec((tr, LANES), lambda j, i: (nt - 1 - i, 0))
    sd = jax.ShapeDtypeStruct
    return pl.pallas_call(
        body, name="ret_bwd_kv", grid=(4, nt),
        in_specs=[slab, slab, slab, slab,
                  pl.BlockSpec((1, 2, c, c), lambda j, i: (j, 0, 0, 0)),
                  pl.BlockSpec((1, c, LANES), lambda j, i: (j, 0, 0)),
                  pl.BlockSpec((1, c, LANES), lambda j, i: (j, 0, 0)),
                  pl.BlockSpec((1, LANES, LANES), lambda j, i: (j, 0, 0)),
                  pl.BlockSpec((LANES, LANES), lambda j, i: (0, 0)), tab, tab],
        out_specs=[slab, slab],
        out_shape=[sd((s, RET_W), BF16), sd((s, RET_W), BF16)],
        scratch_shapes=[pltpu.VMEM((LANES, LANES), F32)],
        compiler_params=_cp("parallel", "arbitrary"),
    )(q, k, v, do, rc["dmask"], rc["zeta"], rc["xi"], rc["cd"], rc["bd"], cos_r, sin_r)


def _flash_bwd_dq_call(qb, k, v, do, tb):
    s = qb.shape[1]
    nb = s // tb

    def body(q_ref, k_ref, v_ref, do_ref, dq_ref, acc_ref):
        qi, ki = pl.program_id(0), pl.program_id(1)
        _zero_first(ki == 0, acc_ref)

        def step(masked):
            if masked:
                keep = lax.broadcasted_iota(jnp.int32, (tb, tb), 1) <= lax.broadcasted_iota(jnp.int32, (tb, tb), 0)
            for h in range(N_HEADS):
                sc = _dot_nt(q_ref[h], k_ref[h])
                if masked:
                    sc = jnp.where(keep, sc, NEG)
                ds = (jnp.exp2(sc) * _dot_nt(do_ref[h], v_ref[h])).astype(BF16)
                acc_ref[h] += _dot(ds, k_ref[h])

        @pl.when(ki < qi)
        def _():
            step(False)

        @pl.when(ki == qi)
        def _():
            step(True)
            dq_ref[...] = (acc_ref[...] * SM_SCALE).astype(BF16)

    qspec = pl.BlockSpec((N_HEADS, tb, LANES), lambda qi, ki: (0, qi, 0))
    kspec = pl.BlockSpec((N_HEADS, tb, LANES), lambda qi, ki: (0, jnp.minimum(ki, qi), 0))
    return pl.pallas_call(
        body, name="mla_flash_bwd_dq", grid=(nb, nb),
        in_specs=[qspec, kspec, kspec, qspec],
        out_specs=qspec,
        out_shape=jax.ShapeDtypeStruct((N_HEADS, s, LANES), BF16),
        scratch_shapes=[pltpu.VMEM((N_HEADS, tb, LANES), F32)],
        compiler_params=_cp("parallel", "arbitrary"),
    )(qb, k, v, do)


def _flash_bwd_dkv_call(qb, k, v, do, tb):
    s = qb.shape[1]
    nb = s // tb

    def body(q_ref, k_ref, v_ref, do_ref, dk_ref, dv_ref, dka_ref, dva_ref):
        ki, qi = pl.program_id(0), pl.program_id(1)
        _zero_first(qi == 0, dka_ref, dva_ref)

        def step(masked):
            if masked:
                keep = lax.broadcasted_iota(jnp.int32, (tb, tb), 0) <= lax.broadcasted_iota(jnp.int32, (tb, tb), 1)
            for h in range(N_HEADS):
                st = _dot_nt(k_ref[h], q_ref[h])
                if masked:
                    st = jnp.where(keep, st, NEG)
                pt = jnp.exp2(st)
                dob = do_ref[h]
                dva_ref[h] += _dot(pt.astype(BF16), dob)
                dst = (pt * _dot_nt(v_ref[h], dob)).astype(BF16)
                dka_ref[h] += _dot(dst, q_ref[h])

        @pl.when(qi > ki)
        def _():
            step(False)

        @pl.when(qi == ki)
        def _():
            step(True)

        @pl.when(qi == nb - 1)
        def _():
            dk_ref[...] = (dka_ref[...] * LN2).astype(BF16)
            dv_ref[...] = dva_ref[...].astype(BF16)

    kspec = pl.BlockSpec((N_HEADS, tb, LANES), lambda ki, qi: (0, ki, 0))
    qspec = pl.BlockSpec((N_HEADS, tb, LANES), lambda ki, qi: (0, jnp.maximum(qi, ki), 0))
    sd = jax.ShapeDtypeStruct((N_HEADS, s, LANES), BF16)
    return pl.pallas_call(
        body, name="mla_flash_bwd_dkv", grid=(nb, nb),
        in_specs=[qspec, kspec, kspec, qspec],
        out_specs=[kspec, kspec],
        out_shape=[sd, sd],
        scratch_shapes=[pltpu.VMEM((N_HEADS, tb, LANES), F32), pltpu.VMEM((N_HEADS, tb, LANES), F32)],
        compiler_params=_cp("parallel", "arbitrary"),
    )(qb, k, v, do)


def _mla_post_call(dq, dk, dv, cq, ckv, qnw, kvnw, wq, wk, wv, cos_m, sin_m, ts):
    s = cq.shape[0]

    def body(dq_ref, dk_ref, dv_ref, cq_ref, ckv_ref, qnw_ref, kvnw_ref, wq_ref, wk_ref, wv_ref, cm_ref, sm_ref,
             dcq_ref, dckv_ref, dkpe_ref, dwq_ref, dwk_ref, dwv_ref, dqnw_ref, dkvnw_ref):
        _zero_first(pl.program_id(0) == 0, dwq_ref, dwk_ref, dwv_ref, dqnw_ref, dkvnw_ref)
        cqv, ckvv = cq_ref[...], ckv_ref[...]
        rq, rkv = _rstd(cqv), _rstd(ckvv)
        qh_, kvh_ = cqv * rq, ckvv * rkv
        qnw_v, kvnw_v = qnw_ref[...], kvnw_ref[...]
        cqn = (qh_ * qnw_v).astype(BF16)
        ckvn = (kvh_ * kvnw_v).astype(BF16)
        cm, sm = cm_ref[...], sm_ref[...]
        dcqn = jnp.zeros((ts, Q_RANK), F32)
        dckvn = jnp.zeros((ts, KV_RANK), F32)
        dkpe = jnp.zeros((ts, LANES), F32)
        for h in range(N_HEADS):
            dqu = _unrope(dq_ref[h].astype(F32), cm, sm, ROPE // 2).astype(BF16)
            dwq_ref[h] += _dot_tn(cqn, dqu)
            dcqn = dcqn + _dot_nt(dqu, wq_ref[h])
            dkb, dvb = dk_ref[h], dv_ref[h]
            dkpe = dkpe + dkb.astype(F32)
            dwk_ref[h] += _dot_tn(ckvn, dkb)
            dwv_ref[h] += _dot_tn(ckvn, dvb)
            dckvn = dckvn + _dot_nt(dkb, wk_ref[h]) + _dot_nt(dvb, wv_ref[h])
        lane = lax.broadcasted_iota(jnp.int32, (ts, LANES), 1)
        dkpe = jnp.where((lane >= KPE_LO) & (lane < KPE_LO + ROPE), dkpe, 0.0)
        dkpe_ref[...] = _unrope(dkpe, cm, sm, ROPE // 2).astype(BF16)
        dqnw_ref[...] += _colsum(dcqn * qh_)
        dkvnw_ref[...] += _colsum(dckvn * kvh_)
        dcq_ref[...] = _norm_bwd(dcqn, qh_, rq, qnw_v).astype(BF16)
        dckv_ref[...] = _norm_bwd(dckvn, kvh_, rkv, kvnw_v).astype(BF16)

    sd = jax.ShapeDtypeStruct
    hm = _hrow(N_HEADS, ts, LANES)
    return pl.pallas_call(
        body, name="mla_post", grid=(s // ts,),
        in_specs=[hm, hm, hm, _row(ts, Q_RANK), _row(ts, KV_RANK), _full((1, Q_RANK)), _full((1, KV_RANK)),
                  _full((N_HEADS, Q_RANK, LANES)), _full((N_HEADS, KV_RANK, LANES)), _full((N_HEADS, KV_RANK, LANES)),
                  _row(ts, LANES), _row(ts, LANES)],
        out_specs=[_row(ts, Q_RANK), _row(ts, KV_RANK), _row(ts, LANES),
                   _full((N_HEADS, Q_RANK, LANES)), _full((N_HEADS, KV_RANK, LANES)), _full((N_HEADS, KV_RANK, LANES)),
                   _full((1, Q_RANK)), _full((1, KV_RANK))],
        out_shape=[sd((s, Q_RANK), BF16), sd((s, KV_RANK), BF16), sd((s, LANES), BF16),
                   sd((N_HEADS, Q_RANK, LANES), F32), sd((N_HEADS, KV_RANK, LANES), F32), sd((N_HEADS, KV_RANK, LANES), F32),
                   sd((1, Q_RANK), F32), sd((1, KV_RANK), F32)],
        compiler_params=_cp("arbitrary"),
    )(dq, dk, dv, cq, ckv, qnw, kvnw, wq, wk, wv, cos_m, sin_m)


def _in_bwd_call(parts, x, r1, anw, dx1, win, ts):
    s = x.shape[0]
    widths = [p.shape[1] for p in parts]
    np_ = len(parts)

    def body(*refs):
        p_refs = refs[:np_]
        x_ref, r_ref, anw_ref, dx1_ref, w_ref, dx_ref, dw_ref, danw_ref = refs[np_:]
        _zero_first(pl.program_id(0) == 0, dw_ref, danw_ref)
        dproj = jnp.concatenate([p[...] for p in p_refs], axis=-1)
        r, anw_v = r_ref[...], anw_ref[...]
        xh = x_ref[...] * r
        dw_ref[...] += _dot_tn((xh * anw_v).astype(BF16), dproj)
        dh = _dot_nt(dproj, w_ref[...])
        danw_ref[...] += _colsum(dh * xh)
        dx_ref[...] = dx1_ref[...] + _norm_bwd(dh, xh, r, anw_v)

    sd = jax.ShapeDtypeStruct
    return pl.pallas_call(
        body, name="in_proj_bwd", grid=(s // ts,),
        in_specs=[_row(ts, w) for w in widths]
        + [_row(ts, D_MODEL), _row(ts, 1), _full((1, D_MODEL)), _row(ts, D_MODEL), _full((D_MODEL, IN_EXT))],
        out_specs=[_row(ts, D_MODEL), _full((D_MODEL, IN_EXT)), _full((1, D_MODEL))],
        out_shape=[sd((s, D_MODEL), F32), sd((D_MODEL, IN_EXT), F32), sd((1, D_MODEL), F32)],
        compiler_params=_cp("arbitrary"),
    )(*parts, x, r1, anw, dx1, win)


def _local_step(x, positions, tgt, w, small):
    s = x.shape[0]
    t = _tiles(s)
    f = _forward(x, positions, w, small)
    pw, rc = f["pw"], f["rc"]
    cos_r, sin_r, cos_m, sin_m = f["tabs"]
    dx2, loss, g_fw = _loss_call(f["x2"], tgt, small["final_norm_w"], t["ts"])
    dgate, dval, g_wd, dcw_g, dcw_v, dcb_g, dcb_v = _ffn_bwd1_call(dx2, f["u"], w["conv_w"], small["conv_b"], pw["wdown"], t["tg"], t["tf"])
    du, dx1, g_fnw = _ffn_bwd2_call(dgate, dval, w["conv_w"], pw["wup"], f["x1"], f["r2"], small["ffn_norm_w"], dx2, t["t2"])
    g_wup = _dw_norm_call(f["x1"], f["r2"], small["ffn_norm_w"], du, t["ts"], F2 // 4, "dw_up")
    dy_ret, do, g_wout = _out_bwd_call(dx1, f["y_ret"], f["y_mla"], pw["wout"], t["ts"])
    drq, dg, do_ret, g_gnw = _ret_bwd_q_call(f["q"], f["k"], f["v"], f["o_ret"], f["g"], dy_ret, small["ret_gn_w"], rc, cos_r, sin_r, t["tr"])
    drk, drv = _ret_bwd_kv_call(f["q"], f["k"], f["v"], do_ret, rc, cos_r, sin_r, t["tr"])
    dmq = _flash_bwd_dq_call(f["mqb"], f["mk"], f["mv"], do, t["tb"])
    dmk, dmv = _flash_bwd_dkv_call(f["mqb"], f["mk"], f["mv"], do, t["tb"])
    dcq, dckv, dkpe, g_wq, g_wk, g_wv, g_qnw, g_kvnw = _mla_post_call(
        dmq, dmk, dmv, f["cq"], f["ckv"], small["mla_q_norm_w"], small["mla_kv_norm_w"], pw["wq"], pw["wk"], pw["wv"], cos_m, sin_m, t["ts"])
    gx, g_win_ext, g_anw = _in_bwd_call([drq, drk, drv, dg, dcq, dckv, dkpe], x, f["r1"], small["attn_norm_w"], dx1, pw["win"], t["ts"])
    lo = IN_W - ROPE
    g_win = jnp.concatenate([g_win_ext[:, :lo], g_win_ext[:, lo + KPE_LO:lo + KPE_LO + ROPE]], -1)
    g_wuq = g_wq.transpose(1, 0, 2)[:, :, :HEAD + ROPE].reshape(Q_RANK, N_HEADS * (HEAD + ROPE))
    g_wukv = jnp.concatenate([g_wk[:, :, :HEAD], g_wv[:, :, :HEAD]], -1).transpose(1, 0, 2).reshape(KV_RANK, 2 * MLA_W)
    gw = dict(w_in=g_win, w_uq=g_wuq, w_ukv=g_wukv, w_out=g_wout, w_up=g_wup,
              conv_w=jnp.concatenate([dcw_g, dcw_v], -1), w_down=g_wd)
    gs = dict(attn_norm_w=g_anw, ret_gn_w=g_gnw, mla_q_norm_w=g_qnw, mla_kv_norm_w=g_kvnw, ffn_norm_w=g_fnw,
              conv_b=jnp.concatenate([dcb_g, dcb_v], -1), final_norm_w=g_fw)
    return loss, gx, gw, gs


MESH_ID = pl.DeviceIdType.MESH
ANY = pl.BlockSpec(memory_space=pl.ANY)
VMEM_SPEC = pl.BlockSpec(memory_space=pltpu.VMEM)
N_DEV = 8
PACKED = (("w_in", (D_MODEL, IN_W // 4), 1), ("w_uq", (Q_RANK, 192), 1), ("w_ukv", (KV_RANK, 256), 1),
          ("w_out", (D_MODEL // 4, D_MODEL), 0), ("w_up", (D_MODEL, F2 // 4), 1), ("w_down", (D_FF // 4, D_MODEL), 0))
PACK_ROWS = 24576
HALF_ROWS = PACK_ROWS // 2
ADD_ROWS = 2048


def _mesh_pos():
    return lax.axis_index("x"), lax.axis_index("y"), lax.axis_index("c")


def _other_chips(x, y):
    return [(1 - x, y), (x, 1 - y), (1 - x, 1 - y)]


def _pack(parts, dtype):
    flat = jnp.concatenate([p.reshape(-1).astype(dtype) for p in parts])
    flat = jnp.concatenate([flat, jnp.zeros((PACK_ROWS * LANES - flat.shape[0],), dtype)])
    return flat.reshape(2, HALF_ROWS, LANES)


def _unpack(flat):
    out, off = [], 0
    for _, (r, c), _ in PACKED:
        out.append(flat[..., off:off + r * c].reshape(flat.shape[:-1] + (r, c)))
        off += r * c
    return out


def _all_gather_call(packed):
    _, h, _ = packed.shape

    def body(src_ref, out_ref, send_sems, recv_sems):
        x, y, c = _mesh_pos()
        chips = _other_chips(x, y)

        def rcopy(k, src, dst, dev):
            return pltpu.make_async_remote_copy(src_ref=src, dst_ref=dst, send_sem=send_sems.at[k], recv_sem=recv_sems.at[k],
                                                device_id=dev, device_id_type=MESH_ID)

        first = [rcopy(j, src_ref.at[c], out_ref.at[j, c], (cx, cy, c)) for j, (cx, cy) in enumerate(chips)]
        for cp in first:
            cp.start()
        passed = []
        for j in range(3):
            first[j].wait_recv()
            cp = rcopy(3 + j, out_ref.at[j, c], out_ref.at[j, c], (x, y, 1 - c))
            cp.start()
            passed.append(cp)
        for j in range(3):
            rcopy(3 + j, src_ref.at[c], out_ref.at[j, 1 - c], (x, y, 1 - c)).wait_recv()
        for cp in first + passed:
            cp.wait_send()

    return pl.pallas_call(
        body, name="weights_all_gather",
        in_specs=[ANY], out_specs=ANY,
        out_shape=jax.ShapeDtypeStruct((3, 2, h, LANES), packed.dtype),
        scratch_shapes=[pltpu.SemaphoreType.DMA((6,)), pltpu.SemaphoreType.DMA((6,))],
    )(packed)


def _rs_sibling_call(g):
    _, _, h, _ = g.shape

    def body(g_ref, buf_ref, send_sem, recv_sem):
        x, y, c = _mesh_pos()
        cp = pltpu.make_async_remote_copy(src_ref=g_ref.at[1 - c], dst_ref=buf_ref, send_sem=send_sem, recv_sem=recv_sem,
                                          device_id=(x, y, 1 - c), device_id_type=MESH_ID)
        cp.start()
        cp.wait()

    return pl.pallas_call(
        body, name="grads_rs_sibling",
        in_specs=[ANY], out_specs=ANY,
        out_shape=jax.ShapeDtypeStruct((4, h, LANES), g.dtype),
        scratch_shapes=[pltpu.SemaphoreType.DMA, pltpu.SemaphoreType.DMA],
    )(g)


def _rs_add1_call(g, buf, c):
    _, _, h, _ = g.shape

    def body(c_ref, g_ref, b_ref, p_ref, pb_ref):
        p = g_ref[...] + b_ref[...]
        p_ref[...] = p
        pb_ref[...] = p.astype(BF16)

    blk = pl.BlockSpec((None, ADD_ROWS, LANES), lambda s, i, c_ref: (s, i, 0))
    return pl.pallas_call(
        body, name="grads_rs_add_sibling",
        grid_spec=pltpu.PrefetchScalarGridSpec(
            num_scalar_prefetch=1, grid=(4, h // ADD_ROWS),
            in_specs=[pl.BlockSpec((None, None, ADD_ROWS, LANES), lambda s, i, c_ref: (c_ref[0], s, i, 0)), blk],
            out_specs=[blk, blk]),
        out_shape=[jax.ShapeDtypeStruct((4, h, LANES), F32), jax.ShapeDtypeStruct((4, h, LANES), BF16)],
        compiler_params=_cp("parallel", "parallel"),
    )(c, g, buf)


def _rs_chips_call(pb):
    _, h, _ = pb.shape

    def body(pb_ref, buf_ref, send_sems, recv_sems):
        x, y, c = _mesh_pos()
        cps = [pltpu.make_async_remote_copy(src_ref=pb_ref.at[2 * cx + cy], dst_ref=buf_ref.at[j], send_sem=send_sems.at[j],
                                            recv_sem=recv_sems.at[j], device_id=(cx, cy, c), device_id_type=MESH_ID)
               for j, (cx, cy) in enumerate(_other_chips(x, y))]
        for cp in cps:
            cp.start()
        for cp in cps:
            cp.wait()

    return pl.pallas_call(
        body, name="grads_rs_chips",
        in_specs=[ANY], out_specs=ANY,
        out_shape=jax.ShapeDtypeStruct((3, h, LANES), pb.dtype),
        scratch_shapes=[pltpu.SemaphoreType.DMA((3,)), pltpu.SemaphoreType.DMA((3,))],
    )(pb)


def _rs_add2_call(p, buf, sm):
    _, h, _ = p.shape

    def body(sm_ref, p_ref, b_ref, f_ref):
        f_ref[...] = ((p_ref[...] + b_ref[0].astype(F32)) + b_ref[1].astype(F32)) + b_ref[2].astype(F32)

    return pl.pallas_call(
        body, name="grads_rs_add_chips",
        grid_spec=pltpu.PrefetchScalarGridSpec(
            num_scalar_prefetch=1, grid=(h // ADD_ROWS,),
            in_specs=[pl.BlockSpec((None, ADD_ROWS, LANES), lambda i, sm_ref: (sm_ref[0], i, 0)),
                      pl.BlockSpec((3, ADD_ROWS, LANES), lambda i, sm_ref: (0, i, 0))],
            out_specs=pl.BlockSpec((ADD_ROWS, LANES), lambda i, sm_ref: (i, 0))),
        out_shape=jax.ShapeDtypeStruct((h, LANES), F32),
        compiler_params=_cp("parallel"),
    )(sm, p, buf)


def _rs_share_call(f):
    h, _ = f.shape

    def body(f_ref, out_ref, send_sem, recv_sem):
        x, y, c = _mesh_pos()
        cp = pltpu.make_async_remote_copy(src_ref=f_ref, dst_ref=out_ref, send_sem=send_sem, recv_sem=recv_sem,
                                          device_id=(x, y, 1 - c), device_id_type=MESH_ID)
        cp.start()
        cp.wait()

    return pl.pallas_call(
        body, name="grads_rs_share",
        in_specs=[ANY], out_specs=ANY,
        out_shape=jax.ShapeDtypeStruct((h, LANES), f.dtype),
        scratch_shapes=[pltpu.SemaphoreType.DMA, pltpu.SemaphoreType.DMA],
    )(f)


def _exchange8_call(vec, reduce, name):
    rows = vec.shape[0]

    def body(v_ref, out_ref, *rest):
        slots, send_sems, recv_sems = (rest if reduce else (out_ref,) + rest)
        x, y, c = _mesh_pos()
        me = 4 * x + 2 * y + c
        slots[me] = v_ref[...]

        def rcopy(k, to_me):
            bx, by, bc = (k >> 2) & 1, (k >> 1) & 1, k & 1
            px, py, pc = (1 - x if bx else x), (1 - y if by else y), (1 - c if bc else c)
            slot = 4 * px + 2 * py + pc if to_me else me
            return pltpu.make_async_remote_copy(src_ref=v_ref, dst_ref=slots.at[slot], send_sem=send_sems.at[k - 1],
                                                recv_sem=recv_sems.at[k - 1], device_id=(px, py, pc), device_id_type=MESH_ID)

        for k in range(1, N_DEV):
            rcopy(k, False).start()
        for k in range(1, N_DEV):
            rcopy(k, True).wait_recv()
        for k in range(1, N_DEV):
            rcopy(k, False).wait_send()
        if reduce:
            tot = slots[0]
            for d in range(1, N_DEV):
                tot = tot + slots[d]
            out_ref[...] = tot

    stack = jax.ShapeDtypeStruct((N_DEV, rows, LANES), F32)
    return pl.pallas_call(
        body, name=name,
        in_specs=[VMEM_SPEC], out_specs=VMEM_SPEC,
        out_shape=jax.ShapeDtypeStruct((rows, LANES), F32) if reduce else stack,
        scratch_shapes=([pltpu.VMEM((N_DEV, rows, LANES), F32)] if reduce else [])
        + [pltpu.SemaphoreType.DMA((N_DEV - 1,)), pltpu.SemaphoreType.DMA((N_DEV - 1,))],
    )(vec)


def _adamw_call(w, g, m, v, name):
    r, c = w.shape
    rb = r if r <= 256 else (256 if r % 256 == 0 else 352)
    assert r % rb == 0

    def body(w_ref, g_ref, m_ref, v_ref, d_ref, nm_ref, nv_ref):
        gv = g_ref[...]
        nm = ADAM_B1 * m_ref[...] + (1.0 - ADAM_B1) * gv
        nv = ADAM_B2 * v_ref[...] + (1.0 - ADAM_B2) * jnp.square(gv)
        m_hat = nm / (1.0 - ADAM_B1 ** ADAM_STEP)
        v_hat = nv / (1.0 - ADAM_B2 ** ADAM_STEP)
        d_ref[...] = -ADAM_LR * (m_hat / (jnp.sqrt(v_hat) + ADAM_EPS) + ADAM_WD * w_ref[...])
        nm_ref[...] = nm
        nv_ref[...] = nv

    spec = pl.BlockSpec((rb, c), lambda i: (i, 0))
    sd = jax.ShapeDtypeStruct((r, c), F32)
    return pl.pallas_call(
        body, name=name, grid=(r // rb,),
        in_specs=[spec] * 4, out_specs=[spec] * 3, out_shape=[sd, sd, sd],
        compiler_params=_cp("parallel"),
    )(w, g, m, v)


SMALL = (("attn_norm_w", D_MODEL), ("ret_gn_w", RET_W), ("mla_q_norm_w", Q_RANK), ("mla_kv_norm_w", KV_RANK),
         ("ffn_norm_w", D_MODEL), ("conv_b", F2), ("final_norm_w", D_MODEL))
WEIGHT_ORDER = ("attn_norm_w", "w_in", "ret_gn_w", "mla_q_norm_w", "w_uq", "mla_kv_norm_w", "w_ukv", "w_out",
                "ffn_norm_w", "w_up", "conv_w", "conv_b", "w_down", "final_norm_w")


def _pad_rows(flat, rows):
    return jnp.concatenate([flat, jnp.zeros((rows * LANES - flat.shape[0],), flat.dtype)]).reshape(rows, LANES)


def kernel(x, positions, attn_norm_w, w_in, ret_gn_w, mla_q_norm_w, w_uq, mla_kv_norm_w, w_ukv, w_out, ffn_norm_w, w_up, conv_w, conv_b, w_down, final_norm_w, loss_target, m_attn_norm_w, m_w_in, m_ret_gn_w, m_mla_q_norm_w, m_w_uq, m_mla_kv_norm_w, m_w_ukv, m_w_out, m_ffn_norm_w, m_w_up, m_conv_w, m_conv_b, m_w_down, m_final_norm_w, v_attn_norm_w, v_w_in, v_ret_gn_w, v_mla_q_norm_w, v_w_uq, v_mla_kv_norm_w, v_w_ukv, v_w_out, v_ffn_norm_w, v_w_up, v_conv_w, v_conv_b, v_w_down, v_final_norm_w):
    args = dict(locals())
    cx, cy, cc = _mesh_pos()
    sm = 2 * cx + cy

    packed = _pack([args[n][0] for n, _, _ in PACKED], BF16)
    others = _all_gather_call(packed)
    by_xor = jnp.stack([packed, others[1], others[0], others[2]])
    gathered = jnp.take(by_xor, jnp.bitwise_xor(jnp.arange(4, dtype=jnp.int32), sm), axis=0)
    full = {}
    for (n, (r, c), axis), piece in zip(PACKED, _unpack(gathered.reshape(4, PACK_ROWS * LANES))):
        full[n] = piece.transpose(1, 0, 2).reshape(r, 4 * c) if axis == 1 else piece.reshape(4 * r, c)
    cw_rows = 40
    cw_all = _exchange8_call(_pad_rows(conv_w[0].reshape(-1), cw_rows), False, "conv_w_all_gather")
    cw_all = cw_all[0::2].reshape(4, cw_rows * LANES)[:, :3 * F2 // 4].reshape(4, 3, F2 // 4)
    full["conv_w"] = cw_all.transpose(1, 0, 2).reshape(3, F2)
    small = {n: args[n].reshape(1, d) for n, d in SMALL}

    loss, gx, gw, gs = _local_step(x[0], positions[0], loss_target[0], full, small)

    shards = []
    for n, (r, c), axis in PACKED:
        g = gw[n]
        shards.append(g.reshape(r, 4, c).transpose(1, 0, 2).reshape(4, r * c) if axis == 1 else g.reshape(4, r * c))
    gflat = jnp.concatenate(shards + [jnp.zeros((4, PACK_ROWS * LANES - sum(s.shape[1] for s in shards)), F32)], axis=1)
    gpk = gflat.reshape(4, 2, HALF_ROWS, LANES).transpose(1, 0, 2, 3)
    p, pb = _rs_add1_call(gpk, _rs_sibling_call(gpk), cc.reshape(1).astype(jnp.int32))
    fin = _rs_add2_call(p, _rs_chips_call(pb), sm.reshape(1).astype(jnp.int32))
    sib = _rs_share_call(fin)
    both = jnp.where(cc == 0, jnp.stack([fin, sib]), jnp.stack([sib, fin]))
    red = dict(zip([n for n, _, _ in PACKED], _unpack(both.reshape(PACK_ROWS * LANES))))

    vec = jnp.concatenate([gs[n].reshape(-1) for n, _ in SMALL] + [gw["conv_w"].reshape(-1), loss.reshape(-1)])
    n_small = sum(d for _, d in SMALL)
    tot = _exchange8_call(_pad_rows(vec, 216), True, "small_all_reduce").reshape(-1)
    off = 0
    for n, d in SMALL:
        red[n] = tot[off:off + d].reshape(1, d)
        off += d
    red["conv_w"] = lax.dynamic_slice(tot[off:off + 3 * F2].reshape(3, F2), (0, sm * (F2 // 4)), (3, F2 // 4))
    loss_tot = tot[off + 3 * F2]

    grads, deltas, new_m, new_v = [], [], [], []
    for n in WEIGHT_ORDER:
        shape = args[n].shape
        two_d = (1, shape[0]) if len(shape) == 1 else shape[-2:]
        g = red[n].reshape(two_d)
        d, nm, nv = _adamw_call(args[n].reshape(two_d), g, args["m_" + n].reshape(two_d), args["v_" + n].reshape(two_d), "adamw_" + n)
        grads.append(g.reshape(shape))
        deltas.append(d.reshape(shape))
        new_m.append(nm.reshape(shape))
        new_v.append(nv.reshape(shape))
    return (loss_tot, gx[None], *grads, *deltas, *new_m, *new_v)
```

```python
import functools
import math

import numpy as np
import jax
import jax.numpy as jnp
from jax import lax
from jax.experimental import pallas as pl
from jax.experimental.pallas import tpu as pltpu

F32 = jnp.float32
BF16 = jnp.bfloat16

D_MODEL = 1024
N_HEADS = 8
HEAD = 64
RET_W = N_HEADS * HEAD
MLA_W = N_HEADS * HEAD
ROPE = 32
Q_RANK = 256
KV_RANK = 128
D_FF = 2816
F2 = 2 * D_FF
IN_W = 4 * RET_W + Q_RANK + KV_RANK + ROPE
IN_EXT = 4 * RET_W + Q_RANK + KV_RANK + 128
KPE_LO = 64
ROPE_BASE = 10000.0
EPS = 1e-6
RET_CHUNK = 128
SM_SCALE = (HEAD + ROPE) ** -0.5
LOG2E = math.log2(math.e)
LN2 = math.log(2.0)
NEG = -1e30
LANES = 128
VMEM_LIMIT = 56 * 1024 * 1024

ADAM_LR = 0.001
ADAM_B1 = 0.9
ADAM_B2 = 0.999
ADAM_EPS = 1e-08
ADAM_WD = 0.01
ADAM_STEP = 10


def _cp(*sem):
    return pltpu.CompilerParams(dimension_semantics=sem, vmem_limit_bytes=VMEM_LIMIT)


def _full(shape):
    n = len(shape)
    return pl.BlockSpec(tuple(shape), lambda *_: (0,) * n)


def _row(ts, c):
    return pl.BlockSpec((ts, c), lambda i: (i, 0))


def _hrow(h, ts, c):
    return pl.BlockSpec((h, ts, c), lambda i: (0, i, 0))


def _dot(a, b):
    return jnp.dot(a, b, preferred_element_type=F32)


def _dot_nt(a, b):
    return lax.dot_general(a, b, (((1,), (1,)), ((), ())), preferred_element_type=F32)


def _dot_tn(a, b):
    return lax.dot_general(a, b, (((0,), (0,)), ((), ())), preferred_element_type=F32)


def _dot_hi(a, b):
    hi = a.astype(BF16)
    lo = (a - hi.astype(F32)).astype(BF16)
    bb = b.astype(BF16)
    return _dot(hi, bb) + _dot(lo, bb)


def _rot_half(x, half):
    w = x.shape[-1]
    lane = lax.broadcasted_iota(jnp.int32, x.shape, x.ndim - 1)
    first = (lane % (2 * half)) < half
    return jnp.where(first, -pltpu.roll(x, w - half, x.ndim - 1), pltpu.roll(x, half, x.ndim - 1))


def _rope(x, cos, sin, half):
    return x * cos + _rot_half(x, half) * sin


def _unrope(dy, cos, sin, half):
    return dy * cos - _rot_half(dy, half) * sin


def _silu(g):
    return g / (1.0 + jnp.exp(-g))


def _rstd(x):
    return lax.rsqrt(jnp.mean(x * x, axis=-1, keepdims=True) + EPS)


def _rope_tables(positions):
    pos = positions.astype(F32)[:, None]
    s = pos.shape[0]
    inv = ROPE_BASE ** (-jnp.arange(0, HEAD, 2, dtype=F32) / HEAD)
    ang = pos * inv
    c, sn = jnp.cos(ang), jnp.sin(ang)
    cos_r = jnp.tile(jnp.concatenate([c, c], -1), (1, 2))
    sin_r = jnp.tile(jnp.concatenate([sn, sn], -1), (1, 2))
    inv = ROPE_BASE ** (-jnp.arange(0, ROPE, 2, dtype=F32) / ROPE)
    ang = pos * inv
    c, sn = jnp.cos(ang), jnp.sin(ang)
    one, zero = jnp.ones((s, KPE_LO), F32), jnp.zeros((s, KPE_LO), F32)
    cos_m = jnp.concatenate([one, c, c, one[:, :LANES - KPE_LO - ROPE]], -1)
    sin_m = jnp.concatenate([zero, sn, sn, zero[:, :LANES - KPE_LO - ROPE]], -1)
    return cos_r, sin_r, cos_m, sin_m


def _ret_consts():
    c = RET_CHUNK
    lg = np.log1p(-np.power(2.0, -5.0 - np.arange(N_HEADS, dtype=np.float64)))
    idx = np.arange(c, dtype=np.float64)
    diff = idx[:, None] - idx[None, :]
    lane_head = np.arange(LANES) // HEAD
    dmask = np.zeros((4, 2, c, c))
    zeta = np.zeros((4, c, LANES))
    xi = np.zeros((4, c, LANES))
    cd = np.zeros((4, LANES, LANES))
    bd = (lane_head[:, None] == lane_head[None, :]).astype(np.float64)
    for j in range(4):
        for hh in range(2):
            dmask[j, hh] = np.where(diff >= 0, np.exp(lg[2 * j + hh] * np.maximum(diff, 0.0)), 0.0)
        lgl = lg[2 * j + lane_head]
        zeta[j] = np.exp(lgl[None, :] * (c - 1.0 - idx[:, None]))
        xi[j] = np.exp(lgl[None, :] * (idx[:, None] + 1.0))
        cd[j] = np.exp(lgl * c)[:, None] * bd
    f = lambda a: jnp.asarray(a, F32)
    side = lambda d: np.concatenate([d[:, 0], d[:, 1]], axis=-1)
    return dict(dmask=f(side(dmask)), dmask_t=f(side(np.swapaxes(dmask, 2, 3))), zeta=f(zeta), xi=f(xi), cd=f(cd), bd=f(bd))


def _f1_call(x, anw, win, cos_r, sin_r, cos_m, sin_m, ts):
    s = x.shape[0]

    def body(x_ref, anw_ref, w_ref, cr_ref, sr_ref, cm_ref, sm_ref,
             q_ref, k_ref, v_ref, g_ref, cq_ref, ckv_ref, kpe_ref, r_ref):
        xv = x_ref[...]
        r = _rstd(xv)
        r_ref[...] = r
        h = (xv * r * anw_ref[...]).astype(BF16)
        cr, sr = cr_ref[...], sr_ref[...]
        qk = _dot(h, w_ref[:, 0:2 * RET_W])
        for j in range(4):
            sl = slice(j * LANES, (j + 1) * LANES)
            q_ref[:, sl] = _rope(qk[:, sl], cr, sr, HEAD // 2).astype(BF16)
            kk = qk[:, RET_W + j * LANES:RET_W + (j + 1) * LANES]
            k_ref[:, sl] = (_rope(kk, cr, sr, HEAD // 2) * (HEAD ** -0.5)).astype(BF16)
        v_ref[...] = _dot(h, w_ref[:, 2 * RET_W:3 * RET_W]).astype(BF16)
        g_ref[...] = _dot(h, w_ref[:, 3 * RET_W:4 * RET_W])
        o = 4 * RET_W
        cq_ref[...] = _dot(h, w_ref[:, o:o + Q_RANK])
        ckv_ref[...] = _dot(h, w_ref[:, o + Q_RANK:o + Q_RANK + KV_RANK])
        kp = _dot(h, w_ref[:, o + Q_RANK + KV_RANK:IN_EXT])
        kpe_ref[...] = _rope(kp, cm_ref[...], sm_ref[...], ROPE // 2)

    sd = jax.ShapeDtypeStruct
    return pl.pallas_call(
        body, name="f1_in_proj", grid=(s // ts,),
        in_specs=[_row(ts, D_MODEL), _full((1, D_MODEL)), _full((D_MODEL, IN_EXT)),
                  _row(ts, LANES), _row(ts, LANES), _row(ts, LANES), _row(ts, LANES)],
        out_specs=[_row(ts, RET_W), _row(ts, RET_W), _row(ts, RET_W), _row(ts, RET_W),
                   _row(ts, Q_RANK), _row(ts, KV_RANK), _row(ts, LANES), _row(ts, 1)],
        out_shape=[sd((s, RET_W), BF16), sd((s, RET_W), BF16), sd((s, RET_W), BF16), sd((s, RET_W), F32),
                   sd((s, Q_RANK), F32), sd((s, KV_RANK), F32), sd((s, LANES), F32), sd((s, 1), F32)],
        compiler_params=_cp("parallel"),
    )(x, anw, win, cos_r, sin_r, cos_m, sin_m)


def _stack_heads(a):
    lo = lax.broadcasted_iota(jnp.int32, a.shape, 1) < HEAD
    zero = jnp.zeros_like(a)
    return jnp.concatenate([jnp.where(lo, a, zero), jnp.where(lo, zero, a)], axis=0)


def _pair_product(a, b2, decay2, w2):
    return _dot((_dot_nt(a, b2) * decay2).astype(BF16), w2)


def _ret_fwd_call(q, k, v, g, gnw, rc, tr):
    s = q.shape[0]
    c = RET_CHUNK
    nc = tr // c

    def body(q_ref, k_ref, v_ref, g_ref, gnw_ref, dm_ref, zeta_ref, xi_ref, cd_ref, bd_ref, o_ref, y_ref, st_ref):
        @pl.when(pl.program_id(1) == 0)
        def _():
            st_ref[...] = jnp.zeros_like(st_ref)

        lane = lax.broadcasted_iota(jnp.int32, (c, LANES), 1)
        bd = bd_ref[...]
        chunks = [slice(ci * c, (ci + 1) * c) for ci in range(nc)]
        contrib = [_dot_tn((k_ref[rows, :].astype(F32) * zeta_ref[0]).astype(BF16), v_ref[rows, :]) * bd for rows in chunks]
        st, states = st_ref[...], []
        for ci in range(nc):
            states.append(st.astype(BF16))
            st = st * cd_ref[0] + contrib[ci]
        st_ref[...] = st
        for ci, rows in enumerate(chunks):
            qc = q_ref[rows, :]
            o_ref[rows, :] = (_dot(qc, states[ci]) * xi_ref[0]
                              + _pair_product(qc, _stack_heads(k_ref[rows, :]), dm_ref[0], _stack_heads(v_ref[rows, :])))
        o = o_ref[...]
        avg = bd * (1.0 / HEAD)
        ctr = o - _dot_hi(o, avg)
        var = _dot_hi(ctr * ctr, avg)
        y_ref[...] = (_silu(g_ref[...]) * (ctr * lax.rsqrt(var + EPS) * gnw_ref[...])).astype(BF16)

    slab = pl.BlockSpec((tr, LANES), lambda j, i: (i, j))
    sd = jax.ShapeDtypeStruct
    return pl.pallas_call(
        body, name="ret_fwd", grid=(4, s // tr),
        in_specs=[slab, slab, slab, slab, pl.BlockSpec((1, LANES), lambda j, i: (0, j)),
                  pl.BlockSpec((1, c, 2 * c), lambda j, i: (j, 0, 0)),
                  pl.BlockSpec((1, c, LANES), lambda j, i: (j, 0, 0)),
                  pl.BlockSpec((1, c, LANES), lambda j, i: (j, 0, 0)),
                  pl.BlockSpec((1, LANES, LANES), lambda j, i: (j, 0, 0)),
                  pl.BlockSpec((LANES, LANES), lambda j, i: (0, 0))],
        out_specs=[slab, slab],
        out_shape=[sd((s, RET_W), F32), sd((s, RET_W), BF16)],
        scratch_shapes=[pltpu.VMEM((LANES, LANES), F32)],
        compiler_params=_cp("parallel", "arbitrary"),
    )(q, k, v, g, gnw, rc["dmask"], rc["zeta"], rc["xi"], rc["cd"], rc["bd"])


QK_AUX = HEAD + ROPE
V_AUX = HEAD


def _lane_pair(shape, lo, a, b, rest):
    lane = lax.broadcasted_iota(jnp.int32, shape, len(shape) - 1)
    return jnp.where(lane == lo, a, jnp.where(lane == lo + 1, b, rest))


def _hi_lo(v):
    hi = v.astype(BF16).astype(F32)
    return hi, v - hi


def _mla_pre_call(cq, ckv, kpe, qnw, kvnw, wq, wk, wv, cos_m, sin_m, ts):
    s = cq.shape[0]

    def body(cq_ref, ckv_ref, kpe_ref, qnw_ref, kvnw_ref, wq_ref, wk_ref, wv_ref, cm_ref, sm_ref, q_ref, k_ref, v_ref):
        cqv, ckvv = cq_ref[...], ckv_ref[...]
        cqn = (cqv * _rstd(cqv) * qnw_ref[...]).astype(BF16)
        ckvn = (ckvv * _rstd(ckvv) * kvnw_ref[...]).astype(BF16)
        cm, sm = cm_ref[...], sm_ref[...]
        kp = _lane_pair((ts, LANES), QK_AUX, -1.0, -1.0, kpe_ref[...])
        for h in range(N_HEADS):
            qh = _rope(_dot(cqn, wq_ref[h]), cm, sm, ROPE // 2)
            q_ref[h] = (qh * (SM_SCALE * LOG2E)).astype(BF16)
            k_ref[h] = (_dot(ckvn, wk_ref[h]) + kp).astype(BF16)
            v_ref[h] = _lane_pair((ts, LANES), V_AUX, 1.0, 1.0, _dot(ckvn, wv_ref[h])).astype(BF16)

    sd = jax.ShapeDtypeStruct
    hm = sd((N_HEADS, s, LANES), BF16)
    return pl.pallas_call(
        body, name="mla_pre", grid=(s // ts,),
        in_specs=[_row(ts, Q_RANK), _row(ts, KV_RANK), _row(ts, LANES), _full((1, Q_RANK)), _full((1, KV_RANK)),
                  _full((N_HEADS, Q_RANK, LANES)), _full((N_HEADS, KV_RANK, LANES)), _full((N_HEADS, KV_RANK, LANES)),
                  _row(ts, LANES), _row(ts, LANES)],
        out_specs=[_hrow(N_HEADS, ts, LANES)] * 3,
        out_shape=[hm, hm, hm],
        compiler_params=_cp("parallel"),
    )(cq, ckv, kpe, qnw, kvnw, wq, wk, wv, cos_m, sin_m)


def _flash_fwd_call(q, k, v, tb):
    s = q.shape[1]
    nb = s // tb

    def body(q_ref, k_ref, v_ref, o_ref, qb_ref, m_ref, acc_ref):
        qi, ki = pl.program_id(0), pl.program_id(1)

        @pl.when(ki == 0)
        def _():
            m_ref[...] = jnp.full_like(m_ref, NEG)
            acc_ref[...] = jnp.zeros_like(acc_ref)

        def step(masked):
            if masked:
                keep = lax.broadcasted_iota(jnp.int32, (tb, tb), 1) <= lax.broadcasted_iota(jnp.int32, (tb, tb), 0)
            for h in range(N_HEADS):
                sc = _dot_nt(q_ref[h], k_ref[h])
                if masked:
                    sc = jnp.where(keep, sc, NEG)
                m_prev = m_ref[h]
                m_new = jnp.maximum(m_prev, jnp.max(sc, axis=1, keepdims=True))
                pe = jnp.exp2(sc - jnp.tile(m_new, (1, tb // LANES)))
                m_ref[h] = m_new
                acc_ref[h] = acc_ref[h] * jnp.exp2(m_prev - m_new) + _dot(pe.astype(BF16), v_ref[h])

        @pl.when(ki < qi)
        def _():
            step(False)

        @pl.when(ki == qi)
        def _():
            step(True)
            lane = lax.broadcasted_iota(jnp.int32, (tb, LANES), 1)
            for p in range(N_HEADS // 2):
                outs = []
                for h in (2 * p, 2 * p + 1):
                    acc = acc_ref[h]
                    l = acc[:, V_AUX:V_AUX + 1]
                    outs.append(acc * (1.0 / l))
                    hi, lo = _hi_lo(m_ref[h][:, 0:1] + jnp.log(l) * LOG2E)
                    qb_ref[h] = _lane_pair((tb, LANES), QK_AUX, hi, lo, q_ref[h].astype(F32)).astype(BF16)
                o_ref[:, p * LANES:(p + 1) * LANES] = jnp.where(lane < HEAD, outs[0], pltpu.roll(outs[1], HEAD, 1)).astype(BF16)

    sd = jax.ShapeDtypeStruct
    qspec = pl.BlockSpec((N_HEADS, tb, LANES), lambda qi, ki: (0, qi, 0))
    kspec = pl.BlockSpec((N_HEADS, tb, LANES), lambda qi, ki: (0, jnp.minimum(ki, qi), 0))
    return pl.pallas_call(
        body, name="mla_flash_fwd", grid=(nb, nb),
        in_specs=[qspec, kspec, kspec],
        out_specs=[pl.BlockSpec((tb, MLA_W), lambda qi, ki: (qi, 0)), qspec],
        out_shape=[sd((s, MLA_W), BF16), sd((N_HEADS, s, LANES), BF16)],
        scratch_shapes=[pltpu.VMEM((N_HEADS, tb, LANES), F32), pltpu.VMEM((N_HEADS, tb, LANES), F32)],
        compiler_params=_cp("parallel", "arbitrary"),
    )(q, k, v)


def _out_proj_call(x, yret, ymla, wout, ts):
    s = x.shape[0]

    def body(x_ref, yr_ref, ym_ref, w_ref, x1_ref, r_ref):
        x1 = x_ref[...] + _dot(yr_ref[...], w_ref[0:RET_W, :]) + _dot(ym_ref[...], w_ref[RET_W:, :])
        x1_ref[...] = x1
        r_ref[...] = _rstd(x1)

    sd = jax.ShapeDtypeStruct
    return pl.pallas_call(
        body, name="out_proj", grid=(s // ts,),
        in_specs=[_row(ts, D_MODEL), _row(ts, RET_W), _row(ts, MLA_W), _full((D_MODEL, D_MODEL))],
        out_specs=[_row(ts, D_MODEL), _row(ts, 1)],
        out_shape=[sd((s, D_MODEL), F32), sd((s, 1), F32)],
        compiler_params=_cp("parallel"),
    )(x, yret, ymla, wout)


W_UP_SHARD = F2 // 4


def _ffn_fwd_call(x1, r2, fnw, wup4, cw, cb, wdown, ts):
    s = x1.shape[0]
    wsh = W_UP_SHARD

    def body(x_ref, r_ref, fnw_ref, wup_ref, cw_ref, cb_ref, wd_ref, u_ref, x2_ref, carry_ref):
        _zero_first(pl.program_id(0) == 0, carry_ref)
        xv = x_ref[...]
        h = (xv * r_ref[...] * fnw_ref[...]).astype(BF16)
        conv = []
        for j in range(4):
            cols = slice(j * wsh, (j + 1) * wsh)
            ub = _dot(h, wup_ref[j]).astype(BF16)
            u_ref[:, cols] = ub
            u = ub.astype(F32)
            u1, u2 = _shifted(u, carry_ref[:, cols])
            w = cw_ref[:, cols]
            conv.append(cb_ref[:, cols] + w[0:1, :] * u2 + w[1:2, :] * u1 + w[2:3, :] * u)
            carry_ref[:, cols] = u[ts - 8:, :]
        acc = xv
        for j in range(2):
            a = (_silu(conv[j]) * conv[j + 2]).astype(BF16)
            acc = acc + _dot(a, wd_ref[j * wsh:(j + 1) * wsh, :])
        x2_ref[...] = acc

    sd = jax.ShapeDtypeStruct
    return pl.pallas_call(
        body, name="ffn_fwd", grid=(s // ts,),
        in_specs=[_row(ts, D_MODEL), _row(ts, 1), _full((1, D_MODEL)), _full((4, D_MODEL, wsh)),
                  _full((3, F2)), _full((1, F2)), _full((D_FF, D_MODEL))],
        out_specs=[_row(ts, F2), _row(ts, D_MODEL)],
        out_shape=[sd((s, F2), BF16), sd((s, D_MODEL), F32)],
        scratch_shapes=[pltpu.VMEM((8, F2), F32)],
        compiler_params=_cp("arbitrary"),
    )(x1, r2, fnw, wup4, cw, cb, wdown)


def _shifted(u, hal):
    row = lax.broadcasted_iota(jnp.int32, u.shape, 0)
    u1 = jnp.where(row == 0, hal[7:8, :], pltpu.roll(u, 1, 0))
    u2 = jnp.where(row == 0, hal[6:7, :], jnp.where(row == 1, hal[7:8, :], pltpu.roll(u, 2, 0)))
    return u1, u2


def _conv_tile(u_ref, hal_ref, w_ref, b_ref, first):
    u = u_ref[...].astype(F32)
    hal = jnp.where(first, 0.0, hal_ref[...].astype(F32))
    u1, u2 = _shifted(u, hal)
    w = w_ref[...]
    return b_ref[...] + w[0:1, :] * u2 + w[1:2, :] * u1 + w[2:3, :] * u, u1, u2, u


def _gate_specs(ts, tf, rows_inner=False):
    nf = D_FF // tf
    hb = ts // 8

    def spec(shape, fn):
        return pl.BlockSpec(shape, (lambda j, i: fn(i, j)) if rows_inner else fn)

    return [
        spec((ts, tf), lambda i, j: (i, j)),
        spec((8, tf), lambda i, j: (jnp.maximum(i * hb - 1, 0), j)),
        spec((ts, tf), lambda i, j: (i, j + nf)),
        spec((8, tf), lambda i, j: (jnp.maximum(i * hb - 1, 0), j + nf)),
        spec((3, tf), lambda i, j: (0, j)),
        spec((3, tf), lambda i, j: (0, j + nf)),
        spec((1, tf), lambda i, j: (0, j)),
        spec((1, tf), lambda i, j: (0, j + nf)),
    ]


def _prep_weights(w):
    win = w["w_in"]
    pad = lambda n: jnp.zeros((D_MODEL, n), win.dtype)
    win_ext = jnp.concatenate([win[:, :IN_W - ROPE], pad(KPE_LO), win[:, IN_W - ROPE:], pad(LANES - KPE_LO - ROPE)], -1)
    wuq = w["w_uq"].reshape(Q_RANK, N_HEADS, HEAD + ROPE)
    wq = jnp.concatenate([wuq, jnp.zeros((Q_RANK, N_HEADS, LANES - HEAD - ROPE), wuq.dtype)], -1).transpose(1, 0, 2)
    wukv = w["w_ukv"].reshape(KV_RANK, N_HEADS, 2 * HEAD)
    zk = jnp.zeros((KV_RANK, N_HEADS, HEAD), wukv.dtype)
    wk = jnp.concatenate([wukv[:, :, :HEAD], zk], -1).transpose(1, 0, 2)
    wv = jnp.concatenate([wukv[:, :, HEAD:], zk], -1).transpose(1, 0, 2)
    c = lambda a: a.astype(BF16)
    wup = w["w_up"]
    if wup.ndim == 2:
        wup = wup.reshape(D_MODEL, 4, W_UP_SHARD).transpose(1, 0, 2)
    return dict(win=c(win_ext), wq=c(wq), wk=c(wk), wv=c(wv), wout=c(w["w_out"]), wup=c(wup), wdown=c(w["w_down"]))


def _tiles(s):
    return dict(ts=min(s, 512), tr=min(s, 1024), tb=min(s, 512), tg=min(s, 512), tf=D_FF // 2, t2=min(s, 256))


def _forward(x, positions, w, small):
    s = x.shape[0]
    t = _tiles(s)
    pw = _prep_weights(w)
    cos_r, sin_r, cos_m, sin_m = _rope_tables(positions)
    rc = _ret_consts()
    q, k, v, g, cq, ckv, kpe, r1 = _f1_call(x, small["attn_norm_w"], pw["win"], cos_r, sin_r, cos_m, sin_m, t["ts"])
    o_ret, y_ret = _ret_fwd_call(q, k, v, g, small["ret_gn_w"], rc, t["tr"])
    mq, mk, mv = _mla_pre_call(cq, ckv, kpe, small["mla_q_norm_w"], small["mla_kv_norm_w"],
                               pw["wq"], pw["wk"], pw["wv"], cos_m, sin_m, t["ts"])
    y_mla, mqb = _flash_fwd_call(mq, mk, mv, t["tb"])
    x1, r2 = _out_proj_call(x, y_ret, y_mla, pw["wout"], t["ts"])
    u, x2 = _ffn_fwd_call(x1, r2, small["ffn_norm_w"], pw["wup"], w["conv_w"], small["conv_b"], pw["wdown"], t["t2"])
    return dict(pw=pw, tabs=(cos_r, sin_r, cos_m, sin_m), rc=rc, q=q, k=k, v=v, g=g, cq=cq, ckv=ckv, kpe=kpe, r1=r1,
                o_ret=o_ret, y_ret=y_ret, mqb=mqb, mk=mk, mv=mv, y_mla=y_mla, x1=x1, r2=r2, u=u, x2=x2)


def _norm_bwd(dh, xh, r, nw):
    dxn = dh * nw
    return r * (dxn - xh * jnp.mean(dxn * xh, axis=-1, keepdims=True))


def _zero_first(first, *refs):
    @pl.when(first)
    def _():
        for ref in refs:
            ref[...] = jnp.zeros_like(ref)


def _colsum(v):
    return jnp.sum(v, axis=0, keepdims=True)


def _dsilu(g, sg):
    return sg * (1.0 + g * (1.0 - sg))


def _loss_call(x2, tgt, fw, ts):
    s = x2.shape[0]

    def body(x_ref, t_ref, fw_ref, dx_ref, loss_ref, gfw_ref):
        _zero_first(pl.program_id(0) == 0, loss_ref, gfw_ref)
        xv = x_ref[...]
        r = _rstd(xv)
        xh = xv * r
        fwv = fw_ref[...]
        e = xh * fwv - t_ref[...]
        loss_ref[...] += (0.5 / D_MODEL) * _colsum(jnp.sum(e * e, axis=1, keepdims=True))
        dy = e * (1.0 / D_MODEL)
        gfw_ref[...] += _colsum(dy * xh)
        dx_ref[...] = _norm_bwd(dy, xh, r, fwv)

    sd = jax.ShapeDtypeStruct
    return pl.pallas_call(
        body, name="loss_bwd", grid=(s // ts,),
        in_specs=[_row(ts, D_MODEL), _row(ts, D_MODEL), _full((1, D_MODEL))],
        out_specs=[_row(ts, D_MODEL), _full((1, 1)), _full((1, D_MODEL))],
        out_shape=[sd((s, D_MODEL), F32), sd((1, 1), F32), sd((1, D_MODEL), F32)],
        compiler_params=_cp("arbitrary"),
    )(x2, tgt, fw)


def _ffn_bwd1_call(dx2, u, cw, cb, wdown, ts, tf):
    s = dx2.shape[0]
    nf = D_FF // tf

    def body(dx_ref, wd_ref, ug_ref, hg_ref, uv_ref, hv_ref, wg_ref, wv_ref, bg_ref, bv_ref,
             dgate_ref, dval_ref, dwd_ref, dcwg_ref, dcwv_ref, dcbg_ref, dcbv_ref):
        first = pl.program_id(1) == 0
        _zero_first(first, dwd_ref, dcwg_ref, dcwv_ref, dcbg_ref, dcbv_ref)
        gate, g1, g2, g0 = _conv_tile(ug_ref, hg_ref, wg_ref, bg_ref, first)
        val, v1, v2, v0 = _conv_tile(uv_ref, hv_ref, wv_ref, bv_ref, first)
        dxb = dx_ref[...].astype(BF16)
        da = _dot_nt(dxb, wd_ref[...])
        sg = 1.0 / (1.0 + jnp.exp(-gate))
        sl = gate * sg
        dgate = da * val * _dsilu(gate, sg)
        dval = da * sl
        dgate_ref[...] = dgate.astype(BF16)
        dval_ref[...] = dval.astype(BF16)
        dwd_ref[...] += _dot_tn((sl * val).astype(BF16), dxb)
        for ref, d, taps in ((dcwg_ref, dgate, (g2, g1, g0)), (dcwv_ref, dval, (v2, v1, v0))):
            for t in range(3):
                ref[t:t + 1, :] += _colsum(d * taps[t])
        dcbg_ref[...] += _colsum(dgate)
        dcbv_ref[...] += _colsum(dval)

    sd = jax.ShapeDtypeStruct
    colacc = lambda r: pl.BlockSpec((r, tf), lambda j, i: (0, j))
    return pl.pallas_call(
        body, name="ffn_bwd_gate", grid=(nf, s // ts),
        in_specs=[pl.BlockSpec((ts, D_MODEL), lambda j, i: (i, 0)), pl.BlockSpec((tf, D_MODEL), lambda j, i: (j, 0))]
        + _gate_specs(ts, tf, rows_inner=True),
        out_specs=[pl.BlockSpec((ts, tf), lambda j, i: (i, j)), pl.BlockSpec((ts, tf), lambda j, i: (i, j)),
                   pl.BlockSpec((tf, D_MODEL), lambda j, i: (j, 0)), colacc(3), colacc(3), colacc(1), colacc(1)],
        out_shape=[sd((s, D_FF), BF16), sd((s, D_FF), BF16), sd((D_FF, D_MODEL), F32),
                   sd((3, D_FF), F32), sd((3, D_FF), F32), sd((1, D_FF), F32), sd((1, D_FF), F32)],
        compiler_params=_cp("parallel", "arbitrary"),
    )(dx2, wdown, u, u, u, u, cw, cw, cb, cb)


def _shifted_up(d, hal):
    n = d.shape[0]
    row = lax.broadcasted_iota(jnp.int32, d.shape, 0)
    d1 = jnp.where(row == n - 1, hal[0:1, :], pltpu.roll(d, n - 1, 0))
    d2 = jnp.where(row == n - 2, hal[0:1, :], jnp.where(row == n - 1, hal[1:2, :], pltpu.roll(d, n - 2, 0)))
    return d1, d2


def _ffn_bwd2_call(dgate, dval, cw, wup, x1, r2, fnw, dx2, ts):
    s = dx2.shape[0]
    nt = s // ts
    hb = ts // 8
    nxt = pl.BlockSpec((8, D_FF), lambda i: (jnp.minimum((i + 1) * hb, s // 8 - 1), 0))

    def body(dg_ref, hg_ref, dv_ref, hv_ref, cw_ref, wup_ref, x_ref, r_ref, fnw_ref, dx2_ref, du_ref, dx1_ref, dfnw_ref):
        i = pl.program_id(0)
        _zero_first(i == 0, dfnw_ref)
        dh = jnp.zeros((ts, D_MODEL), F32)
        for part, (d_ref, h_ref) in enumerate(((dg_ref, hg_ref), (dv_ref, hv_ref))):
            off = part * D_FF
            d = d_ref[...].astype(F32)
            hal = jnp.where(i == nt - 1, 0.0, h_ref[...].astype(F32))
            d1, d2 = _shifted_up(d, hal)
            w = cw_ref[:, off:off + D_FF]
            du = (w[2:3, :] * d + w[1:2, :] * d1 + w[0:1, :] * d2).astype(BF16)
            du_ref[:, off:off + D_FF] = du
            for j in range(2):
                dh = dh + _dot_nt(du[:, j * W_UP_SHARD:(j + 1) * W_UP_SHARD], wup_ref[2 * part + j])
        r = r_ref[...]
        xh = x_ref[...] * r
        dfnw_ref[...] += _colsum(dh * xh)
        dx1_ref[...] = dx2_ref[...] + _norm_bwd(dh, xh, r, fnw_ref[...])

    sd = jax.ShapeDtypeStruct
    return pl.pallas_call(
        body, name="ffn_bwd_up", grid=(nt,),
        in_specs=[_row(ts, D_FF), nxt, _row(ts, D_FF), nxt, _full((3, F2)), _full((4, D_MODEL, W_UP_SHARD)),
                  _row(ts, D_MODEL), _row(ts, 1), _full((1, D_MODEL)), _row(ts, D_MODEL)],
        out_specs=[_row(ts, F2), _row(ts, D_MODEL), _full((1, D_MODEL))],
        out_shape=[sd((s, F2), BF16), sd((s, D_MODEL), F32), sd((1, D_MODEL), F32)],
        compiler_params=_cp("arbitrary"),
    )(dgate, dgate, dval, dval, cw, wup, x1, r2, fnw, dx2)


def _dw_norm_call(x, r, nw, b, ts, tn, name):
    s, n = b.shape
    k = x.shape[1]

    def body(x_ref, r_ref, nw_ref, b_ref, dw_ref):
        _zero_first(pl.program_id(1) == 0, dw_ref)
        h = (x_ref[...] * r_ref[...] * nw_ref[...]).astype(BF16)
        dw_ref[...] += _dot_tn(h, b_ref[...])

    return pl.pallas_call(
        body, name=name, grid=(n // tn, s // ts),
        in_specs=[pl.BlockSpec((ts, k), lambda j, i: (i, 0)), pl.BlockSpec((ts, 1), lambda j, i: (i, 0)),
                  pl.BlockSpec((1, k), lambda j, i: (0, 0)), pl.BlockSpec((ts, tn), lambda j, i: (i, j))],
        out_specs=pl.BlockSpec((None, k, tn), lambda j, i: (j, 0, 0)),
        out_shape=jax.ShapeDtypeStruct((n // tn, k, tn), F32),
        compiler_params=_cp("parallel", "arbitrary"),
    )(x, r, nw, b)


def _out_bwd_call(dx1, yret, ymla, wout, ts):
    s = dx1.shape[0]

    def body(dx_ref, yr_ref, ym_ref, w_ref, dyr_ref, do_ref, dwo_ref):
        _zero_first(pl.program_id(0) == 0, dwo_ref)
        dxb = dx_ref[...].astype(BF16)
        dmix = _dot_nt(dxb, w_ref[...])
        dyr_ref[...] = dmix[:, :RET_W]
        ym = ym_ref[...]
        lane = lax.broadcasted_iota(jnp.int32, (ts, LANES), 1)
        for p in range(N_HEADS // 2):
            dom = dmix[:, RET_W + p * LANES:RET_W + (p + 1) * LANES]
            prod = dom * ym[:, p * LANES:(p + 1) * LANES].astype(F32)
            for hh in range(2):
                mine = (lane >= HEAD) if hh else (lane < HEAD)
                hi, lo = _hi_lo(jnp.sum(jnp.where(mine, prod, 0.0), axis=1, keepdims=True))
                base = jnp.where(lane < HEAD, pltpu.roll(dom, HEAD, 1) if hh else dom, 0.0)
                do_ref[2 * p + hh] = _lane_pair((ts, LANES), V_AUX, -hi, -lo, base).astype(BF16)
        dwo_ref[0:RET_W, :] += _dot_tn(yr_ref[...], dxb)
        dwo_ref[RET_W:, :] += _dot_tn(ym, dxb)

    sd = jax.ShapeDtypeStruct
    return pl.pallas_call(
        body, name="out_proj_bwd", grid=(s // ts,),
        in_specs=[_row(ts, D_MODEL), _row(ts, RET_W), _row(ts, MLA_W), _full((D_MODEL, D_MODEL))],
        out_specs=[_row(ts, RET_W), _hrow(N_HEADS, ts, LANES), _full((D_MODEL, D_MODEL))],
        out_shape=[sd((s, RET_W), F32), sd((N_HEADS, s, LANES), BF16), sd((D_MODEL, D_MODEL), F32)],
        compiler_params=_cp("arbitrary"),
    )(dx1, yret, ymla, wout)


def _ret_bwd_q_call(q, k, v, o, g, dy, gnw, rc, cos_r, sin_r, tr):
    s = q.shape[0]
    c = RET_CHUNK
    nc = tr // c

    def body(q_ref, k_ref, v_ref, o_ref, g_ref, dy_ref, gnw_ref, dm_ref, zeta_ref, xi_ref, cd_ref, bd_ref, cr_ref, sr_ref,
             dq_ref, dg_ref, do_ref, dgnw_ref, st_ref):
        _zero_first(pl.program_id(1) == 0, st_ref, dgnw_ref)
        bd = bd_ref[...]
        avg = bd * (1.0 / HEAD)
        ov = o_ref[...]
        ctr = ov - _dot_hi(ov, avg)
        rs = lax.rsqrt(_dot_hi(ctr * ctr, avg) + EPS)
        oh = ctr * rs
        gg, dyv, gnw_v = g_ref[...], dy_ref[...], gnw_ref[...]
        sg = 1.0 / (1.0 + jnp.exp(-gg))
        sl = gg * sg
        dg_ref[...] = (dyv * oh * gnw_v * _dsilu(gg, sg)).astype(BF16)
        dgnw_ref[...] += _colsum(dyv * sl * oh)
        doh = dyv * sl * gnw_v
        dov = (rs * (doh - _dot_hi(doh, avg) - oh * _dot_hi(doh * oh, avg))).astype(BF16)
        do_ref[...] = dov
        chunks = [slice(ci * c, (ci + 1) * c) for ci in range(nc)]
        contrib = [_dot_tn((k_ref[rows, :].astype(F32) * zeta_ref[0]).astype(BF16), v_ref[rows, :]) * bd for rows in chunks]
        st, states = st_ref[...], []
        for ci in range(nc):
            states.append(st.astype(BF16))
            st = st * cd_ref[0] + contrib[ci]
        st_ref[...] = st
        for ci, rows in enumerate(chunks):
            doc = dov[rows, :]
            dq = (_dot_nt(doc, states[ci]) * xi_ref[0]
                  + _pair_product(doc, _stack_heads(v_ref[rows, :]), dm_ref[0], _stack_heads(k_ref[rows, :])))
            dq_ref[rows, :] = _unrope(dq, cr_ref[rows, :], sr_ref[rows, :], HEAD // 2).astype(BF16)

    slab = pl.BlockSpec((tr, LANES), lambda j, i: (i, j))
    tab = pl.BlockSpec((tr, LANES), lambda j, i: (i, 0))
    vec = pl.BlockSpec((1, LANES), lambda j, i: (0, j))
    sd = jax.ShapeDtypeStruct
    return pl.pallas_call(
        body, name="ret_bwd_q", grid=(4, s // tr),
        in_specs=[slab, slab, slab, slab, slab, slab, vec,
                  pl.BlockSpec((1, c, 2 * c), lambda j, i: (j, 0, 0)),
                  pl.BlockSpec((1, c, LANES), lambda j, i: (j, 0, 0)),
                  pl.BlockSpec((1, c, LANES), lambda j, i: (j, 0, 0)),
                  pl.BlockSpec((1, LANES, LANES), lambda j, i: (j, 0, 0)),
                  pl.BlockSpec((LANES, LANES), lambda j, i: (0, 0)), tab, tab],
        out_specs=[slab, slab, slab, vec],
        out_shape=[sd((s, RET_W), BF16), sd((s, RET_W), BF16), sd((s, RET_W), BF16), sd((1, RET_W), F32)],
        scratch_shapes=[pltpu.VMEM((LANES, LANES), F32)],
        compiler_params=_cp("parallel", "arbitrary"),
    )(q, k, v, o, g, dy, gnw, rc["dmask"], rc["zeta"], rc["xi"], rc["cd"], rc["bd"], cos_r, sin_r)


def _ret_bwd_kv_call(q, k, v, do, rc, cos_r, sin_r, tr):
    s = q.shape[0]
    c = RET_CHUNK
    nc = tr // c
    nt = s // tr

    def body(q_ref, k_ref, v_ref, do_ref, dm_ref, zeta_ref, xi_ref, cd_ref, bd_ref, cr_ref, sr_ref, dk_ref, dv_ref, gs_ref):
        _zero_first(pl.program_id(1) == 0, gs_ref)
        bd = bd_ref[...]
        chunks = [slice(ci * c, (ci + 1) * c) for ci in range(nc)]
        contrib = [_dot_tn((q_ref[rows, :].astype(F32) * xi_ref[0]).astype(BF16), do_ref[rows, :]) * bd for rows in chunks]
        gs, states = gs_ref[...], [None] * nc
        for ci in reversed(range(nc)):
            states[ci] = gs.astype(BF16)
            gs = gs * cd_ref[0] + contrib[ci]
        gs_ref[...] = gs
        for ci, rows in enumerate(chunks):
            kc, vc = k_ref[rows, :], v_ref[rows, :]
            q2, do2 = _stack_heads(q_ref[rows, :]), _stack_heads(do_ref[rows, :])
            gb = states[ci]
            dk = _dot_nt(vc, gb) * zeta_ref[0] + _pair_product(vc, do2, dm_ref[0], q2)
            dv = _dot(kc, gb) * zeta_ref[0] + _pair_product(kc, q2, dm_ref[0], do2)
            dk_ref[rows, :] = (_unrope(dk, cr_ref[rows, :], sr_ref[rows, :], HEAD // 2) * (HEAD ** -0.5)).astype(BF16)
            dv_ref[rows, :] = dv.astype(BF16)

    slab = pl.BlockSpec((tr, LANES), lambda j, i: (nt - 1 - i, j))
    tab = pl.BlockSpec((tr, LANES), lambda j, i: (nt - 1 - i, 0))
    sd = jax.ShapeDtypeStruct
    return pl.pallas_call(
        body, name="ret_bwd_kv", grid=(4, nt),
        in_specs=[slab, slab, slab, slab,
                  pl.BlockSpec((1, c, 2 * c), lambda j, i: (j, 0, 0)),
                  pl.BlockSpec((1, c, LANES), lambda j, i: (j, 0, 0)),
                  pl.BlockSpec((1, c, LANES), lambda j, i: (j, 0, 0)),
                  pl.BlockSpec((1, LANES, LANES), lambda j, i: (j, 0, 0)),
                  pl.BlockSpec((LANES, LANES), lambda j, i: (0, 0)), tab, tab],
        out_specs=[slab, slab],
        out_shape=[sd((s, RET_W), BF16), sd((s, RET_W), BF16)],
        scratch_shapes=[pltpu.VMEM((LANES, LANES), F32)],
        compiler_params=_cp("parallel", "arbitrary"),
    )(q, k, v, do, rc["dmask_t"], rc["zeta"], rc["xi"], rc["cd"], rc["bd"], cos_r, sin_r)


def _flash_bwd_dq_call(qb, k, v, do, tb):
    s = qb.shape[1]
    nb = s // tb

    def body(q_ref, k_ref, v_ref, do_ref, dq_ref, acc_ref):
        qi, ki = pl.program_id(0), pl.program_id(1)
        _zero_first(ki == 0, acc_ref)

        def step(masked):
            if masked:
                keep = lax.broadcasted_iota(jnp.int32, (tb, tb), 1) <= lax.broadcasted_iota(jnp.int32, (tb, tb), 0)
            for h in range(N_HEADS):
                sc = _dot_nt(q_ref[h], k_ref[h])
                if masked:
                    sc = jnp.where(keep, sc, NEG)
                ds = (jnp.exp2(sc) * _dot_nt(do_ref[h], v_ref[h])).astype(BF16)
                acc_ref[h] += _dot(ds, k_ref[h])

        @pl.when(ki < qi)
        def _():
            step(False)

        @pl.when(ki == qi)
        def _():
            step(True)
            dq_ref[...] = (acc_ref[...] * SM_SCALE).astype(BF16)

    qspec = pl.BlockSpec((N_HEADS, tb, LANES), lambda qi, ki: (0, qi, 0))
    kspec = pl.BlockSpec((N_HEADS, tb, LANES), lambda qi, ki: (0, jnp.minimum(ki, qi), 0))
    return pl.pallas_call(
        body, name="mla_flash_bwd_dq", grid=(nb, nb),
        in_specs=[qspec, kspec, kspec, qspec],
        out_specs=qspec,
        out_shape=jax.ShapeDtypeStruct((N_HEADS, s, LANES), BF16),
        scratch_shapes=[pltpu.VMEM((N_HEADS, tb, LANES), F32)],
        compiler_params=_cp("parallel", "arbitrary"),
    )(qb, k, v, do)


def _flash_bwd_dkv_call(qb, k, v, do, tb):
    s = qb.shape[1]
    nb = s // tb

    def body(q_ref, k_ref, v_ref, do_ref, dk_ref, dv_ref, dka_ref, dva_ref):
        ki, qi = pl.program_id(0), pl.program_id(1)
        _zero_first(qi == 0, dka_ref, dva_ref)

        def step(masked):
            if masked:
                keep = lax.broadcasted_iota(jnp.int32, (tb, tb), 0) <= lax.broadcasted_iota(jnp.int32, (tb, tb), 1)
            for h in range(N_HEADS):
                st = _dot_nt(k_ref[h], q_ref[h])
                if masked:
                    st = jnp.where(keep, st, NEG)
                pt = jnp.exp2(st)
                dob = do_ref[h]
                dva_ref[h] += _dot(pt.astype(BF16), dob)
                dst = (pt * _dot_nt(v_ref[h], dob)).astype(BF16)
                dka_ref[h] += _dot(dst, q_ref[h])

        @pl.when(qi > ki)
        def _():
            step(False)

        @pl.when(qi == ki)
        def _():
            step(True)

        @pl.when(qi == nb - 1)
        def _():
            dk_ref[...] = (dka_ref[...] * LN2).astype(BF16)
            dv_ref[...] = dva_ref[...].astype(BF16)

    kspec = pl.BlockSpec((N_HEADS, tb, LANES), lambda ki, qi: (0, ki, 0))
    qspec = pl.BlockSpec((N_HEADS, tb, LANES), lambda ki, qi: (0, jnp.maximum(qi, ki), 0))
    sd = jax.ShapeDtypeStruct((N_HEADS, s, LANES), BF16)
    return pl.pallas_call(
        body, name="mla_flash_bwd_dkv", grid=(nb, nb),
        in_specs=[qspec, kspec, kspec, qspec],
        out_specs=[kspec, kspec],
        out_shape=[sd, sd],
        scratch_shapes=[pltpu.VMEM((N_HEADS, tb, LANES), F32), pltpu.VMEM((N_HEADS, tb, LANES), F32)],
        compiler_params=_cp("parallel", "arbitrary"),
    )(qb, k, v, do)


def _mla_post_call(dq, dk, dv, cq, ckv, qnw, kvnw, wq, wk, wv, cos_m, sin_m, ts):
    s = cq.shape[0]

    def body(dq_ref, dk_ref, dv_ref, cq_ref, ckv_ref, qnw_ref, kvnw_ref, wq_ref, wk_ref, wv_ref, cm_ref, sm_ref,
             dcq_ref, dckv_ref, dkpe_ref, dwq_ref, dwk_ref, dwv_ref, dqnw_ref, dkvnw_ref):
        _zero_first(pl.program_id(0) == 0, dwq_ref, dwk_ref, dwv_ref, dqnw_ref, dkvnw_ref)
        cqv, ckvv = cq_ref[...], ckv_ref[...]
        rq, rkv = _rstd(cqv), _rstd(ckvv)
        qh_, kvh_ = cqv * rq, ckvv * rkv
        qnw_v, kvnw_v = qnw_ref[...], kvnw_ref[...]
        cqn = (qh_ * qnw_v).astype(BF16)
        ckvn = (kvh_ * kvnw_v).astype(BF16)
        cm, sm = cm_ref[...], sm_ref[...]
        dcqn = jnp.zeros((ts, Q_RANK), F32)
        dckvn = jnp.zeros((ts, KV_RANK), F32)
        dkpe = jnp.zeros((ts, LANES), F32)
        for h in range(N_HEADS):
            dqu = _unrope(dq_ref[h].astype(F32), cm, sm, ROPE // 2).astype(BF16)
            dwq_ref[h] += _dot_tn(cqn, dqu)
            dcqn = dcqn + _dot_nt(dqu, wq_ref[h])
            dkb, dvb = dk_ref[h], dv_ref[h]
            dkpe = dkpe + dkb.astype(F32)
            dwk_ref[h] += _dot_tn(ckvn, dkb)
            dwv_ref[h] += _dot_tn(ckvn, dvb)
            dckvn = dckvn + _dot_nt(dkb, wk_ref[h]) + _dot_nt(dvb, wv_ref[h])
        lane = lax.broadcasted_iota(jnp.int32, (ts, LANES), 1)
        dkpe = jnp.where((lane >= KPE_LO) & (lane < KPE_LO + ROPE), dkpe, 0.0)
        dkpe_ref[...] = _unrope(dkpe, cm, sm, ROPE // 2).astype(BF16)
        dqnw_ref[...] += _colsum(dcqn * qh_)
        dkvnw_ref[...] += _colsum(dckvn * kvh_)
        dcq_ref[...] = _norm_bwd(dcqn, qh_, rq, qnw_v).astype(BF16)
        dckv_ref[...] = _norm_bwd(dckvn, kvh_, rkv, kvnw_v).astype(BF16)

    sd = jax.ShapeDtypeStruct
    hm = _hrow(N_HEADS, ts, LANES)
    return pl.pallas_call(
        body, name="mla_post", grid=(s // ts,),
        in_specs=[hm, hm, hm, _row(ts, Q_RANK), _row(ts, KV_RANK), _full((1, Q_RANK)), _full((1, KV_RANK)),
                  _full((N_HEADS, Q_RANK, LANES)), _full((N_HEADS, KV_RANK, LANES)), _full((N_HEADS, KV_RANK, LANES)),
                  _row(ts, LANES), _row(ts, LANES)],
        out_specs=[_row(ts, Q_RANK), _row(ts, KV_RANK), _row(ts, LANES),
                   _full((N_HEADS, Q_RANK, LANES)), _full((N_HEADS, KV_RANK, LANES)), _full((N_HEADS, KV_RANK, LANES)),
                   _full((1, Q_RANK)), _full((1, KV_RANK))],
        out_shape=[sd((s, Q_RANK), BF16), sd((s, KV_RANK), BF16), sd((s, LANES), BF16),
                   sd((N_HEADS, Q_RANK, LANES), F32), sd((N_HEADS, KV_RANK, LANES), F32), sd((N_HEADS, KV_RANK, LANES), F32),
                   sd((1, Q_RANK), F32), sd((1, KV_RANK), F32)],
        compiler_params=_cp("arbitrary"),
    )(dq, dk, dv, cq, ckv, qnw, kvnw, wq, wk, wv, cos_m, sin_m)


def _in_bwd_call(parts, x, r1, anw, dx1, win, ts):
    s = x.shape[0]
    widths = [p.shape[1] for p in parts]
    np_ = len(parts)

    def body(*refs):
        p_refs = refs[:np_]
        x_ref, r_ref, anw_ref, dx1_ref, w_ref, dx_ref, dw_ref, danw_ref = refs[np_:]
        _zero_first(pl.program_id(0) == 0, dw_ref, danw_ref)
        dproj = jnp.concatenate([p[...] for p in p_refs], axis=-1)
        r, anw_v = r_ref[...], anw_ref[...]
        xh = x_ref[...] * r
        dw_ref[...] += _dot_tn((xh * anw_v).astype(BF16), dproj)
        dh = _dot_nt(dproj, w_ref[...])
        danw_ref[...] += _colsum(dh * xh)
        dx_ref[...] = dx1_ref[...] + _norm_bwd(dh, xh, r, anw_v)

    sd = jax.ShapeDtypeStruct
    return pl.pallas_call(
        body, name="in_proj_bwd", grid=(s // ts,),
        in_specs=[_row(ts, w) for w in widths]
        + [_row(ts, D_MODEL), _row(ts, 1), _full((1, D_MODEL)), _row(ts, D_MODEL), _full((D_MODEL, IN_EXT))],
        out_specs=[_row(ts, D_MODEL), _full((D_MODEL, IN_EXT)), _full((1, D_MODEL))],
        out_shape=[sd((s, D_MODEL), F32), sd((D_MODEL, IN_EXT), F32), sd((1, D_MODEL), F32)],
        compiler_params=_cp("arbitrary"),
    )(*parts, x, r1, anw, dx1, win)


def _local_step(x, positions, tgt, w, small):
    s = x.shape[0]
    t = _tiles(s)
    f = _forward(x, positions, w, small)
    pw, rc = f["pw"], f["rc"]
    cos_r, sin_r, cos_m, sin_m = f["tabs"]
    dx2, loss, g_fw = _loss_call(f["x2"], tgt, small["final_norm_w"], t["ts"])
    dgate, dval, g_wd, dcw_g, dcw_v, dcb_g, dcb_v = _ffn_bwd1_call(dx2, f["u"], w["conv_w"], small["conv_b"], pw["wdown"], t["tg"], t["tf"])
    du, dx1, g_fnw = _ffn_bwd2_call(dgate, dval, w["conv_w"], pw["wup"], f["x1"], f["r2"], small["ffn_norm_w"], dx2, t["t2"])
    g_wup = _dw_norm_call(f["x1"], f["r2"], small["ffn_norm_w"], du, t["ts"], F2 // 4, "dw_up")
    dy_ret, do, g_wout = _out_bwd_call(dx1, f["y_ret"], f["y_mla"], pw["wout"], t["ts"])
    drq, dg, do_ret, g_gnw = _ret_bwd_q_call(f["q"], f["k"], f["v"], f["o_ret"], f["g"], dy_ret, small["ret_gn_w"], rc, cos_r, sin_r, t["tr"])
    drk, drv = _ret_bwd_kv_call(f["q"], f["k"], f["v"], do_ret, rc, cos_r, sin_r, t["tr"])
    dmq = _flash_bwd_dq_call(f["mqb"], f["mk"], f["mv"], do, t["tb"])
    dmk, dmv = _flash_bwd_dkv_call(f["mqb"], f["mk"], f["mv"], do, t["tb"])
    dcq, dckv, dkpe, g_wq, g_wk, g_wv, g_qnw, g_kvnw = _mla_post_call(
        dmq, dmk, dmv, f["cq"], f["ckv"], small["mla_q_norm_w"], small["mla_kv_norm_w"], pw["wq"], pw["wk"], pw["wv"], cos_m, sin_m, t["ts"])
    gx, g_win_ext, g_anw = _in_bwd_call([drq, drk, drv, dg, dcq, dckv, dkpe], x, f["r1"], small["attn_norm_w"], dx1, pw["win"], t["ts"])
    lo = IN_W - ROPE
    g_win = jnp.concatenate([g_win_ext[:, :lo], g_win_ext[:, lo + KPE_LO:lo + KPE_LO + ROPE]], -1)
    g_wuq = g_wq.transpose(1, 0, 2)[:, :, :HEAD + ROPE].reshape(Q_RANK, N_HEADS * (HEAD + ROPE))
    g_wukv = jnp.concatenate([g_wk[:, :, :HEAD], g_wv[:, :, :HEAD]], -1).transpose(1, 0, 2).reshape(KV_RANK, 2 * MLA_W)
    gw = dict(w_in=g_win, w_uq=g_wuq, w_ukv=g_wukv, w_out=g_wout, w_up=g_wup,
              conv_w=jnp.concatenate([dcw_g, dcw_v], -1), w_down=g_wd)
    gs = dict(attn_norm_w=g_anw, ret_gn_w=g_gnw, mla_q_norm_w=g_qnw, mla_kv_norm_w=g_kvnw, ffn_norm_w=g_fnw,
              conv_b=jnp.concatenate([dcb_g, dcb_v], -1), final_norm_w=g_fw)
    return loss, gx, gw, gs


MESH_ID = pl.DeviceIdType.MESH
ANY = pl.BlockSpec(memory_space=pl.ANY)
VMEM_SPEC = pl.BlockSpec(memory_space=pltpu.VMEM)
N_DEV = 8
PACKED = (("w_in", (D_MODEL, IN_W // 4), 1), ("w_uq", (Q_RANK, 192), 1), ("w_ukv", (KV_RANK, 256), 1),
          ("w_out", (D_MODEL // 4, D_MODEL), 0), ("w_up", (D_MODEL, F2 // 4), 1), ("w_down", (D_FF // 4, D_MODEL), 0))
PACK_ROWS = 24576
HALF_ROWS = PACK_ROWS // 2
ADD_ROWS = 2048


def _mesh_pos():
    return lax.axis_index("x"), lax.axis_index("y"), lax.axis_index("c")


def _other_chips(x, y):
    return [(1 - x, y), (x, 1 - y), (1 - x, 1 - y)]


def _pack(parts, dtype):
    flat = jnp.concatenate([p.reshape(-1).astype(dtype) for p in parts])
    flat = jnp.concatenate([flat, jnp.zeros((PACK_ROWS * LANES - flat.shape[0],), dtype)])
    return flat.reshape(2, HALF_ROWS, LANES)


def _unpack(flat):
    out, off = [], 0
    for _, (r, c), _ in PACKED:
        out.append(flat[..., off:off + r * c].reshape(flat.shape[:-1] + (r, c)))
        off += r * c
    return out


def _all_gather_call(packed):
    _, h, _ = packed.shape

    def body(src_ref, out_ref, send_sems, recv_sems):
        x, y, c = _mesh_pos()
        sm = 2 * x + y
        chips = _other_chips(x, y)
        sib = (x, y, 1 - c)

        def rcopy(k, src, dst, dev):
            return pltpu.make_async_remote_copy(src_ref=src, dst_ref=dst, send_sem=send_sems.at[k], recv_sem=recv_sems.at[k],
                                                device_id=dev, device_id_type=MESH_ID)

        first = [rcopy(j, src_ref.at[c], out_ref.at[sm, c], (cx, cy, c)) for j, (cx, cy) in enumerate(chips)]
        own = rcopy(6, src_ref, out_ref.at[sm], sib)
        for cp in first + [own]:
            cp.start()
        passed = []
        for j, (cx, cy) in enumerate(chips):
            sj = 2 * cx + cy
            rcopy(j, src_ref.at[c], out_ref.at[sj, c], (cx, cy, c)).wait_recv()
            cp = rcopy(3 + j, out_ref.at[sj, c], out_ref.at[sj, c], sib)
            cp.start()
            passed.append(cp)
        for j, (cx, cy) in enumerate(chips):
            rcopy(3 + j, src_ref.at[c], out_ref.at[2 * cx + cy, 1 - c], sib).wait_recv()
        own.wait_recv()
        for cp in first + passed + [own]:
            cp.wait_send()

    return pl.pallas_call(
        body, name="weights_all_gather",
        in_specs=[ANY], out_specs=ANY,
        out_shape=jax.ShapeDtypeStruct((4, 2, h, LANES), packed.dtype),
        scratch_shapes=[pltpu.SemaphoreType.DMA((7,)), pltpu.SemaphoreType.DMA((7,))],
    )(packed)


def _rs_sibling_call(g):
    _, _, h, _ = g.shape

    def body(g_ref, buf_ref, send_sems, recv_sems):
        x, y, c = _mesh_pos()
        cps = [pltpu.make_async_remote_copy(src_ref=g_ref.at[s, 1 - c], dst_ref=buf_ref.at[s], send_sem=send_sems.at[s],
                                            recv_sem=recv_sems.at[s], device_id=(x, y, 1 - c), device_id_type=MESH_ID)
               for s in range(4)]
        for cp in cps:
            cp.start()
        for cp in cps:
            cp.wait()

    return pl.pallas_call(
        body, name="grads_rs_sibling",
        in_specs=[ANY], out_specs=ANY,
        out_shape=jax.ShapeDtypeStruct((4, h, LANES), g.dtype),
        scratch_shapes=[pltpu.SemaphoreType.DMA((4,)), pltpu.SemaphoreType.DMA((4,))],
    )(g)


def _rs_add1_call(g, buf, c):
    _, _, h, _ = g.shape

    def body(c_ref, g_ref, b_ref, p_ref, pb_ref):
        p = g_ref[...] + b_ref[...]
        p_ref[...] = p
        pb_ref[...] = p.astype(BF16)

    blk = pl.BlockSpec((None, ADD_ROWS, LANES), lambda s, i, c_ref: (s, i, 0))
    return pl.pallas_call(
        body, name="grads_rs_add_sibling",
        grid_spec=pltpu.PrefetchScalarGridSpec(
            num_scalar_prefetch=1, grid=(4, h // ADD_ROWS),
            in_specs=[pl.BlockSpec((None, None, ADD_ROWS, LANES), lambda s, i, c_ref: (s, c_ref[0], i, 0)), blk],
            out_specs=[blk, blk]),
        out_shape=[jax.ShapeDtypeStruct((4, h, LANES), F32), jax.ShapeDtypeStruct((4, h, LANES), BF16)],
        compiler_params=_cp("parallel", "parallel"),
    )(c, g, buf)


def _rs_chips_call(pb):
    _, h, _ = pb.shape

    def body(pb_ref, buf_ref, send_sems, recv_sems):
        x, y, c = _mesh_pos()
        cps = [pltpu.make_async_remote_copy(src_ref=pb_ref.at[2 * cx + cy], dst_ref=buf_ref.at[j], send_sem=send_sems.at[j],
                                            recv_sem=recv_sems.at[j], device_id=(cx, cy, c), device_id_type=MESH_ID)
               for j, (cx, cy) in enumerate(_other_chips(x, y))]
        for cp in cps:
            cp.start()
        for cp in cps:
            cp.wait()

    return pl.pallas_call(
        body, name="grads_rs_chips",
        in_specs=[ANY], out_specs=ANY,
        out_shape=jax.ShapeDtypeStruct((3, h, LANES), pb.dtype),
        scratch_shapes=[pltpu.SemaphoreType.DMA((3,)), pltpu.SemaphoreType.DMA((3,))],
    )(pb)


def _rs_add2_call(p, buf, sm):
    _, h, _ = p.shape

    def body(sm_ref, p_ref, b_ref, f_ref):
        f_ref[...] = ((p_ref[...] + b_ref[0].astype(F32)) + b_ref[1].astype(F32)) + b_ref[2].astype(F32)

    return pl.pallas_call(
        body, name="grads_rs_add_chips",
        grid_spec=pltpu.PrefetchScalarGridSpec(
            num_scalar_prefetch=1, grid=(h // ADD_ROWS,),
            in_specs=[pl.BlockSpec((None, ADD_ROWS, LANES), lambda i, sm_ref: (sm_ref[0], i, 0)),
                      pl.BlockSpec((3, ADD_ROWS, LANES), lambda i, sm_ref: (0, i, 0))],
            out_specs=pl.BlockSpec((ADD_ROWS, LANES), lambda i, sm_ref: (i, 0))),
        out_shape=jax.ShapeDtypeStruct((h, LANES), F32),
        compiler_params=_cp("parallel"),
    )(sm, p, buf)


def _rs_share_call(f):
    h, _ = f.shape

    def body(f_ref, out_ref, send_sem, recv_sem):
        x, y, c = _mesh_pos()
        cp = pltpu.make_async_remote_copy(src_ref=f_ref, dst_ref=out_ref, send_sem=send_sem, recv_sem=recv_sem,
                                          device_id=(x, y, 1 - c), device_id_type=MESH_ID)
        cp.start()
        cp.wait()

    return pl.pallas_call(
        body, name="grads_rs_share",
        in_specs=[ANY], out_specs=ANY,
        out_shape=jax.ShapeDtypeStruct((h, LANES), f.dtype),
        scratch_shapes=[pltpu.SemaphoreType.DMA, pltpu.SemaphoreType.DMA],
    )(f)


def _exchange8_call(vec, reduce, name):
    rows = vec.shape[0]

    def body(v_ref, out_ref, *rest):
        slots, send_sems, recv_sems = (rest if reduce else (out_ref,) + rest)
        x, y, c = _mesh_pos()
        me = 4 * x + 2 * y + c
        slots[me] = v_ref[...]

        def rcopy(k, to_me):
            bx, by, bc = (k >> 2) & 1, (k >> 1) & 1, k & 1
            px, py, pc = (1 - x if bx else x), (1 - y if by else y), (1 - c if bc else c)
            slot = 4 * px + 2 * py + pc if to_me else me
            return pltpu.make_async_remote_copy(src_ref=v_ref, dst_ref=slots.at[slot], send_sem=send_sems.at[k - 1],
                                                recv_sem=recv_sems.at[k - 1], device_id=(px, py, pc), device_id_type=MESH_ID)

        for k in range(1, N_DEV):
            rcopy(k, False).start()
        for k in range(1, N_DEV):
            rcopy(k, True).wait_recv()
        for k in range(1, N_DEV):
            rcopy(k, False).wait_send()
        if reduce:
            tot = slots[0]
            for d in range(1, N_DEV):
                tot = tot + slots[d]
            out_ref[...] = tot

    stack = jax.ShapeDtypeStruct((N_DEV, rows, LANES), F32)
    return pl.pallas_call(
        body, name=name,
        in_specs=[VMEM_SPEC], out_specs=VMEM_SPEC,
        out_shape=jax.ShapeDtypeStruct((rows, LANES), F32) if reduce else stack,
        scratch_shapes=([pltpu.VMEM((N_DEV, rows, LANES), F32)] if reduce else [])
        + [pltpu.SemaphoreType.DMA((N_DEV - 1,)), pltpu.SemaphoreType.DMA((N_DEV - 1,))],
    )(vec)


def _adamw_call(w, g, m, v, name):
    r, c = w.shape
    rb = r if r <= 256 else (256 if r % 256 == 0 else 352)
    assert r % rb == 0

    def body(w_ref, g_ref, m_ref, v_ref, d_ref, nm_ref, nv_ref):
        gv = g_ref[...]
        nm = ADAM_B1 * m_ref[...] + (1.0 - ADAM_B1) * gv
        nv = ADAM_B2 * v_ref[...] + (1.0 - ADAM_B2) * jnp.square(gv)
        m_hat = nm / (1.0 - ADAM_B1 ** ADAM_STEP)
        v_hat = nv / (1.0 - ADAM_B2 ** ADAM_STEP)
        d_ref[...] = -ADAM_LR * (m_hat / (jnp.sqrt(v_hat) + ADAM_EPS) + ADAM_WD * w_ref[...])
        nm_ref[...] = nm
        nv_ref[...] = nv

    spec = pl.BlockSpec((rb, c), lambda i: (i, 0))
    sd = jax.ShapeDtypeStruct((r, c), F32)
    return pl.pallas_call(
        body, name=name, grid=(r // rb,),
        in_specs=[spec] * 4, out_specs=[spec] * 3, out_shape=[sd, sd, sd],
        compiler_params=_cp("parallel"),
    )(w, g, m, v)


SMALL = (("attn_norm_w", D_MODEL), ("ret_gn_w", RET_W), ("mla_q_norm_w", Q_RANK), ("mla_kv_norm_w", KV_RANK),
         ("ffn_norm_w", D_MODEL), ("conv_b", F2), ("final_norm_w", D_MODEL))
WEIGHT_ORDER = ("attn_norm_w", "w_in", "ret_gn_w", "mla_q_norm_w", "w_uq", "mla_kv_norm_w", "w_ukv", "w_out",
                "ffn_norm_w", "w_up", "conv_w", "conv_b", "w_down", "final_norm_w")


def _pad_rows(flat, rows):
    return jnp.concatenate([flat, jnp.zeros((rows * LANES - flat.shape[0],), flat.dtype)]).reshape(rows, LANES)


def kernel(x, positions, attn_norm_w, w_in, ret_gn_w, mla_q_norm_w, w_uq, mla_kv_norm_w, w_ukv, w_out, ffn_norm_w, w_up, conv_w, conv_b, w_down, final_norm_w, loss_target, m_attn_norm_w, m_w_in, m_ret_gn_w, m_mla_q_norm_w, m_w_uq, m_mla_kv_norm_w, m_w_ukv, m_w_out, m_ffn_norm_w, m_w_up, m_conv_w, m_conv_b, m_w_down, m_final_norm_w, v_attn_norm_w, v_w_in, v_ret_gn_w, v_mla_q_norm_w, v_w_uq, v_mla_kv_norm_w, v_w_ukv, v_w_out, v_ffn_norm_w, v_w_up, v_conv_w, v_conv_b, v_w_down, v_final_norm_w):
    args = dict(locals())
    cx, cy, cc = _mesh_pos()
    sm = 2 * cx + cy

    gathered = _all_gather_call(_pack([args[n][0] for n, _, _ in PACKED], BF16))
    full = {}
    for (n, (r, c), axis), piece in zip(PACKED, _unpack(gathered.reshape(4, PACK_ROWS * LANES))):
        if n == "w_up":
            full[n] = piece
        else:
            full[n] = piece.transpose(1, 0, 2).reshape(r, 4 * c) if axis == 1 else piece.reshape(4 * r, c)
    cw_rows = 40
    cw_all = _exchange8_call(_pad_rows(conv_w[0].reshape(-1), cw_rows), False, "conv_w_all_gather")
    cw_all = cw_all[0::2].reshape(4, cw_rows * LANES)[:, :3 * F2 // 4].reshape(4, 3, F2 // 4)
    full["conv_w"] = cw_all.transpose(1, 0, 2).reshape(3, F2)
    small = {n: args[n].reshape(1, d) for n, d in SMALL}

    loss, gx, gw, gs = _local_step(x[0], positions[0], loss_target[0], full, small)

    shards = []
    for n, (r, c), axis in PACKED:
        g = gw[n]
        if axis == 1 and g.ndim == 2:
            g = g.reshape(r, 4, c).transpose(1, 0, 2)
        shards.append(g.reshape(4, r * c))
    gflat = jnp.concatenate(shards + [jnp.zeros((4, PACK_ROWS * LANES - sum(s.shape[1] for s in shards)), F32)], axis=1)
    gpk = gflat.reshape(4, 2, HALF_ROWS, LANES)
    p, pb = _rs_add1_call(gpk, _rs_sibling_call(gpk), cc.reshape(1).astype(jnp.int32))
    fin = _rs_add2_call(p, _rs_chips_call(pb), sm.reshape(1).astype(jnp.int32))
    sib = _rs_share_call(fin)
    both = jnp.where(cc == 0, jnp.stack([fin, sib]), jnp.stack([sib, fin]))
    red = dict(zip([n for n, _, _ in PACKED], _unpack(both.reshape(PACK_ROWS * LANES))))

    vec = jnp.concatenate([gs[n].reshape(-1) for n, _ in SMALL] + [gw["conv_w"].reshape(-1), loss.reshape(-1)])
    n_small = sum(d for _, d in SMALL)
    tot = _exchange8_call(_pad_rows(vec, 216), True, "small_all_reduce").reshape(-1)
    off = 0
    for n, d in SMALL:
        red[n] = tot[off:off + d].reshape(1, d)
        off += d
    red["conv_w"] = lax.dynamic_slice(tot[off:off + 3 * F2].reshape(3, F2), (0, sm * (F2 // 4)), (3, F2 // 4))
    loss_tot = tot[off + 3 * F2]

    grads, deltas, new_m, new_v = [], [], [], []
    for n in WEIGHT_ORDER:
        shape = args[n].shape
        two_d = (1, shape[0]) if len(shape) == 1 else shape[-2:]
        g = red[n].reshape(two_d)
        d, nm, nv = _adamw_call(args[n].reshape(two_d), g, args["m_" + n].reshape(two_d), args["v_" + n].reshape(two_d), "adamw_" + n)
        grads.append(g.reshape(shape))
        deltas.append(d.reshape(shape))
        new_m.append(nm.reshape(shape))
        new_v.append(nv.reshape(shape))
    return (loss_tot, gx[None], *grads, *deltas, *new_m, *new_v)
```

```python
import functools
import math

import numpy as np
import jax
import jax.numpy as jnp
from jax import lax
from jax.experimental import pallas as pl
from jax.experimental.pallas import tpu as pltpu

F32 = jnp.float32
BF16 = jnp.bfloat16

D_MODEL = 1024
N_HEADS = 8
HEAD = 64
RET_W = N_HEADS * HEAD
MLA_W = N_HEADS * HEAD
ROPE = 32
Q_RANK = 256
KV_RANK = 128
D_FF = 2816
F2 = 2 * D_FF
IN_W = 4 * RET_W + Q_RANK + KV_RANK + ROPE
IN_EXT = 4 * RET_W + Q_RANK + KV_RANK + 128
KPE_LO = 64
ROPE_BASE = 10000.0
EPS = 1e-6
RET_CHUNK = 128
SM_SCALE = (HEAD + ROPE) ** -0.5
LOG2E = math.log2(math.e)
LN2 = math.log(2.0)
NEG = -1e30
LANES = 128
VMEM_LIMIT = 56 * 1024 * 1024

ADAM_LR = 0.001
ADAM_B1 = 0.9
ADAM_B2 = 0.999
ADAM_EPS = 1e-08
ADAM_WD = 0.01
ADAM_STEP = 10


def _cp(*sem):
    return pltpu.CompilerParams(dimension_semantics=sem, vmem_limit_bytes=VMEM_LIMIT)


def _full(shape):
    n = len(shape)
    return pl.BlockSpec(tuple(shape), lambda *_: (0,) * n)


def _row(ts, c):
    return pl.BlockSpec((ts, c), lambda i: (i, 0))


def _hrow(h, ts, c):
    return pl.BlockSpec((h, ts, c), lambda i: (0, i, 0))


def _dot(a, b):
    return jnp.dot(a, b, preferred_element_type=F32)


def _dot_nt(a, b):
    return lax.dot_general(a, b, (((1,), (1,)), ((), ())), preferred_element_type=F32)


def _dot_tn(a, b):
    return lax.dot_general(a, b, (((0,), (0,)), ((), ())), preferred_element_type=F32)


def _dot_hi(a, b):
    hi = a.astype(BF16)
    lo = (a - hi.astype(F32)).astype(BF16)
    bb = b.astype(BF16)
    return _dot(hi, bb) + _dot(lo, bb)


def _rot_half(x, half):
    w = x.shape[-1]
    lane = lax.broadcasted_iota(jnp.int32, x.shape, x.ndim - 1)
    first = (lane % (2 * half)) < half
    return jnp.where(first, -pltpu.roll(x, w - half, x.ndim - 1), pltpu.roll(x, half, x.ndim - 1))


def _rope(x, cos, sin, half):
    return x * cos + _rot_half(x, half) * sin


def _unrope(dy, cos, sin, half):
    return dy * cos - _rot_half(dy, half) * sin


def _silu(g):
    return g / (1.0 + jnp.exp(-g))


def _rstd(x):
    return lax.rsqrt(jnp.mean(x * x, axis=-1, keepdims=True) + EPS)


def _rope_tables(positions):
    pos = positions.astype(F32)[:, None]
    s = pos.shape[0]
    inv = ROPE_BASE ** (-jnp.arange(0, HEAD, 2, dtype=F32) / HEAD)
    ang = pos * inv
    c, sn = jnp.cos(ang), jnp.sin(ang)
    cos_r = jnp.tile(jnp.concatenate([c, c], -1), (1, 2))
    sin_r = jnp.tile(jnp.concatenate([sn, sn], -1), (1, 2))
    inv = ROPE_BASE ** (-jnp.arange(0, ROPE, 2, dtype=F32) / ROPE)
    ang = pos * inv
    c, sn = jnp.cos(ang), jnp.sin(ang)
    one, zero = jnp.ones((s, KPE_LO), F32), jnp.zeros((s, KPE_LO), F32)
    cos_m = jnp.concatenate([one, c, c, one[:, :LANES - KPE_LO - ROPE]], -1)
    sin_m = jnp.concatenate([zero, sn, sn, zero[:, :LANES - KPE_LO - ROPE]], -1)
    return cos_r, sin_r, cos_m, sin_m


def _ret_consts():
    c = RET_CHUNK
    lg = np.log1p(-np.power(2.0, -5.0 - np.arange(N_HEADS, dtype=np.float64)))
    idx = np.arange(c, dtype=np.float64)
    diff = idx[:, None] - idx[None, :]
    lane_head = np.arange(LANES) // HEAD
    dmask = np.zeros((4, 2, c, c))
    zeta = np.zeros((4, c, LANES))
    xi = np.zeros((4, c, LANES))
    cd = np.zeros((4, LANES, LANES))
    bd = (lane_head[:, None] == lane_head[None, :]).astype(np.float64)
    for j in range(4):
        for hh in range(2):
            dmask[j, hh] = np.where(diff >= 0, np.exp(lg[2 * j + hh] * np.maximum(diff, 0.0)), 0.0)
        lgl = lg[2 * j + lane_head]
        zeta[j] = np.exp(lgl[None, :] * (c - 1.0 - idx[:, None]))
        xi[j] = np.exp(lgl[None, :] * (idx[:, None] + 1.0))
        cd[j] = np.exp(lgl * c)[:, None] * bd
    f = lambda a: jnp.asarray(a, F32)
    side = lambda d: np.concatenate([d[:, 0], d[:, 1]], axis=-1)
    return dict(dmask=f(side(dmask)), dmask_t=f(side(np.swapaxes(dmask, 2, 3))), zeta=f(zeta), xi=f(xi), cd=f(cd), bd=f(bd))


def _f1_call(x, anw, win, cos_r, sin_r, cos_m, sin_m, ts):
    s = x.shape[0]

    def body(x_ref, anw_ref, w_ref, cr_ref, sr_ref, cm_ref, sm_ref,
             q_ref, k_ref, v_ref, g_ref, cq_ref, ckv_ref, kpe_ref, r_ref):
        xv = x_ref[...]
        r = _rstd(xv)
        r_ref[...] = r
        h = (xv * r * anw_ref[...]).astype(BF16)
        cr, sr = cr_ref[...], sr_ref[...]
        qk = _dot(h, w_ref[:, 0:2 * RET_W])
        for j in range(4):
            sl = slice(j * LANES, (j + 1) * LANES)
            q_ref[:, sl] = _rope(qk[:, sl], cr, sr, HEAD // 2).astype(BF16)
            kk = qk[:, RET_W + j * LANES:RET_W + (j + 1) * LANES]
            k_ref[:, sl] = (_rope(kk, cr, sr, HEAD // 2) * (HEAD ** -0.5)).astype(BF16)
        v_ref[...] = _dot(h, w_ref[:, 2 * RET_W:3 * RET_W]).astype(BF16)
        g_ref[...] = _dot(h, w_ref[:, 3 * RET_W:4 * RET_W])
        o = 4 * RET_W
        cq_ref[...] = _dot(h, w_ref[:, o:o + Q_RANK])
        ckv_ref[...] = _dot(h, w_ref[:, o + Q_RANK:o + Q_RANK + KV_RANK])
        kp = _dot(h, w_ref[:, o + Q_RANK + KV_RANK:IN_EXT])
        kpe_ref[...] = _rope(kp, cm_ref[...], sm_ref[...], ROPE // 2)

    sd = jax.ShapeDtypeStruct
    return pl.pallas_call(
        body, name="f1_in_proj", grid=(s // ts,),
        in_specs=[_row(ts, D_MODEL), _full((1, D_MODEL)), _full((D_MODEL, IN_EXT)),
                  _row(ts, LANES), _row(ts, LANES), _row(ts, LANES), _row(ts, LANES)],
        out_specs=[_row(ts, RET_W), _row(ts, RET_W), _row(ts, RET_W), _row(ts, RET_W),
                   _row(ts, Q_RANK), _row(ts, KV_RANK), _row(ts, LANES), _row(ts, 1)],
        out_shape=[sd((s, RET_W), BF16), sd((s, RET_W), BF16), sd((s, RET_W), BF16), sd((s, RET_W), F32),
                   sd((s, Q_RANK), F32), sd((s, KV_RANK), F32), sd((s, LANES), F32), sd((s, 1), F32)],
        compiler_params=_cp("parallel"),
    )(x, anw, win, cos_r, sin_r, cos_m, sin_m)


def _stack_heads(a):
    lo = lax.broadcasted_iota(jnp.int32, a.shape, 1) < HEAD
    zero = jnp.zeros_like(a)
    return jnp.concatenate([jnp.where(lo, a, zero), jnp.where(lo, zero, a)], axis=0)


def _pair_product(a, b2, decay2, w2):
    return _dot((_dot_nt(a, b2) * decay2).astype(BF16), w2)


def _ret_fwd_call(q, k, v, g, gnw, rc, tr):
    s = q.shape[0]
    c = RET_CHUNK
    nc = tr // c

    def body(q_ref, k_ref, v_ref, g_ref, gnw_ref, dm_ref, zeta_ref, xi_ref, cd_ref, bd_ref, o_ref, y_ref, st_ref):
        @pl.when(pl.program_id(1) == 0)
        def _():
            st_ref[...] = jnp.zeros_like(st_ref)

        lane = lax.broadcasted_iota(jnp.int32, (c, LANES), 1)
        bd = bd_ref[...]
        chunks = [slice(ci * c, (ci + 1) * c) for ci in range(nc)]
        contrib = [_dot_tn((k_ref[rows, :].astype(F32) * zeta_ref[0]).astype(BF16), v_ref[rows, :]) * bd for rows in chunks]
        st, states = st_ref[...], []
        for ci in range(nc):
            states.append(st.astype(BF16))
            st = st * cd_ref[0] + contrib[ci]
        st_ref[...] = st
        for ci, rows in enumerate(chunks):
            qc = q_ref[rows, :]
            o_ref[rows, :] = (_dot(qc, states[ci]) * xi_ref[0]
                              + _pair_product(qc, _stack_heads(k_ref[rows, :]), dm_ref[0], _stack_heads(v_ref[rows, :])))
        o = o_ref[...]
        avg = bd * (1.0 / HEAD)
        ctr = o - _dot_hi(o, avg)
        var = _dot_hi(ctr * ctr, avg)
        y_ref[...] = (_silu(g_ref[...]) * (ctr * lax.rsqrt(var + EPS) * gnw_ref[...])).astype(BF16)

    slab = pl.BlockSpec((tr, LANES), lambda j, i: (i, j))
    sd = jax.ShapeDtypeStruct
    return pl.pallas_call(
        body, name="ret_fwd", grid=(4, s // tr),
        in_specs=[slab, slab, slab, slab, pl.BlockSpec((1, LANES), lambda j, i: (0, j)),
                  pl.BlockSpec((1, c, 2 * c), lambda j, i: (j, 0, 0)),
                  pl.BlockSpec((1, c, LANES), lambda j, i: (j, 0, 0)),
                  pl.BlockSpec((1, c, LANES), lambda j, i: (j, 0, 0)),
                  pl.BlockSpec((1, LANES, LANES), lambda j, i: (j, 0, 0)),
                  pl.BlockSpec((LANES, LANES), lambda j, i: (0, 0))],
        out_specs=[slab, slab],
        out_shape=[sd((s, RET_W), F32), sd((s, RET_W), BF16)],
        scratch_shapes=[pltpu.VMEM((LANES, LANES), F32)],
        compiler_params=_cp("parallel", "arbitrary"),
    )(q, k, v, g, gnw, rc["dmask"], rc["zeta"], rc["xi"], rc["cd"], rc["bd"])


QK_AUX = HEAD + ROPE
V_AUX = HEAD


def _lane_pair(shape, lo, a, b, rest):
    lane = lax.broadcasted_iota(jnp.int32, shape, len(shape) - 1)
    return jnp.where(lane == lo, a, jnp.where(lane == lo + 1, b, rest))


def _hi_lo(v):
    hi = v.astype(BF16).astype(F32)
    return hi, v - hi


def _mla_pre_call(cq, ckv, kpe, qnw, kvnw, wq, wk, wv, cos_m, sin_m, ts):
    s = cq.shape[0]

    def body(cq_ref, ckv_ref, kpe_ref, qnw_ref, kvnw_ref, wq_ref, wk_ref, wv_ref, cm_ref, sm_ref, q_ref, k_ref, v_ref):
        cqv, ckvv = cq_ref[...], ckv_ref[...]
        cqn = (cqv * _rstd(cqv) * qnw_ref[...]).astype(BF16)
        ckvn = (ckvv * _rstd(ckvv) * kvnw_ref[...]).astype(BF16)
        cm, sm = cm_ref[...], sm_ref[...]
        kp = _lane_pair((ts, LANES), QK_AUX, -1.0, -1.0, kpe_ref[...])
        for h in range(N_HEADS):
            qh = _rope(_dot(cqn, wq_ref[h]), cm, sm, ROPE // 2)
            q_ref[h] = (qh * (SM_SCALE * LOG2E)).astype(BF16)
            k_ref[h] = (_dot(ckvn, wk_ref[h]) + kp).astype(BF16)
            v_ref[h] = _lane_pair((ts, LANES), V_AUX, 1.0, 1.0, _dot(ckvn, wv_ref[h])).astype(BF16)

    sd = jax.ShapeDtypeStruct
    hm = sd((N_HEADS, s, LANES), BF16)
    return pl.pallas_call(
        body, name="mla_pre", grid=(s // ts,),
        in_specs=[_row(ts, Q_RANK), _row(ts, KV_RANK), _row(ts, LANES), _full((1, Q_RANK)), _full((1, KV_RANK)),
                  _full((N_HEADS, Q_RANK, LANES)), _full((N_HEADS, KV_RANK, LANES)), _full((N_HEADS, KV_RANK, LANES)),
                  _row(ts, LANES), _row(ts, LANES)],
        out_specs=[_hrow(N_HEADS, ts, LANES)] * 3,
        out_shape=[hm, hm, hm],
        compiler_params=_cp("parallel"),
    )(cq, ckv, kpe, qnw, kvnw, wq, wk, wv, cos_m, sin_m)


def _flash_fwd_call(q, k, v, tb):
    s = q.shape[1]
    nb = s // tb

    def body(q_ref, k_ref, v_ref, o_ref, qb_ref, m_ref, acc_ref):
        qi, ki = pl.program_id(0), pl.program_id(1)

        @pl.when(ki == 0)
        def _():
            m_ref[...] = jnp.full_like(m_ref, NEG)
            acc_ref[...] = jnp.zeros_like(acc_ref)

        def step(masked):
            if masked:
                keep = lax.broadcasted_iota(jnp.int32, (tb, tb), 1) <= lax.broadcasted_iota(jnp.int32, (tb, tb), 0)
            def finish(h, pe, alpha):
                acc_ref[h] = acc_ref[h] * alpha + _dot(pe, v_ref[h])

            nxt, pending = _dot_nt(q_ref[0], k_ref[0]), None
            for h in range(N_HEADS):
                sc = nxt
                if h + 1 < N_HEADS:
                    nxt = _dot_nt(q_ref[h + 1], k_ref[h + 1])
                if masked:
                    sc = jnp.where(keep, sc, NEG)
                m_prev = m_ref[h]
                m_new = jnp.maximum(m_prev, jnp.max(sc, axis=1, keepdims=True))
                pe = jnp.exp2(sc - jnp.tile(m_new, (1, tb // LANES))).astype(BF16)
                m_ref[h] = m_new
                if pending is not None:
                    finish(*pending)
                pending = (h, pe, jnp.exp2(m_prev - m_new))
            finish(*pending)

        @pl.when(ki < qi)
        def _():
            step(False)

        @pl.when(ki == qi)
        def _():
            step(True)
            lane = lax.broadcasted_iota(jnp.int32, (tb, LANES), 1)
            for p in range(N_HEADS // 2):
                outs = []
                for h in (2 * p, 2 * p + 1):
                    acc = acc_ref[h]
                    l = acc[:, V_AUX:V_AUX + 1]
                    outs.append(acc * (1.0 / l))
                    hi, lo = _hi_lo(m_ref[h][:, 0:1] + jnp.log(l) * LOG2E)
                    qb_ref[h] = _lane_pair((tb, LANES), QK_AUX, hi, lo, q_ref[h].astype(F32)).astype(BF16)
                o_ref[:, p * LANES:(p + 1) * LANES] = jnp.where(lane < HEAD, outs[0], pltpu.roll(outs[1], HEAD, 1)).astype(BF16)

    sd = jax.ShapeDtypeStruct
    qspec = pl.BlockSpec((N_HEADS, tb, LANES), lambda qi, ki: (0, qi, 0))
    kspec = pl.BlockSpec((N_HEADS, tb, LANES), lambda qi, ki: (0, jnp.minimum(ki, qi), 0))
    return pl.pallas_call(
        body, name="mla_flash_fwd", grid=(nb, nb),
        in_specs=[qspec, kspec, kspec],
        out_specs=[pl.BlockSpec((tb, MLA_W), lambda qi, ki: (qi, 0)), qspec],
        out_shape=[sd((s, MLA_W), BF16), sd((N_HEADS, s, LANES), BF16)],
        scratch_shapes=[pltpu.VMEM((N_HEADS, tb, LANES), F32), pltpu.VMEM((N_HEADS, tb, LANES), F32)],
        compiler_params=_cp("parallel", "arbitrary"),
    )(q, k, v)


def _out_proj_call(x, yret, ymla, wout, ts):
    s = x.shape[0]

    def body(x_ref, yr_ref, ym_ref, w_ref, x1_ref, r_ref):
        x1 = x_ref[...] + _dot(yr_ref[...], w_ref[0:RET_W, :]) + _dot(ym_ref[...], w_ref[RET_W:, :])
        x1_ref[...] = x1
        r_ref[...] = _rstd(x1)

    sd = jax.ShapeDtypeStruct
    return pl.pallas_call(
        body, name="out_proj", grid=(s // ts,),
        in_specs=[_row(ts, D_MODEL), _row(ts, RET_W), _row(ts, MLA_W), _full((D_MODEL, D_MODEL))],
        out_specs=[_row(ts, D_MODEL), _row(ts, 1)],
        out_shape=[sd((s, D_MODEL), F32), sd((s, 1), F32)],
        compiler_params=_cp("parallel"),
    )(x, yret, ymla, wout)


W_UP_SHARD = F2 // 4


def _ffn_fwd_call(x1, r2, fnw, wup4, cw, cb, wdown, ts):
    s = x1.shape[0]
    wsh = W_UP_SHARD

    def body(x_ref, r_ref, fnw_ref, wup_ref, cw_ref, cb_ref, wd_ref, u_ref, x2_ref, carry_ref):
        _zero_first(pl.program_id(0) == 0, carry_ref)
        xv = x_ref[...]
        h = (xv * r_ref[...] * fnw_ref[...]).astype(BF16)
        conv = []
        for j in range(4):
            cols = slice(j * wsh, (j + 1) * wsh)
            ub = _dot(h, wup_ref[j]).astype(BF16)
            u_ref[:, cols] = ub
            u = ub.astype(F32)
            u1, u2 = _shifted(u, carry_ref[:, cols])
            w = cw_ref[:, cols]
            conv.append(cb_ref[:, cols] + w[0:1, :] * u2 + w[1:2, :] * u1 + w[2:3, :] * u)
            carry_ref[:, cols] = u[ts - 8:, :]
        acc = xv
        for j in range(2):
            a = (_silu(conv[j]) * conv[j + 2]).astype(BF16)
            acc = acc + _dot(a, wd_ref[j * wsh:(j + 1) * wsh, :])
        x2_ref[...] = acc

    sd = jax.ShapeDtypeStruct
    return pl.pallas_call(
        body, name="ffn_fwd", grid=(s // ts,),
        in_specs=[_row(ts, D_MODEL), _row(ts, 1), _full((1, D_MODEL)), _full((4, D_MODEL, wsh)),
                  _full((3, F2)), _full((1, F2)), _full((D_FF, D_MODEL))],
        out_specs=[_row(ts, F2), _row(ts, D_MODEL)],
        out_shape=[sd((s, F2), BF16), sd((s, D_MODEL), F32)],
        scratch_shapes=[pltpu.VMEM((8, F2), F32)],
        compiler_params=_cp("arbitrary"),
    )(x1, r2, fnw, wup4, cw, cb, wdown)


def _shifted(u, hal):
    row = lax.broadcasted_iota(jnp.int32, u.shape, 0)
    u1 = jnp.where(row == 0, hal[7:8, :], pltpu.roll(u, 1, 0))
    u2 = jnp.where(row == 0, hal[6:7, :], jnp.where(row == 1, hal[7:8, :], pltpu.roll(u, 2, 0)))
    return u1, u2


def _conv_tile(u_ref, hal_ref, w_ref, b_ref, first):
    u = u_ref[...].astype(F32)
    hal = jnp.where(first, 0.0, hal_ref[...].astype(F32))
    u1, u2 = _shifted(u, hal)
    w = w_ref[...]
    return b_ref[...] + w[0:1, :] * u2 + w[1:2, :] * u1 + w[2:3, :] * u, u1, u2, u


def _gate_specs(ts, tf, rows_inner=False):
    nf = D_FF // tf
    hb = ts // 8

    def spec(shape, fn):
        return pl.BlockSpec(shape, (lambda j, i: fn(i, j)) if rows_inner else fn)

    return [
        spec((ts, tf), lambda i, j: (i, j)),
        spec((8, tf), lambda i, j: (jnp.maximum(i * hb - 1, 0), j)),
        spec((ts, tf), lambda i, j: (i, j + nf)),
        spec((8, tf), lambda i, j: (jnp.maximum(i * hb - 1, 0), j + nf)),
        spec((3, tf), lambda i, j: (0, j)),
        spec((3, tf), lambda i, j: (0, j + nf)),
        spec((1, tf), lambda i, j: (0, j)),
        spec((1, tf), lambda i, j: (0, j + nf)),
    ]


def _prep_weights(w):
    win = w["w_in"]
    pad = lambda n: jnp.zeros((D_MODEL, n), win.dtype)
    win_ext = jnp.concatenate([win[:, :IN_W - ROPE], pad(KPE_LO), win[:, IN_W - ROPE:], pad(LANES - KPE_LO - ROPE)], -1)
    wuq = w["w_uq"].reshape(Q_RANK, N_HEADS, HEAD + ROPE)
    wq = jnp.concatenate([wuq, jnp.zeros((Q_RANK, N_HEADS, LANES - HEAD - ROPE), wuq.dtype)], -1).transpose(1, 0, 2)
    wukv = w["w_ukv"].reshape(KV_RANK, N_HEADS, 2 * HEAD)
    zk = jnp.zeros((KV_RANK, N_HEADS, HEAD), wukv.dtype)
    wk = jnp.concatenate([wukv[:, :, :HEAD], zk], -1).transpose(1, 0, 2)
    wv = jnp.concatenate([wukv[:, :, HEAD:], zk], -1).transpose(1, 0, 2)
    c = lambda a: a.astype(BF16)
    wup = w["w_up"]
    if wup.ndim == 2:
        wup = wup.reshape(D_MODEL, 4, W_UP_SHARD).transpose(1, 0, 2)
    return dict(win=c(win_ext), wq=c(wq), wk=c(wk), wv=c(wv), wout=c(w["w_out"]), wup=c(wup), wdown=c(w["w_down"]))


def _tiles(s):
    return dict(ts=min(s, 512), tr=min(s, 1024), tb=min(s, 512), tg=min(s, 512), tf=D_FF // 2, t2=min(s, 256))


def _forward(x, positions, w, small):
    s = x.shape[0]
    t = _tiles(s)
    pw = _prep_weights(w)
    cos_r, sin_r, cos_m, sin_m = _rope_tables(positions)
    rc = _ret_consts()
    q, k, v, g, cq, ckv, kpe, r1 = _f1_call(x, small["attn_norm_w"], pw["win"], cos_r, sin_r, cos_m, sin_m, t["ts"])
    o_ret, y_ret = _ret_fwd_call(q, k, v, g, small["ret_gn_w"], rc, t["tr"])
    mq, mk, mv = _mla_pre_call(cq, ckv, kpe, small["mla_q_norm_w"], small["mla_kv_norm_w"],
                               pw["wq"], pw["wk"], pw["wv"], cos_m, sin_m, t["ts"])
    y_mla, mqb = _flash_fwd_call(mq, mk, mv, t["tb"])
    x1, r2 = _out_proj_call(x, y_ret, y_mla, pw["wout"], t["ts"])
    u, x2 = _ffn_fwd_call(x1, r2, small["ffn_norm_w"], pw["wup"], w["conv_w"], small["conv_b"], pw["wdown"], t["t2"])
    return dict(pw=pw, tabs=(cos_r, sin_r, cos_m, sin_m), rc=rc, q=q, k=k, v=v, g=g, cq=cq, ckv=ckv, kpe=kpe, r1=r1,
                o_ret=o_ret, y_ret=y_ret, mqb=mqb, mk=mk, mv=mv, y_mla=y_mla, x1=x1, r2=r2, u=u, x2=x2)


def _norm_bwd(dh, xh, r, nw):
    dxn = dh * nw
    return r * (dxn - xh * jnp.mean(dxn * xh, axis=-1, keepdims=True))


def _zero_first(first, *refs):
    @pl.when(first)
    def _():
        for ref in refs:
            ref[...] = jnp.zeros_like(ref)


def _colsum(v):
    return jnp.sum(v, axis=0, keepdims=True)


def _dsilu(g, sg):
    return sg * (1.0 + g * (1.0 - sg))


def _loss_call(x2, tgt, fw, ts):
    s = x2.shape[0]

    def body(x_ref, t_ref, fw_ref, dx_ref, loss_ref, gfw_ref):
        _zero_first(pl.program_id(0) == 0, loss_ref, gfw_ref)
        xv = x_ref[...]
        r = _rstd(xv)
        xh = xv * r
        fwv = fw_ref[...]
        e = xh * fwv - t_ref[...]
        loss_ref[...] += (0.5 / D_MODEL) * _colsum(jnp.sum(e * e, axis=1, keepdims=True))
        dy = e * (1.0 / D_MODEL)
        gfw_ref[...] += _colsum(dy * xh)
        dx_ref[...] = _norm_bwd(dy, xh, r, fwv)

    sd = jax.ShapeDtypeStruct
    return pl.pallas_call(
        body, name="loss_bwd", grid=(s // ts,),
        in_specs=[_row(ts, D_MODEL), _row(ts, D_MODEL), _full((1, D_MODEL))],
        out_specs=[_row(ts, D_MODEL), _full((1, 1)), _full((1, D_MODEL))],
        out_shape=[sd((s, D_MODEL), F32), sd((1, 1), F32), sd((1, D_MODEL), F32)],
        compiler_params=_cp("arbitrary"),
    )(x2, tgt, fw)


def _ffn_bwd1_call(dx2, u, cw, cb, wdown, ts, tf):
    s = dx2.shape[0]
    nf = D_FF // tf

    def body(dx_ref, wd_ref, ug_ref, hg_ref, uv_ref, hv_ref, wg_ref, wv_ref, bg_ref, bv_ref,
             dgate_ref, dval_ref, dwd_ref, dcwg_ref, dcwv_ref, dcbg_ref, dcbv_ref):
        first = pl.program_id(1) == 0
        _zero_first(first, dwd_ref, dcwg_ref, dcwv_ref, dcbg_ref, dcbv_ref)
        gate, g1, g2, g0 = _conv_tile(ug_ref, hg_ref, wg_ref, bg_ref, first)
        val, v1, v2, v0 = _conv_tile(uv_ref, hv_ref, wv_ref, bv_ref, first)
        dxb = dx_ref[...].astype(BF16)
        da = _dot_nt(dxb, wd_ref[...])
        sg = 1.0 / (1.0 + jnp.exp(-gate))
        sl = gate * sg
        dgate = da * val * _dsilu(gate, sg)
        dval = da * sl
        dgate_ref[...] = dgate.astype(BF16)
        dval_ref[...] = dval.astype(BF16)
        dwd_ref[...] += _dot_tn((sl * val).astype(BF16), dxb)
        for ref, d, taps in ((dcwg_ref, dgate, (g2, g1, g0)), (dcwv_ref, dval, (v2, v1, v0))):
            for t in range(3):
                ref[t:t + 1, :] += _colsum(d * taps[t])
        dcbg_ref[...] += _colsum(dgate)
        dcbv_ref[...] += _colsum(dval)

    sd = jax.ShapeDtypeStruct
    colacc = lambda r: pl.BlockSpec((r, tf), lambda j, i: (0, j))
    return pl.pallas_call(
        body, name="ffn_bwd_gate", grid=(nf, s // ts),
        in_specs=[pl.BlockSpec((ts, D_MODEL), lambda j, i: (i, 0)), pl.BlockSpec((tf, D_MODEL), lambda j, i: (j, 0))]
        + _gate_specs(ts, tf, rows_inner=True),
        out_specs=[pl.BlockSpec((ts, tf), lambda j, i: (i, j)), pl.BlockSpec((ts, tf), lambda j, i: (i, j)),
                   pl.BlockSpec((tf, D_MODEL), lambda j, i: (j, 0)), colacc(3), colacc(3), colacc(1), colacc(1)],
        out_shape=[sd((s, D_FF), BF16), sd((s, D_FF), BF16), sd((D_FF, D_MODEL), F32),
                   sd((3, D_FF), F32), sd((3, D_FF), F32), sd((1, D_FF), F32), sd((1, D_FF), F32)],
        compiler_params=_cp("parallel", "arbitrary"),
    )(dx2, wdown, u, u, u, u, cw, cw, cb, cb)


def _shifted_up(d, hal):
    n = d.shape[0]
    row = lax.broadcasted_iota(jnp.int32, d.shape, 0)
    d1 = jnp.where(row == n - 1, hal[0:1, :], pltpu.roll(d, n - 1, 0))
    d2 = jnp.where(row == n - 2, hal[0:1, :], jnp.where(row == n - 1, hal[1:2, :], pltpu.roll(d, n - 2, 0)))
    return d1, d2


def _ffn_bwd2_call(dgate, dval, cw, wup, x1, r2, fnw, dx2, ts):
    s = dx2.shape[0]
    nt = s // ts
    hb = ts // 8
    nxt = pl.BlockSpec((8, D_FF), lambda i: (jnp.minimum((i + 1) * hb, s // 8 - 1), 0))

    def body(dg_ref, hg_ref, dv_ref, hv_ref, cw_ref, wup_ref, x_ref, r_ref, fnw_ref, dx2_ref, du_ref, dx1_ref, dfnw_ref):
        i = pl.program_id(0)
        _zero_first(i == 0, dfnw_ref)
        dh = jnp.zeros((ts, D_MODEL), F32)
        for part, (d_ref, h_ref) in enumerate(((dg_ref, hg_ref), (dv_ref, hv_ref))):
            off = part * D_FF
            d = d_ref[...].astype(F32)
            hal = jnp.where(i == nt - 1, 0.0, h_ref[...].astype(F32))
            d1, d2 = _shifted_up(d, hal)
            w = cw_ref[:, off:off + D_FF]
            du = (w[2:3, :] * d + w[1:2, :] * d1 + w[0:1, :] * d2).astype(BF16)
            du_ref[:, off:off + D_FF] = du
            for j in range(2):
                dh = dh + _dot_nt(du[:, j * W_UP_SHARD:(j + 1) * W_UP_SHARD], wup_ref[2 * part + j])
        r = r_ref[...]
        xh = x_ref[...] * r
        dfnw_ref[...] += _colsum(dh * xh)
        dx1_ref[...] = dx2_ref[...] + _norm_bwd(dh, xh, r, fnw_ref[...])

    sd = jax.ShapeDtypeStruct
    return pl.pallas_call(
        body, name="ffn_bwd_up", grid=(nt,),
        in_specs=[_row(ts, D_FF), nxt, _row(ts, D_FF), nxt, _full((3, F2)), _full((4, D_MODEL, W_UP_SHARD)),
                  _row(ts, D_MODEL), _row(ts, 1), _full((1, D_MODEL)), _row(ts, D_MODEL)],
        out_specs=[_row(ts, F2), _row(ts, D_MODEL), _full((1, D_MODEL))],
        out_shape=[sd((s, F2), BF16), sd((s, D_MODEL), F32), sd((1, D_MODEL), F32)],
        compiler_params=_cp("arbitrary"),
    )(dgate, dgate, dval, dval, cw, wup, x1, r2, fnw, dx2)


def _dw_norm_call(x, r, nw, b, ts, tn, name):
    s, n = b.shape
    k = x.shape[1]

    def body(x_ref, r_ref, nw_ref, b_ref, dw_ref):
        _zero_first(pl.program_id(1) == 0, dw_ref)
        h = (x_ref[...] * r_ref[...] * nw_ref[...]).astype(BF16)
        dw_ref[...] += _dot_tn(h, b_ref[...])

    return pl.pallas_call(
        body, name=name, grid=(n // tn, s // ts),
        in_specs=[pl.BlockSpec((ts, k), lambda j, i: (i, 0)), pl.BlockSpec((ts, 1), lambda j, i: (i, 0)),
                  pl.BlockSpec((1, k), lambda j, i: (0, 0)), pl.BlockSpec((ts, tn), lambda j, i: (i, j))],
        out_specs=pl.BlockSpec((None, k, tn), lambda j, i: (j, 0, 0)),
        out_shape=jax.ShapeDtypeStruct((n // tn, k, tn), F32),
        compiler_params=_cp("parallel", "arbitrary"),
    )(x, r, nw, b)


def _out_bwd_call(dx1, yret, ymla, wout, ts):
    s = dx1.shape[0]

    def body(dx_ref, yr_ref, ym_ref, w_ref, dyr_ref, do_ref, dwo_ref):
        _zero_first(pl.program_id(0) == 0, dwo_ref)
        dxb = dx_ref[...].astype(BF16)
        dmix = _dot_nt(dxb, w_ref[...])
        dyr_ref[...] = dmix[:, :RET_W]
        ym = ym_ref[...]
        lane = lax.broadcasted_iota(jnp.int32, (ts, LANES), 1)
        for p in range(N_HEADS // 2):
            dom = dmix[:, RET_W + p * LANES:RET_W + (p + 1) * LANES]
            prod = dom * ym[:, p * LANES:(p + 1) * LANES].astype(F32)
            for hh in range(2):
                mine = (lane >= HEAD) if hh else (lane < HEAD)
                hi, lo = _hi_lo(jnp.sum(jnp.where(mine, prod, 0.0), axis=1, keepdims=True))
                base = jnp.where(lane < HEAD, pltpu.roll(dom, HEAD, 1) if hh else dom, 0.0)
                do_ref[2 * p + hh] = _lane_pair((ts, LANES), V_AUX, -hi, -lo, base).astype(BF16)
        dwo_ref[0:RET_W, :] += _dot_tn(yr_ref[...], dxb)
        dwo_ref[RET_W:, :] += _dot_tn(ym, dxb)

    sd = jax.ShapeDtypeStruct
    return pl.pallas_call(
        body, name="out_proj_bwd", grid=(s // ts,),
        in_specs=[_row(ts, D_MODEL), _row(ts, RET_W), _row(ts, MLA_W), _full((D_MODEL, D_MODEL))],
        out_specs=[_row(ts, RET_W), _hrow(N_HEADS, ts, LANES), _full((D_MODEL, D_MODEL))],
        out_shape=[sd((s, RET_W), F32), sd((N_HEADS, s, LANES), BF16), sd((D_MODEL, D_MODEL), F32)],
        compiler_params=_cp("arbitrary"),
    )(dx1, yret, ymla, wout)


def _ret_bwd_q_call(q, k, v, o, g, dy, gnw, rc, cos_r, sin_r, tr):
    s = q.shape[0]
    c = RET_CHUNK
    nc = tr // c

    def body(q_ref, k_ref, v_ref, o_ref, g_ref, dy_ref, gnw_ref, dm_ref, zeta_ref, xi_ref, cd_ref, bd_ref, cr_ref, sr_ref,
             dq_ref, dg_ref, do_ref, dgnw_ref, st_ref):
        _zero_first(pl.program_id(1) == 0, st_ref, dgnw_ref)
        bd = bd_ref[...]
        avg = bd * (1.0 / HEAD)
        ov = o_ref[...]
        ctr = ov - _dot_hi(ov, avg)
        rs = lax.rsqrt(_dot_hi(ctr * ctr, avg) + EPS)
        oh = ctr * rs
        gg, dyv, gnw_v = g_ref[...], dy_ref[...], gnw_ref[...]
        sg = 1.0 / (1.0 + jnp.exp(-gg))
        sl = gg * sg
        dg_ref[...] = (dyv * oh * gnw_v * _dsilu(gg, sg)).astype(BF16)
        dgnw_ref[...] += _colsum(dyv * sl * oh)
        doh = dyv * sl * gnw_v
        dov = (rs * (doh - _dot_hi(doh, avg) - oh * _dot_hi(doh * oh, avg))).astype(BF16)
        do_ref[...] = dov
        chunks = [slice(ci * c, (ci + 1) * c) for ci in range(nc)]
        contrib = [_dot_tn((k_ref[rows, :].astype(F32) * zeta_ref[0]).astype(BF16), v_ref[rows, :]) * bd for rows in chunks]
        st, states = st_ref[...], []
        for ci in range(nc):
            states.append(st.astype(BF16))
            st = st * cd_ref[0] + contrib[ci]
        st_ref[...] = st
        for ci, rows in enumerate(chunks):
            doc = dov[rows, :]
            dq = (_dot_nt(doc, states[ci]) * xi_ref[0]
                  + _pair_product(doc, _stack_heads(v_ref[rows, :]), dm_ref[0], _stack_heads(k_ref[rows, :])))
            dq_ref[rows, :] = _unrope(dq, cr_ref[rows, :], sr_ref[rows, :], HEAD // 2).astype(BF16)

    slab = pl.BlockSpec((tr, LANES), lambda j, i: (i, j))
    tab = pl.BlockSpec((tr, LANES), lambda j, i: (i, 0))
    vec = pl.BlockSpec((1, LANES), lambda j, i: (0, j))
    sd = jax.ShapeDtypeStruct
    return pl.pallas_call(
        body, name="ret_bwd_q", grid=(4, s // tr),
        in_specs=[slab, slab, slab, slab, slab, slab, vec,
                  pl.BlockSpec((1, c, 2 * c), lambda j, i: (j, 0, 0)),
                  pl.BlockSpec((1, c, LANES), lambda j, i: (j, 0, 0)),
                  pl.BlockSpec((1, c, LANES), lambda j, i: (j, 0, 0)),
                  pl.BlockSpec((1, LANES, LANES), lambda j, i: (j, 0, 0)),
                  pl.BlockSpec((LANES, LANES), lambda j, i: (0, 0)), tab, tab],
        out_specs=[slab, slab, slab, vec],
        out_shape=[sd((s, RET_W), BF16), sd((s, RET_W), BF16), sd((s, RET_W), BF16), sd((1, RET_W), F32)],
        scratch_shapes=[pltpu.VMEM((LANES, LANES), F32)],
        compiler_params=_cp("parallel", "arbitrary"),
    )(q, k, v, o, g, dy, gnw, rc["dmask"], rc["zeta"], rc["xi"], rc["cd"], rc["bd"], cos_r, sin_r)


def _ret_bwd_kv_call(q, k, v, do, rc, cos_r, sin_r, tr):
    s = q.shape[0]
    c = RET_CHUNK
    nc = tr // c
    nt = s // tr

    def body(q_ref, k_ref, v_ref, do_ref, dm_ref, zeta_ref, xi_ref, cd_ref, bd_ref, cr_ref, sr_ref, dk_ref, dv_ref, gs_ref):
        _zero_first(pl.program_id(1) == 0, gs_ref)
        bd = bd_ref[...]
        chunks = [slice(ci * c, (ci + 1) * c) for ci in range(nc)]
        contrib = [_dot_tn((q_ref[rows, :].astype(F32) * xi_ref[0]).astype(BF16), do_ref[rows, :]) * bd for rows in chunks]
        gs, states = gs_ref[...], [None] * nc
        for ci in reversed(range(nc)):
            states[ci] = gs.astype(BF16)
            gs = gs * cd_ref[0] + contrib[ci]
        gs_ref[...] = gs
        for ci, rows in enumerate(chunks):
            kc, vc = k_ref[rows, :], v_ref[rows, :]
            q2, do2 = _stack_heads(q_ref[rows, :]), _stack_heads(do_ref[rows, :])
            gb = states[ci]
            dk = _dot_nt(vc, gb) * zeta_ref[0] + _pair_product(vc, do2, dm_ref[0], q2)
            dv = _dot(kc, gb) * zeta_ref[0] + _pair_product(kc, q2, dm_ref[0], do2)
            dk_ref[rows, :] = (_unrope(dk, cr_ref[rows, :], sr_ref[rows, :], HEAD // 2) * (HEAD ** -0.5)).astype(BF16)
            dv_ref[rows, :] = dv.astype(BF16)

    slab = pl.BlockSpec((tr, LANES), lambda j, i: (nt - 1 - i, j))
    tab = pl.BlockSpec((tr, LANES), lambda j, i: (nt - 1 - i, 0))
    sd = jax.ShapeDtypeStruct
    return pl.pallas_call(
        body, name="ret_bwd_kv", grid=(4, nt),
        in_specs=[slab, slab, slab, slab,
                  pl.BlockSpec((1, c, 2 * c), lambda j, i: (j, 0, 0)),
                  pl.BlockSpec((1, c, LANES), lambda j, i: (j, 0, 0)),
                  pl.BlockSpec((1, c, LANES), lambda j, i: (j, 0, 0)),
                  pl.BlockSpec((1, LANES, LANES), lambda j, i: (j, 0, 0)),
                  pl.BlockSpec((LANES, LANES), lambda j, i: (0, 0)), tab, tab],
        out_specs=[slab, slab],
        out_shape=[sd((s, RET_W), BF16), sd((s, RET_W), BF16)],
        scratch_shapes=[pltpu.VMEM((LANES, LANES), F32)],
        compiler_params=_cp("parallel", "arbitrary"),
    )(q, k, v, do, rc["dmask_t"], rc["zeta"], rc["xi"], rc["cd"], rc["bd"], cos_r, sin_r)


FLASH_BWD_HEADS = 4


def _flash_bwd_call(qb, k, v, do, tb):
    s = qb.shape[1]
    nb = s // tb
    hg = FLASH_BWD_HEADS

    def body(q_ref, k_ref, v_ref, do_ref, dk_ref, dv_ref, dq_hbm, dka_ref, dva_ref, dq_ref, sem):
        g, ki, qi = pl.program_id(0), pl.program_id(1), pl.program_id(2)
        _zero_first((ki == 0) & (qi == 0), dq_ref)
        _zero_first(qi == 0, dka_ref, dva_ref)
        rows = pl.ds(pl.multiple_of(qi * tb, tb), tb)

        def step(masked):
            if masked:
                keep = lax.broadcasted_iota(jnp.int32, (tb, tb), 0) <= lax.broadcasted_iota(jnp.int32, (tb, tb), 1)
            for h in range(hg):
                st = _dot_nt(k_ref[h], q_ref[h])
                if masked:
                    st = jnp.where(keep, st, NEG)
                pt = jnp.exp2(st)
                dob = do_ref[h]
                dva_ref[h] += _dot(pt.astype(BF16), dob)
                dst = (pt * _dot_nt(v_ref[h], dob)).astype(BF16)
                dka_ref[h] += _dot(dst, q_ref[h])
                dq_ref[h, rows, :] += _dot_tn(dst, k_ref[h])

        @pl.when(qi > ki)
        def _():
            step(False)

        @pl.when(qi == ki)
        def _():
            step(True)

        @pl.when(qi == nb - 1)
        def _():
            dk_ref[...] = (dka_ref[...] * LN2).astype(BF16)
            dv_ref[...] = dva_ref[...].astype(BF16)

        @pl.when((ki == nb - 1) & (qi == nb - 1))
        def _():
            cp = pltpu.make_async_copy(dq_ref, dq_hbm.at[pl.ds(g * hg, hg)], sem)
            cp.start()
            cp.wait()

    kspec = pl.BlockSpec((hg, tb, LANES), lambda g, ki, qi: (g, ki, 0))
    qspec = pl.BlockSpec((hg, tb, LANES), lambda g, ki, qi: (g, jnp.maximum(qi, ki), 0))
    hm = jax.ShapeDtypeStruct((N_HEADS, s, LANES), BF16)
    return pl.pallas_call(
        body, name="mla_flash_bwd", grid=(N_HEADS // hg, nb, nb),
        in_specs=[qspec, kspec, kspec, qspec],
        out_specs=[kspec, kspec, ANY],
        out_shape=[hm, hm, jax.ShapeDtypeStruct((N_HEADS, s, LANES), F32)],
        scratch_shapes=[pltpu.VMEM((hg, tb, LANES), F32), pltpu.VMEM((hg, tb, LANES), F32),
                        pltpu.VMEM((hg, s, LANES), F32), pltpu.SemaphoreType.DMA],
        compiler_params=_cp("arbitrary", "arbitrary", "arbitrary"),
    )(qb, k, v, do)


def _mla_post_call(dq, dk, dv, cq, ckv, qnw, kvnw, wq, wk, wv, cos_m, sin_m, ts):
    s = cq.shape[0]

    def body(dq_ref, dk_ref, dv_ref, cq_ref, ckv_ref, qnw_ref, kvnw_ref, wq_ref, wk_ref, wv_ref, cm_ref, sm_ref,
             dcq_ref, dckv_ref, dkpe_ref, dwq_ref, dwk_ref, dwv_ref, dqnw_ref, dkvnw_ref):
        _zero_first(pl.program_id(0) == 0, dwq_ref, dwk_ref, dwv_ref, dqnw_ref, dkvnw_ref)
        cqv, ckvv = cq_ref[...], ckv_ref[...]
        rq, rkv = _rstd(cqv), _rstd(ckvv)
        qh_, kvh_ = cqv * rq, ckvv * rkv
        qnw_v, kvnw_v = qnw_ref[...], kvnw_ref[...]
        cqn = (qh_ * qnw_v).astype(BF16)
        ckvn = (kvh_ * kvnw_v).astype(BF16)
        cm, sm = cm_ref[...], sm_ref[...]
        dcqn = jnp.zeros((ts, Q_RANK), F32)
        dckvn = jnp.zeros((ts, KV_RANK), F32)
        dkpe = jnp.zeros((ts, LANES), F32)
        for h in range(N_HEADS):
            dqu = _unrope(dq_ref[h] * SM_SCALE, cm, sm, ROPE // 2).astype(BF16)
            dwq_ref[h] += _dot_tn(cqn, dqu)
            dcqn = dcqn + _dot_nt(dqu, wq_ref[h])
            dkb, dvb = dk_ref[h], dv_ref[h]
            dkpe = dkpe + dkb.astype(F32)
            dwk_ref[h] += _dot_tn(ckvn, dkb)
            dwv_ref[h] += _dot_tn(ckvn, dvb)
            dckvn = dckvn + _dot_nt(dkb, wk_ref[h]) + _dot_nt(dvb, wv_ref[h])
        lane = lax.broadcasted_iota(jnp.int32, (ts, LANES), 1)
        dkpe = jnp.where((lane >= KPE_LO) & (lane < KPE_LO + ROPE), dkpe, 0.0)
        dkpe_ref[...] = _unrope(dkpe, cm, sm, ROPE // 2).astype(BF16)
        dqnw_ref[...] += _colsum(dcqn * qh_)
        dkvnw_ref[...] += _colsum(dckvn * kvh_)
        dcq_ref[...] = _norm_bwd(dcqn, qh_, rq, qnw_v).astype(BF16)
        dckv_ref[...] = _norm_bwd(dckvn, kvh_, rkv, kvnw_v).astype(BF16)

    sd = jax.ShapeDtypeStruct
    hm = _hrow(N_HEADS, ts, LANES)
    return pl.pallas_call(
        body, name="mla_post", grid=(s // ts,),
        in_specs=[hm, hm, hm, _row(ts, Q_RANK), _row(ts, KV_RANK), _full((1, Q_RANK)), _full((1, KV_RANK)),
                  _full((N_HEADS, Q_RANK, LANES)), _full((N_HEADS, KV_RANK, LANES)), _full((N_HEADS, KV_RANK, LANES)),
                  _row(ts, LANES), _row(ts, LANES)],
        out_specs=[_row(ts, Q_RANK), _row(ts, KV_RANK), _row(ts, LANES),
                   _full((N_HEADS, Q_RANK, LANES)), _full((N_HEADS, KV_RANK, LANES)), _full((N_HEADS, KV_RANK, LANES)),
                   _full((1, Q_RANK)), _full((1, KV_RANK))],
        out_shape=[sd((s, Q_RANK), BF16), sd((s, KV_RANK), BF16), sd((s, LANES), BF16),
                   sd((N_HEADS, Q_RANK, LANES), F32), sd((N_HEADS, KV_RANK, LANES), F32), sd((N_HEADS, KV_RANK, LANES), F32),
                   sd((1, Q_RANK), F32), sd((1, KV_RANK), F32)],
        compiler_params=_cp("arbitrary"),
    )(dq, dk, dv, cq, ckv, qnw, kvnw, wq, wk, wv, cos_m, sin_m)


def _in_bwd_call(parts, x, r1, anw, dx1, win, ts):
    s = x.shape[0]
    widths = [p.shape[1] for p in parts]
    np_ = len(parts)

    def body(*refs):
        p_refs = refs[:np_]
        x_ref, r_ref, anw_ref, dx1_ref, w_ref, dx_ref, dw_ref, danw_ref = refs[np_:]
        _zero_first(pl.program_id(0) == 0, dw_ref, danw_ref)
        dproj = jnp.concatenate([p[...] for p in p_refs], axis=-1)
        r, anw_v = r_ref[...], anw_ref[...]
        xh = x_ref[...] * r
        dw_ref[...] += _dot_tn((xh * anw_v).astype(BF16), dproj)
        dh = _dot_nt(dproj, w_ref[...])
        danw_ref[...] += _colsum(dh * xh)
        dx_ref[...] = dx1_ref[...] + _norm_bwd(dh, xh, r, anw_v)

    sd = jax.ShapeDtypeStruct
    return pl.pallas_call(
        body, name="in_proj_bwd", grid=(s // ts,),
        in_specs=[_row(ts, w) for w in widths]
        + [_row(ts, D_MODEL), _row(ts, 1), _full((1, D_MODEL)), _row(ts, D_MODEL), _full((D_MODEL, IN_EXT))],
        out_specs=[_row(ts, D_MODEL), _full((D_MODEL, IN_EXT)), _full((1, D_MODEL))],
        out_shape=[sd((s, D_MODEL), F32), sd((D_MODEL, IN_EXT), F32), sd((1, D_MODEL), F32)],
        compiler_params=_cp("arbitrary"),
    )(*parts, x, r1, anw, dx1, win)


def _local_step(x, positions, tgt, w, small):
    s = x.shape[0]
    t = _tiles(s)
    f = _forward(x, positions, w, small)
    pw, rc = f["pw"], f["rc"]
    cos_r, sin_r, cos_m, sin_m = f["tabs"]
    dx2, loss, g_fw = _loss_call(f["x2"], tgt, small["final_norm_w"], t["ts"])
    dgate, dval, g_wd, dcw_g, dcw_v, dcb_g, dcb_v = _ffn_bwd1_call(dx2, f["u"], w["conv_w"], small["conv_b"], pw["wdown"], t["tg"], t["tf"])
    du, dx1, g_fnw = _ffn_bwd2_call(dgate, dval, w["conv_w"], pw["wup"], f["x1"], f["r2"], small["ffn_norm_w"], dx2, t["t2"])
    g_wup = _dw_norm_call(f["x1"], f["r2"], small["ffn_norm_w"], du, t["ts"], F2 // 4, "dw_up")
    dy_ret, do, g_wout = _out_bwd_call(dx1, f["y_ret"], f["y_mla"], pw["wout"], t["ts"])
    drq, dg, do_ret, g_gnw = _ret_bwd_q_call(f["q"], f["k"], f["v"], f["o_ret"], f["g"], dy_ret, small["ret_gn_w"], rc, cos_r, sin_r, t["tr"])
    drk, drv = _ret_bwd_kv_call(f["q"], f["k"], f["v"], do_ret, rc, cos_r, sin_r, t["tr"])
    dmk, dmv, dmq = _flash_bwd_call(f["mqb"], f["mk"], f["mv"], do, t["tb"])
    dcq, dckv, dkpe, g_wq, g_wk, g_wv, g_qnw, g_kvnw = _mla_post_call(
        dmq, dmk, dmv, f["cq"], f["ckv"], small["mla_q_norm_w"], small["mla_kv_norm_w"], pw["wq"], pw["wk"], pw["wv"], cos_m, sin_m, t["ts"])
    gx, g_win_ext, g_anw = _in_bwd_call([drq, drk, drv, dg, dcq, dckv, dkpe], x, f["r1"], small["attn_norm_w"], dx1, pw["win"], t["ts"])
    lo = IN_W - ROPE
    g_win = jnp.concatenate([g_win_ext[:, :lo], g_win_ext[:, lo + KPE_LO:lo + KPE_LO + ROPE]], -1)
    g_wuq = g_wq.transpose(1, 0, 2)[:, :, :HEAD + ROPE].reshape(Q_RANK, N_HEADS * (HEAD + ROPE))
    g_wukv = jnp.concatenate([g_wk[:, :, :HEAD], g_wv[:, :, :HEAD]], -1).transpose(1, 0, 2).reshape(KV_RANK, 2 * MLA_W)
    gw = dict(w_in=g_win, w_uq=g_wuq, w_ukv=g_wukv, w_out=g_wout, w_up=g_wup,
              conv_w=jnp.concatenate([dcw_g, dcw_v], -1), w_down=g_wd)
    gs = dict(attn_norm_w=g_anw, ret_gn_w=g_gnw, mla_q_norm_w=g_qnw, mla_kv_norm_w=g_kvnw, ffn_norm_w=g_fnw,
              conv_b=jnp.concatenate([dcb_g, dcb_v], -1), final_norm_w=g_fw)
    return loss, gx, gw, gs


MESH_ID = pl.DeviceIdType.MESH
ANY = pl.BlockSpec(memory_space=pl.ANY)
VMEM_SPEC = pl.BlockSpec(memory_space=pltpu.VMEM)
N_DEV = 8
PACKED = (("w_in", (D_MODEL, IN_W // 4), 1), ("w_uq", (Q_RANK, 192), 1), ("w_ukv", (KV_RANK, 256), 1),
          ("w_out", (D_MODEL // 4, D_MODEL), 0), ("w_up", (D_MODEL, F2 // 4), 1), ("w_down", (D_FF // 4, D_MODEL), 0))
PACK_ROWS = 24576
HALF_ROWS = PACK_ROWS // 2
ADD_ROWS = 2048


def _mesh_pos():
    return lax.axis_index("x"), lax.axis_index("y"), lax.axis_index("c")


def _other_chips(x, y):
    return [(1 - x, y), (x, 1 - y), (1 - x, 1 - y)]


def _pack(parts, dtype):
    flat = jnp.concatenate([p.reshape(-1).astype(dtype) for p in parts])
    flat = jnp.concatenate([flat, jnp.zeros((PACK_ROWS * LANES - flat.shape[0],), dtype)])
    return flat.reshape(2, HALF_ROWS, LANES)


def _unpack(flat):
    out, off = [], 0
    for _, (r, c), _ in PACKED:
        out.append(flat[..., off:off + r * c].reshape(flat.shape[:-1] + (r, c)))
        off += r * c
    return out


def _all_gather_call(packed):
    _, h, _ = packed.shape

    def body(src_ref, out_ref, send_sems, recv_sems):
        x, y, c = _mesh_pos()
        sm = 2 * x + y
        chips = _other_chips(x, y)
        sib = (x, y, 1 - c)

        def rcopy(k, src, dst, dev):
            return pltpu.make_async_remote_copy(src_ref=src, dst_ref=dst, send_sem=send_sems.at[k], recv_sem=recv_sems.at[k],
                                                device_id=dev, device_id_type=MESH_ID)

        first = [rcopy(j, src_ref.at[c], out_ref.at[sm, c], (cx, cy, c)) for j, (cx, cy) in enumerate(chips)]
        own = rcopy(6, src_ref, out_ref.at[sm], sib)
        for cp in first + [own]:
            cp.start()
        passed = []
        for j, (cx, cy) in enumerate(chips):
            sj = 2 * cx + cy
            rcopy(j, src_ref.at[c], out_ref.at[sj, c], (cx, cy, c)).wait_recv()
            cp = rcopy(3 + j, out_ref.at[sj, c], out_ref.at[sj, c], sib)
            cp.start()
            passed.append(cp)
        for j, (cx, cy) in enumerate(chips):
            rcopy(3 + j, src_ref.at[c], out_ref.at[2 * cx + cy, 1 - c], sib).wait_recv()
        own.wait_recv()
        for cp in first + passed + [own]:
            cp.wait_send()

    return pl.pallas_call(
        body, name="weights_all_gather",
        in_specs=[ANY], out_specs=ANY,
        out_shape=jax.ShapeDtypeStruct((4, 2, h, LANES), packed.dtype),
        scratch_shapes=[pltpu.SemaphoreType.DMA((7,)), pltpu.SemaphoreType.DMA((7,))],
    )(packed)


def _rs_sibling_call(g):
    _, _, h, _ = g.shape

    def body(g_ref, buf_ref, send_sems, recv_sems):
        x, y, c = _mesh_pos()
        cps = [pltpu.make_async_remote_copy(src_ref=g_ref.at[s, 1 - c], dst_ref=buf_ref.at[s], send_sem=send_sems.at[s],
                                            recv_sem=recv_sems.at[s], device_id=(x, y, 1 - c), device_id_type=MESH_ID)
               for s in range(4)]
        for cp in cps:
            cp.start()
        for cp in cps:
            cp.wait()

    return pl.pallas_call(
        body, name="grads_rs_sibling",
        in_specs=[ANY], out_specs=ANY,
        out_shape=jax.ShapeDtypeStruct((4, h, LANES), g.dtype),
        scratch_shapes=[pltpu.SemaphoreType.DMA((4,)), pltpu.SemaphoreType.DMA((4,))],
    )(g)


def _rs_add1_call(g, buf, c):
    _, _, h, _ = g.shape

    def body(c_ref, g_ref, b_ref, p_ref, pb_ref):
        p = g_ref[...] + b_ref[...]
        p_ref[...] = p
        pb_ref[...] = p.astype(BF16)

    blk = pl.BlockSpec((None, ADD_ROWS, LANES), lambda s, i, c_ref: (s, i, 0))
    return pl.pallas_call(
        body, name="grads_rs_add_sibling",
        grid_spec=pltpu.PrefetchScalarGridSpec(
            num_scalar_prefetch=1, grid=(4, h // ADD_ROWS),
            in_specs=[pl.BlockSpec((None, None, ADD_ROWS, LANES), lambda s, i, c_ref: (s, c_ref[0], i, 0)), blk],
            out_specs=[blk, blk]),
        out_shape=[jax.ShapeDtypeStruct((4, h, LANES), F32), jax.ShapeDtypeStruct((4, h, LANES), BF16)],
        compiler_params=_cp("parallel", "parallel"),
    )(c, g, buf)


def _rs_chips_call(pb):
    _, h, _ = pb.shape

    def body(pb_ref, buf_ref, send_sems, recv_sems):
        x, y, c = _mesh_pos()
        cps = [pltpu.make_async_remote_copy(src_ref=pb_ref.at[2 * cx + cy], dst_ref=buf_ref.at[j], send_sem=send_sems.at[j],
                                            recv_sem=recv_sems.at[j], device_id=(cx, cy, c), device_id_type=MESH_ID)
               for j, (cx, cy) in enumerate(_other_chips(x, y))]
        for cp in cps:
            cp.start()
        for cp in cps:
            cp.wait()

    return pl.pallas_call(
        body, name="grads_rs_chips",
        in_specs=[ANY], out_specs=ANY,
        out_shape=jax.ShapeDtypeStruct((3, h, LANES), pb.dtype),
        scratch_shapes=[pltpu.SemaphoreType.DMA((3,)), pltpu.SemaphoreType.DMA((3,))],
    )(pb)


def _rs_add2_call(p, buf, sm):
    _, h, _ = p.shape

    def body(sm_ref, p_ref, b_ref, f_ref):
        f_ref[...] = ((p_ref[...] + b_ref[0].astype(F32)) + b_ref[1].astype(F32)) + b_ref[2].astype(F32)

    return pl.pallas_call(
        body, name="grads_rs_add_chips",
        grid_spec=pltpu.PrefetchScalarGridSpec(
            num_scalar_prefetch=1, grid=(h // ADD_ROWS,),
            in_specs=[pl.BlockSpec((None, ADD_ROWS, LANES), lambda i, sm_ref: (sm_ref[0], i, 0)),
                      pl.BlockSpec((3, ADD_ROWS, LANES), lambda i, sm_ref: (0, i, 0))],
            out_specs=pl.BlockSpec((ADD_ROWS, LANES), lambda i, sm_ref: (i, 0))),
        out_shape=jax.ShapeDtypeStruct((h, LANES), F32),
        compiler_params=_cp("parallel"),
    )(sm, p, buf)


def _rs_share_call(f):
    h, _ = f.shape

    def body(f_ref, out_ref, send_sem, recv_sem):
        x, y, c = _mesh_pos()
        cp = pltpu.make_async_remote_copy(src_ref=f_ref, dst_ref=out_ref, send_sem=send_sem, recv_sem=recv_sem,
                                          device_id=(x, y, 1 - c), device_id_type=MESH_ID)
        cp.start()
        cp.wait()

    return pl.pallas_call(
        body, name="grads_rs_share",
        in_specs=[ANY], out_specs=ANY,
        out_shape=jax.ShapeDtypeStruct((h, LANES), f.dtype),
        scratch_shapes=[pltpu.SemaphoreType.DMA, pltpu.SemaphoreType.DMA],
    )(f)


def _exchange8_call(vec, reduce, name):
    rows = vec.shape[0]

    def body(v_ref, out_ref, *rest):
        slots, send_sems, recv_sems = (rest if reduce else (out_ref,) + rest)
        x, y, c = _mesh_pos()
        me = 4 * x + 2 * y + c
        slots[me] = v_ref[...]

        def rcopy(k, to_me):
            bx, by, bc = (k >> 2) & 1, (k >> 1) & 1, k & 1
            px, py, pc = (1 - x if bx else x), (1 - y if by else y), (1 - c if bc else c)
            slot = 4 * px + 2 * py + pc if to_me else me
            return pltpu.make_async_remote_copy(src_ref=v_ref, dst_ref=slots.at[slot], send_sem=send_sems.at[k - 1],
                                                recv_sem=recv_sems.at[k - 1], device_id=(px, py, pc), device_id_type=MESH_ID)

        for k in range(1, N_DEV):
            rcopy(k, False).start()
        for k in range(1, N_DEV):
            rcopy(k, True).wait_recv()
        for k in range(1, N_DEV):
            rcopy(k, False).wait_send()
        if reduce:
            tot = slots[0]
            for d in range(1, N_DEV):
                tot = tot + slots[d]
            out_ref[...] = tot

    stack = jax.ShapeDtypeStruct((N_DEV, rows, LANES), F32)
    return pl.pallas_call(
        body, name=name,
        in_specs=[VMEM_SPEC], out_specs=VMEM_SPEC,
        out_shape=jax.ShapeDtypeStruct((rows, LANES), F32) if reduce else stack,
        scratch_shapes=([pltpu.VMEM((N_DEV, rows, LANES), F32)] if reduce else [])
        + [pltpu.SemaphoreType.DMA((N_DEV - 1,)), pltpu.SemaphoreType.DMA((N_DEV - 1,))],
    )(vec)


def _adamw_call(w, g, m, v, name):
    r, c = w.shape
    rb = r if r <= 256 else (256 if r % 256 == 0 else 352)
    assert r % rb == 0

    def body(w_ref, g_ref, m_ref, v_ref, d_ref, nm_ref, nv_ref):
        gv = g_ref[...]
        nm = ADAM_B1 * m_ref[...] + (1.0 - ADAM_B1) * gv
        nv = ADAM_B2 * v_ref[...] + (1.0 - ADAM_B2) * jnp.square(gv)
        m_hat = nm / (1.0 - ADAM_B1 ** ADAM_STEP)
        v_hat = nv / (1.0 - ADAM_B2 ** ADAM_STEP)
        d_ref[...] = -ADAM_LR * (m_hat / (jnp.sqrt(v_hat) + ADAM_EPS) + ADAM_WD * w_ref[...])
        nm_ref[...] = nm
        nv_ref[...] = nv

    spec = pl.BlockSpec((rb, c), lambda i: (i, 0))
    sd = jax.ShapeDtypeStruct((r, c), F32)
    return pl.pallas_call(
        body, name=name, grid=(r // rb,),
        in_specs=[spec] * 4, out_specs=[spec] * 3, out_shape=[sd, sd, sd],
        compiler_params=_cp("parallel"),
    )(w, g, m, v)


SMALL = (("attn_norm_w", D_MODEL), ("ret_gn_w", RET_W), ("mla_q_norm_w", Q_RANK), ("mla_kv_norm_w", KV_RANK),
         ("ffn_norm_w", D_MODEL), ("conv_b", F2), ("final_norm_w", D_MODEL))
WEIGHT_ORDER = ("attn_norm_w", "w_in", "ret_gn_w", "mla_q_norm_w", "w_uq", "mla_kv_norm_w", "w_ukv", "w_out",
                "ffn_norm_w", "w_up", "conv_w", "conv_b", "w_down", "final_norm_w")


def _pad_rows(flat, rows):
    return jnp.concatenate([flat, jnp.zeros((rows * LANES - flat.shape[0],), flat.dtype)]).reshape(rows, LANES)


def kernel(x, positions, attn_norm_w, w_in, ret_gn_w, mla_q_norm_w, w_uq, mla_kv_norm_w, w_ukv, w_out, ffn_norm_w, w_up, conv_w, conv_b, w_down, final_norm_w, loss_target, m_attn_norm_w, m_w_in, m_ret_gn_w, m_mla_q_norm_w, m_w_uq, m_mla_kv_norm_w, m_w_ukv, m_w_out, m_ffn_norm_w, m_w_up, m_conv_w, m_conv_b, m_w_down, m_final_norm_w, v_attn_norm_w, v_w_in, v_ret_gn_w, v_mla_q_norm_w, v_w_uq, v_mla_kv_norm_w, v_w_ukv, v_w_out, v_ffn_norm_w, v_w_up, v_conv_w, v_conv_b, v_w_down, v_final_norm_w):
    args = dict(locals())
    cx, cy, cc = _mesh_pos()
    sm = 2 * cx + cy

    gathered = _all_gather_call(_pack([args[n][0] for n, _, _ in PACKED], BF16))
    full = {}
    for (n, (r, c), axis), piece in zip(PACKED, _unpack(gathered.reshape(4, PACK_ROWS * LANES))):
        if n == "w_up":
            full[n] = piece
        else:
            full[n] = piece.transpose(1, 0, 2).reshape(r, 4 * c) if axis == 1 else piece.reshape(4 * r, c)
    cw_rows = 40
    cw_all = _exchange8_call(_pad_rows(conv_w[0].reshape(-1), cw_rows), False, "conv_w_all_gather")
    cw_all = cw_all[0::2].reshape(4, cw_rows * LANES)[:, :3 * F2 // 4].reshape(4, 3, F2 // 4)
    full["conv_w"] = cw_all.transpose(1, 0, 2).reshape(3, F2)
    small = {n: args[n].reshape(1, d) for n, d in SMALL}

    loss, gx, gw, gs = _local_step(x[0], positions[0], loss_target[0], full, small)

    shards = []
    for n, (r, c), axis in PACKED:
        g = gw[n]
        if axis == 1 and g.ndim == 2:
            g = g.reshape(r, 4, c).transpose(1, 0, 2)
        shards.append(g.reshape(4, r * c))
    gflat = jnp.concatenate(shards + [jnp.zeros((4, PACK_ROWS * LANES - sum(s.shape[1] for s in shards)), F32)], axis=1)
    gpk = gflat.reshape(4, 2, HALF_ROWS, LANES)
    p, pb = _rs_add1_call(gpk, _rs_sibling_call(gpk), cc.reshape(1).astype(jnp.int32))
    fin = _rs_add2_call(p, _rs_chips_call(pb), sm.reshape(1).astype(jnp.int32))
    sib = _rs_share_call(fin)
    both = jnp.where(cc == 0, jnp.stack([fin, sib]), jnp.stack([sib, fin]))
    red = dict(zip([n for n, _, _ in PACKED], _unpack(both.reshape(PACK_ROWS * LANES))))

    vec = jnp.concatenate([gs[n].reshape(-1) for n, _ in SMALL] + [gw["conv_w"].reshape(-1), loss.reshape(-1)])
    n_small = sum(d for _, d in SMALL)
    tot = _exchange8_call(_pad_rows(vec, 216), True, "small_all_reduce").reshape(-1)
    off = 0
    for n, d in SMALL:
        red[n] = tot[off:off + d].reshape(1, d)
        off += d
    red["conv_w"] = lax.dynamic_slice(tot[off:off + 3 * F2].reshape(3, F2), (0, sm * (F2 // 4)), (3, F2 // 4))
    loss_tot = tot[off + 3 * F2]

    grads, deltas, new_m, new_v = [], [], [], []
    for n in WEIGHT_ORDER:
        shape = args[n].shape
        two_d = (1, shape[0]) if len(shape) == 1 else shape[-2:]
        g = red[n].reshape(two_d)
        d, nm, nv = _adamw_call(args[n].reshape(two_d), g, args["m_" + n].reshape(two_d), args["v_" + n].reshape(two_d), "adamw_" + n)
        grads.append(g.reshape(shape))
        deltas.append(d.reshape(shape))
        new_m.append(nm.reshape(shape))
        new_v.append(nv.reshape(shape))
    return (loss_tot, gx[None], *grads, *deltas, *new_m, *new_v)
```

```python
import functools
import math

import numpy as np
import jax
import jax.numpy as jnp
from jax import lax
from jax.experimental import pallas as pl
from jax.experimental.pallas import tpu as pltpu

F32 = jnp.float32
BF16 = jnp.bfloat16

D_MODEL = 1024
N_HEADS = 8
HEAD = 64
RET_W = N_HEADS * HEAD
MLA_W = N_HEADS * HEAD
ROPE = 32
Q_RANK = 256
KV_RANK = 128
D_FF = 2816
F2 = 2 * D_FF
IN_W = 4 * RET_W + Q_RANK + KV_RANK + ROPE
IN_EXT = 4 * RET_W + Q_RANK + KV_RANK + 128
KPE_LO = 64
ROPE_BASE = 10000.0
EPS = 1e-6
RET_CHUNK = 128
SM_SCALE = (HEAD + ROPE) ** -0.5
LOG2E = math.log2(math.e)
LN2 = math.log(2.0)
NEG = -1e30
LANES = 128
VMEM_LIMIT = 56 * 1024 * 1024

ADAM_LR = 0.001
ADAM_B1 = 0.9
ADAM_B2 = 0.999
ADAM_EPS = 1e-08
ADAM_WD = 0.01
ADAM_STEP = 10


def _cp(*sem):
    return pltpu.CompilerParams(dimension_semantics=sem, vmem_limit_bytes=VMEM_LIMIT)


def _full(shape):
    n = len(shape)
    return pl.BlockSpec(tuple(shape), lambda *_: (0,) * n)


def _row(ts, c):
    return pl.BlockSpec((ts, c), lambda i: (i, 0))


def _hrow(h, ts, c):
    return pl.BlockSpec((h, ts, c), lambda i: (0, i, 0))


def _dot(a, b):
    return jnp.dot(a, b, preferred_element_type=F32)


def _dot_nt(a, b):
    return lax.dot_general(a, b, (((1,), (1,)), ((), ())), preferred_element_type=F32)


def _dot_tn(a, b):
    return lax.dot_general(a, b, (((0,), (0,)), ((), ())), preferred_element_type=F32)


def _dot_hi(a, b):
    hi = a.astype(BF16)
    lo = (a - hi.astype(F32)).astype(BF16)
    bb = b.astype(BF16)
    return _dot(hi, bb) + _dot(lo, bb)


def _rot_half(x, half):
    w = x.shape[-1]
    lane = lax.broadcasted_iota(jnp.int32, x.shape, x.ndim - 1)
    first = (lane % (2 * half)) < half
    return jnp.where(first, -pltpu.roll(x, w - half, x.ndim - 1), pltpu.roll(x, half, x.ndim - 1))


def _rope(x, cos, sin, half):
    return x * cos + _rot_half(x, half) * sin


def _unrope(dy, cos, sin, half):
    return dy * cos - _rot_half(dy, half) * sin


def _silu(g):
    return g / (1.0 + jnp.exp(-g))


def _rstd(x):
    return lax.rsqrt(jnp.mean(x * x, axis=-1, keepdims=True) + EPS)


def _rope_tables(positions):
    pos = positions.astype(F32)[:, None]
    s = pos.shape[0]
    inv = ROPE_BASE ** (-jnp.arange(0, HEAD, 2, dtype=F32) / HEAD)
    ang = pos * inv
    c, sn = jnp.cos(ang), jnp.sin(ang)
    cos_r = jnp.tile(jnp.concatenate([c, c], -1), (1, 2))
    sin_r = jnp.tile(jnp.concatenate([sn, sn], -1), (1, 2))
    inv = ROPE_BASE ** (-jnp.arange(0, ROPE, 2, dtype=F32) / ROPE)
    ang = pos * inv
    c, sn = jnp.cos(ang), jnp.sin(ang)
    one, zero = jnp.ones((s, KPE_LO), F32), jnp.zeros((s, KPE_LO), F32)
    cos_m = jnp.concatenate([one, c, c, one[:, :LANES - KPE_LO - ROPE]], -1)
    sin_m = jnp.concatenate([zero, sn, sn, zero[:, :LANES - KPE_LO - ROPE]], -1)
    return cos_r, sin_r, cos_m, sin_m


def _ret_consts():
    c = RET_CHUNK
    lg = np.log1p(-np.power(2.0, -5.0 - np.arange(N_HEADS, dtype=np.float64)))
    idx = np.arange(c, dtype=np.float64)
    diff = idx[:, None] - idx[None, :]
    lane_head = np.arange(LANES) // HEAD
    dmask = np.zeros((4, 2, c, c))
    zeta = np.zeros((4, c, LANES))
    xi = np.zeros((4, c, LANES))
    cd = np.zeros((4, LANES, LANES))
    bd = (lane_head[:, None] == lane_head[None, :]).astype(np.float64)
    for j in range(4):
        for hh in range(2):
            dmask[j, hh] = np.where(diff >= 0, np.exp(lg[2 * j + hh] * np.maximum(diff, 0.0)), 0.0)
        lgl = lg[2 * j + lane_head]
        zeta[j] = np.exp(lgl[None, :] * (c - 1.0 - idx[:, None]))
        xi[j] = np.exp(lgl[None, :] * (idx[:, None] + 1.0))
        cd[j] = np.exp(lgl * c)[:, None] * bd
    f = lambda a: jnp.asarray(a, F32)
    side = lambda d: np.concatenate([d[:, 0], d[:, 1]], axis=-1)
    return dict(dmask=f(side(dmask)), dmask_t=f(side(np.swapaxes(dmask, 2, 3))), zeta=f(zeta), xi=f(xi), cd=f(cd), bd=f(bd))


def _f1_call(x, anw, win, cos_r, sin_r, cos_m, sin_m, ts):
    s = x.shape[0]

    def body(x_ref, anw_ref, w_ref, cr_ref, sr_ref, cm_ref, sm_ref,
             q_ref, k_ref, v_ref, g_ref, cq_ref, ckv_ref, kpe_ref, r_ref):
        xv = x_ref[...]
        r = _rstd(xv)
        r_ref[...] = r
        h = (xv * r * anw_ref[...]).astype(BF16)
        cr, sr = cr_ref[...], sr_ref[...]
        qk = _dot(h, w_ref[:, 0:2 * RET_W])
        for j in range(4):
            sl = slice(j * LANES, (j + 1) * LANES)
            q_ref[:, sl] = _rope(qk[:, sl], cr, sr, HEAD // 2).astype(BF16)
            kk = qk[:, RET_W + j * LANES:RET_W + (j + 1) * LANES]
            k_ref[:, sl] = (_rope(kk, cr, sr, HEAD // 2) * (HEAD ** -0.5)).astype(BF16)
        v_ref[...] = _dot(h, w_ref[:, 2 * RET_W:3 * RET_W]).astype(BF16)
        g_ref[...] = _dot(h, w_ref[:, 3 * RET_W:4 * RET_W])
        o = 4 * RET_W
        cq_ref[...] = _dot(h, w_ref[:, o:o + Q_RANK])
        ckv_ref[...] = _dot(h, w_ref[:, o + Q_RANK:o + Q_RANK + KV_RANK])
        kp = _dot(h, w_ref[:, o + Q_RANK + KV_RANK:IN_EXT])
        kpe_ref[...] = _rope(kp, cm_ref[...], sm_ref[...], ROPE // 2)

    sd = jax.ShapeDtypeStruct
    return pl.pallas_call(
        body, name="f1_in_proj", grid=(s // ts,),
        in_specs=[_row(ts, D_MODEL), _full((1, D_MODEL)), _full((D_MODEL, IN_EXT)),
                  _row(ts, LANES), _row(ts, LANES), _row(ts, LANES), _row(ts, LANES)],
        out_specs=[_row(ts, RET_W), _row(ts, RET_W), _row(ts, RET_W), _row(ts, RET_W),
                   _row(ts, Q_RANK), _row(ts, KV_RANK), _row(ts, LANES), _row(ts, 1)],
        out_shape=[sd((s, RET_W), BF16), sd((s, RET_W), BF16), sd((s, RET_W), BF16), sd((s, RET_W), F32),
                   sd((s, Q_RANK), F32), sd((s, KV_RANK), F32), sd((s, LANES), F32), sd((s, 1), F32)],
        compiler_params=_cp("parallel"),
    )(x, anw, win, cos_r, sin_r, cos_m, sin_m)


def _stack_heads(a):
    lo = lax.broadcasted_iota(jnp.int32, a.shape, 1) < HEAD
    zero = jnp.zeros_like(a)
    return jnp.concatenate([jnp.where(lo, a, zero), jnp.where(lo, zero, a)], axis=0)


def _pair_product(a, b2, decay2, w2):
    return _dot((_dot_nt(a, b2) * decay2).astype(BF16), w2)


def _ret_fwd_call(q, k, v, g, gnw, rc, tr):
    s = q.shape[0]
    c = RET_CHUNK
    nc = tr // c

    def body(q_ref, k_ref, v_ref, g_ref, gnw_ref, dm_ref, zeta_ref, xi_ref, cd_ref, bd_ref, o_ref, y_ref, st_ref):
        @pl.when(pl.program_id(1) == 0)
        def _():
            st_ref[...] = jnp.zeros_like(st_ref)

        lane = lax.broadcasted_iota(jnp.int32, (c, LANES), 1)
        bd = bd_ref[...]
        chunks = [slice(ci * c, (ci + 1) * c) for ci in range(nc)]
        contrib = [_dot_tn((k_ref[rows, :].astype(F32) * zeta_ref[0]).astype(BF16), v_ref[rows, :]) * bd for rows in chunks]
        st, states = st_ref[...], []
        for ci in range(nc):
            states.append(st.astype(BF16))
            st = st * cd_ref[0] + contrib[ci]
        st_ref[...] = st
        for ci, rows in enumerate(chunks):
            qc = q_ref[rows, :]
            o_ref[rows, :] = (_dot(qc, states[ci]) * xi_ref[0]
                              + _pair_product(qc, _stack_heads(k_ref[rows, :]), dm_ref[0], _stack_heads(v_ref[rows, :])))
        o = o_ref[...]
        avg = bd * (1.0 / HEAD)
        ctr = o - _dot_hi(o, avg)
        var = _dot_hi(ctr * ctr, avg)
        y_ref[...] = (_silu(g_ref[...]) * (ctr * lax.rsqrt(var + EPS) * gnw_ref[...])).astype(BF16)

    slab = pl.BlockSpec((tr, LANES), lambda j, i: (i, j))
    sd = jax.ShapeDtypeStruct
    return pl.pallas_call(
        body, name="ret_fwd", grid=(4, s // tr),
        in_specs=[slab, slab, slab, slab, pl.BlockSpec((1, LANES), lambda j, i: (0, j)),
                  pl.BlockSpec((1, c, 2 * c), lambda j, i: (j, 0, 0)),
                  pl.BlockSpec((1, c, LANES), lambda j, i: (j, 0, 0)),
                  pl.BlockSpec((1, c, LANES), lambda j, i: (j, 0, 0)),
                  pl.BlockSpec((1, LANES, LANES), lambda j, i: (j, 0, 0)),
                  pl.BlockSpec((LANES, LANES), lambda j, i: (0, 0))],
        out_specs=[slab, slab],
        out_shape=[sd((s, RET_W), F32), sd((s, RET_W), BF16)],
        scratch_shapes=[pltpu.VMEM((LANES, LANES), F32)],
        compiler_params=_cp("parallel", "arbitrary"),
    )(q, k, v, g, gnw, rc["dmask"], rc["zeta"], rc["xi"], rc["cd"], rc["bd"])


QK_AUX = HEAD + ROPE
V_AUX = HEAD


def _lane_pair(shape, lo, a, b, rest):
    lane = lax.broadcasted_iota(jnp.int32, shape, len(shape) - 1)
    return jnp.where(lane == lo, a, jnp.where(lane == lo + 1, b, rest))


def _hi_lo(v):
    hi = v.astype(BF16).astype(F32)
    return hi, v - hi


def _mla_pre_call(cq, ckv, kpe, qnw, kvnw, wq, wk, wv, cos_m, sin_m, ts):
    s = cq.shape[0]

    def body(cq_ref, ckv_ref, kpe_ref, qnw_ref, kvnw_ref, wq_ref, wk_ref, wv_ref, cm_ref, sm_ref, q_ref, k_ref, v_ref):
        cqv, ckvv = cq_ref[...], ckv_ref[...]
        cqn = (cqv * _rstd(cqv) * qnw_ref[...]).astype(BF16)
        ckvn = (ckvv * _rstd(ckvv) * kvnw_ref[...]).astype(BF16)
        cm, sm = cm_ref[...], sm_ref[...]
        kp = _lane_pair((ts, LANES), QK_AUX, -1.0, -1.0, kpe_ref[...])
        for h in range(N_HEADS):
            qh = _rope(_dot(cqn, wq_ref[h]), cm, sm, ROPE // 2)
            q_ref[h] = (qh * (SM_SCALE * LOG2E)).astype(BF16)
            k_ref[h] = (_dot(ckvn, wk_ref[h]) + kp).astype(BF16)
            v_ref[h] = _lane_pair((ts, LANES), V_AUX, 1.0, 1.0, _dot(ckvn, wv_ref[h])).astype(BF16)

    sd = jax.ShapeDtypeStruct
    hm = sd((N_HEADS, s, LANES), BF16)
    return pl.pallas_call(
        body, name="mla_pre", grid=(s // ts,),
        in_specs=[_row(ts, Q_RANK), _row(ts, KV_RANK), _row(ts, LANES), _full((1, Q_RANK)), _full((1, KV_RANK)),
                  _full((N_HEADS, Q_RANK, LANES)), _full((N_HEADS, KV_RANK, LANES)), _full((N_HEADS, KV_RANK, LANES)),
                  _row(ts, LANES), _row(ts, LANES)],
        out_specs=[_hrow(N_HEADS, ts, LANES)] * 3,
        out_shape=[hm, hm, hm],
        compiler_params=_cp("parallel"),
    )(cq, ckv, kpe, qnw, kvnw, wq, wk, wv, cos_m, sin_m)


def _flash_fwd_call(q, k, v, tb):
    s = q.shape[1]
    nb = s // tb

    def body(q_ref, k_ref, v_ref, o_ref, qb_ref, m_ref, acc_ref):
        qi, ki = pl.program_id(0), pl.program_id(1)

        @pl.when(ki == 0)
        def _():
            m_ref[...] = jnp.full_like(m_ref, NEG)
            acc_ref[...] = jnp.zeros_like(acc_ref)

        def step(masked):
            if masked:
                keep = lax.broadcasted_iota(jnp.int32, (tb, tb), 1) <= lax.broadcasted_iota(jnp.int32, (tb, tb), 0)
            def finish(h, pe, alpha):
                acc_ref[h] = acc_ref[h] * alpha + _dot(pe, v_ref[h])

            nxt, pending = _dot_nt(q_ref[0], k_ref[0]), None
            for h in range(N_HEADS):
                sc = nxt
                if h + 1 < N_HEADS:
                    nxt = _dot_nt(q_ref[h + 1], k_ref[h + 1])
                if masked:
                    sc = jnp.where(keep, sc, NEG)
                m_prev = m_ref[h]
                m_new = jnp.maximum(m_prev, jnp.max(sc, axis=1, keepdims=True))
                pe = jnp.exp2(sc - jnp.tile(m_new, (1, tb // LANES))).astype(BF16)
                m_ref[h] = m_new
                if pending is not None:
                    finish(*pending)
                pending = (h, pe, jnp.exp2(m_prev - m_new))
            finish(*pending)

        @pl.when(ki < qi)
        def _():
            step(False)

        @pl.when(ki == qi)
        def _():
            step(True)
            lane = lax.broadcasted_iota(jnp.int32, (tb, LANES), 1)
            for p in range(N_HEADS // 2):
                outs = []
                for h in (2 * p, 2 * p + 1):
                    acc = acc_ref[h]
                    l = acc[:, V_AUX:V_AUX + 1]
                    outs.append(acc * (1.0 / l))
                    hi, lo = _hi_lo(m_ref[h][:, 0:1] + jnp.log(l) * LOG2E)
                    qb_ref[h] = _lane_pair((tb, LANES), QK_AUX, hi, lo, q_ref[h].astype(F32)).astype(BF16)
                o_ref[:, p * LANES:(p + 1) * LANES] = jnp.where(lane < HEAD, outs[0], pltpu.roll(outs[1], HEAD, 1)).astype(BF16)

    sd = jax.ShapeDtypeStruct
    qspec = pl.BlockSpec((N_HEADS, tb, LANES), lambda qi, ki: (0, qi, 0))
    kspec = pl.BlockSpec((N_HEADS, tb, LANES), lambda qi, ki: (0, jnp.minimum(ki, qi), 0))
    return pl.pallas_call(
        body, name="mla_flash_fwd", grid=(nb, nb),
        in_specs=[qspec, kspec, kspec],
        out_specs=[pl.BlockSpec((tb, MLA_W), lambda qi, ki: (qi, 0)), qspec],
        out_shape=[sd((s, MLA_W), BF16), sd((N_HEADS, s, LANES), BF16)],
        scratch_shapes=[pltpu.VMEM((N_HEADS, tb, LANES), F32), pltpu.VMEM((N_HEADS, tb, LANES), F32)],
        compiler_params=_cp("parallel", "arbitrary"),
    )(q, k, v)


def _out_proj_call(x, yret, ymla, wout, ts):
    s = x.shape[0]

    def body(x_ref, yr_ref, ym_ref, w_ref, x1_ref, r_ref):
        x1 = x_ref[...] + _dot(yr_ref[...], w_ref[0:RET_W, :]) + _dot(ym_ref[...], w_ref[RET_W:, :])
        x1_ref[...] = x1
        r_ref[...] = _rstd(x1)

    sd = jax.ShapeDtypeStruct
    return pl.pallas_call(
        body, name="out_proj", grid=(s // ts,),
        in_specs=[_row(ts, D_MODEL), _row(ts, RET_W), _row(ts, MLA_W), _full((D_MODEL, D_MODEL))],
        out_specs=[_row(ts, D_MODEL), _row(ts, 1)],
        out_shape=[sd((s, D_MODEL), F32), sd((s, 1), F32)],
        compiler_params=_cp("parallel"),
    )(x, yret, ymla, wout)


W_UP_SHARD = F2 // 4


def _ffn_fwd_call(x1, r2, fnw, wup4, cw, cb, wdown, ts):
    s = x1.shape[0]
    wsh = W_UP_SHARD

    def body(x_ref, r_ref, fnw_ref, wup_ref, cw_ref, cb_ref, wd_ref, u_ref, x2_ref, carry_ref):
        _zero_first(pl.program_id(0) == 0, carry_ref)
        xv = x_ref[...]
        h = (xv * r_ref[...] * fnw_ref[...]).astype(BF16)
        conv = []
        for j in range(4):
            cols = slice(j * wsh, (j + 1) * wsh)
            ub = _dot(h, wup_ref[j]).astype(BF16)
            u_ref[:, cols] = ub
            u = ub.astype(F32)
            u1, u2 = _shifted(u, carry_ref[:, cols])
            w = cw_ref[:, cols]
            conv.append(cb_ref[:, cols] + w[0:1, :] * u2 + w[1:2, :] * u1 + w[2:3, :] * u)
            carry_ref[:, cols] = u[ts - 8:, :]
        acc = xv
        for j in range(2):
            a = (_silu(conv[j]) * conv[j + 2]).astype(BF16)
            acc = acc + _dot(a, wd_ref[j * wsh:(j + 1) * wsh, :])
        x2_ref[...] = acc

    sd = jax.ShapeDtypeStruct
    return pl.pallas_call(
        body, name="ffn_fwd", grid=(s // ts,),
        in_specs=[_row(ts, D_MODEL), _row(ts, 1), _full((1, D_MODEL)), _full((4, D_MODEL, wsh)),
                  _full((3, F2)), _full((1, F2)), _full((D_FF, D_MODEL))],
        out_specs=[_row(ts, F2), _row(ts, D_MODEL)],
        out_shape=[sd((s, F2), BF16), sd((s, D_MODEL), F32)],
        scratch_shapes=[pltpu.VMEM((8, F2), F32)],
        compiler_params=_cp("arbitrary"),
    )(x1, r2, fnw, wup4, cw, cb, wdown)


def _shifted(u, hal):
    row = lax.broadcasted_iota(jnp.int32, u.shape, 0)
    u1 = jnp.where(row == 0, hal[7:8, :], pltpu.roll(u, 1, 0))
    u2 = jnp.where(row == 0, hal[6:7, :], jnp.where(row == 1, hal[7:8, :], pltpu.roll(u, 2, 0)))
    return u1, u2


def _prep_weights(w):
    win = w["w_in"]
    pad = lambda n: jnp.zeros((D_MODEL, n), win.dtype)
    win_ext = jnp.concatenate([win[:, :IN_W - ROPE], pad(KPE_LO), win[:, IN_W - ROPE:], pad(LANES - KPE_LO - ROPE)], -1)
    wuq = w["w_uq"].reshape(Q_RANK, N_HEADS, HEAD + ROPE)
    wq = jnp.concatenate([wuq, jnp.zeros((Q_RANK, N_HEADS, LANES - HEAD - ROPE), wuq.dtype)], -1).transpose(1, 0, 2)
    wukv = w["w_ukv"].reshape(KV_RANK, N_HEADS, 2 * HEAD)
    zk = jnp.zeros((KV_RANK, N_HEADS, HEAD), wukv.dtype)
    wk = jnp.concatenate([wukv[:, :, :HEAD], zk], -1).transpose(1, 0, 2)
    wv = jnp.concatenate([wukv[:, :, HEAD:], zk], -1).transpose(1, 0, 2)
    c = lambda a: a.astype(BF16)
    return dict(win=c(win_ext), wq=c(wq), wk=c(wk), wv=c(wv), wout=c(w["w_out"]))


def _prep_mlp_weights(w):
    wup = w["w_up"]
    if wup.ndim == 2:
        wup = wup.reshape(D_MODEL, 4, W_UP_SHARD).transpose(1, 0, 2)
    return dict(wup=wup.astype(BF16), wdown=w["w_down"].astype(BF16))


def _tiles(s):
    return dict(ts=min(s, 512), tr=min(s, 1024), tb=min(s, 512), tg=min(s, 512), tf=D_FF // 2, t2=min(s, 256))


class _Exchanges:
    def __init__(self, w):
        self.w = w

    def mlp_weights(self, after):
        return self.w

    def mlp_grads(self, gw):
        pass

    def behind_out_bwd(self, after):
        pass

    def behind_attention(self, after):
        pass


def _forward(x, positions, w, small, ex):
    s = x.shape[0]
    t = _tiles(s)
    pw = _prep_weights(w)
    cos_r, sin_r, cos_m, sin_m = _rope_tables(positions)
    rc = _ret_consts()
    q, k, v, g, cq, ckv, kpe, r1 = _f1_call(x, small["attn_norm_w"], pw["win"], cos_r, sin_r, cos_m, sin_m, t["ts"])
    o_ret, y_ret = _ret_fwd_call(q, k, v, g, small["ret_gn_w"], rc, t["tr"])
    mq, mk, mv = _mla_pre_call(cq, ckv, kpe, small["mla_q_norm_w"], small["mla_kv_norm_w"],
                               pw["wq"], pw["wk"], pw["wv"], cos_m, sin_m, t["ts"])
    y_mla, mqb = _flash_fwd_call(mq, mk, mv, t["tb"])
    x1, r2 = _out_proj_call(x, y_ret, y_mla, pw["wout"], t["ts"])
    pw.update(_prep_mlp_weights(ex.mlp_weights(r2)))
    u, x2 = _ffn_fwd_call(x1, r2, small["ffn_norm_w"], pw["wup"], w["conv_w"], small["conv_b"], pw["wdown"], t["t2"])
    return dict(pw=pw, tabs=(cos_r, sin_r, cos_m, sin_m), rc=rc, q=q, k=k, v=v, g=g, cq=cq, ckv=ckv, kpe=kpe, r1=r1,
                o_ret=o_ret, y_ret=y_ret, mqb=mqb, mk=mk, mv=mv, y_mla=y_mla, x1=x1, r2=r2, u=u, x2=x2)


def _norm_bwd(dh, xh, r, nw):
    dxn = dh * nw
    return r * (dxn - xh * jnp.mean(dxn * xh, axis=-1, keepdims=True))


def _zero_first(first, *refs):
    @pl.when(first)
    def _():
        for ref in refs:
            ref[...] = jnp.zeros_like(ref)


def _colsum(v):
    return jnp.sum(v, axis=0, keepdims=True)


def _dsilu(g, sg):
    return sg * (1.0 + g * (1.0 - sg))


def _loss_call(x2, tgt, fw, ts):
    s = x2.shape[0]

    def body(x_ref, t_ref, fw_ref, dx_ref, loss_ref, gfw_ref):
        _zero_first(pl.program_id(0) == 0, loss_ref, gfw_ref)
        xv = x_ref[...]
        r = _rstd(xv)
        xh = xv * r
        fwv = fw_ref[...]
        e = xh * fwv - t_ref[...]
        loss_ref[...] += (0.5 / D_MODEL) * _colsum(jnp.sum(e * e, axis=1, keepdims=True))
        dy = e * (1.0 / D_MODEL)
        gfw_ref[...] += _colsum(dy * xh)
        dx_ref[...] = _norm_bwd(dy, xh, r, fwv)

    sd = jax.ShapeDtypeStruct
    return pl.pallas_call(
        body, name="loss_bwd", grid=(s // ts,),
        in_specs=[_row(ts, D_MODEL), _row(ts, D_MODEL), _full((1, D_MODEL))],
        out_specs=[_row(ts, D_MODEL), _full((1, 1)), _full((1, D_MODEL))],
        out_shape=[sd((s, D_MODEL), F32), sd((1, 1), F32), sd((1, D_MODEL), F32)],
        compiler_params=_cp("arbitrary"),
    )(x2, tgt, fw)


def _ffn_bwd1_call(dx2, u, cw, cb, wdown, ts, tf):
    s = dx2.shape[0]
    nf = D_FF // tf

    def body(dx_ref, wd_ref, ug_ref, hg_ref, uv_ref, hv_ref, wg_ref, wv_ref, bg_ref, bv_ref,
             dgate_ref, dval_ref, dwd_ref, dcwg_ref, dcwv_ref, dcbg_ref, dcbv_ref):
        first = pl.program_id(1) == 0
        _zero_first(first, dwd_ref, dcwg_ref, dcwv_ref, dcbg_ref, dcbv_ref)
        gate, g1, g2, g0 = _conv_tile(ug_ref, hg_ref, wg_ref, bg_ref, first)
        val, v1, v2, v0 = _conv_tile(uv_ref, hv_ref, wv_ref, bv_ref, first)
        dxb = dx_ref[...].astype(BF16)
        da = _dot_nt(dxb, wd_ref[...])
        sg = 1.0 / (1.0 + jnp.exp(-gate))
        sl = gate * sg
        dgate = da * val * _dsilu(gate, sg)
        dval = da * sl
        dgate_ref[...] = dgate.astype(BF16)
        dval_ref[...] = dval.astype(BF16)
        dwd_ref[...] += _dot_tn((sl * val).astype(BF16), dxb)
        for ref, d, taps in ((dcwg_ref, dgate, (g2, g1, g0)), (dcwv_ref, dval, (v2, v1, v0))):
            for t in range(3):
                ref[t:t + 1, :] += _colsum(d * taps[t])
        dcbg_ref[...] += _colsum(dgate)
        dcbv_ref[...] += _colsum(dval)

    sd = jax.ShapeDtypeStruct
    colacc = lambda r: pl.BlockSpec((r, tf), lambda j, i: (0, j))
    return pl.pallas_call(
        body, name="ffn_bwd_gate", grid=(nf, s // ts),
        in_specs=[pl.BlockSpec((ts, D_MODEL), lambda j, i: (i, 0)), pl.BlockSpec((tf, D_MODEL), lambda j, i: (j, 0))]
        + _gate_specs(ts, tf, rows_inner=True),
        out_specs=[pl.BlockSpec((ts, tf), lambda j, i: (i, j)), pl.BlockSpec((ts, tf), lambda j, i: (i, j)),
                   pl.BlockSpec((tf, D_MODEL), lambda j, i: (j, 0)), colacc(3), colacc(3), colacc(1), colacc(1)],
        out_shape=[sd((s, D_FF), BF16), sd((s, D_FF), BF16), sd((D_FF, D_MODEL), F32),
                   sd((3, D_FF), F32), sd((3, D_FF), F32), sd((1, D_FF), F32), sd((1, D_FF), F32)],
        compiler_params=_cp("parallel", "arbitrary"),
    )(dx2, wdown, u, u, u, u, cw, cw, cb, cb)


def _ffn_bwd_call(dx2, u, cw, cb, wdown, wup4, x1, r2, fnw, ts):
    s = dx2.shape[0]
    nt = s // ts
    hb = ts // 8
    wsh = W_UP_SHARD
    rev = lambda i: nt - 1 - i

    def body(dx2_ref, u_ref, h_ref, cw_ref, cb_ref, wd_ref, wup_ref, x_ref, r_ref, fnw_ref,
             du_ref, dx1_ref, dcw_ref, dcb_ref, dfnw_ref, dwd_hbm, carry_ref, dwd_ref, sem):
        i = pl.program_id(0)
        _zero_first(i == 0, carry_ref, dwd_ref, dcw_ref, dcb_ref, dfnw_ref)
        seq_start = i == nt - 1
        dxb = dx2_ref[...].astype(BF16)
        dh = jnp.zeros((ts, D_MODEL), F32)

        def conv(cols):
            uv = u_ref[:, cols].astype(F32)
            u1, u2 = _shifted(uv, jnp.where(seq_start, 0.0, h_ref[:, cols].astype(F32)))
            w = cw_ref[:, cols]
            return cb_ref[:, cols] + w[0:1, :] * u2 + w[1:2, :] * u1 + w[2:3, :] * uv, (u2, u1, uv)

        for j in range(2):
            gcols = slice(j * wsh, (j + 1) * wsh)
            vcols = slice(D_FF + j * wsh, D_FF + (j + 1) * wsh)
            gate, gtaps = conv(gcols)
            val, vtaps = conv(vcols)
            da = _dot_nt(dxb, wd_ref[gcols, :])
            sg = 1.0 / (1.0 + jnp.exp(-gate))
            sl = gate * sg
            dwd_ref[gcols, :] += _dot_tn((sl * val).astype(BF16), dxb)
            for d, cols, taps, shard in ((da * val * _dsilu(gate, sg), gcols, gtaps, j), (da * sl, vcols, vtaps, 2 + j)):
                for t in range(3):
                    dcw_ref[t:t + 1, cols] += _colsum(d * taps[t])
                dcb_ref[:, cols] += _colsum(d)
                d1, d2 = _shifted_up(d, carry_ref[:, cols])
                w = cw_ref[:, cols]
                du = (w[2:3, :] * d + w[1:2, :] * d1 + w[0:1, :] * d2).astype(BF16)
                du_ref[:, cols] = du
                dh = dh + _dot_nt(du, wup_ref[shard])
                carry_ref[:, cols] = d[0:8, :]
        r = r_ref[...]
        xh = x_ref[...] * r
        dfnw_ref[...] += _colsum(dh * xh)
        dx1_ref[...] = dx2_ref[...] + _norm_bwd(dh, xh, r, fnw_ref[...])

        @pl.when(i == nt - 1)
        def _():
            cp = pltpu.make_async_copy(dwd_ref, dwd_hbm, sem)
            cp.start()
            cp.wait()

    sd = jax.ShapeDtypeStruct
    row = lambda c: pl.BlockSpec((ts, c), lambda i: (rev(i), 0))
    once = lambda shape: pl.BlockSpec(shape, lambda i: (0,) * len(shape), pipeline_mode=pl.Buffered(1))
    return pl.pallas_call(
        body, name="ffn_bwd", grid=(nt,),
        in_specs=[row(D_MODEL), row(F2), pl.BlockSpec((8, F2), lambda i: (jnp.maximum(rev(i) * hb - 1, 0), 0)),
                  once((3, F2)), once((1, F2)), once((D_FF, D_MODEL)), once((4, D_MODEL, wsh)),
                  row(D_MODEL), row(1), once((1, D_MODEL))],
        out_specs=[row(F2), row(D_MODEL), _full((3, F2)), _full((1, F2)), _full((1, D_MODEL)), pl.BlockSpec(memory_space=pl.ANY)],
        out_shape=[sd((s, F2), BF16), sd((s, D_MODEL), F32), sd((3, F2), F32), sd((1, F2), F32), sd((1, D_MODEL), F32),
                   sd((D_FF, D_MODEL), F32)],
        scratch_shapes=[pltpu.VMEM((8, F2), F32), pltpu.VMEM((D_FF, D_MODEL), F32), pltpu.SemaphoreType.DMA],
        compiler_params=_cp("arbitrary"),
    )(dx2, u, u, cw, cb, wdown, wup4, x1, r2, fnw)


def _shifted_up(d, hal):
    n = d.shape[0]
    row = lax.broadcasted_iota(jnp.int32, d.shape, 0)
    d1 = jnp.where(row == n - 1, hal[0:1, :], pltpu.roll(d, n - 1, 0))
    d2 = jnp.where(row == n - 2, hal[0:1, :], jnp.where(row == n - 1, hal[1:2, :], pltpu.roll(d, n - 2, 0)))
    return d1, d2


def _ffn_bwd2_call(dgate, dval, cw, wup, x1, r2, fnw, dx2, ts):
    s = dx2.shape[0]
    nt = s // ts
    hb = ts // 8
    nxt = pl.BlockSpec((8, D_FF), lambda i: (jnp.minimum((i + 1) * hb, s // 8 - 1), 0))

    def body(dg_ref, hg_ref, dv_ref, hv_ref, cw_ref, wup_ref, x_ref, r_ref, fnw_ref, dx2_ref, du_ref, dx1_ref, dfnw_ref):
        i = pl.program_id(0)
        _zero_first(i == 0, dfnw_ref)
        dh = jnp.zeros((ts, D_MODEL), F32)
        for part, (d_ref, h_ref) in enumerate(((dg_ref, hg_ref), (dv_ref, hv_ref))):
            off = part * D_FF
            d = d_ref[...].astype(F32)
            hal = jnp.where(i == nt - 1, 0.0, h_ref[...].astype(F32))
            d1, d2 = _shifted_up(d, hal)
            w = cw_ref[:, off:off + D_FF]
            du = (w[2:3, :] * d + w[1:2, :] * d1 + w[0:1, :] * d2).astype(BF16)
            du_ref[:, off:off + D_FF] = du
            for j in range(2):
                dh = dh + _dot_nt(du[:, j * W_UP_SHARD:(j + 1) * W_UP_SHARD], wup_ref[2 * part + j])
        r = r_ref[...]
        xh = x_ref[...] * r
        dfnw_ref[...] += _colsum(dh * xh)
        dx1_ref[...] = dx2_ref[...] + _norm_bwd(dh, xh, r, fnw_ref[...])

    sd = jax.ShapeDtypeStruct
    return pl.pallas_call(
        body, name="ffn_bwd_up", grid=(nt,),
        in_specs=[_row(ts, D_FF), nxt, _row(ts, D_FF), nxt, _full((3, F2)), _full((4, D_MODEL, W_UP_SHARD)),
                  _row(ts, D_MODEL), _row(ts, 1), _full((1, D_MODEL)), _row(ts, D_MODEL)],
        out_specs=[_row(ts, F2), _row(ts, D_MODEL), _full((1, D_MODEL))],
        out_shape=[sd((s, F2), BF16), sd((s, D_MODEL), F32), sd((1, D_MODEL), F32)],
        compiler_params=_cp("arbitrary"),
    )(dgate, dgate, dval, dval, cw, wup, x1, r2, fnw, dx2)


def _dw_norm_call(x, r, nw, b, ts, tn, name):
    s, n = b.shape
    k = x.shape[1]

    def body(x_ref, r_ref, nw_ref, b_ref, dw_ref):
        _zero_first(pl.program_id(1) == 0, dw_ref)
        h = (x_ref[...] * r_ref[...] * nw_ref[...]).astype(BF16)
        dw_ref[...] += _dot_tn(h, b_ref[...])

    return pl.pallas_call(
        body, name=name, grid=(n // tn, s // ts),
        in_specs=[pl.BlockSpec((ts, k), lambda j, i: (i, 0)), pl.BlockSpec((ts, 1), lambda j, i: (i, 0)),
                  pl.BlockSpec((1, k), lambda j, i: (0, 0)), pl.BlockSpec((ts, tn), lambda j, i: (i, j))],
        out_specs=pl.BlockSpec((None, k, tn), lambda j, i: (j, 0, 0)),
        out_shape=jax.ShapeDtypeStruct((n // tn, k, tn), F32),
        compiler_params=_cp("parallel", "arbitrary"),
    )(x, r, nw, b)


def _out_bwd_call(dx1, yret, ymla, wout, ts):
    s = dx1.shape[0]

    def body(dx_ref, yr_ref, ym_ref, w_ref, dyr_ref, do_ref, dwo_ref):
        _zero_first(pl.program_id(0) == 0, dwo_ref)
        dxb = dx_ref[...].astype(BF16)
        dmix = _dot_nt(dxb, w_ref[...])
        dyr_ref[...] = dmix[:, :RET_W]
        ym = ym_ref[...]
        lane = lax.broadcasted_iota(jnp.int32, (ts, LANES), 1)
        for p in range(N_HEADS // 2):
            dom = dmix[:, RET_W + p * LANES:RET_W + (p + 1) * LANES]
            prod = dom * ym[:, p * LANES:(p + 1) * LANES].astype(F32)
            for hh in range(2):
                mine = (lane >= HEAD) if hh else (lane < HEAD)
                hi, lo = _hi_lo(jnp.sum(jnp.where(mine, prod, 0.0), axis=1, keepdims=True))
                base = jnp.where(lane < HEAD, pltpu.roll(dom, HEAD, 1) if hh else dom, 0.0)
                do_ref[2 * p + hh] = _lane_pair((ts, LANES), V_AUX, -hi, -lo, base).astype(BF16)
        dwo_ref[0:RET_W, :] += _dot_tn(yr_ref[...], dxb)
        dwo_ref[RET_W:, :] += _dot_tn(ym, dxb)

    sd = jax.ShapeDtypeStruct
    return pl.pallas_call(
        body, name="out_proj_bwd", grid=(s // ts,),
        in_specs=[_row(ts, D_MODEL), _row(ts, RET_W), _row(ts, MLA_W), _full((D_MODEL, D_MODEL))],
        out_specs=[_row(ts, RET_W), _hrow(N_HEADS, ts, LANES), _full((D_MODEL, D_MODEL))],
        out_shape=[sd((s, RET_W), F32), sd((N_HEADS, s, LANES), BF16), sd((D_MODEL, D_MODEL), F32)],
        compiler_params=_cp("arbitrary"),
    )(dx1, yret, ymla, wout)


def _ret_bwd_q_call(q, k, v, o, g, dy, gnw, rc, cos_r, sin_r, tr):
    s = q.shape[0]
    c = RET_CHUNK
    nc = tr // c

    def body(q_ref, k_ref, v_ref, o_ref, g_ref, dy_ref, gnw_ref, dm_ref, zeta_ref, xi_ref, cd_ref, bd_ref, cr_ref, sr_ref,
             dq_ref, dg_ref, do_ref, dgnw_ref, st_ref):
        _zero_first(pl.program_id(1) == 0, st_ref, dgnw_ref)
        bd = bd_ref[...]
        avg = bd * (1.0 / HEAD)
        ov = o_ref[...]
        ctr = ov - _dot_hi(ov, avg)
        rs = lax.rsqrt(_dot_hi(ctr * ctr, avg) + EPS)
        oh = ctr * rs
        gg, dyv, gnw_v = g_ref[...], dy_ref[...], gnw_ref[...]
        sg = 1.0 / (1.0 + jnp.exp(-gg))
        sl = gg * sg
        dg_ref[...] = (dyv * oh * gnw_v * _dsilu(gg, sg)).astype(BF16)
        dgnw_ref[...] += _colsum(dyv * sl * oh)
        doh = dyv * sl * gnw_v
        dov = (rs * (doh - _dot_hi(doh, avg) - oh * _dot_hi(doh * oh, avg))).astype(BF16)
        do_ref[...] = dov
        chunks = [slice(ci * c, (ci + 1) * c) for ci in range(nc)]
        contrib = [_dot_tn((k_ref[rows, :].astype(F32) * zeta_ref[0]).astype(BF16), v_ref[rows, :]) * bd for rows in chunks]
        st, states = st_ref[...], []
        for ci in range(nc):
            states.append(st.astype(BF16))
            st = st * cd_ref[0] + contrib[ci]
        st_ref[...] = st
        for ci, rows in enumerate(chunks):
            doc = dov[rows, :]
            dq = (_dot_nt(doc, states[ci]) * xi_ref[0]
                  + _pair_product(doc, _stack_heads(v_ref[rows, :]), dm_ref[0], _stack_heads(k_ref[rows, :])))
            dq_ref[rows, :] = _unrope(dq, cr_ref[rows, :], sr_ref[rows, :], HEAD // 2).astype(BF16)

    slab = pl.BlockSpec((tr, LANES), lambda j, i: (i, j))
    tab = pl.BlockSpec((tr, LANES), lambda j, i: (i, 0))
    vec = pl.BlockSpec((1, LANES), lambda j, i: (0, j))
    sd = jax.ShapeDtypeStruct
    return pl.pallas_call(
        body, name="ret_bwd_q", grid=(4, s // tr),
        in_specs=[slab, slab, slab, slab, slab, slab, vec,
                  pl.BlockSpec((1, c, 2 * c), lambda j, i: (j, 0, 0)),
                  pl.BlockSpec((1, c, LANES), lambda j, i: (j, 0, 0)),
                  pl.BlockSpec((1, c, LANES), lambda j, i: (j, 0, 0)),
                  pl.BlockSpec((1, LANES, LANES), lambda j, i: (j, 0, 0)),
                  pl.BlockSpec((LANES, LANES), lambda j, i: (0, 0)), tab, tab],
        out_specs=[slab, slab, slab, vec],
        out_shape=[sd((s, RET_W), BF16), sd((s, RET_W), BF16), sd((s, RET_W), BF16), sd((1, RET_W), F32)],
        scratch_shapes=[pltpu.VMEM((LANES, LANES), F32)],
        compiler_params=_cp("parallel", "arbitrary"),
    )(q, k, v, o, g, dy, gnw, rc["dmask"], rc["zeta"], rc["xi"], rc["cd"], rc["bd"], cos_r, sin_r)


def _ret_bwd_kv_call(q, k, v, do, rc, cos_r, sin_r, tr):
    s = q.shape[0]
    c = RET_CHUNK
    nc = tr // c
    nt = s // tr

    def body(q_ref, k_ref, v_ref, do_ref, dm_ref, zeta_ref, xi_ref, cd_ref, bd_ref, cr_ref, sr_ref, dk_ref, dv_ref, gs_ref):
        _zero_first(pl.program_id(1) == 0, gs_ref)
        bd = bd_ref[...]
        chunks = [slice(ci * c, (ci + 1) * c) for ci in range(nc)]
        contrib = [_dot_tn((q_ref[rows, :].astype(F32) * xi_ref[0]).astype(BF16), do_ref[rows, :]) * bd for rows in chunks]
        gs, states = gs_ref[...], [None] * nc
        for ci in reversed(range(nc)):
            states[ci] = gs.astype(BF16)
            gs = gs * cd_ref[0] + contrib[ci]
        gs_ref[...] = gs
        for ci, rows in enumerate(chunks):
            kc, vc = k_ref[rows, :], v_ref[rows, :]
            q2, do2 = _stack_heads(q_ref[rows, :]), _stack_heads(do_ref[rows, :])
            gb = states[ci]
            dk = _dot_nt(vc, gb) * zeta_ref[0] + _pair_product(vc, do2, dm_ref[0], q2)
            dv = _dot(kc, gb) * zeta_ref[0] + _pair_product(kc, q2, dm_ref[0], do2)
            dk_ref[rows, :] = (_unrope(dk, cr_ref[rows, :], sr_ref[rows, :], HEAD // 2) * (HEAD ** -0.5)).astype(BF16)
            dv_ref[rows, :] = dv.astype(BF16)

    slab = pl.BlockSpec((tr, LANES), lambda j, i: (nt - 1 - i, j))
    tab = pl.BlockSpec((tr, LANES), lambda j, i: (nt - 1 - i, 0))
    sd = jax.ShapeDtypeStruct
    return pl.pallas_call(
        body, name="ret_bwd_kv", grid=(4, nt),
        in_specs=[slab, slab, slab, slab,
                  pl.BlockSpec((1, c, 2 * c), lambda j, i: (j, 0, 0)),
                  pl.BlockSpec((1, c, LANES), lambda j, i: (j, 0, 0)),
                  pl.BlockSpec((1, c, LANES), lambda j, i: (j, 0, 0)),
                  pl.BlockSpec((1, LANES, LANES), lambda j, i: (j, 0, 0)),
                  pl.BlockSpec((LANES, LANES), lambda j, i: (0, 0)), tab, tab],
        out_specs=[slab, slab],
        out_shape=[sd((s, RET_W), BF16), sd((s, RET_W), BF16)],
        scratch_shapes=[pltpu.VMEM((LANES, LANES), F32)],
        compiler_params=_cp("parallel", "arbitrary"),
    )(q, k, v, do, rc["dmask_t"], rc["zeta"], rc["xi"], rc["cd"], rc["bd"], cos_r, sin_r)


FLASH_BWD_HEADS = 4


def _flash_bwd_call(qb, k, v, do, tb):
    s = qb.shape[1]
    nb = s // tb
    hg = FLASH_BWD_HEADS

    def body(q_ref, k_ref, v_ref, do_ref, dk_ref, dv_ref, dq_hbm, dka_ref, dva_ref, dq_ref, sem):
        g, ki, qi = pl.program_id(0), pl.program_id(1), pl.program_id(2)
        _zero_first((ki == 0) & (qi == 0), dq_ref)
        _zero_first(qi == 0, dka_ref, dva_ref)
        rows = pl.ds(pl.multiple_of(qi * tb, tb), tb)

        def step(masked):
            if masked:
                keep = lax.broadcasted_iota(jnp.int32, (tb, tb), 0) <= lax.broadcasted_iota(jnp.int32, (tb, tb), 1)
            for h in range(hg):
                st = _dot_nt(k_ref[h], q_ref[h])
                if masked:
                    st = jnp.where(keep, st, NEG)
                pt = jnp.exp2(st)
                dob = do_ref[h]
                dva_ref[h] += _dot(pt.astype(BF16), dob)
                dst = (pt * _dot_nt(v_ref[h], dob)).astype(BF16)
                dka_ref[h] += _dot(dst, q_ref[h])
                dq_ref[h, rows, :] += _dot_tn(dst, k_ref[h])

        @pl.when(qi > ki)
        def _():
            step(False)

        @pl.when(qi == ki)
        def _():
            step(True)

        @pl.when(qi == nb - 1)
        def _():
            dk_ref[...] = (dka_ref[...] * LN2).astype(BF16)
            dv_ref[...] = dva_ref[...].astype(BF16)

        @pl.when((ki == nb - 1) & (qi == nb - 1))
        def _():
            cp = pltpu.make_async_copy(dq_ref, dq_hbm.at[pl.ds(g * hg, hg)], sem)
            cp.start()
            cp.wait()

    kspec = pl.BlockSpec((hg, tb, LANES), lambda g, ki, qi: (g, ki, 0))
    qspec = pl.BlockSpec((hg, tb, LANES), lambda g, ki, qi: (g, jnp.maximum(qi, ki), 0))
    hm = jax.ShapeDtypeStruct((N_HEADS, s, LANES), BF16)
    return pl.pallas_call(
        body, name="mla_flash_bwd", grid=(N_HEADS // hg, nb, nb),
        in_specs=[qspec, kspec, kspec, qspec],
        out_specs=[kspec, kspec, ANY],
        out_shape=[hm, hm, jax.ShapeDtypeStruct((N_HEADS, s, LANES), F32)],
        scratch_shapes=[pltpu.VMEM((hg, tb, LANES), F32), pltpu.VMEM((hg, tb, LANES), F32),
                        pltpu.VMEM((hg, s, LANES), F32), pltpu.SemaphoreType.DMA],
        compiler_params=_cp("arbitrary", "arbitrary", "arbitrary"),
    )(qb, k, v, do)


def _mla_post_call(dq, dk, dv, cq, ckv, qnw, kvnw, wq, wk, wv, cos_m, sin_m, ts):
    s = cq.shape[0]

    def body(dq_ref, dk_ref, dv_ref, cq_ref, ckv_ref, qnw_ref, kvnw_ref, wq_ref, wk_ref, wv_ref, cm_ref, sm_ref,
             dcq_ref, dckv_ref, dkpe_ref, dwq_ref, dwk_ref, dwv_ref, dqnw_ref, dkvnw_ref):
        _zero_first(pl.program_id(0) == 0, dwq_ref, dwk_ref, dwv_ref, dqnw_ref, dkvnw_ref)
        cqv, ckvv = cq_ref[...], ckv_ref[...]
        rq, rkv = _rstd(cqv), _rstd(ckvv)
        qh_, kvh_ = cqv * rq, ckvv * rkv
        qnw_v, kvnw_v = qnw_ref[...], kvnw_ref[...]
        cqn = (qh_ * qnw_v).astype(BF16)
        ckvn = (kvh_ * kvnw_v).astype(BF16)
        cm, sm = cm_ref[...], sm_ref[...]
        dcqn = jnp.zeros((ts, Q_RANK), F32)
        dckvn = jnp.zeros((ts, KV_RANK), F32)
        dkpe = jnp.zeros((ts, LANES), F32)
        for h in range(N_HEADS):
            dqu = _unrope(dq_ref[h] * SM_SCALE, cm, sm, ROPE // 2).astype(BF16)
            dwq_ref[h] += _dot_tn(cqn, dqu)
            dcqn = dcqn + _dot_nt(dqu, wq_ref[h])
            dkb, dvb = dk_ref[h], dv_ref[h]
            dkpe = dkpe + dkb.astype(F32)
            dwk_ref[h] += _dot_tn(ckvn, dkb)
            dwv_ref[h] += _dot_tn(ckvn, dvb)
            dckvn = dckvn + _dot_nt(dkb, wk_ref[h]) + _dot_nt(dvb, wv_ref[h])
        lane = lax.broadcasted_iota(jnp.int32, (ts, LANES), 1)
        dkpe = jnp.where((lane >= KPE_LO) & (lane < KPE_LO + ROPE), dkpe, 0.0)
        dkpe_ref[...] = _unrope(dkpe, cm, sm, ROPE // 2).astype(BF16)
        dqnw_ref[...] += _colsum(dcqn * qh_)
        dkvnw_ref[...] += _colsum(dckvn * kvh_)
        dcq_ref[...] = _norm_bwd(dcqn, qh_, rq, qnw_v).astype(BF16)
        dckv_ref[...] = _norm_bwd(dckvn, kvh_, rkv, kvnw_v).astype(BF16)

    sd = jax.ShapeDtypeStruct
    hm = _hrow(N_HEADS, ts, LANES)
    return pl.pallas_call(
        body, name="mla_post", grid=(s // ts,),
        in_specs=[hm, hm, hm, _row(ts, Q_RANK), _row(ts, KV_RANK), _full((1, Q_RANK)), _full((1, KV_RANK)),
                  _full((N_HEADS, Q_RANK, LANES)), _full((N_HEADS, KV_RANK, LANES)), _full((N_HEADS, KV_RANK, LANES)),
                  _row(ts, LANES), _row(ts, LANES)],
        out_specs=[_row(ts, Q_RANK), _row(ts, KV_RANK), _row(ts, LANES),
                   _full((N_HEADS, Q_RANK, LANES)), _full((N_HEADS, KV_RANK, LANES)), _full((N_HEADS, KV_RANK, LANES)),
                   _full((1, Q_RANK)), _full((1, KV_RANK))],
        out_shape=[sd((s, Q_RANK), BF16), sd((s, KV_RANK), BF16), sd((s, LANES), BF16),
                   sd((N_HEADS, Q_RANK, LANES), F32), sd((N_HEADS, KV_RANK, LANES), F32), sd((N_HEADS, KV_RANK, LANES), F32),
                   sd((1, Q_RANK), F32), sd((1, KV_RANK), F32)],
        compiler_params=_cp("arbitrary"),
    )(dq, dk, dv, cq, ckv, qnw, kvnw, wq, wk, wv, cos_m, sin_m)


def _in_bwd_call(parts, x, r1, anw, dx1, win, ts):
    s = x.shape[0]
    widths = [p.shape[1] for p in parts]
    np_ = len(parts)

    def body(*refs):
        p_refs = refs[:np_]
        x_ref, r_ref, anw_ref, dx1_ref, w_ref, dx_ref, dw_ref, danw_ref = refs[np_:]
        _zero_first(pl.program_id(0) == 0, dw_ref, danw_ref)
        dproj = jnp.concatenate([p[...] for p in p_refs], axis=-1)
        r, anw_v = r_ref[...], anw_ref[...]
        xh = x_ref[...] * r
        dw_ref[...] += _dot_tn((xh * anw_v).astype(BF16), dproj)
        dh = _dot_nt(dproj, w_ref[...])
        danw_ref[...] += _colsum(dh * xh)
        dx_ref[...] = dx1_ref[...] + _norm_bwd(dh, xh, r, anw_v)

    sd = jax.ShapeDtypeStruct
    return pl.pallas_call(
        body, name="in_proj_bwd", grid=(s // ts,),
        in_specs=[_row(ts, w) for w in widths]
        + [_row(ts, D_MODEL), _row(ts, 1), _full((1, D_MODEL)), _row(ts, D_MODEL), _full((D_MODEL, IN_EXT))],
        out_specs=[_row(ts, D_MODEL), _full((D_MODEL, IN_EXT)), _full((1, D_MODEL))],
        out_shape=[sd((s, D_MODEL), F32), sd((D_MODEL, IN_EXT), F32), sd((1, D_MODEL), F32)],
        compiler_params=_cp("arbitrary"),
    )(*parts, x, r1, anw, dx1, win)


def _local_step(x, positions, tgt, w, small, ex=None):
    s = x.shape[0]
    t = _tiles(s)
    ex = _Exchanges(w) if ex is None else ex
    f = _forward(x, positions, w, small, ex)
    pw, rc = f["pw"], f["rc"]
    cos_r, sin_r, cos_m, sin_m = f["tabs"]
    dx2, loss, g_fw = _loss_call(f["x2"], tgt, small["final_norm_w"], t["ts"])
    du, dx1, g_cw, g_cb, g_fnw, g_wd = _ffn_bwd_call(dx2, f["u"], w["conv_w"], small["conv_b"], pw["wdown"], pw["wup"],
                                                     f["x1"], f["r2"], small["ffn_norm_w"], t["t2"])
    g_wup = _dw_norm_call(f["x1"], f["r2"], small["ffn_norm_w"], du, t["ts"], F2 // 4, "dw_up")
    ex.mlp_grads(dict(w_up=g_wup, w_down=g_wd))
    dy_ret, do, g_wout = _out_bwd_call(dx1, f["y_ret"], f["y_mla"], pw["wout"], t["ts"])
    ex.behind_out_bwd(g_wout)
    drq, dg, do_ret, g_gnw = _ret_bwd_q_call(f["q"], f["k"], f["v"], f["o_ret"], f["g"], dy_ret, small["ret_gn_w"], rc, cos_r, sin_r, t["tr"])
    drk, drv = _ret_bwd_kv_call(f["q"], f["k"], f["v"], do_ret, rc, cos_r, sin_r, t["tr"])
    dmk, dmv, dmq = _flash_bwd_call(f["mqb"], f["mk"], f["mv"], do, t["tb"])
    ex.behind_attention(dmk)
    dcq, dckv, dkpe, g_wq, g_wk, g_wv, g_qnw, g_kvnw = _mla_post_call(
        dmq, dmk, dmv, f["cq"], f["ckv"], small["mla_q_norm_w"], small["mla_kv_norm_w"], pw["wq"], pw["wk"], pw["wv"], cos_m, sin_m, t["ts"])
    gx, g_win_ext, g_anw = _in_bwd_call([drq, drk, drv, dg, dcq, dckv, dkpe], x, f["r1"], small["attn_norm_w"], dx1, pw["win"], t["ts"])
    lo = IN_W - ROPE
    g_win = jnp.concatenate([g_win_ext[:, :lo], g_win_ext[:, lo + KPE_LO:lo + KPE_LO + ROPE]], -1)
    g_wuq = g_wq.transpose(1, 0, 2)[:, :, :HEAD + ROPE].reshape(Q_RANK, N_HEADS * (HEAD + ROPE))
    g_wukv = jnp.concatenate([g_wk[:, :, :HEAD], g_wv[:, :, :HEAD]], -1).transpose(1, 0, 2).reshape(KV_RANK, 2 * MLA_W)
    gw = dict(w_in=g_win, w_uq=g_wuq, w_ukv=g_wukv, w_out=g_wout, w_up=g_wup,
              conv_w=g_cw, w_down=g_wd)
    gs = dict(attn_norm_w=g_anw, ret_gn_w=g_gnw, mla_q_norm_w=g_qnw, mla_kv_norm_w=g_kvnw, ffn_norm_w=g_fnw,
              conv_b=g_cb, final_norm_w=g_fw)
    return loss, gx, gw, gs


MESH_ID = pl.DeviceIdType.MESH
ANY = pl.BlockSpec(memory_space=pl.ANY)
VMEM_SPEC = pl.BlockSpec(memory_space=pltpu.VMEM)
N_DEV = 8
GROUP_A = (("w_in", (D_MODEL, IN_W // 4), 1), ("w_uq", (Q_RANK, 192), 1), ("w_ukv", (KV_RANK, 256), 1),
           ("w_out", (D_MODEL // 4, D_MODEL), 0))
GROUP_B = (("w_up", (D_MODEL, F2 // 4), 1), ("w_down", (D_FF // 4, D_MODEL), 0))


def _group_rows(group):
    n = sum(r * c for _, (r, c), _ in group)
    assert n % (2 * 16 * LANES) == 0
    return n // LANES


def _add_tile(half_rows):
    return max(t for t in range(16, 2049, 16) if half_rows % t == 0)


def _mesh_pos():
    return lax.axis_index("x"), lax.axis_index("y"), lax.axis_index("c")


def _other_chips(x, y):
    return [(1 - x, y), (x, 1 - y), (1 - x, 1 - y)]


def _pack(parts, dtype):
    flat = jnp.concatenate([p.reshape(-1).astype(dtype) for p in parts])
    return flat.reshape(2, flat.shape[0] // (2 * LANES), LANES)


def _unpack(flat, group):
    out, off = [], 0
    for _, (r, c), _ in group:
        out.append(flat[..., off:off + r * c].reshape(flat.shape[:-1] + (r, c)))
        off += r * c
    return out


def _full_weights(gathered, group):
    full = {}
    for (n, (r, c), axis), piece in zip(group, _unpack(gathered.reshape(4, -1), group)):
        if n == "w_up":
            full[n] = piece
        else:
            full[n] = piece.transpose(1, 0, 2).reshape(r, 4 * c) if axis == 1 else piece.reshape(4 * r, c)
    return full


def _grad_shards(gw, group):
    shards = []
    for n, (r, c), axis in group:
        g = gw[n]
        if axis == 1 and g.ndim == 2:
            g = g.reshape(r, 4, c).transpose(1, 0, 2)
        shards.append(g.reshape(4, r * c))
    flat = jnp.concatenate(shards, axis=1)
    return flat.reshape(4, 2, flat.shape[1] // (2 * LANES), LANES)


def _all_gather_call(packed, tag):
    _, h, _ = packed.shape

    def body(src_ref, out_ref, send_sems, recv_sems):
        x, y, c = _mesh_pos()
        sm = 2 * x + y
        chips = _other_chips(x, y)
        sib = (x, y, 1 - c)

        def rcopy(k, src, dst, dev):
            return pltpu.make_async_remote_copy(src_ref=src, dst_ref=dst, send_sem=send_sems.at[k], recv_sem=recv_sems.at[k],
                                                device_id=dev, device_id_type=MESH_ID)

        first = [rcopy(j, src_ref.at[c], out_ref.at[sm, c], (cx, cy, c)) for j, (cx, cy) in enumerate(chips)]
        own = rcopy(6, src_ref, out_ref.at[sm], sib)
        for cp in first + [own]:
            cp.start()
        passed = []
        for j, (cx, cy) in enumerate(chips):
            sj = 2 * cx + cy
            rcopy(j, src_ref.at[c], out_ref.at[sj, c], (cx, cy, c)).wait_recv()
            cp = rcopy(3 + j, out_ref.at[sj, c], out_ref.at[sj, c], sib)
            cp.start()
            passed.append(cp)
        for j, (cx, cy) in enumerate(chips):
            rcopy(3 + j, src_ref.at[c], out_ref.at[2 * cx + cy, 1 - c], sib).wait_recv()
        own.wait_recv()
        for cp in first + passed + [own]:
            cp.wait_send()

    return pl.pallas_call(
        body, name="weights_all_gather_" + tag,
        in_specs=[ANY], out_specs=ANY,
        out_shape=jax.ShapeDtypeStruct((4, 2, h, LANES), packed.dtype),
        scratch_shapes=[pltpu.SemaphoreType.DMA((7,)), pltpu.SemaphoreType.DMA((7,))],
    )(packed)


HBM_SPEC = pl.BlockSpec(memory_space=pltpu.HBM)
SEM_SPEC = pl.BlockSpec(memory_space=pltpu.SEMAPHORE)
N_GATHER_SEMS = 13


def _gather_copies(src_ref, land_ref, send_sems, recv_sems):
    x, y, c = _mesh_pos()
    sm = 2 * x + y

    def rcopy(k, src, dst, dev):
        return pltpu.make_async_remote_copy(src_ref=src, dst_ref=dst, send_sem=send_sems.at[k], recv_sem=recv_sems.at[k],
                                            device_id=dev, device_id_type=MESH_ID)

    sends, recvs = [], []
    for j, (cx, cy) in enumerate(_other_chips(x, y)):
        for t in range(2):
            sends.append(rcopy(4 * j + 2 * c + t, src_ref.at[c], land_ref.at[sm, c], (cx, cy, t)))
            recvs.append(rcopy(4 * j + 2 * t + c, src_ref.at[t], land_ref.at[2 * cx + cy, t], (cx, cy, t)))
    sends.append(rcopy(12, src_ref, land_ref.at[sm], (x, y, 1 - c)))
    recvs.append(rcopy(12, src_ref, land_ref.at[sm], (x, y, 1 - c)))
    return sends, recvs


def _gather_start_call(packed, tag):
    _, h, _ = packed.shape

    def body(src_ref, land_ref, send_sems, recv_sems, src_thru, land_thru, token_ref):
        sends, _ = _gather_copies(src_ref, land_ref, send_sems, recv_sems)
        for cp in sends:
            cp.start()
        token_ref[...] = jnp.zeros_like(token_ref)

    land = pltpu.with_memory_space_constraint(lax.empty((4, 2, h, LANES), packed.dtype), pltpu.HBM)
    return pl.pallas_call(
        body, name="weights_gather_start_" + tag,
        out_shape=(pltpu.SemaphoreType.DMA((N_GATHER_SEMS,)), pltpu.SemaphoreType.DMA((N_GATHER_SEMS,)),
                   pltpu.HBM(packed.shape, packed.dtype), pltpu.HBM(land.shape, land.dtype), jax.ShapeDtypeStruct((8, LANES), F32)),
        in_specs=(HBM_SPEC, HBM_SPEC), out_specs=(SEM_SPEC, SEM_SPEC, HBM_SPEC, HBM_SPEC, VMEM_SPEC),
        input_output_aliases={0: 2, 1: 3},
        compiler_params=pltpu.CompilerParams(has_side_effects=pltpu.SideEffectType.DATAFLOW_SIDE_EFFECTING),
    )(pltpu.with_memory_space_constraint(packed, pltpu.HBM), land)


def _gather_wait_call(send_sems, recv_sems, src_thru, land_thru, after, tag):
    def body(src_ref, land_ref, send_sems, recv_sems, after_ref, src_dead, got_ref):
        sends, recvs = _gather_copies(src_ref, land_ref, send_sems, recv_sems)
        for cp in sends:
            cp.wait_send()
        for cp in recvs:
            cp.wait_recv()

    return pl.pallas_call(
        body, name="weights_gather_wait_" + tag,
        out_shape=(pltpu.HBM(src_thru.shape, src_thru.dtype), pltpu.HBM(land_thru.shape, land_thru.dtype)),
        in_specs=(HBM_SPEC, HBM_SPEC, SEM_SPEC, SEM_SPEC, ANY), out_specs=(HBM_SPEC, HBM_SPEC),
        input_output_aliases={0: 0, 1: 1},
        compiler_params=pltpu.CompilerParams(has_side_effects=pltpu.SideEffectType.DATAFLOW_SIDE_EFFECTING),
    )(src_thru, land_thru, send_sems, recv_sems, after)[1]


def _rs_sibling_call(g, tag):
    _, _, h, _ = g.shape

    def body(g_ref, buf_ref, send_sems, recv_sems):
        x, y, c = _mesh_pos()
        cps = [pltpu.make_async_remote_copy(src_ref=g_ref.at[s, 1 - c], dst_ref=buf_ref.at[s], send_sem=send_sems.at[s],
                                            recv_sem=recv_sems.at[s], device_id=(x, y, 1 - c), device_id_type=MESH_ID)
               for s in range(4)]
        for cp in cps:
            cp.start()
        for cp in cps:
            cp.wait()

    return pl.pallas_call(
        body, name="grads_rs_sibling_" + tag,
        in_specs=[ANY], out_specs=ANY,
        out_shape=jax.ShapeDtypeStruct((4, h, LANES), g.dtype),
        scratch_shapes=[pltpu.SemaphoreType.DMA((4,)), pltpu.SemaphoreType.DMA((4,))],
    )(g)


def _rs_add1_call(g, buf, c, tag):
    _, _, h, _ = g.shape
    tile = _add_tile(h)

    def body(c_ref, g_ref, b_ref, p_ref, pb_ref):
        p = g_ref[...] + b_ref[...]
        p_ref[...] = p
        pb_ref[...] = p.astype(BF16)

    blk = pl.BlockSpec((None, tile, LANES), lambda s, i, c_ref: (s, i, 0))
    return pl.pallas_call(
        body, name="grads_rs_add_sibling_" + tag,
        grid_spec=pltpu.PrefetchScalarGridSpec(
            num_scalar_prefetch=1, grid=(4, h // tile),
            in_specs=[pl.BlockSpec((None, None, tile, LANES), lambda s, i, c_ref: (s, c_ref[0], i, 0)), blk],
            out_specs=[blk, blk]),
        out_shape=[jax.ShapeDtypeStruct((4, h, LANES), F32), jax.ShapeDtypeStruct((4, h, LANES), BF16)],
        compiler_params=_cp("parallel", "parallel"),
    )(c, g, buf)


def _rs_chips_call(pb, tag):
    _, h, _ = pb.shape

    def body(pb_ref, buf_ref, send_sems, recv_sems):
        x, y, c = _mesh_pos()
        cps = [pltpu.make_async_remote_copy(src_ref=pb_ref.at[2 * cx + cy], dst_ref=buf_ref.at[j], send_sem=send_sems.at[j],
                                            recv_sem=recv_sems.at[j], device_id=(cx, cy, c), device_id_type=MESH_ID)
               for j, (cx, cy) in enumerate(_other_chips(x, y))]
        for cp in cps:
            cp.start()
        for cp in cps:
            cp.wait()

    return pl.pallas_call(
        body, name="grads_rs_chips_" + tag,
        in_specs=[ANY], out_specs=ANY,
        out_shape=jax.ShapeDtypeStruct((3, h, LANES), pb.dtype),
        scratch_shapes=[pltpu.SemaphoreType.DMA((3,)), pltpu.SemaphoreType.DMA((3,))],
    )(pb)


def _rs_add2_call(p, buf, sm, tag):
    _, h, _ = p.shape
    tile = _add_tile(h)

    def body(sm_ref, p_ref, b_ref, f_ref):
        f_ref[...] = ((p_ref[...] + b_ref[0].astype(F32)) + b_ref[1].astype(F32)) + b_ref[2].astype(F32)

    return pl.pallas_call(
        body, name="grads_rs_add_chips_" + tag,
        grid_spec=pltpu.PrefetchScalarGridSpec(
            num_scalar_prefetch=1, grid=(h // tile,),
            in_specs=[pl.BlockSpec((None, tile, LANES), lambda i, sm_ref: (sm_ref[0], i, 0)),
                      pl.BlockSpec((3, tile, LANES), lambda i, sm_ref: (0, i, 0))],
            out_specs=pl.BlockSpec((tile, LANES), lambda i, sm_ref: (i, 0))),
        out_shape=jax.ShapeDtypeStruct((h, LANES), F32),
        compiler_params=_cp("parallel"),
    )(sm, p, buf)


def _rs_share_call(f, tag):
    h, _ = f.shape

    def body(f_ref, out_ref, send_sem, recv_sem):
        x, y, c = _mesh_pos()
        cp = pltpu.make_async_remote_copy(src_ref=f_ref, dst_ref=out_ref, send_sem=send_sem, recv_sem=recv_sem,
                                          device_id=(x, y, 1 - c), device_id_type=MESH_ID)
        cp.start()
        cp.wait()

    return pl.pallas_call(
        body, name="grads_rs_share_" + tag,
        in_specs=[ANY], out_specs=ANY,
        out_shape=jax.ShapeDtypeStruct((h, LANES), f.dtype),
        scratch_shapes=[pltpu.SemaphoreType.DMA, pltpu.SemaphoreType.DMA],
    )(f)


def _exchange8_call(vec, reduce, name):
    rows = vec.shape[0]

    def body(v_ref, out_ref, *rest):
        slots, send_sems, recv_sems = (rest if reduce else (out_ref,) + rest)
        x, y, c = _mesh_pos()
        me = 4 * x + 2 * y + c
        slots[me] = v_ref[...]

        def rcopy(k, to_me):
            bx, by, bc = (k >> 2) & 1, (k >> 1) & 1, k & 1
            px, py, pc = (1 - x if bx else x), (1 - y if by else y), (1 - c if bc else c)
            slot = 4 * px + 2 * py + pc if to_me else me
            return pltpu.make_async_remote_copy(src_ref=v_ref, dst_ref=slots.at[slot], send_sem=send_sems.at[k - 1],
                                                recv_sem=recv_sems.at[k - 1], device_id=(px, py, pc), device_id_type=MESH_ID)

        for k in range(1, N_DEV):
            rcopy(k, False).start()
        for k in range(1, N_DEV):
            rcopy(k, True).wait_recv()
        for k in range(1, N_DEV):
            rcopy(k, False).wait_send()
        if reduce:
            tot = slots[0]
            for d in range(1, N_DEV):
                tot = tot + slots[d]
            out_ref[...] = tot

    stack = jax.ShapeDtypeStruct((N_DEV, rows, LANES), F32)
    return pl.pallas_call(
        body, name=name,
        in_specs=[VMEM_SPEC], out_specs=VMEM_SPEC,
        out_shape=jax.ShapeDtypeStruct((rows, LANES), F32) if reduce else stack,
        scratch_shapes=([pltpu.VMEM((N_DEV, rows, LANES), F32)] if reduce else [])
        + [pltpu.SemaphoreType.DMA((N_DEV - 1,)), pltpu.SemaphoreType.DMA((N_DEV - 1,))],
    )(vec)


def _adamw_call(w, g, m, v, name):
    r, c = w.shape
    rb = r if r <= 256 else (256 if r % 256 == 0 else 352)
    assert r % rb == 0

    def body(w_ref, g_ref, m_ref, v_ref, d_ref, nm_ref, nv_ref):
        gv = g_ref[...]
        nm = ADAM_B1 * m_ref[...] + (1.0 - ADAM_B1) * gv
        nv = ADAM_B2 * v_ref[...] + (1.0 - ADAM_B2) * jnp.square(gv)
        m_hat = nm / (1.0 - ADAM_B1 ** ADAM_STEP)
        v_hat = nv / (1.0 - ADAM_B2 ** ADAM_STEP)
        d_ref[...] = -ADAM_LR * (m_hat / (jnp.sqrt(v_hat) + ADAM_EPS) + ADAM_WD * w_ref[...])
        nm_ref[...] = nm
        nv_ref[...] = nv

    spec = pl.BlockSpec((rb, c), lambda i: (i, 0))
    sd = jax.ShapeDtypeStruct((r, c), F32)
    return pl.pallas_call(
        body, name=name, grid=(r // rb,),
        in_specs=[spec] * 4, out_specs=[spec] * 3, out_shape=[sd, sd, sd],
        compiler_params=_cp("parallel"),
    )(w, g, m, v)


SMALL = (("attn_norm_w", D_MODEL), ("ret_gn_w", RET_W), ("mla_q_norm_w", Q_RANK), ("mla_kv_norm_w", KV_RANK),
         ("ffn_norm_w", D_MODEL), ("conv_b", F2), ("final_norm_w", D_MODEL))
WEIGHT_ORDER = ("attn_norm_w", "w_in", "ret_gn_w", "mla_q_norm_w", "w_uq", "mla_kv_norm_w", "w_ukv", "w_out",
                "ffn_norm_w", "w_up", "conv_w", "conv_b", "w_down", "final_norm_w")


def _pad_rows(flat, rows):
    return jnp.concatenate([flat, jnp.zeros((rows * LANES - flat.shape[0],), flat.dtype)]).reshape(rows, LANES)


def kernel(x, positions, attn_norm_w, w_in, ret_gn_w, mla_q_norm_w, w_uq, mla_kv_norm_w, w_ukv, w_out, ffn_norm_w, w_up, conv_w, conv_b, w_down, final_norm_w, loss_target, m_attn_norm_w, m_w_in, m_ret_gn_w, m_mla_q_norm_w, m_w_uq, m_mla_kv_norm_w, m_w_ukv, m_w_out, m_ffn_norm_w, m_w_up, m_conv_w, m_conv_b, m_w_down, m_final_norm_w, v_attn_norm_w, v_w_in, v_ret_gn_w, v_mla_q_norm_w, v_w_uq, v_mla_kv_norm_w, v_w_ukv, v_w_out, v_ffn_norm_w, v_w_up, v_conv_w, v_conv_b, v_w_down, v_final_norm_w):
    args = dict(locals())
    cx, cy, cc = _mesh_pos()
    sm = 2 * cx + cy

    c_arr, sm_arr = cc.reshape(1).astype(jnp.int32), sm.reshape(1).astype(jnp.int32)

    def reduce_scatter(gw, group, tag):
        gpk = _grad_shards(gw, group)
        p, pb = _rs_add1_call(gpk, _rs_sibling_call(gpk, tag), c_arr, tag)
        fin = _rs_add2_call(p, _rs_chips_call(pb, tag), sm_arr, tag)
        sib = _rs_share_call(fin, tag)
        both = jnp.where(cc == 0, jnp.stack([fin, sib]), jnp.stack([sib, fin]))
        return dict(zip([n for n, _, _ in group], _unpack(both.reshape(-1), group)))

    class StepExchanges(_Exchanges):
        def __init__(self):
            self.gather_b = _gather_start_call(_pack([args[n][0] for n, _, _ in GROUP_B], BF16), "b")
            self.red_b = None

        def token(self):
            return self.gather_b[4][0:1, 0:1]

        def mlp_weights(self, after):
            send_sems, recv_sems, src_thru, land_thru, _ = self.gather_b
            return _full_weights(_gather_wait_call(send_sems, recv_sems, src_thru, land_thru, after, "b"), GROUP_B)

        def mlp_grads(self, gw):
            self.red_b = reduce_scatter(gw, GROUP_B, "b")

    ex = StepExchanges()
    full = _full_weights(_all_gather_call(_pack([args[n][0] for n, _, _ in GROUP_A], BF16), "a"), GROUP_A)
    cw_rows = 40
    cw_all = _exchange8_call(_pad_rows(conv_w[0].reshape(-1), cw_rows), False, "conv_w_all_gather")
    cw_all = cw_all[0::2].reshape(4, cw_rows * LANES)[:, :3 * F2 // 4].reshape(4, 3, F2 // 4)
    full["conv_w"] = cw_all.transpose(1, 0, 2).reshape(3, F2)
    small = {n: args[n].reshape(1, d) for n, d in SMALL}
    small["attn_norm_w"] = small["attn_norm_w"] + ex.token()

    loss, gx, gw, gs = _local_step(x[0], positions[0], loss_target[0], full, small, ex)
    red = {**ex.red_b, **reduce_scatter(gw, GROUP_A, "a")}

    vec = jnp.concatenate([gs[n].reshape(-1) for n, _ in SMALL] + [gw["conv_w"].reshape(-1), loss.reshape(-1)])
    n_small = sum(d for _, d in SMALL)
    tot = _exchange8_call(_pad_rows(vec, 216), True, "small_all_reduce").reshape(-1)
    off = 0
    for n, d in SMALL:
        red[n] = tot[off:off + d].reshape(1, d)
        off += d
    red["conv_w"] = lax.dynamic_slice(tot[off:off + 3 * F2].reshape(3, F2), (0, sm * (F2 // 4)), (3, F2 // 4))
    loss_tot = tot[off + 3 * F2]

    grads, deltas, new_m, new_v = [], [], [], []
    for n in WEIGHT_ORDER:
        shape = args[n].shape
        two_d = (1, shape[0]) if len(shape) == 1 else shape[-2:]
        g = red[n].reshape(two_d)
        d, nm, nv = _adamw_call(args[n].reshape(two_d), g, args["m_" + n].reshape(two_d), args["v_" + n].reshape(two_d), "adamw_" + n)
        grads.append(g.reshape(shape))
        deltas.append(d.reshape(shape))
        new_m.append(nm.reshape(shape))
        new_v.append(nv.reshape(shape))
    return (loss_tot, gx[None], *grads, *deltas, *new_m, *new_v)
```

```python
import functools
import math

import numpy as np
import jax
import jax.numpy as jnp
from jax import lax
from jax.experimental import pallas as pl
from jax.experimental.pallas import tpu as pltpu

F32 = jnp.float32
BF16 = jnp.bfloat16

D_MODEL = 1024
N_HEADS = 8
HEAD = 64
RET_W = N_HEADS * HEAD
MLA_W = N_HEADS * HEAD
ROPE = 32
Q_RANK = 256
KV_RANK = 128
D_FF = 2816
F2 = 2 * D_FF
IN_W = 4 * RET_W + Q_RANK + KV_RANK + ROPE
IN_EXT = 4 * RET_W + Q_RANK + KV_RANK + 128
KPE_LO = 64
ROPE_BASE = 10000.0
EPS = 1e-6
RET_CHUNK = 128
SM_SCALE = (HEAD + ROPE) ** -0.5
LOG2E = math.log2(math.e)
LN2 = math.log(2.0)
NEG = -1e30
LANES = 128
VMEM_LIMIT = 56 * 1024 * 1024

ADAM_LR = 0.001
ADAM_B1 = 0.9
ADAM_B2 = 0.999
ADAM_EPS = 1e-08
ADAM_WD = 0.01
ADAM_STEP = 10


def _cp(*sem):
    return pltpu.CompilerParams(dimension_semantics=sem, vmem_limit_bytes=VMEM_LIMIT)


def _full(shape):
    n = len(shape)
    return pl.BlockSpec(tuple(shape), lambda *_: (0,) * n)


def _row(ts, c):
    return pl.BlockSpec((ts, c), lambda i: (i, 0))


def _hrow(h, ts, c):
    return pl.BlockSpec((h, ts, c), lambda i: (0, i, 0))


def _dot(a, b):
    return jnp.dot(a, b, preferred_element_type=F32)


def _dot_nt(a, b):
    return lax.dot_general(a, b, (((1,), (1,)), ((), ())), preferred_element_type=F32)


def _dot_tn(a, b):
    return lax.dot_general(a, b, (((0,), (0,)), ((), ())), preferred_element_type=F32)


def _dot_hi(a, b):
    hi = a.astype(BF16)
    lo = (a - hi.astype(F32)).astype(BF16)
    bb = b.astype(BF16)
    return _dot(hi, bb) + _dot(lo, bb)


def _rot_half(x, half):
    w = x.shape[-1]
    lane = lax.broadcasted_iota(jnp.int32, x.shape, x.ndim - 1)
    first = (lane % (2 * half)) < half
    return jnp.where(first, -pltpu.roll(x, w - half, x.ndim - 1), pltpu.roll(x, half, x.ndim - 1))


def _rope(x, cos, sin, half):
    return x * cos + _rot_half(x, half) * sin


def _unrope(dy, cos, sin, half):
    return dy * cos - _rot_half(dy, half) * sin


def _silu(g):
    return g / (1.0 + jnp.exp(-g))


def _rstd(x):
    return lax.rsqrt(jnp.mean(x * x, axis=-1, keepdims=True) + EPS)


def _rope_tables(positions):
    pos = positions.astype(F32)[:, None]
    s = pos.shape[0]
    inv = ROPE_BASE ** (-jnp.arange(0, HEAD, 2, dtype=F32) / HEAD)
    ang = pos * inv
    c, sn = jnp.cos(ang), jnp.sin(ang)
    cos_r = jnp.tile(jnp.concatenate([c, c], -1), (1, 2))
    sin_r = jnp.tile(jnp.concatenate([sn, sn], -1), (1, 2))
    inv = ROPE_BASE ** (-jnp.arange(0, ROPE, 2, dtype=F32) / ROPE)
    ang = pos * inv
    c, sn = jnp.cos(ang), jnp.sin(ang)
    one, zero = jnp.ones((s, KPE_LO), F32), jnp.zeros((s, KPE_LO), F32)
    cos_m = jnp.concatenate([one, c, c, one[:, :LANES - KPE_LO - ROPE]], -1)
    sin_m = jnp.concatenate([zero, sn, sn, zero[:, :LANES - KPE_LO - ROPE]], -1)
    return cos_r, sin_r, cos_m, sin_m


def _ret_consts():
    c = RET_CHUNK
    lg = np.log1p(-np.power(2.0, -5.0 - np.arange(N_HEADS, dtype=np.float64)))
    idx = np.arange(c, dtype=np.float64)
    diff = idx[:, None] - idx[None, :]
    lane_head = np.arange(LANES) // HEAD
    dmask = np.zeros((4, 2, c, c))
    zeta = np.zeros((4, c, LANES))
    xi = np.zeros((4, c, LANES))
    cd = np.zeros((4, LANES, LANES))
    bd = (lane_head[:, None] == lane_head[None, :]).astype(np.float64)
    for j in range(4):
        for hh in range(2):
            dmask[j, hh] = np.where(diff >= 0, np.exp(lg[2 * j + hh] * np.maximum(diff, 0.0)), 0.0)
        lgl = lg[2 * j + lane_head]
        zeta[j] = np.exp(lgl[None, :] * (c - 1.0 - idx[:, None]))
        xi[j] = np.exp(lgl[None, :] * (idx[:, None] + 1.0))
        cd[j] = np.exp(lgl * c)[:, None] * bd
    f = lambda a: jnp.asarray(a, F32)
    side = lambda d: np.concatenate([d[:, 0], d[:, 1]], axis=-1)
    return dict(dmask=f(side(dmask)), dmask_t=f(side(np.swapaxes(dmask, 2, 3))), zeta=f(zeta), xi=f(xi), cd=f(cd), bd=f(bd))


def _f1_call(x, anw, win, cos_r, sin_r, cos_m, sin_m, ts):
    s = x.shape[0]

    def body(x_ref, anw_ref, w_ref, cr_ref, sr_ref, cm_ref, sm_ref,
             q_ref, k_ref, v_ref, g_ref, cq_ref, ckv_ref, kpe_ref, r_ref):
        xv = x_ref[...]
        r = _rstd(xv)
        r_ref[...] = r
        h = (xv * r * anw_ref[...]).astype(BF16)
        cr, sr = cr_ref[...], sr_ref[...]
        qk = _dot(h, w_ref[:, 0:2 * RET_W])
        for j in range(4):
            sl = slice(j * LANES, (j + 1) * LANES)
            q_ref[:, sl] = _rope(qk[:, sl], cr, sr, HEAD // 2).astype(BF16)
            kk = qk[:, RET_W + j * LANES:RET_W + (j + 1) * LANES]
            k_ref[:, sl] = (_rope(kk, cr, sr, HEAD // 2) * (HEAD ** -0.5)).astype(BF16)
        v_ref[...] = _dot(h, w_ref[:, 2 * RET_W:3 * RET_W]).astype(BF16)
        g_ref[...] = _dot(h, w_ref[:, 3 * RET_W:4 * RET_W])
        o = 4 * RET_W
        cq_ref[...] = _dot(h, w_ref[:, o:o + Q_RANK])
        ckv_ref[...] = _dot(h, w_ref[:, o + Q_RANK:o + Q_RANK + KV_RANK])
        kp = _dot(h, w_ref[:, o + Q_RANK + KV_RANK:IN_EXT])
        kpe_ref[...] = _rope(kp, cm_ref[...], sm_ref[...], ROPE // 2)

    sd = jax.ShapeDtypeStruct
    return pl.pallas_call(
        body, name="f1_in_proj", grid=(s // ts,),
        in_specs=[_row(ts, D_MODEL), _full((1, D_MODEL)), _full((D_MODEL, IN_EXT)),
                  _row(ts, LANES), _row(ts, LANES), _row(ts, LANES), _row(ts, LANES)],
        out_specs=[_row(ts, RET_W), _row(ts, RET_W), _row(ts, RET_W), _row(ts, RET_W),
                   _row(ts, Q_RANK), _row(ts, KV_RANK), _row(ts, LANES), _row(ts, 1)],
        out_shape=[sd((s, RET_W), BF16), sd((s, RET_W), BF16), sd((s, RET_W), BF16), sd((s, RET_W), F32),
                   sd((s, Q_RANK), F32), sd((s, KV_RANK), F32), sd((s, LANES), F32), sd((s, 1), F32)],
        compiler_params=_cp("parallel"),
    )(x, anw, win, cos_r, sin_r, cos_m, sin_m)


def _stack_heads(a):
    lo = lax.broadcasted_iota(jnp.int32, a.shape, 1) < HEAD
    zero = jnp.zeros_like(a)
    return jnp.concatenate([jnp.where(lo, a, zero), jnp.where(lo, zero, a)], axis=0)


def _pair_product(a, b2, decay2, w2):
    return _dot((_dot_nt(a, b2) * decay2).astype(BF16), w2)


def _ret_fwd_call(q, k, v, g, gnw, rc, tr):
    s = q.shape[0]
    c = RET_CHUNK
    nc = tr // c

    def body(q_ref, k_ref, v_ref, g_ref, gnw_ref, dm_ref, zeta_ref, xi_ref, cd_ref, bd_ref, o_ref, y_ref, st_ref):
        @pl.when(pl.program_id(1) == 0)
        def _():
            st_ref[...] = jnp.zeros_like(st_ref)

        lane = lax.broadcasted_iota(jnp.int32, (c, LANES), 1)
        bd = bd_ref[...]
        chunks = [slice(ci * c, (ci + 1) * c) for ci in range(nc)]
        contrib = [_dot_tn((k_ref[rows, :].astype(F32) * zeta_ref[0]).astype(BF16), v_ref[rows, :]) * bd for rows in chunks]
        st, states = st_ref[...], []
        for ci in range(nc):
            states.append(st.astype(BF16))
            st = st * cd_ref[0] + contrib[ci]
        st_ref[...] = st
        for ci, rows in enumerate(chunks):
            qc = q_ref[rows, :]
            o_ref[rows, :] = (_dot(qc, states[ci]) * xi_ref[0]
                              + _pair_product(qc, _stack_heads(k_ref[rows, :]), dm_ref[0], _stack_heads(v_ref[rows, :])))
        o = o_ref[...]
        avg = bd * (1.0 / HEAD)
        ctr = o - _dot_hi(o, avg)
        var = _dot_hi(ctr * ctr, avg)
        y_ref[...] = (_silu(g_ref[...]) * (ctr * lax.rsqrt(var + EPS) * gnw_ref[...])).astype(BF16)

    slab = pl.BlockSpec((tr, LANES), lambda j, i: (i, j))
    sd = jax.ShapeDtypeStruct
    return pl.pallas_call(
        body, name="ret_fwd", grid=(4, s // tr),
        in_specs=[slab, slab, slab, slab, pl.BlockSpec((1, LANES), lambda j, i: (0, j)),
                  pl.BlockSpec((1, c, 2 * c), lambda j, i: (j, 0, 0)),
                  pl.BlockSpec((1, c, LANES), lambda j, i: (j, 0, 0)),
                  pl.BlockSpec((1, c, LANES), lambda j, i: (j, 0, 0)),
                  pl.BlockSpec((1, LANES, LANES), lambda j, i: (j, 0, 0)),
                  pl.BlockSpec((LANES, LANES), lambda j, i: (0, 0))],
        out_specs=[slab, slab],
        out_shape=[sd((s, RET_W), F32), sd((s, RET_W), BF16)],
        scratch_shapes=[pltpu.VMEM((LANES, LANES), F32)],
        compiler_params=_cp("parallel", "arbitrary"),
    )(q, k, v, g, gnw, rc["dmask"], rc["zeta"], rc["xi"], rc["cd"], rc["bd"])


QK_AUX = HEAD + ROPE
V_AUX = HEAD


def _lane_pair(shape, lo, a, b, rest):
    lane = lax.broadcasted_iota(jnp.int32, shape, len(shape) - 1)
    return jnp.where(lane == lo, a, jnp.where(lane == lo + 1, b, rest))


def _hi_lo(v):
    hi = v.astype(BF16).astype(F32)
    return hi, v - hi


def _mla_pre_call(cq, ckv, kpe, qnw, kvnw, wq, wk, wv, cos_m, sin_m, ts):
    s = cq.shape[0]

    def body(cq_ref, ckv_ref, kpe_ref, qnw_ref, kvnw_ref, wq_ref, wk_ref, wv_ref, cm_ref, sm_ref, q_ref, k_ref, v_ref):
        cqv, ckvv = cq_ref[...], ckv_ref[...]
        cqn = (cqv * _rstd(cqv) * qnw_ref[...]).astype(BF16)
        ckvn = (ckvv * _rstd(ckvv) * kvnw_ref[...]).astype(BF16)
        cm, sm = cm_ref[...], sm_ref[...]
        kp = _lane_pair((ts, LANES), QK_AUX, -1.0, -1.0, kpe_ref[...])
        for h in range(N_HEADS):
            qh = _rope(_dot(cqn, wq_ref[h]), cm, sm, ROPE // 2)
            q_ref[h] = (qh * (SM_SCALE * LOG2E)).astype(BF16)
            k_ref[h] = (_dot(ckvn, wk_ref[h]) + kp).astype(BF16)
            v_ref[h] = _lane_pair((ts, LANES), V_AUX, 1.0, 1.0, _dot(ckvn, wv_ref[h])).astype(BF16)

    sd = jax.ShapeDtypeStruct
    hm = sd((N_HEADS, s, LANES), BF16)
    return pl.pallas_call(
        body, name="mla_pre", grid=(s // ts,),
        in_specs=[_row(ts, Q_RANK), _row(ts, KV_RANK), _row(ts, LANES), _full((1, Q_RANK)), _full((1, KV_RANK)),
                  _full((N_HEADS, Q_RANK, LANES)), _full((N_HEADS, KV_RANK, LANES)), _full((N_HEADS, KV_RANK, LANES)),
                  _row(ts, LANES), _row(ts, LANES)],
        out_specs=[_hrow(N_HEADS, ts, LANES)] * 3,
        out_shape=[hm, hm, hm],
        compiler_params=_cp("parallel"),
    )(cq, ckv, kpe, qnw, kvnw, wq, wk, wv, cos_m, sin_m)


def _flash_fwd_call(q, k, v, tb):
    s = q.shape[1]
    nb = s // tb

    def body(q_ref, k_ref, v_ref, o_ref, qb_ref, m_ref, acc_ref):
        qi, ki = pl.program_id(0), pl.program_id(1)

        @pl.when(ki == 0)
        def _():
            m_ref[...] = jnp.full_like(m_ref, NEG)
            acc_ref[...] = jnp.zeros_like(acc_ref)

        def step(masked):
            if masked:
                keep = lax.broadcasted_iota(jnp.int32, (tb, tb), 1) <= lax.broadcasted_iota(jnp.int32, (tb, tb), 0)
            def finish(h, pe, alpha):
                acc_ref[h] = acc_ref[h] * alpha + _dot(pe, v_ref[h])

            nxt, pending = _dot_nt(q_ref[0], k_ref[0]), None
            for h in range(N_HEADS):
                sc = nxt
                if h + 1 < N_HEADS:
                    nxt = _dot_nt(q_ref[h + 1], k_ref[h + 1])
                if masked:
                    sc = jnp.where(keep, sc, NEG)
                m_prev = m_ref[h]
                m_new = jnp.maximum(m_prev, jnp.max(sc, axis=1, keepdims=True))
                pe = jnp.exp2(sc - jnp.tile(m_new, (1, tb // LANES))).astype(BF16)
                m_ref[h] = m_new
                if pending is not None:
                    finish(*pending)
                pending = (h, pe, jnp.exp2(m_prev - m_new))
            finish(*pending)

        @pl.when(ki < qi)
        def _():
            step(False)

        @pl.when(ki == qi)
        def _():
            step(True)
            lane = lax.broadcasted_iota(jnp.int32, (tb, LANES), 1)
            for p in range(N_HEADS // 2):
                outs = []
                for h in (2 * p, 2 * p + 1):
                    acc = acc_ref[h]
                    l = acc[:, V_AUX:V_AUX + 1]
                    outs.append(acc * (1.0 / l))
                    hi, lo = _hi_lo(m_ref[h][:, 0:1] + jnp.log(l) * LOG2E)
                    qb_ref[h] = _lane_pair((tb, LANES), QK_AUX, hi, lo, q_ref[h].astype(F32)).astype(BF16)
                o_ref[:, p * LANES:(p + 1) * LANES] = jnp.where(lane < HEAD, outs[0], pltpu.roll(outs[1], HEAD, 1)).astype(BF16)

    sd = jax.ShapeDtypeStruct
    qspec = pl.BlockSpec((N_HEADS, tb, LANES), lambda qi, ki: (0, qi, 0))
    kspec = pl.BlockSpec((N_HEADS, tb, LANES), lambda qi, ki: (0, jnp.minimum(ki, qi), 0))
    return pl.pallas_call(
        body, name="mla_flash_fwd", grid=(nb, nb),
        in_specs=[qspec, kspec, kspec],
        out_specs=[pl.BlockSpec((tb, MLA_W), lambda qi, ki: (qi, 0)), qspec],
        out_shape=[sd((s, MLA_W), BF16), sd((N_HEADS, s, LANES), BF16)],
        scratch_shapes=[pltpu.VMEM((N_HEADS, tb, LANES), F32), pltpu.VMEM((N_HEADS, tb, LANES), F32)],
        compiler_params=_cp("parallel", "arbitrary"),
    )(q, k, v)


def _out_proj_call(x, yret, ymla, wout, ts):
    s = x.shape[0]

    def body(x_ref, yr_ref, ym_ref, w_ref, x1_ref, r_ref):
        x1 = x_ref[...] + _dot(yr_ref[...], w_ref[0:RET_W, :]) + _dot(ym_ref[...], w_ref[RET_W:, :])
        x1_ref[...] = x1
        r_ref[...] = _rstd(x1)

    sd = jax.ShapeDtypeStruct
    return pl.pallas_call(
        body, name="out_proj", grid=(s // ts,),
        in_specs=[_row(ts, D_MODEL), _row(ts, RET_W), _row(ts, MLA_W), _full((D_MODEL, D_MODEL))],
        out_specs=[_row(ts, D_MODEL), _row(ts, 1)],
        out_shape=[sd((s, D_MODEL), F32), sd((s, 1), F32)],
        compiler_params=_cp("parallel"),
    )(x, yret, ymla, wout)


W_UP_SHARD = F2 // 4


def _ffn_fwd_call(x1, r2, fnw, wup4, cw, cb, wdown, ts):
    s = x1.shape[0]
    wsh = W_UP_SHARD

    def body(x_ref, r_ref, fnw_ref, wup_ref, cw_ref, cb_ref, wd_ref, u_ref, x2_ref, carry_ref):
        _zero_first(pl.program_id(0) == 0, carry_ref)
        xv = x_ref[...]
        h = (xv * r_ref[...] * fnw_ref[...]).astype(BF16)
        conv = []
        for j in range(4):
            cols = slice(j * wsh, (j + 1) * wsh)
            ub = _dot(h, wup_ref[j]).astype(BF16)
            u_ref[:, cols] = ub
            u = ub.astype(F32)
            u1, u2 = _shifted(u, carry_ref[:, cols])
            w = cw_ref[:, cols]
            conv.append(cb_ref[:, cols] + w[0:1, :] * u2 + w[1:2, :] * u1 + w[2:3, :] * u)
            carry_ref[:, cols] = u[ts - 8:, :]
        acc = xv
        for j in range(2):
            a = (_silu(conv[j]) * conv[j + 2]).astype(BF16)
            acc = acc + _dot(a, wd_ref[j * wsh:(j + 1) * wsh, :])
        x2_ref[...] = acc

    sd = jax.ShapeDtypeStruct
    return pl.pallas_call(
        body, name="ffn_fwd", grid=(s // ts,),
        in_specs=[_row(ts, D_MODEL), _row(ts, 1), _full((1, D_MODEL)), _full((4, D_MODEL, wsh)),
                  _full((3, F2)), _full((1, F2)), _full((D_FF, D_MODEL))],
        out_specs=[_row(ts, F2), _row(ts, D_MODEL)],
        out_shape=[sd((s, F2), BF16), sd((s, D_MODEL), F32)],
        scratch_shapes=[pltpu.VMEM((8, F2), F32)],
        compiler_params=_cp("arbitrary"),
    )(x1, r2, fnw, wup4, cw, cb, wdown)


def _shifted(u, hal):
    row = lax.broadcasted_iota(jnp.int32, u.shape, 0)
    u1 = jnp.where(row == 0, hal[7:8, :], pltpu.roll(u, 1, 0))
    u2 = jnp.where(row == 0, hal[6:7, :], jnp.where(row == 1, hal[7:8, :], pltpu.roll(u, 2, 0)))
    return u1, u2


def _prep_weights(w):
    win = w["w_in"]
    pad = lambda n: jnp.zeros((D_MODEL, n), win.dtype)
    win_ext = jnp.concatenate([win[:, :IN_W - ROPE], pad(KPE_LO), win[:, IN_W - ROPE:], pad(LANES - KPE_LO - ROPE)], -1)
    wuq = w["w_uq"].reshape(Q_RANK, N_HEADS, HEAD + ROPE)
    wq = jnp.concatenate([wuq, jnp.zeros((Q_RANK, N_HEADS, LANES - HEAD - ROPE), wuq.dtype)], -1).transpose(1, 0, 2)
    wukv = w["w_ukv"].reshape(KV_RANK, N_HEADS, 2 * HEAD)
    zk = jnp.zeros((KV_RANK, N_HEADS, HEAD), wukv.dtype)
    wk = jnp.concatenate([wukv[:, :, :HEAD], zk], -1).transpose(1, 0, 2)
    wv = jnp.concatenate([wukv[:, :, HEAD:], zk], -1).transpose(1, 0, 2)
    c = lambda a: a.astype(BF16)
    return dict(win=c(win_ext), wq=c(wq), wk=c(wk), wv=c(wv), wout=c(w["w_out"]))


def _prep_mlp_weights(w):
    wup = w["w_up"]
    if wup.ndim == 2:
        wup = wup.reshape(D_MODEL, 4, W_UP_SHARD).transpose(1, 0, 2)
    return dict(wup=wup.astype(BF16), wdown=w["w_down"].astype(BF16))


def _tiles(s):
    return dict(ts=min(s, 512), tr=min(s, 1024), tb=min(s, 512), tg=min(s, 512), tf=D_FF // 2, t2=min(s, 256))


class _Exchanges:
    def __init__(self, w):
        self.w = w

    def mlp_weights(self, after):
        return self.w

    def mlp_grads(self, gw):
        pass

    def behind_out_bwd(self, after):
        pass

    def behind_attention(self, after):
        pass


def _forward(x, positions, w, small, ex):
    s = x.shape[0]
    t = _tiles(s)
    pw = _prep_weights(w)
    cos_r, sin_r, cos_m, sin_m = _rope_tables(positions)
    rc = _ret_consts()
    q, k, v, g, cq, ckv, kpe, r1 = _f1_call(x, small["attn_norm_w"], pw["win"], cos_r, sin_r, cos_m, sin_m, t["ts"])
    o_ret, y_ret = _ret_fwd_call(q, k, v, g, small["ret_gn_w"], rc, t["tr"])
    mq, mk, mv = _mla_pre_call(cq, ckv, kpe, small["mla_q_norm_w"], small["mla_kv_norm_w"],
                               pw["wq"], pw["wk"], pw["wv"], cos_m, sin_m, t["ts"])
    y_mla, mqb = _flash_fwd_call(mq, mk, mv, t["tb"])
    x1, r2 = _out_proj_call(x, y_ret, y_mla, pw["wout"], t["ts"])
    pw.update(_prep_mlp_weights(ex.mlp_weights(r2)))
    u, x2 = _ffn_fwd_call(x1, r2, small["ffn_norm_w"], pw["wup"], w["conv_w"], small["conv_b"], pw["wdown"], t["t2"])
    return dict(pw=pw, tabs=(cos_r, sin_r, cos_m, sin_m), rc=rc, q=q, k=k, v=v, g=g, cq=cq, ckv=ckv, kpe=kpe, r1=r1,
                o_ret=o_ret, y_ret=y_ret, mqb=mqb, mk=mk, mv=mv, y_mla=y_mla, x1=x1, r2=r2, u=u, x2=x2)


def _norm_bwd(dh, xh, r, nw):
    dxn = dh * nw
    return r * (dxn - xh * jnp.mean(dxn * xh, axis=-1, keepdims=True))


def _ordered_after(body, order):
    if order is None:
        return body, [], []
    return (lambda order_ref, *refs: body(*refs)), [pl.BlockSpec(memory_space=pl.ANY)], [order]


def _zero_first(first, *refs):
    @pl.when(first)
    def _():
        for ref in refs:
            ref[...] = jnp.zeros_like(ref)


def _colsum(v):
    return jnp.sum(v, axis=0, keepdims=True)


def _dsilu(g, sg):
    return sg * (1.0 + g * (1.0 - sg))


def _loss_call(x2, tgt, fw, ts):
    s = x2.shape[0]

    def body(x_ref, t_ref, fw_ref, dx_ref, loss_ref, gfw_ref):
        _zero_first(pl.program_id(0) == 0, loss_ref, gfw_ref)
        xv = x_ref[...]
        r = _rstd(xv)
        xh = xv * r
        fwv = fw_ref[...]
        e = xh * fwv - t_ref[...]
        loss_ref[...] += (0.5 / D_MODEL) * _colsum(jnp.sum(e * e, axis=1, keepdims=True))
        dy = e * (1.0 / D_MODEL)
        gfw_ref[...] += _colsum(dy * xh)
        dx_ref[...] = _norm_bwd(dy, xh, r, fwv)

    sd = jax.ShapeDtypeStruct
    return pl.pallas_call(
        body, name="loss_bwd", grid=(s // ts,),
        in_specs=[_row(ts, D_MODEL), _row(ts, D_MODEL), _full((1, D_MODEL))],
        out_specs=[_row(ts, D_MODEL), _full((1, 1)), _full((1, D_MODEL))],
        out_shape=[sd((s, D_MODEL), F32), sd((1, 1), F32), sd((1, D_MODEL), F32)],
        compiler_params=_cp("arbitrary"),
    )(x2, tgt, fw)


def _ffn_bwd1_call(dx2, u, cw, cb, wdown, ts, tf):
    s = dx2.shape[0]
    nf = D_FF // tf

    def body(dx_ref, wd_ref, ug_ref, hg_ref, uv_ref, hv_ref, wg_ref, wv_ref, bg_ref, bv_ref,
             dgate_ref, dval_ref, dwd_ref, dcwg_ref, dcwv_ref, dcbg_ref, dcbv_ref):
        first = pl.program_id(1) == 0
        _zero_first(first, dwd_ref, dcwg_ref, dcwv_ref, dcbg_ref, dcbv_ref)
        gate, g1, g2, g0 = _conv_tile(ug_ref, hg_ref, wg_ref, bg_ref, first)
        val, v1, v2, v0 = _conv_tile(uv_ref, hv_ref, wv_ref, bv_ref, first)
        dxb = dx_ref[...].astype(BF16)
        da = _dot_nt(dxb, wd_ref[...])
        sg = 1.0 / (1.0 + jnp.exp(-gate))
        sl = gate * sg
        dgate = da * val * _dsilu(gate, sg)
        dval = da * sl
        dgate_ref[...] = dgate.astype(BF16)
        dval_ref[...] = dval.astype(BF16)
        dwd_ref[...] += _dot_tn((sl * val).astype(BF16), dxb)
        for ref, d, taps in ((dcwg_ref, dgate, (g2, g1, g0)), (dcwv_ref, dval, (v2, v1, v0))):
            for t in range(3):
                ref[t:t + 1, :] += _colsum(d * taps[t])
        dcbg_ref[...] += _colsum(dgate)
        dcbv_ref[...] += _colsum(dval)

    sd = jax.ShapeDtypeStruct
    colacc = lambda r: pl.BlockSpec((r, tf), lambda j, i: (0, j))
    return pl.pallas_call(
        body, name="ffn_bwd_gate", grid=(nf, s // ts),
        in_specs=[pl.BlockSpec((ts, D_MODEL), lambda j, i: (i, 0)), pl.BlockSpec((tf, D_MODEL), lambda j, i: (j, 0))]
        + _gate_specs(ts, tf, rows_inner=True),
        out_specs=[pl.BlockSpec((ts, tf), lambda j, i: (i, j)), pl.BlockSpec((ts, tf), lambda j, i: (i, j)),
                   pl.BlockSpec((tf, D_MODEL), lambda j, i: (j, 0)), colacc(3), colacc(3), colacc(1), colacc(1)],
        out_shape=[sd((s, D_FF), BF16), sd((s, D_FF), BF16), sd((D_FF, D_MODEL), F32),
                   sd((3, D_FF), F32), sd((3, D_FF), F32), sd((1, D_FF), F32), sd((1, D_FF), F32)],
        compiler_params=_cp("parallel", "arbitrary"),
    )(dx2, wdown, u, u, u, u, cw, cw, cb, cb)


def _ffn_bwd_call(dx2, u, cw, cb, wdown, wup4, x1, r2, fnw, ts):
    s = dx2.shape[0]
    nt = s // ts
    hb = ts // 8
    wsh = W_UP_SHARD
    rev = lambda i: nt - 1 - i

    def body(dx2_ref, u_ref, h_ref, cw_ref, cb_ref, wd_ref, wup_ref, x_ref, r_ref, fnw_ref,
             du_ref, dx1_ref, dcw_ref, dcb_ref, dfnw_ref, dwd_hbm, carry_ref, dwd_ref, sem):
        i = pl.program_id(0)
        _zero_first(i == 0, carry_ref, dwd_ref, dcw_ref, dcb_ref, dfnw_ref)
        seq_start = i == nt - 1
        dxb = dx2_ref[...].astype(BF16)
        dh = jnp.zeros((ts, D_MODEL), F32)

        def conv(cols):
            uv = u_ref[:, cols].astype(F32)
            u1, u2 = _shifted(uv, jnp.where(seq_start, 0.0, h_ref[:, cols].astype(F32)))
            w = cw_ref[:, cols]
            return cb_ref[:, cols] + w[0:1, :] * u2 + w[1:2, :] * u1 + w[2:3, :] * uv, (u2, u1, uv)

        for j in range(2):
            gcols = slice(j * wsh, (j + 1) * wsh)
            vcols = slice(D_FF + j * wsh, D_FF + (j + 1) * wsh)
            gate, gtaps = conv(gcols)
            val, vtaps = conv(vcols)
            da = _dot_nt(dxb, wd_ref[gcols, :])
            sg = 1.0 / (1.0 + jnp.exp(-gate))
            sl = gate * sg
            dwd_ref[gcols, :] += _dot_tn((sl * val).astype(BF16), dxb)
            for d, cols, taps, shard in ((da * val * _dsilu(gate, sg), gcols, gtaps, j), (da * sl, vcols, vtaps, 2 + j)):
                for t in range(3):
                    dcw_ref[t:t + 1, cols] += _colsum(d * taps[t])
                dcb_ref[:, cols] += _colsum(d)
                d1, d2 = _shifted_up(d, carry_ref[:, cols])
                w = cw_ref[:, cols]
                du = (w[2:3, :] * d + w[1:2, :] * d1 + w[0:1, :] * d2).astype(BF16)
                du_ref[:, cols] = du
                dh = dh + _dot_nt(du, wup_ref[shard])
                carry_ref[:, cols] = d[0:8, :]
        r = r_ref[...]
        xh = x_ref[...] * r
        dfnw_ref[...] += _colsum(dh * xh)
        dx1_ref[...] = dx2_ref[...] + _norm_bwd(dh, xh, r, fnw_ref[...])

        @pl.when(i == nt - 1)
        def _():
            cp = pltpu.make_async_copy(dwd_ref, dwd_hbm, sem)
            cp.start()
            cp.wait()

    sd = jax.ShapeDtypeStruct
    row = lambda c: pl.BlockSpec((ts, c), lambda i: (rev(i), 0))
    once = lambda shape: pl.BlockSpec(shape, lambda i: (0,) * len(shape), pipeline_mode=pl.Buffered(1))
    return pl.pallas_call(
        body, name="ffn_bwd", grid=(nt,),
        in_specs=[row(D_MODEL), row(F2), pl.BlockSpec((8, F2), lambda i: (jnp.maximum(rev(i) * hb - 1, 0), 0)),
                  once((3, F2)), once((1, F2)), once((D_FF, D_MODEL)), once((4, D_MODEL, wsh)),
                  row(D_MODEL), row(1), once((1, D_MODEL))],
        out_specs=[row(F2), row(D_MODEL), _full((3, F2)), _full((1, F2)), _full((1, D_MODEL)), pl.BlockSpec(memory_space=pl.ANY)],
        out_shape=[sd((s, F2), BF16), sd((s, D_MODEL), F32), sd((3, F2), F32), sd((1, F2), F32), sd((1, D_MODEL), F32),
                   sd((D_FF, D_MODEL), F32)],
        scratch_shapes=[pltpu.VMEM((8, F2), F32), pltpu.VMEM((D_FF, D_MODEL), F32), pltpu.SemaphoreType.DMA],
        compiler_params=_cp("arbitrary"),
    )(dx2, u, u, cw, cb, wdown, wup4, x1, r2, fnw)


def _shifted_up(d, hal):
    n = d.shape[0]
    row = lax.broadcasted_iota(jnp.int32, d.shape, 0)
    d1 = jnp.where(row == n - 1, hal[0:1, :], pltpu.roll(d, n - 1, 0))
    d2 = jnp.where(row == n - 2, hal[0:1, :], jnp.where(row == n - 1, hal[1:2, :], pltpu.roll(d, n - 2, 0)))
    return d1, d2


def _ffn_bwd2_call(dgate, dval, cw, wup, x1, r2, fnw, dx2, ts):
    s = dx2.shape[0]
    nt = s // ts
    hb = ts // 8
    nxt = pl.BlockSpec((8, D_FF), lambda i: (jnp.minimum((i + 1) * hb, s // 8 - 1), 0))

    def body(dg_ref, hg_ref, dv_ref, hv_ref, cw_ref, wup_ref, x_ref, r_ref, fnw_ref, dx2_ref, du_ref, dx1_ref, dfnw_ref):
        i = pl.program_id(0)
        _zero_first(i == 0, dfnw_ref)
        dh = jnp.zeros((ts, D_MODEL), F32)
        for part, (d_ref, h_ref) in enumerate(((dg_ref, hg_ref), (dv_ref, hv_ref))):
            off = part * D_FF
            d = d_ref[...].astype(F32)
            hal = jnp.where(i == nt - 1, 0.0, h_ref[...].astype(F32))
            d1, d2 = _shifted_up(d, hal)
            w = cw_ref[:, off:off + D_FF]
            du = (w[2:3, :] * d + w[1:2, :] * d1 + w[0:1, :] * d2).astype(BF16)
            du_ref[:, off:off + D_FF] = du
            for j in range(2):
                dh = dh + _dot_nt(du[:, j * W_UP_SHARD:(j + 1) * W_UP_SHARD], wup_ref[2 * part + j])
        r = r_ref[...]
        xh = x_ref[...] * r
        dfnw_ref[...] += _colsum(dh * xh)
        dx1_ref[...] = dx2_ref[...] + _norm_bwd(dh, xh, r, fnw_ref[...])

    sd = jax.ShapeDtypeStruct
    return pl.pallas_call(
        body, name="ffn_bwd_up", grid=(nt,),
        in_specs=[_row(ts, D_FF), nxt, _row(ts, D_FF), nxt, _full((3, F2)), _full((4, D_MODEL, W_UP_SHARD)),
                  _row(ts, D_MODEL), _row(ts, 1), _full((1, D_MODEL)), _row(ts, D_MODEL)],
        out_specs=[_row(ts, F2), _row(ts, D_MODEL), _full((1, D_MODEL))],
        out_shape=[sd((s, F2), BF16), sd((s, D_MODEL), F32), sd((1, D_MODEL), F32)],
        compiler_params=_cp("arbitrary"),
    )(dgate, dgate, dval, dval, cw, wup, x1, r2, fnw, dx2)


def _dw_norm_call(x, r, nw, b, ts, tn, name):
    s, n = b.shape
    k = x.shape[1]

    def body(x_ref, r_ref, nw_ref, b_ref, dw_ref):
        _zero_first(pl.program_id(1) == 0, dw_ref)
        h = (x_ref[...] * r_ref[...] * nw_ref[...]).astype(BF16)
        dw_ref[...] += _dot_tn(h, b_ref[...])

    return pl.pallas_call(
        body, name=name, grid=(n // tn, s // ts),
        in_specs=[pl.BlockSpec((ts, k), lambda j, i: (i, 0)), pl.BlockSpec((ts, 1), lambda j, i: (i, 0)),
                  pl.BlockSpec((1, k), lambda j, i: (0, 0)), pl.BlockSpec((ts, tn), lambda j, i: (i, j))],
        out_specs=pl.BlockSpec((None, k, tn), lambda j, i: (j, 0, 0)),
        out_shape=jax.ShapeDtypeStruct((n // tn, k, tn), F32),
        compiler_params=_cp("parallel", "arbitrary"),
    )(x, r, nw, b)


def _out_bwd_call(dx1, yret, ymla, wout, ts, order=None):
    s = dx1.shape[0]

    def body(dx_ref, yr_ref, ym_ref, w_ref, dyr_ref, do_ref, dwo_ref):
        _zero_first(pl.program_id(0) == 0, dwo_ref)
        dxb = dx_ref[...].astype(BF16)
        dmix = _dot_nt(dxb, w_ref[...])
        dyr_ref[...] = dmix[:, :RET_W]
        ym = ym_ref[...]
        lane = lax.broadcasted_iota(jnp.int32, (ts, LANES), 1)
        for p in range(N_HEADS // 2):
            dom = dmix[:, RET_W + p * LANES:RET_W + (p + 1) * LANES]
            prod = dom * ym[:, p * LANES:(p + 1) * LANES].astype(F32)
            for hh in range(2):
                mine = (lane >= HEAD) if hh else (lane < HEAD)
                hi, lo = _hi_lo(jnp.sum(jnp.where(mine, prod, 0.0), axis=1, keepdims=True))
                base = jnp.where(lane < HEAD, pltpu.roll(dom, HEAD, 1) if hh else dom, 0.0)
                do_ref[2 * p + hh] = _lane_pair((ts, LANES), V_AUX, -hi, -lo, base).astype(BF16)
        dwo_ref[0:RET_W, :] += _dot_tn(yr_ref[...], dxb)
        dwo_ref[RET_W:, :] += _dot_tn(ym, dxb)

    sd = jax.ShapeDtypeStruct
    body, first_specs, first = _ordered_after(body, order)
    return pl.pallas_call(
        body, name="out_proj_bwd", grid=(s // ts,),
        in_specs=first_specs + [_row(ts, D_MODEL), _row(ts, RET_W), _row(ts, MLA_W), _full((D_MODEL, D_MODEL))],
        out_specs=[_row(ts, RET_W), _hrow(N_HEADS, ts, LANES), _full((D_MODEL, D_MODEL))],
        out_shape=[sd((s, RET_W), F32), sd((N_HEADS, s, LANES), BF16), sd((D_MODEL, D_MODEL), F32)],
        compiler_params=_cp("arbitrary"),
    )(*first, dx1, yret, ymla, wout)


def _ret_bwd_q_call(q, k, v, o, g, dy, gnw, rc, cos_r, sin_r, tr):
    s = q.shape[0]
    c = RET_CHUNK
    nc = tr // c

    def body(q_ref, k_ref, v_ref, o_ref, g_ref, dy_ref, gnw_ref, dm_ref, zeta_ref, xi_ref, cd_ref, bd_ref, cr_ref, sr_ref,
             dq_ref, dg_ref, do_ref, dgnw_ref, st_ref):
        _zero_first(pl.program_id(1) == 0, st_ref, dgnw_ref)
        bd = bd_ref[...]
        avg = bd * (1.0 / HEAD)
        ov = o_ref[...]
        ctr = ov - _dot_hi(ov, avg)
        rs = lax.rsqrt(_dot_hi(ctr * ctr, avg) + EPS)
        oh = ctr * rs
        gg, dyv, gnw_v = g_ref[...], dy_ref[...], gnw_ref[...]
        sg = 1.0 / (1.0 + jnp.exp(-gg))
        sl = gg * sg
        dg_ref[...] = (dyv * oh * gnw_v * _dsilu(gg, sg)).astype(BF16)
        dgnw_ref[...] += _colsum(dyv * sl * oh)
        doh = dyv * sl * gnw_v
        dov = (rs * (doh - _dot_hi(doh, avg) - oh * _dot_hi(doh * oh, avg))).astype(BF16)
        do_ref[...] = dov
        chunks = [slice(ci * c, (ci + 1) * c) for ci in range(nc)]
        contrib = [_dot_tn((k_ref[rows, :].astype(F32) * zeta_ref[0]).astype(BF16), v_ref[rows, :]) * bd for rows in chunks]
        st, states = st_ref[...], []
        for ci in range(nc):
            states.append(st.astype(BF16))
            st = st * cd_ref[0] + contrib[ci]
        st_ref[...] = st
        for ci, rows in enumerate(chunks):
            doc = dov[rows, :]
            dq = (_dot_nt(doc, states[ci]) * xi_ref[0]
                  + _pair_product(doc, _stack_heads(v_ref[rows, :]), dm_ref[0], _stack_heads(k_ref[rows, :])))
            dq_ref[rows, :] = _unrope(dq, cr_ref[rows, :], sr_ref[rows, :], HEAD // 2).astype(BF16)

    slab = pl.BlockSpec((tr, LANES), lambda j, i: (i, j))
    tab = pl.BlockSpec((tr, LANES), lambda j, i: (i, 0))
    vec = pl.BlockSpec((1, LANES), lambda j, i: (0, j))
    sd = jax.ShapeDtypeStruct
    return pl.pallas_call(
        body, name="ret_bwd_q", grid=(4, s // tr),
        in_specs=[slab, slab, slab, slab, slab, slab, vec,
                  pl.BlockSpec((1, c, 2 * c), lambda j, i: (j, 0, 0)),
                  pl.BlockSpec((1, c, LANES), lambda j, i: (j, 0, 0)),
                  pl.BlockSpec((1, c, LANES), lambda j, i: (j, 0, 0)),
                  pl.BlockSpec((1, LANES, LANES), lambda j, i: (j, 0, 0)),
                  pl.BlockSpec((LANES, LANES), lambda j, i: (0, 0)), tab, tab],
        out_specs=[slab, slab, slab, vec],
        out_shape=[sd((s, RET_W), BF16), sd((s, RET_W), BF16), sd((s, RET_W), BF16), sd((1, RET_W), F32)],
        scratch_shapes=[pltpu.VMEM((LANES, LANES), F32)],
        compiler_params=_cp("parallel", "arbitrary"),
    )(q, k, v, o, g, dy, gnw, rc["dmask"], rc["zeta"], rc["xi"], rc["cd"], rc["bd"], cos_r, sin_r)


def _ret_bwd_kv_call(q, k, v, do, rc, cos_r, sin_r, tr):
    s = q.shape[0]
    c = RET_CHUNK
    nc = tr // c
    nt = s // tr

    def body(q_ref, k_ref, v_ref, do_ref, dm_ref, zeta_ref, xi_ref, cd_ref, bd_ref, cr_ref, sr_ref, dk_ref, dv_ref, gs_ref):
        _zero_first(pl.program_id(1) == 0, gs_ref)
        bd = bd_ref[...]
        chunks = [slice(ci * c, (ci + 1) * c) for ci in range(nc)]
        contrib = [_dot_tn((q_ref[rows, :].astype(F32) * xi_ref[0]).astype(BF16), do_ref[rows, :]) * bd for rows in chunks]
        gs, states = gs_ref[...], [None] * nc
        for ci in reversed(range(nc)):
            states[ci] = gs.astype(BF16)
            gs = gs * cd_ref[0] + contrib[ci]
        gs_ref[...] = gs
        for ci, rows in enumerate(chunks):
            kc, vc = k_ref[rows, :], v_ref[rows, :]
            q2, do2 = _stack_heads(q_ref[rows, :]), _stack_heads(do_ref[rows, :])
            gb = states[ci]
            dk = _dot_nt(vc, gb) * zeta_ref[0] + _pair_product(vc, do2, dm_ref[0], q2)
            dv = _dot(kc, gb) * zeta_ref[0] + _pair_product(kc, q2, dm_ref[0], do2)
            dk_ref[rows, :] = (_unrope(dk, cr_ref[rows, :], sr_ref[rows, :], HEAD // 2) * (HEAD ** -0.5)).astype(BF16)
            dv_ref[rows, :] = dv.astype(BF16)

    slab = pl.BlockSpec((tr, LANES), lambda j, i: (nt - 1 - i, j))
    tab = pl.BlockSpec((tr, LANES), lambda j, i: (nt - 1 - i, 0))
    sd = jax.ShapeDtypeStruct
    return pl.pallas_call(
        body, name="ret_bwd_kv", grid=(4, nt),
        in_specs=[slab, slab, slab, slab,
                  pl.BlockSpec((1, c, 2 * c), lambda j, i: (j, 0, 0)),
                  pl.BlockSpec((1, c, LANES), lambda j, i: (j, 0, 0)),
                  pl.BlockSpec((1, c, LANES), lambda j, i: (j, 0, 0)),
                  pl.BlockSpec((1, LANES, LANES), lambda j, i: (j, 0, 0)),
                  pl.BlockSpec((LANES, LANES), lambda j, i: (0, 0)), tab, tab],
        out_specs=[slab, slab],
        out_shape=[sd((s, RET_W), BF16), sd((s, RET_W), BF16)],
        scratch_shapes=[pltpu.VMEM((LANES, LANES), F32)],
        compiler_params=_cp("parallel", "arbitrary"),
    )(q, k, v, do, rc["dmask_t"], rc["zeta"], rc["xi"], rc["cd"], rc["bd"], cos_r, sin_r)


FLASH_BWD_HEADS = 4


def _flash_bwd_call(qb, k, v, do, tb, order=None):
    s = qb.shape[1]
    nb = s // tb
    hg = FLASH_BWD_HEADS

    def body(q_ref, k_ref, v_ref, do_ref, dk_ref, dv_ref, dq_hbm, dka_ref, dva_ref, dq_ref, sem):
        g, ki, qi = pl.program_id(0), pl.program_id(1), pl.program_id(2)
        _zero_first((ki == 0) & (qi == 0), dq_ref)
        _zero_first(qi == 0, dka_ref, dva_ref)
        rows = pl.ds(pl.multiple_of(qi * tb, tb), tb)

        def step(masked):
            if masked:
                keep = lax.broadcasted_iota(jnp.int32, (tb, tb), 0) <= lax.broadcasted_iota(jnp.int32, (tb, tb), 1)
            for h in range(hg):
                st = _dot_nt(k_ref[h], q_ref[h])
                if masked:
                    st = jnp.where(keep, st, NEG)
                pt = jnp.exp2(st)
                dob = do_ref[h]
                dva_ref[h] += _dot(pt.astype(BF16), dob)
                dst = (pt * _dot_nt(v_ref[h], dob)).astype(BF16)
                dka_ref[h] += _dot(dst, q_ref[h])
                dq_ref[h, rows, :] += _dot_tn(dst, k_ref[h])

        @pl.when(qi > ki)
        def _():
            step(False)

        @pl.when(qi == ki)
        def _():
            step(True)

        @pl.when(qi == nb - 1)
        def _():
            dk_ref[...] = (dka_ref[...] * LN2).astype(BF16)
            dv_ref[...] = dva_ref[...].astype(BF16)

        @pl.when((ki == nb - 1) & (qi == nb - 1))
        def _():
            cp = pltpu.make_async_copy(dq_ref, dq_hbm.at[pl.ds(g * hg, hg)], sem)
            cp.start()
            cp.wait()

    kspec = pl.BlockSpec((hg, tb, LANES), lambda g, ki, qi: (g, ki, 0))
    qspec = pl.BlockSpec((hg, tb, LANES), lambda g, ki, qi: (g, jnp.maximum(qi, ki), 0))
    hm = jax.ShapeDtypeStruct((N_HEADS, s, LANES), BF16)
    body, first_specs, first = _ordered_after(body, order)
    return pl.pallas_call(
        body, name="mla_flash_bwd", grid=(N_HEADS // hg, nb, nb),
        in_specs=first_specs + [qspec, kspec, kspec, qspec],
        out_specs=[kspec, kspec, ANY],
        out_shape=[hm, hm, jax.ShapeDtypeStruct((N_HEADS, s, LANES), F32)],
        scratch_shapes=[pltpu.VMEM((hg, tb, LANES), F32), pltpu.VMEM((hg, tb, LANES), F32),
                        pltpu.VMEM((hg, s, LANES), F32), pltpu.SemaphoreType.DMA],
        compiler_params=_cp("arbitrary", "arbitrary", "arbitrary"),
    )(*first, qb, k, v, do)


def _mla_post_call(dq, dk, dv, cq, ckv, qnw, kvnw, wq, wk, wv, cos_m, sin_m, ts):
    s = cq.shape[0]

    def body(dq_ref, dk_ref, dv_ref, cq_ref, ckv_ref, qnw_ref, kvnw_ref, wq_ref, wk_ref, wv_ref, cm_ref, sm_ref,
             dcq_ref, dckv_ref, dkpe_ref, dwq_ref, dwk_ref, dwv_ref, dqnw_ref, dkvnw_ref):
        _zero_first(pl.program_id(0) == 0, dwq_ref, dwk_ref, dwv_ref, dqnw_ref, dkvnw_ref)
        cqv, ckvv = cq_ref[...], ckv_ref[...]
        rq, rkv = _rstd(cqv), _rstd(ckvv)
        qh_, kvh_ = cqv * rq, ckvv * rkv
        qnw_v, kvnw_v = qnw_ref[...], kvnw_ref[...]
        cqn = (qh_ * qnw_v).astype(BF16)
        ckvn = (kvh_ * kvnw_v).astype(BF16)
        cm, sm = cm_ref[...], sm_ref[...]
        dcqn = jnp.zeros((ts, Q_RANK), F32)
        dckvn = jnp.zeros((ts, KV_RANK), F32)
        dkpe = jnp.zeros((ts, LANES), F32)
        for h in range(N_HEADS):
            dqu = _unrope(dq_ref[h] * SM_SCALE, cm, sm, ROPE // 2).astype(BF16)
            dwq_ref[h] += _dot_tn(cqn, dqu)
            dcqn = dcqn + _dot_nt(dqu, wq_ref[h])
            dkb, dvb = dk_ref[h], dv_ref[h]
            dkpe = dkpe + dkb.astype(F32)
            dwk_ref[h] += _dot_tn(ckvn, dkb)
            dwv_ref[h] += _dot_tn(ckvn, dvb)
            dckvn = dckvn + _dot_nt(dkb, wk_ref[h]) + _dot_nt(dvb, wv_ref[h])
        lane = lax.broadcasted_iota(jnp.int32, (ts, LANES), 1)
        dkpe = jnp.where((lane >= KPE_LO) & (lane < KPE_LO + ROPE), dkpe, 0.0)
        dkpe_ref[...] = _unrope(dkpe, cm, sm, ROPE // 2).astype(BF16)
        dqnw_ref[...] += _colsum(dcqn * qh_)
        dkvnw_ref[...] += _colsum(dckvn * kvh_)
        dcq_ref[...] = _norm_bwd(dcqn, qh_, rq, qnw_v).astype(BF16)
        dckv_ref[...] = _norm_bwd(dckvn, kvh_, rkv, kvnw_v).astype(BF16)

    sd = jax.ShapeDtypeStruct
    hm = _hrow(N_HEADS, ts, LANES)
    return pl.pallas_call(
        body, name="mla_post", grid=(s // ts,),
        in_specs=[hm, hm, hm, _row(ts, Q_RANK), _row(ts, KV_RANK), _full((1, Q_RANK)), _full((1, KV_RANK)),
                  _full((N_HEADS, Q_RANK, LANES)), _full((N_HEADS, KV_RANK, LANES)), _full((N_HEADS, KV_RANK, LANES)),
                  _row(ts, LANES), _row(ts, LANES)],
        out_specs=[_row(ts, Q_RANK), _row(ts, KV_RANK), _row(ts, LANES),
                   _full((N_HEADS, Q_RANK, LANES)), _full((N_HEADS, KV_RANK, LANES)), _full((N_HEADS, KV_RANK, LANES)),
                   _full((1, Q_RANK)), _full((1, KV_RANK))],
        out_shape=[sd((s, Q_RANK), BF16), sd((s, KV_RANK), BF16), sd((s, LANES), BF16),
                   sd((N_HEADS, Q_RANK, LANES), F32), sd((N_HEADS, KV_RANK, LANES), F32), sd((N_HEADS, KV_RANK, LANES), F32),
                   sd((1, Q_RANK), F32), sd((1, KV_RANK), F32)],
        compiler_params=_cp("arbitrary"),
    )(dq, dk, dv, cq, ckv, qnw, kvnw, wq, wk, wv, cos_m, sin_m)


def _in_bwd_call(parts, x, r1, anw, dx1, win, ts):
    s = x.shape[0]
    widths = [p.shape[1] for p in parts]
    np_ = len(parts)

    def body(*refs):
        p_refs = refs[:np_]
        x_ref, r_ref, anw_ref, dx1_ref, w_ref, dx_ref, dw_ref, danw_ref = refs[np_:]
        _zero_first(pl.program_id(0) == 0, dw_ref, danw_ref)
        dproj = jnp.concatenate([p[...] for p in p_refs], axis=-1)
        r, anw_v = r_ref[...], anw_ref[...]
        xh = x_ref[...] * r
        dw_ref[...] += _dot_tn((xh * anw_v).astype(BF16), dproj)
        dh = _dot_nt(dproj, w_ref[...])
        danw_ref[...] += _colsum(dh * xh)
        dx_ref[...] = dx1_ref[...] + _norm_bwd(dh, xh, r, anw_v)

    sd = jax.ShapeDtypeStruct
    return pl.pallas_call(
        body, name="in_proj_bwd", grid=(s // ts,),
        in_specs=[_row(ts, w) for w in widths]
        + [_row(ts, D_MODEL), _row(ts, 1), _full((1, D_MODEL)), _row(ts, D_MODEL), _full((D_MODEL, IN_EXT))],
        out_specs=[_row(ts, D_MODEL), _full((D_MODEL, IN_EXT)), _full((1, D_MODEL))],
        out_shape=[sd((s, D_MODEL), F32), sd((D_MODEL, IN_EXT), F32), sd((1, D_MODEL), F32)],
        compiler_params=_cp("arbitrary"),
    )(*parts, x, r1, anw, dx1, win)


def _local_step(x, positions, tgt, w, small, ex=None):
    s = x.shape[0]
    t = _tiles(s)
    ex = _Exchanges(w) if ex is None else ex
    f = _forward(x, positions, w, small, ex)
    pw, rc = f["pw"], f["rc"]
    cos_r, sin_r, cos_m, sin_m = f["tabs"]
    dx2, loss, g_fw = _loss_call(f["x2"], tgt, small["final_norm_w"], t["ts"])
    du, dx1, g_cw, g_cb, g_fnw, g_wd = _ffn_bwd_call(dx2, f["u"], w["conv_w"], small["conv_b"], pw["wdown"], pw["wup"],
                                                     f["x1"], f["r2"], small["ffn_norm_w"], t["t2"])
    g_wup = _dw_norm_call(f["x1"], f["r2"], small["ffn_norm_w"], du, t["ts"], F2 // 4, "dw_up")
    started = ex.mlp_grads(dict(w_up=g_wup, w_down=g_wd))
    dy_ret, do, g_wout = _out_bwd_call(dx1, f["y_ret"], f["y_mla"], pw["wout"], t["ts"], started)
    started = ex.behind_out_bwd(g_wout)
    drq, dg, do_ret, g_gnw = _ret_bwd_q_call(f["q"], f["k"], f["v"], f["o_ret"], f["g"], dy_ret, small["ret_gn_w"], rc, cos_r, sin_r, t["tr"])
    drk, drv = _ret_bwd_kv_call(f["q"], f["k"], f["v"], do_ret, rc, cos_r, sin_r, t["tr"])
    dmk, dmv, dmq = _flash_bwd_call(f["mqb"], f["mk"], f["mv"], do, t["tb"], started)
    ex.behind_attention(dmk)
    dcq, dckv, dkpe, g_wq, g_wk, g_wv, g_qnw, g_kvnw = _mla_post_call(
        dmq, dmk, dmv, f["cq"], f["ckv"], small["mla_q_norm_w"], small["mla_kv_norm_w"], pw["wq"], pw["wk"], pw["wv"], cos_m, sin_m, t["ts"])
    gx, g_win_ext, g_anw = _in_bwd_call([drq, drk, drv, dg, dcq, dckv, dkpe], x, f["r1"], small["attn_norm_w"], dx1, pw["win"], t["ts"])
    lo = IN_W - ROPE
    g_win = jnp.concatenate([g_win_ext[:, :lo], g_win_ext[:, lo + KPE_LO:lo + KPE_LO + ROPE]], -1)
    g_wuq = g_wq.transpose(1, 0, 2)[:, :, :HEAD + ROPE].reshape(Q_RANK, N_HEADS * (HEAD + ROPE))
    g_wukv = jnp.concatenate([g_wk[:, :, :HEAD], g_wv[:, :, :HEAD]], -1).transpose(1, 0, 2).reshape(KV_RANK, 2 * MLA_W)
    gw = dict(w_in=g_win, w_uq=g_wuq, w_ukv=g_wukv, w_out=g_wout, w_up=g_wup,
              conv_w=g_cw, w_down=g_wd)
    gs = dict(attn_norm_w=g_anw, ret_gn_w=g_gnw, mla_q_norm_w=g_qnw, mla_kv_norm_w=g_kvnw, ffn_norm_w=g_fnw,
              conv_b=g_cb, final_norm_w=g_fw)
    return loss, gx, gw, gs


MESH_ID = pl.DeviceIdType.MESH
ANY = pl.BlockSpec(memory_space=pl.ANY)
VMEM_SPEC = pl.BlockSpec(memory_space=pltpu.VMEM)
N_DEV = 8
GROUP_A = (("w_in", (D_MODEL, IN_W // 4), 1), ("w_uq", (Q_RANK, 192), 1), ("w_ukv", (KV_RANK, 256), 1),
           ("w_out", (D_MODEL // 4, D_MODEL), 0))
GROUP_B = (("w_up", (D_MODEL, F2 // 4), 1), ("w_down", (D_FF // 4, D_MODEL), 0))


def _group_rows(group):
    n = sum(r * c for _, (r, c), _ in group)
    assert n % (2 * 16 * LANES) == 0
    return n // LANES


def _add_tile(half_rows):
    return max(t for t in range(16, 2049, 16) if half_rows % t == 0)


def _mesh_pos():
    return lax.axis_index("x"), lax.axis_index("y"), lax.axis_index("c")


def _other_chips(x, y):
    return [(1 - x, y), (x, 1 - y), (1 - x, 1 - y)]


def _pack(parts, dtype):
    flat = jnp.concatenate([p.reshape(-1).astype(dtype) for p in parts])
    return flat.reshape(2, flat.shape[0] // (2 * LANES), LANES)


def _unpack(flat, group):
    out, off = [], 0
    for _, (r, c), _ in group:
        out.append(flat[..., off:off + r * c].reshape(flat.shape[:-1] + (r, c)))
        off += r * c
    return out


def _full_weights(gathered, group):
    full = {}
    for (n, (r, c), axis), piece in zip(group, _unpack(gathered.reshape(4, -1), group)):
        if n == "w_up":
            full[n] = piece
        else:
            full[n] = piece.transpose(1, 0, 2).reshape(r, 4 * c) if axis == 1 else piece.reshape(4 * r, c)
    return full


def _grad_shards(gw, group):
    shards = []
    for n, (r, c), axis in group:
        g = gw[n]
        if axis == 1 and g.ndim == 2:
            g = g.reshape(r, 4, c).transpose(1, 0, 2)
        shards.append(g.reshape(4, r * c))
    flat = jnp.concatenate(shards, axis=1)
    return flat.reshape(4, 2, flat.shape[1] // (2 * LANES), LANES)


def _all_gather_call(packed, tag):
    _, h, _ = packed.shape

    def body(src_ref, out_ref, send_sems, recv_sems):
        x, y, c = _mesh_pos()
        sm = 2 * x + y
        chips = _other_chips(x, y)
        sib = (x, y, 1 - c)

        def rcopy(k, src, dst, dev):
            return pltpu.make_async_remote_copy(src_ref=src, dst_ref=dst, send_sem=send_sems.at[k], recv_sem=recv_sems.at[k],
                                                device_id=dev, device_id_type=MESH_ID)

        first = [rcopy(j, src_ref.at[c], out_ref.at[sm, c], (cx, cy, c)) for j, (cx, cy) in enumerate(chips)]
        own = rcopy(6, src_ref, out_ref.at[sm], sib)
        for cp in first + [own]:
            cp.start()
        passed = []
        for j, (cx, cy) in enumerate(chips):
            sj = 2 * cx + cy
            rcopy(j, src_ref.at[c], out_ref.at[sj, c], (cx, cy, c)).wait_recv()
            cp = rcopy(3 + j, out_ref.at[sj, c], out_ref.at[sj, c], sib)
            cp.start()
            passed.append(cp)
        for j, (cx, cy) in enumerate(chips):
            rcopy(3 + j, src_ref.at[c], out_ref.at[2 * cx + cy, 1 - c], sib).wait_recv()
        own.wait_recv()
        for cp in first + passed + [own]:
            cp.wait_send()

    return pl.pallas_call(
        body, name="weights_all_gather_" + tag,
        in_specs=[ANY], out_specs=ANY,
        out_shape=jax.ShapeDtypeStruct((4, 2, h, LANES), packed.dtype),
        scratch_shapes=[pltpu.SemaphoreType.DMA((7,)), pltpu.SemaphoreType.DMA((7,))],
    )(packed)


HBM_SPEC = pl.BlockSpec(memory_space=pltpu.HBM)
SEM_SPEC = pl.BlockSpec(memory_space=pltpu.SEMAPHORE)
N_GATHER_SEMS = 13


def _gather_copies(src_ref, land_ref, send_sems, recv_sems):
    x, y, c = _mesh_pos()
    sm = 2 * x + y

    def rcopy(k, src, dst, dev):
        return pltpu.make_async_remote_copy(src_ref=src, dst_ref=dst, send_sem=send_sems.at[k], recv_sem=recv_sems.at[k],
                                            device_id=dev, device_id_type=MESH_ID)

    sends, recvs = [], []
    for j, (cx, cy) in enumerate(_other_chips(x, y)):
        for t in range(2):
            sends.append(rcopy(4 * j + 2 * c + t, src_ref.at[c], land_ref.at[sm, c], (cx, cy, t)))
            recvs.append(rcopy(4 * j + 2 * t + c, src_ref.at[t], land_ref.at[2 * cx + cy, t], (cx, cy, t)))
    sends.append(rcopy(12, src_ref, land_ref.at[sm], (x, y, 1 - c)))
    recvs.append(rcopy(12, src_ref, land_ref.at[sm], (x, y, 1 - c)))
    return sends, recvs


def _gather_start_call(packed, tag):
    _, h, _ = packed.shape

    def body(src_ref, land_ref, send_sems, recv_sems, src_thru, land_thru, token_ref):
        sends, _ = _gather_copies(src_ref, land_ref, send_sems, recv_sems)
        for cp in sends:
            cp.start()
        token_ref[...] = jnp.zeros_like(token_ref)

    land = pltpu.with_memory_space_constraint(lax.empty((4, 2, h, LANES), packed.dtype), pltpu.HBM)
    return pl.pallas_call(
        body, name="weights_gather_start_" + tag,
        out_shape=(pltpu.SemaphoreType.DMA((N_GATHER_SEMS,)), pltpu.SemaphoreType.DMA((N_GATHER_SEMS,)),
                   pltpu.HBM(packed.shape, packed.dtype), pltpu.HBM(land.shape, land.dtype), jax.ShapeDtypeStruct((8, LANES), F32)),
        in_specs=(HBM_SPEC, HBM_SPEC), out_specs=(SEM_SPEC, SEM_SPEC, HBM_SPEC, HBM_SPEC, VMEM_SPEC),
        input_output_aliases={0: 2, 1: 3},
        compiler_params=pltpu.CompilerParams(has_side_effects=pltpu.SideEffectType.DATAFLOW_SIDE_EFFECTING),
    )(pltpu.with_memory_space_constraint(packed, pltpu.HBM), land)


def _gather_wait_call(send_sems, recv_sems, src_thru, land_thru, after, tag):
    def body(src_ref, land_ref, send_sems, recv_sems, after_ref, src_dead, got_ref):
        sends, recvs = _gather_copies(src_ref, land_ref, send_sems, recv_sems)
        for cp in sends:
            cp.wait_send()
        for cp in recvs:
            cp.wait_recv()

    return pl.pallas_call(
        body, name="weights_gather_wait_" + tag,
        out_shape=(pltpu.HBM(src_thru.shape, src_thru.dtype), pltpu.HBM(land_thru.shape, land_thru.dtype)),
        in_specs=(HBM_SPEC, HBM_SPEC, SEM_SPEC, SEM_SPEC, ANY), out_specs=(HBM_SPEC, HBM_SPEC),
        input_output_aliases={0: 0, 1: 1},
        compiler_params=pltpu.CompilerParams(has_side_effects=pltpu.SideEffectType.DATAFLOW_SIDE_EFFECTING),
    )(src_thru, land_thru, send_sems, recv_sems, after)[1]


def _rs_sibling_call(g, tag):
    _, _, h, _ = g.shape

    def body(g_ref, buf_ref, send_sems, recv_sems):
        x, y, c = _mesh_pos()
        cps = [pltpu.make_async_remote_copy(src_ref=g_ref.at[s, 1 - c], dst_ref=buf_ref.at[s], send_sem=send_sems.at[s],
                                            recv_sem=recv_sems.at[s], device_id=(x, y, 1 - c), device_id_type=MESH_ID)
               for s in range(4)]
        for cp in cps:
            cp.start()
        for cp in cps:
            cp.wait()

    return pl.pallas_call(
        body, name="grads_rs_sibling_" + tag,
        in_specs=[ANY], out_specs=ANY,
        out_shape=jax.ShapeDtypeStruct((4, h, LANES), g.dtype),
        scratch_shapes=[pltpu.SemaphoreType.DMA((4,)), pltpu.SemaphoreType.DMA((4,))],
    )(g)


def _rs_add1_call(g, buf, c, tag):
    _, _, h, _ = g.shape
    tile = _add_tile(h)

    def body(c_ref, g_ref, b_ref, p_ref, pb_ref):
        p = g_ref[...] + b_ref[...]
        p_ref[...] = p
        pb_ref[...] = p.astype(BF16)

    blk = pl.BlockSpec((None, tile, LANES), lambda s, i, c_ref: (s, i, 0))
    return pl.pallas_call(
        body, name="grads_rs_add_sibling_" + tag,
        grid_spec=pltpu.PrefetchScalarGridSpec(
            num_scalar_prefetch=1, grid=(4, h // tile),
            in_specs=[pl.BlockSpec((None, None, tile, LANES), lambda s, i, c_ref: (s, c_ref[0], i, 0)), blk],
            out_specs=[blk, blk]),
        out_shape=[jax.ShapeDtypeStruct((4, h, LANES), F32), jax.ShapeDtypeStruct((4, h, LANES), BF16)],
        compiler_params=_cp("parallel", "parallel"),
    )(c, g, buf)


def _rs_chips_call(pb, tag):
    _, h, _ = pb.shape

    def body(pb_ref, buf_ref, send_sems, recv_sems):
        x, y, c = _mesh_pos()
        cps = [pltpu.make_async_remote_copy(src_ref=pb_ref.at[2 * cx + cy], dst_ref=buf_ref.at[j], send_sem=send_sems.at[j],
                                            recv_sem=recv_sems.at[j], device_id=(cx, cy, c), device_id_type=MESH_ID)
               for j, (cx, cy) in enumerate(_other_chips(x, y))]
        for cp in cps:
            cp.start()
        for cp in cps:
            cp.wait()

    return pl.pallas_call(
        body, name="grads_rs_chips_" + tag,
        in_specs=[ANY], out_specs=ANY,
        out_shape=jax.ShapeDtypeStruct((3, h, LANES), pb.dtype),
        scratch_shapes=[pltpu.SemaphoreType.DMA((3,)), pltpu.SemaphoreType.DMA((3,))],
    )(pb)


def _rs_add2_call(p, buf, sm, tag):
    _, h, _ = p.shape
    tile = _add_tile(h)

    def body(sm_ref, p_ref, b_ref, f_ref):
        f_ref[...] = ((p_ref[...] + b_ref[0].astype(F32)) + b_ref[1].astype(F32)) + b_ref[2].astype(F32)

    return pl.pallas_call(
        body, name="grads_rs_add_chips_" + tag,
        grid_spec=pltpu.PrefetchScalarGridSpec(
            num_scalar_prefetch=1, grid=(h // tile,),
            in_specs=[pl.BlockSpec((None, tile, LANES), lambda i, sm_ref: (sm_ref[0], i, 0)),
                      pl.BlockSpec((3, tile, LANES), lambda i, sm_ref: (0, i, 0))],
            out_specs=pl.BlockSpec((tile, LANES), lambda i, sm_ref: (i, 0))),
        out_shape=jax.ShapeDtypeStruct((h, LANES), F32),
        compiler_params=_cp("parallel"),
    )(sm, p, buf)


def _rs_share_call(f, tag):
    h, _ = f.shape

    def body(f_ref, out_ref, send_sem, recv_sem):
        x, y, c = _mesh_pos()
        cp = pltpu.make_async_remote_copy(src_ref=f_ref, dst_ref=out_ref, send_sem=send_sem, recv_sem=recv_sem,
                                          device_id=(x, y, 1 - c), device_id_type=MESH_ID)
        cp.start()
        cp.wait()

    return pl.pallas_call(
        body, name="grads_rs_share_" + tag,
        in_specs=[ANY], out_specs=ANY,
        out_shape=jax.ShapeDtypeStruct((h, LANES), f.dtype),
        scratch_shapes=[pltpu.SemaphoreType.DMA, pltpu.SemaphoreType.DMA],
    )(f)


def _remote(src, dst, send_sems, recv_sems, k, dev):
    return pltpu.make_async_remote_copy(src_ref=src, dst_ref=dst, send_sem=send_sems.at[k], recv_sem=recv_sems.at[k],
                                        device_id=dev, device_id_type=MESH_ID)


def _gather_list_call(parts, tag):
    n = len(parts)

    def body(*refs):
        srcs, outs, (send_sems, recv_sems) = refs[:n], refs[n:2 * n], refs[2 * n:]
        x, y, c = _mesh_pos()
        sm = 2 * x + y
        chips = _other_chips(x, y)
        sib = (x, y, 1 - c)
        rc = lambda k, src, dst, dev: _remote(src, dst, send_sems, recv_sems, k, dev)
        first = [rc(7 * i + j, srcs[i].at[c], outs[i].at[sm, c], (cx, cy, c)) for i in range(n) for j, (cx, cy) in enumerate(chips)]
        own = [rc(7 * i + 6, srcs[i], outs[i].at[sm], sib) for i in range(n)]
        for cp in first + own:
            cp.start()
        passed = []
        for j, (cx, cy) in enumerate(chips):
            for i in range(n):
                land = outs[i].at[2 * cx + cy, c]
                rc(7 * i + j, srcs[i].at[c], land, (cx, cy, c)).wait_recv()
                cp = rc(7 * i + 3 + j, land, land, sib)
                cp.start()
                passed.append(cp)
        for j, (cx, cy) in enumerate(chips):
            for i in range(n):
                rc(7 * i + 3 + j, srcs[i].at[c], outs[i].at[2 * cx + cy, 1 - c], sib).wait_recv()
        for cp in own:
            cp.wait_recv()
        for cp in first + passed + own:
            cp.wait_send()

    return pl.pallas_call(
        body, name="weights_all_gather_" + tag,
        in_specs=[ANY] * n, out_specs=[ANY] * n,
        out_shape=[jax.ShapeDtypeStruct((4,) + p.shape, p.dtype) for p in parts],
        scratch_shapes=[pltpu.SemaphoreType.DMA((7 * n,)), pltpu.SemaphoreType.DMA((7 * n,))],
    )(*parts)


def _direct_gather_copies(srcs, lands, send_sems, recv_sems):
    x, y, c = _mesh_pos()
    sm = 2 * x + y
    sends, recvs = [], []
    for i, (src, land) in enumerate(zip(srcs, lands)):
        for j, (cx, cy) in enumerate(_other_chips(x, y)):
            for t in range(2):
                sends.append(_remote(src.at[c], land.at[sm, c], send_sems, recv_sems, 13 * i + 4 * j + 2 * c + t, (cx, cy, t)))
                recvs.append(_remote(src.at[t], land.at[2 * cx + cy, t], send_sems, recv_sems, 13 * i + 4 * j + 2 * t + c, (cx, cy, t)))
        sends.append(_remote(src, land.at[sm], send_sems, recv_sems, 13 * i + 12, (x, y, 1 - c)))
        recvs.append(_remote(src, land.at[sm], send_sems, recv_sems, 13 * i + 12, (x, y, 1 - c)))
    return sends, recvs


def _sibling_copies(srcs, lands, send_sems, recv_sems):
    x, y, c = _mesh_pos()
    cps = [_remote(src.at[s, 1 - c], land.at[s], send_sems, recv_sems, 4 * i + s, (x, y, 1 - c))
           for i, (src, land) in enumerate(zip(srcs, lands)) for s in range(4)]
    return cps, cps


def _chips_copies(srcs, lands, send_sems, recv_sems):
    x, y, c = _mesh_pos()
    cps = [_remote(src.at[2 * cx + cy], land.at[j], send_sems, recv_sems, 3 * i + j, (cx, cy, c))
           for i, (src, land) in enumerate(zip(srcs, lands)) for j, (cx, cy) in enumerate(_other_chips(x, y))]
    return cps, cps


def _share_copies(srcs, lands, send_sems, recv_sems):
    x, y, c = _mesh_pos()
    cps = [_remote(src, land, send_sems, recv_sems, i, (x, y, 1 - c)) for i, (src, land) in enumerate(zip(srcs, lands))]
    return cps, cps


def _exchange_call(name, copies, srcs, land_shapes, n_sems):
    n = len(srcs)

    def body(*refs):
        sends, recvs = copies(refs[:n], refs[n:2 * n], refs[2 * n], refs[2 * n + 1])
        for cp in sends:
            cp.start()
        for cp in sends:
            cp.wait_send()
        for cp in recvs:
            cp.wait_recv()

    return pl.pallas_call(
        body, name=name, in_specs=[ANY] * n, out_specs=[ANY] * n, out_shape=list(land_shapes),
        scratch_shapes=[pltpu.SemaphoreType.DMA((n_sems,)), pltpu.SemaphoreType.DMA((n_sems,))],
    )(*srcs)


def _exchange_start_call(name, copies, srcs, land_shapes, n_sems):
    n = len(srcs)

    def body(*refs):
        sends, _ = copies(refs[:n], refs[n:2 * n], refs[2 * n], refs[2 * n + 1])
        for cp in sends:
            cp.start()
        refs[-1][...] = jnp.zeros_like(refs[-1])

    hbm = lambda a: pltpu.with_memory_space_constraint(a, pltpu.HBM)
    lands = [hbm(lax.empty(sd.shape, sd.dtype)) for sd in land_shapes]
    sem = pltpu.SemaphoreType.DMA((n_sems,))
    out = pl.pallas_call(
        body, name=name,
        out_shape=(sem, sem, *[pltpu.HBM(a.shape, a.dtype) for a in list(srcs) + lands], jax.ShapeDtypeStruct((8, LANES), F32)),
        in_specs=[HBM_SPEC] * (2 * n), out_specs=(SEM_SPEC, SEM_SPEC, *[HBM_SPEC] * (2 * n), VMEM_SPEC),
        input_output_aliases={i: 2 + i for i in range(2 * n)},
        compiler_params=pltpu.CompilerParams(has_side_effects=pltpu.SideEffectType.DATAFLOW_SIDE_EFFECTING),
    )(*[hbm(a) for a in srcs], *lands)
    return out[0], out[1], out[2:2 + n], out[2 + n:2 + 2 * n], out[-1]


def _exchange_wait_call(name, copies, started, after):
    send_sems, recv_sems, srcs, lands, _ = started
    n = len(srcs)

    def body(*refs):
        sends, recvs = copies(refs[:n], refs[n:2 * n], refs[2 * n], refs[2 * n + 1])
        for cp in sends:
            cp.wait_send()
        for cp in recvs:
            cp.wait_recv()

    out = pl.pallas_call(
        body, name=name,
        out_shape=tuple(pltpu.HBM(a.shape, a.dtype) for a in list(srcs) + list(lands)),
        in_specs=[HBM_SPEC] * (2 * n) + [SEM_SPEC, SEM_SPEC, ANY], out_specs=tuple([HBM_SPEC] * (2 * n)),
        input_output_aliases={i: i for i in range(2 * n)},
        compiler_params=pltpu.CompilerParams(has_side_effects=pltpu.SideEffectType.DATAFLOW_SIDE_EFFECTING),
    )(*srcs, *lands, send_sems, recv_sems, after)
    return out[:n], out[n:]


def _rows_tile(rows, width, itemsize=4):
    limit = max(16, (3 << 20) // (width * itemsize))
    if rows <= limit:
        return rows
    return max(t for t in range(16, limit + 1, 16) if rows % t == 0)


def _sum_sibling_call(g, buf, c, name):
    _, _, rh, w = g.shape
    tile = _rows_tile(rh, w)

    def body(c_ref, g_ref, b_ref, p_ref, pb_ref):
        p = g_ref[...] + b_ref[...]
        p_ref[...] = p
        pb_ref[...] = p.astype(BF16)

    blk = pl.BlockSpec((None, tile, w), lambda s, i, c_ref: (s, i, 0))
    return pl.pallas_call(
        body, name=name,
        grid_spec=pltpu.PrefetchScalarGridSpec(
            num_scalar_prefetch=1, grid=(4, rh // tile),
            in_specs=[pl.BlockSpec((None, None, tile, w), lambda s, i, c_ref: (s, c_ref[0], i, 0)), blk],
            out_specs=[blk, blk]),
        out_shape=[jax.ShapeDtypeStruct((4, rh, w), F32), jax.ShapeDtypeStruct((4, rh, w), BF16)],
        compiler_params=_cp("parallel", "parallel"),
    )(c, g, buf)


def _sum_chips_call(p, buf, sm, name):
    _, rh, w = p.shape
    tile = _rows_tile(rh, w)

    def body(sm_ref, p_ref, b_ref, f_ref):
        f_ref[...] = ((p_ref[...] + b_ref[0].astype(F32)) + b_ref[1].astype(F32)) + b_ref[2].astype(F32)

    return pl.pallas_call(
        body, name=name,
        grid_spec=pltpu.PrefetchScalarGridSpec(
            num_scalar_prefetch=1, grid=(rh // tile,),
            in_specs=[pl.BlockSpec((None, tile, w), lambda i, sm_ref: (sm_ref[0], i, 0)),
                      pl.BlockSpec((3, tile, w), lambda i, sm_ref: (0, i, 0))],
            out_specs=pl.BlockSpec((tile, w), lambda i, sm_ref: (i, 0))),
        out_shape=jax.ShapeDtypeStruct((rh, w), F32),
        compiler_params=_cp("parallel"),
    )(sm, p, buf)


def _adamw_halves_call(w, g_mine, g_sib, c, m, v, name):
    r, wd = w.shape
    rh = r // 2
    tile = _rows_tile(rh, wd)
    nt = rh // tile

    def body(c_ref, w_ref, gm_ref, gs_ref, m_ref, v_ref, g_ref, d_ref, nm_ref, nv_ref):
        gv = jnp.where(pl.program_id(0) == c_ref[0], gm_ref[...], gs_ref[...])
        g_ref[...] = gv
        nm = ADAM_B1 * m_ref[...] + (1.0 - ADAM_B1) * gv
        nv = ADAM_B2 * v_ref[...] + (1.0 - ADAM_B2) * jnp.square(gv)
        m_hat = nm / (1.0 - ADAM_B1 ** ADAM_STEP)
        v_hat = nv / (1.0 - ADAM_B2 ** ADAM_STEP)
        d_ref[...] = -ADAM_LR * (m_hat / (jnp.sqrt(v_hat) + ADAM_EPS) + ADAM_WD * w_ref[...])
        nm_ref[...] = nm
        nv_ref[...] = nv

    whole = pl.BlockSpec((tile, wd), lambda h, i, c_ref: (h * nt + i, 0))
    half = pl.BlockSpec((tile, wd), lambda h, i, c_ref: (i, 0))
    sd = jax.ShapeDtypeStruct((r, wd), F32)
    return pl.pallas_call(
        body, name=name,
        grid_spec=pltpu.PrefetchScalarGridSpec(
            num_scalar_prefetch=1, grid=(2, nt),
            in_specs=[whole, half, half, whole, whole], out_specs=[whole] * 4),
        out_shape=[sd, sd, sd, sd],
        compiler_params=_cp("parallel", "parallel"),
    )(c, w, g_mine, g_sib, m, v)


def _exchange8_call(vec, reduce, name):
    rows = vec.shape[0]

    def body(v_ref, out_ref, *rest):
        slots, send_sems, recv_sems = (rest if reduce else (out_ref,) + rest)
        x, y, c = _mesh_pos()
        me = 4 * x + 2 * y + c
        slots[me] = v_ref[...]

        def rcopy(k, to_me):
            bx, by, bc = (k >> 2) & 1, (k >> 1) & 1, k & 1
            px, py, pc = (1 - x if bx else x), (1 - y if by else y), (1 - c if bc else c)
            slot = 4 * px + 2 * py + pc if to_me else me
            return pltpu.make_async_remote_copy(src_ref=v_ref, dst_ref=slots.at[slot], send_sem=send_sems.at[k - 1],
                                                recv_sem=recv_sems.at[k - 1], device_id=(px, py, pc), device_id_type=MESH_ID)

        for k in range(1, N_DEV):
            rcopy(k, False).start()
        for k in range(1, N_DEV):
            rcopy(k, True).wait_recv()
        for k in range(1, N_DEV):
            rcopy(k, False).wait_send()
        if reduce:
            tot = slots[0]
            for d in range(1, N_DEV):
                tot = tot + slots[d]
            out_ref[...] = tot

    stack = jax.ShapeDtypeStruct((N_DEV, rows, LANES), F32)
    return pl.pallas_call(
        body, name=name,
        in_specs=[VMEM_SPEC], out_specs=VMEM_SPEC,
        out_shape=jax.ShapeDtypeStruct((rows, LANES), F32) if reduce else stack,
        scratch_shapes=([pltpu.VMEM((N_DEV, rows, LANES), F32)] if reduce else [])
        + [pltpu.SemaphoreType.DMA((N_DEV - 1,)), pltpu.SemaphoreType.DMA((N_DEV - 1,))],
    )(vec)


def _adamw_call(w, g, m, v, name):
    r, c = w.shape
    rb = r if r <= 256 else (256 if r % 256 == 0 else 352)
    assert r % rb == 0

    def body(w_ref, g_ref, m_ref, v_ref, d_ref, nm_ref, nv_ref):
        gv = g_ref[...]
        nm = ADAM_B1 * m_ref[...] + (1.0 - ADAM_B1) * gv
        nv = ADAM_B2 * v_ref[...] + (1.0 - ADAM_B2) * jnp.square(gv)
        m_hat = nm / (1.0 - ADAM_B1 ** ADAM_STEP)
        v_hat = nv / (1.0 - ADAM_B2 ** ADAM_STEP)
        d_ref[...] = -ADAM_LR * (m_hat / (jnp.sqrt(v_hat) + ADAM_EPS) + ADAM_WD * w_ref[...])
        nm_ref[...] = nm
        nv_ref[...] = nv

    spec = pl.BlockSpec((rb, c), lambda i: (i, 0))
    sd = jax.ShapeDtypeStruct((r, c), F32)
    return pl.pallas_call(
        body, name=name, grid=(r // rb,),
        in_specs=[spec] * 4, out_specs=[spec] * 3, out_shape=[sd, sd, sd],
        compiler_params=_cp("parallel"),
    )(w, g, m, v)


SMALL = (("attn_norm_w", D_MODEL), ("ret_gn_w", RET_W), ("mla_q_norm_w", Q_RANK), ("mla_kv_norm_w", KV_RANK),
         ("ffn_norm_w", D_MODEL), ("conv_b", F2), ("final_norm_w", D_MODEL))
WEIGHT_ORDER = ("attn_norm_w", "w_in", "ret_gn_w", "mla_q_norm_w", "w_uq", "mla_kv_norm_w", "w_ukv", "w_out",
                "ffn_norm_w", "w_up", "conv_w", "conv_b", "w_down", "final_norm_w")


def _pad_rows(flat, rows):
    return jnp.concatenate([flat, jnp.zeros((rows * LANES - flat.shape[0],), flat.dtype)]).reshape(rows, LANES)


def kernel(x, positions, attn_norm_w, w_in, ret_gn_w, mla_q_norm_w, w_uq, mla_kv_norm_w, w_ukv, w_out, ffn_norm_w, w_up, conv_w, conv_b, w_down, final_norm_w, loss_target, m_attn_norm_w, m_w_in, m_ret_gn_w, m_mla_q_norm_w, m_w_uq, m_mla_kv_norm_w, m_w_ukv, m_w_out, m_ffn_norm_w, m_w_up, m_conv_w, m_conv_b, m_w_down, m_final_norm_w, v_attn_norm_w, v_w_in, v_ret_gn_w, v_mla_q_norm_w, v_w_uq, v_mla_kv_norm_w, v_w_ukv, v_w_out, v_ffn_norm_w, v_w_up, v_conv_w, v_conv_b, v_w_down, v_final_norm_w):
    args = dict(locals())
    cx, cy, cc = _mesh_pos()
    sm = 2 * cx + cy

    c_arr, sm_arr = cc.reshape(1).astype(jnp.int32), sm.reshape(1).astype(jnp.int32)
    sds = jax.ShapeDtypeStruct

    def my_shards(group):
        return [args[n][0].astype(BF16).reshape(2, r // 2, c) for n, (r, c), _ in group]

    def full_weights(gathered, group):
        full = {}
        for (n, (r, c), axis), got in zip(group, gathered):
            piece = got.reshape(4, r, c)
            full[n] = piece if n == "w_up" else (piece.transpose(1, 0, 2).reshape(r, 4 * c) if axis == 1 else piece.reshape(4 * r, c))
        return full

    def by_owner(gw, group):
        out = []
        for n, (r, c), axis in group:
            g = gw[n]
            if axis == 1 and g.ndim == 2:
                g = g.reshape(r, 4, c).transpose(1, 0, 2)
            out.append(g.reshape(4, 2, r // 2, c))
        return out

    def sibling_shapes(gs):
        return [sds((4,) + g.shape[2:], F32) for g in gs]

    def chip_sums(gs, bufs, group):
        res = [_sum_sibling_call(g, b, c_arr, "grads_sum_sibling_" + n) for g, b, (n, _, _) in zip(gs, bufs, group)]
        return [p for p, _ in res], [pb for _, pb in res]

    def chips_shapes(pbs):
        return [sds((3,) + pb.shape[1:], BF16) for pb in pbs]

    def totals(ps, lands, group, tag):
        fins = [_sum_chips_call(p, l, sm_arr, "grads_sum_chips_" + n) for p, l, (n, _, _) in zip(ps, lands, group)]
        sibs = _exchange_call("grads_rs_share_" + tag, _share_copies, fins, [sds(f.shape, F32) for f in fins], len(fins))
        return {n: (f, s) for (n, _, _), f, s in zip(group, fins, sibs)}

    class StepExchanges(_Exchanges):
        def __init__(self):
            shards = my_shards(GROUP_B)
            self.gather = _exchange_start_call("weights_gather_start_b", _direct_gather_copies, shards,
                                               [sds((4,) + s.shape, BF16) for s in shards], 13 * len(shards))
            self.red = None

        def token(self):
            return self.gather[4][0:1, 0:1]

        def mlp_weights(self, after):
            return full_weights(_exchange_wait_call("weights_gather_wait_b", _direct_gather_copies, self.gather, after)[1], GROUP_B)

        def mlp_grads(self, gw):
            gs = by_owner(gw, GROUP_B)
            self.step1 = _exchange_start_call("grads_rs_sibling_start_b", _sibling_copies, gs, sibling_shapes(gs), 4 * len(gs))
            return self.step1[4]

        def behind_out_bwd(self, after):
            gs, bufs = _exchange_wait_call("grads_rs_sibling_wait_b", _sibling_copies, self.step1, after)
            self.ps, pbs = chip_sums(gs, bufs, GROUP_B)
            self.step2 = _exchange_start_call("grads_rs_chips_start_b", _chips_copies, pbs, chips_shapes(pbs), 3 * len(pbs))
            return self.step2[4]

        def behind_attention(self, after):
            _, lands = _exchange_wait_call("grads_rs_chips_wait_b", _chips_copies, self.step2, after)
            self.red = totals(self.ps, lands, GROUP_B, "b")

    ex = StepExchanges()
    full = full_weights(_gather_list_call(my_shards(GROUP_A), "a"), GROUP_A)
    cw_rows = 40
    cw_all = _exchange8_call(_pad_rows(conv_w[0].reshape(-1), cw_rows), False, "conv_w_all_gather")
    cw_all = cw_all[0::2].reshape(4, cw_rows * LANES)[:, :3 * F2 // 4].reshape(4, 3, F2 // 4)
    full["conv_w"] = cw_all.transpose(1, 0, 2).reshape(3, F2)
    small = {n: args[n].reshape(1, d) for n, d in SMALL}
    small["attn_norm_w"] = small["attn_norm_w"] + ex.token()

    loss, gx, gw, gs = _local_step(x[0], positions[0], loss_target[0], full, small, ex)

    ga = by_owner(gw, GROUP_A)
    bufs = _exchange_call("grads_rs_sibling_a", _sibling_copies, ga, sibling_shapes(ga), 4 * len(ga))
    ps, pbs = chip_sums(ga, bufs, GROUP_A)
    lands = _exchange_call("grads_rs_chips_a", _chips_copies, pbs, chips_shapes(pbs), 3 * len(pbs))
    halves = {**ex.red, **totals(ps, lands, GROUP_A, "a")}

    vec = jnp.concatenate([gs[n].reshape(-1) for n, _ in SMALL] + [gw["conv_w"].reshape(-1), loss.reshape(-1)])
    tot = _exchange8_call(_pad_rows(vec, 216), True, "small_all_reduce").reshape(-1)
    red, off = {}, 0
    for n, d in SMALL:
        red[n] = tot[off:off + d].reshape(1, d)
        off += d
    red["conv_w"] = lax.dynamic_slice(tot[off:off + 3 * F2].reshape(3, F2), (0, sm * (F2 // 4)), (3, F2 // 4))
    loss_tot = tot[off + 3 * F2]

    grads, deltas, new_m, new_v = [], [], [], []
    for n in WEIGHT_ORDER:
        shape = args[n].shape
        two_d = (1, shape[0]) if len(shape) == 1 else shape[-2:]
        wmv = [args[k + n].reshape(two_d) for k in ("", "m_", "v_")]
        if n in halves:
            g, d, nm, nv = _adamw_halves_call(wmv[0], *halves[n], c_arr, wmv[1], wmv[2], "adamw_" + n)
        else:
            g = red[n].reshape(two_d)
            d, nm, nv = _adamw_call(wmv[0], g, wmv[1], wmv[2], "adamw_" + n)
        grads.append(g.reshape(shape))
        deltas.append(d.reshape(shape))
        new_m.append(nm.reshape(shape))
        new_v.append(nv.reshape(shape))
    return (loss_tot, gx[None], *grads, *deltas, *new_m, *new_v)
```

```python
import functools
import math

import numpy as np
import jax
import jax.numpy as jnp
from jax import lax
from jax.experimental import pallas as pl
from jax.experimental.pallas import tpu as pltpu

F32 = jnp.float32
BF16 = jnp.bfloat16

D_MODEL = 1024
N_HEADS = 8
HEAD = 64
RET_W = N_HEADS * HEAD
MLA_W = N_HEADS * HEAD
ROPE = 32
Q_RANK = 256
KV_RANK = 128
D_FF = 2816
F2 = 2 * D_FF
IN_W = 4 * RET_W + Q_RANK + KV_RANK + ROPE
IN_EXT = 4 * RET_W + Q_RANK + KV_RANK + 128
KPE_LO = 64
ROPE_BASE = 10000.0
EPS = 1e-6
RET_CHUNK = 128
SM_SCALE = (HEAD + ROPE) ** -0.5
LOG2E = math.log2(math.e)
LN2 = math.log(2.0)
NEG = -1e30
LANES = 128
VMEM_LIMIT = 56 * 1024 * 1024

ADAM_LR = 0.001
ADAM_B1 = 0.9
ADAM_B2 = 0.999
ADAM_EPS = 1e-08
ADAM_WD = 0.01
ADAM_STEP = 10


def _cp(*sem):
    return pltpu.CompilerParams(dimension_semantics=sem, vmem_limit_bytes=VMEM_LIMIT)


def _full(shape):
    n = len(shape)
    return pl.BlockSpec(tuple(shape), lambda *_: (0,) * n)


def _row(ts, c):
    return pl.BlockSpec((ts, c), lambda i: (i, 0))


def _hrow(h, ts, c):
    return pl.BlockSpec((h, ts, c), lambda i: (0, i, 0))


def _dot(a, b):
    return jnp.dot(a, b, preferred_element_type=F32)


def _dot_nt(a, b):
    return lax.dot_general(a, b, (((1,), (1,)), ((), ())), preferred_element_type=F32)


def _dot_tn(a, b):
    return lax.dot_general(a, b, (((0,), (0,)), ((), ())), preferred_element_type=F32)


def _dot_hi(a, b):
    hi = a.astype(BF16)
    lo = (a - hi.astype(F32)).astype(BF16)
    bb = b.astype(BF16)
    return _dot(hi, bb) + _dot(lo, bb)


def _rot_half(x, half):
    w = x.shape[-1]
    lane = lax.broadcasted_iota(jnp.int32, x.shape, x.ndim - 1)
    first = (lane % (2 * half)) < half
    return jnp.where(first, -pltpu.roll(x, w - half, x.ndim - 1), pltpu.roll(x, half, x.ndim - 1))


def _rope(x, cos, sin, half):
    return x * cos + _rot_half(x, half) * sin


def _unrope(dy, cos, sin, half):
    return dy * cos - _rot_half(dy, half) * sin


def _silu(g):
    return g / (1.0 + jnp.exp(-g))


def _rstd(x):
    return lax.rsqrt(jnp.mean(x * x, axis=-1, keepdims=True) + EPS)


def _rope_tables(positions):
    pos = positions.astype(F32)[:, None]
    s = pos.shape[0]
    inv = ROPE_BASE ** (-jnp.arange(0, HEAD, 2, dtype=F32) / HEAD)
    ang = pos * inv
    c, sn = jnp.cos(ang), jnp.sin(ang)
    cos_r = jnp.tile(jnp.concatenate([c, c], -1), (1, 2))
    sin_r = jnp.tile(jnp.concatenate([sn, sn], -1), (1, 2))
    inv = ROPE_BASE ** (-jnp.arange(0, ROPE, 2, dtype=F32) / ROPE)
    ang = pos * inv
    c, sn = jnp.cos(ang), jnp.sin(ang)
    one, zero = jnp.ones((s, KPE_LO), F32), jnp.zeros((s, KPE_LO), F32)
    cos_m = jnp.concatenate([one, c, c, one[:, :LANES - KPE_LO - ROPE]], -1)
    sin_m = jnp.concatenate([zero, sn, sn, zero[:, :LANES - KPE_LO - ROPE]], -1)
    return cos_r, sin_r, cos_m, sin_m


def _ret_consts():
    c = RET_CHUNK
    lg = np.log1p(-np.power(2.0, -5.0 - np.arange(N_HEADS, dtype=np.float64)))
    idx = np.arange(c, dtype=np.float64)
    diff = idx[:, None] - idx[None, :]
    lane_head = np.arange(LANES) // HEAD
    dmask = np.zeros((4, 2, c, c))
    zeta = np.zeros((4, c, LANES))
    xi = np.zeros((4, c, LANES))
    cd = np.zeros((4, LANES, LANES))
    bd = (lane_head[:, None] == lane_head[None, :]).astype(np.float64)
    for j in range(4):
        for hh in range(2):
            dmask[j, hh] = np.where(diff >= 0, np.exp(lg[2 * j + hh] * np.maximum(diff, 0.0)), 0.0)
        lgl = lg[2 * j + lane_head]
        zeta[j] = np.exp(lgl[None, :] * (c - 1.0 - idx[:, None]))
        xi[j] = np.exp(lgl[None, :] * (idx[:, None] + 1.0))
        cd[j] = np.exp(lgl * c)[:, None] * bd
    f = lambda a: jnp.asarray(a, F32)
    side = lambda d: np.concatenate([d[:, 0], d[:, 1]], axis=-1)
    return dict(dmask=f(side(dmask)), dmask_t=f(side(np.swapaxes(dmask, 2, 3))), zeta=f(zeta), xi=f(xi), cd=f(cd), bd=f(bd))


def _f1_call(x, anw, win, cos_r, sin_r, cos_m, sin_m, ts):
    s = x.shape[0]

    def body(x_ref, anw_ref, w_ref, cr_ref, sr_ref, cm_ref, sm_ref,
             q_ref, k_ref, v_ref, g_ref, cq_ref, ckv_ref, kpe_ref, r_ref):
        xv = x_ref[...]
        r = _rstd(xv)
        r_ref[...] = r
        h = (xv * r * anw_ref[...]).astype(BF16)
        cr, sr = cr_ref[...], sr_ref[...]
        qk = _dot(h, w_ref[:, 0:2 * RET_W])
        for j in range(4):
            sl = slice(j * LANES, (j + 1) * LANES)
            q_ref[:, sl] = _rope(qk[:, sl], cr, sr, HEAD // 2).astype(BF16)
            kk = qk[:, RET_W + j * LANES:RET_W + (j + 1) * LANES]
            k_ref[:, sl] = (_rope(kk, cr, sr, HEAD // 2) * (HEAD ** -0.5)).astype(BF16)
        v_ref[...] = _dot(h, w_ref[:, 2 * RET_W:3 * RET_W]).astype(BF16)
        g_ref[...] = _dot(h, w_ref[:, 3 * RET_W:4 * RET_W])
        o = 4 * RET_W
        cq_ref[...] = _dot(h, w_ref[:, o:o + Q_RANK])
        ckv_ref[...] = _dot(h, w_ref[:, o + Q_RANK:o + Q_RANK + KV_RANK])
        kp = _dot(h, w_ref[:, o + Q_RANK + KV_RANK:IN_EXT])
        kpe_ref[...] = _rope(kp, cm_ref[...], sm_ref[...], ROPE // 2)

    sd = jax.ShapeDtypeStruct
    return pl.pallas_call(
        body, name="f1_in_proj", grid=(s // ts,),
        in_specs=[_row(ts, D_MODEL), _full((1, D_MODEL)), _full((D_MODEL, IN_EXT)),
                  _row(ts, LANES), _row(ts, LANES), _row(ts, LANES), _row(ts, LANES)],
        out_specs=[_row(ts, RET_W), _row(ts, RET_W), _row(ts, RET_W), _row(ts, RET_W),
                   _row(ts, Q_RANK), _row(ts, KV_RANK), _row(ts, LANES), _row(ts, 1)],
        out_shape=[sd((s, RET_W), BF16), sd((s, RET_W), BF16), sd((s, RET_W), BF16), sd((s, RET_W), F32),
                   sd((s, Q_RANK), F32), sd((s, KV_RANK), F32), sd((s, LANES), F32), sd((s, 1), F32)],
        compiler_params=_cp("parallel"),
    )(x, anw, win, cos_r, sin_r, cos_m, sin_m)


def _stack_heads(a):
    lo = lax.broadcasted_iota(jnp.int32, a.shape, 1) < HEAD
    zero = jnp.zeros_like(a)
    return jnp.concatenate([jnp.where(lo, a, zero), jnp.where(lo, zero, a)], axis=0)


def _pair_product(a, b2, decay2, w2):
    return _dot((_dot_nt(a, b2) * decay2).astype(BF16), w2)


def _ret_fwd_call(q, k, v, g, gnw, rc, tr):
    s = q.shape[0]
    c = RET_CHUNK
    nc = tr // c

    def body(q_ref, k_ref, v_ref, g_ref, gnw_ref, dm_ref, zeta_ref, xi_ref, cd_ref, bd_ref, o_ref, y_ref, st_ref):
        @pl.when(pl.program_id(1) == 0)
        def _():
            st_ref[...] = jnp.zeros_like(st_ref)

        lane = lax.broadcasted_iota(jnp.int32, (c, LANES), 1)
        bd = bd_ref[...]
        chunks = [slice(ci * c, (ci + 1) * c) for ci in range(nc)]
        contrib = [_dot_tn((k_ref[rows, :].astype(F32) * zeta_ref[0]).astype(BF16), v_ref[rows, :]) * bd for rows in chunks]
        st, states = st_ref[...], []
        for ci in range(nc):
            states.append(st.astype(BF16))
            st = st * cd_ref[0] + contrib[ci]
        st_ref[...] = st
        for ci, rows in enumerate(chunks):
            qc = q_ref[rows, :]
            o_ref[rows, :] = (_dot(qc, states[ci]) * xi_ref[0]
                              + _pair_product(qc, _stack_heads(k_ref[rows, :]), dm_ref[0], _stack_heads(v_ref[rows, :])))
        o = o_ref[...]
        avg = bd * (1.0 / HEAD)
        ctr = o - _dot_hi(o, avg)
        var = _dot_hi(ctr * ctr, avg)
        y_ref[...] = (_silu(g_ref[...]) * (ctr * lax.rsqrt(var + EPS) * gnw_ref[...])).astype(BF16)

    slab = pl.BlockSpec((tr, LANES), lambda j, i: (i, j))
    sd = jax.ShapeDtypeStruct
    return pl.pallas_call(
        body, name="ret_fwd", grid=(4, s // tr),
        in_specs=[slab, slab, slab, slab, pl.BlockSpec((1, LANES), lambda j, i: (0, j)),
                  pl.BlockSpec((1, c, 2 * c), lambda j, i: (j, 0, 0)),
                  pl.BlockSpec((1, c, LANES), lambda j, i: (j, 0, 0)),
                  pl.BlockSpec((1, c, LANES), lambda j, i: (j, 0, 0)),
                  pl.BlockSpec((1, LANES, LANES), lambda j, i: (j, 0, 0)),
                  pl.BlockSpec((LANES, LANES), lambda j, i: (0, 0))],
        out_specs=[slab, slab],
        out_shape=[sd((s, RET_W), F32), sd((s, RET_W), BF16)],
        scratch_shapes=[pltpu.VMEM((LANES, LANES), F32)],
        compiler_params=_cp("parallel", "arbitrary"),
    )(q, k, v, g, gnw, rc["dmask"], rc["zeta"], rc["xi"], rc["cd"], rc["bd"])


QK_AUX = HEAD + ROPE
V_AUX = HEAD


def _lane_pair(shape, lo, a, b, rest):
    lane = lax.broadcasted_iota(jnp.int32, shape, len(shape) - 1)
    return jnp.where(lane == lo, a, jnp.where(lane == lo + 1, b, rest))


def _hi_lo(v):
    hi = v.astype(BF16).astype(F32)
    return hi, v - hi


def _mla_pre_call(cq, ckv, kpe, qnw, kvnw, wq, wk, wv, cos_m, sin_m, ts):
    s = cq.shape[0]

    def body(cq_ref, ckv_ref, kpe_ref, qnw_ref, kvnw_ref, wq_ref, wk_ref, wv_ref, cm_ref, sm_ref, q_ref, k_ref, v_ref):
        cqv, ckvv = cq_ref[...], ckv_ref[...]
        cqn = (cqv * _rstd(cqv) * qnw_ref[...]).astype(BF16)
        ckvn = (ckvv * _rstd(ckvv) * kvnw_ref[...]).astype(BF16)
        cm, sm = cm_ref[...], sm_ref[...]
        kp = _lane_pair((ts, LANES), QK_AUX, -1.0, -1.0, kpe_ref[...])
        for h in range(N_HEADS):
            qh = _rope(_dot(cqn, wq_ref[h]), cm, sm, ROPE // 2)
            q_ref[h] = (qh * (SM_SCALE * LOG2E)).astype(BF16)
            k_ref[h] = (_dot(ckvn, wk_ref[h]) + kp).astype(BF16)
            v_ref[h] = _lane_pair((ts, LANES), V_AUX, 1.0, 1.0, _dot(ckvn, wv_ref[h])).astype(BF16)

    sd = jax.ShapeDtypeStruct
    hm = sd((N_HEADS, s, LANES), BF16)
    return pl.pallas_call(
        body, name="mla_pre", grid=(s // ts,),
        in_specs=[_row(ts, Q_RANK), _row(ts, KV_RANK), _row(ts, LANES), _full((1, Q_RANK)), _full((1, KV_RANK)),
                  _full((N_HEADS, Q_RANK, LANES)), _full((N_HEADS, KV_RANK, LANES)), _full((N_HEADS, KV_RANK, LANES)),
                  _row(ts, LANES), _row(ts, LANES)],
        out_specs=[_hrow(N_HEADS, ts, LANES)] * 3,
        out_shape=[hm, hm, hm],
        compiler_params=_cp("parallel"),
    )(cq, ckv, kpe, qnw, kvnw, wq, wk, wv, cos_m, sin_m)


def _flash_fwd_call(q, k, v, tb):
    s = q.shape[1]
    nb = s // tb

    def body(q_ref, k_ref, v_ref, o_ref, qb_ref, m_ref, acc_ref):
        qi, ki = pl.program_id(0), pl.program_id(1)

        @pl.when(ki == 0)
        def _():
            m_ref[...] = jnp.full_like(m_ref, NEG)
            acc_ref[...] = jnp.zeros_like(acc_ref)

        def step(masked):
            if masked:
                keep = lax.broadcasted_iota(jnp.int32, (tb, tb), 1) <= lax.broadcasted_iota(jnp.int32, (tb, tb), 0)
            def finish(h, pe, alpha):
                acc_ref[h] = acc_ref[h] * alpha + _dot(pe, v_ref[h])

            nxt, pending = _dot_nt(q_ref[0], k_ref[0]), None
            for h in range(N_HEADS):
                sc = nxt
                if h + 1 < N_HEADS:
                    nxt = _dot_nt(q_ref[h + 1], k_ref[h + 1])
                if masked:
                    sc = jnp.where(keep, sc, NEG)
                m_prev = m_ref[h]
                m_new = jnp.maximum(m_prev, jnp.max(sc, axis=1, keepdims=True))
                pe = jnp.exp2(sc - jnp.tile(m_new, (1, tb // LANES))).astype(BF16)
                m_ref[h] = m_new
                if pending is not None:
                    finish(*pending)
                pending = (h, pe, jnp.exp2(m_prev - m_new))
            finish(*pending)

        @pl.when(ki < qi)
        def _():
            step(False)

        @pl.when(ki == qi)
        def _():
            step(True)
            lane = lax.broadcasted_iota(jnp.int32, (tb, LANES), 1)
            for p in range(N_HEADS // 2):
                outs = []
                for h in (2 * p, 2 * p + 1):
                    acc = acc_ref[h]
                    l = acc[:, V_AUX:V_AUX + 1]
                    outs.append(acc * (1.0 / l))
                    hi, lo = _hi_lo(m_ref[h][:, 0:1] + jnp.log(l) * LOG2E)
                    qb_ref[h] = _lane_pair((tb, LANES), QK_AUX, hi, lo, q_ref[h].astype(F32)).astype(BF16)
                o_ref[:, p * LANES:(p + 1) * LANES] = jnp.where(lane < HEAD, outs[0], pltpu.roll(outs[1], HEAD, 1)).astype(BF16)

    sd = jax.ShapeDtypeStruct
    qspec = pl.BlockSpec((N_HEADS, tb, LANES), lambda qi, ki: (0, qi, 0))
    kspec = pl.BlockSpec((N_HEADS, tb, LANES), lambda qi, ki: (0, jnp.minimum(ki, qi), 0))
    return pl.pallas_call(
        body, name="mla_flash_fwd", grid=(nb, nb),
        in_specs=[qspec, kspec, kspec],
        out_specs=[pl.BlockSpec((tb, MLA_W), lambda qi, ki: (qi, 0)), qspec],
        out_shape=[sd((s, MLA_W), BF16), sd((N_HEADS, s, LANES), BF16)],
        scratch_shapes=[pltpu.VMEM((N_HEADS, tb, LANES), F32), pltpu.VMEM((N_HEADS, tb, LANES), F32)],
        compiler_params=_cp("parallel", "arbitrary"),
    )(q, k, v)


def _out_proj_call(x, yret, ymla, wout, ts):
    s = x.shape[0]

    def body(x_ref, yr_ref, ym_ref, w_ref, x1_ref, r_ref):
        x1 = x_ref[...] + _dot(yr_ref[...], w_ref[0:RET_W, :]) + _dot(ym_ref[...], w_ref[RET_W:, :])
        x1_ref[...] = x1
        r_ref[...] = _rstd(x1)

    sd = jax.ShapeDtypeStruct
    return pl.pallas_call(
        body, name="out_proj", grid=(s // ts,),
        in_specs=[_row(ts, D_MODEL), _row(ts, RET_W), _row(ts, MLA_W), _full((D_MODEL, D_MODEL))],
        out_specs=[_row(ts, D_MODEL), _row(ts, 1)],
        out_shape=[sd((s, D_MODEL), F32), sd((s, 1), F32)],
        compiler_params=_cp("parallel"),
    )(x, yret, ymla, wout)


W_UP_SHARD = F2 // 4


def _ffn_fwd_call(x1, r2, fnw, wup4, cw, cb, wdown, ts):
    s = x1.shape[0]
    wsh = W_UP_SHARD

    def body(x_ref, r_ref, fnw_ref, wup_ref, cw_ref, cb_ref, wd_ref, u_ref, x2_ref, carry_ref):
        _zero_first(pl.program_id(0) == 0, carry_ref)
        xv = x_ref[...]
        h = (xv * r_ref[...] * fnw_ref[...]).astype(BF16)
        conv = []
        for j in range(4):
            cols = slice(j * wsh, (j + 1) * wsh)
            ub = _dot(h, wup_ref[j]).astype(BF16)
            u_ref[:, cols] = ub
            u = ub.astype(F32)
            u1, u2 = _shifted(u, carry_ref[:, cols])
            w = cw_ref[:, cols]
            conv.append(cb_ref[:, cols] + w[0:1, :] * u2 + w[1:2, :] * u1 + w[2:3, :] * u)
            carry_ref[:, cols] = u[ts - 8:, :]
        acc = xv
        for j in range(2):
            a = (_silu(conv[j]) * conv[j + 2]).astype(BF16)
            acc = acc + _dot(a, wd_ref[j * wsh:(j + 1) * wsh, :])
        x2_ref[...] = acc

    sd = jax.ShapeDtypeStruct
    return pl.pallas_call(
        body, name="ffn_fwd", grid=(s // ts,),
        in_specs=[_row(ts, D_MODEL), _row(ts, 1), _full((1, D_MODEL)), _full((4, D_MODEL, wsh)),
                  _full((3, F2)), _full((1, F2)), _full((D_FF, D_MODEL))],
        out_specs=[_row(ts, F2), _row(ts, D_MODEL)],
        out_shape=[sd((s, F2), BF16), sd((s, D_MODEL), F32)],
        scratch_shapes=[pltpu.VMEM((8, F2), F32)],
        compiler_params=_cp("arbitrary"),
    )(x1, r2, fnw, wup4, cw, cb, wdown)


def _shifted(u, hal):
    row = lax.broadcasted_iota(jnp.int32, u.shape, 0)
    u1 = jnp.where(row == 0, hal[7:8, :], pltpu.roll(u, 1, 0))
    u2 = jnp.where(row == 0, hal[6:7, :], jnp.where(row == 1, hal[7:8, :], pltpu.roll(u, 2, 0)))
    return u1, u2


def _prep_weights(w):
    win = w["w_in"]
    pad = lambda n: jnp.zeros((D_MODEL, n), win.dtype)
    win_ext = jnp.concatenate([win[:, :IN_W - ROPE], pad(KPE_LO), win[:, IN_W - ROPE:], pad(LANES - KPE_LO - ROPE)], -1)
    wuq = w["w_uq"].reshape(Q_RANK, N_HEADS, HEAD + ROPE)
    wq = jnp.concatenate([wuq, jnp.zeros((Q_RANK, N_HEADS, LANES - HEAD - ROPE), wuq.dtype)], -1).transpose(1, 0, 2)
    wukv = w["w_ukv"].reshape(KV_RANK, N_HEADS, 2 * HEAD)
    zk = jnp.zeros((KV_RANK, N_HEADS, HEAD), wukv.dtype)
    wk = jnp.concatenate([wukv[:, :, :HEAD], zk], -1).transpose(1, 0, 2)
    wv = jnp.concatenate([wukv[:, :, HEAD:], zk], -1).transpose(1, 0, 2)
    c = lambda a: a.astype(BF16)
    return dict(win=c(win_ext), wq=c(wq), wk=c(wk), wv=c(wv), wout=c(w["w_out"]))


def _prep_mlp_weights(w):
    wup = w["w_up"]
    if wup.ndim == 2:
        wup = wup.reshape(D_MODEL, 4, W_UP_SHARD).transpose(1, 0, 2)
    return dict(wup=wup.astype(BF16), wdown=w["w_down"].astype(BF16))


def _tiles(s):
    return dict(ts=min(s, 512), tr=min(s, 1024), tb=min(s, 512), tg=min(s, 512), tf=D_FF // 2, t2=min(s, 256))


class _Exchanges:
    def __init__(self, w):
        self.w = w

    def mlp_weights(self, after):
        return self.w

    def mlp_grads(self, gw):
        pass

    def behind_out_bwd(self, after):
        pass

    def behind_attention(self, after):
        pass


def _forward(x, positions, w, small, ex):
    s = x.shape[0]
    t = _tiles(s)
    pw = _prep_weights(w)
    cos_r, sin_r, cos_m, sin_m = _rope_tables(positions)
    rc = _ret_consts()
    q, k, v, g, cq, ckv, kpe, r1 = _f1_call(x, small["attn_norm_w"], pw["win"], cos_r, sin_r, cos_m, sin_m, t["ts"])
    o_ret, y_ret = _ret_fwd_call(q, k, v, g, small["ret_gn_w"], rc, t["tr"])
    mq, mk, mv = _mla_pre_call(cq, ckv, kpe, small["mla_q_norm_w"], small["mla_kv_norm_w"],
                               pw["wq"], pw["wk"], pw["wv"], cos_m, sin_m, t["ts"])
    y_mla, mqb = _flash_fwd_call(mq, mk, mv, t["tb"])
    x1, r2 = _out_proj_call(x, y_ret, y_mla, pw["wout"], t["ts"])
    pw.update(_prep_mlp_weights(ex.mlp_weights(r2)))
    u, x2 = _ffn_fwd_call(x1, r2, small["ffn_norm_w"], pw["wup"], w["conv_w"], small["conv_b"], pw["wdown"], t["t2"])
    return dict(pw=pw, tabs=(cos_r, sin_r, cos_m, sin_m), rc=rc, q=q, k=k, v=v, g=g, cq=cq, ckv=ckv, kpe=kpe, r1=r1,
                o_ret=o_ret, y_ret=y_ret, mqb=mqb, mk=mk, mv=mv, y_mla=y_mla, x1=x1, r2=r2, u=u, x2=x2)


def _norm_bwd(dh, xh, r, nw):
    dxn = dh * nw
    return r * (dxn - xh * jnp.mean(dxn * xh, axis=-1, keepdims=True))


def _ordered_after(body, order):
    if order is None:
        return body, [], []
    return (lambda order_ref, *refs: body(*refs)), [pl.BlockSpec(memory_space=pl.ANY)], [order]


def _zero_first(first, *refs):
    @pl.when(first)
    def _():
        for ref in refs:
            ref[...] = jnp.zeros_like(ref)


def _colsum(v):
    return jnp.sum(v, axis=0, keepdims=True)


def _dsilu(g, sg):
    return sg * (1.0 + g * (1.0 - sg))


def _loss_call(x2, tgt, fw, ts):
    s = x2.shape[0]

    def body(x_ref, t_ref, fw_ref, dx_ref, loss_ref, gfw_ref):
        _zero_first(pl.program_id(0) == 0, loss_ref, gfw_ref)
        xv = x_ref[...]
        r = _rstd(xv)
        xh = xv * r
        fwv = fw_ref[...]
        e = xh * fwv - t_ref[...]
        loss_ref[...] += (0.5 / D_MODEL) * _colsum(jnp.sum(e * e, axis=1, keepdims=True))
        dy = e * (1.0 / D_MODEL)
        gfw_ref[...] += _colsum(dy * xh)
        dx_ref[...] = _norm_bwd(dy, xh, r, fwv)

    sd = jax.ShapeDtypeStruct
    return pl.pallas_call(
        body, name="loss_bwd", grid=(s // ts,),
        in_specs=[_row(ts, D_MODEL), _row(ts, D_MODEL), _full((1, D_MODEL))],
        out_specs=[_row(ts, D_MODEL), _full((1, 1)), _full((1, D_MODEL))],
        out_shape=[sd((s, D_MODEL), F32), sd((1, 1), F32), sd((1, D_MODEL), F32)],
        compiler_params=_cp("arbitrary"),
    )(x2, tgt, fw)


def _ffn_bwd_call(dx2, u, cw, cb, wdown, wup4, x1, r2, fnw, ts):
    s = dx2.shape[0]
    nt = s // ts
    hb = ts // 8
    wsh = W_UP_SHARD
    rev = lambda i: nt - 1 - i

    def body(dx2_ref, u_ref, h_ref, cw_ref, cb_ref, wd_ref, wup_ref, x_ref, r_ref, fnw_ref,
             du_ref, dx1_ref, dcw_ref, dcb_ref, dfnw_ref, dwd_hbm, carry_ref, dwd_ref, sem):
        i = pl.program_id(0)
        _zero_first(i == 0, carry_ref, dwd_ref, dcw_ref, dcb_ref, dfnw_ref)
        seq_start = i == nt - 1
        dxb = dx2_ref[...].astype(BF16)
        dh = jnp.zeros((ts, D_MODEL), F32)

        def conv(cols):
            uv = u_ref[:, cols].astype(F32)
            u1, u2 = _shifted(uv, jnp.where(seq_start, 0.0, h_ref[:, cols].astype(F32)))
            w = cw_ref[:, cols]
            return cb_ref[:, cols] + w[0:1, :] * u2 + w[1:2, :] * u1 + w[2:3, :] * uv, (u2, u1, uv)

        for j in range(2):
            gcols = slice(j * wsh, (j + 1) * wsh)
            vcols = slice(D_FF + j * wsh, D_FF + (j + 1) * wsh)
            gate, gtaps = conv(gcols)
            val, vtaps = conv(vcols)
            da = _dot_nt(dxb, wd_ref[gcols, :])
            sg = 1.0 / (1.0 + jnp.exp(-gate))
            sl = gate * sg
            dwd_ref[gcols, :] += _dot_tn((sl * val).astype(BF16), dxb)
            for d, cols, taps, shard in ((da * val * _dsilu(gate, sg), gcols, gtaps, j), (da * sl, vcols, vtaps, 2 + j)):
                for t in range(3):
                    dcw_ref[t:t + 1, cols] += _colsum(d * taps[t])
                dcb_ref[:, cols] += _colsum(d)
                d1, d2 = _shifted_up(d, carry_ref[:, cols])
                w = cw_ref[:, cols]
                du = (w[2:3, :] * d + w[1:2, :] * d1 + w[0:1, :] * d2).astype(BF16)
                du_ref[:, cols] = du
                dh = dh + _dot_nt(du, wup_ref[shard])
                carry_ref[:, cols] = d[0:8, :]
        r = r_ref[...]
        xh = x_ref[...] * r
        dfnw_ref[...] += _colsum(dh * xh)
        dx1_ref[...] = dx2_ref[...] + _norm_bwd(dh, xh, r, fnw_ref[...])

        @pl.when(i == nt - 1)
        def _():
            cp = pltpu.make_async_copy(dwd_ref, dwd_hbm, sem)
            cp.start()
            cp.wait()

    sd = jax.ShapeDtypeStruct
    row = lambda c: pl.BlockSpec((ts, c), lambda i: (rev(i), 0))
    once = lambda shape: pl.BlockSpec(shape, lambda i: (0,) * len(shape), pipeline_mode=pl.Buffered(1))
    return pl.pallas_call(
        body, name="ffn_bwd", grid=(nt,),
        in_specs=[row(D_MODEL), row(F2), pl.BlockSpec((8, F2), lambda i: (jnp.maximum(rev(i) * hb - 1, 0), 0)),
                  once((3, F2)), once((1, F2)), once((D_FF, D_MODEL)), once((4, D_MODEL, wsh)),
                  row(D_MODEL), row(1), once((1, D_MODEL))],
        out_specs=[row(F2), row(D_MODEL), _full((3, F2)), _full((1, F2)), _full((1, D_MODEL)), pl.BlockSpec(memory_space=pl.ANY)],
        out_shape=[sd((s, F2), BF16), sd((s, D_MODEL), F32), sd((3, F2), F32), sd((1, F2), F32), sd((1, D_MODEL), F32),
                   sd((D_FF, D_MODEL), F32)],
        scratch_shapes=[pltpu.VMEM((8, F2), F32), pltpu.VMEM((D_FF, D_MODEL), F32), pltpu.SemaphoreType.DMA],
        compiler_params=_cp("arbitrary"),
    )(dx2, u, u, cw, cb, wdown, wup4, x1, r2, fnw)


def _shifted_up(d, hal):
    n = d.shape[0]
    row = lax.broadcasted_iota(jnp.int32, d.shape, 0)
    d1 = jnp.where(row == n - 1, hal[0:1, :], pltpu.roll(d, n - 1, 0))
    d2 = jnp.where(row == n - 2, hal[0:1, :], jnp.where(row == n - 1, hal[1:2, :], pltpu.roll(d, n - 2, 0)))
    return d1, d2


def _dw_norm_call(x, r, nw, b, ts, tn, name):
    s, n = b.shape
    k = x.shape[1]

    def body(x_ref, r_ref, nw_ref, b_ref, dw_ref):
        _zero_first(pl.program_id(1) == 0, dw_ref)
        h = (x_ref[...] * r_ref[...] * nw_ref[...]).astype(BF16)
        dw_ref[...] += _dot_tn(h, b_ref[...])

    return pl.pallas_call(
        body, name=name, grid=(n // tn, s // ts),
        in_specs=[pl.BlockSpec((ts, k), lambda j, i: (i, 0)), pl.BlockSpec((ts, 1), lambda j, i: (i, 0)),
                  pl.BlockSpec((1, k), lambda j, i: (0, 0)), pl.BlockSpec((ts, tn), lambda j, i: (i, j))],
        out_specs=pl.BlockSpec((None, k, tn), lambda j, i: (j, 0, 0)),
        out_shape=jax.ShapeDtypeStruct((n // tn, k, tn), F32),
        compiler_params=_cp("parallel", "arbitrary"),
    )(x, r, nw, b)


def _out_bwd_call(dx1, yret, ymla, wout, ts, order=None):
    s = dx1.shape[0]

    def body(dx_ref, yr_ref, ym_ref, w_ref, dyr_ref, do_ref, dwo_ref):
        _zero_first(pl.program_id(0) == 0, dwo_ref)
        dxb = dx_ref[...].astype(BF16)
        dmix = _dot_nt(dxb, w_ref[...])
        dyr_ref[...] = dmix[:, :RET_W]
        ym = ym_ref[...]
        lane = lax.broadcasted_iota(jnp.int32, (ts, LANES), 1)
        for p in range(N_HEADS // 2):
            dom = dmix[:, RET_W + p * LANES:RET_W + (p + 1) * LANES]
            prod = dom * ym[:, p * LANES:(p + 1) * LANES].astype(F32)
            for hh in range(2):
                mine = (lane >= HEAD) if hh else (lane < HEAD)
                hi, lo = _hi_lo(jnp.sum(jnp.where(mine, prod, 0.0), axis=1, keepdims=True))
                base = jnp.where(lane < HEAD, pltpu.roll(dom, HEAD, 1) if hh else dom, 0.0)
                do_ref[2 * p + hh] = _lane_pair((ts, LANES), V_AUX, -hi, -lo, base).astype(BF16)
        dwo_ref[0:RET_W, :] += _dot_tn(yr_ref[...], dxb)
        dwo_ref[RET_W:, :] += _dot_tn(ym, dxb)

    sd = jax.ShapeDtypeStruct
    body, first_specs, first = _ordered_after(body, order)
    return pl.pallas_call(
        body, name="out_proj_bwd", grid=(s // ts,),
        in_specs=first_specs + [_row(ts, D_MODEL), _row(ts, RET_W), _row(ts, MLA_W), _full((D_MODEL, D_MODEL))],
        out_specs=[_row(ts, RET_W), _hrow(N_HEADS, ts, LANES), _full((D_MODEL, D_MODEL))],
        out_shape=[sd((s, RET_W), F32), sd((N_HEADS, s, LANES), BF16), sd((D_MODEL, D_MODEL), F32)],
        compiler_params=_cp("arbitrary"),
    )(*first, dx1, yret, ymla, wout)


def _ret_bwd_q_call(q, k, v, o, g, dy, gnw, rc, cos_r, sin_r, tr):
    s = q.shape[0]
    c = RET_CHUNK
    nc = tr // c

    def body(q_ref, k_ref, v_ref, o_ref, g_ref, dy_ref, gnw_ref, dm_ref, zeta_ref, xi_ref, cd_ref, bd_ref, cr_ref, sr_ref,
             dq_ref, dg_ref, do_ref, dgnw_ref, st_ref):
        _zero_first(pl.program_id(1) == 0, st_ref, dgnw_ref)
        bd = bd_ref[...]
        avg = bd * (1.0 / HEAD)
        ov = o_ref[...]
        ctr = ov - _dot_hi(ov, avg)
        rs = lax.rsqrt(_dot_hi(ctr * ctr, avg) + EPS)
        oh = ctr * rs
        gg, dyv, gnw_v = g_ref[...], dy_ref[...], gnw_ref[...]
        sg = 1.0 / (1.0 + jnp.exp(-gg))
        sl = gg * sg
        dg_ref[...] = (dyv * oh * gnw_v * _dsilu(gg, sg)).astype(BF16)
        dgnw_ref[...] += _colsum(dyv * sl * oh)
        doh = dyv * sl * gnw_v
        dov = (rs * (doh - _dot_hi(doh, avg) - oh * _dot_hi(doh * oh, avg))).astype(BF16)
        do_ref[...] = dov
        chunks = [slice(ci * c, (ci + 1) * c) for ci in range(nc)]
        contrib = [_dot_tn((k_ref[rows, :].astype(F32) * zeta_ref[0]).astype(BF16), v_ref[rows, :]) * bd for rows in chunks]
        st, states = st_ref[...], []
        for ci in range(nc):
            states.append(st.astype(BF16))
            st = st * cd_ref[0] + contrib[ci]
        st_ref[...] = st
        for ci, rows in enumerate(chunks):
            doc = dov[rows, :]
            dq = (_dot_nt(doc, states[ci]) * xi_ref[0]
                  + _pair_product(doc, _stack_heads(v_ref[rows, :]), dm_ref[0], _stack_heads(k_ref[rows, :])))
            dq_ref[rows, :] = _unrope(dq, cr_ref[rows, :], sr_ref[rows, :], HEAD // 2).astype(BF16)

    slab = pl.BlockSpec((tr, LANES), lambda j, i: (i, j))
    tab = pl.BlockSpec((tr, LANES), lambda j, i: (i, 0))
    vec = pl.BlockSpec((1, LANES), lambda j, i: (0, j))
    sd = jax.ShapeDtypeStruct
    return pl.pallas_call(
        body, name="ret_bwd_q", grid=(4, s // tr),
        in_specs=[slab, slab, slab, slab, slab, slab, vec,
                  pl.BlockSpec((1, c, 2 * c), lambda j, i: (j, 0, 0)),
                  pl.BlockSpec((1, c, LANES), lambda j, i: (j, 0, 0)),
                  pl.BlockSpec((1, c, LANES), lambda j, i: (j, 0, 0)),
                  pl.BlockSpec((1, LANES, LANES), lambda j, i: (j, 0, 0)),
                  pl.BlockSpec((LANES, LANES), lambda j, i: (0, 0)), tab, tab],
        out_specs=[slab, slab, slab, vec],
        out_shape=[sd((s, RET_W), BF16), sd((s, RET_W), BF16), sd((s, RET_W), BF16), sd((1, RET_W), F32)],
        scratch_shapes=[pltpu.VMEM((LANES, LANES), F32)],
        compiler_params=_cp("parallel", "arbitrary"),
    )(q, k, v, o, g, dy, gnw, rc["dmask"], rc["zeta"], rc["xi"], rc["cd"], rc["bd"], cos_r, sin_r)


def _ret_bwd_kv_call(q, k, v, do, rc, cos_r, sin_r, tr):
    s = q.shape[0]
    c = RET_CHUNK
    nc = tr // c
    nt = s // tr

    def body(q_ref, k_ref, v_ref, do_ref, dm_ref, zeta_ref, xi_ref, cd_ref, bd_ref, cr_ref, sr_ref, dk_ref, dv_ref, gs_ref):
        _zero_first(pl.program_id(1) == 0, gs_ref)
        bd = bd_ref[...]
        chunks = [slice(ci * c, (ci + 1) * c) for ci in range(nc)]
        contrib = [_dot_tn((q_ref[rows, :].astype(F32) * xi_ref[0]).astype(BF16), do_ref[rows, :]) * bd for rows in chunks]
        gs, states = gs_ref[...], [None] * nc
        for ci in reversed(range(nc)):
            states[ci] = gs.astype(BF16)
            gs = gs * cd_ref[0] + contrib[ci]
        gs_ref[...] = gs
        for ci, rows in enumerate(chunks):
            kc, vc = k_ref[rows, :], v_ref[rows, :]
            q2, do2 = _stack_heads(q_ref[rows, :]), _stack_heads(do_ref[rows, :])
            gb = states[ci]
            dk = _dot_nt(vc, gb) * zeta_ref[0] + _pair_product(vc, do2, dm_ref[0], q2)
            dv = _dot(kc, gb) * zeta_ref[0] + _pair_product(kc, q2, dm_ref[0], do2)
            dk_ref[rows, :] = (_unrope(dk, cr_ref[rows, :], sr_ref[rows, :], HEAD // 2) * (HEAD ** -0.5)).astype(BF16)
            dv_ref[rows, :] = dv.astype(BF16)

    slab = pl.BlockSpec((tr, LANES), lambda j, i: (nt - 1 - i, j))
    tab = pl.BlockSpec((tr, LANES), lambda j, i: (nt - 1 - i, 0))
    sd = jax.ShapeDtypeStruct
    return pl.pallas_call(
        body, name="ret_bwd_kv", grid=(4, nt),
        in_specs=[slab, slab, slab, slab,
                  pl.BlockSpec((1, c, 2 * c), lambda j, i: (j, 0, 0)),
                  pl.BlockSpec((1, c, LANES), lambda j, i: (j, 0, 0)),
                  pl.BlockSpec((1, c, LANES), lambda j, i: (j, 0, 0)),
                  pl.BlockSpec((1, LANES, LANES), lambda j, i: (j, 0, 0)),
                  pl.BlockSpec((LANES, LANES), lambda j, i: (0, 0)), tab, tab],
        out_specs=[slab, slab],
        out_shape=[sd((s, RET_W), BF16), sd((s, RET_W), BF16)],
        scratch_shapes=[pltpu.VMEM((LANES, LANES), F32)],
        compiler_params=_cp("parallel", "arbitrary"),
    )(q, k, v, do, rc["dmask_t"], rc["zeta"], rc["xi"], rc["cd"], rc["bd"], cos_r, sin_r)


FLASH_BWD_HEADS = 4


def _flash_bwd_call(qb, k, v, do, tb, order=None):
    s = qb.shape[1]
    nb = s // tb
    hg = FLASH_BWD_HEADS

    def body(q_ref, k_ref, v_ref, do_ref, dk_ref, dv_ref, dq_hbm, dka_ref, dva_ref, dq_ref, sem):
        g, ki, qi = pl.program_id(0), pl.program_id(1), pl.program_id(2)
        _zero_first((ki == 0) & (qi == 0), dq_ref)
        _zero_first(qi == 0, dka_ref, dva_ref)
        rows = pl.ds(pl.multiple_of(qi * tb, tb), tb)

        def step(masked):
            if masked:
                keep = lax.broadcasted_iota(jnp.int32, (tb, tb), 0) <= lax.broadcasted_iota(jnp.int32, (tb, tb), 1)
            for h in range(hg):
                st = _dot_nt(k_ref[h], q_ref[h])
                if masked:
                    st = jnp.where(keep, st, NEG)
                pt = jnp.exp2(st)
                dob = do_ref[h]
                dva_ref[h] += _dot(pt.astype(BF16), dob)
                dst = (pt * _dot_nt(v_ref[h], dob)).astype(BF16)
                dka_ref[h] += _dot(dst, q_ref[h])
                dq_ref[h, rows, :] += _dot_tn(dst, k_ref[h])

        @pl.when(qi > ki)
        def _():
            step(False)

        @pl.when(qi == ki)
        def _():
            step(True)

        @pl.when(qi == nb - 1)
        def _():
            dk_ref[...] = (dka_ref[...] * LN2).astype(BF16)
            dv_ref[...] = dva_ref[...].astype(BF16)

        @pl.when((ki == nb - 1) & (qi == nb - 1))
        def _():
            cp = pltpu.make_async_copy(dq_ref, dq_hbm.at[pl.ds(g * hg, hg)], sem)
            cp.start()
            cp.wait()

    kspec = pl.BlockSpec((hg, tb, LANES), lambda g, ki, qi: (g, ki, 0))
    qspec = pl.BlockSpec((hg, tb, LANES), lambda g, ki, qi: (g, jnp.maximum(qi, ki), 0))
    hm = jax.ShapeDtypeStruct((N_HEADS, s, LANES), BF16)
    body, first_specs, first = _ordered_after(body, order)
    return pl.pallas_call(
        body, name="mla_flash_bwd", grid=(N_HEADS // hg, nb, nb),
        in_specs=first_specs + [qspec, kspec, kspec, qspec],
        out_specs=[kspec, kspec, ANY],
        out_shape=[hm, hm, jax.ShapeDtypeStruct((N_HEADS, s, LANES), F32)],
        scratch_shapes=[pltpu.VMEM((hg, tb, LANES), F32), pltpu.VMEM((hg, tb, LANES), F32),
                        pltpu.VMEM((hg, s, LANES), F32), pltpu.SemaphoreType.DMA],
        compiler_params=_cp("arbitrary", "arbitrary", "arbitrary"),
    )(*first, qb, k, v, do)


def _mla_post_call(dq, dk, dv, cq, ckv, qnw, kvnw, wq, wk, wv, cos_m, sin_m, ts):
    s = cq.shape[0]

    def body(dq_ref, dk_ref, dv_ref, cq_ref, ckv_ref, qnw_ref, kvnw_ref, wq_ref, wk_ref, wv_ref, cm_ref, sm_ref,
             dcq_ref, dckv_ref, dkpe_ref, dwq_ref, dwk_ref, dwv_ref, dqnw_ref, dkvnw_ref):
        _zero_first(pl.program_id(0) == 0, dwq_ref, dwk_ref, dwv_ref, dqnw_ref, dkvnw_ref)
        cqv, ckvv = cq_ref[...], ckv_ref[...]
        rq, rkv = _rstd(cqv), _rstd(ckvv)
        qh_, kvh_ = cqv * rq, ckvv * rkv
        qnw_v, kvnw_v = qnw_ref[...], kvnw_ref[...]
        cqn = (qh_ * qnw_v).astype(BF16)
        ckvn = (kvh_ * kvnw_v).astype(BF16)
        cm, sm = cm_ref[...], sm_ref[...]
        dcqn = jnp.zeros((ts, Q_RANK), F32)
        dckvn = jnp.zeros((ts, KV_RANK), F32)
        dkpe = jnp.zeros((ts, LANES), F32)
        for h in range(N_HEADS):
            dqu = _unrope(dq_ref[h] * SM_SCALE, cm, sm, ROPE // 2).astype(BF16)
            dwq_ref[h] += _dot_tn(cqn, dqu)
            dcqn = dcqn + _dot_nt(dqu, wq_ref[h])
            dkb, dvb = dk_ref[h], dv_ref[h]
            dkpe = dkpe + dkb.astype(F32)
            dwk_ref[h] += _dot_tn(ckvn, dkb)
            dwv_ref[h] += _dot_tn(ckvn, dvb)
            dckvn = dckvn + _dot_nt(dkb, wk_ref[h]) + _dot_nt(dvb, wv_ref[h])
        lane = lax.broadcasted_iota(jnp.int32, (ts, LANES), 1)
        dkpe = jnp.where((lane >= KPE_LO) & (lane < KPE_LO + ROPE), dkpe, 0.0)
        dkpe_ref[...] = _unrope(dkpe, cm, sm, ROPE // 2).astype(BF16)
        dqnw_ref[...] += _colsum(dcqn * qh_)
        dkvnw_ref[...] += _colsum(dckvn * kvh_)
        dcq_ref[...] = _norm_bwd(dcqn, qh_, rq, qnw_v).astype(BF16)
        dckv_ref[...] = _norm_bwd(dckvn, kvh_, rkv, kvnw_v).astype(BF16)

    sd = jax.ShapeDtypeStruct
    hm = _hrow(N_HEADS, ts, LANES)
    return pl.pallas_call(
        body, name="mla_post", grid=(s // ts,),
        in_specs=[hm, hm, hm, _row(ts, Q_RANK), _row(ts, KV_RANK), _full((1, Q_RANK)), _full((1, KV_RANK)),
                  _full((N_HEADS, Q_RANK, LANES)), _full((N_HEADS, KV_RANK, LANES)), _full((N_HEADS, KV_RANK, LANES)),
                  _row(ts, LANES), _row(ts, LANES)],
        out_specs=[_row(ts, Q_RANK), _row(ts, KV_RANK), _row(ts, LANES),
                   _full((N_HEADS, Q_RANK, LANES)), _full((N_HEADS, KV_RANK, LANES)), _full((N_HEADS, KV_RANK, LANES)),
                   _full((1, Q_RANK)), _full((1, KV_RANK))],
        out_shape=[sd((s, Q_RANK), BF16), sd((s, KV_RANK), BF16), sd((s, LANES), BF16),
                   sd((N_HEADS, Q_RANK, LANES), F32), sd((N_HEADS, KV_RANK, LANES), F32), sd((N_HEADS, KV_RANK, LANES), F32),
                   sd((1, Q_RANK), F32), sd((1, KV_RANK), F32)],
        compiler_params=_cp("arbitrary"),
    )(dq, dk, dv, cq, ckv, qnw, kvnw, wq, wk, wv, cos_m, sin_m)


def _in_bwd_call(parts, x, r1, anw, dx1, win, ts):
    s = x.shape[0]
    widths = [p.shape[1] for p in parts]
    np_ = len(parts)

    def body(*refs):
        p_refs = refs[:np_]
        x_ref, r_ref, anw_ref, dx1_ref, w_ref, dx_ref, dw_ref, danw_ref = refs[np_:]
        _zero_first(pl.program_id(0) == 0, dw_ref, danw_ref)
        dproj = jnp.concatenate([p[...] for p in p_refs], axis=-1)
        r, anw_v = r_ref[...], anw_ref[...]
        xh = x_ref[...] * r
        dw_ref[...] += _dot_tn((xh * anw_v).astype(BF16), dproj)
        dh = _dot_nt(dproj, w_ref[...])
        danw_ref[...] += _colsum(dh * xh)
        dx_ref[...] = dx1_ref[...] + _norm_bwd(dh, xh, r, anw_v)

    sd = jax.ShapeDtypeStruct
    return pl.pallas_call(
        body, name="in_proj_bwd", grid=(s // ts,),
        in_specs=[_row(ts, w) for w in widths]
        + [_row(ts, D_MODEL), _row(ts, 1), _full((1, D_MODEL)), _row(ts, D_MODEL), _full((D_MODEL, IN_EXT))],
        out_specs=[_row(ts, D_MODEL), _full((D_MODEL, IN_EXT)), _full((1, D_MODEL))],
        out_shape=[sd((s, D_MODEL), F32), sd((D_MODEL, IN_EXT), F32), sd((1, D_MODEL), F32)],
        compiler_params=_cp("arbitrary"),
    )(*parts, x, r1, anw, dx1, win)


def _local_step(x, positions, tgt, w, small, ex=None):
    s = x.shape[0]
    t = _tiles(s)
    ex = _Exchanges(w) if ex is None else ex
    f = _forward(x, positions, w, small, ex)
    pw, rc = f["pw"], f["rc"]
    cos_r, sin_r, cos_m, sin_m = f["tabs"]
    dx2, loss, g_fw = _loss_call(f["x2"], tgt, small["final_norm_w"], t["ts"])
    du, dx1, g_cw, g_cb, g_fnw, g_wd = _ffn_bwd_call(dx2, f["u"], w["conv_w"], small["conv_b"], pw["wdown"], pw["wup"],
                                                     f["x1"], f["r2"], small["ffn_norm_w"], t["t2"])
    g_wup = _dw_norm_call(f["x1"], f["r2"], small["ffn_norm_w"], du, t["ts"], F2 // 4, "dw_up")
    started = ex.mlp_grads(dict(w_up=g_wup, w_down=g_wd))
    dy_ret, do, g_wout = _out_bwd_call(dx1, f["y_ret"], f["y_mla"], pw["wout"], t["ts"], started)
    started = ex.behind_out_bwd(g_wout)
    drq, dg, do_ret, g_gnw = _ret_bwd_q_call(f["q"], f["k"], f["v"], f["o_ret"], f["g"], dy_ret, small["ret_gn_w"], rc, cos_r, sin_r, t["tr"])
    drk, drv = _ret_bwd_kv_call(f["q"], f["k"], f["v"], do_ret, rc, cos_r, sin_r, t["tr"])
    dmk, dmv, dmq = _flash_bwd_call(f["mqb"], f["mk"], f["mv"], do, t["tb"], started)
    ex.behind_attention(dmk)
    dcq, dckv, dkpe, g_wq, g_wk, g_wv, g_qnw, g_kvnw = _mla_post_call(
        dmq, dmk, dmv, f["cq"], f["ckv"], small["mla_q_norm_w"], small["mla_kv_norm_w"], pw["wq"], pw["wk"], pw["wv"], cos_m, sin_m, t["ts"])
    gx, g_win_ext, g_anw = _in_bwd_call([drq, drk, drv, dg, dcq, dckv, dkpe], x, f["r1"], small["attn_norm_w"], dx1, pw["win"], t["ts"])
    lo = IN_W - ROPE
    g_win = jnp.concatenate([g_win_ext[:, :lo], g_win_ext[:, lo + KPE_LO:lo + KPE_LO + ROPE]], -1)
    g_wuq = g_wq.transpose(1, 0, 2)[:, :, :HEAD + ROPE].reshape(Q_RANK, N_HEADS * (HEAD + ROPE))
    g_wukv = jnp.concatenate([g_wk[:, :, :HEAD], g_wv[:, :, :HEAD]], -1).transpose(1, 0, 2).reshape(KV_RANK, 2 * MLA_W)
    gw = dict(w_in=g_win, w_uq=g_wuq, w_ukv=g_wukv, w_out=g_wout, w_up=g_wup,
              conv_w=g_cw, w_down=g_wd)
    gs = dict(attn_norm_w=g_anw, ret_gn_w=g_gnw, mla_q_norm_w=g_qnw, mla_kv_norm_w=g_kvnw, ffn_norm_w=g_fnw,
              conv_b=g_cb, final_norm_w=g_fw)
    return loss, gx, gw, gs


MESH_ID = pl.DeviceIdType.MESH
ANY = pl.BlockSpec(memory_space=pl.ANY)
VMEM_SPEC = pl.BlockSpec(memory_space=pltpu.VMEM)
N_DEV = 8
GROUP_A = (("w_in", (D_MODEL, IN_W // 4), 1), ("w_uq", (Q_RANK, 192), 1), ("w_ukv", (KV_RANK, 256), 1),
           ("w_out", (D_MODEL // 4, D_MODEL), 0))
GROUP_B = (("w_up", (D_MODEL, F2 // 4), 1), ("w_down", (D_FF // 4, D_MODEL), 0))
HBM_SPEC = pl.BlockSpec(memory_space=pltpu.HBM)
SEM_SPEC = pl.BlockSpec(memory_space=pltpu.SEMAPHORE)


def _mesh_pos():
    return lax.axis_index("x"), lax.axis_index("y"), lax.axis_index("c")


def _other_chips(x, y):
    return [(1 - x, y), (x, 1 - y), (1 - x, 1 - y)]


def _remote(src, dst, send_sems, recv_sems, k, dev):
    return pltpu.make_async_remote_copy(src_ref=src, dst_ref=dst, send_sem=send_sems.at[k], recv_sem=recv_sems.at[k],
                                        device_id=dev, device_id_type=MESH_ID)


def _gather_list_call(parts, tag):
    n = len(parts)

    def body(*refs):
        srcs, outs, (send_sems, recv_sems) = refs[:n], refs[n:2 * n], refs[2 * n:]
        x, y, c = _mesh_pos()
        sm = 2 * x + y
        chips = _other_chips(x, y)
        sib = (x, y, 1 - c)
        rc = lambda k, src, dst, dev: _remote(src, dst, send_sems, recv_sems, k, dev)
        first = [rc(7 * i + j, srcs[i].at[c], outs[i].at[sm, c], (cx, cy, c)) for i in range(n) for j, (cx, cy) in enumerate(chips)]
        own = [rc(7 * i + 6, srcs[i], outs[i].at[sm], sib) for i in range(n)]
        for cp in first + own:
            cp.start()
        passed = []
        for j, (cx, cy) in enumerate(chips):
            for i in range(n):
                land = outs[i].at[2 * cx + cy, c]
                rc(7 * i + j, srcs[i].at[c], land, (cx, cy, c)).wait_recv()
                cp = rc(7 * i + 3 + j, land, land, sib)
                cp.start()
                passed.append(cp)
        for j, (cx, cy) in enumerate(chips):
            for i in range(n):
                rc(7 * i + 3 + j, srcs[i].at[c], outs[i].at[2 * cx + cy, 1 - c], sib).wait_recv()
        for cp in own:
            cp.wait_recv()
        for cp in first + passed + own:
            cp.wait_send()

    return pl.pallas_call(
        body, name="weights_all_gather_" + tag,
        in_specs=[ANY] * n, out_specs=[ANY] * n,
        out_shape=[jax.ShapeDtypeStruct((4,) + p.shape, p.dtype) for p in parts],
        scratch_shapes=[pltpu.SemaphoreType.DMA((7 * n,)), pltpu.SemaphoreType.DMA((7 * n,))],
    )(*parts)


def _direct_gather_copies(srcs, lands, send_sems, recv_sems):
    x, y, c = _mesh_pos()
    sm = 2 * x + y
    sends, recvs = [], []
    for i, (src, land) in enumerate(zip(srcs, lands)):
        for j, (cx, cy) in enumerate(_other_chips(x, y)):
            for t in range(2):
                sends.append(_remote(src.at[c], land.at[sm, c], send_sems, recv_sems, 13 * i + 4 * j + 2 * c + t, (cx, cy, t)))
                recvs.append(_remote(src.at[t], land.at[2 * cx + cy, t], send_sems, recv_sems, 13 * i + 4 * j + 2 * t + c, (cx, cy, t)))
        sends.append(_remote(src, land.at[sm], send_sems, recv_sems, 13 * i + 12, (x, y, 1 - c)))
        recvs.append(_remote(src, land.at[sm], send_sems, recv_sems, 13 * i + 12, (x, y, 1 - c)))
    return sends, recvs


def _sibling_copies(srcs, lands, send_sems, recv_sems):
    x, y, c = _mesh_pos()
    cps = [_remote(src.at[s, 1 - c], land.at[s], send_sems, recv_sems, 4 * i + s, (x, y, 1 - c))
           for i, (src, land) in enumerate(zip(srcs, lands)) for s in range(4)]
    return cps, cps


def _chips_copies(srcs, lands, send_sems, recv_sems):
    x, y, c = _mesh_pos()
    cps = [_remote(src.at[2 * cx + cy], land.at[j], send_sems, recv_sems, 3 * i + j, (cx, cy, c))
           for i, (src, land) in enumerate(zip(srcs, lands)) for j, (cx, cy) in enumerate(_other_chips(x, y))]
    return cps, cps


def _share_copies(srcs, lands, send_sems, recv_sems):
    x, y, c = _mesh_pos()
    cps = [_remote(src, land, send_sems, recv_sems, i, (x, y, 1 - c)) for i, (src, land) in enumerate(zip(srcs, lands))]
    return cps, cps


def _exchange_call(name, copies, srcs, land_shapes, n_sems):
    n = len(srcs)

    def body(*refs):
        sends, recvs = copies(refs[:n], refs[n:2 * n], refs[2 * n], refs[2 * n + 1])
        for cp in sends:
            cp.start()
        for cp in sends:
            cp.wait_send()
        for cp in recvs:
            cp.wait_recv()

    return pl.pallas_call(
        body, name=name, in_specs=[ANY] * n, out_specs=[ANY] * n, out_shape=list(land_shapes),
        scratch_shapes=[pltpu.SemaphoreType.DMA((n_sems,)), pltpu.SemaphoreType.DMA((n_sems,))],
    )(*srcs)


def _exchange_start_call(name, copies, srcs, land_shapes, n_sems, order=None):
    n = len(srcs)
    extra = [] if order is None else [order]
    k = 2 * n + len(extra)

    def body(*refs):
        sends, _ = copies(refs[:n], refs[n:2 * n], refs[k], refs[k + 1])
        for cp in sends:
            cp.start()
        refs[-1][...] = jnp.zeros_like(refs[-1])

    hbm = lambda a: pltpu.with_memory_space_constraint(a, pltpu.HBM)
    lands = [hbm(lax.empty(sd.shape, sd.dtype)) for sd in land_shapes]
    sem = pltpu.SemaphoreType.DMA((n_sems,))
    out = pl.pallas_call(
        body, name=name,
        out_shape=(sem, sem, *[pltpu.HBM(a.shape, a.dtype) for a in list(srcs) + lands], jax.ShapeDtypeStruct((8, LANES), F32)),
        in_specs=[HBM_SPEC] * (2 * n) + [ANY] * len(extra), out_specs=(SEM_SPEC, SEM_SPEC, *[HBM_SPEC] * (2 * n), VMEM_SPEC),
        input_output_aliases={i: 2 + i for i in range(2 * n)},
        compiler_params=pltpu.CompilerParams(has_side_effects=pltpu.SideEffectType.DATAFLOW_SIDE_EFFECTING),
    )(*[hbm(a) for a in srcs], *lands, *extra)
    return out[0], out[1], out[2:2 + n], out[2 + n:2 + 2 * n], out[-1]


def _exchange_wait_call(name, copies, started, after):
    send_sems, recv_sems, srcs, lands, _ = started
    n = len(srcs)

    def body(*refs):
        sends, recvs = copies(refs[:n], refs[n:2 * n], refs[2 * n], refs[2 * n + 1])
        for cp in sends:
            cp.wait_send()
        for cp in recvs:
            cp.wait_recv()

    out = pl.pallas_call(
        body, name=name,
        out_shape=tuple(pltpu.HBM(a.shape, a.dtype) for a in list(srcs) + list(lands)),
        in_specs=[HBM_SPEC] * (2 * n) + [SEM_SPEC, SEM_SPEC, ANY], out_specs=tuple([HBM_SPEC] * (2 * n)),
        input_output_aliases={i: i for i in range(2 * n)},
        compiler_params=pltpu.CompilerParams(has_side_effects=pltpu.SideEffectType.DATAFLOW_SIDE_EFFECTING),
    )(*srcs, *lands, send_sems, recv_sems, after)
    return out[:n], out[n:]


def _rows_tile(rows, width, itemsize=4):
    limit = max(16, (3 << 20) // (width * itemsize))
    if rows <= limit:
        return rows
    return max(t for t in range(16, limit + 1, 16) if rows % t == 0)


def _sum_sibling_call(g, buf, c, name):
    _, _, rh, w = g.shape
    tile = _rows_tile(rh, w)

    def body(c_ref, g_ref, b_ref, p_ref, pb_ref):
        p = g_ref[...] + b_ref[...]
        p_ref[...] = p
        pb_ref[...] = p.astype(BF16)

    blk = pl.BlockSpec((None, tile, w), lambda s, i, c_ref: (s, i, 0))
    return pl.pallas_call(
        body, name=name,
        grid_spec=pltpu.PrefetchScalarGridSpec(
            num_scalar_prefetch=1, grid=(4, rh // tile),
            in_specs=[pl.BlockSpec((None, None, tile, w), lambda s, i, c_ref: (s, c_ref[0], i, 0)), blk],
            out_specs=[blk, blk]),
        out_shape=[jax.ShapeDtypeStruct((4, rh, w), F32), jax.ShapeDtypeStruct((4, rh, w), BF16)],
        compiler_params=_cp("parallel", "parallel"),
    )(c, g, buf)


def _sum_chips_call(p, buf, sm, name):
    _, rh, w = p.shape
    tile = _rows_tile(rh, w)

    def body(sm_ref, p_ref, b_ref, f_ref):
        f_ref[...] = ((p_ref[...] + b_ref[0].astype(F32)) + b_ref[1].astype(F32)) + b_ref[2].astype(F32)

    return pl.pallas_call(
        body, name=name,
        grid_spec=pltpu.PrefetchScalarGridSpec(
            num_scalar_prefetch=1, grid=(rh // tile,),
            in_specs=[pl.BlockSpec((None, tile, w), lambda i, sm_ref: (sm_ref[0], i, 0)),
                      pl.BlockSpec((3, tile, w), lambda i, sm_ref: (0, i, 0))],
            out_specs=pl.BlockSpec((tile, w), lambda i, sm_ref: (i, 0))),
        out_shape=jax.ShapeDtypeStruct((rh, w), F32),
        compiler_params=_cp("parallel"),
    )(sm, p, buf)


def _adamw_halves_call(w, g_mine, g_sib, c, m, v, name):
    r, wd = w.shape
    rh = r // 2
    tile = _rows_tile(rh, wd)
    nt = rh // tile

    def body(c_ref, w_ref, gm_ref, gs_ref, m_ref, v_ref, g_ref, d_ref, nm_ref, nv_ref):
        gv = jnp.where(pl.program_id(0) == c_ref[0], gm_ref[...], gs_ref[...])
        g_ref[...] = gv
        nm = ADAM_B1 * m_ref[...] + (1.0 - ADAM_B1) * gv
        nv = ADAM_B2 * v_ref[...] + (1.0 - ADAM_B2) * jnp.square(gv)
        m_hat = nm / (1.0 - ADAM_B1 ** ADAM_STEP)
        v_hat = nv / (1.0 - ADAM_B2 ** ADAM_STEP)
        d_ref[...] = -ADAM_LR * (m_hat / (jnp.sqrt(v_hat) + ADAM_EPS) + ADAM_WD * w_ref[...])
        nm_ref[...] = nm
        nv_ref[...] = nv

    whole = pl.BlockSpec((tile, wd), lambda h, i, c_ref: (h * nt + i, 0))
    half = pl.BlockSpec((tile, wd), lambda h, i, c_ref: (i, 0))
    sd = jax.ShapeDtypeStruct((r, wd), F32)
    return pl.pallas_call(
        body, name=name,
        grid_spec=pltpu.PrefetchScalarGridSpec(
            num_scalar_prefetch=1, grid=(2, nt),
            in_specs=[whole, half, half, whole, whole], out_specs=[whole] * 4),
        out_shape=[sd, sd, sd, sd],
        compiler_params=_cp("parallel", "parallel"),
    )(c, w, g_mine, g_sib, m, v)


def _exchange8_call(vec, reduce, name):
    rows = vec.shape[0]

    def body(v_ref, out_ref, *rest):
        slots, send_sems, recv_sems = (rest if reduce else (out_ref,) + rest)
        x, y, c = _mesh_pos()
        me = 4 * x + 2 * y + c
        slots[me] = v_ref[...]

        def rcopy(k, to_me):
            bx, by, bc = (k >> 2) & 1, (k >> 1) & 1, k & 1
            px, py, pc = (1 - x if bx else x), (1 - y if by else y), (1 - c if bc else c)
            slot = 4 * px + 2 * py + pc if to_me else me
            return pltpu.make_async_remote_copy(src_ref=v_ref, dst_ref=slots.at[slot], send_sem=send_sems.at[k - 1],
                                                recv_sem=recv_sems.at[k - 1], device_id=(px, py, pc), device_id_type=MESH_ID)

        for k in range(1, N_DEV):
            rcopy(k, False).start()
        for k in range(1, N_DEV):
            rcopy(k, True).wait_recv()
        for k in range(1, N_DEV):
            rcopy(k, False).wait_send()
        if reduce:
            tot = slots[0]
            for d in range(1, N_DEV):
                tot = tot + slots[d]
            out_ref[...] = tot

    stack = jax.ShapeDtypeStruct((N_DEV, rows, LANES), F32)
    return pl.pallas_call(
        body, name=name,
        in_specs=[VMEM_SPEC], out_specs=VMEM_SPEC,
        out_shape=jax.ShapeDtypeStruct((rows, LANES), F32) if reduce else stack,
        scratch_shapes=([pltpu.VMEM((N_DEV, rows, LANES), F32)] if reduce else [])
        + [pltpu.SemaphoreType.DMA((N_DEV - 1,)), pltpu.SemaphoreType.DMA((N_DEV - 1,))],
    )(vec)


def _adamw_call(w, g, m, v, name):
    r, c = w.shape
    rb = r if r <= 256 else (256 if r % 256 == 0 else 352)
    assert r % rb == 0

    def body(w_ref, g_ref, m_ref, v_ref, d_ref, nm_ref, nv_ref):
        gv = g_ref[...]
        nm = ADAM_B1 * m_ref[...] + (1.0 - ADAM_B1) * gv
        nv = ADAM_B2 * v_ref[...] + (1.0 - ADAM_B2) * jnp.square(gv)
        m_hat = nm / (1.0 - ADAM_B1 ** ADAM_STEP)
        v_hat = nv / (1.0 - ADAM_B2 ** ADAM_STEP)
        d_ref[...] = -ADAM_LR * (m_hat / (jnp.sqrt(v_hat) + ADAM_EPS) + ADAM_WD * w_ref[...])
        nm_ref[...] = nm
        nv_ref[...] = nv

    spec = pl.BlockSpec((rb, c), lambda i: (i, 0))
    sd = jax.ShapeDtypeStruct((r, c), F32)
    return pl.pallas_call(
        body, name=name, grid=(r // rb,),
        in_specs=[spec] * 4, out_specs=[spec] * 3, out_shape=[sd, sd, sd],
        compiler_params=_cp("parallel"),
    )(w, g, m, v)


SMALL = (("attn_norm_w", D_MODEL), ("ret_gn_w", RET_W), ("mla_q_norm_w", Q_RANK), ("mla_kv_norm_w", KV_RANK),
         ("ffn_norm_w", D_MODEL), ("conv_b", F2), ("final_norm_w", D_MODEL))
WEIGHT_ORDER = ("attn_norm_w", "w_in", "ret_gn_w", "mla_q_norm_w", "w_uq", "mla_kv_norm_w", "w_ukv", "w_out",
                "ffn_norm_w", "w_up", "conv_w", "conv_b", "w_down", "final_norm_w")


def _pad_rows(flat, rows):
    return jnp.concatenate([flat, jnp.zeros((rows * LANES - flat.shape[0],), flat.dtype)]).reshape(rows, LANES)


def kernel(x, positions, attn_norm_w, w_in, ret_gn_w, mla_q_norm_w, w_uq, mla_kv_norm_w, w_ukv, w_out, ffn_norm_w, w_up, conv_w, conv_b, w_down, final_norm_w, loss_target, m_attn_norm_w, m_w_in, m_ret_gn_w, m_mla_q_norm_w, m_w_uq, m_mla_kv_norm_w, m_w_ukv, m_w_out, m_ffn_norm_w, m_w_up, m_conv_w, m_conv_b, m_w_down, m_final_norm_w, v_attn_norm_w, v_w_in, v_ret_gn_w, v_mla_q_norm_w, v_w_uq, v_mla_kv_norm_w, v_w_ukv, v_w_out, v_ffn_norm_w, v_w_up, v_conv_w, v_conv_b, v_w_down, v_final_norm_w):
    args = dict(locals())
    cx, cy, cc = _mesh_pos()
    sm = 2 * cx + cy

    c_arr, sm_arr = cc.reshape(1).astype(jnp.int32), sm.reshape(1).astype(jnp.int32)
    sds = jax.ShapeDtypeStruct

    def my_shards(group):
        return [args[n][0].astype(BF16).reshape(2, r // 2, c) for n, (r, c), _ in group]

    def full_weights(gathered, group):
        full = {}
        for (n, (r, c), axis), got in zip(group, gathered):
            piece = got.reshape(4, r, c)
            full[n] = piece if n == "w_up" else (piece.transpose(1, 0, 2).reshape(r, 4 * c) if axis == 1 else piece.reshape(4 * r, c))
        return full

    def by_owner(gw, group):
        out = []
        for n, (r, c), axis in group:
            g = gw[n]
            if axis == 1 and g.ndim == 2:
                g = g.reshape(r, 4, c).transpose(1, 0, 2)
            out.append(g.reshape(4, 2, r // 2, c))
        return out

    def sibling_shapes(gs):
        return [sds((4,) + g.shape[2:], F32) for g in gs]

    def chip_sums(gs, bufs, group):
        res = [_sum_sibling_call(g, b, c_arr, "grads_sum_sibling_" + n) for g, b, (n, _, _) in zip(gs, bufs, group)]
        return [p for p, _ in res], [pb for _, pb in res]

    def chips_shapes(pbs):
        return [sds((3,) + pb.shape[1:], BF16) for pb in pbs]

    def totals(ps, lands, group, tag):
        fins = [_sum_chips_call(p, l, sm_arr, "grads_sum_chips_" + n) for p, l, (n, _, _) in zip(ps, lands, group)]
        sibs = _exchange_call("grads_rs_share_" + tag, _share_copies, fins, [sds(f.shape, F32) for f in fins], len(fins))
        return {n: (f, s) for (n, _, _), f, s in zip(group, fins, sibs)}

    class StepExchanges(_Exchanges):
        def __init__(self, order):
            shards = my_shards(GROUP_B)
            self.gather = _exchange_start_call("weights_gather_start_b", _direct_gather_copies, shards,
                                               [sds((4,) + s.shape, BF16) for s in shards], 13 * len(shards), order)
            self.red = None

        def token(self):
            return self.gather[4][0:1, 0:1]

        def mlp_weights(self, after):
            return full_weights(_exchange_wait_call("weights_gather_wait_b", _direct_gather_copies, self.gather, after)[1], GROUP_B)

        def mlp_grads(self, gw):
            gs = by_owner(gw, GROUP_B)
            self.step1 = _exchange_start_call("grads_rs_sibling_start_b", _sibling_copies, gs, sibling_shapes(gs), 4 * len(gs))
            return self.step1[4]

        def behind_out_bwd(self, after):
            gs, bufs = _exchange_wait_call("grads_rs_sibling_wait_b", _sibling_copies, self.step1, after)
            self.ps, pbs = chip_sums(gs, bufs, GROUP_B)
            self.step2 = _exchange_start_call("grads_rs_chips_start_b", _chips_copies, pbs, chips_shapes(pbs), 3 * len(pbs))
            return self.step2[4]

        def behind_attention(self, after):
            _, lands = _exchange_wait_call("grads_rs_chips_wait_b", _chips_copies, self.step2, after)
            self.red = totals(self.ps, lands, GROUP_B, "b")

    full = full_weights(_gather_list_call(my_shards(GROUP_A), "a"), GROUP_A)
    cw_rows = 40
    cw_all = _exchange8_call(_pad_rows(conv_w[0].reshape(-1), cw_rows), False, "conv_w_all_gather")
    ex = StepExchanges(cw_all)
    cw_all = cw_all[0::2].reshape(4, cw_rows * LANES)[:, :3 * F2 // 4].reshape(4, 3, F2 // 4)
    full["conv_w"] = cw_all.transpose(1, 0, 2).reshape(3, F2)
    small = {n: args[n].reshape(1, d) for n, d in SMALL}
    small["attn_norm_w"] = small["attn_norm_w"] + ex.token()

    loss, gx, gw, gs = _local_step(x[0], positions[0], loss_target[0], full, small, ex)

    ga = by_owner(gw, GROUP_A)
    bufs = _exchange_call("grads_rs_sibling_a", _sibling_copies, ga, sibling_shapes(ga), 4 * len(ga))
    ps, pbs = chip_sums(ga, bufs, GROUP_A)
    lands = _exchange_call("grads_rs_chips_a", _chips_copies, pbs, chips_shapes(pbs), 3 * len(pbs))
    halves = {**ex.red, **totals(ps, lands, GROUP_A, "a")}

    vec = jnp.concatenate([gs[n].reshape(-1) for n, _ in SMALL] + [gw["conv_w"].reshape(-1), loss.reshape(-1)])
    tot = _exchange8_call(_pad_rows(vec, 216), True, "small_all_reduce").reshape(-1)
    red, off = {}, 0
    for n, d in SMALL:
        red[n] = tot[off:off + d].reshape(1, d)
        off += d
    red["conv_w"] = lax.dynamic_slice(tot[off:off + 3 * F2].reshape(3, F2), (0, sm * (F2 // 4)), (3, F2 // 4))
    loss_tot = tot[off + 3 * F2]

    grads, deltas, new_m, new_v = [], [], [], []
    for n in WEIGHT_ORDER:
        shape = args[n].shape
        two_d = (1, shape[0]) if len(shape) == 1 else shape[-2:]
        wmv = [args[k + n].reshape(two_d) for k in ("", "m_", "v_")]
        if n in halves:
            g, d, nm, nv = _adamw_halves_call(wmv[0], *halves[n], c_arr, wmv[1], wmv[2], "adamw_" + n)
        else:
            g = red[n].reshape(two_d)
            d, nm, nv = _adamw_call(wmv[0], g, wmv[1], wmv[2], "adamw_" + n)
        grads.append(g.reshape(shape))
        deltas.append(d.reshape(shape))
        new_m.append(nm.reshape(shape))
        new_v.append(nv.reshape(shape))
    return (loss_tot, gx[None], *grads, *deltas, *new_m, *new_v)
```

```python
import functools
import math

import numpy as np
import jax
import jax.numpy as jnp
from jax import lax
from jax.experimental import pallas as pl
from jax.experimental.pallas import tpu as pltpu

F32 = jnp.float32
BF16 = jnp.bfloat16

D_MODEL = 1024
N_HEADS = 8
HEAD = 64
RET_W = N_HEADS * HEAD
MLA_W = N_HEADS * HEAD
ROPE = 32
Q_RANK = 256
KV_RANK = 128
D_FF = 2816
F2 = 2 * D_FF
IN_W = 4 * RET_W + Q_RANK + KV_RANK + ROPE
IN_EXT = 4 * RET_W + Q_RANK + KV_RANK + 128
KPE_LO = 64
ROPE_BASE = 10000.0
EPS = 1e-6
RET_CHUNK = 128
SM_SCALE = (HEAD + ROPE) ** -0.5
LOG2E = math.log2(math.e)
LN2 = math.log(2.0)
NEG = -1e30
LANES = 128
VMEM_LIMIT = 56 * 1024 * 1024

ADAM_LR = 0.001
ADAM_B1 = 0.9
ADAM_B2 = 0.999
ADAM_EPS = 1e-08
ADAM_WD = 0.01
ADAM_STEP = 10


def _cp(*sem):
    return pltpu.CompilerParams(dimension_semantics=sem, vmem_limit_bytes=VMEM_LIMIT)


def _full(shape):
    n = len(shape)
    return pl.BlockSpec(tuple(shape), lambda *_: (0,) * n)


def _row(ts, c):
    return pl.BlockSpec((ts, c), lambda i: (i, 0))


def _hrow(h, ts, c):
    return pl.BlockSpec((h, ts, c), lambda i: (0, i, 0))


def _dot(a, b):
    return jnp.dot(a, b, preferred_element_type=F32)


def _dot_nt(a, b):
    return lax.dot_general(a, b, (((1,), (1,)), ((), ())), preferred_element_type=F32)


def _dot_tn(a, b):
    return lax.dot_general(a, b, (((0,), (0,)), ((), ())), preferred_element_type=F32)


def _dot_hi(a, b):
    hi = a.astype(BF16)
    lo = (a - hi.astype(F32)).astype(BF16)
    bb = b.astype(BF16)
    return _dot(hi, bb) + _dot(lo, bb)


def _rot_half(x, half):
    w = x.shape[-1]
    lane = lax.broadcasted_iota(jnp.int32, x.shape, x.ndim - 1)
    first = (lane % (2 * half)) < half
    return jnp.where(first, -pltpu.roll(x, w - half, x.ndim - 1), pltpu.roll(x, half, x.ndim - 1))


def _rope(x, cos, sin, half):
    return x * cos + _rot_half(x, half) * sin


def _unrope(dy, cos, sin, half):
    return dy * cos - _rot_half(dy, half) * sin


def _sigmoid(g):
    return 0.5 * jnp.tanh(0.5 * g) + 0.5


def _silu(g):
    return g * _sigmoid(g)


def _rstd(x):
    return lax.rsqrt(jnp.mean(x * x, axis=-1, keepdims=True) + EPS)


def _rope_tables(positions):
    pos = positions.astype(F32)[:, None]
    s = pos.shape[0]
    inv = ROPE_BASE ** (-jnp.arange(0, HEAD, 2, dtype=F32) / HEAD)
    ang = pos * inv
    c, sn = jnp.cos(ang), jnp.sin(ang)
    cos_r = jnp.tile(jnp.concatenate([c, c], -1), (1, 2))
    sin_r = jnp.tile(jnp.concatenate([sn, sn], -1), (1, 2))
    inv = ROPE_BASE ** (-jnp.arange(0, ROPE, 2, dtype=F32) / ROPE)
    ang = pos * inv
    c, sn = jnp.cos(ang), jnp.sin(ang)
    one, zero = jnp.ones((s, KPE_LO), F32), jnp.zeros((s, KPE_LO), F32)
    cos_m = jnp.concatenate([one, c, c, one[:, :LANES - KPE_LO - ROPE]], -1)
    sin_m = jnp.concatenate([zero, sn, sn, zero[:, :LANES - KPE_LO - ROPE]], -1)
    return cos_r, sin_r, cos_m, sin_m


def _ret_consts():
    c = RET_CHUNK
    lg = np.log1p(-np.power(2.0, -5.0 - np.arange(N_HEADS, dtype=np.float64)))
    idx = np.arange(c, dtype=np.float64)
    diff = idx[:, None] - idx[None, :]
    lane_head = np.arange(LANES) // HEAD
    dmask = np.zeros((4, 2, c, c))
    zeta = np.zeros((4, c, LANES))
    xi = np.zeros((4, c, LANES))
    cd = np.zeros((4, LANES, LANES))
    bd = (lane_head[:, None] == lane_head[None, :]).astype(np.float64)
    for j in range(4):
        for hh in range(2):
            dmask[j, hh] = np.where(diff >= 0, np.exp(lg[2 * j + hh] * np.maximum(diff, 0.0)), 0.0)
        lgl = lg[2 * j + lane_head]
        zeta[j] = np.exp(lgl[None, :] * (c - 1.0 - idx[:, None]))
        xi[j] = np.exp(lgl[None, :] * (idx[:, None] + 1.0))
        cd[j] = np.exp(lgl * c)[:, None] * bd
    f = lambda a: jnp.asarray(a, F32)
    side = lambda d: np.concatenate([d[:, 0], d[:, 1]], axis=-1)
    return dict(dmask=f(side(dmask)), dmask_t=f(side(np.swapaxes(dmask, 2, 3))), zeta=f(zeta), xi=f(xi), cd=f(cd), bd=f(bd))


def _f1_call(x, anw, win, cos_r, sin_r, cos_m, sin_m, ts):
    s = x.shape[0]

    def body(x_ref, anw_ref, w_ref, cr_ref, sr_ref, cm_ref, sm_ref,
             q_ref, k_ref, v_ref, g_ref, cq_ref, ckv_ref, kpe_ref, r_ref):
        xv = x_ref[...]
        r = _rstd(xv)
        r_ref[...] = r
        h = (xv * r * anw_ref[...]).astype(BF16)
        cr, sr = cr_ref[...], sr_ref[...]
        qk = _dot(h, w_ref[:, 0:2 * RET_W])
        for j in range(4):
            sl = slice(j * LANES, (j + 1) * LANES)
            q_ref[:, sl] = _rope(qk[:, sl], cr, sr, HEAD // 2).astype(BF16)
            kk = qk[:, RET_W + j * LANES:RET_W + (j + 1) * LANES]
            k_ref[:, sl] = (_rope(kk, cr, sr, HEAD // 2) * (HEAD ** -0.5)).astype(BF16)
        v_ref[...] = _dot(h, w_ref[:, 2 * RET_W:3 * RET_W]).astype(BF16)
        g_ref[...] = _dot(h, w_ref[:, 3 * RET_W:4 * RET_W])
        o = 4 * RET_W
        cq_ref[...] = _dot(h, w_ref[:, o:o + Q_RANK])
        ckv_ref[...] = _dot(h, w_ref[:, o + Q_RANK:o + Q_RANK + KV_RANK])
        kp = _dot(h, w_ref[:, o + Q_RANK + KV_RANK:IN_EXT])
        kpe_ref[...] = _rope(kp, cm_ref[...], sm_ref[...], ROPE // 2)

    sd = jax.ShapeDtypeStruct
    return pl.pallas_call(
        body, name="f1_in_proj", grid=(s // ts,),
        in_specs=[_row(ts, D_MODEL), _full((1, D_MODEL)), _full((D_MODEL, IN_EXT)),
                  _row(ts, LANES), _row(ts, LANES), _row(ts, LANES), _row(ts, LANES)],
        out_specs=[_row(ts, RET_W), _row(ts, RET_W), _row(ts, RET_W), _row(ts, RET_W),
                   _row(ts, Q_RANK), _row(ts, KV_RANK), _row(ts, LANES), _row(ts, 1)],
        out_shape=[sd((s, RET_W), BF16), sd((s, RET_W), BF16), sd((s, RET_W), BF16), sd((s, RET_W), F32),
                   sd((s, Q_RANK), F32), sd((s, KV_RANK), F32), sd((s, LANES), F32), sd((s, 1), F32)],
        compiler_params=_cp("parallel"),
    )(x, anw, win, cos_r, sin_r, cos_m, sin_m)


def _stack_heads(a):
    lo = lax.broadcasted_iota(jnp.int32, a.shape, 1) < HEAD
    zero = jnp.zeros_like(a)
    return jnp.concatenate([jnp.where(lo, a, zero), jnp.where(lo, zero, a)], axis=0)


def _pair_product(a, b2, decay2, w2):
    return _dot((_dot_nt(a, b2) * decay2).astype(BF16), w2)


def _ret_fwd_call(q, k, v, g, gnw, rc, tr):
    s = q.shape[0]
    c = RET_CHUNK
    nc = tr // c

    def body(q_ref, k_ref, v_ref, g_ref, gnw_ref, dm_ref, zeta_ref, xi_ref, cd_ref, bd_ref, o_ref, y_ref, st_ref):
        @pl.when(pl.program_id(1) == 0)
        def _():
            st_ref[...] = jnp.zeros_like(st_ref)

        lane = lax.broadcasted_iota(jnp.int32, (c, LANES), 1)
        bd = bd_ref[...]
        chunks = [slice(ci * c, (ci + 1) * c) for ci in range(nc)]
        contrib = [_dot_tn((k_ref[rows, :].astype(F32) * zeta_ref[0]).astype(BF16), v_ref[rows, :]) * bd for rows in chunks]
        st, states = st_ref[...], []
        for ci in range(nc):
            states.append(st.astype(BF16))
            st = st * cd_ref[0] + contrib[ci]
        st_ref[...] = st
        for ci, rows in enumerate(chunks):
            qc = q_ref[rows, :]
            o_ref[rows, :] = (_dot(qc, states[ci]) * xi_ref[0]
                              + _pair_product(qc, _stack_heads(k_ref[rows, :]), dm_ref[0], _stack_heads(v_ref[rows, :])))
        o = o_ref[...]
        avg = bd * (1.0 / HEAD)
        ctr = o - _dot_hi(o, avg)
        var = _dot_hi(ctr * ctr, avg)
        y_ref[...] = (_silu(g_ref[...]) * (ctr * lax.rsqrt(var + EPS) * gnw_ref[...])).astype(BF16)

    slab = pl.BlockSpec((tr, LANES), lambda j, i: (i, j))
    sd = jax.ShapeDtypeStruct
    return pl.pallas_call(
        body, name="ret_fwd", grid=(4, s // tr),
        in_specs=[slab, slab, slab, slab, pl.BlockSpec((1, LANES), lambda j, i: (0, j)),
                  pl.BlockSpec((1, c, 2 * c), lambda j, i: (j, 0, 0)),
                  pl.BlockSpec((1, c, LANES), lambda j, i: (j, 0, 0)),
                  pl.BlockSpec((1, c, LANES), lambda j, i: (j, 0, 0)),
                  pl.BlockSpec((1, LANES, LANES), lambda j, i: (j, 0, 0)),
                  pl.BlockSpec((LANES, LANES), lambda j, i: (0, 0))],
        out_specs=[slab, slab],
        out_shape=[sd((s, RET_W), F32), sd((s, RET_W), BF16)],
        scratch_shapes=[pltpu.VMEM((LANES, LANES), F32)],
        compiler_params=_cp("parallel", "arbitrary"),
    )(q, k, v, g, gnw, rc["dmask"], rc["zeta"], rc["xi"], rc["cd"], rc["bd"])


QK_AUX = HEAD + ROPE
V_AUX = HEAD


def _lane_pair(shape, lo, a, b, rest):
    lane = lax.broadcasted_iota(jnp.int32, shape, len(shape) - 1)
    return jnp.where(lane == lo, a, jnp.where(lane == lo + 1, b, rest))


def _hi_lo(v):
    hi = v.astype(BF16).astype(F32)
    return hi, v - hi


def _mla_pre_call(cq, ckv, kpe, qnw, kvnw, wq, wk, wv, cos_m, sin_m, ts):
    s = cq.shape[0]

    def body(cq_ref, ckv_ref, kpe_ref, qnw_ref, kvnw_ref, wq_ref, wk_ref, wv_ref, cm_ref, sm_ref, q_ref, k_ref, v_ref):
        cqv, ckvv = cq_ref[...], ckv_ref[...]
        cqn = (cqv * _rstd(cqv) * qnw_ref[...]).astype(BF16)
        ckvn = (ckvv * _rstd(ckvv) * kvnw_ref[...]).astype(BF16)
        cm, sm = cm_ref[...], sm_ref[...]
        kp = _lane_pair((ts, LANES), QK_AUX, -1.0, -1.0, kpe_ref[...])
        for h in range(N_HEADS):
            qh = _rope(_dot(cqn, wq_ref[h]), cm, sm, ROPE // 2)
            q_ref[h] = (qh * (SM_SCALE * LOG2E)).astype(BF16)
            k_ref[h] = (_dot(ckvn, wk_ref[h]) + kp).astype(BF16)
            v_ref[h] = _lane_pair((ts, LANES), V_AUX, 1.0, 1.0, _dot(ckvn, wv_ref[h])).astype(BF16)

    sd = jax.ShapeDtypeStruct
    hm = sd((N_HEADS, s, LANES), BF16)
    return pl.pallas_call(
        body, name="mla_pre", grid=(s // ts,),
        in_specs=[_row(ts, Q_RANK), _row(ts, KV_RANK), _row(ts, LANES), _full((1, Q_RANK)), _full((1, KV_RANK)),
                  _full((N_HEADS, Q_RANK, LANES)), _full((N_HEADS, KV_RANK, LANES)), _full((N_HEADS, KV_RANK, LANES)),
                  _row(ts, LANES), _row(ts, LANES)],
        out_specs=[_hrow(N_HEADS, ts, LANES)] * 3,
        out_shape=[hm, hm, hm],
        compiler_params=_cp("parallel"),
    )(cq, ckv, kpe, qnw, kvnw, wq, wk, wv, cos_m, sin_m)


def _flash_fwd_call(q, k, v, tb):
    s = q.shape[1]
    nb = s // tb
    pairs = [(a, b) for a in range(nb) for b in range(a + 1)]
    qi_of, ki_of = (jnp.asarray(np.array(col, np.int32)) for col in zip(*pairs))

    def body(qi_ref, ki_ref, q_ref, k_ref, v_ref, o_ref, qb_ref, m_ref, acc_ref):
        qi, ki = qi_ref[pl.program_id(0)], ki_ref[pl.program_id(0)]

        @pl.when(ki == 0)
        def _():
            m_ref[...] = jnp.full_like(m_ref, NEG)
            acc_ref[...] = jnp.zeros_like(acc_ref)

        def step(masked):
            if masked:
                keep = lax.broadcasted_iota(jnp.int32, (tb, tb), 1) <= lax.broadcasted_iota(jnp.int32, (tb, tb), 0)
            def finish(h, pe, alpha):
                acc_ref[h] = acc_ref[h] * alpha + _dot(pe, v_ref[h])

            nxt, pending = _dot_nt(q_ref[0], k_ref[0]), None
            for h in range(N_HEADS):
                sc = nxt
                if h + 1 < N_HEADS:
                    nxt = _dot_nt(q_ref[h + 1], k_ref[h + 1])
                if masked:
                    sc = jnp.where(keep, sc, NEG)
                m_prev = m_ref[h]
                m_new = jnp.maximum(m_prev, jnp.max(sc, axis=1, keepdims=True))
                pe = jnp.exp2(sc - jnp.tile(m_new, (1, tb // LANES))).astype(BF16)
                m_ref[h] = m_new
                if pending is not None:
                    finish(*pending)
                pending = (h, pe, jnp.exp2(m_prev - m_new))
            finish(*pending)

        @pl.when(ki < qi)
        def _():
            step(False)

        @pl.when(ki == qi)
        def _():
            step(True)
            lane = lax.broadcasted_iota(jnp.int32, (tb, LANES), 1)
            for p in range(N_HEADS // 2):
                outs = []
                for h in (2 * p, 2 * p + 1):
                    acc = acc_ref[h]
                    l = acc[:, V_AUX:V_AUX + 1]
                    outs.append(acc * (1.0 / l))
                    hi, lo = _hi_lo(m_ref[h][:, 0:1] + jnp.log(l) * LOG2E)
                    qb_ref[h] = _lane_pair((tb, LANES), QK_AUX, hi, lo, q_ref[h].astype(F32)).astype(BF16)
                o_ref[:, p * LANES:(p + 1) * LANES] = jnp.where(lane < HEAD, outs[0], pltpu.roll(outs[1], HEAD, 1)).astype(BF16)

    sd = jax.ShapeDtypeStruct
    qspec = pl.BlockSpec((N_HEADS, tb, LANES), lambda p, qi_ref, ki_ref: (0, qi_ref[p], 0))
    kspec = pl.BlockSpec((N_HEADS, tb, LANES), lambda p, qi_ref, ki_ref: (0, ki_ref[p], 0))
    return pl.pallas_call(
        body, name="mla_flash_fwd",
        grid_spec=pltpu.PrefetchScalarGridSpec(
            num_scalar_prefetch=2, grid=(len(pairs),),
            in_specs=[qspec, kspec, kspec],
            out_specs=[pl.BlockSpec((tb, MLA_W), lambda p, qi_ref, ki_ref: (qi_ref[p], 0)), qspec],
            scratch_shapes=[pltpu.VMEM((N_HEADS, tb, LANES), F32), pltpu.VMEM((N_HEADS, tb, LANES), F32)]),
        out_shape=[sd((s, MLA_W), BF16), sd((N_HEADS, s, LANES), BF16)],
        compiler_params=_cp("arbitrary"),
    )(qi_of, ki_of, q, k, v)


def _out_proj_call(x, yret, ymla, wout, ts):
    s = x.shape[0]

    def body(x_ref, yr_ref, ym_ref, w_ref, x1_ref, r_ref):
        x1 = x_ref[...] + _dot(yr_ref[...], w_ref[0:RET_W, :]) + _dot(ym_ref[...], w_ref[RET_W:, :])
        x1_ref[...] = x1
        r_ref[...] = _rstd(x1)

    sd = jax.ShapeDtypeStruct
    return pl.pallas_call(
        body, name="out_proj", grid=(s // ts,),
        in_specs=[_row(ts, D_MODEL), _row(ts, RET_W), _row(ts, MLA_W), _full((D_MODEL, D_MODEL))],
        out_specs=[_row(ts, D_MODEL), _row(ts, 1)],
        out_shape=[sd((s, D_MODEL), F32), sd((s, 1), F32)],
        compiler_params=_cp("parallel"),
    )(x, yret, ymla, wout)


W_UP_SHARD = F2 // 4


def _ffn_fwd_call(x1, r2, fnw, wup4, cw, cb, wdown, ts):
    s = x1.shape[0]
    wsh = W_UP_SHARD

    def body(x_ref, r_ref, fnw_ref, wup_ref, cw_ref, cb_ref, wd_ref, u_ref, x2_ref, carry_ref):
        _zero_first(pl.program_id(0) == 0, carry_ref)
        xv = x_ref[...]
        h = (xv * r_ref[...] * fnw_ref[...]).astype(BF16)
        conv = []
        for j in range(4):
            cols = slice(j * wsh, (j + 1) * wsh)
            ub = _dot(h, wup_ref[j]).astype(BF16)
            u_ref[:, cols] = ub
            u = ub.astype(F32)
            u1, u2 = _shifted(u, carry_ref[:, cols])
            w = cw_ref[:, cols]
            conv.append(cb_ref[:, cols] + w[0:1, :] * u2 + w[1:2, :] * u1 + w[2:3, :] * u)
            carry_ref[:, cols] = u[ts - 8:, :]
        acc = xv
        for j in range(2):
            a = (_silu(conv[j]) * conv[j + 2]).astype(BF16)
            acc = acc + _dot(a, wd_ref[j * wsh:(j + 1) * wsh, :])
        x2_ref[...] = acc

    sd = jax.ShapeDtypeStruct
    return pl.pallas_call(
        body, name="ffn_fwd", grid=(s // ts,),
        in_specs=[_row(ts, D_MODEL), _row(ts, 1), _full((1, D_MODEL)), _full((4, D_MODEL, wsh)),
                  _full((3, F2)), _full((1, F2)), _full((D_FF, D_MODEL))],
        out_specs=[_row(ts, F2), _row(ts, D_MODEL)],
        out_shape=[sd((s, F2), BF16), sd((s, D_MODEL), F32)],
        scratch_shapes=[pltpu.VMEM((8, F2), F32)],
        compiler_params=_cp("arbitrary"),
    )(x1, r2, fnw, wup4, cw, cb, wdown)


def _shifted(u, hal):
    row = lax.broadcasted_iota(jnp.int32, hal.shape, 0)
    r1, r2 = pltpu.roll(u, 1, 0), pltpu.roll(u, 2, 0)
    top1 = jnp.where(row == 0, hal[7:8, :], r1[0:8, :])
    top2 = jnp.where(row == 0, hal[6:7, :], jnp.where(row == 1, hal[7:8, :], r2[0:8, :]))
    return jnp.concatenate([top1, r1[8:, :]], axis=0), jnp.concatenate([top2, r2[8:, :]], axis=0)


def _prep_weights(w):
    win = w["w_in"]
    pad = lambda n: jnp.zeros((D_MODEL, n), win.dtype)
    win_ext = jnp.concatenate([win[:, :IN_W - ROPE], pad(KPE_LO), win[:, IN_W - ROPE:], pad(LANES - KPE_LO - ROPE)], -1)
    wuq = w["w_uq"].reshape(Q_RANK, N_HEADS, HEAD + ROPE)
    wq = jnp.concatenate([wuq, jnp.zeros((Q_RANK, N_HEADS, LANES - HEAD - ROPE), wuq.dtype)], -1).transpose(1, 0, 2)
    wukv = w["w_ukv"].reshape(KV_RANK, N_HEADS, 2 * HEAD)
    zk = jnp.zeros((KV_RANK, N_HEADS, HEAD), wukv.dtype)
    wk = jnp.concatenate([wukv[:, :, :HEAD], zk], -1).transpose(1, 0, 2)
    wv = jnp.concatenate([wukv[:, :, HEAD:], zk], -1).transpose(1, 0, 2)
    c = lambda a: a.astype(BF16)
    return dict(win=c(win_ext), wq=c(wq), wk=c(wk), wv=c(wv), wout=c(w["w_out"]))


def _prep_mlp_weights(w):
    wup = w["w_up"]
    if wup.ndim == 2:
        wup = wup.reshape(D_MODEL, 4, W_UP_SHARD).transpose(1, 0, 2)
    return dict(wup=wup.astype(BF16), wdown=w["w_down"].astype(BF16))


def _tiles(s):
    return dict(ts=min(s, 512), tr=min(s, 1024), tb=min(s, 512), tg=min(s, 512), tf=D_FF // 2, t2=min(s, 256))


class _Exchanges:
    def __init__(self, w):
        self.w = w

    def mlp_weights(self, after):
        return self.w

    def mlp_grads(self, gw):
        pass

    def behind_out_bwd(self, after):
        pass

    def behind_attention(self, after):
        pass


def _forward(x, positions, w, small, ex):
    s = x.shape[0]
    t = _tiles(s)
    pw = _prep_weights(w)
    cos_r, sin_r, cos_m, sin_m = _rope_tables(positions)
    rc = _ret_consts()
    q, k, v, g, cq, ckv, kpe, r1 = _f1_call(x, small["attn_norm_w"], pw["win"], cos_r, sin_r, cos_m, sin_m, t["ts"])
    o_ret, y_ret = _ret_fwd_call(q, k, v, g, small["ret_gn_w"], rc, t["tr"])
    mq, mk, mv = _mla_pre_call(cq, ckv, kpe, small["mla_q_norm_w"], small["mla_kv_norm_w"],
                               pw["wq"], pw["wk"], pw["wv"], cos_m, sin_m, t["ts"])
    y_mla, mqb = _flash_fwd_call(mq, mk, mv, t["tb"])
    x1, r2 = _out_proj_call(x, y_ret, y_mla, pw["wout"], t["ts"])
    pw.update(_prep_mlp_weights(ex.mlp_weights(r2)))
    u, x2 = _ffn_fwd_call(x1, r2, small["ffn_norm_w"], pw["wup"], w["conv_w"], small["conv_b"], pw["wdown"], t["t2"])
    return dict(pw=pw, tabs=(cos_r, sin_r, cos_m, sin_m), rc=rc, q=q, k=k, v=v, g=g, cq=cq, ckv=ckv, kpe=kpe, r1=r1,
                o_ret=o_ret, y_ret=y_ret, mqb=mqb, mk=mk, mv=mv, y_mla=y_mla, x1=x1, r2=r2, u=u, x2=x2)


def _norm_bwd(dh, xh, r, nw):
    dxn = dh * nw
    return r * (dxn - xh * jnp.mean(dxn * xh, axis=-1, keepdims=True))


def _ordered_after(body, order):
    if order is None:
        return body, [], []
    return (lambda order_ref, *refs: body(*refs)), [pl.BlockSpec(memory_space=pl.ANY)], [order]


def _zero_first(first, *refs):
    @pl.when(first)
    def _():
        for ref in refs:
            ref[...] = jnp.zeros_like(ref)


def _colsum(v):
    return jnp.sum(v, axis=0, keepdims=True)


def _dsilu(g, sg):
    return sg * (1.0 + g * (1.0 - sg))


def _loss_call(x2, tgt, fw, ts):
    s = x2.shape[0]

    def body(x_ref, t_ref, fw_ref, dx_ref, loss_ref, gfw_ref):
        _zero_first(pl.program_id(0) == 0, loss_ref, gfw_ref)
        xv = x_ref[...]
        r = _rstd(xv)
        xh = xv * r
        fwv = fw_ref[...]
        e = xh * fwv - t_ref[...]
        loss_ref[...] += (0.5 / D_MODEL) * _colsum(jnp.sum(e * e, axis=1, keepdims=True))
        dy = e * (1.0 / D_MODEL)
        gfw_ref[...] += _colsum(dy * xh)
        dx_ref[...] = _norm_bwd(dy, xh, r, fwv)

    sd = jax.ShapeDtypeStruct
    return pl.pallas_call(
        body, name="loss_bwd", grid=(s // ts,),
        in_specs=[_row(ts, D_MODEL), _row(ts, D_MODEL), _full((1, D_MODEL))],
        out_specs=[_row(ts, D_MODEL), _full((1, 1)), _full((1, D_MODEL))],
        out_shape=[sd((s, D_MODEL), F32), sd((1, 1), F32), sd((1, D_MODEL), F32)],
        compiler_params=_cp("arbitrary"),
    )(x2, tgt, fw)


def _ffn_bwd_call(dx2, u, cw, cb, wdown, wup4, x1, r2, fnw, ts):
    s = dx2.shape[0]
    nt = s // ts
    hb = ts // 8
    wsh = W_UP_SHARD
    rev = lambda i: nt - 1 - i

    def body(dx2_ref, u_ref, h_ref, cw_ref, cb_ref, wd_ref, wup_ref, x_ref, r_ref, fnw_ref,
             du_ref, dx1_ref, dcw_ref, dcb_ref, dfnw_ref, dwd_hbm, carry_ref, dwd_ref, sem):
        i = pl.program_id(0)
        _zero_first(i == 0, carry_ref, dwd_ref, dcw_ref, dcb_ref, dfnw_ref)
        seq_start = i == nt - 1
        dxb = dx2_ref[...].astype(BF16)
        dh = jnp.zeros((ts, D_MODEL), F32)

        def conv(cols):
            uv = u_ref[:, cols].astype(F32)
            u1, u2 = _shifted(uv, jnp.where(seq_start, 0.0, h_ref[:, cols].astype(F32)))
            w = cw_ref[:, cols]
            return cb_ref[:, cols] + w[0:1, :] * u2 + w[1:2, :] * u1 + w[2:3, :] * uv, (u2, u1, uv)

        for j in range(2):
            gcols = slice(j * wsh, (j + 1) * wsh)
            vcols = slice(D_FF + j * wsh, D_FF + (j + 1) * wsh)
            gate, gtaps = conv(gcols)
            val, vtaps = conv(vcols)
            da = _dot_nt(dxb, wd_ref[gcols, :])
            sg = _sigmoid(gate)
            sl = gate * sg
            dwd_ref[gcols, :] += _dot_tn((sl * val).astype(BF16), dxb)
            for d, cols, taps, shard in ((da * val * _dsilu(gate, sg), gcols, gtaps, j), (da * sl, vcols, vtaps, 2 + j)):
                for t in range(3):
                    dcw_ref[t:t + 1, cols] += _colsum(d * taps[t])
                dcb_ref[:, cols] += _colsum(d)
                d1, d2 = _shifted_up(d, carry_ref[:, cols])
                w = cw_ref[:, cols]
                du = (w[2:3, :] * d + w[1:2, :] * d1 + w[0:1, :] * d2).astype(BF16)
                du_ref[:, cols] = du
                dh = dh + _dot_nt(du, wup_ref[shard])
                carry_ref[:, cols] = d[0:8, :]
        r = r_ref[...]
        xh = x_ref[...] * r
        dfnw_ref[...] += _colsum(dh * xh)
        dx1_ref[...] = dx2_ref[...] + _norm_bwd(dh, xh, r, fnw_ref[...])

        @pl.when(i == nt - 1)
        def _():
            cp = pltpu.make_async_copy(dwd_ref, dwd_hbm, sem)
            cp.start()
            cp.wait()

    sd = jax.ShapeDtypeStruct
    row = lambda c: pl.BlockSpec((ts, c), lambda i: (rev(i), 0))
    once = lambda shape: pl.BlockSpec(shape, lambda i: (0,) * len(shape), pipeline_mode=pl.Buffered(1))
    return pl.pallas_call(
        body, name="ffn_bwd", grid=(nt,),
        in_specs=[row(D_MODEL), row(F2), pl.BlockSpec((8, F2), lambda i: (jnp.maximum(rev(i) * hb - 1, 0), 0)),
                  once((3, F2)), once((1, F2)), once((D_FF, D_MODEL)), once((4, D_MODEL, wsh)),
                  row(D_MODEL), row(1), once((1, D_MODEL))],
        out_specs=[row(F2), row(D_MODEL), _full((3, F2)), _full((1, F2)), _full((1, D_MODEL)), pl.BlockSpec(memory_space=pl.ANY)],
        out_shape=[sd((s, F2), BF16), sd((s, D_MODEL), F32), sd((3, F2), F32), sd((1, F2), F32), sd((1, D_MODEL), F32),
                   sd((D_FF, D_MODEL), F32)],
        scratch_shapes=[pltpu.VMEM((8, F2), F32), pltpu.VMEM((D_FF, D_MODEL), F32), pltpu.SemaphoreType.DMA],
        compiler_params=_cp("arbitrary"),
    )(dx2, u, u, cw, cb, wdown, wup4, x1, r2, fnw)


def _shifted_up(d, hal):
    n = d.shape[0]
    row = lax.broadcasted_iota(jnp.int32, hal.shape, 0)
    r1, r2 = pltpu.roll(d, n - 1, 0), pltpu.roll(d, n - 2, 0)
    end1 = jnp.where(row == 7, hal[0:1, :], r1[n - 8:, :])
    end2 = jnp.where(row == 6, hal[0:1, :], jnp.where(row == 7, hal[1:2, :], r2[n - 8:, :]))
    return jnp.concatenate([r1[:n - 8, :], end1], axis=0), jnp.concatenate([r2[:n - 8, :], end2], axis=0)


def _dw_norm_call(x, r, nw, b, ts, tn, name):
    s, n = b.shape
    k = x.shape[1]

    def body(x_ref, r_ref, nw_ref, b_ref, dw_ref):
        _zero_first(pl.program_id(1) == 0, dw_ref)
        h = (x_ref[...] * r_ref[...] * nw_ref[...]).astype(BF16)
        dw_ref[...] += _dot_tn(h, b_ref[...])

    return pl.pallas_call(
        body, name=name, grid=(n // tn, s // ts),
        in_specs=[pl.BlockSpec((ts, k), lambda j, i: (i, 0)), pl.BlockSpec((ts, 1), lambda j, i: (i, 0)),
                  pl.BlockSpec((1, k), lambda j, i: (0, 0)), pl.BlockSpec((ts, tn), lambda j, i: (i, j))],
        out_specs=pl.BlockSpec((None, k, tn), lambda j, i: (j, 0, 0)),
        out_shape=jax.ShapeDtypeStruct((n // tn, k, tn), F32),
        compiler_params=_cp("parallel", "arbitrary"),
    )(x, r, nw, b)


def _out_bwd_call(dx1, yret, ymla, wout, ts, order=None):
    s = dx1.shape[0]

    def body(dx_ref, yr_ref, ym_ref, w_ref, dyr_ref, do_ref, dwo_ref):
        _zero_first(pl.program_id(0) == 0, dwo_ref)
        dxb = dx_ref[...].astype(BF16)
        dmix = _dot_nt(dxb, w_ref[...])
        dyr_ref[...] = dmix[:, :RET_W]
        ym = ym_ref[...]
        lane = lax.broadcasted_iota(jnp.int32, (ts, LANES), 1)
        for p in range(N_HEADS // 2):
            dom = dmix[:, RET_W + p * LANES:RET_W + (p + 1) * LANES]
            prod = dom * ym[:, p * LANES:(p + 1) * LANES].astype(F32)
            for hh in range(2):
                mine = (lane >= HEAD) if hh else (lane < HEAD)
                hi, lo = _hi_lo(jnp.sum(jnp.where(mine, prod, 0.0), axis=1, keepdims=True))
                base = jnp.where(lane < HEAD, pltpu.roll(dom, HEAD, 1) if hh else dom, 0.0)
                do_ref[2 * p + hh] = _lane_pair((ts, LANES), V_AUX, -hi, -lo, base).astype(BF16)
        dwo_ref[0:RET_W, :] += _dot_tn(yr_ref[...], dxb)
        dwo_ref[RET_W:, :] += _dot_tn(ym, dxb)

    sd = jax.ShapeDtypeStruct
    body, first_specs, first = _ordered_after(body, order)
    return pl.pallas_call(
        body, name="out_proj_bwd", grid=(s // ts,),
        in_specs=first_specs + [_row(ts, D_MODEL), _row(ts, RET_W), _row(ts, MLA_W), _full((D_MODEL, D_MODEL))],
        out_specs=[_row(ts, RET_W), _hrow(N_HEADS, ts, LANES), _full((D_MODEL, D_MODEL))],
        out_shape=[sd((s, RET_W), F32), sd((N_HEADS, s, LANES), BF16), sd((D_MODEL, D_MODEL), F32)],
        compiler_params=_cp("arbitrary"),
    )(*first, dx1, yret, ymla, wout)


def _ret_bwd_q_call(q, k, v, o, g, dy, gnw, rc, cos_r, sin_r, tr):
    s = q.shape[0]
    c = RET_CHUNK
    nc = tr // c

    def body(q_ref, k_ref, v_ref, o_ref, g_ref, dy_ref, gnw_ref, dm_ref, zeta_ref, xi_ref, cd_ref, bd_ref, cr_ref, sr_ref,
             dq_ref, dg_ref, do_ref, dgnw_ref, st_ref):
        _zero_first(pl.program_id(1) == 0, st_ref, dgnw_ref)
        bd = bd_ref[...]
        avg = bd * (1.0 / HEAD)
        ov = o_ref[...]
        ctr = ov - _dot_hi(ov, avg)
        rs = lax.rsqrt(_dot_hi(ctr * ctr, avg) + EPS)
        oh = ctr * rs
        gg, dyv, gnw_v = g_ref[...], dy_ref[...], gnw_ref[...]
        sg = _sigmoid(gg)
        sl = gg * sg
        dg_ref[...] = (dyv * oh * gnw_v * _dsilu(gg, sg)).astype(BF16)
        dgnw_ref[...] += _colsum(dyv * sl * oh)
        doh = dyv * sl * gnw_v
        dov = (rs * (doh - _dot_hi(doh, avg) - oh * _dot_hi(doh * oh, avg))).astype(BF16)
        do_ref[...] = dov
        chunks = [slice(ci * c, (ci + 1) * c) for ci in range(nc)]
        contrib = [_dot_tn((k_ref[rows, :].astype(F32) * zeta_ref[0]).astype(BF16), v_ref[rows, :]) * bd for rows in chunks]
        st, states = st_ref[...], []
        for ci in range(nc):
            states.append(st.astype(BF16))
            st = st * cd_ref[0] + contrib[ci]
        st_ref[...] = st
        for ci, rows in enumerate(chunks):
            doc = dov[rows, :]
            dq = (_dot_nt(doc, states[ci]) * xi_ref[0]
                  + _pair_product(doc, _stack_heads(v_ref[rows, :]), dm_ref[0], _stack_heads(k_ref[rows, :])))
            dq_ref[rows, :] = _unrope(dq, cr_ref[rows, :], sr_ref[rows, :], HEAD // 2).astype(BF16)

    slab = pl.BlockSpec((tr, LANES), lambda j, i: (i, j))
    tab = pl.BlockSpec((tr, LANES), lambda j, i: (i, 0))
    vec = pl.BlockSpec((1, LANES), lambda j, i: (0, j))
    sd = jax.ShapeDtypeStruct
    return pl.pallas_call(
        body, name="ret_bwd_q", grid=(4, s // tr),
        in_specs=[slab, slab, slab, slab, slab, slab, vec,
                  pl.BlockSpec((1, c, 2 * c), lambda j, i: (j, 0, 0)),
                  pl.BlockSpec((1, c, LANES), lambda j, i: (j, 0, 0)),
                  pl.BlockSpec((1, c, LANES), lambda j, i: (j, 0, 0)),
                  pl.BlockSpec((1, LANES, LANES), lambda j, i: (j, 0, 0)),
                  pl.BlockSpec((LANES, LANES), lambda j, i: (0, 0)), tab, tab],
        out_specs=[slab, slab, slab, vec],
        out_shape=[sd((s, RET_W), BF16), sd((s, RET_W), BF16), sd((s, RET_W), BF16), sd((1, RET_W), F32)],
        scratch_shapes=[pltpu.VMEM((LANES, LANES), F32)],
        compiler_params=_cp("parallel", "arbitrary"),
    )(q, k, v, o, g, dy, gnw, rc["dmask"], rc["zeta"], rc["xi"], rc["cd"], rc["bd"], cos_r, sin_r)


def _ret_bwd_kv_call(q, k, v, do, rc, cos_r, sin_r, tr):
    s = q.shape[0]
    c = RET_CHUNK
    nc = tr // c
    nt = s // tr

    def body(q_ref, k_ref, v_ref, do_ref, dm_ref, zeta_ref, xi_ref, cd_ref, bd_ref, cr_ref, sr_ref, dk_ref, dv_ref, gs_ref):
        _zero_first(pl.program_id(1) == 0, gs_ref)
        bd = bd_ref[...]
        chunks = [slice(ci * c, (ci + 1) * c) for ci in range(nc)]
        contrib = [_dot_tn((q_ref[rows, :].astype(F32) * xi_ref[0]).astype(BF16), do_ref[rows, :]) * bd for rows in chunks]
        gs, states = gs_ref[...], [None] * nc
        for ci in reversed(range(nc)):
            states[ci] = gs.astype(BF16)
            gs = gs * cd_ref[0] + contrib[ci]
        gs_ref[...] = gs
        for ci, rows in enumerate(chunks):
            kc, vc = k_ref[rows, :], v_ref[rows, :]
            q2, do2 = _stack_heads(q_ref[rows, :]), _stack_heads(do_ref[rows, :])
            gb = states[ci]
            dk = _dot_nt(vc, gb) * zeta_ref[0] + _pair_product(vc, do2, dm_ref[0], q2)
            dv = _dot(kc, gb) * zeta_ref[0] + _pair_product(kc, q2, dm_ref[0], do2)
            dk_ref[rows, :] = (_unrope(dk, cr_ref[rows, :], sr_ref[rows, :], HEAD // 2) * (HEAD ** -0.5)).astype(BF16)
            dv_ref[rows, :] = dv.astype(BF16)

    slab = pl.BlockSpec((tr, LANES), lambda j, i: (nt - 1 - i, j))
    tab = pl.BlockSpec((tr, LANES), lambda j, i: (nt - 1 - i, 0))
    sd = jax.ShapeDtypeStruct
    return pl.pallas_call(
        body, name="ret_bwd_kv", grid=(4, nt),
        in_specs=[slab, slab, slab, slab,
                  pl.BlockSpec((1, c, 2 * c), lambda j, i: (j, 0, 0)),
                  pl.BlockSpec((1, c, LANES), lambda j, i: (j, 0, 0)),
                  pl.BlockSpec((1, c, LANES), lambda j, i: (j, 0, 0)),
                  pl.BlockSpec((1, LANES, LANES), lambda j, i: (j, 0, 0)),
                  pl.BlockSpec((LANES, LANES), lambda j, i: (0, 0)), tab, tab],
        out_specs=[slab, slab],
        out_shape=[sd((s, RET_W), BF16), sd((s, RET_W), BF16)],
        scratch_shapes=[pltpu.VMEM((LANES, LANES), F32)],
        compiler_params=_cp("parallel", "arbitrary"),
    )(q, k, v, do, rc["dmask_t"], rc["zeta"], rc["xi"], rc["cd"], rc["bd"], cos_r, sin_r)


FLASH_BWD_HEADS = 4


def _flash_bwd_call(qb, k, v, do, tb, order=None):
    s = qb.shape[1]
    nb = s // tb
    hg = FLASH_BWD_HEADS
    pairs = [(a, b) for a in range(nb) for b in range(a, nb)]
    ki_of, qi_of = (jnp.asarray(np.array(col, np.int32)) for col in zip(*pairs))
    extra = [] if order is None else [order]

    def body(ki_ref, qi_ref, *refs):
        q_ref, k_ref, v_ref, do_ref, dk_ref, dv_ref, dq_hbm, dka_ref, dva_ref, dq_ref, sem = refs[len(extra):]
        g, p = pl.program_id(0), pl.program_id(1)
        ki, qi = ki_ref[p], qi_ref[p]
        _zero_first(p == 0, dq_ref)
        _zero_first(qi == ki, dka_ref, dva_ref)
        rows = pl.ds(pl.multiple_of(qi * tb, tb), tb)

        def step(masked):
            if masked:
                keep = lax.broadcasted_iota(jnp.int32, (tb, tb), 0) <= lax.broadcasted_iota(jnp.int32, (tb, tb), 1)
            for h in range(hg):
                st = _dot_nt(k_ref[h], q_ref[h])
                if masked:
                    st = jnp.where(keep, st, NEG)
                pt = jnp.exp2(st)
                dob = do_ref[h]
                dva_ref[h] += _dot(pt.astype(BF16), dob)
                dst = (pt * _dot_nt(v_ref[h], dob)).astype(BF16)
                dka_ref[h] += _dot(dst, q_ref[h])
                dq_ref[h, rows, :] += _dot_tn(dst, k_ref[h])

        @pl.when(qi > ki)
        def _():
            step(False)

        @pl.when(qi == ki)
        def _():
            step(True)

        @pl.when(qi == nb - 1)
        def _():
            dk_ref[...] = (dka_ref[...] * LN2).astype(BF16)
            dv_ref[...] = dva_ref[...].astype(BF16)

        @pl.when(p == len(pairs) - 1)
        def _():
            cp = pltpu.make_async_copy(dq_ref, dq_hbm.at[pl.ds(g * hg, hg)], sem)
            cp.start()
            cp.wait()

    kspec = pl.BlockSpec((hg, tb, LANES), lambda g, p, ki_ref, qi_ref: (g, ki_ref[p], 0))
    qspec = pl.BlockSpec((hg, tb, LANES), lambda g, p, ki_ref, qi_ref: (g, qi_ref[p], 0))
    hm = jax.ShapeDtypeStruct((N_HEADS, s, LANES), BF16)
    return pl.pallas_call(
        body, name="mla_flash_bwd",
        grid_spec=pltpu.PrefetchScalarGridSpec(
            num_scalar_prefetch=2, grid=(N_HEADS // hg, len(pairs)),
            in_specs=[ANY] * len(extra) + [qspec, kspec, kspec, qspec],
            out_specs=[kspec, kspec, ANY],
            scratch_shapes=[pltpu.VMEM((hg, tb, LANES), F32), pltpu.VMEM((hg, tb, LANES), F32),
                            pltpu.VMEM((hg, s, LANES), F32), pltpu.SemaphoreType.DMA]),
        out_shape=[hm, hm, jax.ShapeDtypeStruct((N_HEADS, s, LANES), F32)],
        compiler_params=_cp("arbitrary", "arbitrary"),
    )(ki_of, qi_of, *extra, qb, k, v, do)


def _mla_post_call(dq, dk, dv, cq, ckv, qnw, kvnw, wq, wk, wv, cos_m, sin_m, ts):
    s = cq.shape[0]

    def body(dq_ref, dk_ref, dv_ref, cq_ref, ckv_ref, qnw_ref, kvnw_ref, wq_ref, wk_ref, wv_ref, cm_ref, sm_ref,
             dcq_ref, dckv_ref, dkpe_ref, dwq_ref, dwk_ref, dwv_ref, dqnw_ref, dkvnw_ref):
        _zero_first(pl.program_id(0) == 0, dwq_ref, dwk_ref, dwv_ref, dqnw_ref, dkvnw_ref)
        cqv, ckvv = cq_ref[...], ckv_ref[...]
        rq, rkv = _rstd(cqv), _rstd(ckvv)
        qh_, kvh_ = cqv * rq, ckvv * rkv
        qnw_v, kvnw_v = qnw_ref[...], kvnw_ref[...]
        cqn = (qh_ * qnw_v).astype(BF16)
        ckvn = (kvh_ * kvnw_v).astype(BF16)
        cm, sm = cm_ref[...], sm_ref[...]
        dcqn = jnp.zeros((ts, Q_RANK), F32)
        dckvn = jnp.zeros((ts, KV_RANK), F32)
        dkpe = jnp.zeros((ts, LANES), F32)
        for h in range(N_HEADS):
            dqu = _unrope(dq_ref[h] * SM_SCALE, cm, sm, ROPE // 2).astype(BF16)
            dwq_ref[h] += _dot_tn(cqn, dqu)
            dcqn = dcqn + _dot_nt(dqu, wq_ref[h])
            dkb, dvb = dk_ref[h], dv_ref[h]
            dkpe = dkpe + dkb.astype(F32)
            dwk_ref[h] += _dot_tn(ckvn, dkb)
            dwv_ref[h] += _dot_tn(ckvn, dvb)
            dckvn = dckvn + _dot_nt(dkb, wk_ref[h]) + _dot_nt(dvb, wv_ref[h])
        lane = lax.broadcasted_iota(jnp.int32, (ts, LANES), 1)
        dkpe = jnp.where((lane >= KPE_LO) & (lane < KPE_LO + ROPE), dkpe, 0.0)
        dkpe_ref[...] = _unrope(dkpe, cm, sm, ROPE // 2).astype(BF16)
        dqnw_ref[...] += _colsum(dcqn * qh_)
        dkvnw_ref[...] += _colsum(dckvn * kvh_)
        dcq_ref[...] = _norm_bwd(dcqn, qh_, rq, qnw_v).astype(BF16)
        dckv_ref[...] = _norm_bwd(dckvn, kvh_, rkv, kvnw_v).astype(BF16)

    sd = jax.ShapeDtypeStruct
    hm = _hrow(N_HEADS, ts, LANES)
    return pl.pallas_call(
        body, name="mla_post", grid=(s // ts,),
        in_specs=[hm, hm, hm, _row(ts, Q_RANK), _row(ts, KV_RANK), _full((1, Q_RANK)), _full((1, KV_RANK)),
                  _full((N_HEADS, Q_RANK, LANES)), _full((N_HEADS, KV_RANK, LANES)), _full((N_HEADS, KV_RANK, LANES)),
                  _row(ts, LANES), _row(ts, LANES)],
        out_specs=[_row(ts, Q_RANK), _row(ts, KV_RANK), _row(ts, LANES),
                   _full((N_HEADS, Q_RANK, LANES)), _full((N_HEADS, KV_RANK, LANES)), _full((N_HEADS, KV_RANK, LANES)),
                   _full((1, Q_RANK)), _full((1, KV_RANK))],
        out_shape=[sd((s, Q_RANK), BF16), sd((s, KV_RANK), BF16), sd((s, LANES), BF16),
                   sd((N_HEADS, Q_RANK, LANES), F32), sd((N_HEADS, KV_RANK, LANES), F32), sd((N_HEADS, KV_RANK, LANES), F32),
                   sd((1, Q_RANK), F32), sd((1, KV_RANK), F32)],
        compiler_params=_cp("arbitrary"),
    )(dq, dk, dv, cq, ckv, qnw, kvnw, wq, wk, wv, cos_m, sin_m)


def _in_bwd_call(parts, x, r1, anw, dx1, win, ts):
    s = x.shape[0]
    widths = [p.shape[1] for p in parts]
    np_ = len(parts)

    def body(*refs):
        p_refs = refs[:np_]
        x_ref, r_ref, anw_ref, dx1_ref, w_ref, dx_ref, dw_ref, danw_ref = refs[np_:]
        _zero_first(pl.program_id(0) == 0, dw_ref, danw_ref)
        dproj = jnp.concatenate([p[...] for p in p_refs], axis=-1)
        r, anw_v = r_ref[...], anw_ref[...]
        xh = x_ref[...] * r
        dw_ref[...] += _dot_tn((xh * anw_v).astype(BF16), dproj)
        dh = _dot_nt(dproj, w_ref[...])
        danw_ref[...] += _colsum(dh * xh)
        dx_ref[...] = dx1_ref[...] + _norm_bwd(dh, xh, r, anw_v)

    sd = jax.ShapeDtypeStruct
    return pl.pallas_call(
        body, name="in_proj_bwd", grid=(s // ts,),
        in_specs=[_row(ts, w) for w in widths]
        + [_row(ts, D_MODEL), _row(ts, 1), _full((1, D_MODEL)), _row(ts, D_MODEL), _full((D_MODEL, IN_EXT))],
        out_specs=[_row(ts, D_MODEL), _full((D_MODEL, IN_EXT)), _full((1, D_MODEL))],
        out_shape=[sd((s, D_MODEL), F32), sd((D_MODEL, IN_EXT), F32), sd((1, D_MODEL), F32)],
        compiler_params=_cp("arbitrary"),
    )(*parts, x, r1, anw, dx1, win)


def _local_step(x, positions, tgt, w, small, ex=None):
    s = x.shape[0]
    t = _tiles(s)
    ex = _Exchanges(w) if ex is None else ex
    f = _forward(x, positions, w, small, ex)
    pw, rc = f["pw"], f["rc"]
    cos_r, sin_r, cos_m, sin_m = f["tabs"]
    dx2, loss, g_fw = _loss_call(f["x2"], tgt, small["final_norm_w"], t["ts"])
    du, dx1, g_cw, g_cb, g_fnw, g_wd = _ffn_bwd_call(dx2, f["u"], w["conv_w"], small["conv_b"], pw["wdown"], pw["wup"],
                                                     f["x1"], f["r2"], small["ffn_norm_w"], t["t2"])
    g_wup = _dw_norm_call(f["x1"], f["r2"], small["ffn_norm_w"], du, t["ts"], F2 // 4, "dw_up")
    started = ex.mlp_grads(dict(w_up=g_wup, w_down=g_wd))
    dy_ret, do, g_wout = _out_bwd_call(dx1, f["y_ret"], f["y_mla"], pw["wout"], t["ts"], started)
    started = ex.behind_out_bwd(g_wout)
    drq, dg, do_ret, g_gnw = _ret_bwd_q_call(f["q"], f["k"], f["v"], f["o_ret"], f["g"], dy_ret, small["ret_gn_w"], rc, cos_r, sin_r, t["tr"])
    drk, drv = _ret_bwd_kv_call(f["q"], f["k"], f["v"], do_ret, rc, cos_r, sin_r, t["tr"])
    dmk, dmv, dmq = _flash_bwd_call(f["mqb"], f["mk"], f["mv"], do, t["tb"], started)
    ex.behind_attention(dmk)
    dcq, dckv, dkpe, g_wq, g_wk, g_wv, g_qnw, g_kvnw = _mla_post_call(
        dmq, dmk, dmv, f["cq"], f["ckv"], small["mla_q_norm_w"], small["mla_kv_norm_w"], pw["wq"], pw["wk"], pw["wv"], cos_m, sin_m, t["ts"])
    gx, g_win_ext, g_anw = _in_bwd_call([drq, drk, drv, dg, dcq, dckv, dkpe], x, f["r1"], small["attn_norm_w"], dx1, pw["win"], t["ts"])
    lo = IN_W - ROPE
    g_win = jnp.concatenate([g_win_ext[:, :lo], g_win_ext[:, lo + KPE_LO:lo + KPE_LO + ROPE]], -1)
    g_wuq = g_wq.transpose(1, 0, 2)[:, :, :HEAD + ROPE].reshape(Q_RANK, N_HEADS * (HEAD + ROPE))
    g_wukv = jnp.concatenate([g_wk[:, :, :HEAD], g_wv[:, :, :HEAD]], -1).transpose(1, 0, 2).reshape(KV_RANK, 2 * MLA_W)
    gw = dict(w_in=g_win, w_uq=g_wuq, w_ukv=g_wukv, w_out=g_wout, w_up=g_wup,
              conv_w=g_cw, w_down=g_wd)
    gs = dict(attn_norm_w=g_anw, ret_gn_w=g_gnw, mla_q_norm_w=g_qnw, mla_kv_norm_w=g_kvnw, ffn_norm_w=g_fnw,
              conv_b=g_cb, final_norm_w=g_fw)
    return loss, gx, gw, gs


MESH_ID = pl.DeviceIdType.MESH
ANY = pl.BlockSpec(memory_space=pl.ANY)
VMEM_SPEC = pl.BlockSpec(memory_space=pltpu.VMEM)
N_DEV = 8
GROUP_A = (("w_in", (D_MODEL, IN_W // 4), 1), ("w_uq", (Q_RANK, 192), 1), ("w_ukv", (KV_RANK, 256), 1),
           ("w_out", (D_MODEL // 4, D_MODEL), 0))
GROUP_B = (("w_up", (D_MODEL, F2 // 4), 1), ("w_down", (D_FF // 4, D_MODEL), 0))
HBM_SPEC = pl.BlockSpec(memory_space=pltpu.HBM)
SEM_SPEC = pl.BlockSpec(memory_space=pltpu.SEMAPHORE)


def _mesh_pos():
    return lax.axis_index("x"), lax.axis_index("y"), lax.axis_index("c")


def _other_chips(x, y):
    return [(1 - x, y), (x, 1 - y), (1 - x, 1 - y)]


def _remote(src, dst, send_sems, recv_sems, k, dev):
    return pltpu.make_async_remote_copy(src_ref=src, dst_ref=dst, send_sem=send_sems.at[k], recv_sem=recv_sems.at[k],
                                        device_id=dev, device_id_type=MESH_ID)


def _gather_list_call(parts, tag):
    n = len(parts)

    def body(*refs):
        srcs, outs, (send_sems, recv_sems) = refs[:n], refs[n:2 * n], refs[2 * n:]
        x, y, c = _mesh_pos()
        sm = 2 * x + y
        chips = _other_chips(x, y)
        sib = (x, y, 1 - c)
        rc = lambda k, src, dst, dev: _remote(src, dst, send_sems, recv_sems, k, dev)
        first = [rc(7 * i + j, srcs[i].at[c], outs[i].at[sm, c], (cx, cy, c)) for i in range(n) for j, (cx, cy) in enumerate(chips)]
        own = [rc(7 * i + 6, srcs[i], outs[i].at[sm], sib) for i in range(n)]
        for cp in first + own:
            cp.start()
        passed = []
        for j, (cx, cy) in enumerate(chips):
            for i in range(n):
                land = outs[i].at[2 * cx + cy, c]
                rc(7 * i + j, srcs[i].at[c], land, (cx, cy, c)).wait_recv()
                cp = rc(7 * i + 3 + j, land, land, sib)
                cp.start()
                passed.append(cp)
        for j, (cx, cy) in enumerate(chips):
            for i in range(n):
                rc(7 * i + 3 + j, srcs[i].at[c], outs[i].at[2 * cx + cy, 1 - c], sib).wait_recv()
        for cp in own:
            cp.wait_recv()
        for cp in first + passed + own:
            cp.wait_send()

    return pl.pallas_call(
        body, name="weights_all_gather_" + tag,
        in_specs=[ANY] * n, out_specs=[ANY] * n,
        out_shape=[jax.ShapeDtypeStruct((4,) + p.shape, p.dtype) for p in parts],
        scratch_shapes=[pltpu.SemaphoreType.DMA((7 * n,)), pltpu.SemaphoreType.DMA((7 * n,))],
    )(*parts)


def _direct_gather_copies(srcs, lands, send_sems, recv_sems):
    x, y, c = _mesh_pos()
    sm = 2 * x + y
    sends, recvs = [], []
    for i, (src, land) in enumerate(zip(srcs, lands)):
        for j, (cx, cy) in enumerate(_other_chips(x, y)):
            for t in range(2):
                sends.append(_remote(src.at[c], land.at[sm, c], send_sems, recv_sems, 13 * i + 4 * j + 2 * c + t, (cx, cy, t)))
                recvs.append(_remote(src.at[t], land.at[2 * cx + cy, t], send_sems, recv_sems, 13 * i + 4 * j + 2 * t + c, (cx, cy, t)))
        sends.append(_remote(src, land.at[sm], send_sems, recv_sems, 13 * i + 12, (x, y, 1 - c)))
        recvs.append(_remote(src, land.at[sm], send_sems, recv_sems, 13 * i + 12, (x, y, 1 - c)))
    return sends, recvs


def _sibling_copies(srcs, lands, send_sems, recv_sems):
    x, y, c = _mesh_pos()
    cps = [_remote(src.at[s, 1 - c], land.at[s], send_sems, recv_sems, 4 * i + s, (x, y, 1 - c))
           for i, (src, land) in enumerate(zip(srcs, lands)) for s in range(4)]
    return cps, cps


def _chips_copies(srcs, lands, send_sems, recv_sems):
    x, y, c = _mesh_pos()
    cps = [_remote(src.at[2 * cx + cy], land.at[j], send_sems, recv_sems, 3 * i + j, (cx, cy, c))
           for i, (src, land) in enumerate(zip(srcs, lands)) for j, (cx, cy) in enumerate(_other_chips(x, y))]
    return cps, cps


def _share_copies(srcs, lands, send_sems, recv_sems):
    x, y, c = _mesh_pos()
    cps = [_remote(src, land, send_sems, recv_sems, i, (x, y, 1 - c)) for i, (src, land) in enumerate(zip(srcs, lands))]
    return cps, cps


def _exchange_call(name, copies, srcs, land_shapes, n_sems):
    n = len(srcs)

    def body(*refs):
        sends, recvs = copies(refs[:n], refs[n:2 * n], refs[2 * n], refs[2 * n + 1])
        for cp in sends:
            cp.start()
        for cp in sends:
            cp.wait_send()
        for cp in recvs:
            cp.wait_recv()

    return pl.pallas_call(
        body, name=name, in_specs=[ANY] * n, out_specs=[ANY] * n, out_shape=list(land_shapes),
        scratch_shapes=[pltpu.SemaphoreType.DMA((n_sems,)), pltpu.SemaphoreType.DMA((n_sems,))],
    )(*srcs)


def _exchange_start_call(name, copies, srcs, land_shapes, n_sems, order=None):
    n = len(srcs)
    extra = [] if order is None else [order]
    k = 2 * n + len(extra)

    def body(*refs):
        sends, _ = copies(refs[:n], refs[n:2 * n], refs[k], refs[k + 1])
        for cp in sends:
            cp.start()
        refs[-1][...] = jnp.zeros_like(refs[-1])

    hbm = lambda a: pltpu.with_memory_space_constraint(a, pltpu.HBM)
    lands = [hbm(lax.empty(sd.shape, sd.dtype)) for sd in land_shapes]
    sem = pltpu.SemaphoreType.DMA((n_sems,))
    out = pl.pallas_call(
        body, name=name,
        out_shape=(sem, sem, *[pltpu.HBM(a.shape, a.dtype) for a in list(srcs) + lands], jax.ShapeDtypeStruct((8, LANES), F32)),
        in_specs=[HBM_SPEC] * (2 * n) + [ANY] * len(extra), out_specs=(SEM_SPEC, SEM_SPEC, *[HBM_SPEC] * (2 * n), VMEM_SPEC),
        input_output_aliases={i: 2 + i for i in range(2 * n)},
        compiler_params=pltpu.CompilerParams(has_side_effects=pltpu.SideEffectType.DATAFLOW_SIDE_EFFECTING),
    )(*[hbm(a) for a in srcs], *lands, *extra)
    return out[0], out[1], out[2:2 + n], out[2 + n:2 + 2 * n], out[-1]


def _exchange_wait_call(name, copies, started, after):
    send_sems, recv_sems, srcs, lands, _ = started
    n = len(srcs)

    def body(*refs):
        sends, recvs = copies(refs[:n], refs[n:2 * n], refs[2 * n], refs[2 * n + 1])
        for cp in sends:
            cp.wait_send()
        for cp in recvs:
            cp.wait_recv()

    out = pl.pallas_call(
        body, name=name,
        out_shape=tuple(pltpu.HBM(a.shape, a.dtype) for a in list(srcs) + list(lands)),
        in_specs=[HBM_SPEC] * (2 * n) + [SEM_SPEC, SEM_SPEC, ANY], out_specs=tuple([HBM_SPEC] * (2 * n)),
        input_output_aliases={i: i for i in range(2 * n)},
        compiler_params=pltpu.CompilerParams(has_side_effects=pltpu.SideEffectType.DATAFLOW_SIDE_EFFECTING),
    )(*srcs, *lands, send_sems, recv_sems, after)
    return out[:n], out[n:]


def _rows_tile(rows, width, itemsize=4):
    limit = max(16, (3 << 20) // (width * itemsize))
    if rows <= limit:
        return rows
    return max(t for t in range(16, limit + 1, 16) if rows % t == 0)


def _sum_sibling_call(g, buf, c, name):
    _, _, rh, w = g.shape
    tile = _rows_tile(rh, w)

    def body(c_ref, g_ref, b_ref, p_ref, pb_ref):
        p = g_ref[...] + b_ref[...]
        p_ref[...] = p
        pb_ref[...] = p.astype(BF16)

    blk = pl.BlockSpec((None, tile, w), lambda s, i, c_ref: (s, i, 0))
    return pl.pallas_call(
        body, name=name,
        grid_spec=pltpu.PrefetchScalarGridSpec(
            num_scalar_prefetch=1, grid=(4, rh // tile),
            in_specs=[pl.BlockSpec((None, None, tile, w), lambda s, i, c_ref: (s, c_ref[0], i, 0)), blk],
            out_specs=[blk, blk]),
        out_shape=[jax.ShapeDtypeStruct((4, rh, w), F32), jax.ShapeDtypeStruct((4, rh, w), BF16)],
        compiler_params=_cp("parallel", "parallel"),
    )(c, g, buf)


def _sum_chips_call(p, buf, sm, name):
    _, rh, w = p.shape
    tile = _rows_tile(rh, w)

    def body(sm_ref, p_ref, b_ref, f_ref):
        f_ref[...] = ((p_ref[...] + b_ref[0].astype(F32)) + b_ref[1].astype(F32)) + b_ref[2].astype(F32)

    return pl.pallas_call(
        body, name=name,
        grid_spec=pltpu.PrefetchScalarGridSpec(
            num_scalar_prefetch=1, grid=(rh // tile,),
            in_specs=[pl.BlockSpec((None, tile, w), lambda i, sm_ref: (sm_ref[0], i, 0)),
                      pl.BlockSpec((3, tile, w), lambda i, sm_ref: (0, i, 0))],
            out_specs=pl.BlockSpec((tile, w), lambda i, sm_ref: (i, 0))),
        out_shape=jax.ShapeDtypeStruct((rh, w), F32),
        compiler_params=_cp("parallel"),
    )(sm, p, buf)


def _adamw_halves_call(w, g_mine, g_sib, c, m, v, name):
    r, wd = w.shape
    rh = r // 2
    tile = _rows_tile(rh, wd)
    nt = rh // tile

    def body(c_ref, w_ref, gm_ref, gs_ref, m_ref, v_ref, g_ref, d_ref, nm_ref, nv_ref):
        gv = jnp.where(pl.program_id(0) == c_ref[0], gm_ref[...], gs_ref[...])
        g_ref[...] = gv
        nm = ADAM_B1 * m_ref[...] + (1.0 - ADAM_B1) * gv
        nv = ADAM_B2 * v_ref[...] + (1.0 - ADAM_B2) * jnp.square(gv)
        m_hat = nm / (1.0 - ADAM_B1 ** ADAM_STEP)
        v_hat = nv / (1.0 - ADAM_B2 ** ADAM_STEP)
        d_ref[...] = -ADAM_LR * (m_hat / (jnp.sqrt(v_hat) + ADAM_EPS) + ADAM_WD * w_ref[...])
        nm_ref[...] = nm
        nv_ref[...] = nv

    whole = pl.BlockSpec((tile, wd), lambda h, i, c_ref: (h * nt + i, 0))
    half = pl.BlockSpec((tile, wd), lambda h, i, c_ref: (i, 0))
    sd = jax.ShapeDtypeStruct((r, wd), F32)
    return pl.pallas_call(
        body, name=name,
        grid_spec=pltpu.PrefetchScalarGridSpec(
            num_scalar_prefetch=1, grid=(2, nt),
            in_specs=[whole, half, half, whole, whole], out_specs=[whole] * 4),
        out_shape=[sd, sd, sd, sd],
        compiler_params=_cp("parallel", "parallel"),
    )(c, w, g_mine, g_sib, m, v)


def _exchange8_call(vec, reduce, name):
    rows = vec.shape[0]

    def body(v_ref, out_ref, *rest):
        slots, send_sems, recv_sems = (rest if reduce else (out_ref,) + rest)
        x, y, c = _mesh_pos()
        me = 4 * x + 2 * y + c
        slots[me] = v_ref[...]

        def rcopy(k, to_me):
            bx, by, bc = (k >> 2) & 1, (k >> 1) & 1, k & 1
            px, py, pc = (1 - x if bx else x), (1 - y if by else y), (1 - c if bc else c)
            slot = 4 * px + 2 * py + pc if to_me else me
            return pltpu.make_async_remote_copy(src_ref=v_ref, dst_ref=slots.at[slot], send_sem=send_sems.at[k - 1],
                                                recv_sem=recv_sems.at[k - 1], device_id=(px, py, pc), device_id_type=MESH_ID)

        for k in range(1, N_DEV):
            rcopy(k, False).start()
        for k in range(1, N_DEV):
            rcopy(k, True).wait_recv()
        for k in range(1, N_DEV):
            rcopy(k, False).wait_send()
        if reduce:
            tot = slots[0]
            for d in range(1, N_DEV):
                tot = tot + slots[d]
            out_ref[...] = tot

    stack = jax.ShapeDtypeStruct((N_DEV, rows, LANES), F32)
    return pl.pallas_call(
        body, name=name,
        in_specs=[VMEM_SPEC], out_specs=VMEM_SPEC,
        out_shape=jax.ShapeDtypeStruct((rows, LANES), F32) if reduce else stack,
        scratch_shapes=([pltpu.VMEM((N_DEV, rows, LANES), F32)] if reduce else [])
        + [pltpu.SemaphoreType.DMA((N_DEV - 1,)), pltpu.SemaphoreType.DMA((N_DEV - 1,))],
    )(vec)


def _adamw_call(w, g, m, v, name):
    r, c = w.shape
    rb = r if r <= 256 else (256 if r % 256 == 0 else 352)
    assert r % rb == 0

    def body(w_ref, g_ref, m_ref, v_ref, d_ref, nm_ref, nv_ref):
        gv = g_ref[...]
        nm = ADAM_B1 * m_ref[...] + (1.0 - ADAM_B1) * gv
        nv = ADAM_B2 * v_ref[...] + (1.0 - ADAM_B2) * jnp.square(gv)
        m_hat = nm / (1.0 - ADAM_B1 ** ADAM_STEP)
        v_hat = nv / (1.0 - ADAM_B2 ** ADAM_STEP)
        d_ref[...] = -ADAM_LR * (m_hat / (jnp.sqrt(v_hat) + ADAM_EPS) + ADAM_WD * w_ref[...])
        nm_ref[...] = nm
        nv_ref[...] = nv

    spec = pl.BlockSpec((rb, c), lambda i: (i, 0))
    sd = jax.ShapeDtypeStruct((r, c), F32)
    return pl.pallas_call(
        body, name=name, grid=(r // rb,),
        in_specs=[spec] * 4, out_specs=[spec] * 3, out_shape=[sd, sd, sd],
        compiler_params=_cp("parallel"),
    )(w, g, m, v)


SMALL = (("attn_norm_w", D_MODEL), ("ret_gn_w", RET_W), ("mla_q_norm_w", Q_RANK), ("mla_kv_norm_w", KV_RANK),
         ("ffn_norm_w", D_MODEL), ("conv_b", F2), ("final_norm_w", D_MODEL))
WEIGHT_ORDER = ("attn_norm_w", "w_in", "ret_gn_w", "mla_q_norm_w", "w_uq", "mla_kv_norm_w", "w_ukv", "w_out",
                "ffn_norm_w", "w_up", "conv_w", "conv_b", "w_down", "final_norm_w")


def _pad_rows(flat, rows):
    return jnp.concatenate([flat, jnp.zeros((rows * LANES - flat.shape[0],), flat.dtype)]).reshape(rows, LANES)


def kernel(x, positions, attn_norm_w, w_in, ret_gn_w, mla_q_norm_w, w_uq, mla_kv_norm_w, w_ukv, w_out, ffn_norm_w, w_up, conv_w, conv_b, w_down, final_norm_w, loss_target, m_attn_norm_w, m_w_in, m_ret_gn_w, m_mla_q_norm_w, m_w_uq, m_mla_kv_norm_w, m_w_ukv, m_w_out, m_ffn_norm_w, m_w_up, m_conv_w, m_conv_b, m_w_down, m_final_norm_w, v_attn_norm_w, v_w_in, v_ret_gn_w, v_mla_q_norm_w, v_w_uq, v_mla_kv_norm_w, v_w_ukv, v_w_out, v_ffn_norm_w, v_w_up, v_conv_w, v_conv_b, v_w_down, v_final_norm_w):
    args = dict(locals())
    cx, cy, cc = _mesh_pos()
    sm = 2 * cx + cy

    c_arr, sm_arr = cc.reshape(1).astype(jnp.int32), sm.reshape(1).astype(jnp.int32)
    sds = jax.ShapeDtypeStruct

    def my_shards(group):
        return [args[n][0].astype(BF16).reshape(2, r // 2, c) for n, (r, c), _ in group]

    def full_weights(gathered, group):
        full = {}
        for (n, (r, c), axis), got in zip(group, gathered):
            piece = got.reshape(4, r, c)
            full[n] = piece if n == "w_up" else (piece.transpose(1, 0, 2).reshape(r, 4 * c) if axis == 1 else piece.reshape(4 * r, c))
        return full

    def by_owner(gw, group):
        out = []
        for n, (r, c), axis in group:
            g = gw[n]
            if axis == 1 and g.ndim == 2:
                g = g.reshape(r, 4, c).transpose(1, 0, 2)
            out.append(g.reshape(4, 2, r // 2, c))
        return out

    def sibling_shapes(gs):
        return [sds((4,) + g.shape[2:], F32) for g in gs]

    def chip_sums(gs, bufs, group):
        res = [_sum_sibling_call(g, b, c_arr, "grads_sum_sibling_" + n) for g, b, (n, _, _) in zip(gs, bufs, group)]
        return [p for p, _ in res], [pb for _, pb in res]

    def chips_shapes(pbs):
        return [sds((3,) + pb.shape[1:], BF16) for pb in pbs]

    def totals(ps, lands, group, tag):
        fins = [_sum_chips_call(p, l, sm_arr, "grads_sum_chips_" + n) for p, l, (n, _, _) in zip(ps, lands, group)]
        sibs = _exchange_call("grads_rs_share_" + tag, _share_copies, fins, [sds(f.shape, F32) for f in fins], len(fins))
        return {n: (f, s) for (n, _, _), f, s in zip(group, fins, sibs)}

    class StepExchanges(_Exchanges):
        def __init__(self, order):
            shards = my_shards(GROUP_B)
            self.gather = _exchange_start_call("weights_gather_start_b", _direct_gather_copies, shards,
                                               [sds((4,) + s.shape, BF16) for s in shards], 13 * len(shards), order)
            self.red = None

        def token(self):
            return self.gather[4][0:1, 0:1]

        def mlp_weights(self, after):
            return full_weights(_exchange_wait_call("weights_gather_wait_b", _direct_gather_copies, self.gather, after)[1], GROUP_B)

        def mlp_grads(self, gw):
            gs = by_owner(gw, GROUP_B)
            self.step1 = _exchange_start_call("grads_rs_sibling_start_b", _sibling_copies, gs, sibling_shapes(gs), 4 * len(gs))
            return self.step1[4]

        def behind_out_bwd(self, after):
            gs, bufs = _exchange_wait_call("grads_rs_sibling_wait_b", _sibling_copies, self.step1, after)
            self.ps, pbs = chip_sums(gs, bufs, GROUP_B)
            self.step2 = _exchange_start_call("grads_rs_chips_start_b", _chips_copies, pbs, chips_shapes(pbs), 3 * len(pbs))
            return self.step2[4]

        def behind_attention(self, after):
            _, lands = _exchange_wait_call("grads_rs_chips_wait_b", _chips_copies, self.step2, after)
            self.red = totals(self.ps, lands, GROUP_B, "b")

    full = full_weights(_gather_list_call(my_shards(GROUP_A), "a"), GROUP_A)
    cw_rows = 40
    cw_all = _exchange8_call(_pad_rows(conv_w[0].reshape(-1), cw_rows), False, "conv_w_all_gather")
    ex = StepExchanges(cw_all)
    cw_all = cw_all[0::2].reshape(4, cw_rows * LANES)[:, :3 * F2 // 4].reshape(4, 3, F2 // 4)
    full["conv_w"] = cw_all.transpose(1, 0, 2).reshape(3, F2)
    small = {n: args[n].reshape(1, d) for n, d in SMALL}
    small["attn_norm_w"] = small["attn_norm_w"] + ex.token()

    loss, gx, gw, gs = _local_step(x[0], positions[0], loss_target[0], full, small, ex)

    ga = by_owner(gw, GROUP_A)
    bufs = _exchange_call("grads_rs_sibling_a", _sibling_copies, ga, sibling_shapes(ga), 4 * len(ga))
    ps, pbs = chip_sums(ga, bufs, GROUP_A)
    lands = _exchange_call("grads_rs_chips_a", _chips_copies, pbs, chips_shapes(pbs), 3 * len(pbs))
    halves = {**ex.red, **totals(ps, lands, GROUP_A, "a")}

    vec = jnp.concatenate([gs[n].reshape(-1) for n, _ in SMALL] + [gw["conv_w"].reshape(-1), loss.reshape(-1)])
    tot = _exchange8_call(_pad_rows(vec, 216), True, "small_all_reduce").reshape(-1)
    red, off = {}, 0
    for n, d in SMALL:
        red[n] = tot[off:off + d].reshape(1, d)
        off += d
    red["conv_w"] = lax.dynamic_slice(tot[off:off + 3 * F2].reshape(3, F2), (0, sm * (F2 // 4)), (3, F2 // 4))
    loss_tot = tot[off + 3 * F2]

    grads, deltas, new_m, new_v = [], [], [], []
    for n in WEIGHT_ORDER:
        shape = args[n].shape
        two_d = (1, shape[0]) if len(shape) == 1 else shape[-2:]
        wmv = [args[k + n].reshape(two_d) for k in ("", "m_", "v_")]
        if n in halves:
            g, d, nm, nv = _adamw_halves_call(wmv[0], *halves[n], c_arr, wmv[1], wmv[2], "adamw_" + n)
        else:
            g = red[n].reshape(two_d)
            d, nm, nv = _adamw_call(wmv[0], g, wmv[1], wmv[2], "adamw_" + n)
        grads.append(g.reshape(shape))
        deltas.append(d.reshape(shape))
        new_m.append(nm.reshape(shape))
        new_v.append(nv.reshape(shape))
    return (loss_tot, gx[None], *grads, *deltas, *new_m, *new_v)
```

```python
import functools
import math

import numpy as np
import jax
import jax.numpy as jnp
from jax import lax
from jax.experimental import pallas as pl
from jax.experimental.pallas import tpu as pltpu

F32 = jnp.float32
BF16 = jnp.bfloat16

D_MODEL = 1024
N_HEADS = 8
HEAD = 64
RET_W = N_HEADS * HEAD
MLA_W = N_HEADS * HEAD
ROPE = 32
Q_RANK = 256
KV_RANK = 128
D_FF = 2816
F2 = 2 * D_FF
IN_W = 4 * RET_W + Q_RANK + KV_RANK + ROPE
IN_EXT = 4 * RET_W + Q_RANK + KV_RANK + 128
KPE_LO = 64
ROPE_BASE = 10000.0
EPS = 1e-6
RET_CHUNK = 128
SM_SCALE = (HEAD + ROPE) ** -0.5
LOG2E = math.log2(math.e)
LN2 = math.log(2.0)
NEG = -1e30
LANES = 128
VMEM_LIMIT = 56 * 1024 * 1024

ADAM_LR = 0.001
ADAM_B1 = 0.9
ADAM_B2 = 0.999
ADAM_EPS = 1e-08
ADAM_WD = 0.01
ADAM_STEP = 10


VMEM_LIMIT_MLP_BWD = 60 * 1024 * 1024


def _cp(*sem, vmem=VMEM_LIMIT):
    return pltpu.CompilerParams(dimension_semantics=sem, vmem_limit_bytes=vmem)


def _full(shape):
    n = len(shape)
    return pl.BlockSpec(tuple(shape), lambda *_: (0,) * n)


def _row(ts, c):
    return pl.BlockSpec((ts, c), lambda i: (i, 0))


def _hrow(h, ts, c):
    return pl.BlockSpec((h, ts, c), lambda i: (0, i, 0))


def _dot(a, b):
    return jnp.dot(a, b, preferred_element_type=F32)


def _dot_nt(a, b):
    return lax.dot_general(a, b, (((1,), (1,)), ((), ())), preferred_element_type=F32)


def _dot_tn(a, b):
    return lax.dot_general(a, b, (((0,), (0,)), ((), ())), preferred_element_type=F32)


def _dot_hi(a, b):
    hi = a.astype(BF16)
    lo = (a - hi.astype(F32)).astype(BF16)
    bb = b.astype(BF16)
    return _dot(hi, bb) + _dot(lo, bb)


def _rot_half(x, half):
    w = x.shape[-1]
    lane = lax.broadcasted_iota(jnp.int32, x.shape, x.ndim - 1)
    first = (lane % (2 * half)) < half
    return jnp.where(first, -pltpu.roll(x, w - half, x.ndim - 1), pltpu.roll(x, half, x.ndim - 1))


def _rope(x, cos, sin, half):
    return x * cos + _rot_half(x, half) * sin


def _unrope(dy, cos, sin, half):
    return dy * cos - _rot_half(dy, half) * sin


def _sigmoid(g):
    return 0.5 * jnp.tanh(0.5 * g) + 0.5


def _silu(g):
    return g * _sigmoid(g)


def _rstd(x):
    return lax.rsqrt(jnp.mean(x * x, axis=-1, keepdims=True) + EPS)


def _rope_tables(positions):
    pos = positions.astype(F32)[:, None]
    s = pos.shape[0]
    inv = ROPE_BASE ** (-jnp.arange(0, HEAD, 2, dtype=F32) / HEAD)
    ang = pos * inv
    c, sn = jnp.cos(ang), jnp.sin(ang)
    cos_r = jnp.tile(jnp.concatenate([c, c], -1), (1, 2))
    sin_r = jnp.tile(jnp.concatenate([sn, sn], -1), (1, 2))
    inv = ROPE_BASE ** (-jnp.arange(0, ROPE, 2, dtype=F32) / ROPE)
    ang = pos * inv
    c, sn = jnp.cos(ang), jnp.sin(ang)
    one, zero = jnp.ones((s, KPE_LO), F32), jnp.zeros((s, KPE_LO), F32)
    cos_m = jnp.concatenate([one, c, c, one[:, :LANES - KPE_LO - ROPE]], -1)
    sin_m = jnp.concatenate([zero, sn, sn, zero[:, :LANES - KPE_LO - ROPE]], -1)
    return cos_r, sin_r, cos_m, sin_m


def _ret_consts():
    c = RET_CHUNK
    lg = np.log1p(-np.power(2.0, -5.0 - np.arange(N_HEADS, dtype=np.float64)))
    idx = np.arange(c, dtype=np.float64)
    diff = idx[:, None] - idx[None, :]
    lane_head = np.arange(LANES) // HEAD
    dmask = np.zeros((4, 2, c, c))
    zeta = np.zeros((4, c, LANES))
    xi = np.zeros((4, c, LANES))
    cd = np.zeros((4, LANES, LANES))
    bd = (lane_head[:, None] == lane_head[None, :]).astype(np.float64)
    for j in range(4):
        for hh in range(2):
            dmask[j, hh] = np.where(diff >= 0, np.exp(lg[2 * j + hh] * np.maximum(diff, 0.0)), 0.0)
        lgl = lg[2 * j + lane_head]
        zeta[j] = np.exp(lgl[None, :] * (c - 1.0 - idx[:, None]))
        xi[j] = np.exp(lgl[None, :] * (idx[:, None] + 1.0))
        cd[j] = np.exp(lgl * c)[:, None] * bd
    f = lambda a: jnp.asarray(a, F32)
    side = lambda d: np.concatenate([d[:, 0], d[:, 1]], axis=-1)
    return dict(dmask=f(side(dmask)), dmask_t=f(side(np.swapaxes(dmask, 2, 3))), zeta=f(zeta), xi=f(xi), cd=f(cd), bd=f(bd))


def _f1_call(x, anw, win, cos_r, sin_r, cos_m, sin_m, ts):
    s = x.shape[0]

    def body(x_ref, anw_ref, w_ref, cr_ref, sr_ref, cm_ref, sm_ref,
             q_ref, k_ref, v_ref, g_ref, cq_ref, ckv_ref, kpe_ref, r_ref):
        xv = x_ref[...]
        r = _rstd(xv)
        r_ref[...] = r
        h = (xv * r * anw_ref[...]).astype(BF16)
        cr, sr = cr_ref[...], sr_ref[...]
        qk = _dot(h, w_ref[:, 0:2 * RET_W])
        for j in range(4):
            sl = slice(j * LANES, (j + 1) * LANES)
            q_ref[:, sl] = _rope(qk[:, sl], cr, sr, HEAD // 2).astype(BF16)
            kk = qk[:, RET_W + j * LANES:RET_W + (j + 1) * LANES]
            k_ref[:, sl] = (_rope(kk, cr, sr, HEAD // 2) * (HEAD ** -0.5)).astype(BF16)
        v_ref[...] = _dot(h, w_ref[:, 2 * RET_W:3 * RET_W]).astype(BF16)
        g_ref[...] = _dot(h, w_ref[:, 3 * RET_W:4 * RET_W])
        o = 4 * RET_W
        cq_ref[...] = _dot(h, w_ref[:, o:o + Q_RANK])
        ckv_ref[...] = _dot(h, w_ref[:, o + Q_RANK:o + Q_RANK + KV_RANK])
        kp = _dot(h, w_ref[:, o + Q_RANK + KV_RANK:IN_EXT])
        kpe_ref[...] = _rope(kp, cm_ref[...], sm_ref[...], ROPE // 2)

    sd = jax.ShapeDtypeStruct
    return pl.pallas_call(
        body, name="f1_in_proj", grid=(s // ts,),
        in_specs=[_row(ts, D_MODEL), _full((1, D_MODEL)), _full((D_MODEL, IN_EXT)),
                  _row(ts, LANES), _row(ts, LANES), _row(ts, LANES), _row(ts, LANES)],
        out_specs=[_row(ts, RET_W), _row(ts, RET_W), _row(ts, RET_W), _row(ts, RET_W),
                   _row(ts, Q_RANK), _row(ts, KV_RANK), _row(ts, LANES), _row(ts, 1)],
        out_shape=[sd((s, RET_W), BF16), sd((s, RET_W), BF16), sd((s, RET_W), BF16), sd((s, RET_W), F32),
                   sd((s, Q_RANK), F32), sd((s, KV_RANK), F32), sd((s, LANES), F32), sd((s, 1), F32)],
        compiler_params=_cp("parallel"),
    )(x, anw, win, cos_r, sin_r, cos_m, sin_m)


def _stack_heads(a):
    lo = lax.broadcasted_iota(jnp.int32, a.shape, 1) < HEAD
    zero = jnp.zeros_like(a)
    return jnp.concatenate([jnp.where(lo, a, zero), jnp.where(lo, zero, a)], axis=0)


def _pair_product(a, b2, decay2, w2):
    return _dot((_dot_nt(a, b2) * decay2).astype(BF16), w2)


def _ret_fwd_call(q, k, v, g, gnw, rc, tr):
    s = q.shape[0]
    c = RET_CHUNK
    nc = tr // c

    def body(q_ref, k_ref, v_ref, g_ref, gnw_ref, dm_ref, zeta_ref, xi_ref, cd_ref, bd_ref, o_ref, y_ref, st_ref):
        @pl.when(pl.program_id(1) == 0)
        def _():
            st_ref[...] = jnp.zeros_like(st_ref)

        lane = lax.broadcasted_iota(jnp.int32, (c, LANES), 1)
        bd = bd_ref[...]
        chunks = [slice(ci * c, (ci + 1) * c) for ci in range(nc)]
        contrib = [_dot_tn((k_ref[rows, :].astype(F32) * zeta_ref[0]).astype(BF16), v_ref[rows, :]) * bd for rows in chunks]
        st, states = st_ref[...], []
        for ci in range(nc):
            states.append(st.astype(BF16))
            st = st * cd_ref[0] + contrib[ci]
        st_ref[...] = st
        for ci, rows in enumerate(chunks):
            qc = q_ref[rows, :]
            o_ref[rows, :] = (_dot(qc, states[ci]) * xi_ref[0]
                              + _pair_product(qc, _stack_heads(k_ref[rows, :]), dm_ref[0], _stack_heads(v_ref[rows, :])))
        o = o_ref[...]
        avg = bd * (1.0 / HEAD)
        ctr = o - _dot_hi(o, avg)
        var = _dot_hi(ctr * ctr, avg)
        y_ref[...] = (_silu(g_ref[...]) * (ctr * lax.rsqrt(var + EPS) * gnw_ref[...])).astype(BF16)

    slab = pl.BlockSpec((tr, LANES), lambda j, i: (i, j))
    sd = jax.ShapeDtypeStruct
    return pl.pallas_call(
        body, name="ret_fwd", grid=(4, s // tr),
        in_specs=[slab, slab, slab, slab, pl.BlockSpec((1, LANES), lambda j, i: (0, j)),
                  pl.BlockSpec((1, c, 2 * c), lambda j, i: (j, 0, 0)),
                  pl.BlockSpec((1, c, LANES), lambda j, i: (j, 0, 0)),
                  pl.BlockSpec((1, c, LANES), lambda j, i: (j, 0, 0)),
                  pl.BlockSpec((1, LANES, LANES), lambda j, i: (j, 0, 0)),
                  pl.BlockSpec((LANES, LANES), lambda j, i: (0, 0))],
        out_specs=[slab, slab],
        out_shape=[sd((s, RET_W), F32), sd((s, RET_W), BF16)],
        scratch_shapes=[pltpu.VMEM((LANES, LANES), F32)],
        compiler_params=_cp("parallel", "arbitrary"),
    )(q, k, v, g, gnw, rc["dmask"], rc["zeta"], rc["xi"], rc["cd"], rc["bd"])


QK_AUX = HEAD + ROPE
V_AUX = HEAD


def _lane_pair(shape, lo, a, b, rest):
    lane = lax.broadcasted_iota(jnp.int32, shape, len(shape) - 1)
    return jnp.where(lane == lo, a, jnp.where(lane == lo + 1, b, rest))


def _hi_lo(v):
    hi = v.astype(BF16).astype(F32)
    return hi, v - hi


def _mla_pre_call(cq, ckv, kpe, qnw, kvnw, wq, wk, wv, cos_m, sin_m, ts):
    s = cq.shape[0]

    def body(cq_ref, ckv_ref, kpe_ref, qnw_ref, kvnw_ref, wq_ref, wk_ref, wv_ref, cm_ref, sm_ref, q_ref, k_ref, v_ref):
        cqv, ckvv = cq_ref[...], ckv_ref[...]
        cqn = (cqv * _rstd(cqv) * qnw_ref[...]).astype(BF16)
        ckvn = (ckvv * _rstd(ckvv) * kvnw_ref[...]).astype(BF16)
        cm, sm = cm_ref[...], sm_ref[...]
        kp = _lane_pair((ts, LANES), QK_AUX, -1.0, -1.0, kpe_ref[...])
        for h in range(N_HEADS):
            qh = _rope(_dot(cqn, wq_ref[h]), cm, sm, ROPE // 2)
            q_ref[h] = (qh * (SM_SCALE * LOG2E)).astype(BF16)
            k_ref[h] = (_dot(ckvn, wk_ref[h]) + kp).astype(BF16)
            v_ref[h] = _lane_pair((ts, LANES), V_AUX, 1.0, 1.0, _dot(ckvn, wv_ref[h])).astype(BF16)

    sd = jax.ShapeDtypeStruct
    hm = sd((N_HEADS, s, LANES), BF16)
    return pl.pallas_call(
        body, name="mla_pre", grid=(s // ts,),
        in_specs=[_row(ts, Q_RANK), _row(ts, KV_RANK), _row(ts, LANES), _full((1, Q_RANK)), _full((1, KV_RANK)),
                  _full((N_HEADS, Q_RANK, LANES)), _full((N_HEADS, KV_RANK, LANES)), _full((N_HEADS, KV_RANK, LANES)),
                  _row(ts, LANES), _row(ts, LANES)],
        out_specs=[_hrow(N_HEADS, ts, LANES)] * 3,
        out_shape=[hm, hm, hm],
        compiler_params=_cp("parallel"),
    )(cq, ckv, kpe, qnw, kvnw, wq, wk, wv, cos_m, sin_m)


def _flash_fwd_call(q, k, v, tb):
    s = q.shape[1]
    nb = s // tb
    pairs = [(a, b) for a in range(nb) for b in range(a + 1)]
    qi_of, ki_of = (jnp.asarray(np.array(col, np.int32)) for col in zip(*pairs))

    def body(qi_ref, ki_ref, q_ref, k_ref, v_ref, o_ref, qb_ref, m_ref, acc_ref):
        qi, ki = qi_ref[pl.program_id(0)], ki_ref[pl.program_id(0)]

        @pl.when(ki == 0)
        def _():
            m_ref[...] = jnp.full_like(m_ref, NEG)
            acc_ref[...] = jnp.zeros_like(acc_ref)

        def step(masked):
            if masked:
                keep = lax.broadcasted_iota(jnp.int32, (tb, tb), 1) <= lax.broadcasted_iota(jnp.int32, (tb, tb), 0)
            def finish(h, pe, alpha):
                acc_ref[h] = acc_ref[h] * alpha + _dot(pe, v_ref[h])

            nxt, pending = _dot_nt(q_ref[0], k_ref[0]), None
            for h in range(N_HEADS):
                sc = nxt
                if h + 1 < N_HEADS:
                    nxt = _dot_nt(q_ref[h + 1], k_ref[h + 1])
                if masked:
                    sc = jnp.where(keep, sc, NEG)
                m_prev = m_ref[h]
                m_new = jnp.maximum(m_prev, jnp.max(sc, axis=1, keepdims=True))
                pe = jnp.exp2(sc - jnp.tile(m_new, (1, tb // LANES))).astype(BF16)
                m_ref[h] = m_new
                if pending is not None:
                    finish(*pending)
                pending = (h, pe, jnp.exp2(m_prev - m_new))
            finish(*pending)

        @pl.when(ki < qi)
        def _():
            step(False)

        @pl.when(ki == qi)
        def _():
            step(True)
            lane = lax.broadcasted_iota(jnp.int32, (tb, LANES), 1)
            for p in range(N_HEADS // 2):
                outs = []
                for h in (2 * p, 2 * p + 1):
                    acc = acc_ref[h]
                    l = acc[:, V_AUX:V_AUX + 1]
                    outs.append(acc * (1.0 / l))
                    hi, lo = _hi_lo(m_ref[h][:, 0:1] + jnp.log(l) * LOG2E)
                    qb_ref[h] = _lane_pair((tb, LANES), QK_AUX, hi, lo, q_ref[h].astype(F32)).astype(BF16)
                o_ref[:, p * LANES:(p + 1) * LANES] = jnp.where(lane < HEAD, outs[0], pltpu.roll(outs[1], HEAD, 1)).astype(BF16)

    sd = jax.ShapeDtypeStruct
    qspec = pl.BlockSpec((N_HEADS, tb, LANES), lambda p, qi_ref, ki_ref: (0, qi_ref[p], 0))
    kspec = pl.BlockSpec((N_HEADS, tb, LANES), lambda p, qi_ref, ki_ref: (0, ki_ref[p], 0))
    return pl.pallas_call(
        body, name="mla_flash_fwd",
        grid_spec=pltpu.PrefetchScalarGridSpec(
            num_scalar_prefetch=2, grid=(len(pairs),),
            in_specs=[qspec, kspec, kspec],
            out_specs=[pl.BlockSpec((tb, MLA_W), lambda p, qi_ref, ki_ref: (qi_ref[p], 0)), qspec],
            scratch_shapes=[pltpu.VMEM((N_HEADS, tb, LANES), F32), pltpu.VMEM((N_HEADS, tb, LANES), F32)]),
        out_shape=[sd((s, MLA_W), BF16), sd((N_HEADS, s, LANES), BF16)],
        compiler_params=_cp("arbitrary"),
    )(qi_of, ki_of, q, k, v)


def _out_proj_call(x, yret, ymla, wout, ts):
    s = x.shape[0]

    def body(x_ref, yr_ref, ym_ref, w_ref, x1_ref, r_ref):
        x1 = x_ref[...] + _dot(yr_ref[...], w_ref[0:RET_W, :]) + _dot(ym_ref[...], w_ref[RET_W:, :])
        x1_ref[...] = x1
        r_ref[...] = _rstd(x1)

    sd = jax.ShapeDtypeStruct
    return pl.pallas_call(
        body, name="out_proj", grid=(s // ts,),
        in_specs=[_row(ts, D_MODEL), _row(ts, RET_W), _row(ts, MLA_W), _full((D_MODEL, D_MODEL))],
        out_specs=[_row(ts, D_MODEL), _row(ts, 1)],
        out_shape=[sd((s, D_MODEL), F32), sd((s, 1), F32)],
        compiler_params=_cp("parallel"),
    )(x, yret, ymla, wout)


W_UP_SHARD = F2 // 4


def _ffn_fwd_call(x1, r2, fnw, wup4, cw, cb, wdown, ts):
    s = x1.shape[0]
    wsh = W_UP_SHARD

    def body(x_ref, r_ref, fnw_ref, wup_ref, cw_ref, cb_ref, wd_ref, u_ref, uc_ref, x2_ref, carry_ref):
        _zero_first(pl.program_id(0) == 0, carry_ref)
        xv = x_ref[...]
        h = (xv * r_ref[...] * fnw_ref[...]).astype(BF16)
        conv = []
        for j in range(4):
            cols = slice(j * wsh, (j + 1) * wsh)
            ub = _dot(h, wup_ref[j]).astype(BF16)
            u_ref[:, cols] = ub
            u = ub.astype(F32)
            u1, u2 = _shifted(u, carry_ref[:, cols])
            w = cw_ref[:, cols]
            cb16 = (cb_ref[:, cols] + w[0:1, :] * u2 + w[1:2, :] * u1 + w[2:3, :] * u).astype(BF16)
            uc_ref[:, cols] = cb16
            conv.append(cb16.astype(F32))
            carry_ref[:, cols] = u[ts - 8:, :]
        acc = xv
        for j in range(2):
            a = (_silu(conv[j]) * conv[j + 2]).astype(BF16)
            acc = acc + _dot(a, wd_ref[j * wsh:(j + 1) * wsh, :])
        x2_ref[...] = acc

    sd = jax.ShapeDtypeStruct
    return pl.pallas_call(
        body, name="ffn_fwd", grid=(s // ts,),
        in_specs=[_row(ts, D_MODEL), _row(ts, 1), _full((1, D_MODEL)), _full((4, D_MODEL, wsh)),
                  _full((3, F2)), _full((1, F2)), _full((D_FF, D_MODEL))],
        out_specs=[_row(ts, F2), _row(ts, F2), _row(ts, D_MODEL)],
        out_shape=[sd((s, F2), BF16), sd((s, F2), BF16), sd((s, D_MODEL), F32)],
        scratch_shapes=[pltpu.VMEM((8, F2), F32)],
        compiler_params=_cp("arbitrary"),
    )(x1, r2, fnw, wup4, cw, cb, wdown)


def _shifted(u, hal):
    row = lax.broadcasted_iota(jnp.int32, hal.shape, 0)
    r1, r2 = pltpu.roll(u, 1, 0), pltpu.roll(u, 2, 0)
    top1 = jnp.where(row == 0, hal[7:8, :], r1[0:8, :])
    top2 = jnp.where(row == 0, hal[6:7, :], jnp.where(row == 1, hal[7:8, :], r2[0:8, :]))
    return jnp.concatenate([top1, r1[8:, :]], axis=0), jnp.concatenate([top2, r2[8:, :]], axis=0)


def _prep_weights(w):
    win = w["w_in"]
    pad = lambda n: jnp.zeros((D_MODEL, n), win.dtype)
    win_ext = jnp.concatenate([win[:, :IN_W - ROPE], pad(KPE_LO), win[:, IN_W - ROPE:], pad(LANES - KPE_LO - ROPE)], -1)
    wuq = w["w_uq"].reshape(Q_RANK, N_HEADS, HEAD + ROPE)
    wq = jnp.concatenate([wuq, jnp.zeros((Q_RANK, N_HEADS, LANES - HEAD - ROPE), wuq.dtype)], -1).transpose(1, 0, 2)
    wukv = w["w_ukv"].reshape(KV_RANK, N_HEADS, 2 * HEAD)
    zk = jnp.zeros((KV_RANK, N_HEADS, HEAD), wukv.dtype)
    wk = jnp.concatenate([wukv[:, :, :HEAD], zk], -1).transpose(1, 0, 2)
    wv = jnp.concatenate([wukv[:, :, HEAD:], zk], -1).transpose(1, 0, 2)
    c = lambda a: a.astype(BF16)
    return dict(win=c(win_ext), wq=c(wq), wk=c(wk), wv=c(wv), wout=c(w["w_out"]))


def _prep_mlp_weights(w):
    wup = w["w_up"]
    if wup.ndim == 2:
        wup = wup.reshape(D_MODEL, 4, W_UP_SHARD).transpose(1, 0, 2)
    return dict(wup=wup.astype(BF16), wdown=w["w_down"].astype(BF16))


def _tiles(s):
    return dict(ts=min(s, 512), tr=min(s, 1024), tb=min(s, 512), tg=min(s, 512), tf=D_FF // 2, t2=min(s, 256))


class _Exchanges:
    def __init__(self, w):
        self.w = w

    def mlp_weights(self, after):
        return self.w

    def mlp_grads(self, gw):
        pass

    def behind_out_bwd(self, after):
        pass

    def behind_attention(self, after):
        pass


def _forward(x, positions, w, small, ex):
    s = x.shape[0]
    t = _tiles(s)
    pw = _prep_weights(w)
    cos_r, sin_r, cos_m, sin_m = _rope_tables(positions)
    rc = _ret_consts()
    q, k, v, g, cq, ckv, kpe, r1 = _f1_call(x, small["attn_norm_w"], pw["win"], cos_r, sin_r, cos_m, sin_m, t["ts"])
    o_ret, y_ret = _ret_fwd_call(q, k, v, g, small["ret_gn_w"], rc, t["tr"])
    mq, mk, mv = _mla_pre_call(cq, ckv, kpe, small["mla_q_norm_w"], small["mla_kv_norm_w"],
                               pw["wq"], pw["wk"], pw["wv"], cos_m, sin_m, t["ts"])
    y_mla, mqb = _flash_fwd_call(mq, mk, mv, t["tb"])
    x1, r2 = _out_proj_call(x, y_ret, y_mla, pw["wout"], t["ts"])
    pw.update(_prep_mlp_weights(ex.mlp_weights(r2)))
    u, uc, x2 = _ffn_fwd_call(x1, r2, small["ffn_norm_w"], pw["wup"], w["conv_w"], small["conv_b"], pw["wdown"], t["t2"])
    return dict(pw=pw, tabs=(cos_r, sin_r, cos_m, sin_m), rc=rc, q=q, k=k, v=v, g=g, cq=cq, ckv=ckv, kpe=kpe, r1=r1,
                o_ret=o_ret, y_ret=y_ret, mqb=mqb, mk=mk, mv=mv, y_mla=y_mla, x1=x1, r2=r2, u=u, uc=uc, x2=x2)


def _norm_bwd(dh, xh, r, nw):
    dxn = dh * nw
    return r * (dxn - xh * jnp.mean(dxn * xh, axis=-1, keepdims=True))


def _ordered_after(body, order):
    if order is None:
        return body, [], []
    return (lambda order_ref, *refs: body(*refs)), [pl.BlockSpec(memory_space=pl.ANY)], [order]


def _zero_first(first, *refs):
    @pl.when(first)
    def _():
        for ref in refs:
            ref[...] = jnp.zeros_like(ref)


def _colsum(v):
    return jnp.sum(v, axis=0, keepdims=True)


def _dsilu(g, sg):
    return sg * (1.0 + g * (1.0 - sg))


def _loss_call(x2, tgt, fw, ts):
    s = x2.shape[0]

    def body(x_ref, t_ref, fw_ref, dx_ref, loss_ref, gfw_ref):
        _zero_first(pl.program_id(0) == 0, loss_ref, gfw_ref)
        xv = x_ref[...]
        r = _rstd(xv)
        xh = xv * r
        fwv = fw_ref[...]
        e = xh * fwv - t_ref[...]
        loss_ref[...] += (0.5 / D_MODEL) * _colsum(jnp.sum(e * e, axis=1, keepdims=True))
        dy = e * (1.0 / D_MODEL)
        gfw_ref[...] += _colsum(dy * xh)
        dx_ref[...] = _norm_bwd(dy, xh, r, fwv)

    sd = jax.ShapeDtypeStruct
    return pl.pallas_call(
        body, name="loss_bwd", grid=(s // ts,),
        in_specs=[_row(ts, D_MODEL), _row(ts, D_MODEL), _full((1, D_MODEL))],
        out_specs=[_row(ts, D_MODEL), _full((1, 1)), _full((1, D_MODEL))],
        out_shape=[sd((s, D_MODEL), F32), sd((1, 1), F32), sd((1, D_MODEL), F32)],
        compiler_params=_cp("arbitrary"),
    )(x2, tgt, fw)


def _ffn_bwd_call(dx2, u, uc, cw, wdown, wup4, x1, r2, fnw, ts):
    s = dx2.shape[0]
    nt = s // ts
    wsh = W_UP_SHARD
    rev = lambda i: nt - 1 - i

    def body(dx2_ref, u_ref, uc_ref, cw_ref, wd_ref, wup_ref, x_ref, r_ref, fnw_ref,
             du_ref, dx1_ref, dcw_ref, dcb_ref, dfnw_ref, dwd_hbm, carry_ref, dwd_ref, sem):
        i = pl.program_id(0)
        _zero_first(i == 0, carry_ref, dwd_ref, dcw_ref, dcb_ref, dfnw_ref)
        dxb = dx2_ref[...].astype(BF16)
        dh = jnp.zeros((ts, D_MODEL), F32)
        for j in range(2):
            gcols = slice(j * wsh, (j + 1) * wsh)
            vcols = slice(D_FF + j * wsh, D_FF + (j + 1) * wsh)
            gate, val = uc_ref[:, gcols].astype(F32), uc_ref[:, vcols].astype(F32)
            da = _dot_nt(dxb, wd_ref[gcols, :])
            sg = _sigmoid(gate)
            sl = gate * sg
            dwd_ref[gcols, :] += _dot_tn((sl * val).astype(BF16), dxb)
            for d, cols, shard in ((da * val * _dsilu(gate, sg), gcols, j), (da * sl, vcols, 2 + j)):
                d1, d2 = _shifted_up(d, carry_ref[:, cols])
                uv = u_ref[:, cols].astype(F32)
                for t, dt in enumerate((d2, d1, d)):
                    dcw_ref[t:t + 1, cols] += _colsum(dt * uv)
                dcb_ref[:, cols] += _colsum(d)
                w = cw_ref[:, cols]
                du = (w[2:3, :] * d + w[1:2, :] * d1 + w[0:1, :] * d2).astype(BF16)
                du_ref[:, cols] = du
                dh = dh + _dot_nt(du, wup_ref[shard])
                carry_ref[:, cols] = d[0:8, :]
        r = r_ref[...]
        xh = x_ref[...] * r
        dfnw_ref[...] += _colsum(dh * xh)
        dx1_ref[...] = dx2_ref[...] + _norm_bwd(dh, xh, r, fnw_ref[...])

        @pl.when(i == nt - 1)
        def _():
            cp = pltpu.make_async_copy(dwd_ref, dwd_hbm, sem)
            cp.start()
            cp.wait()

    sd = jax.ShapeDtypeStruct
    row = lambda c: pl.BlockSpec((ts, c), lambda i: (rev(i), 0))
    once = lambda shape: pl.BlockSpec(shape, lambda i: (0,) * len(shape), pipeline_mode=pl.Buffered(1))
    return pl.pallas_call(
        body, name="ffn_bwd", grid=(nt,),
        in_specs=[row(D_MODEL), row(F2), row(F2), once((3, F2)), once((D_FF, D_MODEL)), once((4, D_MODEL, wsh)),
                  row(D_MODEL), row(1), once((1, D_MODEL))],
        out_specs=[row(F2), row(D_MODEL), _full((3, F2)), _full((1, F2)), _full((1, D_MODEL)), pl.BlockSpec(memory_space=pl.ANY)],
        out_shape=[sd((s, F2), BF16), sd((s, D_MODEL), F32), sd((3, F2), F32), sd((1, F2), F32), sd((1, D_MODEL), F32),
                   sd((D_FF, D_MODEL), F32)],
        scratch_shapes=[pltpu.VMEM((8, F2), F32), pltpu.VMEM((D_FF, D_MODEL), F32), pltpu.SemaphoreType.DMA],
        compiler_params=_cp("arbitrary", vmem=VMEM_LIMIT_MLP_BWD),
    )(dx2, u, uc, cw, wdown, wup4, x1, r2, fnw)


def _shifted_up(d, hal):
    n = d.shape[0]
    row = lax.broadcasted_iota(jnp.int32, hal.shape, 0)
    r1, r2 = pltpu.roll(d, n - 1, 0), pltpu.roll(d, n - 2, 0)
    end1 = jnp.where(row == 7, hal[0:1, :], r1[n - 8:, :])
    end2 = jnp.where(row == 6, hal[0:1, :], jnp.where(row == 7, hal[1:2, :], r2[n - 8:, :]))
    return jnp.concatenate([r1[:n - 8, :], end1], axis=0), jnp.concatenate([r2[:n - 8, :], end2], axis=0)


def _dw_norm_call(x, r, nw, b, ts, tn, name):
    s, n = b.shape
    k = x.shape[1]

    def body(x_ref, r_ref, nw_ref, b_ref, dw_ref):
        _zero_first(pl.program_id(1) == 0, dw_ref)
        h = (x_ref[...] * r_ref[...] * nw_ref[...]).astype(BF16)
        dw_ref[...] += _dot_tn(h, b_ref[...])

    return pl.pallas_call(
        body, name=name, grid=(n // tn, s // ts),
        in_specs=[pl.BlockSpec((ts, k), lambda j, i: (i, 0)), pl.BlockSpec((ts, 1), lambda j, i: (i, 0)),
                  pl.BlockSpec((1, k), lambda j, i: (0, 0)), pl.BlockSpec((ts, tn), lambda j, i: (i, j))],
        out_specs=pl.BlockSpec((None, k, tn), lambda j, i: (j, 0, 0)),
        out_shape=jax.ShapeDtypeStruct((n // tn, k, tn), F32),
        compiler_params=_cp("parallel", "arbitrary"),
    )(x, r, nw, b)


def _out_bwd_call(dx1, yret, ymla, wout, ts, order=None):
    s = dx1.shape[0]

    def body(dx_ref, yr_ref, ym_ref, w_ref, dyr_ref, do_ref, dwo_ref):
        _zero_first(pl.program_id(0) == 0, dwo_ref)
        dxb = dx_ref[...].astype(BF16)
        dmix = _dot_nt(dxb, w_ref[...])
        dyr_ref[...] = dmix[:, :RET_W]
        ym = ym_ref[...]
        lane = lax.broadcasted_iota(jnp.int32, (ts, LANES), 1)
        for p in range(N_HEADS // 2):
            dom = dmix[:, RET_W + p * LANES:RET_W + (p + 1) * LANES]
            prod = dom * ym[:, p * LANES:(p + 1) * LANES].astype(F32)
            for hh in range(2):
                mine = (lane >= HEAD) if hh else (lane < HEAD)
                hi, lo = _hi_lo(jnp.sum(jnp.where(mine, prod, 0.0), axis=1, keepdims=True))
                base = jnp.where(lane < HEAD, pltpu.roll(dom, HEAD, 1) if hh else dom, 0.0)
                do_ref[2 * p + hh] = _lane_pair((ts, LANES), V_AUX, -hi, -lo, base).astype(BF16)
        dwo_ref[0:RET_W, :] += _dot_tn(yr_ref[...], dxb)
        dwo_ref[RET_W:, :] += _dot_tn(ym, dxb)

    sd = jax.ShapeDtypeStruct
    body, first_specs, first = _ordered_after(body, order)
    return pl.pallas_call(
        body, name="out_proj_bwd", grid=(s // ts,),
        in_specs=first_specs + [_row(ts, D_MODEL), _row(ts, RET_W), _row(ts, MLA_W), _full((D_MODEL, D_MODEL))],
        out_specs=[_row(ts, RET_W), _hrow(N_HEADS, ts, LANES), _full((D_MODEL, D_MODEL))],
        out_shape=[sd((s, RET_W), F32), sd((N_HEADS, s, LANES), BF16), sd((D_MODEL, D_MODEL), F32)],
        compiler_params=_cp("arbitrary"),
    )(*first, dx1, yret, ymla, wout)


def _ret_bwd_q_call(q, k, v, o, g, dy, gnw, rc, cos_r, sin_r, tr):
    s = q.shape[0]
    c = RET_CHUNK
    nc = tr // c

    def body(q_ref, k_ref, v_ref, o_ref, g_ref, dy_ref, gnw_ref, dm_ref, zeta_ref, xi_ref, cd_ref, bd_ref, cr_ref, sr_ref,
             dq_ref, dg_ref, do_ref, dgnw_ref, st_ref):
        _zero_first(pl.program_id(1) == 0, st_ref, dgnw_ref)
        bd = bd_ref[...]
        avg = bd * (1.0 / HEAD)
        ov = o_ref[...]
        ctr = ov - _dot_hi(ov, avg)
        rs = lax.rsqrt(_dot_hi(ctr * ctr, avg) + EPS)
        oh = ctr * rs
        gg, dyv, gnw_v = g_ref[...], dy_ref[...], gnw_ref[...]
        sg = _sigmoid(gg)
        sl = gg * sg
        dg_ref[...] = (dyv * oh * gnw_v * _dsilu(gg, sg)).astype(BF16)
        dgnw_ref[...] += _colsum(dyv * sl * oh)
        doh = dyv * sl * gnw_v
        dov = (rs * (doh - _dot_hi(doh, avg) - oh * _dot_hi(doh * oh, avg))).astype(BF16)
        do_ref[...] = dov
        chunks = [slice(ci * c, (ci + 1) * c) for ci in range(nc)]
        contrib = [_dot_tn((k_ref[rows, :].astype(F32) * zeta_ref[0]).astype(BF16), v_ref[rows, :]) * bd for rows in chunks]
        st, states = st_ref[...], []
        for ci in range(nc):
            states.append(st.astype(BF16))
            st = st * cd_ref[0] + contrib[ci]
        st_ref[...] = st
        for ci, rows in enumerate(chunks):
            doc = dov[rows, :]
            dq = (_dot_nt(doc, states[ci]) * xi_ref[0]
                  + _pair_product(doc, _stack_heads(v_ref[rows, :]), dm_ref[0], _stack_heads(k_ref[rows, :])))
            dq_ref[rows, :] = _unrope(dq, cr_ref[rows, :], sr_ref[rows, :], HEAD // 2).astype(BF16)

    slab = pl.BlockSpec((tr, LANES), lambda j, i: (i, j))
    tab = pl.BlockSpec((tr, LANES), lambda j, i: (i, 0))
    vec = pl.BlockSpec((1, LANES), lambda j, i: (0, j))
    sd = jax.ShapeDtypeStruct
    return pl.pallas_call(
        body, name="ret_bwd_q", grid=(4, s // tr),
        in_specs=[slab, slab, slab, slab, slab, slab, vec,
                  pl.BlockSpec((1, c, 2 * c), lambda j, i: (j, 0, 0)),
                  pl.BlockSpec((1, c, LANES), lambda j, i: (j, 0, 0)),
                  pl.BlockSpec((1, c, LANES), lambda j, i: (j, 0, 0)),
                  pl.BlockSpec((1, LANES, LANES), lambda j, i: (j, 0, 0)),
                  pl.BlockSpec((LANES, LANES), lambda j, i: (0, 0)), tab, tab],
        out_specs=[slab, slab, slab, vec],
        out_shape=[sd((s, RET_W), BF16), sd((s, RET_W), BF16), sd((s, RET_W), BF16), sd((1, RET_W), F32)],
        scratch_shapes=[pltpu.VMEM((LANES, LANES), F32)],
        compiler_params=_cp("parallel", "arbitrary"),
    )(q, k, v, o, g, dy, gnw, rc["dmask"], rc["zeta"], rc["xi"], rc["cd"], rc["bd"], cos_r, sin_r)


def _ret_bwd_kv_call(q, k, v, do, rc, cos_r, sin_r, tr):
    s = q.shape[0]
    c = RET_CHUNK
    nc = tr // c
    nt = s // tr

    def body(q_ref, k_ref, v_ref, do_ref, dm_ref, zeta_ref, xi_ref, cd_ref, bd_ref, cr_ref, sr_ref, dk_ref, dv_ref, gs_ref):
        _zero_first(pl.program_id(1) == 0, gs_ref)
        bd = bd_ref[...]
        chunks = [slice(ci * c, (ci + 1) * c) for ci in range(nc)]
        contrib = [_dot_tn((q_ref[rows, :].astype(F32) * xi_ref[0]).astype(BF16), do_ref[rows, :]) * bd for rows in chunks]
        gs, states = gs_ref[...], [None] * nc
        for ci in reversed(range(nc)):
            states[ci] = gs.astype(BF16)
            gs = gs * cd_ref[0] + contrib[ci]
        gs_ref[...] = gs
        for ci, rows in enumerate(chunks):
            kc, vc = k_ref[rows, :], v_ref[rows, :]
            q2, do2 = _stack_heads(q_ref[rows, :]), _stack_heads(do_ref[rows, :])
            gb = states[ci]
            dk = _dot_nt(vc, gb) * zeta_ref[0] + _pair_product(vc, do2, dm_ref[0], q2)
            dv = _dot(kc, gb) * zeta_ref[0] + _pair_product(kc, q2, dm_ref[0], do2)
            dk_ref[rows, :] = (_unrope(dk, cr_ref[rows, :], sr_ref[rows, :], HEAD // 2) * (HEAD ** -0.5)).astype(BF16)
            dv_ref[rows, :] = dv.astype(BF16)

    slab = pl.BlockSpec((tr, LANES), lambda j, i: (nt - 1 - i, j))
    tab = pl.BlockSpec((tr, LANES), lambda j, i: (nt - 1 - i, 0))
    sd = jax.ShapeDtypeStruct
    return pl.pallas_call(
        body, name="ret_bwd_kv", grid=(4, nt),
        in_specs=[slab, slab, slab, slab,
                  pl.BlockSpec((1, c, 2 * c), lambda j, i: (j, 0, 0)),
                  pl.BlockSpec((1, c, LANES), lambda j, i: (j, 0, 0)),
                  pl.BlockSpec((1, c, LANES), lambda j, i: (j, 0, 0)),
                  pl.BlockSpec((1, LANES, LANES), lambda j, i: (j, 0, 0)),
                  pl.BlockSpec((LANES, LANES), lambda j, i: (0, 0)), tab, tab],
        out_specs=[slab, slab],
        out_shape=[sd((s, RET_W), BF16), sd((s, RET_W), BF16)],
        scratch_shapes=[pltpu.VMEM((LANES, LANES), F32)],
        compiler_params=_cp("parallel", "arbitrary"),
    )(q, k, v, do, rc["dmask_t"], rc["zeta"], rc["xi"], rc["cd"], rc["bd"], cos_r, sin_r)


FLASH_BWD_HEADS = 4


def _flash_bwd_call(qb, k, v, do, tb, order=None):
    s = qb.shape[1]
    nb = s // tb
    hg = FLASH_BWD_HEADS
    pairs = [(a, b) for a in range(nb) for b in range(a, nb)]
    ki_of, qi_of = (jnp.asarray(np.array(col, np.int32)) for col in zip(*pairs))
    extra = [] if order is None else [order]

    def body(ki_ref, qi_ref, *refs):
        q_ref, k_ref, v_ref, do_ref, dk_ref, dv_ref, dq_hbm, dka_ref, dva_ref, dq_ref, sem = refs[len(extra):]
        g, p = pl.program_id(0), pl.program_id(1)
        ki, qi = ki_ref[p], qi_ref[p]
        _zero_first(p == 0, dq_ref)
        _zero_first(qi == ki, dka_ref, dva_ref)
        rows = pl.ds(pl.multiple_of(qi * tb, tb), tb)

        def step(masked):
            if masked:
                keep = lax.broadcasted_iota(jnp.int32, (tb, tb), 0) <= lax.broadcasted_iota(jnp.int32, (tb, tb), 1)
            for h in range(hg):
                st = _dot_nt(k_ref[h], q_ref[h])
                if masked:
                    st = jnp.where(keep, st, NEG)
                pt = jnp.exp2(st)
                dob = do_ref[h]
                dva_ref[h] += _dot(pt.astype(BF16), dob)
                dst = (pt * _dot_nt(v_ref[h], dob)).astype(BF16)
                dka_ref[h] += _dot(dst, q_ref[h])
                dq_ref[h, rows, :] += _dot_tn(dst, k_ref[h])

        @pl.when(qi > ki)
        def _():
            step(False)

        @pl.when(qi == ki)
        def _():
            step(True)

        @pl.when(qi == nb - 1)
        def _():
            dk_ref[...] = (dka_ref[...] * LN2).astype(BF16)
            dv_ref[...] = dva_ref[...].astype(BF16)

        @pl.when(p == len(pairs) - 1)
        def _():
            cp = pltpu.make_async_copy(dq_ref, dq_hbm.at[pl.ds(g * hg, hg)], sem)
            cp.start()
            cp.wait()

    kspec = pl.BlockSpec((hg, tb, LANES), lambda g, p, ki_ref, qi_ref: (g, ki_ref[p], 0))
    qspec = pl.BlockSpec((hg, tb, LANES), lambda g, p, ki_ref, qi_ref: (g, qi_ref[p], 0))
    hm = jax.ShapeDtypeStruct((N_HEADS, s, LANES), BF16)
    return pl.pallas_call(
        body, name="mla_flash_bwd",
        grid_spec=pltpu.PrefetchScalarGridSpec(
            num_scalar_prefetch=2, grid=(N_HEADS // hg, len(pairs)),
            in_specs=[ANY] * len(extra) + [qspec, kspec, kspec, qspec],
            out_specs=[kspec, kspec, ANY],
            scratch_shapes=[pltpu.VMEM((hg, tb, LANES), F32), pltpu.VMEM((hg, tb, LANES), F32),
                            pltpu.VMEM((hg, s, LANES), F32), pltpu.SemaphoreType.DMA]),
        out_shape=[hm, hm, jax.ShapeDtypeStruct((N_HEADS, s, LANES), F32)],
        compiler_params=_cp("arbitrary", "arbitrary"),
    )(ki_of, qi_of, *extra, qb, k, v, do)


def _mla_post_call(dq, dk, dv, cq, ckv, qnw, kvnw, wq, wk, wv, cos_m, sin_m, ts):
    s = cq.shape[0]

    def body(dq_ref, dk_ref, dv_ref, cq_ref, ckv_ref, qnw_ref, kvnw_ref, wq_ref, wk_ref, wv_ref, cm_ref, sm_ref,
             dcq_ref, dckv_ref, dkpe_ref, dwq_ref, dwk_ref, dwv_ref, dqnw_ref, dkvnw_ref):
        _zero_first(pl.program_id(0) == 0, dwq_ref, dwk_ref, dwv_ref, dqnw_ref, dkvnw_ref)
        cqv, ckvv = cq_ref[...], ckv_ref[...]
        rq, rkv = _rstd(cqv), _rstd(ckvv)
        qh_, kvh_ = cqv * rq, ckvv * rkv
        qnw_v, kvnw_v = qnw_ref[...], kvnw_ref[...]
        cqn = (qh_ * qnw_v).astype(BF16)
        ckvn = (kvh_ * kvnw_v).astype(BF16)
        cm, sm = cm_ref[...], sm_ref[...]
        dcqn = jnp.zeros((ts, Q_RANK), F32)
        dckvn = jnp.zeros((ts, KV_RANK), F32)
        dkpe = jnp.zeros((ts, LANES), F32)
        for h in range(N_HEADS):
            dqu = _unrope(dq_ref[h] * SM_SCALE, cm, sm, ROPE // 2).astype(BF16)
            dwq_ref[h] += _dot_tn(cqn, dqu)
            dcqn = dcqn + _dot_nt(dqu, wq_ref[h])
            dkb, dvb = dk_ref[h], dv_ref[h]
            dkpe = dkpe + dkb.astype(F32)
            dwk_ref[h] += _dot_tn(ckvn, dkb)
            dwv_ref[h] += _dot_tn(ckvn, dvb)
            dckvn = dckvn + _dot_nt(dkb, wk_ref[h]) + _dot_nt(dvb, wv_ref[h])
        lane = lax.broadcasted_iota(jnp.int32, (ts, LANES), 1)
        dkpe = jnp.where((lane >= KPE_LO) & (lane < KPE_LO + ROPE), dkpe, 0.0)
        dkpe_ref[...] = _unrope(dkpe, cm, sm, ROPE // 2).astype(BF16)
        dqnw_ref[...] += _colsum(dcqn * qh_)
        dkvnw_ref[...] += _colsum(dckvn * kvh_)
        dcq_ref[...] = _norm_bwd(dcqn, qh_, rq, qnw_v).astype(BF16)
        dckv_ref[...] = _norm_bwd(dckvn, kvh_, rkv, kvnw_v).astype(BF16)

    sd = jax.ShapeDtypeStruct
    hm = _hrow(N_HEADS, ts, LANES)
    return pl.pallas_call(
        body, name="mla_post", grid=(s // ts,),
        in_specs=[hm, hm, hm, _row(ts, Q_RANK), _row(ts, KV_RANK), _full((1, Q_RANK)), _full((1, KV_RANK)),
                  _full((N_HEADS, Q_RANK, LANES)), _full((N_HEADS, KV_RANK, LANES)), _full((N_HEADS, KV_RANK, LANES)),
                  _row(ts, LANES), _row(ts, LANES)],
        out_specs=[_row(ts, Q_RANK), _row(ts, KV_RANK), _row(ts, LANES),
                   _full((N_HEADS, Q_RANK, LANES)), _full((N_HEADS, KV_RANK, LANES)), _full((N_HEADS, KV_RANK, LANES)),
                   _full((1, Q_RANK)), _full((1, KV_RANK))],
        out_shape=[sd((s, Q_RANK), BF16), sd((s, KV_RANK), BF16), sd((s, LANES), BF16),
                   sd((N_HEADS, Q_RANK, LANES), F32), sd((N_HEADS, KV_RANK, LANES), F32), sd((N_HEADS, KV_RANK, LANES), F32),
                   sd((1, Q_RANK), F32), sd((1, KV_RANK), F32)],
        compiler_params=_cp("arbitrary"),
    )(dq, dk, dv, cq, ckv, qnw, kvnw, wq, wk, wv, cos_m, sin_m)


def _in_bwd_call(parts, x, r1, anw, dx1, win, ts):
    s = x.shape[0]
    widths = [p.shape[1] for p in parts]
    np_ = len(parts)

    def body(*refs):
        p_refs = refs[:np_]
        x_ref, r_ref, anw_ref, dx1_ref, w_ref, dx_ref, dw_ref, danw_ref = refs[np_:]
        _zero_first(pl.program_id(0) == 0, dw_ref, danw_ref)
        dproj = jnp.concatenate([p[...] for p in p_refs], axis=-1)
        r, anw_v = r_ref[...], anw_ref[...]
        xh = x_ref[...] * r
        dw_ref[...] += _dot_tn((xh * anw_v).astype(BF16), dproj)
        dh = _dot_nt(dproj, w_ref[...])
        danw_ref[...] += _colsum(dh * xh)
        dx_ref[...] = dx1_ref[...] + _norm_bwd(dh, xh, r, anw_v)

    sd = jax.ShapeDtypeStruct
    return pl.pallas_call(
        body, name="in_proj_bwd", grid=(s // ts,),
        in_specs=[_row(ts, w) for w in widths]
        + [_row(ts, D_MODEL), _row(ts, 1), _full((1, D_MODEL)), _row(ts, D_MODEL), _full((D_MODEL, IN_EXT))],
        out_specs=[_row(ts, D_MODEL), _full((D_MODEL, IN_EXT)), _full((1, D_MODEL))],
        out_shape=[sd((s, D_MODEL), F32), sd((D_MODEL, IN_EXT), F32), sd((1, D_MODEL), F32)],
        compiler_params=_cp("arbitrary"),
    )(*parts, x, r1, anw, dx1, win)


def _local_step(x, positions, tgt, w, small, ex=None):
    s = x.shape[0]
    t = _tiles(s)
    ex = _Exchanges(w) if ex is None else ex
    f = _forward(x, positions, w, small, ex)
    pw, rc = f["pw"], f["rc"]
    cos_r, sin_r, cos_m, sin_m = f["tabs"]
    dx2, loss, g_fw = _loss_call(f["x2"], tgt, small["final_norm_w"], t["ts"])
    du, dx1, g_cw, g_cb, g_fnw, g_wd = _ffn_bwd_call(dx2, f["u"], f["uc"], w["conv_w"], pw["wdown"], pw["wup"],
                                                     f["x1"], f["r2"], small["ffn_norm_w"], t["t2"])
    g_wup = _dw_norm_call(f["x1"], f["r2"], small["ffn_norm_w"], du, t["ts"], F2 // 4, "dw_up")
    started = ex.mlp_grads(dict(w_up=g_wup, w_down=g_wd))
    dy_ret, do, g_wout = _out_bwd_call(dx1, f["y_ret"], f["y_mla"], pw["wout"], t["ts"], started)
    started = ex.behind_out_bwd(g_wout)
    drq, dg, do_ret, g_gnw = _ret_bwd_q_call(f["q"], f["k"], f["v"], f["o_ret"], f["g"], dy_ret, small["ret_gn_w"], rc, cos_r, sin_r, t["tr"])
    drk, drv = _ret_bwd_kv_call(f["q"], f["k"], f["v"], do_ret, rc, cos_r, sin_r, t["tr"])
    dmk, dmv, dmq = _flash_bwd_call(f["mqb"], f["mk"], f["mv"], do, t["tb"], started)
    ex.behind_attention(dmk)
    dcq, dckv, dkpe, g_wq, g_wk, g_wv, g_qnw, g_kvnw = _mla_post_call(
        dmq, dmk, dmv, f["cq"], f["ckv"], small["mla_q_norm_w"], small["mla_kv_norm_w"], pw["wq"], pw["wk"], pw["wv"], cos_m, sin_m, t["ts"])
    gx, g_win_ext, g_anw = _in_bwd_call([drq, drk, drv, dg, dcq, dckv, dkpe], x, f["r1"], small["attn_norm_w"], dx1, pw["win"], t["ts"])
    lo = IN_W - ROPE
    g_win = jnp.concatenate([g_win_ext[:, :lo], g_win_ext[:, lo + KPE_LO:lo + KPE_LO + ROPE]], -1)
    g_wuq = g_wq.transpose(1, 0, 2)[:, :, :HEAD + ROPE].reshape(Q_RANK, N_HEADS * (HEAD + ROPE))
    g_wukv = jnp.concatenate([g_wk[:, :, :HEAD], g_wv[:, :, :HEAD]], -1).transpose(1, 0, 2).reshape(KV_RANK, 2 * MLA_W)
    gw = dict(w_in=g_win, w_uq=g_wuq, w_ukv=g_wukv, w_out=g_wout, w_up=g_wup,
              conv_w=g_cw, w_down=g_wd)
    gs = dict(attn_norm_w=g_anw, ret_gn_w=g_gnw, mla_q_norm_w=g_qnw, mla_kv_norm_w=g_kvnw, ffn_norm_w=g_fnw,
              conv_b=g_cb, final_norm_w=g_fw)
    return loss, gx, gw, gs


MESH_ID = pl.DeviceIdType.MESH
ANY = pl.BlockSpec(memory_space=pl.ANY)
VMEM_SPEC = pl.BlockSpec(memory_space=pltpu.VMEM)
N_DEV = 8
GROUP_A = (("w_in", (D_MODEL, IN_W // 4), 1), ("w_uq", (Q_RANK, 192), 1), ("w_ukv", (KV_RANK, 256), 1),
           ("w_out", (D_MODEL // 4, D_MODEL), 0))
GROUP_B = (("w_up", (D_MODEL, F2 // 4), 1), ("w_down", (D_FF // 4, D_MODEL), 0))
HBM_SPEC = pl.BlockSpec(memory_space=pltpu.HBM)
SEM_SPEC = pl.BlockSpec(memory_space=pltpu.SEMAPHORE)


def _mesh_pos():
    return lax.axis_index("x"), lax.axis_index("y"), lax.axis_index("c")


def _other_chips(x, y):
    return [(1 - x, y), (x, 1 - y), (1 - x, 1 - y)]


def _remote(src, dst, send_sems, recv_sems, k, dev):
    return pltpu.make_async_remote_copy(src_ref=src, dst_ref=dst, send_sem=send_sems.at[k], recv_sem=recv_sems.at[k],
                                        device_id=dev, device_id_type=MESH_ID)


def _gather_list_call(parts, tag):
    n = len(parts)

    def body(*refs):
        srcs, outs, (send_sems, recv_sems) = refs[:n], refs[n:2 * n], refs[2 * n:]
        x, y, c = _mesh_pos()
        sm = 2 * x + y
        chips = _other_chips(x, y)
        sib = (x, y, 1 - c)
        rc = lambda k, src, dst, dev: _remote(src, dst, send_sems, recv_sems, k, dev)
        first = [rc(7 * i + j, srcs[i].at[c], outs[i].at[sm, c], (cx, cy, c)) for i in range(n) for j, (cx, cy) in enumerate(chips)]
        own = [rc(7 * i + 6, srcs[i], outs[i].at[sm], sib) for i in range(n)]
        for cp in first + own:
            cp.start()
        passed = []
        for j, (cx, cy) in enumerate(chips):
            for i in range(n):
                land = outs[i].at[2 * cx + cy, c]
                rc(7 * i + j, srcs[i].at[c], land, (cx, cy, c)).wait_recv()
                cp = rc(7 * i + 3 + j, land, land, sib)
                cp.start()
                passed.append(cp)
        for j, (cx, cy) in enumerate(chips):
            for i in range(n):
                rc(7 * i + 3 + j, srcs[i].at[c], outs[i].at[2 * cx + cy, 1 - c], sib).wait_recv()
        for cp in own:
            cp.wait_recv()
        for cp in first + passed + own:
            cp.wait_send()

    return pl.pallas_call(
        body, name="weights_all_gather_" + tag,
        in_specs=[ANY] * n, out_specs=[ANY] * n,
        out_shape=[jax.ShapeDtypeStruct((4,) + p.shape, p.dtype) for p in parts],
        scratch_shapes=[pltpu.SemaphoreType.DMA((7 * n,)), pltpu.SemaphoreType.DMA((7 * n,))],
    )(*parts)


def _direct_gather_copies(srcs, lands, send_sems, recv_sems):
    x, y, c = _mesh_pos()
    sm = 2 * x + y
    sends, recvs = [], []
    for i, (src, land) in enumerate(zip(srcs, lands)):
        for j, (cx, cy) in enumerate(_other_chips(x, y)):
            for t in range(2):
                sends.append(_remote(src.at[c], land.at[sm, c], send_sems, recv_sems, 13 * i + 4 * j + 2 * c + t, (cx, cy, t)))
                recvs.append(_remote(src.at[t], land.at[2 * cx + cy, t], send_sems, recv_sems, 13 * i + 4 * j + 2 * t + c, (cx, cy, t)))
        sends.append(_remote(src, land.at[sm], send_sems, recv_sems, 13 * i + 12, (x, y, 1 - c)))
        recvs.append(_remote(src, land.at[sm], send_sems, recv_sems, 13 * i + 12, (x, y, 1 - c)))
    return sends, recvs


def _sibling_copies(srcs, lands, send_sems, recv_sems):
    x, y, c = _mesh_pos()
    cps = [_remote(src.at[s, 1 - c], land.at[s], send_sems, recv_sems, 4 * i + s, (x, y, 1 - c))
           for i, (src, land) in enumerate(zip(srcs, lands)) for s in range(4)]
    return cps, cps


def _chips_copies(srcs, lands, send_sems, recv_sems):
    x, y, c = _mesh_pos()
    cps = [_remote(src.at[2 * cx + cy], land.at[j], send_sems, recv_sems, 3 * i + j, (cx, cy, c))
           for i, (src, land) in enumerate(zip(srcs, lands)) for j, (cx, cy) in enumerate(_other_chips(x, y))]
    return cps, cps


def _share_copies(srcs, lands, send_sems, recv_sems):
    x, y, c = _mesh_pos()
    cps = [_remote(src, land, send_sems, recv_sems, i, (x, y, 1 - c)) for i, (src, land) in enumerate(zip(srcs, lands))]
    return cps, cps


def _exchange_call(name, copies, srcs, land_shapes, n_sems):
    n = len(srcs)

    def body(*refs):
        sends, recvs = copies(refs[:n], refs[n:2 * n], refs[2 * n], refs[2 * n + 1])
        for cp in sends:
            cp.start()
        for cp in sends:
            cp.wait_send()
        for cp in recvs:
            cp.wait_recv()

    return pl.pallas_call(
        body, name=name, in_specs=[ANY] * n, out_specs=[ANY] * n, out_shape=list(land_shapes),
        scratch_shapes=[pltpu.SemaphoreType.DMA((n_sems,)), pltpu.SemaphoreType.DMA((n_sems,))],
    )(*srcs)


def _exchange_start_call(name, copies, srcs, land_shapes, n_sems, order=None):
    n = len(srcs)
    extra = [] if order is None else [order]
    k = 2 * n + len(extra)

    def body(*refs):
        sends, _ = copies(refs[:n], refs[n:2 * n], refs[k], refs[k + 1])
        for cp in sends:
            cp.start()
        refs[-1][...] = jnp.zeros_like(refs[-1])

    hbm = lambda a: pltpu.with_memory_space_constraint(a, pltpu.HBM)
    lands = [hbm(lax.empty(sd.shape, sd.dtype)) for sd in land_shapes]
    sem = pltpu.SemaphoreType.DMA((n_sems,))
    out = pl.pallas_call(
        body, name=name,
        out_shape=(sem, sem, *[pltpu.HBM(a.shape, a.dtype) for a in list(srcs) + lands], jax.ShapeDtypeStruct((8, LANES), F32)),
        in_specs=[HBM_SPEC] * (2 * n) + [ANY] * len(extra), out_specs=(SEM_SPEC, SEM_SPEC, *[HBM_SPEC] * (2 * n), VMEM_SPEC),
        input_output_aliases={i: 2 + i for i in range(2 * n)},
        compiler_params=pltpu.CompilerParams(has_side_effects=pltpu.SideEffectType.DATAFLOW_SIDE_EFFECTING),
    )(*[hbm(a) for a in srcs], *lands, *extra)
    return out[0], out[1], out[2:2 + n], out[2 + n:2 + 2 * n], out[-1]


def _exchange_wait_call(name, copies, started, after):
    send_sems, recv_sems, srcs, lands, _ = started
    n = len(srcs)

    def body(*refs):
        sends, recvs = copies(refs[:n], refs[n:2 * n], refs[2 * n], refs[2 * n + 1])
        for cp in sends:
            cp.wait_send()
        for cp in recvs:
            cp.wait_recv()

    out = pl.pallas_call(
        body, name=name,
        out_shape=tuple(pltpu.HBM(a.shape, a.dtype) for a in list(srcs) + list(lands)),
        in_specs=[HBM_SPEC] * (2 * n) + [SEM_SPEC, SEM_SPEC, ANY], out_specs=tuple([HBM_SPEC] * (2 * n)),
        input_output_aliases={i: i for i in range(2 * n)},
        compiler_params=pltpu.CompilerParams(has_side_effects=pltpu.SideEffectType.DATAFLOW_SIDE_EFFECTING),
    )(*srcs, *lands, send_sems, recv_sems, after)
    return out[:n], out[n:]


def _rows_tile(rows, width, itemsize=4):
    limit = max(16, (3 << 20) // (width * itemsize))
    if rows <= limit:
        return rows
    return max(t for t in range(16, limit + 1, 16) if rows % t == 0)


def _sum_sibling_call(g, buf, c, name):
    _, _, rh, w = g.shape
    tile = _rows_tile(rh, w)

    def body(c_ref, g_ref, b_ref, p_ref, pb_ref):
        p = g_ref[...] + b_ref[...]
        p_ref[...] = p
        pb_ref[...] = p.astype(BF16)

    blk = pl.BlockSpec((None, tile, w), lambda s, i, c_ref: (s, i, 0))
    return pl.pallas_call(
        body, name=name,
        grid_spec=pltpu.PrefetchScalarGridSpec(
            num_scalar_prefetch=1, grid=(4, rh // tile),
            in_specs=[pl.BlockSpec((None, None, tile, w), lambda s, i, c_ref: (s, c_ref[0], i, 0)), blk],
            out_specs=[blk, blk]),
        out_shape=[jax.ShapeDtypeStruct((4, rh, w), F32), jax.ShapeDtypeStruct((4, rh, w), BF16)],
        compiler_params=_cp("parallel", "parallel"),
    )(c, g, buf)


def _sum_chips_call(p, buf, sm, name):
    _, rh, w = p.shape
    tile = _rows_tile(rh, w)

    def body(sm_ref, p_ref, b_ref, f_ref):
        f_ref[...] = ((p_ref[...] + b_ref[0].astype(F32)) + b_ref[1].astype(F32)) + b_ref[2].astype(F32)

    return pl.pallas_call(
        body, name=name,
        grid_spec=pltpu.PrefetchScalarGridSpec(
            num_scalar_prefetch=1, grid=(rh // tile,),
            in_specs=[pl.BlockSpec((None, tile, w), lambda i, sm_ref: (sm_ref[0], i, 0)),
                      pl.BlockSpec((3, tile, w), lambda i, sm_ref: (0, i, 0))],
            out_specs=pl.BlockSpec((tile, w), lambda i, sm_ref: (i, 0))),
        out_shape=jax.ShapeDtypeStruct((rh, w), F32),
        compiler_params=_cp("parallel"),
    )(sm, p, buf)


def _adamw_halves_call(w, g_mine, g_sib, c, m, v, name):
    r, wd = w.shape
    rh = r // 2
    tile = _rows_tile(rh, wd)
    nt = rh // tile

    def body(c_ref, w_ref, gm_ref, gs_ref, m_ref, v_ref, g_ref, d_ref, nm_ref, nv_ref):
        gv = jnp.where(pl.program_id(0) == c_ref[0], gm_ref[...], gs_ref[...])
        g_ref[...] = gv
        nm = ADAM_B1 * m_ref[...] + (1.0 - ADAM_B1) * gv
        nv = ADAM_B2 * v_ref[...] + (1.0 - ADAM_B2) * jnp.square(gv)
        m_hat = nm / (1.0 - ADAM_B1 ** ADAM_STEP)
        v_hat = nv / (1.0 - ADAM_B2 ** ADAM_STEP)
        d_ref[...] = -ADAM_LR * (m_hat / (jnp.sqrt(v_hat) + ADAM_EPS) + ADAM_WD * w_ref[...])
        nm_ref[...] = nm
        nv_ref[...] = nv

    whole = pl.BlockSpec((tile, wd), lambda h, i, c_ref: (h * nt + i, 0))
    half = pl.BlockSpec((tile, wd), lambda h, i, c_ref: (i, 0))
    sd = jax.ShapeDtypeStruct((r, wd), F32)
    return pl.pallas_call(
        body, name=name,
        grid_spec=pltpu.PrefetchScalarGridSpec(
            num_scalar_prefetch=1, grid=(2, nt),
            in_specs=[whole, half, half, whole, whole], out_specs=[whole] * 4),
        out_shape=[sd, sd, sd, sd],
        compiler_params=_cp("parallel", "parallel"),
    )(c, w, g_mine, g_sib, m, v)


def _exchange8_call(vec, reduce, name):
    rows = vec.shape[0]

    def body(v_ref, out_ref, *rest):
        slots, send_sems, recv_sems = (rest if reduce else (out_ref,) + rest)
        x, y, c = _mesh_pos()
        me = 4 * x + 2 * y + c
        slots[me] = v_ref[...]

        def rcopy(k, to_me):
            bx, by, bc = (k >> 2) & 1, (k >> 1) & 1, k & 1
            px, py, pc = (1 - x if bx else x), (1 - y if by else y), (1 - c if bc else c)
            slot = 4 * px + 2 * py + pc if to_me else me
            return pltpu.make_async_remote_copy(src_ref=v_ref, dst_ref=slots.at[slot], send_sem=send_sems.at[k - 1],
                                                recv_sem=recv_sems.at[k - 1], device_id=(px, py, pc), device_id_type=MESH_ID)

        for k in range(1, N_DEV):
            rcopy(k, False).start()
        for k in range(1, N_DEV):
            rcopy(k, True).wait_recv()
        for k in range(1, N_DEV):
            rcopy(k, False).wait_send()
        if reduce:
            tot = slots[0]
            for d in range(1, N_DEV):
                tot = tot + slots[d]
            out_ref[...] = tot

    stack = jax.ShapeDtypeStruct((N_DEV, rows, LANES), F32)
    return pl.pallas_call(
        body, name=name,
        in_specs=[VMEM_SPEC], out_specs=VMEM_SPEC,
        out_shape=jax.ShapeDtypeStruct((rows, LANES), F32) if reduce else stack,
        scratch_shapes=([pltpu.VMEM((N_DEV, rows, LANES), F32)] if reduce else [])
        + [pltpu.SemaphoreType.DMA((N_DEV - 1,)), pltpu.SemaphoreType.DMA((N_DEV - 1,))],
    )(vec)


def _adamw_call(w, g, m, v, name):
    r, c = w.shape
    rb = r if r <= 256 else (256 if r % 256 == 0 else 352)
    assert r % rb == 0

    def body(w_ref, g_ref, m_ref, v_ref, d_ref, nm_ref, nv_ref):
        gv = g_ref[...]
        nm = ADAM_B1 * m_ref[...] + (1.0 - ADAM_B1) * gv
        nv = ADAM_B2 * v_ref[...] + (1.0 - ADAM_B2) * jnp.square(gv)
        m_hat = nm / (1.0 - ADAM_B1 ** ADAM_STEP)
        v_hat = nv / (1.0 - ADAM_B2 ** ADAM_STEP)
        d_ref[...] = -ADAM_LR * (m_hat / (jnp.sqrt(v_hat) + ADAM_EPS) + ADAM_WD * w_ref[...])
        nm_ref[...] = nm
        nv_ref[...] = nv

    spec = pl.BlockSpec((rb, c), lambda i: (i, 0))
    sd = jax.ShapeDtypeStruct((r, c), F32)
    return pl.pallas_call(
        body, name=name, grid=(r // rb,),
        in_specs=[spec] * 4, out_specs=[spec] * 3, out_shape=[sd, sd, sd],
        compiler_params=_cp("parallel"),
    )(w, g, m, v)


SMALL = (("attn_norm_w", D_MODEL), ("ret_gn_w", RET_W), ("mla_q_norm_w", Q_RANK), ("mla_kv_norm_w", KV_RANK),
         ("ffn_norm_w", D_MODEL), ("conv_b", F2), ("final_norm_w", D_MODEL))
WEIGHT_ORDER = ("attn_norm_w", "w_in", "ret_gn_w", "mla_q_norm_w", "w_uq", "mla_kv_norm_w", "w_ukv", "w_out",
                "ffn_norm_w", "w_up", "conv_w", "conv_b", "w_down", "final_norm_w")


def _pad_rows(flat, rows):
    return jnp.concatenate([flat, jnp.zeros((rows * LANES - flat.shape[0],), flat.dtype)]).reshape(rows, LANES)


def kernel(x, positions, attn_norm_w, w_in, ret_gn_w, mla_q_norm_w, w_uq, mla_kv_norm_w, w_ukv, w_out, ffn_norm_w, w_up, conv_w, conv_b, w_down, final_norm_w, loss_target, m_attn_norm_w, m_w_in, m_ret_gn_w, m_mla_q_norm_w, m_w_uq, m_mla_kv_norm_w, m_w_ukv, m_w_out, m_ffn_norm_w, m_w_up, m_conv_w, m_conv_b, m_w_down, m_final_norm_w, v_attn_norm_w, v_w_in, v_ret_gn_w, v_mla_q_norm_w, v_w_uq, v_mla_kv_norm_w, v_w_ukv, v_w_out, v_ffn_norm_w, v_w_up, v_conv_w, v_conv_b, v_w_down, v_final_norm_w):
    args = dict(locals())
    cx, cy, cc = _mesh_pos()
    sm = 2 * cx + cy

    c_arr, sm_arr = cc.reshape(1).astype(jnp.int32), sm.reshape(1).astype(jnp.int32)
    sds = jax.ShapeDtypeStruct

    def my_shards(group):
        return [args[n][0].astype(BF16).reshape(2, r // 2, c) for n, (r, c), _ in group]

    def full_weights(gathered, group):
        full = {}
        for (n, (r, c), axis), got in zip(group, gathered):
            piece = got.reshape(4, r, c)
            full[n] = piece if n == "w_up" else (piece.transpose(1, 0, 2).reshape(r, 4 * c) if axis == 1 else piece.reshape(4 * r, c))
        return full

    def by_owner(gw, group):
        out = []
        for n, (r, c), axis in group:
            g = gw[n]
            if axis == 1 and g.ndim == 2:
                g = g.reshape(r, 4, c).transpose(1, 0, 2)
            out.append(g.reshape(4, 2, r // 2, c))
        return out

    def sibling_shapes(gs):
        return [sds((4,) + g.shape[2:], F32) for g in gs]

    def chip_sums(gs, bufs, group):
        res = [_sum_sibling_call(g, b, c_arr, "grads_sum_sibling_" + n) for g, b, (n, _, _) in zip(gs, bufs, group)]
        return [p for p, _ in res], [pb for _, pb in res]

    def chips_shapes(pbs):
        return [sds((3,) + pb.shape[1:], BF16) for pb in pbs]

    def totals(ps, lands, group, tag):
        fins = [_sum_chips_call(p, l, sm_arr, "grads_sum_chips_" + n) for p, l, (n, _, _) in zip(ps, lands, group)]
        sibs = _exchange_call("grads_rs_share_" + tag, _share_copies, fins, [sds(f.shape, F32) for f in fins], len(fins))
        return {n: (f, s) for (n, _, _), f, s in zip(group, fins, sibs)}

    class StepExchanges(_Exchanges):
        def __init__(self, order):
            shards = my_shards(GROUP_B)
            self.gather = _exchange_start_call("weights_gather_start_b", _direct_gather_copies, shards,
                                               [sds((4,) + s.shape, BF16) for s in shards], 13 * len(shards), order)
            self.red = None

        def token(self):
            return self.gather[4][0:1, 0:1]

        def mlp_weights(self, after):
            return full_weights(_exchange_wait_call("weights_gather_wait_b", _direct_gather_copies, self.gather, after)[1], GROUP_B)

        def mlp_grads(self, gw):
            gs = by_owner(gw, GROUP_B)
            self.step1 = _exchange_start_call("grads_rs_sibling_start_b", _sibling_copies, gs, sibling_shapes(gs), 4 * len(gs))
            return self.step1[4]

        def behind_out_bwd(self, after):
            gs, bufs = _exchange_wait_call("grads_rs_sibling_wait_b", _sibling_copies, self.step1, after)
            self.ps, pbs = chip_sums(gs, bufs, GROUP_B)
            self.step2 = _exchange_start_call("grads_rs_chips_start_b", _chips_copies, pbs, chips_shapes(pbs), 3 * len(pbs))
            return self.step2[4]

        def behind_attention(self, after):
            _, lands = _exchange_wait_call("grads_rs_chips_wait_b", _chips_copies, self.step2, after)
            self.red = totals(self.ps, lands, GROUP_B, "b")

    full = full_weights(_gather_list_call(my_shards(GROUP_A), "a"), GROUP_A)
    cw_rows = 40
    cw_all = _exchange8_call(_pad_rows(conv_w[0].reshape(-1), cw_rows), False, "conv_w_all_gather")
    ex = StepExchanges(cw_all)
    cw_all = cw_all[0::2].reshape(4, cw_rows * LANES)[:, :3 * F2 // 4].reshape(4, 3, F2 // 4)
    full["conv_w"] = cw_all.transpose(1, 0, 2).reshape(3, F2)
    small = {n: args[n].reshape(1, d) for n, d in SMALL}
    small["attn_norm_w"] = small["attn_norm_w"] + ex.token()

    loss, gx, gw, gs = _local_step(x[0], positions[0], loss_target[0], full, small, ex)

    ga = by_owner(gw, GROUP_A)
    bufs = _exchange_call("grads_rs_sibling_a", _sibling_copies, ga, sibling_shapes(ga), 4 * len(ga))
    ps, pbs = chip_sums(ga, bufs, GROUP_A)
    lands = _exchange_call("grads_rs_chips_a", _chips_copies, pbs, chips_shapes(pbs), 3 * len(pbs))
    halves = {**ex.red, **totals(ps, lands, GROUP_A, "a")}

    vec = jnp.concatenate([gs[n].reshape(-1) for n, _ in SMALL] + [gw["conv_w"].reshape(-1), loss.reshape(-1)])
    tot = _exchange8_call(_pad_rows(vec, 216), True, "small_all_reduce").reshape(-1)
    red, off = {}, 0
    for n, d in SMALL:
        red[n] = tot[off:off + d].reshape(1, d)
        off += d
    red["conv_w"] = lax.dynamic_slice(tot[off:off + 3 * F2].reshape(3, F2), (0, sm * (F2 // 4)), (3, F2 // 4))
    loss_tot = tot[off + 3 * F2]

    grads, deltas, new_m, new_v = [], [], [], []
    for n in WEIGHT_ORDER:
        shape = args[n].shape
        two_d = (1, shape[0]) if len(shape) == 1 else shape[-2:]
        wmv = [args[k + n].reshape(two_d) for k in ("", "m_", "v_")]
        if n in halves:
            g, d, nm, nv = _adamw_halves_call(wmv[0], *halves[n], c_arr, wmv[1], wmv[2], "adamw_" + n)
        else:
            g = red[n].reshape(two_d)
            d, nm, nv = _adamw_call(wmv[0], g, wmv[1], wmv[2], "adamw_" + n)
        grads.append(g.reshape(shape))
        deltas.append(d.reshape(shape))
        new_m.append(nm.reshape(shape))
        new_v.append(nv.reshape(shape))
    return (loss_tot, gx[None], *grads, *deltas, *new_m, *new_v)
```

```python
import functools
import math

import numpy as np
import jax
import jax.numpy as jnp
from jax import lax
from jax.experimental import pallas as pl
from jax.experimental.pallas import tpu as pltpu

F32 = jnp.float32
BF16 = jnp.bfloat16

D_MODEL = 1024
N_HEADS = 8
HEAD = 64
RET_W = N_HEADS * HEAD
MLA_W = N_HEADS * HEAD
ROPE = 32
Q_RANK = 256
KV_RANK = 128
D_FF = 2816
F2 = 2 * D_FF
IN_W = 4 * RET_W + Q_RANK + KV_RANK + ROPE
IN_EXT = 4 * RET_W + Q_RANK + KV_RANK + 128
KPE_LO = 64
ROPE_BASE = 10000.0
EPS = 1e-6
RET_CHUNK = 128
SM_SCALE = (HEAD + ROPE) ** -0.5
LOG2E = math.log2(math.e)
LN2 = math.log(2.0)
NEG = -1e30
LANES = 128
VMEM_LIMIT = 56 * 1024 * 1024

ADAM_LR = 0.001
ADAM_B1 = 0.9
ADAM_B2 = 0.999
ADAM_EPS = 1e-08
ADAM_WD = 0.01
ADAM_STEP = 10


VMEM_LIMIT_MLP_BWD = 60 * 1024 * 1024


def _cp(*sem, vmem=VMEM_LIMIT):
    return pltpu.CompilerParams(dimension_semantics=sem, vmem_limit_bytes=vmem)


def _full(shape):
    n = len(shape)
    return pl.BlockSpec(tuple(shape), lambda *_: (0,) * n)


def _row(ts, c):
    return pl.BlockSpec((ts, c), lambda i: (i, 0))


def _hrow(h, ts, c):
    return pl.BlockSpec((h, ts, c), lambda i: (0, i, 0))


def _dot(a, b):
    return jnp.dot(a, b, preferred_element_type=F32)


def _dot_nt(a, b):
    return lax.dot_general(a, b, (((1,), (1,)), ((), ())), preferred_element_type=F32)


def _dot_tn(a, b):
    return lax.dot_general(a, b, (((0,), (0,)), ((), ())), preferred_element_type=F32)


def _dot_hi(a, b):
    hi = a.astype(BF16)
    lo = (a - hi.astype(F32)).astype(BF16)
    bb = b.astype(BF16)
    return _dot(hi, bb) + _dot(lo, bb)


def _rot_half(x, half):
    w = x.shape[-1]
    lane = lax.broadcasted_iota(jnp.int32, x.shape, x.ndim - 1)
    first = (lane % (2 * half)) < half
    return jnp.where(first, -pltpu.roll(x, w - half, x.ndim - 1), pltpu.roll(x, half, x.ndim - 1))


def _rope(x, cos, sin, half):
    return x * cos + _rot_half(x, half) * sin


def _unrope(dy, cos, sin, half):
    return dy * cos - _rot_half(dy, half) * sin


def _sigmoid(g):
    return 0.5 * jnp.tanh(0.5 * g) + 0.5


def _silu(g):
    return g * _sigmoid(g)


def _rstd(x):
    return lax.rsqrt(jnp.mean(x * x, axis=-1, keepdims=True) + EPS)


def _rope_tables(positions):
    pos = positions.astype(F32)[:, None]
    s = pos.shape[0]
    inv = ROPE_BASE ** (-jnp.arange(0, HEAD, 2, dtype=F32) / HEAD)
    ang = pos * inv
    c, sn = jnp.cos(ang), jnp.sin(ang)
    cos_r = jnp.tile(jnp.concatenate([c, c], -1), (1, 2))
    sin_r = jnp.tile(jnp.concatenate([sn, sn], -1), (1, 2))
    inv = ROPE_BASE ** (-jnp.arange(0, ROPE, 2, dtype=F32) / ROPE)
    ang = pos * inv
    c, sn = jnp.cos(ang), jnp.sin(ang)
    one, zero = jnp.ones((s, KPE_LO), F32), jnp.zeros((s, KPE_LO), F32)
    cos_m = jnp.concatenate([one, c, c, one[:, :LANES - KPE_LO - ROPE]], -1)
    sin_m = jnp.concatenate([zero, sn, sn, zero[:, :LANES - KPE_LO - ROPE]], -1)
    return cos_r, sin_r, cos_m, sin_m


def _ret_consts():
    c = RET_CHUNK
    lg = np.log1p(-np.power(2.0, -5.0 - np.arange(N_HEADS, dtype=np.float64)))
    idx = np.arange(c, dtype=np.float64)
    diff = idx[:, None] - idx[None, :]
    lane_head = np.arange(LANES) // HEAD
    dmask = np.zeros((4, 2, c, c))
    zeta = np.zeros((4, c, LANES))
    xi = np.zeros((4, c, LANES))
    cd = np.zeros((4, LANES, LANES))
    bd = (lane_head[:, None] == lane_head[None, :]).astype(np.float64)
    for j in range(4):
        for hh in range(2):
            dmask[j, hh] = np.where(diff >= 0, np.exp(lg[2 * j + hh] * np.maximum(diff, 0.0)), 0.0)
        lgl = lg[2 * j + lane_head]
        zeta[j] = np.exp(lgl[None, :] * (c - 1.0 - idx[:, None]))
        xi[j] = np.exp(lgl[None, :] * (idx[:, None] + 1.0))
        cd[j] = np.exp(lgl * c)[:, None] * bd
    f = lambda a: jnp.asarray(a, F32)
    side = lambda d: np.concatenate([d[:, 0], d[:, 1]], axis=-1)
    return dict(dmask=f(side(dmask)), dmask_t=f(side(np.swapaxes(dmask, 2, 3))), zeta=f(zeta), xi=f(xi), cd=f(cd), bd=f(bd))


def _f1_call(x, anw, win, cos_r, sin_r, cos_m, sin_m, ts):
    s = x.shape[0]

    def body(x_ref, anw_ref, w_ref, cr_ref, sr_ref, cm_ref, sm_ref,
             q_ref, k_ref, v_ref, g_ref, cq_ref, ckv_ref, kpe_ref, r_ref):
        xv = x_ref[...]
        r = _rstd(xv)
        r_ref[...] = r
        h = (xv * r * anw_ref[...]).astype(BF16)
        cr, sr = cr_ref[...], sr_ref[...]
        qk = _dot(h, w_ref[:, 0:2 * RET_W])
        for j in range(4):
            sl = slice(j * LANES, (j + 1) * LANES)
            q_ref[:, sl] = _rope(qk[:, sl], cr, sr, HEAD // 2).astype(BF16)
            kk = qk[:, RET_W + j * LANES:RET_W + (j + 1) * LANES]
            k_ref[:, sl] = (_rope(kk, cr, sr, HEAD // 2) * (HEAD ** -0.5)).astype(BF16)
        v_ref[...] = _dot(h, w_ref[:, 2 * RET_W:3 * RET_W]).astype(BF16)
        g_ref[...] = _dot(h, w_ref[:, 3 * RET_W:4 * RET_W])
        o = 4 * RET_W
        cq_ref[...] = _dot(h, w_ref[:, o:o + Q_RANK])
        ckv_ref[...] = _dot(h, w_ref[:, o + Q_RANK:o + Q_RANK + KV_RANK])
        kp = _dot(h, w_ref[:, o + Q_RANK + KV_RANK:IN_EXT])
        kpe_ref[...] = _rope(kp, cm_ref[...], sm_ref[...], ROPE // 2)

    sd = jax.ShapeDtypeStruct
    return pl.pallas_call(
        body, name="f1_in_proj", grid=(s // ts,),
        in_specs=[_row(ts, D_MODEL), _full((1, D_MODEL)), _full((D_MODEL, IN_EXT)),
                  _row(ts, LANES), _row(ts, LANES), _row(ts, LANES), _row(ts, LANES)],
        out_specs=[_row(ts, RET_W), _row(ts, RET_W), _row(ts, RET_W), _row(ts, RET_W),
                   _row(ts, Q_RANK), _row(ts, KV_RANK), _row(ts, LANES), _row(ts, 1)],
        out_shape=[sd((s, RET_W), BF16), sd((s, RET_W), BF16), sd((s, RET_W), BF16), sd((s, RET_W), F32),
                   sd((s, Q_RANK), F32), sd((s, KV_RANK), F32), sd((s, LANES), F32), sd((s, 1), F32)],
        compiler_params=_cp("parallel"),
    )(x, anw, win, cos_r, sin_r, cos_m, sin_m)


def _stack_heads(a):
    lo = lax.broadcasted_iota(jnp.int32, a.shape, 1) < HEAD
    zero = jnp.zeros_like(a)
    return jnp.concatenate([jnp.where(lo, a, zero), jnp.where(lo, zero, a)], axis=0)


def _pair_product(a, b2, decay2, w2):
    return _dot((_dot_nt(a, b2) * decay2).astype(BF16), w2)


RET_SLABS = 2


def _ret_specs(tr, tile_of):
    c, ns = RET_CHUNK, RET_SLABS
    return dict(
        slab=pl.BlockSpec((tr, ns * LANES), lambda j, i: (tile_of(i), j)),
        tab=pl.BlockSpec((tr, LANES), lambda j, i: (tile_of(i), 0)),
        vec=pl.BlockSpec((1, ns * LANES), lambda j, i: (0, j)),
        dmask=pl.BlockSpec((ns, c, 2 * c), lambda j, i: (j, 0, 0)),
        rows=pl.BlockSpec((ns, c, LANES), lambda j, i: (j, 0, 0)),
        state=pl.BlockSpec((ns, LANES, LANES), lambda j, i: (j, 0, 0)),
        bd=pl.BlockSpec((LANES, LANES), lambda j, i: (0, 0)))


def _ret_states(a_ref, b_ref, scale_ref, cd_ref, bd, st_ref, chunks, lanes, reverse):
    nc = len(chunks)
    contrib = [[_dot_tn((a_ref[rows, ln].astype(F32) * scale_ref[sl]).astype(BF16), b_ref[rows, ln]) * bd for rows in chunks]
               for sl, ln in enumerate(lanes)]
    states = []
    for sl in range(len(lanes)):
        st, seen = st_ref[sl], [None] * nc
        for ci in (reversed(range(nc)) if reverse else range(nc)):
            seen[ci] = st.astype(BF16)
            st = st * cd_ref[sl] + contrib[sl][ci]
        st_ref[sl] = st
        states.append(seen)
    return states


def _ret_fwd_call(q, k, v, g, gnw, rc, tr):
    s = q.shape[0]
    c = RET_CHUNK
    nc = tr // c
    ns = RET_SLABS

    def body(q_ref, k_ref, v_ref, g_ref, gnw_ref, dm_ref, zeta_ref, xi_ref, cd_ref, bd_ref, o_ref, y_ref, st_ref):
        @pl.when(pl.program_id(1) == 0)
        def _():
            st_ref[...] = jnp.zeros_like(st_ref)

        bd = bd_ref[...]
        chunks = [slice(ci * c, (ci + 1) * c) for ci in range(nc)]
        lanes = [slice(sl * LANES, (sl + 1) * LANES) for sl in range(ns)]
        states = _ret_states(k_ref, v_ref, zeta_ref, cd_ref, bd, st_ref, chunks, lanes, False)
        for ci, rows in enumerate(chunks):
            for sl, ln in enumerate(lanes):
                qc = q_ref[rows, ln]
                o_ref[rows, ln] = (_dot(qc, states[sl][ci]) * xi_ref[sl]
                                   + _pair_product(qc, _stack_heads(k_ref[rows, ln]), dm_ref[sl], _stack_heads(v_ref[rows, ln])))
        avg = bd * (1.0 / HEAD)
        for ln in lanes:
            o = o_ref[:, ln]
            ctr = o - _dot_hi(o, avg)
            var = _dot_hi(ctr * ctr, avg)
            y_ref[:, ln] = (_silu(g_ref[:, ln]) * (ctr * lax.rsqrt(var + EPS) * gnw_ref[:, ln])).astype(BF16)

    specs = _ret_specs(tr, lambda i: i)
    sd = jax.ShapeDtypeStruct
    return pl.pallas_call(
        body, name="ret_fwd", grid=(4 // ns, s // tr),
        in_specs=[specs["slab"]] * 4 + [specs["vec"], specs["dmask"], specs["rows"], specs["rows"], specs["state"], specs["bd"]],
        out_specs=[specs["slab"]] * 2,
        out_shape=[sd((s, RET_W), F32), sd((s, RET_W), BF16)],
        scratch_shapes=[pltpu.VMEM((ns, LANES, LANES), F32)],
        compiler_params=_cp("parallel", "arbitrary"),
    )(q, k, v, g, gnw, rc["dmask"], rc["zeta"], rc["xi"], rc["cd"], rc["bd"])


QK_AUX = HEAD + ROPE
V_AUX = HEAD


def _lane_pair(shape, lo, a, b, rest):
    lane = lax.broadcasted_iota(jnp.int32, shape, len(shape) - 1)
    return jnp.where(lane == lo, a, jnp.where(lane == lo + 1, b, rest))


def _hi_lo(v):
    hi = v.astype(BF16).astype(F32)
    return hi, v - hi


def _mla_pre_call(cq, ckv, kpe, qnw, kvnw, wq, wk, wv, cos_m, sin_m, ts):
    s = cq.shape[0]

    def body(cq_ref, ckv_ref, kpe_ref, qnw_ref, kvnw_ref, wq_ref, wk_ref, wv_ref, cm_ref, sm_ref, q_ref, k_ref, v_ref):
        cqv, ckvv = cq_ref[...], ckv_ref[...]
        cqn = (cqv * _rstd(cqv) * qnw_ref[...]).astype(BF16)
        ckvn = (ckvv * _rstd(ckvv) * kvnw_ref[...]).astype(BF16)
        cm, sm = cm_ref[...], sm_ref[...]
        kp = _lane_pair((ts, LANES), QK_AUX, -1.0, -1.0, kpe_ref[...])
        for h in range(N_HEADS):
            qh = _rope(_dot(cqn, wq_ref[h]), cm, sm, ROPE // 2)
            q_ref[h] = (qh * (SM_SCALE * LOG2E)).astype(BF16)
            k_ref[h] = (_dot(ckvn, wk_ref[h]) + kp).astype(BF16)
            v_ref[h] = _lane_pair((ts, LANES), V_AUX, 1.0, 1.0, _dot(ckvn, wv_ref[h])).astype(BF16)

    sd = jax.ShapeDtypeStruct
    hm = sd((N_HEADS, s, LANES), BF16)
    return pl.pallas_call(
        body, name="mla_pre", grid=(s // ts,),
        in_specs=[_row(ts, Q_RANK), _row(ts, KV_RANK), _row(ts, LANES), _full((1, Q_RANK)), _full((1, KV_RANK)),
                  _full((N_HEADS, Q_RANK, LANES)), _full((N_HEADS, KV_RANK, LANES)), _full((N_HEADS, KV_RANK, LANES)),
                  _row(ts, LANES), _row(ts, LANES)],
        out_specs=[_hrow(N_HEADS, ts, LANES)] * 3,
        out_shape=[hm, hm, hm],
        compiler_params=_cp("parallel"),
    )(cq, ckv, kpe, qnw, kvnw, wq, wk, wv, cos_m, sin_m)


def _flash_fwd_call(q, k, v, tb):
    s = q.shape[1]
    nb = s // tb
    pairs = [(a, b) for a in range(nb) for b in range(a + 1)]
    qi_of, ki_of = (jnp.asarray(np.array(col, np.int32)) for col in zip(*pairs))

    def body(qi_ref, ki_ref, q_ref, k_ref, v_ref, o_ref, qb_ref, m_ref, acc_ref):
        qi, ki = qi_ref[pl.program_id(0)], ki_ref[pl.program_id(0)]

        @pl.when(ki == 0)
        def _():
            m_ref[...] = jnp.full_like(m_ref, NEG)
            acc_ref[...] = jnp.zeros_like(acc_ref)

        def step(masked):
            if masked:
                keep = lax.broadcasted_iota(jnp.int32, (tb, tb), 1) <= lax.broadcasted_iota(jnp.int32, (tb, tb), 0)
            def finish(h, pe, alpha):
                acc_ref[h] = acc_ref[h] * alpha + _dot(pe, v_ref[h])

            nxt, pending = _dot_nt(q_ref[0], k_ref[0]), None
            for h in range(N_HEADS):
                sc = nxt
                if h + 1 < N_HEADS:
                    nxt = _dot_nt(q_ref[h + 1], k_ref[h + 1])
                if masked:
                    sc = jnp.where(keep, sc, NEG)
                m_prev = m_ref[h]
                m_new = jnp.maximum(m_prev, jnp.max(sc, axis=1, keepdims=True))
                pe = jnp.exp2(sc - jnp.tile(m_new, (1, tb // LANES))).astype(BF16)
                m_ref[h] = m_new
                if pending is not None:
                    finish(*pending)
                pending = (h, pe, jnp.exp2(m_prev - m_new))
            finish(*pending)

        @pl.when(ki < qi)
        def _():
            step(False)

        @pl.when(ki == qi)
        def _():
            step(True)
            lane = lax.broadcasted_iota(jnp.int32, (tb, LANES), 1)
            for p in range(N_HEADS // 2):
                outs = []
                for h in (2 * p, 2 * p + 1):
                    acc = acc_ref[h]
                    l = acc[:, V_AUX:V_AUX + 1]
                    outs.append(acc * (1.0 / l))
                    hi, lo = _hi_lo(m_ref[h][:, 0:1] + jnp.log(l) * LOG2E)
                    qb_ref[h] = _lane_pair((tb, LANES), QK_AUX, hi, lo, q_ref[h].astype(F32)).astype(BF16)
                o_ref[:, p * LANES:(p + 1) * LANES] = jnp.where(lane < HEAD, outs[0], pltpu.roll(outs[1], HEAD, 1)).astype(BF16)

    sd = jax.ShapeDtypeStruct
    qspec = pl.BlockSpec((N_HEADS, tb, LANES), lambda p, qi_ref, ki_ref: (0, qi_ref[p], 0))
    kspec = pl.BlockSpec((N_HEADS, tb, LANES), lambda p, qi_ref, ki_ref: (0, ki_ref[p], 0))
    return pl.pallas_call(
        body, name="mla_flash_fwd",
        grid_spec=pltpu.PrefetchScalarGridSpec(
            num_scalar_prefetch=2, grid=(len(pairs),),
            in_specs=[qspec, kspec, kspec],
            out_specs=[pl.BlockSpec((tb, MLA_W), lambda p, qi_ref, ki_ref: (qi_ref[p], 0)), qspec],
            scratch_shapes=[pltpu.VMEM((N_HEADS, tb, LANES), F32), pltpu.VMEM((N_HEADS, tb, LANES), F32)]),
        out_shape=[sd((s, MLA_W), BF16), sd((N_HEADS, s, LANES), BF16)],
        compiler_params=_cp("arbitrary"),
    )(qi_of, ki_of, q, k, v)


def _out_proj_call(x, yret, ymla, wout, ts):
    s = x.shape[0]

    def body(x_ref, yr_ref, ym_ref, w_ref, x1_ref, r_ref):
        x1 = x_ref[...] + _dot(yr_ref[...], w_ref[0:RET_W, :]) + _dot(ym_ref[...], w_ref[RET_W:, :])
        x1_ref[...] = x1
        r_ref[...] = _rstd(x1)

    sd = jax.ShapeDtypeStruct
    return pl.pallas_call(
        body, name="out_proj", grid=(s // ts,),
        in_specs=[_row(ts, D_MODEL), _row(ts, RET_W), _row(ts, MLA_W), _full((D_MODEL, D_MODEL))],
        out_specs=[_row(ts, D_MODEL), _row(ts, 1)],
        out_shape=[sd((s, D_MODEL), F32), sd((s, 1), F32)],
        compiler_params=_cp("parallel"),
    )(x, yret, ymla, wout)


W_UP_SHARD = F2 // 4


def _ffn_fwd_call(x1, r2, fnw, wup4, cw, cb, wdown, ts):
    s = x1.shape[0]
    wsh = W_UP_SHARD

    def body(x_ref, r_ref, fnw_ref, wup_ref, cw_ref, cb_ref, wd_ref, u_ref, uc_ref, x2_ref, carry_ref):
        _zero_first(pl.program_id(0) == 0, carry_ref)
        xv = x_ref[...]
        h = (xv * r_ref[...] * fnw_ref[...]).astype(BF16)
        conv = []
        for j in range(4):
            cols = slice(j * wsh, (j + 1) * wsh)
            ub = _dot(h, wup_ref[j]).astype(BF16)
            u_ref[:, cols] = ub
            u = ub.astype(F32)
            u1, u2 = _shifted(u, carry_ref[:, cols])
            w = cw_ref[:, cols]
            cb16 = (cb_ref[:, cols] + w[0:1, :] * u2 + w[1:2, :] * u1 + w[2:3, :] * u).astype(BF16)
            uc_ref[:, cols] = cb16
            conv.append(cb16.astype(F32))
            carry_ref[:, cols] = u[ts - 8:, :]
        acc = xv
        for j in range(2):
            a = (_silu(conv[j]) * conv[j + 2]).astype(BF16)
            acc = acc + _dot(a, wd_ref[j * wsh:(j + 1) * wsh, :])
        x2_ref[...] = acc

    sd = jax.ShapeDtypeStruct
    return pl.pallas_call(
        body, name="ffn_fwd", grid=(s // ts,),
        in_specs=[_row(ts, D_MODEL), _row(ts, 1), _full((1, D_MODEL)), _full((4, D_MODEL, wsh)),
                  _full((3, F2)), _full((1, F2)), _full((D_FF, D_MODEL))],
        out_specs=[_row(ts, F2), _row(ts, F2), _row(ts, D_MODEL)],
        out_shape=[sd((s, F2), BF16), sd((s, F2), BF16), sd((s, D_MODEL), F32)],
        scratch_shapes=[pltpu.VMEM((8, F2), F32)],
        compiler_params=_cp("arbitrary"),
    )(x1, r2, fnw, wup4, cw, cb, wdown)


def _shifted(u, hal):
    row = lax.broadcasted_iota(jnp.int32, hal.shape, 0)
    r1, r2 = pltpu.roll(u, 1, 0), pltpu.roll(u, 2, 0)
    top1 = jnp.where(row == 0, hal[7:8, :], r1[0:8, :])
    top2 = jnp.where(row == 0, hal[6:7, :], jnp.where(row == 1, hal[7:8, :], r2[0:8, :]))
    return jnp.concatenate([top1, r1[8:, :]], axis=0), jnp.concatenate([top2, r2[8:, :]], axis=0)


def _prep_weights(w):
    win = w["w_in"]
    pad = lambda n: jnp.zeros((D_MODEL, n), win.dtype)
    win_ext = jnp.concatenate([win[:, :IN_W - ROPE], pad(KPE_LO), win[:, IN_W - ROPE:], pad(LANES - KPE_LO - ROPE)], -1)
    wuq = w["w_uq"].reshape(Q_RANK, N_HEADS, HEAD + ROPE)
    wq = jnp.concatenate([wuq, jnp.zeros((Q_RANK, N_HEADS, LANES - HEAD - ROPE), wuq.dtype)], -1).transpose(1, 0, 2)
    wukv = w["w_ukv"].reshape(KV_RANK, N_HEADS, 2 * HEAD)
    zk = jnp.zeros((KV_RANK, N_HEADS, HEAD), wukv.dtype)
    wk = jnp.concatenate([wukv[:, :, :HEAD], zk], -1).transpose(1, 0, 2)
    wv = jnp.concatenate([wukv[:, :, HEAD:], zk], -1).transpose(1, 0, 2)
    c = lambda a: a.astype(BF16)
    return dict(win=c(win_ext), wq=c(wq), wk=c(wk), wv=c(wv), wout=c(w["w_out"]))


def _prep_mlp_weights(w):
    wup = w["w_up"]
    if wup.ndim == 2:
        wup = wup.reshape(D_MODEL, 4, W_UP_SHARD).transpose(1, 0, 2)
    return dict(wup=wup.astype(BF16), wdown=w["w_down"].astype(BF16))


def _tiles(s):
    return dict(ts=min(s, 512), tr=min(s, 1024), tb=min(s, 512), t2=min(s, 256))


class _Exchanges:
    def __init__(self, w):
        self.w = w

    def mlp_weights(self, after):
        return self.w

    def mlp_grads(self, gw):
        pass

    def behind_out_bwd(self, after):
        pass

    def behind_attention(self, after):
        pass


def _forward(x, positions, w, small, ex):
    s = x.shape[0]
    t = _tiles(s)
    pw = _prep_weights(w)
    cos_r, sin_r, cos_m, sin_m = _rope_tables(positions)
    rc = _ret_consts()
    q, k, v, g, cq, ckv, kpe, r1 = _f1_call(x, small["attn_norm_w"], pw["win"], cos_r, sin_r, cos_m, sin_m, t["ts"])
    o_ret, y_ret = _ret_fwd_call(q, k, v, g, small["ret_gn_w"], rc, t["tr"])
    mq, mk, mv = _mla_pre_call(cq, ckv, kpe, small["mla_q_norm_w"], small["mla_kv_norm_w"],
                               pw["wq"], pw["wk"], pw["wv"], cos_m, sin_m, t["ts"])
    y_mla, mqb = _flash_fwd_call(mq, mk, mv, t["tb"])
    x1, r2 = _out_proj_call(x, y_ret, y_mla, pw["wout"], t["ts"])
    pw.update(_prep_mlp_weights(ex.mlp_weights(r2)))
    u, uc, x2 = _ffn_fwd_call(x1, r2, small["ffn_norm_w"], pw["wup"], w["conv_w"], small["conv_b"], pw["wdown"], t["ts"])
    return dict(pw=pw, tabs=(cos_r, sin_r, cos_m, sin_m), rc=rc, q=q, k=k, v=v, g=g, cq=cq, ckv=ckv, kpe=kpe, r1=r1,
                o_ret=o_ret, y_ret=y_ret, mqb=mqb, mk=mk, mv=mv, y_mla=y_mla, x1=x1, r2=r2, u=u, uc=uc, x2=x2)


def _norm_bwd(dh, xh, r, nw):
    dxn = dh * nw
    return r * (dxn - xh * jnp.mean(dxn * xh, axis=-1, keepdims=True))


def _ordered_after(body, order):
    if order is None:
        return body, [], []
    return (lambda order_ref, *refs: body(*refs)), [pl.BlockSpec(memory_space=pl.ANY)], [order]


def _zero_first(first, *refs):
    @pl.when(first)
    def _():
        for ref in refs:
            ref[...] = jnp.zeros_like(ref)


def _colsum(v):
    return jnp.sum(v, axis=0, keepdims=True)


def _dsilu(g, sg):
    return sg * (1.0 + g * (1.0 - sg))


def _loss_call(x2, tgt, fw, ts):
    s = x2.shape[0]

    def body(x_ref, t_ref, fw_ref, dx_ref, loss_ref, gfw_ref):
        _zero_first(pl.program_id(0) == 0, loss_ref, gfw_ref)
        xv = x_ref[...]
        r = _rstd(xv)
        xh = xv * r
        fwv = fw_ref[...]
        e = xh * fwv - t_ref[...]
        loss_ref[...] += (0.5 / D_MODEL) * _colsum(jnp.sum(e * e, axis=1, keepdims=True))
        dy = e * (1.0 / D_MODEL)
        gfw_ref[...] += _colsum(dy * xh)
        dx_ref[...] = _norm_bwd(dy, xh, r, fwv)

    sd = jax.ShapeDtypeStruct
    return pl.pallas_call(
        body, name="loss_bwd", grid=(s // ts,),
        in_specs=[_row(ts, D_MODEL), _row(ts, D_MODEL), _full((1, D_MODEL))],
        out_specs=[_row(ts, D_MODEL), _full((1, 1)), _full((1, D_MODEL))],
        out_shape=[sd((s, D_MODEL), F32), sd((1, 1), F32), sd((1, D_MODEL), F32)],
        compiler_params=_cp("arbitrary"),
    )(x2, tgt, fw)


def _ffn_bwd_call(dx2, u, uc, cw, wdown, wup4, x1, r2, fnw, ts):
    s = dx2.shape[0]
    nt = s // ts
    wsh = W_UP_SHARD
    rev = lambda i: nt - 1 - i

    def body(dx2_ref, u_ref, uc_ref, cw_ref, wd_ref, wup_ref, x_ref, r_ref, fnw_ref,
             du_ref, dx1_ref, dcw_ref, dcb_ref, dfnw_ref, dwd_hbm, carry_ref, dwd_ref, sem):
        i = pl.program_id(0)
        _zero_first(i == 0, carry_ref, dwd_ref, dcw_ref, dcb_ref, dfnw_ref)
        dxb = dx2_ref[...].astype(BF16)
        dh = jnp.zeros((ts, D_MODEL), F32)
        for j in range(2):
            gcols = slice(j * wsh, (j + 1) * wsh)
            vcols = slice(D_FF + j * wsh, D_FF + (j + 1) * wsh)
            gate, val = uc_ref[:, gcols].astype(F32), uc_ref[:, vcols].astype(F32)
            da = _dot_nt(dxb, wd_ref[gcols, :])
            sg = _sigmoid(gate)
            sl = gate * sg
            dwd_ref[gcols, :] += _dot_tn((sl * val).astype(BF16), dxb)
            for d, cols, shard in ((da * val * _dsilu(gate, sg), gcols, j), (da * sl, vcols, 2 + j)):
                d1, d2 = _shifted_up(d, carry_ref[:, cols])
                uv = u_ref[:, cols].astype(F32)
                for t, dt in enumerate((d2, d1, d)):
                    dcw_ref[t:t + 1, cols] += _colsum(dt * uv)
                dcb_ref[:, cols] += _colsum(d)
                w = cw_ref[:, cols]
                du = (w[2:3, :] * d + w[1:2, :] * d1 + w[0:1, :] * d2).astype(BF16)
                du_ref[:, cols] = du
                dh = dh + _dot_nt(du, wup_ref[shard])
                carry_ref[:, cols] = d[0:8, :]
        r = r_ref[...]
        xh = x_ref[...] * r
        dfnw_ref[...] += _colsum(dh * xh)
        dx1_ref[...] = dx2_ref[...] + _norm_bwd(dh, xh, r, fnw_ref[...])

        @pl.when(i == nt - 1)
        def _():
            cp = pltpu.make_async_copy(dwd_ref, dwd_hbm, sem)
            cp.start()
            cp.wait()

    sd = jax.ShapeDtypeStruct
    row = lambda c: pl.BlockSpec((ts, c), lambda i: (rev(i), 0))
    once = lambda shape: pl.BlockSpec(shape, lambda i: (0,) * len(shape), pipeline_mode=pl.Buffered(1))
    return pl.pallas_call(
        body, name="ffn_bwd", grid=(nt,),
        in_specs=[row(D_MODEL), row(F2), row(F2), once((3, F2)), once((D_FF, D_MODEL)), once((4, D_MODEL, wsh)),
                  row(D_MODEL), row(1), once((1, D_MODEL))],
        out_specs=[row(F2), row(D_MODEL), _full((3, F2)), _full((1, F2)), _full((1, D_MODEL)), pl.BlockSpec(memory_space=pl.ANY)],
        out_shape=[sd((s, F2), BF16), sd((s, D_MODEL), F32), sd((3, F2), F32), sd((1, F2), F32), sd((1, D_MODEL), F32),
                   sd((D_FF, D_MODEL), F32)],
        scratch_shapes=[pltpu.VMEM((8, F2), F32), pltpu.VMEM((D_FF, D_MODEL), F32), pltpu.SemaphoreType.DMA],
        compiler_params=_cp("arbitrary", vmem=VMEM_LIMIT_MLP_BWD),
    )(dx2, u, uc, cw, wdown, wup4, x1, r2, fnw)


def _shifted_up(d, hal):
    n = d.shape[0]
    row = lax.broadcasted_iota(jnp.int32, hal.shape, 0)
    r1, r2 = pltpu.roll(d, n - 1, 0), pltpu.roll(d, n - 2, 0)
    end1 = jnp.where(row == 7, hal[0:1, :], r1[n - 8:, :])
    end2 = jnp.where(row == 6, hal[0:1, :], jnp.where(row == 7, hal[1:2, :], r2[n - 8:, :]))
    return jnp.concatenate([r1[:n - 8, :], end1], axis=0), jnp.concatenate([r2[:n - 8, :], end2], axis=0)


def _dw_norm_call(x, r, nw, b, ts, tn, name):
    s, n = b.shape
    k = x.shape[1]

    def body(x_ref, r_ref, nw_ref, b_ref, dw_ref):
        _zero_first(pl.program_id(1) == 0, dw_ref)
        h = (x_ref[...] * r_ref[...] * nw_ref[...]).astype(BF16)
        dw_ref[...] += _dot_tn(h, b_ref[...])

    return pl.pallas_call(
        body, name=name, grid=(n // tn, s // ts),
        in_specs=[pl.BlockSpec((ts, k), lambda j, i: (i, 0)), pl.BlockSpec((ts, 1), lambda j, i: (i, 0)),
                  pl.BlockSpec((1, k), lambda j, i: (0, 0)), pl.BlockSpec((ts, tn), lambda j, i: (i, j))],
        out_specs=pl.BlockSpec((None, k, tn), lambda j, i: (j, 0, 0)),
        out_shape=jax.ShapeDtypeStruct((n // tn, k, tn), F32),
        compiler_params=_cp("parallel", "arbitrary"),
    )(x, r, nw, b)


def _out_bwd_call(dx1, yret, ymla, wout, ts, order=None):
    s = dx1.shape[0]

    def body(dx_ref, yr_ref, ym_ref, w_ref, dyr_ref, do_ref, dwo_ref):
        _zero_first(pl.program_id(0) == 0, dwo_ref)
        dxb = dx_ref[...].astype(BF16)
        dmix = _dot_nt(dxb, w_ref[...])
        dyr_ref[...] = dmix[:, :RET_W]
        ym = ym_ref[...]
        lane = lax.broadcasted_iota(jnp.int32, (ts, LANES), 1)
        for p in range(N_HEADS // 2):
            dom = dmix[:, RET_W + p * LANES:RET_W + (p + 1) * LANES]
            prod = dom * ym[:, p * LANES:(p + 1) * LANES].astype(F32)
            for hh in range(2):
                mine = (lane >= HEAD) if hh else (lane < HEAD)
                hi, lo = _hi_lo(jnp.sum(jnp.where(mine, prod, 0.0), axis=1, keepdims=True))
                base = jnp.where(lane < HEAD, pltpu.roll(dom, HEAD, 1) if hh else dom, 0.0)
                do_ref[2 * p + hh] = _lane_pair((ts, LANES), V_AUX, -hi, -lo, base).astype(BF16)
        dwo_ref[0:RET_W, :] += _dot_tn(yr_ref[...], dxb)
        dwo_ref[RET_W:, :] += _dot_tn(ym, dxb)

    sd = jax.ShapeDtypeStruct
    body, first_specs, first = _ordered_after(body, order)
    return pl.pallas_call(
        body, name="out_proj_bwd", grid=(s // ts,),
        in_specs=first_specs + [_row(ts, D_MODEL), _row(ts, RET_W), _row(ts, MLA_W), _full((D_MODEL, D_MODEL))],
        out_specs=[_row(ts, RET_W), _hrow(N_HEADS, ts, LANES), _full((D_MODEL, D_MODEL))],
        out_shape=[sd((s, RET_W), F32), sd((N_HEADS, s, LANES), BF16), sd((D_MODEL, D_MODEL), F32)],
        compiler_params=_cp("arbitrary"),
    )(*first, dx1, yret, ymla, wout)


def _ret_bwd_q_call(q, k, v, o, g, dy, gnw, rc, cos_r, sin_r, tr):
    s = q.shape[0]
    c = RET_CHUNK
    nc = tr // c
    ns = RET_SLABS

    def body(q_ref, k_ref, v_ref, o_ref, g_ref, dy_ref, gnw_ref, dm_ref, zeta_ref, xi_ref, cd_ref, bd_ref, cr_ref, sr_ref,
             dq_ref, dg_ref, do_ref, dgnw_ref, st_ref):
        _zero_first(pl.program_id(1) == 0, st_ref, dgnw_ref)
        bd = bd_ref[...]
        avg = bd * (1.0 / HEAD)
        chunks = [slice(ci * c, (ci + 1) * c) for ci in range(nc)]
        lanes = [slice(sl * LANES, (sl + 1) * LANES) for sl in range(ns)]
        dov = []
        for ln in lanes:
            ov = o_ref[:, ln]
            ctr = ov - _dot_hi(ov, avg)
            rs = lax.rsqrt(_dot_hi(ctr * ctr, avg) + EPS)
            oh = ctr * rs
            gg, dyv, gnw_v = g_ref[:, ln], dy_ref[:, ln], gnw_ref[:, ln]
            sg = _sigmoid(gg)
            sl = gg * sg
            dg_ref[:, ln] = (dyv * oh * gnw_v * _dsilu(gg, sg)).astype(BF16)
            dgnw_ref[:, ln] += _colsum(dyv * sl * oh)
            doh = dyv * sl * gnw_v
            dov.append((rs * (doh - _dot_hi(doh, avg) - oh * _dot_hi(doh * oh, avg))).astype(BF16))
            do_ref[:, ln] = dov[-1]
        states = _ret_states(k_ref, v_ref, zeta_ref, cd_ref, bd, st_ref, chunks, lanes, False)
        for ci, rows in enumerate(chunks):
            for sl, ln in enumerate(lanes):
                doc = dov[sl][rows, :]
                dq = (_dot_nt(doc, states[sl][ci]) * xi_ref[sl]
                      + _pair_product(doc, _stack_heads(v_ref[rows, ln]), dm_ref[sl], _stack_heads(k_ref[rows, ln])))
                dq_ref[rows, ln] = _unrope(dq, cr_ref[rows, :], sr_ref[rows, :], HEAD // 2).astype(BF16)

    specs = _ret_specs(tr, lambda i: i)
    sd = jax.ShapeDtypeStruct
    return pl.pallas_call(
        body, name="ret_bwd_q", grid=(4 // ns, s // tr),
        in_specs=[specs["slab"]] * 6 + [specs["vec"], specs["dmask"], specs["rows"], specs["rows"], specs["state"], specs["bd"],
                                        specs["tab"], specs["tab"]],
        out_specs=[specs["slab"]] * 3 + [specs["vec"]],
        out_shape=[sd((s, RET_W), BF16), sd((s, RET_W), BF16), sd((s, RET_W), BF16), sd((1, RET_W), F32)],
        scratch_shapes=[pltpu.VMEM((ns, LANES, LANES), F32)],
        compiler_params=_cp("parallel", "arbitrary"),
    )(q, k, v, o, g, dy, gnw, rc["dmask"], rc["zeta"], rc["xi"], rc["cd"], rc["bd"], cos_r, sin_r)


def _ret_bwd_kv_call(q, k, v, do, rc, cos_r, sin_r, tr):
    s = q.shape[0]
    c = RET_CHUNK
    nc = tr // c
    nt = s // tr
    ns = RET_SLABS

    def body(q_ref, k_ref, v_ref, do_ref, dm_ref, zeta_ref, xi_ref, cd_ref, bd_ref, cr_ref, sr_ref, dk_ref, dv_ref, gs_ref):
        _zero_first(pl.program_id(1) == 0, gs_ref)
        bd = bd_ref[...]
        chunks = [slice(ci * c, (ci + 1) * c) for ci in range(nc)]
        lanes = [slice(sl * LANES, (sl + 1) * LANES) for sl in range(ns)]
        states = _ret_states(q_ref, do_ref, xi_ref, cd_ref, bd, gs_ref, chunks, lanes, True)
        for ci, rows in enumerate(chunks):
            for sl, ln in enumerate(lanes):
                kc, vc = k_ref[rows, ln], v_ref[rows, ln]
                q2, do2 = _stack_heads(q_ref[rows, ln]), _stack_heads(do_ref[rows, ln])
                gb = states[sl][ci]
                dk = _dot_nt(vc, gb) * zeta_ref[sl] + _pair_product(vc, do2, dm_ref[sl], q2)
                dv = _dot(kc, gb) * zeta_ref[sl] + _pair_product(kc, q2, dm_ref[sl], do2)
                dk_ref[rows, ln] = (_unrope(dk, cr_ref[rows, :], sr_ref[rows, :], HEAD // 2) * (HEAD ** -0.5)).astype(BF16)
                dv_ref[rows, ln] = dv.astype(BF16)

    specs = _ret_specs(tr, lambda i: nt - 1 - i)
    sd = jax.ShapeDtypeStruct
    return pl.pallas_call(
        body, name="ret_bwd_kv", grid=(4 // ns, nt),
        in_specs=[specs["slab"]] * 4 + [specs["dmask"], specs["rows"], specs["rows"], specs["state"], specs["bd"],
                                        specs["tab"], specs["tab"]],
        out_specs=[specs["slab"]] * 2,
        out_shape=[sd((s, RET_W), BF16), sd((s, RET_W), BF16)],
        scratch_shapes=[pltpu.VMEM((ns, LANES, LANES), F32)],
        compiler_params=_cp("parallel", "arbitrary"),
    )(q, k, v, do, rc["dmask_t"], rc["zeta"], rc["xi"], rc["cd"], rc["bd"], cos_r, sin_r)


FLASH_BWD_HEADS = 8


def _flash_bwd_call(qb, k, v, do, tb, order=None):
    s = qb.shape[1]
    nb = s // tb
    hg = FLASH_BWD_HEADS
    pairs = [(a, b) for a in range(nb) for b in range(a, nb)]
    ki_of, qi_of = (jnp.asarray(np.array(col, np.int32)) for col in zip(*pairs))
    extra = [] if order is None else [order]

    def body(ki_ref, qi_ref, *refs):
        q_ref, k_ref, v_ref, do_ref, dk_ref, dv_ref, dq_hbm, dka_ref, dva_ref, dq_ref, sem = refs[len(extra):]
        g, p = pl.program_id(0), pl.program_id(1)
        ki, qi = ki_ref[p], qi_ref[p]
        _zero_first(p == 0, dq_ref)
        _zero_first(qi == ki, dka_ref, dva_ref)
        rows = pl.ds(pl.multiple_of(qi * tb, tb), tb)

        def step(masked):
            if masked:
                keep = lax.broadcasted_iota(jnp.int32, (tb, tb), 0) <= lax.broadcasted_iota(jnp.int32, (tb, tb), 1)
            for h in range(hg):
                st = _dot_nt(k_ref[h], q_ref[h])
                if masked:
                    st = jnp.where(keep, st, NEG)
                pt = jnp.exp2(st)
                dob = do_ref[h]
                dva_ref[h] += _dot(pt.astype(BF16), dob)
                dst = (pt * _dot_nt(v_ref[h], dob)).astype(BF16)
                dka_ref[h] += _dot(dst, q_ref[h])
                dq_ref[h, rows, :] += _dot_tn(dst, k_ref[h])

        @pl.when(qi > ki)
        def _():
            step(False)

        @pl.when(qi == ki)
        def _():
            step(True)

        @pl.when(qi == nb - 1)
        def _():
            dk_ref[...] = (dka_ref[...] * LN2).astype(BF16)
            dv_ref[...] = dva_ref[...].astype(BF16)

        @pl.when(p == len(pairs) - 1)
        def _():
            cp = pltpu.make_async_copy(dq_ref, dq_hbm.at[pl.ds(g * hg, hg)], sem)
            cp.start()
            cp.wait()

    kspec = pl.BlockSpec((hg, tb, LANES), lambda g, p, ki_ref, qi_ref: (g, ki_ref[p], 0))
    qspec = pl.BlockSpec((hg, tb, LANES), lambda g, p, ki_ref, qi_ref: (g, qi_ref[p], 0))
    hm = jax.ShapeDtypeStruct((N_HEADS, s, LANES), BF16)
    return pl.pallas_call(
        body, name="mla_flash_bwd",
        grid_spec=pltpu.PrefetchScalarGridSpec(
            num_scalar_prefetch=2, grid=(N_HEADS // hg, len(pairs)),
            in_specs=[ANY] * len(extra) + [qspec, kspec, kspec, qspec],
            out_specs=[kspec, kspec, ANY],
            scratch_shapes=[pltpu.VMEM((hg, tb, LANES), F32), pltpu.VMEM((hg, tb, LANES), F32),
                            pltpu.VMEM((hg, s, LANES), F32), pltpu.SemaphoreType.DMA]),
        out_shape=[hm, hm, jax.ShapeDtypeStruct((N_HEADS, s, LANES), F32)],
        compiler_params=_cp("arbitrary", "arbitrary"),
    )(ki_of, qi_of, *extra, qb, k, v, do)


def _mla_post_call(dq, dk, dv, cq, ckv, qnw, kvnw, wq, wk, wv, cos_m, sin_m, ts):
    s = cq.shape[0]

    def body(dq_ref, dk_ref, dv_ref, cq_ref, ckv_ref, qnw_ref, kvnw_ref, wq_ref, wk_ref, wv_ref, cm_ref, sm_ref,
             dcq_ref, dckv_ref, dkpe_ref, dwq_ref, dwk_ref, dwv_ref, dqnw_ref, dkvnw_ref):
        _zero_first(pl.program_id(0) == 0, dwq_ref, dwk_ref, dwv_ref, dqnw_ref, dkvnw_ref)
        cqv, ckvv = cq_ref[...], ckv_ref[...]
        rq, rkv = _rstd(cqv), _rstd(ckvv)
        qh_, kvh_ = cqv * rq, ckvv * rkv
        qnw_v, kvnw_v = qnw_ref[...], kvnw_ref[...]
        cqn = (qh_ * qnw_v).astype(BF16)
        ckvn = (kvh_ * kvnw_v).astype(BF16)
        cm, sm = cm_ref[...], sm_ref[...]
        dcqn = jnp.zeros((ts, Q_RANK), F32)
        dckvn = jnp.zeros((ts, KV_RANK), F32)
        dkpe = jnp.zeros((ts, LANES), F32)
        for h in range(N_HEADS):
            dqu = _unrope(dq_ref[h] * SM_SCALE, cm, sm, ROPE // 2).astype(BF16)
            dwq_ref[h] += _dot_tn(cqn, dqu)
            dcqn = dcqn + _dot_nt(dqu, wq_ref[h])
            dkb, dvb = dk_ref[h], dv_ref[h]
            dkpe = dkpe + dkb.astype(F32)
            dwk_ref[h] += _dot_tn(ckvn, dkb)
            dwv_ref[h] += _dot_tn(ckvn, dvb)
            dckvn = dckvn + _dot_nt(dkb, wk_ref[h]) + _dot_nt(dvb, wv_ref[h])
        lane = lax.broadcasted_iota(jnp.int32, (ts, LANES), 1)
        dkpe = jnp.where((lane >= KPE_LO) & (lane < KPE_LO + ROPE), dkpe, 0.0)
        dkpe_ref[...] = _unrope(dkpe, cm, sm, ROPE // 2).astype(BF16)
        dqnw_ref[...] += _colsum(dcqn * qh_)
        dkvnw_ref[...] += _colsum(dckvn * kvh_)
        dcq_ref[...] = _norm_bwd(dcqn, qh_, rq, qnw_v).astype(BF16)
        dckv_ref[...] = _norm_bwd(dckvn, kvh_, rkv, kvnw_v).astype(BF16)

    sd = jax.ShapeDtypeStruct
    hm = _hrow(N_HEADS, ts, LANES)
    return pl.pallas_call(
        body, name="mla_post", grid=(s // ts,),
        in_specs=[hm, hm, hm, _row(ts, Q_RANK), _row(ts, KV_RANK), _full((1, Q_RANK)), _full((1, KV_RANK)),
                  _full((N_HEADS, Q_RANK, LANES)), _full((N_HEADS, KV_RANK, LANES)), _full((N_HEADS, KV_RANK, LANES)),
                  _row(ts, LANES), _row(ts, LANES)],
        out_specs=[_row(ts, Q_RANK), _row(ts, KV_RANK), _row(ts, LANES),
                   _full((N_HEADS, Q_RANK, LANES)), _full((N_HEADS, KV_RANK, LANES)), _full((N_HEADS, KV_RANK, LANES)),
                   _full((1, Q_RANK)), _full((1, KV_RANK))],
        out_shape=[sd((s, Q_RANK), BF16), sd((s, KV_RANK), BF16), sd((s, LANES), BF16),
                   sd((N_HEADS, Q_RANK, LANES), F32), sd((N_HEADS, KV_RANK, LANES), F32), sd((N_HEADS, KV_RANK, LANES), F32),
                   sd((1, Q_RANK), F32), sd((1, KV_RANK), F32)],
        compiler_params=_cp("arbitrary"),
    )(dq, dk, dv, cq, ckv, qnw, kvnw, wq, wk, wv, cos_m, sin_m)


def _in_bwd_call(parts, x, r1, anw, dx1, win, ts):
    s = x.shape[0]
    widths = [p.shape[1] for p in parts]
    np_ = len(parts)

    def body(*refs):
        p_refs = refs[:np_]
        x_ref, r_ref, anw_ref, dx1_ref, w_ref, dx_ref, dw_ref, danw_ref = refs[np_:]
        _zero_first(pl.program_id(0) == 0, dw_ref, danw_ref)
        dproj = jnp.concatenate([p[...] for p in p_refs], axis=-1)
        r, anw_v = r_ref[...], anw_ref[...]
        xh = x_ref[...] * r
        dw_ref[...] += _dot_tn((xh * anw_v).astype(BF16), dproj)
        dh = _dot_nt(dproj, w_ref[...])
        danw_ref[...] += _colsum(dh * xh)
        dx_ref[...] = dx1_ref[...] + _norm_bwd(dh, xh, r, anw_v)

    sd = jax.ShapeDtypeStruct
    return pl.pallas_call(
        body, name="in_proj_bwd", grid=(s // ts,),
        in_specs=[_row(ts, w) for w in widths]
        + [_row(ts, D_MODEL), _row(ts, 1), _full((1, D_MODEL)), _row(ts, D_MODEL), _full((D_MODEL, IN_EXT))],
        out_specs=[_row(ts, D_MODEL), _full((D_MODEL, IN_EXT)), _full((1, D_MODEL))],
        out_shape=[sd((s, D_MODEL), F32), sd((D_MODEL, IN_EXT), F32), sd((1, D_MODEL), F32)],
        compiler_params=_cp("arbitrary"),
    )(*parts, x, r1, anw, dx1, win)


def _local_step(x, positions, tgt, w, small, ex=None):
    s = x.shape[0]
    t = _tiles(s)
    ex = _Exchanges(w) if ex is None else ex
    f = _forward(x, positions, w, small, ex)
    pw, rc = f["pw"], f["rc"]
    cos_r, sin_r, cos_m, sin_m = f["tabs"]
    dx2, loss, g_fw = _loss_call(f["x2"], tgt, small["final_norm_w"], t["ts"])
    du, dx1, g_cw, g_cb, g_fnw, g_wd = _ffn_bwd_call(dx2, f["u"], f["uc"], w["conv_w"], pw["wdown"], pw["wup"],
                                                     f["x1"], f["r2"], small["ffn_norm_w"], t["t2"])
    g_wup = _dw_norm_call(f["x1"], f["r2"], small["ffn_norm_w"], du, t["ts"], F2 // 4, "dw_up")
    started = ex.mlp_grads(dict(w_up=g_wup, w_down=g_wd))
    dy_ret, do, g_wout = _out_bwd_call(dx1, f["y_ret"], f["y_mla"], pw["wout"], t["ts"], started)
    started = ex.behind_out_bwd(g_wout)
    drq, dg, do_ret, g_gnw = _ret_bwd_q_call(f["q"], f["k"], f["v"], f["o_ret"], f["g"], dy_ret, small["ret_gn_w"], rc, cos_r, sin_r, t["tr"])
    drk, drv = _ret_bwd_kv_call(f["q"], f["k"], f["v"], do_ret, rc, cos_r, sin_r, t["tr"])
    dmk, dmv, dmq = _flash_bwd_call(f["mqb"], f["mk"], f["mv"], do, t["tb"], started)
    ex.behind_attention(dmk)
    dcq, dckv, dkpe, g_wq, g_wk, g_wv, g_qnw, g_kvnw = _mla_post_call(
        dmq, dmk, dmv, f["cq"], f["ckv"], small["mla_q_norm_w"], small["mla_kv_norm_w"], pw["wq"], pw["wk"], pw["wv"], cos_m, sin_m, t["ts"])
    gx, g_win_ext, g_anw = _in_bwd_call([drq, drk, drv, dg, dcq, dckv, dkpe], x, f["r1"], small["attn_norm_w"], dx1, pw["win"], t["ts"])
    lo = IN_W - ROPE
    g_win = jnp.concatenate([g_win_ext[:, :lo], g_win_ext[:, lo + KPE_LO:lo + KPE_LO + ROPE]], -1)
    g_wuq = g_wq.transpose(1, 0, 2)[:, :, :HEAD + ROPE].reshape(Q_RANK, N_HEADS * (HEAD + ROPE))
    g_wukv = jnp.concatenate([g_wk[:, :, :HEAD], g_wv[:, :, :HEAD]], -1).transpose(1, 0, 2).reshape(KV_RANK, 2 * MLA_W)
    gw = dict(w_in=g_win, w_uq=g_wuq, w_ukv=g_wukv, w_out=g_wout, w_up=g_wup,
              conv_w=g_cw, w_down=g_wd)
    gs = dict(attn_norm_w=g_anw, ret_gn_w=g_gnw, mla_q_norm_w=g_qnw, mla_kv_norm_w=g_kvnw, ffn_norm_w=g_fnw,
              conv_b=g_cb, final_norm_w=g_fw)
    return loss, gx, gw, gs


MESH_ID = pl.DeviceIdType.MESH
ANY = pl.BlockSpec(memory_space=pl.ANY)
VMEM_SPEC = pl.BlockSpec(memory_space=pltpu.VMEM)
N_DEV = 8
GROUP_A = (("w_in", (D_MODEL, IN_W // 4), 1), ("w_uq", (Q_RANK, 192), 1), ("w_ukv", (KV_RANK, 256), 1),
           ("w_out", (D_MODEL // 4, D_MODEL), 0))
GROUP_B = (("w_up", (D_MODEL, F2 // 4), 1), ("w_down", (D_FF // 4, D_MODEL), 0))
HBM_SPEC = pl.BlockSpec(memory_space=pltpu.HBM)
SEM_SPEC = pl.BlockSpec(memory_space=pltpu.SEMAPHORE)


def _mesh_pos():
    return lax.axis_index("x"), lax.axis_index("y"), lax.axis_index("c")


def _other_chips(x, y):
    return [(1 - x, y), (x, 1 - y), (1 - x, 1 - y)]


def _remote(src, dst, send_sems, recv_sems, k, dev):
    return pltpu.make_async_remote_copy(src_ref=src, dst_ref=dst, send_sem=send_sems.at[k], recv_sem=recv_sems.at[k],
                                        device_id=dev, device_id_type=MESH_ID)


def _gather_list_call(parts, tag):
    n = len(parts)

    def body(*refs):
        srcs, outs, (send_sems, recv_sems) = refs[:n], refs[n:2 * n], refs[2 * n:]
        x, y, c = _mesh_pos()
        sm = 2 * x + y
        chips = _other_chips(x, y)
        sib = (x, y, 1 - c)
        rc = lambda k, src, dst, dev: _remote(src, dst, send_sems, recv_sems, k, dev)
        first = [rc(7 * i + j, srcs[i].at[c], outs[i].at[sm, c], (cx, cy, c)) for i in range(n) for j, (cx, cy) in enumerate(chips)]
        own = [rc(7 * i + 6, srcs[i], outs[i].at[sm], sib) for i in range(n)]
        for cp in first + own:
            cp.start()
        passed = []
        for j, (cx, cy) in enumerate(chips):
            for i in range(n):
                land = outs[i].at[2 * cx + cy, c]
                rc(7 * i + j, srcs[i].at[c], land, (cx, cy, c)).wait_recv()
                cp = rc(7 * i + 3 + j, land, land, sib)
                cp.start()
                passed.append(cp)
        for j, (cx, cy) in enumerate(chips):
            for i in range(n):
                rc(7 * i + 3 + j, srcs[i].at[c], outs[i].at[2 * cx + cy, 1 - c], sib).wait_recv()
        for cp in own:
            cp.wait_recv()
        for cp in first + passed + own:
            cp.wait_send()

    return pl.pallas_call(
        body, name="weights_all_gather_" + tag,
        in_specs=[ANY] * n, out_specs=[ANY] * n,
        out_shape=[jax.ShapeDtypeStruct((4,) + p.shape, p.dtype) for p in parts],
        scratch_shapes=[pltpu.SemaphoreType.DMA((7 * n,)), pltpu.SemaphoreType.DMA((7 * n,))],
    )(*parts)


def _direct_gather_copies(srcs, lands, send_sems, recv_sems):
    x, y, c = _mesh_pos()
    sm = 2 * x + y
    sends, recvs = [], []
    for i, (src, land) in enumerate(zip(srcs, lands)):
        for j, (cx, cy) in enumerate(_other_chips(x, y)):
            for t in range(2):
                sends.append(_remote(src.at[c], land.at[sm, c], send_sems, recv_sems, 13 * i + 4 * j + 2 * c + t, (cx, cy, t)))
                recvs.append(_remote(src.at[t], land.at[2 * cx + cy, t], send_sems, recv_sems, 13 * i + 4 * j + 2 * t + c, (cx, cy, t)))
        sends.append(_remote(src, land.at[sm], send_sems, recv_sems, 13 * i + 12, (x, y, 1 - c)))
        recvs.append(_remote(src, land.at[sm], send_sems, recv_sems, 13 * i + 12, (x, y, 1 - c)))
    return sends, recvs


def _sibling_copies(srcs, lands, send_sems, recv_sems):
    x, y, c = _mesh_pos()
    cps = [_remote(src.at[s, 1 - c], land.at[s], send_sems, recv_sems, 4 * i + s, (x, y, 1 - c))
           for i, (src, land) in enumerate(zip(srcs, lands)) for s in range(4)]
    return cps, cps


def _chips_copies(srcs, lands, send_sems, recv_sems):
    x, y, c = _mesh_pos()
    cps = [_remote(src.at[2 * cx + cy], land.at[j], send_sems, recv_sems, 3 * i + j, (cx, cy, c))
           for i, (src, land) in enumerate(zip(srcs, lands)) for j, (cx, cy) in enumerate(_other_chips(x, y))]
    return cps, cps


def _share_copies(srcs, lands, send_sems, recv_sems):
    x, y, c = _mesh_pos()
    cps = [_remote(src, land, send_sems, recv_sems, i, (x, y, 1 - c)) for i, (src, land) in enumerate(zip(srcs, lands))]
    return cps, cps


def _exchange_call(name, copies, srcs, land_shapes, n_sems):
    n = len(srcs)

    def body(*refs):
        sends, recvs = copies(refs[:n], refs[n:2 * n], refs[2 * n], refs[2 * n + 1])
        for cp in sends:
            cp.start()
        for cp in sends:
            cp.wait_send()
        for cp in recvs:
            cp.wait_recv()

    return pl.pallas_call(
        body, name=name, in_specs=[ANY] * n, out_specs=[ANY] * n, out_shape=list(land_shapes),
        scratch_shapes=[pltpu.SemaphoreType.DMA((n_sems,)), pltpu.SemaphoreType.DMA((n_sems,))],
    )(*srcs)


def _exchange_start_call(name, copies, srcs, land_shapes, n_sems, order=None):
    n = len(srcs)
    extra = [] if order is None else [order]
    k = 2 * n + len(extra)

    def body(*refs):
        sends, _ = copies(refs[:n], refs[n:2 * n], refs[k], refs[k + 1])
        for cp in sends:
            cp.start()
        refs[-1][...] = jnp.zeros_like(refs[-1])

    hbm = lambda a: pltpu.with_memory_space_constraint(a, pltpu.HBM)
    lands = [hbm(lax.empty(sd.shape, sd.dtype)) for sd in land_shapes]
    sem = pltpu.SemaphoreType.DMA((n_sems,))
    out = pl.pallas_call(
        body, name=name,
        out_shape=(sem, sem, *[pltpu.HBM(a.shape, a.dtype) for a in list(srcs) + lands], jax.ShapeDtypeStruct((8, LANES), F32)),
        in_specs=[HBM_SPEC] * (2 * n) + [ANY] * len(extra), out_specs=(SEM_SPEC, SEM_SPEC, *[HBM_SPEC] * (2 * n), VMEM_SPEC),
        input_output_aliases={i: 2 + i for i in range(2 * n)},
        compiler_params=pltpu.CompilerParams(has_side_effects=pltpu.SideEffectType.DATAFLOW_SIDE_EFFECTING),
    )(*[hbm(a) for a in srcs], *lands, *extra)
    return out[0], out[1], out[2:2 + n], out[2 + n:2 + 2 * n], out[-1]


def _exchange_wait_call(name, copies, started, after):
    send_sems, recv_sems, srcs, lands, _ = started
    n = len(srcs)

    def body(*refs):
        sends, recvs = copies(refs[:n], refs[n:2 * n], refs[2 * n], refs[2 * n + 1])
        for cp in sends:
            cp.wait_send()
        for cp in recvs:
            cp.wait_recv()

    out = pl.pallas_call(
        body, name=name,
        out_shape=tuple(pltpu.HBM(a.shape, a.dtype) for a in list(srcs) + list(lands)),
        in_specs=[HBM_SPEC] * (2 * n) + [SEM_SPEC, SEM_SPEC, ANY], out_specs=tuple([HBM_SPEC] * (2 * n)),
        input_output_aliases={i: i for i in range(2 * n)},
        compiler_params=pltpu.CompilerParams(has_side_effects=pltpu.SideEffectType.DATAFLOW_SIDE_EFFECTING),
    )(*srcs, *lands, send_sems, recv_sems, after)
    return out[:n], out[n:]


def _rows_tile(rows, width, itemsize=4):
    limit = max(16, (3 << 20) // (width * itemsize))
    if rows <= limit:
        return rows
    return max(t for t in range(16, limit + 1, 16) if rows % t == 0)


def _sum_sibling_call(g, buf, c, name):
    _, _, rh, w = g.shape
    tile = _rows_tile(rh, w)

    def body(c_ref, g_ref, b_ref, p_ref, pb_ref):
        p = g_ref[...] + b_ref[...]
        p_ref[...] = p
        pb_ref[...] = p.astype(BF16)

    blk = pl.BlockSpec((None, tile, w), lambda s, i, c_ref: (s, i, 0))
    return pl.pallas_call(
        body, name=name,
        grid_spec=pltpu.PrefetchScalarGridSpec(
            num_scalar_prefetch=1, grid=(4, rh // tile),
            in_specs=[pl.BlockSpec((None, None, tile, w), lambda s, i, c_ref: (s, c_ref[0], i, 0)), blk],
            out_specs=[blk, blk]),
        out_shape=[jax.ShapeDtypeStruct((4, rh, w), F32), jax.ShapeDtypeStruct((4, rh, w), BF16)],
        compiler_params=_cp("parallel", "parallel"),
    )(c, g, buf)


def _sum_chips_call(p, buf, sm, name):
    _, rh, w = p.shape
    tile = _rows_tile(rh, w)

    def body(sm_ref, p_ref, b_ref, f_ref):
        f_ref[...] = ((p_ref[...] + b_ref[0].astype(F32)) + b_ref[1].astype(F32)) + b_ref[2].astype(F32)

    return pl.pallas_call(
        body, name=name,
        grid_spec=pltpu.PrefetchScalarGridSpec(
            num_scalar_prefetch=1, grid=(rh // tile,),
            in_specs=[pl.BlockSpec((None, tile, w), lambda i, sm_ref: (sm_ref[0], i, 0)),
                      pl.BlockSpec((3, tile, w), lambda i, sm_ref: (0, i, 0))],
            out_specs=pl.BlockSpec((tile, w), lambda i, sm_ref: (i, 0))),
        out_shape=jax.ShapeDtypeStruct((rh, w), F32),
        compiler_params=_cp("parallel"),
    )(sm, p, buf)


def _adamw_halves_call(w, g_mine, g_sib, c, m, v, name):
    r, wd = w.shape
    rh = r // 2
    tile = _rows_tile(rh, wd)
    nt = rh // tile

    def body(c_ref, w_ref, gm_ref, gs_ref, m_ref, v_ref, g_ref, d_ref, nm_ref, nv_ref):
        gv = jnp.where(pl.program_id(0) == c_ref[0], gm_ref[...], gs_ref[...])
        g_ref[...] = gv
        nm = ADAM_B1 * m_ref[...] + (1.0 - ADAM_B1) * gv
        nv = ADAM_B2 * v_ref[...] + (1.0 - ADAM_B2) * jnp.square(gv)
        m_hat = nm / (1.0 - ADAM_B1 ** ADAM_STEP)
        v_hat = nv / (1.0 - ADAM_B2 ** ADAM_STEP)
        d_ref[...] = -ADAM_LR * (m_hat / (jnp.sqrt(v_hat) + ADAM_EPS) + ADAM_WD * w_ref[...])
        nm_ref[...] = nm
        nv_ref[...] = nv

    whole = pl.BlockSpec((tile, wd), lambda h, i, c_ref: (h * nt + i, 0))
    half = pl.BlockSpec((tile, wd), lambda h, i, c_ref: (i, 0))
    sd = jax.ShapeDtypeStruct((r, wd), F32)
    return pl.pallas_call(
        body, name=name,
        grid_spec=pltpu.PrefetchScalarGridSpec(
            num_scalar_prefetch=1, grid=(2, nt),
            in_specs=[whole, half, half, whole, whole], out_specs=[whole] * 4),
        out_shape=[sd, sd, sd, sd],
        compiler_params=_cp("parallel", "parallel"),
    )(c, w, g_mine, g_sib, m, v)


def _exchange8_call(vec, reduce, name):
    rows = vec.shape[0]

    def body(v_ref, out_ref, *rest):
        slots, send_sems, recv_sems = (rest if reduce else (out_ref,) + rest)
        x, y, c = _mesh_pos()
        me = 4 * x + 2 * y + c
        slots[me] = v_ref[...]

        def rcopy(k, to_me):
            bx, by, bc = (k >> 2) & 1, (k >> 1) & 1, k & 1
            px, py, pc = (1 - x if bx else x), (1 - y if by else y), (1 - c if bc else c)
            slot = 4 * px + 2 * py + pc if to_me else me
            return pltpu.make_async_remote_copy(src_ref=v_ref, dst_ref=slots.at[slot], send_sem=send_sems.at[k - 1],
                                                recv_sem=recv_sems.at[k - 1], device_id=(px, py, pc), device_id_type=MESH_ID)

        for k in range(1, N_DEV):
            rcopy(k, False).start()
        for k in range(1, N_DEV):
            rcopy(k, True).wait_recv()
        for k in range(1, N_DEV):
            rcopy(k, False).wait_send()
        if reduce:
            tot = slots[0]
            for d in range(1, N_DEV):
                tot = tot + slots[d]
            out_ref[...] = tot

    stack = jax.ShapeDtypeStruct((N_DEV, rows, LANES), F32)
    return pl.pallas_call(
        body, name=name,
        in_specs=[VMEM_SPEC], out_specs=VMEM_SPEC,
        out_shape=jax.ShapeDtypeStruct((rows, LANES), F32) if reduce else stack,
        scratch_shapes=([pltpu.VMEM((N_DEV, rows, LANES), F32)] if reduce else [])
        + [pltpu.SemaphoreType.DMA((N_DEV - 1,)), pltpu.SemaphoreType.DMA((N_DEV - 1,))],
    )(vec)


def _adamw_call(w, g, m, v, name):
    r, c = w.shape
    rb = r if r <= 256 else (256 if r % 256 == 0 else 352)
    assert r % rb == 0

    def body(w_ref, g_ref, m_ref, v_ref, d_ref, nm_ref, nv_ref):
        gv = g_ref[...]
        nm = ADAM_B1 * m_ref[...] + (1.0 - ADAM_B1) * gv
        nv = ADAM_B2 * v_ref[...] + (1.0 - ADAM_B2) * jnp.square(gv)
        m_hat = nm / (1.0 - ADAM_B1 ** ADAM_STEP)
        v_hat = nv / (1.0 - ADAM_B2 ** ADAM_STEP)
        d_ref[...] = -ADAM_LR * (m_hat / (jnp.sqrt(v_hat) + ADAM_EPS) + ADAM_WD * w_ref[...])
        nm_ref[...] = nm
        nv_ref[...] = nv

    spec = pl.BlockSpec((rb, c), lambda i: (i, 0))
    sd = jax.ShapeDtypeStruct((r, c), F32)
    return pl.pallas_call(
        body, name=name, grid=(r // rb,),
        in_specs=[spec] * 4, out_specs=[spec] * 3, out_shape=[sd, sd, sd],
        compiler_params=_cp("parallel"),
    )(w, g, m, v)


SMALL = (("attn_norm_w", D_MODEL), ("ret_gn_w", RET_W), ("mla_q_norm_w", Q_RANK), ("mla_kv_norm_w", KV_RANK),
         ("ffn_norm_w", D_MODEL), ("conv_b", F2), ("final_norm_w", D_MODEL))
WEIGHT_ORDER = ("attn_norm_w", "w_in", "ret_gn_w", "mla_q_norm_w", "w_uq", "mla_kv_norm_w", "w_ukv", "w_out",
                "ffn_norm_w", "w_up", "conv_w", "conv_b", "w_down", "final_norm_w")


def _pad_rows(flat, rows):
    return jnp.concatenate([flat, jnp.zeros((rows * LANES - flat.shape[0],), flat.dtype)]).reshape(rows, LANES)


def kernel(x, positions, attn_norm_w, w_in, ret_gn_w, mla_q_norm_w, w_uq, mla_kv_norm_w, w_ukv, w_out, ffn_norm_w, w_up, conv_w, conv_b, w_down, final_norm_w, loss_target, m_attn_norm_w, m_w_in, m_ret_gn_w, m_mla_q_norm_w, m_w_uq, m_mla_kv_norm_w, m_w_ukv, m_w_out, m_ffn_norm_w, m_w_up, m_conv_w, m_conv_b, m_w_down, m_final_norm_w, v_attn_norm_w, v_w_in, v_ret_gn_w, v_mla_q_norm_w, v_w_uq, v_mla_kv_norm_w, v_w_ukv, v_w_out, v_ffn_norm_w, v_w_up, v_conv_w, v_conv_b, v_w_down, v_final_norm_w):
    args = dict(locals())
    cx, cy, cc = _mesh_pos()
    sm = 2 * cx + cy

    c_arr, sm_arr = cc.reshape(1).astype(jnp.int32), sm.reshape(1).astype(jnp.int32)
    sds = jax.ShapeDtypeStruct

    def my_shards(group):
        return [args[n][0].astype(BF16).reshape(2, r // 2, c) for n, (r, c), _ in group]

    def full_weights(gathered, group):
        full = {}
        for (n, (r, c), axis), got in zip(group, gathered):
            piece = got.reshape(4, r, c)
            full[n] = piece if n == "w_up" else (piece.transpose(1, 0, 2).reshape(r, 4 * c) if axis == 1 else piece.reshape(4 * r, c))
        return full

    def by_owner(gw, group):
        out = []
        for n, (r, c), axis in group:
            g = gw[n]
            if axis == 1 and g.ndim == 2:
                g = g.reshape(r, 4, c).transpose(1, 0, 2)
            out.append(g.reshape(4, 2, r // 2, c))
        return out

    def sibling_shapes(gs):
        return [sds((4,) + g.shape[2:], F32) for g in gs]

    def chip_sums(gs, bufs, group):
        res = [_sum_sibling_call(g, b, c_arr, "grads_sum_sibling_" + n) for g, b, (n, _, _) in zip(gs, bufs, group)]
        return [p for p, _ in res], [pb for _, pb in res]

    def chips_shapes(pbs):
        return [sds((3,) + pb.shape[1:], BF16) for pb in pbs]

    def totals(ps, lands, group, tag):
        fins = [_sum_chips_call(p, l, sm_arr, "grads_sum_chips_" + n) for p, l, (n, _, _) in zip(ps, lands, group)]
        sibs = _exchange_call("grads_rs_share_" + tag, _share_copies, fins, [sds(f.shape, F32) for f in fins], len(fins))
        return {n: (f, s) for (n, _, _), f, s in zip(group, fins, sibs)}

    class StepExchanges(_Exchanges):
        def __init__(self, order):
            shards = my_shards(GROUP_B)
            self.gather = _exchange_start_call("weights_gather_start_b", _direct_gather_copies, shards,
                                               [sds((4,) + s.shape, BF16) for s in shards], 13 * len(shards), order)
            self.red = None

        def token(self):
            return self.gather[4][0:1, 0:1]

        def mlp_weights(self, after):
            return full_weights(_exchange_wait_call("weights_gather_wait_b", _direct_gather_copies, self.gather, after)[1], GROUP_B)

        def mlp_grads(self, gw):
            gs = by_owner(gw, GROUP_B)
            self.step1 = _exchange_start_call("grads_rs_sibling_start_b", _sibling_copies, gs, sibling_shapes(gs), 4 * len(gs))
            return self.step1[4]

        def behind_out_bwd(self, after):
            gs, bufs = _exchange_wait_call("grads_rs_sibling_wait_b", _sibling_copies, self.step1, after)
            self.ps, pbs = chip_sums(gs, bufs, GROUP_B)
            self.step2 = _exchange_start_call("grads_rs_chips_start_b", _chips_copies, pbs, chips_shapes(pbs), 3 * len(pbs))
            return self.step2[4]

        def behind_attention(self, after):
            _, lands = _exchange_wait_call("grads_rs_chips_wait_b", _chips_copies, self.step2, after)
            self.red = totals(self.ps, lands, GROUP_B, "b")

    full = full_weights(_gather_list_call(my_shards(GROUP_A), "a"), GROUP_A)
    cw_rows = 40
    cw_all = _exchange8_call(_pad_rows(conv_w[0].reshape(-1), cw_rows), False, "conv_w_all_gather")
    ex = StepExchanges(cw_all)
    cw_all = cw_all[0::2].reshape(4, cw_rows * LANES)[:, :3 * F2 // 4].reshape(4, 3, F2 // 4)
    full["conv_w"] = cw_all.transpose(1, 0, 2).reshape(3, F2)
    small = {n: args[n].reshape(1, d) for n, d in SMALL}
    small["attn_norm_w"] = small["attn_norm_w"] + ex.token()

    loss, gx, gw, gs = _local_step(x[0], positions[0], loss_target[0], full, small, ex)

    ga = by_owner(gw, GROUP_A)
    bufs = _exchange_call("grads_rs_sibling_a", _sibling_copies, ga, sibling_shapes(ga), 4 * len(ga))
    ps, pbs = chip_sums(ga, bufs, GROUP_A)
    lands = _exchange_call("grads_rs_chips_a", _chips_copies, pbs, chips_shapes(pbs), 3 * len(pbs))
    halves = {**ex.red, **totals(ps, lands, GROUP_A, "a")}

    vec = jnp.concatenate([gs[n].reshape(-1) for n, _ in SMALL] + [gw["conv_w"].reshape(-1), loss.reshape(-1)])
    tot = _exchange8_call(_pad_rows(vec, 216), True, "small_all_reduce").reshape(-1)
    red, off = {}, 0
    for n, d in SMALL:
        red[n] = tot[off:off + d].reshape(1, d)
        off += d
    red["conv_w"] = lax.dynamic_slice(tot[off:off + 3 * F2].reshape(3, F2), (0, sm * (F2 // 4)), (3, F2 // 4))
    loss_tot = tot[off + 3 * F2]

    grads, deltas, new_m, new_v = [], [], [], []
    for n in WEIGHT_ORDER:
        shape = args[n].shape
        two_d = (1, shape[0]) if len(shape) == 1 else shape[-2:]
        wmv = [args[k + n].reshape(two_d) for k in ("", "m_", "v_")]
        if n in halves:
            g, d, nm, nv = _adamw_halves_call(wmv[0], *halves[n], c_arr, wmv[1], wmv[2], "adamw_" + n)
        else:
            g = red[n].reshape(two_d)
            d, nm, nv = _adamw_call(wmv[0], g, wmv[1], wmv[2], "adamw_" + n)
        grads.append(g.reshape(shape))
        deltas.append(d.reshape(shape))
        new_m.append(nm.reshape(shape))
        new_v.append(nv.reshape(shape))
    return (loss_tot, gx[None], *grads, *deltas, *new_m, *new_v)
```

```python
import functools
import math

import numpy as np
import jax
import jax.numpy as jnp
from jax import lax
from jax.experimental import pallas as pl
from jax.experimental.pallas import tpu as pltpu

F32 = jnp.float32
BF16 = jnp.bfloat16

D_MODEL = 1024
N_HEADS = 8
HEAD = 64
RET_W = N_HEADS * HEAD
MLA_W = N_HEADS * HEAD
ROPE = 32
Q_RANK = 256
KV_RANK = 128
D_FF = 2816
F2 = 2 * D_FF
IN_W = 4 * RET_W + Q_RANK + KV_RANK + ROPE
IN_EXT = 4 * RET_W + Q_RANK + KV_RANK + 128
KPE_LO = 64
ROPE_BASE = 10000.0
EPS = 1e-6
RET_CHUNK = 128
SM_SCALE = (HEAD + ROPE) ** -0.5
LOG2E = math.log2(math.e)
LN2 = math.log(2.0)
NEG = -1e30
LANES = 128
VMEM_LIMIT = 56 * 1024 * 1024

ADAM_LR = 0.001
ADAM_B1 = 0.9
ADAM_B2 = 0.999
ADAM_EPS = 1e-08
ADAM_WD = 0.01
ADAM_STEP = 10


VMEM_LIMIT_MLP_BWD = 60 * 1024 * 1024


def _cp(*sem, vmem=VMEM_LIMIT):
    return pltpu.CompilerParams(dimension_semantics=sem, vmem_limit_bytes=vmem)


def _full(shape):
    n = len(shape)
    return pl.BlockSpec(tuple(shape), lambda *_: (0,) * n)


def _row(ts, c):
    return pl.BlockSpec((ts, c), lambda i: (i, 0))


def _hrow(h, ts, c):
    return pl.BlockSpec((h, ts, c), lambda i: (0, i, 0))


def _dot(a, b):
    return jnp.dot(a, b, preferred_element_type=F32)


def _dot_nt(a, b):
    return lax.dot_general(a, b, (((1,), (1,)), ((), ())), preferred_element_type=F32)


def _dot_tn(a, b):
    return lax.dot_general(a, b, (((0,), (0,)), ((), ())), preferred_element_type=F32)


def _dot_hi(a, b):
    hi = a.astype(BF16)
    lo = (a - hi.astype(F32)).astype(BF16)
    bb = b.astype(BF16)
    return _dot(hi, bb) + _dot(lo, bb)


def _rot_half(x, half):
    w = x.shape[-1]
    lane = lax.broadcasted_iota(jnp.int32, x.shape, x.ndim - 1)
    first = (lane % (2 * half)) < half
    return jnp.where(first, -pltpu.roll(x, w - half, x.ndim - 1), pltpu.roll(x, half, x.ndim - 1))


def _rope(x, cos, sin, half):
    return x * cos + _rot_half(x, half) * sin


def _unrope(dy, cos, sin, half):
    return dy * cos - _rot_half(dy, half) * sin


def _sigmoid(g):
    return 0.5 * jnp.tanh(0.5 * g) + 0.5


def _silu(g):
    return g * _sigmoid(g)


def _rstd(x):
    return lax.rsqrt(jnp.mean(x * x, axis=-1, keepdims=True) + EPS)


def _rope_tables(positions):
    pos = positions.astype(F32)[:, None]
    s = pos.shape[0]
    inv = ROPE_BASE ** (-jnp.arange(0, HEAD, 2, dtype=F32) / HEAD)
    ang = pos * inv
    c, sn = jnp.cos(ang), jnp.sin(ang)
    cos_r = jnp.tile(jnp.concatenate([c, c], -1), (1, 2))
    sin_r = jnp.tile(jnp.concatenate([sn, sn], -1), (1, 2))
    inv = ROPE_BASE ** (-jnp.arange(0, ROPE, 2, dtype=F32) / ROPE)
    ang = pos * inv
    c, sn = jnp.cos(ang), jnp.sin(ang)
    one, zero = jnp.ones((s, KPE_LO), F32), jnp.zeros((s, KPE_LO), F32)
    cos_m = jnp.concatenate([one, c, c, one[:, :LANES - KPE_LO - ROPE]], -1)
    sin_m = jnp.concatenate([zero, sn, sn, zero[:, :LANES - KPE_LO - ROPE]], -1)
    return cos_r, sin_r, cos_m, sin_m


def _ret_consts():
    c = RET_CHUNK
    lg = np.log1p(-np.power(2.0, -5.0 - np.arange(N_HEADS, dtype=np.float64)))
    idx = np.arange(c, dtype=np.float64)
    diff = idx[:, None] - idx[None, :]
    lane_head = np.arange(LANES) // HEAD
    dmask = np.zeros((4, 2, c, c))
    zeta = np.zeros((4, c, LANES))
    xi = np.zeros((4, c, LANES))
    cd = np.zeros((4, LANES, LANES))
    bd = (lane_head[:, None] == lane_head[None, :]).astype(np.float64)
    for j in range(4):
        for hh in range(2):
            dmask[j, hh] = np.where(diff >= 0, np.exp(lg[2 * j + hh] * np.maximum(diff, 0.0)), 0.0)
        lgl = lg[2 * j + lane_head]
        zeta[j] = np.exp(lgl[None, :] * (c - 1.0 - idx[:, None]))
        xi[j] = np.exp(lgl[None, :] * (idx[:, None] + 1.0))
        cd[j] = np.exp(lgl * c)[:, None] * bd
    f = lambda a: jnp.asarray(a, F32)
    side = lambda d: np.concatenate([d[:, 0], d[:, 1]], axis=-1)
    return dict(dmask=f(side(dmask)), dmask_t=f(side(np.swapaxes(dmask, 2, 3))), zeta=f(zeta), xi=f(xi), cd=f(cd), bd=f(bd))


def _f1_call(x, anw, win, cos_r, sin_r, cos_m, sin_m, ts):
    s = x.shape[0]

    def body(x_ref, anw_ref, w_ref, cr_ref, sr_ref, cm_ref, sm_ref,
             q_ref, k_ref, v_ref, g_ref, cq_ref, ckv_ref, kpe_ref, r_ref):
        xv = x_ref[...]
        r = _rstd(xv)
        r_ref[...] = r
        h = (xv * r * anw_ref[...]).astype(BF16)
        cr, sr = cr_ref[...], sr_ref[...]
        qk = _dot(h, w_ref[:, 0:2 * RET_W])
        for j in range(4):
            sl = slice(j * LANES, (j + 1) * LANES)
            q_ref[:, sl] = _rope(qk[:, sl], cr, sr, HEAD // 2).astype(BF16)
            kk = qk[:, RET_W + j * LANES:RET_W + (j + 1) * LANES]
            k_ref[:, sl] = (_rope(kk, cr, sr, HEAD // 2) * (HEAD ** -0.5)).astype(BF16)
        v_ref[...] = _dot(h, w_ref[:, 2 * RET_W:3 * RET_W]).astype(BF16)
        g_ref[...] = _dot(h, w_ref[:, 3 * RET_W:4 * RET_W])
        o = 4 * RET_W
        cq_ref[...] = _dot(h, w_ref[:, o:o + Q_RANK])
        ckv_ref[...] = _dot(h, w_ref[:, o + Q_RANK:o + Q_RANK + KV_RANK])
        kp = _dot(h, w_ref[:, o + Q_RANK + KV_RANK:IN_EXT])
        kpe_ref[...] = _rope(kp, cm_ref[...], sm_ref[...], ROPE // 2)

    sd = jax.ShapeDtypeStruct
    return pl.pallas_call(
        body, name="f1_in_proj", grid=(s // ts,),
        in_specs=[_row(ts, D_MODEL), _full((1, D_MODEL)), _full((D_MODEL, IN_EXT)),
                  _row(ts, LANES), _row(ts, LANES), _row(ts, LANES), _row(ts, LANES)],
        out_specs=[_row(ts, RET_W), _row(ts, RET_W), _row(ts, RET_W), _row(ts, RET_W),
                   _row(ts, Q_RANK), _row(ts, KV_RANK), _row(ts, LANES), _row(ts, 1)],
        out_shape=[sd((s, RET_W), BF16), sd((s, RET_W), BF16), sd((s, RET_W), BF16), sd((s, RET_W), F32),
                   sd((s, Q_RANK), F32), sd((s, KV_RANK), F32), sd((s, LANES), F32), sd((s, 1), F32)],
        compiler_params=_cp("parallel"),
    )(x, anw, win, cos_r, sin_r, cos_m, sin_m)


def _stack_heads(a):
    lo = lax.broadcasted_iota(jnp.int32, a.shape, 1) < HEAD
    zero = jnp.zeros_like(a)
    return jnp.concatenate([jnp.where(lo, a, zero), jnp.where(lo, zero, a)], axis=0)


def _pair_product(a, b2, decay2, w2):
    return _dot((_dot_nt(a, b2) * decay2).astype(BF16), w2)


RET_SLABS = 2


def _ret_specs(tr, tile_of):
    c, ns = RET_CHUNK, RET_SLABS
    return dict(
        slab=pl.BlockSpec((tr, ns * LANES), lambda j, i: (tile_of(i), j)),
        tab=pl.BlockSpec((tr, LANES), lambda j, i: (tile_of(i), 0)),
        vec=pl.BlockSpec((1, ns * LANES), lambda j, i: (0, j)),
        dmask=pl.BlockSpec((ns, c, 2 * c), lambda j, i: (j, 0, 0)),
        rows=pl.BlockSpec((ns, c, LANES), lambda j, i: (j, 0, 0)),
        state=pl.BlockSpec((ns, LANES, LANES), lambda j, i: (j, 0, 0)),
        bd=pl.BlockSpec((LANES, LANES), lambda j, i: (0, 0)))


def _ret_states(a_ref, b_ref, scale_ref, cd_ref, bd, st_ref, chunks, lanes, reverse):
    nc = len(chunks)
    contrib = [[_dot_tn((a_ref[rows, ln].astype(F32) * scale_ref[sl]).astype(BF16), b_ref[rows, ln]) * bd for rows in chunks]
               for sl, ln in enumerate(lanes)]
    states = []
    for sl in range(len(lanes)):
        st, seen = st_ref[sl], [None] * nc
        for ci in (reversed(range(nc)) if reverse else range(nc)):
            seen[ci] = st.astype(BF16)
            st = st * cd_ref[sl] + contrib[sl][ci]
        st_ref[sl] = st
        states.append(seen)
    return states


def _ret_fwd_call(q, k, v, g, gnw, rc, tr):
    s = q.shape[0]
    c = RET_CHUNK
    nc = tr // c
    ns = RET_SLABS

    def body(q_ref, k_ref, v_ref, g_ref, gnw_ref, dm_ref, zeta_ref, xi_ref, cd_ref, bd_ref, o_ref, y_ref, st_ref):
        @pl.when(pl.program_id(1) == 0)
        def _():
            st_ref[...] = jnp.zeros_like(st_ref)

        bd = bd_ref[...]
        chunks = [slice(ci * c, (ci + 1) * c) for ci in range(nc)]
        lanes = [slice(sl * LANES, (sl + 1) * LANES) for sl in range(ns)]
        states = _ret_states(k_ref, v_ref, zeta_ref, cd_ref, bd, st_ref, chunks, lanes, False)
        for ci, rows in enumerate(chunks):
            for sl, ln in enumerate(lanes):
                qc = q_ref[rows, ln]
                o_ref[rows, ln] = (_dot(qc, states[sl][ci]) * xi_ref[sl]
                                   + _pair_product(qc, _stack_heads(k_ref[rows, ln]), dm_ref[sl], _stack_heads(v_ref[rows, ln])))
        avg = bd * (1.0 / HEAD)
        for ln in lanes:
            o = o_ref[:, ln]
            ctr = o - _dot_hi(o, avg)
            var = _dot_hi(ctr * ctr, avg)
            y_ref[:, ln] = (_silu(g_ref[:, ln]) * (ctr * lax.rsqrt(var + EPS) * gnw_ref[:, ln])).astype(BF16)

    specs = _ret_specs(tr, lambda i: i)
    sd = jax.ShapeDtypeStruct
    return pl.pallas_call(
        body, name="ret_fwd", grid=(4 // ns, s // tr),
        in_specs=[specs["slab"]] * 4 + [specs["vec"], specs["dmask"], specs["rows"], specs["rows"], specs["state"], specs["bd"]],
        out_specs=[specs["slab"]] * 2,
        out_shape=[sd((s, RET_W), F32), sd((s, RET_W), BF16)],
        scratch_shapes=[pltpu.VMEM((ns, LANES, LANES), F32)],
        compiler_params=_cp("parallel", "arbitrary"),
    )(q, k, v, g, gnw, rc["dmask"], rc["zeta"], rc["xi"], rc["cd"], rc["bd"])


QK_AUX = HEAD + ROPE
V_AUX = HEAD


def _lane_pair(shape, lo, a, b, rest):
    lane = lax.broadcasted_iota(jnp.int32, shape, len(shape) - 1)
    return jnp.where(lane == lo, a, jnp.where(lane == lo + 1, b, rest))


def _hi_lo(v):
    hi = v.astype(BF16).astype(F32)
    return hi, v - hi


def _mla_pre_call(cq, ckv, kpe, qnw, kvnw, wq, wk, wv, cos_m, sin_m, ts):
    s = cq.shape[0]

    def body(cq_ref, ckv_ref, kpe_ref, qnw_ref, kvnw_ref, wq_ref, wk_ref, wv_ref, cm_ref, sm_ref, q_ref, k_ref, v_ref):
        cqv, ckvv = cq_ref[...], ckv_ref[...]
        cqn = (cqv * _rstd(cqv) * qnw_ref[...]).astype(BF16)
        ckvn = (ckvv * _rstd(ckvv) * kvnw_ref[...]).astype(BF16)
        cm, sm = cm_ref[...], sm_ref[...]
        kp = _lane_pair((ts, LANES), QK_AUX, -1.0, -1.0, kpe_ref[...])
        for h in range(N_HEADS):
            qh = _rope(_dot(cqn, wq_ref[h]), cm, sm, ROPE // 2)
            q_ref[h] = (qh * (SM_SCALE * LOG2E)).astype(BF16)
            k_ref[h] = (_dot(ckvn, wk_ref[h]) + kp).astype(BF16)
            v_ref[h] = _lane_pair((ts, LANES), V_AUX, 1.0, 1.0, _dot(ckvn, wv_ref[h])).astype(BF16)

    sd = jax.ShapeDtypeStruct
    hm = sd((N_HEADS, s, LANES), BF16)
    return pl.pallas_call(
        body, name="mla_pre", grid=(s // ts,),
        in_specs=[_row(ts, Q_RANK), _row(ts, KV_RANK), _row(ts, LANES), _full((1, Q_RANK)), _full((1, KV_RANK)),
                  _full((N_HEADS, Q_RANK, LANES)), _full((N_HEADS, KV_RANK, LANES)), _full((N_HEADS, KV_RANK, LANES)),
                  _row(ts, LANES), _row(ts, LANES)],
        out_specs=[_hrow(N_HEADS, ts, LANES)] * 3,
        out_shape=[hm, hm, hm],
        compiler_params=_cp("parallel"),
    )(cq, ckv, kpe, qnw, kvnw, wq, wk, wv, cos_m, sin_m)


def _flash_fwd_call(q, k, v, tb):
    s = q.shape[1]
    nb = s // tb
    pairs = [(a, b) for a in range(nb) for b in range(a + 1)]
    qi_of, ki_of = (jnp.asarray(np.array(col, np.int32)) for col in zip(*pairs))

    def body(qi_ref, ki_ref, q_ref, k_ref, v_ref, o_ref, qb_ref, m_ref, acc_ref):
        qi, ki = qi_ref[pl.program_id(0)], ki_ref[pl.program_id(0)]

        @pl.when(ki == 0)
        def _():
            m_ref[...] = jnp.full_like(m_ref, NEG)
            acc_ref[...] = jnp.zeros_like(acc_ref)

        def step(masked):
            if masked:
                keep = lax.broadcasted_iota(jnp.int32, (tb, tb), 1) <= lax.broadcasted_iota(jnp.int32, (tb, tb), 0)
            def finish(h, pe, alpha):
                acc_ref[h] = acc_ref[h] * alpha + _dot(pe, v_ref[h])

            nxt, pending = _dot_nt(q_ref[0], k_ref[0]), None
            for h in range(N_HEADS):
                sc = nxt
                if h + 1 < N_HEADS:
                    nxt = _dot_nt(q_ref[h + 1], k_ref[h + 1])
                if masked:
                    sc = jnp.where(keep, sc, NEG)
                m_prev = m_ref[h]
                m_new = jnp.maximum(m_prev, jnp.max(sc, axis=1, keepdims=True))
                pe = jnp.exp2(sc - jnp.tile(m_new, (1, tb // LANES))).astype(BF16)
                m_ref[h] = m_new
                if pending is not None:
                    finish(*pending)
                pending = (h, pe, jnp.exp2(m_prev - m_new))
            finish(*pending)

        @pl.when(ki < qi)
        def _():
            step(False)

        @pl.when(ki == qi)
        def _():
            step(True)
            lane = lax.broadcasted_iota(jnp.int32, (tb, LANES), 1)
            for p in range(N_HEADS // 2):
                outs = []
                for h in (2 * p, 2 * p + 1):
                    acc = acc_ref[h]
                    l = acc[:, V_AUX:V_AUX + 1]
                    outs.append(acc * (1.0 / l))
                    hi, lo = _hi_lo(m_ref[h][:, 0:1] + jnp.log(l) * LOG2E)
                    qb_ref[h] = _lane_pair((tb, LANES), QK_AUX, hi, lo, q_ref[h].astype(F32)).astype(BF16)
                o_ref[:, p * LANES:(p + 1) * LANES] = jnp.where(lane < HEAD, outs[0], pltpu.roll(outs[1], HEAD, 1)).astype(BF16)

    sd = jax.ShapeDtypeStruct
    qspec = pl.BlockSpec((N_HEADS, tb, LANES), lambda p, qi_ref, ki_ref: (0, qi_ref[p], 0))
    kspec = pl.BlockSpec((N_HEADS, tb, LANES), lambda p, qi_ref, ki_ref: (0, ki_ref[p], 0))
    return pl.pallas_call(
        body, name="mla_flash_fwd",
        grid_spec=pltpu.PrefetchScalarGridSpec(
            num_scalar_prefetch=2, grid=(len(pairs),),
            in_specs=[qspec, kspec, kspec],
            out_specs=[pl.BlockSpec((tb, MLA_W), lambda p, qi_ref, ki_ref: (qi_ref[p], 0)), qspec],
            scratch_shapes=[pltpu.VMEM((N_HEADS, tb, LANES), F32), pltpu.VMEM((N_HEADS, tb, LANES), F32)]),
        out_shape=[sd((s, MLA_W), BF16), sd((N_HEADS, s, LANES), BF16)],
        compiler_params=_cp("arbitrary"),
    )(qi_of, ki_of, q, k, v)


def _out_proj_call(x, yret, ymla, wout, ts):
    s = x.shape[0]

    def body(x_ref, yr_ref, ym_ref, w_ref, x1_ref, r_ref):
        x1 = x_ref[...] + _dot(yr_ref[...], w_ref[0:RET_W, :]) + _dot(ym_ref[...], w_ref[RET_W:, :])
        x1_ref[...] = x1
        r_ref[...] = _rstd(x1)

    sd = jax.ShapeDtypeStruct
    return pl.pallas_call(
        body, name="out_proj", grid=(s // ts,),
        in_specs=[_row(ts, D_MODEL), _row(ts, RET_W), _row(ts, MLA_W), _full((D_MODEL, D_MODEL))],
        out_specs=[_row(ts, D_MODEL), _row(ts, 1)],
        out_shape=[sd((s, D_MODEL), F32), sd((s, 1), F32)],
        compiler_params=_cp("parallel"),
    )(x, yret, ymla, wout)


W_UP_SHARD = F2 // 4


def _ffn_fwd_call(x1, r2, fnw, wup4, cw, cb, wdown, ts):
    s = x1.shape[0]
    wsh = W_UP_SHARD

    def body(x_ref, r_ref, fnw_ref, wup_ref, cw_ref, cb_ref, wd_ref, u_ref, uc_ref, x2_ref, carry_ref):
        _zero_first(pl.program_id(0) == 0, carry_ref)
        xv = x_ref[...]
        h = (xv * r_ref[...] * fnw_ref[...]).astype(BF16)
        conv = []
        for j in range(4):
            cols = slice(j * wsh, (j + 1) * wsh)
            ub = _dot(h, wup_ref[j]).astype(BF16)
            u_ref[:, cols] = ub
            u = ub.astype(F32)
            u1, u2 = _shifted(u, carry_ref[:, cols])
            w = cw_ref[:, cols]
            cb16 = (cb_ref[:, cols] + w[0:1, :] * u2 + w[1:2, :] * u1 + w[2:3, :] * u).astype(BF16)
            uc_ref[:, cols] = cb16
            conv.append(cb16.astype(F32))
            carry_ref[:, cols] = u[ts - 8:, :]
        acc = xv
        for j in range(2):
            a = (_silu(conv[j]) * conv[j + 2]).astype(BF16)
            acc = acc + _dot(a, wd_ref[j * wsh:(j + 1) * wsh, :])
        x2_ref[...] = acc

    sd = jax.ShapeDtypeStruct
    return pl.pallas_call(
        body, name="ffn_fwd", grid=(s // ts,),
        in_specs=[_row(ts, D_MODEL), _row(ts, 1), _full((1, D_MODEL)), _full((4, D_MODEL, wsh)),
                  _full((3, F2)), _full((1, F2)), _full((D_FF, D_MODEL))],
        out_specs=[_row(ts, F2), _row(ts, F2), _row(ts, D_MODEL)],
        out_shape=[sd((s, F2), BF16), sd((s, F2), BF16), sd((s, D_MODEL), F32)],
        scratch_shapes=[pltpu.VMEM((8, F2), F32)],
        compiler_params=_cp("arbitrary"),
    )(x1, r2, fnw, wup4, cw, cb, wdown)


def _shifted(u, hal):
    row = lax.broadcasted_iota(jnp.int32, hal.shape, 0)
    r1, r2 = pltpu.roll(u, 1, 0), pltpu.roll(u, 2, 0)
    top1 = jnp.where(row == 0, hal[7:8, :], r1[0:8, :])
    top2 = jnp.where(row == 0, hal[6:7, :], jnp.where(row == 1, hal[7:8, :], r2[0:8, :]))
    return jnp.concatenate([top1, r1[8:, :]], axis=0), jnp.concatenate([top2, r2[8:, :]], axis=0)


def _prep_weights(w):
    win = w["w_in"]
    pad = lambda n: jnp.zeros((D_MODEL, n), win.dtype)
    win_ext = jnp.concatenate([win[:, :IN_W - ROPE], pad(KPE_LO), win[:, IN_W - ROPE:], pad(LANES - KPE_LO - ROPE)], -1)
    wuq = w["w_uq"].reshape(Q_RANK, N_HEADS, HEAD + ROPE)
    wq = jnp.concatenate([wuq, jnp.zeros((Q_RANK, N_HEADS, LANES - HEAD - ROPE), wuq.dtype)], -1).transpose(1, 0, 2)
    wukv = w["w_ukv"].reshape(KV_RANK, N_HEADS, 2 * HEAD)
    zk = jnp.zeros((KV_RANK, N_HEADS, HEAD), wukv.dtype)
    wk = jnp.concatenate([wukv[:, :, :HEAD], zk], -1).transpose(1, 0, 2)
    wv = jnp.concatenate([wukv[:, :, HEAD:], zk], -1).transpose(1, 0, 2)
    c = lambda a: a.astype(BF16)
    return dict(win=c(win_ext), wq=c(wq), wk=c(wk), wv=c(wv), wout=c(w["w_out"]))


def _prep_mlp_weights(w):
    wup = w["w_up"]
    if wup.ndim == 2:
        wup = wup.reshape(D_MODEL, 4, W_UP_SHARD).transpose(1, 0, 2)
    return dict(wup=wup.astype(BF16), wdown=w["w_down"].astype(BF16))


def _tiles(s):
    return dict(ts=min(s, 512), tr=min(s, 1024), tb=min(s, 512), t2=min(s, 256))


class _Exchanges:
    def __init__(self, w):
        self.w = w

    def mlp_weights(self, after):
        return self.w

    def mlp_grads(self, gw):
        pass

    def behind_out_bwd(self, after):
        pass

    def behind_attention(self, after):
        pass


def _forward(x, positions, w, small, ex):
    s = x.shape[0]
    t = _tiles(s)
    pw = _prep_weights(w)
    cos_r, sin_r, cos_m, sin_m = _rope_tables(positions)
    rc = _ret_consts()
    q, k, v, g, cq, ckv, kpe, r1 = _f1_call(x, small["attn_norm_w"], pw["win"], cos_r, sin_r, cos_m, sin_m, t["ts"])
    o_ret, y_ret = _ret_fwd_call(q, k, v, g, small["ret_gn_w"], rc, t["tr"])
    mq, mk, mv = _mla_pre_call(cq, ckv, kpe, small["mla_q_norm_w"], small["mla_kv_norm_w"],
                               pw["wq"], pw["wk"], pw["wv"], cos_m, sin_m, t["ts"])
    y_mla, mqb = _flash_fwd_call(mq, mk, mv, t["tb"])
    x1, r2 = _out_proj_call(x, y_ret, y_mla, pw["wout"], t["ts"])
    pw.update(_prep_mlp_weights(ex.mlp_weights(r2)))
    u, uc, x2 = _ffn_fwd_call(x1, r2, small["ffn_norm_w"], pw["wup"], w["conv_w"], small["conv_b"], pw["wdown"], t["ts"])
    return dict(pw=pw, tabs=(cos_r, sin_r, cos_m, sin_m), rc=rc, q=q, k=k, v=v, g=g, cq=cq, ckv=ckv, kpe=kpe, r1=r1,
                o_ret=o_ret, y_ret=y_ret, mqb=mqb, mk=mk, mv=mv, y_mla=y_mla, x1=x1, r2=r2, u=u, uc=uc, x2=x2)


def _norm_bwd(dh, xh, r, nw):
    dxn = dh * nw
    return r * (dxn - xh * jnp.mean(dxn * xh, axis=-1, keepdims=True))


def _ordered_after(body, order):
    if order is None:
        return body, [], []
    return (lambda order_ref, *refs: body(*refs)), [pl.BlockSpec(memory_space=pl.ANY)], [order]


def _zero_first(first, *refs):
    @pl.when(first)
    def _():
        for ref in refs:
            ref[...] = jnp.zeros_like(ref)


def _colsum(v):
    return jnp.sum(v, axis=0, keepdims=True)


def _dsilu(g, sg):
    return sg * (1.0 + g * (1.0 - sg))


def _loss_call(x2, tgt, fw, ts):
    s = x2.shape[0]

    def body(x_ref, t_ref, fw_ref, dx_ref, loss_ref, gfw_ref):
        _zero_first(pl.program_id(0) == 0, loss_ref, gfw_ref)
        xv = x_ref[...]
        r = _rstd(xv)
        xh = xv * r
        fwv = fw_ref[...]
        e = xh * fwv - t_ref[...]
        loss_ref[...] += (0.5 / D_MODEL) * _colsum(jnp.sum(e * e, axis=1, keepdims=True))
        dy = e * (1.0 / D_MODEL)
        gfw_ref[...] += _colsum(dy * xh)
        dx_ref[...] = _norm_bwd(dy, xh, r, fwv)

    sd = jax.ShapeDtypeStruct
    return pl.pallas_call(
        body, name="loss_bwd", grid=(s // ts,),
        in_specs=[_row(ts, D_MODEL), _row(ts, D_MODEL), _full((1, D_MODEL))],
        out_specs=[_row(ts, D_MODEL), _full((1, 1)), _full((1, D_MODEL))],
        out_shape=[sd((s, D_MODEL), F32), sd((1, 1), F32), sd((1, D_MODEL), F32)],
        compiler_params=_cp("arbitrary"),
    )(x2, tgt, fw)


def _ffn_bwd_call(dx2, u, uc, cw, wdown, wup4, x1, r2, fnw, ts):
    s = dx2.shape[0]
    nt = s // ts
    wsh = W_UP_SHARD
    rev = lambda i: nt - 1 - i

    def body(dx2_ref, u_ref, uc_ref, cw_ref, wd_ref, wup_ref, x_ref, r_ref, fnw_ref,
             du_ref, dx1_ref, dcw_ref, dcb_ref, dfnw_ref, dwd_hbm, carry_ref, dwd_ref, sem):
        i = pl.program_id(0)
        _zero_first(i == 0, carry_ref, dwd_ref, dcw_ref, dcb_ref, dfnw_ref)
        dxb = dx2_ref[...].astype(BF16)
        dh = jnp.zeros((ts, D_MODEL), F32)
        for j in range(2):
            gcols = slice(j * wsh, (j + 1) * wsh)
            vcols = slice(D_FF + j * wsh, D_FF + (j + 1) * wsh)
            gate, val = uc_ref[:, gcols].astype(F32), uc_ref[:, vcols].astype(F32)
            da = _dot_nt(dxb, wd_ref[gcols, :])
            sg = _sigmoid(gate)
            sl = gate * sg
            dwd_ref[gcols, :] += _dot_tn((sl * val).astype(BF16), dxb)
            for d, cols, shard in ((da * val * _dsilu(gate, sg), gcols, j), (da * sl, vcols, 2 + j)):
                d1, d2 = _shifted_up(d, carry_ref[:, cols])
                uv = u_ref[:, cols].astype(F32)
                for t, dt in enumerate((d2, d1, d)):
                    dcw_ref[t:t + 1, cols] += _colsum(dt * uv)
                dcb_ref[:, cols] += _colsum(d)
                w = cw_ref[:, cols]
                du = (w[2:3, :] * d + w[1:2, :] * d1 + w[0:1, :] * d2).astype(BF16)
                du_ref[:, cols] = du
                dh = dh + _dot_nt(du, wup_ref[shard])
                carry_ref[:, cols] = d[0:8, :]
        r = r_ref[...]
        xh = x_ref[...] * r
        dfnw_ref[...] += _colsum(dh * xh)
        dx1_ref[...] = dx2_ref[...] + _norm_bwd(dh, xh, r, fnw_ref[...])

        @pl.when(i == nt - 1)
        def _():
            cp = pltpu.make_async_copy(dwd_ref, dwd_hbm, sem)
            cp.start()
            cp.wait()

    sd = jax.ShapeDtypeStruct
    row = lambda c: pl.BlockSpec((ts, c), lambda i: (rev(i), 0))
    once = lambda shape: pl.BlockSpec(shape, lambda i: (0,) * len(shape), pipeline_mode=pl.Buffered(1))
    return pl.pallas_call(
        body, name="ffn_bwd", grid=(nt,),
        in_specs=[row(D_MODEL), row(F2), row(F2), once((3, F2)), once((D_FF, D_MODEL)), once((4, D_MODEL, wsh)),
                  row(D_MODEL), row(1), once((1, D_MODEL))],
        out_specs=[row(F2), row(D_MODEL), _full((3, F2)), _full((1, F2)), _full((1, D_MODEL)), pl.BlockSpec(memory_space=pl.ANY)],
        out_shape=[sd((s, F2), BF16), sd((s, D_MODEL), F32), sd((3, F2), F32), sd((1, F2), F32), sd((1, D_MODEL), F32),
                   sd((D_FF, D_MODEL), F32)],
        scratch_shapes=[pltpu.VMEM((8, F2), F32), pltpu.VMEM((D_FF, D_MODEL), F32), pltpu.SemaphoreType.DMA],
        compiler_params=_cp("arbitrary", vmem=VMEM_LIMIT_MLP_BWD),
    )(dx2, u, uc, cw, wdown, wup4, x1, r2, fnw)


def _shifted_up(d, hal):
    n = d.shape[0]
    row = lax.broadcasted_iota(jnp.int32, hal.shape, 0)
    r1, r2 = pltpu.roll(d, n - 1, 0), pltpu.roll(d, n - 2, 0)
    end1 = jnp.where(row == 7, hal[0:1, :], r1[n - 8:, :])
    end2 = jnp.where(row == 6, hal[0:1, :], jnp.where(row == 7, hal[1:2, :], r2[n - 8:, :]))
    return jnp.concatenate([r1[:n - 8, :], end1], axis=0), jnp.concatenate([r2[:n - 8, :], end2], axis=0)


def _dw_norm_call(x, r, nw, b, ts, tn, name):
    s, n = b.shape
    k = x.shape[1]

    def body(x_ref, r_ref, nw_ref, b_ref, dw_ref):
        _zero_first(pl.program_id(1) == 0, dw_ref)
        h = (x_ref[...] * r_ref[...] * nw_ref[...]).astype(BF16)
        dw_ref[...] += _dot_tn(h, b_ref[...])

    return pl.pallas_call(
        body, name=name, grid=(n // tn, s // ts),
        in_specs=[pl.BlockSpec((ts, k), lambda j, i: (i, 0)), pl.BlockSpec((ts, 1), lambda j, i: (i, 0)),
                  pl.BlockSpec((1, k), lambda j, i: (0, 0)), pl.BlockSpec((ts, tn), lambda j, i: (i, j))],
        out_specs=pl.BlockSpec((None, k, tn), lambda j, i: (j, 0, 0)),
        out_shape=jax.ShapeDtypeStruct((n // tn, k, tn), F32),
        compiler_params=_cp("parallel", "arbitrary"),
    )(x, r, nw, b)


def _out_bwd_call(dx1, yret, ymla, wout, ts, order=None):
    s = dx1.shape[0]

    def body(dx_ref, yr_ref, ym_ref, w_ref, dyr_ref, do_ref, dwo_ref):
        _zero_first(pl.program_id(0) == 0, dwo_ref)
        dxb = dx_ref[...].astype(BF16)
        dmix = _dot_nt(dxb, w_ref[...])
        dyr_ref[...] = dmix[:, :RET_W]
        ym = ym_ref[...]
        lane = lax.broadcasted_iota(jnp.int32, (ts, LANES), 1)
        for p in range(N_HEADS // 2):
            dom = dmix[:, RET_W + p * LANES:RET_W + (p + 1) * LANES]
            prod = dom * ym[:, p * LANES:(p + 1) * LANES].astype(F32)
            for hh in range(2):
                mine = (lane >= HEAD) if hh else (lane < HEAD)
                hi, lo = _hi_lo(jnp.sum(jnp.where(mine, prod, 0.0), axis=1, keepdims=True))
                base = jnp.where(lane < HEAD, pltpu.roll(dom, HEAD, 1) if hh else dom, 0.0)
                do_ref[2 * p + hh] = _lane_pair((ts, LANES), V_AUX, -hi, -lo, base).astype(BF16)
        dwo_ref[0:RET_W, :] += _dot_tn(yr_ref[...], dxb)
        dwo_ref[RET_W:, :] += _dot_tn(ym, dxb)

    sd = jax.ShapeDtypeStruct
    body, first_specs, first = _ordered_after(body, order)
    return pl.pallas_call(
        body, name="out_proj_bwd", grid=(s // ts,),
        in_specs=first_specs + [_row(ts, D_MODEL), _row(ts, RET_W), _row(ts, MLA_W), _full((D_MODEL, D_MODEL))],
        out_specs=[_row(ts, RET_W), _hrow(N_HEADS, ts, LANES), _full((D_MODEL, D_MODEL))],
        out_shape=[sd((s, RET_W), F32), sd((N_HEADS, s, LANES), BF16), sd((D_MODEL, D_MODEL), F32)],
        compiler_params=_cp("arbitrary"),
    )(*first, dx1, yret, ymla, wout)


def _ret_bwd_q_call(q, k, v, o, g, dy, gnw, rc, cos_r, sin_r, tr):
    s = q.shape[0]
    c = RET_CHUNK
    nc = tr // c
    ns = RET_SLABS

    def body(q_ref, k_ref, v_ref, o_ref, g_ref, dy_ref, gnw_ref, dm_ref, zeta_ref, xi_ref, cd_ref, bd_ref, cr_ref, sr_ref,
             dq_ref, dg_ref, do_ref, dgnw_ref, st_ref):
        _zero_first(pl.program_id(1) == 0, st_ref, dgnw_ref)
        bd = bd_ref[...]
        avg = bd * (1.0 / HEAD)
        chunks = [slice(ci * c, (ci + 1) * c) for ci in range(nc)]
        lanes = [slice(sl * LANES, (sl + 1) * LANES) for sl in range(ns)]
        dov = []
        for ln in lanes:
            ov = o_ref[:, ln]
            ctr = ov - _dot_hi(ov, avg)
            rs = lax.rsqrt(_dot_hi(ctr * ctr, avg) + EPS)
            oh = ctr * rs
            gg, dyv, gnw_v = g_ref[:, ln], dy_ref[:, ln], gnw_ref[:, ln]
            sg = _sigmoid(gg)
            sl = gg * sg
            dg_ref[:, ln] = (dyv * oh * gnw_v * _dsilu(gg, sg)).astype(BF16)
            dgnw_ref[:, ln] += _colsum(dyv * sl * oh)
            doh = dyv * sl * gnw_v
            dov.append((rs * (doh - _dot_hi(doh, avg) - oh * _dot_hi(doh * oh, avg))).astype(BF16))
            do_ref[:, ln] = dov[-1]
        states = _ret_states(k_ref, v_ref, zeta_ref, cd_ref, bd, st_ref, chunks, lanes, False)
        for ci, rows in enumerate(chunks):
            for sl, ln in enumerate(lanes):
                doc = dov[sl][rows, :]
                dq = (_dot_nt(doc, states[sl][ci]) * xi_ref[sl]
                      + _pair_product(doc, _stack_heads(v_ref[rows, ln]), dm_ref[sl], _stack_heads(k_ref[rows, ln])))
                dq_ref[rows, ln] = _unrope(dq, cr_ref[rows, :], sr_ref[rows, :], HEAD // 2).astype(BF16)

    specs = _ret_specs(tr, lambda i: i)
    sd = jax.ShapeDtypeStruct
    return pl.pallas_call(
        body, name="ret_bwd_q", grid=(4 // ns, s // tr),
        in_specs=[specs["slab"]] * 6 + [specs["vec"], specs["dmask"], specs["rows"], specs["rows"], specs["state"], specs["bd"],
                                        specs["tab"], specs["tab"]],
        out_specs=[specs["slab"]] * 3 + [specs["vec"]],
        out_shape=[sd((s, RET_W), BF16), sd((s, RET_W), BF16), sd((s, RET_W), BF16), sd((1, RET_W), F32)],
        scratch_shapes=[pltpu.VMEM((ns, LANES, LANES), F32)],
        compiler_params=_cp("parallel", "arbitrary"),
    )(q, k, v, o, g, dy, gnw, rc["dmask"], rc["zeta"], rc["xi"], rc["cd"], rc["bd"], cos_r, sin_r)


def _ret_bwd_kv_call(q, k, v, do, rc, cos_r, sin_r, tr):
    s = q.shape[0]
    c = RET_CHUNK
    nc = tr // c
    nt = s // tr
    ns = RET_SLABS

    def body(q_ref, k_ref, v_ref, do_ref, dm_ref, zeta_ref, xi_ref, cd_ref, bd_ref, cr_ref, sr_ref, dk_ref, dv_ref, gs_ref):
        _zero_first(pl.program_id(1) == 0, gs_ref)
        bd = bd_ref[...]
        chunks = [slice(ci * c, (ci + 1) * c) for ci in range(nc)]
        lanes = [slice(sl * LANES, (sl + 1) * LANES) for sl in range(ns)]
        states = _ret_states(q_ref, do_ref, xi_ref, cd_ref, bd, gs_ref, chunks, lanes, True)
        for ci, rows in enumerate(chunks):
            for sl, ln in enumerate(lanes):
                kc, vc = k_ref[rows, ln], v_ref[rows, ln]
                q2, do2 = _stack_heads(q_ref[rows, ln]), _stack_heads(do_ref[rows, ln])
                gb = states[sl][ci]
                dk = _dot_nt(vc, gb) * zeta_ref[sl] + _pair_product(vc, do2, dm_ref[sl], q2)
                dv = _dot(kc, gb) * zeta_ref[sl] + _pair_product(kc, q2, dm_ref[sl], do2)
                dk_ref[rows, ln] = (_unrope(dk, cr_ref[rows, :], sr_ref[rows, :], HEAD // 2) * (HEAD ** -0.5)).astype(BF16)
                dv_ref[rows, ln] = dv.astype(BF16)

    specs = _ret_specs(tr, lambda i: nt - 1 - i)
    sd = jax.ShapeDtypeStruct
    return pl.pallas_call(
        body, name="ret_bwd_kv", grid=(4 // ns, nt),
        in_specs=[specs["slab"]] * 4 + [specs["dmask"], specs["rows"], specs["rows"], specs["state"], specs["bd"],
                                        specs["tab"], specs["tab"]],
        out_specs=[specs["slab"]] * 2,
        out_shape=[sd((s, RET_W), BF16), sd((s, RET_W), BF16)],
        scratch_shapes=[pltpu.VMEM((ns, LANES, LANES), F32)],
        compiler_params=_cp("parallel", "arbitrary"),
    )(q, k, v, do, rc["dmask_t"], rc["zeta"], rc["xi"], rc["cd"], rc["bd"], cos_r, sin_r)


FLASH_BWD_HEADS = 8


def _flash_bwd_call(qb, k, v, do, tb, order=None):
    s = qb.shape[1]
    nb = s // tb
    hg = FLASH_BWD_HEADS
    pairs = [(a, b) for a in range(nb) for b in range(a, nb)]
    ki_of, qi_of = (jnp.asarray(np.array(col, np.int32)) for col in zip(*pairs))
    extra = [] if order is None else [order]

    def body(ki_ref, qi_ref, *refs):
        q_ref, k_ref, v_ref, do_ref, dk_ref, dv_ref, dq_hbm, dka_ref, dva_ref, dq_ref, sem = refs[len(extra):]
        g, p = pl.program_id(0), pl.program_id(1)
        ki, qi = ki_ref[p], qi_ref[p]
        _zero_first(p == 0, dq_ref)
        _zero_first(qi == ki, dka_ref, dva_ref)
        rows = pl.ds(pl.multiple_of(qi * tb, tb), tb)

        def step(masked):
            if masked:
                keep = lax.broadcasted_iota(jnp.int32, (tb, tb), 0) <= lax.broadcasted_iota(jnp.int32, (tb, tb), 1)
            for h in range(hg):
                st = _dot_nt(k_ref[h], q_ref[h])
                if masked:
                    st = jnp.where(keep, st, NEG)
                pt = jnp.exp2(st)
                dob = do_ref[h]
                dva_ref[h] += _dot(pt.astype(BF16), dob)
                dst = (pt * _dot_nt(v_ref[h], dob)).astype(BF16)
                dka_ref[h] += _dot(dst, q_ref[h])
                dq_ref[h, rows, :] += _dot_tn(dst, k_ref[h])

        @pl.when(qi > ki)
        def _():
            step(False)

        @pl.when(qi == ki)
        def _():
            step(True)

        @pl.when(qi == nb - 1)
        def _():
            dk_ref[...] = (dka_ref[...] * LN2).astype(BF16)
            dv_ref[...] = dva_ref[...].astype(BF16)

        @pl.when(p == len(pairs) - 1)
        def _():
            cp = pltpu.make_async_copy(dq_ref, dq_hbm.at[pl.ds(g * hg, hg)], sem)
            cp.start()
            cp.wait()

    kspec = pl.BlockSpec((hg, tb, LANES), lambda g, p, ki_ref, qi_ref: (g, ki_ref[p], 0))
    qspec = pl.BlockSpec((hg, tb, LANES), lambda g, p, ki_ref, qi_ref: (g, qi_ref[p], 0))
    hm = jax.ShapeDtypeStruct((N_HEADS, s, LANES), BF16)
    return pl.pallas_call(
        body, name="mla_flash_bwd",
        grid_spec=pltpu.PrefetchScalarGridSpec(
            num_scalar_prefetch=2, grid=(N_HEADS // hg, len(pairs)),
            in_specs=[ANY] * len(extra) + [qspec, kspec, kspec, qspec],
            out_specs=[kspec, kspec, ANY],
            scratch_shapes=[pltpu.VMEM((hg, tb, LANES), F32), pltpu.VMEM((hg, tb, LANES), F32),
                            pltpu.VMEM((hg, s, LANES), F32), pltpu.SemaphoreType.DMA]),
        out_shape=[hm, hm, jax.ShapeDtypeStruct((N_HEADS, s, LANES), F32)],
        compiler_params=_cp("arbitrary", "arbitrary"),
    )(ki_of, qi_of, *extra, qb, k, v, do)


def _mla_post_call(dq, dk, dv, cq, ckv, qnw, kvnw, wq, wk, wv, cos_m, sin_m, ts):
    s = cq.shape[0]

    def body(dq_ref, dk_ref, dv_ref, cq_ref, ckv_ref, qnw_ref, kvnw_ref, wq_ref, wk_ref, wv_ref, cm_ref, sm_ref,
             dcq_ref, dckv_ref, dkpe_ref, dwq_ref, dwk_ref, dwv_ref, dqnw_ref, dkvnw_ref):
        _zero_first(pl.program_id(0) == 0, dwq_ref, dwk_ref, dwv_ref, dqnw_ref, dkvnw_ref)
        cqv, ckvv = cq_ref[...], ckv_ref[...]
        rq, rkv = _rstd(cqv), _rstd(ckvv)
        qh_, kvh_ = cqv * rq, ckvv * rkv
        qnw_v, kvnw_v = qnw_ref[...], kvnw_ref[...]
        cqn = (qh_ * qnw_v).astype(BF16)
        ckvn = (kvh_ * kvnw_v).astype(BF16)
        cm, sm = cm_ref[...], sm_ref[...]
        dcqn = jnp.zeros((ts, Q_RANK), F32)
        dckvn = jnp.zeros((ts, KV_RANK), F32)
        dkpe = jnp.zeros((ts, LANES), F32)
        for h in range(N_HEADS):
            dqu = _unrope(dq_ref[h] * SM_SCALE, cm, sm, ROPE // 2).astype(BF16)
            dwq_ref[h] += _dot_tn(cqn, dqu)
            dcqn = dcqn + _dot_nt(dqu, wq_ref[h])
            dkb, dvb = dk_ref[h], dv_ref[h]
            dkpe = dkpe + dkb.astype(F32)
            dwk_ref[h] += _dot_tn(ckvn, dkb)
            dwv_ref[h] += _dot_tn(ckvn, dvb)
            dckvn = dckvn + _dot_nt(dkb, wk_ref[h]) + _dot_nt(dvb, wv_ref[h])
        lane = lax.broadcasted_iota(jnp.int32, (ts, LANES), 1)
        dkpe = jnp.where((lane >= KPE_LO) & (lane < KPE_LO + ROPE), dkpe, 0.0)
        dkpe_ref[...] = _unrope(dkpe, cm, sm, ROPE // 2).astype(BF16)
        dqnw_ref[...] += _colsum(dcqn * qh_)
        dkvnw_ref[...] += _colsum(dckvn * kvh_)
        dcq_ref[...] = _norm_bwd(dcqn, qh_, rq, qnw_v).astype(BF16)
        dckv_ref[...] = _norm_bwd(dckvn, kvh_, rkv, kvnw_v).astype(BF16)

    sd = jax.ShapeDtypeStruct
    hm = _hrow(N_HEADS, ts, LANES)
    return pl.pallas_call(
        body, name="mla_post", grid=(s // ts,),
        in_specs=[hm, hm, hm, _row(ts, Q_RANK), _row(ts, KV_RANK), _full((1, Q_RANK)), _full((1, KV_RANK)),
                  _full((N_HEADS, Q_RANK, LANES)), _full((N_HEADS, KV_RANK, LANES)), _full((N_HEADS, KV_RANK, LANES)),
                  _row(ts, LANES), _row(ts, LANES)],
        out_specs=[_row(ts, Q_RANK), _row(ts, KV_RANK), _row(ts, LANES),
                   _full((N_HEADS, Q_RANK, LANES)), _full((N_HEADS, KV_RANK, LANES)), _full((N_HEADS, KV_RANK, LANES)),
                   _full((1, Q_RANK)), _full((1, KV_RANK))],
        out_shape=[sd((s, Q_RANK), BF16), sd((s, KV_RANK), BF16), sd((s, LANES), BF16),
                   sd((N_HEADS, Q_RANK, LANES), F32), sd((N_HEADS, KV_RANK, LANES), F32), sd((N_HEADS, KV_RANK, LANES), F32),
                   sd((1, Q_RANK), F32), sd((1, KV_RANK), F32)],
        compiler_params=_cp("arbitrary"),
    )(dq, dk, dv, cq, ckv, qnw, kvnw, wq, wk, wv, cos_m, sin_m)


def _in_bwd_call(parts, x, r1, anw, dx1, win, ts):
    s = x.shape[0]
    widths = [p.shape[1] for p in parts]
    np_ = len(parts)

    def body(*refs):
        p_refs = refs[:np_]
        x_ref, r_ref, anw_ref, dx1_ref, w_ref, dx_ref, dw_ref, danw_ref = refs[np_:]
        _zero_first(pl.program_id(0) == 0, dw_ref, danw_ref)
        dproj = jnp.concatenate([p[...] for p in p_refs], axis=-1)
        r, anw_v = r_ref[...], anw_ref[...]
        xh = x_ref[...] * r
        dw_ref[...] += _dot_tn((xh * anw_v).astype(BF16), dproj)
        dh = _dot_nt(dproj, w_ref[...])
        danw_ref[...] += _colsum(dh * xh)
        dx_ref[...] = dx1_ref[...] + _norm_bwd(dh, xh, r, anw_v)

    sd = jax.ShapeDtypeStruct
    return pl.pallas_call(
        body, name="in_proj_bwd", grid=(s // ts,),
        in_specs=[_row(ts, w) for w in widths]
        + [_row(ts, D_MODEL), _row(ts, 1), _full((1, D_MODEL)), _row(ts, D_MODEL), _full((D_MODEL, IN_EXT))],
        out_specs=[_row(ts, D_MODEL), _full((D_MODEL, IN_EXT)), _full((1, D_MODEL))],
        out_shape=[sd((s, D_MODEL), F32), sd((D_MODEL, IN_EXT), F32), sd((1, D_MODEL), F32)],
        compiler_params=_cp("arbitrary"),
    )(*parts, x, r1, anw, dx1, win)


def _local_step(x, positions, tgt, w, small, ex=None):
    s = x.shape[0]
    t = _tiles(s)
    ex = _Exchanges(w) if ex is None else ex
    f = _forward(x, positions, w, small, ex)
    pw, rc = f["pw"], f["rc"]
    cos_r, sin_r, cos_m, sin_m = f["tabs"]
    dx2, loss, g_fw = _loss_call(f["x2"], tgt, small["final_norm_w"], t["ts"])
    du, dx1, g_cw, g_cb, g_fnw, g_wd = _ffn_bwd_call(dx2, f["u"], f["uc"], w["conv_w"], pw["wdown"], pw["wup"],
                                                     f["x1"], f["r2"], small["ffn_norm_w"], t["t2"])
    g_wup = _dw_norm_call(f["x1"], f["r2"], small["ffn_norm_w"], du, t["ts"], F2 // 4, "dw_up")
    started = ex.mlp_grads(dict(w_up=g_wup, w_down=g_wd))
    dy_ret, do, g_wout = _out_bwd_call(dx1, f["y_ret"], f["y_mla"], pw["wout"], t["ts"], started)
    started = ex.behind_out_bwd(g_wout)
    drq, dg, do_ret, g_gnw = _ret_bwd_q_call(f["q"], f["k"], f["v"], f["o_ret"], f["g"], dy_ret, small["ret_gn_w"], rc, cos_r, sin_r, t["tr"])
    drk, drv = _ret_bwd_kv_call(f["q"], f["k"], f["v"], do_ret, rc, cos_r, sin_r, t["tr"])
    dmk, dmv, dmq = _flash_bwd_call(f["mqb"], f["mk"], f["mv"], do, t["tb"], started)
    ex.behind_attention(dmk)
    dcq, dckv, dkpe, g_wq, g_wk, g_wv, g_qnw, g_kvnw = _mla_post_call(
        dmq, dmk, dmv, f["cq"], f["ckv"], small["mla_q_norm_w"], small["mla_kv_norm_w"], pw["wq"], pw["wk"], pw["wv"], cos_m, sin_m, t["ts"])
    gx, g_win_ext, g_anw = _in_bwd_call([drq, drk, drv, dg, dcq, dckv, dkpe], x, f["r1"], small["attn_norm_w"], dx1, pw["win"], t["ts"])
    lo = IN_W - ROPE
    g_win = jnp.concatenate([g_win_ext[:, :lo], g_win_ext[:, lo + KPE_LO:lo + KPE_LO + ROPE]], -1)
    g_wuq = g_wq.transpose(1, 0, 2)[:, :, :HEAD + ROPE].reshape(Q_RANK, N_HEADS * (HEAD + ROPE))
    g_wukv = jnp.concatenate([g_wk[:, :, :HEAD], g_wv[:, :, :HEAD]], -1).transpose(1, 0, 2).reshape(KV_RANK, 2 * MLA_W)
    gw = dict(w_in=g_win, w_uq=g_wuq, w_ukv=g_wukv, w_out=g_wout, w_up=g_wup,
              conv_w=g_cw, w_down=g_wd)
    gs = dict(attn_norm_w=g_anw, ret_gn_w=g_gnw, mla_q_norm_w=g_qnw, mla_kv_norm_w=g_kvnw, ffn_norm_w=g_fnw,
              conv_b=g_cb, final_norm_w=g_fw)
    return loss, gx, gw, gs


MESH_ID = pl.DeviceIdType.MESH
ANY = pl.BlockSpec(memory_space=pl.ANY)
VMEM_SPEC = pl.BlockSpec(memory_space=pltpu.VMEM)
N_DEV = 8
GROUP_A = (("w_in", (D_MODEL, IN_W // 4), 1), ("w_uq", (Q_RANK, 192), 1), ("w_ukv", (KV_RANK, 256), 1),
           ("w_out", (D_MODEL // 4, D_MODEL), 0))
GROUP_B = (("w_up", (D_MODEL, F2 // 4), 1), ("w_down", (D_FF // 4, D_MODEL), 0))
HBM_SPEC = pl.BlockSpec(memory_space=pltpu.HBM)
SEM_SPEC = pl.BlockSpec(memory_space=pltpu.SEMAPHORE)


def _mesh_pos():
    return lax.axis_index("x"), lax.axis_index("y"), lax.axis_index("c")


def _other_chips(x, y):
    return [(1 - x, y), (x, 1 - y), (1 - x, 1 - y)]


def _remote(src, dst, send_sems, recv_sems, k, dev):
    return pltpu.make_async_remote_copy(src_ref=src, dst_ref=dst, send_sem=send_sems.at[k], recv_sem=recv_sems.at[k],
                                        device_id=dev, device_id_type=MESH_ID)


def _gather_list_call(parts, tag):
    n = len(parts)

    def body(*refs):
        srcs, outs, (send_sems, recv_sems) = refs[:n], refs[n:2 * n], refs[2 * n:]
        x, y, c = _mesh_pos()
        sm = 2 * x + y
        chips = _other_chips(x, y)
        sib = (x, y, 1 - c)
        rc = lambda k, src, dst, dev: _remote(src, dst, send_sems, recv_sems, k, dev)
        first = [rc(7 * i + j, srcs[i].at[c], outs[i].at[sm, c], (cx, cy, c)) for i in range(n) for j, (cx, cy) in enumerate(chips)]
        own = [rc(7 * i + 6, srcs[i], outs[i].at[sm], sib) for i in range(n)]
        for cp in first + own:
            cp.start()
        passed = []
        for j, (cx, cy) in enumerate(chips):
            for i in range(n):
                land = outs[i].at[2 * cx + cy, c]
                rc(7 * i + j, srcs[i].at[c], land, (cx, cy, c)).wait_recv()
                cp = rc(7 * i + 3 + j, land, land, sib)
                cp.start()
                passed.append(cp)
        for j, (cx, cy) in enumerate(chips):
            for i in range(n):
                rc(7 * i + 3 + j, srcs[i].at[c], outs[i].at[2 * cx + cy, 1 - c], sib).wait_recv()
        for cp in own:
            cp.wait_recv()
        for cp in first + passed + own:
            cp.wait_send()

    return pl.pallas_call(
        body, name="weights_all_gather_" + tag,
        in_specs=[ANY] * n, out_specs=[ANY] * n,
        out_shape=[jax.ShapeDtypeStruct((4,) + p.shape, p.dtype) for p in parts],
        scratch_shapes=[pltpu.SemaphoreType.DMA((7 * n,)), pltpu.SemaphoreType.DMA((7 * n,))],
    )(*parts)


def _direct_gather_copies(srcs, lands, send_sems, recv_sems):
    x, y, c = _mesh_pos()
    sm = 2 * x + y
    sends, recvs = [], []
    for i, (src, land) in enumerate(zip(srcs, lands)):
        for j, (cx, cy) in enumerate(_other_chips(x, y)):
            for t in range(2):
                sends.append(_remote(src.at[c], land.at[sm, c], send_sems, recv_sems, 13 * i + 4 * j + 2 * c + t, (cx, cy, t)))
                recvs.append(_remote(src.at[t], land.at[2 * cx + cy, t], send_sems, recv_sems, 13 * i + 4 * j + 2 * t + c, (cx, cy, t)))
        sends.append(_remote(src, land.at[sm], send_sems, recv_sems, 13 * i + 12, (x, y, 1 - c)))
        recvs.append(_remote(src, land.at[sm], send_sems, recv_sems, 13 * i + 12, (x, y, 1 - c)))
    return sends, recvs


def _sibling_copies(srcs, lands, send_sems, recv_sems):
    x, y, c = _mesh_pos()
    cps = [_remote(src.at[s, 1 - c], land.at[s], send_sems, recv_sems, 4 * i + s, (x, y, 1 - c))
           for i, (src, land) in enumerate(zip(srcs, lands)) for s in range(4)]
    return cps, cps


def _chips_copies(srcs, lands, send_sems, recv_sems):
    x, y, c = _mesh_pos()
    cps = [_remote(src.at[2 * cx + cy], land.at[j], send_sems, recv_sems, 3 * i + j, (cx, cy, c))
           for i, (src, land) in enumerate(zip(srcs, lands)) for j, (cx, cy) in enumerate(_other_chips(x, y))]
    return cps, cps


def _share_copies(srcs, lands, send_sems, recv_sems):
    x, y, c = _mesh_pos()
    cps = [_remote(src, land, send_sems, recv_sems, i, (x, y, 1 - c)) for i, (src, land) in enumerate(zip(srcs, lands))]
    return cps, cps


def _exchange_call(name, copies, srcs, land_shapes, n_sems):
    n = len(srcs)

    def body(*refs):
        sends, recvs = copies(refs[:n], refs[n:2 * n], refs[2 * n], refs[2 * n + 1])
        for cp in sends:
            cp.start()
        for cp in sends:
            cp.wait_send()
        for cp in recvs:
            cp.wait_recv()

    return pl.pallas_call(
        body, name=name, in_specs=[ANY] * n, out_specs=[ANY] * n, out_shape=list(land_shapes),
        scratch_shapes=[pltpu.SemaphoreType.DMA((n_sems,)), pltpu.SemaphoreType.DMA((n_sems,))],
    )(*srcs)


def _exchange_start_call(name, copies, srcs, land_shapes, n_sems, order=None):
    n = len(srcs)
    extra = [] if order is None else [order]
    k = 2 * n + len(extra)

    def body(*refs):
        sends, _ = copies(refs[:n], refs[n:2 * n], refs[k], refs[k + 1])
        for cp in sends:
            cp.start()
        refs[-1][...] = jnp.zeros_like(refs[-1])

    hbm = lambda a: pltpu.with_memory_space_constraint(a, pltpu.HBM)
    lands = [hbm(lax.empty(sd.shape, sd.dtype)) for sd in land_shapes]
    sem = pltpu.SemaphoreType.DMA((n_sems,))
    out = pl.pallas_call(
        body, name=name,
        out_shape=(sem, sem, *[pltpu.HBM(a.shape, a.dtype) for a in list(srcs) + lands], jax.ShapeDtypeStruct((8, LANES), F32)),
        in_specs=[HBM_SPEC] * (2 * n) + [ANY] * len(extra), out_specs=(SEM_SPEC, SEM_SPEC, *[HBM_SPEC] * (2 * n), VMEM_SPEC),
        input_output_aliases={i: 2 + i for i in range(2 * n)},
        compiler_params=pltpu.CompilerParams(has_side_effects=pltpu.SideEffectType.DATAFLOW_SIDE_EFFECTING),
    )(*[hbm(a) for a in srcs], *lands, *extra)
    return out[0], out[1], out[2:2 + n], out[2 + n:2 + 2 * n], out[-1]


def _exchange_wait_call(name, copies, started, after):
    send_sems, recv_sems, srcs, lands, _ = started
    n = len(srcs)

    def body(*refs):
        sends, recvs = copies(refs[:n], refs[n:2 * n], refs[2 * n], refs[2 * n + 1])
        for cp in sends:
            cp.wait_send()
        for cp in recvs:
            cp.wait_recv()

    out = pl.pallas_call(
        body, name=name,
        out_shape=tuple(pltpu.HBM(a.shape, a.dtype) for a in list(srcs) + list(lands)),
        in_specs=[HBM_SPEC] * (2 * n) + [SEM_SPEC, SEM_SPEC, ANY], out_specs=tuple([HBM_SPEC] * (2 * n)),
        input_output_aliases={i: i for i in range(2 * n)},
        compiler_params=pltpu.CompilerParams(has_side_effects=pltpu.SideEffectType.DATAFLOW_SIDE_EFFECTING),
    )(*srcs, *lands, send_sems, recv_sems, after)
    return out[:n], out[n:]


def _rows_tile(rows, width, itemsize=4):
    limit = max(16, (3 << 20) // (width * itemsize))
    if rows <= limit:
        return rows
    return max(t for t in range(16, limit + 1, 16) if rows % t == 0)


def _sum_sibling_call(g, buf, c, name):
    _, _, rh, w = g.shape
    tile = _rows_tile(rh, w)

    def body(c_ref, g_ref, b_ref, p_ref, pb_ref):
        p = g_ref[...] + b_ref[...]
        p_ref[...] = p
        pb_ref[...] = p.astype(BF16)

    blk = pl.BlockSpec((None, tile, w), lambda s, i, c_ref: (s, i, 0))
    return pl.pallas_call(
        body, name=name,
        grid_spec=pltpu.PrefetchScalarGridSpec(
            num_scalar_prefetch=1, grid=(4, rh // tile),
            in_specs=[pl.BlockSpec((None, None, tile, w), lambda s, i, c_ref: (s, c_ref[0], i, 0)), blk],
            out_specs=[blk, blk]),
        out_shape=[jax.ShapeDtypeStruct((4, rh, w), F32), jax.ShapeDtypeStruct((4, rh, w), BF16)],
        compiler_params=_cp("parallel", "parallel"),
    )(c, g, buf)


def _sum_chips_call(p, buf, sm, name):
    _, rh, w = p.shape
    tile = _rows_tile(rh, w)

    def body(sm_ref, p_ref, b_ref, f_ref):
        f_ref[...] = ((p_ref[...] + b_ref[0].astype(F32)) + b_ref[1].astype(F32)) + b_ref[2].astype(F32)

    return pl.pallas_call(
        body, name=name,
        grid_spec=pltpu.PrefetchScalarGridSpec(
            num_scalar_prefetch=1, grid=(rh // tile,),
            in_specs=[pl.BlockSpec((None, tile, w), lambda i, sm_ref: (sm_ref[0], i, 0)),
                      pl.BlockSpec((3, tile, w), lambda i, sm_ref: (0, i, 0))],
            out_specs=pl.BlockSpec((tile, w), lambda i, sm_ref: (i, 0))),
        out_shape=jax.ShapeDtypeStruct((rh, w), F32),
        compiler_params=_cp("parallel"),
    )(sm, p, buf)


def _adamw_halves_call(w, g_mine, g_sib, c, m, v, name, order=None):
    r, wd = w.shape
    rh = r // 2
    tile = _rows_tile(rh, wd)
    nt = rh // tile
    extra = [] if order is None else [order]

    def body(c_ref, *refs):
        w_ref, gm_ref, gs_ref, m_ref, v_ref, g_ref, d_ref, nm_ref, nv_ref = refs[len(extra):]
        gv = jnp.where(pl.program_id(0) == c_ref[0], gm_ref[...], gs_ref[...])
        g_ref[...] = gv
        nm = ADAM_B1 * m_ref[...] + (1.0 - ADAM_B1) * gv
        nv = ADAM_B2 * v_ref[...] + (1.0 - ADAM_B2) * jnp.square(gv)
        m_hat = nm / (1.0 - ADAM_B1 ** ADAM_STEP)
        v_hat = nv / (1.0 - ADAM_B2 ** ADAM_STEP)
        d_ref[...] = -ADAM_LR * (m_hat / (jnp.sqrt(v_hat) + ADAM_EPS) + ADAM_WD * w_ref[...])
        nm_ref[...] = nm
        nv_ref[...] = nv

    whole = pl.BlockSpec((tile, wd), lambda h, i, c_ref: (h * nt + i, 0))
    half = pl.BlockSpec((tile, wd), lambda h, i, c_ref: (i, 0))
    sd = jax.ShapeDtypeStruct((r, wd), F32)
    return pl.pallas_call(
        body, name=name,
        grid_spec=pltpu.PrefetchScalarGridSpec(
            num_scalar_prefetch=1, grid=(2, nt),
            in_specs=[ANY] * len(extra) + [whole, half, half, whole, whole], out_specs=[whole] * 4),
        out_shape=[sd, sd, sd, sd],
        compiler_params=_cp("parallel", "parallel"),
    )(c, *extra, w, g_mine, g_sib, m, v)


def _exchange8_call(vec, reduce, name):
    rows = vec.shape[0]

    def body(v_ref, out_ref, *rest):
        slots, send_sems, recv_sems = (rest if reduce else (out_ref,) + rest)
        x, y, c = _mesh_pos()
        me = 4 * x + 2 * y + c
        slots[me] = v_ref[...]

        def rcopy(k, to_me):
            bx, by, bc = (k >> 2) & 1, (k >> 1) & 1, k & 1
            px, py, pc = (1 - x if bx else x), (1 - y if by else y), (1 - c if bc else c)
            slot = 4 * px + 2 * py + pc if to_me else me
            return pltpu.make_async_remote_copy(src_ref=v_ref, dst_ref=slots.at[slot], send_sem=send_sems.at[k - 1],
                                                recv_sem=recv_sems.at[k - 1], device_id=(px, py, pc), device_id_type=MESH_ID)

        for k in range(1, N_DEV):
            rcopy(k, False).start()
        for k in range(1, N_DEV):
            rcopy(k, True).wait_recv()
        for k in range(1, N_DEV):
            rcopy(k, False).wait_send()
        if reduce:
            tot = slots[0]
            for d in range(1, N_DEV):
                tot = tot + slots[d]
            out_ref[...] = tot

    stack = jax.ShapeDtypeStruct((N_DEV, rows, LANES), F32)
    return pl.pallas_call(
        body, name=name,
        in_specs=[VMEM_SPEC], out_specs=VMEM_SPEC,
        out_shape=jax.ShapeDtypeStruct((rows, LANES), F32) if reduce else stack,
        scratch_shapes=([pltpu.VMEM((N_DEV, rows, LANES), F32)] if reduce else [])
        + [pltpu.SemaphoreType.DMA((N_DEV - 1,)), pltpu.SemaphoreType.DMA((N_DEV - 1,))],
    )(vec)


def _adamw_call(w, g, m, v, name):
    r, c = w.shape
    rb = r if r <= 256 else (256 if r % 256 == 0 else 352)
    assert r % rb == 0

    def body(w_ref, g_ref, m_ref, v_ref, d_ref, nm_ref, nv_ref):
        gv = g_ref[...]
        nm = ADAM_B1 * m_ref[...] + (1.0 - ADAM_B1) * gv
        nv = ADAM_B2 * v_ref[...] + (1.0 - ADAM_B2) * jnp.square(gv)
        m_hat = nm / (1.0 - ADAM_B1 ** ADAM_STEP)
        v_hat = nv / (1.0 - ADAM_B2 ** ADAM_STEP)
        d_ref[...] = -ADAM_LR * (m_hat / (jnp.sqrt(v_hat) + ADAM_EPS) + ADAM_WD * w_ref[...])
        nm_ref[...] = nm
        nv_ref[...] = nv

    spec = pl.BlockSpec((rb, c), lambda i: (i, 0))
    sd = jax.ShapeDtypeStruct((r, c), F32)
    return pl.pallas_call(
        body, name=name, grid=(r // rb,),
        in_specs=[spec] * 4, out_specs=[spec] * 3, out_shape=[sd, sd, sd],
        compiler_params=_cp("parallel"),
    )(w, g, m, v)


SMALL = (("attn_norm_w", D_MODEL), ("ret_gn_w", RET_W), ("mla_q_norm_w", Q_RANK), ("mla_kv_norm_w", KV_RANK),
         ("ffn_norm_w", D_MODEL), ("conv_b", F2), ("final_norm_w", D_MODEL))
WEIGHT_ORDER = ("attn_norm_w", "w_in", "ret_gn_w", "mla_q_norm_w", "w_uq", "mla_kv_norm_w", "w_ukv", "w_out",
                "ffn_norm_w", "w_up", "conv_w", "conv_b", "w_down", "final_norm_w")


def _pad_rows(flat, rows):
    return jnp.concatenate([flat, jnp.zeros((rows * LANES - flat.shape[0],), flat.dtype)]).reshape(rows, LANES)


def kernel(x, positions, attn_norm_w, w_in, ret_gn_w, mla_q_norm_w, w_uq, mla_kv_norm_w, w_ukv, w_out, ffn_norm_w, w_up, conv_w, conv_b, w_down, final_norm_w, loss_target, m_attn_norm_w, m_w_in, m_ret_gn_w, m_mla_q_norm_w, m_w_uq, m_mla_kv_norm_w, m_w_ukv, m_w_out, m_ffn_norm_w, m_w_up, m_conv_w, m_conv_b, m_w_down, m_final_norm_w, v_attn_norm_w, v_w_in, v_ret_gn_w, v_mla_q_norm_w, v_w_uq, v_mla_kv_norm_w, v_w_ukv, v_w_out, v_ffn_norm_w, v_w_up, v_conv_w, v_conv_b, v_w_down, v_final_norm_w):
    args = dict(locals())
    cx, cy, cc = _mesh_pos()
    sm = 2 * cx + cy

    c_arr, sm_arr = cc.reshape(1).astype(jnp.int32), sm.reshape(1).astype(jnp.int32)
    sds = jax.ShapeDtypeStruct

    def my_shards(group):
        return [args[n][0].astype(BF16).reshape(2, r // 2, c) for n, (r, c), _ in group]

    def full_weights(gathered, group):
        full = {}
        for (n, (r, c), axis), got in zip(group, gathered):
            piece = got.reshape(4, r, c)
            full[n] = piece if n == "w_up" else (piece.transpose(1, 0, 2).reshape(r, 4 * c) if axis == 1 else piece.reshape(4 * r, c))
        return full

    def by_owner(gw, group):
        out = []
        for n, (r, c), axis in group:
            g = gw[n]
            if axis == 1 and g.ndim == 2:
                g = g.reshape(r, 4, c).transpose(1, 0, 2)
            out.append(g.reshape(4, 2, r // 2, c))
        return out

    def sibling_shapes(gs):
        return [sds((4,) + g.shape[2:], F32) for g in gs]

    def chip_sums(gs, bufs, group):
        res = [_sum_sibling_call(g, b, c_arr, "grads_sum_sibling_" + n) for g, b, (n, _, _) in zip(gs, bufs, group)]
        return [p for p, _ in res], [pb for _, pb in res]

    def chips_shapes(pbs):
        return [sds((3,) + pb.shape[1:], BF16) for pb in pbs]

    def totals(ps, lands, group, tag):
        fins = [_sum_chips_call(p, l, sm_arr, "grads_sum_chips_" + n) for p, l, (n, _, _) in zip(ps, lands, group)]
        sibs = _exchange_call("grads_rs_share_" + tag, _share_copies, fins, [sds(f.shape, F32) for f in fins], len(fins))
        return {n: (f, s) for (n, _, _), f, s in zip(group, fins, sibs)}

    class StepExchanges(_Exchanges):
        def __init__(self, order):
            shards = my_shards(GROUP_B)
            self.gather = _exchange_start_call("weights_gather_start_b", _direct_gather_copies, shards,
                                               [sds((4,) + s.shape, BF16) for s in shards], 13 * len(shards), order)
            self.red = None

        def token(self):
            return self.gather[4][0:1, 0:1]

        def mlp_weights(self, after):
            return full_weights(_exchange_wait_call("weights_gather_wait_b", _direct_gather_copies, self.gather, after)[1], GROUP_B)

        def mlp_grads(self, gw):
            gs = by_owner(gw, GROUP_B)
            self.step1 = _exchange_start_call("grads_rs_sibling_start_b", _sibling_copies, gs, sibling_shapes(gs), 4 * len(gs))
            return self.step1[4]

        def behind_out_bwd(self, after):
            gs, bufs = _exchange_wait_call("grads_rs_sibling_wait_b", _sibling_copies, self.step1, after)
            self.ps, pbs = chip_sums(gs, bufs, GROUP_B)
            self.step2 = _exchange_start_call("grads_rs_chips_start_b", _chips_copies, pbs, chips_shapes(pbs), 3 * len(pbs))
            return self.step2[4]

        def behind_attention(self, after):
            _, lands = _exchange_wait_call("grads_rs_chips_wait_b", _chips_copies, self.step2, after)
            self.red = totals(self.ps, lands, GROUP_B, "b")

    full = full_weights(_gather_list_call(my_shards(GROUP_A), "a"), GROUP_A)
    cw_rows = 40
    cw_all = _exchange8_call(_pad_rows(conv_w[0].reshape(-1), cw_rows), False, "conv_w_all_gather")
    ex = StepExchanges(cw_all)
    cw_all = cw_all[0::2].reshape(4, cw_rows * LANES)[:, :3 * F2 // 4].reshape(4, 3, F2 // 4)
    full["conv_w"] = cw_all.transpose(1, 0, 2).reshape(3, F2)
    small = {n: args[n].reshape(1, d) for n, d in SMALL}
    small["attn_norm_w"] = small["attn_norm_w"] + ex.token()

    loss, gx, gw, gs = _local_step(x[0], positions[0], loss_target[0], full, small, ex)

    done = {}

    def adamw(n, halves=None, g=None, order=None):
        shape = args[n].shape
        two_d = (1, shape[0]) if len(shape) == 1 else shape[-2:]
        wmv = [args[k + n].reshape(two_d) for k in ("", "m_", "v_")]
        if halves is not None:
            res = _adamw_halves_call(wmv[0], *halves, c_arr, wmv[1], wmv[2], "adamw_" + n, order)
        else:
            res = (g.reshape(two_d), *_adamw_call(wmv[0], g.reshape(two_d), wmv[1], wmv[2], "adamw_" + n))
        done[n] = tuple(a.reshape(shape) for a in res)

    ga = by_owner(gw, GROUP_A)
    step1 = _exchange_start_call("grads_rs_sibling_start_a", _sibling_copies, ga, sibling_shapes(ga), 4 * len(ga))
    adamw("w_up", halves=ex.red["w_up"], order=step1[4])
    ga, bufs = _exchange_wait_call("grads_rs_sibling_wait_a", _sibling_copies, step1, done["w_up"][1])
    ps, pbs = chip_sums(ga, bufs, GROUP_A)
    step2 = _exchange_start_call("grads_rs_chips_start_a", _chips_copies, pbs, chips_shapes(pbs), 3 * len(pbs))
    adamw("w_down", halves=ex.red["w_down"], order=step2[4])

    vec = jnp.concatenate([gs[n].reshape(-1) for n, _ in SMALL] + [gw["conv_w"].reshape(-1), loss.reshape(-1)])
    tot = _exchange8_call(_pad_rows(vec + step2[4][0, 0], 216), True, "small_all_reduce").reshape(-1)
    off = 0
    for n, d in SMALL:
        adamw(n, g=tot[off:off + d])
        off += d
    adamw("conv_w", g=lax.dynamic_slice(tot[off:off + 3 * F2].reshape(3, F2), (0, sm * (F2 // 4)), (3, F2 // 4)))
    loss_tot = tot[off + 3 * F2]

    _, lands = _exchange_wait_call("grads_rs_chips_wait_a", _chips_copies, step2, tot)
    for n, halves in totals(ps, lands, GROUP_A, "a").items():
        adamw(n, halves=halves)
    outs = [done[n] for n in WEIGHT_ORDER]
    return (loss_tot, gx[None], *[o[0] for o in outs], *[o[1] for o in outs], *[o[2] for o in outs], *[o[3] for o in outs])
```

```python
import functools
import math

import numpy as np
import jax
import jax.numpy as jnp
from jax import lax
from jax.experimental import pallas as pl
from jax.experimental.pallas import tpu as pltpu

F32 = jnp.float32
BF16 = jnp.bfloat16

D_MODEL = 1024
N_HEADS = 8
HEAD = 64
RET_W = N_HEADS * HEAD
MLA_W = N_HEADS * HEAD
ROPE = 32
Q_RANK = 256
KV_RANK = 128
D_FF = 2816
F2 = 2 * D_FF
IN_W = 4 * RET_W + Q_RANK + KV_RANK + ROPE
IN_EXT = 4 * RET_W + Q_RANK + KV_RANK + 128
KPE_LO = 64
ROPE_BASE = 10000.0
EPS = 1e-6
RET_CHUNK = 256
SM_SCALE = (HEAD + ROPE) ** -0.5
LOG2E = math.log2(math.e)
LN2 = math.log(2.0)
NEG = -1e30
LANES = 128
VMEM_LIMIT = 56 * 1024 * 1024

ADAM_LR = 0.001
ADAM_B1 = 0.9
ADAM_B2 = 0.999
ADAM_EPS = 1e-08
ADAM_WD = 0.01
ADAM_STEP = 10


VMEM_LIMIT_MLP_BWD = 60 * 1024 * 1024


def _cp(*sem, vmem=VMEM_LIMIT):
    return pltpu.CompilerParams(dimension_semantics=sem, vmem_limit_bytes=vmem)


def _full(shape):
    n = len(shape)
    return pl.BlockSpec(tuple(shape), lambda *_: (0,) * n)


def _row(ts, c):
    return pl.BlockSpec((ts, c), lambda i: (i, 0))


def _hrow(h, ts, c):
    return pl.BlockSpec((h, ts, c), lambda i: (0, i, 0))


def _dot(a, b):
    return jnp.dot(a, b, preferred_element_type=F32)


def _dot_nt(a, b):
    return lax.dot_general(a, b, (((1,), (1,)), ((), ())), preferred_element_type=F32)


def _dot_tn(a, b):
    return lax.dot_general(a, b, (((0,), (0,)), ((), ())), preferred_element_type=F32)


def _dot_hi(a, b):
    hi = a.astype(BF16)
    lo = (a - hi.astype(F32)).astype(BF16)
    bb = b.astype(BF16)
    return _dot(hi, bb) + _dot(lo, bb)


def _rot_half(x, half):
    w = x.shape[-1]
    lane = lax.broadcasted_iota(jnp.int32, x.shape, x.ndim - 1)
    first = (lane % (2 * half)) < half
    return jnp.where(first, -pltpu.roll(x, w - half, x.ndim - 1), pltpu.roll(x, half, x.ndim - 1))


def _rope(x, cos, sin, half):
    return x * cos + _rot_half(x, half) * sin


def _unrope(dy, cos, sin, half):
    return dy * cos - _rot_half(dy, half) * sin


def _sigmoid(g):
    return 0.5 * jnp.tanh(0.5 * g) + 0.5


def _silu(g):
    return g * _sigmoid(g)


def _rstd(x):
    return lax.rsqrt(jnp.mean(x * x, axis=-1, keepdims=True) + EPS)


def _rope_tables(positions):
    pos = positions.astype(F32)[:, None]
    s = pos.shape[0]
    inv = ROPE_BASE ** (-jnp.arange(0, HEAD, 2, dtype=F32) / HEAD)
    ang = pos * inv
    c, sn = jnp.cos(ang), jnp.sin(ang)
    cos_r = jnp.tile(jnp.concatenate([c, c], -1), (1, 2))
    sin_r = jnp.tile(jnp.concatenate([sn, sn], -1), (1, 2))
    c, sn = c[:, 0::HEAD // ROPE], sn[:, 0::HEAD // ROPE]
    one, zero = jnp.ones((s, KPE_LO), F32), jnp.zeros((s, KPE_LO), F32)
    cos_m = jnp.concatenate([one, c, c, one[:, :LANES - KPE_LO - ROPE]], -1)
    sin_m = jnp.concatenate([zero, sn, sn, zero[:, :LANES - KPE_LO - ROPE]], -1)
    return cos_r, sin_r, cos_m, sin_m


def _ret_consts():
    c = RET_CHUNK
    lg = np.log1p(-np.power(2.0, -5.0 - np.arange(N_HEADS, dtype=np.float64)))
    idx = np.arange(c, dtype=np.float64)
    diff = idx[:, None] - idx[None, :]
    lane_head = np.arange(LANES) // HEAD
    dmask = np.zeros((4, 2, c, c))
    zeta = np.zeros((4, c, LANES))
    xi = np.zeros((4, c, LANES))
    cd = np.zeros((4, LANES, LANES))
    bd = (lane_head[:, None] == lane_head[None, :]).astype(np.float64)
    for j in range(4):
        for hh in range(2):
            dmask[j, hh] = np.where(diff >= 0, np.exp(lg[2 * j + hh] * np.maximum(diff, 0.0)), 0.0)
        lgl = lg[2 * j + lane_head]
        zeta[j] = np.exp(lgl[None, :] * (c - 1.0 - idx[:, None]))
        xi[j] = np.exp(lgl[None, :] * (idx[:, None] + 1.0))
        cd[j] = np.exp(lgl * c)[:, None] * bd
    f = lambda a: jnp.asarray(a, F32)
    side = lambda d: np.concatenate([d[:, 0], d[:, 1]], axis=-1)
    return dict(dmask=f(side(dmask)), dmask_t=f(side(np.swapaxes(dmask, 2, 3))), zeta=f(zeta), xi=f(xi), cd=f(cd), bd=f(bd))


def _f1_call(x, anw, win, cos_r, sin_r, cos_m, sin_m, ts):
    s = x.shape[0]

    def body(x_ref, anw_ref, w_ref, cr_ref, sr_ref, cm_ref, sm_ref,
             q_ref, k_ref, v_ref, g_ref, cq_ref, ckv_ref, kpe_ref, r_ref):
        xv = x_ref[...]
        r = _rstd(xv)
        r_ref[...] = r
        h = (xv * r * anw_ref[...]).astype(BF16)
        cr, sr = cr_ref[...], sr_ref[...]
        qk = _dot(h, w_ref[:, 0:2 * RET_W])
        for j in range(4):
            sl = slice(j * LANES, (j + 1) * LANES)
            q_ref[:, sl] = _rope(qk[:, sl], cr, sr, HEAD // 2).astype(BF16)
            kk = qk[:, RET_W + j * LANES:RET_W + (j + 1) * LANES]
            k_ref[:, sl] = (_rope(kk, cr, sr, HEAD // 2) * (HEAD ** -0.5)).astype(BF16)
        v_ref[...] = _dot(h, w_ref[:, 2 * RET_W:3 * RET_W]).astype(BF16)
        g_ref[...] = _dot(h, w_ref[:, 3 * RET_W:4 * RET_W])
        o = 4 * RET_W
        cq_ref[...] = _dot(h, w_ref[:, o:o + Q_RANK])
        ckv_ref[...] = _dot(h, w_ref[:, o + Q_RANK:o + Q_RANK + KV_RANK])
        kp = _dot(h, w_ref[:, o + Q_RANK + KV_RANK:IN_EXT])
        kpe_ref[...] = _rope(kp, cm_ref[...], sm_ref[...], ROPE // 2)

    sd = jax.ShapeDtypeStruct
    return pl.pallas_call(
        body, name="f1_in_proj", grid=(s // ts,),
        in_specs=[_row(ts, D_MODEL), _full((1, D_MODEL)), _full((D_MODEL, IN_EXT)),
                  _row(ts, LANES), _row(ts, LANES), _row(ts, LANES), _row(ts, LANES)],
        out_specs=[_row(ts, RET_W), _row(ts, RET_W), _row(ts, RET_W), _row(ts, RET_W),
                   _row(ts, Q_RANK), _row(ts, KV_RANK), _row(ts, LANES), _row(ts, 1)],
        out_shape=[sd((s, RET_W), BF16), sd((s, RET_W), BF16), sd((s, RET_W), BF16), sd((s, RET_W), F32),
                   sd((s, Q_RANK), F32), sd((s, KV_RANK), F32), sd((s, LANES), F32), sd((s, 1), F32)],
        compiler_params=_cp("parallel"),
    )(x, anw, win, cos_r, sin_r, cos_m, sin_m)


def _stack_heads(a):
    lo = lax.broadcasted_iota(jnp.int32, a.shape, 1) < HEAD
    zero = jnp.zeros_like(a)
    return jnp.concatenate([jnp.where(lo, a, zero), jnp.where(lo, zero, a)], axis=0)


def _pair_product(a, b2, decay2, w2):
    return _dot((_dot_nt(a, b2) * decay2).astype(BF16), w2)


RET_SLABS = 2


def _ret_specs(tr, tile_of):
    c, ns = RET_CHUNK, RET_SLABS
    return dict(
        slab=pl.BlockSpec((tr, ns * LANES), lambda j, i: (tile_of(i), j)),
        tab=pl.BlockSpec((tr, LANES), lambda j, i: (tile_of(i), 0)),
        vec=pl.BlockSpec((1, ns * LANES), lambda j, i: (0, j)),
        dmask=pl.BlockSpec((ns, c, 2 * c), lambda j, i: (j, 0, 0)),
        rows=pl.BlockSpec((ns, c, LANES), lambda j, i: (j, 0, 0)),
        state=pl.BlockSpec((ns, LANES, LANES), lambda j, i: (j, 0, 0)),
        bd=pl.BlockSpec((LANES, LANES), lambda j, i: (0, 0)))


def _ret_states(a_ref, b_ref, scale_ref, cd_ref, bd, st_ref, chunks, lanes, reverse):
    nc = len(chunks)
    contrib = [[_dot_tn((a_ref[rows, ln].astype(F32) * scale_ref[sl]).astype(BF16), b_ref[rows, ln]) * bd for rows in chunks]
               for sl, ln in enumerate(lanes)]
    states = []
    for sl in range(len(lanes)):
        st, seen = st_ref[sl], [None] * nc
        for ci in (reversed(range(nc)) if reverse else range(nc)):
            seen[ci] = st.astype(BF16)
            st = st * cd_ref[sl] + contrib[sl][ci]
        st_ref[sl] = st
        states.append(seen)
    return states


def _ret_fwd_call(q, k, v, g, gnw, rc, tr):
    s = q.shape[0]
    c = RET_CHUNK
    nc = tr // c
    ns = RET_SLABS

    def body(q_ref, k_ref, v_ref, g_ref, gnw_ref, dm_ref, zeta_ref, xi_ref, cd_ref, bd_ref, o_ref, y_ref, st_ref):
        @pl.when(pl.program_id(1) == 0)
        def _():
            st_ref[...] = jnp.zeros_like(st_ref)

        bd = bd_ref[...]
        chunks = [slice(ci * c, (ci + 1) * c) for ci in range(nc)]
        lanes = [slice(sl * LANES, (sl + 1) * LANES) for sl in range(ns)]
        states = _ret_states(k_ref, v_ref, zeta_ref, cd_ref, bd, st_ref, chunks, lanes, False)
        for ci, rows in enumerate(chunks):
            for sl, ln in enumerate(lanes):
                qc = q_ref[rows, ln]
                o_ref[rows, ln] = (_dot(qc, states[sl][ci]) * xi_ref[sl]
                                   + _pair_product(qc, _stack_heads(k_ref[rows, ln]), dm_ref[sl], _stack_heads(v_ref[rows, ln])))
        avg = bd * (1.0 / HEAD)
        for ln in lanes:
            o = o_ref[:, ln]
            ctr = o - _dot_hi(o, avg)
            var = _dot_hi(ctr * ctr, avg)
            y_ref[:, ln] = (_silu(g_ref[:, ln]) * (ctr * lax.rsqrt(var + EPS) * gnw_ref[:, ln])).astype(BF16)

    specs = _ret_specs(tr, lambda i: i)
    sd = jax.ShapeDtypeStruct
    return pl.pallas_call(
        body, name="ret_fwd", grid=(4 // ns, s // tr),
        in_specs=[specs["slab"]] * 4 + [specs["vec"], specs["dmask"], specs["rows"], specs["rows"], specs["state"], specs["bd"]],
        out_specs=[specs["slab"]] * 2,
        out_shape=[sd((s, RET_W), F32), sd((s, RET_W), BF16)],
        scratch_shapes=[pltpu.VMEM((ns, LANES, LANES), F32)],
        compiler_params=_cp("parallel", "arbitrary"),
    )(q, k, v, g, gnw, rc["dmask"], rc["zeta"], rc["xi"], rc["cd"], rc["bd"])


QK_AUX = HEAD + ROPE
V_AUX = HEAD


def _lane_pair(shape, lo, a, b, rest):
    lane = lax.broadcasted_iota(jnp.int32, shape, len(shape) - 1)
    return jnp.where(lane == lo, a, jnp.where(lane == lo + 1, b, rest))


def _hi_lo(v):
    hi = v.astype(BF16).astype(F32)
    return hi, v - hi


def _mla_pre_call(cq, ckv, kpe, qnw, kvnw, wq, wk, wv, cos_m, sin_m, ts):
    s = cq.shape[0]

    def body(cq_ref, ckv_ref, kpe_ref, qnw_ref, kvnw_ref, wq_ref, wk_ref, wv_ref, cm_ref, sm_ref, q_ref, k_ref, v_ref):
        cqv, ckvv = cq_ref[...], ckv_ref[...]
        cqn = (cqv * _rstd(cqv) * qnw_ref[...]).astype(BF16)
        ckvn = (ckvv * _rstd(ckvv) * kvnw_ref[...]).astype(BF16)
        cm, sm = cm_ref[...], sm_ref[...]
        kp = _lane_pair((ts, LANES), QK_AUX, -1.0, -1.0, kpe_ref[...])
        for h in range(N_HEADS):
            qh = _rope(_dot(cqn, wq_ref[h]), cm, sm, ROPE // 2)
            q_ref[h] = (qh * (SM_SCALE * LOG2E)).astype(BF16)
            k_ref[h] = (_dot(ckvn, wk_ref[h]) + kp).astype(BF16)
            v_ref[h] = _lane_pair((ts, LANES), V_AUX, 1.0, 1.0, _dot(ckvn, wv_ref[h])).astype(BF16)

    sd = jax.ShapeDtypeStruct
    hm = sd((N_HEADS, s, LANES), BF16)
    return pl.pallas_call(
        body, name="mla_pre", grid=(s // ts,),
        in_specs=[_row(ts, Q_RANK), _row(ts, KV_RANK), _row(ts, LANES), _full((1, Q_RANK)), _full((1, KV_RANK)),
                  _full((N_HEADS, Q_RANK, LANES)), _full((N_HEADS, KV_RANK, LANES)), _full((N_HEADS, KV_RANK, LANES)),
                  _row(ts, LANES), _row(ts, LANES)],
        out_specs=[_hrow(N_HEADS, ts, LANES)] * 3,
        out_shape=[hm, hm, hm],
        compiler_params=_cp("parallel"),
    )(cq, ckv, kpe, qnw, kvnw, wq, wk, wv, cos_m, sin_m)


def _flash_fwd_call(q, k, v, tb):
    s = q.shape[1]
    nb = s // tb
    pairs = [(a, b) for a in range(nb) for b in range(a + 1)]
    qi_of, ki_of = (jnp.asarray(np.array(col, np.int32)) for col in zip(*pairs))

    def body(qi_ref, ki_ref, q_ref, k_ref, v_ref, o_ref, qb_ref, m_ref, acc_ref):
        qi, ki = qi_ref[pl.program_id(0)], ki_ref[pl.program_id(0)]

        @pl.when(ki == 0)
        def _():
            m_ref[...] = jnp.full_like(m_ref, NEG)
            acc_ref[...] = jnp.zeros_like(acc_ref)

        def step(masked):
            if masked:
                keep = lax.broadcasted_iota(jnp.int32, (tb, tb), 1) <= lax.broadcasted_iota(jnp.int32, (tb, tb), 0)
            def finish(h, pe, alpha):
                acc_ref[h] = acc_ref[h] * alpha + _dot(pe, v_ref[h])

            nxt, pending = _dot_nt(q_ref[0], k_ref[0]), None
            for h in range(N_HEADS):
                sc = nxt
                if h + 1 < N_HEADS:
                    nxt = _dot_nt(q_ref[h + 1], k_ref[h + 1])
                if masked:
                    sc = jnp.where(keep, sc, NEG)
                m_prev = m_ref[h]
                m_new = jnp.maximum(m_prev, jnp.max(sc, axis=1, keepdims=True))
                pe = jnp.exp2(sc - jnp.tile(m_new, (1, tb // LANES))).astype(BF16)
                m_ref[h] = m_new
                if pending is not None:
                    finish(*pending)
                pending = (h, pe, jnp.exp2(m_prev - m_new))
            finish(*pending)

        @pl.when(ki < qi)
        def _():
            step(False)

        @pl.when(ki == qi)
        def _():
            step(True)
            lane = lax.broadcasted_iota(jnp.int32, (tb, LANES), 1)
            for p in range(N_HEADS // 2):
                outs = []
                for h in (2 * p, 2 * p + 1):
                    acc = acc_ref[h]
                    l = acc[:, V_AUX:V_AUX + 1]
                    outs.append(acc * (1.0 / l))
                    hi, lo = _hi_lo(m_ref[h][:, 0:1] + jnp.log(l) * LOG2E)
                    qb_ref[h] = _lane_pair((tb, LANES), QK_AUX, hi, lo, q_ref[h].astype(F32)).astype(BF16)
                o_ref[:, p * LANES:(p + 1) * LANES] = jnp.where(lane < HEAD, outs[0], pltpu.roll(outs[1], HEAD, 1)).astype(BF16)

    sd = jax.ShapeDtypeStruct
    qspec = pl.BlockSpec((N_HEADS, tb, LANES), lambda p, qi_ref, ki_ref: (0, qi_ref[p], 0))
    kspec = pl.BlockSpec((N_HEADS, tb, LANES), lambda p, qi_ref, ki_ref: (0, ki_ref[p], 0))
    return pl.pallas_call(
        body, name="mla_flash_fwd",
        grid_spec=pltpu.PrefetchScalarGridSpec(
            num_scalar_prefetch=2, grid=(len(pairs),),
            in_specs=[qspec, kspec, kspec],
            out_specs=[pl.BlockSpec((tb, MLA_W), lambda p, qi_ref, ki_ref: (qi_ref[p], 0)), qspec],
            scratch_shapes=[pltpu.VMEM((N_HEADS, tb, LANES), F32), pltpu.VMEM((N_HEADS, tb, LANES), F32)]),
        out_shape=[sd((s, MLA_W), BF16), sd((N_HEADS, s, LANES), BF16)],
        compiler_params=_cp("arbitrary"),
    )(qi_of, ki_of, q, k, v)


def _out_proj_call(x, yret, ymla, wout, ts):
    s = x.shape[0]

    def body(x_ref, yr_ref, ym_ref, w_ref, x1_ref, r_ref):
        x1 = x_ref[...] + _dot(yr_ref[...], w_ref[0:RET_W, :]) + _dot(ym_ref[...], w_ref[RET_W:, :])
        x1_ref[...] = x1
        r_ref[...] = _rstd(x1)

    sd = jax.ShapeDtypeStruct
    return pl.pallas_call(
        body, name="out_proj", grid=(s // ts,),
        in_specs=[_row(ts, D_MODEL), _row(ts, RET_W), _row(ts, MLA_W), _full((D_MODEL, D_MODEL))],
        out_specs=[_row(ts, D_MODEL), _row(ts, 1)],
        out_shape=[sd((s, D_MODEL), F32), sd((s, 1), F32)],
        compiler_params=_cp("parallel"),
    )(x, yret, ymla, wout)


W_UP_SHARD = F2 // 4


def _ffn_fwd_call(x1, r2, fnw, wup4, cw, cb, wdown, ts):
    s = x1.shape[0]
    wsh = W_UP_SHARD

    def body(x_ref, r_ref, fnw_ref, wup_ref, cw_ref, cb_ref, wd_ref, u_ref, uc_ref, x2_ref, carry_ref):
        _zero_first(pl.program_id(0) == 0, carry_ref)
        xv = x_ref[...]
        h = (xv * r_ref[...] * fnw_ref[...]).astype(BF16)
        conv = []
        for j in range(4):
            cols = slice(j * wsh, (j + 1) * wsh)
            ub = _dot(h, wup_ref[j]).astype(BF16)
            u_ref[:, cols] = ub
            u = ub.astype(F32)
            u1, u2 = _shifted(u, carry_ref[:, cols])
            w = cw_ref[:, cols]
            cb16 = (cb_ref[:, cols] + w[0:1, :] * u2 + w[1:2, :] * u1 + w[2:3, :] * u).astype(BF16)
            uc_ref[:, cols] = cb16
            conv.append(cb16.astype(F32))
            carry_ref[:, cols] = u[ts - 8:, :]
        acc = xv
        for j in range(2):
            a = (_silu(conv[j]) * conv[j + 2]).astype(BF16)
            acc = acc + _dot(a, wd_ref[j * wsh:(j + 1) * wsh, :])
        x2_ref[...] = acc

    sd = jax.ShapeDtypeStruct
    return pl.pallas_call(
        body, name="ffn_fwd", grid=(s // ts,),
        in_specs=[_row(ts, D_MODEL), _row(ts, 1), _full((1, D_MODEL)), _full((4, D_MODEL, wsh)),
                  _full((3, F2)), _full((1, F2)), _full((D_FF, D_MODEL))],
        out_specs=[_row(ts, F2), _row(ts, F2), _row(ts, D_MODEL)],
        out_shape=[sd((s, F2), BF16), sd((s, F2), BF16), sd((s, D_MODEL), F32)],
        scratch_shapes=[pltpu.VMEM((8, F2), F32)],
        compiler_params=_cp("arbitrary"),
    )(x1, r2, fnw, wup4, cw, cb, wdown)


def _shifted(u, hal):
    row = lax.broadcasted_iota(jnp.int32, hal.shape, 0)
    r1, r2 = pltpu.roll(u, 1, 0), pltpu.roll(u, 2, 0)
    top1 = jnp.where(row == 0, hal[7:8, :], r1[0:8, :])
    top2 = jnp.where(row == 0, hal[6:7, :], jnp.where(row == 1, hal[7:8, :], r2[0:8, :]))
    return jnp.concatenate([top1, r1[8:, :]], axis=0), jnp.concatenate([top2, r2[8:, :]], axis=0)


def _prep_weights(w):
    win = w["w_in"]
    pad = lambda n: jnp.zeros((D_MODEL, n), win.dtype)
    win_ext = jnp.concatenate([win[:, :IN_W - ROPE], pad(KPE_LO), win[:, IN_W - ROPE:], pad(LANES - KPE_LO - ROPE)], -1)
    wuq = w["w_uq"].reshape(Q_RANK, N_HEADS, HEAD + ROPE)
    wq = jnp.concatenate([wuq, jnp.zeros((Q_RANK, N_HEADS, LANES - HEAD - ROPE), wuq.dtype)], -1).transpose(1, 0, 2)
    wukv = w["w_ukv"].reshape(KV_RANK, N_HEADS, 2 * HEAD)
    zk = jnp.zeros((KV_RANK, N_HEADS, HEAD), wukv.dtype)
    wk = jnp.concatenate([wukv[:, :, :HEAD], zk], -1).transpose(1, 0, 2)
    wv = jnp.concatenate([wukv[:, :, HEAD:], zk], -1).transpose(1, 0, 2)
    c = lambda a: a.astype(BF16)
    return dict(win=c(win_ext), wq=c(wq), wk=c(wk), wv=c(wv), wout=c(w["w_out"]))


def _prep_mlp_weights(w):
    wup = w["w_up"]
    if wup.ndim == 2:
        wup = wup.reshape(D_MODEL, 4, W_UP_SHARD).transpose(1, 0, 2)
    return dict(wup=wup.astype(BF16), wdown=w["w_down"].astype(BF16))


def _tiles(s):
    return dict(ts=min(s, 512), tr=min(s, 1024), tb=min(s, 512), t2=min(s, 256))


class _Exchanges:
    def __init__(self, w):
        self.w = w

    def mlp_weights(self, after):
        return self.w

    def mlp_grads(self, gw):
        pass

    def behind_out_bwd(self, after):
        pass

    def behind_attention(self, after):
        pass


def _forward(x, positions, w, small, ex):
    s = x.shape[0]
    t = _tiles(s)
    pw = _prep_weights(w)
    cos_r, sin_r, cos_m, sin_m = _rope_tables(positions)
    rc = _ret_consts()
    q, k, v, g, cq, ckv, kpe, r1 = _f1_call(x, small["attn_norm_w"], pw["win"], cos_r, sin_r, cos_m, sin_m, t["ts"])
    o_ret, y_ret = _ret_fwd_call(q, k, v, g, small["ret_gn_w"], rc, t["tr"])
    mq, mk, mv = _mla_pre_call(cq, ckv, kpe, small["mla_q_norm_w"], small["mla_kv_norm_w"],
                               pw["wq"], pw["wk"], pw["wv"], cos_m, sin_m, t["ts"])
    y_mla, mqb = _flash_fwd_call(mq, mk, mv, t["tb"])
    x1, r2 = _out_proj_call(x, y_ret, y_mla, pw["wout"], t["ts"])
    pw.update(_prep_mlp_weights(ex.mlp_weights(r2)))
    u, uc, x2 = _ffn_fwd_call(x1, r2, small["ffn_norm_w"], pw["wup"], w["conv_w"], small["conv_b"], pw["wdown"], t["ts"])
    return dict(pw=pw, tabs=(cos_r, sin_r, cos_m, sin_m), rc=rc, q=q, k=k, v=v, g=g, cq=cq, ckv=ckv, kpe=kpe, r1=r1,
                o_ret=o_ret, y_ret=y_ret, mqb=mqb, mk=mk, mv=mv, y_mla=y_mla, x1=x1, r2=r2, u=u, uc=uc, x2=x2)


def _norm_bwd(dh, xh, r, nw):
    dxn = dh * nw
    return r * (dxn - xh * jnp.mean(dxn * xh, axis=-1, keepdims=True))


def _ordered_after(body, order):
    if order is None:
        return body, [], []
    return (lambda order_ref, *refs: body(*refs)), [pl.BlockSpec(memory_space=pl.ANY)], [order]


def _zero_first(first, *refs):
    @pl.when(first)
    def _():
        for ref in refs:
            ref[...] = jnp.zeros_like(ref)


def _colsum(v):
    return jnp.sum(v, axis=0, keepdims=True)


def _dsilu(g, sg):
    return sg * (1.0 + g * (1.0 - sg))


def _loss_call(x2, tgt, fw, ts):
    s = x2.shape[0]

    def body(x_ref, t_ref, fw_ref, dx_ref, loss_ref, gfw_ref):
        _zero_first(pl.program_id(0) == 0, loss_ref, gfw_ref)
        xv = x_ref[...]
        r = _rstd(xv)
        xh = xv * r
        fwv = fw_ref[...]
        e = xh * fwv - t_ref[...]
        loss_ref[...] += (0.5 / D_MODEL) * _colsum(jnp.sum(e * e, axis=1, keepdims=True))
        dy = e * (1.0 / D_MODEL)
        gfw_ref[...] += _colsum(dy * xh)
        dx_ref[...] = _norm_bwd(dy, xh, r, fwv)

    sd = jax.ShapeDtypeStruct
    return pl.pallas_call(
        body, name="loss_bwd", grid=(s // ts,),
        in_specs=[_row(ts, D_MODEL), _row(ts, D_MODEL), _full((1, D_MODEL))],
        out_specs=[_row(ts, D_MODEL), _full((1, 1)), _full((1, D_MODEL))],
        out_shape=[sd((s, D_MODEL), F32), sd((1, 1), F32), sd((1, D_MODEL), F32)],
        compiler_params=_cp("arbitrary"),
    )(x2, tgt, fw)


def _ffn_bwd_call(dx2, u, uc, cw, wdown, wup4, x1, r2, fnw, ts):
    s = dx2.shape[0]
    nt = s // ts
    wsh = W_UP_SHARD
    rev = lambda i: nt - 1 - i

    def body(dx2_ref, u_ref, uc_ref, cw_ref, wd_ref, wup_ref, x_ref, r_ref, fnw_ref,
             du_ref, dx1_ref, dcw_ref, dcb_ref, dfnw_ref, dwd_hbm, carry_ref, dwd_ref, sem):
        i = pl.program_id(0)
        _zero_first(i == 0, carry_ref, dwd_ref, dcw_ref, dcb_ref, dfnw_ref)
        dxb = dx2_ref[...].astype(BF16)
        dh = jnp.zeros((ts, D_MODEL), F32)
        for j in range(2):
            gcols = slice(j * wsh, (j + 1) * wsh)
            vcols = slice(D_FF + j * wsh, D_FF + (j + 1) * wsh)
            gate, val = uc_ref[:, gcols].astype(F32), uc_ref[:, vcols].astype(F32)
            da = _dot_nt(dxb, wd_ref[gcols, :])
            sg = _sigmoid(gate)
            sl = gate * sg
            dwd_ref[gcols, :] += _dot_tn((sl * val).astype(BF16), dxb)
            for d, cols, shard in ((da * val * _dsilu(gate, sg), gcols, j), (da * sl, vcols, 2 + j)):
                d1, d2 = _shifted_up(d, carry_ref[:, cols])
                uv = u_ref[:, cols].astype(F32)
                for t, dt in enumerate((d2, d1, d)):
                    dcw_ref[t:t + 1, cols] += _colsum(dt * uv)
                dcb_ref[:, cols] += _colsum(d)
                w = cw_ref[:, cols]
                du = (w[2:3, :] * d + w[1:2, :] * d1 + w[0:1, :] * d2).astype(BF16)
                du_ref[:, cols] = du
                dh = dh + _dot_nt(du, wup_ref[shard])
                carry_ref[:, cols] = d[0:8, :]
        r = r_ref[...]
        xh = x_ref[...] * r
        dfnw_ref[...] += _colsum(dh * xh)
        dx1_ref[...] = dx2_ref[...] + _norm_bwd(dh, xh, r, fnw_ref[...])

        @pl.when(i == nt - 1)
        def _():
            cp = pltpu.make_async_copy(dwd_ref, dwd_hbm, sem)
            cp.start()
            cp.wait()

    sd = jax.ShapeDtypeStruct
    row = lambda c: pl.BlockSpec((ts, c), lambda i: (rev(i), 0))
    once = lambda shape: pl.BlockSpec(shape, lambda i: (0,) * len(shape), pipeline_mode=pl.Buffered(1))
    return pl.pallas_call(
        body, name="ffn_bwd", grid=(nt,),
        in_specs=[row(D_MODEL), row(F2), row(F2), once((3, F2)), once((D_FF, D_MODEL)), once((4, D_MODEL, wsh)),
                  row(D_MODEL), row(1), once((1, D_MODEL))],
        out_specs=[row(F2), row(D_MODEL), _full((3, F2)), _full((1, F2)), _full((1, D_MODEL)), pl.BlockSpec(memory_space=pl.ANY)],
        out_shape=[sd((s, F2), BF16), sd((s, D_MODEL), F32), sd((3, F2), F32), sd((1, F2), F32), sd((1, D_MODEL), F32),
                   sd((D_FF, D_MODEL), F32)],
        scratch_shapes=[pltpu.VMEM((8, F2), F32), pltpu.VMEM((D_FF, D_MODEL), F32), pltpu.SemaphoreType.DMA],
        compiler_params=_cp("arbitrary", vmem=VMEM_LIMIT_MLP_BWD),
    )(dx2, u, uc, cw, wdown, wup4, x1, r2, fnw)


def _shifted_up(d, hal):
    n = d.shape[0]
    row = lax.broadcasted_iota(jnp.int32, hal.shape, 0)
    r1, r2 = pltpu.roll(d, n - 1, 0), pltpu.roll(d, n - 2, 0)
    end1 = jnp.where(row == 7, hal[0:1, :], r1[n - 8:, :])
    end2 = jnp.where(row == 6, hal[0:1, :], jnp.where(row == 7, hal[1:2, :], r2[n - 8:, :]))
    return jnp.concatenate([r1[:n - 8, :], end1], axis=0), jnp.concatenate([r2[:n - 8, :], end2], axis=0)


def _dw_norm_call(x, r, nw, b, ts, tn, name):
    s, n = b.shape
    k = x.shape[1]

    def body(x_ref, r_ref, nw_ref, b_ref, dw_ref):
        _zero_first(pl.program_id(1) == 0, dw_ref)
        h = (x_ref[...] * r_ref[...] * nw_ref[...]).astype(BF16)
        dw_ref[...] += _dot_tn(h, b_ref[...])

    return pl.pallas_call(
        body, name=name, grid=(n // tn, s // ts),
        in_specs=[pl.BlockSpec((ts, k), lambda j, i: (i, 0)), pl.BlockSpec((ts, 1), lambda j, i: (i, 0)),
                  pl.BlockSpec((1, k), lambda j, i: (0, 0)), pl.BlockSpec((ts, tn), lambda j, i: (i, j))],
        out_specs=pl.BlockSpec((None, k, tn), lambda j, i: (j, 0, 0)),
        out_shape=jax.ShapeDtypeStruct((n // tn, k, tn), F32),
        compiler_params=_cp("parallel", "arbitrary"),
    )(x, r, nw, b)


def _out_bwd_call(dx1, yret, ymla, wout, ts, order=None):
    s = dx1.shape[0]

    def body(dx_ref, yr_ref, ym_ref, w_ref, dyr_ref, do_ref, dwo_ref):
        _zero_first(pl.program_id(0) == 0, dwo_ref)
        dxb = dx_ref[...].astype(BF16)
        dmix = _dot_nt(dxb, w_ref[...])
        dyr_ref[...] = dmix[:, :RET_W]
        ym = ym_ref[...]
        lane = lax.broadcasted_iota(jnp.int32, (ts, LANES), 1)
        for p in range(N_HEADS // 2):
            dom = dmix[:, RET_W + p * LANES:RET_W + (p + 1) * LANES]
            prod = dom * ym[:, p * LANES:(p + 1) * LANES].astype(F32)
            for hh in range(2):
                mine = (lane >= HEAD) if hh else (lane < HEAD)
                hi, lo = _hi_lo(jnp.sum(jnp.where(mine, prod, 0.0), axis=1, keepdims=True))
                base = jnp.where(lane < HEAD, pltpu.roll(dom, HEAD, 1) if hh else dom, 0.0)
                do_ref[2 * p + hh] = _lane_pair((ts, LANES), V_AUX, -hi, -lo, base).astype(BF16)
        dwo_ref[0:RET_W, :] += _dot_tn(yr_ref[...], dxb)
        dwo_ref[RET_W:, :] += _dot_tn(ym, dxb)

    sd = jax.ShapeDtypeStruct
    body, first_specs, first = _ordered_after(body, order)
    return pl.pallas_call(
        body, name="out_proj_bwd", grid=(s // ts,),
        in_specs=first_specs + [_row(ts, D_MODEL), _row(ts, RET_W), _row(ts, MLA_W), _full((D_MODEL, D_MODEL))],
        out_specs=[_row(ts, RET_W), _hrow(N_HEADS, ts, LANES), _full((D_MODEL, D_MODEL))],
        out_shape=[sd((s, RET_W), F32), sd((N_HEADS, s, LANES), BF16), sd((D_MODEL, D_MODEL), F32)],
        compiler_params=_cp("arbitrary"),
    )(*first, dx1, yret, ymla, wout)


def _ret_bwd_q_call(q, k, v, o, g, dy, gnw, rc, cos_r, sin_r, tr):
    s = q.shape[0]
    c = RET_CHUNK
    nc = tr // c
    ns = RET_SLABS

    def body(q_ref, k_ref, v_ref, o_ref, g_ref, dy_ref, gnw_ref, dm_ref, zeta_ref, xi_ref, cd_ref, bd_ref, cr_ref, sr_ref,
             dq_ref, dg_ref, do_ref, dgnw_ref, st_ref):
        _zero_first(pl.program_id(1) == 0, st_ref, dgnw_ref)
        bd = bd_ref[...]
        avg = bd * (1.0 / HEAD)
        chunks = [slice(ci * c, (ci + 1) * c) for ci in range(nc)]
        lanes = [slice(sl * LANES, (sl + 1) * LANES) for sl in range(ns)]
        dov = []
        for ln in lanes:
            ov = o_ref[:, ln]
            ctr = ov - _dot_hi(ov, avg)
            rs = lax.rsqrt(_dot_hi(ctr * ctr, avg) + EPS)
            oh = ctr * rs
            gg, dyv, gnw_v = g_ref[:, ln], dy_ref[:, ln], gnw_ref[:, ln]
            sg = _sigmoid(gg)
            sl = gg * sg
            dg_ref[:, ln] = (dyv * oh * gnw_v * _dsilu(gg, sg)).astype(BF16)
            dgnw_ref[:, ln] += _colsum(dyv * sl * oh)
            doh = dyv * sl * gnw_v
            dov.append((rs * (doh - _dot_hi(doh, avg) - oh * _dot_hi(doh * oh, avg))).astype(BF16))
            do_ref[:, ln] = dov[-1]
        states = _ret_states(k_ref, v_ref, zeta_ref, cd_ref, bd, st_ref, chunks, lanes, False)
        for ci, rows in enumerate(chunks):
            for sl, ln in enumerate(lanes):
                doc = dov[sl][rows, :]
                dq = (_dot_nt(doc, states[sl][ci]) * xi_ref[sl]
                      + _pair_product(doc, _stack_heads(v_ref[rows, ln]), dm_ref[sl], _stack_heads(k_ref[rows, ln])))
                dq_ref[rows, ln] = _unrope(dq, cr_ref[rows, :], sr_ref[rows, :], HEAD // 2).astype(BF16)

    specs = _ret_specs(tr, lambda i: i)
    sd = jax.ShapeDtypeStruct
    return pl.pallas_call(
        body, name="ret_bwd_q", grid=(4 // ns, s // tr),
        in_specs=[specs["slab"]] * 6 + [specs["vec"], specs["dmask"], specs["rows"], specs["rows"], specs["state"], specs["bd"],
                                        specs["tab"], specs["tab"]],
        out_specs=[specs["slab"]] * 3 + [specs["vec"]],
        out_shape=[sd((s, RET_W), BF16), sd((s, RET_W), BF16), sd((s, RET_W), BF16), sd((1, RET_W), F32)],
        scratch_shapes=[pltpu.VMEM((ns, LANES, LANES), F32)],
        compiler_params=_cp("parallel", "arbitrary"),
    )(q, k, v, o, g, dy, gnw, rc["dmask"], rc["zeta"], rc["xi"], rc["cd"], rc["bd"], cos_r, sin_r)


def _ret_bwd_kv_call(q, k, v, do, rc, cos_r, sin_r, tr):
    s = q.shape[0]
    c = RET_CHUNK
    nc = tr // c
    nt = s // tr
    ns = RET_SLABS

    def body(q_ref, k_ref, v_ref, do_ref, dm_ref, zeta_ref, xi_ref, cd_ref, bd_ref, cr_ref, sr_ref, dk_ref, dv_ref, gs_ref):
        _zero_first(pl.program_id(1) == 0, gs_ref)
        bd = bd_ref[...]
        chunks = [slice(ci * c, (ci + 1) * c) for ci in range(nc)]
        lanes = [slice(sl * LANES, (sl + 1) * LANES) for sl in range(ns)]
        states = _ret_states(q_ref, do_ref, xi_ref, cd_ref, bd, gs_ref, chunks, lanes, True)
        for ci, rows in enumerate(chunks):
            for sl, ln in enumerate(lanes):
                kc, vc = k_ref[rows, ln], v_ref[rows, ln]
                q2, do2 = _stack_heads(q_ref[rows, ln]), _stack_heads(do_ref[rows, ln])
                gb = states[sl][ci]
                dk = _dot_nt(vc, gb) * zeta_ref[sl] + _pair_product(vc, do2, dm_ref[sl], q2)
                dv = _dot(kc, gb) * zeta_ref[sl] + _pair_product(kc, q2, dm_ref[sl], do2)
                dk_ref[rows, ln] = (_unrope(dk, cr_ref[rows, :], sr_ref[rows, :], HEAD // 2) * (HEAD ** -0.5)).astype(BF16)
                dv_ref[rows, ln] = dv.astype(BF16)

    specs = _ret_specs(tr, lambda i: nt - 1 - i)
    sd = jax.ShapeDtypeStruct
    return pl.pallas_call(
        body, name="ret_bwd_kv", grid=(4 // ns, nt),
        in_specs=[specs["slab"]] * 4 + [specs["dmask"], specs["rows"], specs["rows"], specs["state"], specs["bd"],
                                        specs["tab"], specs["tab"]],
        out_specs=[specs["slab"]] * 2,
        out_shape=[sd((s, RET_W), BF16), sd((s, RET_W), BF16)],
        scratch_shapes=[pltpu.VMEM((ns, LANES, LANES), F32)],
        compiler_params=_cp("parallel", "arbitrary"),
    )(q, k, v, do, rc["dmask_t"], rc["zeta"], rc["xi"], rc["cd"], rc["bd"], cos_r, sin_r)


FLASH_BWD_HEADS = 8


def _flash_bwd_call(qb, k, v, do, tb, order=None):
    s = qb.shape[1]
    nb = s // tb
    hg = FLASH_BWD_HEADS
    pairs = [(a, b) for a in range(nb) for b in range(a, nb)]
    ki_of, qi_of = (jnp.asarray(np.array(col, np.int32)) for col in zip(*pairs))
    extra = [] if order is None else [order]

    def body(ki_ref, qi_ref, *refs):
        q_ref, k_ref, v_ref, do_ref, dk_ref, dv_ref, dq_hbm, dka_ref, dva_ref, dq_ref, sem = refs[len(extra):]
        g, p = pl.program_id(0), pl.program_id(1)
        ki, qi = ki_ref[p], qi_ref[p]
        _zero_first(p == 0, dq_ref)
        _zero_first(qi == ki, dka_ref, dva_ref)
        rows = pl.ds(pl.multiple_of(qi * tb, tb), tb)

        def step(masked):
            if masked:
                keep = lax.broadcasted_iota(jnp.int32, (tb, tb), 0) <= lax.broadcasted_iota(jnp.int32, (tb, tb), 1)
            for h in range(hg):
                st = _dot_nt(k_ref[h], q_ref[h])
                if masked:
                    st = jnp.where(keep, st, NEG)
                pt = jnp.exp2(st)
                dob = do_ref[h]
                dva_ref[h] += _dot(pt.astype(BF16), dob)
                dst = (pt * _dot_nt(v_ref[h], dob)).astype(BF16)
                dka_ref[h] += _dot(dst, q_ref[h])
                dq_ref[h, rows, :] += _dot_tn(dst, k_ref[h])

        @pl.when(qi > ki)
        def _():
            step(False)

        @pl.when(qi == ki)
        def _():
            step(True)

        @pl.when(qi == nb - 1)
        def _():
            dk_ref[...] = (dka_ref[...] * LN2).astype(BF16)
            dv_ref[...] = dva_ref[...].astype(BF16)

        @pl.when(p == len(pairs) - 1)
        def _():
            cp = pltpu.make_async_copy(dq_ref, dq_hbm.at[pl.ds(g * hg, hg)], sem)
            cp.start()
            cp.wait()

    kspec = pl.BlockSpec((hg, tb, LANES), lambda g, p, ki_ref, qi_ref: (g, ki_ref[p], 0))
    qspec = pl.BlockSpec((hg, tb, LANES), lambda g, p, ki_ref, qi_ref: (g, qi_ref[p], 0))
    hm = jax.ShapeDtypeStruct((N_HEADS, s, LANES), BF16)
    return pl.pallas_call(
        body, name="mla_flash_bwd",
        grid_spec=pltpu.PrefetchScalarGridSpec(
            num_scalar_prefetch=2, grid=(N_HEADS // hg, len(pairs)),
            in_specs=[ANY] * len(extra) + [qspec, kspec, kspec, qspec],
            out_specs=[kspec, kspec, ANY],
            scratch_shapes=[pltpu.VMEM((hg, tb, LANES), F32), pltpu.VMEM((hg, tb, LANES), F32),
                            pltpu.VMEM((hg, s, LANES), F32), pltpu.SemaphoreType.DMA]),
        out_shape=[hm, hm, jax.ShapeDtypeStruct((N_HEADS, s, LANES), F32)],
        compiler_params=_cp("arbitrary", "arbitrary"),
    )(ki_of, qi_of, *extra, qb, k, v, do)


def _mla_post_call(dq, dk, dv, cq, ckv, qnw, kvnw, wq, wk, wv, cos_m, sin_m, ts):
    s = cq.shape[0]

    def body(dq_ref, dk_ref, dv_ref, cq_ref, ckv_ref, qnw_ref, kvnw_ref, wq_ref, wk_ref, wv_ref, cm_ref, sm_ref,
             dcq_ref, dckv_ref, dkpe_ref, dwq_ref, dwk_ref, dwv_ref, dqnw_ref, dkvnw_ref):
        _zero_first(pl.program_id(0) == 0, dwq_ref, dwk_ref, dwv_ref, dqnw_ref, dkvnw_ref)
        cqv, ckvv = cq_ref[...], ckv_ref[...]
        rq, rkv = _rstd(cqv), _rstd(ckvv)
        qh_, kvh_ = cqv * rq, ckvv * rkv
        qnw_v, kvnw_v = qnw_ref[...], kvnw_ref[...]
        cqn = (qh_ * qnw_v).astype(BF16)
        ckvn = (kvh_ * kvnw_v).astype(BF16)
        cm, sm = cm_ref[...], sm_ref[...]
        dcqn = jnp.zeros((ts, Q_RANK), F32)
        dckvn = jnp.zeros((ts, KV_RANK), F32)
        dkpe = jnp.zeros((ts, LANES), F32)
        for h in range(N_HEADS):
            dqu = _unrope(dq_ref[h] * SM_SCALE, cm, sm, ROPE // 2).astype(BF16)
            dwq_ref[h] += _dot_tn(cqn, dqu)
            dcqn = dcqn + _dot_nt(dqu, wq_ref[h])
            dkb, dvb = dk_ref[h], dv_ref[h]
            dkpe = dkpe + dkb.astype(F32)
            dwk_ref[h] += _dot_tn(ckvn, dkb)
            dwv_ref[h] += _dot_tn(ckvn, dvb)
            dckvn = dckvn + _dot_nt(dkb, wk_ref[h]) + _dot_nt(dvb, wv_ref[h])
        lane = lax.broadcasted_iota(jnp.int32, (ts, LANES), 1)
        dkpe = jnp.where((lane >= KPE_LO) & (lane < KPE_LO + ROPE), dkpe, 0.0)
        dkpe_ref[...] = _unrope(dkpe, cm, sm, ROPE // 2).astype(BF16)
        dqnw_ref[...] += _colsum(dcqn * qh_)
        dkvnw_ref[...] += _colsum(dckvn * kvh_)
        dcq_ref[...] = _norm_bwd(dcqn, qh_, rq, qnw_v).astype(BF16)
        dckv_ref[...] = _norm_bwd(dckvn, kvh_, rkv, kvnw_v).astype(BF16)

    sd = jax.ShapeDtypeStruct
    hm = _hrow(N_HEADS, ts, LANES)
    return pl.pallas_call(
        body, name="mla_post", grid=(s // ts,),
        in_specs=[hm, hm, hm, _row(ts, Q_RANK), _row(ts, KV_RANK), _full((1, Q_RANK)), _full((1, KV_RANK)),
                  _full((N_HEADS, Q_RANK, LANES)), _full((N_HEADS, KV_RANK, LANES)), _full((N_HEADS, KV_RANK, LANES)),
                  _row(ts, LANES), _row(ts, LANES)],
        out_specs=[_row(ts, Q_RANK), _row(ts, KV_RANK), _row(ts, LANES),
                   _full((N_HEADS, Q_RANK, LANES)), _full((N_HEADS, KV_RANK, LANES)), _full((N_HEADS, KV_RANK, LANES)),
                   _full((1, Q_RANK)), _full((1, KV_RANK))],
        out_shape=[sd((s, Q_RANK), BF16), sd((s, KV_RANK), BF16), sd((s, LANES), BF16),
                   sd((N_HEADS, Q_RANK, LANES), F32), sd((N_HEADS, KV_RANK, LANES), F32), sd((N_HEADS, KV_RANK, LANES), F32),
                   sd((1, Q_RANK), F32), sd((1, KV_RANK), F32)],
        compiler_params=_cp("arbitrary"),
    )(dq, dk, dv, cq, ckv, qnw, kvnw, wq, wk, wv, cos_m, sin_m)


def _in_bwd_call(parts, x, r1, anw, dx1, win, ts):
    s = x.shape[0]
    widths = [p.shape[1] for p in parts]
    np_ = len(parts)

    def body(*refs):
        p_refs = refs[:np_]
        x_ref, r_ref, anw_ref, dx1_ref, w_ref, dx_ref, dw_ref, danw_ref = refs[np_:]
        _zero_first(pl.program_id(0) == 0, dw_ref, danw_ref)
        dproj = jnp.concatenate([p[...] for p in p_refs], axis=-1)
        r, anw_v = r_ref[...], anw_ref[...]
        xh = x_ref[...] * r
        dw_ref[...] += _dot_tn((xh * anw_v).astype(BF16), dproj)
        dh = _dot_nt(dproj, w_ref[...])
        danw_ref[...] += _colsum(dh * xh)
        dx_ref[...] = dx1_ref[...] + _norm_bwd(dh, xh, r, anw_v)

    sd = jax.ShapeDtypeStruct
    return pl.pallas_call(
        body, name="in_proj_bwd", grid=(s // ts,),
        in_specs=[_row(ts, w) for w in widths]
        + [_row(ts, D_MODEL), _row(ts, 1), _full((1, D_MODEL)), _row(ts, D_MODEL), _full((D_MODEL, IN_EXT))],
        out_specs=[_row(ts, D_MODEL), _full((D_MODEL, IN_EXT)), _full((1, D_MODEL))],
        out_shape=[sd((s, D_MODEL), F32), sd((D_MODEL, IN_EXT), F32), sd((1, D_MODEL), F32)],
        compiler_params=_cp("arbitrary"),
    )(*parts, x, r1, anw, dx1, win)


def _local_step(x, positions, tgt, w, small, ex=None):
    s = x.shape[0]
    t = _tiles(s)
    ex = _Exchanges(w) if ex is None else ex
    f = _forward(x, positions, w, small, ex)
    pw, rc = f["pw"], f["rc"]
    cos_r, sin_r, cos_m, sin_m = f["tabs"]
    dx2, loss, g_fw = _loss_call(f["x2"], tgt, small["final_norm_w"], t["ts"])
    du, dx1, g_cw, g_cb, g_fnw, g_wd = _ffn_bwd_call(dx2, f["u"], f["uc"], w["conv_w"], pw["wdown"], pw["wup"],
                                                     f["x1"], f["r2"], small["ffn_norm_w"], t["t2"])
    g_wup = _dw_norm_call(f["x1"], f["r2"], small["ffn_norm_w"], du, t["ts"], F2 // 4, "dw_up")
    started = ex.mlp_grads(dict(w_up=g_wup, w_down=g_wd))
    dy_ret, do, g_wout = _out_bwd_call(dx1, f["y_ret"], f["y_mla"], pw["wout"], t["ts"], started)
    started = ex.behind_out_bwd(g_wout)
    drq, dg, do_ret, g_gnw = _ret_bwd_q_call(f["q"], f["k"], f["v"], f["o_ret"], f["g"], dy_ret, small["ret_gn_w"], rc, cos_r, sin_r, t["tr"])
    drk, drv = _ret_bwd_kv_call(f["q"], f["k"], f["v"], do_ret, rc, cos_r, sin_r, t["tr"])
    dmk, dmv, dmq = _flash_bwd_call(f["mqb"], f["mk"], f["mv"], do, t["tb"], started)
    ex.behind_attention(dmk)
    dcq, dckv, dkpe, g_wq, g_wk, g_wv, g_qnw, g_kvnw = _mla_post_call(
        dmq, dmk, dmv, f["cq"], f["ckv"], small["mla_q_norm_w"], small["mla_kv_norm_w"], pw["wq"], pw["wk"], pw["wv"], cos_m, sin_m, t["ts"])
    gx, g_win_ext, g_anw = _in_bwd_call([drq, drk, drv, dg, dcq, dckv, dkpe], x, f["r1"], small["attn_norm_w"], dx1, pw["win"], t["ts"])
    lo = IN_W - ROPE
    g_win = jnp.concatenate([g_win_ext[:, :lo], g_win_ext[:, lo + KPE_LO:lo + KPE_LO + ROPE]], -1)
    g_wuq = g_wq.transpose(1, 0, 2)[:, :, :HEAD + ROPE].reshape(Q_RANK, N_HEADS * (HEAD + ROPE))
    g_wukv = jnp.concatenate([g_wk[:, :, :HEAD], g_wv[:, :, :HEAD]], -1).transpose(1, 0, 2).reshape(KV_RANK, 2 * MLA_W)
    gw = dict(w_in=g_win, w_uq=g_wuq, w_ukv=g_wukv, w_out=g_wout, w_up=g_wup,
              conv_w=g_cw, w_down=g_wd)
    gs = dict(attn_norm_w=g_anw, ret_gn_w=g_gnw, mla_q_norm_w=g_qnw, mla_kv_norm_w=g_kvnw, ffn_norm_w=g_fnw,
              conv_b=g_cb, final_norm_w=g_fw)
    return loss, gx, gw, gs


MESH_ID = pl.DeviceIdType.MESH
ANY = pl.BlockSpec(memory_space=pl.ANY)
VMEM_SPEC = pl.BlockSpec(memory_space=pltpu.VMEM)
N_DEV = 8
GROUP_A = (("w_in", (D_MODEL, IN_W // 4), 1), ("w_uq", (Q_RANK, 192), 1), ("w_ukv", (KV_RANK, 256), 1),
           ("w_out", (D_MODEL // 4, D_MODEL), 0))
GROUP_B = (("w_up", (D_MODEL, F2 // 4), 1), ("w_down", (D_FF // 4, D_MODEL), 0))
HBM_SPEC = pl.BlockSpec(memory_space=pltpu.HBM)
SEM_SPEC = pl.BlockSpec(memory_space=pltpu.SEMAPHORE)


def _mesh_pos():
    return lax.axis_index("x"), lax.axis_index("y"), lax.axis_index("c")


def _other_chips(x, y):
    return [(1 - x, y), (x, 1 - y), (1 - x, 1 - y)]


def _remote(src, dst, send_sems, recv_sems, k, dev):
    return pltpu.make_async_remote_copy(src_ref=src, dst_ref=dst, send_sem=send_sems.at[k], recv_sem=recv_sems.at[k],
                                        device_id=dev, device_id_type=MESH_ID)


def _gather_list_call(parts, tag):
    n = len(parts)

    def body(*refs):
        srcs, outs, (send_sems, recv_sems) = refs[:n], refs[n:2 * n], refs[2 * n:]
        x, y, c = _mesh_pos()
        sm = 2 * x + y
        chips = _other_chips(x, y)
        sib = (x, y, 1 - c)
        rc = lambda k, src, dst, dev: _remote(src, dst, send_sems, recv_sems, k, dev)
        first = [rc(7 * i + j, srcs[i].at[c], outs[i].at[sm, c], (cx, cy, c)) for i in range(n) for j, (cx, cy) in enumerate(chips)]
        own = [rc(7 * i + 6, srcs[i], outs[i].at[sm], sib) for i in range(n)]
        for cp in first + own:
            cp.start()
        passed = []
        for j, (cx, cy) in enumerate(chips):
            for i in range(n):
                land = outs[i].at[2 * cx + cy, c]
                rc(7 * i + j, srcs[i].at[c], land, (cx, cy, c)).wait_recv()
                cp = rc(7 * i + 3 + j, land, land, sib)
                cp.start()
                passed.append(cp)
        for j, (cx, cy) in enumerate(chips):
            for i in range(n):
                rc(7 * i + 3 + j, srcs[i].at[c], outs[i].at[2 * cx + cy, 1 - c], sib).wait_recv()
        for cp in own:
            cp.wait_recv()
        for cp in first + passed + own:
            cp.wait_send()

    return pl.pallas_call(
        body, name="weights_all_gather_" + tag,
        in_specs=[ANY] * n, out_specs=[ANY] * n,
        out_shape=[jax.ShapeDtypeStruct((4,) + p.shape, p.dtype) for p in parts],
        scratch_shapes=[pltpu.SemaphoreType.DMA((7 * n,)), pltpu.SemaphoreType.DMA((7 * n,))],
    )(*parts)


def _direct_gather_copies(srcs, lands, send_sems, recv_sems):
    x, y, c = _mesh_pos()
    sm = 2 * x + y
    sends, recvs = [], []
    for i, (src, land) in enumerate(zip(srcs, lands)):
        for j, (cx, cy) in enumerate(_other_chips(x, y)):
            for t in range(2):
                sends.append(_remote(src.at[c], land.at[sm, c], send_sems, recv_sems, 13 * i + 4 * j + 2 * c + t, (cx, cy, t)))
                recvs.append(_remote(src.at[t], land.at[2 * cx + cy, t], send_sems, recv_sems, 13 * i + 4 * j + 2 * t + c, (cx, cy, t)))
        sends.append(_remote(src, land.at[sm], send_sems, recv_sems, 13 * i + 12, (x, y, 1 - c)))
        recvs.append(_remote(src, land.at[sm], send_sems, recv_sems, 13 * i + 12, (x, y, 1 - c)))
    return sends, recvs


def _sibling_copies(srcs, lands, send_sems, recv_sems):
    x, y, c = _mesh_pos()
    cps = [_remote(src.at[s, 1 - c], land.at[s], send_sems, recv_sems, 4 * i + s, (x, y, 1 - c))
           for i, (src, land) in enumerate(zip(srcs, lands)) for s in range(4)]
    return cps, cps


def _chips_copies(srcs, lands, send_sems, recv_sems):
    x, y, c = _mesh_pos()
    cps = [_remote(src.at[2 * cx + cy], land.at[j], send_sems, recv_sems, 3 * i + j, (cx, cy, c))
           for i, (src, land) in enumerate(zip(srcs, lands)) for j, (cx, cy) in enumerate(_other_chips(x, y))]
    return cps, cps


def _share_copies(srcs, lands, send_sems, recv_sems):
    x, y, c = _mesh_pos()
    cps = [_remote(src, land, send_sems, recv_sems, i, (x, y, 1 - c)) for i, (src, land) in enumerate(zip(srcs, lands))]
    return cps, cps


def _exchange_call(name, copies, srcs, land_shapes, n_sems):
    n = len(srcs)

    def body(*refs):
        sends, recvs = copies(refs[:n], refs[n:2 * n], refs[2 * n], refs[2 * n + 1])
        for cp in sends:
            cp.start()
        for cp in sends:
            cp.wait_send()
        for cp in recvs:
            cp.wait_recv()

    return pl.pallas_call(
        body, name=name, in_specs=[ANY] * n, out_specs=[ANY] * n, out_shape=list(land_shapes),
        scratch_shapes=[pltpu.SemaphoreType.DMA((n_sems,)), pltpu.SemaphoreType.DMA((n_sems,))],
    )(*srcs)


def _exchange_start_call(name, copies, srcs, land_shapes, n_sems, order=None):
    n = len(srcs)
    extra = [] if order is None else [order]
    k = 2 * n + len(extra)

    def body(*refs):
        sends, _ = copies(refs[:n], refs[n:2 * n], refs[k], refs[k + 1])
        for cp in sends:
            cp.start()
        refs[-1][...] = jnp.zeros_like(refs[-1])

    hbm = lambda a: pltpu.with_memory_space_constraint(a, pltpu.HBM)
    lands = [hbm(lax.empty(sd.shape, sd.dtype)) for sd in land_shapes]
    sem = pltpu.SemaphoreType.DMA((n_sems,))
    out = pl.pallas_call(
        body, name=name,
        out_shape=(sem, sem, *[pltpu.HBM(a.shape, a.dtype) for a in list(srcs) + lands], jax.ShapeDtypeStruct((8, LANES), F32)),
        in_specs=[HBM_SPEC] * (2 * n) + [ANY] * len(extra), out_specs=(SEM_SPEC, SEM_SPEC, *[HBM_SPEC] * (2 * n), VMEM_SPEC),
        input_output_aliases={i: 2 + i for i in range(2 * n)},
        compiler_params=pltpu.CompilerParams(has_side_effects=pltpu.SideEffectType.DATAFLOW_SIDE_EFFECTING),
    )(*[hbm(a) for a in srcs], *lands, *extra)
    return out[0], out[1], out[2:2 + n], out[2 + n:2 + 2 * n], out[-1]


def _exchange_wait_call(name, copies, started, after):
    send_sems, recv_sems, srcs, lands, _ = started
    n = len(srcs)

    def body(*refs):
        sends, recvs = copies(refs[:n], refs[n:2 * n], refs[2 * n], refs[2 * n + 1])
        for cp in sends:
            cp.wait_send()
        for cp in recvs:
            cp.wait_recv()

    out = pl.pallas_call(
        body, name=name,
        out_shape=tuple(pltpu.HBM(a.shape, a.dtype) for a in list(srcs) + list(lands)),
        in_specs=[HBM_SPEC] * (2 * n) + [SEM_SPEC, SEM_SPEC, ANY], out_specs=tuple([HBM_SPEC] * (2 * n)),
        input_output_aliases={i: i for i in range(2 * n)},
        compiler_params=pltpu.CompilerParams(has_side_effects=pltpu.SideEffectType.DATAFLOW_SIDE_EFFECTING),
    )(*srcs, *lands, send_sems, recv_sems, after)
    return out[:n], out[n:]


def _rows_tile(rows, width, itemsize=4):
    limit = max(16, (3 << 20) // (width * itemsize))
    if rows <= limit:
        return rows
    return max(t for t in range(16, limit + 1, 16) if rows % t == 0)


def _sum_sibling_call(g, buf, c, name):
    _, _, rh, w = g.shape
    tile = _rows_tile(rh, w)

    def body(c_ref, g_ref, b_ref, p_ref, pb_ref):
        p = g_ref[...] + b_ref[...]
        p_ref[...] = p
        pb_ref[...] = p.astype(BF16)

    blk = pl.BlockSpec((None, tile, w), lambda s, i, c_ref: (s, i, 0))
    return pl.pallas_call(
        body, name=name,
        grid_spec=pltpu.PrefetchScalarGridSpec(
            num_scalar_prefetch=1, grid=(4, rh // tile),
            in_specs=[pl.BlockSpec((None, None, tile, w), lambda s, i, c_ref: (s, c_ref[0], i, 0)), blk],
            out_specs=[blk, blk]),
        out_shape=[jax.ShapeDtypeStruct((4, rh, w), F32), jax.ShapeDtypeStruct((4, rh, w), BF16)],
        compiler_params=_cp("parallel", "parallel"),
    )(c, g, buf)


def _sum_chips_call(p, buf, sm, name):
    _, rh, w = p.shape
    tile = _rows_tile(rh, w)

    def body(sm_ref, p_ref, b_ref, f_ref):
        f_ref[...] = ((p_ref[...] + b_ref[0].astype(F32)) + b_ref[1].astype(F32)) + b_ref[2].astype(F32)

    return pl.pallas_call(
        body, name=name,
        grid_spec=pltpu.PrefetchScalarGridSpec(
            num_scalar_prefetch=1, grid=(rh // tile,),
            in_specs=[pl.BlockSpec((None, tile, w), lambda i, sm_ref: (sm_ref[0], i, 0)),
                      pl.BlockSpec((3, tile, w), lambda i, sm_ref: (0, i, 0))],
            out_specs=pl.BlockSpec((tile, w), lambda i, sm_ref: (i, 0))),
        out_shape=jax.ShapeDtypeStruct((rh, w), F32),
        compiler_params=_cp("parallel"),
    )(sm, p, buf)


def _adamw_halves_call(w, g_mine, g_sib, c, m, v, name):
    r, wd = w.shape
    rh = r // 2
    tile = _rows_tile(rh, wd)
    nt = rh // tile

    def body(c_ref, w_ref, gm_ref, gs_ref, m_ref, v_ref, g_ref, d_ref, nm_ref, nv_ref):
        gv = jnp.where(pl.program_id(0) == c_ref[0], gm_ref[...], gs_ref[...])
        g_ref[...] = gv
        nm = ADAM_B1 * m_ref[...] + (1.0 - ADAM_B1) * gv
        nv = ADAM_B2 * v_ref[...] + (1.0 - ADAM_B2) * jnp.square(gv)
        m_hat = nm / (1.0 - ADAM_B1 ** ADAM_STEP)
        v_hat = nv / (1.0 - ADAM_B2 ** ADAM_STEP)
        d_ref[...] = -ADAM_LR * (m_hat / (jnp.sqrt(v_hat) + ADAM_EPS) + ADAM_WD * w_ref[...])
        nm_ref[...] = nm
        nv_ref[...] = nv

    whole = pl.BlockSpec((tile, wd), lambda h, i, c_ref: (h * nt + i, 0))
    half = pl.BlockSpec((tile, wd), lambda h, i, c_ref: (i, 0))
    sd = jax.ShapeDtypeStruct((r, wd), F32)
    return pl.pallas_call(
        body, name=name,
        grid_spec=pltpu.PrefetchScalarGridSpec(
            num_scalar_prefetch=1, grid=(2, nt),
            in_specs=[whole, half, half, whole, whole], out_specs=[whole] * 4),
        out_shape=[sd, sd, sd, sd],
        compiler_params=_cp("parallel", "parallel"),
    )(c, w, g_mine, g_sib, m, v)


def _exchange8_call(vec, reduce, name):
    rows = vec.shape[0]

    def body(v_ref, out_ref, *rest):
        slots, send_sems, recv_sems = (rest if reduce else (out_ref,) + rest)
        x, y, c = _mesh_pos()
        me = 4 * x + 2 * y + c
        slots[me] = v_ref[...]

        def rcopy(k, to_me):
            bx, by, bc = (k >> 2) & 1, (k >> 1) & 1, k & 1
            px, py, pc = (1 - x if bx else x), (1 - y if by else y), (1 - c if bc else c)
            slot = 4 * px + 2 * py + pc if to_me else me
            return pltpu.make_async_remote_copy(src_ref=v_ref, dst_ref=slots.at[slot], send_sem=send_sems.at[k - 1],
                                                recv_sem=recv_sems.at[k - 1], device_id=(px, py, pc), device_id_type=MESH_ID)

        for k in range(1, N_DEV):
            rcopy(k, False).start()
        for k in range(1, N_DEV):
            rcopy(k, True).wait_recv()
        for k in range(1, N_DEV):
            rcopy(k, False).wait_send()
        if reduce:
            tot = slots[0]
            for d in range(1, N_DEV):
                tot = tot + slots[d]
            out_ref[...] = tot

    stack = jax.ShapeDtypeStruct((N_DEV, rows, LANES), F32)
    return pl.pallas_call(
        body, name=name,
        in_specs=[VMEM_SPEC], out_specs=VMEM_SPEC,
        out_shape=jax.ShapeDtypeStruct((rows, LANES), F32) if reduce else stack,
        scratch_shapes=([pltpu.VMEM((N_DEV, rows, LANES), F32)] if reduce else [])
        + [pltpu.SemaphoreType.DMA((N_DEV - 1,)), pltpu.SemaphoreType.DMA((N_DEV - 1,))],
    )(vec)


def _adamw_call(w, g, m, v, name):
    r, c = w.shape
    rb = r if r <= 256 else (256 if r % 256 == 0 else 352)
    assert r % rb == 0

    def body(w_ref, g_ref, m_ref, v_ref, d_ref, nm_ref, nv_ref):
        gv = g_ref[...]
        nm = ADAM_B1 * m_ref[...] + (1.0 - ADAM_B1) * gv
        nv = ADAM_B2 * v_ref[...] + (1.0 - ADAM_B2) * jnp.square(gv)
        m_hat = nm / (1.0 - ADAM_B1 ** ADAM_STEP)
        v_hat = nv / (1.0 - ADAM_B2 ** ADAM_STEP)
        d_ref[...] = -ADAM_LR * (m_hat / (jnp.sqrt(v_hat) + ADAM_EPS) + ADAM_WD * w_ref[...])
        nm_ref[...] = nm
        nv_ref[...] = nv

    spec = pl.BlockSpec((rb, c), lambda i: (i, 0))
    sd = jax.ShapeDtypeStruct((r, c), F32)
    return pl.pallas_call(
        body, name=name, grid=(r // rb,),
        in_specs=[spec] * 4, out_specs=[spec] * 3, out_shape=[sd, sd, sd],
        compiler_params=_cp("parallel"),
    )(w, g, m, v)


SMALL = (("attn_norm_w", D_MODEL), ("ret_gn_w", RET_W), ("mla_q_norm_w", Q_RANK), ("mla_kv_norm_w", KV_RANK),
         ("ffn_norm_w", D_MODEL), ("conv_b", F2), ("final_norm_w", D_MODEL))
WEIGHT_ORDER = ("attn_norm_w", "w_in", "ret_gn_w", "mla_q_norm_w", "w_uq", "mla_kv_norm_w", "w_ukv", "w_out",
                "ffn_norm_w", "w_up", "conv_w", "conv_b", "w_down", "final_norm_w")


def _pad_rows(flat, rows):
    return jnp.concatenate([flat, jnp.zeros((rows * LANES - flat.shape[0],), flat.dtype)]).reshape(rows, LANES)


def kernel(x, positions, attn_norm_w, w_in, ret_gn_w, mla_q_norm_w, w_uq, mla_kv_norm_w, w_ukv, w_out, ffn_norm_w, w_up, conv_w, conv_b, w_down, final_norm_w, loss_target, m_attn_norm_w, m_w_in, m_ret_gn_w, m_mla_q_norm_w, m_w_uq, m_mla_kv_norm_w, m_w_ukv, m_w_out, m_ffn_norm_w, m_w_up, m_conv_w, m_conv_b, m_w_down, m_final_norm_w, v_attn_norm_w, v_w_in, v_ret_gn_w, v_mla_q_norm_w, v_w_uq, v_mla_kv_norm_w, v_w_ukv, v_w_out, v_ffn_norm_w, v_w_up, v_conv_w, v_conv_b, v_w_down, v_final_norm_w):
    args = dict(locals())
    cx, cy, cc = _mesh_pos()
    sm = 2 * cx + cy

    c_arr, sm_arr = cc.reshape(1).astype(jnp.int32), sm.reshape(1).astype(jnp.int32)
    sds = jax.ShapeDtypeStruct

    def my_shards(group):
        return [args[n][0].astype(BF16).reshape(2, r // 2, c) for n, (r, c), _ in group]

    def full_weights(gathered, group):
        full = {}
        for (n, (r, c), axis), got in zip(group, gathered):
            piece = got.reshape(4, r, c)
            full[n] = piece if n == "w_up" else (piece.transpose(1, 0, 2).reshape(r, 4 * c) if axis == 1 else piece.reshape(4 * r, c))
        return full

    def by_owner(gw, group):
        out = []
        for n, (r, c), axis in group:
            g = gw[n]
            if axis == 1 and g.ndim == 2:
                g = g.reshape(r, 4, c).transpose(1, 0, 2)
            out.append(g.reshape(4, 2, r // 2, c))
        return out

    def sibling_shapes(gs):
        return [sds((4,) + g.shape[2:], F32) for g in gs]

    def chip_sums(gs, bufs, group):
        res = [_sum_sibling_call(g, b, c_arr, "grads_sum_sibling_" + n) for g, b, (n, _, _) in zip(gs, bufs, group)]
        return [p for p, _ in res], [pb for _, pb in res]

    def chips_shapes(pbs):
        return [sds((3,) + pb.shape[1:], BF16) for pb in pbs]

    def totals(ps, lands, group, tag):
        fins = [_sum_chips_call(p, l, sm_arr, "grads_sum_chips_" + n) for p, l, (n, _, _) in zip(ps, lands, group)]
        sibs = _exchange_call("grads_rs_share_" + tag, _share_copies, fins, [sds(f.shape, F32) for f in fins], len(fins))
        return {n: (f, s) for (n, _, _), f, s in zip(group, fins, sibs)}

    class StepExchanges(_Exchanges):
        def __init__(self, order):
            shards = my_shards(GROUP_B)
            self.gather = _exchange_start_call("weights_gather_start_b", _direct_gather_copies, shards,
                                               [sds((4,) + s.shape, BF16) for s in shards], 13 * len(shards), order)
            self.red = None

        def token(self):
            return self.gather[4][0:1, 0:1]

        def mlp_weights(self, after):
            return full_weights(_exchange_wait_call("weights_gather_wait_b", _direct_gather_copies, self.gather, after)[1], GROUP_B)

        def mlp_grads(self, gw):
            gs = by_owner(gw, GROUP_B)
            self.step1 = _exchange_start_call("grads_rs_sibling_start_b", _sibling_copies, gs, sibling_shapes(gs), 4 * len(gs))
            return self.step1[4]

        def behind_out_bwd(self, after):
            gs, bufs = _exchange_wait_call("grads_rs_sibling_wait_b", _sibling_copies, self.step1, after)
            self.ps, pbs = chip_sums(gs, bufs, GROUP_B)
            self.step2 = _exchange_start_call("grads_rs_chips_start_b", _chips_copies, pbs, chips_shapes(pbs), 3 * len(pbs))
            return self.step2[4]

        def behind_attention(self, after):
            _, lands = _exchange_wait_call("grads_rs_chips_wait_b", _chips_copies, self.step2, after)
            self.red = totals(self.ps, lands, GROUP_B, "b")

    gathered = _gather_list_call(my_shards(GROUP_A) + [conv_w[0].reshape(2, 1, 3 * F2 // 8)], "a")
    full = full_weights(gathered[:-1], GROUP_A)
    ex = StepExchanges(gathered[-1])
    full["conv_w"] = gathered[-1].reshape(4, 3, F2 // 4).transpose(1, 0, 2).reshape(3, F2)
    small = {n: args[n].reshape(1, d) for n, d in SMALL}
    small["attn_norm_w"] = small["attn_norm_w"] + ex.token()

    loss, gx, gw, gs = _local_step(x[0], positions[0], loss_target[0], full, small, ex)

    ga = by_owner(gw, GROUP_A)
    bufs = _exchange_call("grads_rs_sibling_a", _sibling_copies, ga, sibling_shapes(ga), 4 * len(ga))
    ps, pbs = chip_sums(ga, bufs, GROUP_A)
    lands = _exchange_call("grads_rs_chips_a", _chips_copies, pbs, chips_shapes(pbs), 3 * len(pbs))
    halves = {**ex.red, **totals(ps, lands, GROUP_A, "a")}

    vec = jnp.concatenate([gs[n].reshape(-1) for n, _ in SMALL] + [gw["conv_w"].reshape(-1), loss.reshape(-1)])
    tot = _exchange8_call(_pad_rows(vec, 216), True, "small_all_reduce").reshape(-1)
    red, off = {}, 0
    for n, d in SMALL:
        red[n] = tot[off:off + d].reshape(1, d)
        off += d
    red["conv_w"] = lax.dynamic_slice(tot[off:off + 3 * F2].reshape(3, F2), (0, sm * (F2 // 4)), (3, F2 // 4))
    loss_tot = tot[off + 3 * F2]

    grads, deltas, new_m, new_v = [], [], [], []
    for n in WEIGHT_ORDER:
        shape = args[n].shape
        two_d = (1, shape[0]) if len(shape) == 1 else shape[-2:]
        wmv = [args[k + n].reshape(two_d) for k in ("", "m_", "v_")]
        if n in halves:
            g, d, nm, nv = _adamw_halves_call(wmv[0], *halves[n], c_arr, wmv[1], wmv[2], "adamw_" + n)
        else:
            g = red[n].reshape(two_d)
            d, nm, nv = _adamw_call(wmv[0], g, wmv[1], wmv[2], "adamw_" + n)
        grads.append(g.reshape(shape))
        deltas.append(d.reshape(shape))
        new_m.append(nm.reshape(shape))
        new_v.append(nv.reshape(shape))
    return (loss_tot, gx[None], *grads, *deltas, *new_m, *new_v)
```

```python
import functools
import math

import numpy as np
import jax
import jax.numpy as jnp
from jax import lax
from jax.experimental import pallas as pl
from jax.experimental.pallas import tpu as pltpu

F32 = jnp.float32
BF16 = jnp.bfloat16

D_MODEL = 1024
N_HEADS = 8
HEAD = 64
RET_W = N_HEADS * HEAD
MLA_W = N_HEADS * HEAD
ROPE = 32
Q_RANK = 256
KV_RANK = 128
D_FF = 2816
F2 = 2 * D_FF
IN_W = 4 * RET_W + Q_RANK + KV_RANK + ROPE
IN_EXT = 4 * RET_W + Q_RANK + KV_RANK + 128
KPE_LO = 64
ROPE_BASE = 10000.0
EPS = 1e-6
RET_CHUNK = 256
SM_SCALE = (HEAD + ROPE) ** -0.5
LOG2E = math.log2(math.e)
LN2 = math.log(2.0)
NEG = -1e30
LANES = 128
VMEM_LIMIT = 56 * 1024 * 1024

ADAM_LR = 0.001
ADAM_B1 = 0.9
ADAM_B2 = 0.999
ADAM_EPS = 1e-08
ADAM_WD = 0.01
ADAM_STEP = 10


VMEM_LIMIT_MLP_BWD = 60 * 1024 * 1024


def _cp(*sem, vmem=VMEM_LIMIT):
    return pltpu.CompilerParams(dimension_semantics=sem, vmem_limit_bytes=vmem)


def _full(shape):
    n = len(shape)
    return pl.BlockSpec(tuple(shape), lambda *_: (0,) * n)


def _row(ts, c):
    return pl.BlockSpec((ts, c), lambda i: (i, 0))


def _hrow(h, ts, c):
    return pl.BlockSpec((h, ts, c), lambda i: (0, i, 0))


def _dot(a, b):
    return jnp.dot(a, b, preferred_element_type=F32)


def _dot_nt(a, b):
    return lax.dot_general(a, b, (((1,), (1,)), ((), ())), preferred_element_type=F32)


def _dot_tn(a, b):
    return lax.dot_general(a, b, (((0,), (0,)), ((), ())), preferred_element_type=F32)


def _dot_hi(a, b):
    hi = a.astype(BF16)
    lo = (a - hi.astype(F32)).astype(BF16)
    bb = b.astype(BF16)
    return _dot(hi, bb) + _dot(lo, bb)


def _rot_half(x, half):
    w = x.shape[-1]
    lane = lax.broadcasted_iota(jnp.int32, x.shape, x.ndim - 1)
    first = (lane % (2 * half)) < half
    return jnp.where(first, -pltpu.roll(x, w - half, x.ndim - 1), pltpu.roll(x, half, x.ndim - 1))


def _rope(x, cos, sin, half):
    return x * cos + _rot_half(x, half) * sin


def _unrope(dy, cos, sin, half):
    return dy * cos - _rot_half(dy, half) * sin


def _sigmoid(g):
    return 0.5 * jnp.tanh(0.5 * g) + 0.5


def _silu(g):
    return g * _sigmoid(g)


def _rstd(x):
    return lax.rsqrt(jnp.mean(x * x, axis=-1, keepdims=True) + EPS)


def _rope_tables(positions):
    pos = positions.astype(F32)[:, None]
    s = pos.shape[0]
    inv = ROPE_BASE ** (-jnp.arange(0, HEAD, 2, dtype=F32) / HEAD)
    ang = pos * inv
    c, sn = jnp.cos(ang), jnp.sin(ang)
    cos_r = jnp.tile(jnp.concatenate([c, c], -1), (1, 2))
    sin_r = jnp.tile(jnp.concatenate([sn, sn], -1), (1, 2))
    inv = ROPE_BASE ** (-jnp.arange(0, ROPE, 2, dtype=F32) / ROPE)
    ang = pos * inv
    c, sn = jnp.cos(ang), jnp.sin(ang)
    one, zero = jnp.ones((s, KPE_LO), F32), jnp.zeros((s, KPE_LO), F32)
    cos_m = jnp.concatenate([one, c, c, one[:, :LANES - KPE_LO - ROPE]], -1)
    sin_m = jnp.concatenate([zero, sn, sn, zero[:, :LANES - KPE_LO - ROPE]], -1)
    return cos_r, sin_r, cos_m, sin_m


def _ret_consts():
    c = RET_CHUNK
    lg = np.log1p(-np.power(2.0, -5.0 - np.arange(N_HEADS, dtype=np.float64)))
    idx = np.arange(c, dtype=np.float64)
    diff = idx[:, None] - idx[None, :]
    lane_head = np.arange(LANES) // HEAD
    dmask = np.zeros((4, 2, c, c))
    zeta = np.zeros((4, c, LANES))
    xi = np.zeros((4, c, LANES))
    cd = np.zeros((4, LANES, LANES))
    bd = (lane_head[:, None] == lane_head[None, :]).astype(np.float64)
    for j in range(4):
        for hh in range(2):
            dmask[j, hh] = np.where(diff >= 0, np.exp(lg[2 * j + hh] * np.maximum(diff, 0.0)), 0.0)
        lgl = lg[2 * j + lane_head]
        zeta[j] = np.exp(lgl[None, :] * (c - 1.0 - idx[:, None]))
        xi[j] = np.exp(lgl[None, :] * (idx[:, None] + 1.0))
        cd[j] = np.exp(lgl * c)[:, None] * bd
    f = lambda a: jnp.asarray(a, F32)
    side = lambda d: np.concatenate([d[:, 0], d[:, 1]], axis=-1)
    return dict(dmask=f(side(dmask)), dmask_t=f(side(np.swapaxes(dmask, 2, 3))), zeta=f(zeta), xi=f(xi), cd=f(cd), bd=f(bd))


def _f1_call(x, anw, win, cos_r, sin_r, cos_m, sin_m, ts):
    s = x.shape[0]

    def body(x_ref, anw_ref, w_ref, cr_ref, sr_ref, cm_ref, sm_ref,
             q_ref, k_ref, v_ref, g_ref, cq_ref, ckv_ref, kpe_ref, r_ref):
        xv = x_ref[...]
        r = _rstd(xv)
        r_ref[...] = r
        h = (xv * r * anw_ref[...]).astype(BF16)
        cr, sr = cr_ref[...], sr_ref[...]
        qk = _dot(h, w_ref[:, 0:2 * RET_W])
        for j in range(4):
            sl = slice(j * LANES, (j + 1) * LANES)
            q_ref[:, sl] = _rope(qk[:, sl], cr, sr, HEAD // 2).astype(BF16)
            kk = qk[:, RET_W + j * LANES:RET_W + (j + 1) * LANES]
            k_ref[:, sl] = (_rope(kk, cr, sr, HEAD // 2) * (HEAD ** -0.5)).astype(BF16)
        v_ref[...] = _dot(h, w_ref[:, 2 * RET_W:3 * RET_W]).astype(BF16)
        g_ref[...] = _dot(h, w_ref[:, 3 * RET_W:4 * RET_W])
        o = 4 * RET_W
        cq_ref[...] = _dot(h, w_ref[:, o:o + Q_RANK])
        ckv_ref[...] = _dot(h, w_ref[:, o + Q_RANK:o + Q_RANK + KV_RANK])
        kp = _dot(h, w_ref[:, o + Q_RANK + KV_RANK:IN_EXT])
        kpe_ref[...] = _rope(kp, cm_ref[...], sm_ref[...], ROPE // 2)

    sd = jax.ShapeDtypeStruct
    return pl.pallas_call(
        body, name="f1_in_proj", grid=(s // ts,),
        in_specs=[_row(ts, D_MODEL), _full((1, D_MODEL)), _full((D_MODEL, IN_EXT)),
                  _row(ts, LANES), _row(ts, LANES), _row(ts, LANES), _row(ts, LANES)],
        out_specs=[_row(ts, RET_W), _row(ts, RET_W), _row(ts, RET_W), _row(ts, RET_W),
                   _row(ts, Q_RANK), _row(ts, KV_RANK), _row(ts, LANES), _row(ts, 1)],
        out_shape=[sd((s, RET_W), BF16), sd((s, RET_W), BF16), sd((s, RET_W), BF16), sd((s, RET_W), F32),
                   sd((s, Q_RANK), F32), sd((s, KV_RANK), F32), sd((s, LANES), F32), sd((s, 1), F32)],
        compiler_params=_cp("parallel"),
    )(x, anw, win, cos_r, sin_r, cos_m, sin_m)


def _stack_heads(a):
    lo = lax.broadcasted_iota(jnp.int32, a.shape, 1) < HEAD
    zero = jnp.zeros_like(a)
    return jnp.concatenate([jnp.where(lo, a, zero), jnp.where(lo, zero, a)], axis=0)


def _pair_product(a, b2, decay2, w2):
    return _dot((_dot_nt(a, b2) * decay2).astype(BF16), w2)


RET_SLABS = 2


def _ret_specs(tr, tile_of):
    c, ns = RET_CHUNK, RET_SLABS
    return dict(
        slab=pl.BlockSpec((tr, ns * LANES), lambda j, i: (tile_of(i), j)),
        tab=pl.BlockSpec((tr, LANES), lambda j, i: (tile_of(i), 0)),
        vec=pl.BlockSpec((1, ns * LANES), lambda j, i: (0, j)),
        dmask=pl.BlockSpec((ns, c, 2 * c), lambda j, i: (j, 0, 0)),
        rows=pl.BlockSpec((ns, c, LANES), lambda j, i: (j, 0, 0)),
        state=pl.BlockSpec((ns, LANES, LANES), lambda j, i: (j, 0, 0)),
        bd=pl.BlockSpec((LANES, LANES), lambda j, i: (0, 0)))


def _ret_states(a_ref, b_ref, scale_ref, cd_ref, bd, st_ref, chunks, lanes, reverse):
    nc = len(chunks)
    contrib = [[_dot_tn((a_ref[rows, ln].astype(F32) * scale_ref[sl]).astype(BF16), b_ref[rows, ln]) * bd for rows in chunks]
               for sl, ln in enumerate(lanes)]
    states = []
    for sl in range(len(lanes)):
        st, seen = st_ref[sl], [None] * nc
        for ci in (reversed(range(nc)) if reverse else range(nc)):
            seen[ci] = st.astype(BF16)
            st = st * cd_ref[sl] + contrib[sl][ci]
        st_ref[sl] = st
        states.append(seen)
    return states


def _ret_fwd_call(q, k, v, g, gnw, rc, tr):
    s = q.shape[0]
    c = RET_CHUNK
    nc = tr // c
    ns = RET_SLABS

    def body(q_ref, k_ref, v_ref, g_ref, gnw_ref, dm_ref, zeta_ref, xi_ref, cd_ref, bd_ref, o_ref, y_ref, st_ref):
        @pl.when(pl.program_id(1) == 0)
        def _():
            st_ref[...] = jnp.zeros_like(st_ref)

        bd = bd_ref[...]
        chunks = [slice(ci * c, (ci + 1) * c) for ci in range(nc)]
        lanes = [slice(sl * LANES, (sl + 1) * LANES) for sl in range(ns)]
        states = _ret_states(k_ref, v_ref, zeta_ref, cd_ref, bd, st_ref, chunks, lanes, False)
        for ci, rows in enumerate(chunks):
            for sl, ln in enumerate(lanes):
                qc = q_ref[rows, ln]
                o_ref[rows, ln] = (_dot(qc, states[sl][ci]) * xi_ref[sl]
                                   + _pair_product(qc, _stack_heads(k_ref[rows, ln]), dm_ref[sl], _stack_heads(v_ref[rows, ln])))
        avg = bd * (1.0 / HEAD)
        for ln in lanes:
            o = o_ref[:, ln]
            ctr = o - _dot_hi(o, avg)
            var = _dot_hi(ctr * ctr, avg)
            y_ref[:, ln] = (_silu(g_ref[:, ln]) * (ctr * lax.rsqrt(var + EPS) * gnw_ref[:, ln])).astype(BF16)

    specs = _ret_specs(tr, lambda i: i)
    sd = jax.ShapeDtypeStruct
    return pl.pallas_call(
        body, name="ret_fwd", grid=(4 // ns, s // tr),
        in_specs=[specs["slab"]] * 4 + [specs["vec"], specs["dmask"], specs["rows"], specs["rows"], specs["state"], specs["bd"]],
        out_specs=[specs["slab"]] * 2,
        out_shape=[sd((s, RET_W), F32), sd((s, RET_W), BF16)],
        scratch_shapes=[pltpu.VMEM((ns, LANES, LANES), F32)],
        compiler_params=_cp("parallel", "arbitrary"),
    )(q, k, v, g, gnw, rc["dmask"], rc["zeta"], rc["xi"], rc["cd"], rc["bd"])


QK_AUX = HEAD + ROPE
V_AUX = HEAD


def _lane_pair(shape, lo, a, b, rest):
    lane = lax.broadcasted_iota(jnp.int32, shape, len(shape) - 1)
    return jnp.where(lane == lo, a, jnp.where(lane == lo + 1, b, rest))


def _hi_lo(v):
    hi = v.astype(BF16).astype(F32)
    return hi, v - hi


def _mla_pre_call(cq, ckv, kpe, qnw, kvnw, wq, wk, wv, cos_m, sin_m, ts):
    s = cq.shape[0]

    def body(cq_ref, ckv_ref, kpe_ref, qnw_ref, kvnw_ref, wq_ref, wk_ref, wv_ref, cm_ref, sm_ref, q_ref, k_ref, v_ref):
        cqv, ckvv = cq_ref[...], ckv_ref[...]
        cqn = (cqv * _rstd(cqv) * qnw_ref[...]).astype(BF16)
        ckvn = (ckvv * _rstd(ckvv) * kvnw_ref[...]).astype(BF16)
        cm, sm = cm_ref[...], sm_ref[...]
        kp = _lane_pair((ts, LANES), QK_AUX, -1.0, -1.0, kpe_ref[...])
        for h in range(N_HEADS):
            qh = _rope(_dot(cqn, wq_ref[h]), cm, sm, ROPE // 2)
            q_ref[h] = (qh * (SM_SCALE * LOG2E)).astype(BF16)
            k_ref[h] = (_dot(ckvn, wk_ref[h]) + kp).astype(BF16)
            v_ref[h] = _lane_pair((ts, LANES), V_AUX, 1.0, 1.0, _dot(ckvn, wv_ref[h])).astype(BF16)

    sd = jax.ShapeDtypeStruct
    hm = sd((N_HEADS, s, LANES), BF16)
    return pl.pallas_call(
        body, name="mla_pre", grid=(s // ts,),
        in_specs=[_row(ts, Q_RANK), _row(ts, KV_RANK), _row(ts, LANES), _full((1, Q_RANK)), _full((1, KV_RANK)),
                  _full((N_HEADS, Q_RANK, LANES)), _full((N_HEADS, KV_RANK, LANES)), _full((N_HEADS, KV_RANK, LANES)),
                  _row(ts, LANES), _row(ts, LANES)],
        out_specs=[_hrow(N_HEADS, ts, LANES)] * 3,
        out_shape=[hm, hm, hm],
        compiler_params=_cp("parallel"),
    )(cq, ckv, kpe, qnw, kvnw, wq, wk, wv, cos_m, sin_m)


def _flash_fwd_call(q, k, v, tb):
    s = q.shape[1]
    nb = s // tb
    pairs = [(a, b) for a in range(nb) for b in range(a + 1)]
    qi_of, ki_of = (jnp.asarray(np.array(col, np.int32)) for col in zip(*pairs))

    def body(qi_ref, ki_ref, q_ref, k_ref, v_ref, o_ref, qb_ref, m_ref, acc_ref):
        qi, ki = qi_ref[pl.program_id(0)], ki_ref[pl.program_id(0)]

        @pl.when(ki == 0)
        def _():
            m_ref[...] = jnp.full_like(m_ref, NEG)
            acc_ref[...] = jnp.zeros_like(acc_ref)

        def step(masked):
            if masked:
                keep = lax.broadcasted_iota(jnp.int32, (tb, tb), 1) <= lax.broadcasted_iota(jnp.int32, (tb, tb), 0)
            def finish(h, pe, alpha):
                acc_ref[h] = acc_ref[h] * alpha + _dot(pe, v_ref[h])

            nxt, pending = _dot_nt(q_ref[0], k_ref[0]), None
            for h in range(N_HEADS):
                sc = nxt
                if h + 1 < N_HEADS:
                    nxt = _dot_nt(q_ref[h + 1], k_ref[h + 1])
                if masked:
                    sc = jnp.where(keep, sc, NEG)
                m_prev = m_ref[h]
                m_new = jnp.maximum(m_prev, jnp.max(sc, axis=1, keepdims=True))
                pe = jnp.exp2(sc - jnp.tile(m_new, (1, tb // LANES))).astype(BF16)
                m_ref[h] = m_new
                if pending is not None:
                    finish(*pending)
                pending = (h, pe, jnp.exp2(m_prev - m_new))
            finish(*pending)

        @pl.when(ki < qi)
        def _():
            step(False)

        @pl.when(ki == qi)
        def _():
            step(True)
            lane = lax.broadcasted_iota(jnp.int32, (tb, LANES), 1)
            for p in range(N_HEADS // 2):
                outs = []
                for h in (2 * p, 2 * p + 1):
                    acc = acc_ref[h]
                    l = acc[:, V_AUX:V_AUX + 1]
                    outs.append(acc * (1.0 / l))
                    hi, lo = _hi_lo(m_ref[h][:, 0:1] + jnp.log(l) * LOG2E)
                    qb_ref[h] = _lane_pair((tb, LANES), QK_AUX, hi, lo, q_ref[h].astype(F32)).astype(BF16)
                o_ref[:, p * LANES:(p + 1) * LANES] = jnp.where(lane < HEAD, outs[0], pltpu.roll(outs[1], HEAD, 1)).astype(BF16)

    sd = jax.ShapeDtypeStruct
    qspec = pl.BlockSpec((N_HEADS, tb, LANES), lambda p, qi_ref, ki_ref: (0, qi_ref[p], 0))
    kspec = pl.BlockSpec((N_HEADS, tb, LANES), lambda p, qi_ref, ki_ref: (0, ki_ref[p], 0))
    return pl.pallas_call(
        body, name="mla_flash_fwd",
        grid_spec=pltpu.PrefetchScalarGridSpec(
            num_scalar_prefetch=2, grid=(len(pairs),),
            in_specs=[qspec, kspec, kspec],
            out_specs=[pl.BlockSpec((tb, MLA_W), lambda p, qi_ref, ki_ref: (qi_ref[p], 0)), qspec],
            scratch_shapes=[pltpu.VMEM((N_HEADS, tb, LANES), F32), pltpu.VMEM((N_HEADS, tb, LANES), F32)]),
        out_shape=[sd((s, MLA_W), BF16), sd((N_HEADS, s, LANES), BF16)],
        compiler_params=_cp("arbitrary"),
    )(qi_of, ki_of, q, k, v)


def _out_proj_call(x, yret, ymla, wout, ts):
    s = x.shape[0]

    def body(x_ref, yr_ref, ym_ref, w_ref, x1_ref, r_ref):
        x1 = x_ref[...] + _dot(yr_ref[...], w_ref[0:RET_W, :]) + _dot(ym_ref[...], w_ref[RET_W:, :])
        x1_ref[...] = x1
        r_ref[...] = _rstd(x1)

    sd = jax.ShapeDtypeStruct
    return pl.pallas_call(
        body, name="out_proj", grid=(s // ts,),
        in_specs=[_row(ts, D_MODEL), _row(ts, RET_W), _row(ts, MLA_W), _full((D_MODEL, D_MODEL))],
        out_specs=[_row(ts, D_MODEL), _row(ts, 1)],
        out_shape=[sd((s, D_MODEL), F32), sd((s, 1), F32)],
        compiler_params=_cp("parallel"),
    )(x, yret, ymla, wout)


W_UP_SHARD = F2 // 4


def _ffn_fwd_call(x1, r2, fnw, wup4, cw, cb, wdown, ts):
    s = x1.shape[0]
    wsh = W_UP_SHARD

    def body(x_ref, r_ref, fnw_ref, wup_ref, cw_ref, cb_ref, wd_ref, u_ref, uc_ref, x2_ref, carry_ref):
        _zero_first(pl.program_id(0) == 0, carry_ref)
        xv = x_ref[...]
        h = (xv * r_ref[...] * fnw_ref[...]).astype(BF16)
        conv = []
        for j in range(4):
            cols = slice(j * wsh, (j + 1) * wsh)
            ub = _dot(h, wup_ref[j]).astype(BF16)
            u_ref[:, cols] = ub
            u = ub.astype(F32)
            u1, u2 = _shifted(u, carry_ref[:, cols])
            w = cw_ref[:, cols]
            cb16 = (cb_ref[:, cols] + w[0:1, :] * u2 + w[1:2, :] * u1 + w[2:3, :] * u).astype(BF16)
            uc_ref[:, cols] = cb16
            conv.append(cb16.astype(F32))
            carry_ref[:, cols] = u[ts - 8:, :]
        acc = xv
        for j in range(2):
            a = (_silu(conv[j]) * conv[j + 2]).astype(BF16)
            acc = acc + _dot(a, wd_ref[j * wsh:(j + 1) * wsh, :])
        x2_ref[...] = acc

    sd = jax.ShapeDtypeStruct
    return pl.pallas_call(
        body, name="ffn_fwd", grid=(s // ts,),
        in_specs=[_row(ts, D_MODEL), _row(ts, 1), _full((1, D_MODEL)), _full((4, D_MODEL, wsh)),
                  _full((3, F2)), _full((1, F2)), _full((D_FF, D_MODEL))],
        out_specs=[_row(ts, F2), _row(ts, F2), _row(ts, D_MODEL)],
        out_shape=[sd((s, F2), BF16), sd((s, F2), BF16), sd((s, D_MODEL), F32)],
        scratch_shapes=[pltpu.VMEM((8, F2), F32)],
        compiler_params=_cp("arbitrary"),
    )(x1, r2, fnw, wup4, cw, cb, wdown)


def _shifted(u, hal):
    row = lax.broadcasted_iota(jnp.int32, hal.shape, 0)
    r1, r2 = pltpu.roll(u, 1, 0), pltpu.roll(u, 2, 0)
    top1 = jnp.where(row == 0, hal[7:8, :], r1[0:8, :])
    top2 = jnp.where(row == 0, hal[6:7, :], jnp.where(row == 1, hal[7:8, :], r2[0:8, :]))
    return jnp.concatenate([top1, r1[8:, :]], axis=0), jnp.concatenate([top2, r2[8:, :]], axis=0)


def _prep_weights(w):
    win = w["w_in"]
    pad = lambda n: jnp.zeros((D_MODEL, n), win.dtype)
    win_ext = jnp.concatenate([win[:, :IN_W - ROPE], pad(KPE_LO), win[:, IN_W - ROPE:], pad(LANES - KPE_LO - ROPE)], -1)
    wuq = w["w_uq"].reshape(Q_RANK, N_HEADS, HEAD + ROPE)
    wq = jnp.concatenate([wuq, jnp.zeros((Q_RANK, N_HEADS, LANES - HEAD - ROPE), wuq.dtype)], -1).transpose(1, 0, 2)
    wukv = w["w_ukv"].reshape(KV_RANK, N_HEADS, 2 * HEAD)
    zk = jnp.zeros((KV_RANK, N_HEADS, HEAD), wukv.dtype)
    wk = jnp.concatenate([wukv[:, :, :HEAD], zk], -1).transpose(1, 0, 2)
    wv = jnp.concatenate([wukv[:, :, HEAD:], zk], -1).transpose(1, 0, 2)
    c = lambda a: a.astype(BF16)
    return dict(win=c(win_ext), wq=c(wq), wk=c(wk), wv=c(wv), wout=c(w["w_out"]))


def _prep_mlp_weights(w):
    wup = w["w_up"]
    if wup.ndim == 2:
        wup = wup.reshape(D_MODEL, 4, W_UP_SHARD).transpose(1, 0, 2)
    return dict(wup=wup.astype(BF16), wdown=w["w_down"].astype(BF16))


def _tiles(s):
    return dict(ts=min(s, 512), tr=min(s, 1024), tb=min(s, 512), t2=min(s, 256))


class _Exchanges:
    def __init__(self, w):
        self.w = w

    def mlp_weights(self, after):
        return self.w

    def mlp_grads(self, gw):
        pass

    def behind_out_bwd(self, after):
        pass

    def behind_attention(self, after):
        pass


def _forward(x, positions, w, small, ex):
    s = x.shape[0]
    t = _tiles(s)
    pw = _prep_weights(w)
    cos_r, sin_r, cos_m, sin_m = _rope_tables(positions)
    rc = _ret_consts()
    q, k, v, g, cq, ckv, kpe, r1 = _f1_call(x, small["attn_norm_w"], pw["win"], cos_r, sin_r, cos_m, sin_m, t["ts"])
    o_ret, y_ret = _ret_fwd_call(q, k, v, g, small["ret_gn_w"], rc, t["tr"])
    mq, mk, mv = _mla_pre_call(cq, ckv, kpe, small["mla_q_norm_w"], small["mla_kv_norm_w"],
                               pw["wq"], pw["wk"], pw["wv"], cos_m, sin_m, t["ts"])
    y_mla, mqb = _flash_fwd_call(mq, mk, mv, t["tb"])
    x1, r2 = _out_proj_call(x, y_ret, y_mla, pw["wout"], t["ts"])
    pw.update(_prep_mlp_weights(ex.mlp_weights(r2)))
    u, uc, x2 = _ffn_fwd_call(x1, r2, small["ffn_norm_w"], pw["wup"], w["conv_w"], small["conv_b"], pw["wdown"], t["ts"])
    return dict(pw=pw, tabs=(cos_r, sin_r, cos_m, sin_m), rc=rc, q=q, k=k, v=v, g=g, cq=cq, ckv=ckv, kpe=kpe, r1=r1,
                o_ret=o_ret, y_ret=y_ret, mqb=mqb, mk=mk, mv=mv, y_mla=y_mla, x1=x1, r2=r2, u=u, uc=uc, x2=x2)


def _norm_bwd(dh, xh, r, nw):
    dxn = dh * nw
    return r * (dxn - xh * jnp.mean(dxn * xh, axis=-1, keepdims=True))


def _ordered_after(body, order):
    if order is None:
        return body, [], []
    return (lambda order_ref, *refs: body(*refs)), [pl.BlockSpec(memory_space=pl.ANY)], [order]


def _zero_first(first, *refs):
    @pl.when(first)
    def _():
        for ref in refs:
            ref[...] = jnp.zeros_like(ref)


def _colsum(v):
    return jnp.sum(v, axis=0, keepdims=True)


def _dsilu(g, sg):
    return sg * (1.0 + g * (1.0 - sg))


def _loss_call(x2, tgt, fw, ts):
    s = x2.shape[0]

    def body(x_ref, t_ref, fw_ref, dx_ref, loss_ref, gfw_ref):
        _zero_first(pl.program_id(0) == 0, loss_ref, gfw_ref)
        xv = x_ref[...]
        r = _rstd(xv)
        xh = xv * r
        fwv = fw_ref[...]
        e = xh * fwv - t_ref[...]
        loss_ref[...] += (0.5 / D_MODEL) * _colsum(jnp.sum(e * e, axis=1, keepdims=True))
        dy = e * (1.0 / D_MODEL)
        gfw_ref[...] += _colsum(dy * xh)
        dx_ref[...] = _norm_bwd(dy, xh, r, fwv)

    sd = jax.ShapeDtypeStruct
    return pl.pallas_call(
        body, name="loss_bwd", grid=(s // ts,),
        in_specs=[_row(ts, D_MODEL), _row(ts, D_MODEL), _full((1, D_MODEL))],
        out_specs=[_row(ts, D_MODEL), _full((1, 1)), _full((1, D_MODEL))],
        out_shape=[sd((s, D_MODEL), F32), sd((1, 1), F32), sd((1, D_MODEL), F32)],
        compiler_params=_cp("arbitrary"),
    )(x2, tgt, fw)


def _ffn_bwd_call(dx2, u, uc, cw, wdown, wup4, x1, r2, fnw, ts):
    s = dx2.shape[0]
    nt = s // ts
    wsh = W_UP_SHARD
    rev = lambda i: nt - 1 - i

    def body(dx2_ref, u_ref, uc_ref, cw_ref, wd_ref, wup_ref, x_ref, r_ref, fnw_ref,
             du_ref, dx1_ref, dcw_ref, dcb_ref, dfnw_ref, dwd_hbm, carry_ref, dwd_ref, sem):
        i = pl.program_id(0)
        _zero_first(i == 0, carry_ref, dwd_ref, dcw_ref, dcb_ref, dfnw_ref)
        dxb = dx2_ref[...].astype(BF16)
        dh = jnp.zeros((ts, D_MODEL), F32)
        for j in range(2):
            gcols = slice(j * wsh, (j + 1) * wsh)
            vcols = slice(D_FF + j * wsh, D_FF + (j + 1) * wsh)
            gate, val = uc_ref[:, gcols].astype(F32), uc_ref[:, vcols].astype(F32)
            da = _dot_nt(dxb, wd_ref[gcols, :])
            sg = _sigmoid(gate)
            sl = gate * sg
            dwd_ref[gcols, :] += _dot_tn((sl * val).astype(BF16), dxb)
            for d, cols, shard in ((da * val * _dsilu(gate, sg), gcols, j), (da * sl, vcols, 2 + j)):
                d1, d2 = _shifted_up(d, carry_ref[:, cols])
                uv = u_ref[:, cols].astype(F32)
                for t, dt in enumerate((d2, d1, d)):
                    dcw_ref[t:t + 1, cols] += _colsum(dt * uv)
                dcb_ref[:, cols] += _colsum(d)
                w = cw_ref[:, cols]
                du = (w[2:3, :] * d + w[1:2, :] * d1 + w[0:1, :] * d2).astype(BF16)
                du_ref[:, cols] = du
                dh = dh + _dot_nt(du, wup_ref[shard])
                carry_ref[:, cols] = d[0:8, :]
        r = r_ref[...]
        xh = x_ref[...] * r
        dfnw_ref[...] += _colsum(dh * xh)
        dx1_ref[...] = dx2_ref[...] + _norm_bwd(dh, xh, r, fnw_ref[...])

        @pl.when(i == nt - 1)
        def _():
            cp = pltpu.make_async_copy(dwd_ref, dwd_hbm, sem)
            cp.start()
            cp.wait()

    sd = jax.ShapeDtypeStruct
    row = lambda c: pl.BlockSpec((ts, c), lambda i: (rev(i), 0))
    once = lambda shape: pl.BlockSpec(shape, lambda i: (0,) * len(shape), pipeline_mode=pl.Buffered(1))
    return pl.pallas_call(
        body, name="ffn_bwd", grid=(nt,),
        in_specs=[row(D_MODEL), row(F2), row(F2), once((3, F2)), once((D_FF, D_MODEL)), once((4, D_MODEL, wsh)),
                  row(D_MODEL), row(1), once((1, D_MODEL))],
        out_specs=[row(F2), row(D_MODEL), _full((3, F2)), _full((1, F2)), _full((1, D_MODEL)), pl.BlockSpec(memory_space=pl.ANY)],
        out_shape=[sd((s, F2), BF16), sd((s, D_MODEL), F32), sd((3, F2), F32), sd((1, F2), F32), sd((1, D_MODEL), F32),
                   sd((D_FF, D_MODEL), F32)],
        scratch_shapes=[pltpu.VMEM((8, F2), F32), pltpu.VMEM((D_FF, D_MODEL), F32), pltpu.SemaphoreType.DMA],
        compiler_params=_cp("arbitrary", vmem=VMEM_LIMIT_MLP_BWD),
    )(dx2, u, uc, cw, wdown, wup4, x1, r2, fnw)


def _shifted_up(d, hal):
    n = d.shape[0]
    row = lax.broadcasted_iota(jnp.int32, hal.shape, 0)
    r1, r2 = pltpu.roll(d, n - 1, 0), pltpu.roll(d, n - 2, 0)
    end1 = jnp.where(row == 7, hal[0:1, :], r1[n - 8:, :])
    end2 = jnp.where(row == 6, hal[0:1, :], jnp.where(row == 7, hal[1:2, :], r2[n - 8:, :]))
    return jnp.concatenate([r1[:n - 8, :], end1], axis=0), jnp.concatenate([r2[:n - 8, :], end2], axis=0)


def _dw_norm_call(x, r, nw, b, ts, tn, name):
    s, n = b.shape
    k = x.shape[1]

    def body(x_ref, r_ref, nw_ref, b_ref, dw_ref):
        _zero_first(pl.program_id(1) == 0, dw_ref)
        h = (x_ref[...] * r_ref[...] * nw_ref[...]).astype(BF16)
        dw_ref[...] += _dot_tn(h, b_ref[...])

    return pl.pallas_call(
        body, name=name, grid=(n // tn, s // ts),
        in_specs=[pl.BlockSpec((ts, k), lambda j, i: (i, 0)), pl.BlockSpec((ts, 1), lambda j, i: (i, 0)),
                  pl.BlockSpec((1, k), lambda j, i: (0, 0)), pl.BlockSpec((ts, tn), lambda j, i: (i, j))],
        out_specs=pl.BlockSpec((None, k, tn), lambda j, i: (j, 0, 0)),
        out_shape=jax.ShapeDtypeStruct((n // tn, k, tn), F32),
        compiler_params=_cp("parallel", "arbitrary"),
    )(x, r, nw, b)


def _out_bwd_call(dx1, yret, ymla, wout, ts, order=None):
    s = dx1.shape[0]

    def body(dx_ref, yr_ref, ym_ref, w_ref, dyr_ref, do_ref, dwo_ref):
        _zero_first(pl.program_id(0) == 0, dwo_ref)
        dxb = dx_ref[...].astype(BF16)
        dmix = _dot_nt(dxb, w_ref[...])
        dyr_ref[...] = dmix[:, :RET_W]
        ym = ym_ref[...]
        lane = lax.broadcasted_iota(jnp.int32, (ts, LANES), 1)
        for p in range(N_HEADS // 2):
            dom = dmix[:, RET_W + p * LANES:RET_W + (p + 1) * LANES]
            prod = dom * ym[:, p * LANES:(p + 1) * LANES].astype(F32)
            for hh in range(2):
                mine = (lane >= HEAD) if hh else (lane < HEAD)
                hi, lo = _hi_lo(jnp.sum(jnp.where(mine, prod, 0.0), axis=1, keepdims=True))
                base = jnp.where(lane < HEAD, pltpu.roll(dom, HEAD, 1) if hh else dom, 0.0)
                do_ref[2 * p + hh] = _lane_pair((ts, LANES), V_AUX, -hi, -lo, base).astype(BF16)
        dwo_ref[0:RET_W, :] += _dot_tn(yr_ref[...], dxb)
        dwo_ref[RET_W:, :] += _dot_tn(ym, dxb)

    sd = jax.ShapeDtypeStruct
    body, first_specs, first = _ordered_after(body, order)
    return pl.pallas_call(
        body, name="out_proj_bwd", grid=(s // ts,),
        in_specs=first_specs + [_row(ts, D_MODEL), _row(ts, RET_W), _row(ts, MLA_W), _full((D_MODEL, D_MODEL))],
        out_specs=[_row(ts, RET_W), _hrow(N_HEADS, ts, LANES), _full((D_MODEL, D_MODEL))],
        out_shape=[sd((s, RET_W), F32), sd((N_HEADS, s, LANES), BF16), sd((D_MODEL, D_MODEL), F32)],
        compiler_params=_cp("arbitrary"),
    )(*first, dx1, yret, ymla, wout)


def _ret_bwd_q_call(q, k, v, o, g, dy, gnw, rc, cos_r, sin_r, tr):
    s = q.shape[0]
    c = RET_CHUNK
    nc = tr // c
    ns = RET_SLABS

    def body(q_ref, k_ref, v_ref, o_ref, g_ref, dy_ref, gnw_ref, dm_ref, zeta_ref, xi_ref, cd_ref, bd_ref, cr_ref, sr_ref,
             dq_ref, dg_ref, do_ref, dgnw_ref, st_ref):
        _zero_first(pl.program_id(1) == 0, st_ref, dgnw_ref)
        bd = bd_ref[...]
        avg = bd * (1.0 / HEAD)
        chunks = [slice(ci * c, (ci + 1) * c) for ci in range(nc)]
        lanes = [slice(sl * LANES, (sl + 1) * LANES) for sl in range(ns)]
        dov = []
        for ln in lanes:
            ov = o_ref[:, ln]
            ctr = ov - _dot_hi(ov, avg)
            rs = lax.rsqrt(_dot_hi(ctr * ctr, avg) + EPS)
            oh = ctr * rs
            gg, dyv, gnw_v = g_ref[:, ln], dy_ref[:, ln], gnw_ref[:, ln]
            sg = _sigmoid(gg)
            sl = gg * sg
            dg_ref[:, ln] = (dyv * oh * gnw_v * _dsilu(gg, sg)).astype(BF16)
            dgnw_ref[:, ln] += _colsum(dyv * sl * oh)
            doh = dyv * sl * gnw_v
            dov.append((rs * (doh - _dot_hi(doh, avg) - oh * _dot_hi(doh * oh, avg))).astype(BF16))
            do_ref[:, ln] = dov[-1]
        states = _ret_states(k_ref, v_ref, zeta_ref, cd_ref, bd, st_ref, chunks, lanes, False)
        for ci, rows in enumerate(chunks):
            for sl, ln in enumerate(lanes):
                doc = dov[sl][rows, :]
                dq = (_dot_nt(doc, states[sl][ci]) * xi_ref[sl]
                      + _pair_product(doc, _stack_heads(v_ref[rows, ln]), dm_ref[sl], _stack_heads(k_ref[rows, ln])))
                dq_ref[rows, ln] = _unrope(dq, cr_ref[rows, :], sr_ref[rows, :], HEAD // 2).astype(BF16)

    specs = _ret_specs(tr, lambda i: i)
    sd = jax.ShapeDtypeStruct
    return pl.pallas_call(
        body, name="ret_bwd_q", grid=(4 // ns, s // tr),
        in_specs=[specs["slab"]] * 6 + [specs["vec"], specs["dmask"], specs["rows"], specs["rows"], specs["state"], specs["bd"],
                                        specs["tab"], specs["tab"]],
        out_specs=[specs["slab"]] * 3 + [specs["vec"]],
        out_shape=[sd((s, RET_W), BF16), sd((s, RET_W), BF16), sd((s, RET_W), BF16), sd((1, RET_W), F32)],
        scratch_shapes=[pltpu.VMEM((ns, LANES, LANES), F32)],
        compiler_params=_cp("parallel", "arbitrary"),
    )(q, k, v, o, g, dy, gnw, rc["dmask"], rc["zeta"], rc["xi"], rc["cd"], rc["bd"], cos_r, sin_r)


def _ret_bwd_kv_call(q, k, v, do, rc, cos_r, sin_r, tr):
    s = q.shape[0]
    c = RET_CHUNK
    nc = tr // c
    nt = s // tr
    ns = RET_SLABS

    def body(q_ref, k_ref, v_ref, do_ref, dm_ref, zeta_ref, xi_ref, cd_ref, bd_ref, cr_ref, sr_ref, dk_ref, dv_ref, gs_ref):
        _zero_first(pl.program_id(1) == 0, gs_ref)
        bd = bd_ref[...]
        chunks = [slice(ci * c, (ci + 1) * c) for ci in range(nc)]
        lanes = [slice(sl * LANES, (sl + 1) * LANES) for sl in range(ns)]
        states = _ret_states(q_ref, do_ref, xi_ref, cd_ref, bd, gs_ref, chunks, lanes, True)
        for ci, rows in enumerate(chunks):
            for sl, ln in enumerate(lanes):
                kc, vc = k_ref[rows, ln], v_ref[rows, ln]
                q2, do2 = _stack_heads(q_ref[rows, ln]), _stack_heads(do_ref[rows, ln])
                gb = states[sl][ci]
                dk = _dot_nt(vc, gb) * zeta_ref[sl] + _pair_product(vc, do2, dm_ref[sl], q2)
                dv = _dot(kc, gb) * zeta_ref[sl] + _pair_product(kc, q2, dm_ref[sl], do2)
                dk_ref[rows, ln] = (_unrope(dk, cr_ref[rows, :], sr_ref[rows, :], HEAD // 2) * (HEAD ** -0.5)).astype(BF16)
                dv_ref[rows, ln] = dv.astype(BF16)

    specs = _ret_specs(tr, lambda i: nt - 1 - i)
    sd = jax.ShapeDtypeStruct
    return pl.pallas_call(
        body, name="ret_bwd_kv", grid=(4 // ns, nt),
        in_specs=[specs["slab"]] * 4 + [specs["dmask"], specs["rows"], specs["rows"], specs["state"], specs["bd"],
                                        specs["tab"], specs["tab"]],
        out_specs=[specs["slab"]] * 2,
        out_shape=[sd((s, RET_W), BF16), sd((s, RET_W), BF16)],
        scratch_shapes=[pltpu.VMEM((ns, LANES, LANES), F32)],
        compiler_params=_cp("parallel", "arbitrary"),
    )(q, k, v, do, rc["dmask_t"], rc["zeta"], rc["xi"], rc["cd"], rc["bd"], cos_r, sin_r)


FLASH_BWD_HEADS = 8


def _flash_bwd_call(qb, k, v, do, tb, order=None):
    s = qb.shape[1]
    nb = s // tb
    hg = FLASH_BWD_HEADS
    pairs = [(a, b) for a in range(nb) for b in range(a, nb)]
    ki_of, qi_of = (jnp.asarray(np.array(col, np.int32)) for col in zip(*pairs))
    extra = [] if order is None else [order]

    def body(ki_ref, qi_ref, *refs):
        q_ref, k_ref, v_ref, do_ref, dk_ref, dv_ref, dq_hbm, dka_ref, dva_ref, dq_ref, sem = refs[len(extra):]
        g, p = pl.program_id(0), pl.program_id(1)
        ki, qi = ki_ref[p], qi_ref[p]
        _zero_first(p == 0, dq_ref)
        _zero_first(qi == ki, dka_ref, dva_ref)
        rows = pl.ds(pl.multiple_of(qi * tb, tb), tb)

        def step(masked):
            if masked:
                keep = lax.broadcasted_iota(jnp.int32, (tb, tb), 0) <= lax.broadcasted_iota(jnp.int32, (tb, tb), 1)
            for h in range(hg):
                st = _dot_nt(k_ref[h], q_ref[h])
                if masked:
                    st = jnp.where(keep, st, NEG)
                pt = jnp.exp2(st)
                dob = do_ref[h]
                dva_ref[h] += _dot(pt.astype(BF16), dob)
                dst = (pt * _dot_nt(v_ref[h], dob)).astype(BF16)
                dka_ref[h] += _dot(dst, q_ref[h])
                dq_ref[h, rows, :] += _dot_tn(dst, k_ref[h])

        @pl.when(qi > ki)
        def _():
            step(False)

        @pl.when(qi == ki)
        def _():
            step(True)

        @pl.when(qi == nb - 1)
        def _():
            dk_ref[...] = (dka_ref[...] * LN2).astype(BF16)
            dv_ref[...] = dva_ref[...].astype(BF16)

        @pl.when(p == len(pairs) - 1)
        def _():
            cp = pltpu.make_async_copy(dq_ref, dq_hbm.at[pl.ds(g * hg, hg)], sem)
            cp.start()
            cp.wait()

    kspec = pl.BlockSpec((hg, tb, LANES), lambda g, p, ki_ref, qi_ref: (g, ki_ref[p], 0))
    qspec = pl.BlockSpec((hg, tb, LANES), lambda g, p, ki_ref, qi_ref: (g, qi_ref[p], 0))
    hm = jax.ShapeDtypeStruct((N_HEADS, s, LANES), BF16)
    return pl.pallas_call(
        body, name="mla_flash_bwd",
        grid_spec=pltpu.PrefetchScalarGridSpec(
            num_scalar_prefetch=2, grid=(N_HEADS // hg, len(pairs)),
            in_specs=[ANY] * len(extra) + [qspec, kspec, kspec, qspec],
            out_specs=[kspec, kspec, ANY],
            scratch_shapes=[pltpu.VMEM((hg, tb, LANES), F32), pltpu.VMEM((hg, tb, LANES), F32),
                            pltpu.VMEM((hg, s, LANES), F32), pltpu.SemaphoreType.DMA]),
        out_shape=[hm, hm, jax.ShapeDtypeStruct((N_HEADS, s, LANES), F32)],
        compiler_params=_cp("arbitrary", "arbitrary"),
    )(ki_of, qi_of, *extra, qb, k, v, do)


def _mla_post_call(dq, dk, dv, cq, ckv, qnw, kvnw, wq, wk, wv, cos_m, sin_m, ts):
    s = cq.shape[0]

    def body(dq_ref, dk_ref, dv_ref, cq_ref, ckv_ref, qnw_ref, kvnw_ref, wq_ref, wk_ref, wv_ref, cm_ref, sm_ref,
             dcq_ref, dckv_ref, dkpe_ref, dwq_ref, dwk_ref, dwv_ref, dqnw_ref, dkvnw_ref):
        _zero_first(pl.program_id(0) == 0, dwq_ref, dwk_ref, dwv_ref, dqnw_ref, dkvnw_ref)
        cqv, ckvv = cq_ref[...], ckv_ref[...]
        rq, rkv = _rstd(cqv), _rstd(ckvv)
        qh_, kvh_ = cqv * rq, ckvv * rkv
        qnw_v, kvnw_v = qnw_ref[...], kvnw_ref[...]
        cqn = (qh_ * qnw_v).astype(BF16)
        ckvn = (kvh_ * kvnw_v).astype(BF16)
        cm, sm = cm_ref[...], sm_ref[...]
        dcqn = jnp.zeros((ts, Q_RANK), F32)
        dckvn = jnp.zeros((ts, KV_RANK), F32)
        dkpe = jnp.zeros((ts, LANES), F32)
        for h in range(N_HEADS):
            dqu = _unrope(dq_ref[h] * SM_SCALE, cm, sm, ROPE // 2).astype(BF16)
            dwq_ref[h] += _dot_tn(cqn, dqu)
            dcqn = dcqn + _dot_nt(dqu, wq_ref[h])
            dkb, dvb = dk_ref[h], dv_ref[h]
            dkpe = dkpe + dkb.astype(F32)
            dwk_ref[h] += _dot_tn(ckvn, dkb)
            dwv_ref[h] += _dot_tn(ckvn, dvb)
            dckvn = dckvn + _dot_nt(dkb, wk_ref[h]) + _dot_nt(dvb, wv_ref[h])
        lane = lax.broadcasted_iota(jnp.int32, (ts, LANES), 1)
        dkpe = jnp.where((lane >= KPE_LO) & (lane < KPE_LO + ROPE), dkpe, 0.0)
        dkpe_ref[...] = _unrope(dkpe, cm, sm, ROPE // 2).astype(BF16)
        dqnw_ref[...] += _colsum(dcqn * qh_)
        dkvnw_ref[...] += _colsum(dckvn * kvh_)
        dcq_ref[...] = _norm_bwd(dcqn, qh_, rq, qnw_v).astype(BF16)
        dckv_ref[...] = _norm_bwd(dckvn, kvh_, rkv, kvnw_v).astype(BF16)

    sd = jax.ShapeDtypeStruct
    hm = _hrow(N_HEADS, ts, LANES)
    return pl.pallas_call(
        body, name="mla_post", grid=(s // ts,),
        in_specs=[hm, hm, hm, _row(ts, Q_RANK), _row(ts, KV_RANK), _full((1, Q_RANK)), _full((1, KV_RANK)),
                  _full((N_HEADS, Q_RANK, LANES)), _full((N_HEADS, KV_RANK, LANES)), _full((N_HEADS, KV_RANK, LANES)),
                  _row(ts, LANES), _row(ts, LANES)],
        out_specs=[_row(ts, Q_RANK), _row(ts, KV_RANK), _row(ts, LANES),
                   _full((N_HEADS, Q_RANK, LANES)), _full((N_HEADS, KV_RANK, LANES)), _full((N_HEADS, KV_RANK, LANES)),
                   _full((1, Q_RANK)), _full((1, KV_RANK))],
        out_shape=[sd((s, Q_RANK), BF16), sd((s, KV_RANK), BF16), sd((s, LANES), BF16),
                   sd((N_HEADS, Q_RANK, LANES), F32), sd((N_HEADS, KV_RANK, LANES), F32), sd((N_HEADS, KV_RANK, LANES), F32),
                   sd((1, Q_RANK), F32), sd((1, KV_RANK), F32)],
        compiler_params=_cp("arbitrary"),
    )(dq, dk, dv, cq, ckv, qnw, kvnw, wq, wk, wv, cos_m, sin_m)


def _in_bwd_call(parts, x, r1, anw, dx1, win, ts):
    s = x.shape[0]
    widths = [p.shape[1] for p in parts]
    np_ = len(parts)

    def body(*refs):
        p_refs = refs[:np_]
        x_ref, r_ref, anw_ref, dx1_ref, w_ref, dx_ref, dw_ref, danw_ref = refs[np_:]
        _zero_first(pl.program_id(0) == 0, dw_ref, danw_ref)
        dproj = jnp.concatenate([p[...] for p in p_refs], axis=-1)
        r, anw_v = r_ref[...], anw_ref[...]
        xh = x_ref[...] * r
        dw_ref[...] += _dot_tn((xh * anw_v).astype(BF16), dproj)
        dh = _dot_nt(dproj, w_ref[...])
        danw_ref[...] += _colsum(dh * xh)
        dx_ref[...] = dx1_ref[...] + _norm_bwd(dh, xh, r, anw_v)

    sd = jax.ShapeDtypeStruct
    return pl.pallas_call(
        body, name="in_proj_bwd", grid=(s // ts,),
        in_specs=[_row(ts, w) for w in widths]
        + [_row(ts, D_MODEL), _row(ts, 1), _full((1, D_MODEL)), _row(ts, D_MODEL), _full((D_MODEL, IN_EXT))],
        out_specs=[_row(ts, D_MODEL), _full((D_MODEL, IN_EXT)), _full((1, D_MODEL))],
        out_shape=[sd((s, D_MODEL), F32), sd((D_MODEL, IN_EXT), F32), sd((1, D_MODEL), F32)],
        compiler_params=_cp("arbitrary"),
    )(*parts, x, r1, anw, dx1, win)


def _local_step(x, positions, tgt, w, small, ex=None):
    s = x.shape[0]
    t = _tiles(s)
    ex = _Exchanges(w) if ex is None else ex
    f = _forward(x, positions, w, small, ex)
    pw, rc = f["pw"], f["rc"]
    cos_r, sin_r, cos_m, sin_m = f["tabs"]
    dx2, loss, g_fw = _loss_call(f["x2"], tgt, small["final_norm_w"], t["ts"])
    du, dx1, g_cw, g_cb, g_fnw, g_wd = _ffn_bwd_call(dx2, f["u"], f["uc"], w["conv_w"], pw["wdown"], pw["wup"],
                                                     f["x1"], f["r2"], small["ffn_norm_w"], t["t2"])
    g_wup = _dw_norm_call(f["x1"], f["r2"], small["ffn_norm_w"], du, t["ts"], F2 // 4, "dw_up")
    started = ex.mlp_grads(dict(w_up=g_wup, w_down=g_wd))
    dy_ret, do, g_wout = _out_bwd_call(dx1, f["y_ret"], f["y_mla"], pw["wout"], t["ts"], started)
    started = ex.behind_out_bwd(g_wout)
    drq, dg, do_ret, g_gnw = _ret_bwd_q_call(f["q"], f["k"], f["v"], f["o_ret"], f["g"], dy_ret, small["ret_gn_w"], rc, cos_r, sin_r, t["tr"])
    drk, drv = _ret_bwd_kv_call(f["q"], f["k"], f["v"], do_ret, rc, cos_r, sin_r, t["tr"])
    dmk, dmv, dmq = _flash_bwd_call(f["mqb"], f["mk"], f["mv"], do, t["tb"], started)
    ex.behind_attention(dmk)
    dcq, dckv, dkpe, g_wq, g_wk, g_wv, g_qnw, g_kvnw = _mla_post_call(
        dmq, dmk, dmv, f["cq"], f["ckv"], small["mla_q_norm_w"], small["mla_kv_norm_w"], pw["wq"], pw["wk"], pw["wv"], cos_m, sin_m, t["ts"])
    gx, g_win_ext, g_anw = _in_bwd_call([drq, drk, drv, dg, dcq, dckv, dkpe], x, f["r1"], small["attn_norm_w"], dx1, pw["win"], t["ts"])
    lo = IN_W - ROPE
    g_win = jnp.concatenate([g_win_ext[:, :lo], g_win_ext[:, lo + KPE_LO:lo + KPE_LO + ROPE]], -1)
    g_wuq = g_wq.transpose(1, 0, 2)[:, :, :HEAD + ROPE].reshape(Q_RANK, N_HEADS * (HEAD + ROPE))
    g_wukv = jnp.concatenate([g_wk[:, :, :HEAD], g_wv[:, :, :HEAD]], -1).transpose(1, 0, 2).reshape(KV_RANK, 2 * MLA_W)
    gw = dict(w_in=g_win, w_uq=g_wuq, w_ukv=g_wukv, w_out=g_wout, w_up=g_wup,
              conv_w=g_cw, w_down=g_wd)
    gs = dict(attn_norm_w=g_anw, ret_gn_w=g_gnw, mla_q_norm_w=g_qnw, mla_kv_norm_w=g_kvnw, ffn_norm_w=g_fnw,
              conv_b=g_cb, final_norm_w=g_fw)
    return loss, gx, gw, gs


MESH_ID = pl.DeviceIdType.MESH
ANY = pl.BlockSpec(memory_space=pl.ANY)
VMEM_SPEC = pl.BlockSpec(memory_space=pltpu.VMEM)
N_DEV = 8
GROUP_A = (("w_in", (D_MODEL, IN_W // 4), 1), ("w_uq", (Q_RANK, 192), 1), ("w_ukv", (KV_RANK, 256), 1),
           ("w_out", (D_MODEL // 4, D_MODEL), 0))
GROUP_B = (("w_up", (D_MODEL, F2 // 4), 1), ("w_down", (D_FF // 4, D_MODEL), 0))
HBM_SPEC = pl.BlockSpec(memory_space=pltpu.HBM)
SEM_SPEC = pl.BlockSpec(memory_space=pltpu.SEMAPHORE)


def _mesh_pos():
    return lax.axis_index("x"), lax.axis_index("y"), lax.axis_index("c")


def _other_chips(x, y):
    return [(1 - x, y), (x, 1 - y), (1 - x, 1 - y)]


def _remote(src, dst, send_sems, recv_sems, k, dev):
    return pltpu.make_async_remote_copy(src_ref=src, dst_ref=dst, send_sem=send_sems.at[k], recv_sem=recv_sems.at[k],
                                        device_id=dev, device_id_type=MESH_ID)


def _gather_list_call(parts, tag):
    n = len(parts)

    def body(*refs):
        srcs, outs, (send_sems, recv_sems) = refs[:n], refs[n:2 * n], refs[2 * n:]
        x, y, c = _mesh_pos()
        sm = 2 * x + y
        chips = _other_chips(x, y)
        sib = (x, y, 1 - c)
        rc = lambda k, src, dst, dev: _remote(src, dst, send_sems, recv_sems, k, dev)
        first = [rc(7 * i + j, srcs[i].at[c], outs[i].at[sm, c], (cx, cy, c)) for i in range(n) for j, (cx, cy) in enumerate(chips)]
        own = [rc(7 * i + 6, srcs[i], outs[i].at[sm], sib) for i in range(n)]
        for cp in first + own:
            cp.start()
        passed = []
        for j, (cx, cy) in enumerate(chips):
            for i in range(n):
                land = outs[i].at[2 * cx + cy, c]
                rc(7 * i + j, srcs[i].at[c], land, (cx, cy, c)).wait_recv()
                cp = rc(7 * i + 3 + j, land, land, sib)
                cp.start()
                passed.append(cp)
        for j, (cx, cy) in enumerate(chips):
            for i in range(n):
                rc(7 * i + 3 + j, srcs[i].at[c], outs[i].at[2 * cx + cy, 1 - c], sib).wait_recv()
        for cp in own:
            cp.wait_recv()
        for cp in first + passed + own:
            cp.wait_send()

    return pl.pallas_call(
        body, name="weights_all_gather_" + tag,
        in_specs=[ANY] * n, out_specs=[ANY] * n,
        out_shape=[jax.ShapeDtypeStruct((4,) + p.shape, p.dtype) for p in parts],
        scratch_shapes=[pltpu.SemaphoreType.DMA((7 * n,)), pltpu.SemaphoreType.DMA((7 * n,))],
    )(*parts)


def _direct_gather_copies(srcs, lands, send_sems, recv_sems):
    x, y, c = _mesh_pos()
    sm = 2 * x + y
    sends, recvs = [], []
    for i, (src, land) in enumerate(zip(srcs, lands)):
        for j, (cx, cy) in enumerate(_other_chips(x, y)):
            for t in range(2):
                sends.append(_remote(src.at[c], land.at[sm, c], send_sems, recv_sems, 13 * i + 4 * j + 2 * c + t, (cx, cy, t)))
                recvs.append(_remote(src.at[t], land.at[2 * cx + cy, t], send_sems, recv_sems, 13 * i + 4 * j + 2 * t + c, (cx, cy, t)))
        sends.append(_remote(src, land.at[sm], send_sems, recv_sems, 13 * i + 12, (x, y, 1 - c)))
        recvs.append(_remote(src, land.at[sm], send_sems, recv_sems, 13 * i + 12, (x, y, 1 - c)))
    return sends, recvs


def _sibling_copies(srcs, lands, send_sems, recv_sems):
    x, y, c = _mesh_pos()
    cps = [_remote(src.at[s, 1 - c], land.at[s], send_sems, recv_sems, 4 * i + s, (x, y, 1 - c))
           for i, (src, land) in enumerate(zip(srcs, lands)) for s in range(4)]
    return cps, cps


def _chips_copies(srcs, lands, send_sems, recv_sems):
    x, y, c = _mesh_pos()
    cps = [_remote(src.at[2 * cx + cy], land.at[j], send_sems, recv_sems, 3 * i + j, (cx, cy, c))
           for i, (src, land) in enumerate(zip(srcs, lands)) for j, (cx, cy) in enumerate(_other_chips(x, y))]
    return cps, cps


def _share_copies(srcs, lands, send_sems, recv_sems):
    x, y, c = _mesh_pos()
    cps = [_remote(src, land, send_sems, recv_sems, i, (x, y, 1 - c)) for i, (src, land) in enumerate(zip(srcs, lands))]
    return cps, cps


def _exchange_call(name, copies, srcs, land_shapes, n_sems):
    n = len(srcs)

    def body(*refs):
        sends, recvs = copies(refs[:n], refs[n:2 * n], refs[2 * n], refs[2 * n + 1])
        for cp in sends:
            cp.start()
        for cp in sends:
            cp.wait_send()
        for cp in recvs:
            cp.wait_recv()

    return pl.pallas_call(
        body, name=name, in_specs=[ANY] * n, out_specs=[ANY] * n, out_shape=list(land_shapes),
        scratch_shapes=[pltpu.SemaphoreType.DMA((n_sems,)), pltpu.SemaphoreType.DMA((n_sems,))],
    )(*srcs)


def _exchange_start_call(name, copies, srcs, land_shapes, n_sems, order=None):
    n = len(srcs)
    extra = [] if order is None else [order]
    k = 2 * n + len(extra)

    def body(*refs):
        sends, _ = copies(refs[:n], refs[n:2 * n], refs[k], refs[k + 1])
        for cp in sends:
            cp.start()
        refs[-1][...] = jnp.zeros_like(refs[-1])

    hbm = lambda a: pltpu.with_memory_space_constraint(a, pltpu.HBM)
    lands = [hbm(lax.empty(sd.shape, sd.dtype)) for sd in land_shapes]
    sem = pltpu.SemaphoreType.DMA((n_sems,))
    out = pl.pallas_call(
        body, name=name,
        out_shape=(sem, sem, *[pltpu.HBM(a.shape, a.dtype) for a in list(srcs) + lands], jax.ShapeDtypeStruct((8, LANES), F32)),
        in_specs=[HBM_SPEC] * (2 * n) + [ANY] * len(extra), out_specs=(SEM_SPEC, SEM_SPEC, *[HBM_SPEC] * (2 * n), VMEM_SPEC),
        input_output_aliases={i: 2 + i for i in range(2 * n)},
        compiler_params=pltpu.CompilerParams(has_side_effects=pltpu.SideEffectType.DATAFLOW_SIDE_EFFECTING),
    )(*[hbm(a) for a in srcs], *lands, *extra)
    return out[0], out[1], out[2:2 + n], out[2 + n:2 + 2 * n], out[-1]


def _exchange_wait_call(name, copies, started, after):
    send_sems, recv_sems, srcs, lands, _ = started
    n = len(srcs)

    def body(*refs):
        sends, recvs = copies(refs[:n], refs[n:2 * n], refs[2 * n], refs[2 * n + 1])
        for cp in sends:
            cp.wait_send()
        for cp in recvs:
            cp.wait_recv()

    out = pl.pallas_call(
        body, name=name,
        out_shape=tuple(pltpu.HBM(a.shape, a.dtype) for a in list(srcs) + list(lands)),
        in_specs=[HBM_SPEC] * (2 * n) + [SEM_SPEC, SEM_SPEC, ANY], out_specs=tuple([HBM_SPEC] * (2 * n)),
        input_output_aliases={i: i for i in range(2 * n)},
        compiler_params=pltpu.CompilerParams(has_side_effects=pltpu.SideEffectType.DATAFLOW_SIDE_EFFECTING),
    )(*srcs, *lands, send_sems, recv_sems, after)
    return out[:n], out[n:]


def _rows_tile(rows, width, itemsize=4):
    limit = max(16, (3 << 20) // (width * itemsize))
    if rows <= limit:
        return rows
    return max(t for t in range(16, limit + 1, 16) if rows % t == 0)


def _sum_sibling_call(g, buf, c, name):
    _, _, rh, w = g.shape
    tile = _rows_tile(rh, w)

    def body(c_ref, g_ref, b_ref, p_ref, pb_ref):
        p = g_ref[...] + b_ref[...]
        p_ref[...] = p
        pb_ref[...] = p.astype(BF16)

    blk = pl.BlockSpec((None, tile, w), lambda s, i, c_ref: (s, i, 0))
    return pl.pallas_call(
        body, name=name,
        grid_spec=pltpu.PrefetchScalarGridSpec(
            num_scalar_prefetch=1, grid=(4, rh // tile),
            in_specs=[pl.BlockSpec((None, None, tile, w), lambda s, i, c_ref: (s, c_ref[0], i, 0)), blk],
            out_specs=[blk, blk]),
        out_shape=[jax.ShapeDtypeStruct((4, rh, w), F32), jax.ShapeDtypeStruct((4, rh, w), BF16)],
        compiler_params=_cp("parallel", "parallel"),
    )(c, g, buf)


def _sum_chips_call(p, buf, sm, name):
    _, rh, w = p.shape
    tile = _rows_tile(rh, w)

    def body(sm_ref, p_ref, b_ref, f_ref):
        f_ref[...] = ((p_ref[...] + b_ref[0].astype(F32)) + b_ref[1].astype(F32)) + b_ref[2].astype(F32)

    return pl.pallas_call(
        body, name=name,
        grid_spec=pltpu.PrefetchScalarGridSpec(
            num_scalar_prefetch=1, grid=(rh // tile,),
            in_specs=[pl.BlockSpec((None, tile, w), lambda i, sm_ref: (sm_ref[0], i, 0)),
                      pl.BlockSpec((3, tile, w), lambda i, sm_ref: (0, i, 0))],
            out_specs=pl.BlockSpec((tile, w), lambda i, sm_ref: (i, 0))),
        out_shape=jax.ShapeDtypeStruct((rh, w), F32),
        compiler_params=_cp("parallel"),
    )(sm, p, buf)


def _adamw_halves_call(w, g_mine, g_sib, c, m, v, name):
    r, wd = w.shape
    rh = r // 2
    tile = _rows_tile(rh, wd)
    nt = rh // tile

    def body(c_ref, w_ref, gm_ref, gs_ref, m_ref, v_ref, g_ref, d_ref, nm_ref, nv_ref):
        gv = jnp.where(pl.program_id(0) == c_ref[0], gm_ref[...], gs_ref[...])
        g_ref[...] = gv
        nm = ADAM_B1 * m_ref[...] + (1.0 - ADAM_B1) * gv
        nv = ADAM_B2 * v_ref[...] + (1.0 - ADAM_B2) * jnp.square(gv)
        m_hat = nm / (1.0 - ADAM_B1 ** ADAM_STEP)
        v_hat = nv / (1.0 - ADAM_B2 ** ADAM_STEP)
        d_ref[...] = -ADAM_LR * (m_hat / (jnp.sqrt(v_hat) + ADAM_EPS) + ADAM_WD * w_ref[...])
        nm_ref[...] = nm
        nv_ref[...] = nv

    whole = pl.BlockSpec((tile, wd), lambda h, i, c_ref: (h * nt + i, 0))
    half = pl.BlockSpec((tile, wd), lambda h, i, c_ref: (i, 0))
    sd = jax.ShapeDtypeStruct((r, wd), F32)
    return pl.pallas_call(
        body, name=name,
        grid_spec=pltpu.PrefetchScalarGridSpec(
            num_scalar_prefetch=1, grid=(2, nt),
            in_specs=[whole, half, half, whole, whole], out_specs=[whole] * 4),
        out_shape=[sd, sd, sd, sd],
        compiler_params=_cp("parallel", "parallel"),
    )(c, w, g_mine, g_sib, m, v)


def _exchange8_call(vec, reduce, name):
    rows = vec.shape[0]

    def body(v_ref, out_ref, *rest):
        slots, send_sems, recv_sems = (rest if reduce else (out_ref,) + rest)
        x, y, c = _mesh_pos()
        me = 4 * x + 2 * y + c
        slots[me] = v_ref[...]

        def rcopy(k, to_me):
            bx, by, bc = (k >> 2) & 1, (k >> 1) & 1, k & 1
            px, py, pc = (1 - x if bx else x), (1 - y if by else y), (1 - c if bc else c)
            slot = 4 * px + 2 * py + pc if to_me else me
            return pltpu.make_async_remote_copy(src_ref=v_ref, dst_ref=slots.at[slot], send_sem=send_sems.at[k - 1],
                                                recv_sem=recv_sems.at[k - 1], device_id=(px, py, pc), device_id_type=MESH_ID)

        for k in range(1, N_DEV):
            rcopy(k, False).start()
        for k in range(1, N_DEV):
            rcopy(k, True).wait_recv()
        for k in range(1, N_DEV):
            rcopy(k, False).wait_send()
        if reduce:
            tot = slots[0]
            for d in range(1, N_DEV):
                tot = tot + slots[d]
            out_ref[...] = tot

    stack = jax.ShapeDtypeStruct((N_DEV, rows, LANES), F32)
    return pl.pallas_call(
        body, name=name,
        in_specs=[VMEM_SPEC], out_specs=VMEM_SPEC,
        out_shape=jax.ShapeDtypeStruct((rows, LANES), F32) if reduce else stack,
        scratch_shapes=([pltpu.VMEM((N_DEV, rows, LANES), F32)] if reduce else [])
        + [pltpu.SemaphoreType.DMA((N_DEV - 1,)), pltpu.SemaphoreType.DMA((N_DEV - 1,))],
    )(vec)


def _adamw_call(w, g, m, v, name):
    r, c = w.shape
    rb = r if r <= 256 else (256 if r % 256 == 0 else 352)
    assert r % rb == 0

    def body(w_ref, g_ref, m_ref, v_ref, d_ref, nm_ref, nv_ref):
        gv = g_ref[...]
        nm = ADAM_B1 * m_ref[...] + (1.0 - ADAM_B1) * gv
        nv = ADAM_B2 * v_ref[...] + (1.0 - ADAM_B2) * jnp.square(gv)
        m_hat = nm / (1.0 - ADAM_B1 ** ADAM_STEP)
        v_hat = nv / (1.0 - ADAM_B2 ** ADAM_STEP)
        d_ref[...] = -ADAM_LR * (m_hat / (jnp.sqrt(v_hat) + ADAM_EPS) + ADAM_WD * w_ref[...])
        nm_ref[...] = nm
        nv_ref[...] = nv

    spec = pl.BlockSpec((rb, c), lambda i: (i, 0))
    sd = jax.ShapeDtypeStruct((r, c), F32)
    return pl.pallas_call(
        body, name=name, grid=(r // rb,),
        in_specs=[spec] * 4, out_specs=[spec] * 3, out_shape=[sd, sd, sd],
        compiler_params=_cp("parallel"),
    )(w, g, m, v)


SMALL = (("attn_norm_w", D_MODEL), ("ret_gn_w", RET_W), ("mla_q_norm_w", Q_RANK), ("mla_kv_norm_w", KV_RANK),
         ("ffn_norm_w", D_MODEL), ("conv_b", F2), ("final_norm_w", D_MODEL))
WEIGHT_ORDER = ("attn_norm_w", "w_in", "ret_gn_w", "mla_q_norm_w", "w_uq", "mla_kv_norm_w", "w_ukv", "w_out",
                "ffn_norm_w", "w_up", "conv_w", "conv_b", "w_down", "final_norm_w")


def _pad_rows(flat, rows):
    return jnp.concatenate([flat, jnp.zeros((rows * LANES - flat.shape[0],), flat.dtype)]).reshape(rows, LANES)


def kernel(x, positions, attn_norm_w, w_in, ret_gn_w, mla_q_norm_w, w_uq, mla_kv_norm_w, w_ukv, w_out, ffn_norm_w, w_up, conv_w, conv_b, w_down, final_norm_w, loss_target, m_attn_norm_w, m_w_in, m_ret_gn_w, m_mla_q_norm_w, m_w_uq, m_mla_kv_norm_w, m_w_ukv, m_w_out, m_ffn_norm_w, m_w_up, m_conv_w, m_conv_b, m_w_down, m_final_norm_w, v_attn_norm_w, v_w_in, v_ret_gn_w, v_mla_q_norm_w, v_w_uq, v_mla_kv_norm_w, v_w_ukv, v_w_out, v_ffn_norm_w, v_w_up, v_conv_w, v_conv_b, v_w_down, v_final_norm_w):
    args = dict(locals())
    cx, cy, cc = _mesh_pos()
    sm = 2 * cx + cy

    c_arr, sm_arr = cc.reshape(1).astype(jnp.int32), sm.reshape(1).astype(jnp.int32)
    sds = jax.ShapeDtypeStruct

    def my_shards(group):
        return [args[n][0].astype(BF16).reshape(2, r // 2, c) for n, (r, c), _ in group]

    def full_weights(gathered, group):
        full = {}
        for (n, (r, c), axis), got in zip(group, gathered):
            piece = got.reshape(4, r, c)
            full[n] = piece if n == "w_up" else (piece.transpose(1, 0, 2).reshape(r, 4 * c) if axis == 1 else piece.reshape(4 * r, c))
        return full

    def by_owner(gw, group):
        out = []
        for n, (r, c), axis in group:
            g = gw[n]
            if axis == 1 and g.ndim == 2:
                g = g.reshape(r, 4, c).transpose(1, 0, 2)
            out.append(g.reshape(4, 2, r // 2, c))
        return out

    def sibling_shapes(gs):
        return [sds((4,) + g.shape[2:], F32) for g in gs]

    def chip_sums(gs, bufs, group):
        res = [_sum_sibling_call(g, b, c_arr, "grads_sum_sibling_" + n) for g, b, (n, _, _) in zip(gs, bufs, group)]
        return [p for p, _ in res], [pb for _, pb in res]

    def chips_shapes(pbs):
        return [sds((3,) + pb.shape[1:], BF16) for pb in pbs]

    def totals(ps, lands, group, tag):
        fins = [_sum_chips_call(p, l, sm_arr, "grads_sum_chips_" + n) for p, l, (n, _, _) in zip(ps, lands, group)]
        sibs = _exchange_call("grads_rs_share_" + tag, _share_copies, fins, [sds(f.shape, F32) for f in fins], len(fins))
        return {n: (f, s) for (n, _, _), f, s in zip(group, fins, sibs)}

    class StepExchanges(_Exchanges):
        def __init__(self, order):
            shards = my_shards(GROUP_B)
            self.gather = _exchange_start_call("weights_gather_start_b", _direct_gather_copies, shards,
                                               [sds((4,) + s.shape, BF16) for s in shards], 13 * len(shards), order)
            self.red = None

        def token(self):
            return self.gather[4][0:1, 0:1]

        def mlp_weights(self, after):
            return full_weights(_exchange_wait_call("weights_gather_wait_b", _direct_gather_copies, self.gather, after)[1], GROUP_B)

        def mlp_grads(self, gw):
            gs = by_owner(gw, GROUP_B)
            self.step1 = _exchange_start_call("grads_rs_sibling_start_b", _sibling_copies, gs, sibling_shapes(gs), 4 * len(gs))
            return self.step1[4]

        def behind_out_bwd(self, after):
            gs, bufs = _exchange_wait_call("grads_rs_sibling_wait_b", _sibling_copies, self.step1, after)
            self.ps, pbs = chip_sums(gs, bufs, GROUP_B)
            self.step2 = _exchange_start_call("grads_rs_chips_start_b", _chips_copies, pbs, chips_shapes(pbs), 3 * len(pbs))
            return self.step2[4]

        def behind_attention(self, after):
            _, lands = _exchange_wait_call("grads_rs_chips_wait_b", _chips_copies, self.step2, after)
            self.red = totals(self.ps, lands, GROUP_B, "b")

    gathered = _gather_list_call(my_shards(GROUP_A) + [conv_w[0].reshape(2, 1, 3 * F2 // 8)], "a")
    full = full_weights(gathered[:-1], GROUP_A)
    ex = StepExchanges(gathered[-1])
    full["conv_w"] = gathered[-1].reshape(4, 3, F2 // 4).transpose(1, 0, 2).reshape(3, F2)
    small = {n: args[n].reshape(1, d) for n, d in SMALL}
    small["attn_norm_w"] = small["attn_norm_w"] + ex.token()

    loss, gx, gw, gs = _local_step(x[0], positions[0], loss_target[0], full, small, ex)

    ga = by_owner(gw, GROUP_A)
    bufs = _exchange_call("grads_rs_sibling_a", _sibling_copies, ga, sibling_shapes(ga), 4 * len(ga))
    ps, pbs = chip_sums(ga, bufs, GROUP_A)
    lands = _exchange_call("grads_rs_chips_a", _chips_copies, pbs, chips_shapes(pbs), 3 * len(pbs))
    halves = {**ex.red, **totals(ps, lands, GROUP_A, "a")}

    vec = jnp.concatenate([gs[n].reshape(-1) for n, _ in SMALL] + [gw["conv_w"].reshape(-1), loss.reshape(-1)])
    tot = _exchange8_call(_pad_rows(vec, 216), True, "small_all_reduce").reshape(-1)
    red, off = {}, 0
    for n, d in SMALL:
        red[n] = tot[off:off + d].reshape(1, d)
        off += d
    red["conv_w"] = lax.dynamic_slice(tot[off:off + 3 * F2].reshape(3, F2), (0, sm * (F2 // 4)), (3, F2 // 4))
    loss_tot = tot[off + 3 * F2]

    grads, deltas, new_m, new_v = [], [], [], []
    for n in WEIGHT_ORDER:
        shape = args[n].shape
        two_d = (1, shape[0]) if len(shape) == 1 else shape[-2:]
        wmv = [args[k + n].reshape(two_d) for k in ("", "m_", "v_")]
        if n in halves:
            g, d, nm, nv = _adamw_halves_call(wmv[0], *halves[n], c_arr, wmv[1], wmv[2], "adamw_" + n)
        else:
            g = red[n].reshape(two_d)
            d, nm, nv = _adamw_call(wmv[0], g, wmv[1], wmv[2], "adamw_" + n)
        grads.append(g.reshape(shape))
        deltas.append(d.reshape(shape))
        new_m.append(nm.reshape(shape))
        new_v.append(nv.reshape(shape))
    return (loss_tot, gx[None], *grads, *deltas, *new_m, *new_v)
```

```python
import functools
import math

import numpy as np
import jax
import jax.numpy as jnp
from jax import lax
from jax.experimental import pallas as pl
from jax.experimental.pallas import tpu as pltpu

F32 = jnp.float32
BF16 = jnp.bfloat16

D_MODEL = 1024
N_HEADS = 8
HEAD = 64
RET_W = N_HEADS * HEAD
MLA_W = N_HEADS * HEAD
ROPE = 32
Q_RANK = 256
KV_RANK = 128
D_FF = 2816
F2 = 2 * D_FF
IN_W = 4 * RET_W + Q_RANK + KV_RANK + ROPE
IN_EXT = 4 * RET_W + Q_RANK + KV_RANK + 128
KPE_LO = 64
ROPE_BASE = 10000.0
EPS = 1e-6
RET_CHUNK = 256
SM_SCALE = (HEAD + ROPE) ** -0.5
LOG2E = math.log2(math.e)
LN2 = math.log(2.0)
NEG = -1e30
LANES = 128
VMEM_LIMIT = 56 * 1024 * 1024

ADAM_LR = 0.001
ADAM_B1 = 0.9
ADAM_B2 = 0.999
ADAM_EPS = 1e-08
ADAM_WD = 0.01
ADAM_STEP = 10


VMEM_LIMIT_MLP_BWD = 60 * 1024 * 1024


def _cp(*sem, vmem=VMEM_LIMIT):
    return pltpu.CompilerParams(dimension_semantics=sem, vmem_limit_bytes=vmem)


def _full(shape):
    n = len(shape)
    return pl.BlockSpec(tuple(shape), lambda *_: (0,) * n)


def _row(ts, c):
    return pl.BlockSpec((ts, c), lambda i: (i, 0))


def _hrow(h, ts, c):
    return pl.BlockSpec((h, ts, c), lambda i: (0, i, 0))


def _dot(a, b):
    return jnp.dot(a, b, preferred_element_type=F32)


def _dot_nt(a, b):
    return lax.dot_general(a, b, (((1,), (1,)), ((), ())), preferred_element_type=F32)


def _dot_tn(a, b):
    return lax.dot_general(a, b, (((0,), (0,)), ((), ())), preferred_element_type=F32)


def _dot_hi(a, b):
    hi = a.astype(BF16)
    lo = (a - hi.astype(F32)).astype(BF16)
    bb = b.astype(BF16)
    return _dot(hi, bb) + _dot(lo, bb)


def _rot_half(x, half):
    w = x.shape[-1]
    lane = lax.broadcasted_iota(jnp.int32, x.shape, x.ndim - 1)
    first = (lane % (2 * half)) < half
    return jnp.where(first, -pltpu.roll(x, w - half, x.ndim - 1), pltpu.roll(x, half, x.ndim - 1))


def _rope(x, cos, sin, half):
    return x * cos + _rot_half(x, half) * sin


def _unrope(dy, cos, sin, half):
    return dy * cos - _rot_half(dy, half) * sin


def _sigmoid(g):
    return 0.5 * jnp.tanh(0.5 * g) + 0.5


def _silu(g):
    return g * _sigmoid(g)


def _rstd(x):
    return lax.rsqrt(jnp.mean(x * x, axis=-1, keepdims=True) + EPS)


def _rope_tables(positions):
    pos = positions.astype(F32)[:, None]
    s = pos.shape[0]
    inv = ROPE_BASE ** (-jnp.arange(0, HEAD, 2, dtype=F32) / HEAD)
    ang = pos * inv
    c, sn = jnp.cos(ang), jnp.sin(ang)
    cos_r = jnp.tile(jnp.concatenate([c, c], -1), (1, 2))
    sin_r = jnp.tile(jnp.concatenate([sn, sn], -1), (1, 2))
    inv = ROPE_BASE ** (-jnp.arange(0, ROPE, 2, dtype=F32) / ROPE)
    ang = pos * inv
    c, sn = jnp.cos(ang), jnp.sin(ang)
    one, zero = jnp.ones((s, KPE_LO), F32), jnp.zeros((s, KPE_LO), F32)
    cos_m = jnp.concatenate([one, c, c, one[:, :LANES - KPE_LO - ROPE]], -1)
    sin_m = jnp.concatenate([zero, sn, sn, zero[:, :LANES - KPE_LO - ROPE]], -1)
    return cos_r, sin_r, cos_m, sin_m


def _ret_consts():
    c = RET_CHUNK
    lg = np.log1p(-np.power(2.0, -5.0 - np.arange(N_HEADS, dtype=np.float64)))
    idx = np.arange(c, dtype=np.float64)
    diff = idx[:, None] - idx[None, :]
    lane_head = np.arange(LANES) // HEAD
    dmask = np.zeros((4, 2, c, c))
    zeta = np.zeros((4, c, LANES))
    xi = np.zeros((4, c, LANES))
    cd = np.zeros((4, LANES, LANES))
    bd = (lane_head[:, None] == lane_head[None, :]).astype(np.float64)
    for j in range(4):
        for hh in range(2):
            dmask[j, hh] = np.where(diff >= 0, np.exp(lg[2 * j + hh] * np.maximum(diff, 0.0)), 0.0)
        lgl = lg[2 * j + lane_head]
        zeta[j] = np.exp(lgl[None, :] * (c - 1.0 - idx[:, None]))
        xi[j] = np.exp(lgl[None, :] * (idx[:, None] + 1.0))
        cd[j] = np.exp(lgl * c)[:, None] * bd
    f = lambda a: jnp.asarray(a, F32)
    side = lambda d: np.concatenate([d[:, 0], d[:, 1]], axis=-1)
    return dict(dmask=f(side(dmask)), dmask_t=f(side(np.swapaxes(dmask, 2, 3))), zeta=f(zeta), xi=f(xi), cd=f(cd), bd=f(bd))


def _f1_call(x, anw, win, cos_r, sin_r, cos_m, sin_m, ts):
    s = x.shape[0]

    def body(x_ref, anw_ref, w_ref, cr_ref, sr_ref, cm_ref, sm_ref,
             q_ref, k_ref, v_ref, g_ref, cq_ref, ckv_ref, kpe_ref, r_ref):
        xv = x_ref[...]
        r = _rstd(xv)
        r_ref[...] = r
        h = (xv * r * anw_ref[...]).astype(BF16)
        cr, sr = cr_ref[...], sr_ref[...]
        qk = _dot(h, w_ref[:, 0:2 * RET_W])
        for j in range(4):
            sl = slice(j * LANES, (j + 1) * LANES)
            q_ref[:, sl] = _rope(qk[:, sl], cr, sr, HEAD // 2).astype(BF16)
            kk = qk[:, RET_W + j * LANES:RET_W + (j + 1) * LANES]
            k_ref[:, sl] = (_rope(kk, cr, sr, HEAD // 2) * (HEAD ** -0.5)).astype(BF16)
        v_ref[...] = _dot(h, w_ref[:, 2 * RET_W:3 * RET_W]).astype(BF16)
        g_ref[...] = _dot(h, w_ref[:, 3 * RET_W:4 * RET_W])
        o = 4 * RET_W
        cq_ref[...] = _dot(h, w_ref[:, o:o + Q_RANK])
        ckv_ref[...] = _dot(h, w_ref[:, o + Q_RANK:o + Q_RANK + KV_RANK])
        kp = _dot(h, w_ref[:, o + Q_RANK + KV_RANK:IN_EXT])
        kpe_ref[...] = _rope(kp, cm_ref[...], sm_ref[...], ROPE // 2)

    sd = jax.ShapeDtypeStruct
    return pl.pallas_call(
        body, name="f1_in_proj", grid=(s // ts,),
        in_specs=[_row(ts, D_MODEL), _full((1, D_MODEL)), _full((D_MODEL, IN_EXT)),
                  _row(ts, LANES), _row(ts, LANES), _row(ts, LANES), _row(ts, LANES)],
        out_specs=[_row(ts, RET_W), _row(ts, RET_W), _row(ts, RET_W), _row(ts, RET_W),
                   _row(ts, Q_RANK), _row(ts, KV_RANK), _row(ts, LANES), _row(ts, 1)],
        out_shape=[sd((s, RET_W), BF16), sd((s, RET_W), BF16), sd((s, RET_W), BF16), sd((s, RET_W), F32),
                   sd((s, Q_RANK), F32), sd((s, KV_RANK), F32), sd((s, LANES), F32), sd((s, 1), F32)],
        compiler_params=_cp("parallel"),
    )(x, anw, win, cos_r, sin_r, cos_m, sin_m)


def _stack_heads(a):
    lo = lax.broadcasted_iota(jnp.int32, a.shape, 1) < HEAD
    zero = jnp.zeros_like(a)
    return jnp.concatenate([jnp.where(lo, a, zero), jnp.where(lo, zero, a)], axis=0)


def _pair_product(a, b2, decay2, w2):
    return _dot((_dot_nt(a, b2) * decay2).astype(BF16), w2)


RET_SLABS = 4


def _ret_specs(tr, tile_of):
    c, ns = RET_CHUNK, RET_SLABS
    return dict(
        slab=pl.BlockSpec((tr, ns * LANES), lambda j, i: (tile_of(i), j)),
        tab=pl.BlockSpec((tr, LANES), lambda j, i: (tile_of(i), 0)),
        vec=pl.BlockSpec((1, ns * LANES), lambda j, i: (0, j)),
        dmask=pl.BlockSpec((ns, c, 2 * c), lambda j, i: (j, 0, 0)),
        rows=pl.BlockSpec((ns, c, LANES), lambda j, i: (j, 0, 0)),
        state=pl.BlockSpec((ns, LANES, LANES), lambda j, i: (j, 0, 0)),
        bd=pl.BlockSpec((LANES, LANES), lambda j, i: (0, 0)))


def _ret_states(a_ref, b_ref, scale_ref, cd_ref, bd, st_ref, chunks, lanes, reverse):
    nc = len(chunks)
    contrib = [[_dot_tn((a_ref[rows, ln].astype(F32) * scale_ref[sl]).astype(BF16), b_ref[rows, ln]) * bd for rows in chunks]
               for sl, ln in enumerate(lanes)]
    states = []
    for sl in range(len(lanes)):
        st, seen = st_ref[sl], [None] * nc
        for ci in (reversed(range(nc)) if reverse else range(nc)):
            seen[ci] = st.astype(BF16)
            st = st * cd_ref[sl] + contrib[sl][ci]
        st_ref[sl] = st
        states.append(seen)
    return states


def _ret_fwd_call(q, k, v, g, gnw, rc, tr):
    s = q.shape[0]
    c = RET_CHUNK
    nc = tr // c
    ns = RET_SLABS

    def body(q_ref, k_ref, v_ref, g_ref, gnw_ref, dm_ref, zeta_ref, xi_ref, cd_ref, bd_ref, o_ref, y_ref, st_ref):
        @pl.when(pl.program_id(1) == 0)
        def _():
            st_ref[...] = jnp.zeros_like(st_ref)

        bd = bd_ref[...]
        chunks = [slice(ci * c, (ci + 1) * c) for ci in range(nc)]
        lanes = [slice(sl * LANES, (sl + 1) * LANES) for sl in range(ns)]
        states = _ret_states(k_ref, v_ref, zeta_ref, cd_ref, bd, st_ref, chunks, lanes, False)
        for ci, rows in enumerate(chunks):
            for sl, ln in enumerate(lanes):
                qc = q_ref[rows, ln]
                o_ref[rows, ln] = (_dot(qc, states[sl][ci]) * xi_ref[sl]
                                   + _pair_product(qc, _stack_heads(k_ref[rows, ln]), dm_ref[sl], _stack_heads(v_ref[rows, ln])))
        avg = bd * (1.0 / HEAD)
        for ln in lanes:
            o = o_ref[:, ln]
            ctr = o - _dot_hi(o, avg)
            var = _dot_hi(ctr * ctr, avg)
            y_ref[:, ln] = (_silu(g_ref[:, ln]) * (ctr * lax.rsqrt(var + EPS) * gnw_ref[:, ln])).astype(BF16)

    specs = _ret_specs(tr, lambda i: i)
    sd = jax.ShapeDtypeStruct
    return pl.pallas_call(
        body, name="ret_fwd", grid=(4 // ns, s // tr),
        in_specs=[specs["slab"]] * 4 + [specs["vec"], specs["dmask"], specs["rows"], specs["rows"], specs["state"], specs["bd"]],
        out_specs=[specs["slab"]] * 2,
        out_shape=[sd((s, RET_W), F32), sd((s, RET_W), BF16)],
        scratch_shapes=[pltpu.VMEM((ns, LANES, LANES), F32)],
        compiler_params=_cp("parallel", "arbitrary"),
    )(q, k, v, g, gnw, rc["dmask"], rc["zeta"], rc["xi"], rc["cd"], rc["bd"])


QK_AUX = HEAD + ROPE
V_AUX = HEAD


def _lane_pair(shape, lo, a, b, rest):
    lane = lax.broadcasted_iota(jnp.int32, shape, len(shape) - 1)
    return jnp.where(lane == lo, a, jnp.where(lane == lo + 1, b, rest))


def _hi_lo(v):
    hi = v.astype(BF16).astype(F32)
    return hi, v - hi


def _mla_pre_call(cq, ckv, kpe, qnw, kvnw, wq, wk, wv, cos_m, sin_m, ts):
    s = cq.shape[0]

    def body(cq_ref, ckv_ref, kpe_ref, qnw_ref, kvnw_ref, wq_ref, wk_ref, wv_ref, cm_ref, sm_ref, q_ref, k_ref, v_ref):
        cqv, ckvv = cq_ref[...], ckv_ref[...]
        cqn = (cqv * _rstd(cqv) * qnw_ref[...]).astype(BF16)
        ckvn = (ckvv * _rstd(ckvv) * kvnw_ref[...]).astype(BF16)
        cm, sm = cm_ref[...], sm_ref[...]
        kp = _lane_pair((ts, LANES), QK_AUX, -1.0, -1.0, kpe_ref[...])
        for h in range(N_HEADS):
            qh = _rope(_dot(cqn, wq_ref[h]), cm, sm, ROPE // 2)
            q_ref[h] = (qh * (SM_SCALE * LOG2E)).astype(BF16)
            k_ref[h] = (_dot(ckvn, wk_ref[h]) + kp).astype(BF16)
            v_ref[h] = _lane_pair((ts, LANES), V_AUX, 1.0, 1.0, _dot(ckvn, wv_ref[h])).astype(BF16)

    sd = jax.ShapeDtypeStruct
    hm = sd((N_HEADS, s, LANES), BF16)
    return pl.pallas_call(
        body, name="mla_pre", grid=(s // ts,),
        in_specs=[_row(ts, Q_RANK), _row(ts, KV_RANK), _row(ts, LANES), _full((1, Q_RANK)), _full((1, KV_RANK)),
                  _full((N_HEADS, Q_RANK, LANES)), _full((N_HEADS, KV_RANK, LANES)), _full((N_HEADS, KV_RANK, LANES)),
                  _row(ts, LANES), _row(ts, LANES)],
        out_specs=[_hrow(N_HEADS, ts, LANES)] * 3,
        out_shape=[hm, hm, hm],
        compiler_params=_cp("parallel"),
    )(cq, ckv, kpe, qnw, kvnw, wq, wk, wv, cos_m, sin_m)


def _flash_fwd_call(q, k, v, tb):
    s = q.shape[1]
    nb = s // tb
    pairs = [(a, b) for a in range(nb) for b in range(a + 1)]
    qi_of, ki_of = (jnp.asarray(np.array(col, np.int32)) for col in zip(*pairs))

    def body(qi_ref, ki_ref, q_ref, k_ref, v_ref, o_ref, qb_ref, m_ref, acc_ref):
        qi, ki = qi_ref[pl.program_id(0)], ki_ref[pl.program_id(0)]

        @pl.when(ki == 0)
        def _():
            m_ref[...] = jnp.full_like(m_ref, NEG)
            acc_ref[...] = jnp.zeros_like(acc_ref)

        def step(masked):
            if masked:
                keep = lax.broadcasted_iota(jnp.int32, (tb, tb), 1) <= lax.broadcasted_iota(jnp.int32, (tb, tb), 0)
            def finish(h, pe, alpha):
                acc_ref[h] = acc_ref[h] * alpha + _dot(pe, v_ref[h])

            nxt, pending = _dot_nt(q_ref[0], k_ref[0]), None
            for h in range(N_HEADS):
                sc = nxt
                if h + 1 < N_HEADS:
                    nxt = _dot_nt(q_ref[h + 1], k_ref[h + 1])
                if masked:
                    sc = jnp.where(keep, sc, NEG)
                m_prev = m_ref[h]
                m_new = jnp.maximum(m_prev, jnp.max(sc, axis=1, keepdims=True))
                pe = jnp.exp2(sc - jnp.tile(m_new, (1, tb // LANES))).astype(BF16)
                m_ref[h] = m_new
                if pending is not None:
                    finish(*pending)
                pending = (h, pe, jnp.exp2(m_prev - m_new))
            finish(*pending)

        @pl.when(ki < qi)
        def _():
            step(False)

        @pl.when(ki == qi)
        def _():
            step(True)
            lane = lax.broadcasted_iota(jnp.int32, (tb, LANES), 1)
            for p in range(N_HEADS // 2):
                outs = []
                for h in (2 * p, 2 * p + 1):
                    acc = acc_ref[h]
                    l = acc[:, V_AUX:V_AUX + 1]
                    outs.append(acc * (1.0 / l))
                    hi, lo = _hi_lo(m_ref[h][:, 0:1] + jnp.log(l) * LOG2E)
                    qb_ref[h] = _lane_pair((tb, LANES), QK_AUX, hi, lo, q_ref[h].astype(F32)).astype(BF16)
                o_ref[:, p * LANES:(p + 1) * LANES] = jnp.where(lane < HEAD, outs[0], pltpu.roll(outs[1], HEAD, 1)).astype(BF16)

    sd = jax.ShapeDtypeStruct
    qspec = pl.BlockSpec((N_HEADS, tb, LANES), lambda p, qi_ref, ki_ref: (0, qi_ref[p], 0))
    kspec = pl.BlockSpec((N_HEADS, tb, LANES), lambda p, qi_ref, ki_ref: (0, ki_ref[p], 0))
    return pl.pallas_call(
        body, name="mla_flash_fwd",
        grid_spec=pltpu.PrefetchScalarGridSpec(
            num_scalar_prefetch=2, grid=(len(pairs),),
            in_specs=[qspec, kspec, kspec],
            out_specs=[pl.BlockSpec((tb, MLA_W), lambda p, qi_ref, ki_ref: (qi_ref[p], 0)), qspec],
            scratch_shapes=[pltpu.VMEM((N_HEADS, tb, LANES), F32), pltpu.VMEM((N_HEADS, tb, LANES), F32)]),
        out_shape=[sd((s, MLA_W), BF16), sd((N_HEADS, s, LANES), BF16)],
        compiler_params=_cp("arbitrary"),
    )(qi_of, ki_of, q, k, v)


def _out_proj_call(x, yret, ymla, wout, ts):
    s = x.shape[0]

    def body(x_ref, yr_ref, ym_ref, w_ref, x1_ref, r_ref):
        x1 = x_ref[...] + _dot(yr_ref[...], w_ref[0:RET_W, :]) + _dot(ym_ref[...], w_ref[RET_W:, :])
        x1_ref[...] = x1
        r_ref[...] = _rstd(x1)

    sd = jax.ShapeDtypeStruct
    return pl.pallas_call(
        body, name="out_proj", grid=(s // ts,),
        in_specs=[_row(ts, D_MODEL), _row(ts, RET_W), _row(ts, MLA_W), _full((D_MODEL, D_MODEL))],
        out_specs=[_row(ts, D_MODEL), _row(ts, 1)],
        out_shape=[sd((s, D_MODEL), F32), sd((s, 1), F32)],
        compiler_params=_cp("parallel"),
    )(x, yret, ymla, wout)


W_UP_SHARD = F2 // 4


def _ffn_fwd_call(x1, r2, fnw, wup4, cw, cb, wdown, tgt, fw, ts):
    s = x1.shape[0]
    wsh = W_UP_SHARD

    def body(x_ref, r_ref, fnw_ref, wup_ref, cw_ref, cb_ref, wd_ref, t_ref, fw_ref,
             u_ref, uc_ref, dx2_ref, loss_ref, gfw_ref, carry_ref):
        _zero_first(pl.program_id(0) == 0, carry_ref, loss_ref, gfw_ref)
        xv = x_ref[...]
        h = (xv * r_ref[...] * fnw_ref[...]).astype(BF16)
        conv = []
        for j in range(4):
            cols = slice(j * wsh, (j + 1) * wsh)
            ub = _dot(h, wup_ref[j]).astype(BF16)
            u_ref[:, cols] = ub
            u = ub.astype(F32)
            u1, u2 = _shifted(u, carry_ref[:, cols])
            w = cw_ref[:, cols]
            cb16 = (cb_ref[:, cols] + w[0:1, :] * u2 + w[1:2, :] * u1 + w[2:3, :] * u).astype(BF16)
            uc_ref[:, cols] = cb16
            conv.append(cb16.astype(F32))
            carry_ref[:, cols] = u[ts - 8:, :]
        acc = xv
        for j in range(2):
            a = (_silu(conv[j]) * conv[j + 2]).astype(BF16)
            acc = acc + _dot(a, wd_ref[j * wsh:(j + 1) * wsh, :])
        r = _rstd(acc)
        xh = acc * r
        fwv = fw_ref[...]
        e = xh * fwv - t_ref[...]
        loss_ref[...] += (0.5 / D_MODEL) * _colsum(jnp.sum(e * e, axis=1, keepdims=True))
        dy = e * (1.0 / D_MODEL)
        gfw_ref[...] += _colsum(dy * xh)
        dx2_ref[...] = _norm_bwd(dy, xh, r, fwv)

    sd = jax.ShapeDtypeStruct
    once = lambda shape: pl.BlockSpec(shape, lambda i: (0,) * len(shape), pipeline_mode=pl.Buffered(1))
    return pl.pallas_call(
        body, name="ffn_fwd_loss", grid=(s // ts,),
        in_specs=[_row(ts, D_MODEL), _row(ts, 1), once((1, D_MODEL)), once((4, D_MODEL, wsh)),
                  once((3, F2)), once((1, F2)), once((D_FF, D_MODEL)), _row(ts, D_MODEL), once((1, D_MODEL))],
        out_specs=[_row(ts, F2), _row(ts, F2), _row(ts, D_MODEL), _full((1, 1)), _full((1, D_MODEL))],
        out_shape=[sd((s, F2), BF16), sd((s, F2), BF16), sd((s, D_MODEL), F32), sd((1, 1), F32), sd((1, D_MODEL), F32)],
        scratch_shapes=[pltpu.VMEM((8, F2), F32)],
        compiler_params=_cp("arbitrary"),
    )(x1, r2, fnw, wup4, cw, cb, wdown, tgt, fw)


def _shifted(u, hal):
    row = lax.broadcasted_iota(jnp.int32, hal.shape, 0)
    r1, r2 = pltpu.roll(u, 1, 0), pltpu.roll(u, 2, 0)
    top1 = jnp.where(row == 0, hal[7:8, :], r1[0:8, :])
    top2 = jnp.where(row == 0, hal[6:7, :], jnp.where(row == 1, hal[7:8, :], r2[0:8, :]))
    return jnp.concatenate([top1, r1[8:, :]], axis=0), jnp.concatenate([top2, r2[8:, :]], axis=0)


def _prep_weights(w):
    win = w["w_in"]
    pad = lambda n: jnp.zeros((D_MODEL, n), win.dtype)
    win_ext = jnp.concatenate([win[:, :IN_W - ROPE], pad(KPE_LO), win[:, IN_W - ROPE:], pad(LANES - KPE_LO - ROPE)], -1)
    wuq = w["w_uq"].reshape(Q_RANK, N_HEADS, HEAD + ROPE)
    wq = jnp.concatenate([wuq, jnp.zeros((Q_RANK, N_HEADS, LANES - HEAD - ROPE), wuq.dtype)], -1).transpose(1, 0, 2)
    wukv = w["w_ukv"].reshape(KV_RANK, N_HEADS, 2 * HEAD)
    zk = jnp.zeros((KV_RANK, N_HEADS, HEAD), wukv.dtype)
    wk = jnp.concatenate([wukv[:, :, :HEAD], zk], -1).transpose(1, 0, 2)
    wv = jnp.concatenate([wukv[:, :, HEAD:], zk], -1).transpose(1, 0, 2)
    c = lambda a: a.astype(BF16)
    return dict(win=c(win_ext), wq=c(wq), wk=c(wk), wv=c(wv), wout=c(w["w_out"]))


def _prep_mlp_weights(w):
    wup = w["w_up"]
    if wup.ndim == 2:
        wup = wup.reshape(D_MODEL, 4, W_UP_SHARD).transpose(1, 0, 2)
    return dict(wup=wup.astype(BF16), wdown=w["w_down"].astype(BF16))


def _tiles(s):
    return dict(ts=min(s, 512), tr=min(s, 1024), tb=min(s, 512), t2=min(s, 256))


class _Exchanges:
    def __init__(self, w):
        self.w = w

    def mlp_weights(self, after):
        return self.w

    def mlp_grads(self, gw):
        pass

    def behind_out_bwd(self, after):
        pass

    def behind_attention(self, after):
        pass


def _forward(x, positions, tgt, w, small, ex):
    s = x.shape[0]
    t = _tiles(s)
    pw = _prep_weights(w)
    cos_r, sin_r, cos_m, sin_m = _rope_tables(positions)
    rc = _ret_consts()
    q, k, v, g, cq, ckv, kpe, r1 = _f1_call(x, small["attn_norm_w"], pw["win"], cos_r, sin_r, cos_m, sin_m, t["ts"])
    o_ret, y_ret = _ret_fwd_call(q, k, v, g, small["ret_gn_w"], rc, t["tr"])
    mq, mk, mv = _mla_pre_call(cq, ckv, kpe, small["mla_q_norm_w"], small["mla_kv_norm_w"],
                               pw["wq"], pw["wk"], pw["wv"], cos_m, sin_m, t["ts"])
    y_mla, mqb = _flash_fwd_call(mq, mk, mv, t["tb"])
    x1, r2 = _out_proj_call(x, y_ret, y_mla, pw["wout"], t["ts"])
    pw.update(_prep_mlp_weights(ex.mlp_weights(r2)))
    u, uc, dx2, loss, g_fw = _ffn_fwd_call(x1, r2, small["ffn_norm_w"], pw["wup"], w["conv_w"], small["conv_b"], pw["wdown"],
                                           tgt, small["final_norm_w"], t["t2"])
    return dict(pw=pw, tabs=(cos_r, sin_r, cos_m, sin_m), rc=rc, q=q, k=k, v=v, g=g, cq=cq, ckv=ckv, kpe=kpe, r1=r1,
                o_ret=o_ret, y_ret=y_ret, mqb=mqb, mk=mk, mv=mv, y_mla=y_mla, x1=x1, r2=r2, u=u, uc=uc,
                dx2=dx2, loss=loss, g_fw=g_fw)


def _norm_bwd(dh, xh, r, nw):
    dxn = dh * nw
    return r * (dxn - xh * jnp.mean(dxn * xh, axis=-1, keepdims=True))


def _ordered_after(body, order):
    if order is None:
        return body, [], []
    return (lambda order_ref, *refs: body(*refs)), [pl.BlockSpec(memory_space=pl.ANY)], [order]


def _zero_first(first, *refs):
    @pl.when(first)
    def _():
        for ref in refs:
            ref[...] = jnp.zeros_like(ref)


def _colsum(v):
    return jnp.sum(v, axis=0, keepdims=True)


def _dsilu(g, sg):
    return sg * (1.0 + g * (1.0 - sg))


def _ffn_bwd_call(dx2, u, uc, cw, wdown, wup4, x1, r2, fnw, ts):
    s = dx2.shape[0]
    nt = s // ts
    wsh = W_UP_SHARD
    rev = lambda i: nt - 1 - i

    def body(dx2_ref, u_ref, uc_ref, cw_ref, wd_ref, wup_ref, x_ref, r_ref, fnw_ref,
             du_ref, dx1_ref, dcw_ref, dcb_ref, dfnw_ref, dwd_hbm, carry_ref, dwd_ref, sem):
        i = pl.program_id(0)
        _zero_first(i == 0, carry_ref, dwd_ref, dcw_ref, dcb_ref, dfnw_ref)
        dxb = dx2_ref[...].astype(BF16)
        dh = jnp.zeros((ts, D_MODEL), F32)
        for j in range(2):
            gcols = slice(j * wsh, (j + 1) * wsh)
            vcols = slice(D_FF + j * wsh, D_FF + (j + 1) * wsh)
            gate, val = uc_ref[:, gcols].astype(F32), uc_ref[:, vcols].astype(F32)
            da = _dot_nt(dxb, wd_ref[gcols, :])
            sg = _sigmoid(gate)
            sl = gate * sg
            dwd_ref[gcols, :] += _dot_tn((sl * val).astype(BF16), dxb)
            for d, cols, shard in ((da * val * _dsilu(gate, sg), gcols, j), (da * sl, vcols, 2 + j)):
                d1, d2 = _shifted_up(d, carry_ref[:, cols])
                uv = u_ref[:, cols].astype(F32)
                for t, dt in enumerate((d2, d1, d)):
                    dcw_ref[t:t + 1, cols] += _colsum(dt * uv)
                dcb_ref[:, cols] += _colsum(d)
                w = cw_ref[:, cols]
                du = (w[2:3, :] * d + w[1:2, :] * d1 + w[0:1, :] * d2).astype(BF16)
                du_ref[:, cols] = du
                dh = dh + _dot_nt(du, wup_ref[shard])
                carry_ref[:, cols] = d[0:8, :]
        r = r_ref[...]
        xh = x_ref[...] * r
        dfnw_ref[...] += _colsum(dh * xh)
        dx1_ref[...] = dx2_ref[...] + _norm_bwd(dh, xh, r, fnw_ref[...])

        @pl.when(i == nt - 1)
        def _():
            cp = pltpu.make_async_copy(dwd_ref, dwd_hbm, sem)
            cp.start()
            cp.wait()

    sd = jax.ShapeDtypeStruct
    row = lambda c: pl.BlockSpec((ts, c), lambda i: (rev(i), 0))
    once = lambda shape: pl.BlockSpec(shape, lambda i: (0,) * len(shape), pipeline_mode=pl.Buffered(1))
    return pl.pallas_call(
        body, name="ffn_bwd", grid=(nt,),
        in_specs=[row(D_MODEL), row(F2), row(F2), once((3, F2)), once((D_FF, D_MODEL)), once((4, D_MODEL, wsh)),
                  row(D_MODEL), row(1), once((1, D_MODEL))],
        out_specs=[row(F2), row(D_MODEL), _full((3, F2)), _full((1, F2)), _full((1, D_MODEL)), pl.BlockSpec(memory_space=pl.ANY)],
        out_shape=[sd((s, F2), BF16), sd((s, D_MODEL), F32), sd((3, F2), F32), sd((1, F2), F32), sd((1, D_MODEL), F32),
                   sd((D_FF, D_MODEL), F32)],
        scratch_shapes=[pltpu.VMEM((8, F2), F32), pltpu.VMEM((D_FF, D_MODEL), F32), pltpu.SemaphoreType.DMA],
        compiler_params=_cp("arbitrary", vmem=VMEM_LIMIT_MLP_BWD),
    )(dx2, u, uc, cw, wdown, wup4, x1, r2, fnw)


def _shifted_up(d, hal):
    n = d.shape[0]
    row = lax.broadcasted_iota(jnp.int32, hal.shape, 0)
    r1, r2 = pltpu.roll(d, n - 1, 0), pltpu.roll(d, n - 2, 0)
    end1 = jnp.where(row == 7, hal[0:1, :], r1[n - 8:, :])
    end2 = jnp.where(row == 6, hal[0:1, :], jnp.where(row == 7, hal[1:2, :], r2[n - 8:, :]))
    return jnp.concatenate([r1[:n - 8, :], end1], axis=0), jnp.concatenate([r2[:n - 8, :], end2], axis=0)


def _dw_norm_call(x, r, nw, b, ts, tn, name):
    s, n = b.shape
    k = x.shape[1]

    def body(x_ref, r_ref, nw_ref, b_ref, dw_ref):
        _zero_first(pl.program_id(1) == 0, dw_ref)
        h = (x_ref[...] * r_ref[...] * nw_ref[...]).astype(BF16)
        dw_ref[...] += _dot_tn(h, b_ref[...])

    return pl.pallas_call(
        body, name=name, grid=(n // tn, s // ts),
        in_specs=[pl.BlockSpec((ts, k), lambda j, i: (i, 0)), pl.BlockSpec((ts, 1), lambda j, i: (i, 0)),
                  pl.BlockSpec((1, k), lambda j, i: (0, 0)), pl.BlockSpec((ts, tn), lambda j, i: (i, j))],
        out_specs=pl.BlockSpec((None, k, tn), lambda j, i: (j, 0, 0)),
        out_shape=jax.ShapeDtypeStruct((n // tn, k, tn), F32),
        compiler_params=_cp("parallel", "arbitrary"),
    )(x, r, nw, b)


def _out_bwd_call(dx1, yret, ymla, wout, ts, order=None):
    s = dx1.shape[0]

    def body(dx_ref, yr_ref, ym_ref, w_ref, dyr_ref, do_ref, dwo_ref):
        _zero_first(pl.program_id(0) == 0, dwo_ref)
        dxb = dx_ref[...].astype(BF16)
        dmix = _dot_nt(dxb, w_ref[...])
        dyr_ref[...] = dmix[:, :RET_W]
        ym = ym_ref[...]
        lane = lax.broadcasted_iota(jnp.int32, (ts, LANES), 1)
        for p in range(N_HEADS // 2):
            dom = dmix[:, RET_W + p * LANES:RET_W + (p + 1) * LANES]
            prod = dom * ym[:, p * LANES:(p + 1) * LANES].astype(F32)
            for hh in range(2):
                mine = (lane >= HEAD) if hh else (lane < HEAD)
                hi, lo = _hi_lo(jnp.sum(jnp.where(mine, prod, 0.0), axis=1, keepdims=True))
                base = jnp.where(lane < HEAD, pltpu.roll(dom, HEAD, 1) if hh else dom, 0.0)
                do_ref[2 * p + hh] = _lane_pair((ts, LANES), V_AUX, -hi, -lo, base).astype(BF16)
        dwo_ref[0:RET_W, :] += _dot_tn(yr_ref[...], dxb)
        dwo_ref[RET_W:, :] += _dot_tn(ym, dxb)

    sd = jax.ShapeDtypeStruct
    body, first_specs, first = _ordered_after(body, order)
    return pl.pallas_call(
        body, name="out_proj_bwd", grid=(s // ts,),
        in_specs=first_specs + [_row(ts, D_MODEL), _row(ts, RET_W), _row(ts, MLA_W), _full((D_MODEL, D_MODEL))],
        out_specs=[_row(ts, RET_W), _hrow(N_HEADS, ts, LANES), _full((D_MODEL, D_MODEL))],
        out_shape=[sd((s, RET_W), F32), sd((N_HEADS, s, LANES), BF16), sd((D_MODEL, D_MODEL), F32)],
        compiler_params=_cp("arbitrary"),
    )(*first, dx1, yret, ymla, wout)


def _ret_bwd_q_call(q, k, v, o, g, dy, gnw, rc, cos_r, sin_r, tr):
    s = q.shape[0]
    c = RET_CHUNK
    nc = tr // c
    ns = RET_SLABS

    def body(q_ref, k_ref, v_ref, o_ref, g_ref, dy_ref, gnw_ref, dm_ref, zeta_ref, xi_ref, cd_ref, bd_ref, cr_ref, sr_ref,
             dq_ref, dg_ref, do_ref, dgnw_ref, st_ref):
        _zero_first(pl.program_id(1) == 0, st_ref, dgnw_ref)
        bd = bd_ref[...]
        avg = bd * (1.0 / HEAD)
        chunks = [slice(ci * c, (ci + 1) * c) for ci in range(nc)]
        lanes = [slice(sl * LANES, (sl + 1) * LANES) for sl in range(ns)]
        dov = []
        for ln in lanes:
            ov = o_ref[:, ln]
            ctr = ov - _dot_hi(ov, avg)
            rs = lax.rsqrt(_dot_hi(ctr * ctr, avg) + EPS)
            oh = ctr * rs
            gg, dyv, gnw_v = g_ref[:, ln], dy_ref[:, ln], gnw_ref[:, ln]
            sg = _sigmoid(gg)
            sl = gg * sg
            dg_ref[:, ln] = (dyv * oh * gnw_v * _dsilu(gg, sg)).astype(BF16)
            dgnw_ref[:, ln] += _colsum(dyv * sl * oh)
            doh = dyv * sl * gnw_v
            dov.append((rs * (doh - _dot_hi(doh, avg) - oh * _dot_hi(doh * oh, avg))).astype(BF16))
            do_ref[:, ln] = dov[-1]
        states = _ret_states(k_ref, v_ref, zeta_ref, cd_ref, bd, st_ref, chunks, lanes, False)
        for ci, rows in enumerate(chunks):
            for sl, ln in enumerate(lanes):
                doc = dov[sl][rows, :]
                dq = (_dot_nt(doc, states[sl][ci]) * xi_ref[sl]
                      + _pair_product(doc, _stack_heads(v_ref[rows, ln]), dm_ref[sl], _stack_heads(k_ref[rows, ln])))
                dq_ref[rows, ln] = _unrope(dq, cr_ref[rows, :], sr_ref[rows, :], HEAD // 2).astype(BF16)

    specs = _ret_specs(tr, lambda i: i)
    sd = jax.ShapeDtypeStruct
    return pl.pallas_call(
        body, name="ret_bwd_q", grid=(4 // ns, s // tr),
        in_specs=[specs["slab"]] * 6 + [specs["vec"], specs["dmask"], specs["rows"], specs["rows"], specs["state"], specs["bd"],
                                        specs["tab"], specs["tab"]],
        out_specs=[specs["slab"]] * 3 + [specs["vec"]],
        out_shape=[sd((s, RET_W), BF16), sd((s, RET_W), BF16), sd((s, RET_W), BF16), sd((1, RET_W), F32)],
        scratch_shapes=[pltpu.VMEM((ns, LANES, LANES), F32)],
        compiler_params=_cp("parallel", "arbitrary"),
    )(q, k, v, o, g, dy, gnw, rc["dmask"], rc["zeta"], rc["xi"], rc["cd"], rc["bd"], cos_r, sin_r)


def _ret_bwd_kv_call(q, k, v, do, rc, cos_r, sin_r, tr):
    s = q.shape[0]
    c = RET_CHUNK
    nc = tr // c
    nt = s // tr
    ns = RET_SLABS

    def body(q_ref, k_ref, v_ref, do_ref, dm_ref, zeta_ref, xi_ref, cd_ref, bd_ref, cr_ref, sr_ref, dk_ref, dv_ref, gs_ref):
        _zero_first(pl.program_id(1) == 0, gs_ref)
        bd = bd_ref[...]
        chunks = [slice(ci * c, (ci + 1) * c) for ci in range(nc)]
        lanes = [slice(sl * LANES, (sl + 1) * LANES) for sl in range(ns)]
        states = _ret_states(q_ref, do_ref, xi_ref, cd_ref, bd, gs_ref, chunks, lanes, True)
        for ci, rows in enumerate(chunks):
            for sl, ln in enumerate(lanes):
                kc, vc = k_ref[rows, ln], v_ref[rows, ln]
                q2, do2 = _stack_heads(q_ref[rows, ln]), _stack_heads(do_ref[rows, ln])
                gb = states[sl][ci]
                dk = _dot_nt(vc, gb) * zeta_ref[sl] + _pair_product(vc, do2, dm_ref[sl], q2)
                dv = _dot(kc, gb) * zeta_ref[sl] + _pair_product(kc, q2, dm_ref[sl], do2)
                dk_ref[rows, ln] = (_unrope(dk, cr_ref[rows, :], sr_ref[rows, :], HEAD // 2) * (HEAD ** -0.5)).astype(BF16)
                dv_ref[rows, ln] = dv.astype(BF16)

    specs = _ret_specs(tr, lambda i: nt - 1 - i)
    sd = jax.ShapeDtypeStruct
    return pl.pallas_call(
        body, name="ret_bwd_kv", grid=(4 // ns, nt),
        in_specs=[specs["slab"]] * 4 + [specs["dmask"], specs["rows"], specs["rows"], specs["state"], specs["bd"],
                                        specs["tab"], specs["tab"]],
        out_specs=[specs["slab"]] * 2,
        out_shape=[sd((s, RET_W), BF16), sd((s, RET_W), BF16)],
        scratch_shapes=[pltpu.VMEM((ns, LANES, LANES), F32)],
        compiler_params=_cp("parallel", "arbitrary"),
    )(q, k, v, do, rc["dmask_t"], rc["zeta"], rc["xi"], rc["cd"], rc["bd"], cos_r, sin_r)


FLASH_BWD_HEADS = 8


def _flash_bwd_call(qb, k, v, do, tb, order=None):
    s = qb.shape[1]
    nb = s // tb
    hg = FLASH_BWD_HEADS
    pairs = [(a, b) for a in range(nb) for b in range(a, nb)]
    ki_of, qi_of = (jnp.asarray(np.array(col, np.int32)) for col in zip(*pairs))
    extra = [] if order is None else [order]

    def body(ki_ref, qi_ref, *refs):
        q_ref, k_ref, v_ref, do_ref, dk_ref, dv_ref, dq_hbm, dka_ref, dva_ref, dq_ref, sem = refs[len(extra):]
        g, p = pl.program_id(0), pl.program_id(1)
        ki, qi = ki_ref[p], qi_ref[p]
        _zero_first(p == 0, dq_ref)
        _zero_first(qi == ki, dka_ref, dva_ref)
        rows = pl.ds(pl.multiple_of(qi * tb, tb), tb)

        def step(masked):
            if masked:
                keep = lax.broadcasted_iota(jnp.int32, (tb, tb), 0) <= lax.broadcasted_iota(jnp.int32, (tb, tb), 1)
            for h in range(hg):
                st = _dot_nt(k_ref[h], q_ref[h])
                if masked:
                    st = jnp.where(keep, st, NEG)
                pt = jnp.exp2(st)
                dob = do_ref[h]
                dva_ref[h] += _dot(pt.astype(BF16), dob)
                dst = (pt * _dot_nt(v_ref[h], dob)).astype(BF16)
                dka_ref[h] += _dot(dst, q_ref[h])
                dq_ref[h, rows, :] += _dot_tn(dst, k_ref[h])

        @pl.when(qi > ki)
        def _():
            step(False)

        @pl.when(qi == ki)
        def _():
            step(True)

        @pl.when(qi == nb - 1)
        def _():
            dk_ref[...] = (dka_ref[...] * LN2).astype(BF16)
            dv_ref[...] = dva_ref[...].astype(BF16)

        @pl.when(p == len(pairs) - 1)
        def _():
            cp = pltpu.make_async_copy(dq_ref, dq_hbm.at[pl.ds(g * hg, hg)], sem)
            cp.start()
            cp.wait()

    kspec = pl.BlockSpec((hg, tb, LANES), lambda g, p, ki_ref, qi_ref: (g, ki_ref[p], 0))
    qspec = pl.BlockSpec((hg, tb, LANES), lambda g, p, ki_ref, qi_ref: (g, qi_ref[p], 0))
    hm = jax.ShapeDtypeStruct((N_HEADS, s, LANES), BF16)
    return pl.pallas_call(
        body, name="mla_flash_bwd",
        grid_spec=pltpu.PrefetchScalarGridSpec(
            num_scalar_prefetch=2, grid=(N_HEADS // hg, len(pairs)),
            in_specs=[ANY] * len(extra) + [qspec, kspec, kspec, qspec],
            out_specs=[kspec, kspec, ANY],
            scratch_shapes=[pltpu.VMEM((hg, tb, LANES), F32), pltpu.VMEM((hg, tb, LANES), F32),
                            pltpu.VMEM((hg, s, LANES), F32), pltpu.SemaphoreType.DMA]),
        out_shape=[hm, hm, jax.ShapeDtypeStruct((N_HEADS, s, LANES), F32)],
        compiler_params=_cp("arbitrary", "arbitrary"),
    )(ki_of, qi_of, *extra, qb, k, v, do)


def _mla_post_call(dq, dk, dv, cq, ckv, qnw, kvnw, wq, wk, wv, cos_m, sin_m, ts):
    s = cq.shape[0]

    def body(dq_ref, dk_ref, dv_ref, cq_ref, ckv_ref, qnw_ref, kvnw_ref, wq_ref, wk_ref, wv_ref, cm_ref, sm_ref,
             dcq_ref, dckv_ref, dkpe_ref, dwq_ref, dwk_ref, dwv_ref, dqnw_ref, dkvnw_ref):
        _zero_first(pl.program_id(0) == 0, dwq_ref, dwk_ref, dwv_ref, dqnw_ref, dkvnw_ref)
        cqv, ckvv = cq_ref[...], ckv_ref[...]
        rq, rkv = _rstd(cqv), _rstd(ckvv)
        qh_, kvh_ = cqv * rq, ckvv * rkv
        qnw_v, kvnw_v = qnw_ref[...], kvnw_ref[...]
        cqn = (qh_ * qnw_v).astype(BF16)
        ckvn = (kvh_ * kvnw_v).astype(BF16)
        cm, sm = cm_ref[...], sm_ref[...]
        dcqn = jnp.zeros((ts, Q_RANK), F32)
        dckvn = jnp.zeros((ts, KV_RANK), F32)
        dkpe = jnp.zeros((ts, LANES), F32)
        for h in range(N_HEADS):
            dqu = _unrope(dq_ref[h] * SM_SCALE, cm, sm, ROPE // 2).astype(BF16)
            dwq_ref[h] += _dot_tn(cqn, dqu)
            dcqn = dcqn + _dot_nt(dqu, wq_ref[h])
            dkb, dvb = dk_ref[h], dv_ref[h]
            dkpe = dkpe + dkb.astype(F32)
            dwk_ref[h] += _dot_tn(ckvn, dkb)
            dwv_ref[h] += _dot_tn(ckvn, dvb)
            dckvn = dckvn + _dot_nt(dkb, wk_ref[h]) + _dot_nt(dvb, wv_ref[h])
        lane = lax.broadcasted_iota(jnp.int32, (ts, LANES), 1)
        dkpe = jnp.where((lane >= KPE_LO) & (lane < KPE_LO + ROPE), dkpe, 0.0)
        dkpe_ref[...] = _unrope(dkpe, cm, sm, ROPE // 2).astype(BF16)
        dqnw_ref[...] += _colsum(dcqn * qh_)
        dkvnw_ref[...] += _colsum(dckvn * kvh_)
        dcq_ref[...] = _norm_bwd(dcqn, qh_, rq, qnw_v).astype(BF16)
        dckv_ref[...] = _norm_bwd(dckvn, kvh_, rkv, kvnw_v).astype(BF16)

    sd = jax.ShapeDtypeStruct
    hm = _hrow(N_HEADS, ts, LANES)
    return pl.pallas_call(
        body, name="mla_post", grid=(s // ts,),
        in_specs=[hm, hm, hm, _row(ts, Q_RANK), _row(ts, KV_RANK), _full((1, Q_RANK)), _full((1, KV_RANK)),
                  _full((N_HEADS, Q_RANK, LANES)), _full((N_HEADS, KV_RANK, LANES)), _full((N_HEADS, KV_RANK, LANES)),
                  _row(ts, LANES), _row(ts, LANES)],
        out_specs=[_row(ts, Q_RANK), _row(ts, KV_RANK), _row(ts, LANES),
                   _full((N_HEADS, Q_RANK, LANES)), _full((N_HEADS, KV_RANK, LANES)), _full((N_HEADS, KV_RANK, LANES)),
                   _full((1, Q_RANK)), _full((1, KV_RANK))],
        out_shape=[sd((s, Q_RANK), BF16), sd((s, KV_RANK), BF16), sd((s, LANES), BF16),
                   sd((N_HEADS, Q_RANK, LANES), F32), sd((N_HEADS, KV_RANK, LANES), F32), sd((N_HEADS, KV_RANK, LANES), F32),
                   sd((1, Q_RANK), F32), sd((1, KV_RANK), F32)],
        compiler_params=_cp("arbitrary"),
    )(dq, dk, dv, cq, ckv, qnw, kvnw, wq, wk, wv, cos_m, sin_m)


def _in_bwd_call(parts, x, r1, anw, dx1, win, ts):
    s = x.shape[0]
    widths = [p.shape[1] for p in parts]
    np_ = len(parts)

    def body(*refs):
        p_refs = refs[:np_]
        x_ref, r_ref, anw_ref, dx1_ref, w_ref, dx_ref, dw_ref, danw_ref = refs[np_:]
        _zero_first(pl.program_id(0) == 0, dw_ref, danw_ref)
        dproj = jnp.concatenate([p[...] for p in p_refs], axis=-1)
        r, anw_v = r_ref[...], anw_ref[...]
        xh = x_ref[...] * r
        dw_ref[...] += _dot_tn((xh * anw_v).astype(BF16), dproj)
        dh = _dot_nt(dproj, w_ref[...])
        danw_ref[...] += _colsum(dh * xh)
        dx_ref[...] = dx1_ref[...] + _norm_bwd(dh, xh, r, anw_v)

    sd = jax.ShapeDtypeStruct
    return pl.pallas_call(
        body, name="in_proj_bwd", grid=(s // ts,),
        in_specs=[_row(ts, w) for w in widths]
        + [_row(ts, D_MODEL), _row(ts, 1), _full((1, D_MODEL)), _row(ts, D_MODEL), _full((D_MODEL, IN_EXT))],
        out_specs=[_row(ts, D_MODEL), _full((D_MODEL, IN_EXT)), _full((1, D_MODEL))],
        out_shape=[sd((s, D_MODEL), F32), sd((D_MODEL, IN_EXT), F32), sd((1, D_MODEL), F32)],
        compiler_params=_cp("arbitrary"),
    )(*parts, x, r1, anw, dx1, win)


def _local_step(x, positions, tgt, w, small, ex=None):
    s = x.shape[0]
    t = _tiles(s)
    ex = _Exchanges(w) if ex is None else ex
    f = _forward(x, positions, tgt, w, small, ex)
    pw, rc = f["pw"], f["rc"]
    cos_r, sin_r, cos_m, sin_m = f["tabs"]
    dx2, loss, g_fw = f["dx2"], f["loss"], f["g_fw"]
    du, dx1, g_cw, g_cb, g_fnw, g_wd = _ffn_bwd_call(dx2, f["u"], f["uc"], w["conv_w"], pw["wdown"], pw["wup"],
                                                     f["x1"], f["r2"], small["ffn_norm_w"], t["t2"])
    g_wup = _dw_norm_call(f["x1"], f["r2"], small["ffn_norm_w"], du, t["ts"], F2 // 4, "dw_up")
    started = ex.mlp_grads(dict(w_up=g_wup, w_down=g_wd))
    dy_ret, do, g_wout = _out_bwd_call(dx1, f["y_ret"], f["y_mla"], pw["wout"], t["ts"], started)
    started = ex.behind_out_bwd(g_wout)
    drq, dg, do_ret, g_gnw = _ret_bwd_q_call(f["q"], f["k"], f["v"], f["o_ret"], f["g"], dy_ret, small["ret_gn_w"], rc, cos_r, sin_r, t["tr"])
    drk, drv = _ret_bwd_kv_call(f["q"], f["k"], f["v"], do_ret, rc, cos_r, sin_r, t["tr"])
    dmk, dmv, dmq = _flash_bwd_call(f["mqb"], f["mk"], f["mv"], do, t["tb"], started)
    ex.behind_attention(dmk)
    dcq, dckv, dkpe, g_wq, g_wk, g_wv, g_qnw, g_kvnw = _mla_post_call(
        dmq, dmk, dmv, f["cq"], f["ckv"], small["mla_q_norm_w"], small["mla_kv_norm_w"], pw["wq"], pw["wk"], pw["wv"], cos_m, sin_m, t["ts"])
    gx, g_win_ext, g_anw = _in_bwd_call([drq, drk, drv, dg, dcq, dckv, dkpe], x, f["r1"], small["attn_norm_w"], dx1, pw["win"], t["ts"])
    lo = IN_W - ROPE
    g_win = jnp.concatenate([g_win_ext[:, :lo], g_win_ext[:, lo + KPE_LO:lo + KPE_LO + ROPE]], -1)
    g_wuq = g_wq.transpose(1, 0, 2)[:, :, :HEAD + ROPE].reshape(Q_RANK, N_HEADS * (HEAD + ROPE))
    g_wukv = jnp.concatenate([g_wk[:, :, :HEAD], g_wv[:, :, :HEAD]], -1).transpose(1, 0, 2).reshape(KV_RANK, 2 * MLA_W)
    gw = dict(w_in=g_win, w_uq=g_wuq, w_ukv=g_wukv, w_out=g_wout, w_up=g_wup,
              conv_w=g_cw, w_down=g_wd)
    gs = dict(attn_norm_w=g_anw, ret_gn_w=g_gnw, mla_q_norm_w=g_qnw, mla_kv_norm_w=g_kvnw, ffn_norm_w=g_fnw,
              conv_b=g_cb, final_norm_w=g_fw)
    return loss, gx, gw, gs


MESH_ID = pl.DeviceIdType.MESH
ANY = pl.BlockSpec(memory_space=pl.ANY)
VMEM_SPEC = pl.BlockSpec(memory_space=pltpu.VMEM)
N_DEV = 8
GROUP_A = (("w_in", (D_MODEL, IN_W // 4), 1), ("w_uq", (Q_RANK, 192), 1), ("w_ukv", (KV_RANK, 256), 1),
           ("w_out", (D_MODEL // 4, D_MODEL), 0))
GROUP_B = (("w_up", (D_MODEL, F2 // 4), 1), ("w_down", (D_FF // 4, D_MODEL), 0))
HBM_SPEC = pl.BlockSpec(memory_space=pltpu.HBM)
SEM_SPEC = pl.BlockSpec(memory_space=pltpu.SEMAPHORE)


def _mesh_pos():
    return lax.axis_index("x"), lax.axis_index("y"), lax.axis_index("c")


def _other_chips(x, y):
    return [(1 - x, y), (x, 1 - y), (1 - x, 1 - y)]


def _remote(src, dst, send_sems, recv_sems, k, dev):
    return pltpu.make_async_remote_copy(src_ref=src, dst_ref=dst, send_sem=send_sems.at[k], recv_sem=recv_sems.at[k],
                                        device_id=dev, device_id_type=MESH_ID)


def _gather_list_call(parts, tag):
    n = len(parts)

    def body(*refs):
        srcs, outs, (send_sems, recv_sems) = refs[:n], refs[n:2 * n], refs[2 * n:]
        x, y, c = _mesh_pos()
        sm = 2 * x + y
        chips = _other_chips(x, y)
        sib = (x, y, 1 - c)
        rc = lambda k, src, dst, dev: _remote(src, dst, send_sems, recv_sems, k, dev)
        first = [rc(7 * i + j, srcs[i].at[c], outs[i].at[sm, c], (cx, cy, c)) for i in range(n) for j, (cx, cy) in enumerate(chips)]
        own = [rc(7 * i + 6, srcs[i], outs[i].at[sm], sib) for i in range(n)]
        for cp in first + own:
            cp.start()
        passed = []
        for j, (cx, cy) in enumerate(chips):
            for i in range(n):
                land = outs[i].at[2 * cx + cy, c]
                rc(7 * i + j, srcs[i].at[c], land, (cx, cy, c)).wait_recv()
                cp = rc(7 * i + 3 + j, land, land, sib)
                cp.start()
                passed.append(cp)
        for j, (cx, cy) in enumerate(chips):
            for i in range(n):
                rc(7 * i + 3 + j, srcs[i].at[c], outs[i].at[2 * cx + cy, 1 - c], sib).wait_recv()
        for cp in own:
            cp.wait_recv()
        for cp in first + passed + own:
            cp.wait_send()

    return pl.pallas_call(
        body, name="weights_all_gather_" + tag,
        in_specs=[ANY] * n, out_specs=[ANY] * n,
        out_shape=[jax.ShapeDtypeStruct((4,) + p.shape, p.dtype) for p in parts],
        scratch_shapes=[pltpu.SemaphoreType.DMA((7 * n,)), pltpu.SemaphoreType.DMA((7 * n,))],
    )(*parts)


def _direct_gather_copies(srcs, lands, send_sems, recv_sems):
    x, y, c = _mesh_pos()
    sm = 2 * x + y
    sends, recvs = [], []
    for i, (src, land) in enumerate(zip(srcs, lands)):
        for j, (cx, cy) in enumerate(_other_chips(x, y)):
            for t in range(2):
                sends.append(_remote(src.at[c], land.at[sm, c], send_sems, recv_sems, 13 * i + 4 * j + 2 * c + t, (cx, cy, t)))
                recvs.append(_remote(src.at[t], land.at[2 * cx + cy, t], send_sems, recv_sems, 13 * i + 4 * j + 2 * t + c, (cx, cy, t)))
        sends.append(_remote(src, land.at[sm], send_sems, recv_sems, 13 * i + 12, (x, y, 1 - c)))
        recvs.append(_remote(src, land.at[sm], send_sems, recv_sems, 13 * i + 12, (x, y, 1 - c)))
    return sends, recvs


def _sibling_copies(srcs, lands, send_sems, recv_sems):
    x, y, c = _mesh_pos()
    cps = [_remote(src.at[s, 1 - c], land.at[s], send_sems, recv_sems, 4 * i + s, (x, y, 1 - c))
           for i, (src, land) in enumerate(zip(srcs, lands)) for s in range(4)]
    return cps, cps


def _chips_copies(srcs, lands, send_sems, recv_sems):
    x, y, c = _mesh_pos()
    cps = [_remote(src.at[2 * cx + cy], land.at[j], send_sems, recv_sems, 3 * i + j, (cx, cy, c))
           for i, (src, land) in enumerate(zip(srcs, lands)) for j, (cx, cy) in enumerate(_other_chips(x, y))]
    return cps, cps


def _share_copies(srcs, lands, send_sems, recv_sems):
    x, y, c = _mesh_pos()
    cps = [_remote(src, land, send_sems, recv_sems, i, (x, y, 1 - c)) for i, (src, land) in enumerate(zip(srcs, lands))]
    return cps, cps


def _exchange_call(name, copies, srcs, land_shapes, n_sems):
    n = len(srcs)

    def body(*refs):
        sends, recvs = copies(refs[:n], refs[n:2 * n], refs[2 * n], refs[2 * n + 1])
        for cp in sends:
            cp.start()
        for cp in sends:
            cp.wait_send()
        for cp in recvs:
            cp.wait_recv()

    return pl.pallas_call(
        body, name=name, in_specs=[ANY] * n, out_specs=[ANY] * n, out_shape=list(land_shapes),
        scratch_shapes=[pltpu.SemaphoreType.DMA((n_sems,)), pltpu.SemaphoreType.DMA((n_sems,))],
    )(*srcs)


def _exchange_start_call(name, copies, srcs, land_shapes, n_sems, order=None):
    n = len(srcs)
    extra = [] if order is None else [order]
    k = 2 * n + len(extra)

    def body(*refs):
        sends, _ = copies(refs[:n], refs[n:2 * n], refs[k], refs[k + 1])
        for cp in sends:
            cp.start()
        refs[-1][...] = jnp.zeros_like(refs[-1])

    hbm = lambda a: pltpu.with_memory_space_constraint(a, pltpu.HBM)
    lands = [hbm(lax.empty(sd.shape, sd.dtype)) for sd in land_shapes]
    sem = pltpu.SemaphoreType.DMA((n_sems,))
    out = pl.pallas_call(
        body, name=name,
        out_shape=(sem, sem, *[pltpu.HBM(a.shape, a.dtype) for a in list(srcs) + lands], jax.ShapeDtypeStruct((8, LANES), F32)),
        in_specs=[HBM_SPEC] * (2 * n) + [ANY] * len(extra), out_specs=(SEM_SPEC, SEM_SPEC, *[HBM_SPEC] * (2 * n), VMEM_SPEC),
        input_output_aliases={i: 2 + i for i in range(2 * n)},
        compiler_params=pltpu.CompilerParams(has_side_effects=pltpu.SideEffectType.DATAFLOW_SIDE_EFFECTING),
    )(*[hbm(a) for a in srcs], *lands, *extra)
    return out[0], out[1], out[2:2 + n], out[2 + n:2 + 2 * n], out[-1]


def _exchange_wait_call(name, copies, started, after):
    send_sems, recv_sems, srcs, lands, _ = started
    n = len(srcs)

    def body(*refs):
        sends, recvs = copies(refs[:n], refs[n:2 * n], refs[2 * n], refs[2 * n + 1])
        for cp in sends:
            cp.wait_send()
        for cp in recvs:
            cp.wait_recv()

    out = pl.pallas_call(
        body, name=name,
        out_shape=tuple(pltpu.HBM(a.shape, a.dtype) for a in list(srcs) + list(lands)),
        in_specs=[HBM_SPEC] * (2 * n) + [SEM_SPEC, SEM_SPEC, ANY], out_specs=tuple([HBM_SPEC] * (2 * n)),
        input_output_aliases={i: i for i in range(2 * n)},
        compiler_params=pltpu.CompilerParams(has_side_effects=pltpu.SideEffectType.DATAFLOW_SIDE_EFFECTING),
    )(*srcs, *lands, send_sems, recv_sems, after)
    return out[:n], out[n:]


def _rows_tile(rows, width, itemsize=4):
    limit = max(16, (3 << 20) // (width * itemsize))
    if rows <= limit:
        return rows
    return max(t for t in range(16, limit + 1, 16) if rows % t == 0)


def _sum_sibling_call(g, buf, c, name):
    _, _, rh, w = g.shape
    tile = _rows_tile(rh, w)

    def body(c_ref, g_ref, b_ref, p_ref, pb_ref):
        p = g_ref[...] + b_ref[...]
        p_ref[...] = p
        pb_ref[...] = p.astype(BF16)

    blk = pl.BlockSpec((None, tile, w), lambda s, i, c_ref: (s, i, 0))
    return pl.pallas_call(
        body, name=name,
        grid_spec=pltpu.PrefetchScalarGridSpec(
            num_scalar_prefetch=1, grid=(4, rh // tile),
            in_specs=[pl.BlockSpec((None, None, tile, w), lambda s, i, c_ref: (s, c_ref[0], i, 0)), blk],
            out_specs=[blk, blk]),
        out_shape=[jax.ShapeDtypeStruct((4, rh, w), F32), jax.ShapeDtypeStruct((4, rh, w), BF16)],
        compiler_params=_cp("parallel", "parallel"),
    )(c, g, buf)


def _sum_chips_call(p, buf, sm, name):
    _, rh, w = p.shape
    tile = _rows_tile(rh, w)

    def body(sm_ref, p_ref, b_ref, f_ref):
        f_ref[...] = ((p_ref[...] + b_ref[0].astype(F32)) + b_ref[1].astype(F32)) + b_ref[2].astype(F32)

    return pl.pallas_call(
        body, name=name,
        grid_spec=pltpu.PrefetchScalarGridSpec(
            num_scalar_prefetch=1, grid=(rh // tile,),
            in_specs=[pl.BlockSpec((None, tile, w), lambda i, sm_ref: (sm_ref[0], i, 0)),
                      pl.BlockSpec((3, tile, w), lambda i, sm_ref: (0, i, 0))],
            out_specs=pl.BlockSpec((tile, w), lambda i, sm_ref: (i, 0))),
        out_shape=jax.ShapeDtypeStruct((rh, w), F32),
        compiler_params=_cp("parallel"),
    )(sm, p, buf)


def _adamw_halves_call(w, g_mine, g_sib, c, m, v, name):
    r, wd = w.shape
    rh = r // 2
    tile = _rows_tile(rh, wd)
    nt = rh // tile

    def body(c_ref, w_ref, gm_ref, gs_ref, m_ref, v_ref, g_ref, d_ref, nm_ref, nv_ref):
        gv = jnp.where(pl.program_id(0) == c_ref[0], gm_ref[...], gs_ref[...])
        g_ref[...] = gv
        nm = ADAM_B1 * m_ref[...] + (1.0 - ADAM_B1) * gv
        nv = ADAM_B2 * v_ref[...] + (1.0 - ADAM_B2) * jnp.square(gv)
        m_hat = nm / (1.0 - ADAM_B1 ** ADAM_STEP)
        v_hat = nv / (1.0 - ADAM_B2 ** ADAM_STEP)
        d_ref[...] = -ADAM_LR * (m_hat / (jnp.sqrt(v_hat) + ADAM_EPS) + ADAM_WD * w_ref[...])
        nm_ref[...] = nm
        nv_ref[...] = nv

    whole = pl.BlockSpec((tile, wd), lambda h, i, c_ref: (h * nt + i, 0))
    half = pl.BlockSpec((tile, wd), lambda h, i, c_ref: (i, 0))
    sd = jax.ShapeDtypeStruct((r, wd), F32)
    return pl.pallas_call(
        body, name=name,
        grid_spec=pltpu.PrefetchScalarGridSpec(
            num_scalar_prefetch=1, grid=(2, nt),
            in_specs=[whole, half, half, whole, whole], out_specs=[whole] * 4),
        out_shape=[sd, sd, sd, sd],
        compiler_params=_cp("parallel", "parallel"),
    )(c, w, g_mine, g_sib, m, v)


def _exchange8_call(vec, reduce, name):
    rows = vec.shape[0]

    def body(v_ref, out_ref, *rest):
        slots, send_sems, recv_sems = (rest if reduce else (out_ref,) + rest)
        x, y, c = _mesh_pos()
        me = 4 * x + 2 * y + c
        slots[me] = v_ref[...]

        def rcopy(k, to_me):
            bx, by, bc = (k >> 2) & 1, (k >> 1) & 1, k & 1
            px, py, pc = (1 - x if bx else x), (1 - y if by else y), (1 - c if bc else c)
            slot = 4 * px + 2 * py + pc if to_me else me
            return pltpu.make_async_remote_copy(src_ref=v_ref, dst_ref=slots.at[slot], send_sem=send_sems.at[k - 1],
                                                recv_sem=recv_sems.at[k - 1], device_id=(px, py, pc), device_id_type=MESH_ID)

        for k in range(1, N_DEV):
            rcopy(k, False).start()
        for k in range(1, N_DEV):
            rcopy(k, True).wait_recv()
        for k in range(1, N_DEV):
            rcopy(k, False).wait_send()
        if reduce:
            tot = slots[0]
            for d in range(1, N_DEV):
                tot = tot + slots[d]
            out_ref[...] = tot

    stack = jax.ShapeDtypeStruct((N_DEV, rows, LANES), F32)
    return pl.pallas_call(
        body, name=name,
        in_specs=[VMEM_SPEC], out_specs=VMEM_SPEC,
        out_shape=jax.ShapeDtypeStruct((rows, LANES), F32) if reduce else stack,
        scratch_shapes=([pltpu.VMEM((N_DEV, rows, LANES), F32)] if reduce else [])
        + [pltpu.SemaphoreType.DMA((N_DEV - 1,)), pltpu.SemaphoreType.DMA((N_DEV - 1,))],
    )(vec)


def _adamw_call(w, g, m, v, name):
    r, c = w.shape
    rb = r if r <= 256 else (256 if r % 256 == 0 else 352)
    assert r % rb == 0

    def body(w_ref, g_ref, m_ref, v_ref, d_ref, nm_ref, nv_ref):
        gv = g_ref[...]
        nm = ADAM_B1 * m_ref[...] + (1.0 - ADAM_B1) * gv
        nv = ADAM_B2 * v_ref[...] + (1.0 - ADAM_B2) * jnp.square(gv)
        m_hat = nm / (1.0 - ADAM_B1 ** ADAM_STEP)
        v_hat = nv / (1.0 - ADAM_B2 ** ADAM_STEP)
        d_ref[...] = -ADAM_LR * (m_hat / (jnp.sqrt(v_hat) + ADAM_EPS) + ADAM_WD * w_ref[...])
        nm_ref[...] = nm
        nv_ref[...] = nv

    spec = pl.BlockSpec((rb, c), lambda i: (i, 0))
    sd = jax.ShapeDtypeStruct((r, c), F32)
    return pl.pallas_call(
        body, name=name, grid=(r // rb,),
        in_specs=[spec] * 4, out_specs=[spec] * 3, out_shape=[sd, sd, sd],
        compiler_params=_cp("parallel"),
    )(w, g, m, v)


SMALL = (("attn_norm_w", D_MODEL), ("ret_gn_w", RET_W), ("mla_q_norm_w", Q_RANK), ("mla_kv_norm_w", KV_RANK),
         ("ffn_norm_w", D_MODEL), ("conv_b", F2), ("final_norm_w", D_MODEL))
WEIGHT_ORDER = ("attn_norm_w", "w_in", "ret_gn_w", "mla_q_norm_w", "w_uq", "mla_kv_norm_w", "w_ukv", "w_out",
                "ffn_norm_w", "w_up", "conv_w", "conv_b", "w_down", "final_norm_w")


def _pad_rows(flat, rows):
    return jnp.concatenate([flat, jnp.zeros((rows * LANES - flat.shape[0],), flat.dtype)]).reshape(rows, LANES)


def kernel(x, positions, attn_norm_w, w_in, ret_gn_w, mla_q_norm_w, w_uq, mla_kv_norm_w, w_ukv, w_out, ffn_norm_w, w_up, conv_w, conv_b, w_down, final_norm_w, loss_target, m_attn_norm_w, m_w_in, m_ret_gn_w, m_mla_q_norm_w, m_w_uq, m_mla_kv_norm_w, m_w_ukv, m_w_out, m_ffn_norm_w, m_w_up, m_conv_w, m_conv_b, m_w_down, m_final_norm_w, v_attn_norm_w, v_w_in, v_ret_gn_w, v_mla_q_norm_w, v_w_uq, v_mla_kv_norm_w, v_w_ukv, v_w_out, v_ffn_norm_w, v_w_up, v_conv_w, v_conv_b, v_w_down, v_final_norm_w):
    args = dict(locals())
    cx, cy, cc = _mesh_pos()
    sm = 2 * cx + cy

    c_arr, sm_arr = cc.reshape(1).astype(jnp.int32), sm.reshape(1).astype(jnp.int32)
    sds = jax.ShapeDtypeStruct

    def my_shards(group):
        return [args[n][0].astype(BF16).reshape(2, r // 2, c) for n, (r, c), _ in group]

    def full_weights(gathered, group):
        full = {}
        for (n, (r, c), axis), got in zip(group, gathered):
            piece = got.reshape(4, r, c)
            full[n] = piece if n == "w_up" else (piece.transpose(1, 0, 2).reshape(r, 4 * c) if axis == 1 else piece.reshape(4 * r, c))
        return full

    def by_owner(gw, group):
        out = []
        for n, (r, c), axis in group:
            g = gw[n]
            if axis == 1 and g.ndim == 2:
                g = g.reshape(r, 4, c).transpose(1, 0, 2)
            out.append(g.reshape(4, 2, r // 2, c))
        return out

    def sibling_shapes(gs):
        return [sds((4,) + g.shape[2:], F32) for g in gs]

    def chip_sums(gs, bufs, group):
        res = [_sum_sibling_call(g, b, c_arr, "grads_sum_sibling_" + n) for g, b, (n, _, _) in zip(gs, bufs, group)]
        return [p for p, _ in res], [pb for _, pb in res]

    def chips_shapes(pbs):
        return [sds((3,) + pb.shape[1:], BF16) for pb in pbs]

    def totals(ps, lands, group, tag):
        fins = [_sum_chips_call(p, l, sm_arr, "grads_sum_chips_" + n) for p, l, (n, _, _) in zip(ps, lands, group)]
        sibs = _exchange_call("grads_rs_share_" + tag, _share_copies, fins, [sds(f.shape, F32) for f in fins], len(fins))
        return {n: (f, s) for (n, _, _), f, s in zip(group, fins, sibs)}

    class StepExchanges(_Exchanges):
        def __init__(self, order):
            shards = my_shards(GROUP_B)
            self.gather = _exchange_start_call("weights_gather_start_b", _direct_gather_copies, shards,
                                               [sds((4,) + s.shape, BF16) for s in shards], 13 * len(shards), order)
            self.red = None

        def token(self):
            return self.gather[4][0:1, 0:1]

        def mlp_weights(self, after):
            return full_weights(_exchange_wait_call("weights_gather_wait_b", _direct_gather_copies, self.gather, after)[1], GROUP_B)

        def mlp_grads(self, gw):
            gs = by_owner(gw, GROUP_B)
            self.step1 = _exchange_start_call("grads_rs_sibling_start_b", _sibling_copies, gs, sibling_shapes(gs), 4 * len(gs))
            return self.step1[4]

        def behind_out_bwd(self, after):
            gs, bufs = _exchange_wait_call("grads_rs_sibling_wait_b", _sibling_copies, self.step1, after)
            self.ps, pbs = chip_sums(gs, bufs, GROUP_B)
            self.step2 = _exchange_start_call("grads_rs_chips_start_b", _chips_copies, pbs, chips_shapes(pbs), 3 * len(pbs))
            return self.step2[4]

        def behind_attention(self, after):
            _, lands = _exchange_wait_call("grads_rs_chips_wait_b", _chips_copies, self.step2, after)
            self.red = totals(self.ps, lands, GROUP_B, "b")

    gathered = _gather_list_call(my_shards(GROUP_A) + [conv_w[0].reshape(2, 1, 3 * F2 // 8)], "a")
    full = full_weights(gathered[:-1], GROUP_A)
    ex = StepExchanges(gathered[-1])
    full["conv_w"] = gathered[-1].reshape(4, 3, F2 // 4).transpose(1, 0, 2).reshape(3, F2)
    small = {n: args[n].reshape(1, d) for n, d in SMALL}
    small["attn_norm_w"] = small["attn_norm_w"] + ex.token()

    loss, gx, gw, gs = _local_step(x[0], positions[0], loss_target[0], full, small, ex)

    ga = by_owner(gw, GROUP_A)
    bufs = _exchange_call("grads_rs_sibling_a", _sibling_copies, ga, sibling_shapes(ga), 4 * len(ga))
    ps, pbs = chip_sums(ga, bufs, GROUP_A)
    lands = _exchange_call("grads_rs_chips_a", _chips_copies, pbs, chips_shapes(pbs), 3 * len(pbs))
    halves = {**ex.red, **totals(ps, lands, GROUP_A, "a")}

    vec = jnp.concatenate([gs[n].reshape(-1) for n, _ in SMALL] + [gw["conv_w"].reshape(-1), loss.reshape(-1)])
    tot = _exchange8_call(_pad_rows(vec, 216), True, "small_all_reduce").reshape(-1)
    red, off = {}, 0
    for n, d in SMALL:
        red[n] = tot[off:off + d].reshape(1, d)
        off += d
    red["conv_w"] = lax.dynamic_slice(tot[off:off + 3 * F2].reshape(3, F2), (0, sm * (F2 // 4)), (3, F2 // 4))
    loss_tot = tot[off + 3 * F2]

    grads, deltas, new_m, new_v = [], [], [], []
    for n in WEIGHT_ORDER:
        shape = args[n].shape
        two_d = (1, shape[0]) if len(shape) == 1 else shape[-2:]
        wmv = [args[k + n].reshape(two_d) for k in ("", "m_", "v_")]
        if n in halves:
            g, d, nm, nv = _adamw_halves_call(wmv[0], *halves[n], c_arr, wmv[1], wmv[2], "adamw_" + n)
        else:
            g = red[n].reshape(two_d)
            d, nm, nv = _adamw_call(wmv[0], g, wmv[1], wmv[2], "adamw_" + n)
        grads.append(g.reshape(shape))
        deltas.append(d.reshape(shape))
        new_m.append(nm.reshape(shape))
        new_v.append(nv.reshape(shape))
    return (loss_tot, gx[None], *grads, *deltas, *new_m, *new_v)
```

```python
import functools
import math

import numpy as np
import jax
import jax.numpy as jnp
from jax import lax
from jax.experimental import pallas as pl
from jax.experimental.pallas import tpu as pltpu

F32 = jnp.float32
BF16 = jnp.bfloat16

D_MODEL = 1024
N_HEADS = 8
HEAD = 64
RET_W = N_HEADS * HEAD
MLA_W = N_HEADS * HEAD
ROPE = 32
Q_RANK = 256
KV_RANK = 128
D_FF = 2816
F2 = 2 * D_FF
IN_W = 4 * RET_W + Q_RANK + KV_RANK + ROPE
IN_EXT = 4 * RET_W + Q_RANK + KV_RANK + 128
KPE_LO = 64
ROPE_BASE = 10000.0
EPS = 1e-6
RET_CHUNK = 256
SM_SCALE = (HEAD + ROPE) ** -0.5
LOG2E = math.log2(math.e)
LN2 = math.log(2.0)
NEG = -1e30
LANES = 128
VMEM_LIMIT = 56 * 1024 * 1024

ADAM_LR = 0.001
ADAM_B1 = 0.9
ADAM_B2 = 0.999
ADAM_EPS = 1e-08
ADAM_WD = 0.01
ADAM_STEP = 10


VMEM_LIMIT_MLP_BWD = 60 * 1024 * 1024


def _cp(*sem, vmem=VMEM_LIMIT):
    return pltpu.CompilerParams(dimension_semantics=sem, vmem_limit_bytes=vmem)


def _full(shape):
    n = len(shape)
    return pl.BlockSpec(tuple(shape), lambda *_: (0,) * n)


def _row(ts, c):
    return pl.BlockSpec((ts, c), lambda i: (i, 0))


def _hrow(h, ts, c):
    return pl.BlockSpec((h, ts, c), lambda i: (0, i, 0))


def _dot(a, b):
    return jnp.dot(a, b, preferred_element_type=F32)


def _dot_nt(a, b):
    return lax.dot_general(a, b, (((1,), (1,)), ((), ())), preferred_element_type=F32)


def _dot_tn(a, b):
    return lax.dot_general(a, b, (((0,), (0,)), ((), ())), preferred_element_type=F32)


def _dot_hi(a, b):
    hi = a.astype(BF16)
    lo = (a - hi.astype(F32)).astype(BF16)
    bb = b.astype(BF16)
    return _dot(hi, bb) + _dot(lo, bb)


def _rot_half(x, half):
    w = x.shape[-1]
    lane = lax.broadcasted_iota(jnp.int32, x.shape, x.ndim - 1)
    first = (lane % (2 * half)) < half
    return jnp.where(first, -pltpu.roll(x, w - half, x.ndim - 1), pltpu.roll(x, half, x.ndim - 1))


def _rope(x, cos, sin, half):
    return x * cos + _rot_half(x, half) * sin


def _unrope(dy, cos, sin, half):
    return dy * cos - _rot_half(dy, half) * sin


def _sigmoid(g):
    return 0.5 * jnp.tanh(0.5 * g) + 0.5


def _silu(g):
    return g * _sigmoid(g)


def _rstd(x):
    return lax.rsqrt(jnp.mean(x * x, axis=-1, keepdims=True) + EPS)


def _rope_tables(positions):
    pos = positions.astype(F32)[:, None]
    s = pos.shape[0]
    inv = ROPE_BASE ** (-jnp.arange(0, HEAD, 2, dtype=F32) / HEAD)
    ang = pos * inv
    c, sn = jnp.cos(ang), jnp.sin(ang)
    cos_r = jnp.tile(jnp.concatenate([c, c], -1), (1, 2))
    sin_r = jnp.tile(jnp.concatenate([sn, sn], -1), (1, 2))
    inv = ROPE_BASE ** (-jnp.arange(0, ROPE, 2, dtype=F32) / ROPE)
    ang = pos * inv
    c, sn = jnp.cos(ang), jnp.sin(ang)
    one, zero = jnp.ones((s, KPE_LO), F32), jnp.zeros((s, KPE_LO), F32)
    cos_m = jnp.concatenate([one, c, c, one[:, :LANES - KPE_LO - ROPE]], -1)
    sin_m = jnp.concatenate([zero, sn, sn, zero[:, :LANES - KPE_LO - ROPE]], -1)
    return cos_r, sin_r, cos_m, sin_m


def _ret_consts():
    c = RET_CHUNK
    lg = np.log1p(-np.power(2.0, -5.0 - np.arange(N_HEADS, dtype=np.float64)))
    idx = np.arange(c, dtype=np.float64)
    diff = idx[:, None] - idx[None, :]
    lane_head = np.arange(LANES) // HEAD
    dmask = np.zeros((4, 2, c, c))
    zeta = np.zeros((4, c, LANES))
    xi = np.zeros((4, c, LANES))
    cd = np.zeros((4, LANES, LANES))
    bd = (lane_head[:, None] == lane_head[None, :]).astype(np.float64)
    for j in range(4):
        for hh in range(2):
            dmask[j, hh] = np.where(diff >= 0, np.exp(lg[2 * j + hh] * np.maximum(diff, 0.0)), 0.0)
        lgl = lg[2 * j + lane_head]
        zeta[j] = np.exp(lgl[None, :] * (c - 1.0 - idx[:, None]))
        xi[j] = np.exp(lgl[None, :] * (idx[:, None] + 1.0))
        cd[j] = np.exp(lgl * c)[:, None] * bd
    f = lambda a: jnp.asarray(a, F32)
    side = lambda d: np.concatenate([d[:, 0], d[:, 1]], axis=-1)
    return dict(dmask=f(side(dmask)), dmask_t=f(side(np.swapaxes(dmask, 2, 3))), zeta=f(zeta), xi=f(xi), cd=f(cd), bd=f(bd))


def _f1_call(x, anw, win, cos_r, sin_r, cos_m, sin_m, ts):
    s = x.shape[0]

    def body(x_ref, anw_ref, w_ref, cr_ref, sr_ref, cm_ref, sm_ref,
             q_ref, k_ref, v_ref, g_ref, cq_ref, ckv_ref, kpe_ref, r_ref):
        xv = x_ref[...]
        r = _rstd(xv)
        r_ref[...] = r
        h = (xv * r * anw_ref[...]).astype(BF16)
        cr, sr = cr_ref[...], sr_ref[...]
        qk = _dot(h, w_ref[:, 0:2 * RET_W])
        for j in range(4):
            sl = slice(j * LANES, (j + 1) * LANES)
            q_ref[:, sl] = _rope(qk[:, sl], cr, sr, HEAD // 2).astype(BF16)
            kk = qk[:, RET_W + j * LANES:RET_W + (j + 1) * LANES]
            k_ref[:, sl] = (_rope(kk, cr, sr, HEAD // 2) * (HEAD ** -0.5)).astype(BF16)
        v_ref[...] = _dot(h, w_ref[:, 2 * RET_W:3 * RET_W]).astype(BF16)
        g_ref[...] = _dot(h, w_ref[:, 3 * RET_W:4 * RET_W])
        o = 4 * RET_W
        cq_ref[...] = _dot(h, w_ref[:, o:o + Q_RANK])
        ckv_ref[...] = _dot(h, w_ref[:, o + Q_RANK:o + Q_RANK + KV_RANK])
        kp = _dot(h, w_ref[:, o + Q_RANK + KV_RANK:IN_EXT])
        kpe_ref[...] = _rope(kp, cm_ref[...], sm_ref[...], ROPE // 2)

    sd = jax.ShapeDtypeStruct
    return pl.pallas_call(
        body, name="f1_in_proj", grid=(s // ts,),
        in_specs=[_row(ts, D_MODEL), _full((1, D_MODEL)), _full((D_MODEL, IN_EXT)),
                  _row(ts, LANES), _row(ts, LANES), _row(ts, LANES), _row(ts, LANES)],
        out_specs=[_row(ts, RET_W), _row(ts, RET_W), _row(ts, RET_W), _row(ts, RET_W),
                   _row(ts, Q_RANK), _row(ts, KV_RANK), _row(ts, LANES), _row(ts, 1)],
        out_shape=[sd((s, RET_W), BF16), sd((s, RET_W), BF16), sd((s, RET_W), BF16), sd((s, RET_W), F32),
                   sd((s, Q_RANK), F32), sd((s, KV_RANK), F32), sd((s, LANES), F32), sd((s, 1), F32)],
        compiler_params=_cp("parallel"),
    )(x, anw, win, cos_r, sin_r, cos_m, sin_m)


def _stack_heads(a):
    lo = lax.broadcasted_iota(jnp.int32, a.shape, 1) < HEAD
    zero = jnp.zeros_like(a)
    return jnp.concatenate([jnp.where(lo, a, zero), jnp.where(lo, zero, a)], axis=0)


def _pair_product(a, b2, decay2, w2):
    return _dot((_dot_nt(a, b2) * decay2).astype(BF16), w2)


RET_SLABS = 4


def _ret_specs(tr, tile_of):
    c, ns = RET_CHUNK, RET_SLABS
    return dict(
        slab=pl.BlockSpec((tr, ns * LANES), lambda j, i: (tile_of(i), j)),
        tab=pl.BlockSpec((tr, LANES), lambda j, i: (tile_of(i), 0)),
        vec=pl.BlockSpec((1, ns * LANES), lambda j, i: (0, j)),
        dmask=pl.BlockSpec((ns, c, 2 * c), lambda j, i: (j, 0, 0)),
        rows=pl.BlockSpec((ns, c, LANES), lambda j, i: (j, 0, 0)),
        state=pl.BlockSpec((ns, LANES, LANES), lambda j, i: (j, 0, 0)),
        bd=pl.BlockSpec((LANES, LANES), lambda j, i: (0, 0)))


def _ret_states(a_ref, b_ref, scale_ref, cd_ref, bd, st_ref, chunks, lanes, reverse):
    nc = len(chunks)
    contrib = [[_dot_tn((a_ref[rows, ln].astype(F32) * scale_ref[sl]).astype(BF16), b_ref[rows, ln]) * bd for rows in chunks]
               for sl, ln in enumerate(lanes)]
    states = []
    for sl in range(len(lanes)):
        st, seen = st_ref[sl], [None] * nc
        for ci in (reversed(range(nc)) if reverse else range(nc)):
            seen[ci] = st.astype(BF16)
            st = st * cd_ref[sl] + contrib[sl][ci]
        st_ref[sl] = st
        states.append(seen)
    return states


def _ret_fwd_call(q, k, v, g, gnw, rc, tr):
    s = q.shape[0]
    c = RET_CHUNK
    nc = tr // c
    ns = RET_SLABS

    def body(q_ref, k_ref, v_ref, g_ref, gnw_ref, dm_ref, zeta_ref, xi_ref, cd_ref, bd_ref, o_ref, y_ref, st_ref):
        @pl.when(pl.program_id(1) == 0)
        def _():
            st_ref[...] = jnp.zeros_like(st_ref)

        bd = bd_ref[...]
        chunks = [slice(ci * c, (ci + 1) * c) for ci in range(nc)]
        lanes = [slice(sl * LANES, (sl + 1) * LANES) for sl in range(ns)]
        states = _ret_states(k_ref, v_ref, zeta_ref, cd_ref, bd, st_ref, chunks, lanes, False)
        for ci, rows in enumerate(chunks):
            for sl, ln in enumerate(lanes):
                qc = q_ref[rows, ln]
                o_ref[rows, ln] = (_dot(qc, states[sl][ci]) * xi_ref[sl]
                                   + _pair_product(qc, _stack_heads(k_ref[rows, ln]), dm_ref[sl], _stack_heads(v_ref[rows, ln])))
        avg = bd * (1.0 / HEAD)
        for ln in lanes:
            o = o_ref[:, ln]
            ctr = o - _dot_hi(o, avg)
            var = _dot_hi(ctr * ctr, avg)
            y_ref[:, ln] = (_silu(g_ref[:, ln]) * (ctr * lax.rsqrt(var + EPS) * gnw_ref[:, ln])).astype(BF16)

    specs = _ret_specs(tr, lambda i: i)
    sd = jax.ShapeDtypeStruct
    return pl.pallas_call(
        body, name="ret_fwd", grid=(4 // ns, s // tr),
        in_specs=[specs["slab"]] * 4 + [specs["vec"], specs["dmask"], specs["rows"], specs["rows"], specs["state"], specs["bd"]],
        out_specs=[specs["slab"]] * 2,
        out_shape=[sd((s, RET_W), F32), sd((s, RET_W), BF16)],
        scratch_shapes=[pltpu.VMEM((ns, LANES, LANES), F32)],
        compiler_params=_cp("parallel", "arbitrary"),
    )(q, k, v, g, gnw, rc["dmask"], rc["zeta"], rc["xi"], rc["cd"], rc["bd"])


QK_AUX = HEAD + ROPE
V_AUX = HEAD


def _lane_pair(shape, lo, a, b, rest):
    lane = lax.broadcasted_iota(jnp.int32, shape, len(shape) - 1) % LANES
    return jnp.where(lane == lo, a, jnp.where(lane == lo + 1, b, rest))


def _hi_lo(v):
    hi = v.astype(BF16).astype(F32)
    return hi, v - hi


def _mla_pre_call(cq, ckv, kpe, qnw, kvnw, wq, wk, wv, cos_m, sin_m, ts):
    s = cq.shape[0]

    def body(cq_ref, ckv_ref, kpe_ref, qnw_ref, kvnw_ref, wq_ref, wk_ref, wv_ref, cm_ref, sm_ref, q_ref, k_ref, v_ref):
        cqv, ckvv = cq_ref[...], ckv_ref[...]
        cqn = (cqv * _rstd(cqv) * qnw_ref[...]).astype(BF16)
        ckvn = (ckvv * _rstd(ckvv) * kvnw_ref[...]).astype(BF16)
        all_heads = lambda a: jnp.tile(a, (1, N_HEADS))
        qa = _rope(_dot(cqn, wq_ref[...]), all_heads(cm_ref[...]), all_heads(sm_ref[...]), ROPE // 2) * (SM_SCALE * LOG2E)
        ka = _dot(ckvn, wk_ref[...]) + all_heads(_lane_pair((ts, LANES), QK_AUX, -1.0, -1.0, kpe_ref[...]))
        va = _lane_pair((ts, N_HEADS * LANES), V_AUX, 1.0, 1.0, _dot(ckvn, wv_ref[...]))
        for h in range(N_HEADS):
            sl = slice(h * LANES, (h + 1) * LANES)
            q_ref[h] = qa[:, sl].astype(BF16)
            k_ref[h] = ka[:, sl].astype(BF16)
            v_ref[h] = va[:, sl].astype(BF16)

    sd = jax.ShapeDtypeStruct
    hm = sd((N_HEADS, s, LANES), BF16)
    return pl.pallas_call(
        body, name="mla_pre", grid=(s // ts,),
        in_specs=[_row(ts, Q_RANK), _row(ts, KV_RANK), _row(ts, LANES), _full((1, Q_RANK)), _full((1, KV_RANK)),
                  _full((Q_RANK, N_HEADS * LANES)), _full((KV_RANK, N_HEADS * LANES)), _full((KV_RANK, N_HEADS * LANES)),
                  _row(ts, LANES), _row(ts, LANES)],
        out_specs=[_hrow(N_HEADS, ts, LANES)] * 3,
        out_shape=[hm, hm, hm],
        compiler_params=_cp("parallel"),
    )(cq, ckv, kpe, qnw, kvnw, wq, wk, wv, cos_m, sin_m)


def _flash_fwd_call(q, k, v, tb):
    s = q.shape[1]
    nb = s // tb
    pairs = [(a, b) for a in range(nb) for b in range(a + 1)]
    qi_of, ki_of = (jnp.asarray(np.array(col, np.int32)) for col in zip(*pairs))

    def body(qi_ref, ki_ref, q_ref, k_ref, v_ref, o_ref, qb_ref, m_ref, acc_ref):
        qi, ki = qi_ref[pl.program_id(0)], ki_ref[pl.program_id(0)]

        @pl.when(ki == 0)
        def _():
            m_ref[...] = jnp.full_like(m_ref, NEG)
            acc_ref[...] = jnp.zeros_like(acc_ref)

        def step(masked):
            if masked:
                keep = lax.broadcasted_iota(jnp.int32, (tb, tb), 1) <= lax.broadcasted_iota(jnp.int32, (tb, tb), 0)
            def finish(h, pe, alpha):
                acc_ref[h] = acc_ref[h] * alpha + _dot(pe, v_ref[h])

            nxt, pending = _dot_nt(q_ref[0], k_ref[0]), None
            for h in range(N_HEADS):
                sc = nxt
                if h + 1 < N_HEADS:
                    nxt = _dot_nt(q_ref[h + 1], k_ref[h + 1])
                if masked:
                    sc = jnp.where(keep, sc, NEG)
                m_prev = m_ref[h]
                m_new = jnp.maximum(m_prev, jnp.max(sc, axis=1, keepdims=True))
                pe = jnp.exp2(sc - jnp.tile(m_new, (1, tb // LANES))).astype(BF16)
                m_ref[h] = m_new
                if pending is not None:
                    finish(*pending)
                pending = (h, pe, jnp.exp2(m_prev - m_new))
            finish(*pending)

        @pl.when(ki < qi)
        def _():
            step(False)

        @pl.when(ki == qi)
        def _():
            step(True)
            lane = lax.broadcasted_iota(jnp.int32, (tb, LANES), 1)
            for p in range(N_HEADS // 2):
                outs = []
                for h in (2 * p, 2 * p + 1):
                    acc = acc_ref[h]
                    l = acc[:, V_AUX:V_AUX + 1]
                    outs.append(acc * (1.0 / l))
                    hi, lo = _hi_lo(m_ref[h][:, 0:1] + jnp.log(l) * LOG2E)
                    qb_ref[h] = _lane_pair((tb, LANES), QK_AUX, hi, lo, q_ref[h].astype(F32)).astype(BF16)
                o_ref[:, p * LANES:(p + 1) * LANES] = jnp.where(lane < HEAD, outs[0], pltpu.roll(outs[1], HEAD, 1)).astype(BF16)

    sd = jax.ShapeDtypeStruct
    qspec = pl.BlockSpec((N_HEADS, tb, LANES), lambda p, qi_ref, ki_ref: (0, qi_ref[p], 0))
    kspec = pl.BlockSpec((N_HEADS, tb, LANES), lambda p, qi_ref, ki_ref: (0, ki_ref[p], 0))
    return pl.pallas_call(
        body, name="mla_flash_fwd",
        grid_spec=pltpu.PrefetchScalarGridSpec(
            num_scalar_prefetch=2, grid=(len(pairs),),
            in_specs=[qspec, kspec, kspec],
            out_specs=[pl.BlockSpec((tb, MLA_W), lambda p, qi_ref, ki_ref: (qi_ref[p], 0)), qspec],
            scratch_shapes=[pltpu.VMEM((N_HEADS, tb, LANES), F32), pltpu.VMEM((N_HEADS, tb, LANES), F32)]),
        out_shape=[sd((s, MLA_W), BF16), sd((N_HEADS, s, LANES), BF16)],
        compiler_params=_cp("arbitrary"),
    )(qi_of, ki_of, q, k, v)


def _out_proj_call(x, yret, ymla, wout, ts):
    s = x.shape[0]

    def body(x_ref, yr_ref, ym_ref, w_ref, x1_ref, r_ref):
        x1 = x_ref[...] + _dot(yr_ref[...], w_ref[0:RET_W, :]) + _dot(ym_ref[...], w_ref[RET_W:, :])
        x1_ref[...] = x1
        r_ref[...] = _rstd(x1)

    sd = jax.ShapeDtypeStruct
    return pl.pallas_call(
        body, name="out_proj", grid=(s // ts,),
        in_specs=[_row(ts, D_MODEL), _row(ts, RET_W), _row(ts, MLA_W), _full((D_MODEL, D_MODEL))],
        out_specs=[_row(ts, D_MODEL), _row(ts, 1)],
        out_shape=[sd((s, D_MODEL), F32), sd((s, 1), F32)],
        compiler_params=_cp("parallel"),
    )(x, yret, ymla, wout)


W_UP_SHARD = F2 // 4


def _ffn_fwd_call(x1, r2, fnw, wup4, cw, cb, wdown, tgt, fw, ts):
    s = x1.shape[0]
    wsh = W_UP_SHARD

    def body(x_ref, r_ref, fnw_ref, wup_ref, cw_ref, cb_ref, wd_ref, t_ref, fw_ref,
             u_ref, uc_ref, dx2_ref, loss_ref, gfw_ref, carry_ref):
        _zero_first(pl.program_id(0) == 0, carry_ref, loss_ref, gfw_ref)
        xv = x_ref[...]
        h = (xv * r_ref[...] * fnw_ref[...]).astype(BF16)
        conv = []
        for j in range(4):
            cols = slice(j * wsh, (j + 1) * wsh)
            ub = _dot(h, wup_ref[j]).astype(BF16)
            u_ref[:, cols] = ub
            u = ub.astype(F32)
            u1, u2 = _shifted(u, carry_ref[:, cols])
            w = cw_ref[:, cols]
            cb16 = (cb_ref[:, cols] + w[0:1, :] * u2 + w[1:2, :] * u1 + w[2:3, :] * u).astype(BF16)
            uc_ref[:, cols] = cb16
            conv.append(cb16.astype(F32))
            carry_ref[:, cols] = u[ts - 8:, :]
        acc = xv
        for j in range(2):
            a = (_silu(conv[j]) * conv[j + 2]).astype(BF16)
            acc = acc + _dot(a, wd_ref[j * wsh:(j + 1) * wsh, :])
        r = _rstd(acc)
        xh = acc * r
        fwv = fw_ref[...]
        e = xh * fwv - t_ref[...]
        loss_ref[...] += (0.5 / D_MODEL) * _colsum(jnp.sum(e * e, axis=1, keepdims=True))
        dy = e * (1.0 / D_MODEL)
        gfw_ref[...] += _colsum(dy * xh)
        dx2_ref[...] = _norm_bwd(dy, xh, r, fwv)

    sd = jax.ShapeDtypeStruct
    once = lambda shape: pl.BlockSpec(shape, lambda i: (0,) * len(shape), pipeline_mode=pl.Buffered(1))
    return pl.pallas_call(
        body, name="ffn_fwd_loss", grid=(s // ts,),
        in_specs=[_row(ts, D_MODEL), _row(ts, 1), once((1, D_MODEL)), once((4, D_MODEL, wsh)),
                  once((3, F2)), once((1, F2)), once((D_FF, D_MODEL)), _row(ts, D_MODEL), once((1, D_MODEL))],
        out_specs=[_row(ts, F2), _row(ts, F2), _row(ts, D_MODEL), _full((1, 1)), _full((1, D_MODEL))],
        out_shape=[sd((s, F2), BF16), sd((s, F2), BF16), sd((s, D_MODEL), F32), sd((1, 1), F32), sd((1, D_MODEL), F32)],
        scratch_shapes=[pltpu.VMEM((8, F2), F32)],
        compiler_params=_cp("arbitrary"),
    )(x1, r2, fnw, wup4, cw, cb, wdown, tgt, fw)


def _shifted(u, hal):
    row = lax.broadcasted_iota(jnp.int32, hal.shape, 0)
    r1, r2 = pltpu.roll(u, 1, 0), pltpu.roll(u, 2, 0)
    top1 = jnp.where(row == 0, hal[7:8, :], r1[0:8, :])
    top2 = jnp.where(row == 0, hal[6:7, :], jnp.where(row == 1, hal[7:8, :], r2[0:8, :]))
    return jnp.concatenate([top1, r1[8:, :]], axis=0), jnp.concatenate([top2, r2[8:, :]], axis=0)


def _prep_weights(w):
    win = w["w_in"]
    pad = lambda n: jnp.zeros((D_MODEL, n), win.dtype)
    win_ext = jnp.concatenate([win[:, :IN_W - ROPE], pad(KPE_LO), win[:, IN_W - ROPE:], pad(LANES - KPE_LO - ROPE)], -1)
    wuq = w["w_uq"].reshape(Q_RANK, N_HEADS, HEAD + ROPE)
    wq = jnp.concatenate([wuq, jnp.zeros((Q_RANK, N_HEADS, LANES - HEAD - ROPE), wuq.dtype)], -1).reshape(Q_RANK, N_HEADS * LANES)
    wukv = w["w_ukv"].reshape(KV_RANK, N_HEADS, 2 * HEAD)
    zk = jnp.zeros((KV_RANK, N_HEADS, HEAD), wukv.dtype)
    wk = jnp.concatenate([wukv[:, :, :HEAD], zk], -1).reshape(KV_RANK, N_HEADS * LANES)
    wv = jnp.concatenate([wukv[:, :, HEAD:], zk], -1).reshape(KV_RANK, N_HEADS * LANES)
    c = lambda a: a.astype(BF16)
    return dict(win=c(win_ext), wq=c(wq), wk=c(wk), wv=c(wv), wout=c(w["w_out"]))


def _prep_mlp_weights(w):
    wup = w["w_up"]
    if wup.ndim == 2:
        wup = wup.reshape(D_MODEL, 4, W_UP_SHARD).transpose(1, 0, 2)
    return dict(wup=wup.astype(BF16), wdown=w["w_down"].astype(BF16))


def _tiles(s):
    return dict(ts=min(s, 512), tr=min(s, 1024), tb=min(s, 512), t2=min(s, 256))


class _Exchanges:
    def __init__(self, w):
        self.w = w

    def mlp_weights(self, after):
        return self.w

    def mlp_grads(self, gw):
        pass

    def behind_out_bwd(self, after):
        pass

    def behind_attention(self, after):
        pass


def _forward(x, positions, tgt, w, small, ex):
    s = x.shape[0]
    t = _tiles(s)
    pw = _prep_weights(w)
    cos_r, sin_r, cos_m, sin_m = _rope_tables(positions)
    rc = _ret_consts()
    q, k, v, g, cq, ckv, kpe, r1 = _f1_call(x, small["attn_norm_w"], pw["win"], cos_r, sin_r, cos_m, sin_m, t["ts"])
    o_ret, y_ret = _ret_fwd_call(q, k, v, g, small["ret_gn_w"], rc, t["tr"])
    mq, mk, mv = _mla_pre_call(cq, ckv, kpe, small["mla_q_norm_w"], small["mla_kv_norm_w"],
                               pw["wq"], pw["wk"], pw["wv"], cos_m, sin_m, t["ts"])
    y_mla, mqb = _flash_fwd_call(mq, mk, mv, t["tb"])
    x1, r2 = _out_proj_call(x, y_ret, y_mla, pw["wout"], t["ts"])
    pw.update(_prep_mlp_weights(ex.mlp_weights(r2)))
    u, uc, dx2, loss, g_fw = _ffn_fwd_call(x1, r2, small["ffn_norm_w"], pw["wup"], w["conv_w"], small["conv_b"], pw["wdown"],
                                           tgt, small["final_norm_w"], t["t2"])
    return dict(pw=pw, tabs=(cos_r, sin_r, cos_m, sin_m), rc=rc, q=q, k=k, v=v, g=g, cq=cq, ckv=ckv, kpe=kpe, r1=r1,
                o_ret=o_ret, y_ret=y_ret, mqb=mqb, mk=mk, mv=mv, y_mla=y_mla, x1=x1, r2=r2, u=u, uc=uc,
                dx2=dx2, loss=loss, g_fw=g_fw)


def _norm_bwd(dh, xh, r, nw):
    dxn = dh * nw
    return r * (dxn - xh * jnp.mean(dxn * xh, axis=-1, keepdims=True))


def _ordered_after(body, order):
    if order is None:
        return body, [], []
    return (lambda order_ref, *refs: body(*refs)), [pl.BlockSpec(memory_space=pl.ANY)], [order]


def _zero_first(first, *refs):
    @pl.when(first)
    def _():
        for ref in refs:
            ref[...] = jnp.zeros_like(ref)


def _colsum(v):
    return jnp.sum(v, axis=0, keepdims=True)


def _dsilu(g, sg):
    return sg * (1.0 + g * (1.0 - sg))


def _ffn_bwd_call(dx2, u, uc, cw, wdown, wup4, x1, r2, fnw, ts):
    s = dx2.shape[0]
    nt = s // ts
    wsh = W_UP_SHARD
    rev = lambda i: nt - 1 - i

    def body(dx2_ref, u_ref, uc_ref, cw_ref, wd_ref, wup_ref, x_ref, r_ref, fnw_ref,
             du_ref, dx1_ref, dcw_ref, dcb_ref, dfnw_ref, dwd_hbm, carry_ref, dwd_ref, sem):
        i = pl.program_id(0)
        _zero_first(i == 0, carry_ref, dwd_ref, dcw_ref, dcb_ref, dfnw_ref)
        dxb = dx2_ref[...].astype(BF16)
        dh = jnp.zeros((ts, D_MODEL), F32)
        for j in range(2):
            gcols = slice(j * wsh, (j + 1) * wsh)
            vcols = slice(D_FF + j * wsh, D_FF + (j + 1) * wsh)
            gate, val = uc_ref[:, gcols].astype(F32), uc_ref[:, vcols].astype(F32)
            da = _dot_nt(dxb, wd_ref[gcols, :])
            sg = _sigmoid(gate)
            sl = gate * sg
            dwd_ref[gcols, :] += _dot_tn((sl * val).astype(BF16), dxb)
            for d, cols, shard in ((da * val * _dsilu(gate, sg), gcols, j), (da * sl, vcols, 2 + j)):
                d1, d2 = _shifted_up(d, carry_ref[:, cols])
                uv = u_ref[:, cols].astype(F32)
                for t, dt in enumerate((d2, d1, d)):
                    dcw_ref[t:t + 1, cols] += _colsum(dt * uv)
                dcb_ref[:, cols] += _colsum(d)
                w = cw_ref[:, cols]
                du = (w[2:3, :] * d + w[1:2, :] * d1 + w[0:1, :] * d2).astype(BF16)
                du_ref[:, cols] = du
                dh = dh + _dot_nt(du, wup_ref[shard])
                carry_ref[:, cols] = d[0:8, :]
        r = r_ref[...]
        xh = x_ref[...] * r
        dfnw_ref[...] += _colsum(dh * xh)
        dx1_ref[...] = dx2_ref[...] + _norm_bwd(dh, xh, r, fnw_ref[...])

        @pl.when(i == nt - 1)
        def _():
            cp = pltpu.make_async_copy(dwd_ref, dwd_hbm, sem)
            cp.start()
            cp.wait()

    sd = jax.ShapeDtypeStruct
    row = lambda c: pl.BlockSpec((ts, c), lambda i: (rev(i), 0))
    once = lambda shape: pl.BlockSpec(shape, lambda i: (0,) * len(shape), pipeline_mode=pl.Buffered(1))
    return pl.pallas_call(
        body, name="ffn_bwd", grid=(nt,),
        in_specs=[row(D_MODEL), row(F2), row(F2), once((3, F2)), once((D_FF, D_MODEL)), once((4, D_MODEL, wsh)),
                  row(D_MODEL), row(1), once((1, D_MODEL))],
        out_specs=[row(F2), row(D_MODEL), _full((3, F2)), _full((1, F2)), _full((1, D_MODEL)), pl.BlockSpec(memory_space=pl.ANY)],
        out_shape=[sd((s, F2), BF16), sd((s, D_MODEL), F32), sd((3, F2), F32), sd((1, F2), F32), sd((1, D_MODEL), F32),
                   sd((D_FF, D_MODEL), F32)],
        scratch_shapes=[pltpu.VMEM((8, F2), F32), pltpu.VMEM((D_FF, D_MODEL), F32), pltpu.SemaphoreType.DMA],
        compiler_params=_cp("arbitrary", vmem=VMEM_LIMIT_MLP_BWD),
    )(dx2, u, uc, cw, wdown, wup4, x1, r2, fnw)


def _shifted_up(d, hal):
    n = d.shape[0]
    row = lax.broadcasted_iota(jnp.int32, hal.shape, 0)
    r1, r2 = pltpu.roll(d, n - 1, 0), pltpu.roll(d, n - 2, 0)
    end1 = jnp.where(row == 7, hal[0:1, :], r1[n - 8:, :])
    end2 = jnp.where(row == 6, hal[0:1, :], jnp.where(row == 7, hal[1:2, :], r2[n - 8:, :]))
    return jnp.concatenate([r1[:n - 8, :], end1], axis=0), jnp.concatenate([r2[:n - 8, :], end2], axis=0)


def _dw_norm_call(x, r, nw, b, ts, tn, name):
    s, n = b.shape
    k = x.shape[1]

    def body(x_ref, r_ref, nw_ref, b_ref, dw_ref):
        _zero_first(pl.program_id(1) == 0, dw_ref)
        h = (x_ref[...] * r_ref[...] * nw_ref[...]).astype(BF16)
        dw_ref[...] += _dot_tn(h, b_ref[...])

    return pl.pallas_call(
        body, name=name, grid=(n // tn, s // ts),
        in_specs=[pl.BlockSpec((ts, k), lambda j, i: (i, 0)), pl.BlockSpec((ts, 1), lambda j, i: (i, 0)),
                  pl.BlockSpec((1, k), lambda j, i: (0, 0)), pl.BlockSpec((ts, tn), lambda j, i: (i, j))],
        out_specs=pl.BlockSpec((None, k, tn), lambda j, i: (j, 0, 0)),
        out_shape=jax.ShapeDtypeStruct((n // tn, k, tn), F32),
        compiler_params=_cp("parallel", "arbitrary"),
    )(x, r, nw, b)


def _out_bwd_call(dx1, yret, ymla, wout, ts, order=None):
    s = dx1.shape[0]

    def body(dx_ref, yr_ref, ym_ref, w_ref, dyr_ref, do_ref, dwo_ref):
        _zero_first(pl.program_id(0) == 0, dwo_ref)
        dxb = dx_ref[...].astype(BF16)
        dmix = _dot_nt(dxb, w_ref[...])
        dyr_ref[...] = dmix[:, :RET_W]
        ym = ym_ref[...]
        lane = lax.broadcasted_iota(jnp.int32, (ts, LANES), 1)
        for p in range(N_HEADS // 2):
            dom = dmix[:, RET_W + p * LANES:RET_W + (p + 1) * LANES]
            prod = dom * ym[:, p * LANES:(p + 1) * LANES].astype(F32)
            for hh in range(2):
                mine = (lane >= HEAD) if hh else (lane < HEAD)
                hi, lo = _hi_lo(jnp.sum(jnp.where(mine, prod, 0.0), axis=1, keepdims=True))
                base = jnp.where(lane < HEAD, pltpu.roll(dom, HEAD, 1) if hh else dom, 0.0)
                do_ref[2 * p + hh] = _lane_pair((ts, LANES), V_AUX, -hi, -lo, base).astype(BF16)
        dwo_ref[0:RET_W, :] += _dot_tn(yr_ref[...], dxb)
        dwo_ref[RET_W:, :] += _dot_tn(ym, dxb)

    sd = jax.ShapeDtypeStruct
    body, first_specs, first = _ordered_after(body, order)
    return pl.pallas_call(
        body, name="out_proj_bwd", grid=(s // ts,),
        in_specs=first_specs + [_row(ts, D_MODEL), _row(ts, RET_W), _row(ts, MLA_W), _full((D_MODEL, D_MODEL))],
        out_specs=[_row(ts, RET_W), _hrow(N_HEADS, ts, LANES), _full((D_MODEL, D_MODEL))],
        out_shape=[sd((s, RET_W), F32), sd((N_HEADS, s, LANES), BF16), sd((D_MODEL, D_MODEL), F32)],
        compiler_params=_cp("arbitrary"),
    )(*first, dx1, yret, ymla, wout)


def _ret_bwd_q_call(q, k, v, o, g, dy, gnw, rc, cos_r, sin_r, tr):
    s = q.shape[0]
    c = RET_CHUNK
    nc = tr // c
    ns = RET_SLABS

    def body(q_ref, k_ref, v_ref, o_ref, g_ref, dy_ref, gnw_ref, dm_ref, zeta_ref, xi_ref, cd_ref, bd_ref, cr_ref, sr_ref,
             dq_ref, dg_ref, do_ref, dgnw_ref, st_ref):
        _zero_first(pl.program_id(1) == 0, st_ref, dgnw_ref)
        bd = bd_ref[...]
        avg = bd * (1.0 / HEAD)
        chunks = [slice(ci * c, (ci + 1) * c) for ci in range(nc)]
        lanes = [slice(sl * LANES, (sl + 1) * LANES) for sl in range(ns)]
        dov = []
        for ln in lanes:
            ov = o_ref[:, ln]
            ctr = ov - _dot_hi(ov, avg)
            rs = lax.rsqrt(_dot_hi(ctr * ctr, avg) + EPS)
            oh = ctr * rs
            gg, dyv, gnw_v = g_ref[:, ln], dy_ref[:, ln], gnw_ref[:, ln]
            sg = _sigmoid(gg)
            sl = gg * sg
            dg_ref[:, ln] = (dyv * oh * gnw_v * _dsilu(gg, sg)).astype(BF16)
            dgnw_ref[:, ln] += _colsum(dyv * sl * oh)
            doh = dyv * sl * gnw_v
            dov.append((rs * (doh - _dot_hi(doh, avg) - oh * _dot_hi(doh * oh, avg))).astype(BF16))
            do_ref[:, ln] = dov[-1]
        states = _ret_states(k_ref, v_ref, zeta_ref, cd_ref, bd, st_ref, chunks, lanes, False)
        for ci, rows in enumerate(chunks):
            for sl, ln in enumerate(lanes):
                doc = dov[sl][rows, :]
                dq = (_dot_nt(doc, states[sl][ci]) * xi_ref[sl]
                      + _pair_product(doc, _stack_heads(v_ref[rows, ln]), dm_ref[sl], _stack_heads(k_ref[rows, ln])))
                dq_ref[rows, ln] = _unrope(dq, cr_ref[rows, :], sr_ref[rows, :], HEAD // 2).astype(BF16)

    specs = _ret_specs(tr, lambda i: i)
    sd = jax.ShapeDtypeStruct
    return pl.pallas_call(
        body, name="ret_bwd_q", grid=(4 // ns, s // tr),
        in_specs=[specs["slab"]] * 6 + [specs["vec"], specs["dmask"], specs["rows"], specs["rows"], specs["state"], specs["bd"],
                                        specs["tab"], specs["tab"]],
        out_specs=[specs["slab"]] * 3 + [specs["vec"]],
        out_shape=[sd((s, RET_W), BF16), sd((s, RET_W), BF16), sd((s, RET_W), BF16), sd((1, RET_W), F32)],
        scratch_shapes=[pltpu.VMEM((ns, LANES, LANES), F32)],
        compiler_params=_cp("parallel", "arbitrary"),
    )(q, k, v, o, g, dy, gnw, rc["dmask"], rc["zeta"], rc["xi"], rc["cd"], rc["bd"], cos_r, sin_r)


def _ret_bwd_kv_call(q, k, v, do, rc, cos_r, sin_r, tr):
    s = q.shape[0]
    c = RET_CHUNK
    nc = tr // c
    nt = s // tr
    ns = RET_SLABS

    def body(q_ref, k_ref, v_ref, do_ref, dm_ref, zeta_ref, xi_ref, cd_ref, bd_ref, cr_ref, sr_ref, dk_ref, dv_ref, gs_ref):
        _zero_first(pl.program_id(1) == 0, gs_ref)
        bd = bd_ref[...]
        chunks = [slice(ci * c, (ci + 1) * c) for ci in range(nc)]
        lanes = [slice(sl * LANES, (sl + 1) * LANES) for sl in range(ns)]
        states = _ret_states(q_ref, do_ref, xi_ref, cd_ref, bd, gs_ref, chunks, lanes, True)
        for ci, rows in enumerate(chunks):
            for sl, ln in enumerate(lanes):
                kc, vc = k_ref[rows, ln], v_ref[rows, ln]
                q2, do2 = _stack_heads(q_ref[rows, ln]), _stack_heads(do_ref[rows, ln])
                gb = states[sl][ci]
                dk = _dot_nt(vc, gb) * zeta_ref[sl] + _pair_product(vc, do2, dm_ref[sl], q2)
                dv = _dot(kc, gb) * zeta_ref[sl] + _pair_product(kc, q2, dm_ref[sl], do2)
                dk_ref[rows, ln] = (_unrope(dk, cr_ref[rows, :], sr_ref[rows, :], HEAD // 2) * (HEAD ** -0.5)).astype(BF16)
                dv_ref[rows, ln] = dv.astype(BF16)

    specs = _ret_specs(tr, lambda i: nt - 1 - i)
    sd = jax.ShapeDtypeStruct
    return pl.pallas_call(
        body, name="ret_bwd_kv", grid=(4 // ns, nt),
        in_specs=[specs["slab"]] * 4 + [specs["dmask"], specs["rows"], specs["rows"], specs["state"], specs["bd"],
                                        specs["tab"], specs["tab"]],
        out_specs=[specs["slab"]] * 2,
        out_shape=[sd((s, RET_W), BF16), sd((s, RET_W), BF16)],
        scratch_shapes=[pltpu.VMEM((ns, LANES, LANES), F32)],
        compiler_params=_cp("parallel", "arbitrary"),
    )(q, k, v, do, rc["dmask_t"], rc["zeta"], rc["xi"], rc["cd"], rc["bd"], cos_r, sin_r)


FLASH_BWD_HEADS = 8


def _flash_bwd_call(qb, k, v, do, tb, order=None):
    s = qb.shape[1]
    nb = s // tb
    hg = FLASH_BWD_HEADS
    pairs = [(a, b) for a in range(nb) for b in range(a, nb)]
    ki_of, qi_of = (jnp.asarray(np.array(col, np.int32)) for col in zip(*pairs))
    extra = [] if order is None else [order]

    def body(ki_ref, qi_ref, *refs):
        q_ref, k_ref, v_ref, do_ref, dk_ref, dv_ref, dq_hbm, dka_ref, dva_ref, dq_ref, sem = refs[len(extra):]
        g, p = pl.program_id(0), pl.program_id(1)
        ki, qi = ki_ref[p], qi_ref[p]
        _zero_first(p == 0, dq_ref)
        _zero_first(qi == ki, dka_ref, dva_ref)
        rows = pl.ds(pl.multiple_of(qi * tb, tb), tb)

        def step(masked):
            if masked:
                keep = lax.broadcasted_iota(jnp.int32, (tb, tb), 0) <= lax.broadcasted_iota(jnp.int32, (tb, tb), 1)
            for h in range(hg):
                st = _dot_nt(k_ref[h], q_ref[h])
                if masked:
                    st = jnp.where(keep, st, NEG)
                pt = jnp.exp2(st)
                dob = do_ref[h]
                dva_ref[h] += _dot(pt.astype(BF16), dob)
                dst = (pt * _dot_nt(v_ref[h], dob)).astype(BF16)
                dka_ref[h] += _dot(dst, q_ref[h])
                dq_ref[h, rows, :] += _dot_tn(dst, k_ref[h])

        @pl.when(qi > ki)
        def _():
            step(False)

        @pl.when(qi == ki)
        def _():
            step(True)

        @pl.when(qi == nb - 1)
        def _():
            dk_ref[...] = (dka_ref[...] * LN2).astype(BF16)
            dv_ref[...] = dva_ref[...].astype(BF16)

        @pl.when(p == len(pairs) - 1)
        def _():
            cp = pltpu.make_async_copy(dq_ref, dq_hbm.at[pl.ds(g * hg, hg)], sem)
            cp.start()
            cp.wait()

    kspec = pl.BlockSpec((hg, tb, LANES), lambda g, p, ki_ref, qi_ref: (g, ki_ref[p], 0))
    qspec = pl.BlockSpec((hg, tb, LANES), lambda g, p, ki_ref, qi_ref: (g, qi_ref[p], 0))
    hm = jax.ShapeDtypeStruct((N_HEADS, s, LANES), BF16)
    return pl.pallas_call(
        body, name="mla_flash_bwd",
        grid_spec=pltpu.PrefetchScalarGridSpec(
            num_scalar_prefetch=2, grid=(N_HEADS // hg, len(pairs)),
            in_specs=[ANY] * len(extra) + [qspec, kspec, kspec, qspec],
            out_specs=[kspec, kspec, ANY],
            scratch_shapes=[pltpu.VMEM((hg, tb, LANES), F32), pltpu.VMEM((hg, tb, LANES), F32),
                            pltpu.VMEM((hg, s, LANES), F32), pltpu.SemaphoreType.DMA]),
        out_shape=[hm, hm, jax.ShapeDtypeStruct((N_HEADS, s, LANES), F32)],
        compiler_params=_cp("arbitrary", "arbitrary"),
    )(ki_of, qi_of, *extra, qb, k, v, do)


def _mla_post_call(dq, dk, dv, cq, ckv, qnw, kvnw, wq, wk, wv, cos_m, sin_m, ts):
    s = cq.shape[0]

    def body(dq_ref, dk_ref, dv_ref, cq_ref, ckv_ref, qnw_ref, kvnw_ref, wq_ref, wk_ref, wv_ref, cm_ref, sm_ref,
             dcq_ref, dckv_ref, dkpe_ref, dwq_ref, dwk_ref, dwv_ref, dqnw_ref, dkvnw_ref):
        _zero_first(pl.program_id(0) == 0, dwq_ref, dwk_ref, dwv_ref, dqnw_ref, dkvnw_ref)
        cqv, ckvv = cq_ref[...], ckv_ref[...]
        rq, rkv = _rstd(cqv), _rstd(ckvv)
        qh_, kvh_ = cqv * rq, ckvv * rkv
        qnw_v, kvnw_v = qnw_ref[...], kvnw_ref[...]
        cqn = (qh_ * qnw_v).astype(BF16)
        ckvn = (kvh_ * kvnw_v).astype(BF16)
        cm, sm = cm_ref[...], sm_ref[...]
        all_heads = lambda a: jnp.tile(a, (1, N_HEADS))
        side = lambda ref, f: jnp.concatenate([f(ref[h]) for h in range(N_HEADS)], axis=1)
        dqu = _unrope(side(dq_ref, lambda a: a * SM_SCALE), all_heads(cm), all_heads(sm), ROPE // 2).astype(BF16)
        dka, dva = side(dk_ref, lambda a: a), side(dv_ref, lambda a: a)
        dwq_ref[...] += _dot_tn(cqn, dqu)
        dwk_ref[...] += _dot_tn(ckvn, dka)
        dwv_ref[...] += _dot_tn(ckvn, dva)
        dcqn = _dot_nt(dqu, wq_ref[...])
        dckvn = _dot_nt(dka, wk_ref[...]) + _dot_nt(dva, wv_ref[...])
        dkpe = jnp.zeros((ts, LANES), F32)
        for h in range(N_HEADS):
            dkpe = dkpe + dk_ref[h].astype(F32)
        lane = lax.broadcasted_iota(jnp.int32, (ts, LANES), 1)
        dkpe = jnp.where((lane >= KPE_LO) & (lane < KPE_LO + ROPE), dkpe, 0.0)
        dkpe_ref[...] = _unrope(dkpe, cm, sm, ROPE // 2).astype(BF16)
        dqnw_ref[...] += _colsum(dcqn * qh_)
        dkvnw_ref[...] += _colsum(dckvn * kvh_)
        dcq_ref[...] = _norm_bwd(dcqn, qh_, rq, qnw_v).astype(BF16)
        dckv_ref[...] = _norm_bwd(dckvn, kvh_, rkv, kvnw_v).astype(BF16)

    sd = jax.ShapeDtypeStruct
    hm = _hrow(N_HEADS, ts, LANES)
    wide = N_HEADS * LANES
    return pl.pallas_call(
        body, name="mla_post", grid=(s // ts,),
        in_specs=[hm, hm, hm, _row(ts, Q_RANK), _row(ts, KV_RANK), _full((1, Q_RANK)), _full((1, KV_RANK)),
                  _full((Q_RANK, wide)), _full((KV_RANK, wide)), _full((KV_RANK, wide)),
                  _row(ts, LANES), _row(ts, LANES)],
        out_specs=[_row(ts, Q_RANK), _row(ts, KV_RANK), _row(ts, LANES),
                   _full((Q_RANK, wide)), _full((KV_RANK, wide)), _full((KV_RANK, wide)),
                   _full((1, Q_RANK)), _full((1, KV_RANK))],
        out_shape=[sd((s, Q_RANK), BF16), sd((s, KV_RANK), BF16), sd((s, LANES), BF16),
                   sd((Q_RANK, wide), F32), sd((KV_RANK, wide), F32), sd((KV_RANK, wide), F32),
                   sd((1, Q_RANK), F32), sd((1, KV_RANK), F32)],
        compiler_params=_cp("arbitrary"),
    )(dq, dk, dv, cq, ckv, qnw, kvnw, wq, wk, wv, cos_m, sin_m)


def _in_bwd_call(parts, x, r1, anw, dx1, win, ts):
    s = x.shape[0]
    widths = [p.shape[1] for p in parts]
    np_ = len(parts)

    def body(*refs):
        p_refs = refs[:np_]
        x_ref, r_ref, anw_ref, dx1_ref, w_ref, dx_ref, dw_ref, danw_ref = refs[np_:]
        _zero_first(pl.program_id(0) == 0, dw_ref, danw_ref)
        dproj = jnp.concatenate([p[...] for p in p_refs], axis=-1)
        r, anw_v = r_ref[...], anw_ref[...]
        xh = x_ref[...] * r
        dw_ref[...] += _dot_tn((xh * anw_v).astype(BF16), dproj)
        dh = _dot_nt(dproj, w_ref[...])
        danw_ref[...] += _colsum(dh * xh)
        dx_ref[...] = dx1_ref[...] + _norm_bwd(dh, xh, r, anw_v)

    sd = jax.ShapeDtypeStruct
    return pl.pallas_call(
        body, name="in_proj_bwd", grid=(s // ts,),
        in_specs=[_row(ts, w) for w in widths]
        + [_row(ts, D_MODEL), _row(ts, 1), _full((1, D_MODEL)), _row(ts, D_MODEL), _full((D_MODEL, IN_EXT))],
        out_specs=[_row(ts, D_MODEL), _full((D_MODEL, IN_EXT)), _full((1, D_MODEL))],
        out_shape=[sd((s, D_MODEL), F32), sd((D_MODEL, IN_EXT), F32), sd((1, D_MODEL), F32)],
        compiler_params=_cp("arbitrary"),
    )(*parts, x, r1, anw, dx1, win)


def _local_step(x, positions, tgt, w, small, ex=None):
    s = x.shape[0]
    t = _tiles(s)
    ex = _Exchanges(w) if ex is None else ex
    f = _forward(x, positions, tgt, w, small, ex)
    pw, rc = f["pw"], f["rc"]
    cos_r, sin_r, cos_m, sin_m = f["tabs"]
    dx2, loss, g_fw = f["dx2"], f["loss"], f["g_fw"]
    du, dx1, g_cw, g_cb, g_fnw, g_wd = _ffn_bwd_call(dx2, f["u"], f["uc"], w["conv_w"], pw["wdown"], pw["wup"],
                                                     f["x1"], f["r2"], small["ffn_norm_w"], t["t2"])
    g_wup = _dw_norm_call(f["x1"], f["r2"], small["ffn_norm_w"], du, t["ts"], F2 // 4, "dw_up")
    started = ex.mlp_grads(dict(w_up=g_wup, w_down=g_wd))
    dy_ret, do, g_wout = _out_bwd_call(dx1, f["y_ret"], f["y_mla"], pw["wout"], t["ts"], started)
    started = ex.behind_out_bwd(g_wout)
    drq, dg, do_ret, g_gnw = _ret_bwd_q_call(f["q"], f["k"], f["v"], f["o_ret"], f["g"], dy_ret, small["ret_gn_w"], rc, cos_r, sin_r, t["tr"])
    drk, drv = _ret_bwd_kv_call(f["q"], f["k"], f["v"], do_ret, rc, cos_r, sin_r, t["tr"])
    dmk, dmv, dmq = _flash_bwd_call(f["mqb"], f["mk"], f["mv"], do, t["tb"], started)
    ex.behind_attention(dmk)
    dcq, dckv, dkpe, g_wq, g_wk, g_wv, g_qnw, g_kvnw = _mla_post_call(
        dmq, dmk, dmv, f["cq"], f["ckv"], small["mla_q_norm_w"], small["mla_kv_norm_w"], pw["wq"], pw["wk"], pw["wv"], cos_m, sin_m, t["ts"])
    gx, g_win_ext, g_anw = _in_bwd_call([drq, drk, drv, dg, dcq, dckv, dkpe], x, f["r1"], small["attn_norm_w"], dx1, pw["win"], t["ts"])
    lo = IN_W - ROPE
    g_win = jnp.concatenate([g_win_ext[:, :lo], g_win_ext[:, lo + KPE_LO:lo + KPE_LO + ROPE]], -1)
    per_head = lambda g, r: g.reshape(r, N_HEADS, LANES)
    g_wuq = per_head(g_wq, Q_RANK)[:, :, :HEAD + ROPE].reshape(Q_RANK, N_HEADS * (HEAD + ROPE))
    g_wukv = jnp.concatenate([per_head(g_wk, KV_RANK)[:, :, :HEAD], per_head(g_wv, KV_RANK)[:, :, :HEAD]], -1).reshape(KV_RANK, 2 * MLA_W)
    gw = dict(w_in=g_win, w_uq=g_wuq, w_ukv=g_wukv, w_out=g_wout, w_up=g_wup,
              conv_w=g_cw, w_down=g_wd)
    gs = dict(attn_norm_w=g_anw, ret_gn_w=g_gnw, mla_q_norm_w=g_qnw, mla_kv_norm_w=g_kvnw, ffn_norm_w=g_fnw,
              conv_b=g_cb, final_norm_w=g_fw)
    return loss, gx, gw, gs


MESH_ID = pl.DeviceIdType.MESH
ANY = pl.BlockSpec(memory_space=pl.ANY)
VMEM_SPEC = pl.BlockSpec(memory_space=pltpu.VMEM)
N_DEV = 8
GROUP_A = (("w_in", (D_MODEL, IN_W // 4), 1), ("w_uq", (Q_RANK, 192), 1), ("w_ukv", (KV_RANK, 256), 1),
           ("w_out", (D_MODEL // 4, D_MODEL), 0))
GROUP_B = (("w_up", (D_MODEL, F2 // 4), 1), ("w_down", (D_FF // 4, D_MODEL), 0))
HBM_SPEC = pl.BlockSpec(memory_space=pltpu.HBM)
SEM_SPEC = pl.BlockSpec(memory_space=pltpu.SEMAPHORE)


def _mesh_pos():
    return lax.axis_index("x"), lax.axis_index("y"), lax.axis_index("c")


def _other_chips(x, y):
    return [(1 - x, y), (x, 1 - y), (1 - x, 1 - y)]


def _remote(src, dst, send_sems, recv_sems, k, dev):
    return pltpu.make_async_remote_copy(src_ref=src, dst_ref=dst, send_sem=send_sems.at[k], recv_sem=recv_sems.at[k],
                                        device_id=dev, device_id_type=MESH_ID)


def _gather_list_call(parts, tag):
    n = len(parts)

    def body(*refs):
        srcs, outs, (send_sems, recv_sems) = refs[:n], refs[n:2 * n], refs[2 * n:]
        x, y, c = _mesh_pos()
        sm = 2 * x + y
        chips = _other_chips(x, y)
        sib = (x, y, 1 - c)
        rc = lambda k, src, dst, dev: _remote(src, dst, send_sems, recv_sems, k, dev)
        first = [rc(7 * i + j, srcs[i].at[c], outs[i].at[sm, c], (cx, cy, c)) for i in range(n) for j, (cx, cy) in enumerate(chips)]
        own = [rc(7 * i + 6, srcs[i], outs[i].at[sm], sib) for i in range(n)]
        for cp in first + own:
            cp.start()
        passed = []
        for j, (cx, cy) in enumerate(chips):
            for i in range(n):
                land = outs[i].at[2 * cx + cy, c]
                rc(7 * i + j, srcs[i].at[c], land, (cx, cy, c)).wait_recv()
                cp = rc(7 * i + 3 + j, land, land, sib)
                cp.start()
                passed.append(cp)
        for j, (cx, cy) in enumerate(chips):
            for i in range(n):
                rc(7 * i + 3 + j, srcs[i].at[c], outs[i].at[2 * cx + cy, 1 - c], sib).wait_recv()
        for cp in own:
            cp.wait_recv()
        for cp in first + passed + own:
            cp.wait_send()

    return pl.pallas_call(
        body, name="weights_all_gather_" + tag,
        in_specs=[ANY] * n, out_specs=[ANY] * n,
        out_shape=[jax.ShapeDtypeStruct((4,) + p.shape, p.dtype) for p in parts],
        scratch_shapes=[pltpu.SemaphoreType.DMA((7 * n,)), pltpu.SemaphoreType.DMA((7 * n,))],
    )(*parts)


def _direct_gather_copies(srcs, lands, send_sems, recv_sems):
    x, y, c = _mesh_pos()
    sm = 2 * x + y
    sends, recvs = [], []
    for i, (src, land) in enumerate(zip(srcs, lands)):
        for j, (cx, cy) in enumerate(_other_chips(x, y)):
            for t in range(2):
                sends.append(_remote(src.at[c], land.at[sm, c], send_sems, recv_sems, 13 * i + 4 * j + 2 * c + t, (cx, cy, t)))
                recvs.append(_remote(src.at[t], land.at[2 * cx + cy, t], send_sems, recv_sems, 13 * i + 4 * j + 2 * t + c, (cx, cy, t)))
        sends.append(_remote(src, land.at[sm], send_sems, recv_sems, 13 * i + 12, (x, y, 1 - c)))
        recvs.append(_remote(src, land.at[sm], send_sems, recv_sems, 13 * i + 12, (x, y, 1 - c)))
    return sends, recvs


def _sibling_copies(srcs, lands, send_sems, recv_sems):
    x, y, c = _mesh_pos()
    cps = [_remote(src.at[s, 1 - c], land.at[s], send_sems, recv_sems, 4 * i + s, (x, y, 1 - c))
           for i, (src, land) in enumerate(zip(srcs, lands)) for s in range(4)]
    return cps, cps


def _chips_copies(srcs, lands, send_sems, recv_sems):
    x, y, c = _mesh_pos()
    cps = [_remote(src.at[2 * cx + cy], land.at[j], send_sems, recv_sems, 3 * i + j, (cx, cy, c))
           for i, (src, land) in enumerate(zip(srcs, lands)) for j, (cx, cy) in enumerate(_other_chips(x, y))]
    return cps, cps


def _share_copies(srcs, lands, send_sems, recv_sems):
    x, y, c = _mesh_pos()
    cps = [_remote(src, land, send_sems, recv_sems, i, (x, y, 1 - c)) for i, (src, land) in enumerate(zip(srcs, lands))]
    return cps, cps


def _exchange_call(name, copies, srcs, land_shapes, n_sems):
    n = len(srcs)

    def body(*refs):
        sends, recvs = copies(refs[:n], refs[n:2 * n], refs[2 * n], refs[2 * n + 1])
        for cp in sends:
            cp.start()
        for cp in sends:
            cp.wait_send()
        for cp in recvs:
            cp.wait_recv()

    return pl.pallas_call(
        body, name=name, in_specs=[ANY] * n, out_specs=[ANY] * n, out_shape=list(land_shapes),
        scratch_shapes=[pltpu.SemaphoreType.DMA((n_sems,)), pltpu.SemaphoreType.DMA((n_sems,))],
    )(*srcs)


def _exchange_start_call(name, copies, srcs, land_shapes, n_sems, order=None):
    n = len(srcs)
    extra = [] if order is None else [order]
    k = 2 * n + len(extra)

    def body(*refs):
        sends, _ = copies(refs[:n], refs[n:2 * n], refs[k], refs[k + 1])
        for cp in sends:
            cp.start()
        refs[-1][...] = jnp.zeros_like(refs[-1])

    hbm = lambda a: pltpu.with_memory_space_constraint(a, pltpu.HBM)
    lands = [hbm(lax.empty(sd.shape, sd.dtype)) for sd in land_shapes]
    sem = pltpu.SemaphoreType.DMA((n_sems,))
    out = pl.pallas_call(
        body, name=name,
        out_shape=(sem, sem, *[pltpu.HBM(a.shape, a.dtype) for a in list(srcs) + lands], jax.ShapeDtypeStruct((8, LANES), F32)),
        in_specs=[HBM_SPEC] * (2 * n) + [ANY] * len(extra), out_specs=(SEM_SPEC, SEM_SPEC, *[HBM_SPEC] * (2 * n), VMEM_SPEC),
        input_output_aliases={i: 2 + i for i in range(2 * n)},
        compiler_params=pltpu.CompilerParams(has_side_effects=pltpu.SideEffectType.DATAFLOW_SIDE_EFFECTING),
    )(*[hbm(a) for a in srcs], *lands, *extra)
    return out[0], out[1], out[2:2 + n], out[2 + n:2 + 2 * n], out[-1]


def _exchange_wait_call(name, copies, started, after):
    send_sems, recv_sems, srcs, lands, _ = started
    n = len(srcs)

    def body(*refs):
        sends, recvs = copies(refs[:n], refs[n:2 * n], refs[2 * n], refs[2 * n + 1])
        for cp in sends:
            cp.wait_send()
        for cp in recvs:
            cp.wait_recv()

    out = pl.pallas_call(
        body, name=name,
        out_shape=tuple(pltpu.HBM(a.shape, a.dtype) for a in list(srcs) + list(lands)),
        in_specs=[HBM_SPEC] * (2 * n) + [SEM_SPEC, SEM_SPEC, ANY], out_specs=tuple([HBM_SPEC] * (2 * n)),
        input_output_aliases={i: i for i in range(2 * n)},
        compiler_params=pltpu.CompilerParams(has_side_effects=pltpu.SideEffectType.DATAFLOW_SIDE_EFFECTING),
    )(*srcs, *lands, send_sems, recv_sems, after)
    return out[:n], out[n:]


def _rows_tile(rows, width, itemsize=4):
    limit = max(16, (3 << 20) // (width * itemsize))
    if rows <= limit:
        return rows
    return max(t for t in range(16, limit + 1, 16) if rows % t == 0)


def _sum_sibling_call(g, buf, c, name):
    _, _, rh, w = g.shape
    tile = _rows_tile(rh, w)

    def body(c_ref, g_ref, b_ref, p_ref, pb_ref):
        p = g_ref[...] + b_ref[...]
        p_ref[...] = p
        pb_ref[...] = p.astype(BF16)

    blk = pl.BlockSpec((None, tile, w), lambda s, i, c_ref: (s, i, 0))
    return pl.pallas_call(
        body, name=name,
        grid_spec=pltpu.PrefetchScalarGridSpec(
            num_scalar_prefetch=1, grid=(4, rh // tile),
            in_specs=[pl.BlockSpec((None, None, tile, w), lambda s, i, c_ref: (s, c_ref[0], i, 0)), blk],
            out_specs=[blk, blk]),
        out_shape=[jax.ShapeDtypeStruct((4, rh, w), F32), jax.ShapeDtypeStruct((4, rh, w), BF16)],
        compiler_params=_cp("parallel", "parallel"),
    )(c, g, buf)


def _sum_chips_call(p, buf, sm, name):
    _, rh, w = p.shape
    tile = _rows_tile(rh, w)

    def body(sm_ref, p_ref, b_ref, f_ref):
        f_ref[...] = ((p_ref[...] + b_ref[0].astype(F32)) + b_ref[1].astype(F32)) + b_ref[2].astype(F32)

    return pl.pallas_call(
        body, name=name,
        grid_spec=pltpu.PrefetchScalarGridSpec(
            num_scalar_prefetch=1, grid=(rh // tile,),
            in_specs=[pl.BlockSpec((None, tile, w), lambda i, sm_ref: (sm_ref[0], i, 0)),
                      pl.BlockSpec((3, tile, w), lambda i, sm_ref: (0, i, 0))],
            out_specs=pl.BlockSpec((tile, w), lambda i, sm_ref: (i, 0))),
        out_shape=jax.ShapeDtypeStruct((rh, w), F32),
        compiler_params=_cp("parallel"),
    )(sm, p, buf)


def _adamw_halves_call(w, g_mine, g_sib, c, m, v, name):
    r, wd = w.shape
    rh = r // 2
    tile = _rows_tile(rh, wd)
    nt = rh // tile

    def body(c_ref, w_ref, gm_ref, gs_ref, m_ref, v_ref, g_ref, d_ref, nm_ref, nv_ref):
        gv = jnp.where(pl.program_id(0) == c_ref[0], gm_ref[...], gs_ref[...])
        g_ref[...] = gv
        nm = ADAM_B1 * m_ref[...] + (1.0 - ADAM_B1) * gv
        nv = ADAM_B2 * v_ref[...] + (1.0 - ADAM_B2) * jnp.square(gv)
        m_hat = nm / (1.0 - ADAM_B1 ** ADAM_STEP)
        v_hat = nv / (1.0 - ADAM_B2 ** ADAM_STEP)
        d_ref[...] = -ADAM_LR * (m_hat / (jnp.sqrt(v_hat) + ADAM_EPS) + ADAM_WD * w_ref[...])
        nm_ref[...] = nm
        nv_ref[...] = nv

    whole = pl.BlockSpec((tile, wd), lambda h, i, c_ref: (h * nt + i, 0))
    half = pl.BlockSpec((tile, wd), lambda h, i, c_ref: (i, 0))
    sd = jax.ShapeDtypeStruct((r, wd), F32)
    return pl.pallas_call(
        body, name=name,
        grid_spec=pltpu.PrefetchScalarGridSpec(
            num_scalar_prefetch=1, grid=(2, nt),
            in_specs=[whole, half, half, whole, whole], out_specs=[whole] * 4),
        out_shape=[sd, sd, sd, sd],
        compiler_params=_cp("parallel", "parallel"),
    )(c, w, g_mine, g_sib, m, v)


def _exchange8_call(vec, reduce, name):
    rows = vec.shape[0]

    def body(v_ref, out_ref, *rest):
        slots, send_sems, recv_sems = (rest if reduce else (out_ref,) + rest)
        x, y, c = _mesh_pos()
        me = 4 * x + 2 * y + c
        slots[me] = v_ref[...]

        def rcopy(k, to_me):
            bx, by, bc = (k >> 2) & 1, (k >> 1) & 1, k & 1
            px, py, pc = (1 - x if bx else x), (1 - y if by else y), (1 - c if bc else c)
            slot = 4 * px + 2 * py + pc if to_me else me
            return pltpu.make_async_remote_copy(src_ref=v_ref, dst_ref=slots.at[slot], send_sem=send_sems.at[k - 1],
                                                recv_sem=recv_sems.at[k - 1], device_id=(px, py, pc), device_id_type=MESH_ID)

        for k in range(1, N_DEV):
            rcopy(k, False).start()
        for k in range(1, N_DEV):
            rcopy(k, True).wait_recv()
        for k in range(1, N_DEV):
            rcopy(k, False).wait_send()
        if reduce:
            tot = slots[0]
            for d in range(1, N_DEV):
                tot = tot + slots[d]
            out_ref[...] = tot

    stack = jax.ShapeDtypeStruct((N_DEV, rows, LANES), F32)
    return pl.pallas_call(
        body, name=name,
        in_specs=[VMEM_SPEC], out_specs=VMEM_SPEC,
        out_shape=jax.ShapeDtypeStruct((rows, LANES), F32) if reduce else stack,
        scratch_shapes=([pltpu.VMEM((N_DEV, rows, LANES), F32)] if reduce else [])
        + [pltpu.SemaphoreType.DMA((N_DEV - 1,)), pltpu.SemaphoreType.DMA((N_DEV - 1,))],
    )(vec)


def _adamw_call(w, g, m, v, name):
    r, c = w.shape
    rb = r if r <= 256 else (256 if r % 256 == 0 else 352)
    assert r % rb == 0

    def body(w_ref, g_ref, m_ref, v_ref, d_ref, nm_ref, nv_ref):
        gv = g_ref[...]
        nm = ADAM_B1 * m_ref[...] + (1.0 - ADAM_B1) * gv
        nv = ADAM_B2 * v_ref[...] + (1.0 - ADAM_B2) * jnp.square(gv)
        m_hat = nm / (1.0 - ADAM_B1 ** ADAM_STEP)
        v_hat = nv / (1.0 - ADAM_B2 ** ADAM_STEP)
        d_ref[...] = -ADAM_LR * (m_hat / (jnp.sqrt(v_hat) + ADAM_EPS) + ADAM_WD * w_ref[...])
        nm_ref[...] = nm
        nv_ref[...] = nv

    spec = pl.BlockSpec((rb, c), lambda i: (i, 0))
    sd = jax.ShapeDtypeStruct((r, c), F32)
    return pl.pallas_call(
        body, name=name, grid=(r // rb,),
        in_specs=[spec] * 4, out_specs=[spec] * 3, out_shape=[sd, sd, sd],
        compiler_params=_cp("parallel"),
    )(w, g, m, v)


SMALL = (("attn_norm_w", D_MODEL), ("ret_gn_w", RET_W), ("mla_q_norm_w", Q_RANK), ("mla_kv_norm_w", KV_RANK),
         ("ffn_norm_w", D_MODEL), ("conv_b", F2), ("final_norm_w", D_MODEL))
WEIGHT_ORDER = ("attn_norm_w", "w_in", "ret_gn_w", "mla_q_norm_w", "w_uq", "mla_kv_norm_w", "w_ukv", "w_out",
                "ffn_norm_w", "w_up", "conv_w", "conv_b", "w_down", "final_norm_w")


def _pad_rows(flat, rows):
    return jnp.concatenate([flat, jnp.zeros((rows * LANES - flat.shape[0],), flat.dtype)]).reshape(rows, LANES)


def kernel(x, positions, attn_norm_w, w_in, ret_gn_w, mla_q_norm_w, w_uq, mla_kv_norm_w, w_ukv, w_out, ffn_norm_w, w_up, conv_w, conv_b, w_down, final_norm_w, loss_target, m_attn_norm_w, m_w_in, m_ret_gn_w, m_mla_q_norm_w, m_w_uq, m_mla_kv_norm_w, m_w_ukv, m_w_out, m_ffn_norm_w, m_w_up, m_conv_w, m_conv_b, m_w_down, m_final_norm_w, v_attn_norm_w, v_w_in, v_ret_gn_w, v_mla_q_norm_w, v_w_uq, v_mla_kv_norm_w, v_w_ukv, v_w_out, v_ffn_norm_w, v_w_up, v_conv_w, v_conv_b, v_w_down, v_final_norm_w):
    args = dict(locals())
    cx, cy, cc = _mesh_pos()
    sm = 2 * cx + cy

    c_arr, sm_arr = cc.reshape(1).astype(jnp.int32), sm.reshape(1).astype(jnp.int32)
    sds = jax.ShapeDtypeStruct

    def my_shards(group):
        return [args[n][0].astype(BF16).reshape(2, r // 2, c) for n, (r, c), _ in group]

    def full_weights(gathered, group):
        full = {}
        for (n, (r, c), axis), got in zip(group, gathered):
            piece = got.reshape(4, r, c)
            full[n] = piece if n == "w_up" else (piece.transpose(1, 0, 2).reshape(r, 4 * c) if axis == 1 else piece.reshape(4 * r, c))
        return full

    def by_owner(gw, group):
        out = []
        for n, (r, c), axis in group:
            g = gw[n]
            if axis == 1 and g.ndim == 2:
                g = g.reshape(r, 4, c).transpose(1, 0, 2)
            out.append(g.reshape(4, 2, r // 2, c))
        return out

    def sibling_shapes(gs):
        return [sds((4,) + g.shape[2:], F32) for g in gs]

    def chip_sums(gs, bufs, group):
        res = [_sum_sibling_call(g, b, c_arr, "grads_sum_sibling_" + n) for g, b, (n, _, _) in zip(gs, bufs, group)]
        return [p for p, _ in res], [pb for _, pb in res]

    def chips_shapes(pbs):
        return [sds((3,) + pb.shape[1:], BF16) for pb in pbs]

    def totals(ps, lands, group, tag):
        fins = [_sum_chips_call(p, l, sm_arr, "grads_sum_chips_" + n) for p, l, (n, _, _) in zip(ps, lands, group)]
        sibs = _exchange_call("grads_rs_share_" + tag, _share_copies, fins, [sds(f.shape, F32) for f in fins], len(fins))
        return {n: (f, s) for (n, _, _), f, s in zip(group, fins, sibs)}

    class StepExchanges(_Exchanges):
        def __init__(self, order):
            shards = my_shards(GROUP_B)
            self.gather = _exchange_start_call("weights_gather_start_b", _direct_gather_copies, shards,
                                               [sds((4,) + s.shape, BF16) for s in shards], 13 * len(shards), order)
            self.red = None

        def token(self):
            return self.gather[4][0:1, 0:1]

        def mlp_weights(self, after):
            return full_weights(_exchange_wait_call("weights_gather_wait_b", _direct_gather_copies, self.gather, after)[1], GROUP_B)

        def mlp_grads(self, gw):
            gs = by_owner(gw, GROUP_B)
            self.step1 = _exchange_start_call("grads_rs_sibling_start_b", _sibling_copies, gs, sibling_shapes(gs), 4 * len(gs))
            return self.step1[4]

        def behind_out_bwd(self, after):
            gs, bufs = _exchange_wait_call("grads_rs_sibling_wait_b", _sibling_copies, self.step1, after)
            self.ps, pbs = chip_sums(gs, bufs, GROUP_B)
            self.step2 = _exchange_start_call("grads_rs_chips_start_b", _chips_copies, pbs, chips_shapes(pbs), 3 * len(pbs))
            return self.step2[4]

        def behind_attention(self, after):
            _, lands = _exchange_wait_call("grads_rs_chips_wait_b", _chips_copies, self.step2, after)
            self.red = totals(self.ps, lands, GROUP_B, "b")

    gathered = _gather_list_call(my_shards(GROUP_A) + [conv_w[0].reshape(2, 1, 3 * F2 // 8)], "a")
    full = full_weights(gathered[:-1], GROUP_A)
    ex = StepExchanges(gathered[-1])
    full["conv_w"] = gathered[-1].reshape(4, 3, F2 // 4).transpose(1, 0, 2).reshape(3, F2)
    small = {n: args[n].reshape(1, d) for n, d in SMALL}
    small["attn_norm_w"] = small["attn_norm_w"] + ex.token()

    loss, gx, gw, gs = _local_step(x[0], positions[0], loss_target[0], full, small, ex)

    ga = by_owner(gw, GROUP_A)
    bufs = _exchange_call("grads_rs_sibling_a", _sibling_copies, ga, sibling_shapes(ga), 4 * len(ga))
    ps, pbs = chip_sums(ga, bufs, GROUP_A)
    lands = _exchange_call("grads_rs_chips_a", _chips_copies, pbs, chips_shapes(pbs), 3 * len(pbs))
    halves = {**ex.red, **totals(ps, lands, GROUP_A, "a")}

    vec = jnp.concatenate([gs[n].reshape(-1) for n, _ in SMALL] + [gw["conv_w"].reshape(-1), loss.reshape(-1)])
    tot = _exchange8_call(_pad_rows(vec, 216), True, "small_all_reduce").reshape(-1)
    red, off = {}, 0
    for n, d in SMALL:
        red[n] = tot[off:off + d].reshape(1, d)
        off += d
    red["conv_w"] = lax.dynamic_slice(tot[off:off + 3 * F2].reshape(3, F2), (0, sm * (F2 // 4)), (3, F2 // 4))
    loss_tot = tot[off + 3 * F2]

    grads, deltas, new_m, new_v = [], [], [], []
    for n in WEIGHT_ORDER:
        shape = args[n].shape
        two_d = (1, shape[0]) if len(shape) == 1 else shape[-2:]
        wmv = [args[k + n].reshape(two_d) for k in ("", "m_", "v_")]
        if n in halves:
            g, d, nm, nv = _adamw_halves_call(wmv[0], *halves[n], c_arr, wmv[1], wmv[2], "adamw_" + n)
        else:
            g = red[n].reshape(two_d)
            d, nm, nv = _adamw_call(wmv[0], g, wmv[1], wmv[2], "adamw_" + n)
        grads.append(g.reshape(shape))
        deltas.append(d.reshape(shape))
        new_m.append(nm.reshape(shape))
        new_v.append(nv.reshape(shape))
    return (loss_tot, gx[None], *grads, *deltas, *new_m, *new_v)
```

```python
import math

import numpy as np
import jax
import jax.numpy as jnp
from jax import lax
from jax.experimental import pallas as pl
from jax.experimental.pallas import tpu as pltpu

F32 = jnp.float32
BF16 = jnp.bfloat16

D_MODEL = 1024
N_HEADS = 8
HEAD = 64
RET_W = N_HEADS * HEAD
MLA_W = N_HEADS * HEAD
ROPE = 32
Q_RANK = 256
KV_RANK = 128
D_FF = 2816
F2 = 2 * D_FF
IN_W = 4 * RET_W + Q_RANK + KV_RANK + ROPE
IN_EXT = 4 * RET_W + Q_RANK + KV_RANK + 128
KPE_LO = 64
ROPE_BASE = 10000.0
EPS = 1e-6
RET_CHUNK = 256
SM_SCALE = (HEAD + ROPE) ** -0.5
LOG2E = math.log2(math.e)
LN2 = math.log(2.0)
NEG = -1e30
LANES = 128
VMEM_LIMIT = 56 * 1024 * 1024

ADAM_LR = 0.001
ADAM_B1 = 0.9
ADAM_B2 = 0.999
ADAM_EPS = 1e-08
ADAM_WD = 0.01
ADAM_STEP = 10


VMEM_LIMIT_MLP_BWD = 60 * 1024 * 1024


def _cp(*sem, vmem=VMEM_LIMIT):
    return pltpu.CompilerParams(dimension_semantics=sem, vmem_limit_bytes=vmem)


def _full(shape):
    n = len(shape)
    return pl.BlockSpec(tuple(shape), lambda *_: (0,) * n)


def _row(ts, c):
    return pl.BlockSpec((ts, c), lambda i: (i, 0))


def _hrow(h, ts, c):
    return pl.BlockSpec((h, ts, c), lambda i: (0, i, 0))


def _dot(a, b):
    return jnp.dot(a, b, preferred_element_type=F32)


def _dot_nt(a, b):
    return lax.dot_general(a, b, (((1,), (1,)), ((), ())), preferred_element_type=F32)


def _dot_tn(a, b):
    return lax.dot_general(a, b, (((0,), (0,)), ((), ())), preferred_element_type=F32)


def _dot_hi(a, b):
    hi = a.astype(BF16)
    lo = (a - hi.astype(F32)).astype(BF16)
    bb = b.astype(BF16)
    return _dot(hi, bb) + _dot(lo, bb)


def _rot_half(x, half):
    w = x.shape[-1]
    lane = lax.broadcasted_iota(jnp.int32, x.shape, x.ndim - 1)
    first = (lane % (2 * half)) < half
    return jnp.where(first, -pltpu.roll(x, w - half, x.ndim - 1), pltpu.roll(x, half, x.ndim - 1))


def _rope(x, cos, sin, half):
    return x * cos + _rot_half(x, half) * sin


def _unrope(dy, cos, sin, half):
    return dy * cos - _rot_half(dy, half) * sin


def _sigmoid(g):
    return 0.5 * jnp.tanh(0.5 * g) + 0.5


def _silu(g):
    return g * _sigmoid(g)


def _rstd(x):
    return lax.rsqrt(jnp.mean(x * x, axis=-1, keepdims=True) + EPS)


def _rope_tables(positions):
    pos = positions.astype(F32)[:, None]
    s = pos.shape[0]
    inv = ROPE_BASE ** (-jnp.arange(0, HEAD, 2, dtype=F32) / HEAD)
    ang = pos * inv
    c, sn = jnp.cos(ang), jnp.sin(ang)
    cos_r = jnp.tile(jnp.concatenate([c, c], -1), (1, 2))
    sin_r = jnp.tile(jnp.concatenate([sn, sn], -1), (1, 2))
    inv = ROPE_BASE ** (-jnp.arange(0, ROPE, 2, dtype=F32) / ROPE)
    ang = pos * inv
    c, sn = jnp.cos(ang), jnp.sin(ang)
    one, zero = jnp.ones((s, KPE_LO), F32), jnp.zeros((s, KPE_LO), F32)
    cos_m = jnp.concatenate([one, c, c, one[:, :LANES - KPE_LO - ROPE]], -1)
    sin_m = jnp.concatenate([zero, sn, sn, zero[:, :LANES - KPE_LO - ROPE]], -1)
    return cos_r, sin_r, cos_m, sin_m


def _ret_consts():
    c = RET_CHUNK
    lg = np.log1p(-np.power(2.0, -5.0 - np.arange(N_HEADS, dtype=np.float64)))
    idx = np.arange(c, dtype=np.float64)
    diff = idx[:, None] - idx[None, :]
    lane_head = np.arange(LANES) // HEAD
    dmask = np.zeros((4, 2, c, c))
    zeta = np.zeros((4, c, LANES))
    xi = np.zeros((4, c, LANES))
    cd = np.zeros((4, LANES, LANES))
    bd = (lane_head[:, None] == lane_head[None, :]).astype(np.float64)
    for j in range(4):
        for hh in range(2):
            dmask[j, hh] = np.where(diff >= 0, np.exp(lg[2 * j + hh] * np.maximum(diff, 0.0)), 0.0)
        lgl = lg[2 * j + lane_head]
        zeta[j] = np.exp(lgl[None, :] * (c - 1.0 - idx[:, None]))
        xi[j] = np.exp(lgl[None, :] * (idx[:, None] + 1.0))
        cd[j] = np.exp(lgl * c)[:, None] * bd
    f = lambda a: jnp.asarray(a, F32)
    side = lambda d: np.concatenate([d[:, 0], d[:, 1]], axis=-1)
    return dict(dmask=f(side(dmask)), dmask_t=f(side(np.swapaxes(dmask, 2, 3))), zeta=f(zeta), xi=f(xi), cd=f(cd), bd=f(bd))


def _f1_call(x, anw, win, cos_r, sin_r, cos_m, sin_m, ts):
    s = x.shape[0]

    def body(x_ref, anw_ref, w_ref, cr_ref, sr_ref, cm_ref, sm_ref,
             q_ref, k_ref, v_ref, g_ref, cq_ref, ckv_ref, kpe_ref, r_ref):
        xv = x_ref[...]
        r = _rstd(xv)
        r_ref[...] = r
        h = (xv * r * anw_ref[...]).astype(BF16)
        cr, sr = cr_ref[...], sr_ref[...]
        qk = _dot(h, w_ref[:, 0:2 * RET_W])
        for j in range(4):
            sl = slice(j * LANES, (j + 1) * LANES)
            q_ref[:, sl] = _rope(qk[:, sl], cr, sr, HEAD // 2).astype(BF16)
            kk = qk[:, RET_W + j * LANES:RET_W + (j + 1) * LANES]
            k_ref[:, sl] = (_rope(kk, cr, sr, HEAD // 2) * (HEAD ** -0.5)).astype(BF16)
        v_ref[...] = _dot(h, w_ref[:, 2 * RET_W:3 * RET_W]).astype(BF16)
        g_ref[...] = _dot(h, w_ref[:, 3 * RET_W:4 * RET_W])
        o = 4 * RET_W
        cq_ref[...] = _dot(h, w_ref[:, o:o + Q_RANK])
        ckv_ref[...] = _dot(h, w_ref[:, o + Q_RANK:o + Q_RANK + KV_RANK])
        kp = _dot(h, w_ref[:, o + Q_RANK + KV_RANK:IN_EXT])
        kpe_ref[...] = _rope(kp, cm_ref[...], sm_ref[...], ROPE // 2)

    sd = jax.ShapeDtypeStruct
    return pl.pallas_call(
        body, name="f1_in_proj", grid=(s // ts,),
        in_specs=[_row(ts, D_MODEL), _full((1, D_MODEL)), _full((D_MODEL, IN_EXT)),
                  _row(ts, LANES), _row(ts, LANES), _row(ts, LANES), _row(ts, LANES)],
        out_specs=[_row(ts, RET_W), _row(ts, RET_W), _row(ts, RET_W), _row(ts, RET_W),
                   _row(ts, Q_RANK), _row(ts, KV_RANK), _row(ts, LANES), _row(ts, 1)],
        out_shape=[sd((s, RET_W), BF16), sd((s, RET_W), BF16), sd((s, RET_W), BF16), sd((s, RET_W), F32),
                   sd((s, Q_RANK), F32), sd((s, KV_RANK), F32), sd((s, LANES), F32), sd((s, 1), F32)],
        compiler_params=_cp("parallel"),
    )(x, anw, win, cos_r, sin_r, cos_m, sin_m)


def _stack_heads(a):
    lo = lax.broadcasted_iota(jnp.int32, a.shape, 1) < HEAD
    zero = jnp.zeros_like(a)
    return jnp.concatenate([jnp.where(lo, a, zero), jnp.where(lo, zero, a)], axis=0)


def _pair_product(a, b2, decay2, w2):
    return _dot((_dot_nt(a, b2) * decay2).astype(BF16), w2)


RET_SLABS = 2


def _ret_specs(tr, tile_of):
    c, ns = RET_CHUNK, RET_SLABS
    return dict(
        slab=pl.BlockSpec((tr, ns * LANES), lambda j, i: (tile_of(i), j)),
        tab=pl.BlockSpec((tr, LANES), lambda j, i: (tile_of(i), 0)),
        vec=pl.BlockSpec((1, ns * LANES), lambda j, i: (0, j)),
        dmask=pl.BlockSpec((ns, c, 2 * c), lambda j, i: (j, 0, 0)),
        rows=pl.BlockSpec((ns, c, LANES), lambda j, i: (j, 0, 0)),
        state=pl.BlockSpec((ns, LANES, LANES), lambda j, i: (j, 0, 0)),
        bd=pl.BlockSpec((LANES, LANES), lambda j, i: (0, 0)))


def _ret_states(a_ref, b_ref, scale_ref, cd_ref, bd, st_ref, chunks, lanes, reverse):
    nc = len(chunks)
    contrib = [[_dot_tn((a_ref[rows, ln].astype(F32) * scale_ref[sl]).astype(BF16), b_ref[rows, ln]) * bd for rows in chunks]
               for sl, ln in enumerate(lanes)]
    states = []
    for sl in range(len(lanes)):
        st, seen = st_ref[sl], [None] * nc
        for ci in (reversed(range(nc)) if reverse else range(nc)):
            seen[ci] = st.astype(BF16)
            st = st * cd_ref[sl] + contrib[sl][ci]
        st_ref[sl] = st
        states.append(seen)
    return states


def _ret_fwd_call(q, k, v, g, gnw, rc, tr):
    s = q.shape[0]
    c = RET_CHUNK
    nc = tr // c
    ns = RET_SLABS

    def body(q_ref, k_ref, v_ref, g_ref, gnw_ref, dm_ref, zeta_ref, xi_ref, cd_ref, bd_ref, o_ref, y_ref, st_ref):
        @pl.when(pl.program_id(1) == 0)
        def _():
            st_ref[...] = jnp.zeros_like(st_ref)

        bd = bd_ref[...]
        chunks = [slice(ci * c, (ci + 1) * c) for ci in range(nc)]
        lanes = [slice(sl * LANES, (sl + 1) * LANES) for sl in range(ns)]
        states = _ret_states(k_ref, v_ref, zeta_ref, cd_ref, bd, st_ref, chunks, lanes, False)
        for ci, rows in enumerate(chunks):
            for sl, ln in enumerate(lanes):
                qc = q_ref[rows, ln]
                o_ref[rows, ln] = (_dot(qc, states[sl][ci]) * xi_ref[sl]
                                   + _pair_product(qc, _stack_heads(k_ref[rows, ln]), dm_ref[sl], _stack_heads(v_ref[rows, ln])))
        avg = bd * (1.0 / HEAD)
        for ln in lanes:
            o = o_ref[:, ln]
            ctr = o - _dot_hi(o, avg)
            var = _dot_hi(ctr * ctr, avg)
            y_ref[:, ln] = (_silu(g_ref[:, ln]) * (ctr * lax.rsqrt(var + EPS) * gnw_ref[:, ln])).astype(BF16)

    specs = _ret_specs(tr, lambda i: i)
    sd = jax.ShapeDtypeStruct
    return pl.pallas_call(
        body, name="ret_fwd", grid=(4 // ns, s // tr),
        in_specs=[specs["slab"]] * 4 + [specs["vec"], specs["dmask"], specs["rows"], specs["rows"], specs["state"], specs["bd"]],
        out_specs=[specs["slab"]] * 2,
        out_shape=[sd((s, RET_W), F32), sd((s, RET_W), BF16)],
        scratch_shapes=[pltpu.VMEM((ns, LANES, LANES), F32)],
        compiler_params=_cp("parallel", "arbitrary"),
    )(q, k, v, g, gnw, rc["dmask"], rc["zeta"], rc["xi"], rc["cd"], rc["bd"])


QK_AUX = HEAD + ROPE
V_AUX = HEAD


def _lane_pair(shape, lo, a, b, rest):
    lane = lax.broadcasted_iota(jnp.int32, shape, len(shape) - 1)
    return jnp.where(lane == lo, a, jnp.where(lane == lo + 1, b, rest))


def _hi_lo(v):
    hi = v.astype(BF16).astype(F32)
    return hi, v - hi


def _mla_pre_call(cq, ckv, kpe, qnw, kvnw, wq, wk, wv, cos_m, sin_m, ts):
    s = cq.shape[0]

    def body(cq_ref, ckv_ref, kpe_ref, qnw_ref, kvnw_ref, wq_ref, wk_ref, wv_ref, cm_ref, sm_ref, q_ref, k_ref, v_ref):
        cqv, ckvv = cq_ref[...], ckv_ref[...]
        cqn = (cqv * _rstd(cqv) * qnw_ref[...]).astype(BF16)
        ckvn = (ckvv * _rstd(ckvv) * kvnw_ref[...]).astype(BF16)
        cm, sm = cm_ref[...], sm_ref[...]
        kp = _lane_pair((ts, LANES), QK_AUX, -1.0, -1.0, kpe_ref[...])
        for h in range(N_HEADS):
            qh = _rope(_dot(cqn, wq_ref[h]), cm, sm, ROPE // 2)
            q_ref[h] = (qh * (SM_SCALE * LOG2E)).astype(BF16)
            k_ref[h] = (_dot(ckvn, wk_ref[h]) + kp).astype(BF16)
            v_ref[h] = _lane_pair((ts, LANES), V_AUX, 1.0, 1.0, _dot(ckvn, wv_ref[h])).astype(BF16)

    sd = jax.ShapeDtypeStruct
    hm = sd((N_HEADS, s, LANES), BF16)
    return pl.pallas_call(
        body, name="mla_pre", grid=(s // ts,),
        in_specs=[_row(ts, Q_RANK), _row(ts, KV_RANK), _row(ts, LANES), _full((1, Q_RANK)), _full((1, KV_RANK)),
                  _full((N_HEADS, Q_RANK, LANES)), _full((N_HEADS, KV_RANK, LANES)), _full((N_HEADS, KV_RANK, LANES)),
                  _row(ts, LANES), _row(ts, LANES)],
        out_specs=[_hrow(N_HEADS, ts, LANES)] * 3,
        out_shape=[hm, hm, hm],
        compiler_params=_cp("parallel"),
    )(cq, ckv, kpe, qnw, kvnw, wq, wk, wv, cos_m, sin_m)


def _flash_fwd_call(q, k, v, tb):
    s = q.shape[1]
    nb = s // tb
    pairs = [(a, b) for a in range(nb) for b in range(a + 1)]
    qi_of, ki_of = (jnp.asarray(np.array(col, np.int32)) for col in zip(*pairs))

    def body(qi_ref, ki_ref, q_ref, k_ref, v_ref, o_ref, qb_ref, m_ref, acc_ref):
        qi, ki = qi_ref[pl.program_id(0)], ki_ref[pl.program_id(0)]

        @pl.when(ki == 0)
        def _():
            m_ref[...] = jnp.full_like(m_ref, NEG)
            acc_ref[...] = jnp.zeros_like(acc_ref)

        def step(masked):
            if masked:
                keep = lax.broadcasted_iota(jnp.int32, (tb, tb), 1) <= lax.broadcasted_iota(jnp.int32, (tb, tb), 0)
            def finish(h, pe, alpha):
                acc_ref[h] = acc_ref[h] * alpha + _dot(pe, v_ref[h])

            nxt, pending = _dot_nt(q_ref[0], k_ref[0]), None
            for h in range(N_HEADS):
                sc = nxt
                if h + 1 < N_HEADS:
                    nxt = _dot_nt(q_ref[h + 1], k_ref[h + 1])
                if masked:
                    sc = jnp.where(keep, sc, NEG)
                m_prev = m_ref[h]
                m_new = jnp.maximum(m_prev, jnp.max(sc, axis=1, keepdims=True))
                pe = jnp.exp2(sc - jnp.tile(m_new, (1, tb // LANES))).astype(BF16)
                m_ref[h] = m_new
                if pending is not None:
                    finish(*pending)
                pending = (h, pe, jnp.exp2(m_prev - m_new))
            finish(*pending)

        @pl.when(ki < qi)
        def _():
            step(False)

        @pl.when(ki == qi)
        def _():
            step(True)
            lane = lax.broadcasted_iota(jnp.int32, (tb, LANES), 1)
            for p in range(N_HEADS // 2):
                outs = []
                for h in (2 * p, 2 * p + 1):
                    acc = acc_ref[h]
                    l = acc[:, V_AUX:V_AUX + 1]
                    outs.append(acc * (1.0 / l))
                    hi, lo = _hi_lo(m_ref[h][:, 0:1] + jnp.log(l) * LOG2E)
                    qb_ref[h] = _lane_pair((tb, LANES), QK_AUX, hi, lo, q_ref[h].astype(F32)).astype(BF16)
                o_ref[:, p * LANES:(p + 1) * LANES] = jnp.where(lane < HEAD, outs[0], pltpu.roll(outs[1], HEAD, 1)).astype(BF16)

    sd = jax.ShapeDtypeStruct
    qspec = pl.BlockSpec((N_HEADS, tb, LANES), lambda p, qi_ref, ki_ref: (0, qi_ref[p], 0))
    kspec = pl.BlockSpec((N_HEADS, tb, LANES), lambda p, qi_ref, ki_ref: (0, ki_ref[p], 0))
    return pl.pallas_call(
        body, name="mla_flash_fwd",
        grid_spec=pltpu.PrefetchScalarGridSpec(
            num_scalar_prefetch=2, grid=(len(pairs),),
            in_specs=[qspec, kspec, kspec],
            out_specs=[pl.BlockSpec((tb, MLA_W), lambda p, qi_ref, ki_ref: (qi_ref[p], 0)), qspec],
            scratch_shapes=[pltpu.VMEM((N_HEADS, tb, LANES), F32), pltpu.VMEM((N_HEADS, tb, LANES), F32)]),
        out_shape=[sd((s, MLA_W), BF16), sd((N_HEADS, s, LANES), BF16)],
        compiler_params=_cp("arbitrary"),
    )(qi_of, ki_of, q, k, v)


def _out_proj_call(x, yret, ymla, wout, ts):
    s = x.shape[0]

    def body(x_ref, yr_ref, ym_ref, w_ref, x1_ref, r_ref):
        x1 = x_ref[...] + _dot(yr_ref[...], w_ref[0:RET_W, :]) + _dot(ym_ref[...], w_ref[RET_W:, :])
        x1_ref[...] = x1
        r_ref[...] = _rstd(x1)

    sd = jax.ShapeDtypeStruct
    return pl.pallas_call(
        body, name="out_proj", grid=(s // ts,),
        in_specs=[_row(ts, D_MODEL), _row(ts, RET_W), _row(ts, MLA_W), _full((D_MODEL, D_MODEL))],
        out_specs=[_row(ts, D_MODEL), _row(ts, 1)],
        out_shape=[sd((s, D_MODEL), F32), sd((s, 1), F32)],
        compiler_params=_cp("parallel"),
    )(x, yret, ymla, wout)


W_UP_SHARD = F2 // 4


def _ffn_fwd_call(x1, r2, fnw, wup4, cw, cb, wdown, ts):
    s = x1.shape[0]
    wsh = W_UP_SHARD

    def body(x_ref, r_ref, fnw_ref, wup_ref, cw_ref, cb_ref, wd_ref, u_ref, uc_ref, x2_ref, carry_ref):
        _zero_first(pl.program_id(0) == 0, carry_ref)
        xv = x_ref[...]
        h = (xv * r_ref[...] * fnw_ref[...]).astype(BF16)
        conv = []
        for j in range(4):
            cols = slice(j * wsh, (j + 1) * wsh)
            ub = _dot(h, wup_ref[j]).astype(BF16)
            u_ref[:, cols] = ub
            u = ub.astype(F32)
            u1, u2 = _shifted(u, carry_ref[:, cols])
            w = cw_ref[:, cols]
            cb16 = (cb_ref[:, cols] + w[0:1, :] * u2 + w[1:2, :] * u1 + w[2:3, :] * u).astype(BF16)
            uc_ref[:, cols] = cb16
            conv.append(cb16.astype(F32))
            carry_ref[:, cols] = u[ts - 8:, :]
        acc = xv
        for j in range(2):
            a = (_silu(conv[j]) * conv[j + 2]).astype(BF16)
            acc = acc + _dot(a, wd_ref[j * wsh:(j + 1) * wsh, :])
        x2_ref[...] = acc

    sd = jax.ShapeDtypeStruct
    return pl.pallas_call(
        body, name="ffn_fwd", grid=(s // ts,),
        in_specs=[_row(ts, D_MODEL), _row(ts, 1), _full((1, D_MODEL)), _full((4, D_MODEL, wsh)),
                  _full((3, F2)), _full((1, F2)), _full((D_FF, D_MODEL))],
        out_specs=[_row(ts, F2), _row(ts, F2), _row(ts, D_MODEL)],
        out_shape=[sd((s, F2), BF16), sd((s, F2), BF16), sd((s, D_MODEL), F32)],
        scratch_shapes=[pltpu.VMEM((8, F2), F32)],
        compiler_params=_cp("arbitrary"),
    )(x1, r2, fnw, wup4, cw, cb, wdown)


def _shifted(u, hal):
    row = lax.broadcasted_iota(jnp.int32, hal.shape, 0)
    r1, r2 = pltpu.roll(u, 1, 0), pltpu.roll(u, 2, 0)
    top1 = jnp.where(row == 0, hal[7:8, :], r1[0:8, :])
    top2 = jnp.where(row == 0, hal[6:7, :], jnp.where(row == 1, hal[7:8, :], r2[0:8, :]))
    return jnp.concatenate([top1, r1[8:, :]], axis=0), jnp.concatenate([top2, r2[8:, :]], axis=0)


def _prep_weights(w):
    win = w["w_in"]
    pad = lambda n: jnp.zeros((D_MODEL, n), win.dtype)
    win_ext = jnp.concatenate([win[:, :IN_W - ROPE], pad(KPE_LO), win[:, IN_W - ROPE:], pad(LANES - KPE_LO - ROPE)], -1)
    wuq = w["w_uq"].reshape(Q_RANK, N_HEADS, HEAD + ROPE)
    wq = jnp.concatenate([wuq, jnp.zeros((Q_RANK, N_HEADS, LANES - HEAD - ROPE), wuq.dtype)], -1).transpose(1, 0, 2)
    wukv = w["w_ukv"].reshape(KV_RANK, N_HEADS, 2 * HEAD)
    zk = jnp.zeros((KV_RANK, N_HEADS, HEAD), wukv.dtype)
    wk = jnp.concatenate([wukv[:, :, :HEAD], zk], -1).transpose(1, 0, 2)
    wv = jnp.concatenate([wukv[:, :, HEAD:], zk], -1).transpose(1, 0, 2)
    c = lambda a: a.astype(BF16)
    return dict(win=c(win_ext), wq=c(wq), wk=c(wk), wv=c(wv), wout=c(w["w_out"]))


def _prep_mlp_weights(w):
    wup = w["w_up"]
    if wup.ndim == 2:
        wup = wup.reshape(D_MODEL, 4, W_UP_SHARD).transpose(1, 0, 2)
    return dict(wup=wup.astype(BF16), wdown=w["w_down"].astype(BF16))


def _tiles(s):
    return dict(ts=min(s, 512), tr=min(s, 1024), tbf=min(s, 1024), tb=min(s, 512), t2=min(s, 256))


class _Exchanges:
    def __init__(self, w):
        self.w = w

    def mlp_weights(self, after):
        return self.w

    def mlp_grads(self, gw):
        pass

    def behind_out_bwd(self, after):
        pass

    def behind_attention(self, after):
        pass


def _forward(x, positions, w, small, ex):
    s = x.shape[0]
    t = _tiles(s)
    pw = _prep_weights(w)
    cos_r, sin_r, cos_m, sin_m = _rope_tables(positions)
    rc = _ret_consts()
    q, k, v, g, cq, ckv, kpe, r1 = _f1_call(x, small["attn_norm_w"], pw["win"], cos_r, sin_r, cos_m, sin_m, t["ts"])
    o_ret, y_ret = _ret_fwd_call(q, k, v, g, small["ret_gn_w"], rc, t["tr"])
    mq, mk, mv = _mla_pre_call(cq, ckv, kpe, small["mla_q_norm_w"], small["mla_kv_norm_w"],
                               pw["wq"], pw["wk"], pw["wv"], cos_m, sin_m, t["ts"])
    y_mla, mqb = _flash_fwd_call(mq, mk, mv, t["tbf"])
    x1, r2 = _out_proj_call(x, y_ret, y_mla, pw["wout"], t["ts"])
    pw.update(_prep_mlp_weights(ex.mlp_weights(r2)))
    u, uc, x2 = _ffn_fwd_call(x1, r2, small["ffn_norm_w"], pw["wup"], w["conv_w"], small["conv_b"], pw["wdown"], t["ts"])
    return dict(pw=pw, tabs=(cos_r, sin_r, cos_m, sin_m), rc=rc, q=q, k=k, v=v, g=g, cq=cq, ckv=ckv, kpe=kpe, r1=r1,
                o_ret=o_ret, y_ret=y_ret, mqb=mqb, mk=mk, mv=mv, y_mla=y_mla, x1=x1, r2=r2, u=u, uc=uc, x2=x2)


def _norm_bwd(dh, xh, r, nw):
    dxn = dh * nw
    return r * (dxn - xh * jnp.mean(dxn * xh, axis=-1, keepdims=True))


def _ordered_after(body, order):
    if order is None:
        return body, [], []
    return (lambda order_ref, *refs: body(*refs)), [pl.BlockSpec(memory_space=pl.ANY)], [order]


def _zero_first(first, *refs):
    @pl.when(first)
    def _():
        for ref in refs:
            ref[...] = jnp.zeros_like(ref)


def _colsum(v):
    return jnp.sum(v, axis=0, keepdims=True)


def _dsilu(g, sg):
    return sg * (1.0 + g * (1.0 - sg))


def _loss_call(x2, tgt, fw, ts):
    s = x2.shape[0]

    def body(x_ref, t_ref, fw_ref, dx_ref, loss_ref, gfw_ref):
        _zero_first(pl.program_id(0) == 0, loss_ref, gfw_ref)
        xv = x_ref[...]
        r = _rstd(xv)
        xh = xv * r
        fwv = fw_ref[...]
        e = xh * fwv - t_ref[...]
        loss_ref[...] += (0.5 / D_MODEL) * _colsum(jnp.sum(e * e, axis=1, keepdims=True))
        dy = e * (1.0 / D_MODEL)
        gfw_ref[...] += _colsum(dy * xh)
        dx_ref[...] = _norm_bwd(dy, xh, r, fwv)

    sd = jax.ShapeDtypeStruct
    return pl.pallas_call(
        body, name="loss_bwd", grid=(s // ts,),
        in_specs=[_row(ts, D_MODEL), _row(ts, D_MODEL), _full((1, D_MODEL))],
        out_specs=[_row(ts, D_MODEL), _full((1, 1)), _full((1, D_MODEL))],
        out_shape=[sd((s, D_MODEL), F32), sd((1, 1), F32), sd((1, D_MODEL), F32)],
        compiler_params=_cp("arbitrary"),
    )(x2, tgt, fw)


def _ffn_bwd_call(dx2, u, uc, cw, wdown, wup4, x1, r2, fnw, ts):
    s = dx2.shape[0]
    nt = s // ts
    wsh = W_UP_SHARD
    rev = lambda i: nt - 1 - i

    def body(dx2_ref, u_ref, uc_ref, cw_ref, wd_ref, wup_ref, x_ref, r_ref, fnw_ref,
             du_ref, dx1_ref, dcw_ref, dcb_ref, dfnw_ref, dwd_hbm, carry_ref, dwd_ref, sem):
        i = pl.program_id(0)
        _zero_first(i == 0, carry_ref, dwd_ref, dcw_ref, dcb_ref, dfnw_ref)
        dxb = dx2_ref[...].astype(BF16)
        dh = jnp.zeros((ts, D_MODEL), F32)
        for j in range(2):
            gcols = slice(j * wsh, (j + 1) * wsh)
            vcols = slice(D_FF + j * wsh, D_FF + (j + 1) * wsh)
            gate, val = uc_ref[:, gcols].astype(F32), uc_ref[:, vcols].astype(F32)
            da = _dot_nt(dxb, wd_ref[gcols, :])
            sg = _sigmoid(gate)
            sl = gate * sg
            dwd_ref[gcols, :] += _dot_tn((sl * val).astype(BF16), dxb)
            for d, cols, shard in ((da * val * _dsilu(gate, sg), gcols, j), (da * sl, vcols, 2 + j)):
                d1, d2 = _shifted_up(d, carry_ref[:, cols])
                uv = u_ref[:, cols].astype(F32)
                for t, dt in enumerate((d2, d1, d)):
                    dcw_ref[t:t + 1, cols] += _colsum(dt * uv)
                dcb_ref[:, cols] += _colsum(d)
                w = cw_ref[:, cols]
                du = (w[2:3, :] * d + w[1:2, :] * d1 + w[0:1, :] * d2).astype(BF16)
                du_ref[:, cols] = du
                dh = dh + _dot_nt(du, wup_ref[shard])
                carry_ref[:, cols] = d[0:8, :]
        r = r_ref[...]
        xh = x_ref[...] * r
        dfnw_ref[...] += _colsum(dh * xh)
        dx1_ref[...] = dx2_ref[...] + _norm_bwd(dh, xh, r, fnw_ref[...])

        @pl.when(i == nt - 1)
        def _():
            cp = pltpu.make_async_copy(dwd_ref, dwd_hbm, sem)
            cp.start()
            cp.wait()

    sd = jax.ShapeDtypeStruct
    row = lambda c: pl.BlockSpec((ts, c), lambda i: (rev(i), 0))
    once = lambda shape: pl.BlockSpec(shape, lambda i: (0,) * len(shape), pipeline_mode=pl.Buffered(1))
    return pl.pallas_call(
        body, name="ffn_bwd", grid=(nt,),
        in_specs=[row(D_MODEL), row(F2), row(F2), once((3, F2)), once((D_FF, D_MODEL)), once((4, D_MODEL, wsh)),
                  row(D_MODEL), row(1), once((1, D_MODEL))],
        out_specs=[row(F2), row(D_MODEL), _full((3, F2)), _full((1, F2)), _full((1, D_MODEL)), pl.BlockSpec(memory_space=pl.ANY)],
        out_shape=[sd((s, F2), BF16), sd((s, D_MODEL), F32), sd((3, F2), F32), sd((1, F2), F32), sd((1, D_MODEL), F32),
                   sd((D_FF, D_MODEL), F32)],
        scratch_shapes=[pltpu.VMEM((8, F2), F32), pltpu.VMEM((D_FF, D_MODEL), F32), pltpu.SemaphoreType.DMA],
        compiler_params=_cp("arbitrary", vmem=VMEM_LIMIT_MLP_BWD),
    )(dx2, u, uc, cw, wdown, wup4, x1, r2, fnw)


def _shifted_up(d, hal):
    n = d.shape[0]
    row = lax.broadcasted_iota(jnp.int32, hal.shape, 0)
    r1, r2 = pltpu.roll(d, n - 1, 0), pltpu.roll(d, n - 2, 0)
    end1 = jnp.where(row == 7, hal[0:1, :], r1[n - 8:, :])
    end2 = jnp.where(row == 6, hal[0:1, :], jnp.where(row == 7, hal[1:2, :], r2[n - 8:, :]))
    return jnp.concatenate([r1[:n - 8, :], end1], axis=0), jnp.concatenate([r2[:n - 8, :], end2], axis=0)


def _dw_norm_call(x, r, nw, b, ts, tn, name):
    s, n = b.shape
    k = x.shape[1]

    def body(x_ref, r_ref, nw_ref, b_ref, dw_ref):
        _zero_first(pl.program_id(1) == 0, dw_ref)
        h = (x_ref[...] * r_ref[...] * nw_ref[...]).astype(BF16)
        dw_ref[...] += _dot_tn(h, b_ref[...])

    return pl.pallas_call(
        body, name=name, grid=(n // tn, s // ts),
        in_specs=[pl.BlockSpec((ts, k), lambda j, i: (i, 0)), pl.BlockSpec((ts, 1), lambda j, i: (i, 0)),
                  pl.BlockSpec((1, k), lambda j, i: (0, 0)), pl.BlockSpec((ts, tn), lambda j, i: (i, j))],
        out_specs=pl.BlockSpec((None, k, tn), lambda j, i: (j, 0, 0)),
        out_shape=jax.ShapeDtypeStruct((n // tn, k, tn), F32),
        compiler_params=_cp("parallel", "arbitrary"),
    )(x, r, nw, b)


def _out_bwd_call(dx1, yret, ymla, wout, ts, order=None):
    s = dx1.shape[0]

    def body(dx_ref, yr_ref, ym_ref, w_ref, dyr_ref, do_ref, dwo_ref):
        _zero_first(pl.program_id(0) == 0, dwo_ref)
        dxb = dx_ref[...].astype(BF16)
        dmix = _dot_nt(dxb, w_ref[...])
        dyr_ref[...] = dmix[:, :RET_W]
        ym = ym_ref[...]
        lane = lax.broadcasted_iota(jnp.int32, (ts, LANES), 1)
        for p in range(N_HEADS // 2):
            dom = dmix[:, RET_W + p * LANES:RET_W + (p + 1) * LANES]
            prod = dom * ym[:, p * LANES:(p + 1) * LANES].astype(F32)
            for hh in range(2):
                mine = (lane >= HEAD) if hh else (lane < HEAD)
                hi, lo = _hi_lo(jnp.sum(jnp.where(mine, prod, 0.0), axis=1, keepdims=True))
                base = jnp.where(lane < HEAD, pltpu.roll(dom, HEAD, 1) if hh else dom, 0.0)
                do_ref[2 * p + hh] = _lane_pair((ts, LANES), V_AUX, -hi, -lo, base).astype(BF16)
        dwo_ref[0:RET_W, :] += _dot_tn(yr_ref[...], dxb)
        dwo_ref[RET_W:, :] += _dot_tn(ym, dxb)

    sd = jax.ShapeDtypeStruct
    body, first_specs, first = _ordered_after(body, order)
    return pl.pallas_call(
        body, name="out_proj_bwd", grid=(s // ts,),
        in_specs=first_specs + [_row(ts, D_MODEL), _row(ts, RET_W), _row(ts, MLA_W), _full((D_MODEL, D_MODEL))],
        out_specs=[_row(ts, RET_W), _hrow(N_HEADS, ts, LANES), _full((D_MODEL, D_MODEL))],
        out_shape=[sd((s, RET_W), F32), sd((N_HEADS, s, LANES), BF16), sd((D_MODEL, D_MODEL), F32)],
        compiler_params=_cp("arbitrary"),
    )(*first, dx1, yret, ymla, wout)


def _ret_bwd_q_call(q, k, v, o, g, dy, gnw, rc, cos_r, sin_r, tr):
    s = q.shape[0]
    c = RET_CHUNK
    nc = tr // c
    ns = RET_SLABS

    def body(q_ref, k_ref, v_ref, o_ref, g_ref, dy_ref, gnw_ref, dm_ref, zeta_ref, xi_ref, cd_ref, bd_ref, cr_ref, sr_ref,
             dq_ref, dg_ref, do_ref, dgnw_ref, st_ref):
        _zero_first(pl.program_id(1) == 0, st_ref, dgnw_ref)
        bd = bd_ref[...]
        avg = bd * (1.0 / HEAD)
        chunks = [slice(ci * c, (ci + 1) * c) for ci in range(nc)]
        lanes = [slice(sl * LANES, (sl + 1) * LANES) for sl in range(ns)]
        dov = []
        for ln in lanes:
            ov = o_ref[:, ln]
            ctr = ov - _dot_hi(ov, avg)
            rs = lax.rsqrt(_dot_hi(ctr * ctr, avg) + EPS)
            oh = ctr * rs
            gg, dyv, gnw_v = g_ref[:, ln], dy_ref[:, ln], gnw_ref[:, ln]
            sg = _sigmoid(gg)
            sl = gg * sg
            dg_ref[:, ln] = (dyv * oh * gnw_v * _dsilu(gg, sg)).astype(BF16)
            dgnw_ref[:, ln] += _colsum(dyv * sl * oh)
            doh = dyv * sl * gnw_v
            dov.append((rs * (doh - _dot_hi(doh, avg) - oh * _dot_hi(doh * oh, avg))).astype(BF16))
            do_ref[:, ln] = dov[-1]
        states = _ret_states(k_ref, v_ref, zeta_ref, cd_ref, bd, st_ref, chunks, lanes, False)
        for ci, rows in enumerate(chunks):
            for sl, ln in enumerate(lanes):
                doc = dov[sl][rows, :]
                dq = (_dot_nt(doc, states[sl][ci]) * xi_ref[sl]
                      + _pair_product(doc, _stack_heads(v_ref[rows, ln]), dm_ref[sl], _stack_heads(k_ref[rows, ln])))
                dq_ref[rows, ln] = _unrope(dq, cr_ref[rows, :], sr_ref[rows, :], HEAD // 2).astype(BF16)

    specs = _ret_specs(tr, lambda i: i)
    sd = jax.ShapeDtypeStruct
    return pl.pallas_call(
        body, name="ret_bwd_q", grid=(4 // ns, s // tr),
        in_specs=[specs["slab"]] * 6 + [specs["vec"], specs["dmask"], specs["rows"], specs["rows"], specs["state"], specs["bd"],
                                        specs["tab"], specs["tab"]],
        out_specs=[specs["slab"]] * 3 + [specs["vec"]],
        out_shape=[sd((s, RET_W), BF16), sd((s, RET_W), BF16), sd((s, RET_W), BF16), sd((1, RET_W), F32)],
        scratch_shapes=[pltpu.VMEM((ns, LANES, LANES), F32)],
        compiler_params=_cp("parallel", "arbitrary"),
    )(q, k, v, o, g, dy, gnw, rc["dmask"], rc["zeta"], rc["xi"], rc["cd"], rc["bd"], cos_r, sin_r)


def _ret_bwd_kv_call(q, k, v, do, rc, cos_r, sin_r, tr):
    s = q.shape[0]
    c = RET_CHUNK
    nc = tr // c
    nt = s // tr
    ns = RET_SLABS

    def body(q_ref, k_ref, v_ref, do_ref, dm_ref, zeta_ref, xi_ref, cd_ref, bd_ref, cr_ref, sr_ref, dk_ref, dv_ref, gs_ref):
        _zero_first(pl.program_id(1) == 0, gs_ref)
        bd = bd_ref[...]
        chunks = [slice(ci * c, (ci + 1) * c) for ci in range(nc)]
        lanes = [slice(sl * LANES, (sl + 1) * LANES) for sl in range(ns)]
        states = _ret_states(q_ref, do_ref, xi_ref, cd_ref, bd, gs_ref, chunks, lanes, True)
        for ci, rows in enumerate(chunks):
            for sl, ln in enumerate(lanes):
                kc, vc = k_ref[rows, ln], v_ref[rows, ln]
                q2, do2 = _stack_heads(q_ref[rows, ln]), _stack_heads(do_ref[rows, ln])
                gb = states[sl][ci]
                dk = _dot_nt(vc, gb) * zeta_ref[sl] + _pair_product(vc, do2, dm_ref[sl], q2)
                dv = _dot(kc, gb) * zeta_ref[sl] + _pair_product(kc, q2, dm_ref[sl], do2)
                dk_ref[rows, ln] = (_unrope(dk, cr_ref[rows, :], sr_ref[rows, :], HEAD // 2) * (HEAD ** -0.5)).astype(BF16)
                dv_ref[rows, ln] = dv.astype(BF16)

    specs = _ret_specs(tr, lambda i: nt - 1 - i)
    sd = jax.ShapeDtypeStruct
    return pl.pallas_call(
        body, name="ret_bwd_kv", grid=(4 // ns, nt),
        in_specs=[specs["slab"]] * 4 + [specs["dmask"], specs["rows"], specs["rows"], specs["state"], specs["bd"],
                                        specs["tab"], specs["tab"]],
        out_specs=[specs["slab"]] * 2,
        out_shape=[sd((s, RET_W), BF16), sd((s, RET_W), BF16)],
        scratch_shapes=[pltpu.VMEM((ns, LANES, LANES), F32)],
        compiler_params=_cp("parallel", "arbitrary"),
    )(q, k, v, do, rc["dmask_t"], rc["zeta"], rc["xi"], rc["cd"], rc["bd"], cos_r, sin_r)


FLASH_BWD_HEADS = 8


def _flash_bwd_call(qb, k, v, do, tb, order=None):
    s = qb.shape[1]
    nb = s // tb
    hg = FLASH_BWD_HEADS
    pairs = [(a, b) for a in range(nb) for b in range(a, nb)]
    ki_of, qi_of = (jnp.asarray(np.array(col, np.int32)) for col in zip(*pairs))
    extra = [] if order is None else [order]

    def body(ki_ref, qi_ref, *refs):
        q_ref, k_ref, v_ref, do_ref, dk_ref, dv_ref, dq_hbm, dka_ref, dva_ref, dq_ref, sem = refs[len(extra):]
        g, p = pl.program_id(0), pl.program_id(1)
        ki, qi = ki_ref[p], qi_ref[p]
        _zero_first(p == 0, dq_ref)
        _zero_first(qi == ki, dka_ref, dva_ref)
        rows = pl.ds(pl.multiple_of(qi * tb, tb), tb)

        def step(masked):
            if masked:
                keep = lax.broadcasted_iota(jnp.int32, (tb, tb), 0) <= lax.broadcasted_iota(jnp.int32, (tb, tb), 1)
            for h in range(hg):
                st = _dot_nt(k_ref[h], q_ref[h])
                if masked:
                    st = jnp.where(keep, st, NEG)
                pt = jnp.exp2(st)
                dob = do_ref[h]
                dva_ref[h] += _dot(pt.astype(BF16), dob)
                dst = (pt * _dot_nt(v_ref[h], dob)).astype(BF16)
                dka_ref[h] += _dot(dst, q_ref[h])
                dq_ref[h, rows, :] += _dot_tn(dst, k_ref[h])

        @pl.when(qi > ki)
        def _():
            step(False)

        @pl.when(qi == ki)
        def _():
            step(True)

        @pl.when(qi == nb - 1)
        def _():
            dk_ref[...] = (dka_ref[...] * LN2).astype(BF16)
            dv_ref[...] = dva_ref[...].astype(BF16)

        @pl.when(p == len(pairs) - 1)
        def _():
            cp = pltpu.make_async_copy(dq_ref, dq_hbm.at[pl.ds(g * hg, hg)], sem)
            cp.start()
            cp.wait()

    kspec = pl.BlockSpec((hg, tb, LANES), lambda g, p, ki_ref, qi_ref: (g, ki_ref[p], 0))
    qspec = pl.BlockSpec((hg, tb, LANES), lambda g, p, ki_ref, qi_ref: (g, qi_ref[p], 0))
    hm = jax.ShapeDtypeStruct((N_HEADS, s, LANES), BF16)
    return pl.pallas_call(
        body, name="mla_flash_bwd",
        grid_spec=pltpu.PrefetchScalarGridSpec(
            num_scalar_prefetch=2, grid=(N_HEADS // hg, len(pairs)),
            in_specs=[ANY] * len(extra) + [qspec, kspec, kspec, qspec],
            out_specs=[kspec, kspec, ANY],
            scratch_shapes=[pltpu.VMEM((hg, tb, LANES), F32), pltpu.VMEM((hg, tb, LANES), F32),
                            pltpu.VMEM((hg, s, LANES), F32), pltpu.SemaphoreType.DMA]),
        out_shape=[hm, hm, jax.ShapeDtypeStruct((N_HEADS, s, LANES), F32)],
        compiler_params=_cp("arbitrary", "arbitrary"),
    )(ki_of, qi_of, *extra, qb, k, v, do)


def _mla_post_call(dq, dk, dv, cq, ckv, qnw, kvnw, wq, wk, wv, cos_m, sin_m, ts):
    s = cq.shape[0]

    def body(dq_ref, dk_ref, dv_ref, cq_ref, ckv_ref, qnw_ref, kvnw_ref, wq_ref, wk_ref, wv_ref, cm_ref, sm_ref,
             dcq_ref, dckv_ref, dkpe_ref, dwq_ref, dwk_ref, dwv_ref, dqnw_ref, dkvnw_ref):
        _zero_first(pl.program_id(0) == 0, dwq_ref, dwk_ref, dwv_ref, dqnw_ref, dkvnw_ref)
        cqv, ckvv = cq_ref[...], ckv_ref[...]
        rq, rkv = _rstd(cqv), _rstd(ckvv)
        qh_, kvh_ = cqv * rq, ckvv * rkv
        qnw_v, kvnw_v = qnw_ref[...], kvnw_ref[...]
        cqn = (qh_ * qnw_v).astype(BF16)
        ckvn = (kvh_ * kvnw_v).astype(BF16)
        cm, sm = cm_ref[...], sm_ref[...]
        dcqn = jnp.zeros((ts, Q_RANK), F32)
        dckvn = jnp.zeros((ts, KV_RANK), F32)
        dkpe = jnp.zeros((ts, LANES), F32)
        for h in range(N_HEADS):
            dqu = _unrope(dq_ref[h] * SM_SCALE, cm, sm, ROPE // 2).astype(BF16)
            dwq_ref[h] += _dot_tn(cqn, dqu)
            dcqn = dcqn + _dot_nt(dqu, wq_ref[h])
            dkb, dvb = dk_ref[h], dv_ref[h]
            dkpe = dkpe + dkb.astype(F32)
            dwk_ref[h] += _dot_tn(ckvn, dkb)
            dwv_ref[h] += _dot_tn(ckvn, dvb)
            dckvn = dckvn + _dot_nt(dkb, wk_ref[h]) + _dot_nt(dvb, wv_ref[h])
        lane = lax.broadcasted_iota(jnp.int32, (ts, LANES), 1)
        dkpe = jnp.where((lane >= KPE_LO) & (lane < KPE_LO + ROPE), dkpe, 0.0)
        dkpe_ref[...] = _unrope(dkpe, cm, sm, ROPE // 2).astype(BF16)
        dqnw_ref[...] += _colsum(dcqn * qh_)
        dkvnw_ref[...] += _colsum(dckvn * kvh_)
        dcq_ref[...] = _norm_bwd(dcqn, qh_, rq, qnw_v).astype(BF16)
        dckv_ref[...] = _norm_bwd(dckvn, kvh_, rkv, kvnw_v).astype(BF16)

    sd = jax.ShapeDtypeStruct
    hm = _hrow(N_HEADS, ts, LANES)
    return pl.pallas_call(
        body, name="mla_post", grid=(s // ts,),
        in_specs=[hm, hm, hm, _row(ts, Q_RANK), _row(ts, KV_RANK), _full((1, Q_RANK)), _full((1, KV_RANK)),
                  _full((N_HEADS, Q_RANK, LANES)), _full((N_HEADS, KV_RANK, LANES)), _full((N_HEADS, KV_RANK, LANES)),
                  _row(ts, LANES), _row(ts, LANES)],
        out_specs=[_row(ts, Q_RANK), _row(ts, KV_RANK), _row(ts, LANES),
                   _full((N_HEADS, Q_RANK, LANES)), _full((N_HEADS, KV_RANK, LANES)), _full((N_HEADS, KV_RANK, LANES)),
                   _full((1, Q_RANK)), _full((1, KV_RANK))],
        out_shape=[sd((s, Q_RANK), BF16), sd((s, KV_RANK), BF16), sd((s, LANES), BF16),
                   sd((N_HEADS, Q_RANK, LANES), F32), sd((N_HEADS, KV_RANK, LANES), F32), sd((N_HEADS, KV_RANK, LANES), F32),
                   sd((1, Q_RANK), F32), sd((1, KV_RANK), F32)],
        compiler_params=_cp("arbitrary"),
    )(dq, dk, dv, cq, ckv, qnw, kvnw, wq, wk, wv, cos_m, sin_m)


def _in_bwd_call(parts, x, r1, anw, dx1, win, ts):
    s = x.shape[0]
    widths = [p.shape[1] for p in parts]
    np_ = len(parts)

    def body(*refs):
        p_refs = refs[:np_]
        x_ref, r_ref, anw_ref, dx1_ref, w_ref, dx_ref, dw_ref, danw_ref = refs[np_:]
        _zero_first(pl.program_id(0) == 0, dw_ref, danw_ref)
        dproj = jnp.concatenate([p[...] for p in p_refs], axis=-1)
        r, anw_v = r_ref[...], anw_ref[...]
        xh = x_ref[...] * r
        dw_ref[...] += _dot_tn((xh * anw_v).astype(BF16), dproj)
        dh = _dot_nt(dproj, w_ref[...])
        danw_ref[...] += _colsum(dh * xh)
        dx_ref[...] = dx1_ref[...] + _norm_bwd(dh, xh, r, anw_v)

    sd = jax.ShapeDtypeStruct
    return pl.pallas_call(
        body, name="in_proj_bwd", grid=(s // ts,),
        in_specs=[_row(ts, w) for w in widths]
        + [_row(ts, D_MODEL), _row(ts, 1), _full((1, D_MODEL)), _row(ts, D_MODEL), _full((D_MODEL, IN_EXT))],
        out_specs=[_row(ts, D_MODEL), _full((D_MODEL, IN_EXT)), _full((1, D_MODEL))],
        out_shape=[sd((s, D_MODEL), F32), sd((D_MODEL, IN_EXT), F32), sd((1, D_MODEL), F32)],
        compiler_params=_cp("arbitrary"),
    )(*parts, x, r1, anw, dx1, win)


def _local_step(x, positions, tgt, w, small, ex=None):
    s = x.shape[0]
    t = _tiles(s)
    ex = _Exchanges(w) if ex is None else ex
    f = _forward(x, positions, w, small, ex)
    pw, rc = f["pw"], f["rc"]
    cos_r, sin_r, cos_m, sin_m = f["tabs"]
    dx2, loss, g_fw = _loss_call(f["x2"], tgt, small["final_norm_w"], t["ts"])
    du, dx1, g_cw, g_cb, g_fnw, g_wd = _ffn_bwd_call(dx2, f["u"], f["uc"], w["conv_w"], pw["wdown"], pw["wup"],
                                                     f["x1"], f["r2"], small["ffn_norm_w"], t["t2"])
    g_wup = _dw_norm_call(f["x1"], f["r2"], small["ffn_norm_w"], du, t["ts"], F2 // 4, "dw_up")
    started = ex.mlp_grads(dict(w_up=g_wup, w_down=g_wd))
    dy_ret, do, g_wout = _out_bwd_call(dx1, f["y_ret"], f["y_mla"], pw["wout"], t["ts"], started)
    started = ex.behind_out_bwd(g_wout)
    drq, dg, do_ret, g_gnw = _ret_bwd_q_call(f["q"], f["k"], f["v"], f["o_ret"], f["g"], dy_ret, small["ret_gn_w"], rc, cos_r, sin_r, t["tr"])
    drk, drv = _ret_bwd_kv_call(f["q"], f["k"], f["v"], do_ret, rc, cos_r, sin_r, t["tr"])
    dmk, dmv, dmq = _flash_bwd_call(f["mqb"], f["mk"], f["mv"], do, t["tb"], started)
    ex.behind_attention(dmk)
    dcq, dckv, dkpe, g_wq, g_wk, g_wv, g_qnw, g_kvnw = _mla_post_call(
        dmq, dmk, dmv, f["cq"], f["ckv"], small["mla_q_norm_w"], small["mla_kv_norm_w"], pw["wq"], pw["wk"], pw["wv"], cos_m, sin_m, t["ts"])
    gx, g_win_ext, g_anw = _in_bwd_call([drq, drk, drv, dg, dcq, dckv, dkpe], x, f["r1"], small["attn_norm_w"], dx1, pw["win"], t["ts"])
    lo = IN_W - ROPE
    g_win = jnp.concatenate([g_win_ext[:, :lo], g_win_ext[:, lo + KPE_LO:lo + KPE_LO + ROPE]], -1)
    g_wuq = g_wq.transpose(1, 0, 2)[:, :, :HEAD + ROPE].reshape(Q_RANK, N_HEADS * (HEAD + ROPE))
    g_wukv = jnp.concatenate([g_wk[:, :, :HEAD], g_wv[:, :, :HEAD]], -1).transpose(1, 0, 2).reshape(KV_RANK, 2 * MLA_W)
    gw = dict(w_in=g_win, w_uq=g_wuq, w_ukv=g_wukv, w_out=g_wout, w_up=g_wup,
              conv_w=g_cw, w_down=g_wd)
    gs = dict(attn_norm_w=g_anw, ret_gn_w=g_gnw, mla_q_norm_w=g_qnw, mla_kv_norm_w=g_kvnw, ffn_norm_w=g_fnw,
              conv_b=g_cb, final_norm_w=g_fw)
    return loss, gx, gw, gs


MESH_ID = pl.DeviceIdType.MESH
ANY = pl.BlockSpec(memory_space=pl.ANY)
VMEM_SPEC = pl.BlockSpec(memory_space=pltpu.VMEM)
N_DEV = 8
GROUP_A = (("w_in", (D_MODEL, IN_W // 4), 1), ("w_uq", (Q_RANK, 192), 1), ("w_ukv", (KV_RANK, 256), 1),
           ("w_out", (D_MODEL // 4, D_MODEL), 0))
GROUP_B = (("w_up", (D_MODEL, F2 // 4), 1), ("w_down", (D_FF // 4, D_MODEL), 0))
HBM_SPEC = pl.BlockSpec(memory_space=pltpu.HBM)
SEM_SPEC = pl.BlockSpec(memory_space=pltpu.SEMAPHORE)


def _mesh_pos():
    return lax.axis_index("x"), lax.axis_index("y"), lax.axis_index("c")


def _other_chips(x, y):
    return [(1 - x, y), (x, 1 - y), (1 - x, 1 - y)]


def _remote(src, dst, send_sems, recv_sems, k, dev):
    return pltpu.make_async_remote_copy(src_ref=src, dst_ref=dst, send_sem=send_sems.at[k], recv_sem=recv_sems.at[k],
                                        device_id=dev, device_id_type=MESH_ID)


def _gather_list_call(parts, tag):
    n = len(parts)

    def body(*refs):
        srcs, outs, (send_sems, recv_sems) = refs[:n], refs[n:2 * n], refs[2 * n:]
        x, y, c = _mesh_pos()
        sm = 2 * x + y
        chips = _other_chips(x, y)
        sib = (x, y, 1 - c)
        rc = lambda k, src, dst, dev: _remote(src, dst, send_sems, recv_sems, k, dev)
        first = [rc(7 * i + j, srcs[i].at[c], outs[i].at[sm, c], (cx, cy, c)) for i in range(n) for j, (cx, cy) in enumerate(chips)]
        own = [rc(7 * i + 6, srcs[i], outs[i].at[sm], sib) for i in range(n)]
        for cp in first + own:
            cp.start()
        passed = []
        for j, (cx, cy) in enumerate(chips):
            for i in range(n):
                land = outs[i].at[2 * cx + cy, c]
                rc(7 * i + j, srcs[i].at[c], land, (cx, cy, c)).wait_recv()
                cp = rc(7 * i + 3 + j, land, land, sib)
                cp.start()
                passed.append(cp)
        for j, (cx, cy) in enumerate(chips):
            for i in range(n):
                rc(7 * i + 3 + j, srcs[i].at[c], outs[i].at[2 * cx + cy, 1 - c], sib).wait_recv()
        for cp in own:
            cp.wait_recv()
        for cp in first + passed + own:
            cp.wait_send()

    return pl.pallas_call(
        body, name="weights_all_gather_" + tag,
        in_specs=[ANY] * n, out_specs=[ANY] * n,
        out_shape=[jax.ShapeDtypeStruct((4,) + p.shape, p.dtype) for p in parts],
        scratch_shapes=[pltpu.SemaphoreType.DMA((7 * n,)), pltpu.SemaphoreType.DMA((7 * n,))],
    )(*parts)


def _direct_gather_copies(srcs, lands, send_sems, recv_sems):
    x, y, c = _mesh_pos()
    sm = 2 * x + y
    sends, recvs = [], []
    for i, (src, land) in enumerate(zip(srcs, lands)):
        for j, (cx, cy) in enumerate(_other_chips(x, y)):
            for t in range(2):
                sends.append(_remote(src.at[c], land.at[sm, c], send_sems, recv_sems, 13 * i + 4 * j + 2 * c + t, (cx, cy, t)))
                recvs.append(_remote(src.at[t], land.at[2 * cx + cy, t], send_sems, recv_sems, 13 * i + 4 * j + 2 * t + c, (cx, cy, t)))
        sends.append(_remote(src, land.at[sm], send_sems, recv_sems, 13 * i + 12, (x, y, 1 - c)))
        recvs.append(_remote(src, land.at[sm], send_sems, recv_sems, 13 * i + 12, (x, y, 1 - c)))
    return sends, recvs


def _sibling_copies(srcs, lands, send_sems, recv_sems):
    x, y, c = _mesh_pos()
    cps = [_remote(src.at[s, 1 - c], land.at[s], send_sems, recv_sems, 4 * i + s, (x, y, 1 - c))
           for i, (src, land) in enumerate(zip(srcs, lands)) for s in range(4)]
    return cps, cps


def _chips_copies(srcs, lands, send_sems, recv_sems):
    x, y, c = _mesh_pos()
    cps = [_remote(src.at[2 * cx + cy], land.at[j], send_sems, recv_sems, 3 * i + j, (cx, cy, c))
           for i, (src, land) in enumerate(zip(srcs, lands)) for j, (cx, cy) in enumerate(_other_chips(x, y))]
    return cps, cps


def _share_copies(srcs, lands, send_sems, recv_sems):
    x, y, c = _mesh_pos()
    cps = [_remote(src, land, send_sems, recv_sems, i, (x, y, 1 - c)) for i, (src, land) in enumerate(zip(srcs, lands))]
    return cps, cps


def _exchange_call(name, copies, srcs, land_shapes, n_sems):
    n = len(srcs)

    def body(*refs):
        sends, recvs = copies(refs[:n], refs[n:2 * n], refs[2 * n], refs[2 * n + 1])
        for cp in sends:
            cp.start()
        for cp in sends:
            cp.wait_send()
        for cp in recvs:
            cp.wait_recv()

    return pl.pallas_call(
        body, name=name, in_specs=[ANY] * n, out_specs=[ANY] * n, out_shape=list(land_shapes),
        scratch_shapes=[pltpu.SemaphoreType.DMA((n_sems,)), pltpu.SemaphoreType.DMA((n_sems,))],
    )(*srcs)


def _exchange_start_call(name, copies, srcs, land_shapes, n_sems, order=None):
    n = len(srcs)
    extra = [] if order is None else [order]
    k = 2 * n + len(extra)

    def body(*refs):
        sends, _ = copies(refs[:n], refs[n:2 * n], refs[k], refs[k + 1])
        for cp in sends:
            cp.start()
        refs[-1][...] = jnp.zeros_like(refs[-1])

    hbm = lambda a: pltpu.with_memory_space_constraint(a, pltpu.HBM)
    lands = [hbm(lax.empty(sd.shape, sd.dtype)) for sd in land_shapes]
    sem = pltpu.SemaphoreType.DMA((n_sems,))
    out = pl.pallas_call(
        body, name=name,
        out_shape=(sem, sem, *[pltpu.HBM(a.shape, a.dtype) for a in list(srcs) + lands], jax.ShapeDtypeStruct((8, LANES), F32)),
        in_specs=[HBM_SPEC] * (2 * n) + [ANY] * len(extra), out_specs=(SEM_SPEC, SEM_SPEC, *[HBM_SPEC] * (2 * n), VMEM_SPEC),
        input_output_aliases={i: 2 + i for i in range(2 * n)},
        compiler_params=pltpu.CompilerParams(has_side_effects=pltpu.SideEffectType.DATAFLOW_SIDE_EFFECTING),
    )(*[hbm(a) for a in srcs], *lands, *extra)
    return out[0], out[1], out[2:2 + n], out[2 + n:2 + 2 * n], out[-1]


def _exchange_wait_call(name, copies, started, after):
    send_sems, recv_sems, srcs, lands, _ = started
    n = len(srcs)

    def body(*refs):
        sends, recvs = copies(refs[:n], refs[n:2 * n], refs[2 * n], refs[2 * n + 1])
        for cp in sends:
            cp.wait_send()
        for cp in recvs:
            cp.wait_recv()

    out = pl.pallas_call(
        body, name=name,
        out_shape=tuple(pltpu.HBM(a.shape, a.dtype) for a in list(srcs) + list(lands)),
        in_specs=[HBM_SPEC] * (2 * n) + [SEM_SPEC, SEM_SPEC, ANY], out_specs=tuple([HBM_SPEC] * (2 * n)),
        input_output_aliases={i: i for i in range(2 * n)},
        compiler_params=pltpu.CompilerParams(has_side_effects=pltpu.SideEffectType.DATAFLOW_SIDE_EFFECTING),
    )(*srcs, *lands, send_sems, recv_sems, after)
    return out[:n], out[n:]


def _rows_tile(rows, width, itemsize=4):
    limit = max(16, (3 << 20) // (width * itemsize))
    if rows <= limit:
        return rows
    return max(t for t in range(16, limit + 1, 16) if rows % t == 0)


def _sum_sibling_call(g, buf, c, name):
    _, _, rh, w = g.shape
    tile = _rows_tile(rh, w)

    def body(c_ref, g_ref, b_ref, p_ref, pb_ref):
        p = g_ref[...] + b_ref[...]
        p_ref[...] = p
        pb_ref[...] = p.astype(BF16)

    blk = pl.BlockSpec((None, tile, w), lambda s, i, c_ref: (s, i, 0))
    return pl.pallas_call(
        body, name=name,
        grid_spec=pltpu.PrefetchScalarGridSpec(
            num_scalar_prefetch=1, grid=(4, rh // tile),
            in_specs=[pl.BlockSpec((None, None, tile, w), lambda s, i, c_ref: (s, c_ref[0], i, 0)), blk],
            out_specs=[blk, blk]),
        out_shape=[jax.ShapeDtypeStruct((4, rh, w), F32), jax.ShapeDtypeStruct((4, rh, w), BF16)],
        compiler_params=_cp("parallel", "parallel"),
    )(c, g, buf)


def _sum_chips_call(p, buf, sm, name):
    _, rh, w = p.shape
    tile = _rows_tile(rh, w)

    def body(sm_ref, p_ref, b_ref, f_ref):
        f_ref[...] = ((p_ref[...] + b_ref[0].astype(F32)) + b_ref[1].astype(F32)) + b_ref[2].astype(F32)

    return pl.pallas_call(
        body, name=name,
        grid_spec=pltpu.PrefetchScalarGridSpec(
            num_scalar_prefetch=1, grid=(rh // tile,),
            in_specs=[pl.BlockSpec((None, tile, w), lambda i, sm_ref: (sm_ref[0], i, 0)),
                      pl.BlockSpec((3, tile, w), lambda i, sm_ref: (0, i, 0))],
            out_specs=pl.BlockSpec((tile, w), lambda i, sm_ref: (i, 0))),
        out_shape=jax.ShapeDtypeStruct((rh, w), F32),
        compiler_params=_cp("parallel"),
    )(sm, p, buf)


def _adamw_halves_call(w, g_mine, g_sib, c, m, v, name):
    r, wd = w.shape
    rh = r // 2
    tile = _rows_tile(rh, wd)
    nt = rh // tile

    def body(c_ref, w_ref, gm_ref, gs_ref, m_ref, v_ref, g_ref, d_ref, nm_ref, nv_ref):
        gv = jnp.where(pl.program_id(0) == c_ref[0], gm_ref[...], gs_ref[...])
        g_ref[...] = gv
        nm = ADAM_B1 * m_ref[...] + (1.0 - ADAM_B1) * gv
        nv = ADAM_B2 * v_ref[...] + (1.0 - ADAM_B2) * jnp.square(gv)
        m_hat = nm / (1.0 - ADAM_B1 ** ADAM_STEP)
        v_hat = nv / (1.0 - ADAM_B2 ** ADAM_STEP)
        d_ref[...] = -ADAM_LR * (m_hat / (jnp.sqrt(v_hat) + ADAM_EPS) + ADAM_WD * w_ref[...])
        nm_ref[...] = nm
        nv_ref[...] = nv

    whole = pl.BlockSpec((tile, wd), lambda h, i, c_ref: (h * nt + i, 0))
    half = pl.BlockSpec((tile, wd), lambda h, i, c_ref: (i, 0))
    sd = jax.ShapeDtypeStruct((r, wd), F32)
    return pl.pallas_call(
        body, name=name,
        grid_spec=pltpu.PrefetchScalarGridSpec(
            num_scalar_prefetch=1, grid=(2, nt),
            in_specs=[whole, half, half, whole, whole], out_specs=[whole] * 4),
        out_shape=[sd, sd, sd, sd],
        compiler_params=_cp("parallel", "parallel"),
    )(c, w, g_mine, g_sib, m, v)


def _all_reduce8_call(vec, name):
    rows = vec.shape[0]

    def body(v_ref, out_ref, slots, send_sems, recv_sems):
        x, y, c = _mesh_pos()
        me = 4 * x + 2 * y + c
        slots[me] = v_ref[...]

        def rcopy(k, to_me):
            bx, by, bc = (k >> 2) & 1, (k >> 1) & 1, k & 1
            px, py, pc = (1 - x if bx else x), (1 - y if by else y), (1 - c if bc else c)
            slot = 4 * px + 2 * py + pc if to_me else me
            return pltpu.make_async_remote_copy(src_ref=v_ref, dst_ref=slots.at[slot], send_sem=send_sems.at[k - 1],
                                                recv_sem=recv_sems.at[k - 1], device_id=(px, py, pc), device_id_type=MESH_ID)

        for k in range(1, N_DEV):
            rcopy(k, False).start()
        for k in range(1, N_DEV):
            rcopy(k, True).wait_recv()
        for k in range(1, N_DEV):
            rcopy(k, False).wait_send()
        tot = slots[0]
        for d in range(1, N_DEV):
            tot = tot + slots[d]
        out_ref[...] = tot

    return pl.pallas_call(
        body, name=name,
        in_specs=[VMEM_SPEC], out_specs=VMEM_SPEC,
        out_shape=jax.ShapeDtypeStruct((rows, LANES), F32),
        scratch_shapes=[pltpu.VMEM((N_DEV, rows, LANES), F32),
                        pltpu.SemaphoreType.DMA((N_DEV - 1,)), pltpu.SemaphoreType.DMA((N_DEV - 1,))],
    )(vec)


def _adamw_call(w, g, m, v, name):
    r, c = w.shape
    rb = r if r <= 256 else (256 if r % 256 == 0 else 352)
    assert r % rb == 0

    def body(w_ref, g_ref, m_ref, v_ref, d_ref, nm_ref, nv_ref):
        gv = g_ref[...]
        nm = ADAM_B1 * m_ref[...] + (1.0 - ADAM_B1) * gv
        nv = ADAM_B2 * v_ref[...] + (1.0 - ADAM_B2) * jnp.square(gv)
        m_hat = nm / (1.0 - ADAM_B1 ** ADAM_STEP)
        v_hat = nv / (1.0 - ADAM_B2 ** ADAM_STEP)
        d_ref[...] = -ADAM_LR * (m_hat / (jnp.sqrt(v_hat) + ADAM_EPS) + ADAM_WD * w_ref[...])
        nm_ref[...] = nm
        nv_ref[...] = nv

    spec = pl.BlockSpec((rb, c), lambda i: (i, 0))
    sd = jax.ShapeDtypeStruct((r, c), F32)
    return pl.pallas_call(
        body, name=name, grid=(r // rb,),
        in_specs=[spec] * 4, out_specs=[spec] * 3, out_shape=[sd, sd, sd],
        compiler_params=_cp("parallel"),
    )(w, g, m, v)


SMALL = (("attn_norm_w", D_MODEL), ("ret_gn_w", RET_W), ("mla_q_norm_w", Q_RANK), ("mla_kv_norm_w", KV_RANK),
         ("ffn_norm_w", D_MODEL), ("conv_b", F2), ("final_norm_w", D_MODEL))
WEIGHT_ORDER = ("attn_norm_w", "w_in", "ret_gn_w", "mla_q_norm_w", "w_uq", "mla_kv_norm_w", "w_ukv", "w_out",
                "ffn_norm_w", "w_up", "conv_w", "conv_b", "w_down", "final_norm_w")


def _pad_rows(flat, rows):
    return jnp.concatenate([flat, jnp.zeros((rows * LANES - flat.shape[0],), flat.dtype)]).reshape(rows, LANES)


def kernel(x, positions, attn_norm_w, w_in, ret_gn_w, mla_q_norm_w, w_uq, mla_kv_norm_w, w_ukv, w_out, ffn_norm_w, w_up, conv_w, conv_b, w_down, final_norm_w, loss_target, m_attn_norm_w, m_w_in, m_ret_gn_w, m_mla_q_norm_w, m_w_uq, m_mla_kv_norm_w, m_w_ukv, m_w_out, m_ffn_norm_w, m_w_up, m_conv_w, m_conv_b, m_w_down, m_final_norm_w, v_attn_norm_w, v_w_in, v_ret_gn_w, v_mla_q_norm_w, v_w_uq, v_mla_kv_norm_w, v_w_ukv, v_w_out, v_ffn_norm_w, v_w_up, v_conv_w, v_conv_b, v_w_down, v_final_norm_w):
    args = dict(locals())
    cx, cy, cc = _mesh_pos()
    sm = 2 * cx + cy

    c_arr, sm_arr = cc.reshape(1).astype(jnp.int32), sm.reshape(1).astype(jnp.int32)
    sds = jax.ShapeDtypeStruct

    def my_shards(group):
        return [args[n][0].astype(BF16).reshape(2, r // 2, c) for n, (r, c), _ in group]

    def full_weights(gathered, group):
        full = {}
        for (n, (r, c), axis), got in zip(group, gathered):
            piece = got.reshape(4, r, c)
            full[n] = piece if n == "w_up" else (piece.transpose(1, 0, 2).reshape(r, 4 * c) if axis == 1 else piece.reshape(4 * r, c))
        return full

    def by_owner(gw, group):
        out = []
        for n, (r, c), axis in group:
            g = gw[n]
            if axis == 1 and g.ndim == 2:
                g = g.reshape(r, 4, c).transpose(1, 0, 2)
            out.append(g.reshape(4, 2, r // 2, c))
        return out

    def sibling_shapes(gs):
        return [sds((4,) + g.shape[2:], F32) for g in gs]

    def chip_sums(gs, bufs, group):
        res = [_sum_sibling_call(g, b, c_arr, "grads_sum_sibling_" + n) for g, b, (n, _, _) in zip(gs, bufs, group)]
        return [p for p, _ in res], [pb for _, pb in res]

    def chips_shapes(pbs):
        return [sds((3,) + pb.shape[1:], BF16) for pb in pbs]

    def totals(ps, lands, group, tag):
        fins = [_sum_chips_call(p, l, sm_arr, "grads_sum_chips_" + n) for p, l, (n, _, _) in zip(ps, lands, group)]
        sibs = _exchange_call("grads_rs_share_" + tag, _share_copies, fins, [sds(f.shape, F32) for f in fins], len(fins))
        return {n: (f, s) for (n, _, _), f, s in zip(group, fins, sibs)}

    class StepExchanges(_Exchanges):
        def __init__(self, order):
            shards = my_shards(GROUP_B)
            self.gather = _exchange_start_call("weights_gather_start_b", _direct_gather_copies, shards,
                                               [sds((4,) + s.shape, BF16) for s in shards], 13 * len(shards), order)
            self.red = None

        def token(self):
            return self.gather[4][0:1, 0:1]

        def mlp_weights(self, after):
            return full_weights(_exchange_wait_call("weights_gather_wait_b", _direct_gather_copies, self.gather, after)[1], GROUP_B)

        def mlp_grads(self, gw):
            gs = by_owner(gw, GROUP_B)
            self.step1 = _exchange_start_call("grads_rs_sibling_start_b", _sibling_copies, gs, sibling_shapes(gs), 4 * len(gs))
            return self.step1[4]

        def behind_out_bwd(self, after):
            gs, bufs = _exchange_wait_call("grads_rs_sibling_wait_b", _sibling_copies, self.step1, after)
            self.ps, pbs = chip_sums(gs, bufs, GROUP_B)
            self.step2 = _exchange_start_call("grads_rs_chips_start_b", _chips_copies, pbs, chips_shapes(pbs), 3 * len(pbs))
            return self.step2[4]

        def behind_attention(self, after):
            _, lands = _exchange_wait_call("grads_rs_chips_wait_b", _chips_copies, self.step2, after)
            self.red = totals(self.ps, lands, GROUP_B, "b")

    gathered = _gather_list_call(my_shards(GROUP_A) + [conv_w[0].reshape(2, 1, 3 * F2 // 8)], "a")
    full = full_weights(gathered[:-1], GROUP_A)
    ex = StepExchanges(gathered[-1])
    full["conv_w"] = gathered[-1].reshape(4, 3, F2 // 4).transpose(1, 0, 2).reshape(3, F2)
    small = {n: args[n].reshape(1, d) for n, d in SMALL}
    small["attn_norm_w"] = small["attn_norm_w"] + ex.token()

    loss, gx, gw, gs = _local_step(x[0], positions[0], loss_target[0], full, small, ex)

    ga = by_owner(gw, GROUP_A)
    bufs = _exchange_call("grads_rs_sibling_a", _sibling_copies, ga, sibling_shapes(ga), 4 * len(ga))
    ps, pbs = chip_sums(ga, bufs, GROUP_A)
    lands = _exchange_call("grads_rs_chips_a", _chips_copies, pbs, chips_shapes(pbs), 3 * len(pbs))
    halves = {**ex.red, **totals(ps, lands, GROUP_A, "a")}

    vec = jnp.concatenate([gs[n].reshape(-1) for n, _ in SMALL] + [gw["conv_w"].reshape(-1), loss.reshape(-1)])
    tot = _all_reduce8_call(_pad_rows(vec, 216), "small_all_reduce").reshape(-1)
    red, off = {}, 0
    for n, d in SMALL:
        red[n] = tot[off:off + d].reshape(1, d)
        off += d
    red["conv_w"] = lax.dynamic_slice(tot[off:off + 3 * F2].reshape(3, F2), (0, sm * (F2 // 4)), (3, F2 // 4))
    loss_tot = tot[off + 3 * F2]

    grads, deltas, new_m, new_v = [], [], [], []
    for n in WEIGHT_ORDER:
        shape = args[n].shape
        two_d = (1, shape[0]) if len(shape) == 1 else shape[-2:]
        wmv = [args[k + n].reshape(two_d) for k in ("", "m_", "v_")]
        if n in halves:
            g, d, nm, nv = _adamw_halves_call(wmv[0], *halves[n], c_arr, wmv[1], wmv[2], "adamw_" + n)
        else:
            g = red[n].reshape(two_d)
            d, nm, nv = _adamw_call(wmv[0], g, wmv[1], wmv[2], "adamw_" + n)
        grads.append(g.reshape(shape))
        deltas.append(d.reshape(shape))
        new_m.append(nm.reshape(shape))
        new_v.append(nv.reshape(shape))
    return (loss_tot, gx[None], *grads, *deltas, *new_m, *new_v)
```

```python
import math

import numpy as np
import jax
import jax.numpy as jnp
from jax import lax
from jax.experimental import pallas as pl
from jax.experimental.pallas import tpu as pltpu

F32 = jnp.float32
BF16 = jnp.bfloat16

D_MODEL = 1024
N_HEADS = 8
HEAD = 64
RET_W = N_HEADS * HEAD
MLA_W = N_HEADS * HEAD
ROPE = 32
Q_RANK = 256
KV_RANK = 128
D_FF = 2816
F2 = 2 * D_FF
IN_W = 4 * RET_W + Q_RANK + KV_RANK + ROPE
IN_EXT = 4 * RET_W + Q_RANK + KV_RANK + 128
KPE_LO = 64
ROPE_BASE = 10000.0
EPS = 1e-6
RET_CHUNK = 256
SM_SCALE = (HEAD + ROPE) ** -0.5
LOG2E = math.log2(math.e)
LN2 = math.log(2.0)
NEG = -1e30
LANES = 128
VMEM_LIMIT = 56 * 1024 * 1024

ADAM_LR = 0.001
ADAM_B1 = 0.9
ADAM_B2 = 0.999
ADAM_EPS = 1e-08
ADAM_WD = 0.01
ADAM_STEP = 10


VMEM_LIMIT_MLP_BWD = 60 * 1024 * 1024


def _cp(*sem, vmem=VMEM_LIMIT):
    return pltpu.CompilerParams(dimension_semantics=sem, vmem_limit_bytes=vmem)


def _full(shape):
    n = len(shape)
    return pl.BlockSpec(tuple(shape), lambda *_: (0,) * n)


def _row(ts, c):
    return pl.BlockSpec((ts, c), lambda i: (i, 0))


def _hrow(h, ts, c):
    return pl.BlockSpec((h, ts, c), lambda i: (0, i, 0))


def _dot(a, b):
    return jnp.dot(a, b, preferred_element_type=F32)


def _dot_nt(a, b):
    return lax.dot_general(a, b, (((1,), (1,)), ((), ())), preferred_element_type=F32)


def _dot_tn(a, b):
    return lax.dot_general(a, b, (((0,), (0,)), ((), ())), preferred_element_type=F32)


def _dot_hi(a, b):
    hi = a.astype(BF16)
    lo = (a - hi.astype(F32)).astype(BF16)
    bb = b.astype(BF16)
    return _dot(hi, bb) + _dot(lo, bb)


def _rot_half(x, half):
    w = x.shape[-1]
    lane = lax.broadcasted_iota(jnp.int32, x.shape, x.ndim - 1)
    first = (lane % (2 * half)) < half
    return jnp.where(first, -pltpu.roll(x, w - half, x.ndim - 1), pltpu.roll(x, half, x.ndim - 1))


def _rope(x, cos, sin, half):
    return x * cos + _rot_half(x, half) * sin


def _unrope(dy, cos, sin, half):
    return dy * cos - _rot_half(dy, half) * sin


def _sigmoid(g):
    return 0.5 * jnp.tanh(0.5 * g) + 0.5


def _silu(g):
    return g * _sigmoid(g)


def _rstd(x):
    return lax.rsqrt(jnp.mean(x * x, axis=-1, keepdims=True) + EPS)


def _rope_tables(positions):
    pos = positions.astype(F32)[:, None]
    s = pos.shape[0]
    inv = ROPE_BASE ** (-jnp.arange(0, HEAD, 2, dtype=F32) / HEAD)
    ang = pos * inv
    c, sn = jnp.cos(ang), jnp.sin(ang)
    cos_r = jnp.tile(jnp.concatenate([c, c], -1), (1, 2))
    sin_r = jnp.tile(jnp.concatenate([sn, sn], -1), (1, 2))
    inv = ROPE_BASE ** (-jnp.arange(0, ROPE, 2, dtype=F32) / ROPE)
    ang = pos * inv
    c, sn = jnp.cos(ang), jnp.sin(ang)
    one, zero = jnp.ones((s, KPE_LO), F32), jnp.zeros((s, KPE_LO), F32)
    cos_m = jnp.concatenate([one, c, c, one[:, :LANES - KPE_LO - ROPE]], -1)
    sin_m = jnp.concatenate([zero, sn, sn, zero[:, :LANES - KPE_LO - ROPE]], -1)
    return cos_r, sin_r, cos_m, sin_m


def _ret_consts():
    c = RET_CHUNK
    lg = np.log1p(-np.power(2.0, -5.0 - np.arange(N_HEADS, dtype=np.float64)))
    idx = np.arange(c, dtype=np.float64)
    diff = idx[:, None] - idx[None, :]
    lane_head = np.arange(LANES) // HEAD
    dmask = np.zeros((4, 2, c, c))
    zeta = np.zeros((4, c, LANES))
    xi = np.zeros((4, c, LANES))
    cd = np.zeros((4, LANES, LANES))
    bd = (lane_head[:, None] == lane_head[None, :]).astype(np.float64)
    for j in range(4):
        for hh in range(2):
            dmask[j, hh] = np.where(diff >= 0, np.exp(lg[2 * j + hh] * np.maximum(diff, 0.0)), 0.0)
        lgl = lg[2 * j + lane_head]
        zeta[j] = np.exp(lgl[None, :] * (c - 1.0 - idx[:, None]))
        xi[j] = np.exp(lgl[None, :] * (idx[:, None] + 1.0))
        cd[j] = np.exp(lgl * c)[:, None] * bd
    f = lambda a: jnp.asarray(a, F32)
    side = lambda d: np.concatenate([d[:, 0], d[:, 1]], axis=-1)
    return dict(dmask=f(side(dmask)), dmask_t=f(side(np.swapaxes(dmask, 2, 3))), zeta=f(zeta), xi=f(xi), cd=f(cd), bd=f(bd))


def _f1_call(x, anw, win, cos_r, sin_r, cos_m, sin_m, ts):
    s = x.shape[0]

    def body(x_ref, anw_ref, w_ref, cr_ref, sr_ref, cm_ref, sm_ref,
             q_ref, k_ref, v_ref, g_ref, cq_ref, ckv_ref, kpe_ref, r_ref):
        xv = x_ref[...]
        r = _rstd(xv)
        r_ref[...] = r
        h = (xv * r * anw_ref[...]).astype(BF16)
        cr, sr = cr_ref[...], sr_ref[...]
        qk = _dot(h, w_ref[:, 0:2 * RET_W])
        for j in range(4):
            sl = slice(j * LANES, (j + 1) * LANES)
            q_ref[:, sl] = _rope(qk[:, sl], cr, sr, HEAD // 2).astype(BF16)
            kk = qk[:, RET_W + j * LANES:RET_W + (j + 1) * LANES]
            k_ref[:, sl] = (_rope(kk, cr, sr, HEAD // 2) * (HEAD ** -0.5)).astype(BF16)
        v_ref[...] = _dot(h, w_ref[:, 2 * RET_W:3 * RET_W]).astype(BF16)
        g_ref[...] = _dot(h, w_ref[:, 3 * RET_W:4 * RET_W])
        o = 4 * RET_W
        cq_ref[...] = _dot(h, w_ref[:, o:o + Q_RANK])
        ckv_ref[...] = _dot(h, w_ref[:, o + Q_RANK:o + Q_RANK + KV_RANK])
        kp = _dot(h, w_ref[:, o + Q_RANK + KV_RANK:IN_EXT])
        kpe_ref[...] = _rope(kp, cm_ref[...], sm_ref[...], ROPE // 2)

    sd = jax.ShapeDtypeStruct
    return pl.pallas_call(
        body, name="f1_in_proj", grid=(s // ts,),
        in_specs=[_row(ts, D_MODEL), _full((1, D_MODEL)), _full((D_MODEL, IN_EXT)),
                  _row(ts, LANES), _row(ts, LANES), _row(ts, LANES), _row(ts, LANES)],
        out_specs=[_row(ts, RET_W), _row(ts, RET_W), _row(ts, RET_W), _row(ts, RET_W),
                   _row(ts, Q_RANK), _row(ts, KV_RANK), _row(ts, LANES), _row(ts, 1)],
        out_shape=[sd((s, RET_W), BF16), sd((s, RET_W), BF16), sd((s, RET_W), BF16), sd((s, RET_W), F32),
                   sd((s, Q_RANK), F32), sd((s, KV_RANK), F32), sd((s, LANES), F32), sd((s, 1), F32)],
        compiler_params=_cp("parallel"),
    )(x, anw, win, cos_r, sin_r, cos_m, sin_m)


def _stack_heads(a):
    lo = lax.broadcasted_iota(jnp.int32, a.shape, 1) < HEAD
    zero = jnp.zeros_like(a)
    return jnp.concatenate([jnp.where(lo, a, zero), jnp.where(lo, zero, a)], axis=0)


def _pair_product(a, b2, decay2, w2):
    return _dot((_dot_nt(a, b2) * decay2).astype(BF16), w2)


RET_SLABS = 2


def _ret_specs(tr, tile_of):
    c, ns = RET_CHUNK, RET_SLABS
    return dict(
        slab=pl.BlockSpec((tr, ns * LANES), lambda j, i: (tile_of(i), j)),
        tab=pl.BlockSpec((tr, LANES), lambda j, i: (tile_of(i), 0)),
        vec=pl.BlockSpec((1, ns * LANES), lambda j, i: (0, j)),
        dmask=pl.BlockSpec((ns, c, 2 * c), lambda j, i: (j, 0, 0)),
        rows=pl.BlockSpec((ns, c, LANES), lambda j, i: (j, 0, 0)),
        state=pl.BlockSpec((ns, LANES, LANES), lambda j, i: (j, 0, 0)),
        bd=pl.BlockSpec((LANES, LANES), lambda j, i: (0, 0)))


def _ret_states(a_ref, b_ref, scale_ref, cd_ref, bd, st_ref, chunks, lanes, reverse):
    nc = len(chunks)
    contrib = [[_dot_tn((a_ref[rows, ln].astype(F32) * scale_ref[sl]).astype(BF16), b_ref[rows, ln]) * bd for rows in chunks]
               for sl, ln in enumerate(lanes)]
    states = []
    for sl in range(len(lanes)):
        st, seen = st_ref[sl], [None] * nc
        for ci in (reversed(range(nc)) if reverse else range(nc)):
            seen[ci] = st.astype(BF16)
            st = st * cd_ref[sl] + contrib[sl][ci]
        st_ref[sl] = st
        states.append(seen)
    return states


def _ret_fwd_call(q, k, v, g, gnw, rc, tr):
    s = q.shape[0]
    c = RET_CHUNK
    nc = tr // c
    ns = RET_SLABS

    def body(q_ref, k_ref, v_ref, g_ref, gnw_ref, dm_ref, zeta_ref, xi_ref, cd_ref, bd_ref, o_ref, y_ref, st_ref):
        @pl.when(pl.program_id(1) == 0)
        def _():
            st_ref[...] = jnp.zeros_like(st_ref)

        bd = bd_ref[...]
        chunks = [slice(ci * c, (ci + 1) * c) for ci in range(nc)]
        lanes = [slice(sl * LANES, (sl + 1) * LANES) for sl in range(ns)]
        states = _ret_states(k_ref, v_ref, zeta_ref, cd_ref, bd, st_ref, chunks, lanes, False)
        for ci, rows in enumerate(chunks):
            for sl, ln in enumerate(lanes):
                qc = q_ref[rows, ln]
                o_ref[rows, ln] = (_dot(qc, states[sl][ci]) * xi_ref[sl]
                                   + _pair_product(qc, _stack_heads(k_ref[rows, ln]), dm_ref[sl], _stack_heads(v_ref[rows, ln])))
        avg = bd * (1.0 / HEAD)
        for ln in lanes:
            o = o_ref[:, ln]
            ctr = o - _dot_hi(o, avg)
            var = _dot_hi(ctr * ctr, avg)
            y_ref[:, ln] = (_silu(g_ref[:, ln]) * (ctr * lax.rsqrt(var + EPS) * gnw_ref[:, ln])).astype(BF16)

    specs = _ret_specs(tr, lambda i: i)
    sd = jax.ShapeDtypeStruct
    return pl.pallas_call(
        body, name="ret_fwd", grid=(4 // ns, s // tr),
        in_specs=[specs["slab"]] * 4 + [specs["vec"], specs["dmask"], specs["rows"], specs["rows"], specs["state"], specs["bd"]],
        out_specs=[specs["slab"]] * 2,
        out_shape=[sd((s, RET_W), F32), sd((s, RET_W), BF16)],
        scratch_shapes=[pltpu.VMEM((ns, LANES, LANES), F32)],
        compiler_params=_cp("parallel", "arbitrary"),
    )(q, k, v, g, gnw, rc["dmask"], rc["zeta"], rc["xi"], rc["cd"], rc["bd"])


QK_AUX = HEAD + ROPE
V_AUX = HEAD


def _lane_pair(shape, lo, a, b, rest):
    lane = lax.broadcasted_iota(jnp.int32, shape, len(shape) - 1)
    return jnp.where(lane == lo, a, jnp.where(lane == lo + 1, b, rest))


def _hi_lo(v):
    hi = v.astype(BF16).astype(F32)
    return hi, v - hi


def _mla_pre_call(cq, ckv, kpe, qnw, kvnw, wq, wk, wv, cos_m, sin_m, ts):
    s = cq.shape[0]

    def body(cq_ref, ckv_ref, kpe_ref, qnw_ref, kvnw_ref, wq_ref, wk_ref, wv_ref, cm_ref, sm_ref, q_ref, k_ref, v_ref):
        cqv, ckvv = cq_ref[...], ckv_ref[...]
        cqn = (cqv * _rstd(cqv) * qnw_ref[...]).astype(BF16)
        ckvn = (ckvv * _rstd(ckvv) * kvnw_ref[...]).astype(BF16)
        cm, sm = cm_ref[...], sm_ref[...]
        kp = _lane_pair((ts, LANES), QK_AUX, -1.0, -1.0, kpe_ref[...])
        for h in range(N_HEADS):
            qh = _rope(_dot(cqn, wq_ref[h]), cm, sm, ROPE // 2)
            q_ref[h] = (qh * (SM_SCALE * LOG2E)).astype(BF16)
            k_ref[h] = (_dot(ckvn, wk_ref[h]) + kp).astype(BF16)
            v_ref[h] = _lane_pair((ts, LANES), V_AUX, 1.0, 1.0, _dot(ckvn, wv_ref[h])).astype(BF16)

    sd = jax.ShapeDtypeStruct
    hm = sd((N_HEADS, s, LANES), BF16)
    return pl.pallas_call(
        body, name="mla_pre", grid=(s // ts,),
        in_specs=[_row(ts, Q_RANK), _row(ts, KV_RANK), _row(ts, LANES), _full((1, Q_RANK)), _full((1, KV_RANK)),
                  _full((N_HEADS, Q_RANK, LANES)), _full((N_HEADS, KV_RANK, LANES)), _full((N_HEADS, KV_RANK, LANES)),
                  _row(ts, LANES), _row(ts, LANES)],
        out_specs=[_hrow(N_HEADS, ts, LANES)] * 3,
        out_shape=[hm, hm, hm],
        compiler_params=_cp("parallel"),
    )(cq, ckv, kpe, qnw, kvnw, wq, wk, wv, cos_m, sin_m)


def _flash_fwd_call(q, k, v, tb):
    s = q.shape[1]
    nb = s // tb
    pairs = [(a, b) for a in range(nb) for b in range(a + 1)]
    qi_of, ki_of = (jnp.asarray(np.array(col, np.int32)) for col in zip(*pairs))

    def body(qi_ref, ki_ref, q_ref, k_ref, v_ref, o_ref, qb_ref, m_ref, acc_ref):
        qi, ki = qi_ref[pl.program_id(0)], ki_ref[pl.program_id(0)]

        @pl.when(ki == 0)
        def _():
            m_ref[...] = jnp.full_like(m_ref, NEG)
            acc_ref[...] = jnp.zeros_like(acc_ref)

        def step(masked):
            if masked:
                keep = lax.broadcasted_iota(jnp.int32, (tb, tb), 1) <= lax.broadcasted_iota(jnp.int32, (tb, tb), 0)
            def finish(h, pe, alpha):
                acc_ref[h] = acc_ref[h] * alpha + _dot(pe, v_ref[h])

            nxt, pending = _dot_nt(q_ref[0], k_ref[0]), None
            for h in range(N_HEADS):
                sc = nxt
                if h + 1 < N_HEADS:
                    nxt = _dot_nt(q_ref[h + 1], k_ref[h + 1])
                if masked:
                    sc = jnp.where(keep, sc, NEG)
                m_prev = m_ref[h]
                m_new = jnp.maximum(m_prev, jnp.max(sc, axis=1, keepdims=True))
                pe = jnp.exp2(sc - jnp.tile(m_new, (1, tb // LANES))).astype(BF16)
                m_ref[h] = m_new
                if pending is not None:
                    finish(*pending)
                pending = (h, pe, jnp.exp2(m_prev - m_new))
            finish(*pending)

        @pl.when(ki < qi)
        def _():
            step(False)

        @pl.when(ki == qi)
        def _():
            step(True)
            lane = lax.broadcasted_iota(jnp.int32, (tb, LANES), 1)
            for p in range(N_HEADS // 2):
                outs = []
                for h in (2 * p, 2 * p + 1):
                    acc = acc_ref[h]
                    l = acc[:, V_AUX:V_AUX + 1]
                    outs.append(acc * (1.0 / l))
                    hi, lo = _hi_lo(m_ref[h][:, 0:1] + jnp.log(l) * LOG2E)
                    qb_ref[h] = _lane_pair((tb, LANES), QK_AUX, hi, lo, q_ref[h].astype(F32)).astype(BF16)
                o_ref[:, p * LANES:(p + 1) * LANES] = jnp.where(lane < HEAD, outs[0], pltpu.roll(outs[1], HEAD, 1)).astype(BF16)

    sd = jax.ShapeDtypeStruct
    qspec = pl.BlockSpec((N_HEADS, tb, LANES), lambda p, qi_ref, ki_ref: (0, qi_ref[p], 0))
    kspec = pl.BlockSpec((N_HEADS, tb, LANES), lambda p, qi_ref, ki_ref: (0, ki_ref[p], 0))
    return pl.pallas_call(
        body, name="mla_flash_fwd",
        grid_spec=pltpu.PrefetchScalarGridSpec(
            num_scalar_prefetch=2, grid=(len(pairs),),
            in_specs=[qspec, kspec, kspec],
            out_specs=[pl.BlockSpec((tb, MLA_W), lambda p, qi_ref, ki_ref: (qi_ref[p], 0)), qspec],
            scratch_shapes=[pltpu.VMEM((N_HEADS, tb, LANES), F32), pltpu.VMEM((N_HEADS, tb, LANES), F32)]),
        out_shape=[sd((s, MLA_W), BF16), sd((N_HEADS, s, LANES), BF16)],
        compiler_params=_cp("arbitrary"),
    )(qi_of, ki_of, q, k, v)


def _out_proj_call(x, yret, ymla, wout, ts):
    s = x.shape[0]

    def body(x_ref, yr_ref, ym_ref, w_ref, x1_ref, r_ref):
        x1 = x_ref[...] + _dot(yr_ref[...], w_ref[0:RET_W, :]) + _dot(ym_ref[...], w_ref[RET_W:, :])
        x1_ref[...] = x1
        r_ref[...] = _rstd(x1)

    sd = jax.ShapeDtypeStruct
    return pl.pallas_call(
        body, name="out_proj", grid=(s // ts,),
        in_specs=[_row(ts, D_MODEL), _row(ts, RET_W), _row(ts, MLA_W), _full((D_MODEL, D_MODEL))],
        out_specs=[_row(ts, D_MODEL), _row(ts, 1)],
        out_shape=[sd((s, D_MODEL), F32), sd((s, 1), F32)],
        compiler_params=_cp("parallel"),
    )(x, yret, ymla, wout)


W_UP_SHARD = F2 // 4


def _ffn_fwd_call(x1, r2, fnw, wup4, cw, cb, wdown, ts):
    s = x1.shape[0]
    wsh = W_UP_SHARD

    def body(x_ref, r_ref, fnw_ref, wup_ref, cw_ref, cb_ref, wd_ref, u_ref, uc_ref, x2_ref, carry_ref):
        _zero_first(pl.program_id(0) == 0, carry_ref)
        xv = x_ref[...]
        h = (xv * r_ref[...] * fnw_ref[...]).astype(BF16)
        conv = []
        for j in range(4):
            cols = slice(j * wsh, (j + 1) * wsh)
            ub = _dot(h, wup_ref[j]).astype(BF16)
            u_ref[:, cols] = ub
            u = ub.astype(F32)
            u1, u2 = _shifted(u, carry_ref[:, cols])
            w = cw_ref[:, cols]
            cb16 = (cb_ref[:, cols] + w[0:1, :] * u2 + w[1:2, :] * u1 + w[2:3, :] * u).astype(BF16)
            uc_ref[:, cols] = cb16
            conv.append(cb16.astype(F32))
            carry_ref[:, cols] = u[ts - 8:, :]
        acc = xv
        for j in range(2):
            a = (_silu(conv[j]) * conv[j + 2]).astype(BF16)
            acc = acc + _dot(a, wd_ref[j * wsh:(j + 1) * wsh, :])
        x2_ref[...] = acc

    sd = jax.ShapeDtypeStruct
    return pl.pallas_call(
        body, name="ffn_fwd", grid=(s // ts,),
        in_specs=[_row(ts, D_MODEL), _row(ts, 1), _full((1, D_MODEL)), _full((4, D_MODEL, wsh)),
                  _full((3, F2)), _full((1, F2)), _full((D_FF, D_MODEL))],
        out_specs=[_row(ts, F2), _row(ts, F2), _row(ts, D_MODEL)],
        out_shape=[sd((s, F2), BF16), sd((s, F2), BF16), sd((s, D_MODEL), F32)],
        scratch_shapes=[pltpu.VMEM((8, F2), F32)],
        compiler_params=_cp("arbitrary"),
    )(x1, r2, fnw, wup4, cw, cb, wdown)


def _shifted(u, hal):
    row = lax.broadcasted_iota(jnp.int32, hal.shape, 0)
    r1, r2 = pltpu.roll(u, 1, 0), pltpu.roll(u, 2, 0)
    top1 = jnp.where(row == 0, hal[7:8, :], r1[0:8, :])
    top2 = jnp.where(row == 0, hal[6:7, :], jnp.where(row == 1, hal[7:8, :], r2[0:8, :]))
    return jnp.concatenate([top1, r1[8:, :]], axis=0), jnp.concatenate([top2, r2[8:, :]], axis=0)


def _prep_weights(w):
    win = w["w_in"]
    pad = lambda n: jnp.zeros((D_MODEL, n), win.dtype)
    win_ext = jnp.concatenate([win[:, :IN_W - ROPE], pad(KPE_LO), win[:, IN_W - ROPE:], pad(LANES - KPE_LO - ROPE)], -1)
    wuq = w["w_uq"].reshape(Q_RANK, N_HEADS, HEAD + ROPE)
    wq = jnp.concatenate([wuq, jnp.zeros((Q_RANK, N_HEADS, LANES - HEAD - ROPE), wuq.dtype)], -1).transpose(1, 0, 2)
    wukv = w["w_ukv"].reshape(KV_RANK, N_HEADS, 2 * HEAD)
    zk = jnp.zeros((KV_RANK, N_HEADS, HEAD), wukv.dtype)
    wk = jnp.concatenate([wukv[:, :, :HEAD], zk], -1).transpose(1, 0, 2)
    wv = jnp.concatenate([wukv[:, :, HEAD:], zk], -1).transpose(1, 0, 2)
    c = lambda a: a.astype(BF16)
    return dict(win=c(win_ext), wq=c(wq), wk=c(wk), wv=c(wv), wout=c(w["w_out"]))


def _prep_mlp_weights(w):
    wup = w["w_up"]
    if wup.ndim == 2:
        wup = wup.reshape(D_MODEL, 4, W_UP_SHARD).transpose(1, 0, 2)
    return dict(wup=wup.astype(BF16), wdown=w["w_down"].astype(BF16))


def _tiles(s):
    return dict(ts=min(s, 512), tr=min(s, 2048), tbf=min(s, 1024), tb=min(s, 512), t2=min(s, 256),
                tw=min(s, 2048), t1=min(s, 1024))


class _Exchanges:
    def __init__(self, w):
        self.w = w

    def mlp_weights(self, after):
        return self.w

    def mlp_grads(self, gw):
        pass

    def behind_out_bwd(self, after):
        pass

    def behind_attention(self, after):
        pass


def _forward(x, positions, w, small, ex):
    s = x.shape[0]
    t = _tiles(s)
    pw = _prep_weights(w)
    cos_r, sin_r, cos_m, sin_m = _rope_tables(positions)
    rc = _ret_consts()
    q, k, v, g, cq, ckv, kpe, r1 = _f1_call(x, small["attn_norm_w"], pw["win"], cos_r, sin_r, cos_m, sin_m, t["ts"])
    o_ret, y_ret = _ret_fwd_call(q, k, v, g, small["ret_gn_w"], rc, t["tr"])
    mq, mk, mv = _mla_pre_call(cq, ckv, kpe, small["mla_q_norm_w"], small["mla_kv_norm_w"],
                               pw["wq"], pw["wk"], pw["wv"], cos_m, sin_m, t["ts"])
    y_mla, mqb = _flash_fwd_call(mq, mk, mv, t["tbf"])
    x1, r2 = _out_proj_call(x, y_ret, y_mla, pw["wout"], t["ts"])
    pw.update(_prep_mlp_weights(ex.mlp_weights(r2)))
    u, uc, x2 = _ffn_fwd_call(x1, r2, small["ffn_norm_w"], pw["wup"], w["conv_w"], small["conv_b"], pw["wdown"], t["ts"])
    return dict(pw=pw, tabs=(cos_r, sin_r, cos_m, sin_m), rc=rc, q=q, k=k, v=v, g=g, cq=cq, ckv=ckv, kpe=kpe, r1=r1,
                o_ret=o_ret, y_ret=y_ret, mqb=mqb, mk=mk, mv=mv, y_mla=y_mla, x1=x1, r2=r2, u=u, uc=uc, x2=x2)


def _norm_bwd(dh, xh, r, nw):
    dxn = dh * nw
    return r * (dxn - xh * jnp.mean(dxn * xh, axis=-1, keepdims=True))


def _ordered_after(body, order):
    if order is None:
        return body, [], []
    return (lambda order_ref, *refs: body(*refs)), [pl.BlockSpec(memory_space=pl.ANY)], [order]


def _zero_first(first, *refs):
    @pl.when(first)
    def _():
        for ref in refs:
            ref[...] = jnp.zeros_like(ref)


def _colsum(v):
    return jnp.sum(v, axis=0, keepdims=True)


def _dsilu(g, sg):
    return sg * (1.0 + g * (1.0 - sg))


def _loss_call(x2, tgt, fw, ts):
    s = x2.shape[0]

    def body(x_ref, t_ref, fw_ref, dx_ref, loss_ref, gfw_ref):
        _zero_first(pl.program_id(0) == 0, loss_ref, gfw_ref)
        xv = x_ref[...]
        r = _rstd(xv)
        xh = xv * r
        fwv = fw_ref[...]
        e = xh * fwv - t_ref[...]
        loss_ref[...] += (0.5 / D_MODEL) * _colsum(jnp.sum(e * e, axis=1, keepdims=True))
        dy = e * (1.0 / D_MODEL)
        gfw_ref[...] += _colsum(dy * xh)
        dx_ref[...] = _norm_bwd(dy, xh, r, fwv)

    sd = jax.ShapeDtypeStruct
    return pl.pallas_call(
        body, name="loss_bwd", grid=(s // ts,),
        in_specs=[_row(ts, D_MODEL), _row(ts, D_MODEL), _full((1, D_MODEL))],
        out_specs=[_row(ts, D_MODEL), _full((1, 1)), _full((1, D_MODEL))],
        out_shape=[sd((s, D_MODEL), F32), sd((1, 1), F32), sd((1, D_MODEL), F32)],
        compiler_params=_cp("arbitrary"),
    )(x2, tgt, fw)


def _ffn_bwd_call(dx2, u, uc, cw, wdown, wup4, x1, r2, fnw, ts):
    s = dx2.shape[0]
    nt = s // ts
    wsh = W_UP_SHARD
    rev = lambda i: nt - 1 - i

    def body(dx2_ref, u_ref, uc_ref, cw_ref, wd_ref, wup_ref, x_ref, r_ref, fnw_ref,
             du_ref, dx1_ref, dcw_ref, dcb_ref, dfnw_ref, dwd_hbm, carry_ref, dwd_ref, sem):
        i = pl.program_id(0)
        _zero_first(i == 0, carry_ref, dwd_ref, dcw_ref, dcb_ref, dfnw_ref)
        dxb = dx2_ref[...].astype(BF16)
        dh = jnp.zeros((ts, D_MODEL), F32)
        for j in range(2):
            gcols = slice(j * wsh, (j + 1) * wsh)
            vcols = slice(D_FF + j * wsh, D_FF + (j + 1) * wsh)
            gate, val = uc_ref[:, gcols].astype(F32), uc_ref[:, vcols].astype(F32)
            da = _dot_nt(dxb, wd_ref[gcols, :])
            sg = _sigmoid(gate)
            sl = gate * sg
            dwd_ref[gcols, :] += _dot_tn((sl * val).astype(BF16), dxb)
            for d, cols, shard in ((da * val * _dsilu(gate, sg), gcols, j), (da * sl, vcols, 2 + j)):
                d1, d2 = _shifted_up(d, carry_ref[:, cols])
                uv = u_ref[:, cols].astype(F32)
                for t, dt in enumerate((d2, d1, d)):
                    dcw_ref[t:t + 1, cols] += _colsum(dt * uv)
                dcb_ref[:, cols] += _colsum(d)
                w = cw_ref[:, cols]
                du = (w[2:3, :] * d + w[1:2, :] * d1 + w[0:1, :] * d2).astype(BF16)
                du_ref[:, cols] = du
                dh = dh + _dot_nt(du, wup_ref[shard])
                carry_ref[:, cols] = d[0:8, :]
        r = r_ref[...]
        xh = x_ref[...] * r
        dfnw_ref[...] += _colsum(dh * xh)
        dx1_ref[...] = dx2_ref[...] + _norm_bwd(dh, xh, r, fnw_ref[...])

        @pl.when(i == nt - 1)
        def _():
            cp = pltpu.make_async_copy(dwd_ref, dwd_hbm, sem)
            cp.start()
            cp.wait()

    sd = jax.ShapeDtypeStruct
    row = lambda c: pl.BlockSpec((ts, c), lambda i: (rev(i), 0))
    once = lambda shape: pl.BlockSpec(shape, lambda i: (0,) * len(shape), pipeline_mode=pl.Buffered(1))
    return pl.pallas_call(
        body, name="ffn_bwd", grid=(nt,),
        in_specs=[row(D_MODEL), row(F2), row(F2), once((3, F2)), once((D_FF, D_MODEL)), once((4, D_MODEL, wsh)),
                  row(D_MODEL), row(1), once((1, D_MODEL))],
        out_specs=[row(F2), row(D_MODEL), _full((3, F2)), _full((1, F2)), _full((1, D_MODEL)), pl.BlockSpec(memory_space=pl.ANY)],
        out_shape=[sd((s, F2), BF16), sd((s, D_MODEL), F32), sd((3, F2), F32), sd((1, F2), F32), sd((1, D_MODEL), F32),
                   sd((D_FF, D_MODEL), F32)],
        scratch_shapes=[pltpu.VMEM((8, F2), F32), pltpu.VMEM((D_FF, D_MODEL), F32), pltpu.SemaphoreType.DMA],
        compiler_params=_cp("arbitrary", vmem=VMEM_LIMIT_MLP_BWD),
    )(dx2, u, uc, cw, wdown, wup4, x1, r2, fnw)


def _shifted_up(d, hal):
    n = d.shape[0]
    row = lax.broadcasted_iota(jnp.int32, hal.shape, 0)
    r1, r2 = pltpu.roll(d, n - 1, 0), pltpu.roll(d, n - 2, 0)
    end1 = jnp.where(row == 7, hal[0:1, :], r1[n - 8:, :])
    end2 = jnp.where(row == 6, hal[0:1, :], jnp.where(row == 7, hal[1:2, :], r2[n - 8:, :]))
    return jnp.concatenate([r1[:n - 8, :], end1], axis=0), jnp.concatenate([r2[:n - 8, :], end2], axis=0)


def _dw_norm_call(x, r, nw, b, ts, tn, name):
    s, n = b.shape
    k = x.shape[1]

    def body(x_ref, r_ref, nw_ref, b_ref, dw_ref):
        _zero_first(pl.program_id(1) == 0, dw_ref)
        h = (x_ref[...] * r_ref[...] * nw_ref[...]).astype(BF16)
        dw_ref[...] += _dot_tn(h, b_ref[...])

    return pl.pallas_call(
        body, name=name, grid=(n // tn, s // ts),
        in_specs=[pl.BlockSpec((ts, k), lambda j, i: (i, 0)), pl.BlockSpec((ts, 1), lambda j, i: (i, 0)),
                  pl.BlockSpec((1, k), lambda j, i: (0, 0)), pl.BlockSpec((ts, tn), lambda j, i: (i, j))],
        out_specs=pl.BlockSpec((None, k, tn), lambda j, i: (j, 0, 0)),
        out_shape=jax.ShapeDtypeStruct((n // tn, k, tn), F32),
        compiler_params=_cp("parallel", "arbitrary"),
    )(x, r, nw, b)


def _out_bwd_call(dx1, yret, ymla, wout, ts, order=None):
    s = dx1.shape[0]

    def body(dx_ref, yr_ref, ym_ref, w_ref, dyr_ref, do_ref, dwo_ref):
        _zero_first(pl.program_id(0) == 0, dwo_ref)
        dxb = dx_ref[...].astype(BF16)
        dmix = _dot_nt(dxb, w_ref[...])
        dyr_ref[...] = dmix[:, :RET_W]
        ym = ym_ref[...]
        lane = lax.broadcasted_iota(jnp.int32, (ts, LANES), 1)
        for p in range(N_HEADS // 2):
            dom = dmix[:, RET_W + p * LANES:RET_W + (p + 1) * LANES]
            prod = dom * ym[:, p * LANES:(p + 1) * LANES].astype(F32)
            for hh in range(2):
                mine = (lane >= HEAD) if hh else (lane < HEAD)
                hi, lo = _hi_lo(jnp.sum(jnp.where(mine, prod, 0.0), axis=1, keepdims=True))
                base = jnp.where(lane < HEAD, pltpu.roll(dom, HEAD, 1) if hh else dom, 0.0)
                do_ref[2 * p + hh] = _lane_pair((ts, LANES), V_AUX, -hi, -lo, base).astype(BF16)
        dwo_ref[0:RET_W, :] += _dot_tn(yr_ref[...], dxb)
        dwo_ref[RET_W:, :] += _dot_tn(ym, dxb)

    sd = jax.ShapeDtypeStruct
    body, first_specs, first = _ordered_after(body, order)
    return pl.pallas_call(
        body, name="out_proj_bwd", grid=(s // ts,),
        in_specs=first_specs + [_row(ts, D_MODEL), _row(ts, RET_W), _row(ts, MLA_W), _full((D_MODEL, D_MODEL))],
        out_specs=[_row(ts, RET_W), _hrow(N_HEADS, ts, LANES), _full((D_MODEL, D_MODEL))],
        out_shape=[sd((s, RET_W), F32), sd((N_HEADS, s, LANES), BF16), sd((D_MODEL, D_MODEL), F32)],
        compiler_params=_cp("arbitrary"),
    )(*first, dx1, yret, ymla, wout)


def _ret_bwd_q_call(q, k, v, o, g, dy, gnw, rc, cos_r, sin_r, tr):
    s = q.shape[0]
    c = RET_CHUNK
    nc = tr // c
    ns = RET_SLABS

    def body(q_ref, k_ref, v_ref, o_ref, g_ref, dy_ref, gnw_ref, dm_ref, zeta_ref, xi_ref, cd_ref, bd_ref, cr_ref, sr_ref,
             dq_ref, dg_ref, do_ref, dgnw_ref, st_ref):
        _zero_first(pl.program_id(1) == 0, st_ref, dgnw_ref)
        bd = bd_ref[...]
        avg = bd * (1.0 / HEAD)
        chunks = [slice(ci * c, (ci + 1) * c) for ci in range(nc)]
        lanes = [slice(sl * LANES, (sl + 1) * LANES) for sl in range(ns)]
        dov = []
        for ln in lanes:
            ov = o_ref[:, ln]
            ctr = ov - _dot_hi(ov, avg)
            rs = lax.rsqrt(_dot_hi(ctr * ctr, avg) + EPS)
            oh = ctr * rs
            gg, dyv, gnw_v = g_ref[:, ln], dy_ref[:, ln], gnw_ref[:, ln]
            sg = _sigmoid(gg)
            sl = gg * sg
            dg_ref[:, ln] = (dyv * oh * gnw_v * _dsilu(gg, sg)).astype(BF16)
            dgnw_ref[:, ln] += _colsum(dyv * sl * oh)
            doh = dyv * sl * gnw_v
            dov.append((rs * (doh - _dot_hi(doh, avg) - oh * _dot_hi(doh * oh, avg))).astype(BF16))
            do_ref[:, ln] = dov[-1]
        states = _ret_states(k_ref, v_ref, zeta_ref, cd_ref, bd, st_ref, chunks, lanes, False)
        for ci, rows in enumerate(chunks):
            for sl, ln in enumerate(lanes):
                doc = dov[sl][rows, :]
                dq = (_dot_nt(doc, states[sl][ci]) * xi_ref[sl]
                      + _pair_product(doc, _stack_heads(v_ref[rows, ln]), dm_ref[sl], _stack_heads(k_ref[rows, ln])))
                dq_ref[rows, ln] = _unrope(dq, cr_ref[rows, :], sr_ref[rows, :], HEAD // 2).astype(BF16)

    specs = _ret_specs(tr, lambda i: i)
    sd = jax.ShapeDtypeStruct
    return pl.pallas_call(
        body, name="ret_bwd_q", grid=(4 // ns, s // tr),
        in_specs=[specs["slab"]] * 6 + [specs["vec"], specs["dmask"], specs["rows"], specs["rows"], specs["state"], specs["bd"],
                                        specs["tab"], specs["tab"]],
        out_specs=[specs["slab"]] * 3 + [specs["vec"]],
        out_shape=[sd((s, RET_W), BF16), sd((s, RET_W), BF16), sd((s, RET_W), BF16), sd((1, RET_W), F32)],
        scratch_shapes=[pltpu.VMEM((ns, LANES, LANES), F32)],
        compiler_params=_cp("parallel", "arbitrary"),
    )(q, k, v, o, g, dy, gnw, rc["dmask"], rc["zeta"], rc["xi"], rc["cd"], rc["bd"], cos_r, sin_r)


def _ret_bwd_kv_call(q, k, v, do, rc, cos_r, sin_r, tr):
    s = q.shape[0]
    c = RET_CHUNK
    nc = tr // c
    nt = s // tr
    ns = RET_SLABS

    def body(q_ref, k_ref, v_ref, do_ref, dm_ref, zeta_ref, xi_ref, cd_ref, bd_ref, cr_ref, sr_ref, dk_ref, dv_ref, gs_ref):
        _zero_first(pl.program_id(1) == 0, gs_ref)
        bd = bd_ref[...]
        chunks = [slice(ci * c, (ci + 1) * c) for ci in range(nc)]
        lanes = [slice(sl * LANES, (sl + 1) * LANES) for sl in range(ns)]
        states = _ret_states(q_ref, do_ref, xi_ref, cd_ref, bd, gs_ref, chunks, lanes, True)
        for ci, rows in enumerate(chunks):
            for sl, ln in enumerate(lanes):
                kc, vc = k_ref[rows, ln], v_ref[rows, ln]
                q2, do2 = _stack_heads(q_ref[rows, ln]), _stack_heads(do_ref[rows, ln])
                gb = states[sl][ci]
                dk = _dot_nt(vc, gb) * zeta_ref[sl] + _pair_product(vc, do2, dm_ref[sl], q2)
                dv = _dot(kc, gb) * zeta_ref[sl] + _pair_product(kc, q2, dm_ref[sl], do2)
                dk_ref[rows, ln] = (_unrope(dk, cr_ref[rows, :], sr_ref[rows, :], HEAD // 2) * (HEAD ** -0.5)).astype(BF16)
                dv_ref[rows, ln] = dv.astype(BF16)

    specs = _ret_specs(tr, lambda i: nt - 1 - i)
    sd = jax.ShapeDtypeStruct
    return pl.pallas_call(
        body, name="ret_bwd_kv", grid=(4 // ns, nt),
        in_specs=[specs["slab"]] * 4 + [specs["dmask"], specs["rows"], specs["rows"], specs["state"], specs["bd"],
                                        specs["tab"], specs["tab"]],
        out_specs=[specs["slab"]] * 2,
        out_shape=[sd((s, RET_W), BF16), sd((s, RET_W), BF16)],
        scratch_shapes=[pltpu.VMEM((ns, LANES, LANES), F32)],
        compiler_params=_cp("parallel", "arbitrary"),
    )(q, k, v, do, rc["dmask_t"], rc["zeta"], rc["xi"], rc["cd"], rc["bd"], cos_r, sin_r)


FLASH_BWD_HEADS = 8


def _flash_bwd_call(qb, k, v, do, tb, order=None):
    s = qb.shape[1]
    nb = s // tb
    hg = FLASH_BWD_HEADS
    pairs = [(a, b) for a in range(nb) for b in range(a, nb)]
    ki_of, qi_of = (jnp.asarray(np.array(col, np.int32)) for col in zip(*pairs))
    extra = [] if order is None else [order]

    def body(ki_ref, qi_ref, *refs):
        q_ref, k_ref, v_ref, do_ref, dk_ref, dv_ref, dq_hbm, dka_ref, dva_ref, dq_ref, sem = refs[len(extra):]
        g, p = pl.program_id(0), pl.program_id(1)
        ki, qi = ki_ref[p], qi_ref[p]
        _zero_first(p == 0, dq_ref)
        _zero_first(qi == ki, dka_ref, dva_ref)
        rows = pl.ds(pl.multiple_of(qi * tb, tb), tb)

        def step(masked):
            if masked:
                keep = lax.broadcasted_iota(jnp.int32, (tb, tb), 0) <= lax.broadcasted_iota(jnp.int32, (tb, tb), 1)
            for h in range(hg):
                st = _dot_nt(k_ref[h], q_ref[h])
                if masked:
                    st = jnp.where(keep, st, NEG)
                pt = jnp.exp2(st)
                dob = do_ref[h]
                dva_ref[h] += _dot(pt.astype(BF16), dob)
                dst = (pt * _dot_nt(v_ref[h], dob)).astype(BF16)
                dka_ref[h] += _dot(dst, q_ref[h])
                dq_ref[h, rows, :] += _dot_tn(dst, k_ref[h])

        @pl.when(qi > ki)
        def _():
            step(False)

        @pl.when(qi == ki)
        def _():
            step(True)

        @pl.when(qi == nb - 1)
        def _():
            dk_ref[...] = (dka_ref[...] * LN2).astype(BF16)
            dv_ref[...] = dva_ref[...].astype(BF16)

        @pl.when(p == len(pairs) - 1)
        def _():
            cp = pltpu.make_async_copy(dq_ref, dq_hbm.at[pl.ds(g * hg, hg)], sem)
            cp.start()
            cp.wait()

    kspec = pl.BlockSpec((hg, tb, LANES), lambda g, p, ki_ref, qi_ref: (g, ki_ref[p], 0))
    qspec = pl.BlockSpec((hg, tb, LANES), lambda g, p, ki_ref, qi_ref: (g, qi_ref[p], 0))
    hm = jax.ShapeDtypeStruct((N_HEADS, s, LANES), BF16)
    return pl.pallas_call(
        body, name="mla_flash_bwd",
        grid_spec=pltpu.PrefetchScalarGridSpec(
            num_scalar_prefetch=2, grid=(N_HEADS // hg, len(pairs)),
            in_specs=[ANY] * len(extra) + [qspec, kspec, kspec, qspec],
            out_specs=[kspec, kspec, ANY],
            scratch_shapes=[pltpu.VMEM((hg, tb, LANES), F32), pltpu.VMEM((hg, tb, LANES), F32),
                            pltpu.VMEM((hg, s, LANES), F32), pltpu.SemaphoreType.DMA]),
        out_shape=[hm, hm, jax.ShapeDtypeStruct((N_HEADS, s, LANES), F32)],
        compiler_params=_cp("arbitrary", "arbitrary"),
    )(ki_of, qi_of, *extra, qb, k, v, do)


def _mla_post_call(dq, dk, dv, cq, ckv, qnw, kvnw, wq, wk, wv, cos_m, sin_m, ts):
    s = cq.shape[0]

    def body(dq_ref, dk_ref, dv_ref, cq_ref, ckv_ref, qnw_ref, kvnw_ref, wq_ref, wk_ref, wv_ref, cm_ref, sm_ref,
             dcq_ref, dckv_ref, dkpe_ref, dwq_ref, dwk_ref, dwv_ref, dqnw_ref, dkvnw_ref):
        _zero_first(pl.program_id(0) == 0, dwq_ref, dwk_ref, dwv_ref, dqnw_ref, dkvnw_ref)
        cqv, ckvv = cq_ref[...], ckv_ref[...]
        rq, rkv = _rstd(cqv), _rstd(ckvv)
        qh_, kvh_ = cqv * rq, ckvv * rkv
        qnw_v, kvnw_v = qnw_ref[...], kvnw_ref[...]
        cqn = (qh_ * qnw_v).astype(BF16)
        ckvn = (kvh_ * kvnw_v).astype(BF16)
        cm, sm = cm_ref[...], sm_ref[...]
        dcqn = jnp.zeros((ts, Q_RANK), F32)
        dckvn = jnp.zeros((ts, KV_RANK), F32)
        dkpe = jnp.zeros((ts, LANES), F32)
        for h in range(N_HEADS):
            dqu = _unrope(dq_ref[h] * SM_SCALE, cm, sm, ROPE // 2).astype(BF16)
            dwq_ref[h] += _dot_tn(cqn, dqu)
            dcqn = dcqn + _dot_nt(dqu, wq_ref[h])
            dkb, dvb = dk_ref[h], dv_ref[h]
            dkpe = dkpe + dkb.astype(F32)
            dwk_ref[h] += _dot_tn(ckvn, dkb)
            dwv_ref[h] += _dot_tn(ckvn, dvb)
            dckvn = dckvn + _dot_nt(dkb, wk_ref[h]) + _dot_nt(dvb, wv_ref[h])
        lane = lax.broadcasted_iota(jnp.int32, (ts, LANES), 1)
        dkpe = jnp.where((lane >= KPE_LO) & (lane < KPE_LO + ROPE), dkpe, 0.0)
        dkpe_ref[...] = _unrope(dkpe, cm, sm, ROPE // 2).astype(BF16)
        dqnw_ref[...] += _colsum(dcqn * qh_)
        dkvnw_ref[...] += _colsum(dckvn * kvh_)
        dcq_ref[...] = _norm_bwd(dcqn, qh_, rq, qnw_v).astype(BF16)
        dckv_ref[...] = _norm_bwd(dckvn, kvh_, rkv, kvnw_v).astype(BF16)

    sd = jax.ShapeDtypeStruct
    hm = _hrow(N_HEADS, ts, LANES)
    return pl.pallas_call(
        body, name="mla_post", grid=(s // ts,),
        in_specs=[hm, hm, hm, _row(ts, Q_RANK), _row(ts, KV_RANK), _full((1, Q_RANK)), _full((1, KV_RANK)),
                  _full((N_HEADS, Q_RANK, LANES)), _full((N_HEADS, KV_RANK, LANES)), _full((N_HEADS, KV_RANK, LANES)),
                  _row(ts, LANES), _row(ts, LANES)],
        out_specs=[_row(ts, Q_RANK), _row(ts, KV_RANK), _row(ts, LANES),
                   _full((N_HEADS, Q_RANK, LANES)), _full((N_HEADS, KV_RANK, LANES)), _full((N_HEADS, KV_RANK, LANES)),
                   _full((1, Q_RANK)), _full((1, KV_RANK))],
        out_shape=[sd((s, Q_RANK), BF16), sd((s, KV_RANK), BF16), sd((s, LANES), BF16),
                   sd((N_HEADS, Q_RANK, LANES), F32), sd((N_HEADS, KV_RANK, LANES), F32), sd((N_HEADS, KV_RANK, LANES), F32),
                   sd((1, Q_RANK), F32), sd((1, KV_RANK), F32)],
        compiler_params=_cp("arbitrary"),
    )(dq, dk, dv, cq, ckv, qnw, kvnw, wq, wk, wv, cos_m, sin_m)


def _in_bwd_call(parts, x, r1, anw, dx1, win, ts):
    s = x.shape[0]
    widths = [p.shape[1] for p in parts]
    np_ = len(parts)

    def body(*refs):
        p_refs = refs[:np_]
        x_ref, r_ref, anw_ref, dx1_ref, w_ref, dx_ref, dw_ref, danw_ref = refs[np_:]
        _zero_first(pl.program_id(0) == 0, dw_ref, danw_ref)
        dproj = jnp.concatenate([p[...] for p in p_refs], axis=-1)
        r, anw_v = r_ref[...], anw_ref[...]
        xh = x_ref[...] * r
        dw_ref[...] += _dot_tn((xh * anw_v).astype(BF16), dproj)
        dh = _dot_nt(dproj, w_ref[...])
        danw_ref[...] += _colsum(dh * xh)
        dx_ref[...] = dx1_ref[...] + _norm_bwd(dh, xh, r, anw_v)

    sd = jax.ShapeDtypeStruct
    return pl.pallas_call(
        body, name="in_proj_bwd", grid=(s // ts,),
        in_specs=[_row(ts, w) for w in widths]
        + [_row(ts, D_MODEL), _row(ts, 1), _full((1, D_MODEL)), _row(ts, D_MODEL), _full((D_MODEL, IN_EXT))],
        out_specs=[_row(ts, D_MODEL), _full((D_MODEL, IN_EXT)), _full((1, D_MODEL))],
        out_shape=[sd((s, D_MODEL), F32), sd((D_MODEL, IN_EXT), F32), sd((1, D_MODEL), F32)],
        compiler_params=_cp("arbitrary"),
    )(*parts, x, r1, anw, dx1, win)


def _local_step(x, positions, tgt, w, small, ex=None):
    s = x.shape[0]
    t = _tiles(s)
    ex = _Exchanges(w) if ex is None else ex
    f = _forward(x, positions, w, small, ex)
    pw, rc = f["pw"], f["rc"]
    cos_r, sin_r, cos_m, sin_m = f["tabs"]
    dx2, loss, g_fw = _loss_call(f["x2"], tgt, small["final_norm_w"], t["ts"])
    du, dx1, g_cw, g_cb, g_fnw, g_wd = _ffn_bwd_call(dx2, f["u"], f["uc"], w["conv_w"], pw["wdown"], pw["wup"],
                                                     f["x1"], f["r2"], small["ffn_norm_w"], t["t2"])
    g_wup = _dw_norm_call(f["x1"], f["r2"], small["ffn_norm_w"], du, t["tw"], F2 // 4, "dw_up")
    started = ex.mlp_grads(dict(w_up=g_wup, w_down=g_wd))
    dy_ret, do, g_wout = _out_bwd_call(dx1, f["y_ret"], f["y_mla"], pw["wout"], t["t1"], started)
    started = ex.behind_out_bwd(g_wout)
    drq, dg, do_ret, g_gnw = _ret_bwd_q_call(f["q"], f["k"], f["v"], f["o_ret"], f["g"], dy_ret, small["ret_gn_w"], rc, cos_r, sin_r, t["tr"])
    drk, drv = _ret_bwd_kv_call(f["q"], f["k"], f["v"], do_ret, rc, cos_r, sin_r, t["tr"])
    dmk, dmv, dmq = _flash_bwd_call(f["mqb"], f["mk"], f["mv"], do, t["tb"], started)
    ex.behind_attention(dmk)
    dcq, dckv, dkpe, g_wq, g_wk, g_wv, g_qnw, g_kvnw = _mla_post_call(
        dmq, dmk, dmv, f["cq"], f["ckv"], small["mla_q_norm_w"], small["mla_kv_norm_w"], pw["wq"], pw["wk"], pw["wv"], cos_m, sin_m, t["ts"])
    gx, g_win_ext, g_anw = _in_bwd_call([drq, drk, drv, dg, dcq, dckv, dkpe], x, f["r1"], small["attn_norm_w"], dx1, pw["win"], t["ts"])
    lo = IN_W - ROPE
    g_win = jnp.concatenate([g_win_ext[:, :lo], g_win_ext[:, lo + KPE_LO:lo + KPE_LO + ROPE]], -1)
    g_wuq = g_wq.transpose(1, 0, 2)[:, :, :HEAD + ROPE].reshape(Q_RANK, N_HEADS * (HEAD + ROPE))
    g_wukv = jnp.concatenate([g_wk[:, :, :HEAD], g_wv[:, :, :HEAD]], -1).transpose(1, 0, 2).reshape(KV_RANK, 2 * MLA_W)
    gw = dict(w_in=g_win, w_uq=g_wuq, w_ukv=g_wukv, w_out=g_wout, w_up=g_wup,
              conv_w=g_cw, w_down=g_wd)
    gs = dict(attn_norm_w=g_anw, ret_gn_w=g_gnw, mla_q_norm_w=g_qnw, mla_kv_norm_w=g_kvnw, ffn_norm_w=g_fnw,
              conv_b=g_cb, final_norm_w=g_fw)
    return loss, gx, gw, gs


MESH_ID = pl.DeviceIdType.MESH
ANY = pl.BlockSpec(memory_space=pl.ANY)
VMEM_SPEC = pl.BlockSpec(memory_space=pltpu.VMEM)
N_DEV = 8
GROUP_A = (("w_in", (D_MODEL, IN_W // 4), 1), ("w_uq", (Q_RANK, 192), 1), ("w_ukv", (KV_RANK, 256), 1),
           ("w_out", (D_MODEL // 4, D_MODEL), 0))
GROUP_B = (("w_up", (D_MODEL, F2 // 4), 1), ("w_down", (D_FF // 4, D_MODEL), 0))
HBM_SPEC = pl.BlockSpec(memory_space=pltpu.HBM)
SEM_SPEC = pl.BlockSpec(memory_space=pltpu.SEMAPHORE)


def _mesh_pos():
    return lax.axis_index("x"), lax.axis_index("y"), lax.axis_index("c")


def _other_chips(x, y):
    return [(1 - x, y), (x, 1 - y), (1 - x, 1 - y)]


def _remote(src, dst, send_sems, recv_sems, k, dev):
    return pltpu.make_async_remote_copy(src_ref=src, dst_ref=dst, send_sem=send_sems.at[k], recv_sem=recv_sems.at[k],
                                        device_id=dev, device_id_type=MESH_ID)


def _gather_list_call(parts, tag):
    n = len(parts)

    def body(*refs):
        srcs, outs, (send_sems, recv_sems) = refs[:n], refs[n:2 * n], refs[2 * n:]
        x, y, c = _mesh_pos()
        sm = 2 * x + y
        chips = _other_chips(x, y)
        sib = (x, y, 1 - c)
        rc = lambda k, src, dst, dev: _remote(src, dst, send_sems, recv_sems, k, dev)
        first = [rc(7 * i + j, srcs[i].at[c], outs[i].at[sm, c], (cx, cy, c)) for i in range(n) for j, (cx, cy) in enumerate(chips)]
        own = [rc(7 * i + 6, srcs[i], outs[i].at[sm], sib) for i in range(n)]
        for cp in first + own:
            cp.start()
        passed = []
        for j, (cx, cy) in enumerate(chips):
            for i in range(n):
                land = outs[i].at[2 * cx + cy, c]
                rc(7 * i + j, srcs[i].at[c], land, (cx, cy, c)).wait_recv()
                cp = rc(7 * i + 3 + j, land, land, sib)
                cp.start()
                passed.append(cp)
        for j, (cx, cy) in enumerate(chips):
            for i in range(n):
                rc(7 * i + 3 + j, srcs[i].at[c], outs[i].at[2 * cx + cy, 1 - c], sib).wait_recv()
        for cp in own:
            cp.wait_recv()
        for cp in first + passed + own:
            cp.wait_send()

    return pl.pallas_call(
        body, name="weights_all_gather_" + tag,
        in_specs=[ANY] * n, out_specs=[ANY] * n,
        out_shape=[jax.ShapeDtypeStruct((4,) + p.shape, p.dtype) for p in parts],
        scratch_shapes=[pltpu.SemaphoreType.DMA((7 * n,)), pltpu.SemaphoreType.DMA((7 * n,))],
    )(*parts)


def _direct_gather_copies(srcs, lands, send_sems, recv_sems):
    x, y, c = _mesh_pos()
    sm = 2 * x + y
    sends, recvs = [], []
    for i, (src, land) in enumerate(zip(srcs, lands)):
        for j, (cx, cy) in enumerate(_other_chips(x, y)):
            for t in range(2):
                sends.append(_remote(src.at[c], land.at[sm, c], send_sems, recv_sems, 13 * i + 4 * j + 2 * c + t, (cx, cy, t)))
                recvs.append(_remote(src.at[t], land.at[2 * cx + cy, t], send_sems, recv_sems, 13 * i + 4 * j + 2 * t + c, (cx, cy, t)))
        sends.append(_remote(src, land.at[sm], send_sems, recv_sems, 13 * i + 12, (x, y, 1 - c)))
        recvs.append(_remote(src, land.at[sm], send_sems, recv_sems, 13 * i + 12, (x, y, 1 - c)))
    return sends, recvs


def _sibling_copies(srcs, lands, send_sems, recv_sems):
    x, y, c = _mesh_pos()
    cps = [_remote(src.at[s, 1 - c], land.at[s], send_sems, recv_sems, 4 * i + s, (x, y, 1 - c))
           for i, (src, land) in enumerate(zip(srcs, lands)) for s in range(4)]
    return cps, cps


def _chips_copies(srcs, lands, send_sems, recv_sems):
    x, y, c = _mesh_pos()
    cps = [_remote(src.at[2 * cx + cy], land.at[j], send_sems, recv_sems, 3 * i + j, (cx, cy, c))
           for i, (src, land) in enumerate(zip(srcs, lands)) for j, (cx, cy) in enumerate(_other_chips(x, y))]
    return cps, cps


def _share_copies(srcs, lands, send_sems, recv_sems):
    x, y, c = _mesh_pos()
    cps = [_remote(src, land, send_sems, recv_sems, i, (x, y, 1 - c)) for i, (src, land) in enumerate(zip(srcs, lands))]
    return cps, cps


def _exchange_call(name, copies, srcs, land_shapes, n_sems):
    n = len(srcs)

    def body(*refs):
        sends, recvs = copies(refs[:n], refs[n:2 * n], refs[2 * n], refs[2 * n + 1])
        for cp in sends:
            cp.start()
        for cp in sends:
            cp.wait_send()
        for cp in recvs:
            cp.wait_recv()

    return pl.pallas_call(
        body, name=name, in_specs=[ANY] * n, out_specs=[ANY] * n, out_shape=list(land_shapes),
        scratch_shapes=[pltpu.SemaphoreType.DMA((n_sems,)), pltpu.SemaphoreType.DMA((n_sems,))],
    )(*srcs)


def _exchange_start_call(name, copies, srcs, land_shapes, n_sems, order=None):
    n = len(srcs)
    extra = [] if order is None else [order]
    k = 2 * n + len(extra)

    def body(*refs):
        sends, _ = copies(refs[:n], refs[n:2 * n], refs[k], refs[k + 1])
        for cp in sends:
            cp.start()
        refs[-1][...] = jnp.zeros_like(refs[-1])

    hbm = lambda a: pltpu.with_memory_space_constraint(a, pltpu.HBM)
    lands = [hbm(lax.empty(sd.shape, sd.dtype)) for sd in land_shapes]
    sem = pltpu.SemaphoreType.DMA((n_sems,))
    out = pl.pallas_call(
        body, name=name,
        out_shape=(sem, sem, *[pltpu.HBM(a.shape, a.dtype) for a in list(srcs) + lands], jax.ShapeDtypeStruct((8, LANES), F32)),
        in_specs=[HBM_SPEC] * (2 * n) + [ANY] * len(extra), out_specs=(SEM_SPEC, SEM_SPEC, *[HBM_SPEC] * (2 * n), VMEM_SPEC),
        input_output_aliases={i: 2 + i for i in range(2 * n)},
        compiler_params=pltpu.CompilerParams(has_side_effects=pltpu.SideEffectType.DATAFLOW_SIDE_EFFECTING),
    )(*[hbm(a) for a in srcs], *lands, *extra)
    return out[0], out[1], out[2:2 + n], out[2 + n:2 + 2 * n], out[-1]


def _exchange_wait_call(name, copies, started, after):
    send_sems, recv_sems, srcs, lands, _ = started
    n = len(srcs)

    def body(*refs):
        sends, recvs = copies(refs[:n], refs[n:2 * n], refs[2 * n], refs[2 * n + 1])
        for cp in sends:
            cp.wait_send()
        for cp in recvs:
            cp.wait_recv()

    out = pl.pallas_call(
        body, name=name,
        out_shape=tuple(pltpu.HBM(a.shape, a.dtype) for a in list(srcs) + list(lands)),
        in_specs=[HBM_SPEC] * (2 * n) + [SEM_SPEC, SEM_SPEC, ANY], out_specs=tuple([HBM_SPEC] * (2 * n)),
        input_output_aliases={i: i for i in range(2 * n)},
        compiler_params=pltpu.CompilerParams(has_side_effects=pltpu.SideEffectType.DATAFLOW_SIDE_EFFECTING),
    )(*srcs, *lands, send_sems, recv_sems, after)
    return out[:n], out[n:]


def _rows_tile(rows, width, itemsize=4):
    limit = max(16, (3 << 20) // (width * itemsize))
    if rows <= limit:
        return rows
    return max(t for t in range(16, limit + 1, 16) if rows % t == 0)


def _sum_sibling_call(g, buf, c, name):
    _, _, rh, w = g.shape
    tile = _rows_tile(rh, w)

    def body(c_ref, g_ref, b_ref, p_ref, pb_ref):
        p = g_ref[...] + b_ref[...]
        p_ref[...] = p
        pb_ref[...] = p.astype(BF16)

    blk = pl.BlockSpec((None, tile, w), lambda s, i, c_ref: (s, i, 0))
    return pl.pallas_call(
        body, name=name,
        grid_spec=pltpu.PrefetchScalarGridSpec(
            num_scalar_prefetch=1, grid=(4, rh // tile),
            in_specs=[pl.BlockSpec((None, None, tile, w), lambda s, i, c_ref: (s, c_ref[0], i, 0)), blk],
            out_specs=[blk, blk]),
        out_shape=[jax.ShapeDtypeStruct((4, rh, w), F32), jax.ShapeDtypeStruct((4, rh, w), BF16)],
        compiler_params=_cp("parallel", "parallel"),
    )(c, g, buf)


def _sum_chips_call(p, buf, sm, name):
    _, rh, w = p.shape
    tile = _rows_tile(rh, w)

    def body(sm_ref, p_ref, b_ref, f_ref):
        f_ref[...] = ((p_ref[...] + b_ref[0].astype(F32)) + b_ref[1].astype(F32)) + b_ref[2].astype(F32)

    return pl.pallas_call(
        body, name=name,
        grid_spec=pltpu.PrefetchScalarGridSpec(
            num_scalar_prefetch=1, grid=(rh // tile,),
            in_specs=[pl.BlockSpec((None, tile, w), lambda i, sm_ref: (sm_ref[0], i, 0)),
                      pl.BlockSpec((3, tile, w), lambda i, sm_ref: (0, i, 0))],
            out_specs=pl.BlockSpec((tile, w), lambda i, sm_ref: (i, 0))),
        out_shape=jax.ShapeDtypeStruct((rh, w), F32),
        compiler_params=_cp("parallel"),
    )(sm, p, buf)


def _adamw_halves_call(w, g_mine, g_sib, c, m, v, name):
    r, wd = w.shape
    rh = r // 2
    tile = _rows_tile(rh, wd)
    nt = rh // tile

    def body(c_ref, w_ref, gm_ref, gs_ref, m_ref, v_ref, g_ref, d_ref, nm_ref, nv_ref):
        gv = jnp.where(pl.program_id(0) == c_ref[0], gm_ref[...], gs_ref[...])
        g_ref[...] = gv
        nm = ADAM_B1 * m_ref[...] + (1.0 - ADAM_B1) * gv
        nv = ADAM_B2 * v_ref[...] + (1.0 - ADAM_B2) * jnp.square(gv)
        m_hat = nm / (1.0 - ADAM_B1 ** ADAM_STEP)
        v_hat = nv / (1.0 - ADAM_B2 ** ADAM_STEP)
        d_ref[...] = -ADAM_LR * (m_hat / (jnp.sqrt(v_hat) + ADAM_EPS) + ADAM_WD * w_ref[...])
        nm_ref[...] = nm
        nv_ref[...] = nv

    whole = pl.BlockSpec((tile, wd), lambda h, i, c_ref: (h * nt + i, 0))
    half = pl.BlockSpec((tile, wd), lambda h, i, c_ref: (i, 0))
    sd = jax.ShapeDtypeStruct((r, wd), F32)
    return pl.pallas_call(
        body, name=name,
        grid_spec=pltpu.PrefetchScalarGridSpec(
            num_scalar_prefetch=1, grid=(2, nt),
            in_specs=[whole, half, half, whole, whole], out_specs=[whole] * 4),
        out_shape=[sd, sd, sd, sd],
        compiler_params=_cp("parallel", "parallel"),
    )(c, w, g_mine, g_sib, m, v)


def _all_reduce8_call(vec, name):
    rows = vec.shape[0]

    def body(v_ref, out_ref, slots, send_sems, recv_sems):
        x, y, c = _mesh_pos()
        me = 4 * x + 2 * y + c
        slots[me] = v_ref[...]

        def rcopy(k, to_me):
            bx, by, bc = (k >> 2) & 1, (k >> 1) & 1, k & 1
            px, py, pc = (1 - x if bx else x), (1 - y if by else y), (1 - c if bc else c)
            slot = 4 * px + 2 * py + pc if to_me else me
            return pltpu.make_async_remote_copy(src_ref=v_ref, dst_ref=slots.at[slot], send_sem=send_sems.at[k - 1],
                                                recv_sem=recv_sems.at[k - 1], device_id=(px, py, pc), device_id_type=MESH_ID)

        for k in range(1, N_DEV):
            rcopy(k, False).start()
        for k in range(1, N_DEV):
            rcopy(k, True).wait_recv()
        for k in range(1, N_DEV):
            rcopy(k, False).wait_send()
        tot = slots[0]
        for d in range(1, N_DEV):
            tot = tot + slots[d]
        out_ref[...] = tot

    return pl.pallas_call(
        body, name=name,
        in_specs=[VMEM_SPEC], out_specs=VMEM_SPEC,
        out_shape=jax.ShapeDtypeStruct((rows, LANES), F32),
        scratch_shapes=[pltpu.VMEM((N_DEV, rows, LANES), F32),
                        pltpu.SemaphoreType.DMA((N_DEV - 1,)), pltpu.SemaphoreType.DMA((N_DEV - 1,))],
    )(vec)


def _adamw_call(w, g, m, v, name):
    r, c = w.shape
    rb = r if r <= 256 else (256 if r % 256 == 0 else 352)
    assert r % rb == 0

    def body(w_ref, g_ref, m_ref, v_ref, d_ref, nm_ref, nv_ref):
        gv = g_ref[...]
        nm = ADAM_B1 * m_ref[...] + (1.0 - ADAM_B1) * gv
        nv = ADAM_B2 * v_ref[...] + (1.0 - ADAM_B2) * jnp.square(gv)
        m_hat = nm / (1.0 - ADAM_B1 ** ADAM_STEP)
        v_hat = nv / (1.0 - ADAM_B2 ** ADAM_STEP)
        d_ref[...] = -ADAM_LR * (m_hat / (jnp.sqrt(v_hat) + ADAM_EPS) + ADAM_WD * w_ref[...])
        nm_ref[...] = nm
        nv_ref[...] = nv

    spec = pl.BlockSpec((rb, c), lambda i: (i, 0))
    sd = jax.ShapeDtypeStruct((r, c), F32)
    return pl.pallas_call(
        body, name=name, grid=(r // rb,),
        in_specs=[spec] * 4, out_specs=[spec] * 3, out_shape=[sd, sd, sd],
        compiler_params=_cp("parallel"),
    )(w, g, m, v)


SMALL = (("attn_norm_w", D_MODEL), ("ret_gn_w", RET_W), ("mla_q_norm_w", Q_RANK), ("mla_kv_norm_w", KV_RANK),
         ("ffn_norm_w", D_MODEL), ("conv_b", F2), ("final_norm_w", D_MODEL))
WEIGHT_ORDER = ("attn_norm_w", "w_in", "ret_gn_w", "mla_q_norm_w", "w_uq", "mla_kv_norm_w", "w_ukv", "w_out",
                "ffn_norm_w", "w_up", "conv_w", "conv_b", "w_down", "final_norm_w")


def _pad_rows(flat, rows):
    return jnp.concatenate([flat, jnp.zeros((rows * LANES - flat.shape[0],), flat.dtype)]).reshape(rows, LANES)


def kernel(x, positions, attn_norm_w, w_in, ret_gn_w, mla_q_norm_w, w_uq, mla_kv_norm_w, w_ukv, w_out, ffn_norm_w, w_up, conv_w, conv_b, w_down, final_norm_w, loss_target, m_attn_norm_w, m_w_in, m_ret_gn_w, m_mla_q_norm_w, m_w_uq, m_mla_kv_norm_w, m_w_ukv, m_w_out, m_ffn_norm_w, m_w_up, m_conv_w, m_conv_b, m_w_down, m_final_norm_w, v_attn_norm_w, v_w_in, v_ret_gn_w, v_mla_q_norm_w, v_w_uq, v_mla_kv_norm_w, v_w_ukv, v_w_out, v_ffn_norm_w, v_w_up, v_conv_w, v_conv_b, v_w_down, v_final_norm_w):
    args = dict(locals())
    cx, cy, cc = _mesh_pos()
    sm = 2 * cx + cy

    c_arr, sm_arr = cc.reshape(1).astype(jnp.int32), sm.reshape(1).astype(jnp.int32)
    sds = jax.ShapeDtypeStruct

    def my_shards(group):
        return [args[n][0].astype(BF16).reshape(2, r // 2, c) for n, (r, c), _ in group]

    def full_weights(gathered, group):
        full = {}
        for (n, (r, c), axis), got in zip(group, gathered):
            piece = got.reshape(4, r, c)
            full[n] = piece if n == "w_up" else (piece.transpose(1, 0, 2).reshape(r, 4 * c) if axis == 1 else piece.reshape(4 * r, c))
        return full

    def by_owner(gw, group):
        out = []
        for n, (r, c), axis in group:
            g = gw[n]
            if axis == 1 and g.ndim == 2:
                g = g.reshape(r, 4, c).transpose(1, 0, 2)
            out.append(g.reshape(4, 2, r // 2, c))
        return out

    def sibling_shapes(gs):
        return [sds((4,) + g.shape[2:], F32) for g in gs]

    def chip_sums(gs, bufs, group):
        res = [_sum_sibling_call(g, b, c_arr, "grads_sum_sibling_" + n) for g, b, (n, _, _) in zip(gs, bufs, group)]
        return [p for p, _ in res], [pb for _, pb in res]

    def chips_shapes(pbs):
        return [sds((3,) + pb.shape[1:], BF16) for pb in pbs]

    def totals(ps, lands, group, tag):
        fins = [_sum_chips_call(p, l, sm_arr, "grads_sum_chips_" + n) for p, l, (n, _, _) in zip(ps, lands, group)]
        sibs = _exchange_call("grads_rs_share_" + tag, _share_copies, fins, [sds(f.shape, F32) for f in fins], len(fins))
        return {n: (f, s) for (n, _, _), f, s in zip(group, fins, sibs)}

    class StepExchanges(_Exchanges):
        def __init__(self, order):
            shards = my_shards(GROUP_B)
            self.gather = _exchange_start_call("weights_gather_start_b", _direct_gather_copies, shards,
                                               [sds((4,) + s.shape, BF16) for s in shards], 13 * len(shards), order)
            self.red = None

        def token(self):
            return self.gather[4][0:1, 0:1]

        def mlp_weights(self, after):
            return full_weights(_exchange_wait_call("weights_gather_wait_b", _direct_gather_copies, self.gather, after)[1], GROUP_B)

        def mlp_grads(self, gw):
            gs = by_owner(gw, GROUP_B)
            self.step1 = _exchange_start_call("grads_rs_sibling_start_b", _sibling_copies, gs, sibling_shapes(gs), 4 * len(gs))
            return self.step1[4]

        def behind_out_bwd(self, after):
            gs, bufs = _exchange_wait_call("grads_rs_sibling_wait_b", _sibling_copies, self.step1, after)
            self.ps, pbs = chip_sums(gs, bufs, GROUP_B)
            self.step2 = _exchange_start_call("grads_rs_chips_start_b", _chips_copies, pbs, chips_shapes(pbs), 3 * len(pbs))
            return self.step2[4]

        def behind_attention(self, after):
            _, lands = _exchange_wait_call("grads_rs_chips_wait_b", _chips_copies, self.step2, after)
            self.red = totals(self.ps, lands, GROUP_B, "b")

    gathered = _gather_list_call(my_shards(GROUP_A) + [conv_w[0].reshape(2, 1, 3 * F2 // 8)], "a")
    full = full_weights(gathered[:-1], GROUP_A)
    ex = StepExchanges(gathered[-1])
    full["conv_w"] = gathered[-1].reshape(4, 3, F2 // 4).transpose(1, 0, 2).reshape(3, F2)
    small = {n: args[n].reshape(1, d) for n, d in SMALL}
    small["attn_norm_w"] = small["attn_norm_w"] + ex.token()

    loss, gx, gw, gs = _local_step(x[0], positions[0], loss_target[0], full, small, ex)

    ga = by_owner(gw, GROUP_A)
    bufs = _exchange_call("grads_rs_sibling_a", _sibling_copies, ga, sibling_shapes(ga), 4 * len(ga))
    ps, pbs = chip_sums(ga, bufs, GROUP_A)
    lands = _exchange_call("grads_rs_chips_a", _chips_copies, pbs, chips_shapes(pbs), 3 * len(pbs))
    halves = {**ex.red, **totals(ps, lands, GROUP_A, "a")}

    vec = jnp.concatenate([gs[n].reshape(-1) for n, _ in SMALL] + [gw["conv_w"].reshape(-1), loss.reshape(-1)])
    tot = _all_reduce8_call(_pad_rows(vec, 216), "small_all_reduce").reshape(-1)
    red, off = {}, 0
    for n, d in SMALL:
        red[n] = tot[off:off + d].reshape(1, d)
        off += d
    red["conv_w"] = lax.dynamic_slice(tot[off:off + 3 * F2].reshape(3, F2), (0, sm * (F2 // 4)), (3, F2 // 4))
    loss_tot = tot[off + 3 * F2]

    grads, deltas, new_m, new_v = [], [], [], []
    for n in WEIGHT_ORDER:
        shape = args[n].shape
        two_d = (1, shape[0]) if len(shape) == 1 else shape[-2:]
        wmv = [args[k + n].reshape(two_d) for k in ("", "m_", "v_")]
        if n in halves:
            g, d, nm, nv = _adamw_halves_call(wmv[0], *halves[n], c_arr, wmv[1], wmv[2], "adamw_" + n)
        else:
            g = red[n].reshape(two_d)
            d, nm, nv = _adamw_call(wmv[0], g, wmv[1], wmv[2], "adamw_" + n)
        grads.append(g.reshape(shape))
        deltas.append(d.reshape(shape))
        new_m.append(nm.reshape(shape))
        new_v.append(nv.reshape(shape))
    return (loss_tot, gx[None], *grads, *deltas, *new_m, *new_v)
```

```python
import math

import numpy as np
import jax
import jax.numpy as jnp
from jax import lax
from jax.experimental import pallas as pl
from jax.experimental.pallas import tpu as pltpu

F32 = jnp.float32
BF16 = jnp.bfloat16

D_MODEL = 1024
N_HEADS = 8
HEAD = 64
RET_W = N_HEADS * HEAD
MLA_W = N_HEADS * HEAD
ROPE = 32
Q_RANK = 256
KV_RANK = 128
D_FF = 2816
F2 = 2 * D_FF
IN_W = 4 * RET_W + Q_RANK + KV_RANK + ROPE
IN_EXT = 4 * RET_W + Q_RANK + KV_RANK + 128
KPE_LO = 64
ROPE_BASE = 10000.0
EPS = 1e-6
RET_CHUNK = 256
SM_SCALE = (HEAD + ROPE) ** -0.5
LOG2E = math.log2(math.e)
LN2 = math.log(2.0)
NEG = -1e30
LANES = 128
VMEM_LIMIT = 56 * 1024 * 1024

ADAM_LR = 0.001
ADAM_B1 = 0.9
ADAM_B2 = 0.999
ADAM_EPS = 1e-08
ADAM_WD = 0.01
ADAM_STEP = 10


VMEM_LIMIT_MLP = 60 * 1024 * 1024


def _cp(*sem, vmem=VMEM_LIMIT):
    return pltpu.CompilerParams(dimension_semantics=sem, vmem_limit_bytes=vmem)


def _full(shape):
    n = len(shape)
    return pl.BlockSpec(tuple(shape), lambda *_: (0,) * n)


def _row(ts, c):
    return pl.BlockSpec((ts, c), lambda i: (i, 0))


def _hrow(h, ts, c):
    return pl.BlockSpec((h, ts, c), lambda i: (0, i, 0))


def _dot(a, b):
    return jnp.dot(a, b, preferred_element_type=F32)


def _dot_nt(a, b):
    return lax.dot_general(a, b, (((1,), (1,)), ((), ())), preferred_element_type=F32)


def _dot_tn(a, b):
    return lax.dot_general(a, b, (((0,), (0,)), ((), ())), preferred_element_type=F32)


def _dot_hi(a, b):
    hi = a.astype(BF16)
    lo = (a - hi.astype(F32)).astype(BF16)
    bb = b.astype(BF16)
    return _dot(hi, bb) + _dot(lo, bb)


def _rot_half(x, half):
    w = x.shape[-1]
    lane = lax.broadcasted_iota(jnp.int32, x.shape, x.ndim - 1)
    first = (lane % (2 * half)) < half
    return jnp.where(first, -pltpu.roll(x, w - half, x.ndim - 1), pltpu.roll(x, half, x.ndim - 1))


def _rope(x, cos, sin, half):
    return x * cos + _rot_half(x, half) * sin


def _unrope(dy, cos, sin, half):
    return dy * cos - _rot_half(dy, half) * sin


def _sigmoid(g):
    return 0.5 * jnp.tanh(0.5 * g) + 0.5


def _silu(g):
    return g * _sigmoid(g)


def _rstd(x):
    return lax.rsqrt(jnp.mean(x * x, axis=-1, keepdims=True) + EPS)


def _rope_tables(positions):
    pos = positions.astype(F32)[:, None]
    s = pos.shape[0]
    inv = ROPE_BASE ** (-jnp.arange(0, HEAD, 2, dtype=F32) / HEAD)
    ang = pos * inv
    c, sn = jnp.cos(ang), jnp.sin(ang)
    cos_r = jnp.tile(jnp.concatenate([c, c], -1), (1, 2))
    sin_r = jnp.tile(jnp.concatenate([sn, sn], -1), (1, 2))
    inv = ROPE_BASE ** (-jnp.arange(0, ROPE, 2, dtype=F32) / ROPE)
    ang = pos * inv
    c, sn = jnp.cos(ang), jnp.sin(ang)
    one, zero = jnp.ones((s, KPE_LO), F32), jnp.zeros((s, KPE_LO), F32)
    cos_m = jnp.concatenate([one, c, c, one[:, :LANES - KPE_LO - ROPE]], -1)
    sin_m = jnp.concatenate([zero, sn, sn, zero[:, :LANES - KPE_LO - ROPE]], -1)
    return cos_r, sin_r, cos_m, sin_m


def _ret_consts():
    c = RET_CHUNK
    lg = np.log1p(-np.power(2.0, -5.0 - np.arange(N_HEADS, dtype=np.float64)))
    idx = np.arange(c, dtype=np.float64)
    diff = idx[:, None] - idx[None, :]
    lane_head = np.arange(LANES) // HEAD
    dmask = np.zeros((4, 2, c, c))
    zeta = np.zeros((4, c, LANES))
    xi = np.zeros((4, c, LANES))
    cd = np.zeros((4, LANES, LANES))
    bd = (lane_head[:, None] == lane_head[None, :]).astype(np.float64)
    for j in range(4):
        for hh in range(2):
            dmask[j, hh] = np.where(diff >= 0, np.exp(lg[2 * j + hh] * np.maximum(diff, 0.0)), 0.0)
        lgl = lg[2 * j + lane_head]
        zeta[j] = np.exp(lgl[None, :] * (c - 1.0 - idx[:, None]))
        xi[j] = np.exp(lgl[None, :] * (idx[:, None] + 1.0))
        cd[j] = np.exp(lgl * c)[:, None] * bd
    f = lambda a: jnp.asarray(a, F32)
    side = lambda d: np.concatenate([d[:, 0], d[:, 1]], axis=-1)
    return dict(dmask=f(side(dmask)), dmask_t=f(side(np.swapaxes(dmask, 2, 3))), zeta=f(zeta), xi=f(xi), cd=f(cd), bd=f(bd))


def _f1_call(x, anw, win, cos_r, sin_r, cos_m, sin_m, ts):
    s = x.shape[0]

    def body(x_ref, anw_ref, w_ref, cr_ref, sr_ref, cm_ref, sm_ref,
             q_ref, k_ref, v_ref, g_ref, cq_ref, ckv_ref, kpe_ref, r_ref):
        xv = x_ref[...]
        r = _rstd(xv)
        r_ref[...] = r
        h = (xv * r * anw_ref[...]).astype(BF16)
        cr, sr = cr_ref[...], sr_ref[...]
        qk = _dot(h, w_ref[:, 0:2 * RET_W])
        for j in range(4):
            sl = slice(j * LANES, (j + 1) * LANES)
            q_ref[:, sl] = _rope(qk[:, sl], cr, sr, HEAD // 2).astype(BF16)
            kk = qk[:, RET_W + j * LANES:RET_W + (j + 1) * LANES]
            k_ref[:, sl] = (_rope(kk, cr, sr, HEAD // 2) * (HEAD ** -0.5)).astype(BF16)
        v_ref[...] = _dot(h, w_ref[:, 2 * RET_W:3 * RET_W]).astype(BF16)
        g_ref[...] = _dot(h, w_ref[:, 3 * RET_W:4 * RET_W])
        o = 4 * RET_W
        cq_ref[...] = _dot(h, w_ref[:, o:o + Q_RANK])
        ckv_ref[...] = _dot(h, w_ref[:, o + Q_RANK:o + Q_RANK + KV_RANK])
        kp = _dot(h, w_ref[:, o + Q_RANK + KV_RANK:IN_EXT])
        kpe_ref[...] = _rope(kp, cm_ref[...], sm_ref[...], ROPE // 2)

    sd = jax.ShapeDtypeStruct
    return pl.pallas_call(
        body, name="f1_in_proj", grid=(s // ts,),
        in_specs=[_row(ts, D_MODEL), _full((1, D_MODEL)), _full((D_MODEL, IN_EXT)),
                  _row(ts, LANES), _row(ts, LANES), _row(ts, LANES), _row(ts, LANES)],
        out_specs=[_row(ts, RET_W), _row(ts, RET_W), _row(ts, RET_W), _row(ts, RET_W),
                   _row(ts, Q_RANK), _row(ts, KV_RANK), _row(ts, LANES), _row(ts, 1)],
        out_shape=[sd((s, RET_W), BF16), sd((s, RET_W), BF16), sd((s, RET_W), BF16), sd((s, RET_W), F32),
                   sd((s, Q_RANK), F32), sd((s, KV_RANK), F32), sd((s, LANES), F32), sd((s, 1), F32)],
        compiler_params=_cp("parallel"),
    )(x, anw, win, cos_r, sin_r, cos_m, sin_m)


def _stack_heads(a):
    lo = lax.broadcasted_iota(jnp.int32, a.shape, 1) < HEAD
    zero = jnp.zeros_like(a)
    return jnp.concatenate([jnp.where(lo, a, zero), jnp.where(lo, zero, a)], axis=0)


def _pair_product(a, b2, decay2, w2):
    return _dot((_dot_nt(a, b2) * decay2).astype(BF16), w2)


RET_SLABS = 2


def _ret_specs(tr, tile_of):
    c, ns = RET_CHUNK, RET_SLABS
    return dict(
        slab=pl.BlockSpec((tr, ns * LANES), lambda j, i: (tile_of(i), j)),
        tab=pl.BlockSpec((tr, LANES), lambda j, i: (tile_of(i), 0)),
        vec=pl.BlockSpec((1, ns * LANES), lambda j, i: (0, j)),
        dmask=pl.BlockSpec((ns, c, 2 * c), lambda j, i: (j, 0, 0)),
        rows=pl.BlockSpec((ns, c, LANES), lambda j, i: (j, 0, 0)),
        state=pl.BlockSpec((ns, LANES, LANES), lambda j, i: (j, 0, 0)),
        bd=pl.BlockSpec((LANES, LANES), lambda j, i: (0, 0)))


def _ret_states(a_ref, b_ref, scale_ref, cd_ref, bd, st_ref, chunks, lanes, reverse):
    nc = len(chunks)
    contrib = [[_dot_tn((a_ref[rows, ln].astype(F32) * scale_ref[sl]).astype(BF16), b_ref[rows, ln]) * bd for rows in chunks]
               for sl, ln in enumerate(lanes)]
    states = []
    for sl in range(len(lanes)):
        st, seen = st_ref[sl], [None] * nc
        for ci in (reversed(range(nc)) if reverse else range(nc)):
            seen[ci] = st.astype(BF16)
            st = st * cd_ref[sl] + contrib[sl][ci]
        st_ref[sl] = st
        states.append(seen)
    return states


def _ret_fwd_call(q, k, v, g, gnw, rc, tr):
    s = q.shape[0]
    c = RET_CHUNK
    nc = tr // c
    ns = RET_SLABS

    def body(q_ref, k_ref, v_ref, g_ref, gnw_ref, dm_ref, zeta_ref, xi_ref, cd_ref, bd_ref, o_ref, y_ref, st_ref):
        @pl.when(pl.program_id(1) == 0)
        def _():
            st_ref[...] = jnp.zeros_like(st_ref)

        bd = bd_ref[...]
        chunks = [slice(ci * c, (ci + 1) * c) for ci in range(nc)]
        lanes = [slice(sl * LANES, (sl + 1) * LANES) for sl in range(ns)]
        states = _ret_states(k_ref, v_ref, zeta_ref, cd_ref, bd, st_ref, chunks, lanes, False)
        for ci, rows in enumerate(chunks):
            for sl, ln in enumerate(lanes):
                qc = q_ref[rows, ln]
                o_ref[rows, ln] = (_dot(qc, states[sl][ci]) * xi_ref[sl]
                                   + _pair_product(qc, _stack_heads(k_ref[rows, ln]), dm_ref[sl], _stack_heads(v_ref[rows, ln])))
        avg = bd * (1.0 / HEAD)
        for ln in lanes:
            o = o_ref[:, ln]
            ctr = o - _dot_hi(o, avg)
            var = _dot_hi(ctr * ctr, avg)
            y_ref[:, ln] = (_silu(g_ref[:, ln]) * (ctr * lax.rsqrt(var + EPS) * gnw_ref[:, ln])).astype(BF16)

    specs = _ret_specs(tr, lambda i: i)
    sd = jax.ShapeDtypeStruct
    return pl.pallas_call(
        body, name="ret_fwd", grid=(4 // ns, s // tr),
        in_specs=[specs["slab"]] * 4 + [specs["vec"], specs["dmask"], specs["rows"], specs["rows"], specs["state"], specs["bd"]],
        out_specs=[specs["slab"]] * 2,
        out_shape=[sd((s, RET_W), F32), sd((s, RET_W), BF16)],
        scratch_shapes=[pltpu.VMEM((ns, LANES, LANES), F32)],
        compiler_params=_cp("parallel", "arbitrary"),
    )(q, k, v, g, gnw, rc["dmask"], rc["zeta"], rc["xi"], rc["cd"], rc["bd"])


QK_AUX = HEAD + ROPE
V_AUX = HEAD


def _lane_pair(shape, lo, a, b, rest):
    lane = lax.broadcasted_iota(jnp.int32, shape, len(shape) - 1)
    return jnp.where(lane == lo, a, jnp.where(lane == lo + 1, b, rest))


def _hi_lo(v):
    hi = v.astype(BF16).astype(F32)
    return hi, v - hi


def _mla_pre_call(cq, ckv, kpe, qnw, kvnw, wq, wk, wv, cos_m, sin_m, ts):
    s = cq.shape[0]

    def body(cq_ref, ckv_ref, kpe_ref, qnw_ref, kvnw_ref, wq_ref, wk_ref, wv_ref, cm_ref, sm_ref, q_ref, k_ref, v_ref):
        cqv, ckvv = cq_ref[...], ckv_ref[...]
        cqn = (cqv * _rstd(cqv) * qnw_ref[...]).astype(BF16)
        ckvn = (ckvv * _rstd(ckvv) * kvnw_ref[...]).astype(BF16)
        cm, sm = cm_ref[...], sm_ref[...]
        kp = _lane_pair((ts, LANES), QK_AUX, -1.0, -1.0, kpe_ref[...])
        for h in range(N_HEADS):
            qh = _rope(_dot(cqn, wq_ref[h]), cm, sm, ROPE // 2)
            q_ref[h] = (qh * (SM_SCALE * LOG2E)).astype(BF16)
            k_ref[h] = (_dot(ckvn, wk_ref[h]) + kp).astype(BF16)
            v_ref[h] = _lane_pair((ts, LANES), V_AUX, 1.0, 1.0, _dot(ckvn, wv_ref[h])).astype(BF16)

    sd = jax.ShapeDtypeStruct
    hm = sd((N_HEADS, s, LANES), BF16)
    return pl.pallas_call(
        body, name="mla_pre", grid=(s // ts,),
        in_specs=[_row(ts, Q_RANK), _row(ts, KV_RANK), _row(ts, LANES), _full((1, Q_RANK)), _full((1, KV_RANK)),
                  _full((N_HEADS, Q_RANK, LANES)), _full((N_HEADS, KV_RANK, LANES)), _full((N_HEADS, KV_RANK, LANES)),
                  _row(ts, LANES), _row(ts, LANES)],
        out_specs=[_hrow(N_HEADS, ts, LANES)] * 3,
        out_shape=[hm, hm, hm],
        compiler_params=_cp("parallel"),
    )(cq, ckv, kpe, qnw, kvnw, wq, wk, wv, cos_m, sin_m)


def _flash_fwd_call(q, k, v, tb):
    s = q.shape[1]
    nb = s // tb
    pairs = [(a, b) for a in range(nb) for b in range(a + 1)]
    qi_of, ki_of = (jnp.asarray(np.array(col, np.int32)) for col in zip(*pairs))

    def body(qi_ref, ki_ref, q_ref, k_ref, v_ref, o_ref, qb_ref, m_ref, acc_ref):
        qi, ki = qi_ref[pl.program_id(0)], ki_ref[pl.program_id(0)]

        @pl.when(ki == 0)
        def _():
            m_ref[...] = jnp.full_like(m_ref, NEG)
            acc_ref[...] = jnp.zeros_like(acc_ref)

        def step(masked):
            if masked:
                keep = lax.broadcasted_iota(jnp.int32, (tb, tb), 1) <= lax.broadcasted_iota(jnp.int32, (tb, tb), 0)
            def finish(h, pe, alpha):
                acc_ref[h] = acc_ref[h] * alpha + _dot(pe, v_ref[h])

            nxt, pending = _dot_nt(q_ref[0], k_ref[0]), None
            for h in range(N_HEADS):
                sc = nxt
                if h + 1 < N_HEADS:
                    nxt = _dot_nt(q_ref[h + 1], k_ref[h + 1])
                if masked:
                    sc = jnp.where(keep, sc, NEG)
                m_prev = m_ref[h]
                m_new = jnp.maximum(m_prev, jnp.max(sc, axis=1, keepdims=True))
                pe = jnp.exp2(sc - jnp.tile(m_new, (1, tb // LANES))).astype(BF16)
                m_ref[h] = m_new
                if pending is not None:
                    finish(*pending)
                pending = (h, pe, jnp.exp2(m_prev - m_new))
            finish(*pending)

        @pl.when(ki < qi)
        def _():
            step(False)

        @pl.when(ki == qi)
        def _():
            step(True)
            lane = lax.broadcasted_iota(jnp.int32, (tb, LANES), 1)
            for p in range(N_HEADS // 2):
                outs = []
                for h in (2 * p, 2 * p + 1):
                    acc = acc_ref[h]
                    l = acc[:, V_AUX:V_AUX + 1]
                    outs.append(acc * (1.0 / l))
                    hi, lo = _hi_lo(m_ref[h][:, 0:1] + jnp.log(l) * LOG2E)
                    qb_ref[h] = _lane_pair((tb, LANES), QK_AUX, hi, lo, q_ref[h].astype(F32)).astype(BF16)
                o_ref[:, p * LANES:(p + 1) * LANES] = jnp.where(lane < HEAD, outs[0], pltpu.roll(outs[1], HEAD, 1)).astype(BF16)

    sd = jax.ShapeDtypeStruct
    qspec = pl.BlockSpec((N_HEADS, tb, LANES), lambda p, qi_ref, ki_ref: (0, qi_ref[p], 0))
    kspec = pl.BlockSpec((N_HEADS, tb, LANES), lambda p, qi_ref, ki_ref: (0, ki_ref[p], 0))
    return pl.pallas_call(
        body, name="mla_flash_fwd",
        grid_spec=pltpu.PrefetchScalarGridSpec(
            num_scalar_prefetch=2, grid=(len(pairs),),
            in_specs=[qspec, kspec, kspec],
            out_specs=[pl.BlockSpec((tb, MLA_W), lambda p, qi_ref, ki_ref: (qi_ref[p], 0)), qspec],
            scratch_shapes=[pltpu.VMEM((N_HEADS, tb, LANES), F32), pltpu.VMEM((N_HEADS, tb, LANES), F32)]),
        out_shape=[sd((s, MLA_W), BF16), sd((N_HEADS, s, LANES), BF16)],
        compiler_params=_cp("arbitrary"),
    )(qi_of, ki_of, q, k, v)


def _out_proj_call(x, yret, ymla, wout, ts):
    s = x.shape[0]

    def body(x_ref, yr_ref, ym_ref, w_ref, x1_ref, r_ref):
        x1 = x_ref[...] + _dot(yr_ref[...], w_ref[0:RET_W, :]) + _dot(ym_ref[...], w_ref[RET_W:, :])
        x1_ref[...] = x1
        r_ref[...] = _rstd(x1)

    sd = jax.ShapeDtypeStruct
    return pl.pallas_call(
        body, name="out_proj", grid=(s // ts,),
        in_specs=[_row(ts, D_MODEL), _row(ts, RET_W), _row(ts, MLA_W), _full((D_MODEL, D_MODEL))],
        out_specs=[_row(ts, D_MODEL), _row(ts, 1)],
        out_shape=[sd((s, D_MODEL), F32), sd((s, 1), F32)],
        compiler_params=_cp("parallel"),
    )(x, yret, ymla, wout)


W_UP_SHARD = F2 // 4


def _ffn_fwd_call(x1, r2, fnw, wup4, cw, cb, wdown, tgt, fw, ts):
    s = x1.shape[0]
    wsh = W_UP_SHARD

    def body(x_ref, r_ref, fnw_ref, wup_ref, cw_ref, cb_ref, wd_ref, t_ref, fw_ref,
             u_ref, uc_ref, dx2_ref, loss_ref, gfw_ref, carry_ref):
        _zero_first(pl.program_id(0) == 0, carry_ref, loss_ref, gfw_ref)
        xv = x_ref[...]
        h = (xv * r_ref[...] * fnw_ref[...]).astype(BF16)
        conv = []
        for j in range(4):
            cols = slice(j * wsh, (j + 1) * wsh)
            ub = _dot(h, wup_ref[j]).astype(BF16)
            u_ref[:, cols] = ub
            u = ub.astype(F32)
            u1, u2 = _shifted(u, carry_ref[:, cols])
            w = cw_ref[:, cols]
            cb16 = (cb_ref[:, cols] + w[0:1, :] * u2 + w[1:2, :] * u1 + w[2:3, :] * u).astype(BF16)
            uc_ref[:, cols] = cb16
            conv.append(cb16.astype(F32))
            carry_ref[:, cols] = u[ts - 8:, :]
        acc = xv
        for j in range(2):
            a = (_silu(conv[j]) * conv[j + 2]).astype(BF16)
            acc = acc + _dot(a, wd_ref[j * wsh:(j + 1) * wsh, :])
        r = _rstd(acc)
        xh = acc * r
        fwv = fw_ref[...]
        e = xh * fwv - t_ref[...]
        loss_ref[...] += (0.5 / D_MODEL) * _colsum(jnp.sum(e * e, axis=1, keepdims=True))
        dy = e * (1.0 / D_MODEL)
        gfw_ref[...] += _colsum(dy * xh)
        dx2_ref[...] = _norm_bwd(dy, xh, r, fwv)

    sd = jax.ShapeDtypeStruct
    once = lambda shape: pl.BlockSpec(shape, lambda i: (0,) * len(shape), pipeline_mode=pl.Buffered(1))
    return pl.pallas_call(
        body, name="ffn_fwd_loss", grid=(s // ts,),
        in_specs=[_row(ts, D_MODEL), _row(ts, 1), once((1, D_MODEL)), once((4, D_MODEL, wsh)),
                  once((3, F2)), once((1, F2)), once((D_FF, D_MODEL)), _row(ts, D_MODEL), once((1, D_MODEL))],
        out_specs=[_row(ts, F2), _row(ts, F2), _row(ts, D_MODEL), _full((1, 1)), _full((1, D_MODEL))],
        out_shape=[sd((s, F2), BF16), sd((s, F2), BF16), sd((s, D_MODEL), F32), sd((1, 1), F32), sd((1, D_MODEL), F32)],
        scratch_shapes=[pltpu.VMEM((8, F2), F32)],
        compiler_params=_cp("arbitrary", vmem=VMEM_LIMIT_MLP),
    )(x1, r2, fnw, wup4, cw, cb, wdown, tgt, fw)


def _shifted(u, hal):
    row = lax.broadcasted_iota(jnp.int32, hal.shape, 0)
    r1, r2 = pltpu.roll(u, 1, 0), pltpu.roll(u, 2, 0)
    top1 = jnp.where(row == 0, hal[7:8, :], r1[0:8, :])
    top2 = jnp.where(row == 0, hal[6:7, :], jnp.where(row == 1, hal[7:8, :], r2[0:8, :]))
    return jnp.concatenate([top1, r1[8:, :]], axis=0), jnp.concatenate([top2, r2[8:, :]], axis=0)


def _prep_weights(w):
    win = w["w_in"]
    pad = lambda n: jnp.zeros((D_MODEL, n), win.dtype)
    win_ext = jnp.concatenate([win[:, :IN_W - ROPE], pad(KPE_LO), win[:, IN_W - ROPE:], pad(LANES - KPE_LO - ROPE)], -1)
    wuq = w["w_uq"].reshape(Q_RANK, N_HEADS, HEAD + ROPE)
    wq = jnp.concatenate([wuq, jnp.zeros((Q_RANK, N_HEADS, LANES - HEAD - ROPE), wuq.dtype)], -1).transpose(1, 0, 2)
    wukv = w["w_ukv"].reshape(KV_RANK, N_HEADS, 2 * HEAD)
    zk = jnp.zeros((KV_RANK, N_HEADS, HEAD), wukv.dtype)
    wk = jnp.concatenate([wukv[:, :, :HEAD], zk], -1).transpose(1, 0, 2)
    wv = jnp.concatenate([wukv[:, :, HEAD:], zk], -1).transpose(1, 0, 2)
    c = lambda a: a.astype(BF16)
    return dict(win=c(win_ext), wq=c(wq), wk=c(wk), wv=c(wv), wout=c(w["w_out"]))


def _prep_mlp_weights(w):
    wup = w["w_up"]
    if wup.ndim == 2:
        wup = wup.reshape(D_MODEL, 4, W_UP_SHARD).transpose(1, 0, 2)
    return dict(wup=wup.astype(BF16), wdown=w["w_down"].astype(BF16))


def _tiles(s):
    return dict(ts=min(s, 512), tr=min(s, 2048), tbf=min(s, 1024), tb=min(s, 512), t2=min(s, 256),
                tw=min(s, 2048), t1=min(s, 1024))


class _Exchanges:
    def __init__(self, w):
        self.w = w

    def mlp_weights(self, after):
        return self.w

    def mlp_grads(self, gw):
        pass

    def behind_out_bwd(self, after):
        pass

    def behind_attention(self, after):
        pass


def _forward(x, positions, tgt, w, small, ex):
    s = x.shape[0]
    t = _tiles(s)
    pw = _prep_weights(w)
    cos_r, sin_r, cos_m, sin_m = _rope_tables(positions)
    rc = _ret_consts()
    q, k, v, g, cq, ckv, kpe, r1 = _f1_call(x, small["attn_norm_w"], pw["win"], cos_r, sin_r, cos_m, sin_m, t["ts"])
    o_ret, y_ret = _ret_fwd_call(q, k, v, g, small["ret_gn_w"], rc, t["tr"])
    mq, mk, mv = _mla_pre_call(cq, ckv, kpe, small["mla_q_norm_w"], small["mla_kv_norm_w"],
                               pw["wq"], pw["wk"], pw["wv"], cos_m, sin_m, t["ts"])
    y_mla, mqb = _flash_fwd_call(mq, mk, mv, t["tbf"])
    x1, r2 = _out_proj_call(x, y_ret, y_mla, pw["wout"], t["ts"])
    pw.update(_prep_mlp_weights(ex.mlp_weights(r2)))
    u, uc, dx2, loss, g_fw = _ffn_fwd_call(x1, r2, small["ffn_norm_w"], pw["wup"], w["conv_w"], small["conv_b"], pw["wdown"],
                                           tgt, small["final_norm_w"], t["ts"])
    return dict(pw=pw, tabs=(cos_r, sin_r, cos_m, sin_m), rc=rc, q=q, k=k, v=v, g=g, cq=cq, ckv=ckv, kpe=kpe, r1=r1,
                o_ret=o_ret, y_ret=y_ret, mqb=mqb, mk=mk, mv=mv, y_mla=y_mla, x1=x1, r2=r2, u=u, uc=uc,
                dx2=dx2, loss=loss, g_fw=g_fw)


def _norm_bwd(dh, xh, r, nw):
    dxn = dh * nw
    return r * (dxn - xh * jnp.mean(dxn * xh, axis=-1, keepdims=True))


def _ordered_after(body, order):
    if order is None:
        return body, [], []
    return (lambda order_ref, *refs: body(*refs)), [pl.BlockSpec(memory_space=pl.ANY)], [order]


def _zero_first(first, *refs):
    @pl.when(first)
    def _():
        for ref in refs:
            ref[...] = jnp.zeros_like(ref)


def _colsum(v):
    return jnp.sum(v, axis=0, keepdims=True)


def _dsilu(g, sg):
    return sg * (1.0 + g * (1.0 - sg))


def _ffn_bwd_call(dx2, u, uc, cw, wdown, wup4, x1, r2, fnw, ts):
    s = dx2.shape[0]
    nt = s // ts
    wsh = W_UP_SHARD
    rev = lambda i: nt - 1 - i

    def body(dx2_ref, u_ref, uc_ref, cw_ref, wd_ref, wup_ref, x_ref, r_ref, fnw_ref,
             du_ref, dx1_ref, dcw_ref, dcb_ref, dfnw_ref, dwd_hbm, carry_ref, dwd_ref, sem):
        i = pl.program_id(0)
        _zero_first(i == 0, carry_ref, dwd_ref, dcw_ref, dcb_ref, dfnw_ref)
        dxb = dx2_ref[...].astype(BF16)
        dh = jnp.zeros((ts, D_MODEL), F32)
        for j in range(2):
            gcols = slice(j * wsh, (j + 1) * wsh)
            vcols = slice(D_FF + j * wsh, D_FF + (j + 1) * wsh)
            gate, val = uc_ref[:, gcols].astype(F32), uc_ref[:, vcols].astype(F32)
            da = _dot_nt(dxb, wd_ref[gcols, :])
            sg = _sigmoid(gate)
            sl = gate * sg
            dwd_ref[gcols, :] += _dot_tn((sl * val).astype(BF16), dxb)
            for d, cols, shard in ((da * val * _dsilu(gate, sg), gcols, j), (da * sl, vcols, 2 + j)):
                d1, d2 = _shifted_up(d, carry_ref[:, cols])
                uv = u_ref[:, cols].astype(F32)
                for t, dt in enumerate((d2, d1, d)):
                    dcw_ref[t:t + 1, cols] += _colsum(dt * uv)
                dcb_ref[:, cols] += _colsum(d)
                w = cw_ref[:, cols]
                du = (w[2:3, :] * d + w[1:2, :] * d1 + w[0:1, :] * d2).astype(BF16)
                du_ref[:, cols] = du
                dh = dh + _dot_nt(du, wup_ref[shard])
                carry_ref[:, cols] = d[0:8, :]
        r = r_ref[...]
        xh = x_ref[...] * r
        dfnw_ref[...] += _colsum(dh * xh)
        dx1_ref[...] = dx2_ref[...] + _norm_bwd(dh, xh, r, fnw_ref[...])

        @pl.when(i == nt - 1)
        def _():
            cp = pltpu.make_async_copy(dwd_ref, dwd_hbm, sem)
            cp.start()
            cp.wait()

    sd = jax.ShapeDtypeStruct
    row = lambda c: pl.BlockSpec((ts, c), lambda i: (rev(i), 0))
    once = lambda shape: pl.BlockSpec(shape, lambda i: (0,) * len(shape), pipeline_mode=pl.Buffered(1))
    return pl.pallas_call(
        body, name="ffn_bwd", grid=(nt,),
        in_specs=[row(D_MODEL), row(F2), row(F2), once((3, F2)), once((D_FF, D_MODEL)), once((4, D_MODEL, wsh)),
                  row(D_MODEL), row(1), once((1, D_MODEL))],
        out_specs=[row(F2), row(D_MODEL), _full((3, F2)), _full((1, F2)), _full((1, D_MODEL)), pl.BlockSpec(memory_space=pl.ANY)],
        out_shape=[sd((s, F2), BF16), sd((s, D_MODEL), F32), sd((3, F2), F32), sd((1, F2), F32), sd((1, D_MODEL), F32),
                   sd((D_FF, D_MODEL), F32)],
        scratch_shapes=[pltpu.VMEM((8, F2), F32), pltpu.VMEM((D_FF, D_MODEL), F32), pltpu.SemaphoreType.DMA],
        compiler_params=_cp("arbitrary", vmem=VMEM_LIMIT_MLP),
    )(dx2, u, uc, cw, wdown, wup4, x1, r2, fnw)


def _shifted_up(d, hal):
    n = d.shape[0]
    row = lax.broadcasted_iota(jnp.int32, hal.shape, 0)
    r1, r2 = pltpu.roll(d, n - 1, 0), pltpu.roll(d, n - 2, 0)
    end1 = jnp.where(row == 7, hal[0:1, :], r1[n - 8:, :])
    end2 = jnp.where(row == 6, hal[0:1, :], jnp.where(row == 7, hal[1:2, :], r2[n - 8:, :]))
    return jnp.concatenate([r1[:n - 8, :], end1], axis=0), jnp.concatenate([r2[:n - 8, :], end2], axis=0)


def _dw_norm_call(x, r, nw, b, ts, tn, name):
    s, n = b.shape
    k = x.shape[1]

    def body(x_ref, r_ref, nw_ref, b_ref, dw_ref):
        _zero_first(pl.program_id(1) == 0, dw_ref)
        h = (x_ref[...] * r_ref[...] * nw_ref[...]).astype(BF16)
        dw_ref[...] += _dot_tn(h, b_ref[...])

    return pl.pallas_call(
        body, name=name, grid=(n // tn, s // ts),
        in_specs=[pl.BlockSpec((ts, k), lambda j, i: (i, 0)), pl.BlockSpec((ts, 1), lambda j, i: (i, 0)),
                  pl.BlockSpec((1, k), lambda j, i: (0, 0)), pl.BlockSpec((ts, tn), lambda j, i: (i, j))],
        out_specs=pl.BlockSpec((None, k, tn), lambda j, i: (j, 0, 0)),
        out_shape=jax.ShapeDtypeStruct((n // tn, k, tn), F32),
        compiler_params=_cp("parallel", "arbitrary"),
    )(x, r, nw, b)


def _out_bwd_call(dx1, yret, ymla, wout, ts, order=None):
    s = dx1.shape[0]

    def body(dx_ref, yr_ref, ym_ref, w_ref, dyr_ref, do_ref, dwo_ref):
        _zero_first(pl.program_id(0) == 0, dwo_ref)
        dxb = dx_ref[...].astype(BF16)
        dmix = _dot_nt(dxb, w_ref[...])
        dyr_ref[...] = dmix[:, :RET_W]
        ym = ym_ref[...]
        lane = lax.broadcasted_iota(jnp.int32, (ts, LANES), 1)
        for p in range(N_HEADS // 2):
            dom = dmix[:, RET_W + p * LANES:RET_W + (p + 1) * LANES]
            prod = dom * ym[:, p * LANES:(p + 1) * LANES].astype(F32)
            for hh in range(2):
                mine = (lane >= HEAD) if hh else (lane < HEAD)
                hi, lo = _hi_lo(jnp.sum(jnp.where(mine, prod, 0.0), axis=1, keepdims=True))
                base = jnp.where(lane < HEAD, pltpu.roll(dom, HEAD, 1) if hh else dom, 0.0)
                do_ref[2 * p + hh] = _lane_pair((ts, LANES), V_AUX, -hi, -lo, base).astype(BF16)
        dwo_ref[0:RET_W, :] += _dot_tn(yr_ref[...], dxb)
        dwo_ref[RET_W:, :] += _dot_tn(ym, dxb)

    sd = jax.ShapeDtypeStruct
    body, first_specs, first = _ordered_after(body, order)
    return pl.pallas_call(
        body, name="out_proj_bwd", grid=(s // ts,),
        in_specs=first_specs + [_row(ts, D_MODEL), _row(ts, RET_W), _row(ts, MLA_W), _full((D_MODEL, D_MODEL))],
        out_specs=[_row(ts, RET_W), _hrow(N_HEADS, ts, LANES), _full((D_MODEL, D_MODEL))],
        out_shape=[sd((s, RET_W), F32), sd((N_HEADS, s, LANES), BF16), sd((D_MODEL, D_MODEL), F32)],
        compiler_params=_cp("arbitrary"),
    )(*first, dx1, yret, ymla, wout)


def _ret_bwd_q_call(q, k, v, o, g, dy, gnw, rc, cos_r, sin_r, tr):
    s = q.shape[0]
    c = RET_CHUNK
    nc = tr // c
    ns = RET_SLABS

    def body(q_ref, k_ref, v_ref, o_ref, g_ref, dy_ref, gnw_ref, dm_ref, zeta_ref, xi_ref, cd_ref, bd_ref, cr_ref, sr_ref,
             dq_ref, dg_ref, do_ref, dgnw_ref, st_ref):
        _zero_first(pl.program_id(1) == 0, st_ref, dgnw_ref)
        bd = bd_ref[...]
        avg = bd * (1.0 / HEAD)
        chunks = [slice(ci * c, (ci + 1) * c) for ci in range(nc)]
        lanes = [slice(sl * LANES, (sl + 1) * LANES) for sl in range(ns)]
        dov = []
        for ln in lanes:
            ov = o_ref[:, ln]
            ctr = ov - _dot_hi(ov, avg)
            rs = lax.rsqrt(_dot_hi(ctr * ctr, avg) + EPS)
            oh = ctr * rs
            gg, dyv, gnw_v = g_ref[:, ln], dy_ref[:, ln], gnw_ref[:, ln]
            sg = _sigmoid(gg)
            sl = gg * sg
            dg_ref[:, ln] = (dyv * oh * gnw_v * _dsilu(gg, sg)).astype(BF16)
            dgnw_ref[:, ln] += _colsum(dyv * sl * oh)
            doh = dyv * sl * gnw_v
            dov.append((rs * (doh - _dot_hi(doh, avg) - oh * _dot_hi(doh * oh, avg))).astype(BF16))
            do_ref[:, ln] = dov[-1]
        states = _ret_states(k_ref, v_ref, zeta_ref, cd_ref, bd, st_ref, chunks, lanes, False)
        for ci, rows in enumerate(chunks):
            for sl, ln in enumerate(lanes):
                doc = dov[sl][rows, :]
                dq = (_dot_nt(doc, states[sl][ci]) * xi_ref[sl]
                      + _pair_product(doc, _stack_heads(v_ref[rows, ln]), dm_ref[sl], _stack_heads(k_ref[rows, ln])))
                dq_ref[rows, ln] = _unrope(dq, cr_ref[rows, :], sr_ref[rows, :], HEAD // 2).astype(BF16)

    specs = _ret_specs(tr, lambda i: i)
    sd = jax.ShapeDtypeStruct
    return pl.pallas_call(
        body, name="ret_bwd_q", grid=(4 // ns, s // tr),
        in_specs=[specs["slab"]] * 6 + [specs["vec"], specs["dmask"], specs["rows"], specs["rows"], specs["state"], specs["bd"],
                                        specs["tab"], specs["tab"]],
        out_specs=[specs["slab"]] * 3 + [specs["vec"]],
        out_shape=[sd((s, RET_W), BF16), sd((s, RET_W), BF16), sd((s, RET_W), BF16), sd((1, RET_W), F32)],
        scratch_shapes=[pltpu.VMEM((ns, LANES, LANES), F32)],
        compiler_params=_cp("parallel", "arbitrary"),
    )(q, k, v, o, g, dy, gnw, rc["dmask"], rc["zeta"], rc["xi"], rc["cd"], rc["bd"], cos_r, sin_r)


def _ret_bwd_kv_call(q, k, v, do, rc, cos_r, sin_r, tr):
    s = q.shape[0]
    c = RET_CHUNK
    nc = tr // c
    nt = s // tr
    ns = RET_SLABS

    def body(q_ref, k_ref, v_ref, do_ref, dm_ref, zeta_ref, xi_ref, cd_ref, bd_ref, cr_ref, sr_ref, dk_ref, dv_ref, gs_ref):
        _zero_first(pl.program_id(1) == 0, gs_ref)
        bd = bd_ref[...]
        chunks = [slice(ci * c, (ci + 1) * c) for ci in range(nc)]
        lanes = [slice(sl * LANES, (sl + 1) * LANES) for sl in range(ns)]
        states = _ret_states(q_ref, do_ref, xi_ref, cd_ref, bd, gs_ref, chunks, lanes, True)
        for ci, rows in enumerate(chunks):
            for sl, ln in enumerate(lanes):
                kc, vc = k_ref[rows, ln], v_ref[rows, ln]
                q2, do2 = _stack_heads(q_ref[rows, ln]), _stack_heads(do_ref[rows, ln])
                gb = states[sl][ci]
                dk = _dot_nt(vc, gb) * zeta_ref[sl] + _pair_product(vc, do2, dm_ref[sl], q2)
                dv = _dot(kc, gb) * zeta_ref[sl] + _pair_product(kc, q2, dm_ref[sl], do2)
                dk_ref[rows, ln] = (_unrope(dk, cr_ref[rows, :], sr_ref[rows, :], HEAD // 2) * (HEAD ** -0.5)).astype(BF16)
                dv_ref[rows, ln] = dv.astype(BF16)

    specs = _ret_specs(tr, lambda i: nt - 1 - i)
    sd = jax.ShapeDtypeStruct
    return pl.pallas_call(
        body, name="ret_bwd_kv", grid=(4 // ns, nt),
        in_specs=[specs["slab"]] * 4 + [specs["dmask"], specs["rows"], specs["rows"], specs["state"], specs["bd"],
                                        specs["tab"], specs["tab"]],
        out_specs=[specs["slab"]] * 2,
        out_shape=[sd((s, RET_W), BF16), sd((s, RET_W), BF16)],
        scratch_shapes=[pltpu.VMEM((ns, LANES, LANES), F32)],
        compiler_params=_cp("parallel", "arbitrary"),
    )(q, k, v, do, rc["dmask_t"], rc["zeta"], rc["xi"], rc["cd"], rc["bd"], cos_r, sin_r)


FLASH_BWD_HEADS = 8


def _flash_bwd_call(qb, k, v, do, tb, order=None):
    s = qb.shape[1]
    nb = s // tb
    hg = FLASH_BWD_HEADS
    pairs = [(a, b) for a in range(nb) for b in range(a, nb)]
    ki_of, qi_of = (jnp.asarray(np.array(col, np.int32)) for col in zip(*pairs))
    extra = [] if order is None else [order]

    def body(ki_ref, qi_ref, *refs):
        q_ref, k_ref, v_ref, do_ref, dk_ref, dv_ref, dq_hbm, dka_ref, dva_ref, dq_ref, sem = refs[len(extra):]
        g, p = pl.program_id(0), pl.program_id(1)
        ki, qi = ki_ref[p], qi_ref[p]
        _zero_first(p == 0, dq_ref)
        _zero_first(qi == ki, dka_ref, dva_ref)
        rows = pl.ds(pl.multiple_of(qi * tb, tb), tb)

        def step(masked):
            if masked:
                keep = lax.broadcasted_iota(jnp.int32, (tb, tb), 0) <= lax.broadcasted_iota(jnp.int32, (tb, tb), 1)
            for h in range(hg):
                st = _dot_nt(k_ref[h], q_ref[h])
                if masked:
                    st = jnp.where(keep, st, NEG)
                pt = jnp.exp2(st)
                dob = do_ref[h]
                dva_ref[h] += _dot(pt.astype(BF16), dob)
                dst = (pt * _dot_nt(v_ref[h], dob)).astype(BF16)
                dka_ref[h] += _dot(dst, q_ref[h])
                dq_ref[h, rows, :] += _dot_tn(dst, k_ref[h])

        @pl.when(qi > ki)
        def _():
            step(False)

        @pl.when(qi == ki)
        def _():
            step(True)

        @pl.when(qi == nb - 1)
        def _():
            dk_ref[...] = (dka_ref[...] * LN2).astype(BF16)
            dv_ref[...] = dva_ref[...].astype(BF16)

        @pl.when(p == len(pairs) - 1)
        def _():
            cp = pltpu.make_async_copy(dq_ref, dq_hbm.at[pl.ds(g * hg, hg)], sem)
            cp.start()
            cp.wait()

    kspec = pl.BlockSpec((hg, tb, LANES), lambda g, p, ki_ref, qi_ref: (g, ki_ref[p], 0))
    qspec = pl.BlockSpec((hg, tb, LANES), lambda g, p, ki_ref, qi_ref: (g, qi_ref[p], 0))
    hm = jax.ShapeDtypeStruct((N_HEADS, s, LANES), BF16)
    return pl.pallas_call(
        body, name="mla_flash_bwd",
        grid_spec=pltpu.PrefetchScalarGridSpec(
            num_scalar_prefetch=2, grid=(N_HEADS // hg, len(pairs)),
            in_specs=[ANY] * len(extra) + [qspec, kspec, kspec, qspec],
            out_specs=[kspec, kspec, ANY],
            scratch_shapes=[pltpu.VMEM((hg, tb, LANES), F32), pltpu.VMEM((hg, tb, LANES), F32),
                            pltpu.VMEM((hg, s, LANES), F32), pltpu.SemaphoreType.DMA]),
        out_shape=[hm, hm, jax.ShapeDtypeStruct((N_HEADS, s, LANES), F32)],
        compiler_params=_cp("arbitrary", "arbitrary"),
    )(ki_of, qi_of, *extra, qb, k, v, do)


def _mla_post_call(dq, dk, dv, cq, ckv, qnw, kvnw, wq, wk, wv, cos_m, sin_m, ts):
    s = cq.shape[0]

    def body(dq_ref, dk_ref, dv_ref, cq_ref, ckv_ref, qnw_ref, kvnw_ref, wq_ref, wk_ref, wv_ref, cm_ref, sm_ref,
             dcq_ref, dckv_ref, dkpe_ref, dwq_ref, dwk_ref, dwv_ref, dqnw_ref, dkvnw_ref):
        _zero_first(pl.program_id(0) == 0, dwq_ref, dwk_ref, dwv_ref, dqnw_ref, dkvnw_ref)
        cqv, ckvv = cq_ref[...], ckv_ref[...]
        rq, rkv = _rstd(cqv), _rstd(ckvv)
        qh_, kvh_ = cqv * rq, ckvv * rkv
        qnw_v, kvnw_v = qnw_ref[...], kvnw_ref[...]
        cqn = (qh_ * qnw_v).astype(BF16)
        ckvn = (kvh_ * kvnw_v).astype(BF16)
        cm, sm = cm_ref[...], sm_ref[...]
        dcqn = jnp.zeros((ts, Q_RANK), F32)
        dckvn = jnp.zeros((ts, KV_RANK), F32)
        dkpe = jnp.zeros((ts, LANES), F32)
        for h in range(N_HEADS):
            dqu = _unrope(dq_ref[h] * SM_SCALE, cm, sm, ROPE // 2).astype(BF16)
            dwq_ref[h] += _dot_tn(cqn, dqu)
            dcqn = dcqn + _dot_nt(dqu, wq_ref[h])
            dkb, dvb = dk_ref[h], dv_ref[h]
            dkpe = dkpe + dkb.astype(F32)
            dwk_ref[h] += _dot_tn(ckvn, dkb)
            dwv_ref[h] += _dot_tn(ckvn, dvb)
            dckvn = dckvn + _dot_nt(dkb, wk_ref[h]) + _dot_nt(dvb, wv_ref[h])
        lane = lax.broadcasted_iota(jnp.int32, (ts, LANES), 1)
        dkpe = jnp.where((lane >= KPE_LO) & (lane < KPE_LO + ROPE), dkpe, 0.0)
        dkpe_ref[...] = _unrope(dkpe, cm, sm, ROPE // 2).astype(BF16)
        dqnw_ref[...] += _colsum(dcqn * qh_)
        dkvnw_ref[...] += _colsum(dckvn * kvh_)
        dcq_ref[...] = _norm_bwd(dcqn, qh_, rq, qnw_v).astype(BF16)
        dckv_ref[...] = _norm_bwd(dckvn, kvh_, rkv, kvnw_v).astype(BF16)

    sd = jax.ShapeDtypeStruct
    hm = _hrow(N_HEADS, ts, LANES)
    return pl.pallas_call(
        body, name="mla_post", grid=(s // ts,),
        in_specs=[hm, hm, hm, _row(ts, Q_RANK), _row(ts, KV_RANK), _full((1, Q_RANK)), _full((1, KV_RANK)),
                  _full((N_HEADS, Q_RANK, LANES)), _full((N_HEADS, KV_RANK, LANES)), _full((N_HEADS, KV_RANK, LANES)),
                  _row(ts, LANES), _row(ts, LANES)],
        out_specs=[_row(ts, Q_RANK), _row(ts, KV_RANK), _row(ts, LANES),
                   _full((N_HEADS, Q_RANK, LANES)), _full((N_HEADS, KV_RANK, LANES)), _full((N_HEADS, KV_RANK, LANES)),
                   _full((1, Q_RANK)), _full((1, KV_RANK))],
        out_shape=[sd((s, Q_RANK), BF16), sd((s, KV_RANK), BF16), sd((s, LANES), BF16),
                   sd((N_HEADS, Q_RANK, LANES), F32), sd((N_HEADS, KV_RANK, LANES), F32), sd((N_HEADS, KV_RANK, LANES), F32),
                   sd((1, Q_RANK), F32), sd((1, KV_RANK), F32)],
        compiler_params=_cp("arbitrary"),
    )(dq, dk, dv, cq, ckv, qnw, kvnw, wq, wk, wv, cos_m, sin_m)


def _in_bwd_call(parts, x, r1, anw, dx1, win, ts):
    s = x.shape[0]
    widths = [p.shape[1] for p in parts]
    np_ = len(parts)

    def body(*refs):
        p_refs = refs[:np_]
        x_ref, r_ref, anw_ref, dx1_ref, w_ref, dx_ref, dw_ref, danw_ref = refs[np_:]
        _zero_first(pl.program_id(0) == 0, dw_ref, danw_ref)
        dproj = jnp.concatenate([p[...] for p in p_refs], axis=-1)
        r, anw_v = r_ref[...], anw_ref[...]
        xh = x_ref[...] * r
        dw_ref[...] += _dot_tn((xh * anw_v).astype(BF16), dproj)
        dh = _dot_nt(dproj, w_ref[...])
        danw_ref[...] += _colsum(dh * xh)
        dx_ref[...] = dx1_ref[...] + _norm_bwd(dh, xh, r, anw_v)

    sd = jax.ShapeDtypeStruct
    return pl.pallas_call(
        body, name="in_proj_bwd", grid=(s // ts,),
        in_specs=[_row(ts, w) for w in widths]
        + [_row(ts, D_MODEL), _row(ts, 1), _full((1, D_MODEL)), _row(ts, D_MODEL), _full((D_MODEL, IN_EXT))],
        out_specs=[_row(ts, D_MODEL), _full((D_MODEL, IN_EXT)), _full((1, D_MODEL))],
        out_shape=[sd((s, D_MODEL), F32), sd((D_MODEL, IN_EXT), F32), sd((1, D_MODEL), F32)],
        compiler_params=_cp("arbitrary"),
    )(*parts, x, r1, anw, dx1, win)


def _local_step(x, positions, tgt, w, small, ex=None):
    s = x.shape[0]
    t = _tiles(s)
    ex = _Exchanges(w) if ex is None else ex
    f = _forward(x, positions, tgt, w, small, ex)
    pw, rc = f["pw"], f["rc"]
    cos_r, sin_r, cos_m, sin_m = f["tabs"]
    dx2, loss, g_fw = f["dx2"], f["loss"], f["g_fw"]
    du, dx1, g_cw, g_cb, g_fnw, g_wd = _ffn_bwd_call(dx2, f["u"], f["uc"], w["conv_w"], pw["wdown"], pw["wup"],
                                                     f["x1"], f["r2"], small["ffn_norm_w"], t["t2"])
    g_wup = _dw_norm_call(f["x1"], f["r2"], small["ffn_norm_w"], du, t["tw"], F2 // 4, "dw_up")
    started = ex.mlp_grads(dict(w_up=g_wup, w_down=g_wd))
    dy_ret, do, g_wout = _out_bwd_call(dx1, f["y_ret"], f["y_mla"], pw["wout"], t["t1"], started)
    started = ex.behind_out_bwd(g_wout)
    drq, dg, do_ret, g_gnw = _ret_bwd_q_call(f["q"], f["k"], f["v"], f["o_ret"], f["g"], dy_ret, small["ret_gn_w"], rc, cos_r, sin_r, t["tr"])
    drk, drv = _ret_bwd_kv_call(f["q"], f["k"], f["v"], do_ret, rc, cos_r, sin_r, t["tr"])
    dmk, dmv, dmq = _flash_bwd_call(f["mqb"], f["mk"], f["mv"], do, t["tb"], started)
    ex.behind_attention(dmk)
    dcq, dckv, dkpe, g_wq, g_wk, g_wv, g_qnw, g_kvnw = _mla_post_call(
        dmq, dmk, dmv, f["cq"], f["ckv"], small["mla_q_norm_w"], small["mla_kv_norm_w"], pw["wq"], pw["wk"], pw["wv"], cos_m, sin_m, t["ts"])
    gx, g_win_ext, g_anw = _in_bwd_call([drq, drk, drv, dg, dcq, dckv, dkpe], x, f["r1"], small["attn_norm_w"], dx1, pw["win"], t["ts"])
    lo = IN_W - ROPE
    g_win = jnp.concatenate([g_win_ext[:, :lo], g_win_ext[:, lo + KPE_LO:lo + KPE_LO + ROPE]], -1)
    g_wuq = g_wq.transpose(1, 0, 2)[:, :, :HEAD + ROPE].reshape(Q_RANK, N_HEADS * (HEAD + ROPE))
    g_wukv = jnp.concatenate([g_wk[:, :, :HEAD], g_wv[:, :, :HEAD]], -1).transpose(1, 0, 2).reshape(KV_RANK, 2 * MLA_W)
    gw = dict(w_in=g_win, w_uq=g_wuq, w_ukv=g_wukv, w_out=g_wout, w_up=g_wup,
              conv_w=g_cw, w_down=g_wd)
    gs = dict(attn_norm_w=g_anw, ret_gn_w=g_gnw, mla_q_norm_w=g_qnw, mla_kv_norm_w=g_kvnw, ffn_norm_w=g_fnw,
              conv_b=g_cb, final_norm_w=g_fw)
    return loss, gx, gw, gs


MESH_ID = pl.DeviceIdType.MESH
ANY = pl.BlockSpec(memory_space=pl.ANY)
VMEM_SPEC = pl.BlockSpec(memory_space=pltpu.VMEM)
N_DEV = 8
GROUP_A = (("w_in", (D_MODEL, IN_W // 4), 1), ("w_uq", (Q_RANK, 192), 1), ("w_ukv", (KV_RANK, 256), 1),
           ("w_out", (D_MODEL // 4, D_MODEL), 0))
GROUP_B = (("w_up", (D_MODEL, F2 // 4), 1), ("w_down", (D_FF // 4, D_MODEL), 0))
HBM_SPEC = pl.BlockSpec(memory_space=pltpu.HBM)
SEM_SPEC = pl.BlockSpec(memory_space=pltpu.SEMAPHORE)


def _mesh_pos():
    return lax.axis_index("x"), lax.axis_index("y"), lax.axis_index("c")


def _other_chips(x, y):
    return [(1 - x, y), (x, 1 - y), (1 - x, 1 - y)]


def _remote(src, dst, send_sems, recv_sems, k, dev):
    return pltpu.make_async_remote_copy(src_ref=src, dst_ref=dst, send_sem=send_sems.at[k], recv_sem=recv_sems.at[k],
                                        device_id=dev, device_id_type=MESH_ID)


def _gather_list_call(parts, tag):
    n = len(parts)

    def body(*refs):
        srcs, outs, (send_sems, recv_sems) = refs[:n], refs[n:2 * n], refs[2 * n:]
        x, y, c = _mesh_pos()
        sm = 2 * x + y
        chips = _other_chips(x, y)
        sib = (x, y, 1 - c)
        rc = lambda k, src, dst, dev: _remote(src, dst, send_sems, recv_sems, k, dev)
        first = [rc(7 * i + j, srcs[i].at[c], outs[i].at[sm, c], (cx, cy, c)) for i in range(n) for j, (cx, cy) in enumerate(chips)]
        own = [rc(7 * i + 6, srcs[i], outs[i].at[sm], sib) for i in range(n)]
        for cp in first + own:
            cp.start()
        passed = []
        for j, (cx, cy) in enumerate(chips):
            for i in range(n):
                land = outs[i].at[2 * cx + cy, c]
                rc(7 * i + j, srcs[i].at[c], land, (cx, cy, c)).wait_recv()
                cp = rc(7 * i + 3 + j, land, land, sib)
                cp.start()
                passed.append(cp)
        for j, (cx, cy) in enumerate(chips):
            for i in range(n):
                rc(7 * i + 3 + j, srcs[i].at[c], outs[i].at[2 * cx + cy, 1 - c], sib).wait_recv()
        for cp in own:
            cp.wait_recv()
        for cp in first + passed + own:
            cp.wait_send()

    return pl.pallas_call(
        body, name="weights_all_gather_" + tag,
        in_specs=[ANY] * n, out_specs=[ANY] * n,
        out_shape=[jax.ShapeDtypeStruct((4,) + p.shape, p.dtype) for p in parts],
        scratch_shapes=[pltpu.SemaphoreType.DMA((7 * n,)), pltpu.SemaphoreType.DMA((7 * n,))],
    )(*parts)


def _direct_gather_copies(srcs, lands, send_sems, recv_sems):
    x, y, c = _mesh_pos()
    sm = 2 * x + y
    sends, recvs = [], []
    for i, (src, land) in enumerate(zip(srcs, lands)):
        for j, (cx, cy) in enumerate(_other_chips(x, y)):
            for t in range(2):
                sends.append(_remote(src.at[c], land.at[sm, c], send_sems, recv_sems, 13 * i + 4 * j + 2 * c + t, (cx, cy, t)))
                recvs.append(_remote(src.at[t], land.at[2 * cx + cy, t], send_sems, recv_sems, 13 * i + 4 * j + 2 * t + c, (cx, cy, t)))
        sends.append(_remote(src, land.at[sm], send_sems, recv_sems, 13 * i + 12, (x, y, 1 - c)))
        recvs.append(_remote(src, land.at[sm], send_sems, recv_sems, 13 * i + 12, (x, y, 1 - c)))
    return sends, recvs


def _sibling_copies(srcs, lands, send_sems, recv_sems):
    x, y, c = _mesh_pos()
    cps = [_remote(src.at[s, 1 - c], land.at[s], send_sems, recv_sems, 4 * i + s, (x, y, 1 - c))
           for i, (src, land) in enumerate(zip(srcs, lands)) for s in range(4)]
    return cps, cps


def _chips_copies(srcs, lands, send_sems, recv_sems):
    x, y, c = _mesh_pos()
    cps = [_remote(src.at[2 * cx + cy], land.at[j], send_sems, recv_sems, 3 * i + j, (cx, cy, c))
           for i, (src, land) in enumerate(zip(srcs, lands)) for j, (cx, cy) in enumerate(_other_chips(x, y))]
    return cps, cps


def _share_copies(srcs, lands, send_sems, recv_sems):
    x, y, c = _mesh_pos()
    cps = [_remote(src, land, send_sems, recv_sems, i, (x, y, 1 - c)) for i, (src, land) in enumerate(zip(srcs, lands))]
    return cps, cps


def _exchange_call(name, copies, srcs, land_shapes, n_sems):
    n = len(srcs)

    def body(*refs):
        sends, recvs = copies(refs[:n], refs[n:2 * n], refs[2 * n], refs[2 * n + 1])
        for cp in sends:
            cp.start()
        for cp in sends:
            cp.wait_send()
        for cp in recvs:
            cp.wait_recv()

    return pl.pallas_call(
        body, name=name, in_specs=[ANY] * n, out_specs=[ANY] * n, out_shape=list(land_shapes),
        scratch_shapes=[pltpu.SemaphoreType.DMA((n_sems,)), pltpu.SemaphoreType.DMA((n_sems,))],
    )(*srcs)


def _exchange_start_call(name, copies, srcs, land_shapes, n_sems, order=None):
    n = len(srcs)
    extra = [] if order is None else [order]
    k = 2 * n + len(extra)

    def body(*refs):
        sends, _ = copies(refs[:n], refs[n:2 * n], refs[k], refs[k + 1])
        for cp in sends:
            cp.start()
        refs[-1][...] = jnp.zeros_like(refs[-1])

    hbm = lambda a: pltpu.with_memory_space_constraint(a, pltpu.HBM)
    lands = [hbm(lax.empty(sd.shape, sd.dtype)) for sd in land_shapes]
    sem = pltpu.SemaphoreType.DMA((n_sems,))
    out = pl.pallas_call(
        body, name=name,
        out_shape=(sem, sem, *[pltpu.HBM(a.shape, a.dtype) for a in list(srcs) + lands], jax.ShapeDtypeStruct((8, LANES), F32)),
        in_specs=[HBM_SPEC] * (2 * n) + [ANY] * len(extra), out_specs=(SEM_SPEC, SEM_SPEC, *[HBM_SPEC] * (2 * n), VMEM_SPEC),
        input_output_aliases={i: 2 + i for i in range(2 * n)},
        compiler_params=pltpu.CompilerParams(has_side_effects=pltpu.SideEffectType.DATAFLOW_SIDE_EFFECTING),
    )(*[hbm(a) for a in srcs], *lands, *extra)
    return out[0], out[1], out[2:2 + n], out[2 + n:2 + 2 * n], out[-1]


def _exchange_wait_call(name, copies, started, after):
    send_sems, recv_sems, srcs, lands, _ = started
    n = len(srcs)

    def body(*refs):
        sends, recvs = copies(refs[:n], refs[n:2 * n], refs[2 * n], refs[2 * n + 1])
        for cp in sends:
            cp.wait_send()
        for cp in recvs:
            cp.wait_recv()

    out = pl.pallas_call(
        body, name=name,
        out_shape=tuple(pltpu.HBM(a.shape, a.dtype) for a in list(srcs) + list(lands)),
        in_specs=[HBM_SPEC] * (2 * n) + [SEM_SPEC, SEM_SPEC, ANY], out_specs=tuple([HBM_SPEC] * (2 * n)),
        input_output_aliases={i: i for i in range(2 * n)},
        compiler_params=pltpu.CompilerParams(has_side_effects=pltpu.SideEffectType.DATAFLOW_SIDE_EFFECTING),
    )(*srcs, *lands, send_sems, recv_sems, after)
    return out[:n], out[n:]


def _rows_tile(rows, width, itemsize=4):
    limit = max(16, (3 << 20) // (width * itemsize))
    if rows <= limit:
        return rows
    return max(t for t in range(16, limit + 1, 16) if rows % t == 0)


def _sum_sibling_call(g, buf, c, name):
    _, _, rh, w = g.shape
    tile = _rows_tile(rh, w)

    def body(c_ref, g_ref, b_ref, p_ref, pb_ref):
        p = g_ref[...] + b_ref[...]
        p_ref[...] = p
        pb_ref[...] = p.astype(BF16)

    blk = pl.BlockSpec((None, tile, w), lambda s, i, c_ref: (s, i, 0))
    return pl.pallas_call(
        body, name=name,
        grid_spec=pltpu.PrefetchScalarGridSpec(
            num_scalar_prefetch=1, grid=(4, rh // tile),
            in_specs=[pl.BlockSpec((None, None, tile, w), lambda s, i, c_ref: (s, c_ref[0], i, 0)), blk],
            out_specs=[blk, blk]),
        out_shape=[jax.ShapeDtypeStruct((4, rh, w), F32), jax.ShapeDtypeStruct((4, rh, w), BF16)],
        compiler_params=_cp("parallel", "parallel"),
    )(c, g, buf)


def _sum_chips_call(p, buf, sm, name):
    _, rh, w = p.shape
    tile = _rows_tile(rh, w)

    def body(sm_ref, p_ref, b_ref, f_ref):
        f_ref[...] = ((p_ref[...] + b_ref[0].astype(F32)) + b_ref[1].astype(F32)) + b_ref[2].astype(F32)

    return pl.pallas_call(
        body, name=name,
        grid_spec=pltpu.PrefetchScalarGridSpec(
            num_scalar_prefetch=1, grid=(rh // tile,),
            in_specs=[pl.BlockSpec((None, tile, w), lambda i, sm_ref: (sm_ref[0], i, 0)),
                      pl.BlockSpec((3, tile, w), lambda i, sm_ref: (0, i, 0))],
            out_specs=pl.BlockSpec((tile, w), lambda i, sm_ref: (i, 0))),
        out_shape=jax.ShapeDtypeStruct((rh, w), F32),
        compiler_params=_cp("parallel"),
    )(sm, p, buf)


def _adamw_halves_call(w, g_mine, g_sib, c, m, v, name):
    r, wd = w.shape
    rh = r // 2
    tile = _rows_tile(rh, wd)
    nt = rh // tile

    def body(c_ref, w_ref, gm_ref, gs_ref, m_ref, v_ref, g_ref, d_ref, nm_ref, nv_ref):
        gv = jnp.where(pl.program_id(0) == c_ref[0], gm_ref[...], gs_ref[...])
        g_ref[...] = gv
        nm = ADAM_B1 * m_ref[...] + (1.0 - ADAM_B1) * gv
        nv = ADAM_B2 * v_ref[...] + (1.0 - ADAM_B2) * jnp.square(gv)
        m_hat = nm / (1.0 - ADAM_B1 ** ADAM_STEP)
        v_hat = nv / (1.0 - ADAM_B2 ** ADAM_STEP)
        d_ref[...] = -ADAM_LR * (m_hat / (jnp.sqrt(v_hat) + ADAM_EPS) + ADAM_WD * w_ref[...])
        nm_ref[...] = nm
        nv_ref[...] = nv

    whole = pl.BlockSpec((tile, wd), lambda h, i, c_ref: (h * nt + i, 0))
    half = pl.BlockSpec((tile, wd), lambda h, i, c_ref: (i, 0))
    sd = jax.ShapeDtypeStruct((r, wd), F32)
    return pl.pallas_call(
        body, name=name,
        grid_spec=pltpu.PrefetchScalarGridSpec(
            num_scalar_prefetch=1, grid=(2, nt),
            in_specs=[whole, half, half, whole, whole], out_specs=[whole] * 4),
        out_shape=[sd, sd, sd, sd],
        compiler_params=_cp("parallel", "parallel"),
    )(c, w, g_mine, g_sib, m, v)


def _all_reduce8_call(vec, name):
    rows = vec.shape[0]

    def body(v_ref, out_ref, slots, send_sems, recv_sems):
        x, y, c = _mesh_pos()
        me = 4 * x + 2 * y + c
        slots[me] = v_ref[...]

        def rcopy(k, to_me):
            bx, by, bc = (k >> 2) & 1, (k >> 1) & 1, k & 1
            px, py, pc = (1 - x if bx else x), (1 - y if by else y), (1 - c if bc else c)
            slot = 4 * px + 2 * py + pc if to_me else me
            return pltpu.make_async_remote_copy(src_ref=v_ref, dst_ref=slots.at[slot], send_sem=send_sems.at[k - 1],
                                                recv_sem=recv_sems.at[k - 1], device_id=(px, py, pc), device_id_type=MESH_ID)

        for k in range(1, N_DEV):
            rcopy(k, False).start()
        for k in range(1, N_DEV):
            rcopy(k, True).wait_recv()
        for k in range(1, N_DEV):
            rcopy(k, False).wait_send()
        tot = slots[0]
        for d in range(1, N_DEV):
            tot = tot + slots[d]
        out_ref[...] = tot

    return pl.pallas_call(
        body, name=name,
        in_specs=[VMEM_SPEC], out_specs=VMEM_SPEC,
        out_shape=jax.ShapeDtypeStruct((rows, LANES), F32),
        scratch_shapes=[pltpu.VMEM((N_DEV, rows, LANES), F32),
                        pltpu.SemaphoreType.DMA((N_DEV - 1,)), pltpu.SemaphoreType.DMA((N_DEV - 1,))],
    )(vec)


def _adamw_call(w, g, m, v, name):
    r, c = w.shape
    rb = r if r <= 256 else (256 if r % 256 == 0 else 352)
    assert r % rb == 0

    def body(w_ref, g_ref, m_ref, v_ref, d_ref, nm_ref, nv_ref):
        gv = g_ref[...]
        nm = ADAM_B1 * m_ref[...] + (1.0 - ADAM_B1) * gv
        nv = ADAM_B2 * v_ref[...] + (1.0 - ADAM_B2) * jnp.square(gv)
        m_hat = nm / (1.0 - ADAM_B1 ** ADAM_STEP)
        v_hat = nv / (1.0 - ADAM_B2 ** ADAM_STEP)
        d_ref[...] = -ADAM_LR * (m_hat / (jnp.sqrt(v_hat) + ADAM_EPS) + ADAM_WD * w_ref[...])
        nm_ref[...] = nm
        nv_ref[...] = nv

    spec = pl.BlockSpec((rb, c), lambda i: (i, 0))
    sd = jax.ShapeDtypeStruct((r, c), F32)
    return pl.pallas_call(
        body, name=name, grid=(r // rb,),
        in_specs=[spec] * 4, out_specs=[spec] * 3, out_shape=[sd, sd, sd],
        compiler_params=_cp("parallel"),
    )(w, g, m, v)


SMALL = (("attn_norm_w", D_MODEL), ("ret_gn_w", RET_W), ("mla_q_norm_w", Q_RANK), ("mla_kv_norm_w", KV_RANK),
         ("ffn_norm_w", D_MODEL), ("conv_b", F2), ("final_norm_w", D_MODEL))
WEIGHT_ORDER = ("attn_norm_w", "w_in", "ret_gn_w", "mla_q_norm_w", "w_uq", "mla_kv_norm_w", "w_ukv", "w_out",
                "ffn_norm_w", "w_up", "conv_w", "conv_b", "w_down", "final_norm_w")


def _pad_rows(flat, rows):
    return jnp.concatenate([flat, jnp.zeros((rows * LANES - flat.shape[0],), flat.dtype)]).reshape(rows, LANES)


def kernel(x, positions, attn_norm_w, w_in, ret_gn_w, mla_q_norm_w, w_uq, mla_kv_norm_w, w_ukv, w_out, ffn_norm_w, w_up, conv_w, conv_b, w_down, final_norm_w, loss_target, m_attn_norm_w, m_w_in, m_ret_gn_w, m_mla_q_norm_w, m_w_uq, m_mla_kv_norm_w, m_w_ukv, m_w_out, m_ffn_norm_w, m_w_up, m_conv_w, m_conv_b, m_w_down, m_final_norm_w, v_attn_norm_w, v_w_in, v_ret_gn_w, v_mla_q_norm_w, v_w_uq, v_mla_kv_norm_w, v_w_ukv, v_w_out, v_ffn_norm_w, v_w_up, v_conv_w, v_conv_b, v_w_down, v_final_norm_w):
    args = dict(locals())
    cx, cy, cc = _mesh_pos()
    sm = 2 * cx + cy

    c_arr, sm_arr = cc.reshape(1).astype(jnp.int32), sm.reshape(1).astype(jnp.int32)
    sds = jax.ShapeDtypeStruct

    def my_shards(group):
        return [args[n][0].astype(BF16).reshape(2, r // 2, c) for n, (r, c), _ in group]

    def full_weights(gathered, group):
        full = {}
        for (n, (r, c), axis), got in zip(group, gathered):
            piece = got.reshape(4, r, c)
            full[n] = piece if n == "w_up" else (piece.transpose(1, 0, 2).reshape(r, 4 * c) if axis == 1 else piece.reshape(4 * r, c))
        return full

    def by_owner(gw, group):
        out = []
        for n, (r, c), axis in group:
            g = gw[n]
            if axis == 1 and g.ndim == 2:
                g = g.reshape(r, 4, c).transpose(1, 0, 2)
            out.append(g.reshape(4, 2, r // 2, c))
        return out

    def sibling_shapes(gs):
        return [sds((4,) + g.shape[2:], F32) for g in gs]

    def chip_sums(gs, bufs, group):
        res = [_sum_sibling_call(g, b, c_arr, "grads_sum_sibling_" + n) for g, b, (n, _, _) in zip(gs, bufs, group)]
        return [p for p, _ in res], [pb for _, pb in res]

    def chips_shapes(pbs):
        return [sds((3,) + pb.shape[1:], BF16) for pb in pbs]

    def totals(ps, lands, group, tag):
        fins = [_sum_chips_call(p, l, sm_arr, "grads_sum_chips_" + n) for p, l, (n, _, _) in zip(ps, lands, group)]
        sibs = _exchange_call("grads_rs_share_" + tag, _share_copies, fins, [sds(f.shape, F32) for f in fins], len(fins))
        return {n: (f, s) for (n, _, _), f, s in zip(group, fins, sibs)}

    class StepExchanges(_Exchanges):
        def __init__(self, order):
            shards = my_shards(GROUP_B)
            self.gather = _exchange_start_call("weights_gather_start_b", _direct_gather_copies, shards,
                                               [sds((4,) + s.shape, BF16) for s in shards], 13 * len(shards), order)
            self.red = None

        def token(self):
            return self.gather[4][0:1, 0:1]

        def mlp_weights(self, after):
            return full_weights(_exchange_wait_call("weights_gather_wait_b", _direct_gather_copies, self.gather, after)[1], GROUP_B)

        def mlp_grads(self, gw):
            gs = by_owner(gw, GROUP_B)
            self.step1 = _exchange_start_call("grads_rs_sibling_start_b", _sibling_copies, gs, sibling_shapes(gs), 4 * len(gs))
            return self.step1[4]

        def behind_out_bwd(self, after):
            gs, bufs = _exchange_wait_call("grads_rs_sibling_wait_b", _sibling_copies, self.step1, after)
            self.ps, pbs = chip_sums(gs, bufs, GROUP_B)
            self.step2 = _exchange_start_call("grads_rs_chips_start_b", _chips_copies, pbs, chips_shapes(pbs), 3 * len(pbs))
            return self.step2[4]

        def behind_attention(self, after):
            _, lands = _exchange_wait_call("grads_rs_chips_wait_b", _chips_copies, self.step2, after)
            self.red = totals(self.ps, lands, GROUP_B, "b")

    gathered = _gather_list_call(my_shards(GROUP_A) + [conv_w[0].reshape(2, 1, 3 * F2 // 8)], "a")
    full = full_weights(gathered[:-1], GROUP_A)
    ex = StepExchanges(gathered[-1])
    full["conv_w"] = gathered[-1].reshape(4, 3, F2 // 4).transpose(1, 0, 2).reshape(3, F2)
    small = {n: args[n].reshape(1, d) for n, d in SMALL}
    small["attn_norm_w"] = small["attn_norm_w"] + ex.token()

    loss, gx, gw, gs = _local_step(x[0], positions[0], loss_target[0], full, small, ex)

    ga = by_owner(gw, GROUP_A)
    bufs = _exchange_call("grads_rs_sibling_a", _sibling_copies, ga, sibling_shapes(ga), 4 * len(ga))
    ps, pbs = chip_sums(ga, bufs, GROUP_A)
    lands = _exchange_call("grads_rs_chips_a", _chips_copies, pbs, chips_shapes(pbs), 3 * len(pbs))
    halves = {**ex.red, **totals(ps, lands, GROUP_A, "a")}

    vec = jnp.concatenate([gs[n].reshape(-1) for n, _ in SMALL] + [gw["conv_w"].reshape(-1), loss.reshape(-1)])
    tot = _all_reduce8_call(_pad_rows(vec, 216), "small_all_reduce").reshape(-1)
    red, off = {}, 0
    for n, d in SMALL:
        red[n] = tot[off:off + d].reshape(1, d)
        off += d
    red["conv_w"] = lax.dynamic_slice(tot[off:off + 3 * F2].reshape(3, F2), (0, sm * (F2 // 4)), (3, F2 // 4))
    loss_tot = tot[off + 3 * F2]

    grads, deltas, new_m, new_v = [], [], [], []
    for n in WEIGHT_ORDER:
        shape = args[n].shape
        two_d = (1, shape[0]) if len(shape) == 1 else shape[-2:]
        wmv = [args[k + n].reshape(two_d) for k in ("", "m_", "v_")]
        if n in halves:
            g, d, nm, nv = _adamw_halves_call(wmv[0], *halves[n], c_arr, wmv[1], wmv[2], "adamw_" + n)
        else:
            g = red[n].reshape(two_d)
            d, nm, nv = _adamw_call(wmv[0], g, wmv[1], wmv[2], "adamw_" + n)
        grads.append(g.reshape(shape))
        deltas.append(d.reshape(shape))
        new_m.append(nm.reshape(shape))
        new_v.append(nv.reshape(shape))
    return (loss_tot, gx[None], *grads, *deltas, *new_m, *new_v)
```

```python
import math

import numpy as np
import jax
import jax.numpy as jnp
from jax import lax
from jax.experimental import pallas as pl
from jax.experimental.pallas import tpu as pltpu

F32 = jnp.float32
BF16 = jnp.bfloat16

D_MODEL = 1024
N_HEADS = 8
HEAD = 64
RET_W = N_HEADS * HEAD
MLA_W = N_HEADS * HEAD
ROPE = 32
Q_RANK = 256
KV_RANK = 128
D_FF = 2816
F2 = 2 * D_FF
IN_W = 4 * RET_W + Q_RANK + KV_RANK + ROPE
IN_EXT = 4 * RET_W + Q_RANK + KV_RANK + 128
KPE_LO = 64
ROPE_BASE = 10000.0
EPS = 1e-6
RET_CHUNK = 256
SM_SCALE = (HEAD + ROPE) ** -0.5
LOG2E = math.log2(math.e)
LN2 = math.log(2.0)
NEG = -1e30
LANES = 128
VMEM_LIMIT = 56 * 1024 * 1024

ADAM_LR = 0.001
ADAM_B1 = 0.9
ADAM_B2 = 0.999
ADAM_EPS = 1e-08
ADAM_WD = 0.01
ADAM_STEP = 10


VMEM_LIMIT_MLP = 60 * 1024 * 1024


def _cp(*sem, vmem=VMEM_LIMIT):
    return pltpu.CompilerParams(dimension_semantics=sem, vmem_limit_bytes=vmem)


def _full(shape):
    n = len(shape)
    return pl.BlockSpec(tuple(shape), lambda *_: (0,) * n)


def _row(ts, c):
    return pl.BlockSpec((ts, c), lambda i: (i, 0))


def _hrow(h, ts, c):
    return pl.BlockSpec((h, ts, c), lambda i: (0, i, 0))


def _dot(a, b):
    return jnp.dot(a, b, preferred_element_type=F32)


def _dot_nt(a, b):
    return lax.dot_general(a, b, (((1,), (1,)), ((), ())), preferred_element_type=F32)


def _dot_tn(a, b):
    return lax.dot_general(a, b, (((0,), (0,)), ((), ())), preferred_element_type=F32)


def _dot_hi(a, b):
    hi = a.astype(BF16)
    lo = (a - hi.astype(F32)).astype(BF16)
    bb = b.astype(BF16)
    return _dot(hi, bb) + _dot(lo, bb)


def _rot_half(x, half):
    w = x.shape[-1]
    lane = lax.broadcasted_iota(jnp.int32, x.shape, x.ndim - 1)
    first = (lane % (2 * half)) < half
    return jnp.where(first, -pltpu.roll(x, w - half, x.ndim - 1), pltpu.roll(x, half, x.ndim - 1))


def _rope(x, cos, sin, half):
    return x * cos + _rot_half(x, half) * sin


def _unrope(dy, cos, sin, half):
    return dy * cos - _rot_half(dy, half) * sin


def _sigmoid(g):
    return 0.5 * jnp.tanh(0.5 * g) + 0.5


def _silu(g):
    return g * _sigmoid(g)


def _rstd(x):
    return lax.rsqrt(jnp.mean(x * x, axis=-1, keepdims=True) + EPS)


def _rope_tables(positions):
    pos = positions.astype(F32)[:, None]
    s = pos.shape[0]
    inv = ROPE_BASE ** (-jnp.arange(0, HEAD, 2, dtype=F32) / HEAD)
    ang = pos * inv
    c, sn = jnp.cos(ang), jnp.sin(ang)
    cos_r = jnp.tile(jnp.concatenate([c, c], -1), (1, 2))
    sin_r = jnp.tile(jnp.concatenate([sn, sn], -1), (1, 2))
    inv = ROPE_BASE ** (-jnp.arange(0, ROPE, 2, dtype=F32) / ROPE)
    ang = pos * inv
    c, sn = jnp.cos(ang), jnp.sin(ang)
    one, zero = jnp.ones((s, KPE_LO), F32), jnp.zeros((s, KPE_LO), F32)
    cos_m = jnp.concatenate([one, c, c, one[:, :LANES - KPE_LO - ROPE]], -1)
    sin_m = jnp.concatenate([zero, sn, sn, zero[:, :LANES - KPE_LO - ROPE]], -1)
    return cos_r, sin_r, cos_m, sin_m


def _ret_consts():
    c = RET_CHUNK
    lg = np.log1p(-np.power(2.0, -5.0 - np.arange(N_HEADS, dtype=np.float64)))
    idx = np.arange(c, dtype=np.float64)
    diff = idx[:, None] - idx[None, :]
    lane_head = np.arange(LANES) // HEAD
    dmask = np.zeros((4, 2, c, c))
    zeta = np.zeros((4, c, LANES))
    xi = np.zeros((4, c, LANES))
    cd = np.zeros((4, LANES, LANES))
    bd = (lane_head[:, None] == lane_head[None, :]).astype(np.float64)
    for j in range(4):
        for hh in range(2):
            dmask[j, hh] = np.where(diff >= 0, np.exp(lg[2 * j + hh] * np.maximum(diff, 0.0)), 0.0)
        lgl = lg[2 * j + lane_head]
        zeta[j] = np.exp(lgl[None, :] * (c - 1.0 - idx[:, None]))
        xi[j] = np.exp(lgl[None, :] * (idx[:, None] + 1.0))
        cd[j] = np.exp(lgl * c)[:, None] * bd
    f = lambda a: jnp.asarray(a, F32)
    side = lambda d: np.concatenate([d[:, 0], d[:, 1]], axis=-1)
    return dict(dmask=f(side(dmask)), dmask_t=f(side(np.swapaxes(dmask, 2, 3))), zeta=f(zeta), xi=f(xi), cd=f(cd), bd=f(bd))


def _f1_call(x, anw, win, qnw, kvnw, wq, wk, wv, cos_r, sin_r, cos_m, sin_m, ts):
    s = x.shape[0]

    def body(x_ref, anw_ref, w_ref, qnw_ref, kvnw_ref, wq_ref, wk_ref, wv_ref, cr_ref, sr_ref, cm_ref, sm_ref,
             q_ref, k_ref, v_ref, g_ref, cq_ref, ckv_ref, mq_ref, mk_ref, mv_ref, r_ref):
        xv = x_ref[...]
        r = _rstd(xv)
        r_ref[...] = r
        h = (xv * r * anw_ref[...]).astype(BF16)
        cr, sr = cr_ref[...], sr_ref[...]
        qk = _dot(h, w_ref[:, 0:2 * RET_W])
        for j in range(4):
            sl = slice(j * LANES, (j + 1) * LANES)
            q_ref[:, sl] = _rope(qk[:, sl], cr, sr, HEAD // 2).astype(BF16)
            kk = qk[:, RET_W + j * LANES:RET_W + (j + 1) * LANES]
            k_ref[:, sl] = (_rope(kk, cr, sr, HEAD // 2) * (HEAD ** -0.5)).astype(BF16)
        v_ref[...] = _dot(h, w_ref[:, 2 * RET_W:3 * RET_W]).astype(BF16)
        g_ref[...] = _dot(h, w_ref[:, 3 * RET_W:4 * RET_W])
        o = 4 * RET_W
        cqv = _dot(h, w_ref[:, o:o + Q_RANK])
        ckvv = _dot(h, w_ref[:, o + Q_RANK:o + Q_RANK + KV_RANK])
        cq_ref[...] = cqv
        ckv_ref[...] = ckvv
        cm, sm = cm_ref[...], sm_ref[...]
        kp = _rope(_dot(h, w_ref[:, o + Q_RANK + KV_RANK:IN_EXT]), cm, sm, ROPE // 2)
        kp = _lane_pair((ts, LANES), QK_AUX, -1.0, -1.0, kp)
        cqn = (cqv * _rstd(cqv) * qnw_ref[...]).astype(BF16)
        ckvn = (ckvv * _rstd(ckvv) * kvnw_ref[...]).astype(BF16)
        for hd in range(N_HEADS):
            qh = _rope(_dot(cqn, wq_ref[hd]), cm, sm, ROPE // 2)
            mq_ref[hd] = (qh * (SM_SCALE * LOG2E)).astype(BF16)
            mk_ref[hd] = (_dot(ckvn, wk_ref[hd]) + kp).astype(BF16)
            mv_ref[hd] = _lane_pair((ts, LANES), V_AUX, 1.0, 1.0, _dot(ckvn, wv_ref[hd])).astype(BF16)

    sd = jax.ShapeDtypeStruct
    hm = sd((N_HEADS, s, LANES), BF16)
    return pl.pallas_call(
        body, name="f1_in_proj", grid=(s // ts,),
        in_specs=[_row(ts, D_MODEL), _full((1, D_MODEL)), _full((D_MODEL, IN_EXT)), _full((1, Q_RANK)), _full((1, KV_RANK)),
                  _full((N_HEADS, Q_RANK, LANES)), _full((N_HEADS, KV_RANK, LANES)), _full((N_HEADS, KV_RANK, LANES)),
                  _row(ts, LANES), _row(ts, LANES), _row(ts, LANES), _row(ts, LANES)],
        out_specs=[_row(ts, RET_W), _row(ts, RET_W), _row(ts, RET_W), _row(ts, RET_W),
                   _row(ts, Q_RANK), _row(ts, KV_RANK)] + [_hrow(N_HEADS, ts, LANES)] * 3 + [_row(ts, 1)],
        out_shape=[sd((s, RET_W), BF16), sd((s, RET_W), BF16), sd((s, RET_W), BF16), sd((s, RET_W), F32),
                   sd((s, Q_RANK), F32), sd((s, KV_RANK), F32), hm, hm, hm, sd((s, 1), F32)],
        compiler_params=_cp("parallel"),
    )(x, anw, win, qnw, kvnw, wq, wk, wv, cos_r, sin_r, cos_m, sin_m)


def _stack_heads(a):
    lo = lax.broadcasted_iota(jnp.int32, a.shape, 1) < HEAD
    zero = jnp.zeros_like(a)
    return jnp.concatenate([jnp.where(lo, a, zero), jnp.where(lo, zero, a)], axis=0)


def _pair_product(a, b2, decay2, w2):
    return _dot((_dot_nt(a, b2) * decay2).astype(BF16), w2)


RET_SLABS = 2


def _ret_specs(tr, tile_of):
    c, ns = RET_CHUNK, RET_SLABS
    return dict(
        slab=pl.BlockSpec((tr, ns * LANES), lambda j, i: (tile_of(i), j)),
        tab=pl.BlockSpec((tr, LANES), lambda j, i: (tile_of(i), 0)),
        vec=pl.BlockSpec((1, ns * LANES), lambda j, i: (0, j)),
        dmask=pl.BlockSpec((ns, c, 2 * c), lambda j, i: (j, 0, 0)),
        rows=pl.BlockSpec((ns, c, LANES), lambda j, i: (j, 0, 0)),
        state=pl.BlockSpec((ns, LANES, LANES), lambda j, i: (j, 0, 0)),
        bd=pl.BlockSpec((LANES, LANES), lambda j, i: (0, 0)))


def _ret_states(a_ref, b_ref, scale_ref, cd_ref, bd, st_ref, chunks, lanes, reverse):
    nc = len(chunks)
    contrib = [[_dot_tn((a_ref[rows, ln].astype(F32) * scale_ref[sl]).astype(BF16), b_ref[rows, ln]) * bd for rows in chunks]
               for sl, ln in enumerate(lanes)]
    states = []
    for sl in range(len(lanes)):
        st, seen = st_ref[sl], [None] * nc
        for ci in (reversed(range(nc)) if reverse else range(nc)):
            seen[ci] = st.astype(BF16)
            st = st * cd_ref[sl] + contrib[sl][ci]
        st_ref[sl] = st
        states.append(seen)
    return states


def _ret_fwd_call(q, k, v, g, gnw, rc, tr):
    s = q.shape[0]
    c = RET_CHUNK
    nc = tr // c
    ns = RET_SLABS

    def body(q_ref, k_ref, v_ref, g_ref, gnw_ref, dm_ref, zeta_ref, xi_ref, cd_ref, bd_ref, o_ref, y_ref, st_ref):
        @pl.when(pl.program_id(1) == 0)
        def _():
            st_ref[...] = jnp.zeros_like(st_ref)

        bd = bd_ref[...]
        chunks = [slice(ci * c, (ci + 1) * c) for ci in range(nc)]
        lanes = [slice(sl * LANES, (sl + 1) * LANES) for sl in range(ns)]
        states = _ret_states(k_ref, v_ref, zeta_ref, cd_ref, bd, st_ref, chunks, lanes, False)
        for ci, rows in enumerate(chunks):
            for sl, ln in enumerate(lanes):
                qc = q_ref[rows, ln]
                o_ref[rows, ln] = (_dot(qc, states[sl][ci]) * xi_ref[sl]
                                   + _pair_product(qc, _stack_heads(k_ref[rows, ln]), dm_ref[sl], _stack_heads(v_ref[rows, ln])))
        avg = bd * (1.0 / HEAD)
        for ln in lanes:
            o = o_ref[:, ln]
            ctr = o - _dot_hi(o, avg)
            var = _dot_hi(ctr * ctr, avg)
            y_ref[:, ln] = (_silu(g_ref[:, ln]) * (ctr * lax.rsqrt(var + EPS) * gnw_ref[:, ln])).astype(BF16)

    specs = _ret_specs(tr, lambda i: i)
    sd = jax.ShapeDtypeStruct
    return pl.pallas_call(
        body, name="ret_fwd", grid=(4 // ns, s // tr),
        in_specs=[specs["slab"]] * 4 + [specs["vec"], specs["dmask"], specs["rows"], specs["rows"], specs["state"], specs["bd"]],
        out_specs=[specs["slab"]] * 2,
        out_shape=[sd((s, RET_W), F32), sd((s, RET_W), BF16)],
        scratch_shapes=[pltpu.VMEM((ns, LANES, LANES), F32)],
        compiler_params=_cp("parallel", "arbitrary"),
    )(q, k, v, g, gnw, rc["dmask"], rc["zeta"], rc["xi"], rc["cd"], rc["bd"])


QK_AUX = HEAD + ROPE
V_AUX = HEAD


def _lane_pair(shape, lo, a, b, rest):
    lane = lax.broadcasted_iota(jnp.int32, shape, len(shape) - 1)
    return jnp.where(lane == lo, a, jnp.where(lane == lo + 1, b, rest))


def _hi_lo(v):
    hi = v.astype(BF16).astype(F32)
    return hi, v - hi


def _flash_fwd_call(q, k, v, tb):
    s = q.shape[1]
    nb = s // tb
    pairs = [(a, b) for a in range(nb) for b in range(a + 1)]
    qi_of, ki_of = (jnp.asarray(np.array(col, np.int32)) for col in zip(*pairs))

    def body(qi_ref, ki_ref, q_ref, k_ref, v_ref, o_ref, qb_ref, m_ref, acc_ref):
        qi, ki = qi_ref[pl.program_id(0)], ki_ref[pl.program_id(0)]

        @pl.when(ki == 0)
        def _():
            m_ref[...] = jnp.full_like(m_ref, NEG)
            acc_ref[...] = jnp.zeros_like(acc_ref)

        def step(masked):
            if masked:
                keep = lax.broadcasted_iota(jnp.int32, (tb, tb), 1) <= lax.broadcasted_iota(jnp.int32, (tb, tb), 0)
            def finish(h, pe, alpha):
                acc_ref[h] = acc_ref[h] * alpha + _dot(pe, v_ref[h])

            nxt, pending = _dot_nt(q_ref[0], k_ref[0]), None
            for h in range(N_HEADS):
                sc = nxt
                if h + 1 < N_HEADS:
                    nxt = _dot_nt(q_ref[h + 1], k_ref[h + 1])
                if masked:
                    sc = jnp.where(keep, sc, NEG)
                m_prev = m_ref[h]
                m_new = jnp.maximum(m_prev, jnp.max(sc, axis=1, keepdims=True))
                pe = jnp.exp2(sc - jnp.tile(m_new, (1, tb // LANES))).astype(BF16)
                m_ref[h] = m_new
                if pending is not None:
                    finish(*pending)
                pending = (h, pe, jnp.exp2(m_prev - m_new))
            finish(*pending)

        @pl.when(ki < qi)
        def _():
            step(False)

        @pl.when(ki == qi)
        def _():
            step(True)
            lane = lax.broadcasted_iota(jnp.int32, (tb, LANES), 1)
            for p in range(N_HEADS // 2):
                outs = []
                for h in (2 * p, 2 * p + 1):
                    acc = acc_ref[h]
                    l = acc[:, V_AUX:V_AUX + 1]
                    outs.append(acc * (1.0 / l))
                    hi, lo = _hi_lo(m_ref[h][:, 0:1] + jnp.log(l) * LOG2E)
                    qb_ref[h] = _lane_pair((tb, LANES), QK_AUX, hi, lo, q_ref[h].astype(F32)).astype(BF16)
                o_ref[:, p * LANES:(p + 1) * LANES] = jnp.where(lane < HEAD, outs[0], pltpu.roll(outs[1], HEAD, 1)).astype(BF16)

    sd = jax.ShapeDtypeStruct
    qspec = pl.BlockSpec((N_HEADS, tb, LANES), lambda p, qi_ref, ki_ref: (0, qi_ref[p], 0))
    kspec = pl.BlockSpec((N_HEADS, tb, LANES), lambda p, qi_ref, ki_ref: (0, ki_ref[p], 0))
    return pl.pallas_call(
        body, name="mla_flash_fwd",
        grid_spec=pltpu.PrefetchScalarGridSpec(
            num_scalar_prefetch=2, grid=(len(pairs),),
            in_specs=[qspec, kspec, kspec],
            out_specs=[pl.BlockSpec((tb, MLA_W), lambda p, qi_ref, ki_ref: (qi_ref[p], 0)), qspec],
            scratch_shapes=[pltpu.VMEM((N_HEADS, tb, LANES), F32), pltpu.VMEM((N_HEADS, tb, LANES), F32)]),
        out_shape=[sd((s, MLA_W), BF16), sd((N_HEADS, s, LANES), BF16)],
        compiler_params=_cp("arbitrary"),
    )(qi_of, ki_of, q, k, v)


def _out_proj_call(x, yret, ymla, wout, ts):
    s = x.shape[0]

    def body(x_ref, yr_ref, ym_ref, w_ref, x1_ref, r_ref):
        x1 = x_ref[...] + _dot(yr_ref[...], w_ref[0:RET_W, :]) + _dot(ym_ref[...], w_ref[RET_W:, :])
        x1_ref[...] = x1
        r_ref[...] = _rstd(x1)

    sd = jax.ShapeDtypeStruct
    return pl.pallas_call(
        body, name="out_proj", grid=(s // ts,),
        in_specs=[_row(ts, D_MODEL), _row(ts, RET_W), _row(ts, MLA_W), _full((D_MODEL, D_MODEL))],
        out_specs=[_row(ts, D_MODEL), _row(ts, 1)],
        out_shape=[sd((s, D_MODEL), F32), sd((s, 1), F32)],
        compiler_params=_cp("parallel"),
    )(x, yret, ymla, wout)


W_UP_SHARD = F2 // 4


def _ffn_fwd_call(x1, r2, fnw, wup4, cw, cb, wdown, tgt, fw, ts):
    s = x1.shape[0]
    wsh = W_UP_SHARD

    def body(x_ref, r_ref, fnw_ref, wup_ref, cw_ref, cb_ref, wd_ref, t_ref, fw_ref,
             u_ref, uc_ref, dx2_ref, loss_ref, gfw_ref, carry_ref):
        _zero_first(pl.program_id(0) == 0, carry_ref, loss_ref, gfw_ref)
        xv = x_ref[...]
        h = (xv * r_ref[...] * fnw_ref[...]).astype(BF16)
        conv = []
        for j in range(4):
            cols = slice(j * wsh, (j + 1) * wsh)
            ub = _dot(h, wup_ref[j]).astype(BF16)
            u_ref[:, cols] = ub
            u = ub.astype(F32)
            u1, u2 = _shifted(u, carry_ref[:, cols])
            w = cw_ref[:, cols]
            cb16 = (cb_ref[:, cols] + w[0:1, :] * u2 + w[1:2, :] * u1 + w[2:3, :] * u).astype(BF16)
            uc_ref[:, cols] = cb16
            conv.append(cb16.astype(F32))
            carry_ref[:, cols] = u[ts - 8:, :]
        acc = xv
        for j in range(2):
            a = (_silu(conv[j]) * conv[j + 2]).astype(BF16)
            acc = acc + _dot(a, wd_ref[j * wsh:(j + 1) * wsh, :])
        r = _rstd(acc)
        xh = acc * r
        fwv = fw_ref[...]
        e = xh * fwv - t_ref[...]
        loss_ref[...] += (0.5 / D_MODEL) * _colsum(jnp.sum(e * e, axis=1, keepdims=True))
        dy = e * (1.0 / D_MODEL)
        gfw_ref[...] += _colsum(dy * xh)
        dx2_ref[...] = _norm_bwd(dy, xh, r, fwv)

    sd = jax.ShapeDtypeStruct
    once = lambda shape: pl.BlockSpec(shape, lambda i: (0,) * len(shape), pipeline_mode=pl.Buffered(1))
    return pl.pallas_call(
        body, name="ffn_fwd_loss", grid=(s // ts,),
        in_specs=[_row(ts, D_MODEL), _row(ts, 1), once((1, D_MODEL)), once((4, D_MODEL, wsh)),
                  once((3, F2)), once((1, F2)), once((D_FF, D_MODEL)), _row(ts, D_MODEL), once((1, D_MODEL))],
        out_specs=[_row(ts, F2), _row(ts, F2), _row(ts, D_MODEL), _full((1, 1)), _full((1, D_MODEL))],
        out_shape=[sd((s, F2), BF16), sd((s, F2), BF16), sd((s, D_MODEL), F32), sd((1, 1), F32), sd((1, D_MODEL), F32)],
        scratch_shapes=[pltpu.VMEM((8, F2), F32)],
        compiler_params=_cp("arbitrary", vmem=VMEM_LIMIT_MLP),
    )(x1, r2, fnw, wup4, cw, cb, wdown, tgt, fw)


def _shifted(u, hal):
    row = lax.broadcasted_iota(jnp.int32, hal.shape, 0)
    r1, r2 = pltpu.roll(u, 1, 0), pltpu.roll(u, 2, 0)
    top1 = jnp.where(row == 0, hal[7:8, :], r1[0:8, :])
    top2 = jnp.where(row == 0, hal[6:7, :], jnp.where(row == 1, hal[7:8, :], r2[0:8, :]))
    return jnp.concatenate([top1, r1[8:, :]], axis=0), jnp.concatenate([top2, r2[8:, :]], axis=0)


def _prep_weights(w):
    win = w["w_in"]
    pad = lambda n: jnp.zeros((D_MODEL, n), win.dtype)
    win_ext = jnp.concatenate([win[:, :IN_W - ROPE], pad(KPE_LO), win[:, IN_W - ROPE:], pad(LANES - KPE_LO - ROPE)], -1)
    wuq = w["w_uq"].reshape(Q_RANK, N_HEADS, HEAD + ROPE)
    wq = jnp.concatenate([wuq, jnp.zeros((Q_RANK, N_HEADS, LANES - HEAD - ROPE), wuq.dtype)], -1).transpose(1, 0, 2)
    wukv = w["w_ukv"].reshape(KV_RANK, N_HEADS, 2 * HEAD)
    zk = jnp.zeros((KV_RANK, N_HEADS, HEAD), wukv.dtype)
    wk = jnp.concatenate([wukv[:, :, :HEAD], zk], -1).transpose(1, 0, 2)
    wv = jnp.concatenate([wukv[:, :, HEAD:], zk], -1).transpose(1, 0, 2)
    c = lambda a: a.astype(BF16)
    return dict(win=c(win_ext), wq=c(wq), wk=c(wk), wv=c(wv), wout=c(w["w_out"]))


def _prep_mlp_weights(w):
    wup = w["w_up"]
    if wup.ndim == 2:
        wup = wup.reshape(D_MODEL, 4, W_UP_SHARD).transpose(1, 0, 2)
    return dict(wup=wup.astype(BF16), wdown=w["w_down"].astype(BF16))


def _tiles(s):
    return dict(ts=min(s, 512), tr=min(s, 2048), tbf=min(s, 1024), tb=min(s, 512), t2=min(s, 256),
                tw=min(s, 2048), t1=min(s, 1024))


class _Exchanges:
    def __init__(self, w):
        self.w = w

    def mlp_weights(self, after):
        return self.w

    def mlp_grads(self, gw):
        pass

    def behind_out_bwd(self, after):
        pass

    def behind_attention(self, after):
        pass


def _forward(x, positions, tgt, w, small, ex):
    s = x.shape[0]
    t = _tiles(s)
    pw = _prep_weights(w)
    cos_r, sin_r, cos_m, sin_m = _rope_tables(positions)
    rc = _ret_consts()
    q, k, v, g, cq, ckv, mq, mk, mv, r1 = _f1_call(
        x, small["attn_norm_w"], pw["win"], small["mla_q_norm_w"], small["mla_kv_norm_w"], pw["wq"], pw["wk"], pw["wv"],
        cos_r, sin_r, cos_m, sin_m, t["ts"])
    o_ret, y_ret = _ret_fwd_call(q, k, v, g, small["ret_gn_w"], rc, t["tr"])
    y_mla, mqb = _flash_fwd_call(mq, mk, mv, t["tbf"])
    x1, r2 = _out_proj_call(x, y_ret, y_mla, pw["wout"], t["ts"])
    pw.update(_prep_mlp_weights(ex.mlp_weights(r2)))
    u, uc, dx2, loss, g_fw = _ffn_fwd_call(x1, r2, small["ffn_norm_w"], pw["wup"], w["conv_w"], small["conv_b"], pw["wdown"],
                                           tgt, small["final_norm_w"], t["ts"])
    return dict(pw=pw, tabs=(cos_r, sin_r, cos_m, sin_m), rc=rc, q=q, k=k, v=v, g=g, cq=cq, ckv=ckv, r1=r1,
                o_ret=o_ret, y_ret=y_ret, mqb=mqb, mk=mk, mv=mv, y_mla=y_mla, x1=x1, r2=r2, u=u, uc=uc,
                dx2=dx2, loss=loss, g_fw=g_fw)


def _norm_bwd(dh, xh, r, nw):
    dxn = dh * nw
    return r * (dxn - xh * jnp.mean(dxn * xh, axis=-1, keepdims=True))


def _ordered_after(body, order):
    if order is None:
        return body, [], []
    return (lambda order_ref, *refs: body(*refs)), [pl.BlockSpec(memory_space=pl.ANY)], [order]


def _zero_first(first, *refs):
    @pl.when(first)
    def _():
        for ref in refs:
            ref[...] = jnp.zeros_like(ref)


def _colsum(v):
    return jnp.sum(v, axis=0, keepdims=True)


def _dsilu(g, sg):
    return sg * (1.0 + g * (1.0 - sg))


def _ffn_bwd_call(dx2, u, uc, cw, wdown, wup4, x1, r2, fnw, ts):
    s = dx2.shape[0]
    nt = s // ts
    wsh = W_UP_SHARD
    rev = lambda i: nt - 1 - i

    def body(dx2_ref, u_ref, uc_ref, cw_ref, wd_ref, wup_ref, x_ref, r_ref, fnw_ref,
             du_ref, dx1_ref, dcw_ref, dcb_ref, dfnw_ref, dwd_hbm, carry_ref, dwd_ref, sem):
        i = pl.program_id(0)
        _zero_first(i == 0, carry_ref, dwd_ref, dcw_ref, dcb_ref, dfnw_ref)
        dxb = dx2_ref[...].astype(BF16)
        dh = jnp.zeros((ts, D_MODEL), F32)
        for j in range(2):
            gcols = slice(j * wsh, (j + 1) * wsh)
            vcols = slice(D_FF + j * wsh, D_FF + (j + 1) * wsh)
            gate, val = uc_ref[:, gcols].astype(F32), uc_ref[:, vcols].astype(F32)
            da = _dot_nt(dxb, wd_ref[gcols, :])
            sg = _sigmoid(gate)
            sl = gate * sg
            dwd_ref[gcols, :] += _dot_tn((sl * val).astype(BF16), dxb)
            for d, cols, shard in ((da * val * _dsilu(gate, sg), gcols, j), (da * sl, vcols, 2 + j)):
                d1, d2 = _shifted_up(d, carry_ref[:, cols])
                uv = u_ref[:, cols].astype(F32)
                for t, dt in enumerate((d2, d1, d)):
                    dcw_ref[t:t + 1, cols] += _colsum(dt * uv)
                dcb_ref[:, cols] += _colsum(d)
                w = cw_ref[:, cols]
                du = (w[2:3, :] * d + w[1:2, :] * d1 + w[0:1, :] * d2).astype(BF16)
                du_ref[:, cols] = du
                dh = dh + _dot_nt(du, wup_ref[shard])
                carry_ref[:, cols] = d[0:8, :]
        r = r_ref[...]
        xh = x_ref[...] * r
        dfnw_ref[...] += _colsum(dh * xh)
        dx1_ref[...] = dx2_ref[...] + _norm_bwd(dh, xh, r, fnw_ref[...])

        @pl.when(i == nt - 1)
        def _():
            cp = pltpu.make_async_copy(dwd_ref, dwd_hbm, sem)
            cp.start()
            cp.wait()

    sd = jax.ShapeDtypeStruct
    row = lambda c: pl.BlockSpec((ts, c), lambda i: (rev(i), 0))
    once = lambda shape: pl.BlockSpec(shape, lambda i: (0,) * len(shape), pipeline_mode=pl.Buffered(1))
    return pl.pallas_call(
        body, name="ffn_bwd", grid=(nt,),
        in_specs=[row(D_MODEL), row(F2), row(F2), once((3, F2)), once((D_FF, D_MODEL)), once((4, D_MODEL, wsh)),
                  row(D_MODEL), row(1), once((1, D_MODEL))],
        out_specs=[row(F2), row(D_MODEL), _full((3, F2)), _full((1, F2)), _full((1, D_MODEL)), pl.BlockSpec(memory_space=pl.ANY)],
        out_shape=[sd((s, F2), BF16), sd((s, D_MODEL), F32), sd((3, F2), F32), sd((1, F2), F32), sd((1, D_MODEL), F32),
                   sd((D_FF, D_MODEL), F32)],
        scratch_shapes=[pltpu.VMEM((8, F2), F32), pltpu.VMEM((D_FF, D_MODEL), F32), pltpu.SemaphoreType.DMA],
        compiler_params=_cp("arbitrary", vmem=VMEM_LIMIT_MLP),
    )(dx2, u, uc, cw, wdown, wup4, x1, r2, fnw)


def _shifted_up(d, hal):
    n = d.shape[0]
    row = lax.broadcasted_iota(jnp.int32, hal.shape, 0)
    r1, r2 = pltpu.roll(d, n - 1, 0), pltpu.roll(d, n - 2, 0)
    end1 = jnp.where(row == 7, hal[0:1, :], r1[n - 8:, :])
    end2 = jnp.where(row == 6, hal[0:1, :], jnp.where(row == 7, hal[1:2, :], r2[n - 8:, :]))
    return jnp.concatenate([r1[:n - 8, :], end1], axis=0), jnp.concatenate([r2[:n - 8, :], end2], axis=0)


def _dw_norm_call(x, r, nw, b, ts, tn, name):
    s, n = b.shape
    k = x.shape[1]

    def body(x_ref, r_ref, nw_ref, b_ref, dw_ref):
        _zero_first(pl.program_id(1) == 0, dw_ref)
        h = (x_ref[...] * r_ref[...] * nw_ref[...]).astype(BF16)
        dw_ref[...] += _dot_tn(h, b_ref[...])

    return pl.pallas_call(
        body, name=name, grid=(n // tn, s // ts),
        in_specs=[pl.BlockSpec((ts, k), lambda j, i: (i, 0)), pl.BlockSpec((ts, 1), lambda j, i: (i, 0)),
                  pl.BlockSpec((1, k), lambda j, i: (0, 0)), pl.BlockSpec((ts, tn), lambda j, i: (i, j))],
        out_specs=pl.BlockSpec((None, k, tn), lambda j, i: (j, 0, 0)),
        out_shape=jax.ShapeDtypeStruct((n // tn, k, tn), F32),
        compiler_params=_cp("parallel", "arbitrary"),
    )(x, r, nw, b)


def _out_bwd_call(dx1, yret, ymla, wout, ts, order=None):
    s = dx1.shape[0]

    def body(dx_ref, yr_ref, ym_ref, w_ref, dyr_ref, do_ref, dwo_ref):
        _zero_first(pl.program_id(0) == 0, dwo_ref)
        dxb = dx_ref[...].astype(BF16)
        dmix = _dot_nt(dxb, w_ref[...])
        dyr_ref[...] = dmix[:, :RET_W]
        ym = ym_ref[...]
        lane = lax.broadcasted_iota(jnp.int32, (ts, LANES), 1)
        for p in range(N_HEADS // 2):
            dom = dmix[:, RET_W + p * LANES:RET_W + (p + 1) * LANES]
            prod = dom * ym[:, p * LANES:(p + 1) * LANES].astype(F32)
            for hh in range(2):
                mine = (lane >= HEAD) if hh else (lane < HEAD)
                hi, lo = _hi_lo(jnp.sum(jnp.where(mine, prod, 0.0), axis=1, keepdims=True))
                base = jnp.where(lane < HEAD, pltpu.roll(dom, HEAD, 1) if hh else dom, 0.0)
                do_ref[2 * p + hh] = _lane_pair((ts, LANES), V_AUX, -hi, -lo, base).astype(BF16)
        dwo_ref[0:RET_W, :] += _dot_tn(yr_ref[...], dxb)
        dwo_ref[RET_W:, :] += _dot_tn(ym, dxb)

    sd = jax.ShapeDtypeStruct
    body, first_specs, first = _ordered_after(body, order)
    return pl.pallas_call(
        body, name="out_proj_bwd", grid=(s // ts,),
        in_specs=first_specs + [_row(ts, D_MODEL), _row(ts, RET_W), _row(ts, MLA_W), _full((D_MODEL, D_MODEL))],
        out_specs=[_row(ts, RET_W), _hrow(N_HEADS, ts, LANES), _full((D_MODEL, D_MODEL))],
        out_shape=[sd((s, RET_W), F32), sd((N_HEADS, s, LANES), BF16), sd((D_MODEL, D_MODEL), F32)],
        compiler_params=_cp("arbitrary"),
    )(*first, dx1, yret, ymla, wout)


def _ret_bwd_q_call(q, k, v, o, g, dy, gnw, rc, cos_r, sin_r, tr):
    s = q.shape[0]
    c = RET_CHUNK
    nc = tr // c
    ns = RET_SLABS

    def body(q_ref, k_ref, v_ref, o_ref, g_ref, dy_ref, gnw_ref, dm_ref, zeta_ref, xi_ref, cd_ref, bd_ref, cr_ref, sr_ref,
             dq_ref, dg_ref, do_ref, dgnw_ref, st_ref):
        _zero_first(pl.program_id(1) == 0, st_ref, dgnw_ref)
        bd = bd_ref[...]
        avg = bd * (1.0 / HEAD)
        chunks = [slice(ci * c, (ci + 1) * c) for ci in range(nc)]
        lanes = [slice(sl * LANES, (sl + 1) * LANES) for sl in range(ns)]
        dov = []
        for ln in lanes:
            ov = o_ref[:, ln]
            ctr = ov - _dot_hi(ov, avg)
            rs = lax.rsqrt(_dot_hi(ctr * ctr, avg) + EPS)
            oh = ctr * rs
            gg, dyv, gnw_v = g_ref[:, ln], dy_ref[:, ln], gnw_ref[:, ln]
            sg = _sigmoid(gg)
            sl = gg * sg
            dg_ref[:, ln] = (dyv * oh * gnw_v * _dsilu(gg, sg)).astype(BF16)
            dgnw_ref[:, ln] += _colsum(dyv * sl * oh)
            doh = dyv * sl * gnw_v
            dov.append((rs * (doh - _dot_hi(doh, avg) - oh * _dot_hi(doh * oh, avg))).astype(BF16))
            do_ref[:, ln] = dov[-1]
        states = _ret_states(k_ref, v_ref, zeta_ref, cd_ref, bd, st_ref, chunks, lanes, False)
        for ci, rows in enumerate(chunks):
            for sl, ln in enumerate(lanes):
                doc = dov[sl][rows, :]
                dq = (_dot_nt(doc, states[sl][ci]) * xi_ref[sl]
                      + _pair_product(doc, _stack_heads(v_ref[rows, ln]), dm_ref[sl], _stack_heads(k_ref[rows, ln])))
                dq_ref[rows, ln] = _unrope(dq, cr_ref[rows, :], sr_ref[rows, :], HEAD // 2).astype(BF16)

    specs = _ret_specs(tr, lambda i: i)
    sd = jax.ShapeDtypeStruct
    return pl.pallas_call(
        body, name="ret_bwd_q", grid=(4 // ns, s // tr),
        in_specs=[specs["slab"]] * 6 + [specs["vec"], specs["dmask"], specs["rows"], specs["rows"], specs["state"], specs["bd"],
                                        specs["tab"], specs["tab"]],
        out_specs=[specs["slab"]] * 3 + [specs["vec"]],
        out_shape=[sd((s, RET_W), BF16), sd((s, RET_W), BF16), sd((s, RET_W), BF16), sd((1, RET_W), F32)],
        scratch_shapes=[pltpu.VMEM((ns, LANES, LANES), F32)],
        compiler_params=_cp("parallel", "arbitrary"),
    )(q, k, v, o, g, dy, gnw, rc["dmask"], rc["zeta"], rc["xi"], rc["cd"], rc["bd"], cos_r, sin_r)


def _ret_bwd_kv_call(q, k, v, do, rc, cos_r, sin_r, tr):
    s = q.shape[0]
    c = RET_CHUNK
    nc = tr // c
    nt = s // tr
    ns = RET_SLABS

    def body(q_ref, k_ref, v_ref, do_ref, dm_ref, zeta_ref, xi_ref, cd_ref, bd_ref, cr_ref, sr_ref, dk_ref, dv_ref, gs_ref):
        _zero_first(pl.program_id(1) == 0, gs_ref)
        bd = bd_ref[...]
        chunks = [slice(ci * c, (ci + 1) * c) for ci in range(nc)]
        lanes = [slice(sl * LANES, (sl + 1) * LANES) for sl in range(ns)]
        states = _ret_states(q_ref, do_ref, xi_ref, cd_ref, bd, gs_ref, chunks, lanes, True)
        for ci, rows in enumerate(chunks):
            for sl, ln in enumerate(lanes):
                kc, vc = k_ref[rows, ln], v_ref[rows, ln]
                q2, do2 = _stack_heads(q_ref[rows, ln]), _stack_heads(do_ref[rows, ln])
                gb = states[sl][ci]
                dk = _dot_nt(vc, gb) * zeta_ref[sl] + _pair_product(vc, do2, dm_ref[sl], q2)
                dv = _dot(kc, gb) * zeta_ref[sl] + _pair_product(kc, q2, dm_ref[sl], do2)
                dk_ref[rows, ln] = (_unrope(dk, cr_ref[rows, :], sr_ref[rows, :], HEAD // 2) * (HEAD ** -0.5)).astype(BF16)
                dv_ref[rows, ln] = dv.astype(BF16)

    specs = _ret_specs(tr, lambda i: nt - 1 - i)
    sd = jax.ShapeDtypeStruct
    return pl.pallas_call(
        body, name="ret_bwd_kv", grid=(4 // ns, nt),
        in_specs=[specs["slab"]] * 4 + [specs["dmask"], specs["rows"], specs["rows"], specs["state"], specs["bd"],
                                        specs["tab"], specs["tab"]],
        out_specs=[specs["slab"]] * 2,
        out_shape=[sd((s, RET_W), BF16), sd((s, RET_W), BF16)],
        scratch_shapes=[pltpu.VMEM((ns, LANES, LANES), F32)],
        compiler_params=_cp("parallel", "arbitrary"),
    )(q, k, v, do, rc["dmask_t"], rc["zeta"], rc["xi"], rc["cd"], rc["bd"], cos_r, sin_r)


FLASH_BWD_HEADS = 8


def _flash_bwd_call(qb, k, v, do, tb, order=None):
    s = qb.shape[1]
    nb = s // tb
    hg = FLASH_BWD_HEADS
    pairs = [(a, b) for a in range(nb) for b in range(a, nb)]
    ki_of, qi_of = (jnp.asarray(np.array(col, np.int32)) for col in zip(*pairs))
    extra = [] if order is None else [order]

    def body(ki_ref, qi_ref, *refs):
        q_ref, k_ref, v_ref, do_ref, dk_ref, dv_ref, dq_hbm, dka_ref, dva_ref, dq_ref, sem = refs[len(extra):]
        g, p = pl.program_id(0), pl.program_id(1)
        ki, qi = ki_ref[p], qi_ref[p]
        _zero_first(p == 0, dq_ref)
        _zero_first(qi == ki, dka_ref, dva_ref)
        rows = pl.ds(pl.multiple_of(qi * tb, tb), tb)

        def step(masked):
            if masked:
                keep = lax.broadcasted_iota(jnp.int32, (tb, tb), 0) <= lax.broadcasted_iota(jnp.int32, (tb, tb), 1)
            for h in range(hg):
                st = _dot_nt(k_ref[h], q_ref[h])
                if masked:
                    st = jnp.where(keep, st, NEG)
                pt = jnp.exp2(st)
                dob = do_ref[h]
                dva_ref[h] += _dot(pt.astype(BF16), dob)
                dst = (pt * _dot_nt(v_ref[h], dob)).astype(BF16)
                dka_ref[h] += _dot(dst, q_ref[h])
                dq_ref[h, rows, :] += _dot_tn(dst, k_ref[h])

        @pl.when(qi > ki)
        def _():
            step(False)

        @pl.when(qi == ki)
        def _():
            step(True)

        @pl.when(qi == nb - 1)
        def _():
            dk_ref[...] = (dka_ref[...] * LN2).astype(BF16)
            dv_ref[...] = dva_ref[...].astype(BF16)

        @pl.when(p == len(pairs) - 1)
        def _():
            cp = pltpu.make_async_copy(dq_ref, dq_hbm.at[pl.ds(g * hg, hg)], sem)
            cp.start()
            cp.wait()

    kspec = pl.BlockSpec((hg, tb, LANES), lambda g, p, ki_ref, qi_ref: (g, ki_ref[p], 0))
    qspec = pl.BlockSpec((hg, tb, LANES), lambda g, p, ki_ref, qi_ref: (g, qi_ref[p], 0))
    hm = jax.ShapeDtypeStruct((N_HEADS, s, LANES), BF16)
    return pl.pallas_call(
        body, name="mla_flash_bwd",
        grid_spec=pltpu.PrefetchScalarGridSpec(
            num_scalar_prefetch=2, grid=(N_HEADS // hg, len(pairs)),
            in_specs=[ANY] * len(extra) + [qspec, kspec, kspec, qspec],
            out_specs=[kspec, kspec, ANY],
            scratch_shapes=[pltpu.VMEM((hg, tb, LANES), F32), pltpu.VMEM((hg, tb, LANES), F32),
                            pltpu.VMEM((hg, s, LANES), F32), pltpu.SemaphoreType.DMA]),
        out_shape=[hm, hm, jax.ShapeDtypeStruct((N_HEADS, s, LANES), F32)],
        compiler_params=_cp("arbitrary", "arbitrary"),
    )(ki_of, qi_of, *extra, qb, k, v, do)


def _mla_post_call(dq, dk, dv, cq, ckv, qnw, kvnw, wq, wk, wv, cos_m, sin_m, ts):
    s = cq.shape[0]

    def body(dq_ref, dk_ref, dv_ref, cq_ref, ckv_ref, qnw_ref, kvnw_ref, wq_ref, wk_ref, wv_ref, cm_ref, sm_ref,
             dcq_ref, dckv_ref, dkpe_ref, dwq_ref, dwk_ref, dwv_ref, dqnw_ref, dkvnw_ref):
        _zero_first(pl.program_id(0) == 0, dwq_ref, dwk_ref, dwv_ref, dqnw_ref, dkvnw_ref)
        cqv, ckvv = cq_ref[...], ckv_ref[...]
        rq, rkv = _rstd(cqv), _rstd(ckvv)
        qh_, kvh_ = cqv * rq, ckvv * rkv
        qnw_v, kvnw_v = qnw_ref[...], kvnw_ref[...]
        cqn = (qh_ * qnw_v).astype(BF16)
        ckvn = (kvh_ * kvnw_v).astype(BF16)
        cm, sm = cm_ref[...], sm_ref[...]
        dcqn = jnp.zeros((ts, Q_RANK), F32)
        dckvn = jnp.zeros((ts, KV_RANK), F32)
        dkpe = jnp.zeros((ts, LANES), F32)
        for h in range(N_HEADS):
            dqu = _unrope(dq_ref[h] * SM_SCALE, cm, sm, ROPE // 2).astype(BF16)
            dwq_ref[h] += _dot_tn(cqn, dqu)
            dcqn = dcqn + _dot_nt(dqu, wq_ref[h])
            dkb, dvb = dk_ref[h], dv_ref[h]
            dkpe = dkpe + dkb.astype(F32)
            dwk_ref[h] += _dot_tn(ckvn, dkb)
            dwv_ref[h] += _dot_tn(ckvn, dvb)
            dckvn = dckvn + _dot_nt(dkb, wk_ref[h]) + _dot_nt(dvb, wv_ref[h])
        lane = lax.broadcasted_iota(jnp.int32, (ts, LANES), 1)
        dkpe = jnp.where((lane >= KPE_LO) & (lane < KPE_LO + ROPE), dkpe, 0.0)
        dkpe_ref[...] = _unrope(dkpe, cm, sm, ROPE // 2).astype(BF16)
        dqnw_ref[...] += _colsum(dcqn * qh_)
        dkvnw_ref[...] += _colsum(dckvn * kvh_)
        dcq_ref[...] = _norm_bwd(dcqn, qh_, rq, qnw_v).astype(BF16)
        dckv_ref[...] = _norm_bwd(dckvn, kvh_, rkv, kvnw_v).astype(BF16)

    sd = jax.ShapeDtypeStruct
    hm = _hrow(N_HEADS, ts, LANES)
    return pl.pallas_call(
        body, name="mla_post", grid=(s // ts,),
        in_specs=[hm, hm, hm, _row(ts, Q_RANK), _row(ts, KV_RANK), _full((1, Q_RANK)), _full((1, KV_RANK)),
                  _full((N_HEADS, Q_RANK, LANES)), _full((N_HEADS, KV_RANK, LANES)), _full((N_HEADS, KV_RANK, LANES)),
                  _row(ts, LANES), _row(ts, LANES)],
        out_specs=[_row(ts, Q_RANK), _row(ts, KV_RANK), _row(ts, LANES),
                   _full((N_HEADS, Q_RANK, LANES)), _full((N_HEADS, KV_RANK, LANES)), _full((N_HEADS, KV_RANK, LANES)),
                   _full((1, Q_RANK)), _full((1, KV_RANK))],
        out_shape=[sd((s, Q_RANK), BF16), sd((s, KV_RANK), BF16), sd((s, LANES), BF16),
                   sd((N_HEADS, Q_RANK, LANES), F32), sd((N_HEADS, KV_RANK, LANES), F32), sd((N_HEADS, KV_RANK, LANES), F32),
                   sd((1, Q_RANK), F32), sd((1, KV_RANK), F32)],
        compiler_params=_cp("arbitrary"),
    )(dq, dk, dv, cq, ckv, qnw, kvnw, wq, wk, wv, cos_m, sin_m)


def _in_bwd_call(parts, x, r1, anw, dx1, win, ts):
    s = x.shape[0]
    widths = [p.shape[1] for p in parts]
    np_ = len(parts)

    def body(*refs):
        p_refs = refs[:np_]
        x_ref, r_ref, anw_ref, dx1_ref, w_ref, dx_ref, dw_ref, danw_ref = refs[np_:]
        _zero_first(pl.program_id(0) == 0, dw_ref, danw_ref)
        dproj = jnp.concatenate([p[...] for p in p_refs], axis=-1)
        r, anw_v = r_ref[...], anw_ref[...]
        xh = x_ref[...] * r
        dw_ref[...] += _dot_tn((xh * anw_v).astype(BF16), dproj)
        dh = _dot_nt(dproj, w_ref[...])
        danw_ref[...] += _colsum(dh * xh)
        dx_ref[...] = dx1_ref[...] + _norm_bwd(dh, xh, r, anw_v)

    sd = jax.ShapeDtypeStruct
    return pl.pallas_call(
        body, name="in_proj_bwd", grid=(s // ts,),
        in_specs=[_row(ts, w) for w in widths]
        + [_row(ts, D_MODEL), _row(ts, 1), _full((1, D_MODEL)), _row(ts, D_MODEL), _full((D_MODEL, IN_EXT))],
        out_specs=[_row(ts, D_MODEL), _full((D_MODEL, IN_EXT)), _full((1, D_MODEL))],
        out_shape=[sd((s, D_MODEL), F32), sd((D_MODEL, IN_EXT), F32), sd((1, D_MODEL), F32)],
        compiler_params=_cp("arbitrary"),
    )(*parts, x, r1, anw, dx1, win)


def _local_step(x, positions, tgt, w, small, ex=None):
    s = x.shape[0]
    t = _tiles(s)
    ex = _Exchanges(w) if ex is None else ex
    f = _forward(x, positions, tgt, w, small, ex)
    pw, rc = f["pw"], f["rc"]
    cos_r, sin_r, cos_m, sin_m = f["tabs"]
    dx2, loss, g_fw = f["dx2"], f["loss"], f["g_fw"]
    du, dx1, g_cw, g_cb, g_fnw, g_wd = _ffn_bwd_call(dx2, f["u"], f["uc"], w["conv_w"], pw["wdown"], pw["wup"],
                                                     f["x1"], f["r2"], small["ffn_norm_w"], t["t2"])
    g_wup = _dw_norm_call(f["x1"], f["r2"], small["ffn_norm_w"], du, t["tw"], F2 // 4, "dw_up")
    started = ex.mlp_grads(dict(w_up=g_wup, w_down=g_wd))
    dy_ret, do, g_wout = _out_bwd_call(dx1, f["y_ret"], f["y_mla"], pw["wout"], t["t1"], started)
    started = ex.behind_out_bwd(g_wout)
    drq, dg, do_ret, g_gnw = _ret_bwd_q_call(f["q"], f["k"], f["v"], f["o_ret"], f["g"], dy_ret, small["ret_gn_w"], rc, cos_r, sin_r, t["tr"])
    drk, drv = _ret_bwd_kv_call(f["q"], f["k"], f["v"], do_ret, rc, cos_r, sin_r, t["tr"])
    dmk, dmv, dmq = _flash_bwd_call(f["mqb"], f["mk"], f["mv"], do, t["tb"], started)
    ex.behind_attention(dmk)
    dcq, dckv, dkpe, g_wq, g_wk, g_wv, g_qnw, g_kvnw = _mla_post_call(
        dmq, dmk, dmv, f["cq"], f["ckv"], small["mla_q_norm_w"], small["mla_kv_norm_w"], pw["wq"], pw["wk"], pw["wv"], cos_m, sin_m, t["ts"])
    gx, g_win_ext, g_anw = _in_bwd_call([drq, drk, drv, dg, dcq, dckv, dkpe], x, f["r1"], small["attn_norm_w"], dx1, pw["win"], t["ts"])
    lo = IN_W - ROPE
    g_win = jnp.concatenate([g_win_ext[:, :lo], g_win_ext[:, lo + KPE_LO:lo + KPE_LO + ROPE]], -1)
    g_wuq = g_wq.transpose(1, 0, 2)[:, :, :HEAD + ROPE].reshape(Q_RANK, N_HEADS * (HEAD + ROPE))
    g_wukv = jnp.concatenate([g_wk[:, :, :HEAD], g_wv[:, :, :HEAD]], -1).transpose(1, 0, 2).reshape(KV_RANK, 2 * MLA_W)
    gw = dict(w_in=g_win, w_uq=g_wuq, w_ukv=g_wukv, w_out=g_wout, w_up=g_wup,
              conv_w=g_cw, w_down=g_wd)
    gs = dict(attn_norm_w=g_anw, ret_gn_w=g_gnw, mla_q_norm_w=g_qnw, mla_kv_norm_w=g_kvnw, ffn_norm_w=g_fnw,
              conv_b=g_cb, final_norm_w=g_fw)
    return loss, gx, gw, gs


MESH_ID = pl.DeviceIdType.MESH
ANY = pl.BlockSpec(memory_space=pl.ANY)
VMEM_SPEC = pl.BlockSpec(memory_space=pltpu.VMEM)
N_DEV = 8
GROUP_A = (("w_in", (D_MODEL, IN_W // 4), 1), ("w_uq", (Q_RANK, 192), 1), ("w_ukv", (KV_RANK, 256), 1),
           ("w_out", (D_MODEL // 4, D_MODEL), 0))
GROUP_B = (("w_up", (D_MODEL, F2 // 4), 1), ("w_down", (D_FF // 4, D_MODEL), 0))
HBM_SPEC = pl.BlockSpec(memory_space=pltpu.HBM)
SEM_SPEC = pl.BlockSpec(memory_space=pltpu.SEMAPHORE)


def _mesh_pos():
    return lax.axis_index("x"), lax.axis_index("y"), lax.axis_index("c")


def _other_chips(x, y):
    return [(1 - x, y), (x, 1 - y), (1 - x, 1 - y)]


def _remote(src, dst, send_sems, recv_sems, k, dev):
    return pltpu.make_async_remote_copy(src_ref=src, dst_ref=dst, send_sem=send_sems.at[k], recv_sem=recv_sems.at[k],
                                        device_id=dev, device_id_type=MESH_ID)


def _gather_list_call(parts, tag):
    n = len(parts)

    def body(*refs):
        srcs, outs, (send_sems, recv_sems) = refs[:n], refs[n:2 * n], refs[2 * n:]
        x, y, c = _mesh_pos()
        sm = 2 * x + y
        chips = _other_chips(x, y)
        sib = (x, y, 1 - c)
        rc = lambda k, src, dst, dev: _remote(src, dst, send_sems, recv_sems, k, dev)
        first = [rc(7 * i + j, srcs[i].at[c], outs[i].at[sm, c], (cx, cy, c)) for i in range(n) for j, (cx, cy) in enumerate(chips)]
        own = [rc(7 * i + 6, srcs[i], outs[i].at[sm], sib) for i in range(n)]
        for cp in first + own:
            cp.start()
        passed = []
        for j, (cx, cy) in enumerate(chips):
            for i in range(n):
                land = outs[i].at[2 * cx + cy, c]
                rc(7 * i + j, srcs[i].at[c], land, (cx, cy, c)).wait_recv()
                cp = rc(7 * i + 3 + j, land, land, sib)
                cp.start()
                passed.append(cp)
        for j, (cx, cy) in enumerate(chips):
            for i in range(n):
                rc(7 * i + 3 + j, srcs[i].at[c], outs[i].at[2 * cx + cy, 1 - c], sib).wait_recv()
        for cp in own:
            cp.wait_recv()
        for cp in first + passed + own:
            cp.wait_send()

    return pl.pallas_call(
        body, name="weights_all_gather_" + tag,
        in_specs=[ANY] * n, out_specs=[ANY] * n,
        out_shape=[jax.ShapeDtypeStruct((4,) + p.shape, p.dtype) for p in parts],
        scratch_shapes=[pltpu.SemaphoreType.DMA((7 * n,)), pltpu.SemaphoreType.DMA((7 * n,))],
    )(*parts)


def _direct_gather_copies(srcs, lands, send_sems, recv_sems):
    x, y, c = _mesh_pos()
    sm = 2 * x + y
    sends, recvs = [], []
    for i, (src, land) in enumerate(zip(srcs, lands)):
        for j, (cx, cy) in enumerate(_other_chips(x, y)):
            for t in range(2):
                sends.append(_remote(src.at[c], land.at[sm, c], send_sems, recv_sems, 13 * i + 4 * j + 2 * c + t, (cx, cy, t)))
                recvs.append(_remote(src.at[t], land.at[2 * cx + cy, t], send_sems, recv_sems, 13 * i + 4 * j + 2 * t + c, (cx, cy, t)))
        sends.append(_remote(src, land.at[sm], send_sems, recv_sems, 13 * i + 12, (x, y, 1 - c)))
        recvs.append(_remote(src, land.at[sm], send_sems, recv_sems, 13 * i + 12, (x, y, 1 - c)))
    return sends, recvs


def _sibling_copies(srcs, lands, send_sems, recv_sems):
    x, y, c = _mesh_pos()
    cps = [_remote(src.at[s, 1 - c], land.at[s], send_sems, recv_sems, 4 * i + s, (x, y, 1 - c))
           for i, (src, land) in enumerate(zip(srcs, lands)) for s in range(4)]
    return cps, cps


def _chips_copies(srcs, lands, send_sems, recv_sems):
    x, y, c = _mesh_pos()
    cps = [_remote(src.at[2 * cx + cy], land.at[j], send_sems, recv_sems, 3 * i + j, (cx, cy, c))
           for i, (src, land) in enumerate(zip(srcs, lands)) for j, (cx, cy) in enumerate(_other_chips(x, y))]
    return cps, cps


def _share_copies(srcs, lands, send_sems, recv_sems):
    x, y, c = _mesh_pos()
    cps = [_remote(src, land, send_sems, recv_sems, i, (x, y, 1 - c)) for i, (src, land) in enumerate(zip(srcs, lands))]
    return cps, cps


def _exchange_call(name, copies, srcs, land_shapes, n_sems):
    n = len(srcs)

    def body(*refs):
        sends, recvs = copies(refs[:n], refs[n:2 * n], refs[2 * n], refs[2 * n + 1])
        for cp in sends:
            cp.start()
        for cp in sends:
            cp.wait_send()
        for cp in recvs:
            cp.wait_recv()

    return pl.pallas_call(
        body, name=name, in_specs=[ANY] * n, out_specs=[ANY] * n, out_shape=list(land_shapes),
        scratch_shapes=[pltpu.SemaphoreType.DMA((n_sems,)), pltpu.SemaphoreType.DMA((n_sems,))],
    )(*srcs)


def _exchange_start_call(name, copies, srcs, land_shapes, n_sems, order=None):
    n = len(srcs)
    extra = [] if order is None else [order]
    k = 2 * n + len(extra)

    def body(*refs):
        sends, _ = copies(refs[:n], refs[n:2 * n], refs[k], refs[k + 1])
        for cp in sends:
            cp.start()
        refs[-1][...] = jnp.zeros_like(refs[-1])

    hbm = lambda a: pltpu.with_memory_space_constraint(a, pltpu.HBM)
    lands = [hbm(lax.empty(sd.shape, sd.dtype)) for sd in land_shapes]
    sem = pltpu.SemaphoreType.DMA((n_sems,))
    out = pl.pallas_call(
        body, name=name,
        out_shape=(sem, sem, *[pltpu.HBM(a.shape, a.dtype) for a in list(srcs) + lands], jax.ShapeDtypeStruct((8, LANES), F32)),
        in_specs=[HBM_SPEC] * (2 * n) + [ANY] * len(extra), out_specs=(SEM_SPEC, SEM_SPEC, *[HBM_SPEC] * (2 * n), VMEM_SPEC),
        input_output_aliases={i: 2 + i for i in range(2 * n)},
        compiler_params=pltpu.CompilerParams(has_side_effects=pltpu.SideEffectType.DATAFLOW_SIDE_EFFECTING),
    )(*[hbm(a) for a in srcs], *lands, *extra)
    return out[0], out[1], out[2:2 + n], out[2 + n:2 + 2 * n], out[-1]


def _exchange_wait_call(name, copies, started, after):
    send_sems, recv_sems, srcs, lands, _ = started
    n = len(srcs)

    def body(*refs):
        sends, recvs = copies(refs[:n], refs[n:2 * n], refs[2 * n], refs[2 * n + 1])
        for cp in sends:
            cp.wait_send()
        for cp in recvs:
            cp.wait_recv()

    out = pl.pallas_call(
        body, name=name,
        out_shape=tuple(pltpu.HBM(a.shape, a.dtype) for a in list(srcs) + list(lands)),
        in_specs=[HBM_SPEC] * (2 * n) + [SEM_SPEC, SEM_SPEC, ANY], out_specs=tuple([HBM_SPEC] * (2 * n)),
        input_output_aliases={i: i for i in range(2 * n)},
        compiler_params=pltpu.CompilerParams(has_side_effects=pltpu.SideEffectType.DATAFLOW_SIDE_EFFECTING),
    )(*srcs, *lands, send_sems, recv_sems, after)
    return out[:n], out[n:]


def _rows_tile(rows, width, itemsize=4):
    limit = max(16, (3 << 20) // (width * itemsize))
    if rows <= limit:
        return rows
    return max(t for t in range(16, limit + 1, 16) if rows % t == 0)


def _sum_sibling_call(g, buf, c, name):
    _, _, rh, w = g.shape
    tile = _rows_tile(rh, w)

    def body(c_ref, g_ref, b_ref, p_ref, pb_ref):
        p = g_ref[...] + b_ref[...]
        p_ref[...] = p
        pb_ref[...] = p.astype(BF16)

    blk = pl.BlockSpec((None, tile, w), lambda s, i, c_ref: (s, i, 0))
    return pl.pallas_call(
        body, name=name,
        grid_spec=pltpu.PrefetchScalarGridSpec(
            num_scalar_prefetch=1, grid=(4, rh // tile),
            in_specs=[pl.BlockSpec((None, None, tile, w), lambda s, i, c_ref: (s, c_ref[0], i, 0)), blk],
            out_specs=[blk, blk]),
        out_shape=[jax.ShapeDtypeStruct((4, rh, w), F32), jax.ShapeDtypeStruct((4, rh, w), BF16)],
        compiler_params=_cp("parallel", "parallel"),
    )(c, g, buf)


def _sum_chips_call(p, buf, sm, name):
    _, rh, w = p.shape
    tile = _rows_tile(rh, w)

    def body(sm_ref, p_ref, b_ref, f_ref):
        f_ref[...] = ((p_ref[...] + b_ref[0].astype(F32)) + b_ref[1].astype(F32)) + b_ref[2].astype(F32)

    return pl.pallas_call(
        body, name=name,
        grid_spec=pltpu.PrefetchScalarGridSpec(
            num_scalar_prefetch=1, grid=(rh // tile,),
            in_specs=[pl.BlockSpec((None, tile, w), lambda i, sm_ref: (sm_ref[0], i, 0)),
                      pl.BlockSpec((3, tile, w), lambda i, sm_ref: (0, i, 0))],
            out_specs=pl.BlockSpec((tile, w), lambda i, sm_ref: (i, 0))),
        out_shape=jax.ShapeDtypeStruct((rh, w), F32),
        compiler_params=_cp("parallel"),
    )(sm, p, buf)


def _adamw_halves_call(w, g_mine, g_sib, c, m, v, name):
    r, wd = w.shape
    rh = r // 2
    tile = _rows_tile(rh, wd)
    nt = rh // tile

    def body(c_ref, w_ref, gm_ref, gs_ref, m_ref, v_ref, g_ref, d_ref, nm_ref, nv_ref):
        gv = jnp.where(pl.program_id(0) == c_ref[0], gm_ref[...], gs_ref[...])
        g_ref[...] = gv
        nm = ADAM_B1 * m_ref[...] + (1.0 - ADAM_B1) * gv
        nv = ADAM_B2 * v_ref[...] + (1.0 - ADAM_B2) * jnp.square(gv)
        m_hat = nm / (1.0 - ADAM_B1 ** ADAM_STEP)
        v_hat = nv / (1.0 - ADAM_B2 ** ADAM_STEP)
        d_ref[...] = -ADAM_LR * (m_hat / (jnp.sqrt(v_hat) + ADAM_EPS) + ADAM_WD * w_ref[...])
        nm_ref[...] = nm
        nv_ref[...] = nv

    whole = pl.BlockSpec((tile, wd), lambda h, i, c_ref: (h * nt + i, 0))
    half = pl.BlockSpec((tile, wd), lambda h, i, c_ref: (i, 0))
    sd = jax.ShapeDtypeStruct((r, wd), F32)
    return pl.pallas_call(
        body, name=name,
        grid_spec=pltpu.PrefetchScalarGridSpec(
            num_scalar_prefetch=1, grid=(2, nt),
            in_specs=[whole, half, half, whole, whole], out_specs=[whole] * 4),
        out_shape=[sd, sd, sd, sd],
        compiler_params=_cp("parallel", "parallel"),
    )(c, w, g_mine, g_sib, m, v)


def _all_reduce8_call(vec, name):
    rows = vec.shape[0]

    def body(v_ref, out_ref, slots, send_sems, recv_sems):
        x, y, c = _mesh_pos()
        me = 4 * x + 2 * y + c
        slots[me] = v_ref[...]

        def rcopy(k, to_me):
            bx, by, bc = (k >> 2) & 1, (k >> 1) & 1, k & 1
            px, py, pc = (1 - x if bx else x), (1 - y if by else y), (1 - c if bc else c)
            slot = 4 * px + 2 * py + pc if to_me else me
            return pltpu.make_async_remote_copy(src_ref=v_ref, dst_ref=slots.at[slot], send_sem=send_sems.at[k - 1],
                                                recv_sem=recv_sems.at[k - 1], device_id=(px, py, pc), device_id_type=MESH_ID)

        for k in range(1, N_DEV):
            rcopy(k, False).start()
        for k in range(1, N_DEV):
            rcopy(k, True).wait_recv()
        for k in range(1, N_DEV):
            rcopy(k, False).wait_send()
        tot = slots[0]
        for d in range(1, N_DEV):
            tot = tot + slots[d]
        out_ref[...] = tot

    return pl.pallas_call(
        body, name=name,
        in_specs=[VMEM_SPEC], out_specs=VMEM_SPEC,
        out_shape=jax.ShapeDtypeStruct((rows, LANES), F32),
        scratch_shapes=[pltpu.VMEM((N_DEV, rows, LANES), F32),
                        pltpu.SemaphoreType.DMA((N_DEV - 1,)), pltpu.SemaphoreType.DMA((N_DEV - 1,))],
    )(vec)


def _adamw_call(w, g, m, v, name):
    r, c = w.shape
    rb = r if r <= 256 else (256 if r % 256 == 0 else 352)
    assert r % rb == 0

    def body(w_ref, g_ref, m_ref, v_ref, d_ref, nm_ref, nv_ref):
        gv = g_ref[...]
        nm = ADAM_B1 * m_ref[...] + (1.0 - ADAM_B1) * gv
        nv = ADAM_B2 * v_ref[...] + (1.0 - ADAM_B2) * jnp.square(gv)
        m_hat = nm / (1.0 - ADAM_B1 ** ADAM_STEP)
        v_hat = nv / (1.0 - ADAM_B2 ** ADAM_STEP)
        d_ref[...] = -ADAM_LR * (m_hat / (jnp.sqrt(v_hat) + ADAM_EPS) + ADAM_WD * w_ref[...])
        nm_ref[...] = nm
        nv_ref[...] = nv

    spec = pl.BlockSpec((rb, c), lambda i: (i, 0))
    sd = jax.ShapeDtypeStruct((r, c), F32)
    return pl.pallas_call(
        body, name=name, grid=(r // rb,),
        in_specs=[spec] * 4, out_specs=[spec] * 3, out_shape=[sd, sd, sd],
        compiler_params=_cp("parallel"),
    )(w, g, m, v)


SMALL = (("attn_norm_w", D_MODEL), ("ret_gn_w", RET_W), ("mla_q_norm_w", Q_RANK), ("mla_kv_norm_w", KV_RANK),
         ("ffn_norm_w", D_MODEL), ("conv_b", F2), ("final_norm_w", D_MODEL))
WEIGHT_ORDER = ("attn_norm_w", "w_in", "ret_gn_w", "mla_q_norm_w", "w_uq", "mla_kv_norm_w", "w_ukv", "w_out",
                "ffn_norm_w", "w_up", "conv_w", "conv_b", "w_down", "final_norm_w")


def _pad_rows(flat, rows):
    return jnp.concatenate([flat, jnp.zeros((rows * LANES - flat.shape[0],), flat.dtype)]).reshape(rows, LANES)


def kernel(x, positions, attn_norm_w, w_in, ret_gn_w, mla_q_norm_w, w_uq, mla_kv_norm_w, w_ukv, w_out, ffn_norm_w, w_up, conv_w, conv_b, w_down, final_norm_w, loss_target, m_attn_norm_w, m_w_in, m_ret_gn_w, m_mla_q_norm_w, m_w_uq, m_mla_kv_norm_w, m_w_ukv, m_w_out, m_ffn_norm_w, m_w_up, m_conv_w, m_conv_b, m_w_down, m_final_norm_w, v_attn_norm_w, v_w_in, v_ret_gn_w, v_mla_q_norm_w, v_w_uq, v_mla_kv_norm_w, v_w_ukv, v_w_out, v_ffn_norm_w, v_w_up, v_conv_w, v_conv_b, v_w_down, v_final_norm_w):
    args = dict(locals())
    cx, cy, cc = _mesh_pos()
    sm = 2 * cx + cy

    c_arr, sm_arr = cc.reshape(1).astype(jnp.int32), sm.reshape(1).astype(jnp.int32)
    sds = jax.ShapeDtypeStruct

    def my_shards(group):
        return [args[n][0].astype(BF16).reshape(2, r // 2, c) for n, (r, c), _ in group]

    def full_weights(gathered, group):
        full = {}
        for (n, (r, c), axis), got in zip(group, gathered):
            piece = got.reshape(4, r, c)
            full[n] = piece if n == "w_up" else (piece.transpose(1, 0, 2).reshape(r, 4 * c) if axis == 1 else piece.reshape(4 * r, c))
        return full

    def by_owner(gw, group):
        out = []
        for n, (r, c), axis in group:
            g = gw[n]
            if axis == 1 and g.ndim == 2:
                g = g.reshape(r, 4, c).transpose(1, 0, 2)
            out.append(g.reshape(4, 2, r // 2, c))
        return out

    def sibling_shapes(gs):
        return [sds((4,) + g.shape[2:], F32) for g in gs]

    def chip_sums(gs, bufs, group):
        res = [_sum_sibling_call(g, b, c_arr, "grads_sum_sibling_" + n) for g, b, (n, _, _) in zip(gs, bufs, group)]
        return [p for p, _ in res], [pb for _, pb in res]

    def chips_shapes(pbs):
        return [sds((3,) + pb.shape[1:], BF16) for pb in pbs]

    def totals(ps, lands, group, tag):
        fins = [_sum_chips_call(p, l, sm_arr, "grads_sum_chips_" + n) for p, l, (n, _, _) in zip(ps, lands, group)]
        sibs = _exchange_call("grads_rs_share_" + tag, _share_copies, fins, [sds(f.shape, F32) for f in fins], len(fins))
        return {n: (f, s) for (n, _, _), f, s in zip(group, fins, sibs)}

    class StepExchanges(_Exchanges):
        def __init__(self, order):
            shards = my_shards(GROUP_B)
            self.gather = _exchange_start_call("weights_gather_start_b", _direct_gather_copies, shards,
                                               [sds((4,) + s.shape, BF16) for s in shards], 13 * len(shards), order)
            self.red = None

        def token(self):
            return self.gather[4][0:1, 0:1]

        def mlp_weights(self, after):
            return full_weights(_exchange_wait_call("weights_gather_wait_b", _direct_gather_copies, self.gather, after)[1], GROUP_B)

        def mlp_grads(self, gw):
            gs = by_owner(gw, GROUP_B)
            self.step1 = _exchange_start_call("grads_rs_sibling_start_b", _sibling_copies, gs, sibling_shapes(gs), 4 * len(gs))
            return self.step1[4]

        def behind_out_bwd(self, after):
            gs, bufs = _exchange_wait_call("grads_rs_sibling_wait_b", _sibling_copies, self.step1, after)
            self.ps, pbs = chip_sums(gs, bufs, GROUP_B)
            self.step2 = _exchange_start_call("grads_rs_chips_start_b", _chips_copies, pbs, chips_shapes(pbs), 3 * len(pbs))
            return self.step2[4]

        def behind_attention(self, after):
            _, lands = _exchange_wait_call("grads_rs_chips_wait_b", _chips_copies, self.step2, after)
            self.red = totals(self.ps, lands, GROUP_B, "b")

    gathered = _gather_list_call(my_shards(GROUP_A) + [conv_w[0].reshape(2, 1, 3 * F2 // 8)], "a")
    full = full_weights(gathered[:-1], GROUP_A)
    ex = StepExchanges(gathered[-1])
    full["conv_w"] = gathered[-1].reshape(4, 3, F2 // 4).transpose(1, 0, 2).reshape(3, F2)
    small = {n: args[n].reshape(1, d) for n, d in SMALL}
    small["attn_norm_w"] = small["attn_norm_w"] + ex.token()

    loss, gx, gw, gs = _local_step(x[0], positions[0], loss_target[0], full, small, ex)

    ga = by_owner(gw, GROUP_A)
    bufs = _exchange_call("grads_rs_sibling_a", _sibling_copies, ga, sibling_shapes(ga), 4 * len(ga))
    ps, pbs = chip_sums(ga, bufs, GROUP_A)
    lands = _exchange_call("grads_rs_chips_a", _chips_copies, pbs, chips_shapes(pbs), 3 * len(pbs))
    halves = {**ex.red, **totals(ps, lands, GROUP_A, "a")}

    vec = jnp.concatenate([gs[n].reshape(-1) for n, _ in SMALL] + [gw["conv_w"].reshape(-1), loss.reshape(-1)])
    tot = _all_reduce8_call(_pad_rows(vec, 216), "small_all_reduce").reshape(-1)
    red, off = {}, 0
    for n, d in SMALL:
        red[n] = tot[off:off + d].reshape(1, d)
        off += d
    red["conv_w"] = lax.dynamic_slice(tot[off:off + 3 * F2].reshape(3, F2), (0, sm * (F2 // 4)), (3, F2 // 4))
    loss_tot = tot[off + 3 * F2]

    grads, deltas, new_m, new_v = [], [], [], []
    for n in WEIGHT_ORDER:
        shape = args[n].shape
        two_d = (1, shape[0]) if len(shape) == 1 else shape[-2:]
        wmv = [args[k + n].reshape(two_d) for k in ("", "m_", "v_")]
        if n in halves:
            g, d, nm, nv = _adamw_halves_call(wmv[0], *halves[n], c_arr, wmv[1], wmv[2], "adamw_" + n)
        else:
            g = red[n].reshape(two_d)
            d, nm, nv = _adamw_call(wmv[0], g, wmv[1], wmv[2], "adamw_" + n)
        grads.append(g.reshape(shape))
        deltas.append(d.reshape(shape))
        new_m.append(nm.reshape(shape))
        new_v.append(nv.reshape(shape))
    return (loss_tot, gx[None], *grads, *deltas, *new_m, *new_v)
```

```python
import math

import numpy as np
import jax
import jax.numpy as jnp
from jax import lax
from jax.experimental import pallas as pl
from jax.experimental.pallas import tpu as pltpu

F32 = jnp.float32
BF16 = jnp.bfloat16

D_MODEL = 1024
N_HEADS = 8
HEAD = 64
RET_W = N_HEADS * HEAD
MLA_W = N_HEADS * HEAD
ROPE = 32
Q_RANK = 256
KV_RANK = 128
D_FF = 2816
F2 = 2 * D_FF
IN_W = 4 * RET_W + Q_RANK + KV_RANK + ROPE
IN_EXT = 4 * RET_W + Q_RANK + KV_RANK + 128
KPE_LO = 64
ROPE_BASE = 10000.0
EPS = 1e-6
RET_CHUNK = 256
SM_SCALE = (HEAD + ROPE) ** -0.5
LOG2E = math.log2(math.e)
LN2 = math.log(2.0)
NEG = -1e30
LANES = 128
VMEM_LIMIT = 56 * 1024 * 1024

ADAM_LR = 0.001
ADAM_B1 = 0.9
ADAM_B2 = 0.999
ADAM_EPS = 1e-08
ADAM_WD = 0.01
ADAM_STEP = 10


VMEM_LIMIT_MLP = 60 * 1024 * 1024


def _cp(*sem, vmem=VMEM_LIMIT):
    return pltpu.CompilerParams(dimension_semantics=sem, vmem_limit_bytes=vmem)


def _full(shape):
    n = len(shape)
    return pl.BlockSpec(tuple(shape), lambda *_: (0,) * n)


def _row(ts, c):
    return pl.BlockSpec((ts, c), lambda i: (i, 0))


def _hrow(h, ts, c):
    return pl.BlockSpec((h, ts, c), lambda i: (0, i, 0))


def _dot(a, b):
    return jnp.dot(a, b, preferred_element_type=F32)


def _dot_nt(a, b):
    return lax.dot_general(a, b, (((1,), (1,)), ((), ())), preferred_element_type=F32)


def _dot_tn(a, b):
    return lax.dot_general(a, b, (((0,), (0,)), ((), ())), preferred_element_type=F32)


def _dot_hi(a, b):
    hi = a.astype(BF16)
    lo = (a - hi.astype(F32)).astype(BF16)
    bb = b.astype(BF16)
    return _dot(hi, bb) + _dot(lo, bb)


def _rot_half(x, half):
    w = x.shape[-1]
    lane = lax.broadcasted_iota(jnp.int32, x.shape, x.ndim - 1)
    first = (lane % (2 * half)) < half
    return jnp.where(first, -pltpu.roll(x, w - half, x.ndim - 1), pltpu.roll(x, half, x.ndim - 1))


def _rope(x, cos, sin, half):
    return x * cos + _rot_half(x, half) * sin


def _unrope(dy, cos, sin, half):
    return dy * cos - _rot_half(dy, half) * sin


def _sigmoid(g):
    return 0.5 * jnp.tanh(0.5 * g) + 0.5


def _silu(g):
    return g * _sigmoid(g)


def _rstd(x):
    return lax.rsqrt(jnp.mean(x * x, axis=-1, keepdims=True) + EPS)


def _rope_tables(positions):
    s = positions.shape[0]
    hr, hm = HEAD // 2, ROPE // 2
    pos = positions.astype(F32)[None, :]
    inv_r = ROPE_BASE ** (-jnp.arange(0, HEAD, 2, dtype=F32) / HEAD)
    inv_m = ROPE_BASE ** (-jnp.arange(0, ROPE, 2, dtype=F32) / ROPE)
    ang = jnp.concatenate([inv_r, inv_m])[:, None] * pos
    packed = jnp.concatenate([jnp.cos(ang), jnp.sin(ang), jnp.zeros((LANES - 2 * (hr + hm), s), F32)], 0)
    tx = min(s, 1024)

    def spread(t, lane, pieces, fill):
        out = jnp.full(t.shape, fill, F32)
        for lo, src, width in pieces:
            moved = t if lo == src else pltpu.roll(t, (lo - src) % LANES, 1)
            out = jnp.where((lane >= lo) & (lane < lo + width), moved, out)
        return out

    def body(p_ref, cr_ref, sr_ref, cm_ref, sm_ref):
        t = p_ref[...].T
        lane = lax.broadcasted_iota(jnp.int32, t.shape, 1)
        cr_ref[...] = spread(t, lane, [(j * hr, 0, hr) for j in range(LANES // hr)], 0.0)
        sr_ref[...] = spread(t, lane, [(j * hr, hr + hm, hr) for j in range(LANES // hr)], 0.0)
        cm_ref[...] = spread(t, lane, [(KPE_LO, hr, hm), (KPE_LO + hm, hr, hm)], 1.0)
        sm_ref[...] = spread(t, lane, [(KPE_LO, 2 * hr + hm, hm), (KPE_LO + hm, 2 * hr + hm, hm)], 0.0)

    tab = jax.ShapeDtypeStruct((s, LANES), F32)
    return pl.pallas_call(
        body, name="rope_tables", grid=(s // tx,),
        in_specs=[pl.BlockSpec((LANES, tx), lambda i: (0, i))],
        out_specs=[_row(tx, LANES)] * 4, out_shape=[tab] * 4,
        compiler_params=_cp("parallel"),
    )(packed)


def _ret_consts():
    c = RET_CHUNK
    lg = np.log1p(-np.power(2.0, -5.0 - np.arange(N_HEADS, dtype=np.float64)))
    idx = np.arange(c, dtype=np.float64)
    diff = idx[:, None] - idx[None, :]
    lane_head = np.arange(LANES) // HEAD
    dmask = np.zeros((4, 2, c, c))
    zeta = np.zeros((4, c, LANES))
    xi = np.zeros((4, c, LANES))
    cd = np.zeros((4, LANES, LANES))
    bd = (lane_head[:, None] == lane_head[None, :]).astype(np.float64)
    for j in range(4):
        for hh in range(2):
            dmask[j, hh] = np.where(diff >= 0, np.exp(lg[2 * j + hh] * np.maximum(diff, 0.0)), 0.0)
        lgl = lg[2 * j + lane_head]
        zeta[j] = np.exp(lgl[None, :] * (c - 1.0 - idx[:, None]))
        xi[j] = np.exp(lgl[None, :] * (idx[:, None] + 1.0))
        cd[j] = np.exp(lgl * c)[:, None] * bd
    f = lambda a: jnp.asarray(a, F32)
    side = lambda d: np.concatenate([d[:, 0], d[:, 1]], axis=-1)
    return dict(dmask=f(side(dmask)), dmask_t=f(side(np.swapaxes(dmask, 2, 3))), zeta=f(zeta), xi=f(xi), cd=f(cd), bd=f(bd))


def _f1_call(x, anw, win, qnw, kvnw, wq, wk, wv, cos_r, sin_r, cos_m, sin_m, ts):
    s = x.shape[0]

    def body(x_ref, anw_ref, w_ref, qnw_ref, kvnw_ref, wq_ref, wk_ref, wv_ref, cr_ref, sr_ref, cm_ref, sm_ref,
             q_ref, k_ref, v_ref, g_ref, cq_ref, ckv_ref, mq_ref, mk_ref, mv_ref, r_ref):
        xv = x_ref[...]
        r = _rstd(xv)
        r_ref[...] = r
        h = (xv * r * anw_ref[...]).astype(BF16)
        cr, sr = cr_ref[...], sr_ref[...]
        qk = _dot(h, w_ref[:, 0:2 * RET_W])
        for j in range(4):
            sl = slice(j * LANES, (j + 1) * LANES)
            q_ref[:, sl] = _rope(qk[:, sl], cr, sr, HEAD // 2).astype(BF16)
            kk = qk[:, RET_W + j * LANES:RET_W + (j + 1) * LANES]
            k_ref[:, sl] = (_rope(kk, cr, sr, HEAD // 2) * (HEAD ** -0.5)).astype(BF16)
        v_ref[...] = _dot(h, w_ref[:, 2 * RET_W:3 * RET_W]).astype(BF16)
        g_ref[...] = _dot(h, w_ref[:, 3 * RET_W:4 * RET_W])
        o = 4 * RET_W
        cqv = _dot(h, w_ref[:, o:o + Q_RANK])
        ckvv = _dot(h, w_ref[:, o + Q_RANK:o + Q_RANK + KV_RANK])
        cq_ref[...] = cqv
        ckv_ref[...] = ckvv
        cm, sm = cm_ref[...], sm_ref[...]
        kp = _rope(_dot(h, w_ref[:, o + Q_RANK + KV_RANK:IN_EXT]), cm, sm, ROPE // 2)
        kp = _lane_pair((ts, LANES), QK_AUX, -1.0, -1.0, kp)
        cqn = (cqv * _rstd(cqv) * qnw_ref[...]).astype(BF16)
        ckvn = (ckvv * _rstd(ckvv) * kvnw_ref[...]).astype(BF16)
        for hd in range(N_HEADS):
            qh = _rope(_dot(cqn, wq_ref[hd]), cm, sm, ROPE // 2)
            mq_ref[hd] = (qh * (SM_SCALE * LOG2E)).astype(BF16)
            mk_ref[hd] = (_dot(ckvn, wk_ref[hd]) + kp).astype(BF16)
            mv_ref[hd] = _lane_pair((ts, LANES), V_AUX, 1.0, 1.0, _dot(ckvn, wv_ref[hd])).astype(BF16)

    sd = jax.ShapeDtypeStruct
    hm = sd((N_HEADS, s, LANES), BF16)
    return pl.pallas_call(
        body, name="f1_in_proj", grid=(s // ts,),
        in_specs=[_row(ts, D_MODEL), _full((1, D_MODEL)), _full((D_MODEL, IN_EXT)), _full((1, Q_RANK)), _full((1, KV_RANK)),
                  _full((N_HEADS, Q_RANK, LANES)), _full((N_HEADS, KV_RANK, LANES)), _full((N_HEADS, KV_RANK, LANES)),
                  _row(ts, LANES), _row(ts, LANES), _row(ts, LANES), _row(ts, LANES)],
        out_specs=[_row(ts, RET_W), _row(ts, RET_W), _row(ts, RET_W), _row(ts, RET_W),
                   _row(ts, Q_RANK), _row(ts, KV_RANK)] + [_hrow(N_HEADS, ts, LANES)] * 3 + [_row(ts, 1)],
        out_shape=[sd((s, RET_W), BF16), sd((s, RET_W), BF16), sd((s, RET_W), BF16), sd((s, RET_W), F32),
                   sd((s, Q_RANK), F32), sd((s, KV_RANK), F32), hm, hm, hm, sd((s, 1), F32)],
        compiler_params=_cp("parallel"),
    )(x, anw, win, qnw, kvnw, wq, wk, wv, cos_r, sin_r, cos_m, sin_m)


def _stack_heads(a):
    lo = lax.broadcasted_iota(jnp.int32, a.shape, 1) < HEAD
    zero = jnp.zeros_like(a)
    return jnp.concatenate([jnp.where(lo, a, zero), jnp.where(lo, zero, a)], axis=0)


def _pair_product(a, b2, decay2, w2):
    return _dot((_dot_nt(a, b2) * decay2).astype(BF16), w2)


RET_SLABS = 2


def _ret_specs(tr, tile_of):
    c, ns = RET_CHUNK, RET_SLABS
    return dict(
        slab=pl.BlockSpec((tr, ns * LANES), lambda j, i: (tile_of(i), j)),
        tab=pl.BlockSpec((tr, LANES), lambda j, i: (tile_of(i), 0)),
        vec=pl.BlockSpec((1, ns * LANES), lambda j, i: (0, j)),
        dmask=pl.BlockSpec((ns, c, 2 * c), lambda j, i: (j, 0, 0)),
        rows=pl.BlockSpec((ns, c, LANES), lambda j, i: (j, 0, 0)),
        state=pl.BlockSpec((ns, LANES, LANES), lambda j, i: (j, 0, 0)),
        bd=pl.BlockSpec((LANES, LANES), lambda j, i: (0, 0)))


def _ret_states(a_ref, b_ref, scale_ref, cd_ref, bd, st_ref, chunks, lanes, reverse):
    nc = len(chunks)
    contrib = [[_dot_tn((a_ref[rows, ln].astype(F32) * scale_ref[sl]).astype(BF16), b_ref[rows, ln]) * bd for rows in chunks]
               for sl, ln in enumerate(lanes)]
    states = []
    for sl in range(len(lanes)):
        st, seen = st_ref[sl], [None] * nc
        for ci in (reversed(range(nc)) if reverse else range(nc)):
            seen[ci] = st.astype(BF16)
            st = st * cd_ref[sl] + contrib[sl][ci]
        st_ref[sl] = st
        states.append(seen)
    return states


def _ret_fwd_call(q, k, v, g, gnw, rc, tr):
    s = q.shape[0]
    c = RET_CHUNK
    nc = tr // c
    ns = RET_SLABS

    def body(q_ref, k_ref, v_ref, g_ref, gnw_ref, dm_ref, zeta_ref, xi_ref, cd_ref, bd_ref, o_ref, y_ref, st_ref):
        @pl.when(pl.program_id(1) == 0)
        def _():
            st_ref[...] = jnp.zeros_like(st_ref)

        bd = bd_ref[...]
        chunks = [slice(ci * c, (ci + 1) * c) for ci in range(nc)]
        lanes = [slice(sl * LANES, (sl + 1) * LANES) for sl in range(ns)]
        states = _ret_states(k_ref, v_ref, zeta_ref, cd_ref, bd, st_ref, chunks, lanes, False)
        for ci, rows in enumerate(chunks):
            for sl, ln in enumerate(lanes):
                qc = q_ref[rows, ln]
                o_ref[rows, ln] = (_dot(qc, states[sl][ci]) * xi_ref[sl]
                                   + _pair_product(qc, _stack_heads(k_ref[rows, ln]), dm_ref[sl], _stack_heads(v_ref[rows, ln])))
        avg = bd * (1.0 / HEAD)
        for ln in lanes:
            o = o_ref[:, ln]
            ctr = o - _dot_hi(o, avg)
            var = _dot_hi(ctr * ctr, avg)
            y_ref[:, ln] = (_silu(g_ref[:, ln]) * (ctr * lax.rsqrt(var + EPS) * gnw_ref[:, ln])).astype(BF16)

    specs = _ret_specs(tr, lambda i: i)
    sd = jax.ShapeDtypeStruct
    return pl.pallas_call(
        body, name="ret_fwd", grid=(4 // ns, s // tr),
        in_specs=[specs["slab"]] * 4 + [specs["vec"], specs["dmask"], specs["rows"], specs["rows"], specs["state"], specs["bd"]],
        out_specs=[specs["slab"]] * 2,
        out_shape=[sd((s, RET_W), F32), sd((s, RET_W), BF16)],
        scratch_shapes=[pltpu.VMEM((ns, LANES, LANES), F32)],
        compiler_params=_cp("parallel", "arbitrary"),
    )(q, k, v, g, gnw, rc["dmask"], rc["zeta"], rc["xi"], rc["cd"], rc["bd"])


QK_AUX = HEAD + ROPE
V_AUX = HEAD


def _lane_pair(shape, lo, a, b, rest):
    lane = lax.broadcasted_iota(jnp.int32, shape, len(shape) - 1)
    return jnp.where(lane == lo, a, jnp.where(lane == lo + 1, b, rest))


def _hi_lo(v):
    hi = v.astype(BF16).astype(F32)
    return hi, v - hi


def _flash_fwd_call(q, k, v, tb):
    s = q.shape[1]
    nb = s // tb
    pairs = [(a, b) for a in range(nb) for b in range(a + 1)]
    qi_of, ki_of = (jnp.asarray(np.array(col, np.int32)) for col in zip(*pairs))

    def body(qi_ref, ki_ref, q_ref, k_ref, v_ref, o_ref, qb_ref, m_ref, acc_ref):
        qi, ki = qi_ref[pl.program_id(0)], ki_ref[pl.program_id(0)]

        @pl.when(ki == 0)
        def _():
            m_ref[...] = jnp.full_like(m_ref, NEG)
            acc_ref[...] = jnp.zeros_like(acc_ref)

        def step(masked):
            if masked:
                keep = lax.broadcasted_iota(jnp.int32, (tb, tb), 1) <= lax.broadcasted_iota(jnp.int32, (tb, tb), 0)
            def finish(h, pe, alpha):
                acc_ref[h] = acc_ref[h] * alpha + _dot(pe, v_ref[h])

            nxt, pending = _dot_nt(q_ref[0], k_ref[0]), None
            for h in range(N_HEADS):
                sc = nxt
                if h + 1 < N_HEADS:
                    nxt = _dot_nt(q_ref[h + 1], k_ref[h + 1])
                if masked:
                    sc = jnp.where(keep, sc, NEG)
                m_prev = m_ref[h]
                m_new = jnp.maximum(m_prev, jnp.max(sc, axis=1, keepdims=True))
                pe = jnp.exp2(sc - jnp.tile(m_new, (1, tb // LANES))).astype(BF16)
                m_ref[h] = m_new
                if pending is not None:
                    finish(*pending)
                pending = (h, pe, jnp.exp2(m_prev - m_new))
            finish(*pending)

        @pl.when(ki < qi)
        def _():
            step(False)

        @pl.when(ki == qi)
        def _():
            step(True)
            lane = lax.broadcasted_iota(jnp.int32, (tb, LANES), 1)
            for p in range(N_HEADS // 2):
                outs = []
                for h in (2 * p, 2 * p + 1):
                    acc = acc_ref[h]
                    l = acc[:, V_AUX:V_AUX + 1]
                    outs.append(acc * (1.0 / l))
                    hi, lo = _hi_lo(m_ref[h][:, 0:1] + jnp.log(l) * LOG2E)
                    qb_ref[h] = _lane_pair((tb, LANES), QK_AUX, hi, lo, q_ref[h].astype(F32)).astype(BF16)
                o_ref[:, p * LANES:(p + 1) * LANES] = jnp.where(lane < HEAD, outs[0], pltpu.roll(outs[1], HEAD, 1)).astype(BF16)

    sd = jax.ShapeDtypeStruct
    qspec = pl.BlockSpec((N_HEADS, tb, LANES), lambda p, qi_ref, ki_ref: (0, qi_ref[p], 0))
    kspec = pl.BlockSpec((N_HEADS, tb, LANES), lambda p, qi_ref, ki_ref: (0, ki_ref[p], 0))
    return pl.pallas_call(
        body, name="mla_flash_fwd",
        grid_spec=pltpu.PrefetchScalarGridSpec(
            num_scalar_prefetch=2, grid=(len(pairs),),
            in_specs=[qspec, kspec, kspec],
            out_specs=[pl.BlockSpec((tb, MLA_W), lambda p, qi_ref, ki_ref: (qi_ref[p], 0)), qspec],
            scratch_shapes=[pltpu.VMEM((N_HEADS, tb, LANES), F32), pltpu.VMEM((N_HEADS, tb, LANES), F32)]),
        out_shape=[sd((s, MLA_W), BF16), sd((N_HEADS, s, LANES), BF16)],
        compiler_params=_cp("arbitrary"),
    )(qi_of, ki_of, q, k, v)


def _out_proj_call(x, yret, ymla, wout, ts):
    s = x.shape[0]

    def body(x_ref, yr_ref, ym_ref, w_ref, x1_ref, r_ref):
        x1 = x_ref[...] + _dot(yr_ref[...], w_ref[0:RET_W, :]) + _dot(ym_ref[...], w_ref[RET_W:, :])
        x1_ref[...] = x1
        r_ref[...] = _rstd(x1)

    sd = jax.ShapeDtypeStruct
    return pl.pallas_call(
        body, name="out_proj", grid=(s // ts,),
        in_specs=[_row(ts, D_MODEL), _row(ts, RET_W), _row(ts, MLA_W), _full((D_MODEL, D_MODEL))],
        out_specs=[_row(ts, D_MODEL), _row(ts, 1)],
        out_shape=[sd((s, D_MODEL), F32), sd((s, 1), F32)],
        compiler_params=_cp("parallel"),
    )(x, yret, ymla, wout)


W_UP_SHARD = F2 // 4


def _ffn_fwd_call(x1, r2, fnw, wup4, cw, cb, wdown, tgt, fw, ts):
    s = x1.shape[0]
    wsh = W_UP_SHARD

    def body(x_ref, r_ref, fnw_ref, wup_ref, cw_ref, cb_ref, wd_ref, t_ref, fw_ref,
             u_ref, uc_ref, dx2_ref, loss_ref, gfw_ref, carry_ref):
        _zero_first(pl.program_id(0) == 0, carry_ref, loss_ref, gfw_ref)
        xv = x_ref[...]
        h = (xv * r_ref[...] * fnw_ref[...]).astype(BF16)
        conv = []
        for j in range(4):
            cols = slice(j * wsh, (j + 1) * wsh)
            ub = _dot(h, wup_ref[j]).astype(BF16)
            u_ref[:, cols] = ub
            u = ub.astype(F32)
            u1, u2 = _shifted(u, carry_ref[:, cols])
            w = cw_ref[:, cols]
            cb16 = (cb_ref[:, cols] + w[0:1, :] * u2 + w[1:2, :] * u1 + w[2:3, :] * u).astype(BF16)
            uc_ref[:, cols] = cb16
            conv.append(cb16.astype(F32))
            carry_ref[:, cols] = u[ts - 8:, :]
        acc = xv
        for j in range(2):
            a = (_silu(conv[j]) * conv[j + 2]).astype(BF16)
            acc = acc + _dot(a, wd_ref[j * wsh:(j + 1) * wsh, :])
        r = _rstd(acc)
        xh = acc * r
        fwv = fw_ref[...]
        e = xh * fwv - t_ref[...]
        loss_ref[...] += (0.5 / D_MODEL) * _colsum(jnp.sum(e * e, axis=1, keepdims=True))
        dy = e * (1.0 / D_MODEL)
        gfw_ref[...] += _colsum(dy * xh)
        dx2_ref[...] = _norm_bwd(dy, xh, r, fwv)

    sd = jax.ShapeDtypeStruct
    once = lambda shape: pl.BlockSpec(shape, lambda i: (0,) * len(shape), pipeline_mode=pl.Buffered(1))
    return pl.pallas_call(
        body, name="ffn_fwd_loss", grid=(s // ts,),
        in_specs=[_row(ts, D_MODEL), _row(ts, 1), once((1, D_MODEL)), once((4, D_MODEL, wsh)),
                  once((3, F2)), once((1, F2)), once((D_FF, D_MODEL)), _row(ts, D_MODEL), once((1, D_MODEL))],
        out_specs=[_row(ts, F2), _row(ts, F2), _row(ts, D_MODEL), _full((1, 1)), _full((1, D_MODEL))],
        out_shape=[sd((s, F2), BF16), sd((s, F2), BF16), sd((s, D_MODEL), F32), sd((1, 1), F32), sd((1, D_MODEL), F32)],
        scratch_shapes=[pltpu.VMEM((8, F2), F32)],
        compiler_params=_cp("arbitrary", vmem=VMEM_LIMIT_MLP),
    )(x1, r2, fnw, wup4, cw, cb, wdown, tgt, fw)


def _shifted(u, hal):
    row = lax.broadcasted_iota(jnp.int32, hal.shape, 0)
    r1, r2 = pltpu.roll(u, 1, 0), pltpu.roll(u, 2, 0)
    top1 = jnp.where(row == 0, hal[7:8, :], r1[0:8, :])
    top2 = jnp.where(row == 0, hal[6:7, :], jnp.where(row == 1, hal[7:8, :], r2[0:8, :]))
    return jnp.concatenate([top1, r1[8:, :]], axis=0), jnp.concatenate([top2, r2[8:, :]], axis=0)


def _prep_weights(w):
    win = w["w_in"]
    pad = lambda n: jnp.zeros((D_MODEL, n), win.dtype)
    win_ext = jnp.concatenate([win[:, :IN_W - ROPE], pad(KPE_LO), win[:, IN_W - ROPE:], pad(LANES - KPE_LO - ROPE)], -1)
    wuq = w["w_uq"].reshape(Q_RANK, N_HEADS, HEAD + ROPE)
    wq = jnp.concatenate([wuq, jnp.zeros((Q_RANK, N_HEADS, LANES - HEAD - ROPE), wuq.dtype)], -1).transpose(1, 0, 2)
    wukv = w["w_ukv"].reshape(KV_RANK, N_HEADS, 2 * HEAD)
    zk = jnp.zeros((KV_RANK, N_HEADS, HEAD), wukv.dtype)
    wk = jnp.concatenate([wukv[:, :, :HEAD], zk], -1).transpose(1, 0, 2)
    wv = jnp.concatenate([wukv[:, :, HEAD:], zk], -1).transpose(1, 0, 2)
    c = lambda a: a.astype(BF16)
    return dict(win=c(win_ext), wq=c(wq), wk=c(wk), wv=c(wv), wout=c(w["w_out"]))


def _prep_mlp_weights(w):
    wup = w["w_up"]
    if wup.ndim == 2:
        wup = wup.reshape(D_MODEL, 4, W_UP_SHARD).transpose(1, 0, 2)
    return dict(wup=wup.astype(BF16), wdown=w["w_down"].astype(BF16))


def _tiles(s):
    return dict(ts=min(s, 512), tr=min(s, 2048), tbf=min(s, 1024), tb=min(s, 512), t2=min(s, 256),
                tw=min(s, 2048), t1=min(s, 1024))


class _Exchanges:
    def __init__(self, w):
        self.w = w

    def mlp_weights(self, after):
        return self.w

    def mlp_grads(self, gw):
        pass

    def behind_out_bwd(self, after):
        pass

    def behind_attention(self, after):
        pass


def _forward(x, positions, tgt, w, small, ex):
    s = x.shape[0]
    t = _tiles(s)
    pw = _prep_weights(w)
    cos_r, sin_r, cos_m, sin_m = _rope_tables(positions)
    rc = _ret_consts()
    q, k, v, g, cq, ckv, mq, mk, mv, r1 = _f1_call(
        x, small["attn_norm_w"], pw["win"], small["mla_q_norm_w"], small["mla_kv_norm_w"], pw["wq"], pw["wk"], pw["wv"],
        cos_r, sin_r, cos_m, sin_m, t["ts"])
    o_ret, y_ret = _ret_fwd_call(q, k, v, g, small["ret_gn_w"], rc, t["tr"])
    y_mla, mqb = _flash_fwd_call(mq, mk, mv, t["tbf"])
    x1, r2 = _out_proj_call(x, y_ret, y_mla, pw["wout"], t["ts"])
    pw.update(_prep_mlp_weights(ex.mlp_weights(r2)))
    u, uc, dx2, loss, g_fw = _ffn_fwd_call(x1, r2, small["ffn_norm_w"], pw["wup"], w["conv_w"], small["conv_b"], pw["wdown"],
                                           tgt, small["final_norm_w"], t["ts"])
    return dict(pw=pw, tabs=(cos_r, sin_r, cos_m, sin_m), rc=rc, q=q, k=k, v=v, g=g, cq=cq, ckv=ckv, r1=r1,
                o_ret=o_ret, y_ret=y_ret, mqb=mqb, mk=mk, mv=mv, y_mla=y_mla, x1=x1, r2=r2, u=u, uc=uc,
                dx2=dx2, loss=loss, g_fw=g_fw)


def _norm_bwd(dh, xh, r, nw):
    dxn = dh * nw
    return r * (dxn - xh * jnp.mean(dxn * xh, axis=-1, keepdims=True))


def _ordered_after(body, order):
    if order is None:
        return body, [], []
    return (lambda order_ref, *refs: body(*refs)), [pl.BlockSpec(memory_space=pl.ANY)], [order]


def _zero_first(first, *refs):
    @pl.when(first)
    def _():
        for ref in refs:
            ref[...] = jnp.zeros_like(ref)


def _colsum(v):
    return jnp.sum(v, axis=0, keepdims=True)


def _dsilu(g, sg):
    return sg * (1.0 + g * (1.0 - sg))


def _ffn_bwd_call(dx2, u, uc, cw, wdown, wup4, x1, r2, fnw, ts):
    s = dx2.shape[0]
    nt = s // ts
    wsh = W_UP_SHARD
    rev = lambda i: nt - 1 - i

    def body(dx2_ref, u_ref, uc_ref, cw_ref, wd_ref, wup_ref, x_ref, r_ref, fnw_ref,
             du_ref, dx1_ref, dcw_ref, dcb_ref, dfnw_ref, dwd_hbm, carry_ref, dwd_ref, sem):
        i = pl.program_id(0)
        _zero_first(i == 0, carry_ref, dwd_ref, dcw_ref, dcb_ref, dfnw_ref)
        dxb = dx2_ref[...].astype(BF16)
        dh = jnp.zeros((ts, D_MODEL), F32)
        for j in range(2):
            gcols = slice(j * wsh, (j + 1) * wsh)
            vcols = slice(D_FF + j * wsh, D_FF + (j + 1) * wsh)
            gate, val = uc_ref[:, gcols].astype(F32), uc_ref[:, vcols].astype(F32)
            da = _dot_nt(dxb, wd_ref[gcols, :])
            sg = _sigmoid(gate)
            sl = gate * sg
            dwd_ref[gcols, :] += _dot_tn((sl * val).astype(BF16), dxb)
            for d, cols, shard in ((da * val * _dsilu(gate, sg), gcols, j), (da * sl, vcols, 2 + j)):
                d1, d2 = _shifted_up(d, carry_ref[:, cols])
                uv = u_ref[:, cols].astype(F32)
                for t, dt in enumerate((d2, d1, d)):
                    dcw_ref[t:t + 1, cols] += _colsum(dt * uv)
                dcb_ref[:, cols] += _colsum(d)
                w = cw_ref[:, cols]
                du = (w[2:3, :] * d + w[1:2, :] * d1 + w[0:1, :] * d2).astype(BF16)
                du_ref[:, cols] = du
                dh = dh + _dot_nt(du, wup_ref[shard])
                carry_ref[:, cols] = d[0:8, :]
        r = r_ref[...]
        xh = x_ref[...] * r
        dfnw_ref[...] += _colsum(dh * xh)
        dx1_ref[...] = dx2_ref[...] + _norm_bwd(dh, xh, r, fnw_ref[...])

        @pl.when(i == nt - 1)
        def _():
            cp = pltpu.make_async_copy(dwd_ref, dwd_hbm, sem)
            cp.start()
            cp.wait()

    sd = jax.ShapeDtypeStruct
    row = lambda c: pl.BlockSpec((ts, c), lambda i: (rev(i), 0))
    once = lambda shape: pl.BlockSpec(shape, lambda i: (0,) * len(shape), pipeline_mode=pl.Buffered(1))
    return pl.pallas_call(
        body, name="ffn_bwd", grid=(nt,),
        in_specs=[row(D_MODEL), row(F2), row(F2), once((3, F2)), once((D_FF, D_MODEL)), once((4, D_MODEL, wsh)),
                  row(D_MODEL), row(1), once((1, D_MODEL))],
        out_specs=[row(F2), row(D_MODEL), _full((3, F2)), _full((1, F2)), _full((1, D_MODEL)), pl.BlockSpec(memory_space=pl.ANY)],
        out_shape=[sd((s, F2), BF16), sd((s, D_MODEL), F32), sd((3, F2), F32), sd((1, F2), F32), sd((1, D_MODEL), F32),
                   sd((D_FF, D_MODEL), F32)],
        scratch_shapes=[pltpu.VMEM((8, F2), F32), pltpu.VMEM((D_FF, D_MODEL), F32), pltpu.SemaphoreType.DMA],
        compiler_params=_cp("arbitrary", vmem=VMEM_LIMIT_MLP),
    )(dx2, u, uc, cw, wdown, wup4, x1, r2, fnw)


def _shifted_up(d, hal):
    n = d.shape[0]
    row = lax.broadcasted_iota(jnp.int32, hal.shape, 0)
    r1, r2 = pltpu.roll(d, n - 1, 0), pltpu.roll(d, n - 2, 0)
    end1 = jnp.where(row == 7, hal[0:1, :], r1[n - 8:, :])
    end2 = jnp.where(row == 6, hal[0:1, :], jnp.where(row == 7, hal[1:2, :], r2[n - 8:, :]))
    return jnp.concatenate([r1[:n - 8, :], end1], axis=0), jnp.concatenate([r2[:n - 8, :], end2], axis=0)


def _dw_norm_call(x, r, nw, b, ts, tn, name):
    s, n = b.shape
    k = x.shape[1]

    def body(x_ref, r_ref, nw_ref, b_ref, dw_ref):
        _zero_first(pl.program_id(1) == 0, dw_ref)
        h = (x_ref[...] * r_ref[...] * nw_ref[...]).astype(BF16)
        dw_ref[...] += _dot_tn(h, b_ref[...])

    return pl.pallas_call(
        body, name=name, grid=(n // tn, s // ts),
        in_specs=[pl.BlockSpec((ts, k), lambda j, i: (i, 0)), pl.BlockSpec((ts, 1), lambda j, i: (i, 0)),
                  pl.BlockSpec((1, k), lambda j, i: (0, 0)), pl.BlockSpec((ts, tn), lambda j, i: (i, j))],
        out_specs=pl.BlockSpec((None, k, tn), lambda j, i: (j, 0, 0)),
        out_shape=jax.ShapeDtypeStruct((n // tn, k, tn), F32),
        compiler_params=_cp("parallel", "arbitrary"),
    )(x, r, nw, b)


def _out_bwd_call(dx1, yret, ymla, wout, ts, order=None):
    s = dx1.shape[0]

    def body(dx_ref, yr_ref, ym_ref, w_ref, dyr_ref, do_ref, dwo_ref):
        _zero_first(pl.program_id(0) == 0, dwo_ref)
        dxb = dx_ref[...].astype(BF16)
        dmix = _dot_nt(dxb, w_ref[...])
        dyr_ref[...] = dmix[:, :RET_W]
        ym = ym_ref[...]
        lane = lax.broadcasted_iota(jnp.int32, (ts, LANES), 1)
        for p in range(N_HEADS // 2):
            dom = dmix[:, RET_W + p * LANES:RET_W + (p + 1) * LANES]
            prod = dom * ym[:, p * LANES:(p + 1) * LANES].astype(F32)
            for hh in range(2):
                mine = (lane >= HEAD) if hh else (lane < HEAD)
                hi, lo = _hi_lo(jnp.sum(jnp.where(mine, prod, 0.0), axis=1, keepdims=True))
                base = jnp.where(lane < HEAD, pltpu.roll(dom, HEAD, 1) if hh else dom, 0.0)
                do_ref[2 * p + hh] = _lane_pair((ts, LANES), V_AUX, -hi, -lo, base).astype(BF16)
        dwo_ref[0:RET_W, :] += _dot_tn(yr_ref[...], dxb)
        dwo_ref[RET_W:, :] += _dot_tn(ym, dxb)

    sd = jax.ShapeDtypeStruct
    body, first_specs, first = _ordered_after(body, order)
    return pl.pallas_call(
        body, name="out_proj_bwd", grid=(s // ts,),
        in_specs=first_specs + [_row(ts, D_MODEL), _row(ts, RET_W), _row(ts, MLA_W), _full((D_MODEL, D_MODEL))],
        out_specs=[_row(ts, RET_W), _hrow(N_HEADS, ts, LANES), _full((D_MODEL, D_MODEL))],
        out_shape=[sd((s, RET_W), F32), sd((N_HEADS, s, LANES), BF16), sd((D_MODEL, D_MODEL), F32)],
        compiler_params=_cp("arbitrary"),
    )(*first, dx1, yret, ymla, wout)


def _ret_bwd_q_call(q, k, v, o, g, dy, gnw, rc, cos_r, sin_r, tr):
    s = q.shape[0]
    c = RET_CHUNK
    nc = tr // c
    ns = RET_SLABS

    def body(q_ref, k_ref, v_ref, o_ref, g_ref, dy_ref, gnw_ref, dm_ref, zeta_ref, xi_ref, cd_ref, bd_ref, cr_ref, sr_ref,
             dq_ref, dg_ref, do_ref, dgnw_ref, st_ref):
        _zero_first(pl.program_id(1) == 0, st_ref, dgnw_ref)
        bd = bd_ref[...]
        avg = bd * (1.0 / HEAD)
        chunks = [slice(ci * c, (ci + 1) * c) for ci in range(nc)]
        lanes = [slice(sl * LANES, (sl + 1) * LANES) for sl in range(ns)]
        dov = []
        for ln in lanes:
            ov = o_ref[:, ln]
            ctr = ov - _dot_hi(ov, avg)
            rs = lax.rsqrt(_dot_hi(ctr * ctr, avg) + EPS)
            oh = ctr * rs
            gg, dyv, gnw_v = g_ref[:, ln], dy_ref[:, ln], gnw_ref[:, ln]
            sg = _sigmoid(gg)
            sl = gg * sg
            dg_ref[:, ln] = (dyv * oh * gnw_v * _dsilu(gg, sg)).astype(BF16)
            dgnw_ref[:, ln] += _colsum(dyv * sl * oh)
            doh = dyv * sl * gnw_v
            dov.append((rs * (doh - _dot_hi(doh, avg) - oh * _dot_hi(doh * oh, avg))).astype(BF16))
            do_ref[:, ln] = dov[-1]
        states = _ret_states(k_ref, v_ref, zeta_ref, cd_ref, bd, st_ref, chunks, lanes, False)
        for ci, rows in enumerate(chunks):
            for sl, ln in enumerate(lanes):
                doc = dov[sl][rows, :]
                dq = (_dot_nt(doc, states[sl][ci]) * xi_ref[sl]
                      + _pair_product(doc, _stack_heads(v_ref[rows, ln]), dm_ref[sl], _stack_heads(k_ref[rows, ln])))
                dq_ref[rows, ln] = _unrope(dq, cr_ref[rows, :], sr_ref[rows, :], HEAD // 2).astype(BF16)

    specs = _ret_specs(tr, lambda i: i)
    sd = jax.ShapeDtypeStruct
    return pl.pallas_call(
        body, name="ret_bwd_q", grid=(4 // ns, s // tr),
        in_specs=[specs["slab"]] * 6 + [specs["vec"], specs["dmask"], specs["rows"], specs["rows"], specs["state"], specs["bd"],
                                        specs["tab"], specs["tab"]],
        out_specs=[specs["slab"]] * 3 + [specs["vec"]],
        out_shape=[sd((s, RET_W), BF16), sd((s, RET_W), BF16), sd((s, RET_W), BF16), sd((1, RET_W), F32)],
        scratch_shapes=[pltpu.VMEM((ns, LANES, LANES), F32)],
        compiler_params=_cp("parallel", "arbitrary"),
    )(q, k, v, o, g, dy, gnw, rc["dmask"], rc["zeta"], rc["xi"], rc["cd"], rc["bd"], cos_r, sin_r)


def _ret_bwd_kv_call(q, k, v, do, rc, cos_r, sin_r, tr):
    s = q.shape[0]
    c = RET_CHUNK
    nc = tr // c
    nt = s // tr
    ns = RET_SLABS

    def body(q_ref, k_ref, v_ref, do_ref, dm_ref, zeta_ref, xi_ref, cd_ref, bd_ref, cr_ref, sr_ref, dk_ref, dv_ref, gs_ref):
        _zero_first(pl.program_id(1) == 0, gs_ref)
        bd = bd_ref[...]
        chunks = [slice(ci * c, (ci + 1) * c) for ci in range(nc)]
        lanes = [slice(sl * LANES, (sl + 1) * LANES) for sl in range(ns)]
        states = _ret_states(q_ref, do_ref, xi_ref, cd_ref, bd, gs_ref, chunks, lanes, True)
        for ci, rows in enumerate(chunks):
            for sl, ln in enumerate(lanes):
                kc, vc = k_ref[rows, ln], v_ref[rows, ln]
                q2, do2 = _stack_heads(q_ref[rows, ln]), _stack_heads(do_ref[rows, ln])
                gb = states[sl][ci]
                dk = _dot_nt(vc, gb) * zeta_ref[sl] + _pair_product(vc, do2, dm_ref[sl], q2)
                dv = _dot(kc, gb) * zeta_ref[sl] + _pair_product(kc, q2, dm_ref[sl], do2)
                dk_ref[rows, ln] = (_unrope(dk, cr_ref[rows, :], sr_ref[rows, :], HEAD // 2) * (HEAD ** -0.5)).astype(BF16)
                dv_ref[rows, ln] = dv.astype(BF16)

    specs = _ret_specs(tr, lambda i: nt - 1 - i)
    sd = jax.ShapeDtypeStruct
    return pl.pallas_call(
        body, name="ret_bwd_kv", grid=(4 // ns, nt),
        in_specs=[specs["slab"]] * 4 + [specs["dmask"], specs["rows"], specs["rows"], specs["state"], specs["bd"],
                                        specs["tab"], specs["tab"]],
        out_specs=[specs["slab"]] * 2,
        out_shape=[sd((s, RET_W), BF16), sd((s, RET_W), BF16)],
        scratch_shapes=[pltpu.VMEM((ns, LANES, LANES), F32)],
        compiler_params=_cp("parallel", "arbitrary"),
    )(q, k, v, do, rc["dmask_t"], rc["zeta"], rc["xi"], rc["cd"], rc["bd"], cos_r, sin_r)


FLASH_BWD_HEADS = 8


def _flash_bwd_call(qb, k, v, do, tb, order=None):
    s = qb.shape[1]
    nb = s // tb
    hg = FLASH_BWD_HEADS
    pairs = [(a, b) for a in range(nb) for b in range(a, nb)]
    ki_of, qi_of = (jnp.asarray(np.array(col, np.int32)) for col in zip(*pairs))
    extra = [] if order is None else [order]

    def body(ki_ref, qi_ref, *refs):
        q_ref, k_ref, v_ref, do_ref, dk_ref, dv_ref, dq_hbm, dka_ref, dva_ref, dq_ref, sem = refs[len(extra):]
        g, p = pl.program_id(0), pl.program_id(1)
        ki, qi = ki_ref[p], qi_ref[p]
        _zero_first(p == 0, dq_ref)
        _zero_first(qi == ki, dka_ref, dva_ref)
        rows = pl.ds(pl.multiple_of(qi * tb, tb), tb)

        def step(masked):
            if masked:
                keep = lax.broadcasted_iota(jnp.int32, (tb, tb), 0) <= lax.broadcasted_iota(jnp.int32, (tb, tb), 1)
            for h in range(hg):
                st = _dot_nt(k_ref[h], q_ref[h])
                if masked:
                    st = jnp.where(keep, st, NEG)
                pt = jnp.exp2(st)
                dob = do_ref[h]
                dva_ref[h] += _dot(pt.astype(BF16), dob)
                dst = (pt * _dot_nt(v_ref[h], dob)).astype(BF16)
                dka_ref[h] += _dot(dst, q_ref[h])
                dq_ref[h, rows, :] += _dot_tn(dst, k_ref[h])

        @pl.when(qi > ki)
        def _():
            step(False)

        @pl.when(qi == ki)
        def _():
            step(True)

        @pl.when(qi == nb - 1)
        def _():
            dk_ref[...] = (dka_ref[...] * LN2).astype(BF16)
            dv_ref[...] = dva_ref[...].astype(BF16)

        @pl.when(p == len(pairs) - 1)
        def _():
            cp = pltpu.make_async_copy(dq_ref, dq_hbm.at[pl.ds(g * hg, hg)], sem)
            cp.start()
            cp.wait()

    kspec = pl.BlockSpec((hg, tb, LANES), lambda g, p, ki_ref, qi_ref: (g, ki_ref[p], 0))
    qspec = pl.BlockSpec((hg, tb, LANES), lambda g, p, ki_ref, qi_ref: (g, qi_ref[p], 0))
    hm = jax.ShapeDtypeStruct((N_HEADS, s, LANES), BF16)
    return pl.pallas_call(
        body, name="mla_flash_bwd",
        grid_spec=pltpu.PrefetchScalarGridSpec(
            num_scalar_prefetch=2, grid=(N_HEADS // hg, len(pairs)),
            in_specs=[ANY] * len(extra) + [qspec, kspec, kspec, qspec],
            out_specs=[kspec, kspec, ANY],
            scratch_shapes=[pltpu.VMEM((hg, tb, LANES), F32), pltpu.VMEM((hg, tb, LANES), F32),
                            pltpu.VMEM((hg, s, LANES), F32), pltpu.SemaphoreType.DMA]),
        out_shape=[hm, hm, jax.ShapeDtypeStruct((N_HEADS, s, LANES), F32)],
        compiler_params=_cp("arbitrary", "arbitrary"),
    )(ki_of, qi_of, *extra, qb, k, v, do)


def _mla_post_call(dq, dk, dv, cq, ckv, qnw, kvnw, wq, wk, wv, cos_m, sin_m, ts):
    s = cq.shape[0]

    def body(dq_ref, dk_ref, dv_ref, cq_ref, ckv_ref, qnw_ref, kvnw_ref, wq_ref, wk_ref, wv_ref, cm_ref, sm_ref,
             dcq_ref, dckv_ref, dkpe_ref, dwq_ref, dwk_ref, dwv_ref, dqnw_ref, dkvnw_ref):
        _zero_first(pl.program_id(0) == 0, dwq_ref, dwk_ref, dwv_ref, dqnw_ref, dkvnw_ref)
        cqv, ckvv = cq_ref[...], ckv_ref[...]
        rq, rkv = _rstd(cqv), _rstd(ckvv)
        qh_, kvh_ = cqv * rq, ckvv * rkv
        qnw_v, kvnw_v = qnw_ref[...], kvnw_ref[...]
        cqn = (qh_ * qnw_v).astype(BF16)
        ckvn = (kvh_ * kvnw_v).astype(BF16)
        cm, sm = cm_ref[...], sm_ref[...]
        dcqn = jnp.zeros((ts, Q_RANK), F32)
        dckvn = jnp.zeros((ts, KV_RANK), F32)
        dkpe = jnp.zeros((ts, LANES), F32)
        for h in range(N_HEADS):
            dqu = _unrope(dq_ref[h] * SM_SCALE, cm, sm, ROPE // 2).astype(BF16)
            dwq_ref[h] += _dot_tn(cqn, dqu)
            dcqn = dcqn + _dot_nt(dqu, wq_ref[h])
            dkb, dvb = dk_ref[h], dv_ref[h]
            dkpe = dkpe + dkb.astype(F32)
            dwk_ref[h] += _dot_tn(ckvn, dkb)
            dwv_ref[h] += _dot_tn(ckvn, dvb)
            dckvn = dckvn + _dot_nt(dkb, wk_ref[h]) + _dot_nt(dvb, wv_ref[h])
        lane = lax.broadcasted_iota(jnp.int32, (ts, LANES), 1)
        dkpe = jnp.where((lane >= KPE_LO) & (lane < KPE_LO + ROPE), dkpe, 0.0)
        dkpe_ref[...] = _unrope(dkpe, cm, sm, ROPE // 2).astype(BF16)
        dqnw_ref[...] += _colsum(dcqn * qh_)
        dkvnw_ref[...] += _colsum(dckvn * kvh_)
        dcq_ref[...] = _norm_bwd(dcqn, qh_, rq, qnw_v).astype(BF16)
        dckv_ref[...] = _norm_bwd(dckvn, kvh_, rkv, kvnw_v).astype(BF16)

    sd = jax.ShapeDtypeStruct
    hm = _hrow(N_HEADS, ts, LANES)
    return pl.pallas_call(
        body, name="mla_post", grid=(s // ts,),
        in_specs=[hm, hm, hm, _row(ts, Q_RANK), _row(ts, KV_RANK), _full((1, Q_RANK)), _full((1, KV_RANK)),
                  _full((N_HEADS, Q_RANK, LANES)), _full((N_HEADS, KV_RANK, LANES)), _full((N_HEADS, KV_RANK, LANES)),
                  _row(ts, LANES), _row(ts, LANES)],
        out_specs=[_row(ts, Q_RANK), _row(ts, KV_RANK), _row(ts, LANES),
                   _full((N_HEADS, Q_RANK, LANES)), _full((N_HEADS, KV_RANK, LANES)), _full((N_HEADS, KV_RANK, LANES)),
                   _full((1, Q_RANK)), _full((1, KV_RANK))],
        out_shape=[sd((s, Q_RANK), BF16), sd((s, KV_RANK), BF16), sd((s, LANES), BF16),
                   sd((N_HEADS, Q_RANK, LANES), F32), sd((N_HEADS, KV_RANK, LANES), F32), sd((N_HEADS, KV_RANK, LANES), F32),
                   sd((1, Q_RANK), F32), sd((1, KV_RANK), F32)],
        compiler_params=_cp("arbitrary"),
    )(dq, dk, dv, cq, ckv, qnw, kvnw, wq, wk, wv, cos_m, sin_m)


def _in_bwd_call(parts, x, r1, anw, dx1, win, ts):
    s = x.shape[0]
    widths = [p.shape[1] for p in parts]
    np_ = len(parts)

    def body(*refs):
        p_refs = refs[:np_]
        x_ref, r_ref, anw_ref, dx1_ref, w_ref, dx_ref, dw_ref, danw_ref = refs[np_:]
        _zero_first(pl.program_id(0) == 0, dw_ref, danw_ref)
        dproj = jnp.concatenate([p[...] for p in p_refs], axis=-1)
        r, anw_v = r_ref[...], anw_ref[...]
        xh = x_ref[...] * r
        dw_ref[...] += _dot_tn((xh * anw_v).astype(BF16), dproj)
        dh = _dot_nt(dproj, w_ref[...])
        danw_ref[...] += _colsum(dh * xh)
        dx_ref[...] = dx1_ref[...] + _norm_bwd(dh, xh, r, anw_v)

    sd = jax.ShapeDtypeStruct
    return pl.pallas_call(
        body, name="in_proj_bwd", grid=(s // ts,),
        in_specs=[_row(ts, w) for w in widths]
        + [_row(ts, D_MODEL), _row(ts, 1), _full((1, D_MODEL)), _row(ts, D_MODEL), _full((D_MODEL, IN_EXT))],
        out_specs=[_row(ts, D_MODEL), _full((D_MODEL, IN_EXT)), _full((1, D_MODEL))],
        out_shape=[sd((s, D_MODEL), F32), sd((D_MODEL, IN_EXT), F32), sd((1, D_MODEL), F32)],
        compiler_params=_cp("arbitrary"),
    )(*parts, x, r1, anw, dx1, win)


def _local_step(x, positions, tgt, w, small, ex=None):
    s = x.shape[0]
    t = _tiles(s)
    ex = _Exchanges(w) if ex is None else ex
    f = _forward(x, positions, tgt, w, small, ex)
    pw, rc = f["pw"], f["rc"]
    cos_r, sin_r, cos_m, sin_m = f["tabs"]
    dx2, loss, g_fw = f["dx2"], f["loss"], f["g_fw"]
    du, dx1, g_cw, g_cb, g_fnw, g_wd = _ffn_bwd_call(dx2, f["u"], f["uc"], w["conv_w"], pw["wdown"], pw["wup"],
                                                     f["x1"], f["r2"], small["ffn_norm_w"], t["t2"])
    g_wup = _dw_norm_call(f["x1"], f["r2"], small["ffn_norm_w"], du, t["tw"], F2 // 4, "dw_up")
    started = ex.mlp_grads(dict(w_up=g_wup, w_down=g_wd))
    dy_ret, do, g_wout = _out_bwd_call(dx1, f["y_ret"], f["y_mla"], pw["wout"], t["t1"], started)
    started = ex.behind_out_bwd(g_wout)
    drq, dg, do_ret, g_gnw = _ret_bwd_q_call(f["q"], f["k"], f["v"], f["o_ret"], f["g"], dy_ret, small["ret_gn_w"], rc, cos_r, sin_r, t["tr"])
    drk, drv = _ret_bwd_kv_call(f["q"], f["k"], f["v"], do_ret, rc, cos_r, sin_r, t["tr"])
    dmk, dmv, dmq = _flash_bwd_call(f["mqb"], f["mk"], f["mv"], do, t["tb"], started)
    ex.behind_attention(dmk)
    dcq, dckv, dkpe, g_wq, g_wk, g_wv, g_qnw, g_kvnw = _mla_post_call(
        dmq, dmk, dmv, f["cq"], f["ckv"], small["mla_q_norm_w"], small["mla_kv_norm_w"], pw["wq"], pw["wk"], pw["wv"], cos_m, sin_m, t["ts"])
    gx, g_win_ext, g_anw = _in_bwd_call([drq, drk, drv, dg, dcq, dckv, dkpe], x, f["r1"], small["attn_norm_w"], dx1, pw["win"], t["ts"])
    lo = IN_W - ROPE
    g_win = jnp.concatenate([g_win_ext[:, :lo], g_win_ext[:, lo + KPE_LO:lo + KPE_LO + ROPE]], -1)
    g_wuq = g_wq.transpose(1, 0, 2)[:, :, :HEAD + ROPE].reshape(Q_RANK, N_HEADS * (HEAD + ROPE))
    g_wukv = jnp.concatenate([g_wk[:, :, :HEAD], g_wv[:, :, :HEAD]], -1).transpose(1, 0, 2).reshape(KV_RANK, 2 * MLA_W)
    gw = dict(w_in=g_win, w_uq=g_wuq, w_ukv=g_wukv, w_out=g_wout, w_up=g_wup,
              conv_w=g_cw, w_down=g_wd)
    gs = dict(attn_norm_w=g_anw, ret_gn_w=g_gnw, mla_q_norm_w=g_qnw, mla_kv_norm_w=g_kvnw, ffn_norm_w=g_fnw,
              conv_b=g_cb, final_norm_w=g_fw)
    return loss, gx, gw, gs


MESH_ID = pl.DeviceIdType.MESH
ANY = pl.BlockSpec(memory_space=pl.ANY)
VMEM_SPEC = pl.BlockSpec(memory_space=pltpu.VMEM)
N_DEV = 8
GROUP_A = (("w_in", (D_MODEL, IN_W // 4), 1), ("w_uq", (Q_RANK, 192), 1), ("w_ukv", (KV_RANK, 256), 1),
           ("w_out", (D_MODEL // 4, D_MODEL), 0))
GROUP_B = (("w_up", (D_MODEL, F2 // 4), 1), ("w_down", (D_FF // 4, D_MODEL), 0))
HBM_SPEC = pl.BlockSpec(memory_space=pltpu.HBM)
SEM_SPEC = pl.BlockSpec(memory_space=pltpu.SEMAPHORE)


def _mesh_pos():
    return lax.axis_index("x"), lax.axis_index("y"), lax.axis_index("c")


def _other_chips(x, y):
    return [(1 - x, y), (x, 1 - y), (1 - x, 1 - y)]


def _remote(src, dst, send_sems, recv_sems, k, dev):
    return pltpu.make_async_remote_copy(src_ref=src, dst_ref=dst, send_sem=send_sems.at[k], recv_sem=recv_sems.at[k],
                                        device_id=dev, device_id_type=MESH_ID)


def _gather_list_call(parts, tag):
    n = len(parts)

    def body(*refs):
        srcs, outs, (send_sems, recv_sems) = refs[:n], refs[n:2 * n], refs[2 * n:]
        x, y, c = _mesh_pos()
        sm = 2 * x + y
        chips = _other_chips(x, y)
        sib = (x, y, 1 - c)
        rc = lambda k, src, dst, dev: _remote(src, dst, send_sems, recv_sems, k, dev)
        first = [rc(7 * i + j, srcs[i].at[c], outs[i].at[sm, c], (cx, cy, c)) for i in range(n) for j, (cx, cy) in enumerate(chips)]
        own = [rc(7 * i + 6, srcs[i], outs[i].at[sm], sib) for i in range(n)]
        for cp in first + own:
            cp.start()
        passed = []
        for j, (cx, cy) in enumerate(chips):
            for i in range(n):
                land = outs[i].at[2 * cx + cy, c]
                rc(7 * i + j, srcs[i].at[c], land, (cx, cy, c)).wait_recv()
                cp = rc(7 * i + 3 + j, land, land, sib)
                cp.start()
                passed.append(cp)
        for j, (cx, cy) in enumerate(chips):
            for i in range(n):
                rc(7 * i + 3 + j, srcs[i].at[c], outs[i].at[2 * cx + cy, 1 - c], sib).wait_recv()
        for cp in own:
            cp.wait_recv()
        for cp in first + passed + own:
            cp.wait_send()

    return pl.pallas_call(
        body, name="weights_all_gather_" + tag,
        in_specs=[ANY] * n, out_specs=[ANY] * n,
        out_shape=[jax.ShapeDtypeStruct((4,) + p.shape, p.dtype) for p in parts],
        scratch_shapes=[pltpu.SemaphoreType.DMA((7 * n,)), pltpu.SemaphoreType.DMA((7 * n,))],
    )(*parts)


def _direct_gather_copies(srcs, lands, send_sems, recv_sems):
    x, y, c = _mesh_pos()
    sm = 2 * x + y
    sends, recvs = [], []
    for i, (src, land) in enumerate(zip(srcs, lands)):
        for j, (cx, cy) in enumerate(_other_chips(x, y)):
            for t in range(2):
                sends.append(_remote(src.at[c], land.at[sm, c], send_sems, recv_sems, 13 * i + 4 * j + 2 * c + t, (cx, cy, t)))
                recvs.append(_remote(src.at[t], land.at[2 * cx + cy, t], send_sems, recv_sems, 13 * i + 4 * j + 2 * t + c, (cx, cy, t)))
        sends.append(_remote(src, land.at[sm], send_sems, recv_sems, 13 * i + 12, (x, y, 1 - c)))
        recvs.append(_remote(src, land.at[sm], send_sems, recv_sems, 13 * i + 12, (x, y, 1 - c)))
    return sends, recvs


def _sibling_copies(srcs, lands, send_sems, recv_sems):
    x, y, c = _mesh_pos()
    cps = [_remote(src.at[s, 1 - c], land.at[s], send_sems, recv_sems, 4 * i + s, (x, y, 1 - c))
           for i, (src, land) in enumerate(zip(srcs, lands)) for s in range(4)]
    return cps, cps


def _chips_copies(srcs, lands, send_sems, recv_sems):
    x, y, c = _mesh_pos()
    cps = [_remote(src.at[2 * cx + cy], land.at[j], send_sems, recv_sems, 3 * i + j, (cx, cy, c))
           for i, (src, land) in enumerate(zip(srcs, lands)) for j, (cx, cy) in enumerate(_other_chips(x, y))]
    return cps, cps


def _share_copies(srcs, lands, send_sems, recv_sems):
    x, y, c = _mesh_pos()
    cps = [_remote(src, land, send_sems, recv_sems, i, (x, y, 1 - c)) for i, (src, land) in enumerate(zip(srcs, lands))]
    return cps, cps


def _exchange_call(name, copies, srcs, land_shapes, n_sems):
    n = len(srcs)

    def body(*refs):
        sends, recvs = copies(refs[:n], refs[n:2 * n], refs[2 * n], refs[2 * n + 1])
        for cp in sends:
            cp.start()
        for cp in sends:
            cp.wait_send()
        for cp in recvs:
            cp.wait_recv()

    return pl.pallas_call(
        body, name=name, in_specs=[ANY] * n, out_specs=[ANY] * n, out_shape=list(land_shapes),
        scratch_shapes=[pltpu.SemaphoreType.DMA((n_sems,)), pltpu.SemaphoreType.DMA((n_sems,))],
    )(*srcs)


def _exchange_start_call(name, copies, srcs, land_shapes, n_sems, order=None):
    n = len(srcs)
    extra = [] if order is None else [order]
    k = 2 * n + len(extra)

    def body(*refs):
        sends, _ = copies(refs[:n], refs[n:2 * n], refs[k], refs[k + 1])
        for cp in sends:
            cp.start()
        refs[-1][...] = jnp.zeros_like(refs[-1])

    hbm = lambda a: pltpu.with_memory_space_constraint(a, pltpu.HBM)
    lands = [hbm(lax.empty(sd.shape, sd.dtype)) for sd in land_shapes]
    sem = pltpu.SemaphoreType.DMA((n_sems,))
    out = pl.pallas_call(
        body, name=name,
        out_shape=(sem, sem, *[pltpu.HBM(a.shape, a.dtype) for a in list(srcs) + lands], jax.ShapeDtypeStruct((8, LANES), F32)),
        in_specs=[HBM_SPEC] * (2 * n) + [ANY] * len(extra), out_specs=(SEM_SPEC, SEM_SPEC, *[HBM_SPEC] * (2 * n), VMEM_SPEC),
        input_output_aliases={i: 2 + i for i in range(2 * n)},
        compiler_params=pltpu.CompilerParams(has_side_effects=pltpu.SideEffectType.DATAFLOW_SIDE_EFFECTING),
    )(*[hbm(a) for a in srcs], *lands, *extra)
    return out[0], out[1], out[2:2 + n], out[2 + n:2 + 2 * n], out[-1]


def _exchange_wait_call(name, copies, started, after):
    send_sems, recv_sems, srcs, lands, _ = started
    n = len(srcs)

    def body(*refs):
        sends, recvs = copies(refs[:n], refs[n:2 * n], refs[2 * n], refs[2 * n + 1])
        for cp in sends:
            cp.wait_send()
        for cp in recvs:
            cp.wait_recv()

    out = pl.pallas_call(
        body, name=name,
        out_shape=tuple(pltpu.HBM(a.shape, a.dtype) for a in list(srcs) + list(lands)),
        in_specs=[HBM_SPEC] * (2 * n) + [SEM_SPEC, SEM_SPEC, ANY], out_specs=tuple([HBM_SPEC] * (2 * n)),
        input_output_aliases={i: i for i in range(2 * n)},
        compiler_params=pltpu.CompilerParams(has_side_effects=pltpu.SideEffectType.DATAFLOW_SIDE_EFFECTING),
    )(*srcs, *lands, send_sems, recv_sems, after)
    return out[:n], out[n:]


def _rows_tile(rows, width, itemsize=4):
    limit = max(16, (3 << 20) // (width * itemsize))
    if rows <= limit:
        return rows
    return max(t for t in range(16, limit + 1, 16) if rows % t == 0)


def _sum_sibling_call(g, buf, c, name):
    _, _, rh, w = g.shape
    tile = _rows_tile(rh, w)

    def body(c_ref, g_ref, b_ref, p_ref, pb_ref):
        p = g_ref[...] + b_ref[...]
        p_ref[...] = p
        pb_ref[...] = p.astype(BF16)

    blk = pl.BlockSpec((None, tile, w), lambda s, i, c_ref: (s, i, 0))
    return pl.pallas_call(
        body, name=name,
        grid_spec=pltpu.PrefetchScalarGridSpec(
            num_scalar_prefetch=1, grid=(4, rh // tile),
            in_specs=[pl.BlockSpec((None, None, tile, w), lambda s, i, c_ref: (s, c_ref[0], i, 0)), blk],
            out_specs=[blk, blk]),
        out_shape=[jax.ShapeDtypeStruct((4, rh, w), F32), jax.ShapeDtypeStruct((4, rh, w), BF16)],
        compiler_params=_cp("parallel", "parallel"),
    )(c, g, buf)


def _sum_chips_call(p, buf, sm, name):
    _, rh, w = p.shape
    tile = _rows_tile(rh, w)

    def body(sm_ref, p_ref, b_ref, f_ref):
        f_ref[...] = ((p_ref[...] + b_ref[0].astype(F32)) + b_ref[1].astype(F32)) + b_ref[2].astype(F32)

    return pl.pallas_call(
        body, name=name,
        grid_spec=pltpu.PrefetchScalarGridSpec(
            num_scalar_prefetch=1, grid=(rh // tile,),
            in_specs=[pl.BlockSpec((None, tile, w), lambda i, sm_ref: (sm_ref[0], i, 0)),
                      pl.BlockSpec((3, tile, w), lambda i, sm_ref: (0, i, 0))],
            out_specs=pl.BlockSpec((tile, w), lambda i, sm_ref: (i, 0))),
        out_shape=jax.ShapeDtypeStruct((rh, w), F32),
        compiler_params=_cp("parallel"),
    )(sm, p, buf)


def _adamw_halves_call(w, g_mine, g_sib, c, m, v, name):
    r, wd = w.shape
    rh = r // 2
    tile = _rows_tile(rh, wd)
    nt = rh // tile

    def body(c_ref, w_ref, gm_ref, gs_ref, m_ref, v_ref, g_ref, d_ref, nm_ref, nv_ref):
        gv = jnp.where(pl.program_id(0) == c_ref[0], gm_ref[...], gs_ref[...])
        g_ref[...] = gv
        nm = ADAM_B1 * m_ref[...] + (1.0 - ADAM_B1) * gv
        nv = ADAM_B2 * v_ref[...] + (1.0 - ADAM_B2) * jnp.square(gv)
        m_hat = nm / (1.0 - ADAM_B1 ** ADAM_STEP)
        v_hat = nv / (1.0 - ADAM_B2 ** ADAM_STEP)
        d_ref[...] = -ADAM_LR * (m_hat / (jnp.sqrt(v_hat) + ADAM_EPS) + ADAM_WD * w_ref[...])
        nm_ref[...] = nm
        nv_ref[...] = nv

    whole = pl.BlockSpec((tile, wd), lambda h, i, c_ref: (h * nt + i, 0))
    half = pl.BlockSpec((tile, wd), lambda h, i, c_ref: (i, 0))
    sd = jax.ShapeDtypeStruct((r, wd), F32)
    return pl.pallas_call(
        body, name=name,
        grid_spec=pltpu.PrefetchScalarGridSpec(
            num_scalar_prefetch=1, grid=(2, nt),
            in_specs=[whole, half, half, whole, whole], out_specs=[whole] * 4),
        out_shape=[sd, sd, sd, sd],
        compiler_params=_cp("parallel", "parallel"),
    )(c, w, g_mine, g_sib, m, v)


def _all_reduce8_call(vec, name):
    rows = vec.shape[0]

    def body(v_ref, out_ref, slots, send_sems, recv_sems):
        x, y, c = _mesh_pos()
        me = 4 * x + 2 * y + c
        slots[me] = v_ref[...]

        def rcopy(k, to_me):
            bx, by, bc = (k >> 2) & 1, (k >> 1) & 1, k & 1
            px, py, pc = (1 - x if bx else x), (1 - y if by else y), (1 - c if bc else c)
            slot = 4 * px + 2 * py + pc if to_me else me
            return pltpu.make_async_remote_copy(src_ref=v_ref, dst_ref=slots.at[slot], send_sem=send_sems.at[k - 1],
                                                recv_sem=recv_sems.at[k - 1], device_id=(px, py, pc), device_id_type=MESH_ID)

        for k in range(1, N_DEV):
            rcopy(k, False).start()
        for k in range(1, N_DEV):
            rcopy(k, True).wait_recv()
        for k in range(1, N_DEV):
            rcopy(k, False).wait_send()
        tot = slots[0]
        for d in range(1, N_DEV):
            tot = tot + slots[d]
        out_ref[...] = tot

    return pl.pallas_call(
        body, name=name,
        in_specs=[VMEM_SPEC], out_specs=VMEM_SPEC,
        out_shape=jax.ShapeDtypeStruct((rows, LANES), F32),
        scratch_shapes=[pltpu.VMEM((N_DEV, rows, LANES), F32),
                        pltpu.SemaphoreType.DMA((N_DEV - 1,)), pltpu.SemaphoreType.DMA((N_DEV - 1,))],
    )(vec)


def _adamw_call(w, g, m, v, name):
    r, c = w.shape
    rb = r if r <= 256 else (256 if r % 256 == 0 else 352)
    assert r % rb == 0

    def body(w_ref, g_ref, m_ref, v_ref, d_ref, nm_ref, nv_ref):
        gv = g_ref[...]
        nm = ADAM_B1 * m_ref[...] + (1.0 - ADAM_B1) * gv
        nv = ADAM_B2 * v_ref[...] + (1.0 - ADAM_B2) * jnp.square(gv)
        m_hat = nm / (1.0 - ADAM_B1 ** ADAM_STEP)
        v_hat = nv / (1.0 - ADAM_B2 ** ADAM_STEP)
        d_ref[...] = -ADAM_LR * (m_hat / (jnp.sqrt(v_hat) + ADAM_EPS) + ADAM_WD * w_ref[...])
        nm_ref[...] = nm
        nv_ref[...] = nv

    spec = pl.BlockSpec((rb, c), lambda i: (i, 0))
    sd = jax.ShapeDtypeStruct((r, c), F32)
    return pl.pallas_call(
        body, name=name, grid=(r // rb,),
        in_specs=[spec] * 4, out_specs=[spec] * 3, out_shape=[sd, sd, sd],
        compiler_params=_cp("parallel"),
    )(w, g, m, v)


SMALL = (("attn_norm_w", D_MODEL), ("ret_gn_w", RET_W), ("mla_q_norm_w", Q_RANK), ("mla_kv_norm_w", KV_RANK),
         ("ffn_norm_w", D_MODEL), ("conv_b", F2), ("final_norm_w", D_MODEL))
WEIGHT_ORDER = ("attn_norm_w", "w_in", "ret_gn_w", "mla_q_norm_w", "w_uq", "mla_kv_norm_w", "w_ukv", "w_out",
                "ffn_norm_w", "w_up", "conv_w", "conv_b", "w_down", "final_norm_w")


def _pad_rows(flat, rows):
    return jnp.concatenate([flat, jnp.zeros((rows * LANES - flat.shape[0],), flat.dtype)]).reshape(rows, LANES)


def kernel(x, positions, attn_norm_w, w_in, ret_gn_w, mla_q_norm_w, w_uq, mla_kv_norm_w, w_ukv, w_out, ffn_norm_w, w_up, conv_w, conv_b, w_down, final_norm_w, loss_target, m_attn_norm_w, m_w_in, m_ret_gn_w, m_mla_q_norm_w, m_w_uq, m_mla_kv_norm_w, m_w_ukv, m_w_out, m_ffn_norm_w, m_w_up, m_conv_w, m_conv_b, m_w_down, m_final_norm_w, v_attn_norm_w, v_w_in, v_ret_gn_w, v_mla_q_norm_w, v_w_uq, v_mla_kv_norm_w, v_w_ukv, v_w_out, v_ffn_norm_w, v_w_up, v_conv_w, v_conv_b, v_w_down, v_final_norm_w):
    args = dict(locals())
    cx, cy, cc = _mesh_pos()
    sm = 2 * cx + cy

    c_arr, sm_arr = cc.reshape(1).astype(jnp.int32), sm.reshape(1).astype(jnp.int32)
    sds = jax.ShapeDtypeStruct

    def my_shards(group):
        return [args[n][0].astype(BF16).reshape(2, r // 2, c) for n, (r, c), _ in group]

    def full_weights(gathered, group):
        full = {}
        for (n, (r, c), axis), got in zip(group, gathered):
            piece = got.reshape(4, r, c)
            full[n] = piece if n == "w_up" else (piece.transpose(1, 0, 2).reshape(r, 4 * c) if axis == 1 else piece.reshape(4 * r, c))
        return full

    def by_owner(gw, group):
        out = []
        for n, (r, c), axis in group:
            g = gw[n]
            if axis == 1 and g.ndim == 2:
                g = g.reshape(r, 4, c).transpose(1, 0, 2)
            out.append(g.reshape(4, 2, r // 2, c))
        return out

    def sibling_shapes(gs):
        return [sds((4,) + g.shape[2:], F32) for g in gs]

    def chip_sums(gs, bufs, group):
        res = [_sum_sibling_call(g, b, c_arr, "grads_sum_sibling_" + n) for g, b, (n, _, _) in zip(gs, bufs, group)]
        return [p for p, _ in res], [pb for _, pb in res]

    def chips_shapes(pbs):
        return [sds((3,) + pb.shape[1:], BF16) for pb in pbs]

    def totals(ps, lands, group, tag):
        fins = [_sum_chips_call(p, l, sm_arr, "grads_sum_chips_" + n) for p, l, (n, _, _) in zip(ps, lands, group)]
        sibs = _exchange_call("grads_rs_share_" + tag, _share_copies, fins, [sds(f.shape, F32) for f in fins], len(fins))
        return {n: (f, s) for (n, _, _), f, s in zip(group, fins, sibs)}

    class StepExchanges(_Exchanges):
        def __init__(self, order):
            shards = my_shards(GROUP_B)
            self.gather = _exchange_start_call("weights_gather_start_b", _direct_gather_copies, shards,
                                               [sds((4,) + s.shape, BF16) for s in shards], 13 * len(shards), order)
            self.red = None

        def token(self):
            return self.gather[4][0:1, 0:1]

        def mlp_weights(self, after):
            return full_weights(_exchange_wait_call("weights_gather_wait_b", _direct_gather_copies, self.gather, after)[1], GROUP_B)

        def mlp_grads(self, gw):
            gs = by_owner(gw, GROUP_B)
            self.step1 = _exchange_start_call("grads_rs_sibling_start_b", _sibling_copies, gs, sibling_shapes(gs), 4 * len(gs))
            return self.step1[4]

        def behind_out_bwd(self, after):
            gs, bufs = _exchange_wait_call("grads_rs_sibling_wait_b", _sibling_copies, self.step1, after)
            self.ps, pbs = chip_sums(gs, bufs, GROUP_B)
            self.step2 = _exchange_start_call("grads_rs_chips_start_b", _chips_copies, pbs, chips_shapes(pbs), 3 * len(pbs))
            return self.step2[4]

        def behind_attention(self, after):
            _, lands = _exchange_wait_call("grads_rs_chips_wait_b", _chips_copies, self.step2, after)
            self.red = totals(self.ps, lands, GROUP_B, "b")

    gathered = _gather_list_call(my_shards(GROUP_A) + [conv_w[0].reshape(2, 1, 3 * F2 // 8)], "a")
    full = full_weights(gathered[:-1], GROUP_A)
    ex = StepExchanges(gathered[-1])
    full["conv_w"] = gathered[-1].reshape(4, 3, F2 // 4).transpose(1, 0, 2).reshape(3, F2)
    small = {n: args[n].reshape(1, d) for n, d in SMALL}
    small["attn_norm_w"] = small["attn_norm_w"] + ex.token()

    loss, gx, gw, gs = _local_step(x[0], positions[0], loss_target[0], full, small, ex)

    ga = by_owner(gw, GROUP_A)
    bufs = _exchange_call("grads_rs_sibling_a", _sibling_copies, ga, sibling_shapes(ga), 4 * len(ga))
    ps, pbs = chip_sums(ga, bufs, GROUP_A)
    lands = _exchange_call("grads_rs_chips_a", _chips_copies, pbs, chips_shapes(pbs), 3 * len(pbs))
    halves = {**ex.red, **totals(ps, lands, GROUP_A, "a")}

    vec = jnp.concatenate([gs[n].reshape(-1) for n, _ in SMALL] + [gw["conv_w"].reshape(-1), loss.reshape(-1)])
    tot = _all_reduce8_call(_pad_rows(vec, 216), "small_all_reduce").reshape(-1)
    red, off = {}, 0
    for n, d in SMALL:
        red[n] = tot[off:off + d].reshape(1, d)
        off += d
    red["conv_w"] = lax.dynamic_slice(tot[off:off + 3 * F2].reshape(3, F2), (0, sm * (F2 // 4)), (3, F2 // 4))
    loss_tot = tot[off + 3 * F2]

    grads, deltas, new_m, new_v = [], [], [], []
    for n in WEIGHT_ORDER:
        shape = args[n].shape
        two_d = (1, shape[0]) if len(shape) == 1 else shape[-2:]
        wmv = [args[k + n].reshape(two_d) for k in ("", "m_", "v_")]
        if n in halves:
            g, d, nm, nv = _adamw_halves_call(wmv[0], *halves[n], c_arr, wmv[1], wmv[2], "adamw_" + n)
        else:
            g = red[n].reshape(two_d)
            d, nm, nv = _adamw_call(wmv[0], g, wmv[1], wmv[2], "adamw_" + n)
        grads.append(g.reshape(shape))
        deltas.append(d.reshape(shape))
        new_m.append(nm.reshape(shape))
        new_v.append(nv.reshape(shape))
    return (loss_tot, gx[None], *grads, *deltas, *new_m, *new_v)
```

```python
import math

import numpy as np
import jax
import jax.numpy as jnp
from jax import lax
from jax.experimental import pallas as pl
from jax.experimental.pallas import tpu as pltpu

F32 = jnp.float32
BF16 = jnp.bfloat16

D_MODEL = 1024
N_HEADS = 8
HEAD = 64
RET_W = N_HEADS * HEAD
MLA_W = N_HEADS * HEAD
ROPE = 32
Q_RANK = 256
KV_RANK = 128
D_FF = 2816
F2 = 2 * D_FF
IN_W = 4 * RET_W + Q_RANK + KV_RANK + ROPE
IN_EXT = 4 * RET_W + Q_RANK + KV_RANK + 128
KPE_LO = 64
ROPE_BASE = 10000.0
EPS = 1e-6
RET_CHUNK = 256
SM_SCALE = (HEAD + ROPE) ** -0.5
LOG2E = math.log2(math.e)
LN2 = math.log(2.0)
NEG = -1e30
LANES = 128
VMEM_LIMIT = 56 * 1024 * 1024

ADAM_LR = 0.001
ADAM_B1 = 0.9
ADAM_B2 = 0.999
ADAM_EPS = 1e-08
ADAM_WD = 0.01
ADAM_STEP = 10


VMEM_LIMIT_MLP = 60 * 1024 * 1024


def _cp(*sem, vmem=VMEM_LIMIT):
    return pltpu.CompilerParams(dimension_semantics=sem, vmem_limit_bytes=vmem)


def _full(shape):
    n = len(shape)
    return pl.BlockSpec(tuple(shape), lambda *_: (0,) * n)


def _row(ts, c):
    return pl.BlockSpec((ts, c), lambda i: (i, 0))


def _hrow(h, ts, c):
    return pl.BlockSpec((h, ts, c), lambda i: (0, i, 0))


def _dot(a, b):
    return jnp.dot(a, b, preferred_element_type=F32)


def _dot_nt(a, b):
    return lax.dot_general(a, b, (((1,), (1,)), ((), ())), preferred_element_type=F32)


def _dot_tn(a, b):
    return lax.dot_general(a, b, (((0,), (0,)), ((), ())), preferred_element_type=F32)


def _dot_hi(a, b):
    hi = a.astype(BF16)
    lo = (a - hi.astype(F32)).astype(BF16)
    bb = b.astype(BF16)
    return _dot(hi, bb) + _dot(lo, bb)


def _rot_half(x, half):
    w = x.shape[-1]
    lane = lax.broadcasted_iota(jnp.int32, x.shape, x.ndim - 1)
    first = (lane % (2 * half)) < half
    return jnp.where(first, -pltpu.roll(x, w - half, x.ndim - 1), pltpu.roll(x, half, x.ndim - 1))


def _rope(x, cos, sin, half):
    return x * cos + _rot_half(x, half) * sin


def _unrope(dy, cos, sin, half):
    return dy * cos - _rot_half(dy, half) * sin


def _sigmoid(g):
    return 0.5 * jnp.tanh(0.5 * g) + 0.5


def _silu(g):
    return g * _sigmoid(g)


def _rstd(x):
    return lax.rsqrt(jnp.mean(x * x, axis=-1, keepdims=True) + EPS)


def _rope_tables(positions):
    s = positions.shape[0]
    hr, hm = HEAD // 2, ROPE // 2
    pos = positions.astype(F32)[None, :]
    inv_r = ROPE_BASE ** (-jnp.arange(0, HEAD, 2, dtype=F32) / HEAD)
    inv_m = ROPE_BASE ** (-jnp.arange(0, ROPE, 2, dtype=F32) / ROPE)
    ang = jnp.concatenate([inv_r, inv_m])[:, None] * pos
    packed = jnp.concatenate([jnp.cos(ang), jnp.sin(ang), jnp.zeros((LANES - 2 * (hr + hm), s), F32)], 0)
    tx = min(s, 1024)

    def spread(t, lane, pieces, fill):
        out = jnp.full(t.shape, fill, F32)
        for lo, src, width in pieces:
            moved = t if lo == src else pltpu.roll(t, (lo - src) % LANES, 1)
            out = jnp.where((lane >= lo) & (lane < lo + width), moved, out)
        return out

    def body(p_ref, cr_ref, sr_ref, cm_ref, sm_ref):
        t = p_ref[...].T
        lane = lax.broadcasted_iota(jnp.int32, t.shape, 1)
        cr_ref[...] = spread(t, lane, [(j * hr, 0, hr) for j in range(LANES // hr)], 0.0)
        sr_ref[...] = spread(t, lane, [(j * hr, hr + hm, hr) for j in range(LANES // hr)], 0.0)
        cm_ref[...] = spread(t, lane, [(KPE_LO, hr, hm), (KPE_LO + hm, hr, hm)], 1.0)
        sm_ref[...] = spread(t, lane, [(KPE_LO, 2 * hr + hm, hm), (KPE_LO + hm, 2 * hr + hm, hm)], 0.0)

    tab = jax.ShapeDtypeStruct((s, LANES), F32)
    return pl.pallas_call(
        body, name="rope_tables", grid=(s // tx,),
        in_specs=[pl.BlockSpec((LANES, tx), lambda i: (0, i))],
        out_specs=[_row(tx, LANES)] * 4, out_shape=[tab] * 4,
        compiler_params=_cp("parallel"),
    )(packed)


def _ret_consts():
    c = RET_CHUNK
    lg = np.log1p(-np.power(2.0, -5.0 - np.arange(N_HEADS, dtype=np.float64)))
    idx = np.arange(c, dtype=np.float64)
    diff = idx[:, None] - idx[None, :]
    lane_head = np.arange(LANES) // HEAD
    dmask = np.zeros((4, 2, c, c))
    zeta = np.zeros((4, c, LANES))
    xi = np.zeros((4, c, LANES))
    cd = np.zeros((4, LANES, LANES))
    bd = (lane_head[:, None] == lane_head[None, :]).astype(np.float64)
    for j in range(4):
        for hh in range(2):
            dmask[j, hh] = np.where(diff >= 0, np.exp(lg[2 * j + hh] * np.maximum(diff, 0.0)), 0.0)
        lgl = lg[2 * j + lane_head]
        zeta[j] = np.exp(lgl[None, :] * (c - 1.0 - idx[:, None]))
        xi[j] = np.exp(lgl[None, :] * (idx[:, None] + 1.0))
        cd[j] = np.exp(lgl * c)[:, None] * bd
    f = lambda a: jnp.asarray(a, F32)
    side = lambda d: np.concatenate([d[:, 0], d[:, 1]], axis=-1)
    return dict(dmask=f(side(dmask)), dmask_t=f(side(np.swapaxes(dmask, 2, 3))), zeta=f(zeta), xi=f(xi), cd=f(cd), bd=f(bd))


def _f1_call(x, anw, win, qnw, kvnw, wq, wk, wv, cos_r, sin_r, cos_m, sin_m, ts):
    s = x.shape[0]

    def body(x_ref, anw_ref, w_ref, qnw_ref, kvnw_ref, wq_ref, wk_ref, wv_ref, cr_ref, sr_ref, cm_ref, sm_ref,
             q_ref, k_ref, v_ref, g_ref, cq_ref, ckv_ref, mq_ref, mk_ref, mv_ref, r_ref):
        xv = x_ref[...]
        r = _rstd(xv)
        r_ref[...] = r
        h = (xv * r * anw_ref[...]).astype(BF16)
        cr, sr = cr_ref[...], sr_ref[...]
        qk = _dot(h, w_ref[:, 0:2 * RET_W])
        for j in range(4):
            sl = slice(j * LANES, (j + 1) * LANES)
            q_ref[:, sl] = _rope(qk[:, sl], cr, sr, HEAD // 2).astype(BF16)
            kk = qk[:, RET_W + j * LANES:RET_W + (j + 1) * LANES]
            k_ref[:, sl] = (_rope(kk, cr, sr, HEAD // 2) * (HEAD ** -0.5)).astype(BF16)
        v_ref[...] = _dot(h, w_ref[:, 2 * RET_W:3 * RET_W]).astype(BF16)
        g_ref[...] = _dot(h, w_ref[:, 3 * RET_W:4 * RET_W])
        o = 4 * RET_W
        cqv = _dot(h, w_ref[:, o:o + Q_RANK])
        ckvv = _dot(h, w_ref[:, o + Q_RANK:o + Q_RANK + KV_RANK])
        cq_ref[...] = cqv
        ckv_ref[...] = ckvv
        cm, sm = cm_ref[...], sm_ref[...]
        kp = _rope(_dot(h, w_ref[:, o + Q_RANK + KV_RANK:IN_EXT]), cm, sm, ROPE // 2)
        kp = _lane_pair((ts, LANES), QK_AUX, -1.0, -1.0, kp)
        cqn = (cqv * _rstd(cqv) * qnw_ref[...]).astype(BF16)
        ckvn = (ckvv * _rstd(ckvv) * kvnw_ref[...]).astype(BF16)
        for hd in range(N_HEADS):
            qh = _rope(_dot(cqn, wq_ref[hd]), cm, sm, ROPE // 2)
            mq_ref[hd] = (qh * (SM_SCALE * LOG2E)).astype(BF16)
            mk_ref[hd] = (_dot(ckvn, wk_ref[hd]) + kp).astype(BF16)
            mv_ref[hd] = _lane_pair((ts, LANES), V_AUX, 1.0, 1.0, _dot(ckvn, wv_ref[hd])).astype(BF16)

    sd = jax.ShapeDtypeStruct
    hm = sd((N_HEADS, s, LANES), BF16)
    return pl.pallas_call(
        body, name="f1_in_proj", grid=(s // ts,),
        in_specs=[_row(ts, D_MODEL), _full((1, D_MODEL)), _full((D_MODEL, IN_EXT)), _full((1, Q_RANK)), _full((1, KV_RANK)),
                  _full((N_HEADS, Q_RANK, LANES)), _full((N_HEADS, KV_RANK, LANES)), _full((N_HEADS, KV_RANK, LANES)),
                  _row(ts, LANES), _row(ts, LANES), _row(ts, LANES), _row(ts, LANES)],
        out_specs=[_row(ts, RET_W), _row(ts, RET_W), _row(ts, RET_W), _row(ts, RET_W),
                   _row(ts, Q_RANK), _row(ts, KV_RANK)] + [_hrow(N_HEADS, ts, LANES)] * 3 + [_row(ts, 1)],
        out_shape=[sd((s, RET_W), BF16), sd((s, RET_W), BF16), sd((s, RET_W), BF16), sd((s, RET_W), F32),
                   sd((s, Q_RANK), F32), sd((s, KV_RANK), F32), hm, hm, hm, sd((s, 1), F32)],
        compiler_params=_cp("parallel"),
    )(x, anw, win, qnw, kvnw, wq, wk, wv, cos_r, sin_r, cos_m, sin_m)


def _stack_heads(a):
    lo = lax.broadcasted_iota(jnp.int32, a.shape, 1) < HEAD
    zero = jnp.zeros_like(a)
    return jnp.concatenate([jnp.where(lo, a, zero), jnp.where(lo, zero, a)], axis=0)


def _pair_product(a, b2, decay2, w2):
    return _dot((_dot_nt(a, b2) * decay2).astype(BF16), w2)


RET_SLABS = 2


def _ret_specs(tr, tile_of):
    c, ns = RET_CHUNK, RET_SLABS
    return dict(
        slab=pl.BlockSpec((tr, ns * LANES), lambda j, i: (tile_of(i), j)),
        tab=pl.BlockSpec((tr, LANES), lambda j, i: (tile_of(i), 0)),
        vec=pl.BlockSpec((1, ns * LANES), lambda j, i: (0, j)),
        dmask=pl.BlockSpec((ns, c, 2 * c), lambda j, i: (j, 0, 0)),
        rows=pl.BlockSpec((ns, c, LANES), lambda j, i: (j, 0, 0)),
        state=pl.BlockSpec((ns, LANES, LANES), lambda j, i: (j, 0, 0)),
        bd=pl.BlockSpec((LANES, LANES), lambda j, i: (0, 0)))


def _ret_states(a_ref, b_ref, scale_ref, cd_ref, bd, st_ref, chunks, lanes, reverse):
    nc = len(chunks)
    contrib = [[_dot_tn((a_ref[rows, ln].astype(F32) * scale_ref[sl]).astype(BF16), b_ref[rows, ln]) * bd for rows in chunks]
               for sl, ln in enumerate(lanes)]
    states = []
    for sl in range(len(lanes)):
        st, seen = st_ref[sl], [None] * nc
        for ci in (reversed(range(nc)) if reverse else range(nc)):
            seen[ci] = st.astype(BF16)
            st = st * cd_ref[sl] + contrib[sl][ci]
        st_ref[sl] = st
        states.append(seen)
    return states


def _ret_fwd_call(q, k, v, g, gnw, rc, tr):
    s = q.shape[0]
    c = RET_CHUNK
    nc = tr // c
    ns = RET_SLABS

    def body(q_ref, k_ref, v_ref, g_ref, gnw_ref, dm_ref, zeta_ref, xi_ref, cd_ref, bd_ref, o_ref, y_ref, st_ref):
        @pl.when(pl.program_id(1) == 0)
        def _():
            st_ref[...] = jnp.zeros_like(st_ref)

        bd = bd_ref[...]
        chunks = [slice(ci * c, (ci + 1) * c) for ci in range(nc)]
        lanes = [slice(sl * LANES, (sl + 1) * LANES) for sl in range(ns)]
        states = _ret_states(k_ref, v_ref, zeta_ref, cd_ref, bd, st_ref, chunks, lanes, False)
        for ci, rows in enumerate(chunks):
            for sl, ln in enumerate(lanes):
                qc = q_ref[rows, ln]
                o_ref[rows, ln] = (_dot(qc, states[sl][ci]) * xi_ref[sl]
                                   + _pair_product(qc, _stack_heads(k_ref[rows, ln]), dm_ref[sl], _stack_heads(v_ref[rows, ln])))
        avg = bd * (1.0 / HEAD)
        for ln in lanes:
            o = o_ref[:, ln]
            ctr = o - _dot_hi(o, avg)
            var = _dot_hi(ctr * ctr, avg)
            y_ref[:, ln] = (_silu(g_ref[:, ln]) * (ctr * lax.rsqrt(var + EPS) * gnw_ref[:, ln])).astype(BF16)

    specs = _ret_specs(tr, lambda i: i)
    sd = jax.ShapeDtypeStruct
    return pl.pallas_call(
        body, name="ret_fwd", grid=(4 // ns, s // tr),
        in_specs=[specs["slab"]] * 4 + [specs["vec"], specs["dmask"], specs["rows"], specs["rows"], specs["state"], specs["bd"]],
        out_specs=[specs["slab"]] * 2,
        out_shape=[sd((s, RET_W), F32), sd((s, RET_W), BF16)],
        scratch_shapes=[pltpu.VMEM((ns, LANES, LANES), F32)],
        compiler_params=_cp("parallel", "arbitrary"),
    )(q, k, v, g, gnw, rc["dmask"], rc["zeta"], rc["xi"], rc["cd"], rc["bd"])


QK_AUX = HEAD + ROPE
V_AUX = HEAD


def _lane_pair(shape, lo, a, b, rest):
    lane = lax.broadcasted_iota(jnp.int32, shape, len(shape) - 1)
    return jnp.where(lane == lo, a, jnp.where(lane == lo + 1, b, rest))


def _hi_lo(v):
    hi = v.astype(BF16).astype(F32)
    return hi, v - hi


def _flash_fwd_call(q, k, v, tb):
    s = q.shape[1]
    nb = s // tb
    pairs = [(a, b) for a in range(nb) for b in range(a + 1)]
    qi_of, ki_of = (jnp.asarray(np.array(col, np.int32)) for col in zip(*pairs))

    def body(qi_ref, ki_ref, q_ref, k_ref, v_ref, o_ref, qb_ref, m_ref, acc_ref):
        qi, ki = qi_ref[pl.program_id(0)], ki_ref[pl.program_id(0)]

        @pl.when(ki == 0)
        def _():
            m_ref[...] = jnp.full_like(m_ref, NEG)
            acc_ref[...] = jnp.zeros_like(acc_ref)

        def step(masked):
            if masked:
                keep = lax.broadcasted_iota(jnp.int32, (tb, tb), 1) <= lax.broadcasted_iota(jnp.int32, (tb, tb), 0)
            def finish(h, pe, alpha):
                acc_ref[h] = acc_ref[h] * alpha + _dot(pe, v_ref[h])

            nxt, pending = _dot_nt(q_ref[0], k_ref[0]), None
            for h in range(N_HEADS):
                sc = nxt
                if h + 1 < N_HEADS:
                    nxt = _dot_nt(q_ref[h + 1], k_ref[h + 1])
                if masked:
                    sc = jnp.where(keep, sc, NEG)
                m_prev = m_ref[h]
                m_new = jnp.maximum(m_prev, jnp.max(sc, axis=1, keepdims=True))
                pe = jnp.exp2(sc - jnp.tile(m_new, (1, tb // LANES))).astype(BF16)
                m_ref[h] = m_new
                if pending is not None:
                    finish(*pending)
                pending = (h, pe, jnp.exp2(m_prev - m_new))
            finish(*pending)

        @pl.when(ki < qi)
        def _():
            step(False)

        @pl.when(ki == qi)
        def _():
            step(True)
            lane = lax.broadcasted_iota(jnp.int32, (tb, LANES), 1)
            for p in range(N_HEADS // 2):
                outs = []
                for h in (2 * p, 2 * p + 1):
                    acc = acc_ref[h]
                    l = acc[:, V_AUX:V_AUX + 1]
                    outs.append(acc * (1.0 / l))
                    hi, lo = _hi_lo(m_ref[h][:, 0:1] + jnp.log(l) * LOG2E)
                    qb_ref[h] = _lane_pair((tb, LANES), QK_AUX, hi, lo, q_ref[h].astype(F32)).astype(BF16)
                o_ref[:, p * LANES:(p + 1) * LANES] = jnp.where(lane < HEAD, outs[0], pltpu.roll(outs[1], HEAD, 1)).astype(BF16)

    sd = jax.ShapeDtypeStruct
    qspec = pl.BlockSpec((N_HEADS, tb, LANES), lambda p, qi_ref, ki_ref: (0, qi_ref[p], 0))
    kspec = pl.BlockSpec((N_HEADS, tb, LANES), lambda p, qi_ref, ki_ref: (0, ki_ref[p], 0))
    return pl.pallas_call(
        body, name="mla_flash_fwd",
        grid_spec=pltpu.PrefetchScalarGridSpec(
            num_scalar_prefetch=2, grid=(len(pairs),),
            in_specs=[qspec, kspec, kspec],
            out_specs=[pl.BlockSpec((tb, MLA_W), lambda p, qi_ref, ki_ref: (qi_ref[p], 0)), qspec],
            scratch_shapes=[pltpu.VMEM((N_HEADS, tb, LANES), F32), pltpu.VMEM((N_HEADS, tb, LANES), F32)]),
        out_shape=[sd((s, MLA_W), BF16), sd((N_HEADS, s, LANES), BF16)],
        compiler_params=_cp("arbitrary"),
    )(qi_of, ki_of, q, k, v)


def _out_proj_call(x, yret, ymla, wout, ts):
    s = x.shape[0]

    def body(x_ref, yr_ref, ym_ref, w_ref, x1_ref, r_ref):
        x1 = x_ref[...] + _dot(yr_ref[...], w_ref[0:RET_W, :]) + _dot(ym_ref[...], w_ref[RET_W:, :])
        x1_ref[...] = x1
        r_ref[...] = _rstd(x1)

    sd = jax.ShapeDtypeStruct
    return pl.pallas_call(
        body, name="out_proj", grid=(s // ts,),
        in_specs=[_row(ts, D_MODEL), _row(ts, RET_W), _row(ts, MLA_W), _full((D_MODEL, D_MODEL))],
        out_specs=[_row(ts, D_MODEL), _row(ts, 1)],
        out_shape=[sd((s, D_MODEL), F32), sd((s, 1), F32)],
        compiler_params=_cp("parallel"),
    )(x, yret, ymla, wout)


W_UP_SHARD = F2 // 4


def _ffn_fwd_call(x1, r2, fnw, wup4, cw, cb, wdown, tgt, fw, ts):
    s = x1.shape[0]
    wsh = W_UP_SHARD

    def body(x_ref, r_ref, fnw_ref, wup_ref, cw_ref, cb_ref, wd_ref, t_ref, fw_ref,
             u_ref, uc_ref, dx2_ref, loss_ref, gfw_ref, carry_ref):
        _zero_first(pl.program_id(0) == 0, carry_ref, loss_ref, gfw_ref)
        xv = x_ref[...]
        h = (xv * r_ref[...] * fnw_ref[...]).astype(BF16)
        conv = []
        for j in range(4):
            cols = slice(j * wsh, (j + 1) * wsh)
            ub = _dot(h, wup_ref[j]).astype(BF16)
            u_ref[:, cols] = ub
            u = ub.astype(F32)
            u1, u2 = _shifted(u, carry_ref[:, cols])
            w = cw_ref[:, cols]
            cb16 = (cb_ref[:, cols] + w[0:1, :] * u2 + w[1:2, :] * u1 + w[2:3, :] * u).astype(BF16)
            uc_ref[:, cols] = cb16
            conv.append(cb16.astype(F32))
            carry_ref[:, cols] = u[ts - 8:, :]
        acc = xv
        for j in range(2):
            a = (_silu(conv[j]) * conv[j + 2]).astype(BF16)
            acc = acc + _dot(a, wd_ref[j * wsh:(j + 1) * wsh, :])
        r = _rstd(acc)
        xh = acc * r
        fwv = fw_ref[...]
        e = xh * fwv - t_ref[...]
        loss_ref[...] += (0.5 / D_MODEL) * _colsum(jnp.sum(e * e, axis=1, keepdims=True))
        dy = e * (1.0 / D_MODEL)
        gfw_ref[...] += _colsum(dy * xh)
        dx2_ref[...] = _norm_bwd(dy, xh, r, fwv)

    sd = jax.ShapeDtypeStruct
    once = lambda shape: pl.BlockSpec(shape, lambda i: (0,) * len(shape), pipeline_mode=pl.Buffered(1))
    return pl.pallas_call(
        body, name="ffn_fwd_loss", grid=(s // ts,),
        in_specs=[_row(ts, D_MODEL), _row(ts, 1), once((1, D_MODEL)), once((4, D_MODEL, wsh)),
                  once((3, F2)), once((1, F2)), once((D_FF, D_MODEL)), _row(ts, D_MODEL), once((1, D_MODEL))],
        out_specs=[_row(ts, F2), _row(ts, F2), _row(ts, D_MODEL), _full((1, 1)), _full((1, D_MODEL))],
        out_shape=[sd((s, F2), BF16), sd((s, F2), BF16), sd((s, D_MODEL), F32), sd((1, 1), F32), sd((1, D_MODEL), F32)],
        scratch_shapes=[pltpu.VMEM((8, F2), F32)],
        compiler_params=_cp("arbitrary", vmem=VMEM_LIMIT_MLP),
    )(x1, r2, fnw, wup4, cw, cb, wdown, tgt, fw)


def _shifted(u, hal):
    row = lax.broadcasted_iota(jnp.int32, hal.shape, 0)
    r1, r2 = pltpu.roll(u, 1, 0), pltpu.roll(u, 2, 0)
    top1 = jnp.where(row == 0, hal[7:8, :], r1[0:8, :])
    top2 = jnp.where(row == 0, hal[6:7, :], jnp.where(row == 1, hal[7:8, :], r2[0:8, :]))
    return jnp.concatenate([top1, r1[8:, :]], axis=0), jnp.concatenate([top2, r2[8:, :]], axis=0)


def _prep_weights(w):
    win = w["w_in"]
    pad = lambda n: jnp.zeros((D_MODEL, n), win.dtype)
    left, right, at = [], [], IN_W - ROPE
    for j, blk in enumerate([win] if win.ndim == 2 else [win[j] for j in range(win.shape[0])]):
        cut = min(max(at - j * blk.shape[1], 0), blk.shape[1])
        left += [blk[:, :cut]] if cut else []
        right += [blk[:, cut:]] if cut < blk.shape[1] else []
    win_ext = jnp.concatenate(left + [pad(KPE_LO)] + right + [pad(LANES - KPE_LO - ROPE)], -1)
    wuq = w["w_uq"].reshape(Q_RANK, N_HEADS, HEAD + ROPE)
    wq = jnp.concatenate([wuq, jnp.zeros((Q_RANK, N_HEADS, LANES - HEAD - ROPE), wuq.dtype)], -1).transpose(1, 0, 2)
    wukv = w["w_ukv"].reshape(KV_RANK, N_HEADS, 2 * HEAD)
    zk = jnp.zeros((KV_RANK, N_HEADS, HEAD), wukv.dtype)
    wk = jnp.concatenate([wukv[:, :, :HEAD], zk], -1).transpose(1, 0, 2)
    wv = jnp.concatenate([wukv[:, :, HEAD:], zk], -1).transpose(1, 0, 2)
    c = lambda a: a.astype(BF16)
    return dict(win=c(win_ext), wq=c(wq), wk=c(wk), wv=c(wv), wout=c(w["w_out"]))


def _prep_mlp_weights(w):
    wup = w["w_up"]
    if wup.ndim == 2:
        wup = wup.reshape(D_MODEL, 4, W_UP_SHARD).transpose(1, 0, 2)
    return dict(wup=wup.astype(BF16), wdown=w["w_down"].astype(BF16))


def _tiles(s):
    return dict(ts=min(s, 512), tr=min(s, 2048), tbf=min(s, 1024), tb=min(s, 512), t2=min(s, 256),
                tw=min(s, 2048), t1=min(s, 1024))


class _Exchanges:
    def __init__(self, w):
        self.w = w

    def mlp_weights(self, after):
        return self.w

    def mlp_grads(self, gw):
        pass

    def behind_out_bwd(self, after):
        pass

    def behind_attention(self, after):
        pass


def _forward(x, positions, tgt, w, small, ex):
    s = x.shape[0]
    t = _tiles(s)
    pw = _prep_weights(w)
    cos_r, sin_r, cos_m, sin_m = _rope_tables(positions)
    rc = _ret_consts()
    q, k, v, g, cq, ckv, mq, mk, mv, r1 = _f1_call(
        x, small["attn_norm_w"], pw["win"], small["mla_q_norm_w"], small["mla_kv_norm_w"], pw["wq"], pw["wk"], pw["wv"],
        cos_r, sin_r, cos_m, sin_m, t["ts"])
    o_ret, y_ret = _ret_fwd_call(q, k, v, g, small["ret_gn_w"], rc, t["tr"])
    y_mla, mqb = _flash_fwd_call(mq, mk, mv, t["tbf"])
    x1, r2 = _out_proj_call(x, y_ret, y_mla, pw["wout"], t["ts"])
    pw.update(_prep_mlp_weights(ex.mlp_weights(r2)))
    u, uc, dx2, loss, g_fw = _ffn_fwd_call(x1, r2, small["ffn_norm_w"], pw["wup"], w["conv_w"], small["conv_b"], pw["wdown"],
                                           tgt, small["final_norm_w"], t["ts"])
    return dict(pw=pw, tabs=(cos_r, sin_r, cos_m, sin_m), rc=rc, q=q, k=k, v=v, g=g, cq=cq, ckv=ckv, r1=r1,
                o_ret=o_ret, y_ret=y_ret, mqb=mqb, mk=mk, mv=mv, y_mla=y_mla, x1=x1, r2=r2, u=u, uc=uc,
                dx2=dx2, loss=loss, g_fw=g_fw)


def _norm_bwd(dh, xh, r, nw):
    dxn = dh * nw
    return r * (dxn - xh * jnp.mean(dxn * xh, axis=-1, keepdims=True))


def _ordered_after(body, order):
    if order is None:
        return body, [], []
    return (lambda order_ref, *refs: body(*refs)), [pl.BlockSpec(memory_space=pl.ANY)], [order]


def _zero_first(first, *refs):
    @pl.when(first)
    def _():
        for ref in refs:
            ref[...] = jnp.zeros_like(ref)


def _colsum(v):
    return jnp.sum(v, axis=0, keepdims=True)


def _dsilu(g, sg):
    return sg * (1.0 + g * (1.0 - sg))


def _ffn_bwd_call(dx2, u, uc, cw, wdown, wup4, x1, r2, fnw, ts):
    s = dx2.shape[0]
    nt = s // ts
    wsh = W_UP_SHARD
    rev = lambda i: nt - 1 - i

    def body(dx2_ref, u_ref, uc_ref, cw_ref, wd_ref, wup_ref, x_ref, r_ref, fnw_ref,
             du_ref, dx1_ref, dcw_ref, dcb_ref, dfnw_ref, dwd_hbm, carry_ref, dwd_ref, sem):
        i = pl.program_id(0)
        _zero_first(i == 0, carry_ref, dwd_ref, dcw_ref, dcb_ref, dfnw_ref)
        dxb = dx2_ref[...].astype(BF16)
        dh = jnp.zeros((ts, D_MODEL), F32)
        for j in range(2):
            gcols = slice(j * wsh, (j + 1) * wsh)
            vcols = slice(D_FF + j * wsh, D_FF + (j + 1) * wsh)
            gate, val = uc_ref[:, gcols].astype(F32), uc_ref[:, vcols].astype(F32)
            da = _dot_nt(dxb, wd_ref[gcols, :])
            sg = _sigmoid(gate)
            sl = gate * sg
            dwd_ref[gcols, :] += _dot_tn((sl * val).astype(BF16), dxb)
            for d, cols, shard in ((da * val * _dsilu(gate, sg), gcols, j), (da * sl, vcols, 2 + j)):
                d1, d2 = _shifted_up(d, carry_ref[:, cols])
                uv = u_ref[:, cols].astype(F32)
                for t, dt in enumerate((d2, d1, d)):
                    dcw_ref[t:t + 1, cols] += _colsum(dt * uv)
                dcb_ref[:, cols] += _colsum(d)
                w = cw_ref[:, cols]
                du = (w[2:3, :] * d + w[1:2, :] * d1 + w[0:1, :] * d2).astype(BF16)
                du_ref[:, cols] = du
                dh = dh + _dot_nt(du, wup_ref[shard])
                carry_ref[:, cols] = d[0:8, :]
        r = r_ref[...]
        xh = x_ref[...] * r
        dfnw_ref[...] += _colsum(dh * xh)
        dx1_ref[...] = dx2_ref[...] + _norm_bwd(dh, xh, r, fnw_ref[...])

        @pl.when(i == nt - 1)
        def _():
            cp = pltpu.make_async_copy(dwd_ref, dwd_hbm, sem)
            cp.start()
            cp.wait()

    sd = jax.ShapeDtypeStruct
    row = lambda c: pl.BlockSpec((ts, c), lambda i: (rev(i), 0))
    once = lambda shape: pl.BlockSpec(shape, lambda i: (0,) * len(shape), pipeline_mode=pl.Buffered(1))
    return pl.pallas_call(
        body, name="ffn_bwd", grid=(nt,),
        in_specs=[row(D_MODEL), row(F2), row(F2), once((3, F2)), once((D_FF, D_MODEL)), once((4, D_MODEL, wsh)),
                  row(D_MODEL), row(1), once((1, D_MODEL))],
        out_specs=[row(F2), row(D_MODEL), _full((3, F2)), _full((1, F2)), _full((1, D_MODEL)), pl.BlockSpec(memory_space=pl.ANY)],
        out_shape=[sd((s, F2), BF16), sd((s, D_MODEL), F32), sd((3, F2), F32), sd((1, F2), F32), sd((1, D_MODEL), F32),
                   sd((D_FF, D_MODEL), F32)],
        scratch_shapes=[pltpu.VMEM((8, F2), F32), pltpu.VMEM((D_FF, D_MODEL), F32), pltpu.SemaphoreType.DMA],
        compiler_params=_cp("arbitrary", vmem=VMEM_LIMIT_MLP),
    )(dx2, u, uc, cw, wdown, wup4, x1, r2, fnw)


def _shifted_up(d, hal):
    n = d.shape[0]
    row = lax.broadcasted_iota(jnp.int32, hal.shape, 0)
    r1, r2 = pltpu.roll(d, n - 1, 0), pltpu.roll(d, n - 2, 0)
    end1 = jnp.where(row == 7, hal[0:1, :], r1[n - 8:, :])
    end2 = jnp.where(row == 6, hal[0:1, :], jnp.where(row == 7, hal[1:2, :], r2[n - 8:, :]))
    return jnp.concatenate([r1[:n - 8, :], end1], axis=0), jnp.concatenate([r2[:n - 8, :], end2], axis=0)


def _dw_norm_call(x, r, nw, b, ts, tn, name):
    s, n = b.shape
    k = x.shape[1]

    def body(x_ref, r_ref, nw_ref, b_ref, dw_ref):
        _zero_first(pl.program_id(1) == 0, dw_ref)
        h = (x_ref[...] * r_ref[...] * nw_ref[...]).astype(BF16)
        dw_ref[...] += _dot_tn(h, b_ref[...])

    return pl.pallas_call(
        body, name=name, grid=(n // tn, s // ts),
        in_specs=[pl.BlockSpec((ts, k), lambda j, i: (i, 0)), pl.BlockSpec((ts, 1), lambda j, i: (i, 0)),
                  pl.BlockSpec((1, k), lambda j, i: (0, 0)), pl.BlockSpec((ts, tn), lambda j, i: (i, j))],
        out_specs=pl.BlockSpec((None, k, tn), lambda j, i: (j, 0, 0)),
        out_shape=jax.ShapeDtypeStruct((n // tn, k, tn), F32),
        compiler_params=_cp("parallel", "arbitrary"),
    )(x, r, nw, b)


def _out_bwd_call(dx1, yret, ymla, wout, ts, order=None):
    s = dx1.shape[0]

    def body(dx_ref, yr_ref, ym_ref, w_ref, dyr_ref, do_ref, dwo_ref):
        _zero_first(pl.program_id(0) == 0, dwo_ref)
        dxb = dx_ref[...].astype(BF16)
        dmix = _dot_nt(dxb, w_ref[...])
        dyr_ref[...] = dmix[:, :RET_W]
        ym = ym_ref[...]
        lane = lax.broadcasted_iota(jnp.int32, (ts, LANES), 1)
        for p in range(N_HEADS // 2):
            dom = dmix[:, RET_W + p * LANES:RET_W + (p + 1) * LANES]
            prod = dom * ym[:, p * LANES:(p + 1) * LANES].astype(F32)
            for hh in range(2):
                mine = (lane >= HEAD) if hh else (lane < HEAD)
                hi, lo = _hi_lo(jnp.sum(jnp.where(mine, prod, 0.0), axis=1, keepdims=True))
                base = jnp.where(lane < HEAD, pltpu.roll(dom, HEAD, 1) if hh else dom, 0.0)
                do_ref[2 * p + hh] = _lane_pair((ts, LANES), V_AUX, -hi, -lo, base).astype(BF16)
        dwo_ref[0:RET_W, :] += _dot_tn(yr_ref[...], dxb)
        dwo_ref[RET_W:, :] += _dot_tn(ym, dxb)

    sd = jax.ShapeDtypeStruct
    body, first_specs, first = _ordered_after(body, order)
    return pl.pallas_call(
        body, name="out_proj_bwd", grid=(s // ts,),
        in_specs=first_specs + [_row(ts, D_MODEL), _row(ts, RET_W), _row(ts, MLA_W), _full((D_MODEL, D_MODEL))],
        out_specs=[_row(ts, RET_W), _hrow(N_HEADS, ts, LANES), _full((D_MODEL, D_MODEL))],
        out_shape=[sd((s, RET_W), F32), sd((N_HEADS, s, LANES), BF16), sd((D_MODEL, D_MODEL), F32)],
        compiler_params=_cp("arbitrary"),
    )(*first, dx1, yret, ymla, wout)


def _ret_bwd_q_call(q, k, v, o, g, dy, gnw, rc, cos_r, sin_r, tr):
    s = q.shape[0]
    c = RET_CHUNK
    nc = tr // c
    ns = RET_SLABS

    def body(q_ref, k_ref, v_ref, o_ref, g_ref, dy_ref, gnw_ref, dm_ref, zeta_ref, xi_ref, cd_ref, bd_ref, cr_ref, sr_ref,
             dq_ref, dg_ref, do_ref, dgnw_ref, st_ref):
        _zero_first(pl.program_id(1) == 0, st_ref, dgnw_ref)
        bd = bd_ref[...]
        avg = bd * (1.0 / HEAD)
        chunks = [slice(ci * c, (ci + 1) * c) for ci in range(nc)]
        lanes = [slice(sl * LANES, (sl + 1) * LANES) for sl in range(ns)]
        dov = []
        for ln in lanes:
            ov = o_ref[:, ln]
            ctr = ov - _dot_hi(ov, avg)
            rs = lax.rsqrt(_dot_hi(ctr * ctr, avg) + EPS)
            oh = ctr * rs
            gg, dyv, gnw_v = g_ref[:, ln], dy_ref[:, ln], gnw_ref[:, ln]
            sg = _sigmoid(gg)
            sl = gg * sg
            dg_ref[:, ln] = (dyv * oh * gnw_v * _dsilu(gg, sg)).astype(BF16)
            dgnw_ref[:, ln] += _colsum(dyv * sl * oh)
            doh = dyv * sl * gnw_v
            dov.append((rs * (doh - _dot_hi(doh, avg) - oh * _dot_hi(doh * oh, avg))).astype(BF16))
            do_ref[:, ln] = dov[-1]
        states = _ret_states(k_ref, v_ref, zeta_ref, cd_ref, bd, st_ref, chunks, lanes, False)
        for ci, rows in enumerate(chunks):
            for sl, ln in enumerate(lanes):
                doc = dov[sl][rows, :]
                dq = (_dot_nt(doc, states[sl][ci]) * xi_ref[sl]
                      + _pair_product(doc, _stack_heads(v_ref[rows, ln]), dm_ref[sl], _stack_heads(k_ref[rows, ln])))
                dq_ref[rows, ln] = _unrope(dq, cr_ref[rows, :], sr_ref[rows, :], HEAD // 2).astype(BF16)

    specs = _ret_specs(tr, lambda i: i)
    sd = jax.ShapeDtypeStruct
    return pl.pallas_call(
        body, name="ret_bwd_q", grid=(4 // ns, s // tr),
        in_specs=[specs["slab"]] * 6 + [specs["vec"], specs["dmask"], specs["rows"], specs["rows"], specs["state"], specs["bd"],
                                        specs["tab"], specs["tab"]],
        out_specs=[specs["slab"]] * 3 + [specs["vec"]],
        out_shape=[sd((s, RET_W), BF16), sd((s, RET_W), BF16), sd((s, RET_W), BF16), sd((1, RET_W), F32)],
        scratch_shapes=[pltpu.VMEM((ns, LANES, LANES), F32)],
        compiler_params=_cp("parallel", "arbitrary"),
    )(q, k, v, o, g, dy, gnw, rc["dmask"], rc["zeta"], rc["xi"], rc["cd"], rc["bd"], cos_r, sin_r)


def _ret_bwd_kv_call(q, k, v, do, rc, cos_r, sin_r, tr):
    s = q.shape[0]
    c = RET_CHUNK
    nc = tr // c
    nt = s // tr
    ns = RET_SLABS

    def body(q_ref, k_ref, v_ref, do_ref, dm_ref, zeta_ref, xi_ref, cd_ref, bd_ref, cr_ref, sr_ref, dk_ref, dv_ref, gs_ref):
        _zero_first(pl.program_id(1) == 0, gs_ref)
        bd = bd_ref[...]
        chunks = [slice(ci * c, (ci + 1) * c) for ci in range(nc)]
        lanes = [slice(sl * LANES, (sl + 1) * LANES) for sl in range(ns)]
        states = _ret_states(q_ref, do_ref, xi_ref, cd_ref, bd, gs_ref, chunks, lanes, True)
        for ci, rows in enumerate(chunks):
            for sl, ln in enumerate(lanes):
                kc, vc = k_ref[rows, ln], v_ref[rows, ln]
                q2, do2 = _stack_heads(q_ref[rows, ln]), _stack_heads(do_ref[rows, ln])
                gb = states[sl][ci]
                dk = _dot_nt(vc, gb) * zeta_ref[sl] + _pair_product(vc, do2, dm_ref[sl], q2)
                dv = _dot(kc, gb) * zeta_ref[sl] + _pair_product(kc, q2, dm_ref[sl], do2)
                dk_ref[rows, ln] = (_unrope(dk, cr_ref[rows, :], sr_ref[rows, :], HEAD // 2) * (HEAD ** -0.5)).astype(BF16)
                dv_ref[rows, ln] = dv.astype(BF16)

    specs = _ret_specs(tr, lambda i: nt - 1 - i)
    sd = jax.ShapeDtypeStruct
    return pl.pallas_call(
        body, name="ret_bwd_kv", grid=(4 // ns, nt),
        in_specs=[specs["slab"]] * 4 + [specs["dmask"], specs["rows"], specs["rows"], specs["state"], specs["bd"],
                                        specs["tab"], specs["tab"]],
        out_specs=[specs["slab"]] * 2,
        out_shape=[sd((s, RET_W), BF16), sd((s, RET_W), BF16)],
        scratch_shapes=[pltpu.VMEM((ns, LANES, LANES), F32)],
        compiler_params=_cp("parallel", "arbitrary"),
    )(q, k, v, do, rc["dmask_t"], rc["zeta"], rc["xi"], rc["cd"], rc["bd"], cos_r, sin_r)


FLASH_BWD_HEADS = 8


def _flash_bwd_call(qb, k, v, do, tb, order=None):
    s = qb.shape[1]
    nb = s // tb
    hg = FLASH_BWD_HEADS
    pairs = [(a, b) for a in range(nb) for b in range(a, nb)]
    ki_of, qi_of = (jnp.asarray(np.array(col, np.int32)) for col in zip(*pairs))
    extra = [] if order is None else [order]

    def body(ki_ref, qi_ref, *refs):
        q_ref, k_ref, v_ref, do_ref, dk_ref, dv_ref, dq_hbm, dka_ref, dva_ref, dq_ref, sem = refs[len(extra):]
        g, p = pl.program_id(0), pl.program_id(1)
        ki, qi = ki_ref[p], qi_ref[p]
        _zero_first(p == 0, dq_ref)
        _zero_first(qi == ki, dka_ref, dva_ref)
        rows = pl.ds(pl.multiple_of(qi * tb, tb), tb)

        def step(masked):
            if masked:
                keep = lax.broadcasted_iota(jnp.int32, (tb, tb), 0) <= lax.broadcasted_iota(jnp.int32, (tb, tb), 1)
            for h in range(hg):
                st = _dot_nt(k_ref[h], q_ref[h])
                if masked:
                    st = jnp.where(keep, st, NEG)
                pt = jnp.exp2(st)
                dob = do_ref[h]
                dva_ref[h] += _dot(pt.astype(BF16), dob)
                dst = (pt * _dot_nt(v_ref[h], dob)).astype(BF16)
                dka_ref[h] += _dot(dst, q_ref[h])
                dq_ref[h, rows, :] += _dot_tn(dst, k_ref[h])

        @pl.when(qi > ki)
        def _():
            step(False)

        @pl.when(qi == ki)
        def _():
            step(True)

        @pl.when(qi == nb - 1)
        def _():
            dk_ref[...] = (dka_ref[...] * LN2).astype(BF16)
            dv_ref[...] = dva_ref[...].astype(BF16)

        @pl.when(p == len(pairs) - 1)
        def _():
            cp = pltpu.make_async_copy(dq_ref, dq_hbm.at[pl.ds(g * hg, hg)], sem)
            cp.start()
            cp.wait()

    kspec = pl.BlockSpec((hg, tb, LANES), lambda g, p, ki_ref, qi_ref: (g, ki_ref[p], 0))
    qspec = pl.BlockSpec((hg, tb, LANES), lambda g, p, ki_ref, qi_ref: (g, qi_ref[p], 0))
    hm = jax.ShapeDtypeStruct((N_HEADS, s, LANES), BF16)
    return pl.pallas_call(
        body, name="mla_flash_bwd",
        grid_spec=pltpu.PrefetchScalarGridSpec(
            num_scalar_prefetch=2, grid=(N_HEADS // hg, len(pairs)),
            in_specs=[ANY] * len(extra) + [qspec, kspec, kspec, qspec],
            out_specs=[kspec, kspec, ANY],
            scratch_shapes=[pltpu.VMEM((hg, tb, LANES), F32), pltpu.VMEM((hg, tb, LANES), F32),
                            pltpu.VMEM((hg, s, LANES), F32), pltpu.SemaphoreType.DMA]),
        out_shape=[hm, hm, jax.ShapeDtypeStruct((N_HEADS, s, LANES), F32)],
        compiler_params=_cp("arbitrary", "arbitrary"),
    )(ki_of, qi_of, *extra, qb, k, v, do)


def _mla_post_call(dq, dk, dv, cq, ckv, qnw, kvnw, wq, wk, wv, cos_m, sin_m, ts):
    s = cq.shape[0]

    def body(dq_ref, dk_ref, dv_ref, cq_ref, ckv_ref, qnw_ref, kvnw_ref, wq_ref, wk_ref, wv_ref, cm_ref, sm_ref,
             dcq_ref, dckv_ref, dkpe_ref, dwq_ref, dwk_ref, dwv_ref, dqnw_ref, dkvnw_ref):
        _zero_first(pl.program_id(0) == 0, dwq_ref, dwk_ref, dwv_ref, dqnw_ref, dkvnw_ref)
        cqv, ckvv = cq_ref[...], ckv_ref[...]
        rq, rkv = _rstd(cqv), _rstd(ckvv)
        qh_, kvh_ = cqv * rq, ckvv * rkv
        qnw_v, kvnw_v = qnw_ref[...], kvnw_ref[...]
        cqn = (qh_ * qnw_v).astype(BF16)
        ckvn = (kvh_ * kvnw_v).astype(BF16)
        cm, sm = cm_ref[...], sm_ref[...]
        dcqn = jnp.zeros((ts, Q_RANK), F32)
        dckvn = jnp.zeros((ts, KV_RANK), F32)
        dkpe = jnp.zeros((ts, LANES), F32)
        for h in range(N_HEADS):
            dqu = _unrope(dq_ref[h] * SM_SCALE, cm, sm, ROPE // 2).astype(BF16)
            dwq_ref[h] += _dot_tn(cqn, dqu)
            dcqn = dcqn + _dot_nt(dqu, wq_ref[h])
            dkb, dvb = dk_ref[h], dv_ref[h]
            dkpe = dkpe + dkb.astype(F32)
            dwk_ref[h] += _dot_tn(ckvn, dkb)
            dwv_ref[h] += _dot_tn(ckvn, dvb)
            dckvn = dckvn + _dot_nt(dkb, wk_ref[h]) + _dot_nt(dvb, wv_ref[h])
        lane = lax.broadcasted_iota(jnp.int32, (ts, LANES), 1)
        dkpe = jnp.where((lane >= KPE_LO) & (lane < KPE_LO + ROPE), dkpe, 0.0)
        dkpe_ref[...] = _unrope(dkpe, cm, sm, ROPE // 2).astype(BF16)
        dqnw_ref[...] += _colsum(dcqn * qh_)
        dkvnw_ref[...] += _colsum(dckvn * kvh_)
        dcq_ref[...] = _norm_bwd(dcqn, qh_, rq, qnw_v).astype(BF16)
        dckv_ref[...] = _norm_bwd(dckvn, kvh_, rkv, kvnw_v).astype(BF16)

    sd = jax.ShapeDtypeStruct
    hm = _hrow(N_HEADS, ts, LANES)
    return pl.pallas_call(
        body, name="mla_post", grid=(s // ts,),
        in_specs=[hm, hm, hm, _row(ts, Q_RANK), _row(ts, KV_RANK), _full((1, Q_RANK)), _full((1, KV_RANK)),
                  _full((N_HEADS, Q_RANK, LANES)), _full((N_HEADS, KV_RANK, LANES)), _full((N_HEADS, KV_RANK, LANES)),
                  _row(ts, LANES), _row(ts, LANES)],
        out_specs=[_row(ts, Q_RANK), _row(ts, KV_RANK), _row(ts, LANES),
                   _full((N_HEADS, Q_RANK, LANES)), _full((N_HEADS, KV_RANK, LANES)), _full((N_HEADS, KV_RANK, LANES)),
                   _full((1, Q_RANK)), _full((1, KV_RANK))],
        out_shape=[sd((s, Q_RANK), BF16), sd((s, KV_RANK), BF16), sd((s, LANES), BF16),
                   sd((N_HEADS, Q_RANK, LANES), F32), sd((N_HEADS, KV_RANK, LANES), F32), sd((N_HEADS, KV_RANK, LANES), F32),
                   sd((1, Q_RANK), F32), sd((1, KV_RANK), F32)],
        compiler_params=_cp("arbitrary"),
    )(dq, dk, dv, cq, ckv, qnw, kvnw, wq, wk, wv, cos_m, sin_m)


def _in_bwd_call(parts, x, r1, anw, dx1, win, ts):
    s = x.shape[0]
    widths = [p.shape[1] for p in parts]
    np_ = len(parts)

    def body(*refs):
        p_refs = refs[:np_]
        x_ref, r_ref, anw_ref, dx1_ref, w_ref, dx_ref, dw_ref, danw_ref = refs[np_:]
        _zero_first(pl.program_id(0) == 0, dw_ref, danw_ref)
        dproj = jnp.concatenate([p[...] for p in p_refs], axis=-1)
        r, anw_v = r_ref[...], anw_ref[...]
        xh = x_ref[...] * r
        dw_ref[...] += _dot_tn((xh * anw_v).astype(BF16), dproj)
        dh = _dot_nt(dproj, w_ref[...])
        danw_ref[...] += _colsum(dh * xh)
        dx_ref[...] = dx1_ref[...] + _norm_bwd(dh, xh, r, anw_v)

    sd = jax.ShapeDtypeStruct
    return pl.pallas_call(
        body, name="in_proj_bwd", grid=(s // ts,),
        in_specs=[_row(ts, w) for w in widths]
        + [_row(ts, D_MODEL), _row(ts, 1), _full((1, D_MODEL)), _row(ts, D_MODEL), _full((D_MODEL, IN_EXT))],
        out_specs=[_row(ts, D_MODEL), _full((D_MODEL, IN_EXT)), _full((1, D_MODEL))],
        out_shape=[sd((s, D_MODEL), F32), sd((D_MODEL, IN_EXT), F32), sd((1, D_MODEL), F32)],
        compiler_params=_cp("arbitrary"),
    )(*parts, x, r1, anw, dx1, win)


def _local_step(x, positions, tgt, w, small, ex=None):
    s = x.shape[0]
    t = _tiles(s)
    ex = _Exchanges(w) if ex is None else ex
    f = _forward(x, positions, tgt, w, small, ex)
    pw, rc = f["pw"], f["rc"]
    cos_r, sin_r, cos_m, sin_m = f["tabs"]
    dx2, loss, g_fw = f["dx2"], f["loss"], f["g_fw"]
    du, dx1, g_cw, g_cb, g_fnw, g_wd = _ffn_bwd_call(dx2, f["u"], f["uc"], w["conv_w"], pw["wdown"], pw["wup"],
                                                     f["x1"], f["r2"], small["ffn_norm_w"], t["t2"])
    g_wup = _dw_norm_call(f["x1"], f["r2"], small["ffn_norm_w"], du, t["tw"], F2 // 4, "dw_up")
    started = ex.mlp_grads(dict(w_up=g_wup, w_down=g_wd))
    dy_ret, do, g_wout = _out_bwd_call(dx1, f["y_ret"], f["y_mla"], pw["wout"], t["t1"], started)
    started = ex.behind_out_bwd(g_wout)
    drq, dg, do_ret, g_gnw = _ret_bwd_q_call(f["q"], f["k"], f["v"], f["o_ret"], f["g"], dy_ret, small["ret_gn_w"], rc, cos_r, sin_r, t["tr"])
    drk, drv = _ret_bwd_kv_call(f["q"], f["k"], f["v"], do_ret, rc, cos_r, sin_r, t["tr"])
    dmk, dmv, dmq = _flash_bwd_call(f["mqb"], f["mk"], f["mv"], do, t["tb"], started)
    ex.behind_attention(dmk)
    dcq, dckv, dkpe, g_wq, g_wk, g_wv, g_qnw, g_kvnw = _mla_post_call(
        dmq, dmk, dmv, f["cq"], f["ckv"], small["mla_q_norm_w"], small["mla_kv_norm_w"], pw["wq"], pw["wk"], pw["wv"], cos_m, sin_m, t["ts"])
    gx, g_win_ext, g_anw = _in_bwd_call([drq, drk, drv, dg, dcq, dckv, dkpe], x, f["r1"], small["attn_norm_w"], dx1, pw["win"], t["ts"])
    lo = IN_W - ROPE

    def win_cols(a, b):
        parts = ([g_win_ext[:, a:min(b, lo)]] if a < lo else []) + ([g_win_ext[:, max(a, lo) + KPE_LO:b + KPE_LO]] if b > lo else [])
        return jnp.concatenate(parts, -1)

    if w["w_in"].ndim == 3:
        blocks, c = w["w_in"].shape[0], w["w_in"].shape[2]
        g_win = jnp.stack([win_cols(j * c, (j + 1) * c) for j in range(blocks)])
    else:
        g_win = win_cols(0, IN_W)
    g_wuq = g_wq.transpose(1, 0, 2)[:, :, :HEAD + ROPE].reshape(Q_RANK, N_HEADS * (HEAD + ROPE))
    g_wukv = jnp.concatenate([g_wk[:, :, :HEAD], g_wv[:, :, :HEAD]], -1).transpose(1, 0, 2).reshape(KV_RANK, 2 * MLA_W)
    gw = dict(w_in=g_win, w_uq=g_wuq, w_ukv=g_wukv, w_out=g_wout, w_up=g_wup,
              conv_w=g_cw, w_down=g_wd)
    gs = dict(attn_norm_w=g_anw, ret_gn_w=g_gnw, mla_q_norm_w=g_qnw, mla_kv_norm_w=g_kvnw, ffn_norm_w=g_fnw,
              conv_b=g_cb, final_norm_w=g_fw)
    return loss, gx, gw, gs


MESH_ID = pl.DeviceIdType.MESH
ANY = pl.BlockSpec(memory_space=pl.ANY)
VMEM_SPEC = pl.BlockSpec(memory_space=pltpu.VMEM)
N_DEV = 8
GROUP_A = (("w_in", (D_MODEL, IN_W // 4), 1), ("w_uq", (Q_RANK, 192), 1), ("w_ukv", (KV_RANK, 256), 1),
           ("w_out", (D_MODEL // 4, D_MODEL), 0))
GROUP_B = (("w_up", (D_MODEL, F2 // 4), 1), ("w_down", (D_FF // 4, D_MODEL), 0))
HBM_SPEC = pl.BlockSpec(memory_space=pltpu.HBM)
SEM_SPEC = pl.BlockSpec(memory_space=pltpu.SEMAPHORE)


def _mesh_pos():
    return lax.axis_index("x"), lax.axis_index("y"), lax.axis_index("c")


def _other_chips(x, y):
    return [(1 - x, y), (x, 1 - y), (1 - x, 1 - y)]


def _remote(src, dst, send_sems, recv_sems, k, dev):
    return pltpu.make_async_remote_copy(src_ref=src, dst_ref=dst, send_sem=send_sems.at[k], recv_sem=recv_sems.at[k],
                                        device_id=dev, device_id_type=MESH_ID)


def _gather_list_call(parts, tag):
    n = len(parts)

    def body(*refs):
        srcs, outs, (send_sems, recv_sems) = refs[:n], refs[n:2 * n], refs[2 * n:]
        x, y, c = _mesh_pos()
        sm = 2 * x + y
        chips = _other_chips(x, y)
        sib = (x, y, 1 - c)
        rc = lambda k, src, dst, dev: _remote(src, dst, send_sems, recv_sems, k, dev)
        first = [rc(7 * i + j, srcs[i].at[c], outs[i].at[sm, c], (cx, cy, c)) for i in range(n) for j, (cx, cy) in enumerate(chips)]
        own = [rc(7 * i + 6, srcs[i], outs[i].at[sm], sib) for i in range(n)]
        for cp in first + own:
            cp.start()
        passed = []
        for j, (cx, cy) in enumerate(chips):
            for i in range(n):
                land = outs[i].at[2 * cx + cy, c]
                rc(7 * i + j, srcs[i].at[c], land, (cx, cy, c)).wait_recv()
                cp = rc(7 * i + 3 + j, land, land, sib)
                cp.start()
                passed.append(cp)
        for j, (cx, cy) in enumerate(chips):
            for i in range(n):
                rc(7 * i + 3 + j, srcs[i].at[c], outs[i].at[2 * cx + cy, 1 - c], sib).wait_recv()
        for cp in own:
            cp.wait_recv()
        for cp in first + passed + own:
            cp.wait_send()

    return pl.pallas_call(
        body, name="weights_all_gather_" + tag,
        in_specs=[ANY] * n, out_specs=[ANY] * n,
        out_shape=[jax.ShapeDtypeStruct((4,) + p.shape, p.dtype) for p in parts],
        scratch_shapes=[pltpu.SemaphoreType.DMA((7 * n,)), pltpu.SemaphoreType.DMA((7 * n,))],
    )(*parts)


def _direct_gather_copies(srcs, lands, send_sems, recv_sems):
    x, y, c = _mesh_pos()
    sm = 2 * x + y
    sends, recvs = [], []
    for i, (src, land) in enumerate(zip(srcs, lands)):
        for j, (cx, cy) in enumerate(_other_chips(x, y)):
            for t in range(2):
                sends.append(_remote(src.at[c], land.at[sm, c], send_sems, recv_sems, 13 * i + 4 * j + 2 * c + t, (cx, cy, t)))
                recvs.append(_remote(src.at[t], land.at[2 * cx + cy, t], send_sems, recv_sems, 13 * i + 4 * j + 2 * t + c, (cx, cy, t)))
        sends.append(_remote(src, land.at[sm], send_sems, recv_sems, 13 * i + 12, (x, y, 1 - c)))
        recvs.append(_remote(src, land.at[sm], send_sems, recv_sems, 13 * i + 12, (x, y, 1 - c)))
    return sends, recvs


def _sibling_copies(srcs, lands, send_sems, recv_sems):
    x, y, c = _mesh_pos()
    cps = [_remote(src.at[s, 1 - c], land.at[s], send_sems, recv_sems, 4 * i + s, (x, y, 1 - c))
           for i, (src, land) in enumerate(zip(srcs, lands)) for s in range(4)]
    return cps, cps


def _chips_copies(srcs, lands, send_sems, recv_sems):
    x, y, c = _mesh_pos()
    cps = [_remote(src.at[2 * cx + cy], land.at[j], send_sems, recv_sems, 3 * i + j, (cx, cy, c))
           for i, (src, land) in enumerate(zip(srcs, lands)) for j, (cx, cy) in enumerate(_other_chips(x, y))]
    return cps, cps


def _share_copies(srcs, lands, send_sems, recv_sems):
    x, y, c = _mesh_pos()
    cps = [_remote(src, land, send_sems, recv_sems, i, (x, y, 1 - c)) for i, (src, land) in enumerate(zip(srcs, lands))]
    return cps, cps


def _exchange_call(name, copies, srcs, land_shapes, n_sems):
    n = len(srcs)

    def body(*refs):
        sends, recvs = copies(refs[:n], refs[n:2 * n], refs[2 * n], refs[2 * n + 1])
        for cp in sends:
            cp.start()
        for cp in sends:
            cp.wait_send()
        for cp in recvs:
            cp.wait_recv()

    return pl.pallas_call(
        body, name=name, in_specs=[ANY] * n, out_specs=[ANY] * n, out_shape=list(land_shapes),
        scratch_shapes=[pltpu.SemaphoreType.DMA((n_sems,)), pltpu.SemaphoreType.DMA((n_sems,))],
    )(*srcs)


def _exchange_start_call(name, copies, srcs, land_shapes, n_sems, order=None):
    n = len(srcs)
    extra = [] if order is None else [order]
    k = 2 * n + len(extra)

    def body(*refs):
        sends, _ = copies(refs[:n], refs[n:2 * n], refs[k], refs[k + 1])
        for cp in sends:
            cp.start()
        refs[-1][...] = jnp.zeros_like(refs[-1])

    hbm = lambda a: pltpu.with_memory_space_constraint(a, pltpu.HBM)
    lands = [hbm(lax.empty(sd.shape, sd.dtype)) for sd in land_shapes]
    sem = pltpu.SemaphoreType.DMA((n_sems,))
    out = pl.pallas_call(
        body, name=name,
        out_shape=(sem, sem, *[pltpu.HBM(a.shape, a.dtype) for a in list(srcs) + lands], jax.ShapeDtypeStruct((8, LANES), F32)),
        in_specs=[HBM_SPEC] * (2 * n) + [ANY] * len(extra), out_specs=(SEM_SPEC, SEM_SPEC, *[HBM_SPEC] * (2 * n), VMEM_SPEC),
        input_output_aliases={i: 2 + i for i in range(2 * n)},
        compiler_params=pltpu.CompilerParams(has_side_effects=pltpu.SideEffectType.DATAFLOW_SIDE_EFFECTING),
    )(*[hbm(a) for a in srcs], *lands, *extra)
    return out[0], out[1], out[2:2 + n], out[2 + n:2 + 2 * n], out[-1]


def _exchange_wait_call(name, copies, started, after):
    send_sems, recv_sems, srcs, lands, _ = started
    n = len(srcs)

    def body(*refs):
        sends, recvs = copies(refs[:n], refs[n:2 * n], refs[2 * n], refs[2 * n + 1])
        for cp in sends:
            cp.wait_send()
        for cp in recvs:
            cp.wait_recv()

    out = pl.pallas_call(
        body, name=name,
        out_shape=tuple(pltpu.HBM(a.shape, a.dtype) for a in list(srcs) + list(lands)),
        in_specs=[HBM_SPEC] * (2 * n) + [SEM_SPEC, SEM_SPEC, ANY], out_specs=tuple([HBM_SPEC] * (2 * n)),
        input_output_aliases={i: i for i in range(2 * n)},
        compiler_params=pltpu.CompilerParams(has_side_effects=pltpu.SideEffectType.DATAFLOW_SIDE_EFFECTING),
    )(*srcs, *lands, send_sems, recv_sems, after)
    return out[:n], out[n:]


def _rows_tile(rows, width, itemsize=4):
    limit = max(16, (3 << 20) // (width * itemsize))
    if rows <= limit:
        return rows
    return max(t for t in range(16, limit + 1, 16) if rows % t == 0)


def _sum_sibling_call(g, buf, c, name):
    _, _, rh, w = g.shape
    tile = _rows_tile(rh, w)

    def body(c_ref, g_ref, b_ref, p_ref, pb_ref):
        p = g_ref[...] + b_ref[...]
        p_ref[...] = p
        pb_ref[...] = p.astype(BF16)

    blk = pl.BlockSpec((None, tile, w), lambda s, i, c_ref: (s, i, 0))
    return pl.pallas_call(
        body, name=name,
        grid_spec=pltpu.PrefetchScalarGridSpec(
            num_scalar_prefetch=1, grid=(4, rh // tile),
            in_specs=[pl.BlockSpec((None, None, tile, w), lambda s, i, c_ref: (s, c_ref[0], i, 0)), blk],
            out_specs=[blk, blk]),
        out_shape=[jax.ShapeDtypeStruct((4, rh, w), F32), jax.ShapeDtypeStruct((4, rh, w), BF16)],
        compiler_params=_cp("parallel", "parallel"),
    )(c, g, buf)


def _sum_chips_call(p, buf, sm, name):
    _, rh, w = p.shape
    tile = _rows_tile(rh, w)

    def body(sm_ref, p_ref, b_ref, f_ref):
        f_ref[...] = ((p_ref[...] + b_ref[0].astype(F32)) + b_ref[1].astype(F32)) + b_ref[2].astype(F32)

    return pl.pallas_call(
        body, name=name,
        grid_spec=pltpu.PrefetchScalarGridSpec(
            num_scalar_prefetch=1, grid=(rh // tile,),
            in_specs=[pl.BlockSpec((None, tile, w), lambda i, sm_ref: (sm_ref[0], i, 0)),
                      pl.BlockSpec((3, tile, w), lambda i, sm_ref: (0, i, 0))],
            out_specs=pl.BlockSpec((tile, w), lambda i, sm_ref: (i, 0))),
        out_shape=jax.ShapeDtypeStruct((rh, w), F32),
        compiler_params=_cp("parallel"),
    )(sm, p, buf)


def _adamw_halves_call(w, g_mine, g_sib, c, m, v, name):
    r, wd = w.shape
    rh = r // 2
    tile = _rows_tile(rh, wd)
    nt = rh // tile

    def body(c_ref, w_ref, gm_ref, gs_ref, m_ref, v_ref, g_ref, d_ref, nm_ref, nv_ref):
        gv = jnp.where(pl.program_id(0) == c_ref[0], gm_ref[...], gs_ref[...])
        g_ref[...] = gv
        nm = ADAM_B1 * m_ref[...] + (1.0 - ADAM_B1) * gv
        nv = ADAM_B2 * v_ref[...] + (1.0 - ADAM_B2) * jnp.square(gv)
        m_hat = nm / (1.0 - ADAM_B1 ** ADAM_STEP)
        v_hat = nv / (1.0 - ADAM_B2 ** ADAM_STEP)
        d_ref[...] = -ADAM_LR * (m_hat / (jnp.sqrt(v_hat) + ADAM_EPS) + ADAM_WD * w_ref[...])
        nm_ref[...] = nm
        nv_ref[...] = nv

    whole = pl.BlockSpec((tile, wd), lambda h, i, c_ref: (h * nt + i, 0))
    half = pl.BlockSpec((tile, wd), lambda h, i, c_ref: (i, 0))
    sd = jax.ShapeDtypeStruct((r, wd), F32)
    return pl.pallas_call(
        body, name=name,
        grid_spec=pltpu.PrefetchScalarGridSpec(
            num_scalar_prefetch=1, grid=(2, nt),
            in_specs=[whole, half, half, whole, whole], out_specs=[whole] * 4),
        out_shape=[sd, sd, sd, sd],
        compiler_params=_cp("parallel", "parallel"),
    )(c, w, g_mine, g_sib, m, v)


def _all_reduce8_call(vec, name):
    rows = vec.shape[0]

    def body(v_ref, out_ref, slots, send_sems, recv_sems):
        x, y, c = _mesh_pos()
        me = 4 * x + 2 * y + c
        slots[me] = v_ref[...]

        def rcopy(k, to_me):
            bx, by, bc = (k >> 2) & 1, (k >> 1) & 1, k & 1
            px, py, pc = (1 - x if bx else x), (1 - y if by else y), (1 - c if bc else c)
            slot = 4 * px + 2 * py + pc if to_me else me
            return pltpu.make_async_remote_copy(src_ref=v_ref, dst_ref=slots.at[slot], send_sem=send_sems.at[k - 1],
                                                recv_sem=recv_sems.at[k - 1], device_id=(px, py, pc), device_id_type=MESH_ID)

        for k in range(1, N_DEV):
            rcopy(k, False).start()
        for k in range(1, N_DEV):
            rcopy(k, True).wait_recv()
        for k in range(1, N_DEV):
            rcopy(k, False).wait_send()
        tot = slots[0]
        for d in range(1, N_DEV):
            tot = tot + slots[d]
        out_ref[...] = tot

    return pl.pallas_call(
        body, name=name,
        in_specs=[VMEM_SPEC], out_specs=VMEM_SPEC,
        out_shape=jax.ShapeDtypeStruct((rows, LANES), F32),
        scratch_shapes=[pltpu.VMEM((N_DEV, rows, LANES), F32),
                        pltpu.SemaphoreType.DMA((N_DEV - 1,)), pltpu.SemaphoreType.DMA((N_DEV - 1,))],
    )(vec)


def _adamw_call(w, g, m, v, name):
    r, c = w.shape
    rb = r if r <= 256 else (256 if r % 256 == 0 else 352)
    assert r % rb == 0

    def body(w_ref, g_ref, m_ref, v_ref, d_ref, nm_ref, nv_ref):
        gv = g_ref[...]
        nm = ADAM_B1 * m_ref[...] + (1.0 - ADAM_B1) * gv
        nv = ADAM_B2 * v_ref[...] + (1.0 - ADAM_B2) * jnp.square(gv)
        m_hat = nm / (1.0 - ADAM_B1 ** ADAM_STEP)
        v_hat = nv / (1.0 - ADAM_B2 ** ADAM_STEP)
        d_ref[...] = -ADAM_LR * (m_hat / (jnp.sqrt(v_hat) + ADAM_EPS) + ADAM_WD * w_ref[...])
        nm_ref[...] = nm
        nv_ref[...] = nv

    spec = pl.BlockSpec((rb, c), lambda i: (i, 0))
    sd = jax.ShapeDtypeStruct((r, c), F32)
    return pl.pallas_call(
        body, name=name, grid=(r // rb,),
        in_specs=[spec] * 4, out_specs=[spec] * 3, out_shape=[sd, sd, sd],
        compiler_params=_cp("parallel"),
    )(w, g, m, v)


SMALL = (("attn_norm_w", D_MODEL), ("ret_gn_w", RET_W), ("mla_q_norm_w", Q_RANK), ("mla_kv_norm_w", KV_RANK),
         ("ffn_norm_w", D_MODEL), ("conv_b", F2), ("final_norm_w", D_MODEL))
WEIGHT_ORDER = ("attn_norm_w", "w_in", "ret_gn_w", "mla_q_norm_w", "w_uq", "mla_kv_norm_w", "w_ukv", "w_out",
                "ffn_norm_w", "w_up", "conv_w", "conv_b", "w_down", "final_norm_w")


def _pad_rows(flat, rows):
    return jnp.concatenate([flat, jnp.zeros((rows * LANES - flat.shape[0],), flat.dtype)]).reshape(rows, LANES)


def kernel(x, positions, attn_norm_w, w_in, ret_gn_w, mla_q_norm_w, w_uq, mla_kv_norm_w, w_ukv, w_out, ffn_norm_w, w_up, conv_w, conv_b, w_down, final_norm_w, loss_target, m_attn_norm_w, m_w_in, m_ret_gn_w, m_mla_q_norm_w, m_w_uq, m_mla_kv_norm_w, m_w_ukv, m_w_out, m_ffn_norm_w, m_w_up, m_conv_w, m_conv_b, m_w_down, m_final_norm_w, v_attn_norm_w, v_w_in, v_ret_gn_w, v_mla_q_norm_w, v_w_uq, v_mla_kv_norm_w, v_w_ukv, v_w_out, v_ffn_norm_w, v_w_up, v_conv_w, v_conv_b, v_w_down, v_final_norm_w):
    args = dict(locals())
    cx, cy, cc = _mesh_pos()
    sm = 2 * cx + cy

    c_arr, sm_arr = cc.reshape(1).astype(jnp.int32), sm.reshape(1).astype(jnp.int32)
    sds = jax.ShapeDtypeStruct

    def my_shards(group):
        return [args[n][0].astype(BF16).reshape(2, r // 2, c) for n, (r, c), _ in group]

    def full_weights(gathered, group):
        full = {}
        for (n, (r, c), axis), got in zip(group, gathered):
            piece = got.reshape(4, r, c)
            full[n] = piece if n in ("w_up", "w_in") else (piece.transpose(1, 0, 2).reshape(r, 4 * c) if axis == 1 else piece.reshape(4 * r, c))
        return full

    def by_owner(gw, group):
        out = []
        for n, (r, c), axis in group:
            g = gw[n]
            if axis == 1 and g.ndim == 2:
                g = g.reshape(r, 4, c).transpose(1, 0, 2)
            out.append(g.reshape(4, 2, r // 2, c))
        return out

    def sibling_shapes(gs):
        return [sds((4,) + g.shape[2:], F32) for g in gs]

    def chip_sums(gs, bufs, group):
        res = [_sum_sibling_call(g, b, c_arr, "grads_sum_sibling_" + n) for g, b, (n, _, _) in zip(gs, bufs, group)]
        return [p for p, _ in res], [pb for _, pb in res]

    def chips_shapes(pbs):
        return [sds((3,) + pb.shape[1:], BF16) for pb in pbs]

    def totals(ps, lands, group, tag):
        fins = [_sum_chips_call(p, l, sm_arr, "grads_sum_chips_" + n) for p, l, (n, _, _) in zip(ps, lands, group)]
        sibs = _exchange_call("grads_rs_share_" + tag, _share_copies, fins, [sds(f.shape, F32) for f in fins], len(fins))
        return {n: (f, s) for (n, _, _), f, s in zip(group, fins, sibs)}

    class StepExchanges(_Exchanges):
        def __init__(self, order):
            shards = my_shards(GROUP_B)
            self.gather = _exchange_start_call("weights_gather_start_b", _direct_gather_copies, shards,
                                               [sds((4,) + s.shape, BF16) for s in shards], 13 * len(shards), order)
            self.red = None

        def token(self):
            return self.gather[4][0:1, 0:1]

        def mlp_weights(self, after):
            return full_weights(_exchange_wait_call("weights_gather_wait_b", _direct_gather_copies, self.gather, after)[1], GROUP_B)

        def mlp_grads(self, gw):
            gs = by_owner(gw, GROUP_B)
            self.step1 = _exchange_start_call("grads_rs_sibling_start_b", _sibling_copies, gs, sibling_shapes(gs), 4 * len(gs))
            return self.step1[4]

        def behind_out_bwd(self, after):
            gs, bufs = _exchange_wait_call("grads_rs_sibling_wait_b", _sibling_copies, self.step1, after)
            self.ps, pbs = chip_sums(gs, bufs, GROUP_B)
            self.step2 = _exchange_start_call("grads_rs_chips_start_b", _chips_copies, pbs, chips_shapes(pbs), 3 * len(pbs))
            return self.step2[4]

        def behind_attention(self, after):
            _, lands = _exchange_wait_call("grads_rs_chips_wait_b", _chips_copies, self.step2, after)
            self.red = totals(self.ps, lands, GROUP_B, "b")

    gathered = _gather_list_call(my_shards(GROUP_A) + [conv_w[0].reshape(2, 1, 3 * F2 // 8)], "a")
    full = full_weights(gathered[:-1], GROUP_A)
    ex = StepExchanges(gathered[-1])
    full["conv_w"] = gathered[-1].reshape(4, 3, F2 // 4).transpose(1, 0, 2).reshape(3, F2)
    small = {n: args[n].reshape(1, d) for n, d in SMALL}
    small["attn_norm_w"] = small["attn_norm_w"] + ex.token()

    loss, gx, gw, gs = _local_step(x[0], positions[0], loss_target[0], full, small, ex)

    ga = by_owner(gw, GROUP_A)
    bufs = _exchange_call("grads_rs_sibling_a", _sibling_copies, ga, sibling_shapes(ga), 4 * len(ga))
    ps, pbs = chip_sums(ga, bufs, GROUP_A)
    lands = _exchange_call("grads_rs_chips_a", _chips_copies, pbs, chips_shapes(pbs), 3 * len(pbs))
    halves = {**ex.red, **totals(ps, lands, GROUP_A, "a")}

    vec = jnp.concatenate([gs[n].reshape(-1) for n, _ in SMALL] + [gw["conv_w"].reshape(-1), loss.reshape(-1)])
    tot = _all_reduce8_call(_pad_rows(vec, 216), "small_all_reduce").reshape(-1)
    red, off = {}, 0
    for n, d in SMALL:
        red[n] = tot[off:off + d].reshape(1, d)
        off += d
    red["conv_w"] = lax.dynamic_slice(tot[off:off + 3 * F2].reshape(3, F2), (0, sm * (F2 // 4)), (3, F2 // 4))
    loss_tot = tot[off + 3 * F2]

    grads, deltas, new_m, new_v = [], [], [], []
    for n in WEIGHT_ORDER:
        shape = args[n].shape
        two_d = (1, shape[0]) if len(shape) == 1 else shape[-2:]
        wmv = [args[k + n].reshape(two_d) for k in ("", "m_", "v_")]
        if n in halves:
            g, d, nm, nv = _adamw_halves_call(wmv[0], *halves[n], c_arr, wmv[1], wmv[2], "adamw_" + n)
        else:
            g = red[n].reshape(two_d)
            d, nm, nv = _adamw_call(wmv[0], g, wmv[1], wmv[2], "adamw_" + n)
        grads.append(g.reshape(shape))
        deltas.append(d.reshape(shape))
        new_m.append(nm.reshape(shape))
        new_v.append(nv.reshape(shape))
    return (loss_tot, gx[None], *grads, *deltas, *new_m, *new_v)
```

```python
import math

import numpy as np
import jax
import jax.numpy as jnp
from jax import lax
from jax.experimental import pallas as pl
from jax.experimental.pallas import tpu as pltpu

F32 = jnp.float32
BF16 = jnp.bfloat16

D_MODEL = 1024
N_HEADS = 8
HEAD = 64
RET_W = N_HEADS * HEAD
MLA_W = N_HEADS * HEAD
ROPE = 32
Q_RANK = 256
KV_RANK = 128
D_FF = 2816
F2 = 2 * D_FF
IN_W = 4 * RET_W + Q_RANK + KV_RANK + ROPE
IN_EXT = 4 * RET_W + Q_RANK + KV_RANK + 128
KPE_LO = 64
ROPE_BASE = 10000.0
EPS = 1e-6
RET_CHUNK = 256
SM_SCALE = (HEAD + ROPE) ** -0.5
LOG2E = math.log2(math.e)
LN2 = math.log(2.0)
NEG = -1e30
LANES = 128
VMEM_LIMIT = 56 * 1024 * 1024

ADAM_LR = 0.001
ADAM_B1 = 0.9
ADAM_B2 = 0.999
ADAM_EPS = 1e-08
ADAM_WD = 0.01
ADAM_STEP = 10


VMEM_LIMIT_MLP = 60 * 1024 * 1024


def _cp(*sem, vmem=VMEM_LIMIT):
    return pltpu.CompilerParams(dimension_semantics=sem, vmem_limit_bytes=vmem)


def _full(shape):
    n = len(shape)
    return pl.BlockSpec(tuple(shape), lambda *_: (0,) * n)


def _row(ts, c):
    return pl.BlockSpec((ts, c), lambda i: (i, 0))


def _hrow(h, ts, c):
    return pl.BlockSpec((h, ts, c), lambda i: (0, i, 0))


def _dot(a, b):
    return jnp.dot(a, b, preferred_element_type=F32)


def _dot_nt(a, b):
    return lax.dot_general(a, b, (((1,), (1,)), ((), ())), preferred_element_type=F32)


def _dot_tn(a, b):
    return lax.dot_general(a, b, (((0,), (0,)), ((), ())), preferred_element_type=F32)


def _dot_hi(a, b):
    hi = a.astype(BF16)
    lo = (a - hi.astype(F32)).astype(BF16)
    bb = b.astype(BF16)
    return _dot(hi, bb) + _dot(lo, bb)


def _rot_half(x, half):
    w = x.shape[-1]
    lane = lax.broadcasted_iota(jnp.int32, x.shape, x.ndim - 1)
    first = (lane % (2 * half)) < half
    return jnp.where(first, -pltpu.roll(x, w - half, x.ndim - 1), pltpu.roll(x, half, x.ndim - 1))


def _rope(x, cos, sin, half):
    return x * cos + _rot_half(x, half) * sin


def _unrope(dy, cos, sin, half):
    return dy * cos - _rot_half(dy, half) * sin


def _sigmoid(g):
    return 0.5 * jnp.tanh(0.5 * g) + 0.5


def _silu(g):
    return g * _sigmoid(g)


def _rstd(x):
    return lax.rsqrt(jnp.mean(x * x, axis=-1, keepdims=True) + EPS)


def _rope_tables(positions):
    s = positions.shape[0]
    hr, hm = HEAD // 2, ROPE // 2
    pos = positions.astype(F32)[None, :]
    inv_r = ROPE_BASE ** (-jnp.arange(0, HEAD, 2, dtype=F32) / HEAD)
    inv_m = ROPE_BASE ** (-jnp.arange(0, ROPE, 2, dtype=F32) / ROPE)
    ang = jnp.concatenate([inv_r, inv_m])[:, None] * pos
    packed = jnp.concatenate([jnp.cos(ang), jnp.sin(ang), jnp.zeros((LANES - 2 * (hr + hm), s), F32)], 0)
    tx = min(s, 1024)

    def spread(t, lane, pieces, fill):
        out = jnp.full(t.shape, fill, F32)
        for lo, src, width in pieces:
            moved = t if lo == src else pltpu.roll(t, (lo - src) % LANES, 1)
            out = jnp.where((lane >= lo) & (lane < lo + width), moved, out)
        return out

    def body(p_ref, cr_ref, sr_ref, cm_ref, sm_ref):
        t = p_ref[...].T
        lane = lax.broadcasted_iota(jnp.int32, t.shape, 1)
        cr_ref[...] = spread(t, lane, [(j * hr, 0, hr) for j in range(LANES // hr)], 0.0)
        sr_ref[...] = spread(t, lane, [(j * hr, hr + hm, hr) for j in range(LANES // hr)], 0.0)
        cm_ref[...] = spread(t, lane, [(KPE_LO, hr, hm), (KPE_LO + hm, hr, hm)], 1.0)
        sm_ref[...] = spread(t, lane, [(KPE_LO, 2 * hr + hm, hm), (KPE_LO + hm, 2 * hr + hm, hm)], 0.0)

    tab = jax.ShapeDtypeStruct((s, LANES), F32)
    return pl.pallas_call(
        body, name="rope_tables", grid=(s // tx,),
        in_specs=[pl.BlockSpec((LANES, tx), lambda i: (0, i))],
        out_specs=[_row(tx, LANES)] * 4, out_shape=[tab] * 4,
        compiler_params=_cp("parallel"),
    )(packed)


def _ret_consts():
    c = RET_CHUNK
    lg = np.log1p(-np.power(2.0, -5.0 - np.arange(N_HEADS, dtype=np.float64)))
    idx = np.arange(c, dtype=np.float64)
    diff = idx[:, None] - idx[None, :]
    lane_head = np.arange(LANES) // HEAD
    dmask = np.zeros((4, 2, c, c))
    zeta = np.zeros((4, c, LANES))
    xi = np.zeros((4, c, LANES))
    cd = np.zeros((4, LANES, LANES))
    bd = (lane_head[:, None] == lane_head[None, :]).astype(np.float64)
    for j in range(4):
        for hh in range(2):
            dmask[j, hh] = np.where(diff >= 0, np.exp(lg[2 * j + hh] * np.maximum(diff, 0.0)), 0.0)
        lgl = lg[2 * j + lane_head]
        zeta[j] = np.exp(lgl[None, :] * (c - 1.0 - idx[:, None]))
        xi[j] = np.exp(lgl[None, :] * (idx[:, None] + 1.0))
        cd[j] = np.exp(lgl * c)[:, None] * bd
    f = lambda a: jnp.asarray(a, F32)
    side = lambda d: np.concatenate([d[:, 0], d[:, 1]], axis=-1)
    return dict(dmask=f(side(dmask)), dmask_t=f(side(np.swapaxes(dmask, 2, 3))), zeta=f(zeta), xi=f(xi), cd=f(cd), bd=f(bd))


def _f1_call(x, anw, win, qnw, kvnw, wq, wk, wv, cos_r, sin_r, cos_m, sin_m, ts):
    s = x.shape[0]

    def body(x_ref, anw_ref, w_ref, qnw_ref, kvnw_ref, wq_ref, wk_ref, wv_ref, cr_ref, sr_ref, cm_ref, sm_ref,
             q_ref, k_ref, v_ref, g_ref, cq_ref, ckv_ref, mq_ref, mk_ref, mv_ref, r_ref):
        xv = x_ref[...]
        r = _rstd(xv)
        r_ref[...] = r
        h = (xv * r * anw_ref[...]).astype(BF16)
        cr, sr = cr_ref[...], sr_ref[...]
        qk = _dot(h, w_ref[:, 0:2 * RET_W])
        for j in range(4):
            sl = slice(j * LANES, (j + 1) * LANES)
            q_ref[:, sl] = _rope(qk[:, sl], cr, sr, HEAD // 2).astype(BF16)
            kk = qk[:, RET_W + j * LANES:RET_W + (j + 1) * LANES]
            k_ref[:, sl] = (_rope(kk, cr, sr, HEAD // 2) * (HEAD ** -0.5)).astype(BF16)
        v_ref[...] = _dot(h, w_ref[:, 2 * RET_W:3 * RET_W]).astype(BF16)
        g_ref[...] = _dot(h, w_ref[:, 3 * RET_W:4 * RET_W])
        o = 4 * RET_W
        cqv = _dot(h, w_ref[:, o:o + Q_RANK])
        ckvv = _dot(h, w_ref[:, o + Q_RANK:o + Q_RANK + KV_RANK])
        cq_ref[...] = cqv
        ckv_ref[...] = ckvv
        cm, sm = cm_ref[...], sm_ref[...]
        kp = _rope(_dot(h, w_ref[:, o + Q_RANK + KV_RANK:IN_EXT]), cm, sm, ROPE // 2)
        kp = _lane_pair((ts, LANES), QK_AUX, -1.0, -1.0, kp)
        cqn = (cqv * _rstd(cqv) * qnw_ref[...]).astype(BF16)
        ckvn = (ckvv * _rstd(ckvv) * kvnw_ref[...]).astype(BF16)
        for hd in range(N_HEADS):
            qh = _rope(_dot(cqn, wq_ref[hd]), cm, sm, ROPE // 2)
            mq_ref[hd] = (qh * (SM_SCALE * LOG2E)).astype(BF16)
            mk_ref[hd] = (_dot(ckvn, wk_ref[hd]) + kp).astype(BF16)
            mv_ref[hd] = _lane_pair((ts, LANES), V_AUX, 1.0, 1.0, _dot(ckvn, wv_ref[hd])).astype(BF16)

    sd = jax.ShapeDtypeStruct
    hm = sd((N_HEADS, s, LANES), BF16)
    return pl.pallas_call(
        body, name="f1_in_proj", grid=(s // ts,),
        in_specs=[_row(ts, D_MODEL), _full((1, D_MODEL)), _full((D_MODEL, IN_EXT)), _full((1, Q_RANK)), _full((1, KV_RANK)),
                  _full((N_HEADS, Q_RANK, LANES)), _full((N_HEADS, KV_RANK, LANES)), _full((N_HEADS, KV_RANK, LANES)),
                  _row(ts, LANES), _row(ts, LANES), _row(ts, LANES), _row(ts, LANES)],
        out_specs=[_row(ts, RET_W), _row(ts, RET_W), _row(ts, RET_W), _row(ts, RET_W),
                   _row(ts, Q_RANK), _row(ts, KV_RANK)] + [_hrow(N_HEADS, ts, LANES)] * 3 + [_row(ts, 1)],
        out_shape=[sd((s, RET_W), BF16), sd((s, RET_W), BF16), sd((s, RET_W), BF16), sd((s, RET_W), F32),
                   sd((s, Q_RANK), F32), sd((s, KV_RANK), F32), hm, hm, hm, sd((s, 1), F32)],
        compiler_params=_cp("parallel"),
    )(x, anw, win, qnw, kvnw, wq, wk, wv, cos_r, sin_r, cos_m, sin_m)


def _stack_heads(a):
    lo = lax.broadcasted_iota(jnp.int32, a.shape, 1) < HEAD
    zero = jnp.zeros_like(a)
    return jnp.concatenate([jnp.where(lo, a, zero), jnp.where(lo, zero, a)], axis=0)


def _pair_product(a, b2, decay2, w2):
    return _dot((_dot_nt(a, b2) * decay2).astype(BF16), w2)


RET_SLABS = 2


def _ret_specs(tr, tile_of):
    c, ns = RET_CHUNK, RET_SLABS
    return dict(
        slab=pl.BlockSpec((tr, ns * LANES), lambda j, i: (tile_of(i), j)),
        tab=pl.BlockSpec((tr, LANES), lambda j, i: (tile_of(i), 0)),
        vec=pl.BlockSpec((1, ns * LANES), lambda j, i: (0, j)),
        dmask=pl.BlockSpec((ns, c, 2 * c), lambda j, i: (j, 0, 0)),
        rows=pl.BlockSpec((ns, c, LANES), lambda j, i: (j, 0, 0)),
        state=pl.BlockSpec((ns, LANES, LANES), lambda j, i: (j, 0, 0)),
        bd=pl.BlockSpec((LANES, LANES), lambda j, i: (0, 0)))


def _ret_states(a_ref, b_ref, scale_ref, cd_ref, bd, st_ref, chunks, lanes, reverse):
    nc = len(chunks)
    contrib = [[_dot_tn((a_ref[rows, ln].astype(F32) * scale_ref[sl]).astype(BF16), b_ref[rows, ln]) * bd for rows in chunks]
               for sl, ln in enumerate(lanes)]
    states = []
    for sl in range(len(lanes)):
        st, seen = st_ref[sl], [None] * nc
        for ci in (reversed(range(nc)) if reverse else range(nc)):
            seen[ci] = st.astype(BF16)
            st = st * cd_ref[sl] + contrib[sl][ci]
        st_ref[sl] = st
        states.append(seen)
    return states


def _ret_fwd_call(q, k, v, g, gnw, rc, tr):
    s = q.shape[0]
    c = RET_CHUNK
    nc = tr // c
    ns = RET_SLABS

    def body(q_ref, k_ref, v_ref, g_ref, gnw_ref, dm_ref, zeta_ref, xi_ref, cd_ref, bd_ref, o_ref, y_ref, st_ref):
        @pl.when(pl.program_id(1) == 0)
        def _():
            st_ref[...] = jnp.zeros_like(st_ref)

        bd = bd_ref[...]
        chunks = [slice(ci * c, (ci + 1) * c) for ci in range(nc)]
        lanes = [slice(sl * LANES, (sl + 1) * LANES) for sl in range(ns)]
        states = _ret_states(k_ref, v_ref, zeta_ref, cd_ref, bd, st_ref, chunks, lanes, False)
        for ci, rows in enumerate(chunks):
            for sl, ln in enumerate(lanes):
                qc = q_ref[rows, ln]
                o_ref[rows, ln] = (_dot(qc, states[sl][ci]) * xi_ref[sl]
                                   + _pair_product(qc, _stack_heads(k_ref[rows, ln]), dm_ref[sl], _stack_heads(v_ref[rows, ln])))
        avg = bd * (1.0 / HEAD)
        for ln in lanes:
            o = o_ref[:, ln]
            ctr = o - _dot_hi(o, avg)
            var = _dot_hi(ctr * ctr, avg)
            y_ref[:, ln] = (_silu(g_ref[:, ln]) * (ctr * lax.rsqrt(var + EPS) * gnw_ref[:, ln])).astype(BF16)

    specs = _ret_specs(tr, lambda i: i)
    sd = jax.ShapeDtypeStruct
    return pl.pallas_call(
        body, name="ret_fwd", grid=(4 // ns, s // tr),
        in_specs=[specs["slab"]] * 4 + [specs["vec"], specs["dmask"], specs["rows"], specs["rows"], specs["state"], specs["bd"]],
        out_specs=[specs["slab"]] * 2,
        out_shape=[sd((s, RET_W), F32), sd((s, RET_W), BF16)],
        scratch_shapes=[pltpu.VMEM((ns, LANES, LANES), F32)],
        compiler_params=_cp("parallel", "arbitrary"),
    )(q, k, v, g, gnw, rc["dmask"], rc["zeta"], rc["xi"], rc["cd"], rc["bd"])


QK_AUX = HEAD + ROPE
V_AUX = HEAD


def _lane_pair(shape, lo, a, b, rest):
    lane = lax.broadcasted_iota(jnp.int32, shape, len(shape) - 1)
    return jnp.where(lane == lo, a, jnp.where(lane == lo + 1, b, rest))


def _hi_lo(v):
    hi = v.astype(BF16).astype(F32)
    return hi, v - hi


def _flash_fwd_call(q, k, v, tb):
    s = q.shape[1]
    nb = s // tb
    pairs = [(a, b) for a in range(nb) for b in range(a + 1)]
    qi_of, ki_of = (jnp.asarray(np.array(col, np.int32)) for col in zip(*pairs))

    def body(qi_ref, ki_ref, q_ref, k_ref, v_ref, o_ref, qb_ref, m_ref, acc_ref):
        qi, ki = qi_ref[pl.program_id(0)], ki_ref[pl.program_id(0)]

        @pl.when(ki == 0)
        def _():
            m_ref[...] = jnp.full_like(m_ref, NEG)
            acc_ref[...] = jnp.zeros_like(acc_ref)

        def step(masked):
            if masked:
                keep = lax.broadcasted_iota(jnp.int32, (tb, tb), 1) <= lax.broadcasted_iota(jnp.int32, (tb, tb), 0)
            def finish(h, pe, alpha):
                acc_ref[h] = acc_ref[h] * alpha + _dot(pe, v_ref[h])

            nxt, pending = _dot_nt(q_ref[0], k_ref[0]), None
            for h in range(N_HEADS):
                sc = nxt
                if h + 1 < N_HEADS:
                    nxt = _dot_nt(q_ref[h + 1], k_ref[h + 1])
                if masked:
                    sc = jnp.where(keep, sc, NEG)
                m_prev = m_ref[h]
                m_new = jnp.maximum(m_prev, jnp.max(sc, axis=1, keepdims=True))
                pe = jnp.exp2(sc - jnp.tile(m_new, (1, tb // LANES))).astype(BF16)
                m_ref[h] = m_new
                if pending is not None:
                    finish(*pending)
                pending = (h, pe, jnp.exp2(m_prev - m_new))
            finish(*pending)

        @pl.when(ki < qi)
        def _():
            step(False)

        @pl.when(ki == qi)
        def _():
            step(True)
            lane = lax.broadcasted_iota(jnp.int32, (tb, LANES), 1)
            for p in range(N_HEADS // 2):
                outs = []
                for h in (2 * p, 2 * p + 1):
                    acc = acc_ref[h]
                    l = acc[:, V_AUX:V_AUX + 1]
                    outs.append(acc * (1.0 / l))
                    hi, lo = _hi_lo(m_ref[h][:, 0:1] + jnp.log(l) * LOG2E)
                    qb_ref[h] = _lane_pair((tb, LANES), QK_AUX, hi, lo, q_ref[h].astype(F32)).astype(BF16)
                o_ref[:, p * LANES:(p + 1) * LANES] = jnp.where(lane < HEAD, outs[0], pltpu.roll(outs[1], HEAD, 1)).astype(BF16)

    sd = jax.ShapeDtypeStruct
    qspec = pl.BlockSpec((N_HEADS, tb, LANES), lambda p, qi_ref, ki_ref: (0, qi_ref[p], 0))
    kspec = pl.BlockSpec((N_HEADS, tb, LANES), lambda p, qi_ref, ki_ref: (0, ki_ref[p], 0))
    return pl.pallas_call(
        body, name="mla_flash_fwd",
        grid_spec=pltpu.PrefetchScalarGridSpec(
            num_scalar_prefetch=2, grid=(len(pairs),),
            in_specs=[qspec, kspec, kspec],
            out_specs=[pl.BlockSpec((tb, MLA_W), lambda p, qi_ref, ki_ref: (qi_ref[p], 0)), qspec],
            scratch_shapes=[pltpu.VMEM((N_HEADS, tb, LANES), F32), pltpu.VMEM((N_HEADS, tb, LANES), F32)]),
        out_shape=[sd((s, MLA_W), BF16), sd((N_HEADS, s, LANES), BF16)],
        compiler_params=_cp("arbitrary"),
    )(qi_of, ki_of, q, k, v)


def _out_proj_call(x, yret, ymla, wout, ts):
    s = x.shape[0]

    def body(x_ref, yr_ref, ym_ref, w_ref, x1_ref, r_ref):
        x1 = x_ref[...] + _dot(yr_ref[...], w_ref[0:RET_W, :]) + _dot(ym_ref[...], w_ref[RET_W:, :])
        x1_ref[...] = x1
        r_ref[...] = _rstd(x1)

    sd = jax.ShapeDtypeStruct
    return pl.pallas_call(
        body, name="out_proj", grid=(s // ts,),
        in_specs=[_row(ts, D_MODEL), _row(ts, RET_W), _row(ts, MLA_W), _full((D_MODEL, D_MODEL))],
        out_specs=[_row(ts, D_MODEL), _row(ts, 1)],
        out_shape=[sd((s, D_MODEL), F32), sd((s, 1), F32)],
        compiler_params=_cp("parallel"),
    )(x, yret, ymla, wout)


W_UP_SHARD = F2 // 4


def _ffn_fwd_call(x1, r2, fnw, wup4, cw, cb, wdown, tgt, fw, ts):
    s = x1.shape[0]
    wsh = W_UP_SHARD

    def body(x_ref, r_ref, fnw_ref, wup_ref, cw_ref, cb_ref, wd_ref, t_ref, fw_ref,
             u_ref, uc_ref, dx2_ref, loss_ref, gfw_ref, carry_ref):
        _zero_first(pl.program_id(0) == 0, carry_ref, loss_ref, gfw_ref)
        xv = x_ref[...]
        h = (xv * r_ref[...] * fnw_ref[...]).astype(BF16)
        conv = []
        for j in range(4):
            cols = slice(j * wsh, (j + 1) * wsh)
            ub = _dot(h, wup_ref[j]).astype(BF16)
            u_ref[:, cols] = ub
            u = ub.astype(F32)
            u1, u2 = _shifted(u, carry_ref[:, cols])
            w = cw_ref[:, cols]
            cb16 = (cb_ref[:, cols] + w[0:1, :] * u2 + w[1:2, :] * u1 + w[2:3, :] * u).astype(BF16)
            uc_ref[:, cols] = cb16
            conv.append(cb16.astype(F32))
            carry_ref[:, cols] = u[ts - 8:, :]
        acc = xv
        for j in range(2):
            a = (_silu(conv[j]) * conv[j + 2]).astype(BF16)
            acc = acc + _dot(a, wd_ref[j * wsh:(j + 1) * wsh, :])
        r = _rstd(acc)
        xh = acc * r
        fwv = fw_ref[...]
        e = xh * fwv - t_ref[...]
        loss_ref[...] += (0.5 / D_MODEL) * _colsum(jnp.sum(e * e, axis=1, keepdims=True))
        dy = e * (1.0 / D_MODEL)
        gfw_ref[...] += _colsum(dy * xh)
        dx2_ref[...] = _norm_bwd(dy, xh, r, fwv)

    sd = jax.ShapeDtypeStruct
    once = lambda shape: pl.BlockSpec(shape, lambda i: (0,) * len(shape), pipeline_mode=pl.Buffered(1))
    return pl.pallas_call(
        body, name="ffn_fwd_loss", grid=(s // ts,),
        in_specs=[_row(ts, D_MODEL), _row(ts, 1), once((1, D_MODEL)), once((4, D_MODEL, wsh)),
                  once((3, F2)), once((1, F2)), once((D_FF, D_MODEL)), _row(ts, D_MODEL), once((1, D_MODEL))],
        out_specs=[_row(ts, F2), _row(ts, F2), _row(ts, D_MODEL), _full((1, 1)), _full((1, D_MODEL))],
        out_shape=[sd((s, F2), BF16), sd((s, F2), BF16), sd((s, D_MODEL), F32), sd((1, 1), F32), sd((1, D_MODEL), F32)],
        scratch_shapes=[pltpu.VMEM((8, F2), F32)],
        compiler_params=_cp("arbitrary", vmem=VMEM_LIMIT_MLP),
    )(x1, r2, fnw, wup4, cw, cb, wdown, tgt, fw)


def _shifted(u, hal):
    row = lax.broadcasted_iota(jnp.int32, hal.shape, 0)
    r1, r2 = pltpu.roll(u, 1, 0), pltpu.roll(u, 2, 0)
    top1 = jnp.where(row == 0, hal[7:8, :], r1[0:8, :])
    top2 = jnp.where(row == 0, hal[6:7, :], jnp.where(row == 1, hal[7:8, :], r2[0:8, :]))
    return jnp.concatenate([top1, r1[8:, :]], axis=0), jnp.concatenate([top2, r2[8:, :]], axis=0)


def _prep_weights(w):
    win = w["w_in"]
    pad = lambda n: jnp.zeros((D_MODEL, n), win.dtype)
    left, right, at = [], [], IN_W - ROPE
    for j, blk in enumerate([win] if win.ndim == 2 else [win[j] for j in range(win.shape[0])]):
        cut = min(max(at - j * blk.shape[1], 0), blk.shape[1])
        left += [blk[:, :cut]] if cut else []
        right += [blk[:, cut:]] if cut < blk.shape[1] else []
    win_ext = jnp.concatenate(left + [pad(KPE_LO)] + right + [pad(LANES - KPE_LO - ROPE)], -1)
    wuq = w["w_uq"].reshape(Q_RANK, N_HEADS, HEAD + ROPE)
    wq = jnp.concatenate([wuq, jnp.zeros((Q_RANK, N_HEADS, LANES - HEAD - ROPE), wuq.dtype)], -1).transpose(1, 0, 2)
    wukv = w["w_ukv"].reshape(KV_RANK, N_HEADS, 2 * HEAD)
    zk = jnp.zeros((KV_RANK, N_HEADS, HEAD), wukv.dtype)
    wk = jnp.concatenate([wukv[:, :, :HEAD], zk], -1).transpose(1, 0, 2)
    wv = jnp.concatenate([wukv[:, :, HEAD:], zk], -1).transpose(1, 0, 2)
    c = lambda a: a.astype(BF16)
    return dict(win=c(win_ext), wq=c(wq), wk=c(wk), wv=c(wv), wout=c(w["w_out"]))


def _prep_mlp_weights(w):
    wup = w["w_up"]
    if wup.ndim == 2:
        wup = wup.reshape(D_MODEL, 4, W_UP_SHARD).transpose(1, 0, 2)
    return dict(wup=wup.astype(BF16), wdown=w["w_down"].astype(BF16))


def _tiles(s):
    return dict(ts=min(s, 512), tr=min(s, 2048), tbf=min(s, 1024), tb=min(s, 512), t2=min(s, 256),
                tw=min(s, 2048), t1=min(s, 1024))


class _Exchanges:
    def __init__(self, w):
        self.w = w

    def mlp_weights(self, after):
        return self.w

    def mlp_grads(self, gw):
        pass

    def behind_out_bwd(self, after):
        pass

    def behind_attention(self, after):
        pass


def _forward(x, positions, tgt, w, small, ex):
    s = x.shape[0]
    t = _tiles(s)
    pw = _prep_weights(w)
    cos_r, sin_r, cos_m, sin_m = _rope_tables(positions)
    rc = _ret_consts()
    q, k, v, g, cq, ckv, mq, mk, mv, r1 = _f1_call(
        x, small["attn_norm_w"], pw["win"], small["mla_q_norm_w"], small["mla_kv_norm_w"], pw["wq"], pw["wk"], pw["wv"],
        cos_r, sin_r, cos_m, sin_m, t["ts"])
    o_ret, y_ret = _ret_fwd_call(q, k, v, g, small["ret_gn_w"], rc, t["tr"])
    y_mla, mqb = _flash_fwd_call(mq, mk, mv, t["tbf"])
    x1, r2 = _out_proj_call(x, y_ret, y_mla, pw["wout"], t["ts"])
    pw.update(_prep_mlp_weights(ex.mlp_weights(r2)))
    u, uc, dx2, loss, g_fw = _ffn_fwd_call(x1, r2, small["ffn_norm_w"], pw["wup"], w["conv_w"], small["conv_b"], pw["wdown"],
                                           tgt, small["final_norm_w"], t["ts"])
    return dict(pw=pw, tabs=(cos_r, sin_r, cos_m, sin_m), rc=rc, q=q, k=k, v=v, g=g, cq=cq, ckv=ckv, r1=r1,
                o_ret=o_ret, y_ret=y_ret, mqb=mqb, mk=mk, mv=mv, y_mla=y_mla, x1=x1, r2=r2, u=u, uc=uc,
                dx2=dx2, loss=loss, g_fw=g_fw)


def _norm_bwd(dh, xh, r, nw):
    dxn = dh * nw
    return r * (dxn - xh * jnp.mean(dxn * xh, axis=-1, keepdims=True))


def _ordered_after(body, order):
    if order is None:
        return body, [], []
    return (lambda order_ref, *refs: body(*refs)), [pl.BlockSpec(memory_space=pl.ANY)], [order]


def _zero_first(first, *refs):
    @pl.when(first)
    def _():
        for ref in refs:
            ref[...] = jnp.zeros_like(ref)


def _colsum(v):
    return jnp.sum(v, axis=0, keepdims=True)


def _dsilu(g, sg):
    return sg * (1.0 + g * (1.0 - sg))


def _ffn_bwd_call(dx2, u, uc, cw, wdown, wup4, x1, r2, fnw, ts):
    s = dx2.shape[0]
    nt = s // ts
    wsh = W_UP_SHARD
    rev = lambda i: nt - 1 - i

    def body(dx2_ref, u_ref, uc_ref, cw_ref, wd_ref, wup_ref, x_ref, r_ref, fnw_ref,
             du_ref, dx1_ref, dcw_ref, dcb_ref, dfnw_ref, dwd_hbm, carry_ref, dwd_ref, sem):
        i = pl.program_id(0)
        _zero_first(i == 0, carry_ref, dwd_ref, dcw_ref, dcb_ref, dfnw_ref)
        dxb = dx2_ref[...].astype(BF16)
        dh = jnp.zeros((ts, D_MODEL), F32)
        for j in range(2):
            gcols = slice(j * wsh, (j + 1) * wsh)
            vcols = slice(D_FF + j * wsh, D_FF + (j + 1) * wsh)
            gate, val = uc_ref[:, gcols].astype(F32), uc_ref[:, vcols].astype(F32)
            da = _dot_nt(dxb, wd_ref[gcols, :])
            sg = _sigmoid(gate)
            sl = gate * sg
            dwd_ref[gcols, :] += _dot_tn((sl * val).astype(BF16), dxb)
            for d, cols, shard in ((da * val * _dsilu(gate, sg), gcols, j), (da * sl, vcols, 2 + j)):
                d1, d2 = _shifted_up(d, carry_ref[:, cols])
                uv = u_ref[:, cols].astype(F32)
                for t, dt in enumerate((d2, d1, d)):
                    dcw_ref[t:t + 1, cols] += _colsum(dt * uv)
                dcb_ref[:, cols] += _colsum(d)
                w = cw_ref[:, cols]
                du = (w[2:3, :] * d + w[1:2, :] * d1 + w[0:1, :] * d2).astype(BF16)
                du_ref[:, cols] = du
                dh = dh + _dot_nt(du, wup_ref[shard])
                carry_ref[:, cols] = d[0:8, :]
        r = r_ref[...]
        xh = x_ref[...] * r
        dfnw_ref[...] += _colsum(dh * xh)
        dx1_ref[...] = dx2_ref[...] + _norm_bwd(dh, xh, r, fnw_ref[...])

        @pl.when(i == nt - 1)
        def _():
            cp = pltpu.make_async_copy(dwd_ref, dwd_hbm, sem)
            cp.start()
            cp.wait()

    sd = jax.ShapeDtypeStruct
    row = lambda c: pl.BlockSpec((ts, c), lambda i: (rev(i), 0))
    once = lambda shape: pl.BlockSpec(shape, lambda i: (0,) * len(shape), pipeline_mode=pl.Buffered(1))
    return pl.pallas_call(
        body, name="ffn_bwd", grid=(nt,),
        in_specs=[row(D_MODEL), row(F2), row(F2), once((3, F2)), once((D_FF, D_MODEL)), once((4, D_MODEL, wsh)),
                  row(D_MODEL), row(1), once((1, D_MODEL))],
        out_specs=[row(F2), row(D_MODEL), _full((3, F2)), _full((1, F2)), _full((1, D_MODEL)), pl.BlockSpec(memory_space=pl.ANY)],
        out_shape=[sd((s, F2), BF16), sd((s, D_MODEL), F32), sd((3, F2), F32), sd((1, F2), F32), sd((1, D_MODEL), F32),
                   sd((D_FF, D_MODEL), F32)],
        scratch_shapes=[pltpu.VMEM((8, F2), F32), pltpu.VMEM((D_FF, D_MODEL), F32), pltpu.SemaphoreType.DMA],
        compiler_params=_cp("arbitrary", vmem=VMEM_LIMIT_MLP),
    )(dx2, u, uc, cw, wdown, wup4, x1, r2, fnw)


def _shifted_up(d, hal):
    n = d.shape[0]
    row = lax.broadcasted_iota(jnp.int32, hal.shape, 0)
    r1, r2 = pltpu.roll(d, n - 1, 0), pltpu.roll(d, n - 2, 0)
    end1 = jnp.where(row == 7, hal[0:1, :], r1[n - 8:, :])
    end2 = jnp.where(row == 6, hal[0:1, :], jnp.where(row == 7, hal[1:2, :], r2[n - 8:, :]))
    return jnp.concatenate([r1[:n - 8, :], end1], axis=0), jnp.concatenate([r2[:n - 8, :], end2], axis=0)


def _dw_norm_call(x, r, nw, b, ts, tn, name):
    s, n = b.shape
    k = x.shape[1]

    def body(x_ref, r_ref, nw_ref, b_ref, dw_ref):
        _zero_first(pl.program_id(1) == 0, dw_ref)
        h = (x_ref[...] * r_ref[...] * nw_ref[...]).astype(BF16)
        dw_ref[...] += _dot_tn(h, b_ref[...])

    return pl.pallas_call(
        body, name=name, grid=(n // tn, s // ts),
        in_specs=[pl.BlockSpec((ts, k), lambda j, i: (i, 0)), pl.BlockSpec((ts, 1), lambda j, i: (i, 0)),
                  pl.BlockSpec((1, k), lambda j, i: (0, 0)), pl.BlockSpec((ts, tn), lambda j, i: (i, j))],
        out_specs=pl.BlockSpec((None, k, tn), lambda j, i: (j, 0, 0)),
        out_shape=jax.ShapeDtypeStruct((n // tn, k, tn), F32),
        compiler_params=_cp("parallel", "arbitrary"),
    )(x, r, nw, b)


def _out_bwd_call(dx1, yret, ymla, wout, ts, order=None):
    s = dx1.shape[0]

    def body(dx_ref, yr_ref, ym_ref, w_ref, dyr_ref, do_ref, dwo_ref):
        _zero_first(pl.program_id(0) == 0, dwo_ref)
        dxb = dx_ref[...].astype(BF16)
        dmix = _dot_nt(dxb, w_ref[...])
        dyr_ref[...] = dmix[:, :RET_W]
        ym = ym_ref[...]
        lane = lax.broadcasted_iota(jnp.int32, (ts, LANES), 1)
        for p in range(N_HEADS // 2):
            dom = dmix[:, RET_W + p * LANES:RET_W + (p + 1) * LANES]
            prod = dom * ym[:, p * LANES:(p + 1) * LANES].astype(F32)
            for hh in range(2):
                mine = (lane >= HEAD) if hh else (lane < HEAD)
                hi, lo = _hi_lo(jnp.sum(jnp.where(mine, prod, 0.0), axis=1, keepdims=True))
                base = jnp.where(lane < HEAD, pltpu.roll(dom, HEAD, 1) if hh else dom, 0.0)
                do_ref[2 * p + hh] = _lane_pair((ts, LANES), V_AUX, -hi, -lo, base).astype(BF16)
        dwo_ref[0:RET_W, :] += _dot_tn(yr_ref[...], dxb)
        dwo_ref[RET_W:, :] += _dot_tn(ym, dxb)

    sd = jax.ShapeDtypeStruct
    body, first_specs, first = _ordered_after(body, order)
    return pl.pallas_call(
        body, name="out_proj_bwd", grid=(s // ts,),
        in_specs=first_specs + [_row(ts, D_MODEL), _row(ts, RET_W), _row(ts, MLA_W), _full((D_MODEL, D_MODEL))],
        out_specs=[_row(ts, RET_W), _hrow(N_HEADS, ts, LANES), _full((D_MODEL, D_MODEL))],
        out_shape=[sd((s, RET_W), F32), sd((N_HEADS, s, LANES), BF16), sd((D_MODEL, D_MODEL), F32)],
        compiler_params=_cp("arbitrary"),
    )(*first, dx1, yret, ymla, wout)


def _ret_bwd_q_call(q, k, v, o, g, dy, gnw, rc, cos_r, sin_r, tr):
    s = q.shape[0]
    c = RET_CHUNK
    nc = tr // c
    ns = RET_SLABS

    def body(q_ref, k_ref, v_ref, o_ref, g_ref, dy_ref, gnw_ref, dm_ref, zeta_ref, xi_ref, cd_ref, bd_ref, cr_ref, sr_ref,
             dq_ref, dg_ref, do_ref, dgnw_ref, st_ref):
        _zero_first(pl.program_id(1) == 0, st_ref, dgnw_ref)
        bd = bd_ref[...]
        avg = bd * (1.0 / HEAD)
        chunks = [slice(ci * c, (ci + 1) * c) for ci in range(nc)]
        lanes = [slice(sl * LANES, (sl + 1) * LANES) for sl in range(ns)]
        dov = []
        for ln in lanes:
            ov = o_ref[:, ln]
            ctr = ov - _dot_hi(ov, avg)
            rs = lax.rsqrt(_dot_hi(ctr * ctr, avg) + EPS)
            oh = ctr * rs
            gg, dyv, gnw_v = g_ref[:, ln], dy_ref[:, ln], gnw_ref[:, ln]
            sg = _sigmoid(gg)
            sl = gg * sg
            dg_ref[:, ln] = (dyv * oh * gnw_v * _dsilu(gg, sg)).astype(BF16)
            dgnw_ref[:, ln] += _colsum(dyv * sl * oh)
            doh = dyv * sl * gnw_v
            dov.append((rs * (doh - _dot_hi(doh, avg) - oh * _dot_hi(doh * oh, avg))).astype(BF16))
            do_ref[:, ln] = dov[-1]
        states = _ret_states(k_ref, v_ref, zeta_ref, cd_ref, bd, st_ref, chunks, lanes, False)
        for ci, rows in enumerate(chunks):
            for sl, ln in enumerate(lanes):
                doc = dov[sl][rows, :]
                dq = (_dot_nt(doc, states[sl][ci]) * xi_ref[sl]
                      + _pair_product(doc, _stack_heads(v_ref[rows, ln]), dm_ref[sl], _stack_heads(k_ref[rows, ln])))
                dq_ref[rows, ln] = _unrope(dq, cr_ref[rows, :], sr_ref[rows, :], HEAD // 2).astype(BF16)

    specs = _ret_specs(tr, lambda i: i)
    sd = jax.ShapeDtypeStruct
    return pl.pallas_call(
        body, name="ret_bwd_q", grid=(4 // ns, s // tr),
        in_specs=[specs["slab"]] * 6 + [specs["vec"], specs["dmask"], specs["rows"], specs["rows"], specs["state"], specs["bd"],
                                        specs["tab"], specs["tab"]],
        out_specs=[specs["slab"]] * 3 + [specs["vec"]],
        out_shape=[sd((s, RET_W), BF16), sd((s, RET_W), BF16), sd((s, RET_W), BF16), sd((1, RET_W), F32)],
        scratch_shapes=[pltpu.VMEM((ns, LANES, LANES), F32)],
        compiler_params=_cp("parallel", "arbitrary"),
    )(q, k, v, o, g, dy, gnw, rc["dmask"], rc["zeta"], rc["xi"], rc["cd"], rc["bd"], cos_r, sin_r)


def _ret_bwd_kv_call(q, k, v, do, rc, cos_r, sin_r, tr):
    s = q.shape[0]
    c = RET_CHUNK
    nc = tr // c
    nt = s // tr
    ns = RET_SLABS

    def body(q_ref, k_ref, v_ref, do_ref, dm_ref, zeta_ref, xi_ref, cd_ref, bd_ref, cr_ref, sr_ref, dk_ref, dv_ref, gs_ref):
        _zero_first(pl.program_id(1) == 0, gs_ref)
        bd = bd_ref[...]
        chunks = [slice(ci * c, (ci + 1) * c) for ci in range(nc)]
        lanes = [slice(sl * LANES, (sl + 1) * LANES) for sl in range(ns)]
        states = _ret_states(q_ref, do_ref, xi_ref, cd_ref, bd, gs_ref, chunks, lanes, True)
        for ci, rows in enumerate(chunks):
            for sl, ln in enumerate(lanes):
                kc, vc = k_ref[rows, ln], v_ref[rows, ln]
                q2, do2 = _stack_heads(q_ref[rows, ln]), _stack_heads(do_ref[rows, ln])
                gb = states[sl][ci]
                dk = _dot_nt(vc, gb) * zeta_ref[sl] + _pair_product(vc, do2, dm_ref[sl], q2)
                dv = _dot(kc, gb) * zeta_ref[sl] + _pair_product(kc, q2, dm_ref[sl], do2)
                dk_ref[rows, ln] = (_unrope(dk, cr_ref[rows, :], sr_ref[rows, :], HEAD // 2) * (HEAD ** -0.5)).astype(BF16)
                dv_ref[rows, ln] = dv.astype(BF16)

    specs = _ret_specs(tr, lambda i: nt - 1 - i)
    sd = jax.ShapeDtypeStruct
    return pl.pallas_call(
        body, name="ret_bwd_kv", grid=(4 // ns, nt),
        in_specs=[specs["slab"]] * 4 + [specs["dmask"], specs["rows"], specs["rows"], specs["state"], specs["bd"],
                                        specs["tab"], specs["tab"]],
        out_specs=[specs["slab"]] * 2,
        out_shape=[sd((s, RET_W), BF16), sd((s, RET_W), BF16)],
        scratch_shapes=[pltpu.VMEM((ns, LANES, LANES), F32)],
        compiler_params=_cp("parallel", "arbitrary"),
    )(q, k, v, do, rc["dmask_t"], rc["zeta"], rc["xi"], rc["cd"], rc["bd"], cos_r, sin_r)


FLASH_BWD_HEADS = 8


def _flash_bwd_call(qb, k, v, do, tb, order=None):
    s = qb.shape[1]
    nb = s // tb
    hg = FLASH_BWD_HEADS
    pairs = [(a, b) for a in range(nb) for b in range(a, nb)]
    ki_of, qi_of = (jnp.asarray(np.array(col, np.int32)) for col in zip(*pairs))
    extra = [] if order is None else [order]

    def body(ki_ref, qi_ref, *refs):
        q_ref, k_ref, v_ref, do_ref, dk_ref, dv_ref, dq_hbm, dka_ref, dva_ref, dq_ref, sem = refs[len(extra):]
        g, p = pl.program_id(0), pl.program_id(1)
        ki, qi = ki_ref[p], qi_ref[p]
        _zero_first(p == 0, dq_ref)
        _zero_first(qi == ki, dka_ref, dva_ref)
        rows = pl.ds(pl.multiple_of(qi * tb, tb), tb)

        def step(masked):
            if masked:
                keep = lax.broadcasted_iota(jnp.int32, (tb, tb), 0) <= lax.broadcasted_iota(jnp.int32, (tb, tb), 1)
            for h in range(hg):
                st = _dot_nt(k_ref[h], q_ref[h])
                if masked:
                    st = jnp.where(keep, st, NEG)
                pt = jnp.exp2(st)
                dob = do_ref[h]
                dva_ref[h] += _dot(pt.astype(BF16), dob)
                dst = (pt * _dot_nt(v_ref[h], dob)).astype(BF16)
                dka_ref[h] += _dot(dst, q_ref[h])
                dq_ref[h, rows, :] += _dot_tn(dst, k_ref[h])

        @pl.when(qi > ki)
        def _():
            step(False)

        @pl.when(qi == ki)
        def _():
            step(True)

        @pl.when(qi == nb - 1)
        def _():
            dk_ref[...] = (dka_ref[...] * LN2).astype(BF16)
            dv_ref[...] = dva_ref[...].astype(BF16)

        @pl.when(p == len(pairs) - 1)
        def _():
            cp = pltpu.make_async_copy(dq_ref, dq_hbm.at[pl.ds(g * hg, hg)], sem)
            cp.start()
            cp.wait()

    kspec = pl.BlockSpec((hg, tb, LANES), lambda g, p, ki_ref, qi_ref: (g, ki_ref[p], 0))
    qspec = pl.BlockSpec((hg, tb, LANES), lambda g, p, ki_ref, qi_ref: (g, qi_ref[p], 0))
    hm = jax.ShapeDtypeStruct((N_HEADS, s, LANES), BF16)
    return pl.pallas_call(
        body, name="mla_flash_bwd",
        grid_spec=pltpu.PrefetchScalarGridSpec(
            num_scalar_prefetch=2, grid=(N_HEADS // hg, len(pairs)),
            in_specs=[ANY] * len(extra) + [qspec, kspec, kspec, qspec],
            out_specs=[kspec, kspec, ANY],
            scratch_shapes=[pltpu.VMEM((hg, tb, LANES), F32), pltpu.VMEM((hg, tb, LANES), F32),
                            pltpu.VMEM((hg, s, LANES), F32), pltpu.SemaphoreType.DMA]),
        out_shape=[hm, hm, jax.ShapeDtypeStruct((N_HEADS, s, LANES), F32)],
        compiler_params=_cp("arbitrary", "arbitrary"),
    )(ki_of, qi_of, *extra, qb, k, v, do)


def _mla_post_call(dq, dk, dv, cq, ckv, qnw, kvnw, wq, wk, wv, cos_m, sin_m, ts):
    s = cq.shape[0]

    def body(dq_ref, dk_ref, dv_ref, cq_ref, ckv_ref, qnw_ref, kvnw_ref, wq_ref, wk_ref, wv_ref, cm_ref, sm_ref,
             dcq_ref, dckv_ref, dkpe_ref, dwq_ref, dwk_ref, dwv_ref, dqnw_ref, dkvnw_ref):
        _zero_first(pl.program_id(0) == 0, dwq_ref, dwk_ref, dwv_ref, dqnw_ref, dkvnw_ref)
        cqv, ckvv = cq_ref[...], ckv_ref[...]
        rq, rkv = _rstd(cqv), _rstd(ckvv)
        qh_, kvh_ = cqv * rq, ckvv * rkv
        qnw_v, kvnw_v = qnw_ref[...], kvnw_ref[...]
        cqn = (qh_ * qnw_v).astype(BF16)
        ckvn = (kvh_ * kvnw_v).astype(BF16)
        cm, sm = cm_ref[...], sm_ref[...]
        dcqn = jnp.zeros((ts, Q_RANK), F32)
        dckvn = jnp.zeros((ts, KV_RANK), F32)
        dkpe = jnp.zeros((ts, LANES), F32)
        for h in range(N_HEADS):
            dqu = _unrope(dq_ref[h] * SM_SCALE, cm, sm, ROPE // 2).astype(BF16)
            dwq_ref[h] += _dot_tn(cqn, dqu)
            dcqn = dcqn + _dot_nt(dqu, wq_ref[h])
            dkb, dvb = dk_ref[h], dv_ref[h]
            dkpe = dkpe + dkb.astype(F32)
            dwk_ref[h] += _dot_tn(ckvn, dkb)
            dwv_ref[h] += _dot_tn(ckvn, dvb)
            dckvn = dckvn + _dot_nt(dkb, wk_ref[h]) + _dot_nt(dvb, wv_ref[h])
        lane = lax.broadcasted_iota(jnp.int32, (ts, LANES), 1)
        dkpe = jnp.where((lane >= KPE_LO) & (lane < KPE_LO + ROPE), dkpe, 0.0)
        dkpe_ref[...] = _unrope(dkpe, cm, sm, ROPE // 2).astype(BF16)
        dqnw_ref[...] += _colsum(dcqn * qh_)
        dkvnw_ref[...] += _colsum(dckvn * kvh_)
        dcq_ref[...] = _norm_bwd(dcqn, qh_, rq, qnw_v).astype(BF16)
        dckv_ref[...] = _norm_bwd(dckvn, kvh_, rkv, kvnw_v).astype(BF16)

    sd = jax.ShapeDtypeStruct
    hm = _hrow(N_HEADS, ts, LANES)
    return pl.pallas_call(
        body, name="mla_post", grid=(s // ts,),
        in_specs=[hm, hm, hm, _row(ts, Q_RANK), _row(ts, KV_RANK), _full((1, Q_RANK)), _full((1, KV_RANK)),
                  _full((N_HEADS, Q_RANK, LANES)), _full((N_HEADS, KV_RANK, LANES)), _full((N_HEADS, KV_RANK, LANES)),
                  _row(ts, LANES), _row(ts, LANES)],
        out_specs=[_row(ts, Q_RANK), _row(ts, KV_RANK), _row(ts, LANES),
                   _full((N_HEADS, Q_RANK, LANES)), _full((N_HEADS, KV_RANK, LANES)), _full((N_HEADS, KV_RANK, LANES)),
                   _full((1, Q_RANK)), _full((1, KV_RANK))],
        out_shape=[sd((s, Q_RANK), BF16), sd((s, KV_RANK), BF16), sd((s, LANES), BF16),
                   sd((N_HEADS, Q_RANK, LANES), F32), sd((N_HEADS, KV_RANK, LANES), F32), sd((N_HEADS, KV_RANK, LANES), F32),
                   sd((1, Q_RANK), F32), sd((1, KV_RANK), F32)],
        compiler_params=_cp("arbitrary"),
    )(dq, dk, dv, cq, ckv, qnw, kvnw, wq, wk, wv, cos_m, sin_m)


def _in_bwd_call(parts, x, r1, anw, dx1, win, ts):
    s = x.shape[0]
    widths = [p.shape[1] for p in parts]
    np_ = len(parts)

    def body(*refs):
        p_refs = refs[:np_]
        x_ref, r_ref, anw_ref, dx1_ref, w_ref, dx_ref, dw_ref, danw_ref = refs[np_:]
        _zero_first(pl.program_id(0) == 0, dw_ref, danw_ref)
        dproj = jnp.concatenate([p[...] for p in p_refs], axis=-1)
        r, anw_v = r_ref[...], anw_ref[...]
        xh = x_ref[...] * r
        dw_ref[...] += _dot_tn((xh * anw_v).astype(BF16), dproj)
        dh = _dot_nt(dproj, w_ref[...])
        danw_ref[...] += _colsum(dh * xh)
        dx_ref[...] = dx1_ref[...] + _norm_bwd(dh, xh, r, anw_v)

    sd = jax.ShapeDtypeStruct
    return pl.pallas_call(
        body, name="in_proj_bwd", grid=(s // ts,),
        in_specs=[_row(ts, w) for w in widths]
        + [_row(ts, D_MODEL), _row(ts, 1), _full((1, D_MODEL)), _row(ts, D_MODEL), _full((D_MODEL, IN_EXT))],
        out_specs=[_row(ts, D_MODEL), _full((D_MODEL, IN_EXT)), _full((1, D_MODEL))],
        out_shape=[sd((s, D_MODEL), F32), sd((D_MODEL, IN_EXT), F32), sd((1, D_MODEL), F32)],
        compiler_params=_cp("arbitrary"),
    )(*parts, x, r1, anw, dx1, win)


def _local_step(x, positions, tgt, w, small, ex=None):
    s = x.shape[0]
    t = _tiles(s)
    ex = _Exchanges(w) if ex is None else ex
    f = _forward(x, positions, tgt, w, small, ex)
    pw, rc = f["pw"], f["rc"]
    cos_r, sin_r, cos_m, sin_m = f["tabs"]
    dx2, loss, g_fw = f["dx2"], f["loss"], f["g_fw"]
    du, dx1, g_cw, g_cb, g_fnw, g_wd = _ffn_bwd_call(dx2, f["u"], f["uc"], w["conv_w"], pw["wdown"], pw["wup"],
                                                     f["x1"], f["r2"], small["ffn_norm_w"], t["t2"])
    g_wup = _dw_norm_call(f["x1"], f["r2"], small["ffn_norm_w"], du, t["tw"], F2 // 4, "dw_up")
    started = ex.mlp_grads(dict(w_up=g_wup, w_down=g_wd))
    dy_ret, do, g_wout = _out_bwd_call(dx1, f["y_ret"], f["y_mla"], pw["wout"], t["t1"], started)
    started = ex.behind_out_bwd(g_wout)
    drq, dg, do_ret, g_gnw = _ret_bwd_q_call(f["q"], f["k"], f["v"], f["o_ret"], f["g"], dy_ret, small["ret_gn_w"], rc, cos_r, sin_r, t["tr"])
    drk, drv = _ret_bwd_kv_call(f["q"], f["k"], f["v"], do_ret, rc, cos_r, sin_r, t["tr"])
    dmk, dmv, dmq = _flash_bwd_call(f["mqb"], f["mk"], f["mv"], do, t["tb"], started)
    ex.behind_attention(dmk)
    dcq, dckv, dkpe, g_wq, g_wk, g_wv, g_qnw, g_kvnw = _mla_post_call(
        dmq, dmk, dmv, f["cq"], f["ckv"], small["mla_q_norm_w"], small["mla_kv_norm_w"], pw["wq"], pw["wk"], pw["wv"], cos_m, sin_m, t["ts"])
    gx, g_win_ext, g_anw = _in_bwd_call([drq, drk, drv, dg, dcq, dckv, dkpe], x, f["r1"], small["attn_norm_w"], dx1, pw["win"], t["ts"])
    lo = IN_W - ROPE

    def win_cols(a, b):
        parts = ([g_win_ext[:, a:min(b, lo)]] if a < lo else []) + ([g_win_ext[:, max(a, lo) + KPE_LO:b + KPE_LO]] if b > lo else [])
        return jnp.concatenate(parts, -1)

    if w["w_in"].ndim == 3:
        blocks, c = w["w_in"].shape[0], w["w_in"].shape[2]
        g_win = jnp.stack([win_cols(j * c, (j + 1) * c) for j in range(blocks)])
    else:
        g_win = win_cols(0, IN_W)
    g_wuq = g_wq.transpose(1, 0, 2)[:, :, :HEAD + ROPE].reshape(Q_RANK, N_HEADS * (HEAD + ROPE))
    g_wukv = jnp.concatenate([g_wk[:, :, :HEAD], g_wv[:, :, :HEAD]], -1).transpose(1, 0, 2).reshape(KV_RANK, 2 * MLA_W)
    gw = dict(w_in=g_win, w_uq=g_wuq, w_ukv=g_wukv, w_out=g_wout, w_up=g_wup,
              conv_w=g_cw, w_down=g_wd)
    gs = dict(attn_norm_w=g_anw, ret_gn_w=g_gnw, mla_q_norm_w=g_qnw, mla_kv_norm_w=g_kvnw, ffn_norm_w=g_fnw,
              conv_b=g_cb, final_norm_w=g_fw)
    return loss, gx, gw, gs


MESH_ID = pl.DeviceIdType.MESH
ANY = pl.BlockSpec(memory_space=pl.ANY)
VMEM_SPEC = pl.BlockSpec(memory_space=pltpu.VMEM)
N_DEV = 8
GROUP_A = (("w_in", (D_MODEL, IN_W // 4), 1), ("w_uq", (Q_RANK, 192), 1), ("w_ukv", (KV_RANK, 256), 1),
           ("w_out", (D_MODEL // 4, D_MODEL), 0))
GROUP_B = (("w_up", (D_MODEL, F2 // 4), 1), ("w_down", (D_FF // 4, D_MODEL), 0))
HBM_SPEC = pl.BlockSpec(memory_space=pltpu.HBM)
SEM_SPEC = pl.BlockSpec(memory_space=pltpu.SEMAPHORE)


def _mesh_pos():
    return lax.axis_index("x"), lax.axis_index("y"), lax.axis_index("c")


def _other_chips(x, y):
    return [(1 - x, y), (x, 1 - y), (1 - x, 1 - y)]


def _remote(src, dst, send_sems, recv_sems, k, dev):
    return pltpu.make_async_remote_copy(src_ref=src, dst_ref=dst, send_sem=send_sems.at[k], recv_sem=recv_sems.at[k],
                                        device_id=dev, device_id_type=MESH_ID)


def _gather_list_call(parts, tag):
    n = len(parts)

    def body(*refs):
        srcs, outs, (send_sems, recv_sems) = refs[:n], refs[n:2 * n], refs[2 * n:]
        x, y, c = _mesh_pos()
        sm = 2 * x + y
        chips = _other_chips(x, y)
        sib = (x, y, 1 - c)
        rc = lambda k, src, dst, dev: _remote(src, dst, send_sems, recv_sems, k, dev)
        first = [rc(7 * i + j, srcs[i].at[c], outs[i].at[sm, c], (cx, cy, c)) for i in range(n) for j, (cx, cy) in enumerate(chips)]
        own = [rc(7 * i + 6, srcs[i], outs[i].at[sm], sib) for i in range(n)]
        for cp in first + own:
            cp.start()
        passed = []
        for j, (cx, cy) in enumerate(chips):
            for i in range(n):
                land = outs[i].at[2 * cx + cy, c]
                rc(7 * i + j, srcs[i].at[c], land, (cx, cy, c)).wait_recv()
                cp = rc(7 * i + 3 + j, land, land, sib)
                cp.start()
                passed.append(cp)
        for j, (cx, cy) in enumerate(chips):
            for i in range(n):
                rc(7 * i + 3 + j, srcs[i].at[c], outs[i].at[2 * cx + cy, 1 - c], sib).wait_recv()
        for cp in own:
            cp.wait_recv()
        for cp in first + passed + own:
            cp.wait_send()

    return pl.pallas_call(
        body, name="weights_all_gather_" + tag,
        in_specs=[ANY] * n, out_specs=[ANY] * n,
        out_shape=[jax.ShapeDtypeStruct((4,) + p.shape, p.dtype) for p in parts],
        scratch_shapes=[pltpu.SemaphoreType.DMA((7 * n,)), pltpu.SemaphoreType.DMA((7 * n,))],
    )(*parts)


def _direct_gather_copies(srcs, lands, send_sems, recv_sems):
    x, y, c = _mesh_pos()
    sm = 2 * x + y
    sends, recvs = [], []
    for i, (src, land) in enumerate(zip(srcs, lands)):
        for j, (cx, cy) in enumerate(_other_chips(x, y)):
            for t in range(2):
                sends.append(_remote(src.at[c], land.at[sm, c], send_sems, recv_sems, 13 * i + 4 * j + 2 * c + t, (cx, cy, t)))
                recvs.append(_remote(src.at[t], land.at[2 * cx + cy, t], send_sems, recv_sems, 13 * i + 4 * j + 2 * t + c, (cx, cy, t)))
        sends.append(_remote(src, land.at[sm], send_sems, recv_sems, 13 * i + 12, (x, y, 1 - c)))
        recvs.append(_remote(src, land.at[sm], send_sems, recv_sems, 13 * i + 12, (x, y, 1 - c)))
    return sends, recvs


def _sibling_copies(srcs, lands, send_sems, recv_sems):
    x, y, c = _mesh_pos()
    cps = [_remote(src.at[s, 1 - c], land.at[s], send_sems, recv_sems, 4 * i + s, (x, y, 1 - c))
           for i, (src, land) in enumerate(zip(srcs, lands)) for s in range(4)]
    return cps, cps


def _chips_copies(srcs, lands, send_sems, recv_sems):
    x, y, c = _mesh_pos()
    cps = [_remote(src.at[2 * cx + cy], land.at[j], send_sems, recv_sems, 3 * i + j, (cx, cy, c))
           for i, (src, land) in enumerate(zip(srcs, lands)) for j, (cx, cy) in enumerate(_other_chips(x, y))]
    return cps, cps


def _share_copies(srcs, lands, send_sems, recv_sems):
    x, y, c = _mesh_pos()
    cps = [_remote(src, land, send_sems, recv_sems, i, (x, y, 1 - c)) for i, (src, land) in enumerate(zip(srcs, lands))]
    return cps, cps


def _exchange_call(name, copies, srcs, land_shapes, n_sems):
    n = len(srcs)

    def body(*refs):
        sends, recvs = copies(refs[:n], refs[n:2 * n], refs[2 * n], refs[2 * n + 1])
        for cp in sends:
            cp.start()
        for cp in sends:
            cp.wait_send()
        for cp in recvs:
            cp.wait_recv()

    return pl.pallas_call(
        body, name=name, in_specs=[ANY] * n, out_specs=[ANY] * n, out_shape=list(land_shapes),
        scratch_shapes=[pltpu.SemaphoreType.DMA((n_sems,)), pltpu.SemaphoreType.DMA((n_sems,))],
    )(*srcs)


def _exchange_start_call(name, copies, srcs, land_shapes, n_sems, order=None):
    n = len(srcs)
    extra = [] if order is None else [order]
    k = 2 * n + len(extra)

    def body(*refs):
        sends, _ = copies(refs[:n], refs[n:2 * n], refs[k], refs[k + 1])
        for cp in sends:
            cp.start()
        refs[-1][...] = jnp.zeros_like(refs[-1])

    hbm = lambda a: pltpu.with_memory_space_constraint(a, pltpu.HBM)
    lands = [hbm(lax.empty(sd.shape, sd.dtype)) for sd in land_shapes]
    sem = pltpu.SemaphoreType.DMA((n_sems,))
    out = pl.pallas_call(
        body, name=name,
        out_shape=(sem, sem, *[pltpu.HBM(a.shape, a.dtype) for a in list(srcs) + lands], jax.ShapeDtypeStruct((8, LANES), F32)),
        in_specs=[HBM_SPEC] * (2 * n) + [ANY] * len(extra), out_specs=(SEM_SPEC, SEM_SPEC, *[HBM_SPEC] * (2 * n), VMEM_SPEC),
        input_output_aliases={i: 2 + i for i in range(2 * n)},
        compiler_params=pltpu.CompilerParams(has_side_effects=pltpu.SideEffectType.DATAFLOW_SIDE_EFFECTING),
    )(*[hbm(a) for a in srcs], *lands, *extra)
    return out[0], out[1], out[2:2 + n], out[2 + n:2 + 2 * n], out[-1]


def _exchange_wait_call(name, copies, started, after):
    send_sems, recv_sems, srcs, lands, _ = started
    n = len(srcs)

    def body(*refs):
        sends, recvs = copies(refs[:n], refs[n:2 * n], refs[2 * n], refs[2 * n + 1])
        for cp in sends:
            cp.wait_send()
        for cp in recvs:
            cp.wait_recv()

    out = pl.pallas_call(
        body, name=name,
        out_shape=tuple(pltpu.HBM(a.shape, a.dtype) for a in list(srcs) + list(lands)),
        in_specs=[HBM_SPEC] * (2 * n) + [SEM_SPEC, SEM_SPEC, ANY], out_specs=tuple([HBM_SPEC] * (2 * n)),
        input_output_aliases={i: i for i in range(2 * n)},
        compiler_params=pltpu.CompilerParams(has_side_effects=pltpu.SideEffectType.DATAFLOW_SIDE_EFFECTING),
    )(*srcs, *lands, send_sems, recv_sems, after)
    return out[:n], out[n:]


def _rows_tile(rows, width, itemsize=4):
    limit = max(16, (3 << 20) // (width * itemsize))
    if rows <= limit:
        return rows
    return max(t for t in range(16, limit + 1, 16) if rows % t == 0)


def _sum_sibling_call(g, buf, c, name):
    _, _, rh, w = g.shape
    tile = _rows_tile(rh, w)

    def body(c_ref, g_ref, b_ref, p_ref, pb_ref):
        p = g_ref[...] + b_ref[...]
        p_ref[...] = p
        pb_ref[...] = p.astype(BF16)

    blk = pl.BlockSpec((None, tile, w), lambda s, i, c_ref: (s, i, 0))
    return pl.pallas_call(
        body, name=name,
        grid_spec=pltpu.PrefetchScalarGridSpec(
            num_scalar_prefetch=1, grid=(4, rh // tile),
            in_specs=[pl.BlockSpec((None, None, tile, w), lambda s, i, c_ref: (s, c_ref[0], i, 0)), blk],
            out_specs=[blk, blk]),
        out_shape=[jax.ShapeDtypeStruct((4, rh, w), F32), jax.ShapeDtypeStruct((4, rh, w), BF16)],
        compiler_params=_cp("parallel", "parallel"),
    )(c, g, buf)


def _sum_chips_call(p, buf, sm, name):
    _, rh, w = p.shape
    tile = _rows_tile(rh, w)

    def body(sm_ref, p_ref, b_ref, f_ref):
        f_ref[...] = ((p_ref[...] + b_ref[0].astype(F32)) + b_ref[1].astype(F32)) + b_ref[2].astype(F32)

    return pl.pallas_call(
        body, name=name,
        grid_spec=pltpu.PrefetchScalarGridSpec(
            num_scalar_prefetch=1, grid=(rh // tile,),
            in_specs=[pl.BlockSpec((None, tile, w), lambda i, sm_ref: (sm_ref[0], i, 0)),
                      pl.BlockSpec((3, tile, w), lambda i, sm_ref: (0, i, 0))],
            out_specs=pl.BlockSpec((tile, w), lambda i, sm_ref: (i, 0))),
        out_shape=jax.ShapeDtypeStruct((rh, w), F32),
        compiler_params=_cp("parallel"),
    )(sm, p, buf)


def _adamw_halves_call(w, g_mine, g_sib, c, m, v, name, transposed=False):
    if transposed:
        rows, r = w.shape
        rh = r // 2
        tile = _rows_tile(rows, rh)
        whole = pl.BlockSpec((tile, rh), lambda h, i, c_ref: (i, h))
        half = pl.BlockSpec((tile, rh), lambda h, i, c_ref: (i, 0))
        nt = rows // tile
    else:
        r, wd = w.shape
        rh = r // 2
        tile = _rows_tile(rh, wd)
        nt = rh // tile
        whole = pl.BlockSpec((tile, wd), lambda h, i, c_ref: (h * nt + i, 0))
        half = pl.BlockSpec((tile, wd), lambda h, i, c_ref: (i, 0))

    def body(c_ref, w_ref, gm_ref, gs_ref, m_ref, v_ref, g_ref, d_ref, nm_ref, nv_ref):
        gv = jnp.where(pl.program_id(0) == c_ref[0], gm_ref[...], gs_ref[...])
        g_ref[...] = gv
        nm = ADAM_B1 * m_ref[...] + (1.0 - ADAM_B1) * gv
        nv = ADAM_B2 * v_ref[...] + (1.0 - ADAM_B2) * jnp.square(gv)
        m_hat = nm / (1.0 - ADAM_B1 ** ADAM_STEP)
        v_hat = nv / (1.0 - ADAM_B2 ** ADAM_STEP)
        d_ref[...] = -ADAM_LR * (m_hat / (jnp.sqrt(v_hat) + ADAM_EPS) + ADAM_WD * w_ref[...])
        nm_ref[...] = nm
        nv_ref[...] = nv

    sd = jax.ShapeDtypeStruct(w.shape, F32)
    return pl.pallas_call(
        body, name=name,
        grid_spec=pltpu.PrefetchScalarGridSpec(
            num_scalar_prefetch=1, grid=(2, nt),
            in_specs=[whole, half, half, whole, whole], out_specs=[whole] * 4),
        out_shape=[sd, sd, sd, sd],
        compiler_params=_cp("parallel", "parallel"),
    )(c, w, g_mine, g_sib, m, v)


def _all_reduce8_call(vec, name):
    rows = vec.shape[0]

    def body(v_ref, out_ref, slots, send_sems, recv_sems):
        x, y, c = _mesh_pos()
        me = 4 * x + 2 * y + c
        slots[me] = v_ref[...]

        def rcopy(k, to_me):
            bx, by, bc = (k >> 2) & 1, (k >> 1) & 1, k & 1
            px, py, pc = (1 - x if bx else x), (1 - y if by else y), (1 - c if bc else c)
            slot = 4 * px + 2 * py + pc if to_me else me
            return pltpu.make_async_remote_copy(src_ref=v_ref, dst_ref=slots.at[slot], send_sem=send_sems.at[k - 1],
                                                recv_sem=recv_sems.at[k - 1], device_id=(px, py, pc), device_id_type=MESH_ID)

        for k in range(1, N_DEV):
            rcopy(k, False).start()
        for k in range(1, N_DEV):
            rcopy(k, True).wait_recv()
        for k in range(1, N_DEV):
            rcopy(k, False).wait_send()
        tot = slots[0]
        for d in range(1, N_DEV):
            tot = tot + slots[d]
        out_ref[...] = tot

    return pl.pallas_call(
        body, name=name,
        in_specs=[VMEM_SPEC], out_specs=VMEM_SPEC,
        out_shape=jax.ShapeDtypeStruct((rows, LANES), F32),
        scratch_shapes=[pltpu.VMEM((N_DEV, rows, LANES), F32),
                        pltpu.SemaphoreType.DMA((N_DEV - 1,)), pltpu.SemaphoreType.DMA((N_DEV - 1,))],
    )(vec)


def _adamw_call(w, g, m, v, name):
    r, c = w.shape
    rb = r if r <= 256 else (256 if r % 256 == 0 else 352)
    assert r % rb == 0

    def body(w_ref, g_ref, m_ref, v_ref, d_ref, nm_ref, nv_ref):
        gv = g_ref[...]
        nm = ADAM_B1 * m_ref[...] + (1.0 - ADAM_B1) * gv
        nv = ADAM_B2 * v_ref[...] + (1.0 - ADAM_B2) * jnp.square(gv)
        m_hat = nm / (1.0 - ADAM_B1 ** ADAM_STEP)
        v_hat = nv / (1.0 - ADAM_B2 ** ADAM_STEP)
        d_ref[...] = -ADAM_LR * (m_hat / (jnp.sqrt(v_hat) + ADAM_EPS) + ADAM_WD * w_ref[...])
        nm_ref[...] = nm
        nv_ref[...] = nv

    spec = pl.BlockSpec((rb, c), lambda i: (i, 0))
    sd = jax.ShapeDtypeStruct((r, c), F32)
    return pl.pallas_call(
        body, name=name, grid=(r // rb,),
        in_specs=[spec] * 4, out_specs=[spec] * 3, out_shape=[sd, sd, sd],
        compiler_params=_cp("parallel"),
    )(w, g, m, v)


SMALL = (("attn_norm_w", D_MODEL), ("ret_gn_w", RET_W), ("mla_q_norm_w", Q_RANK), ("mla_kv_norm_w", KV_RANK),
         ("ffn_norm_w", D_MODEL), ("conv_b", F2), ("final_norm_w", D_MODEL))
WEIGHT_ORDER = ("attn_norm_w", "w_in", "ret_gn_w", "mla_q_norm_w", "w_uq", "mla_kv_norm_w", "w_ukv", "w_out",
                "ffn_norm_w", "w_up", "conv_w", "conv_b", "w_down", "final_norm_w")


def _pad_rows(flat, rows):
    return jnp.concatenate([flat, jnp.zeros((rows * LANES - flat.shape[0],), flat.dtype)]).reshape(rows, LANES)


def kernel(x, positions, attn_norm_w, w_in, ret_gn_w, mla_q_norm_w, w_uq, mla_kv_norm_w, w_ukv, w_out, ffn_norm_w, w_up, conv_w, conv_b, w_down, final_norm_w, loss_target, m_attn_norm_w, m_w_in, m_ret_gn_w, m_mla_q_norm_w, m_w_uq, m_mla_kv_norm_w, m_w_ukv, m_w_out, m_ffn_norm_w, m_w_up, m_conv_w, m_conv_b, m_w_down, m_final_norm_w, v_attn_norm_w, v_w_in, v_ret_gn_w, v_mla_q_norm_w, v_w_uq, v_mla_kv_norm_w, v_w_ukv, v_w_out, v_ffn_norm_w, v_w_up, v_conv_w, v_conv_b, v_w_down, v_final_norm_w):
    args = dict(locals())
    cx, cy, cc = _mesh_pos()
    sm = 2 * cx + cy

    c_arr, sm_arr = cc.reshape(1).astype(jnp.int32), sm.reshape(1).astype(jnp.int32)
    sds = jax.ShapeDtypeStruct

    def my_shards(group):
        return [args[n][0].astype(BF16).reshape(2, r // 2, c) for n, (r, c), _ in group]

    def full_weights(gathered, group):
        full = {}
        for (n, (r, c), axis), got in zip(group, gathered):
            piece = got.reshape(4, r, c)
            full[n] = piece if n in ("w_up", "w_in") else (piece.transpose(1, 0, 2).reshape(r, 4 * c) if axis == 1 else piece.reshape(4 * r, c))
        return full

    def by_owner(gw, group):
        out = []
        for n, (r, c), axis in group:
            g = gw[n]
            if axis == 1 and g.ndim == 2:
                g = g.reshape(r, 4, c).transpose(1, 0, 2)
            out.append(g.reshape(4, 2, r // 2, c))
        return out

    def sibling_shapes(gs):
        return [sds((4,) + g.shape[2:], F32) for g in gs]

    def chip_sums(gs, bufs, group):
        res = [_sum_sibling_call(g, b, c_arr, "grads_sum_sibling_" + n) for g, b, (n, _, _) in zip(gs, bufs, group)]
        return [p for p, _ in res], [pb for _, pb in res]

    def chips_shapes(pbs):
        return [sds((3,) + pb.shape[1:], BF16) for pb in pbs]

    def totals(ps, lands, group, tag):
        fins = [_sum_chips_call(p, l, sm_arr, "grads_sum_chips_" + n) for p, l, (n, _, _) in zip(ps, lands, group)]
        sibs = _exchange_call("grads_rs_share_" + tag, _share_copies, fins, [sds(f.shape, F32) for f in fins], len(fins))
        return {n: (f, s) for (n, _, _), f, s in zip(group, fins, sibs)}

    class StepExchanges(_Exchanges):
        def __init__(self, order):
            shards = my_shards(GROUP_B)
            self.gather = _exchange_start_call("weights_gather_start_b", _direct_gather_copies, shards,
                                               [sds((4,) + s.shape, BF16) for s in shards], 13 * len(shards), order)
            self.red = None

        def token(self):
            return self.gather[4][0:1, 0:1]

        def mlp_weights(self, after):
            return full_weights(_exchange_wait_call("weights_gather_wait_b", _direct_gather_copies, self.gather, after)[1], GROUP_B)

        def mlp_grads(self, gw):
            gs = by_owner(gw, GROUP_B)
            self.step1 = _exchange_start_call("grads_rs_sibling_start_b", _sibling_copies, gs, sibling_shapes(gs), 4 * len(gs))
            return self.step1[4]

        def behind_out_bwd(self, after):
            gs, bufs = _exchange_wait_call("grads_rs_sibling_wait_b", _sibling_copies, self.step1, after)
            self.ps, pbs = chip_sums(gs, bufs, GROUP_B)
            self.step2 = _exchange_start_call("grads_rs_chips_start_b", _chips_copies, pbs, chips_shapes(pbs), 3 * len(pbs))
            return self.step2[4]

        def behind_attention(self, after):
            _, lands = _exchange_wait_call("grads_rs_chips_wait_b", _chips_copies, self.step2, after)
            self.red = totals(self.ps, lands, GROUP_B, "b")

    gathered = _gather_list_call(my_shards(GROUP_A) + [conv_w[0].reshape(2, 1, 3 * F2 // 8)], "a")
    full = full_weights(gathered[:-1], GROUP_A)
    ex = StepExchanges(gathered[-1])
    full["conv_w"] = gathered[-1].reshape(4, 3, F2 // 4).transpose(1, 0, 2).reshape(3, F2)
    small = {n: args[n].reshape(1, d) for n, d in SMALL}
    small["attn_norm_w"] = small["attn_norm_w"] + ex.token()

    loss, gx, gw, gs = _local_step(x[0], positions[0], loss_target[0], full, small, ex)

    ga = by_owner(gw, GROUP_A)
    bufs = _exchange_call("grads_rs_sibling_a", _sibling_copies, ga, sibling_shapes(ga), 4 * len(ga))
    ps, pbs = chip_sums(ga, bufs, GROUP_A)
    lands = _exchange_call("grads_rs_chips_a", _chips_copies, pbs, chips_shapes(pbs), 3 * len(pbs))
    halves = {**ex.red, **totals(ps, lands, GROUP_A, "a")}

    vec = jnp.concatenate([gs[n].reshape(-1) for n, _ in SMALL] + [gw["conv_w"].reshape(-1), loss.reshape(-1)])
    tot = _all_reduce8_call(_pad_rows(vec, 216), "small_all_reduce").reshape(-1)
    red, off = {}, 0
    for n, d in SMALL:
        red[n] = tot[off:off + d].reshape(1, d)
        off += d
    red["conv_w"] = lax.dynamic_slice(tot[off:off + 3 * F2].reshape(3, F2), (0, sm * (F2 // 4)), (3, F2 // 4))
    loss_tot = tot[off + 3 * F2]

    grads, deltas, new_m, new_v = [], [], [], []
    for n in WEIGHT_ORDER:
        shape = args[n].shape
        two_d = (1, shape[0]) if len(shape) == 1 else shape[-2:]
        wmv = [args[k + n].reshape(two_d) for k in ("", "m_", "v_")]
        if n in halves and two_d[1] % LANES:
            tr = lambda a: a.T
            g, d, nm, nv = map(tr, _adamw_halves_call(tr(wmv[0]), *map(tr, halves[n]), c_arr, tr(wmv[1]), tr(wmv[2]),
                                                      "adamw_" + n, transposed=True))
        elif n in halves:
            g, d, nm, nv = _adamw_halves_call(wmv[0], *halves[n], c_arr, wmv[1], wmv[2], "adamw_" + n)
        else:
            g = red[n].reshape(two_d)
            d, nm, nv = _adamw_call(wmv[0], g, wmv[1], wmv[2], "adamw_" + n)
        grads.append(g.reshape(shape))
        deltas.append(d.reshape(shape))
        new_m.append(nm.reshape(shape))
        new_v.append(nv.reshape(shape))
    return (loss_tot, gx[None], *grads, *deltas, *new_m, *new_v)
```

```python
import math

import numpy as np
import jax
import jax.numpy as jnp
from jax import lax
from jax.experimental import pallas as pl
from jax.experimental.pallas import tpu as pltpu

F32 = jnp.float32
BF16 = jnp.bfloat16

D_MODEL = 1024
N_HEADS = 8
HEAD = 64
RET_W = N_HEADS * HEAD
MLA_W = N_HEADS * HEAD
ROPE = 32
Q_RANK = 256
KV_RANK = 128
D_FF = 2816
F2 = 2 * D_FF
IN_W = 4 * RET_W + Q_RANK + KV_RANK + ROPE
IN_EXT = 4 * RET_W + Q_RANK + KV_RANK + 128
KPE_LO = 64
ROPE_BASE = 10000.0
EPS = 1e-6
RET_CHUNK = 256
SM_SCALE = (HEAD + ROPE) ** -0.5
LOG2E = math.log2(math.e)
LN2 = math.log(2.0)
NEG = -1e30
LANES = 128
VMEM_LIMIT = 56 * 1024 * 1024

ADAM_LR = 0.001
ADAM_B1 = 0.9
ADAM_B2 = 0.999
ADAM_EPS = 1e-08
ADAM_WD = 0.01
ADAM_STEP = 10


VMEM_LIMIT_MLP = 60 * 1024 * 1024


def _cp(*sem, vmem=VMEM_LIMIT):
    return pltpu.CompilerParams(dimension_semantics=sem, vmem_limit_bytes=vmem)


def _full(shape):
    n = len(shape)
    return pl.BlockSpec(tuple(shape), lambda *_: (0,) * n)


def _row(ts, c):
    return pl.BlockSpec((ts, c), lambda i: (i, 0))


def _hrow(h, ts, c):
    return pl.BlockSpec((h, ts, c), lambda i: (0, i, 0))


def _dot(a, b):
    return jnp.dot(a, b, preferred_element_type=F32)


def _dot_nt(a, b):
    return lax.dot_general(a, b, (((1,), (1,)), ((), ())), preferred_element_type=F32)


def _dot_tn(a, b):
    return lax.dot_general(a, b, (((0,), (0,)), ((), ())), preferred_element_type=F32)


def _dot_hi(a, b):
    hi = a.astype(BF16)
    lo = (a - hi.astype(F32)).astype(BF16)
    bb = b.astype(BF16)
    return _dot(hi, bb) + _dot(lo, bb)


def _rot_half(x, half):
    w = x.shape[-1]
    lane = lax.broadcasted_iota(jnp.int32, x.shape, x.ndim - 1)
    first = (lane % (2 * half)) < half
    return jnp.where(first, -pltpu.roll(x, w - half, x.ndim - 1), pltpu.roll(x, half, x.ndim - 1))


def _rope(x, cos, sin, half):
    return x * cos + _rot_half(x, half) * sin


def _unrope(dy, cos, sin, half):
    return dy * cos - _rot_half(dy, half) * sin


def _sigmoid(g):
    return 0.5 * jnp.tanh(0.5 * g) + 0.5


def _silu(g):
    return g * _sigmoid(g)


def _rstd(x):
    return lax.rsqrt(jnp.mean(x * x, axis=-1, keepdims=True) + EPS)


def _rope_tables(positions):
    s = positions.shape[0]
    hr, hm = HEAD // 2, ROPE // 2
    pos = positions.astype(F32)[None, :]
    inv_r = ROPE_BASE ** (-jnp.arange(0, HEAD, 2, dtype=F32) / HEAD)
    inv_m = ROPE_BASE ** (-jnp.arange(0, ROPE, 2, dtype=F32) / ROPE)
    ang = jnp.concatenate([inv_r, inv_m])[:, None] * pos
    packed = jnp.concatenate([jnp.cos(ang), jnp.sin(ang), jnp.zeros((LANES - 2 * (hr + hm), s), F32)], 0)
    tx = min(s, 1024)

    def spread(t, lane, pieces, fill):
        out = jnp.full(t.shape, fill, F32)
        for lo, src, width in pieces:
            moved = t if lo == src else pltpu.roll(t, (lo - src) % LANES, 1)
            out = jnp.where((lane >= lo) & (lane < lo + width), moved, out)
        return out

    def body(p_ref, cr_ref, sr_ref, cm_ref, sm_ref):
        t = p_ref[...].T
        lane = lax.broadcasted_iota(jnp.int32, t.shape, 1)
        cr_ref[...] = spread(t, lane, [(j * hr, 0, hr) for j in range(LANES // hr)], 0.0)
        sr_ref[...] = spread(t, lane, [(j * hr, hr + hm, hr) for j in range(LANES // hr)], 0.0)
        cm_ref[...] = spread(t, lane, [(KPE_LO, hr, hm), (KPE_LO + hm, hr, hm)], 1.0)
        sm_ref[...] = spread(t, lane, [(KPE_LO, 2 * hr + hm, hm), (KPE_LO + hm, 2 * hr + hm, hm)], 0.0)

    tab = jax.ShapeDtypeStruct((s, LANES), F32)
    return pl.pallas_call(
        body, name="rope_tables", grid=(s // tx,),
        in_specs=[pl.BlockSpec((LANES, tx), lambda i: (0, i))],
        out_specs=[_row(tx, LANES)] * 4, out_shape=[tab] * 4,
        compiler_params=_cp("parallel"),
    )(packed)


def _ret_consts():
    c = RET_CHUNK
    lg = np.log1p(-np.power(2.0, -5.0 - np.arange(N_HEADS, dtype=np.float64)))
    idx = np.arange(c, dtype=np.float64)
    diff = idx[:, None] - idx[None, :]
    lane_head = np.arange(LANES) // HEAD
    dmask = np.zeros((4, 2, c, c))
    zeta = np.zeros((4, c, LANES))
    xi = np.zeros((4, c, LANES))
    cd = np.zeros((4, LANES, LANES))
    bd = (lane_head[:, None] == lane_head[None, :]).astype(np.float64)
    for j in range(4):
        for hh in range(2):
            dmask[j, hh] = np.where(diff >= 0, np.exp(lg[2 * j + hh] * np.maximum(diff, 0.0)), 0.0)
        lgl = lg[2 * j + lane_head]
        zeta[j] = np.exp(lgl[None, :] * (c - 1.0 - idx[:, None]))
        xi[j] = np.exp(lgl[None, :] * (idx[:, None] + 1.0))
        cd[j] = np.exp(lgl * c)[:, None] * bd
    f = lambda a: jnp.asarray(a, F32)
    side = lambda d: np.concatenate([d[:, 0], d[:, 1]], axis=-1)
    return dict(dmask=f(side(dmask)), dmask_t=f(side(np.swapaxes(dmask, 2, 3))), zeta=f(zeta), xi=f(xi), cd=f(cd), bd=f(bd))


def _f1_call(x, anw, win, qnw, kvnw, wq, wk, wv, cos_r, sin_r, cos_m, sin_m, ts):
    s = x.shape[0]

    def body(x_ref, anw_ref, w_ref, qnw_ref, kvnw_ref, wq_ref, wk_ref, wv_ref, cr_ref, sr_ref, cm_ref, sm_ref,
             q_ref, k_ref, v_ref, g_ref, cq_ref, ckv_ref, mq_ref, mk_ref, mv_ref, r_ref):
        xv = x_ref[...]
        r = _rstd(xv)
        r_ref[...] = r
        h = (xv * r * anw_ref[...]).astype(BF16)
        cr, sr = cr_ref[...], sr_ref[...]
        qk = _dot(h, w_ref[:, 0:2 * RET_W])
        for j in range(4):
            sl = slice(j * LANES, (j + 1) * LANES)
            q_ref[:, sl] = _rope(qk[:, sl], cr, sr, HEAD // 2).astype(BF16)
            kk = qk[:, RET_W + j * LANES:RET_W + (j + 1) * LANES]
            k_ref[:, sl] = (_rope(kk, cr, sr, HEAD // 2) * (HEAD ** -0.5)).astype(BF16)
        v_ref[...] = _dot(h, w_ref[:, 2 * RET_W:3 * RET_W]).astype(BF16)
        g_ref[...] = _dot(h, w_ref[:, 3 * RET_W:4 * RET_W])
        o = 4 * RET_W
        cqv = _dot(h, w_ref[:, o:o + Q_RANK])
        ckvv = _dot(h, w_ref[:, o + Q_RANK:o + Q_RANK + KV_RANK])
        cq_ref[...] = cqv
        ckv_ref[...] = ckvv
        cm, sm = cm_ref[...], sm_ref[...]
        kp = _rope(_dot(h, w_ref[:, o + Q_RANK + KV_RANK:IN_EXT]), cm, sm, ROPE // 2)
        kp = _lane_pair((ts, LANES), QK_AUX, -1.0, -1.0, kp)
        cqn = (cqv * _rstd(cqv) * qnw_ref[...]).astype(BF16)
        ckvn = (ckvv * _rstd(ckvv) * kvnw_ref[...]).astype(BF16)
        for hd in range(N_HEADS):
            qh = _rope(_dot(cqn, wq_ref[hd]), cm, sm, ROPE // 2)
            mq_ref[hd] = (qh * (SM_SCALE * LOG2E)).astype(BF16)
            mk_ref[hd] = (_dot(ckvn, wk_ref[hd]) + kp).astype(BF16)
            mv_ref[hd] = _lane_pair((ts, LANES), V_AUX, 1.0, 1.0, _dot(ckvn, wv_ref[hd])).astype(BF16)

    sd = jax.ShapeDtypeStruct
    hm = sd((N_HEADS, s, LANES), BF16)
    return pl.pallas_call(
        body, name="f1_in_proj", grid=(s // ts,),
        in_specs=[_row(ts, D_MODEL), _full((1, D_MODEL)), _full((D_MODEL, IN_EXT)), _full((1, Q_RANK)), _full((1, KV_RANK)),
                  _full((N_HEADS, Q_RANK, LANES)), _full((N_HEADS, KV_RANK, LANES)), _full((N_HEADS, KV_RANK, LANES)),
                  _row(ts, LANES), _row(ts, LANES), _row(ts, LANES), _row(ts, LANES)],
        out_specs=[_row(ts, RET_W), _row(ts, RET_W), _row(ts, RET_W), _row(ts, RET_W),
                   _row(ts, Q_RANK), _row(ts, KV_RANK)] + [_hrow(N_HEADS, ts, LANES)] * 3 + [_row(ts, 1)],
        out_shape=[sd((s, RET_W), BF16), sd((s, RET_W), BF16), sd((s, RET_W), BF16), sd((s, RET_W), F32),
                   sd((s, Q_RANK), F32), sd((s, KV_RANK), F32), hm, hm, hm, sd((s, 1), F32)],
        compiler_params=_cp("parallel"),
    )(x, anw, win, qnw, kvnw, wq, wk, wv, cos_r, sin_r, cos_m, sin_m)


def _stack_heads(a):
    lo = lax.broadcasted_iota(jnp.int32, a.shape, 1) < HEAD
    zero = jnp.zeros_like(a)
    return jnp.concatenate([jnp.where(lo, a, zero), jnp.where(lo, zero, a)], axis=0)


def _pair_product(a, b2, decay2, w2):
    return _dot((_dot_nt(a, b2) * decay2).astype(BF16), w2)


RET_SLABS = 2


def _ret_specs(tr, tile_of):
    c, ns = RET_CHUNK, RET_SLABS
    return dict(
        slab=pl.BlockSpec((tr, ns * LANES), lambda j, i: (tile_of(i), j)),
        tab=pl.BlockSpec((tr, LANES), lambda j, i: (tile_of(i), 0)),
        vec=pl.BlockSpec((1, ns * LANES), lambda j, i: (0, j)),
        dmask=pl.BlockSpec((ns, c, 2 * c), lambda j, i: (j, 0, 0)),
        rows=pl.BlockSpec((ns, c, LANES), lambda j, i: (j, 0, 0)),
        state=pl.BlockSpec((ns, LANES, LANES), lambda j, i: (j, 0, 0)),
        bd=pl.BlockSpec((LANES, LANES), lambda j, i: (0, 0)))


def _ret_states(a_ref, b_ref, scale_ref, cd_ref, bd, st_ref, chunks, lanes, reverse):
    nc = len(chunks)
    contrib = [[_dot_tn((a_ref[rows, ln].astype(F32) * scale_ref[sl]).astype(BF16), b_ref[rows, ln]) * bd for rows in chunks]
               for sl, ln in enumerate(lanes)]
    states = []
    for sl in range(len(lanes)):
        st, seen = st_ref[sl], [None] * nc
        for ci in (reversed(range(nc)) if reverse else range(nc)):
            seen[ci] = st.astype(BF16)
            st = st * cd_ref[sl] + contrib[sl][ci]
        st_ref[sl] = st
        states.append(seen)
    return states


def _ret_fwd_call(q, k, v, g, gnw, rc, tr):
    s = q.shape[0]
    c = RET_CHUNK
    nc = tr // c
    ns = RET_SLABS

    def body(q_ref, k_ref, v_ref, g_ref, gnw_ref, dm_ref, zeta_ref, xi_ref, cd_ref, bd_ref, o_ref, y_ref, st_ref):
        @pl.when(pl.program_id(1) == 0)
        def _():
            st_ref[...] = jnp.zeros_like(st_ref)

        bd = bd_ref[...]
        chunks = [slice(ci * c, (ci + 1) * c) for ci in range(nc)]
        lanes = [slice(sl * LANES, (sl + 1) * LANES) for sl in range(ns)]
        states = _ret_states(k_ref, v_ref, zeta_ref, cd_ref, bd, st_ref, chunks, lanes, False)
        for ci, rows in enumerate(chunks):
            for sl, ln in enumerate(lanes):
                qc = q_ref[rows, ln]
                o_ref[rows, ln] = (_dot(qc, states[sl][ci]) * xi_ref[sl]
                                   + _pair_product(qc, _stack_heads(k_ref[rows, ln]), dm_ref[sl], _stack_heads(v_ref[rows, ln])))
        avg = bd * (1.0 / HEAD)
        for ln in lanes:
            o = o_ref[:, ln]
            ctr = o - _dot_hi(o, avg)
            var = _dot_hi(ctr * ctr, avg)
            y_ref[:, ln] = (_silu(g_ref[:, ln]) * (ctr * lax.rsqrt(var + EPS) * gnw_ref[:, ln])).astype(BF16)

    specs = _ret_specs(tr, lambda i: i)
    sd = jax.ShapeDtypeStruct
    return pl.pallas_call(
        body, name="ret_fwd", grid=(4 // ns, s // tr),
        in_specs=[specs["slab"]] * 4 + [specs["vec"], specs["dmask"], specs["rows"], specs["rows"], specs["state"], specs["bd"]],
        out_specs=[specs["slab"]] * 2,
        out_shape=[sd((s, RET_W), F32), sd((s, RET_W), BF16)],
        scratch_shapes=[pltpu.VMEM((ns, LANES, LANES), F32)],
        compiler_params=_cp("parallel", "arbitrary"),
    )(q, k, v, g, gnw, rc["dmask"], rc["zeta"], rc["xi"], rc["cd"], rc["bd"])


QK_AUX = HEAD + ROPE
V_AUX = HEAD


def _lane_pair(shape, lo, a, b, rest):
    lane = lax.broadcasted_iota(jnp.int32, shape, len(shape) - 1)
    return jnp.where(lane == lo, a, jnp.where(lane == lo + 1, b, rest))


def _hi_lo(v):
    hi = v.astype(BF16).astype(F32)
    return hi, v - hi


def _flash_fwd_call(q, k, v, tb):
    s = q.shape[1]
    nb = s // tb
    pairs = [(a, b) for a in range(nb) for b in range(a + 1)]
    qi_of, ki_of = (jnp.asarray(np.array(col, np.int32)) for col in zip(*pairs))

    def body(qi_ref, ki_ref, q_ref, k_ref, v_ref, o_ref, qb_ref, m_ref, acc_ref):
        qi, ki = qi_ref[pl.program_id(0)], ki_ref[pl.program_id(0)]

        @pl.when(ki == 0)
        def _():
            m_ref[...] = jnp.full_like(m_ref, NEG)
            acc_ref[...] = jnp.zeros_like(acc_ref)

        def step(masked):
            if masked:
                keep = lax.broadcasted_iota(jnp.int32, (tb, tb), 1) <= lax.broadcasted_iota(jnp.int32, (tb, tb), 0)
            def finish(h, pe, alpha):
                acc_ref[h] = acc_ref[h] * alpha + _dot(pe, v_ref[h])

            nxt, pending = _dot_nt(q_ref[0], k_ref[0]), None
            for h in range(N_HEADS):
                sc = nxt
                if h + 1 < N_HEADS:
                    nxt = _dot_nt(q_ref[h + 1], k_ref[h + 1])
                if masked:
                    sc = jnp.where(keep, sc, NEG)
                m_prev = m_ref[h]
                m_new = jnp.maximum(m_prev, jnp.max(sc, axis=1, keepdims=True))
                pe = jnp.exp2(sc - jnp.tile(m_new, (1, tb // LANES))).astype(BF16)
                m_ref[h] = m_new
                if pending is not None:
                    finish(*pending)
                pending = (h, pe, jnp.exp2(m_prev - m_new))
            finish(*pending)

        @pl.when(ki < qi)
        def _():
            step(False)

        @pl.when(ki == qi)
        def _():
            step(True)
            lane = lax.broadcasted_iota(jnp.int32, (tb, LANES), 1)
            for p in range(N_HEADS // 2):
                outs = []
                for h in (2 * p, 2 * p + 1):
                    acc = acc_ref[h]
                    l = acc[:, V_AUX:V_AUX + 1]
                    outs.append(acc * (1.0 / l))
                    hi, lo = _hi_lo(m_ref[h][:, 0:1] + jnp.log(l) * LOG2E)
                    qb_ref[h] = _lane_pair((tb, LANES), QK_AUX, hi, lo, q_ref[h].astype(F32)).astype(BF16)
                o_ref[:, p * LANES:(p + 1) * LANES] = jnp.where(lane < HEAD, outs[0], pltpu.roll(outs[1], HEAD, 1)).astype(BF16)

    sd = jax.ShapeDtypeStruct
    qspec = pl.BlockSpec((N_HEADS, tb, LANES), lambda p, qi_ref, ki_ref: (0, qi_ref[p], 0))
    kspec = pl.BlockSpec((N_HEADS, tb, LANES), lambda p, qi_ref, ki_ref: (0, ki_ref[p], 0))
    return pl.pallas_call(
        body, name="mla_flash_fwd",
        grid_spec=pltpu.PrefetchScalarGridSpec(
            num_scalar_prefetch=2, grid=(len(pairs),),
            in_specs=[qspec, kspec, kspec],
            out_specs=[pl.BlockSpec((tb, MLA_W), lambda p, qi_ref, ki_ref: (qi_ref[p], 0)), qspec],
            scratch_shapes=[pltpu.VMEM((N_HEADS, tb, LANES), F32), pltpu.VMEM((N_HEADS, tb, LANES), F32)]),
        out_shape=[sd((s, MLA_W), BF16), sd((N_HEADS, s, LANES), BF16)],
        compiler_params=_cp("arbitrary"),
    )(qi_of, ki_of, q, k, v)


def _out_proj_call(x, yret, ymla, wout, ts):
    s = x.shape[0]

    def body(x_ref, yr_ref, ym_ref, w_ref, x1_ref, r_ref):
        x1 = x_ref[...] + _dot(yr_ref[...], w_ref[0:RET_W, :]) + _dot(ym_ref[...], w_ref[RET_W:, :])
        x1_ref[...] = x1
        r_ref[...] = _rstd(x1)

    sd = jax.ShapeDtypeStruct
    return pl.pallas_call(
        body, name="out_proj", grid=(s // ts,),
        in_specs=[_row(ts, D_MODEL), _row(ts, RET_W), _row(ts, MLA_W), _full((D_MODEL, D_MODEL))],
        out_specs=[_row(ts, D_MODEL), _row(ts, 1)],
        out_shape=[sd((s, D_MODEL), F32), sd((s, 1), F32)],
        compiler_params=_cp("parallel"),
    )(x, yret, ymla, wout)


W_UP_SHARD = F2 // 4


def _ffn_fwd_call(x1, r2, fnw, wup4, cw, cb, wdown, tgt, fw, ts):
    s = x1.shape[0]
    wsh = W_UP_SHARD

    def body(x_ref, r_ref, fnw_ref, wup_ref, cw_ref, cb_ref, wd_ref, t_ref, fw_ref,
             u_ref, uc_ref, dx2_ref, loss_ref, gfw_ref, carry_ref):
        _zero_first(pl.program_id(0) == 0, carry_ref, loss_ref, gfw_ref)
        xv = x_ref[...]
        h = (xv * r_ref[...] * fnw_ref[...]).astype(BF16)
        conv = []
        for j in range(4):
            cols = slice(j * wsh, (j + 1) * wsh)
            ub = _dot(h, wup_ref[j]).astype(BF16)
            u_ref[:, cols] = ub
            u = ub.astype(F32)
            u1, u2 = _shifted(u, carry_ref[:, cols])
            w = cw_ref[:, cols]
            cb16 = (cb_ref[:, cols] + w[0:1, :] * u2 + w[1:2, :] * u1 + w[2:3, :] * u).astype(BF16)
            uc_ref[:, cols] = cb16
            conv.append(cb16.astype(F32))
            carry_ref[:, cols] = u[ts - 8:, :]
        acc = xv
        for j in range(2):
            a = (_silu(conv[j]) * conv[j + 2]).astype(BF16)
            acc = acc + _dot(a, wd_ref[j * wsh:(j + 1) * wsh, :])
        r = _rstd(acc)
        xh = acc * r
        fwv = fw_ref[...]
        e = xh * fwv - t_ref[...]
        loss_ref[...] += (0.5 / D_MODEL) * _colsum(jnp.sum(e * e, axis=1, keepdims=True))
        dy = e * (1.0 / D_MODEL)
        gfw_ref[...] += _colsum(dy * xh)
        dx2_ref[...] = _norm_bwd(dy, xh, r, fwv)

    sd = jax.ShapeDtypeStruct
    once = lambda shape: pl.BlockSpec(shape, lambda i: (0,) * len(shape), pipeline_mode=pl.Buffered(1))
    return pl.pallas_call(
        body, name="ffn_fwd_loss", grid=(s // ts,),
        in_specs=[_row(ts, D_MODEL), _row(ts, 1), once((1, D_MODEL)), once((4, D_MODEL, wsh)),
                  once((3, F2)), once((1, F2)), once((D_FF, D_MODEL)), _row(ts, D_MODEL), once((1, D_MODEL))],
        out_specs=[_row(ts, F2), _row(ts, F2), _row(ts, D_MODEL), _full((1, 1)), _full((1, D_MODEL))],
        out_shape=[sd((s, F2), BF16), sd((s, F2), BF16), sd((s, D_MODEL), F32), sd((1, 1), F32), sd((1, D_MODEL), F32)],
        scratch_shapes=[pltpu.VMEM((8, F2), F32)],
        compiler_params=_cp("arbitrary", vmem=VMEM_LIMIT_MLP),
    )(x1, r2, fnw, wup4, cw, cb, wdown, tgt, fw)


def _shifted(u, hal):
    row = lax.broadcasted_iota(jnp.int32, hal.shape, 0)
    r1, r2 = pltpu.roll(u, 1, 0), pltpu.roll(u, 2, 0)
    top1 = jnp.where(row == 0, hal[7:8, :], r1[0:8, :])
    top2 = jnp.where(row == 0, hal[6:7, :], jnp.where(row == 1, hal[7:8, :], r2[0:8, :]))
    return jnp.concatenate([top1, r1[8:, :]], axis=0), jnp.concatenate([top2, r2[8:, :]], axis=0)


def _win_ext_call(win):
    blocks = win[None] if win.ndim == 2 else win
    nb, r, wb = blocks.shape
    tr = min(r, 256)

    def body(b_ref, o_ref):
        left, right, at = [], [], IN_W - ROPE
        for j in range(nb):
            blk = b_ref[j]
            cut = min(max(at - j * wb, 0), wb)
            left += [blk[:, :cut]] if cut else []
            right += [blk[:, cut:]] if cut < wb else []
        pad = lambda n: jnp.zeros((tr, n), o_ref.dtype)
        o_ref[...] = jnp.concatenate(left + [pad(KPE_LO)] + right + [pad(LANES - KPE_LO - ROPE)], -1)

    return pl.pallas_call(
        body, name="w_in_layout", grid=(r // tr,),
        in_specs=[pl.BlockSpec((nb, tr, wb), lambda i: (0, i, 0))], out_specs=_row(tr, IN_EXT),
        out_shape=jax.ShapeDtypeStruct((r, IN_EXT), win.dtype),
        compiler_params=_cp("parallel"),
    )(blocks)


def _win_grad_blocks_call(g_ext, nb):
    r = g_ext.shape[0]
    wb = IN_W // nb
    tr = min(r, 256)
    lo = IN_W - ROPE

    def body(g_ref, o_ref):
        g = g_ref[...]
        for j in range(nb):
            a, b = j * wb, (j + 1) * wb
            parts = ([g[:, a:min(b, lo)]] if a < lo else []) + ([g[:, max(a, lo) + KPE_LO:b + KPE_LO]] if b > lo else [])
            o_ref[j] = jnp.concatenate(parts, -1)

    return pl.pallas_call(
        body, name="w_in_grad_blocks", grid=(r // tr,),
        in_specs=[_row(tr, IN_EXT)], out_specs=pl.BlockSpec((nb, tr, wb), lambda i: (0, i, 0)),
        out_shape=jax.ShapeDtypeStruct((nb, r, wb), g_ext.dtype),
        compiler_params=_cp("parallel"),
    )(g_ext)


def _prep_weights(w):
    win_ext = _win_ext_call(w["w_in"])
    wuq = w["w_uq"].reshape(Q_RANK, N_HEADS, HEAD + ROPE)
    wq = jnp.concatenate([wuq, jnp.zeros((Q_RANK, N_HEADS, LANES - HEAD - ROPE), wuq.dtype)], -1).transpose(1, 0, 2)
    wukv = w["w_ukv"].reshape(KV_RANK, N_HEADS, 2 * HEAD)
    zk = jnp.zeros((KV_RANK, N_HEADS, HEAD), wukv.dtype)
    wk = jnp.concatenate([wukv[:, :, :HEAD], zk], -1).transpose(1, 0, 2)
    wv = jnp.concatenate([wukv[:, :, HEAD:], zk], -1).transpose(1, 0, 2)
    c = lambda a: a.astype(BF16)
    return dict(win=c(win_ext), wq=c(wq), wk=c(wk), wv=c(wv), wout=c(w["w_out"]))


def _prep_mlp_weights(w):
    wup = w["w_up"]
    if wup.ndim == 2:
        wup = wup.reshape(D_MODEL, 4, W_UP_SHARD).transpose(1, 0, 2)
    return dict(wup=wup.astype(BF16), wdown=w["w_down"].astype(BF16))


def _tiles(s):
    return dict(ts=min(s, 512), tr=min(s, 2048), tbf=min(s, 1024), tb=min(s, 512), t2=min(s, 256),
                tw=min(s, 2048), t1=min(s, 1024))


class _Exchanges:
    def __init__(self, w):
        self.w = w

    def mlp_weights(self, after):
        return self.w

    def mlp_grads(self, gw):
        pass

    def behind_out_bwd(self, after):
        pass

    def behind_attention(self, after):
        pass


def _forward(x, positions, tgt, w, small, ex):
    s = x.shape[0]
    t = _tiles(s)
    pw = _prep_weights(w)
    cos_r, sin_r, cos_m, sin_m = _rope_tables(positions)
    rc = _ret_consts()
    q, k, v, g, cq, ckv, mq, mk, mv, r1 = _f1_call(
        x, small["attn_norm_w"], pw["win"], small["mla_q_norm_w"], small["mla_kv_norm_w"], pw["wq"], pw["wk"], pw["wv"],
        cos_r, sin_r, cos_m, sin_m, t["ts"])
    o_ret, y_ret = _ret_fwd_call(q, k, v, g, small["ret_gn_w"], rc, t["tr"])
    y_mla, mqb = _flash_fwd_call(mq, mk, mv, t["tbf"])
    x1, r2 = _out_proj_call(x, y_ret, y_mla, pw["wout"], t["ts"])
    pw.update(_prep_mlp_weights(ex.mlp_weights(r2)))
    u, uc, dx2, loss, g_fw = _ffn_fwd_call(x1, r2, small["ffn_norm_w"], pw["wup"], w["conv_w"], small["conv_b"], pw["wdown"],
                                           tgt, small["final_norm_w"], t["ts"])
    return dict(pw=pw, tabs=(cos_r, sin_r, cos_m, sin_m), rc=rc, q=q, k=k, v=v, g=g, cq=cq, ckv=ckv, r1=r1,
                o_ret=o_ret, y_ret=y_ret, mqb=mqb, mk=mk, mv=mv, y_mla=y_mla, x1=x1, r2=r2, u=u, uc=uc,
                dx2=dx2, loss=loss, g_fw=g_fw)


def _norm_bwd(dh, xh, r, nw):
    dxn = dh * nw
    return r * (dxn - xh * jnp.mean(dxn * xh, axis=-1, keepdims=True))


def _ordered_after(body, order):
    if order is None:
        return body, [], []
    return (lambda order_ref, *refs: body(*refs)), [pl.BlockSpec(memory_space=pl.ANY)], [order]


def _zero_first(first, *refs):
    @pl.when(first)
    def _():
        for ref in refs:
            ref[...] = jnp.zeros_like(ref)


def _colsum(v):
    return jnp.sum(v, axis=0, keepdims=True)


def _dsilu(g, sg):
    return sg * (1.0 + g * (1.0 - sg))


def _ffn_bwd_call(dx2, u, uc, cw, wdown, wup4, x1, r2, fnw, ts):
    s = dx2.shape[0]
    nt = s // ts
    wsh = W_UP_SHARD
    rev = lambda i: nt - 1 - i

    def body(dx2_ref, u_ref, uc_ref, cw_ref, wd_ref, wup_ref, x_ref, r_ref, fnw_ref,
             du_ref, dx1_ref, dcw_ref, dcb_ref, dfnw_ref, dwd_hbm, carry_ref, dwd_ref, sem):
        i = pl.program_id(0)
        _zero_first(i == 0, carry_ref, dwd_ref, dcw_ref, dcb_ref, dfnw_ref)
        dxb = dx2_ref[...].astype(BF16)
        dh = jnp.zeros((ts, D_MODEL), F32)
        for j in range(2):
            gcols = slice(j * wsh, (j + 1) * wsh)
            vcols = slice(D_FF + j * wsh, D_FF + (j + 1) * wsh)
            gate, val = uc_ref[:, gcols].astype(F32), uc_ref[:, vcols].astype(F32)
            da = _dot_nt(dxb, wd_ref[gcols, :])
            sg = _sigmoid(gate)
            sl = gate * sg
            dwd_ref[gcols, :] += _dot_tn((sl * val).astype(BF16), dxb)
            for d, cols, shard in ((da * val * _dsilu(gate, sg), gcols, j), (da * sl, vcols, 2 + j)):
                d1, d2 = _shifted_up(d, carry_ref[:, cols])
                uv = u_ref[:, cols].astype(F32)
                for t, dt in enumerate((d2, d1, d)):
                    dcw_ref[t:t + 1, cols] += _colsum(dt * uv)
                dcb_ref[:, cols] += _colsum(d)
                w = cw_ref[:, cols]
                du = (w[2:3, :] * d + w[1:2, :] * d1 + w[0:1, :] * d2).astype(BF16)
                du_ref[:, cols] = du
                dh = dh + _dot_nt(du, wup_ref[shard])
                carry_ref[:, cols] = d[0:8, :]
        r = r_ref[...]
        xh = x_ref[...] * r
        dfnw_ref[...] += _colsum(dh * xh)
        dx1_ref[...] = dx2_ref[...] + _norm_bwd(dh, xh, r, fnw_ref[...])

        @pl.when(i == nt - 1)
        def _():
            cp = pltpu.make_async_copy(dwd_ref, dwd_hbm, sem)
            cp.start()
            cp.wait()

    sd = jax.ShapeDtypeStruct
    row = lambda c: pl.BlockSpec((ts, c), lambda i: (rev(i), 0))
    once = lambda shape: pl.BlockSpec(shape, lambda i: (0,) * len(shape), pipeline_mode=pl.Buffered(1))
    return pl.pallas_call(
        body, name="ffn_bwd", grid=(nt,),
        in_specs=[row(D_MODEL), row(F2), row(F2), once((3, F2)), once((D_FF, D_MODEL)), once((4, D_MODEL, wsh)),
                  row(D_MODEL), row(1), once((1, D_MODEL))],
        out_specs=[row(F2), row(D_MODEL), _full((3, F2)), _full((1, F2)), _full((1, D_MODEL)), pl.BlockSpec(memory_space=pl.ANY)],
        out_shape=[sd((s, F2), BF16), sd((s, D_MODEL), F32), sd((3, F2), F32), sd((1, F2), F32), sd((1, D_MODEL), F32),
                   sd((D_FF, D_MODEL), F32)],
        scratch_shapes=[pltpu.VMEM((8, F2), F32), pltpu.VMEM((D_FF, D_MODEL), F32), pltpu.SemaphoreType.DMA],
        compiler_params=_cp("arbitrary", vmem=VMEM_LIMIT_MLP),
    )(dx2, u, uc, cw, wdown, wup4, x1, r2, fnw)


def _shifted_up(d, hal):
    n = d.shape[0]
    row = lax.broadcasted_iota(jnp.int32, hal.shape, 0)
    r1, r2 = pltpu.roll(d, n - 1, 0), pltpu.roll(d, n - 2, 0)
    end1 = jnp.where(row == 7, hal[0:1, :], r1[n - 8:, :])
    end2 = jnp.where(row == 6, hal[0:1, :], jnp.where(row == 7, hal[1:2, :], r2[n - 8:, :]))
    return jnp.concatenate([r1[:n - 8, :], end1], axis=0), jnp.concatenate([r2[:n - 8, :], end2], axis=0)


def _dw_norm_call(x, r, nw, b, ts, tn, name):
    s, n = b.shape
    k = x.shape[1]

    def body(x_ref, r_ref, nw_ref, b_ref, dw_ref):
        _zero_first(pl.program_id(1) == 0, dw_ref)
        h = (x_ref[...] * r_ref[...] * nw_ref[...]).astype(BF16)
        dw_ref[...] += _dot_tn(h, b_ref[...])

    return pl.pallas_call(
        body, name=name, grid=(n // tn, s // ts),
        in_specs=[pl.BlockSpec((ts, k), lambda j, i: (i, 0)), pl.BlockSpec((ts, 1), lambda j, i: (i, 0)),
                  pl.BlockSpec((1, k), lambda j, i: (0, 0)), pl.BlockSpec((ts, tn), lambda j, i: (i, j))],
        out_specs=pl.BlockSpec((None, k, tn), lambda j, i: (j, 0, 0)),
        out_shape=jax.ShapeDtypeStruct((n // tn, k, tn), F32),
        compiler_params=_cp("parallel", "arbitrary"),
    )(x, r, nw, b)


def _out_bwd_call(dx1, yret, ymla, wout, ts, order=None):
    s = dx1.shape[0]

    def body(dx_ref, yr_ref, ym_ref, w_ref, dyr_ref, do_ref, dwo_ref):
        _zero_first(pl.program_id(0) == 0, dwo_ref)
        dxb = dx_ref[...].astype(BF16)
        dmix = _dot_nt(dxb, w_ref[...])
        dyr_ref[...] = dmix[:, :RET_W]
        ym = ym_ref[...]
        lane = lax.broadcasted_iota(jnp.int32, (ts, LANES), 1)
        for p in range(N_HEADS // 2):
            dom = dmix[:, RET_W + p * LANES:RET_W + (p + 1) * LANES]
            prod = dom * ym[:, p * LANES:(p + 1) * LANES].astype(F32)
            for hh in range(2):
                mine = (lane >= HEAD) if hh else (lane < HEAD)
                hi, lo = _hi_lo(jnp.sum(jnp.where(mine, prod, 0.0), axis=1, keepdims=True))
                base = jnp.where(lane < HEAD, pltpu.roll(dom, HEAD, 1) if hh else dom, 0.0)
                do_ref[2 * p + hh] = _lane_pair((ts, LANES), V_AUX, -hi, -lo, base).astype(BF16)
        dwo_ref[0:RET_W, :] += _dot_tn(yr_ref[...], dxb)
        dwo_ref[RET_W:, :] += _dot_tn(ym, dxb)

    sd = jax.ShapeDtypeStruct
    body, first_specs, first = _ordered_after(body, order)
    return pl.pallas_call(
        body, name="out_proj_bwd", grid=(s // ts,),
        in_specs=first_specs + [_row(ts, D_MODEL), _row(ts, RET_W), _row(ts, MLA_W), _full((D_MODEL, D_MODEL))],
        out_specs=[_row(ts, RET_W), _hrow(N_HEADS, ts, LANES), _full((D_MODEL, D_MODEL))],
        out_shape=[sd((s, RET_W), F32), sd((N_HEADS, s, LANES), BF16), sd((D_MODEL, D_MODEL), F32)],
        compiler_params=_cp("arbitrary"),
    )(*first, dx1, yret, ymla, wout)


def _ret_bwd_q_call(q, k, v, o, g, dy, gnw, rc, cos_r, sin_r, tr):
    s = q.shape[0]
    c = RET_CHUNK
    nc = tr // c
    ns = RET_SLABS

    def body(q_ref, k_ref, v_ref, o_ref, g_ref, dy_ref, gnw_ref, dm_ref, zeta_ref, xi_ref, cd_ref, bd_ref, cr_ref, sr_ref,
             dq_ref, dg_ref, do_ref, dgnw_ref, st_ref):
        _zero_first(pl.program_id(1) == 0, st_ref, dgnw_ref)
        bd = bd_ref[...]
        avg = bd * (1.0 / HEAD)
        chunks = [slice(ci * c, (ci + 1) * c) for ci in range(nc)]
        lanes = [slice(sl * LANES, (sl + 1) * LANES) for sl in range(ns)]
        dov = []
        for ln in lanes:
            ov = o_ref[:, ln]
            ctr = ov - _dot_hi(ov, avg)
            rs = lax.rsqrt(_dot_hi(ctr * ctr, avg) + EPS)
            oh = ctr * rs
            gg, dyv, gnw_v = g_ref[:, ln], dy_ref[:, ln], gnw_ref[:, ln]
            sg = _sigmoid(gg)
            sl = gg * sg
            dg_ref[:, ln] = (dyv * oh * gnw_v * _dsilu(gg, sg)).astype(BF16)
            dgnw_ref[:, ln] += _colsum(dyv * sl * oh)
            doh = dyv * sl * gnw_v
            dov.append((rs * (doh - _dot_hi(doh, avg) - oh * _dot_hi(doh * oh, avg))).astype(BF16))
            do_ref[:, ln] = dov[-1]
        states = _ret_states(k_ref, v_ref, zeta_ref, cd_ref, bd, st_ref, chunks, lanes, False)
        for ci, rows in enumerate(chunks):
            for sl, ln in enumerate(lanes):
                doc = dov[sl][rows, :]
                dq = (_dot_nt(doc, states[sl][ci]) * xi_ref[sl]
                      + _pair_product(doc, _stack_heads(v_ref[rows, ln]), dm_ref[sl], _stack_heads(k_ref[rows, ln])))
                dq_ref[rows, ln] = _unrope(dq, cr_ref[rows, :], sr_ref[rows, :], HEAD // 2).astype(BF16)

    specs = _ret_specs(tr, lambda i: i)
    sd = jax.ShapeDtypeStruct
    return pl.pallas_call(
        body, name="ret_bwd_q", grid=(4 // ns, s // tr),
        in_specs=[specs["slab"]] * 6 + [specs["vec"], specs["dmask"], specs["rows"], specs["rows"], specs["state"], specs["bd"],
                                        specs["tab"], specs["tab"]],
        out_specs=[specs["slab"]] * 3 + [specs["vec"]],
        out_shape=[sd((s, RET_W), BF16), sd((s, RET_W), BF16), sd((s, RET_W), BF16), sd((1, RET_W), F32)],
        scratch_shapes=[pltpu.VMEM((ns, LANES, LANES), F32)],
        compiler_params=_cp("parallel", "arbitrary"),
    )(q, k, v, o, g, dy, gnw, rc["dmask"], rc["zeta"], rc["xi"], rc["cd"], rc["bd"], cos_r, sin_r)


def _ret_bwd_kv_call(q, k, v, do, rc, cos_r, sin_r, tr):
    s = q.shape[0]
    c = RET_CHUNK
    nc = tr // c
    nt = s // tr
    ns = RET_SLABS

    def body(q_ref, k_ref, v_ref, do_ref, dm_ref, zeta_ref, xi_ref, cd_ref, bd_ref, cr_ref, sr_ref, dk_ref, dv_ref, gs_ref):
        _zero_first(pl.program_id(1) == 0, gs_ref)
        bd = bd_ref[...]
        chunks = [slice(ci * c, (ci + 1) * c) for ci in range(nc)]
        lanes = [slice(sl * LANES, (sl + 1) * LANES) for sl in range(ns)]
        states = _ret_states(q_ref, do_ref, xi_ref, cd_ref, bd, gs_ref, chunks, lanes, True)
        for ci, rows in enumerate(chunks):
            for sl, ln in enumerate(lanes):
                kc, vc = k_ref[rows, ln], v_ref[rows, ln]
                q2, do2 = _stack_heads(q_ref[rows, ln]), _stack_heads(do_ref[rows, ln])
                gb = states[sl][ci]
                dk = _dot_nt(vc, gb) * zeta_ref[sl] + _pair_product(vc, do2, dm_ref[sl], q2)
                dv = _dot(kc, gb) * zeta_ref[sl] + _pair_product(kc, q2, dm_ref[sl], do2)
                dk_ref[rows, ln] = (_unrope(dk, cr_ref[rows, :], sr_ref[rows, :], HEAD // 2) * (HEAD ** -0.5)).astype(BF16)
                dv_ref[rows, ln] = dv.astype(BF16)

    specs = _ret_specs(tr, lambda i: nt - 1 - i)
    sd = jax.ShapeDtypeStruct
    return pl.pallas_call(
        body, name="ret_bwd_kv", grid=(4 // ns, nt),
        in_specs=[specs["slab"]] * 4 + [specs["dmask"], specs["rows"], specs["rows"], specs["state"], specs["bd"],
                                        specs["tab"], specs["tab"]],
        out_specs=[specs["slab"]] * 2,
        out_shape=[sd((s, RET_W), BF16), sd((s, RET_W), BF16)],
        scratch_shapes=[pltpu.VMEM((ns, LANES, LANES), F32)],
        compiler_params=_cp("parallel", "arbitrary"),
    )(q, k, v, do, rc["dmask_t"], rc["zeta"], rc["xi"], rc["cd"], rc["bd"], cos_r, sin_r)


FLASH_BWD_HEADS = 8


def _flash_bwd_call(qb, k, v, do, tb, order=None):
    s = qb.shape[1]
    nb = s // tb
    hg = FLASH_BWD_HEADS
    pairs = [(a, b) for a in range(nb) for b in range(a, nb)]
    ki_of, qi_of = (jnp.asarray(np.array(col, np.int32)) for col in zip(*pairs))
    extra = [] if order is None else [order]

    def body(ki_ref, qi_ref, *refs):
        q_ref, k_ref, v_ref, do_ref, dk_ref, dv_ref, dq_hbm, dka_ref, dva_ref, dq_ref, sem = refs[len(extra):]
        g, p = pl.program_id(0), pl.program_id(1)
        ki, qi = ki_ref[p], qi_ref[p]
        _zero_first(p == 0, dq_ref)
        _zero_first(qi == ki, dka_ref, dva_ref)
        rows = pl.ds(pl.multiple_of(qi * tb, tb), tb)

        def step(masked):
            if masked:
                keep = lax.broadcasted_iota(jnp.int32, (tb, tb), 0) <= lax.broadcasted_iota(jnp.int32, (tb, tb), 1)
            for h in range(hg):
                st = _dot_nt(k_ref[h], q_ref[h])
                if masked:
                    st = jnp.where(keep, st, NEG)
                pt = jnp.exp2(st)
                dob = do_ref[h]
                dva_ref[h] += _dot(pt.astype(BF16), dob)
                dst = (pt * _dot_nt(v_ref[h], dob)).astype(BF16)
                dka_ref[h] += _dot(dst, q_ref[h])
                dq_ref[h, rows, :] += _dot_tn(dst, k_ref[h])

        @pl.when(qi > ki)
        def _():
            step(False)

        @pl.when(qi == ki)
        def _():
            step(True)

        @pl.when(qi == nb - 1)
        def _():
            dk_ref[...] = (dka_ref[...] * LN2).astype(BF16)
            dv_ref[...] = dva_ref[...].astype(BF16)

        @pl.when(p == len(pairs) - 1)
        def _():
            cp = pltpu.make_async_copy(dq_ref, dq_hbm.at[pl.ds(g * hg, hg)], sem)
            cp.start()
            cp.wait()

    kspec = pl.BlockSpec((hg, tb, LANES), lambda g, p, ki_ref, qi_ref: (g, ki_ref[p], 0))
    qspec = pl.BlockSpec((hg, tb, LANES), lambda g, p, ki_ref, qi_ref: (g, qi_ref[p], 0))
    hm = jax.ShapeDtypeStruct((N_HEADS, s, LANES), BF16)
    return pl.pallas_call(
        body, name="mla_flash_bwd",
        grid_spec=pltpu.PrefetchScalarGridSpec(
            num_scalar_prefetch=2, grid=(N_HEADS // hg, len(pairs)),
            in_specs=[ANY] * len(extra) + [qspec, kspec, kspec, qspec],
            out_specs=[kspec, kspec, ANY],
            scratch_shapes=[pltpu.VMEM((hg, tb, LANES), F32), pltpu.VMEM((hg, tb, LANES), F32),
                            pltpu.VMEM((hg, s, LANES), F32), pltpu.SemaphoreType.DMA]),
        out_shape=[hm, hm, jax.ShapeDtypeStruct((N_HEADS, s, LANES), F32)],
        compiler_params=_cp("arbitrary", "arbitrary"),
    )(ki_of, qi_of, *extra, qb, k, v, do)


def _mla_post_call(dq, dk, dv, cq, ckv, qnw, kvnw, wq, wk, wv, cos_m, sin_m, ts):
    s = cq.shape[0]

    def body(dq_ref, dk_ref, dv_ref, cq_ref, ckv_ref, qnw_ref, kvnw_ref, wq_ref, wk_ref, wv_ref, cm_ref, sm_ref,
             dcq_ref, dckv_ref, dkpe_ref, dwq_ref, dwk_ref, dwv_ref, dqnw_ref, dkvnw_ref):
        _zero_first(pl.program_id(0) == 0, dwq_ref, dwk_ref, dwv_ref, dqnw_ref, dkvnw_ref)
        cqv, ckvv = cq_ref[...], ckv_ref[...]
        rq, rkv = _rstd(cqv), _rstd(ckvv)
        qh_, kvh_ = cqv * rq, ckvv * rkv
        qnw_v, kvnw_v = qnw_ref[...], kvnw_ref[...]
        cqn = (qh_ * qnw_v).astype(BF16)
        ckvn = (kvh_ * kvnw_v).astype(BF16)
        cm, sm = cm_ref[...], sm_ref[...]
        dcqn = jnp.zeros((ts, Q_RANK), F32)
        dckvn = jnp.zeros((ts, KV_RANK), F32)
        dkpe = jnp.zeros((ts, LANES), F32)
        for h in range(N_HEADS):
            dqu = _unrope(dq_ref[h] * SM_SCALE, cm, sm, ROPE // 2).astype(BF16)
            dwq_ref[h] += _dot_tn(cqn, dqu)
            dcqn = dcqn + _dot_nt(dqu, wq_ref[h])
            dkb, dvb = dk_ref[h], dv_ref[h]
            dkpe = dkpe + dkb.astype(F32)
            dwk_ref[h] += _dot_tn(ckvn, dkb)
            dwv_ref[h] += _dot_tn(ckvn, dvb)
            dckvn = dckvn + _dot_nt(dkb, wk_ref[h]) + _dot_nt(dvb, wv_ref[h])
        lane = lax.broadcasted_iota(jnp.int32, (ts, LANES), 1)
        dkpe = jnp.where((lane >= KPE_LO) & (lane < KPE_LO + ROPE), dkpe, 0.0)
        dkpe_ref[...] = _unrope(dkpe, cm, sm, ROPE // 2).astype(BF16)
        dqnw_ref[...] += _colsum(dcqn * qh_)
        dkvnw_ref[...] += _colsum(dckvn * kvh_)
        dcq_ref[...] = _norm_bwd(dcqn, qh_, rq, qnw_v).astype(BF16)
        dckv_ref[...] = _norm_bwd(dckvn, kvh_, rkv, kvnw_v).astype(BF16)

    sd = jax.ShapeDtypeStruct
    hm = _hrow(N_HEADS, ts, LANES)
    return pl.pallas_call(
        body, name="mla_post", grid=(s // ts,),
        in_specs=[hm, hm, hm, _row(ts, Q_RANK), _row(ts, KV_RANK), _full((1, Q_RANK)), _full((1, KV_RANK)),
                  _full((N_HEADS, Q_RANK, LANES)), _full((N_HEADS, KV_RANK, LANES)), _full((N_HEADS, KV_RANK, LANES)),
                  _row(ts, LANES), _row(ts, LANES)],
        out_specs=[_row(ts, Q_RANK), _row(ts, KV_RANK), _row(ts, LANES),
                   _full((N_HEADS, Q_RANK, LANES)), _full((N_HEADS, KV_RANK, LANES)), _full((N_HEADS, KV_RANK, LANES)),
                   _full((1, Q_RANK)), _full((1, KV_RANK))],
        out_shape=[sd((s, Q_RANK), BF16), sd((s, KV_RANK), BF16), sd((s, LANES), BF16),
                   sd((N_HEADS, Q_RANK, LANES), F32), sd((N_HEADS, KV_RANK, LANES), F32), sd((N_HEADS, KV_RANK, LANES), F32),
                   sd((1, Q_RANK), F32), sd((1, KV_RANK), F32)],
        compiler_params=_cp("arbitrary"),
    )(dq, dk, dv, cq, ckv, qnw, kvnw, wq, wk, wv, cos_m, sin_m)


def _in_bwd_call(parts, x, r1, anw, dx1, win, ts):
    s = x.shape[0]
    widths = [p.shape[1] for p in parts]
    np_ = len(parts)

    def body(*refs):
        p_refs = refs[:np_]
        x_ref, r_ref, anw_ref, dx1_ref, w_ref, dx_ref, dw_ref, danw_ref = refs[np_:]
        _zero_first(pl.program_id(0) == 0, dw_ref, danw_ref)
        dproj = jnp.concatenate([p[...] for p in p_refs], axis=-1)
        r, anw_v = r_ref[...], anw_ref[...]
        xh = x_ref[...] * r
        dw_ref[...] += _dot_tn((xh * anw_v).astype(BF16), dproj)
        dh = _dot_nt(dproj, w_ref[...])
        danw_ref[...] += _colsum(dh * xh)
        dx_ref[...] = dx1_ref[...] + _norm_bwd(dh, xh, r, anw_v)

    sd = jax.ShapeDtypeStruct
    return pl.pallas_call(
        body, name="in_proj_bwd", grid=(s // ts,),
        in_specs=[_row(ts, w) for w in widths]
        + [_row(ts, D_MODEL), _row(ts, 1), _full((1, D_MODEL)), _row(ts, D_MODEL), _full((D_MODEL, IN_EXT))],
        out_specs=[_row(ts, D_MODEL), _full((D_MODEL, IN_EXT)), _full((1, D_MODEL))],
        out_shape=[sd((s, D_MODEL), F32), sd((D_MODEL, IN_EXT), F32), sd((1, D_MODEL), F32)],
        compiler_params=_cp("arbitrary"),
    )(*parts, x, r1, anw, dx1, win)


def _local_step(x, positions, tgt, w, small, ex=None):
    s = x.shape[0]
    t = _tiles(s)
    ex = _Exchanges(w) if ex is None else ex
    f = _forward(x, positions, tgt, w, small, ex)
    pw, rc = f["pw"], f["rc"]
    cos_r, sin_r, cos_m, sin_m = f["tabs"]
    dx2, loss, g_fw = f["dx2"], f["loss"], f["g_fw"]
    du, dx1, g_cw, g_cb, g_fnw, g_wd = _ffn_bwd_call(dx2, f["u"], f["uc"], w["conv_w"], pw["wdown"], pw["wup"],
                                                     f["x1"], f["r2"], small["ffn_norm_w"], t["t2"])
    g_wup = _dw_norm_call(f["x1"], f["r2"], small["ffn_norm_w"], du, t["tw"], F2 // 4, "dw_up")
    started = ex.mlp_grads(dict(w_up=g_wup, w_down=g_wd))
    dy_ret, do, g_wout = _out_bwd_call(dx1, f["y_ret"], f["y_mla"], pw["wout"], t["t1"], started)
    started = ex.behind_out_bwd(g_wout)
    drq, dg, do_ret, g_gnw = _ret_bwd_q_call(f["q"], f["k"], f["v"], f["o_ret"], f["g"], dy_ret, small["ret_gn_w"], rc, cos_r, sin_r, t["tr"])
    drk, drv = _ret_bwd_kv_call(f["q"], f["k"], f["v"], do_ret, rc, cos_r, sin_r, t["tr"])
    dmk, dmv, dmq = _flash_bwd_call(f["mqb"], f["mk"], f["mv"], do, t["tb"], started)
    ex.behind_attention(dmk)
    dcq, dckv, dkpe, g_wq, g_wk, g_wv, g_qnw, g_kvnw = _mla_post_call(
        dmq, dmk, dmv, f["cq"], f["ckv"], small["mla_q_norm_w"], small["mla_kv_norm_w"], pw["wq"], pw["wk"], pw["wv"], cos_m, sin_m, t["ts"])
    gx, g_win_ext, g_anw = _in_bwd_call([drq, drk, drv, dg, dcq, dckv, dkpe], x, f["r1"], small["attn_norm_w"], dx1, pw["win"], t["ts"])
    if w["w_in"].ndim == 3:
        g_win = _win_grad_blocks_call(g_win_ext, w["w_in"].shape[0])
    else:
        g_win = _win_grad_blocks_call(g_win_ext, 1)[0]
    g_wuq = g_wq.transpose(1, 0, 2)[:, :, :HEAD + ROPE].reshape(Q_RANK, N_HEADS * (HEAD + ROPE))
    g_wukv = jnp.concatenate([g_wk[:, :, :HEAD], g_wv[:, :, :HEAD]], -1).transpose(1, 0, 2).reshape(KV_RANK, 2 * MLA_W)
    gw = dict(w_in=g_win, w_uq=g_wuq, w_ukv=g_wukv, w_out=g_wout, w_up=g_wup,
              conv_w=g_cw, w_down=g_wd)
    gs = dict(attn_norm_w=g_anw, ret_gn_w=g_gnw, mla_q_norm_w=g_qnw, mla_kv_norm_w=g_kvnw, ffn_norm_w=g_fnw,
              conv_b=g_cb, final_norm_w=g_fw)
    return loss, gx, gw, gs


MESH_ID = pl.DeviceIdType.MESH
ANY = pl.BlockSpec(memory_space=pl.ANY)
VMEM_SPEC = pl.BlockSpec(memory_space=pltpu.VMEM)
N_DEV = 8
GROUP_A = (("w_in", (D_MODEL, IN_W // 4), 1), ("w_uq", (Q_RANK, 192), 1), ("w_ukv", (KV_RANK, 256), 1),
           ("w_out", (D_MODEL // 4, D_MODEL), 0))
GROUP_B = (("w_up", (D_MODEL, F2 // 4), 1), ("w_down", (D_FF // 4, D_MODEL), 0))
HBM_SPEC = pl.BlockSpec(memory_space=pltpu.HBM)
SEM_SPEC = pl.BlockSpec(memory_space=pltpu.SEMAPHORE)


def _mesh_pos():
    return lax.axis_index("x"), lax.axis_index("y"), lax.axis_index("c")


def _other_chips(x, y):
    return [(1 - x, y), (x, 1 - y), (1 - x, 1 - y)]


def _remote(src, dst, send_sems, recv_sems, k, dev):
    return pltpu.make_async_remote_copy(src_ref=src, dst_ref=dst, send_sem=send_sems.at[k], recv_sem=recv_sems.at[k],
                                        device_id=dev, device_id_type=MESH_ID)


def _gather_list_call(parts, tag):
    n = len(parts)

    def body(*refs):
        srcs, outs, (send_sems, recv_sems) = refs[:n], refs[n:2 * n], refs[2 * n:]
        x, y, c = _mesh_pos()
        sm = 2 * x + y
        chips = _other_chips(x, y)
        sib = (x, y, 1 - c)
        rc = lambda k, src, dst, dev: _remote(src, dst, send_sems, recv_sems, k, dev)
        first = [rc(7 * i + j, srcs[i].at[c], outs[i].at[sm, c], (cx, cy, c)) for i in range(n) for j, (cx, cy) in enumerate(chips)]
        own = [rc(7 * i + 6, srcs[i], outs[i].at[sm], sib) for i in range(n)]
        for cp in first + own:
            cp.start()
        passed = []
        for j, (cx, cy) in enumerate(chips):
            for i in range(n):
                land = outs[i].at[2 * cx + cy, c]
                rc(7 * i + j, srcs[i].at[c], land, (cx, cy, c)).wait_recv()
                cp = rc(7 * i + 3 + j, land, land, sib)
                cp.start()
                passed.append(cp)
        for j, (cx, cy) in enumerate(chips):
            for i in range(n):
                rc(7 * i + 3 + j, srcs[i].at[c], outs[i].at[2 * cx + cy, 1 - c], sib).wait_recv()
        for cp in own:
            cp.wait_recv()
        for cp in first + passed + own:
            cp.wait_send()

    return pl.pallas_call(
        body, name="weights_all_gather_" + tag,
        in_specs=[ANY] * n, out_specs=[ANY] * n,
        out_shape=[jax.ShapeDtypeStruct((4,) + p.shape, p.dtype) for p in parts],
        scratch_shapes=[pltpu.SemaphoreType.DMA((7 * n,)), pltpu.SemaphoreType.DMA((7 * n,))],
    )(*parts)


def _direct_gather_copies(srcs, lands, send_sems, recv_sems):
    x, y, c = _mesh_pos()
    sm = 2 * x + y
    sends, recvs = [], []
    for i, (src, land) in enumerate(zip(srcs, lands)):
        for j, (cx, cy) in enumerate(_other_chips(x, y)):
            for t in range(2):
                sends.append(_remote(src.at[c], land.at[sm, c], send_sems, recv_sems, 13 * i + 4 * j + 2 * c + t, (cx, cy, t)))
                recvs.append(_remote(src.at[t], land.at[2 * cx + cy, t], send_sems, recv_sems, 13 * i + 4 * j + 2 * t + c, (cx, cy, t)))
        sends.append(_remote(src, land.at[sm], send_sems, recv_sems, 13 * i + 12, (x, y, 1 - c)))
        recvs.append(_remote(src, land.at[sm], send_sems, recv_sems, 13 * i + 12, (x, y, 1 - c)))
    return sends, recvs


def _sibling_copies(srcs, lands, send_sems, recv_sems):
    x, y, c = _mesh_pos()
    cps = [_remote(src.at[s, 1 - c], land.at[s], send_sems, recv_sems, 4 * i + s, (x, y, 1 - c))
           for i, (src, land) in enumerate(zip(srcs, lands)) for s in range(4)]
    return cps, cps


def _chips_copies(srcs, lands, send_sems, recv_sems):
    x, y, c = _mesh_pos()
    cps = [_remote(src.at[2 * cx + cy], land.at[j], send_sems, recv_sems, 3 * i + j, (cx, cy, c))
           for i, (src, land) in enumerate(zip(srcs, lands)) for j, (cx, cy) in enumerate(_other_chips(x, y))]
    return cps, cps


def _share_copies(srcs, lands, send_sems, recv_sems):
    x, y, c = _mesh_pos()
    cps = [_remote(src, land, send_sems, recv_sems, i, (x, y, 1 - c)) for i, (src, land) in enumerate(zip(srcs, lands))]
    return cps, cps


def _exchange_call(name, copies, srcs, land_shapes, n_sems):
    n = len(srcs)

    def body(*refs):
        sends, recvs = copies(refs[:n], refs[n:2 * n], refs[2 * n], refs[2 * n + 1])
        for cp in sends:
            cp.start()
        for cp in sends:
            cp.wait_send()
        for cp in recvs:
            cp.wait_recv()

    return pl.pallas_call(
        body, name=name, in_specs=[ANY] * n, out_specs=[ANY] * n, out_shape=list(land_shapes),
        scratch_shapes=[pltpu.SemaphoreType.DMA((n_sems,)), pltpu.SemaphoreType.DMA((n_sems,))],
    )(*srcs)


def _exchange_start_call(name, copies, srcs, land_shapes, n_sems, order=None):
    n = len(srcs)
    extra = [] if order is None else [order]
    k = 2 * n + len(extra)

    def body(*refs):
        sends, _ = copies(refs[:n], refs[n:2 * n], refs[k], refs[k + 1])
        for cp in sends:
            cp.start()
        refs[-1][...] = jnp.zeros_like(refs[-1])

    hbm = lambda a: pltpu.with_memory_space_constraint(a, pltpu.HBM)
    lands = [hbm(lax.empty(sd.shape, sd.dtype)) for sd in land_shapes]
    sem = pltpu.SemaphoreType.DMA((n_sems,))
    out = pl.pallas_call(
        body, name=name,
        out_shape=(sem, sem, *[pltpu.HBM(a.shape, a.dtype) for a in list(srcs) + lands], jax.ShapeDtypeStruct((8, LANES), F32)),
        in_specs=[HBM_SPEC] * (2 * n) + [ANY] * len(extra), out_specs=(SEM_SPEC, SEM_SPEC, *[HBM_SPEC] * (2 * n), VMEM_SPEC),
        input_output_aliases={i: 2 + i for i in range(2 * n)},
        compiler_params=pltpu.CompilerParams(has_side_effects=pltpu.SideEffectType.DATAFLOW_SIDE_EFFECTING),
    )(*[hbm(a) for a in srcs], *lands, *extra)
    return out[0], out[1], out[2:2 + n], out[2 + n:2 + 2 * n], out[-1]


def _exchange_wait_call(name, copies, started, after):
    send_sems, recv_sems, srcs, lands, _ = started
    n = len(srcs)

    def body(*refs):
        sends, recvs = copies(refs[:n], refs[n:2 * n], refs[2 * n], refs[2 * n + 1])
        for cp in sends:
            cp.wait_send()
        for cp in recvs:
            cp.wait_recv()

    out = pl.pallas_call(
        body, name=name,
        out_shape=tuple(pltpu.HBM(a.shape, a.dtype) for a in list(srcs) + list(lands)),
        in_specs=[HBM_SPEC] * (2 * n) + [SEM_SPEC, SEM_SPEC, ANY], out_specs=tuple([HBM_SPEC] * (2 * n)),
        input_output_aliases={i: i for i in range(2 * n)},
        compiler_params=pltpu.CompilerParams(has_side_effects=pltpu.SideEffectType.DATAFLOW_SIDE_EFFECTING),
    )(*srcs, *lands, send_sems, recv_sems, after)
    return out[:n], out[n:]


def _rows_tile(rows, width, itemsize=4):
    limit = max(16, (3 << 20) // (width * itemsize))
    if rows <= limit:
        return rows
    return max(t for t in range(16, limit + 1, 16) if rows % t == 0)


def _sum_sibling_call(g, buf, c, name):
    _, _, rh, w = g.shape
    tile = _rows_tile(rh, w)

    def body(c_ref, g_ref, b_ref, p_ref, pb_ref):
        p = g_ref[...] + b_ref[...]
        p_ref[...] = p
        pb_ref[...] = p.astype(BF16)

    blk = pl.BlockSpec((None, tile, w), lambda s, i, c_ref: (s, i, 0))
    return pl.pallas_call(
        body, name=name,
        grid_spec=pltpu.PrefetchScalarGridSpec(
            num_scalar_prefetch=1, grid=(4, rh // tile),
            in_specs=[pl.BlockSpec((None, None, tile, w), lambda s, i, c_ref: (s, c_ref[0], i, 0)), blk],
            out_specs=[blk, blk]),
        out_shape=[jax.ShapeDtypeStruct((4, rh, w), F32), jax.ShapeDtypeStruct((4, rh, w), BF16)],
        compiler_params=_cp("parallel", "parallel"),
    )(c, g, buf)


def _sum_chips_call(p, buf, sm, name):
    _, rh, w = p.shape
    tile = _rows_tile(rh, w)

    def body(sm_ref, p_ref, b_ref, f_ref):
        f_ref[...] = ((p_ref[...] + b_ref[0].astype(F32)) + b_ref[1].astype(F32)) + b_ref[2].astype(F32)

    return pl.pallas_call(
        body, name=name,
        grid_spec=pltpu.PrefetchScalarGridSpec(
            num_scalar_prefetch=1, grid=(rh // tile,),
            in_specs=[pl.BlockSpec((None, tile, w), lambda i, sm_ref: (sm_ref[0], i, 0)),
                      pl.BlockSpec((3, tile, w), lambda i, sm_ref: (0, i, 0))],
            out_specs=pl.BlockSpec((tile, w), lambda i, sm_ref: (i, 0))),
        out_shape=jax.ShapeDtypeStruct((rh, w), F32),
        compiler_params=_cp("parallel"),
    )(sm, p, buf)


def _adamw_halves_call(w, g_mine, g_sib, c, m, v, name, transposed=False):
    if transposed:
        rows, r = w.shape
        rh = r // 2
        tile = _rows_tile(rows, rh)
        whole = pl.BlockSpec((tile, rh), lambda h, i, c_ref: (i, h))
        half = pl.BlockSpec((tile, rh), lambda h, i, c_ref: (i, 0))
        nt = rows // tile
    else:
        r, wd = w.shape
        rh = r // 2
        tile = _rows_tile(rh, wd)
        nt = rh // tile
        whole = pl.BlockSpec((tile, wd), lambda h, i, c_ref: (h * nt + i, 0))
        half = pl.BlockSpec((tile, wd), lambda h, i, c_ref: (i, 0))

    def body(c_ref, w_ref, gm_ref, gs_ref, m_ref, v_ref, g_ref, d_ref, nm_ref, nv_ref):
        gv = jnp.where(pl.program_id(0) == c_ref[0], gm_ref[...], gs_ref[...])
        g_ref[...] = gv
        nm = ADAM_B1 * m_ref[...] + (1.0 - ADAM_B1) * gv
        nv = ADAM_B2 * v_ref[...] + (1.0 - ADAM_B2) * jnp.square(gv)
        m_hat = nm / (1.0 - ADAM_B1 ** ADAM_STEP)
        v_hat = nv / (1.0 - ADAM_B2 ** ADAM_STEP)
        d_ref[...] = -ADAM_LR * (m_hat / (jnp.sqrt(v_hat) + ADAM_EPS) + ADAM_WD * w_ref[...])
        nm_ref[...] = nm
        nv_ref[...] = nv

    sd = jax.ShapeDtypeStruct(w.shape, F32)
    return pl.pallas_call(
        body, name=name,
        grid_spec=pltpu.PrefetchScalarGridSpec(
            num_scalar_prefetch=1, grid=(2, nt),
            in_specs=[whole, half, half, whole, whole], out_specs=[whole] * 4),
        out_shape=[sd, sd, sd, sd],
        compiler_params=_cp("parallel", "parallel"),
    )(c, w, g_mine, g_sib, m, v)


def _all_reduce8_call(vec, name):
    rows = vec.shape[0]

    def body(v_ref, out_ref, slots, send_sems, recv_sems):
        x, y, c = _mesh_pos()
        me = 4 * x + 2 * y + c
        slots[me] = v_ref[...]

        def rcopy(k, to_me):
            bx, by, bc = (k >> 2) & 1, (k >> 1) & 1, k & 1
            px, py, pc = (1 - x if bx else x), (1 - y if by else y), (1 - c if bc else c)
            slot = 4 * px + 2 * py + pc if to_me else me
            return pltpu.make_async_remote_copy(src_ref=v_ref, dst_ref=slots.at[slot], send_sem=send_sems.at[k - 1],
                                                recv_sem=recv_sems.at[k - 1], device_id=(px, py, pc), device_id_type=MESH_ID)

        for k in range(1, N_DEV):
            rcopy(k, False).start()
        for k in range(1, N_DEV):
            rcopy(k, True).wait_recv()
        for k in range(1, N_DEV):
            rcopy(k, False).wait_send()
        tot = slots[0]
        for d in range(1, N_DEV):
            tot = tot + slots[d]
        out_ref[...] = tot

    return pl.pallas_call(
        body, name=name,
        in_specs=[VMEM_SPEC], out_specs=VMEM_SPEC,
        out_shape=jax.ShapeDtypeStruct((rows, LANES), F32),
        scratch_shapes=[pltpu.VMEM((N_DEV, rows, LANES), F32),
                        pltpu.SemaphoreType.DMA((N_DEV - 1,)), pltpu.SemaphoreType.DMA((N_DEV - 1,))],
    )(vec)


def _adamw_call(w, g, m, v, name):
    r, c = w.shape
    rb = r if r <= 256 else (256 if r % 256 == 0 else 352)
    assert r % rb == 0

    def body(w_ref, g_ref, m_ref, v_ref, d_ref, nm_ref, nv_ref):
        gv = g_ref[...]
        nm = ADAM_B1 * m_ref[...] + (1.0 - ADAM_B1) * gv
        nv = ADAM_B2 * v_ref[...] + (1.0 - ADAM_B2) * jnp.square(gv)
        m_hat = nm / (1.0 - ADAM_B1 ** ADAM_STEP)
        v_hat = nv / (1.0 - ADAM_B2 ** ADAM_STEP)
        d_ref[...] = -ADAM_LR * (m_hat / (jnp.sqrt(v_hat) + ADAM_EPS) + ADAM_WD * w_ref[...])
        nm_ref[...] = nm
        nv_ref[...] = nv

    spec = pl.BlockSpec((rb, c), lambda i: (i, 0))
    sd = jax.ShapeDtypeStruct((r, c), F32)
    return pl.pallas_call(
        body, name=name, grid=(r // rb,),
        in_specs=[spec] * 4, out_specs=[spec] * 3, out_shape=[sd, sd, sd],
        compiler_params=_cp("parallel"),
    )(w, g, m, v)


SMALL = (("attn_norm_w", D_MODEL), ("ret_gn_w", RET_W), ("mla_q_norm_w", Q_RANK), ("mla_kv_norm_w", KV_RANK),
         ("ffn_norm_w", D_MODEL), ("conv_b", F2), ("final_norm_w", D_MODEL))
WEIGHT_ORDER = ("attn_norm_w", "w_in", "ret_gn_w", "mla_q_norm_w", "w_uq", "mla_kv_norm_w", "w_ukv", "w_out",
                "ffn_norm_w", "w_up", "conv_w", "conv_b", "w_down", "final_norm_w")


def _pad_rows(flat, rows):
    return jnp.concatenate([flat, jnp.zeros((rows * LANES - flat.shape[0],), flat.dtype)]).reshape(rows, LANES)


def kernel(x, positions, attn_norm_w, w_in, ret_gn_w, mla_q_norm_w, w_uq, mla_kv_norm_w, w_ukv, w_out, ffn_norm_w, w_up, conv_w, conv_b, w_down, final_norm_w, loss_target, m_attn_norm_w, m_w_in, m_ret_gn_w, m_mla_q_norm_w, m_w_uq, m_mla_kv_norm_w, m_w_ukv, m_w_out, m_ffn_norm_w, m_w_up, m_conv_w, m_conv_b, m_w_down, m_final_norm_w, v_attn_norm_w, v_w_in, v_ret_gn_w, v_mla_q_norm_w, v_w_uq, v_mla_kv_norm_w, v_w_ukv, v_w_out, v_ffn_norm_w, v_w_up, v_conv_w, v_conv_b, v_w_down, v_final_norm_w):
    args = dict(locals())
    cx, cy, cc = _mesh_pos()
    sm = 2 * cx + cy

    c_arr, sm_arr = cc.reshape(1).astype(jnp.int32), sm.reshape(1).astype(jnp.int32)
    sds = jax.ShapeDtypeStruct

    def my_shards(group):
        return [args[n][0].astype(BF16).reshape(2, r // 2, c) for n, (r, c), _ in group]

    def full_weights(gathered, group):
        full = {}
        for (n, (r, c), axis), got in zip(group, gathered):
            piece = got.reshape(4, r, c)
            full[n] = piece if n in ("w_up", "w_in") else (piece.transpose(1, 0, 2).reshape(r, 4 * c) if axis == 1 else piece.reshape(4 * r, c))
        return full

    def by_owner(gw, group):
        out = []
        for n, (r, c), axis in group:
            g = gw[n]
            if axis == 1 and g.ndim == 2:
                g = g.reshape(r, 4, c).transpose(1, 0, 2)
            out.append(g.reshape(4, 2, r // 2, c))
        return out

    def sibling_shapes(gs):
        return [sds((4,) + g.shape[2:], F32) for g in gs]

    def chip_sums(gs, bufs, group):
        res = [_sum_sibling_call(g, b, c_arr, "grads_sum_sibling_" + n) for g, b, (n, _, _) in zip(gs, bufs, group)]
        return [p for p, _ in res], [pb for _, pb in res]

    def chips_shapes(pbs):
        return [sds((3,) + pb.shape[1:], BF16) for pb in pbs]

    def totals(ps, lands, group, tag):
        fins = [_sum_chips_call(p, l, sm_arr, "grads_sum_chips_" + n) for p, l, (n, _, _) in zip(ps, lands, group)]
        sibs = _exchange_call("grads_rs_share_" + tag, _share_copies, fins, [sds(f.shape, F32) for f in fins], len(fins))
        return {n: (f, s) for (n, _, _), f, s in zip(group, fins, sibs)}

    class StepExchanges(_Exchanges):
        def __init__(self, order):
            shards = my_shards(GROUP_B)
            self.gather = _exchange_start_call("weights_gather_start_b", _direct_gather_copies, shards,
                                               [sds((4,) + s.shape, BF16) for s in shards], 13 * len(shards), order)
            self.red = None

        def token(self):
            return self.gather[4][0:1, 0:1]

        def mlp_weights(self, after):
            return full_weights(_exchange_wait_call("weights_gather_wait_b", _direct_gather_copies, self.gather, after)[1], GROUP_B)

        def mlp_grads(self, gw):
            gs = by_owner(gw, GROUP_B)
            self.step1 = _exchange_start_call("grads_rs_sibling_start_b", _sibling_copies, gs, sibling_shapes(gs), 4 * len(gs))
            return self.step1[4]

        def behind_out_bwd(self, after):
            gs, bufs = _exchange_wait_call("grads_rs_sibling_wait_b", _sibling_copies, self.step1, after)
            self.ps, pbs = chip_sums(gs, bufs, GROUP_B)
            self.step2 = _exchange_start_call("grads_rs_chips_start_b", _chips_copies, pbs, chips_shapes(pbs), 3 * len(pbs))
            return self.step2[4]

        def behind_attention(self, after):
            _, lands = _exchange_wait_call("grads_rs_chips_wait_b", _chips_copies, self.step2, after)
            self.red = totals(self.ps, lands, GROUP_B, "b")

    gathered = _gather_list_call(my_shards(GROUP_A) + [conv_w[0].reshape(2, 1, 3 * F2 // 8)], "a")
    full = full_weights(gathered[:-1], GROUP_A)
    ex = StepExchanges(gathered[-1])
    full["conv_w"] = gathered[-1].reshape(4, 3, F2 // 4).transpose(1, 0, 2).reshape(3, F2)
    small = {n: args[n].reshape(1, d) for n, d in SMALL}
    small["attn_norm_w"] = small["attn_norm_w"] + ex.token()

    loss, gx, gw, gs = _local_step(x[0], positions[0], loss_target[0], full, small, ex)

    ga = by_owner(gw, GROUP_A)
    bufs = _exchange_call("grads_rs_sibling_a", _sibling_copies, ga, sibling_shapes(ga), 4 * len(ga))
    ps, pbs = chip_sums(ga, bufs, GROUP_A)
    lands = _exchange_call("grads_rs_chips_a", _chips_copies, pbs, chips_shapes(pbs), 3 * len(pbs))
    halves = {**ex.red, **totals(ps, lands, GROUP_A, "a")}

    vec = jnp.concatenate([gs[n].reshape(-1) for n, _ in SMALL] + [gw["conv_w"].reshape(-1), loss.reshape(-1)])
    tot = _all_reduce8_call(_pad_rows(vec, 216), "small_all_reduce").reshape(-1)
    red, off = {}, 0
    for n, d in SMALL:
        red[n] = tot[off:off + d].reshape(1, d)
        off += d
    red["conv_w"] = lax.dynamic_slice(tot[off:off + 3 * F2].reshape(3, F2), (0, sm * (F2 // 4)), (3, F2 // 4))
    loss_tot = tot[off + 3 * F2]

    grads, deltas, new_m, new_v = [], [], [], []
    for n in WEIGHT_ORDER:
        shape = args[n].shape
        two_d = (1, shape[0]) if len(shape) == 1 else shape[-2:]
        wmv = [args[k + n].reshape(two_d) for k in ("", "m_", "v_")]
        if n in halves and two_d[1] % LANES:
            tr = lambda a: a.T
            g, d, nm, nv = map(tr, _adamw_halves_call(tr(wmv[0]), *map(tr, halves[n]), c_arr, tr(wmv[1]), tr(wmv[2]),
                                                      "adamw_" + n, transposed=True))
        elif n in halves:
            g, d, nm, nv = _adamw_halves_call(wmv[0], *halves[n], c_arr, wmv[1], wmv[2], "adamw_" + n)
        else:
            g = red[n].reshape(two_d)
            d, nm, nv = _adamw_call(wmv[0], g, wmv[1], wmv[2], "adamw_" + n)
        grads.append(g.reshape(shape))
        deltas.append(d.reshape(shape))
        new_m.append(nm.reshape(shape))
        new_v.append(nv.reshape(shape))
    return (loss_tot, gx[None], *grads, *deltas, *new_m, *new_v)
```

```python
import math

import numpy as np
import jax
import jax.numpy as jnp
from jax import lax
from jax.experimental import pallas as pl
from jax.experimental.pallas import tpu as pltpu

F32 = jnp.float32
BF16 = jnp.bfloat16

D_MODEL = 1024
N_HEADS = 8
HEAD = 64
RET_W = N_HEADS * HEAD
MLA_W = N_HEADS * HEAD
ROPE = 32
Q_RANK = 256
KV_RANK = 128
D_FF = 2816
F2 = 2 * D_FF
IN_W = 4 * RET_W + Q_RANK + KV_RANK + ROPE
IN_EXT = 4 * RET_W + Q_RANK + KV_RANK + 128
KPE_LO = 64
ROPE_BASE = 10000.0
EPS = 1e-6
RET_CHUNK = 256
SM_SCALE = (HEAD + ROPE) ** -0.5
LOG2E = math.log2(math.e)
LN2 = math.log(2.0)
NEG = -1e30
LANES = 128
VMEM_LIMIT = 56 * 1024 * 1024

ADAM_LR = 0.001
ADAM_B1 = 0.9
ADAM_B2 = 0.999
ADAM_EPS = 1e-08
ADAM_WD = 0.01
ADAM_STEP = 10


VMEM_LIMIT_MLP = 60 * 1024 * 1024


def _cp(*sem, vmem=VMEM_LIMIT):
    return pltpu.CompilerParams(dimension_semantics=sem, vmem_limit_bytes=vmem)


def _full(shape):
    n = len(shape)
    return pl.BlockSpec(tuple(shape), lambda *_: (0,) * n)


def _row(ts, c):
    return pl.BlockSpec((ts, c), lambda i: (i, 0))


def _hrow(h, ts, c):
    return pl.BlockSpec((h, ts, c), lambda i: (0, i, 0))


def _dot(a, b):
    return jnp.dot(a, b, preferred_element_type=F32)


def _dot_nt(a, b):
    return lax.dot_general(a, b, (((1,), (1,)), ((), ())), preferred_element_type=F32)


def _dot_tn(a, b):
    return lax.dot_general(a, b, (((0,), (0,)), ((), ())), preferred_element_type=F32)


def _dot_hi(a, b):
    hi = a.astype(BF16)
    lo = (a - hi.astype(F32)).astype(BF16)
    bb = b.astype(BF16)
    return _dot(hi, bb) + _dot(lo, bb)


def _rot_half(x, half):
    w = x.shape[-1]
    lane = lax.broadcasted_iota(jnp.int32, x.shape, x.ndim - 1)
    first = (lane % (2 * half)) < half
    return jnp.where(first, -pltpu.roll(x, w - half, x.ndim - 1), pltpu.roll(x, half, x.ndim - 1))


def _rope(x, cos, sin, half):
    return x * cos + _rot_half(x, half) * sin


def _unrope(dy, cos, sin, half):
    return dy * cos - _rot_half(dy, half) * sin


def _sigmoid(g):
    return 0.5 * jnp.tanh(0.5 * g) + 0.5


def _silu(g):
    return g * _sigmoid(g)


def _rstd(x):
    return lax.rsqrt(jnp.mean(x * x, axis=-1, keepdims=True) + EPS)


def _rope_tables(positions):
    s = positions.shape[0]
    hr, hm = HEAD // 2, ROPE // 2
    pos = positions.astype(F32)[None, :]
    inv_r = ROPE_BASE ** (-jnp.arange(0, HEAD, 2, dtype=F32) / HEAD)
    inv_m = ROPE_BASE ** (-jnp.arange(0, ROPE, 2, dtype=F32) / ROPE)
    ang = jnp.concatenate([inv_r, inv_m])[:, None] * pos
    packed = jnp.concatenate([jnp.cos(ang), jnp.sin(ang), jnp.zeros((LANES - 2 * (hr + hm), s), F32)], 0)
    tx = min(s, 1024)

    def spread(t, lane, pieces, fill):
        out = jnp.full(t.shape, fill, F32)
        for lo, src, width in pieces:
            moved = t if lo == src else pltpu.roll(t, (lo - src) % LANES, 1)
            out = jnp.where((lane >= lo) & (lane < lo + width), moved, out)
        return out

    def body(p_ref, cr_ref, sr_ref, cm_ref, sm_ref):
        t = p_ref[...].T
        lane = lax.broadcasted_iota(jnp.int32, t.shape, 1)
        cr_ref[...] = spread(t, lane, [(j * hr, 0, hr) for j in range(LANES // hr)], 0.0)
        sr_ref[...] = spread(t, lane, [(j * hr, hr + hm, hr) for j in range(LANES // hr)], 0.0)
        cm_ref[...] = spread(t, lane, [(KPE_LO, hr, hm), (KPE_LO + hm, hr, hm)], 1.0)
        sm_ref[...] = spread(t, lane, [(KPE_LO, 2 * hr + hm, hm), (KPE_LO + hm, 2 * hr + hm, hm)], 0.0)

    tab = jax.ShapeDtypeStruct((s, LANES), F32)
    return pl.pallas_call(
        body, name="rope_tables", grid=(s // tx,),
        in_specs=[pl.BlockSpec((LANES, tx), lambda i: (0, i))],
        out_specs=[_row(tx, LANES)] * 4, out_shape=[tab] * 4,
        compiler_params=_cp("parallel"),
    )(packed)


def _ret_consts():
    c = RET_CHUNK
    lg = np.log1p(-np.power(2.0, -5.0 - np.arange(N_HEADS, dtype=np.float64)))
    idx = np.arange(c, dtype=np.float64)
    diff = idx[:, None] - idx[None, :]
    lane_head = np.arange(LANES) // HEAD
    dmask = np.zeros((4, 2, c, c))
    zeta = np.zeros((4, c, LANES))
    xi = np.zeros((4, c, LANES))
    cd = np.zeros((4, LANES, LANES))
    bd = (lane_head[:, None] == lane_head[None, :]).astype(np.float64)
    for j in range(4):
        for hh in range(2):
            dmask[j, hh] = np.where(diff >= 0, np.exp(lg[2 * j + hh] * np.maximum(diff, 0.0)), 0.0)
        lgl = lg[2 * j + lane_head]
        zeta[j] = np.exp(lgl[None, :] * (c - 1.0 - idx[:, None]))
        xi[j] = np.exp(lgl[None, :] * (idx[:, None] + 1.0))
        cd[j] = np.exp(lgl * c)[:, None] * bd
    f = lambda a: jnp.asarray(a, F32)
    side = lambda d: np.concatenate([d[:, 0], d[:, 1]], axis=-1)
    return dict(dmask=f(side(dmask)), dmask_t=f(side(np.swapaxes(dmask, 2, 3))), zeta=f(zeta), xi=f(xi), cd=f(cd), bd=f(bd))


def _f1_call(x, anw, win, qnw, kvnw, wq, wk, wv, cos_r, sin_r, cos_m, sin_m, ts):
    s = x.shape[0]

    def body(x_ref, anw_ref, w_ref, qnw_ref, kvnw_ref, wq_ref, wk_ref, wv_ref, cr_ref, sr_ref, cm_ref, sm_ref,
             q_ref, k_ref, v_ref, g_ref, cq_ref, ckv_ref, mq_ref, mk_ref, mv_ref, r_ref):
        xv = x_ref[...]
        r = _rstd(xv)
        r_ref[...] = r
        h = (xv * r * anw_ref[...]).astype(BF16)
        cr, sr = cr_ref[...], sr_ref[...]
        qk = _dot(h, w_ref[:, 0:2 * RET_W])
        for j in range(4):
            sl = slice(j * LANES, (j + 1) * LANES)
            q_ref[:, sl] = _rope(qk[:, sl], cr, sr, HEAD // 2).astype(BF16)
            kk = qk[:, RET_W + j * LANES:RET_W + (j + 1) * LANES]
            k_ref[:, sl] = (_rope(kk, cr, sr, HEAD // 2) * (HEAD ** -0.5)).astype(BF16)
        v_ref[...] = _dot(h, w_ref[:, 2 * RET_W:3 * RET_W]).astype(BF16)
        g_ref[...] = _dot(h, w_ref[:, 3 * RET_W:4 * RET_W])
        o = 4 * RET_W
        cqv = _dot(h, w_ref[:, o:o + Q_RANK])
        ckvv = _dot(h, w_ref[:, o + Q_RANK:o + Q_RANK + KV_RANK])
        cq_ref[...] = cqv
        ckv_ref[...] = ckvv
        cm, sm = cm_ref[...], sm_ref[...]
        kp = _rope(_dot(h, w_ref[:, o + Q_RANK + KV_RANK:IN_EXT]), cm, sm, ROPE // 2)
        kp = _lane_pair((ts, LANES), QK_AUX, -1.0, -1.0, kp)
        cqn = (cqv * _rstd(cqv) * qnw_ref[...]).astype(BF16)
        ckvn = (ckvv * _rstd(ckvv) * kvnw_ref[...]).astype(BF16)
        for hd in range(N_HEADS):
            qh = _rope(_dot(cqn, wq_ref[hd]), cm, sm, ROPE // 2)
            mq_ref[hd] = (qh * (SM_SCALE * LOG2E)).astype(BF16)
            mk_ref[hd] = (_dot(ckvn, wk_ref[hd]) + kp).astype(BF16)
            mv_ref[hd] = _lane_pair((ts, LANES), V_AUX, 1.0, 1.0, _dot(ckvn, wv_ref[hd])).astype(BF16)

    sd = jax.ShapeDtypeStruct
    hm = sd((N_HEADS, s, LANES), BF16)
    return pl.pallas_call(
        body, name="f1_in_proj", grid=(s // ts,),
        in_specs=[_row(ts, D_MODEL), _full((1, D_MODEL)), _full((D_MODEL, IN_EXT)), _full((1, Q_RANK)), _full((1, KV_RANK)),
                  _full((N_HEADS, Q_RANK, LANES)), _full((N_HEADS, KV_RANK, LANES)), _full((N_HEADS, KV_RANK, LANES)),
                  _row(ts, LANES), _row(ts, LANES), _row(ts, LANES), _row(ts, LANES)],
        out_specs=[_row(ts, RET_W), _row(ts, RET_W), _row(ts, RET_W), _row(ts, RET_W),
                   _row(ts, Q_RANK), _row(ts, KV_RANK)] + [_hrow(N_HEADS, ts, LANES)] * 3 + [_row(ts, 1)],
        out_shape=[sd((s, RET_W), BF16), sd((s, RET_W), BF16), sd((s, RET_W), BF16), sd((s, RET_W), F32),
                   sd((s, Q_RANK), F32), sd((s, KV_RANK), F32), hm, hm, hm, sd((s, 1), F32)],
        compiler_params=_cp("parallel"),
    )(x, anw, win, qnw, kvnw, wq, wk, wv, cos_r, sin_r, cos_m, sin_m)


def _stack_heads(a):
    lo = lax.broadcasted_iota(jnp.int32, a.shape, 1) < HEAD
    zero = jnp.zeros_like(a)
    return jnp.concatenate([jnp.where(lo, a, zero), jnp.where(lo, zero, a)], axis=0)


def _pair_product(a, b2, decay2, w2):
    return _dot((_dot_nt(a, b2) * decay2).astype(BF16), w2)


RET_SLABS = 2


def _ret_specs(tr, tile_of):
    c, ns = RET_CHUNK, RET_SLABS
    return dict(
        slab=pl.BlockSpec((tr, ns * LANES), lambda j, i: (tile_of(i), j)),
        tab=pl.BlockSpec((tr, LANES), lambda j, i: (tile_of(i), 0)),
        vec=pl.BlockSpec((1, ns * LANES), lambda j, i: (0, j)),
        dmask=pl.BlockSpec((ns, c, 2 * c), lambda j, i: (j, 0, 0)),
        rows=pl.BlockSpec((ns, c, LANES), lambda j, i: (j, 0, 0)),
        state=pl.BlockSpec((ns, LANES, LANES), lambda j, i: (j, 0, 0)),
        bd=pl.BlockSpec((LANES, LANES), lambda j, i: (0, 0)))


def _ret_states(a_ref, b_ref, scale_ref, cd_ref, bd, st_ref, chunks, lanes, reverse):
    nc = len(chunks)
    contrib = [[_dot_tn((a_ref[rows, ln].astype(F32) * scale_ref[sl]).astype(BF16), b_ref[rows, ln]) * bd for rows in chunks]
               for sl, ln in enumerate(lanes)]
    states = []
    for sl in range(len(lanes)):
        st, seen = st_ref[sl], [None] * nc
        for ci in (reversed(range(nc)) if reverse else range(nc)):
            seen[ci] = st.astype(BF16)
            st = st * cd_ref[sl] + contrib[sl][ci]
        st_ref[sl] = st
        states.append(seen)
    return states


def _ret_fwd_call(q, k, v, g, gnw, rc, tr):
    s = q.shape[0]
    c = RET_CHUNK
    nc = tr // c
    ns = RET_SLABS

    def body(q_ref, k_ref, v_ref, g_ref, gnw_ref, dm_ref, zeta_ref, xi_ref, cd_ref, bd_ref, o_ref, y_ref, st_ref):
        @pl.when(pl.program_id(1) == 0)
        def _():
            st_ref[...] = jnp.zeros_like(st_ref)

        bd = bd_ref[...]
        chunks = [slice(ci * c, (ci + 1) * c) for ci in range(nc)]
        lanes = [slice(sl * LANES, (sl + 1) * LANES) for sl in range(ns)]
        states = _ret_states(k_ref, v_ref, zeta_ref, cd_ref, bd, st_ref, chunks, lanes, False)
        for ci, rows in enumerate(chunks):
            for sl, ln in enumerate(lanes):
                qc = q_ref[rows, ln]
                o_ref[rows, ln] = (_dot(qc, states[sl][ci]) * xi_ref[sl]
                                   + _pair_product(qc, _stack_heads(k_ref[rows, ln]), dm_ref[sl], _stack_heads(v_ref[rows, ln])))
        avg = bd * (1.0 / HEAD)
        for ln in lanes:
            o = o_ref[:, ln]
            ctr = o - _dot_hi(o, avg)
            var = _dot_hi(ctr * ctr, avg)
            y_ref[:, ln] = (_silu(g_ref[:, ln]) * (ctr * lax.rsqrt(var + EPS) * gnw_ref[:, ln])).astype(BF16)

    specs = _ret_specs(tr, lambda i: i)
    sd = jax.ShapeDtypeStruct
    return pl.pallas_call(
        body, name="ret_fwd", grid=(4 // ns, s // tr),
        in_specs=[specs["slab"]] * 4 + [specs["vec"], specs["dmask"], specs["rows"], specs["rows"], specs["state"], specs["bd"]],
        out_specs=[specs["slab"]] * 2,
        out_shape=[sd((s, RET_W), F32), sd((s, RET_W), BF16)],
        scratch_shapes=[pltpu.VMEM((ns, LANES, LANES), F32)],
        compiler_params=_cp("parallel", "arbitrary"),
    )(q, k, v, g, gnw, rc["dmask"], rc["zeta"], rc["xi"], rc["cd"], rc["bd"])


QK_AUX = HEAD + ROPE
V_AUX = HEAD


def _lane_pair(shape, lo, a, b, rest):
    lane = lax.broadcasted_iota(jnp.int32, shape, len(shape) - 1)
    return jnp.where(lane == lo, a, jnp.where(lane == lo + 1, b, rest))


def _hi_lo(v):
    hi = v.astype(BF16).astype(F32)
    return hi, v - hi


def _flash_fwd_call(q, k, v, tb):
    s = q.shape[1]
    nb = s // tb
    pairs = [(a, b) for a in range(nb) for b in range(a + 1)]
    qi_of, ki_of = (jnp.asarray(np.array(col, np.int32)) for col in zip(*pairs))

    def body(qi_ref, ki_ref, q_ref, k_ref, v_ref, o_ref, qb_ref, m_ref, acc_ref):
        qi, ki = qi_ref[pl.program_id(0)], ki_ref[pl.program_id(0)]

        @pl.when(ki == 0)
        def _():
            m_ref[...] = jnp.full_like(m_ref, NEG)
            acc_ref[...] = jnp.zeros_like(acc_ref)

        def step(masked):
            if masked:
                keep = lax.broadcasted_iota(jnp.int32, (tb, tb), 1) <= lax.broadcasted_iota(jnp.int32, (tb, tb), 0)
            def finish(h, pe, alpha):
                acc_ref[h] = acc_ref[h] * alpha + _dot(pe, v_ref[h])

            nxt, pending = _dot_nt(q_ref[0], k_ref[0]), None
            for h in range(N_HEADS):
                sc = nxt
                if h + 1 < N_HEADS:
                    nxt = _dot_nt(q_ref[h + 1], k_ref[h + 1])
                if masked:
                    sc = jnp.where(keep, sc, NEG)
                m_prev = m_ref[h]
                m_new = jnp.maximum(m_prev, jnp.max(sc, axis=1, keepdims=True))
                pe = jnp.exp2(sc - jnp.tile(m_new, (1, tb // LANES))).astype(BF16)
                m_ref[h] = m_new
                if pending is not None:
                    finish(*pending)
                pending = (h, pe, jnp.exp2(m_prev - m_new))
            finish(*pending)

        @pl.when(ki < qi)
        def _():
            step(False)

        @pl.when(ki == qi)
        def _():
            step(True)
            lane = lax.broadcasted_iota(jnp.int32, (tb, LANES), 1)
            for p in range(N_HEADS // 2):
                outs = []
                for h in (2 * p, 2 * p + 1):
                    acc = acc_ref[h]
                    l = acc[:, V_AUX:V_AUX + 1]
                    outs.append(acc * (1.0 / l))
                    hi, lo = _hi_lo(m_ref[h][:, 0:1] + jnp.log(l) * LOG2E)
                    qb_ref[h] = _lane_pair((tb, LANES), QK_AUX, hi, lo, q_ref[h].astype(F32)).astype(BF16)
                o_ref[:, p * LANES:(p + 1) * LANES] = jnp.where(lane < HEAD, outs[0], pltpu.roll(outs[1], HEAD, 1)).astype(BF16)

    sd = jax.ShapeDtypeStruct
    qspec = pl.BlockSpec((N_HEADS, tb, LANES), lambda p, qi_ref, ki_ref: (0, qi_ref[p], 0))
    kspec = pl.BlockSpec((N_HEADS, tb, LANES), lambda p, qi_ref, ki_ref: (0, ki_ref[p], 0))
    return pl.pallas_call(
        body, name="mla_flash_fwd",
        grid_spec=pltpu.PrefetchScalarGridSpec(
            num_scalar_prefetch=2, grid=(len(pairs),),
            in_specs=[qspec, kspec, kspec],
            out_specs=[pl.BlockSpec((tb, MLA_W), lambda p, qi_ref, ki_ref: (qi_ref[p], 0)), qspec],
            scratch_shapes=[pltpu.VMEM((N_HEADS, tb, LANES), F32), pltpu.VMEM((N_HEADS, tb, LANES), F32)]),
        out_shape=[sd((s, MLA_W), BF16), sd((N_HEADS, s, LANES), BF16)],
        compiler_params=_cp("arbitrary"),
    )(qi_of, ki_of, q, k, v)


def _out_proj_call(x, yret, ymla, wout, ts):
    s = x.shape[0]

    def body(x_ref, yr_ref, ym_ref, w_ref, x1_ref, r_ref):
        x1 = x_ref[...] + _dot(yr_ref[...], w_ref[0:RET_W, :]) + _dot(ym_ref[...], w_ref[RET_W:, :])
        x1_ref[...] = x1
        r_ref[...] = _rstd(x1)

    sd = jax.ShapeDtypeStruct
    return pl.pallas_call(
        body, name="out_proj", grid=(s // ts,),
        in_specs=[_row(ts, D_MODEL), _row(ts, RET_W), _row(ts, MLA_W), _full((D_MODEL, D_MODEL))],
        out_specs=[_row(ts, D_MODEL), _row(ts, 1)],
        out_shape=[sd((s, D_MODEL), F32), sd((s, 1), F32)],
        compiler_params=_cp("parallel"),
    )(x, yret, ymla, wout)


W_UP_SHARD = F2 // 4


def _ffn_fwd_call(x1, r2, fnw, wup4, cw, cb, wdown, tgt, fw, ts):
    s = x1.shape[0]
    wsh = W_UP_SHARD

    def body(x_ref, r_ref, fnw_ref, wup_ref, cw_ref, cb_ref, wd_ref, t_ref, fw_ref,
             u_ref, uc_ref, dx2_ref, loss_ref, gfw_ref, carry_ref):
        _zero_first(pl.program_id(0) == 0, carry_ref, loss_ref, gfw_ref)
        xv = x_ref[...]
        h = (xv * r_ref[...] * fnw_ref[...]).astype(BF16)
        conv = []
        for j in range(4):
            cols = slice(j * wsh, (j + 1) * wsh)
            ub = _dot(h, wup_ref[j]).astype(BF16)
            u_ref[:, cols] = ub
            u = ub.astype(F32)
            u1, u2 = _shifted(u, carry_ref[:, cols])
            w = cw_ref[:, cols]
            cb16 = (cb_ref[:, cols] + w[0:1, :] * u2 + w[1:2, :] * u1 + w[2:3, :] * u).astype(BF16)
            uc_ref[:, cols] = cb16
            conv.append(cb16.astype(F32))
            carry_ref[:, cols] = u[ts - 8:, :]
        acc = xv
        for j in range(2):
            a = (_silu(conv[j]) * conv[j + 2]).astype(BF16)
            acc = acc + _dot(a, wd_ref[j * wsh:(j + 1) * wsh, :])
        r = _rstd(acc)
        xh = acc * r
        fwv = fw_ref[...]
        e = xh * fwv - t_ref[...]
        loss_ref[...] += (0.5 / D_MODEL) * _colsum(jnp.sum(e * e, axis=1, keepdims=True))
        dy = e * (1.0 / D_MODEL)
        gfw_ref[...] += _colsum(dy * xh)
        dx2_ref[...] = _norm_bwd(dy, xh, r, fwv)

    sd = jax.ShapeDtypeStruct
    once = lambda shape: pl.BlockSpec(shape, lambda i: (0,) * len(shape), pipeline_mode=pl.Buffered(1))
    return pl.pallas_call(
        body, name="ffn_fwd_loss", grid=(s // ts,),
        in_specs=[_row(ts, D_MODEL), _row(ts, 1), once((1, D_MODEL)), once((4, D_MODEL, wsh)),
                  once((3, F2)), once((1, F2)), once((D_FF, D_MODEL)), _row(ts, D_MODEL), once((1, D_MODEL))],
        out_specs=[_row(ts, F2), _row(ts, F2), _row(ts, D_MODEL), _full((1, 1)), _full((1, D_MODEL))],
        out_shape=[sd((s, F2), BF16), sd((s, F2), BF16), sd((s, D_MODEL), F32), sd((1, 1), F32), sd((1, D_MODEL), F32)],
        scratch_shapes=[pltpu.VMEM((8, F2), F32)],
        compiler_params=_cp("arbitrary", vmem=VMEM_LIMIT_MLP),
    )(x1, r2, fnw, wup4, cw, cb, wdown, tgt, fw)


def _shifted(u, hal):
    row = lax.broadcasted_iota(jnp.int32, hal.shape, 0)
    r1, r2 = pltpu.roll(u, 1, 0), pltpu.roll(u, 2, 0)
    top1 = jnp.where(row == 0, hal[7:8, :], r1[0:8, :])
    top2 = jnp.where(row == 0, hal[6:7, :], jnp.where(row == 1, hal[7:8, :], r2[0:8, :]))
    return jnp.concatenate([top1, r1[8:, :]], axis=0), jnp.concatenate([top2, r2[8:, :]], axis=0)


def _win_ext_call(win):
    blocks = win[None] if win.ndim == 2 else win
    nb, r, wb = blocks.shape
    tr = min(r, 256)

    def body(b_ref, o_ref):
        left, right, at = [], [], IN_W - ROPE
        for j in range(nb):
            blk = b_ref[j]
            cut = min(max(at - j * wb, 0), wb)
            left += [blk[:, :cut]] if cut else []
            right += [blk[:, cut:]] if cut < wb else []
        pad = lambda n: jnp.zeros((tr, n), o_ref.dtype)
        o_ref[...] = jnp.concatenate(left + [pad(KPE_LO)] + right + [pad(LANES - KPE_LO - ROPE)], -1)

    return pl.pallas_call(
        body, name="w_in_layout", grid=(r // tr,),
        in_specs=[pl.BlockSpec((nb, tr, wb), lambda i: (0, i, 0))], out_specs=_row(tr, IN_EXT),
        out_shape=jax.ShapeDtypeStruct((r, IN_EXT), win.dtype),
        compiler_params=_cp("parallel"),
    )(blocks)


def _win_grad_blocks_call(g_ext, nb):
    r = g_ext.shape[0]
    wb = IN_W // nb
    tr = min(r, 256)
    lo = IN_W - ROPE

    def body(g_ref, o_ref):
        g = g_ref[...]
        for j in range(nb):
            a, b = j * wb, (j + 1) * wb
            parts = ([g[:, a:min(b, lo)]] if a < lo else []) + ([g[:, max(a, lo) + KPE_LO:b + KPE_LO]] if b > lo else [])
            o_ref[j] = jnp.concatenate(parts, -1)

    return pl.pallas_call(
        body, name="w_in_grad_blocks", grid=(r // tr,),
        in_specs=[_row(tr, IN_EXT)], out_specs=pl.BlockSpec((nb, tr, wb), lambda i: (0, i, 0)),
        out_shape=jax.ShapeDtypeStruct((nb, r, wb), g_ext.dtype),
        compiler_params=_cp("parallel"),
    )(g_ext)


def _prep_weights(w):
    win_ext = _win_ext_call(w["w_in"])
    wuq = w["w_uq"].reshape(Q_RANK, N_HEADS, HEAD + ROPE)
    wq = jnp.concatenate([wuq, jnp.zeros((Q_RANK, N_HEADS, LANES - HEAD - ROPE), wuq.dtype)], -1).transpose(1, 0, 2)
    wukv = w["w_ukv"].reshape(KV_RANK, N_HEADS, 2 * HEAD)
    zk = jnp.zeros((KV_RANK, N_HEADS, HEAD), wukv.dtype)
    wk = jnp.concatenate([wukv[:, :, :HEAD], zk], -1).transpose(1, 0, 2)
    wv = jnp.concatenate([wukv[:, :, HEAD:], zk], -1).transpose(1, 0, 2)
    c = lambda a: a.astype(BF16)
    return dict(win=c(win_ext), wq=c(wq), wk=c(wk), wv=c(wv), wout=c(w["w_out"]))


def _prep_mlp_weights(w):
    wup = w["w_up"]
    if wup.ndim == 2:
        wup = wup.reshape(D_MODEL, 4, W_UP_SHARD).transpose(1, 0, 2)
    return dict(wup=wup.astype(BF16), wdown=w["w_down"].astype(BF16))


def _tiles(s):
    return dict(ts=min(s, 512), tr=min(s, 2048), tbf=min(s, 1024), tb=min(s, 512), t2=min(s, 256),
                tw=min(s, 2048), t1=min(s, 1024))


class _Exchanges:
    def __init__(self, w):
        self.w = w

    def mlp_weights(self, after):
        return self.w

    def mlp_grads(self, gw):
        pass

    def behind_out_bwd(self, after):
        pass

    def behind_attention(self, after):
        pass


def _forward(x, positions, tgt, w, small, ex):
    s = x.shape[0]
    t = _tiles(s)
    pw = _prep_weights(w)
    cos_r, sin_r, cos_m, sin_m = _rope_tables(positions)
    rc = _ret_consts()
    q, k, v, g, cq, ckv, mq, mk, mv, r1 = _f1_call(
        x, small["attn_norm_w"], pw["win"], small["mla_q_norm_w"], small["mla_kv_norm_w"], pw["wq"], pw["wk"], pw["wv"],
        cos_r, sin_r, cos_m, sin_m, t["ts"])
    o_ret, y_ret = _ret_fwd_call(q, k, v, g, small["ret_gn_w"], rc, t["tr"])
    y_mla, mqb = _flash_fwd_call(mq, mk, mv, t["tbf"])
    x1, r2 = _out_proj_call(x, y_ret, y_mla, pw["wout"], t["ts"])
    pw.update(_prep_mlp_weights(ex.mlp_weights(r2)))
    u, uc, dx2, loss, g_fw = _ffn_fwd_call(x1, r2, small["ffn_norm_w"], pw["wup"], w["conv_w"], small["conv_b"], pw["wdown"],
                                           tgt, small["final_norm_w"], t["ts"])
    return dict(pw=pw, tabs=(cos_r, sin_r, cos_m, sin_m), rc=rc, q=q, k=k, v=v, g=g, cq=cq, ckv=ckv, r1=r1,
                o_ret=o_ret, y_ret=y_ret, mqb=mqb, mk=mk, mv=mv, y_mla=y_mla, x1=x1, r2=r2, u=u, uc=uc,
                dx2=dx2, loss=loss, g_fw=g_fw)


def _norm_bwd(dh, xh, r, nw):
    dxn = dh * nw
    return r * (dxn - xh * jnp.mean(dxn * xh, axis=-1, keepdims=True))


def _ordered_after(body, order):
    if order is None:
        return body, [], []
    return (lambda order_ref, *refs: body(*refs)), [pl.BlockSpec(memory_space=pl.ANY)], [order]


def _zero_first(first, *refs):
    @pl.when(first)
    def _():
        for ref in refs:
            ref[...] = jnp.zeros_like(ref)


def _colsum(v):
    return jnp.sum(v, axis=0, keepdims=True)


def _dsilu(g, sg):
    return sg * (1.0 + g * (1.0 - sg))


def _ffn_bwd_call(dx2, u, uc, cw, wdown, wup4, x1, r2, fnw, ts):
    s = dx2.shape[0]
    nt = s // ts
    wsh = W_UP_SHARD
    rev = lambda i: nt - 1 - i

    def body(dx2_ref, u_ref, uc_ref, cw_ref, wd_ref, wup_ref, x_ref, r_ref, fnw_ref,
             du_ref, dx1_ref, dcw_ref, dcb_ref, dfnw_ref, dwd_hbm, carry_ref, dwd_ref, sem):
        i = pl.program_id(0)
        _zero_first(i == 0, carry_ref, dwd_ref, dcw_ref, dcb_ref, dfnw_ref)
        dxb = dx2_ref[...].astype(BF16)
        dh = jnp.zeros((ts, D_MODEL), F32)
        for j in range(2):
            gcols = slice(j * wsh, (j + 1) * wsh)
            vcols = slice(D_FF + j * wsh, D_FF + (j + 1) * wsh)
            gate, val = uc_ref[:, gcols].astype(F32), uc_ref[:, vcols].astype(F32)
            da = _dot_nt(dxb, wd_ref[gcols, :])
            sg = _sigmoid(gate)
            sl = gate * sg
            dwd_ref[gcols, :] += _dot_tn((sl * val).astype(BF16), dxb)
            for d, cols, shard in ((da * val * _dsilu(gate, sg), gcols, j), (da * sl, vcols, 2 + j)):
                d1, d2 = _shifted_up(d, carry_ref[:, cols])
                uv = u_ref[:, cols].astype(F32)
                for t, dt in enumerate((d2, d1, d)):
                    dcw_ref[t:t + 1, cols] += _colsum(dt * uv)
                dcb_ref[:, cols] += _colsum(d)
                w = cw_ref[:, cols]
                du = (w[2:3, :] * d + w[1:2, :] * d1 + w[0:1, :] * d2).astype(BF16)
                du_ref[:, cols] = du
                dh = dh + _dot_nt(du, wup_ref[shard])
                carry_ref[:, cols] = d[0:8, :]
        r = r_ref[...]
        xh = x_ref[...] * r
        dfnw_ref[...] += _colsum(dh * xh)
        dx1_ref[...] = dx2_ref[...] + _norm_bwd(dh, xh, r, fnw_ref[...])

        @pl.when(i == nt - 1)
        def _():
            cp = pltpu.make_async_copy(dwd_ref, dwd_hbm, sem)
            cp.start()
            cp.wait()

    sd = jax.ShapeDtypeStruct
    row = lambda c: pl.BlockSpec((ts, c), lambda i: (rev(i), 0))
    once = lambda shape: pl.BlockSpec(shape, lambda i: (0,) * len(shape), pipeline_mode=pl.Buffered(1))
    return pl.pallas_call(
        body, name="ffn_bwd", grid=(nt,),
        in_specs=[row(D_MODEL), row(F2), row(F2), once((3, F2)), once((D_FF, D_MODEL)), once((4, D_MODEL, wsh)),
                  row(D_MODEL), row(1), once((1, D_MODEL))],
        out_specs=[row(F2), row(D_MODEL), _full((3, F2)), _full((1, F2)), _full((1, D_MODEL)), pl.BlockSpec(memory_space=pl.ANY)],
        out_shape=[sd((s, F2), BF16), sd((s, D_MODEL), F32), sd((3, F2), F32), sd((1, F2), F32), sd((1, D_MODEL), F32),
                   sd((D_FF, D_MODEL), F32)],
        scratch_shapes=[pltpu.VMEM((8, F2), F32), pltpu.VMEM((D_FF, D_MODEL), F32), pltpu.SemaphoreType.DMA],
        compiler_params=_cp("arbitrary", vmem=VMEM_LIMIT_MLP),
    )(dx2, u, uc, cw, wdown, wup4, x1, r2, fnw)


def _shifted_up(d, hal):
    n = d.shape[0]
    row = lax.broadcasted_iota(jnp.int32, hal.shape, 0)
    r1, r2 = pltpu.roll(d, n - 1, 0), pltpu.roll(d, n - 2, 0)
    end1 = jnp.where(row == 7, hal[0:1, :], r1[n - 8:, :])
    end2 = jnp.where(row == 6, hal[0:1, :], jnp.where(row == 7, hal[1:2, :], r2[n - 8:, :]))
    return jnp.concatenate([r1[:n - 8, :], end1], axis=0), jnp.concatenate([r2[:n - 8, :], end2], axis=0)


def _dw_norm_call(x, r, nw, b, ts, tn, name):
    s, n = b.shape
    k = x.shape[1]

    def body(x_ref, r_ref, nw_ref, b_ref, dw_ref):
        _zero_first(pl.program_id(1) == 0, dw_ref)
        h = (x_ref[...] * r_ref[...] * nw_ref[...]).astype(BF16)
        dw_ref[...] += _dot_tn(h, b_ref[...])

    return pl.pallas_call(
        body, name=name, grid=(n // tn, s // ts),
        in_specs=[pl.BlockSpec((ts, k), lambda j, i: (i, 0)), pl.BlockSpec((ts, 1), lambda j, i: (i, 0)),
                  pl.BlockSpec((1, k), lambda j, i: (0, 0)), pl.BlockSpec((ts, tn), lambda j, i: (i, j))],
        out_specs=pl.BlockSpec((None, k, tn), lambda j, i: (j, 0, 0)),
        out_shape=jax.ShapeDtypeStruct((n // tn, k, tn), F32),
        compiler_params=_cp("parallel", "arbitrary"),
    )(x, r, nw, b)


def _out_bwd_call(dx1, yret, ymla, wout, ts, order=None):
    s = dx1.shape[0]

    def body(dx_ref, yr_ref, ym_ref, w_ref, dyr_ref, do_ref, dwo_ref):
        _zero_first(pl.program_id(0) == 0, dwo_ref)
        dxb = dx_ref[...].astype(BF16)
        dmix = _dot_nt(dxb, w_ref[...])
        dyr_ref[...] = dmix[:, :RET_W]
        ym = ym_ref[...]
        lane = lax.broadcasted_iota(jnp.int32, (ts, LANES), 1)
        for p in range(N_HEADS // 2):
            dom = dmix[:, RET_W + p * LANES:RET_W + (p + 1) * LANES]
            prod = dom * ym[:, p * LANES:(p + 1) * LANES].astype(F32)
            for hh in range(2):
                mine = (lane >= HEAD) if hh else (lane < HEAD)
                hi, lo = _hi_lo(jnp.sum(jnp.where(mine, prod, 0.0), axis=1, keepdims=True))
                base = jnp.where(lane < HEAD, pltpu.roll(dom, HEAD, 1) if hh else dom, 0.0)
                do_ref[2 * p + hh] = _lane_pair((ts, LANES), V_AUX, -hi, -lo, base).astype(BF16)
        dwo_ref[0:RET_W, :] += _dot_tn(yr_ref[...], dxb)
        dwo_ref[RET_W:, :] += _dot_tn(ym, dxb)

    sd = jax.ShapeDtypeStruct
    body, first_specs, first = _ordered_after(body, order)
    return pl.pallas_call(
        body, name="out_proj_bwd", grid=(s // ts,),
        in_specs=first_specs + [_row(ts, D_MODEL), _row(ts, RET_W), _row(ts, MLA_W), _full((D_MODEL, D_MODEL))],
        out_specs=[_row(ts, RET_W), _hrow(N_HEADS, ts, LANES), _full((D_MODEL, D_MODEL))],
        out_shape=[sd((s, RET_W), F32), sd((N_HEADS, s, LANES), BF16), sd((D_MODEL, D_MODEL), F32)],
        compiler_params=_cp("arbitrary"),
    )(*first, dx1, yret, ymla, wout)


def _ret_bwd_q_call(q, k, v, o, g, dy, gnw, rc, cos_r, sin_r, tr):
    s = q.shape[0]
    c = RET_CHUNK
    nc = tr // c
    ns = RET_SLABS

    def body(q_ref, k_ref, v_ref, o_ref, g_ref, dy_ref, gnw_ref, dm_ref, zeta_ref, xi_ref, cd_ref, bd_ref, cr_ref, sr_ref,
             dq_ref, dg_ref, do_ref, dgnw_ref, st_ref):
        _zero_first(pl.program_id(1) == 0, st_ref, dgnw_ref)
        bd = bd_ref[...]
        avg = bd * (1.0 / HEAD)
        chunks = [slice(ci * c, (ci + 1) * c) for ci in range(nc)]
        lanes = [slice(sl * LANES, (sl + 1) * LANES) for sl in range(ns)]
        dov = []
        for ln in lanes:
            ov = o_ref[:, ln]
            ctr = ov - _dot_hi(ov, avg)
            rs = lax.rsqrt(_dot_hi(ctr * ctr, avg) + EPS)
            oh = ctr * rs
            gg, dyv, gnw_v = g_ref[:, ln], dy_ref[:, ln], gnw_ref[:, ln]
            sg = _sigmoid(gg)
            sl = gg * sg
            dg_ref[:, ln] = (dyv * oh * gnw_v * _dsilu(gg, sg)).astype(BF16)
            dgnw_ref[:, ln] += _colsum(dyv * sl * oh)
            doh = dyv * sl * gnw_v
            dov.append((rs * (doh - _dot_hi(doh, avg) - oh * _dot_hi(doh * oh, avg))).astype(BF16))
            do_ref[:, ln] = dov[-1]
        states = _ret_states(k_ref, v_ref, zeta_ref, cd_ref, bd, st_ref, chunks, lanes, False)
        for ci, rows in enumerate(chunks):
            for sl, ln in enumerate(lanes):
                doc = dov[sl][rows, :]
                dq = (_dot_nt(doc, states[sl][ci]) * xi_ref[sl]
                      + _pair_product(doc, _stack_heads(v_ref[rows, ln]), dm_ref[sl], _stack_heads(k_ref[rows, ln])))
                dq_ref[rows, ln] = _unrope(dq, cr_ref[rows, :], sr_ref[rows, :], HEAD // 2).astype(BF16)

    specs = _ret_specs(tr, lambda i: i)
    sd = jax.ShapeDtypeStruct
    return pl.pallas_call(
        body, name="ret_bwd_q", grid=(4 // ns, s // tr),
        in_specs=[specs["slab"]] * 6 + [specs["vec"], specs["dmask"], specs["rows"], specs["rows"], specs["state"], specs["bd"],
                                        specs["tab"], specs["tab"]],
        out_specs=[specs["slab"]] * 3 + [specs["vec"]],
        out_shape=[sd((s, RET_W), BF16), sd((s, RET_W), BF16), sd((s, RET_W), BF16), sd((1, RET_W), F32)],
        scratch_shapes=[pltpu.VMEM((ns, LANES, LANES), F32)],
        compiler_params=_cp("parallel", "arbitrary"),
    )(q, k, v, o, g, dy, gnw, rc["dmask"], rc["zeta"], rc["xi"], rc["cd"], rc["bd"], cos_r, sin_r)


def _ret_bwd_kv_call(q, k, v, do, rc, cos_r, sin_r, tr):
    s = q.shape[0]
    c = RET_CHUNK
    nc = tr // c
    nt = s // tr
    ns = RET_SLABS

    def body(q_ref, k_ref, v_ref, do_ref, dm_ref, zeta_ref, xi_ref, cd_ref, bd_ref, cr_ref, sr_ref, dk_ref, dv_ref, gs_ref):
        _zero_first(pl.program_id(1) == 0, gs_ref)
        bd = bd_ref[...]
        chunks = [slice(ci * c, (ci + 1) * c) for ci in range(nc)]
        lanes = [slice(sl * LANES, (sl + 1) * LANES) for sl in range(ns)]
        states = _ret_states(q_ref, do_ref, xi_ref, cd_ref, bd, gs_ref, chunks, lanes, True)
        for ci, rows in enumerate(chunks):
            for sl, ln in enumerate(lanes):
                kc, vc = k_ref[rows, ln], v_ref[rows, ln]
                q2, do2 = _stack_heads(q_ref[rows, ln]), _stack_heads(do_ref[rows, ln])
                gb = states[sl][ci]
                dk = _dot_nt(vc, gb) * zeta_ref[sl] + _pair_product(vc, do2, dm_ref[sl], q2)
                dv = _dot(kc, gb) * zeta_ref[sl] + _pair_product(kc, q2, dm_ref[sl], do2)
                dk_ref[rows, ln] = (_unrope(dk, cr_ref[rows, :], sr_ref[rows, :], HEAD // 2) * (HEAD ** -0.5)).astype(BF16)
                dv_ref[rows, ln] = dv.astype(BF16)

    specs = _ret_specs(tr, lambda i: nt - 1 - i)
    sd = jax.ShapeDtypeStruct
    return pl.pallas_call(
        body, name="ret_bwd_kv", grid=(4 // ns, nt),
        in_specs=[specs["slab"]] * 4 + [specs["dmask"], specs["rows"], specs["rows"], specs["state"], specs["bd"],
                                        specs["tab"], specs["tab"]],
        out_specs=[specs["slab"]] * 2,
        out_shape=[sd((s, RET_W), BF16), sd((s, RET_W), BF16)],
        scratch_shapes=[pltpu.VMEM((ns, LANES, LANES), F32)],
        compiler_params=_cp("parallel", "arbitrary"),
    )(q, k, v, do, rc["dmask_t"], rc["zeta"], rc["xi"], rc["cd"], rc["bd"], cos_r, sin_r)


FLASH_BWD_HEADS = 8


def _flash_bwd_call(qb, k, v, do, tb, order=None):
    s = qb.shape[1]
    nb = s // tb
    hg = FLASH_BWD_HEADS
    pairs = [(a, b) for a in range(nb) for b in range(a, nb)]
    ki_of, qi_of = (jnp.asarray(np.array(col, np.int32)) for col in zip(*pairs))
    extra = [] if order is None else [order]

    def body(ki_ref, qi_ref, *refs):
        q_ref, k_ref, v_ref, do_ref, dk_ref, dv_ref, dq_hbm, dka_ref, dva_ref, dq_ref, sem = refs[len(extra):]
        g, p = pl.program_id(0), pl.program_id(1)
        ki, qi = ki_ref[p], qi_ref[p]
        _zero_first(p == 0, dq_ref)
        _zero_first(qi == ki, dka_ref, dva_ref)
        rows = pl.ds(pl.multiple_of(qi * tb, tb), tb)

        def step(masked):
            if masked:
                keep = lax.broadcasted_iota(jnp.int32, (tb, tb), 0) <= lax.broadcasted_iota(jnp.int32, (tb, tb), 1)
            for h in range(hg):
                st = _dot_nt(k_ref[h], q_ref[h])
                if masked:
                    st = jnp.where(keep, st, NEG)
                pt = jnp.exp2(st)
                dob = do_ref[h]
                dva_ref[h] += _dot(pt.astype(BF16), dob)
                dst = (pt * _dot_nt(v_ref[h], dob)).astype(BF16)
                dka_ref[h] += _dot(dst, q_ref[h])
                dq_ref[h, rows, :] += _dot_tn(dst, k_ref[h])

        @pl.when(qi > ki)
        def _():
            step(False)

        @pl.when(qi == ki)
        def _():
            step(True)

        @pl.when(qi == nb - 1)
        def _():
            dk_ref[...] = (dka_ref[...] * LN2).astype(BF16)
            dv_ref[...] = dva_ref[...].astype(BF16)

        @pl.when(p == len(pairs) - 1)
        def _():
            cp = pltpu.make_async_copy(dq_ref, dq_hbm.at[pl.ds(g * hg, hg)], sem)
            cp.start()
            cp.wait()

    kspec = pl.BlockSpec((hg, tb, LANES), lambda g, p, ki_ref, qi_ref: (g, ki_ref[p], 0))
    qspec = pl.BlockSpec((hg, tb, LANES), lambda g, p, ki_ref, qi_ref: (g, qi_ref[p], 0))
    hm = jax.ShapeDtypeStruct((N_HEADS, s, LANES), BF16)
    return pl.pallas_call(
        body, name="mla_flash_bwd",
        grid_spec=pltpu.PrefetchScalarGridSpec(
            num_scalar_prefetch=2, grid=(N_HEADS // hg, len(pairs)),
            in_specs=[ANY] * len(extra) + [qspec, kspec, kspec, qspec],
            out_specs=[kspec, kspec, ANY],
            scratch_shapes=[pltpu.VMEM((hg, tb, LANES), F32), pltpu.VMEM((hg, tb, LANES), F32),
                            pltpu.VMEM((hg, s, LANES), F32), pltpu.SemaphoreType.DMA]),
        out_shape=[hm, hm, jax.ShapeDtypeStruct((N_HEADS, s, LANES), F32)],
        compiler_params=_cp("arbitrary", "arbitrary"),
    )(ki_of, qi_of, *extra, qb, k, v, do)


def _mla_post_call(dq, dk, dv, cq, ckv, qnw, kvnw, wq, wk, wv, cos_m, sin_m, ts):
    s = cq.shape[0]

    def body(dq_ref, dk_ref, dv_ref, cq_ref, ckv_ref, qnw_ref, kvnw_ref, wq_ref, wk_ref, wv_ref, cm_ref, sm_ref,
             dcq_ref, dckv_ref, dkpe_ref, dwq_ref, dwk_ref, dwv_ref, dqnw_ref, dkvnw_ref):
        _zero_first(pl.program_id(0) == 0, dwq_ref, dwk_ref, dwv_ref, dqnw_ref, dkvnw_ref)
        cqv, ckvv = cq_ref[...], ckv_ref[...]
        rq, rkv = _rstd(cqv), _rstd(ckvv)
        qh_, kvh_ = cqv * rq, ckvv * rkv
        qnw_v, kvnw_v = qnw_ref[...], kvnw_ref[...]
        cqn = (qh_ * qnw_v).astype(BF16)
        ckvn = (kvh_ * kvnw_v).astype(BF16)
        cm, sm = cm_ref[...], sm_ref[...]
        dcqn = jnp.zeros((ts, Q_RANK), F32)
        dckvn = jnp.zeros((ts, KV_RANK), F32)
        dkpe = jnp.zeros((ts, LANES), F32)
        for h in range(N_HEADS):
            dqu = _unrope(dq_ref[h] * SM_SCALE, cm, sm, ROPE // 2).astype(BF16)
            dwq_ref[h] += _dot_tn(cqn, dqu)
            dcqn = dcqn + _dot_nt(dqu, wq_ref[h])
            dkb, dvb = dk_ref[h], dv_ref[h]
            dkpe = dkpe + dkb.astype(F32)
            dwk_ref[h] += _dot_tn(ckvn, dkb)
            dwv_ref[h] += _dot_tn(ckvn, dvb)
            dckvn = dckvn + _dot_nt(dkb, wk_ref[h]) + _dot_nt(dvb, wv_ref[h])
        lane = lax.broadcasted_iota(jnp.int32, (ts, LANES), 1)
        dkpe = jnp.where((lane >= KPE_LO) & (lane < KPE_LO + ROPE), dkpe, 0.0)
        dkpe_ref[...] = _unrope(dkpe, cm, sm, ROPE // 2).astype(BF16)
        dqnw_ref[...] += _colsum(dcqn * qh_)
        dkvnw_ref[...] += _colsum(dckvn * kvh_)
        dcq_ref[...] = _norm_bwd(dcqn, qh_, rq, qnw_v).astype(BF16)
        dckv_ref[...] = _norm_bwd(dckvn, kvh_, rkv, kvnw_v).astype(BF16)

    sd = jax.ShapeDtypeStruct
    hm = _hrow(N_HEADS, ts, LANES)
    return pl.pallas_call(
        body, name="mla_post", grid=(s // ts,),
        in_specs=[hm, hm, hm, _row(ts, Q_RANK), _row(ts, KV_RANK), _full((1, Q_RANK)), _full((1, KV_RANK)),
                  _full((N_HEADS, Q_RANK, LANES)), _full((N_HEADS, KV_RANK, LANES)), _full((N_HEADS, KV_RANK, LANES)),
                  _row(ts, LANES), _row(ts, LANES)],
        out_specs=[_row(ts, Q_RANK), _row(ts, KV_RANK), _row(ts, LANES),
                   _full((N_HEADS, Q_RANK, LANES)), _full((N_HEADS, KV_RANK, LANES)), _full((N_HEADS, KV_RANK, LANES)),
                   _full((1, Q_RANK)), _full((1, KV_RANK))],
        out_shape=[sd((s, Q_RANK), BF16), sd((s, KV_RANK), BF16), sd((s, LANES), BF16),
                   sd((N_HEADS, Q_RANK, LANES), F32), sd((N_HEADS, KV_RANK, LANES), F32), sd((N_HEADS, KV_RANK, LANES), F32),
                   sd((1, Q_RANK), F32), sd((1, KV_RANK), F32)],
        compiler_params=_cp("arbitrary"),
    )(dq, dk, dv, cq, ckv, qnw, kvnw, wq, wk, wv, cos_m, sin_m)


def _in_bwd_call(parts, x, r1, anw, dx1, win, ts):
    s = x.shape[0]
    widths = [p.shape[1] for p in parts]
    np_ = len(parts)

    def body(*refs):
        p_refs = refs[:np_]
        x_ref, r_ref, anw_ref, dx1_ref, w_ref, dx_ref, dw_ref, danw_ref = refs[np_:]
        _zero_first(pl.program_id(0) == 0, dw_ref, danw_ref)
        dproj = jnp.concatenate([p[...] for p in p_refs], axis=-1)
        r, anw_v = r_ref[...], anw_ref[...]
        xh = x_ref[...] * r
        dw_ref[...] += _dot_tn((xh * anw_v).astype(BF16), dproj)
        dh = _dot_nt(dproj, w_ref[...])
        danw_ref[...] += _colsum(dh * xh)
        dx_ref[...] = dx1_ref[...] + _norm_bwd(dh, xh, r, anw_v)

    sd = jax.ShapeDtypeStruct
    return pl.pallas_call(
        body, name="in_proj_bwd", grid=(s // ts,),
        in_specs=[_row(ts, w) for w in widths]
        + [_row(ts, D_MODEL), _row(ts, 1), _full((1, D_MODEL)), _row(ts, D_MODEL), _full((D_MODEL, IN_EXT))],
        out_specs=[_row(ts, D_MODEL), _full((D_MODEL, IN_EXT)), _full((1, D_MODEL))],
        out_shape=[sd((s, D_MODEL), F32), sd((D_MODEL, IN_EXT), F32), sd((1, D_MODEL), F32)],
        compiler_params=_cp("arbitrary"),
    )(*parts, x, r1, anw, dx1, win)


def _local_step(x, positions, tgt, w, small, ex=None):
    s = x.shape[0]
    t = _tiles(s)
    ex = _Exchanges(w) if ex is None else ex
    f = _forward(x, positions, tgt, w, small, ex)
    pw, rc = f["pw"], f["rc"]
    cos_r, sin_r, cos_m, sin_m = f["tabs"]
    dx2, loss, g_fw = f["dx2"], f["loss"], f["g_fw"]
    du, dx1, g_cw, g_cb, g_fnw, g_wd = _ffn_bwd_call(dx2, f["u"], f["uc"], w["conv_w"], pw["wdown"], pw["wup"],
                                                     f["x1"], f["r2"], small["ffn_norm_w"], t["t2"])
    g_wup = _dw_norm_call(f["x1"], f["r2"], small["ffn_norm_w"], du, t["tw"], F2 // 4, "dw_up")
    started = ex.mlp_grads(dict(w_up=g_wup, w_down=g_wd))
    dy_ret, do, g_wout = _out_bwd_call(dx1, f["y_ret"], f["y_mla"], pw["wout"], t["t1"], started)
    started = ex.behind_out_bwd(g_wout)
    drq, dg, do_ret, g_gnw = _ret_bwd_q_call(f["q"], f["k"], f["v"], f["o_ret"], f["g"], dy_ret, small["ret_gn_w"], rc, cos_r, sin_r, t["tr"])
    drk, drv = _ret_bwd_kv_call(f["q"], f["k"], f["v"], do_ret, rc, cos_r, sin_r, t["tr"])
    dmk, dmv, dmq = _flash_bwd_call(f["mqb"], f["mk"], f["mv"], do, t["tb"], started)
    ex.behind_attention(dmk)
    dcq, dckv, dkpe, g_wq, g_wk, g_wv, g_qnw, g_kvnw = _mla_post_call(
        dmq, dmk, dmv, f["cq"], f["ckv"], small["mla_q_norm_w"], small["mla_kv_norm_w"], pw["wq"], pw["wk"], pw["wv"], cos_m, sin_m, t["ts"])
    gx, g_win_ext, g_anw = _in_bwd_call([drq, drk, drv, dg, dcq, dckv, dkpe], x, f["r1"], small["attn_norm_w"], dx1, pw["win"], t["ts"])
    if w["w_in"].ndim == 3:
        g_win = _win_grad_blocks_call(g_win_ext, w["w_in"].shape[0])
    else:
        g_win = _win_grad_blocks_call(g_win_ext, 1)[0]
    g_wuq = g_wq.transpose(1, 0, 2)[:, :, :HEAD + ROPE].reshape(Q_RANK, N_HEADS * (HEAD + ROPE))
    g_wukv = jnp.concatenate([g_wk[:, :, :HEAD], g_wv[:, :, :HEAD]], -1).transpose(1, 0, 2).reshape(KV_RANK, 2 * MLA_W)
    gw = dict(w_in=g_win, w_uq=g_wuq, w_ukv=g_wukv, w_out=g_wout, w_up=g_wup,
              conv_w=g_cw, w_down=g_wd)
    gs = dict(attn_norm_w=g_anw, ret_gn_w=g_gnw, mla_q_norm_w=g_qnw, mla_kv_norm_w=g_kvnw, ffn_norm_w=g_fnw,
              conv_b=g_cb, final_norm_w=g_fw)
    return loss, gx, gw, gs


MESH_ID = pl.DeviceIdType.MESH
ANY = pl.BlockSpec(memory_space=pl.ANY)
VMEM_SPEC = pl.BlockSpec(memory_space=pltpu.VMEM)
N_DEV = 8
GROUP_A = (("w_in", (D_MODEL, IN_W // 4), 1), ("w_uq", (Q_RANK, 192), 1), ("w_ukv", (KV_RANK, 256), 1),
           ("w_out", (D_MODEL // 4, D_MODEL), 0))
GROUP_B = (("w_up", (D_MODEL, F2 // 4), 1), ("w_down", (D_FF // 4, D_MODEL), 0))
HBM_SPEC = pl.BlockSpec(memory_space=pltpu.HBM)
SEM_SPEC = pl.BlockSpec(memory_space=pltpu.SEMAPHORE)


def _mesh_pos():
    return lax.axis_index("x"), lax.axis_index("y"), lax.axis_index("c")


def _other_chips(x, y):
    return [(1 - x, y), (x, 1 - y), (1 - x, 1 - y)]


def _remote(src, dst, send_sems, recv_sems, k, dev):
    return pltpu.make_async_remote_copy(src_ref=src, dst_ref=dst, send_sem=send_sems.at[k], recv_sem=recv_sems.at[k],
                                        device_id=dev, device_id_type=MESH_ID)


def _gather_list_call(parts, tag):
    n = len(parts)

    def body(*refs):
        srcs, outs, (send_sems, recv_sems) = refs[:n], refs[n:2 * n], refs[2 * n:]
        x, y, c = _mesh_pos()
        sm = 2 * x + y
        chips = _other_chips(x, y)
        sib = (x, y, 1 - c)
        rc = lambda k, src, dst, dev: _remote(src, dst, send_sems, recv_sems, k, dev)
        first = [rc(7 * i + j, srcs[i].at[c], outs[i].at[sm, c], (cx, cy, c)) for i in range(n) for j, (cx, cy) in enumerate(chips)]
        own = [rc(7 * i + 6, srcs[i], outs[i].at[sm], sib) for i in range(n)]
        for cp in first + own:
            cp.start()
        passed = []
        for j, (cx, cy) in enumerate(chips):
            for i in range(n):
                land = outs[i].at[2 * cx + cy, c]
                rc(7 * i + j, srcs[i].at[c], land, (cx, cy, c)).wait_recv()
                cp = rc(7 * i + 3 + j, land, land, sib)
                cp.start()
                passed.append(cp)
        for j, (cx, cy) in enumerate(chips):
            for i in range(n):
                rc(7 * i + 3 + j, srcs[i].at[c], outs[i].at[2 * cx + cy, 1 - c], sib).wait_recv()
        for cp in own:
            cp.wait_recv()
        for cp in first + passed + own:
            cp.wait_send()

    return pl.pallas_call(
        body, name="weights_all_gather_" + tag,
        in_specs=[ANY] * n, out_specs=[ANY] * n,
        out_shape=[jax.ShapeDtypeStruct((4,) + p.shape, p.dtype) for p in parts],
        scratch_shapes=[pltpu.SemaphoreType.DMA((7 * n,)), pltpu.SemaphoreType.DMA((7 * n,))],
    )(*parts)


def _direct_gather_copies(srcs, lands, send_sems, recv_sems):
    x, y, c = _mesh_pos()
    sm = 2 * x + y
    sends, recvs = [], []
    for i, (src, land) in enumerate(zip(srcs, lands)):
        for j, (cx, cy) in enumerate(_other_chips(x, y)):
            for t in range(2):
                sends.append(_remote(src.at[c], land.at[sm, c], send_sems, recv_sems, 13 * i + 4 * j + 2 * c + t, (cx, cy, t)))
                recvs.append(_remote(src.at[t], land.at[2 * cx + cy, t], send_sems, recv_sems, 13 * i + 4 * j + 2 * t + c, (cx, cy, t)))
        sends.append(_remote(src, land.at[sm], send_sems, recv_sems, 13 * i + 12, (x, y, 1 - c)))
        recvs.append(_remote(src, land.at[sm], send_sems, recv_sems, 13 * i + 12, (x, y, 1 - c)))
    return sends, recvs


def _sibling_copies(srcs, lands, send_sems, recv_sems):
    x, y, c = _mesh_pos()
    cps = [_remote(src.at[s, 1 - c], land.at[s], send_sems, recv_sems, 4 * i + s, (x, y, 1 - c))
           for i, (src, land) in enumerate(zip(srcs, lands)) for s in range(4)]
    return cps, cps


def _chips_copies(srcs, lands, send_sems, recv_sems):
    x, y, c = _mesh_pos()
    cps = [_remote(src.at[2 * cx + cy], land.at[j], send_sems, recv_sems, 3 * i + j, (cx, cy, c))
           for i, (src, land) in enumerate(zip(srcs, lands)) for j, (cx, cy) in enumerate(_other_chips(x, y))]
    return cps, cps


def _share_copies(srcs, lands, send_sems, recv_sems):
    x, y, c = _mesh_pos()
    cps = [_remote(src, land, send_sems, recv_sems, i, (x, y, 1 - c)) for i, (src, land) in enumerate(zip(srcs, lands))]
    return cps, cps


def _exchange_call(name, copies, srcs, land_shapes, n_sems):
    n = len(srcs)

    def body(*refs):
        sends, recvs = copies(refs[:n], refs[n:2 * n], refs[2 * n], refs[2 * n + 1])
        for cp in sends:
            cp.start()
        for cp in sends:
            cp.wait_send()
        for cp in recvs:
            cp.wait_recv()

    return pl.pallas_call(
        body, name=name, in_specs=[ANY] * n, out_specs=[ANY] * n, out_shape=list(land_shapes),
        scratch_shapes=[pltpu.SemaphoreType.DMA((n_sems,)), pltpu.SemaphoreType.DMA((n_sems,))],
    )(*srcs)


def _exchange_start_call(name, copies, srcs, land_shapes, n_sems, order=None):
    n = len(srcs)
    extra = [] if order is None else [order]
    k = 2 * n + len(extra)

    def body(*refs):
        sends, _ = copies(refs[:n], refs[n:2 * n], refs[k], refs[k + 1])
        for cp in sends:
            cp.start()
        refs[-1][...] = jnp.zeros_like(refs[-1])

    hbm = lambda a: pltpu.with_memory_space_constraint(a, pltpu.HBM)
    lands = [hbm(lax.empty(sd.shape, sd.dtype)) for sd in land_shapes]
    sem = pltpu.SemaphoreType.DMA((n_sems,))
    out = pl.pallas_call(
        body, name=name,
        out_shape=(sem, sem, *[pltpu.HBM(a.shape, a.dtype) for a in list(srcs) + lands], jax.ShapeDtypeStruct((8, LANES), F32)),
        in_specs=[HBM_SPEC] * (2 * n) + [ANY] * len(extra), out_specs=(SEM_SPEC, SEM_SPEC, *[HBM_SPEC] * (2 * n), VMEM_SPEC),
        input_output_aliases={i: 2 + i for i in range(2 * n)},
        compiler_params=pltpu.CompilerParams(has_side_effects=pltpu.SideEffectType.DATAFLOW_SIDE_EFFECTING),
    )(*[hbm(a) for a in srcs], *lands, *extra)
    return out[0], out[1], out[2:2 + n], out[2 + n:2 + 2 * n], out[-1]


def _exchange_wait_call(name, copies, started, after):
    send_sems, recv_sems, srcs, lands, _ = started
    n = len(srcs)

    def body(*refs):
        sends, recvs = copies(refs[:n], refs[n:2 * n], refs[2 * n], refs[2 * n + 1])
        for cp in sends:
            cp.wait_send()
        for cp in recvs:
            cp.wait_recv()

    out = pl.pallas_call(
        body, name=name,
        out_shape=tuple(pltpu.HBM(a.shape, a.dtype) for a in list(srcs) + list(lands)),
        in_specs=[HBM_SPEC] * (2 * n) + [SEM_SPEC, SEM_SPEC, ANY], out_specs=tuple([HBM_SPEC] * (2 * n)),
        input_output_aliases={i: i for i in range(2 * n)},
        compiler_params=pltpu.CompilerParams(has_side_effects=pltpu.SideEffectType.DATAFLOW_SIDE_EFFECTING),
    )(*srcs, *lands, send_sems, recv_sems, after)
    return out[:n], out[n:]


def _rows_tile(rows, width, itemsize=4):
    limit = max(16, (3 << 20) // (width * itemsize))
    if rows <= limit:
        return rows
    return max(t for t in range(16, limit + 1, 16) if rows % t == 0)


def _sum_sibling_call(g, buf, c, name):
    _, _, rh, w = g.shape
    tile = _rows_tile(rh, w)

    def body(c_ref, g_ref, b_ref, p_ref, pb_ref):
        p = g_ref[...] + b_ref[...]
        p_ref[...] = p
        pb_ref[...] = p.astype(BF16)

    blk = pl.BlockSpec((None, tile, w), lambda s, i, c_ref: (s, i, 0))
    return pl.pallas_call(
        body, name=name,
        grid_spec=pltpu.PrefetchScalarGridSpec(
            num_scalar_prefetch=1, grid=(4, rh // tile),
            in_specs=[pl.BlockSpec((None, None, tile, w), lambda s, i, c_ref: (s, c_ref[0], i, 0)), blk],
            out_specs=[blk, blk]),
        out_shape=[jax.ShapeDtypeStruct((4, rh, w), F32), jax.ShapeDtypeStruct((4, rh, w), BF16)],
        compiler_params=_cp("parallel", "parallel"),
    )(c, g, buf)


def _sum_chips_call(p, buf, sm, name):
    _, rh, w = p.shape
    tile = _rows_tile(rh, w)

    def body(sm_ref, p_ref, b_ref, f_ref):
        f_ref[...] = ((p_ref[...] + b_ref[0].astype(F32)) + b_ref[1].astype(F32)) + b_ref[2].astype(F32)

    return pl.pallas_call(
        body, name=name,
        grid_spec=pltpu.PrefetchScalarGridSpec(
            num_scalar_prefetch=1, grid=(rh // tile,),
            in_specs=[pl.BlockSpec((None, tile, w), lambda i, sm_ref: (sm_ref[0], i, 0)),
                      pl.BlockSpec((3, tile, w), lambda i, sm_ref: (0, i, 0))],
            out_specs=pl.BlockSpec((tile, w), lambda i, sm_ref: (i, 0))),
        out_shape=jax.ShapeDtypeStruct((rh, w), F32),
        compiler_params=_cp("parallel"),
    )(sm, p, buf)


def _adamw_halves_call(w, g_mine, g_sib, c, m, v, name, transposed=False, order=None):
    extra = [] if order is None else [order]
    if transposed:
        rows, r = w.shape
        rh = r // 2
        tile = _rows_tile(rows, rh)
        whole = pl.BlockSpec((tile, rh), lambda h, i, c_ref: (i, h))
        half = pl.BlockSpec((tile, rh), lambda h, i, c_ref: (i, 0))
        nt = rows // tile
    else:
        r, wd = w.shape
        rh = r // 2
        tile = _rows_tile(rh, wd)
        nt = rh // tile
        whole = pl.BlockSpec((tile, wd), lambda h, i, c_ref: (h * nt + i, 0))
        half = pl.BlockSpec((tile, wd), lambda h, i, c_ref: (i, 0))

    def body(c_ref, w_ref, gm_ref, gs_ref, m_ref, v_ref, *rest):
        g_ref, d_ref, nm_ref, nv_ref = rest[len(extra):]
        gv = jnp.where(pl.program_id(0) == c_ref[0], gm_ref[...], gs_ref[...])
        g_ref[...] = gv
        nm = ADAM_B1 * m_ref[...] + (1.0 - ADAM_B1) * gv
        nv = ADAM_B2 * v_ref[...] + (1.0 - ADAM_B2) * jnp.square(gv)
        m_hat = nm / (1.0 - ADAM_B1 ** ADAM_STEP)
        v_hat = nv / (1.0 - ADAM_B2 ** ADAM_STEP)
        d_ref[...] = -ADAM_LR * (m_hat / (jnp.sqrt(v_hat) + ADAM_EPS) + ADAM_WD * w_ref[...])
        nm_ref[...] = nm
        nv_ref[...] = nv

    sd = jax.ShapeDtypeStruct(w.shape, F32)
    return pl.pallas_call(
        body, name=name,
        grid_spec=pltpu.PrefetchScalarGridSpec(
            num_scalar_prefetch=1, grid=(2, nt),
            in_specs=[whole, half, half, whole, whole] + [ANY] * len(extra), out_specs=[whole] * 4),
        out_shape=[sd, sd, sd, sd],
        compiler_params=_cp("parallel", "parallel"),
    )(c, w, g_mine, g_sib, m, v, *extra)


def _all_reduce8_call(vec, name):
    rows = vec.shape[0]

    def body(v_ref, out_ref, slots, send_sems, recv_sems):
        x, y, c = _mesh_pos()
        me = 4 * x + 2 * y + c
        slots[me] = v_ref[...]

        def rcopy(k, to_me):
            bx, by, bc = (k >> 2) & 1, (k >> 1) & 1, k & 1
            px, py, pc = (1 - x if bx else x), (1 - y if by else y), (1 - c if bc else c)
            slot = 4 * px + 2 * py + pc if to_me else me
            return pltpu.make_async_remote_copy(src_ref=v_ref, dst_ref=slots.at[slot], send_sem=send_sems.at[k - 1],
                                                recv_sem=recv_sems.at[k - 1], device_id=(px, py, pc), device_id_type=MESH_ID)

        for k in range(1, N_DEV):
            rcopy(k, False).start()
        for k in range(1, N_DEV):
            rcopy(k, True).wait_recv()
        for k in range(1, N_DEV):
            rcopy(k, False).wait_send()
        tot = slots[0]
        for d in range(1, N_DEV):
            tot = tot + slots[d]
        out_ref[...] = tot

    return pl.pallas_call(
        body, name=name,
        in_specs=[VMEM_SPEC], out_specs=VMEM_SPEC,
        out_shape=jax.ShapeDtypeStruct((rows, LANES), F32),
        scratch_shapes=[pltpu.VMEM((N_DEV, rows, LANES), F32),
                        pltpu.SemaphoreType.DMA((N_DEV - 1,)), pltpu.SemaphoreType.DMA((N_DEV - 1,))],
    )(vec)


def _adamw_call(w, g, m, v, name):
    r, c = w.shape
    rb = r if r <= 256 else (256 if r % 256 == 0 else 352)
    assert r % rb == 0

    def body(w_ref, g_ref, m_ref, v_ref, d_ref, nm_ref, nv_ref):
        gv = g_ref[...]
        nm = ADAM_B1 * m_ref[...] + (1.0 - ADAM_B1) * gv
        nv = ADAM_B2 * v_ref[...] + (1.0 - ADAM_B2) * jnp.square(gv)
        m_hat = nm / (1.0 - ADAM_B1 ** ADAM_STEP)
        v_hat = nv / (1.0 - ADAM_B2 ** ADAM_STEP)
        d_ref[...] = -ADAM_LR * (m_hat / (jnp.sqrt(v_hat) + ADAM_EPS) + ADAM_WD * w_ref[...])
        nm_ref[...] = nm
        nv_ref[...] = nv

    spec = pl.BlockSpec((rb, c), lambda i: (i, 0))
    sd = jax.ShapeDtypeStruct((r, c), F32)
    return pl.pallas_call(
        body, name=name, grid=(r // rb,),
        in_specs=[spec] * 4, out_specs=[spec] * 3, out_shape=[sd, sd, sd],
        compiler_params=_cp("parallel"),
    )(w, g, m, v)


SMALL = (("attn_norm_w", D_MODEL), ("ret_gn_w", RET_W), ("mla_q_norm_w", Q_RANK), ("mla_kv_norm_w", KV_RANK),
         ("ffn_norm_w", D_MODEL), ("conv_b", F2), ("final_norm_w", D_MODEL))
WEIGHT_ORDER = ("attn_norm_w", "w_in", "ret_gn_w", "mla_q_norm_w", "w_uq", "mla_kv_norm_w", "w_ukv", "w_out",
                "ffn_norm_w", "w_up", "conv_w", "conv_b", "w_down", "final_norm_w")


def _pad_rows(flat, rows):
    return jnp.concatenate([flat, jnp.zeros((rows * LANES - flat.shape[0],), flat.dtype)]).reshape(rows, LANES)


def kernel(x, positions, attn_norm_w, w_in, ret_gn_w, mla_q_norm_w, w_uq, mla_kv_norm_w, w_ukv, w_out, ffn_norm_w, w_up, conv_w, conv_b, w_down, final_norm_w, loss_target, m_attn_norm_w, m_w_in, m_ret_gn_w, m_mla_q_norm_w, m_w_uq, m_mla_kv_norm_w, m_w_ukv, m_w_out, m_ffn_norm_w, m_w_up, m_conv_w, m_conv_b, m_w_down, m_final_norm_w, v_attn_norm_w, v_w_in, v_ret_gn_w, v_mla_q_norm_w, v_w_uq, v_mla_kv_norm_w, v_w_ukv, v_w_out, v_ffn_norm_w, v_w_up, v_conv_w, v_conv_b, v_w_down, v_final_norm_w):
    args = dict(locals())
    cx, cy, cc = _mesh_pos()
    sm = 2 * cx + cy

    c_arr, sm_arr = cc.reshape(1).astype(jnp.int32), sm.reshape(1).astype(jnp.int32)
    sds = jax.ShapeDtypeStruct

    def my_shards(group):
        return [args[n][0].astype(BF16).reshape(2, r // 2, c) for n, (r, c), _ in group]

    def full_weights(gathered, group):
        full = {}
        for (n, (r, c), axis), got in zip(group, gathered):
            piece = got.reshape(4, r, c)
            full[n] = piece if n in ("w_up", "w_in") else (piece.transpose(1, 0, 2).reshape(r, 4 * c) if axis == 1 else piece.reshape(4 * r, c))
        return full

    def by_owner(gw, group):
        out = []
        for n, (r, c), axis in group:
            g = gw[n]
            if axis == 1 and g.ndim == 2:
                g = g.reshape(r, 4, c).transpose(1, 0, 2)
            out.append(g.reshape(4, 2, r // 2, c))
        return out

    def sibling_shapes(gs):
        return [sds((4,) + g.shape[2:], F32) for g in gs]

    def chip_sums(gs, bufs, group):
        res = [_sum_sibling_call(g, b, c_arr, "grads_sum_sibling_" + n) for g, b, (n, _, _) in zip(gs, bufs, group)]
        return [p for p, _ in res], [pb for _, pb in res]

    def chips_shapes(pbs):
        return [sds((3,) + pb.shape[1:], BF16) for pb in pbs]

    def totals(ps, lands, group, tag):
        fins = [_sum_chips_call(p, l, sm_arr, "grads_sum_chips_" + n) for p, l, (n, _, _) in zip(ps, lands, group)]
        sibs = _exchange_call("grads_rs_share_" + tag, _share_copies, fins, [sds(f.shape, F32) for f in fins], len(fins))
        return {n: (f, s) for (n, _, _), f, s in zip(group, fins, sibs)}

    class StepExchanges(_Exchanges):
        def __init__(self, order):
            shards = my_shards(GROUP_B)
            self.gather = _exchange_start_call("weights_gather_start_b", _direct_gather_copies, shards,
                                               [sds((4,) + s.shape, BF16) for s in shards], 13 * len(shards), order)
            self.red = None

        def token(self):
            return self.gather[4][0:1, 0:1]

        def mlp_weights(self, after):
            return full_weights(_exchange_wait_call("weights_gather_wait_b", _direct_gather_copies, self.gather, after)[1], GROUP_B)

        def mlp_grads(self, gw):
            gs = by_owner(gw, GROUP_B)
            self.step1 = _exchange_start_call("grads_rs_sibling_start_b", _sibling_copies, gs, sibling_shapes(gs), 4 * len(gs))
            return self.step1[4]

        def behind_out_bwd(self, after):
            gs, bufs = _exchange_wait_call("grads_rs_sibling_wait_b", _sibling_copies, self.step1, after)
            self.ps, pbs = chip_sums(gs, bufs, GROUP_B)
            self.step2 = _exchange_start_call("grads_rs_chips_start_b", _chips_copies, pbs, chips_shapes(pbs), 3 * len(pbs))
            return self.step2[4]

        def behind_attention(self, after):
            _, lands = _exchange_wait_call("grads_rs_chips_wait_b", _chips_copies, self.step2, after)
            self.red = totals(self.ps, lands, GROUP_B, "b")

    gathered = _gather_list_call(my_shards(GROUP_A) + [conv_w[0].reshape(2, 1, 3 * F2 // 8)], "a")
    full = full_weights(gathered[:-1], GROUP_A)
    ex = StepExchanges(gathered[-1])
    full["conv_w"] = gathered[-1].reshape(4, 3, F2 // 4).transpose(1, 0, 2).reshape(3, F2)
    small = {n: args[n].reshape(1, d) for n, d in SMALL}
    small["attn_norm_w"] = small["attn_norm_w"] + ex.token()

    loss, gx, gw, gs = _local_step(x[0], positions[0], loss_target[0], full, small, ex)

    ga = by_owner(gw, GROUP_A)
    bufs = _exchange_call("grads_rs_sibling_a", _sibling_copies, ga, sibling_shapes(ga), 4 * len(ga))
    ps, pbs = chip_sums(ga, bufs, GROUP_A)
    step2 = _exchange_start_call("grads_rs_chips_start_a", _chips_copies, pbs, chips_shapes(pbs), 3 * len(pbs))
    early, last = {}, step2[4]
    for n, _, _ in GROUP_B:
        wmv = [args[k + n][0] for k in ("", "m_", "v_")]
        early[n] = _adamw_halves_call(wmv[0], *ex.red[n], c_arr, wmv[1], wmv[2], "adamw_" + n, order=last)
        last = early[n][1]
    _, lands = _exchange_wait_call("grads_rs_chips_wait_a", _chips_copies, step2, last)
    halves = totals(ps, lands, GROUP_A, "a")

    vec = jnp.concatenate([gs[n].reshape(-1) for n, _ in SMALL] + [gw["conv_w"].reshape(-1), loss.reshape(-1)])
    tot = _all_reduce8_call(_pad_rows(vec, 216), "small_all_reduce").reshape(-1)
    red, off = {}, 0
    for n, d in SMALL:
        red[n] = tot[off:off + d].reshape(1, d)
        off += d
    red["conv_w"] = lax.dynamic_slice(tot[off:off + 3 * F2].reshape(3, F2), (0, sm * (F2 // 4)), (3, F2 // 4))
    loss_tot = tot[off + 3 * F2]

    grads, deltas, new_m, new_v = [], [], [], []
    for n in WEIGHT_ORDER:
        shape = args[n].shape
        two_d = (1, shape[0]) if len(shape) == 1 else shape[-2:]
        wmv = [args[k + n].reshape(two_d) for k in ("", "m_", "v_")]
        if n in early:
            g, d, nm, nv = early[n]
        elif n in halves and two_d[1] % LANES:
            tr = lambda a: a.T
            g, d, nm, nv = map(tr, _adamw_halves_call(tr(wmv[0]), *map(tr, halves[n]), c_arr, tr(wmv[1]), tr(wmv[2]),
                                                      "adamw_" + n, transposed=True))
        elif n in halves:
            g, d, nm, nv = _adamw_halves_call(wmv[0], *halves[n], c_arr, wmv[1], wmv[2], "adamw_" + n)
        else:
            g = red[n].reshape(two_d)
            d, nm, nv = _adamw_call(wmv[0], g, wmv[1], wmv[2], "adamw_" + n)
        grads.append(g.reshape(shape))
        deltas.append(d.reshape(shape))
        new_m.append(nm.reshape(shape))
        new_v.append(nv.reshape(shape))
    return (loss_tot, gx[None], *grads, *deltas, *new_m, *new_v)
```

```python
import math

import numpy as np
import jax
import jax.numpy as jnp
from jax import lax
from jax.experimental import pallas as pl
from jax.experimental.pallas import tpu as pltpu

F32 = jnp.float32
BF16 = jnp.bfloat16

D_MODEL = 1024
N_HEADS = 8
HEAD = 64
RET_W = N_HEADS * HEAD
MLA_W = N_HEADS * HEAD
ROPE = 32
Q_RANK = 256
KV_RANK = 128
D_FF = 2816
F2 = 2 * D_FF
IN_W = 4 * RET_W + Q_RANK + KV_RANK + ROPE
IN_EXT = 4 * RET_W + Q_RANK + KV_RANK + 128
KPE_LO = 64
ROPE_BASE = 10000.0
EPS = 1e-6
RET_CHUNK = 256
SM_SCALE = (HEAD + ROPE) ** -0.5
LOG2E = math.log2(math.e)
LN2 = math.log(2.0)
NEG = -1e30
LANES = 128
VMEM_LIMIT = 56 * 1024 * 1024

ADAM_LR = 0.001
ADAM_B1 = 0.9
ADAM_B2 = 0.999
ADAM_EPS = 1e-08
ADAM_WD = 0.01
ADAM_STEP = 10


VMEM_LIMIT_MLP = 60 * 1024 * 1024


def _cp(*sem, vmem=VMEM_LIMIT):
    return pltpu.CompilerParams(dimension_semantics=sem, vmem_limit_bytes=vmem)


def _full(shape):
    n = len(shape)
    return pl.BlockSpec(tuple(shape), lambda *_: (0,) * n)


def _row(ts, c):
    return pl.BlockSpec((ts, c), lambda i: (i, 0))


def _hrow(h, ts, c):
    return pl.BlockSpec((h, ts, c), lambda i: (0, i, 0))


def _dot(a, b):
    return jnp.dot(a, b, preferred_element_type=F32)


def _dot_nt(a, b):
    return lax.dot_general(a, b, (((1,), (1,)), ((), ())), preferred_element_type=F32)


def _dot_tn(a, b):
    return lax.dot_general(a, b, (((0,), (0,)), ((), ())), preferred_element_type=F32)


def _dot_hi(a, b):
    hi = a.astype(BF16)
    lo = (a - hi.astype(F32)).astype(BF16)
    bb = b.astype(BF16)
    return _dot(hi, bb) + _dot(lo, bb)


def _rot_half(x, half):
    w = x.shape[-1]
    lane = lax.broadcasted_iota(jnp.int32, x.shape, x.ndim - 1)
    first = (lane % (2 * half)) < half
    return jnp.where(first, -pltpu.roll(x, w - half, x.ndim - 1), pltpu.roll(x, half, x.ndim - 1))


def _rope(x, cos, sin, half):
    return x * cos + _rot_half(x, half) * sin


def _unrope(dy, cos, sin, half):
    return dy * cos - _rot_half(dy, half) * sin


def _sigmoid(g):
    return 0.5 * jnp.tanh(0.5 * g) + 0.5


def _silu(g):
    return g * _sigmoid(g)


def _rstd(x):
    return lax.rsqrt(jnp.mean(x * x, axis=-1, keepdims=True) + EPS)


def _rope_tables(positions):
    s = positions.shape[0]
    hr, hm = HEAD // 2, ROPE // 2
    pos = positions.astype(F32)[None, :]
    inv_r = ROPE_BASE ** (-jnp.arange(0, HEAD, 2, dtype=F32) / HEAD)
    inv_m = ROPE_BASE ** (-jnp.arange(0, ROPE, 2, dtype=F32) / ROPE)
    ang = jnp.concatenate([inv_r, inv_m])[:, None] * pos
    packed = jnp.concatenate([jnp.cos(ang), jnp.sin(ang), jnp.zeros((LANES - 2 * (hr + hm), s), F32)], 0)
    tx = min(s, 1024)

    def spread(t, lane, pieces, fill):
        out = jnp.full(t.shape, fill, F32)
        for lo, src, width in pieces:
            moved = t if lo == src else pltpu.roll(t, (lo - src) % LANES, 1)
            out = jnp.where((lane >= lo) & (lane < lo + width), moved, out)
        return out

    def body(p_ref, cr_ref, sr_ref, cm_ref, sm_ref):
        t = p_ref[...].T
        lane = lax.broadcasted_iota(jnp.int32, t.shape, 1)
        cr_ref[...] = spread(t, lane, [(j * hr, 0, hr) for j in range(LANES // hr)], 0.0)
        sr_ref[...] = spread(t, lane, [(j * hr, hr + hm, hr) for j in range(LANES // hr)], 0.0)
        cm_ref[...] = spread(t, lane, [(KPE_LO, hr, hm), (KPE_LO + hm, hr, hm)], 1.0)
        sm_ref[...] = spread(t, lane, [(KPE_LO, 2 * hr + hm, hm), (KPE_LO + hm, 2 * hr + hm, hm)], 0.0)

    tab = jax.ShapeDtypeStruct((s, LANES), F32)
    return pl.pallas_call(
        body, name="rope_tables", grid=(s // tx,),
        in_specs=[pl.BlockSpec((LANES, tx), lambda i: (0, i))],
        out_specs=[_row(tx, LANES)] * 4, out_shape=[tab] * 4,
        compiler_params=_cp("parallel"),
    )(packed)


def _ret_consts():
    c = RET_CHUNK
    lg = np.log1p(-np.power(2.0, -5.0 - np.arange(N_HEADS, dtype=np.float64)))
    idx = np.arange(c, dtype=np.float64)
    diff = idx[:, None] - idx[None, :]
    lane_head = np.arange(LANES) // HEAD
    dmask = np.zeros((4, 2, c, c))
    zeta = np.zeros((4, c, LANES))
    xi = np.zeros((4, c, LANES))
    cd = np.zeros((4, LANES, LANES))
    bd = (lane_head[:, None] == lane_head[None, :]).astype(np.float64)
    for j in range(4):
        for hh in range(2):
            dmask[j, hh] = np.where(diff >= 0, np.exp(lg[2 * j + hh] * np.maximum(diff, 0.0)), 0.0)
        lgl = lg[2 * j + lane_head]
        zeta[j] = np.exp(lgl[None, :] * (c - 1.0 - idx[:, None]))
        xi[j] = np.exp(lgl[None, :] * (idx[:, None] + 1.0))
        cd[j] = np.exp(lgl * c)[:, None] * bd
    f = lambda a: jnp.asarray(a, F32)
    side = lambda d: np.concatenate([d[:, 0], d[:, 1]], axis=-1)
    return dict(dmask=f(side(dmask)), dmask_t=f(side(np.swapaxes(dmask, 2, 3))), zeta=f(zeta), xi=f(xi), cd=f(cd), bd=f(bd))


def _f1_call(x, anw, win, qnw, kvnw, wq, wk, wv, cos_r, sin_r, cos_m, sin_m, ts):
    s = x.shape[0]

    def body(x_ref, anw_ref, w_ref, qnw_ref, kvnw_ref, wq_ref, wk_ref, wv_ref, cr_ref, sr_ref, cm_ref, sm_ref,
             q_ref, k_ref, v_ref, g_ref, cq_ref, ckv_ref, mq_ref, mk_ref, mv_ref, r_ref):
        xv = x_ref[...]
        r = _rstd(xv)
        r_ref[...] = r
        h = (xv * r * anw_ref[...]).astype(BF16)
        cr, sr = cr_ref[...], sr_ref[...]
        qk = _dot(h, w_ref[:, 0:2 * RET_W])
        for j in range(4):
            sl = slice(j * LANES, (j + 1) * LANES)
            q_ref[:, sl] = _rope(qk[:, sl], cr, sr, HEAD // 2).astype(BF16)
            kk = qk[:, RET_W + j * LANES:RET_W + (j + 1) * LANES]
            k_ref[:, sl] = (_rope(kk, cr, sr, HEAD // 2) * (HEAD ** -0.5)).astype(BF16)
        v_ref[...] = _dot(h, w_ref[:, 2 * RET_W:3 * RET_W]).astype(BF16)
        g_ref[...] = _dot(h, w_ref[:, 3 * RET_W:4 * RET_W])
        o = 4 * RET_W
        cqv = _dot(h, w_ref[:, o:o + Q_RANK])
        ckvv = _dot(h, w_ref[:, o + Q_RANK:o + Q_RANK + KV_RANK])
        cq_ref[...] = cqv
        ckv_ref[...] = ckvv
        cm, sm = cm_ref[...], sm_ref[...]
        kp = _rope(_dot(h, w_ref[:, o + Q_RANK + KV_RANK:IN_EXT]), cm, sm, ROPE // 2)
        kp = _lane_pair((ts, LANES), QK_AUX, -1.0, -1.0, kp)
        cqn = (cqv * _rstd(cqv) * qnw_ref[...]).astype(BF16)
        ckvn = (ckvv * _rstd(ckvv) * kvnw_ref[...]).astype(BF16)
        for hd in range(N_HEADS):
            qh = _rope(_dot(cqn, wq_ref[hd]), cm, sm, ROPE // 2)
            mq_ref[hd] = (qh * (SM_SCALE * LOG2E)).astype(BF16)
            mk_ref[hd] = (_dot(ckvn, wk_ref[hd]) + kp).astype(BF16)
            mv_ref[hd] = _lane_pair((ts, LANES), V_AUX, 1.0, 1.0, _dot(ckvn, wv_ref[hd])).astype(BF16)

    sd = jax.ShapeDtypeStruct
    hm = sd((N_HEADS, s, LANES), BF16)
    return pl.pallas_call(
        body, name="f1_in_proj", grid=(s // ts,),
        in_specs=[_row(ts, D_MODEL), _full((1, D_MODEL)), _full((D_MODEL, IN_EXT)), _full((1, Q_RANK)), _full((1, KV_RANK)),
                  _full((N_HEADS, Q_RANK, LANES)), _full((N_HEADS, KV_RANK, LANES)), _full((N_HEADS, KV_RANK, LANES)),
                  _row(ts, LANES), _row(ts, LANES), _row(ts, LANES), _row(ts, LANES)],
        out_specs=[_row(ts, RET_W), _row(ts, RET_W), _row(ts, RET_W), _row(ts, RET_W),
                   _row(ts, Q_RANK), _row(ts, KV_RANK)] + [_hrow(N_HEADS, ts, LANES)] * 3 + [_row(ts, 1)],
        out_shape=[sd((s, RET_W), BF16), sd((s, RET_W), BF16), sd((s, RET_W), BF16), sd((s, RET_W), F32),
                   sd((s, Q_RANK), F32), sd((s, KV_RANK), F32), hm, hm, hm, sd((s, 1), F32)],
        compiler_params=_cp("parallel"),
    )(x, anw, win, qnw, kvnw, wq, wk, wv, cos_r, sin_r, cos_m, sin_m)


def _stack_heads(a):
    lo = lax.broadcasted_iota(jnp.int32, a.shape, 1) < HEAD
    zero = jnp.zeros_like(a)
    return jnp.concatenate([jnp.where(lo, a, zero), jnp.where(lo, zero, a)], axis=0)


def _pair_product(a, b2, decay2, w2):
    return _dot((_dot_nt(a, b2) * decay2).astype(BF16), w2)


RET_SLABS = 2


def _ret_specs(tr, tile_of):
    c, ns = RET_CHUNK, RET_SLABS
    return dict(
        slab=pl.BlockSpec((tr, ns * LANES), lambda j, i: (tile_of(i), j)),
        tab=pl.BlockSpec((tr, LANES), lambda j, i: (tile_of(i), 0)),
        vec=pl.BlockSpec((1, ns * LANES), lambda j, i: (0, j)),
        dmask=pl.BlockSpec((ns, c, 2 * c), lambda j, i: (j, 0, 0)),
        rows=pl.BlockSpec((ns, c, LANES), lambda j, i: (j, 0, 0)),
        state=pl.BlockSpec((ns, LANES, LANES), lambda j, i: (j, 0, 0)),
        bd=pl.BlockSpec((LANES, LANES), lambda j, i: (0, 0)))


def _ret_states(a_ref, b_ref, scale_ref, cd_ref, bd, st_ref, chunks, lanes, reverse):
    nc = len(chunks)
    contrib = [[_dot_tn((a_ref[rows, ln].astype(F32) * scale_ref[sl]).astype(BF16), b_ref[rows, ln]) * bd for rows in chunks]
               for sl, ln in enumerate(lanes)]
    states = []
    for sl in range(len(lanes)):
        st, seen = st_ref[sl], [None] * nc
        for ci in (reversed(range(nc)) if reverse else range(nc)):
            seen[ci] = st.astype(BF16)
            st = st * cd_ref[sl] + contrib[sl][ci]
        st_ref[sl] = st
        states.append(seen)
    return states


def _ret_fwd_call(q, k, v, g, gnw, rc, tr):
    s = q.shape[0]
    c = RET_CHUNK
    nc = tr // c
    ns = RET_SLABS

    def body(q_ref, k_ref, v_ref, g_ref, gnw_ref, dm_ref, zeta_ref, xi_ref, cd_ref, bd_ref, o_ref, y_ref, st_ref):
        @pl.when(pl.program_id(1) == 0)
        def _():
            st_ref[...] = jnp.zeros_like(st_ref)

        bd = bd_ref[...]
        chunks = [slice(ci * c, (ci + 1) * c) for ci in range(nc)]
        lanes = [slice(sl * LANES, (sl + 1) * LANES) for sl in range(ns)]
        states = _ret_states(k_ref, v_ref, zeta_ref, cd_ref, bd, st_ref, chunks, lanes, False)
        for ci, rows in enumerate(chunks):
            for sl, ln in enumerate(lanes):
                qc = q_ref[rows, ln]
                o_ref[rows, ln] = (_dot(qc, states[sl][ci]) * xi_ref[sl]
                                   + _pair_product(qc, _stack_heads(k_ref[rows, ln]), dm_ref[sl], _stack_heads(v_ref[rows, ln])))
        avg = bd * (1.0 / HEAD)
        for ln in lanes:
            o = o_ref[:, ln]
            ctr = o - _dot_hi(o, avg)
            var = _dot_hi(ctr * ctr, avg)
            y_ref[:, ln] = (_silu(g_ref[:, ln]) * (ctr * lax.rsqrt(var + EPS) * gnw_ref[:, ln])).astype(BF16)

    specs = _ret_specs(tr, lambda i: i)
    sd = jax.ShapeDtypeStruct
    return pl.pallas_call(
        body, name="ret_fwd", grid=(4 // ns, s // tr),
        in_specs=[specs["slab"]] * 4 + [specs["vec"], specs["dmask"], specs["rows"], specs["rows"], specs["state"], specs["bd"]],
        out_specs=[specs["slab"]] * 2,
        out_shape=[sd((s, RET_W), F32), sd((s, RET_W), BF16)],
        scratch_shapes=[pltpu.VMEM((ns, LANES, LANES), F32)],
        compiler_params=_cp("parallel", "arbitrary"),
    )(q, k, v, g, gnw, rc["dmask"], rc["zeta"], rc["xi"], rc["cd"], rc["bd"])


QK_AUX = HEAD + ROPE
V_AUX = HEAD


def _lane_pair(shape, lo, a, b, rest):
    lane = lax.broadcasted_iota(jnp.int32, shape, len(shape) - 1)
    return jnp.where(lane == lo, a, jnp.where(lane == lo + 1, b, rest))


def _hi_lo(v):
    hi = v.astype(BF16).astype(F32)
    return hi, v - hi


def _flash_fwd_call(q, k, v, tb):
    s = q.shape[1]
    nb = s // tb
    pairs = [(a, b) for a in range(nb) for b in range(a + 1)]
    qi_of, ki_of = (jnp.asarray(np.array(col, np.int32)) for col in zip(*pairs))

    def body(qi_ref, ki_ref, q_ref, k_ref, v_ref, o_ref, qb_ref, m_ref, acc_ref):
        qi, ki = qi_ref[pl.program_id(0)], ki_ref[pl.program_id(0)]

        @pl.when(ki == 0)
        def _():
            m_ref[...] = jnp.full_like(m_ref, NEG)
            acc_ref[...] = jnp.zeros_like(acc_ref)

        def step(masked):
            if masked:
                keep = lax.broadcasted_iota(jnp.int32, (tb, tb), 1) <= lax.broadcasted_iota(jnp.int32, (tb, tb), 0)
            def finish(h, pe, alpha):
                acc_ref[h] = acc_ref[h] * alpha + _dot(pe, v_ref[h])

            nxt, pending = _dot_nt(q_ref[0], k_ref[0]), None
            for h in range(N_HEADS):
                sc = nxt
                if h + 1 < N_HEADS:
                    nxt = _dot_nt(q_ref[h + 1], k_ref[h + 1])
                if masked:
                    sc = jnp.where(keep, sc, NEG)
                m_prev = m_ref[h]
                m_new = jnp.maximum(m_prev, jnp.max(sc, axis=1, keepdims=True))
                pe = jnp.exp2(sc - jnp.tile(m_new, (1, tb // LANES))).astype(BF16)
                m_ref[h] = m_new
                if pending is not None:
                    finish(*pending)
                pending = (h, pe, jnp.exp2(m_prev - m_new))
            finish(*pending)

        @pl.when(ki < qi)
        def _():
            step(False)

        @pl.when(ki == qi)
        def _():
            step(True)
            lane = lax.broadcasted_iota(jnp.int32, (tb, LANES), 1)
            for p in range(N_HEADS // 2):
                outs = []
                for h in (2 * p, 2 * p + 1):
                    acc = acc_ref[h]
                    l = acc[:, V_AUX:V_AUX + 1]
                    outs.append(acc * (1.0 / l))
                    hi, lo = _hi_lo(m_ref[h][:, 0:1] + jnp.log(l) * LOG2E)
                    qb_ref[h] = _lane_pair((tb, LANES), QK_AUX, hi, lo, q_ref[h].astype(F32)).astype(BF16)
                o_ref[:, p * LANES:(p + 1) * LANES] = jnp.where(lane < HEAD, outs[0], pltpu.roll(outs[1], HEAD, 1)).astype(BF16)

    sd = jax.ShapeDtypeStruct
    qspec = pl.BlockSpec((N_HEADS, tb, LANES), lambda p, qi_ref, ki_ref: (0, qi_ref[p], 0))
    kspec = pl.BlockSpec((N_HEADS, tb, LANES), lambda p, qi_ref, ki_ref: (0, ki_ref[p], 0))
    return pl.pallas_call(
        body, name="mla_flash_fwd",
        grid_spec=pltpu.PrefetchScalarGridSpec(
            num_scalar_prefetch=2, grid=(len(pairs),),
            in_specs=[qspec, kspec, kspec],
            out_specs=[pl.BlockSpec((tb, MLA_W), lambda p, qi_ref, ki_ref: (qi_ref[p], 0)), qspec],
            scratch_shapes=[pltpu.VMEM((N_HEADS, tb, LANES), F32), pltpu.VMEM((N_HEADS, tb, LANES), F32)]),
        out_shape=[sd((s, MLA_W), BF16), sd((N_HEADS, s, LANES), BF16)],
        compiler_params=_cp("arbitrary"),
    )(qi_of, ki_of, q, k, v)


def _out_proj_call(x, yret, ymla, wout, ts):
    s = x.shape[0]

    def body(x_ref, yr_ref, ym_ref, w_ref, x1_ref, r_ref):
        x1 = x_ref[...] + _dot(yr_ref[...], w_ref[0:RET_W, :]) + _dot(ym_ref[...], w_ref[RET_W:, :])
        x1_ref[...] = x1
        r_ref[...] = _rstd(x1)

    sd = jax.ShapeDtypeStruct
    return pl.pallas_call(
        body, name="out_proj", grid=(s // ts,),
        in_specs=[_row(ts, D_MODEL), _row(ts, RET_W), _row(ts, MLA_W), _full((D_MODEL, D_MODEL))],
        out_specs=[_row(ts, D_MODEL), _row(ts, 1)],
        out_shape=[sd((s, D_MODEL), F32), sd((s, 1), F32)],
        compiler_params=_cp("parallel"),
    )(x, yret, ymla, wout)


W_UP_SHARD = F2 // 4


def _ffn_fwd_call(x1, r2, fnw, wup4, cw, cb, wdown, tgt, fw, ts):
    s = x1.shape[0]
    wsh = W_UP_SHARD

    def body(x_ref, r_ref, fnw_ref, wup_ref, cw_ref, cb_ref, wd_ref, t_ref, fw_ref,
             u_ref, uc_ref, dx2_ref, loss_ref, gfw_ref, carry_ref):
        _zero_first(pl.program_id(0) == 0, carry_ref, loss_ref, gfw_ref)
        xv = x_ref[...]
        h = (xv * r_ref[...] * fnw_ref[...]).astype(BF16)
        conv = []
        for j in range(4):
            cols = slice(j * wsh, (j + 1) * wsh)
            ub = _dot(h, wup_ref[j]).astype(BF16)
            u_ref[:, cols] = ub
            u = ub.astype(F32)
            u1, u2 = _shifted(u, carry_ref[:, cols])
            w = cw_ref[:, cols]
            cb16 = (cb_ref[:, cols] + w[0:1, :] * u2 + w[1:2, :] * u1 + w[2:3, :] * u).astype(BF16)
            uc_ref[:, cols] = cb16
            conv.append(cb16.astype(F32))
            carry_ref[:, cols] = u[ts - 8:, :]
        acc = xv
        for j in range(2):
            a = (_silu(conv[j]) * conv[j + 2]).astype(BF16)
            acc = acc + _dot(a, wd_ref[j * wsh:(j + 1) * wsh, :])
        r = _rstd(acc)
        xh = acc * r
        fwv = fw_ref[...]
        e = xh * fwv - t_ref[...]
        loss_ref[...] += (0.5 / D_MODEL) * _colsum(jnp.sum(e * e, axis=1, keepdims=True))
        dy = e * (1.0 / D_MODEL)
        gfw_ref[...] += _colsum(dy * xh)
        dx2_ref[...] = _norm_bwd(dy, xh, r, fwv)

    sd = jax.ShapeDtypeStruct
    once = lambda shape: pl.BlockSpec(shape, lambda i: (0,) * len(shape), pipeline_mode=pl.Buffered(1))
    return pl.pallas_call(
        body, name="ffn_fwd_loss", grid=(s // ts,),
        in_specs=[_row(ts, D_MODEL), _row(ts, 1), once((1, D_MODEL)), once((4, D_MODEL, wsh)),
                  once((3, F2)), once((1, F2)), once((D_FF, D_MODEL)), _row(ts, D_MODEL), once((1, D_MODEL))],
        out_specs=[_row(ts, F2), _row(ts, F2), _row(ts, D_MODEL), _full((1, 1)), _full((1, D_MODEL))],
        out_shape=[sd((s, F2), BF16), sd((s, F2), BF16), sd((s, D_MODEL), F32), sd((1, 1), F32), sd((1, D_MODEL), F32)],
        scratch_shapes=[pltpu.VMEM((8, F2), F32)],
        compiler_params=_cp("arbitrary", vmem=VMEM_LIMIT_MLP),
    )(x1, r2, fnw, wup4, cw, cb, wdown, tgt, fw)


def _shifted(u, hal):
    row = lax.broadcasted_iota(jnp.int32, hal.shape, 0)
    r1, r2 = pltpu.roll(u, 1, 0), pltpu.roll(u, 2, 0)
    top1 = jnp.where(row == 0, hal[7:8, :], r1[0:8, :])
    top2 = jnp.where(row == 0, hal[6:7, :], jnp.where(row == 1, hal[7:8, :], r2[0:8, :]))
    return jnp.concatenate([top1, r1[8:, :]], axis=0), jnp.concatenate([top2, r2[8:, :]], axis=0)


def _win_ext_call(win):
    blocks = win[None] if win.ndim == 2 else win
    nb, r, wb = blocks.shape
    tr = min(r, 256)

    def body(b_ref, o_ref):
        left, right, at = [], [], IN_W - ROPE
        for j in range(nb):
            blk = b_ref[j]
            cut = min(max(at - j * wb, 0), wb)
            left += [blk[:, :cut]] if cut else []
            right += [blk[:, cut:]] if cut < wb else []
        pad = lambda n: jnp.zeros((tr, n), o_ref.dtype)
        o_ref[...] = jnp.concatenate(left + [pad(KPE_LO)] + right + [pad(LANES - KPE_LO - ROPE)], -1)

    return pl.pallas_call(
        body, name="w_in_layout", grid=(r // tr,),
        in_specs=[pl.BlockSpec((nb, tr, wb), lambda i: (0, i, 0))], out_specs=_row(tr, IN_EXT),
        out_shape=jax.ShapeDtypeStruct((r, IN_EXT), win.dtype),
        compiler_params=_cp("parallel"),
    )(blocks)


def _win_grad_blocks_call(g_ext, nb):
    r = g_ext.shape[0]
    wb = IN_W // nb
    tr = min(r, 256)
    lo = IN_W - ROPE

    def body(g_ref, o_ref):
        g = g_ref[...]
        for j in range(nb):
            a, b = j * wb, (j + 1) * wb
            parts = ([g[:, a:min(b, lo)]] if a < lo else []) + ([g[:, max(a, lo) + KPE_LO:b + KPE_LO]] if b > lo else [])
            o_ref[j] = jnp.concatenate(parts, -1)

    return pl.pallas_call(
        body, name="w_in_grad_blocks", grid=(r // tr,),
        in_specs=[_row(tr, IN_EXT)], out_specs=pl.BlockSpec((nb, tr, wb), lambda i: (0, i, 0)),
        out_shape=jax.ShapeDtypeStruct((nb, r, wb), g_ext.dtype),
        compiler_params=_cp("parallel"),
    )(g_ext)


def _prep_weights(w):
    win_ext = _win_ext_call(w["w_in"])
    wuq = w["w_uq"].reshape(Q_RANK, N_HEADS, HEAD + ROPE)
    wq = jnp.concatenate([wuq, jnp.zeros((Q_RANK, N_HEADS, LANES - HEAD - ROPE), wuq.dtype)], -1).transpose(1, 0, 2)
    wukv = w["w_ukv"].reshape(KV_RANK, N_HEADS, 2 * HEAD)
    zk = jnp.zeros((KV_RANK, N_HEADS, HEAD), wukv.dtype)
    wk = jnp.concatenate([wukv[:, :, :HEAD], zk], -1).transpose(1, 0, 2)
    wv = jnp.concatenate([wukv[:, :, HEAD:], zk], -1).transpose(1, 0, 2)
    c = lambda a: a.astype(BF16)
    return dict(win=c(win_ext), wq=c(wq), wk=c(wk), wv=c(wv), wout=c(w["w_out"]))


def _prep_mlp_weights(w):
    wup = w["w_up"]
    if wup.ndim == 2:
        wup = wup.reshape(D_MODEL, 4, W_UP_SHARD).transpose(1, 0, 2)
    return dict(wup=wup.astype(BF16), wdown=w["w_down"].astype(BF16))


def _tiles(s):
    return dict(ts=min(s, 512), tr=min(s, 2048), tbf=min(s, 1024), tb=min(s, 512), t2=min(s, 256),
                tw=min(s, 2048), t1=min(s, 1024))


class _Exchanges:
    def __init__(self, w):
        self.w = w

    def mlp_weights(self, after):
        return self.w

    def mlp_grads(self, gw):
        pass

    def behind_out_bwd(self, after):
        pass

    def behind_attention(self, after):
        pass


def _forward(x, positions, tgt, w, small, ex):
    s = x.shape[0]
    t = _tiles(s)
    pw = _prep_weights(w)
    cos_r, sin_r, cos_m, sin_m = _rope_tables(positions)
    rc = _ret_consts()
    q, k, v, g, cq, ckv, mq, mk, mv, r1 = _f1_call(
        x, small["attn_norm_w"], pw["win"], small["mla_q_norm_w"], small["mla_kv_norm_w"], pw["wq"], pw["wk"], pw["wv"],
        cos_r, sin_r, cos_m, sin_m, t["ts"])
    o_ret, y_ret = _ret_fwd_call(q, k, v, g, small["ret_gn_w"], rc, t["tr"])
    y_mla, mqb = _flash_fwd_call(mq, mk, mv, t["tbf"])
    x1, r2 = _out_proj_call(x, y_ret, y_mla, pw["wout"], t["ts"])
    pw.update(_prep_mlp_weights(ex.mlp_weights(r2)))
    u, uc, dx2, loss, g_fw = _ffn_fwd_call(x1, r2, small["ffn_norm_w"], pw["wup"], w["conv_w"], small["conv_b"], pw["wdown"],
                                           tgt, small["final_norm_w"], t["ts"])
    return dict(pw=pw, tabs=(cos_r, sin_r, cos_m, sin_m), rc=rc, q=q, k=k, v=v, g=g, cq=cq, ckv=ckv, r1=r1,
                o_ret=o_ret, y_ret=y_ret, mqb=mqb, mk=mk, mv=mv, y_mla=y_mla, x1=x1, r2=r2, u=u, uc=uc,
                dx2=dx2, loss=loss, g_fw=g_fw)


def _norm_bwd(dh, xh, r, nw):
    dxn = dh * nw
    return r * (dxn - xh * jnp.mean(dxn * xh, axis=-1, keepdims=True))


def _ordered_after(body, order):
    if order is None:
        return body, [], []
    return (lambda order_ref, *refs: body(*refs)), [pl.BlockSpec(memory_space=pl.ANY)], [order]


def _zero_first(first, *refs):
    @pl.when(first)
    def _():
        for ref in refs:
            ref[...] = jnp.zeros_like(ref)


def _colsum(v):
    return jnp.sum(v, axis=0, keepdims=True)


def _dsilu(g, sg):
    return sg * (1.0 + g * (1.0 - sg))


def _ffn_bwd_call(dx2, u, uc, cw, wdown, wup4, x1, r2, fnw, ts):
    s = dx2.shape[0]
    nt = s // ts
    wsh = W_UP_SHARD
    rev = lambda i: nt - 1 - i

    def body(dx2_ref, u_ref, uc_ref, cw_ref, wd_ref, wup_ref, x_ref, r_ref, fnw_ref,
             du_ref, dx1_ref, dcw_ref, dcb_ref, dfnw_ref, dwd_hbm, carry_ref, dwd_ref, sem):
        i = pl.program_id(0)
        _zero_first(i == 0, carry_ref, dwd_ref, dcw_ref, dcb_ref, dfnw_ref)
        dxb = dx2_ref[...].astype(BF16)
        dh = jnp.zeros((ts, D_MODEL), F32)
        for j in range(2):
            gcols = slice(j * wsh, (j + 1) * wsh)
            vcols = slice(D_FF + j * wsh, D_FF + (j + 1) * wsh)
            gate, val = uc_ref[:, gcols].astype(F32), uc_ref[:, vcols].astype(F32)
            da = _dot_nt(dxb, wd_ref[gcols, :])
            sg = _sigmoid(gate)
            sl = gate * sg
            dwd_ref[gcols, :] += _dot_tn((sl * val).astype(BF16), dxb)
            for d, cols, shard in ((da * val * _dsilu(gate, sg), gcols, j), (da * sl, vcols, 2 + j)):
                d1, d2 = _shifted_up(d, carry_ref[:, cols])
                uv = u_ref[:, cols].astype(F32)
                for t, dt in enumerate((d2, d1, d)):
                    dcw_ref[t:t + 1, cols] += _colsum(dt * uv)
                dcb_ref[:, cols] += _colsum(d)
                w = cw_ref[:, cols]
                du = (w[2:3, :] * d + w[1:2, :] * d1 + w[0:1, :] * d2).astype(BF16)
                du_ref[:, cols] = du
                dh = dh + _dot_nt(du, wup_ref[shard])
                carry_ref[:, cols] = d[0:8, :]
        r = r_ref[...]
        xh = x_ref[...] * r
        dfnw_ref[...] += _colsum(dh * xh)
        dx1_ref[...] = dx2_ref[...] + _norm_bwd(dh, xh, r, fnw_ref[...])

        @pl.when(i == nt - 1)
        def _():
            cp = pltpu.make_async_copy(dwd_ref, dwd_hbm, sem)
            cp.start()
            cp.wait()

    sd = jax.ShapeDtypeStruct
    row = lambda c: pl.BlockSpec((ts, c), lambda i: (rev(i), 0))
    once = lambda shape: pl.BlockSpec(shape, lambda i: (0,) * len(shape), pipeline_mode=pl.Buffered(1))
    return pl.pallas_call(
        body, name="ffn_bwd", grid=(nt,),
        in_specs=[row(D_MODEL), row(F2), row(F2), once((3, F2)), once((D_FF, D_MODEL)), once((4, D_MODEL, wsh)),
                  row(D_MODEL), row(1), once((1, D_MODEL))],
        out_specs=[row(F2), row(D_MODEL), _full((3, F2)), _full((1, F2)), _full((1, D_MODEL)), pl.BlockSpec(memory_space=pl.ANY)],
        out_shape=[sd((s, F2), BF16), sd((s, D_MODEL), F32), sd((3, F2), F32), sd((1, F2), F32), sd((1, D_MODEL), F32),
                   sd((D_FF, D_MODEL), F32)],
        scratch_shapes=[pltpu.VMEM((8, F2), F32), pltpu.VMEM((D_FF, D_MODEL), F32), pltpu.SemaphoreType.DMA],
        compiler_params=_cp("arbitrary", vmem=VMEM_LIMIT_MLP),
    )(dx2, u, uc, cw, wdown, wup4, x1, r2, fnw)


def _shifted_up(d, hal):
    n = d.shape[0]
    row = lax.broadcasted_iota(jnp.int32, hal.shape, 0)
    r1, r2 = pltpu.roll(d, n - 1, 0), pltpu.roll(d, n - 2, 0)
    end1 = jnp.where(row == 7, hal[0:1, :], r1[n - 8:, :])
    end2 = jnp.where(row == 6, hal[0:1, :], jnp.where(row == 7, hal[1:2, :], r2[n - 8:, :]))
    return jnp.concatenate([r1[:n - 8, :], end1], axis=0), jnp.concatenate([r2[:n - 8, :], end2], axis=0)


def _dw_norm_call(x, r, nw, b, ts, tn, name):
    s, n = b.shape
    k = x.shape[1]

    def body(x_ref, r_ref, nw_ref, b_ref, dw_ref):
        _zero_first(pl.program_id(1) == 0, dw_ref)
        h = (x_ref[...] * r_ref[...] * nw_ref[...]).astype(BF16)
        dw_ref[...] += _dot_tn(h, b_ref[...])

    return pl.pallas_call(
        body, name=name, grid=(n // tn, s // ts),
        in_specs=[pl.BlockSpec((ts, k), lambda j, i: (i, 0)), pl.BlockSpec((ts, 1), lambda j, i: (i, 0)),
                  pl.BlockSpec((1, k), lambda j, i: (0, 0)), pl.BlockSpec((ts, tn), lambda j, i: (i, j))],
        out_specs=pl.BlockSpec((None, k, tn), lambda j, i: (j, 0, 0)),
        out_shape=jax.ShapeDtypeStruct((n // tn, k, tn), F32),
        compiler_params=_cp("parallel", "arbitrary"),
    )(x, r, nw, b)


def _out_bwd_call(dx1, yret, ymla, wout, ts, order=None):
    s = dx1.shape[0]

    def body(dx_ref, yr_ref, ym_ref, w_ref, dyr_ref, do_ref, dwo_ref):
        _zero_first(pl.program_id(0) == 0, dwo_ref)
        dxb = dx_ref[...].astype(BF16)
        dmix = _dot_nt(dxb, w_ref[...])
        dyr_ref[...] = dmix[:, :RET_W]
        ym = ym_ref[...]
        lane = lax.broadcasted_iota(jnp.int32, (ts, LANES), 1)
        for p in range(N_HEADS // 2):
            dom = dmix[:, RET_W + p * LANES:RET_W + (p + 1) * LANES]
            prod = dom * ym[:, p * LANES:(p + 1) * LANES].astype(F32)
            for hh in range(2):
                mine = (lane >= HEAD) if hh else (lane < HEAD)
                hi, lo = _hi_lo(jnp.sum(jnp.where(mine, prod, 0.0), axis=1, keepdims=True))
                base = jnp.where(lane < HEAD, pltpu.roll(dom, HEAD, 1) if hh else dom, 0.0)
                do_ref[2 * p + hh] = _lane_pair((ts, LANES), V_AUX, -hi, -lo, base).astype(BF16)
        dwo_ref[0:RET_W, :] += _dot_tn(yr_ref[...], dxb)
        dwo_ref[RET_W:, :] += _dot_tn(ym, dxb)

    sd = jax.ShapeDtypeStruct
    body, first_specs, first = _ordered_after(body, order)
    return pl.pallas_call(
        body, name="out_proj_bwd", grid=(s // ts,),
        in_specs=first_specs + [_row(ts, D_MODEL), _row(ts, RET_W), _row(ts, MLA_W), _full((D_MODEL, D_MODEL))],
        out_specs=[_row(ts, RET_W), _hrow(N_HEADS, ts, LANES), _full((D_MODEL, D_MODEL))],
        out_shape=[sd((s, RET_W), F32), sd((N_HEADS, s, LANES), BF16), sd((D_MODEL, D_MODEL), F32)],
        compiler_params=_cp("arbitrary"),
    )(*first, dx1, yret, ymla, wout)


def _ret_bwd_q_call(q, k, v, o, g, dy, gnw, rc, cos_r, sin_r, tr):
    s = q.shape[0]
    c = RET_CHUNK
    nc = tr // c
    ns = RET_SLABS

    def body(q_ref, k_ref, v_ref, o_ref, g_ref, dy_ref, gnw_ref, dm_ref, zeta_ref, xi_ref, cd_ref, bd_ref, cr_ref, sr_ref,
             dq_ref, dg_ref, do_ref, dgnw_ref, st_ref):
        _zero_first(pl.program_id(1) == 0, st_ref, dgnw_ref)
        bd = bd_ref[...]
        avg = bd * (1.0 / HEAD)
        chunks = [slice(ci * c, (ci + 1) * c) for ci in range(nc)]
        lanes = [slice(sl * LANES, (sl + 1) * LANES) for sl in range(ns)]
        dov = []
        for ln in lanes:
            ov = o_ref[:, ln]
            ctr = ov - _dot_hi(ov, avg)
            rs = lax.rsqrt(_dot_hi(ctr * ctr, avg) + EPS)
            oh = ctr * rs
            gg, dyv, gnw_v = g_ref[:, ln], dy_ref[:, ln], gnw_ref[:, ln]
            sg = _sigmoid(gg)
            sl = gg * sg
            dg_ref[:, ln] = (dyv * oh * gnw_v * _dsilu(gg, sg)).astype(BF16)
            dgnw_ref[:, ln] += _colsum(dyv * sl * oh)
            doh = dyv * sl * gnw_v
            dov.append((rs * (doh - _dot_hi(doh, avg) - oh * _dot_hi(doh * oh, avg))).astype(BF16))
            do_ref[:, ln] = dov[-1]
        states = _ret_states(k_ref, v_ref, zeta_ref, cd_ref, bd, st_ref, chunks, lanes, False)
        for ci, rows in enumerate(chunks):
            for sl, ln in enumerate(lanes):
                doc = dov[sl][rows, :]
                dq = (_dot_nt(doc, states[sl][ci]) * xi_ref[sl]
                      + _pair_product(doc, _stack_heads(v_ref[rows, ln]), dm_ref[sl], _stack_heads(k_ref[rows, ln])))
                dq_ref[rows, ln] = _unrope(dq, cr_ref[rows, :], sr_ref[rows, :], HEAD // 2).astype(BF16)

    specs = _ret_specs(tr, lambda i: i)
    sd = jax.ShapeDtypeStruct
    return pl.pallas_call(
        body, name="ret_bwd_q", grid=(4 // ns, s // tr),
        in_specs=[specs["slab"]] * 6 + [specs["vec"], specs["dmask"], specs["rows"], specs["rows"], specs["state"], specs["bd"],
                                        specs["tab"], specs["tab"]],
        out_specs=[specs["slab"]] * 3 + [specs["vec"]],
        out_shape=[sd((s, RET_W), BF16), sd((s, RET_W), BF16), sd((s, RET_W), BF16), sd((1, RET_W), F32)],
        scratch_shapes=[pltpu.VMEM((ns, LANES, LANES), F32)],
        compiler_params=_cp("parallel", "arbitrary"),
    )(q, k, v, o, g, dy, gnw, rc["dmask"], rc["zeta"], rc["xi"], rc["cd"], rc["bd"], cos_r, sin_r)


def _ret_bwd_kv_call(q, k, v, do, rc, cos_r, sin_r, tr):
    s = q.shape[0]
    c = RET_CHUNK
    nc = tr // c
    nt = s // tr
    ns = RET_SLABS

    def body(q_ref, k_ref, v_ref, do_ref, dm_ref, zeta_ref, xi_ref, cd_ref, bd_ref, cr_ref, sr_ref, dk_ref, dv_ref, gs_ref):
        _zero_first(pl.program_id(1) == 0, gs_ref)
        bd = bd_ref[...]
        chunks = [slice(ci * c, (ci + 1) * c) for ci in range(nc)]
        lanes = [slice(sl * LANES, (sl + 1) * LANES) for sl in range(ns)]
        states = _ret_states(q_ref, do_ref, xi_ref, cd_ref, bd, gs_ref, chunks, lanes, True)
        for ci, rows in enumerate(chunks):
            for sl, ln in enumerate(lanes):
                kc, vc = k_ref[rows, ln], v_ref[rows, ln]
                q2, do2 = _stack_heads(q_ref[rows, ln]), _stack_heads(do_ref[rows, ln])
                gb = states[sl][ci]
                dk = _dot_nt(vc, gb) * zeta_ref[sl] + _pair_product(vc, do2, dm_ref[sl], q2)
                dv = _dot(kc, gb) * zeta_ref[sl] + _pair_product(kc, q2, dm_ref[sl], do2)
                dk_ref[rows, ln] = (_unrope(dk, cr_ref[rows, :], sr_ref[rows, :], HEAD // 2) * (HEAD ** -0.5)).astype(BF16)
                dv_ref[rows, ln] = dv.astype(BF16)

    specs = _ret_specs(tr, lambda i: nt - 1 - i)
    sd = jax.ShapeDtypeStruct
    return pl.pallas_call(
        body, name="ret_bwd_kv", grid=(4 // ns, nt),
        in_specs=[specs["slab"]] * 4 + [specs["dmask"], specs["rows"], specs["rows"], specs["state"], specs["bd"],
                                        specs["tab"], specs["tab"]],
        out_specs=[specs["slab"]] * 2,
        out_shape=[sd((s, RET_W), BF16), sd((s, RET_W), BF16)],
        scratch_shapes=[pltpu.VMEM((ns, LANES, LANES), F32)],
        compiler_params=_cp("parallel", "arbitrary"),
    )(q, k, v, do, rc["dmask_t"], rc["zeta"], rc["xi"], rc["cd"], rc["bd"], cos_r, sin_r)


FLASH_BWD_HEADS = 8


def _flash_bwd_call(qb, k, v, do, tb, order=None):
    s = qb.shape[1]
    nb = s // tb
    hg = FLASH_BWD_HEADS
    pairs = [(a, b) for a in range(nb) for b in range(a, nb)]
    ki_of, qi_of = (jnp.asarray(np.array(col, np.int32)) for col in zip(*pairs))
    extra = [] if order is None else [order]

    def body(ki_ref, qi_ref, *refs):
        q_ref, k_ref, v_ref, do_ref, dk_ref, dv_ref, dq_hbm, dka_ref, dva_ref, dq_ref, sem = refs[len(extra):]
        g, p = pl.program_id(0), pl.program_id(1)
        ki, qi = ki_ref[p], qi_ref[p]
        _zero_first(p == 0, dq_ref)
        _zero_first(qi == ki, dka_ref, dva_ref)
        rows = pl.ds(pl.multiple_of(qi * tb, tb), tb)

        def step(masked):
            if masked:
                keep = lax.broadcasted_iota(jnp.int32, (tb, tb), 0) <= lax.broadcasted_iota(jnp.int32, (tb, tb), 1)
            for h in range(hg):
                st = _dot_nt(k_ref[h], q_ref[h])
                if masked:
                    st = jnp.where(keep, st, NEG)
                pt = jnp.exp2(st)
                dob = do_ref[h]
                dva_ref[h] += _dot(pt.astype(BF16), dob)
                dst = (pt * _dot_nt(v_ref[h], dob)).astype(BF16)
                dka_ref[h] += _dot(dst, q_ref[h])
                dq_ref[h, rows, :] += _dot_tn(dst, k_ref[h])

        @pl.when(qi > ki)
        def _():
            step(False)

        @pl.when(qi == ki)
        def _():
            step(True)

        @pl.when(qi == nb - 1)
        def _():
            dk_ref[...] = (dka_ref[...] * LN2).astype(BF16)
            dv_ref[...] = dva_ref[...].astype(BF16)

        @pl.when(p == len(pairs) - 1)
        def _():
            cp = pltpu.make_async_copy(dq_ref, dq_hbm.at[pl.ds(g * hg, hg)], sem)
            cp.start()
            cp.wait()

    kspec = pl.BlockSpec((hg, tb, LANES), lambda g, p, ki_ref, qi_ref: (g, ki_ref[p], 0))
    qspec = pl.BlockSpec((hg, tb, LANES), lambda g, p, ki_ref, qi_ref: (g, qi_ref[p], 0))
    hm = jax.ShapeDtypeStruct((N_HEADS, s, LANES), BF16)
    return pl.pallas_call(
        body, name="mla_flash_bwd",
        grid_spec=pltpu.PrefetchScalarGridSpec(
            num_scalar_prefetch=2, grid=(N_HEADS // hg, len(pairs)),
            in_specs=[ANY] * len(extra) + [qspec, kspec, kspec, qspec],
            out_specs=[kspec, kspec, ANY],
            scratch_shapes=[pltpu.VMEM((hg, tb, LANES), F32), pltpu.VMEM((hg, tb, LANES), F32),
                            pltpu.VMEM((hg, s, LANES), F32), pltpu.SemaphoreType.DMA]),
        out_shape=[hm, hm, jax.ShapeDtypeStruct((N_HEADS, s, LANES), F32)],
        compiler_params=_cp("arbitrary", "arbitrary"),
    )(ki_of, qi_of, *extra, qb, k, v, do)


def _mla_post_call(dq, dk, dv, cq, ckv, qnw, kvnw, wq, wk, wv, cos_m, sin_m, ts):
    s = cq.shape[0]

    def body(dq_ref, dk_ref, dv_ref, cq_ref, ckv_ref, qnw_ref, kvnw_ref, wq_ref, wk_ref, wv_ref, cm_ref, sm_ref,
             dcq_ref, dckv_ref, dkpe_ref, dwq_ref, dwk_ref, dwv_ref, dqnw_ref, dkvnw_ref):
        _zero_first(pl.program_id(0) == 0, dwq_ref, dwk_ref, dwv_ref, dqnw_ref, dkvnw_ref)
        cqv, ckvv = cq_ref[...], ckv_ref[...]
        rq, rkv = _rstd(cqv), _rstd(ckvv)
        qh_, kvh_ = cqv * rq, ckvv * rkv
        qnw_v, kvnw_v = qnw_ref[...], kvnw_ref[...]
        cqn = (qh_ * qnw_v).astype(BF16)
        ckvn = (kvh_ * kvnw_v).astype(BF16)
        cm, sm = cm_ref[...], sm_ref[...]
        dcqn = jnp.zeros((ts, Q_RANK), F32)
        dckvn = jnp.zeros((ts, KV_RANK), F32)
        dkpe = jnp.zeros((ts, LANES), F32)
        for h in range(N_HEADS):
            dqu = _unrope(dq_ref[h] * SM_SCALE, cm, sm, ROPE // 2).astype(BF16)
            dwq_ref[h] += _dot_tn(cqn, dqu)
            dcqn = dcqn + _dot_nt(dqu, wq_ref[h])
            dkb, dvb = dk_ref[h], dv_ref[h]
            dkpe = dkpe + dkb.astype(F32)
            dwk_ref[h] += _dot_tn(ckvn, dkb)
            dwv_ref[h] += _dot_tn(ckvn, dvb)
            dckvn = dckvn + _dot_nt(dkb, wk_ref[h]) + _dot_nt(dvb, wv_ref[h])
        lane = lax.broadcasted_iota(jnp.int32, (ts, LANES), 1)
        dkpe = jnp.where((lane >= KPE_LO) & (lane < KPE_LO + ROPE), dkpe, 0.0)
        dkpe_ref[...] = _unrope(dkpe, cm, sm, ROPE // 2).astype(BF16)
        dqnw_ref[...] += _colsum(dcqn * qh_)
        dkvnw_ref[...] += _colsum(dckvn * kvh_)
        dcq_ref[...] = _norm_bwd(dcqn, qh_, rq, qnw_v).astype(BF16)
        dckv_ref[...] = _norm_bwd(dckvn, kvh_, rkv, kvnw_v).astype(BF16)

    sd = jax.ShapeDtypeStruct
    hm = _hrow(N_HEADS, ts, LANES)
    return pl.pallas_call(
        body, name="mla_post", grid=(s // ts,),
        in_specs=[hm, hm, hm, _row(ts, Q_RANK), _row(ts, KV_RANK), _full((1, Q_RANK)), _full((1, KV_RANK)),
                  _full((N_HEADS, Q_RANK, LANES)), _full((N_HEADS, KV_RANK, LANES)), _full((N_HEADS, KV_RANK, LANES)),
                  _row(ts, LANES), _row(ts, LANES)],
        out_specs=[_row(ts, Q_RANK), _row(ts, KV_RANK), _row(ts, LANES),
                   _full((N_HEADS, Q_RANK, LANES)), _full((N_HEADS, KV_RANK, LANES)), _full((N_HEADS, KV_RANK, LANES)),
                   _full((1, Q_RANK)), _full((1, KV_RANK))],
        out_shape=[sd((s, Q_RANK), BF16), sd((s, KV_RANK), BF16), sd((s, LANES), BF16),
                   sd((N_HEADS, Q_RANK, LANES), F32), sd((N_HEADS, KV_RANK, LANES), F32), sd((N_HEADS, KV_RANK, LANES), F32),
                   sd((1, Q_RANK), F32), sd((1, KV_RANK), F32)],
        compiler_params=_cp("arbitrary"),
    )(dq, dk, dv, cq, ckv, qnw, kvnw, wq, wk, wv, cos_m, sin_m)


def _in_bwd_call(parts, x, r1, anw, dx1, win, ts):
    s = x.shape[0]
    widths = [p.shape[1] for p in parts]
    np_ = len(parts)

    def body(*refs):
        p_refs = refs[:np_]
        x_ref, r_ref, anw_ref, dx1_ref, w_ref, dx_ref, dw_ref, danw_ref = refs[np_:]
        _zero_first(pl.program_id(0) == 0, dw_ref, danw_ref)
        dproj = jnp.concatenate([p[...] for p in p_refs], axis=-1)
        r, anw_v = r_ref[...], anw_ref[...]
        xh = x_ref[...] * r
        dw_ref[...] += _dot_tn((xh * anw_v).astype(BF16), dproj)
        dh = _dot_nt(dproj, w_ref[...])
        danw_ref[...] += _colsum(dh * xh)
        dx_ref[...] = dx1_ref[...] + _norm_bwd(dh, xh, r, anw_v)

    sd = jax.ShapeDtypeStruct
    return pl.pallas_call(
        body, name="in_proj_bwd", grid=(s // ts,),
        in_specs=[_row(ts, w) for w in widths]
        + [_row(ts, D_MODEL), _row(ts, 1), _full((1, D_MODEL)), _row(ts, D_MODEL), _full((D_MODEL, IN_EXT))],
        out_specs=[_row(ts, D_MODEL), _full((D_MODEL, IN_EXT)), _full((1, D_MODEL))],
        out_shape=[sd((s, D_MODEL), F32), sd((D_MODEL, IN_EXT), F32), sd((1, D_MODEL), F32)],
        compiler_params=_cp("arbitrary"),
    )(*parts, x, r1, anw, dx1, win)


def _local_step(x, positions, tgt, w, small, ex=None):
    s = x.shape[0]
    t = _tiles(s)
    ex = _Exchanges(w) if ex is None else ex
    f = _forward(x, positions, tgt, w, small, ex)
    pw, rc = f["pw"], f["rc"]
    cos_r, sin_r, cos_m, sin_m = f["tabs"]
    dx2, loss, g_fw = f["dx2"], f["loss"], f["g_fw"]
    du, dx1, g_cw, g_cb, g_fnw, g_wd = _ffn_bwd_call(dx2, f["u"], f["uc"], w["conv_w"], pw["wdown"], pw["wup"],
                                                     f["x1"], f["r2"], small["ffn_norm_w"], t["t2"])
    g_wup = _dw_norm_call(f["x1"], f["r2"], small["ffn_norm_w"], du, t["tw"], F2 // 4, "dw_up")
    started = ex.mlp_grads(dict(w_up=g_wup, w_down=g_wd))
    dy_ret, do, g_wout = _out_bwd_call(dx1, f["y_ret"], f["y_mla"], pw["wout"], t["t1"], started)
    started = ex.behind_out_bwd(g_wout)
    drq, dg, do_ret, g_gnw = _ret_bwd_q_call(f["q"], f["k"], f["v"], f["o_ret"], f["g"], dy_ret, small["ret_gn_w"], rc, cos_r, sin_r, t["tr"])
    drk, drv = _ret_bwd_kv_call(f["q"], f["k"], f["v"], do_ret, rc, cos_r, sin_r, t["tr"])
    dmk, dmv, dmq = _flash_bwd_call(f["mqb"], f["mk"], f["mv"], do, t["tb"], started)
    ex.behind_attention(dmk)
    dcq, dckv, dkpe, g_wq, g_wk, g_wv, g_qnw, g_kvnw = _mla_post_call(
        dmq, dmk, dmv, f["cq"], f["ckv"], small["mla_q_norm_w"], small["mla_kv_norm_w"], pw["wq"], pw["wk"], pw["wv"], cos_m, sin_m, t["ts"])
    gx, g_win_ext, g_anw = _in_bwd_call([drq, drk, drv, dg, dcq, dckv, dkpe], x, f["r1"], small["attn_norm_w"], dx1, pw["win"], t["ts"])
    if w["w_in"].ndim == 3:
        g_win = _win_grad_blocks_call(g_win_ext, w["w_in"].shape[0])
    else:
        g_win = _win_grad_blocks_call(g_win_ext, 1)[0]
    g_wuq = g_wq.transpose(1, 0, 2)[:, :, :HEAD + ROPE].reshape(Q_RANK, N_HEADS * (HEAD + ROPE))
    g_wukv = jnp.concatenate([g_wk[:, :, :HEAD], g_wv[:, :, :HEAD]], -1).transpose(1, 0, 2).reshape(KV_RANK, 2 * MLA_W)
    gw = dict(w_in=g_win, w_uq=g_wuq, w_ukv=g_wukv, w_out=g_wout, w_up=g_wup,
              conv_w=g_cw, w_down=g_wd)
    gs = dict(attn_norm_w=g_anw, ret_gn_w=g_gnw, mla_q_norm_w=g_qnw, mla_kv_norm_w=g_kvnw, ffn_norm_w=g_fnw,
              conv_b=g_cb, final_norm_w=g_fw)
    return loss, gx, gw, gs


MESH_ID = pl.DeviceIdType.MESH
ANY = pl.BlockSpec(memory_space=pl.ANY)
VMEM_SPEC = pl.BlockSpec(memory_space=pltpu.VMEM)
N_DEV = 8
GROUP_A = (("w_in", (D_MODEL, IN_W // 4), 1), ("w_uq", (Q_RANK, 192), 1), ("w_ukv", (KV_RANK, 256), 1),
           ("w_out", (D_MODEL // 4, D_MODEL), 0))
GROUP_B = (("w_up", (D_MODEL, F2 // 4), 1), ("w_down", (D_FF // 4, D_MODEL), 0))
HBM_SPEC = pl.BlockSpec(memory_space=pltpu.HBM)
SEM_SPEC = pl.BlockSpec(memory_space=pltpu.SEMAPHORE)


def _mesh_pos():
    return lax.axis_index("x"), lax.axis_index("y"), lax.axis_index("c")


def _other_chips(x, y):
    return [(1 - x, y), (x, 1 - y), (1 - x, 1 - y)]


def _remote(src, dst, send_sems, recv_sems, k, dev):
    return pltpu.make_async_remote_copy(src_ref=src, dst_ref=dst, send_sem=send_sems.at[k], recv_sem=recv_sems.at[k],
                                        device_id=dev, device_id_type=MESH_ID)


def _gather_list_call(parts, tag):
    n = len(parts)

    def body(*refs):
        srcs, outs, (send_sems, recv_sems) = refs[:n], refs[n:2 * n], refs[2 * n:]
        x, y, c = _mesh_pos()
        sm = 2 * x + y
        chips = _other_chips(x, y)
        sib = (x, y, 1 - c)
        rc = lambda k, src, dst, dev: _remote(src, dst, send_sems, recv_sems, k, dev)
        first = [rc(7 * i + j, srcs[i].at[c], outs[i].at[sm, c], (cx, cy, c)) for i in range(n) for j, (cx, cy) in enumerate(chips)]
        own = [rc(7 * i + 6, srcs[i], outs[i].at[sm], sib) for i in range(n)]
        for cp in first + own:
            cp.start()
        passed = []
        for j, (cx, cy) in enumerate(chips):
            for i in range(n):
                land = outs[i].at[2 * cx + cy, c]
                rc(7 * i + j, srcs[i].at[c], land, (cx, cy, c)).wait_recv()
                cp = rc(7 * i + 3 + j, land, land, sib)
                cp.start()
                passed.append(cp)
        for j, (cx, cy) in enumerate(chips):
            for i in range(n):
                rc(7 * i + 3 + j, srcs[i].at[c], outs[i].at[2 * cx + cy, 1 - c], sib).wait_recv()
        for cp in own:
            cp.wait_recv()
        for cp in first + passed + own:
            cp.wait_send()

    return pl.pallas_call(
        body, name="weights_all_gather_" + tag,
        in_specs=[ANY] * n, out_specs=[ANY] * n,
        out_shape=[jax.ShapeDtypeStruct((4,) + p.shape, p.dtype) for p in parts],
        scratch_shapes=[pltpu.SemaphoreType.DMA((7 * n,)), pltpu.SemaphoreType.DMA((7 * n,))],
    )(*parts)


def _direct_gather_copies(srcs, lands, send_sems, recv_sems):
    x, y, c = _mesh_pos()
    sm = 2 * x + y
    sends, recvs = [], []
    for i, (src, land) in enumerate(zip(srcs, lands)):
        for j, (cx, cy) in enumerate(_other_chips(x, y)):
            for t in range(2):
                sends.append(_remote(src.at[c], land.at[sm, c], send_sems, recv_sems, 13 * i + 4 * j + 2 * c + t, (cx, cy, t)))
                recvs.append(_remote(src.at[t], land.at[2 * cx + cy, t], send_sems, recv_sems, 13 * i + 4 * j + 2 * t + c, (cx, cy, t)))
        sends.append(_remote(src, land.at[sm], send_sems, recv_sems, 13 * i + 12, (x, y, 1 - c)))
        recvs.append(_remote(src, land.at[sm], send_sems, recv_sems, 13 * i + 12, (x, y, 1 - c)))
    return sends, recvs


def _sibling_copies(srcs, lands, send_sems, recv_sems):
    x, y, c = _mesh_pos()
    cps = [_remote(src.at[s, 1 - c], land.at[s], send_sems, recv_sems, 4 * i + s, (x, y, 1 - c))
           for i, (src, land) in enumerate(zip(srcs, lands)) for s in range(4)]
    return cps, cps


def _chips_copies(srcs, lands, send_sems, recv_sems):
    x, y, c = _mesh_pos()
    cps = [_remote(src.at[2 * cx + cy], land.at[j], send_sems, recv_sems, 3 * i + j, (cx, cy, c))
           for i, (src, land) in enumerate(zip(srcs, lands)) for j, (cx, cy) in enumerate(_other_chips(x, y))]
    return cps, cps


def _share_copies(srcs, lands, send_sems, recv_sems):
    x, y, c = _mesh_pos()
    cps = [_remote(src, land, send_sems, recv_sems, i, (x, y, 1 - c)) for i, (src, land) in enumerate(zip(srcs, lands))]
    return cps, cps


def _exchange_call(name, copies, srcs, land_shapes, n_sems):
    n = len(srcs)

    def body(*refs):
        sends, recvs = copies(refs[:n], refs[n:2 * n], refs[2 * n], refs[2 * n + 1])
        for cp in sends:
            cp.start()
        for cp in sends:
            cp.wait_send()
        for cp in recvs:
            cp.wait_recv()

    return pl.pallas_call(
        body, name=name, in_specs=[ANY] * n, out_specs=[ANY] * n, out_shape=list(land_shapes),
        scratch_shapes=[pltpu.SemaphoreType.DMA((n_sems,)), pltpu.SemaphoreType.DMA((n_sems,))],
    )(*srcs)


def _exchange_start_call(name, copies, srcs, land_shapes, n_sems, order=None):
    n = len(srcs)
    extra = [] if order is None else [order]
    k = 2 * n + len(extra)

    def body(*refs):
        sends, _ = copies(refs[:n], refs[n:2 * n], refs[k], refs[k + 1])
        for cp in sends:
            cp.start()
        refs[-1][...] = jnp.zeros_like(refs[-1])

    hbm = lambda a: pltpu.with_memory_space_constraint(a, pltpu.HBM)
    lands = [hbm(lax.empty(sd.shape, sd.dtype)) for sd in land_shapes]
    sem = pltpu.SemaphoreType.DMA((n_sems,))
    out = pl.pallas_call(
        body, name=name,
        out_shape=(sem, sem, *[pltpu.HBM(a.shape, a.dtype) for a in list(srcs) + lands], jax.ShapeDtypeStruct((8, LANES), F32)),
        in_specs=[HBM_SPEC] * (2 * n) + [ANY] * len(extra), out_specs=(SEM_SPEC, SEM_SPEC, *[HBM_SPEC] * (2 * n), VMEM_SPEC),
        input_output_aliases={i: 2 + i for i in range(2 * n)},
        compiler_params=pltpu.CompilerParams(has_side_effects=pltpu.SideEffectType.DATAFLOW_SIDE_EFFECTING),
    )(*[hbm(a) for a in srcs], *lands, *extra)
    return out[0], out[1], out[2:2 + n], out[2 + n:2 + 2 * n], out[-1]


def _exchange_wait_call(name, copies, started, after):
    send_sems, recv_sems, srcs, lands, _ = started
    n = len(srcs)

    def body(*refs):
        sends, recvs = copies(refs[:n], refs[n:2 * n], refs[2 * n], refs[2 * n + 1])
        for cp in sends:
            cp.wait_send()
        for cp in recvs:
            cp.wait_recv()

    out = pl.pallas_call(
        body, name=name,
        out_shape=tuple(pltpu.HBM(a.shape, a.dtype) for a in list(srcs) + list(lands)),
        in_specs=[HBM_SPEC] * (2 * n) + [SEM_SPEC, SEM_SPEC, ANY], out_specs=tuple([HBM_SPEC] * (2 * n)),
        input_output_aliases={i: i for i in range(2 * n)},
        compiler_params=pltpu.CompilerParams(has_side_effects=pltpu.SideEffectType.DATAFLOW_SIDE_EFFECTING),
    )(*srcs, *lands, send_sems, recv_sems, after)
    return out[:n], out[n:]


def _rows_tile(rows, width, itemsize=4):
    limit = max(16, (3 << 20) // (width * itemsize))
    if rows <= limit:
        return rows
    return max(t for t in range(16, limit + 1, 16) if rows % t == 0)


def _sum_sibling_call(g, buf, c, name):
    _, _, rh, w = g.shape
    tile = _rows_tile(rh, w)

    def body(c_ref, g_ref, b_ref, p_ref, pb_ref):
        p = g_ref[...] + b_ref[...]
        p_ref[...] = p
        pb_ref[...] = p.astype(BF16)

    blk = pl.BlockSpec((None, tile, w), lambda s, i, c_ref: (s, i, 0))
    return pl.pallas_call(
        body, name=name,
        grid_spec=pltpu.PrefetchScalarGridSpec(
            num_scalar_prefetch=1, grid=(4, rh // tile),
            in_specs=[pl.BlockSpec((None, None, tile, w), lambda s, i, c_ref: (s, c_ref[0], i, 0)), blk],
            out_specs=[blk, blk]),
        out_shape=[jax.ShapeDtypeStruct((4, rh, w), F32), jax.ShapeDtypeStruct((4, rh, w), BF16)],
        compiler_params=_cp("parallel", "parallel"),
    )(c, g, buf)


def _sum_chips_call(p, buf, sm, name):
    _, rh, w = p.shape
    tile = _rows_tile(rh, w)

    def body(sm_ref, p_ref, b_ref, f_ref):
        f_ref[...] = ((p_ref[...] + b_ref[0].astype(F32)) + b_ref[1].astype(F32)) + b_ref[2].astype(F32)

    return pl.pallas_call(
        body, name=name,
        grid_spec=pltpu.PrefetchScalarGridSpec(
            num_scalar_prefetch=1, grid=(rh // tile,),
            in_specs=[pl.BlockSpec((None, tile, w), lambda i, sm_ref: (sm_ref[0], i, 0)),
                      pl.BlockSpec((3, tile, w), lambda i, sm_ref: (0, i, 0))],
            out_specs=pl.BlockSpec((tile, w), lambda i, sm_ref: (i, 0))),
        out_shape=jax.ShapeDtypeStruct((rh, w), F32),
        compiler_params=_cp("parallel"),
    )(sm, p, buf)


def _adamw_halves_call(w, g_mine, g_sib, c, m, v, name, transposed=False, order=None):
    extra = [] if order is None else [order]
    if transposed:
        rows, r = w.shape
        rh = r // 2
        tile = _rows_tile(rows, rh)
        whole = pl.BlockSpec((tile, rh), lambda h, i, c_ref: (i, h))
        half = pl.BlockSpec((tile, rh), lambda h, i, c_ref: (i, 0))
        nt = rows // tile
    else:
        r, wd = w.shape
        rh = r // 2
        tile = _rows_tile(rh, wd)
        nt = rh // tile
        whole = pl.BlockSpec((tile, wd), lambda h, i, c_ref: (h * nt + i, 0))
        half = pl.BlockSpec((tile, wd), lambda h, i, c_ref: (i, 0))

    def body(c_ref, w_ref, gm_ref, gs_ref, m_ref, v_ref, *rest):
        g_ref, d_ref, nm_ref, nv_ref = rest[len(extra):]
        gv = jnp.where(pl.program_id(0) == c_ref[0], gm_ref[...], gs_ref[...])
        g_ref[...] = gv
        nm = ADAM_B1 * m_ref[...] + (1.0 - ADAM_B1) * gv
        nv = ADAM_B2 * v_ref[...] + (1.0 - ADAM_B2) * jnp.square(gv)
        m_hat = nm / (1.0 - ADAM_B1 ** ADAM_STEP)
        v_hat = nv / (1.0 - ADAM_B2 ** ADAM_STEP)
        d_ref[...] = -ADAM_LR * (m_hat / (jnp.sqrt(v_hat) + ADAM_EPS) + ADAM_WD * w_ref[...])
        nm_ref[...] = nm
        nv_ref[...] = nv

    sd = jax.ShapeDtypeStruct(w.shape, F32)
    return pl.pallas_call(
        body, name=name,
        grid_spec=pltpu.PrefetchScalarGridSpec(
            num_scalar_prefetch=1, grid=(2, nt),
            in_specs=[whole, half, half, whole, whole] + [ANY] * len(extra), out_specs=[whole] * 4),
        out_shape=[sd, sd, sd, sd],
        compiler_params=_cp("parallel", "parallel"),
    )(c, w, g_mine, g_sib, m, v, *extra)


def _all_reduce8_call(vec, name):
    rows = vec.shape[0]

    def body(v_ref, out_ref, slots, send_sems, recv_sems):
        x, y, c = _mesh_pos()
        me = 4 * x + 2 * y + c
        slots[me] = v_ref[...]

        def rcopy(k, to_me):
            bx, by, bc = (k >> 2) & 1, (k >> 1) & 1, k & 1
            px, py, pc = (1 - x if bx else x), (1 - y if by else y), (1 - c if bc else c)
            slot = 4 * px + 2 * py + pc if to_me else me
            return pltpu.make_async_remote_copy(src_ref=v_ref, dst_ref=slots.at[slot], send_sem=send_sems.at[k - 1],
                                                recv_sem=recv_sems.at[k - 1], device_id=(px, py, pc), device_id_type=MESH_ID)

        for k in range(1, N_DEV):
            rcopy(k, False).start()
        for k in range(1, N_DEV):
            rcopy(k, True).wait_recv()
        for k in range(1, N_DEV):
            rcopy(k, False).wait_send()
        tot = slots[0]
        for d in range(1, N_DEV):
            tot = tot + slots[d]
        out_ref[...] = tot

    return pl.pallas_call(
        body, name=name,
        in_specs=[VMEM_SPEC], out_specs=VMEM_SPEC,
        out_shape=jax.ShapeDtypeStruct((rows, LANES), F32),
        scratch_shapes=[pltpu.VMEM((N_DEV, rows, LANES), F32),
                        pltpu.SemaphoreType.DMA((N_DEV - 1,)), pltpu.SemaphoreType.DMA((N_DEV - 1,))],
    )(vec)


def _adamw_call(w, g, m, v, name):
    r, c = w.shape
    rb = r if r <= 256 else (256 if r % 256 == 0 else 352)
    assert r % rb == 0

    def body(w_ref, g_ref, m_ref, v_ref, d_ref, nm_ref, nv_ref):
        gv = g_ref[...]
        nm = ADAM_B1 * m_ref[...] + (1.0 - ADAM_B1) * gv
        nv = ADAM_B2 * v_ref[...] + (1.0 - ADAM_B2) * jnp.square(gv)
        m_hat = nm / (1.0 - ADAM_B1 ** ADAM_STEP)
        v_hat = nv / (1.0 - ADAM_B2 ** ADAM_STEP)
        d_ref[...] = -ADAM_LR * (m_hat / (jnp.sqrt(v_hat) + ADAM_EPS) + ADAM_WD * w_ref[...])
        nm_ref[...] = nm
        nv_ref[...] = nv

    spec = pl.BlockSpec((rb, c), lambda i: (i, 0))
    sd = jax.ShapeDtypeStruct((r, c), F32)
    return pl.pallas_call(
        body, name=name, grid=(r // rb,),
        in_specs=[spec] * 4, out_specs=[spec] * 3, out_shape=[sd, sd, sd],
        compiler_params=_cp("parallel"),
    )(w, g, m, v)


SMALL = (("attn_norm_w", D_MODEL), ("ret_gn_w", RET_W), ("mla_q_norm_w", Q_RANK), ("mla_kv_norm_w", KV_RANK),
         ("ffn_norm_w", D_MODEL), ("conv_b", F2), ("final_norm_w", D_MODEL))
WEIGHT_ORDER = ("attn_norm_w", "w_in", "ret_gn_w", "mla_q_norm_w", "w_uq", "mla_kv_norm_w", "w_ukv", "w_out",
                "ffn_norm_w", "w_up", "conv_w", "conv_b", "w_down", "final_norm_w")


def _pad_rows(flat, rows):
    return jnp.concatenate([flat, jnp.zeros((rows * LANES - flat.shape[0],), flat.dtype)]).reshape(rows, LANES)


def kernel(x, positions, attn_norm_w, w_in, ret_gn_w, mla_q_norm_w, w_uq, mla_kv_norm_w, w_ukv, w_out, ffn_norm_w, w_up, conv_w, conv_b, w_down, final_norm_w, loss_target, m_attn_norm_w, m_w_in, m_ret_gn_w, m_mla_q_norm_w, m_w_uq, m_mla_kv_norm_w, m_w_ukv, m_w_out, m_ffn_norm_w, m_w_up, m_conv_w, m_conv_b, m_w_down, m_final_norm_w, v_attn_norm_w, v_w_in, v_ret_gn_w, v_mla_q_norm_w, v_w_uq, v_mla_kv_norm_w, v_w_ukv, v_w_out, v_ffn_norm_w, v_w_up, v_conv_w, v_conv_b, v_w_down, v_final_norm_w):
    args = dict(locals())
    cx, cy, cc = _mesh_pos()
    sm = 2 * cx + cy

    c_arr, sm_arr = cc.reshape(1).astype(jnp.int32), sm.reshape(1).astype(jnp.int32)
    sds = jax.ShapeDtypeStruct

    def my_shards(group):
        return [args[n][0].astype(BF16).reshape(2, r // 2, c) for n, (r, c), _ in group]

    def full_weights(gathered, group):
        full = {}
        for (n, (r, c), axis), got in zip(group, gathered):
            piece = got.reshape(4, r, c)
            full[n] = piece if n in ("w_up", "w_in") else (piece.transpose(1, 0, 2).reshape(r, 4 * c) if axis == 1 else piece.reshape(4 * r, c))
        return full

    def by_owner(gw, group):
        out = []
        for n, (r, c), axis in group:
            g = gw[n]
            if axis == 1 and g.ndim == 2:
                g = g.reshape(r, 4, c).transpose(1, 0, 2)
            out.append(g.reshape(4, 2, r // 2, c))
        return out

    def sibling_shapes(gs):
        return [sds((4,) + g.shape[2:], F32) for g in gs]

    def chip_sums(gs, bufs, group):
        res = [_sum_sibling_call(g, b, c_arr, "grads_sum_sibling_" + n) for g, b, (n, _, _) in zip(gs, bufs, group)]
        return [p for p, _ in res], [pb for _, pb in res]

    def chips_shapes(pbs):
        return [sds((3,) + pb.shape[1:], BF16) for pb in pbs]

    def totals(ps, lands, group, tag):
        fins = [_sum_chips_call(p, l, sm_arr, "grads_sum_chips_" + n) for p, l, (n, _, _) in zip(ps, lands, group)]
        sibs = _exchange_call("grads_rs_share_" + tag, _share_copies, fins, [sds(f.shape, F32) for f in fins], len(fins))
        return {n: (f, s) for (n, _, _), f, s in zip(group, fins, sibs)}

    class StepExchanges(_Exchanges):
        def __init__(self, order):
            shards = my_shards(GROUP_B)
            self.gather = _exchange_start_call("weights_gather_start_b", _direct_gather_copies, shards,
                                               [sds((4,) + s.shape, BF16) for s in shards], 13 * len(shards), order)
            self.red = None

        def token(self):
            return self.gather[4][0:1, 0:1]

        def mlp_weights(self, after):
            return full_weights(_exchange_wait_call("weights_gather_wait_b", _direct_gather_copies, self.gather, after)[1], GROUP_B)

        def mlp_grads(self, gw):
            gs = by_owner(gw, GROUP_B)
            self.step1 = _exchange_start_call("grads_rs_sibling_start_b", _sibling_copies, gs, sibling_shapes(gs), 4 * len(gs))
            return self.step1[4]

        def behind_out_bwd(self, after):
            gs, bufs = _exchange_wait_call("grads_rs_sibling_wait_b", _sibling_copies, self.step1, after)
            self.ps, pbs = chip_sums(gs, bufs, GROUP_B)
            self.step2 = _exchange_start_call("grads_rs_chips_start_b", _chips_copies, pbs, chips_shapes(pbs), 3 * len(pbs))
            return self.step2[4]

        def behind_attention(self, after):
            _, lands = _exchange_wait_call("grads_rs_chips_wait_b", _chips_copies, self.step2, after)
            self.red = totals(self.ps, lands, GROUP_B, "b")

    gathered = _gather_list_call(my_shards(GROUP_A) + [conv_w[0].reshape(2, 1, 3 * F2 // 8)], "a")
    full = full_weights(gathered[:-1], GROUP_A)
    ex = StepExchanges(gathered[-1])
    full["conv_w"] = gathered[-1].reshape(4, 3, F2 // 4).transpose(1, 0, 2).reshape(3, F2)
    small = {n: args[n].reshape(1, d) for n, d in SMALL}
    small["attn_norm_w"] = small["attn_norm_w"] + ex.token()

    loss, gx, gw, gs = _local_step(x[0], positions[0], loss_target[0], full, small, ex)

    ga = by_owner(gw, GROUP_A)
    step1 = _exchange_start_call("grads_rs_sibling_start_a", _sibling_copies, ga, sibling_shapes(ga), 4 * len(ga))
    vec = jnp.concatenate([gs[n].reshape(-1) for n, _ in SMALL] + [gw["conv_w"].reshape(-1), loss.reshape(-1)])
    tot = _all_reduce8_call(_pad_rows(vec, 216) + step1[4][0:1], "small_all_reduce")
    ga, bufs = _exchange_wait_call("grads_rs_sibling_wait_a", _sibling_copies, step1, tot)
    tot = tot.reshape(-1)
    ps, pbs = chip_sums(ga, bufs, GROUP_A)
    step2 = _exchange_start_call("grads_rs_chips_start_a", _chips_copies, pbs, chips_shapes(pbs), 3 * len(pbs))
    early, last = {}, step2[4]
    for n, _, _ in GROUP_B:
        wmv = [args[k + n][0] for k in ("", "m_", "v_")]
        early[n] = _adamw_halves_call(wmv[0], *ex.red[n], c_arr, wmv[1], wmv[2], "adamw_" + n, order=last)
        last = early[n][1]
    _, lands = _exchange_wait_call("grads_rs_chips_wait_a", _chips_copies, step2, last)
    halves = totals(ps, lands, GROUP_A, "a")

    red, off = {}, 0
    for n, d in SMALL:
        red[n] = tot[off:off + d].reshape(1, d)
        off += d
    red["conv_w"] = lax.dynamic_slice(tot[off:off + 3 * F2].reshape(3, F2), (0, sm * (F2 // 4)), (3, F2 // 4))
    loss_tot = tot[off + 3 * F2]

    grads, deltas, new_m, new_v = [], [], [], []
    for n in WEIGHT_ORDER:
        shape = args[n].shape
        two_d = (1, shape[0]) if len(shape) == 1 else shape[-2:]
        wmv = [args[k + n].reshape(two_d) for k in ("", "m_", "v_")]
        if n in early:
            g, d, nm, nv = early[n]
        elif n in halves and two_d[1] % LANES:
            tr = lambda a: a.T
            g, d, nm, nv = map(tr, _adamw_halves_call(tr(wmv[0]), *map(tr, halves[n]), c_arr, tr(wmv[1]), tr(wmv[2]),
                                                      "adamw_" + n, transposed=True))
        elif n in halves:
            g, d, nm, nv = _adamw_halves_call(wmv[0], *halves[n], c_arr, wmv[1], wmv[2], "adamw_" + n)
        else:
            g = red[n].reshape(two_d)
            d, nm, nv = _adamw_call(wmv[0], g, wmv[1], wmv[2], "adamw_" + n)
        grads.append(g.reshape(shape))
        deltas.append(d.reshape(shape))
        new_m.append(nm.reshape(shape))
        new_v.append(nv.reshape(shape))
    return (loss_tot, gx[None], *grads, *deltas, *new_m, *new_v)
```

```python
import math

import numpy as np
import jax
import jax.numpy as jnp
from jax import lax
from jax.experimental import pallas as pl
from jax.experimental.pallas import tpu as pltpu

F32 = jnp.float32
BF16 = jnp.bfloat16

D_MODEL = 1024
N_HEADS = 8
HEAD = 64
RET_W = N_HEADS * HEAD
MLA_W = N_HEADS * HEAD
ROPE = 32
Q_RANK = 256
KV_RANK = 128
D_FF = 2816
F2 = 2 * D_FF
IN_W = 4 * RET_W + Q_RANK + KV_RANK + ROPE
IN_EXT = 4 * RET_W + Q_RANK + KV_RANK + 128
KPE_LO = 64
ROPE_BASE = 10000.0
EPS = 1e-6
RET_CHUNK = 256
SM_SCALE = (HEAD + ROPE) ** -0.5
LOG2E = math.log2(math.e)
LN2 = math.log(2.0)
NEG = -1e30
LANES = 128
VMEM_LIMIT = 56 * 1024 * 1024

ADAM_LR = 0.001
ADAM_B1 = 0.9
ADAM_B2 = 0.999
ADAM_EPS = 1e-08
ADAM_WD = 0.01
ADAM_STEP = 10


VMEM_LIMIT_MLP = 60 * 1024 * 1024


def _cp(*sem, vmem=VMEM_LIMIT):
    return pltpu.CompilerParams(dimension_semantics=sem, vmem_limit_bytes=vmem)


def _full(shape):
    n = len(shape)
    return pl.BlockSpec(tuple(shape), lambda *_: (0,) * n)


def _row(ts, c):
    return pl.BlockSpec((ts, c), lambda i: (i, 0))


def _hrow(h, ts, c):
    return pl.BlockSpec((h, ts, c), lambda i: (0, i, 0))


def _dot(a, b):
    return jnp.dot(a, b, preferred_element_type=F32)


def _dot_nt(a, b):
    return lax.dot_general(a, b, (((1,), (1,)), ((), ())), preferred_element_type=F32)


def _dot_tn(a, b):
    return lax.dot_general(a, b, (((0,), (0,)), ((), ())), preferred_element_type=F32)


def _dot_hi(a, b):
    hi = a.astype(BF16)
    lo = (a - hi.astype(F32)).astype(BF16)
    bb = b.astype(BF16)
    return _dot(hi, bb) + _dot(lo, bb)


def _rot_half(x, half):
    w = x.shape[-1]
    lane = lax.broadcasted_iota(jnp.int32, x.shape, x.ndim - 1)
    first = (lane % (2 * half)) < half
    return jnp.where(first, -pltpu.roll(x, w - half, x.ndim - 1), pltpu.roll(x, half, x.ndim - 1))


def _rope(x, cos, sin, half):
    return x * cos + _rot_half(x, half) * sin


def _unrope(dy, cos, sin, half):
    return dy * cos - _rot_half(dy, half) * sin


def _sigmoid(g):
    return 0.5 * jnp.tanh(0.5 * g) + 0.5


def _silu(g):
    return g * _sigmoid(g)


def _rstd(x):
    return lax.rsqrt(jnp.mean(x * x, axis=-1, keepdims=True) + EPS)


def _rope_tables(positions):
    s = positions.shape[0]
    hr, hm = HEAD // 2, ROPE // 2
    pos = positions.astype(F32)[None, :]
    inv_r = ROPE_BASE ** (-jnp.arange(0, HEAD, 2, dtype=F32) / HEAD)
    inv_m = ROPE_BASE ** (-jnp.arange(0, ROPE, 2, dtype=F32) / ROPE)
    ang = jnp.concatenate([inv_r, inv_m])[:, None] * pos
    packed = jnp.concatenate([jnp.cos(ang), jnp.sin(ang), jnp.zeros((LANES - 2 * (hr + hm), s), F32)], 0)
    tx = min(s, 1024)

    def spread(t, lane, pieces, fill):
        out = jnp.full(t.shape, fill, F32)
        for lo, src, width in pieces:
            moved = t if lo == src else pltpu.roll(t, (lo - src) % LANES, 1)
            out = jnp.where((lane >= lo) & (lane < lo + width), moved, out)
        return out

    def body(p_ref, cr_ref, sr_ref, cm_ref, sm_ref):
        t = p_ref[...].T
        lane = lax.broadcasted_iota(jnp.int32, t.shape, 1)
        cr_ref[...] = spread(t, lane, [(j * hr, 0, hr) for j in range(LANES // hr)], 0.0)
        sr_ref[...] = spread(t, lane, [(j * hr, hr + hm, hr) for j in range(LANES // hr)], 0.0)
        cm_ref[...] = spread(t, lane, [(KPE_LO, hr, hm), (KPE_LO + hm, hr, hm)], 1.0)
        sm_ref[...] = spread(t, lane, [(KPE_LO, 2 * hr + hm, hm), (KPE_LO + hm, 2 * hr + hm, hm)], 0.0)

    tab = jax.ShapeDtypeStruct((s, LANES), F32)
    return pl.pallas_call(
        body, name="rope_tables", grid=(s // tx,),
        in_specs=[pl.BlockSpec((LANES, tx), lambda i: (0, i))],
        out_specs=[_row(tx, LANES)] * 4, out_shape=[tab] * 4,
        compiler_params=_cp("parallel"),
    )(packed)


def _ret_consts():
    c = RET_CHUNK
    lg = np.log1p(-np.power(2.0, -5.0 - np.arange(N_HEADS, dtype=np.float64)))
    idx = np.arange(c, dtype=np.float64)
    diff = idx[:, None] - idx[None, :]
    lane_head = np.arange(LANES) // HEAD
    dmask = np.zeros((4, 2, c, c))
    zeta = np.zeros((4, c, LANES))
    xi = np.zeros((4, c, LANES))
    cd = np.zeros((4, LANES, LANES))
    bd = (lane_head[:, None] == lane_head[None, :]).astype(np.float64)
    for j in range(4):
        for hh in range(2):
            dmask[j, hh] = np.where(diff >= 0, np.exp(lg[2 * j + hh] * np.maximum(diff, 0.0)), 0.0)
        lgl = lg[2 * j + lane_head]
        zeta[j] = np.exp(lgl[None, :] * (c - 1.0 - idx[:, None]))
        xi[j] = np.exp(lgl[None, :] * (idx[:, None] + 1.0))
        cd[j] = np.exp(lgl * c)[:, None] * bd
    f = lambda a: jnp.asarray(a, F32)
    side = lambda d: np.concatenate([d[:, 0], d[:, 1]], axis=-1)
    return dict(dmask=f(side(dmask)), dmask_t=f(side(np.swapaxes(dmask, 2, 3))), zeta=f(zeta), xi=f(xi), cd=f(cd), bd=f(bd))


def _f1_call(x, anw, win, qnw, kvnw, wq, wk, wv, cos_r, sin_r, cos_m, sin_m, ts):
    s = x.shape[0]

    def body(x_ref, anw_ref, w_ref, qnw_ref, kvnw_ref, wq_ref, wk_ref, wv_ref, cr_ref, sr_ref, cm_ref, sm_ref,
             q_ref, k_ref, v_ref, g_ref, cq_ref, ckv_ref, mq_ref, mk_ref, mv_ref, r_ref):
        xv = x_ref[...]
        r = _rstd(xv)
        r_ref[...] = r
        h = (xv * r * anw_ref[...]).astype(BF16)
        cr, sr = cr_ref[...], sr_ref[...]
        qk = _dot(h, w_ref[:, 0:2 * RET_W])
        for j in range(4):
            sl = slice(j * LANES, (j + 1) * LANES)
            q_ref[:, sl] = _rope(qk[:, sl], cr, sr, HEAD // 2).astype(BF16)
            kk = qk[:, RET_W + j * LANES:RET_W + (j + 1) * LANES]
            k_ref[:, sl] = (_rope(kk, cr, sr, HEAD // 2) * (HEAD ** -0.5)).astype(BF16)
        v_ref[...] = _dot(h, w_ref[:, 2 * RET_W:3 * RET_W]).astype(BF16)
        g_ref[...] = _dot(h, w_ref[:, 3 * RET_W:4 * RET_W])
        o = 4 * RET_W
        cqv = _dot(h, w_ref[:, o:o + Q_RANK])
        ckvv = _dot(h, w_ref[:, o + Q_RANK:o + Q_RANK + KV_RANK])
        cq_ref[...] = cqv
        ckv_ref[...] = ckvv
        cm, sm = cm_ref[...], sm_ref[...]
        kp = _rope(_dot(h, w_ref[:, o + Q_RANK + KV_RANK:IN_EXT]), cm, sm, ROPE // 2)
        kp = _lane_pair((ts, LANES), QK_AUX, -1.0, -1.0, kp)
        cqn = (cqv * _rstd(cqv) * qnw_ref[...]).astype(BF16)
        ckvn = (ckvv * _rstd(ckvv) * kvnw_ref[...]).astype(BF16)
        for hd in range(N_HEADS):
            qh = _rope(_dot(cqn, wq_ref[hd]), cm, sm, ROPE // 2)
            mq_ref[hd] = (qh * (SM_SCALE * LOG2E)).astype(BF16)
            mk_ref[hd] = (_dot(ckvn, wk_ref[hd]) + kp).astype(BF16)
            mv_ref[hd] = _lane_pair((ts, LANES), V_AUX, 1.0, 1.0, _dot(ckvn, wv_ref[hd])).astype(BF16)

    sd = jax.ShapeDtypeStruct
    hm = sd((N_HEADS, s, LANES), BF16)
    return pl.pallas_call(
        body, name="f1_in_proj", grid=(s // ts,),
        in_specs=[_row(ts, D_MODEL), _full((1, D_MODEL)), _full((D_MODEL, IN_EXT)), _full((1, Q_RANK)), _full((1, KV_RANK)),
                  _full((N_HEADS, Q_RANK, LANES)), _full((N_HEADS, KV_RANK, LANES)), _full((N_HEADS, KV_RANK, LANES)),
                  _row(ts, LANES), _row(ts, LANES), _row(ts, LANES), _row(ts, LANES)],
        out_specs=[_row(ts, RET_W), _row(ts, RET_W), _row(ts, RET_W), _row(ts, RET_W),
                   _row(ts, Q_RANK), _row(ts, KV_RANK)] + [_hrow(N_HEADS, ts, LANES)] * 3 + [_row(ts, 1)],
        out_shape=[sd((s, RET_W), BF16), sd((s, RET_W), BF16), sd((s, RET_W), BF16), sd((s, RET_W), F32),
                   sd((s, Q_RANK), F32), sd((s, KV_RANK), F32), hm, hm, hm, sd((s, 1), F32)],
        compiler_params=_cp("parallel"),
    )(x, anw, win, qnw, kvnw, wq, wk, wv, cos_r, sin_r, cos_m, sin_m)


def _stack_heads(a):
    lo = lax.broadcasted_iota(jnp.int32, a.shape, 1) < HEAD
    zero = jnp.zeros_like(a)
    return jnp.concatenate([jnp.where(lo, a, zero), jnp.where(lo, zero, a)], axis=0)


def _pair_product(a, b2, decay2, w2):
    return _dot((_dot_nt(a, b2) * decay2).astype(BF16), w2)


RET_SLABS = 2


def _ret_specs(tr, tile_of):
    c, ns = RET_CHUNK, RET_SLABS
    return dict(
        slab=pl.BlockSpec((tr, ns * LANES), lambda j, i: (tile_of(i), j)),
        tab=pl.BlockSpec((tr, LANES), lambda j, i: (tile_of(i), 0)),
        vec=pl.BlockSpec((1, ns * LANES), lambda j, i: (0, j)),
        dmask=pl.BlockSpec((ns, c, 2 * c), lambda j, i: (j, 0, 0)),
        rows=pl.BlockSpec((ns, c, LANES), lambda j, i: (j, 0, 0)),
        state=pl.BlockSpec((ns, LANES, LANES), lambda j, i: (j, 0, 0)),
        bd=pl.BlockSpec((LANES, LANES), lambda j, i: (0, 0)))


def _ret_states(a_ref, b_ref, scale_ref, cd_ref, bd, st_ref, chunks, lanes, reverse):
    nc = len(chunks)
    contrib = [[_dot_tn((a_ref[rows, ln].astype(F32) * scale_ref[sl]).astype(BF16), b_ref[rows, ln]) * bd for rows in chunks]
               for sl, ln in enumerate(lanes)]
    states = []
    for sl in range(len(lanes)):
        st, seen = st_ref[sl], [None] * nc
        for ci in (reversed(range(nc)) if reverse else range(nc)):
            seen[ci] = st.astype(BF16)
            st = st * cd_ref[sl] + contrib[sl][ci]
        st_ref[sl] = st
        states.append(seen)
    return states


def _ret_fwd_call(q, k, v, g, gnw, rc, tr, order=None):
    s = q.shape[0]
    c = RET_CHUNK
    nc = tr // c
    ns = RET_SLABS

    def body(q_ref, k_ref, v_ref, g_ref, gnw_ref, dm_ref, zeta_ref, xi_ref, cd_ref, bd_ref, o_ref, y_ref, st_ref):
        @pl.when(pl.program_id(1) == 0)
        def _():
            st_ref[...] = jnp.zeros_like(st_ref)

        bd = bd_ref[...]
        chunks = [slice(ci * c, (ci + 1) * c) for ci in range(nc)]
        lanes = [slice(sl * LANES, (sl + 1) * LANES) for sl in range(ns)]
        states = _ret_states(k_ref, v_ref, zeta_ref, cd_ref, bd, st_ref, chunks, lanes, False)
        for ci, rows in enumerate(chunks):
            for sl, ln in enumerate(lanes):
                qc = q_ref[rows, ln]
                o_ref[rows, ln] = (_dot(qc, states[sl][ci]) * xi_ref[sl]
                                   + _pair_product(qc, _stack_heads(k_ref[rows, ln]), dm_ref[sl], _stack_heads(v_ref[rows, ln])))
        avg = bd * (1.0 / HEAD)
        for ln in lanes:
            o = o_ref[:, ln]
            ctr = o - _dot_hi(o, avg)
            var = _dot_hi(ctr * ctr, avg)
            y_ref[:, ln] = (_silu(g_ref[:, ln]) * (ctr * lax.rsqrt(var + EPS) * gnw_ref[:, ln])).astype(BF16)

    specs = _ret_specs(tr, lambda i: i)
    sd = jax.ShapeDtypeStruct
    body, first_specs, first = _ordered_after(body, order)
    return pl.pallas_call(
        body, name="ret_fwd", grid=(4 // ns, s // tr),
        in_specs=first_specs + [specs["slab"]] * 4
        + [specs["vec"], specs["dmask"], specs["rows"], specs["rows"], specs["state"], specs["bd"]],
        out_specs=[specs["slab"]] * 2,
        out_shape=[sd((s, RET_W), F32), sd((s, RET_W), BF16)],
        scratch_shapes=[pltpu.VMEM((ns, LANES, LANES), F32)],
        compiler_params=_cp("parallel", "arbitrary"),
    )(*first, q, k, v, g, gnw, rc["dmask"], rc["zeta"], rc["xi"], rc["cd"], rc["bd"])


QK_AUX = HEAD + ROPE
V_AUX = HEAD


def _lane_pair(shape, lo, a, b, rest):
    lane = lax.broadcasted_iota(jnp.int32, shape, len(shape) - 1)
    return jnp.where(lane == lo, a, jnp.where(lane == lo + 1, b, rest))


def _hi_lo(v):
    hi = v.astype(BF16).astype(F32)
    return hi, v - hi


def _flash_fwd_call(q, k, v, tb):
    s = q.shape[1]
    nb = s // tb
    pairs = [(a, b) for a in range(nb) for b in range(a + 1)]
    qi_of, ki_of = (jnp.asarray(np.array(col, np.int32)) for col in zip(*pairs))

    def body(qi_ref, ki_ref, q_ref, k_ref, v_ref, o_ref, qb_ref, m_ref, acc_ref):
        qi, ki = qi_ref[pl.program_id(0)], ki_ref[pl.program_id(0)]

        @pl.when(ki == 0)
        def _():
            m_ref[...] = jnp.full_like(m_ref, NEG)
            acc_ref[...] = jnp.zeros_like(acc_ref)

        def step(masked):
            if masked:
                keep = lax.broadcasted_iota(jnp.int32, (tb, tb), 1) <= lax.broadcasted_iota(jnp.int32, (tb, tb), 0)
            def finish(h, pe, alpha):
                acc_ref[h] = acc_ref[h] * alpha + _dot(pe, v_ref[h])

            nxt, pending = _dot_nt(q_ref[0], k_ref[0]), None
            for h in range(N_HEADS):
                sc = nxt
                if h + 1 < N_HEADS:
                    nxt = _dot_nt(q_ref[h + 1], k_ref[h + 1])
                if masked:
                    sc = jnp.where(keep, sc, NEG)
                m_prev = m_ref[h]
                m_new = jnp.maximum(m_prev, jnp.max(sc, axis=1, keepdims=True))
                pe = jnp.exp2(sc - jnp.tile(m_new, (1, tb // LANES))).astype(BF16)
                m_ref[h] = m_new
                if pending is not None:
                    finish(*pending)
                pending = (h, pe, jnp.exp2(m_prev - m_new))
            finish(*pending)

        @pl.when(ki < qi)
        def _():
            step(False)

        @pl.when(ki == qi)
        def _():
            step(True)
            lane = lax.broadcasted_iota(jnp.int32, (tb, LANES), 1)
            for p in range(N_HEADS // 2):
                outs = []
                for h in (2 * p, 2 * p + 1):
                    acc = acc_ref[h]
                    l = acc[:, V_AUX:V_AUX + 1]
                    outs.append(acc * (1.0 / l))
                    hi, lo = _hi_lo(m_ref[h][:, 0:1] + jnp.log(l) * LOG2E)
                    qb_ref[h] = _lane_pair((tb, LANES), QK_AUX, hi, lo, q_ref[h].astype(F32)).astype(BF16)
                o_ref[:, p * LANES:(p + 1) * LANES] = jnp.where(lane < HEAD, outs[0], pltpu.roll(outs[1], HEAD, 1)).astype(BF16)

    sd = jax.ShapeDtypeStruct
    qspec = pl.BlockSpec((N_HEADS, tb, LANES), lambda p, qi_ref, ki_ref: (0, qi_ref[p], 0))
    kspec = pl.BlockSpec((N_HEADS, tb, LANES), lambda p, qi_ref, ki_ref: (0, ki_ref[p], 0))
    return pl.pallas_call(
        body, name="mla_flash_fwd",
        grid_spec=pltpu.PrefetchScalarGridSpec(
            num_scalar_prefetch=2, grid=(len(pairs),),
            in_specs=[qspec, kspec, kspec],
            out_specs=[pl.BlockSpec((tb, MLA_W), lambda p, qi_ref, ki_ref: (qi_ref[p], 0)), qspec],
            scratch_shapes=[pltpu.VMEM((N_HEADS, tb, LANES), F32), pltpu.VMEM((N_HEADS, tb, LANES), F32)]),
        out_shape=[sd((s, MLA_W), BF16), sd((N_HEADS, s, LANES), BF16)],
        compiler_params=_cp("arbitrary"),
    )(qi_of, ki_of, q, k, v)


def _out_proj_call(x, yret, ymla, wout, ts):
    s = x.shape[0]

    def body(x_ref, yr_ref, ym_ref, w_ref, x1_ref, r_ref):
        x1 = x_ref[...] + _dot(yr_ref[...], w_ref[0:RET_W, :]) + _dot(ym_ref[...], w_ref[RET_W:, :])
        x1_ref[...] = x1
        r_ref[...] = _rstd(x1)

    sd = jax.ShapeDtypeStruct
    return pl.pallas_call(
        body, name="out_proj", grid=(s // ts,),
        in_specs=[_row(ts, D_MODEL), _row(ts, RET_W), _row(ts, MLA_W), _full((D_MODEL, D_MODEL))],
        out_specs=[_row(ts, D_MODEL), _row(ts, 1)],
        out_shape=[sd((s, D_MODEL), F32), sd((s, 1), F32)],
        compiler_params=_cp("parallel"),
    )(x, yret, ymla, wout)


W_UP_SHARD = F2 // 4


def _ffn_fwd_call(x1, r2, fnw, wup4, cw, cb, wdown, tgt, fw, ts):
    s = x1.shape[0]
    wsh = W_UP_SHARD

    def body(x_ref, r_ref, fnw_ref, wup_ref, cw_ref, cb_ref, wd_ref, t_ref, fw_ref,
             u_ref, uc_ref, dx2_ref, loss_ref, gfw_ref, carry_ref):
        _zero_first(pl.program_id(0) == 0, carry_ref, loss_ref, gfw_ref)
        xv = x_ref[...]
        h = (xv * r_ref[...] * fnw_ref[...]).astype(BF16)
        conv = []
        for j in range(4):
            cols = slice(j * wsh, (j + 1) * wsh)
            ub = _dot(h, wup_ref[j]).astype(BF16)
            u_ref[:, cols] = ub
            u = ub.astype(F32)
            u1, u2 = _shifted(u, carry_ref[:, cols])
            w = cw_ref[:, cols]
            cb16 = (cb_ref[:, cols] + w[0:1, :] * u2 + w[1:2, :] * u1 + w[2:3, :] * u).astype(BF16)
            uc_ref[:, cols] = cb16
            conv.append(cb16.astype(F32))
            carry_ref[:, cols] = u[ts - 8:, :]
        acc = xv
        for j in range(2):
            a = (_silu(conv[j]) * conv[j + 2]).astype(BF16)
            acc = acc + _dot(a, wd_ref[j * wsh:(j + 1) * wsh, :])
        r = _rstd(acc)
        xh = acc * r
        fwv = fw_ref[...]
        e = xh * fwv - t_ref[...]
        loss_ref[...] += (0.5 / D_MODEL) * _colsum(jnp.sum(e * e, axis=1, keepdims=True))
        dy = e * (1.0 / D_MODEL)
        gfw_ref[...] += _colsum(dy * xh)
        dx2_ref[...] = _norm_bwd(dy, xh, r, fwv)

    sd = jax.ShapeDtypeStruct
    once = lambda shape: pl.BlockSpec(shape, lambda i: (0,) * len(shape), pipeline_mode=pl.Buffered(1))
    return pl.pallas_call(
        body, name="ffn_fwd_loss", grid=(s // ts,),
        in_specs=[_row(ts, D_MODEL), _row(ts, 1), once((1, D_MODEL)), once((4, D_MODEL, wsh)),
                  once((3, F2)), once((1, F2)), once((D_FF, D_MODEL)), _row(ts, D_MODEL), once((1, D_MODEL))],
        out_specs=[_row(ts, F2), _row(ts, F2), _row(ts, D_MODEL), _full((1, 1)), _full((1, D_MODEL))],
        out_shape=[sd((s, F2), BF16), sd((s, F2), BF16), sd((s, D_MODEL), F32), sd((1, 1), F32), sd((1, D_MODEL), F32)],
        scratch_shapes=[pltpu.VMEM((8, F2), F32)],
        compiler_params=_cp("arbitrary", vmem=VMEM_LIMIT_MLP),
    )(x1, r2, fnw, wup4, cw, cb, wdown, tgt, fw)


def _shifted(u, hal):
    row = lax.broadcasted_iota(jnp.int32, hal.shape, 0)
    r1, r2 = pltpu.roll(u, 1, 0), pltpu.roll(u, 2, 0)
    top1 = jnp.where(row == 0, hal[7:8, :], r1[0:8, :])
    top2 = jnp.where(row == 0, hal[6:7, :], jnp.where(row == 1, hal[7:8, :], r2[0:8, :]))
    return jnp.concatenate([top1, r1[8:, :]], axis=0), jnp.concatenate([top2, r2[8:, :]], axis=0)


def _win_ext_call(win):
    blocks = win[None] if win.ndim == 2 else win
    nb, r, wb = blocks.shape
    tr = min(r, 256)

    def body(b_ref, o_ref):
        left, right, at = [], [], IN_W - ROPE
        for j in range(nb):
            blk = b_ref[j]
            cut = min(max(at - j * wb, 0), wb)
            left += [blk[:, :cut]] if cut else []
            right += [blk[:, cut:]] if cut < wb else []
        pad = lambda n: jnp.zeros((tr, n), o_ref.dtype)
        o_ref[...] = jnp.concatenate(left + [pad(KPE_LO)] + right + [pad(LANES - KPE_LO - ROPE)], -1)

    return pl.pallas_call(
        body, name="w_in_layout", grid=(r // tr,),
        in_specs=[pl.BlockSpec((nb, tr, wb), lambda i: (0, i, 0))], out_specs=_row(tr, IN_EXT),
        out_shape=jax.ShapeDtypeStruct((r, IN_EXT), win.dtype),
        compiler_params=_cp("parallel"),
    )(blocks)


def _win_grad_blocks_call(g_ext, nb):
    r = g_ext.shape[0]
    wb = IN_W // nb
    tr = min(r, 256)
    lo = IN_W - ROPE

    def body(g_ref, o_ref):
        g = g_ref[...]
        for j in range(nb):
            a, b = j * wb, (j + 1) * wb
            parts = ([g[:, a:min(b, lo)]] if a < lo else []) + ([g[:, max(a, lo) + KPE_LO:b + KPE_LO]] if b > lo else [])
            o_ref[j] = jnp.concatenate(parts, -1)

    return pl.pallas_call(
        body, name="w_in_grad_blocks", grid=(r // tr,),
        in_specs=[_row(tr, IN_EXT)], out_specs=pl.BlockSpec((nb, tr, wb), lambda i: (0, i, 0)),
        out_shape=jax.ShapeDtypeStruct((nb, r, wb), g_ext.dtype),
        compiler_params=_cp("parallel"),
    )(g_ext)


def _prep_weights(w):
    win_ext = _win_ext_call(w["w_in"])
    wuq = w["w_uq"].reshape(Q_RANK, N_HEADS, HEAD + ROPE)
    wq = jnp.concatenate([wuq, jnp.zeros((Q_RANK, N_HEADS, LANES - HEAD - ROPE), wuq.dtype)], -1).transpose(1, 0, 2)
    wukv = w["w_ukv"].reshape(KV_RANK, N_HEADS, 2 * HEAD)
    zk = jnp.zeros((KV_RANK, N_HEADS, HEAD), wukv.dtype)
    wk = jnp.concatenate([wukv[:, :, :HEAD], zk], -1).transpose(1, 0, 2)
    wv = jnp.concatenate([wukv[:, :, HEAD:], zk], -1).transpose(1, 0, 2)
    c = lambda a: a.astype(BF16)
    return dict(win=c(win_ext), wq=c(wq), wk=c(wk), wv=c(wv), wout=c(w["w_out"]))


def _prep_mlp_weights(w):
    wup = w["w_up"]
    if wup.ndim == 2:
        wup = wup.reshape(D_MODEL, 4, W_UP_SHARD).transpose(1, 0, 2)
    return dict(wup=wup.astype(BF16), wdown=w["w_down"].astype(BF16))


def _tiles(s):
    return dict(ts=min(s, 512), tr=min(s, 2048), tbf=min(s, 1024), tb=min(s, 512), t2=min(s, 256),
                tw=min(s, 2048), t1=min(s, 1024))


class _Exchanges:
    def __init__(self, w):
        self.w = w

    def mlp_weights(self, after):
        return self.w

    def mlp_grads(self, gw):
        pass

    def behind_out_bwd(self, after):
        pass

    def behind_attention(self, after):
        pass


def _forward(x, positions, tgt, w, small, ex):
    s = x.shape[0]
    t = _tiles(s)
    pw = _prep_weights(w)
    cos_r, sin_r, cos_m, sin_m = _rope_tables(positions)
    rc = _ret_consts()
    q, k, v, g, cq, ckv, mq, mk, mv, r1 = _f1_call(
        x, small["attn_norm_w"], pw["win"], small["mla_q_norm_w"], small["mla_kv_norm_w"], pw["wq"], pw["wk"], pw["wv"],
        cos_r, sin_r, cos_m, sin_m, t["ts"])
    y_mla, mqb = _flash_fwd_call(mq, mk, mv, t["tbf"])
    o_ret, y_ret = _ret_fwd_call(q, k, v, g, small["ret_gn_w"], rc, t["tr"], order=y_mla)
    x1, r2 = _out_proj_call(x, y_ret, y_mla, pw["wout"], t["ts"])
    pw.update(_prep_mlp_weights(ex.mlp_weights(r2)))
    u, uc, dx2, loss, g_fw = _ffn_fwd_call(x1, r2, small["ffn_norm_w"], pw["wup"], w["conv_w"], small["conv_b"], pw["wdown"],
                                           tgt, small["final_norm_w"], t["ts"])
    return dict(pw=pw, tabs=(cos_r, sin_r, cos_m, sin_m), rc=rc, q=q, k=k, v=v, g=g, cq=cq, ckv=ckv, r1=r1,
                o_ret=o_ret, y_ret=y_ret, mqb=mqb, mk=mk, mv=mv, y_mla=y_mla, x1=x1, r2=r2, u=u, uc=uc,
                dx2=dx2, loss=loss, g_fw=g_fw)


def _norm_bwd(dh, xh, r, nw):
    dxn = dh * nw
    return r * (dxn - xh * jnp.mean(dxn * xh, axis=-1, keepdims=True))


def _ordered_after(body, order):
    if order is None:
        return body, [], []
    return (lambda order_ref, *refs: body(*refs)), [pl.BlockSpec(memory_space=pl.ANY)], [order]


def _zero_first(first, *refs):
    @pl.when(first)
    def _():
        for ref in refs:
            ref[...] = jnp.zeros_like(ref)


def _colsum(v):
    return jnp.sum(v, axis=0, keepdims=True)


def _dsilu(g, sg):
    return sg * (1.0 + g * (1.0 - sg))


def _ffn_bwd_call(dx2, u, uc, cw, wdown, wup4, x1, r2, fnw, ts):
    s = dx2.shape[0]
    nt = s // ts
    wsh = W_UP_SHARD
    rev = lambda i: nt - 1 - i

    def body(dx2_ref, u_ref, uc_ref, cw_ref, wd_ref, wup_ref, x_ref, r_ref, fnw_ref,
             du_ref, dx1_ref, dcw_ref, dcb_ref, dfnw_ref, dwd_hbm, carry_ref, dwd_ref, sem):
        i = pl.program_id(0)
        _zero_first(i == 0, carry_ref, dwd_ref, dcw_ref, dcb_ref, dfnw_ref)
        dxb = dx2_ref[...].astype(BF16)
        dh = jnp.zeros((ts, D_MODEL), F32)
        for j in range(2):
            gcols = slice(j * wsh, (j + 1) * wsh)
            vcols = slice(D_FF + j * wsh, D_FF + (j + 1) * wsh)
            gate, val = uc_ref[:, gcols].astype(F32), uc_ref[:, vcols].astype(F32)
            da = _dot_nt(dxb, wd_ref[gcols, :])
            sg = _sigmoid(gate)
            sl = gate * sg
            dwd_ref[gcols, :] += _dot_tn((sl * val).astype(BF16), dxb)
            for d, cols, shard in ((da * val * _dsilu(gate, sg), gcols, j), (da * sl, vcols, 2 + j)):
                d1, d2 = _shifted_up(d, carry_ref[:, cols])
                uv = u_ref[:, cols].astype(F32)
                for t, dt in enumerate((d2, d1, d)):
                    dcw_ref[t:t + 1, cols] += _colsum(dt * uv)
                dcb_ref[:, cols] += _colsum(d)
                w = cw_ref[:, cols]
                du = (w[2:3, :] * d + w[1:2, :] * d1 + w[0:1, :] * d2).astype(BF16)
                du_ref[:, cols] = du
                dh = dh + _dot_nt(du, wup_ref[shard])
                carry_ref[:, cols] = d[0:8, :]
        r = r_ref[...]
        xh = x_ref[...] * r
        dfnw_ref[...] += _colsum(dh * xh)
        dx1_ref[...] = dx2_ref[...] + _norm_bwd(dh, xh, r, fnw_ref[...])

        @pl.when(i == nt - 1)
        def _():
            cp = pltpu.make_async_copy(dwd_ref, dwd_hbm, sem)
            cp.start()
            cp.wait()

    sd = jax.ShapeDtypeStruct
    row = lambda c: pl.BlockSpec((ts, c), lambda i: (rev(i), 0))
    once = lambda shape: pl.BlockSpec(shape, lambda i: (0,) * len(shape), pipeline_mode=pl.Buffered(1))
    return pl.pallas_call(
        body, name="ffn_bwd", grid=(nt,),
        in_specs=[row(D_MODEL), row(F2), row(F2), once((3, F2)), once((D_FF, D_MODEL)), once((4, D_MODEL, wsh)),
                  row(D_MODEL), row(1), once((1, D_MODEL))],
        out_specs=[row(F2), row(D_MODEL), _full((3, F2)), _full((1, F2)), _full((1, D_MODEL)), pl.BlockSpec(memory_space=pl.ANY)],
        out_shape=[sd((s, F2), BF16), sd((s, D_MODEL), F32), sd((3, F2), F32), sd((1, F2), F32), sd((1, D_MODEL), F32),
                   sd((D_FF, D_MODEL), F32)],
        scratch_shapes=[pltpu.VMEM((8, F2), F32), pltpu.VMEM((D_FF, D_MODEL), F32), pltpu.SemaphoreType.DMA],
        compiler_params=_cp("arbitrary", vmem=VMEM_LIMIT_MLP),
    )(dx2, u, uc, cw, wdown, wup4, x1, r2, fnw)


def _shifted_up(d, hal):
    n = d.shape[0]
    row = lax.broadcasted_iota(jnp.int32, hal.shape, 0)
    r1, r2 = pltpu.roll(d, n - 1, 0), pltpu.roll(d, n - 2, 0)
    end1 = jnp.where(row == 7, hal[0:1, :], r1[n - 8:, :])
    end2 = jnp.where(row == 6, hal[0:1, :], jnp.where(row == 7, hal[1:2, :], r2[n - 8:, :]))
    return jnp.concatenate([r1[:n - 8, :], end1], axis=0), jnp.concatenate([r2[:n - 8, :], end2], axis=0)


def _dw_norm_call(x, r, nw, b, ts, tn, name):
    s, n = b.shape
    k = x.shape[1]

    def body(x_ref, r_ref, nw_ref, b_ref, dw_ref):
        _zero_first(pl.program_id(1) == 0, dw_ref)
        h = (x_ref[...] * r_ref[...] * nw_ref[...]).astype(BF16)
        dw_ref[...] += _dot_tn(h, b_ref[...])

    return pl.pallas_call(
        body, name=name, grid=(n // tn, s // ts),
        in_specs=[pl.BlockSpec((ts, k), lambda j, i: (i, 0)), pl.BlockSpec((ts, 1), lambda j, i: (i, 0)),
                  pl.BlockSpec((1, k), lambda j, i: (0, 0)), pl.BlockSpec((ts, tn), lambda j, i: (i, j))],
        out_specs=pl.BlockSpec((None, k, tn), lambda j, i: (j, 0, 0)),
        out_shape=jax.ShapeDtypeStruct((n // tn, k, tn), F32),
        compiler_params=_cp("parallel", "arbitrary"),
    )(x, r, nw, b)


def _out_bwd_call(dx1, yret, ymla, wout, ts, order=None):
    s = dx1.shape[0]

    def body(dx_ref, yr_ref, ym_ref, w_ref, dyr_ref, do_ref, dwo_ref):
        _zero_first(pl.program_id(0) == 0, dwo_ref)
        dxb = dx_ref[...].astype(BF16)
        dmix = _dot_nt(dxb, w_ref[...])
        dyr_ref[...] = dmix[:, :RET_W]
        ym = ym_ref[...]
        lane = lax.broadcasted_iota(jnp.int32, (ts, LANES), 1)
        for p in range(N_HEADS // 2):
            dom = dmix[:, RET_W + p * LANES:RET_W + (p + 1) * LANES]
            prod = dom * ym[:, p * LANES:(p + 1) * LANES].astype(F32)
            for hh in range(2):
                mine = (lane >= HEAD) if hh else (lane < HEAD)
                hi, lo = _hi_lo(jnp.sum(jnp.where(mine, prod, 0.0), axis=1, keepdims=True))
                base = jnp.where(lane < HEAD, pltpu.roll(dom, HEAD, 1) if hh else dom, 0.0)
                do_ref[2 * p + hh] = _lane_pair((ts, LANES), V_AUX, -hi, -lo, base).astype(BF16)
        dwo_ref[0:RET_W, :] += _dot_tn(yr_ref[...], dxb)
        dwo_ref[RET_W:, :] += _dot_tn(ym, dxb)

    sd = jax.ShapeDtypeStruct
    body, first_specs, first = _ordered_after(body, order)
    return pl.pallas_call(
        body, name="out_proj_bwd", grid=(s // ts,),
        in_specs=first_specs + [_row(ts, D_MODEL), _row(ts, RET_W), _row(ts, MLA_W), _full((D_MODEL, D_MODEL))],
        out_specs=[_row(ts, RET_W), _hrow(N_HEADS, ts, LANES), _full((D_MODEL, D_MODEL))],
        out_shape=[sd((s, RET_W), F32), sd((N_HEADS, s, LANES), BF16), sd((D_MODEL, D_MODEL), F32)],
        compiler_params=_cp("arbitrary"),
    )(*first, dx1, yret, ymla, wout)


def _ret_bwd_q_call(q, k, v, o, g, dy, gnw, rc, cos_r, sin_r, tr):
    s = q.shape[0]
    c = RET_CHUNK
    nc = tr // c
    ns = RET_SLABS

    def body(q_ref, k_ref, v_ref, o_ref, g_ref, dy_ref, gnw_ref, dm_ref, zeta_ref, xi_ref, cd_ref, bd_ref, cr_ref, sr_ref,
             dq_ref, dg_ref, do_ref, dgnw_ref, st_ref):
        _zero_first(pl.program_id(1) == 0, st_ref, dgnw_ref)
        bd = bd_ref[...]
        avg = bd * (1.0 / HEAD)
        chunks = [slice(ci * c, (ci + 1) * c) for ci in range(nc)]
        lanes = [slice(sl * LANES, (sl + 1) * LANES) for sl in range(ns)]
        dov = []
        for ln in lanes:
            ov = o_ref[:, ln]
            ctr = ov - _dot_hi(ov, avg)
            rs = lax.rsqrt(_dot_hi(ctr * ctr, avg) + EPS)
            oh = ctr * rs
            gg, dyv, gnw_v = g_ref[:, ln], dy_ref[:, ln], gnw_ref[:, ln]
            sg = _sigmoid(gg)
            sl = gg * sg
            dg_ref[:, ln] = (dyv * oh * gnw_v * _dsilu(gg, sg)).astype(BF16)
            dgnw_ref[:, ln] += _colsum(dyv * sl * oh)
            doh = dyv * sl * gnw_v
            dov.append((rs * (doh - _dot_hi(doh, avg) - oh * _dot_hi(doh * oh, avg))).astype(BF16))
            do_ref[:, ln] = dov[-1]
        states = _ret_states(k_ref, v_ref, zeta_ref, cd_ref, bd, st_ref, chunks, lanes, False)
        for ci, rows in enumerate(chunks):
            for sl, ln in enumerate(lanes):
                doc = dov[sl][rows, :]
                dq = (_dot_nt(doc, states[sl][ci]) * xi_ref[sl]
                      + _pair_product(doc, _stack_heads(v_ref[rows, ln]), dm_ref[sl], _stack_heads(k_ref[rows, ln])))
                dq_ref[rows, ln] = _unrope(dq, cr_ref[rows, :], sr_ref[rows, :], HEAD // 2).astype(BF16)

    specs = _ret_specs(tr, lambda i: i)
    sd = jax.ShapeDtypeStruct
    return pl.pallas_call(
        body, name="ret_bwd_q", grid=(4 // ns, s // tr),
        in_specs=[specs["slab"]] * 6 + [specs["vec"], specs["dmask"], specs["rows"], specs["rows"], specs["state"], specs["bd"],
                                        specs["tab"], specs["tab"]],
        out_specs=[specs["slab"]] * 3 + [specs["vec"]],
        out_shape=[sd((s, RET_W), BF16), sd((s, RET_W), BF16), sd((s, RET_W), BF16), sd((1, RET_W), F32)],
        scratch_shapes=[pltpu.VMEM((ns, LANES, LANES), F32)],
        compiler_params=_cp("parallel", "arbitrary"),
    )(q, k, v, o, g, dy, gnw, rc["dmask"], rc["zeta"], rc["xi"], rc["cd"], rc["bd"], cos_r, sin_r)


def _ret_bwd_kv_call(q, k, v, do, rc, cos_r, sin_r, tr):
    s = q.shape[0]
    c = RET_CHUNK
    nc = tr // c
    nt = s // tr
    ns = RET_SLABS

    def body(q_ref, k_ref, v_ref, do_ref, dm_ref, zeta_ref, xi_ref, cd_ref, bd_ref, cr_ref, sr_ref, dk_ref, dv_ref, gs_ref):
        _zero_first(pl.program_id(1) == 0, gs_ref)
        bd = bd_ref[...]
        chunks = [slice(ci * c, (ci + 1) * c) for ci in range(nc)]
        lanes = [slice(sl * LANES, (sl + 1) * LANES) for sl in range(ns)]
        states = _ret_states(q_ref, do_ref, xi_ref, cd_ref, bd, gs_ref, chunks, lanes, True)
        for ci, rows in enumerate(chunks):
            for sl, ln in enumerate(lanes):
                kc, vc = k_ref[rows, ln], v_ref[rows, ln]
                q2, do2 = _stack_heads(q_ref[rows, ln]), _stack_heads(do_ref[rows, ln])
                gb = states[sl][ci]
                dk = _dot_nt(vc, gb) * zeta_ref[sl] + _pair_product(vc, do2, dm_ref[sl], q2)
                dv = _dot(kc, gb) * zeta_ref[sl] + _pair_product(kc, q2, dm_ref[sl], do2)
                dk_ref[rows, ln] = (_unrope(dk, cr_ref[rows, :], sr_ref[rows, :], HEAD // 2) * (HEAD ** -0.5)).astype(BF16)
                dv_ref[rows, ln] = dv.astype(BF16)

    specs = _ret_specs(tr, lambda i: nt - 1 - i)
    sd = jax.ShapeDtypeStruct
    return pl.pallas_call(
        body, name="ret_bwd_kv", grid=(4 // ns, nt),
        in_specs=[specs["slab"]] * 4 + [specs["dmask"], specs["rows"], specs["rows"], specs["state"], specs["bd"],
                                        specs["tab"], specs["tab"]],
        out_specs=[specs["slab"]] * 2,
        out_shape=[sd((s, RET_W), BF16), sd((s, RET_W), BF16)],
        scratch_shapes=[pltpu.VMEM((ns, LANES, LANES), F32)],
        compiler_params=_cp("parallel", "arbitrary"),
    )(q, k, v, do, rc["dmask_t"], rc["zeta"], rc["xi"], rc["cd"], rc["bd"], cos_r, sin_r)


FLASH_BWD_HEADS = 8


def _flash_bwd_call(qb, k, v, do, tb, order=None):
    s = qb.shape[1]
    nb = s // tb
    hg = FLASH_BWD_HEADS
    pairs = [(a, b) for a in range(nb) for b in range(a, nb)]
    ki_of, qi_of = (jnp.asarray(np.array(col, np.int32)) for col in zip(*pairs))
    extra = [] if order is None else [order]

    def body(ki_ref, qi_ref, *refs):
        q_ref, k_ref, v_ref, do_ref, dk_ref, dv_ref, dq_hbm, dka_ref, dva_ref, dq_ref, sem = refs[len(extra):]
        g, p = pl.program_id(0), pl.program_id(1)
        ki, qi = ki_ref[p], qi_ref[p]
        _zero_first(p == 0, dq_ref)
        _zero_first(qi == ki, dka_ref, dva_ref)
        rows = pl.ds(pl.multiple_of(qi * tb, tb), tb)

        def step(masked):
            if masked:
                keep = lax.broadcasted_iota(jnp.int32, (tb, tb), 0) <= lax.broadcasted_iota(jnp.int32, (tb, tb), 1)
            for h in range(hg):
                st = _dot_nt(k_ref[h], q_ref[h])
                if masked:
                    st = jnp.where(keep, st, NEG)
                pt = jnp.exp2(st)
                dob = do_ref[h]
                dva_ref[h] += _dot(pt.astype(BF16), dob)
                dst = (pt * _dot_nt(v_ref[h], dob)).astype(BF16)
                dka_ref[h] += _dot(dst, q_ref[h])
                dq_ref[h, rows, :] += _dot_tn(dst, k_ref[h])

        @pl.when(qi > ki)
        def _():
            step(False)

        @pl.when(qi == ki)
        def _():
            step(True)

        @pl.when(qi == nb - 1)
        def _():
            dk_ref[...] = (dka_ref[...] * LN2).astype(BF16)
            dv_ref[...] = dva_ref[...].astype(BF16)

        @pl.when(p == len(pairs) - 1)
        def _():
            cp = pltpu.make_async_copy(dq_ref, dq_hbm.at[pl.ds(g * hg, hg)], sem)
            cp.start()
            cp.wait()

    kspec = pl.BlockSpec((hg, tb, LANES), lambda g, p, ki_ref, qi_ref: (g, ki_ref[p], 0))
    qspec = pl.BlockSpec((hg, tb, LANES), lambda g, p, ki_ref, qi_ref: (g, qi_ref[p], 0))
    hm = jax.ShapeDtypeStruct((N_HEADS, s, LANES), BF16)
    return pl.pallas_call(
        body, name="mla_flash_bwd",
        grid_spec=pltpu.PrefetchScalarGridSpec(
            num_scalar_prefetch=2, grid=(N_HEADS // hg, len(pairs)),
            in_specs=[ANY] * len(extra) + [qspec, kspec, kspec, qspec],
            out_specs=[kspec, kspec, ANY],
            scratch_shapes=[pltpu.VMEM((hg, tb, LANES), F32), pltpu.VMEM((hg, tb, LANES), F32),
                            pltpu.VMEM((hg, s, LANES), F32), pltpu.SemaphoreType.DMA]),
        out_shape=[hm, hm, jax.ShapeDtypeStruct((N_HEADS, s, LANES), F32)],
        compiler_params=_cp("arbitrary", "arbitrary"),
    )(ki_of, qi_of, *extra, qb, k, v, do)


def _mla_post_call(dq, dk, dv, cq, ckv, qnw, kvnw, wq, wk, wv, cos_m, sin_m, ts):
    s = cq.shape[0]

    def body(dq_ref, dk_ref, dv_ref, cq_ref, ckv_ref, qnw_ref, kvnw_ref, wq_ref, wk_ref, wv_ref, cm_ref, sm_ref,
             dcq_ref, dckv_ref, dkpe_ref, dwq_ref, dwk_ref, dwv_ref, dqnw_ref, dkvnw_ref):
        _zero_first(pl.program_id(0) == 0, dwq_ref, dwk_ref, dwv_ref, dqnw_ref, dkvnw_ref)
        cqv, ckvv = cq_ref[...], ckv_ref[...]
        rq, rkv = _rstd(cqv), _rstd(ckvv)
        qh_, kvh_ = cqv * rq, ckvv * rkv
        qnw_v, kvnw_v = qnw_ref[...], kvnw_ref[...]
        cqn = (qh_ * qnw_v).astype(BF16)
        ckvn = (kvh_ * kvnw_v).astype(BF16)
        cm, sm = cm_ref[...], sm_ref[...]
        dcqn = jnp.zeros((ts, Q_RANK), F32)
        dckvn = jnp.zeros((ts, KV_RANK), F32)
        dkpe = jnp.zeros((ts, LANES), F32)
        for h in range(N_HEADS):
            dqu = _unrope(dq_ref[h] * SM_SCALE, cm, sm, ROPE // 2).astype(BF16)
            dwq_ref[h] += _dot_tn(cqn, dqu)
            dcqn = dcqn + _dot_nt(dqu, wq_ref[h])
            dkb, dvb = dk_ref[h], dv_ref[h]
            dkpe = dkpe + dkb.astype(F32)
            dwk_ref[h] += _dot_tn(ckvn, dkb)
            dwv_ref[h] += _dot_tn(ckvn, dvb)
            dckvn = dckvn + _dot_nt(dkb, wk_ref[h]) + _dot_nt(dvb, wv_ref[h])
        lane = lax.broadcasted_iota(jnp.int32, (ts, LANES), 1)
        dkpe = jnp.where((lane >= KPE_LO) & (lane < KPE_LO + ROPE), dkpe, 0.0)
        dkpe_ref[...] = _unrope(dkpe, cm, sm, ROPE // 2).astype(BF16)
        dqnw_ref[...] += _colsum(dcqn * qh_)
        dkvnw_ref[...] += _colsum(dckvn * kvh_)
        dcq_ref[...] = _norm_bwd(dcqn, qh_, rq, qnw_v).astype(BF16)
        dckv_ref[...] = _norm_bwd(dckvn, kvh_, rkv, kvnw_v).astype(BF16)

    sd = jax.ShapeDtypeStruct
    hm = _hrow(N_HEADS, ts, LANES)
    return pl.pallas_call(
        body, name="mla_post", grid=(s // ts,),
        in_specs=[hm, hm, hm, _row(ts, Q_RANK), _row(ts, KV_RANK), _full((1, Q_RANK)), _full((1, KV_RANK)),
                  _full((N_HEADS, Q_RANK, LANES)), _full((N_HEADS, KV_RANK, LANES)), _full((N_HEADS, KV_RANK, LANES)),
                  _row(ts, LANES), _row(ts, LANES)],
        out_specs=[_row(ts, Q_RANK), _row(ts, KV_RANK), _row(ts, LANES),
                   _full((N_HEADS, Q_RANK, LANES)), _full((N_HEADS, KV_RANK, LANES)), _full((N_HEADS, KV_RANK, LANES)),
                   _full((1, Q_RANK)), _full((1, KV_RANK))],
        out_shape=[sd((s, Q_RANK), BF16), sd((s, KV_RANK), BF16), sd((s, LANES), BF16),
                   sd((N_HEADS, Q_RANK, LANES), F32), sd((N_HEADS, KV_RANK, LANES), F32), sd((N_HEADS, KV_RANK, LANES), F32),
                   sd((1, Q_RANK), F32), sd((1, KV_RANK), F32)],
        compiler_params=_cp("arbitrary"),
    )(dq, dk, dv, cq, ckv, qnw, kvnw, wq, wk, wv, cos_m, sin_m)


def _in_bwd_call(parts, x, r1, anw, dx1, win, ts):
    s = x.shape[0]
    widths = [p.shape[1] for p in parts]
    np_ = len(parts)

    def body(*refs):
        p_refs = refs[:np_]
        x_ref, r_ref, anw_ref, dx1_ref, w_ref, dx_ref, dw_ref, danw_ref = refs[np_:]
        _zero_first(pl.program_id(0) == 0, dw_ref, danw_ref)
        dproj = jnp.concatenate([p[...] for p in p_refs], axis=-1)
        r, anw_v = r_ref[...], anw_ref[...]
        xh = x_ref[...] * r
        dw_ref[...] += _dot_tn((xh * anw_v).astype(BF16), dproj)
        dh = _dot_nt(dproj, w_ref[...])
        danw_ref[...] += _colsum(dh * xh)
        dx_ref[...] = dx1_ref[...] + _norm_bwd(dh, xh, r, anw_v)

    sd = jax.ShapeDtypeStruct
    return pl.pallas_call(
        body, name="in_proj_bwd", grid=(s // ts,),
        in_specs=[_row(ts, w) for w in widths]
        + [_row(ts, D_MODEL), _row(ts, 1), _full((1, D_MODEL)), _row(ts, D_MODEL), _full((D_MODEL, IN_EXT))],
        out_specs=[_row(ts, D_MODEL), _full((D_MODEL, IN_EXT)), _full((1, D_MODEL))],
        out_shape=[sd((s, D_MODEL), F32), sd((D_MODEL, IN_EXT), F32), sd((1, D_MODEL), F32)],
        compiler_params=_cp("arbitrary"),
    )(*parts, x, r1, anw, dx1, win)


def _local_step(x, positions, tgt, w, small, ex=None):
    s = x.shape[0]
    t = _tiles(s)
    ex = _Exchanges(w) if ex is None else ex
    f = _forward(x, positions, tgt, w, small, ex)
    pw, rc = f["pw"], f["rc"]
    cos_r, sin_r, cos_m, sin_m = f["tabs"]
    dx2, loss, g_fw = f["dx2"], f["loss"], f["g_fw"]
    du, dx1, g_cw, g_cb, g_fnw, g_wd = _ffn_bwd_call(dx2, f["u"], f["uc"], w["conv_w"], pw["wdown"], pw["wup"],
                                                     f["x1"], f["r2"], small["ffn_norm_w"], t["t2"])
    g_wup = _dw_norm_call(f["x1"], f["r2"], small["ffn_norm_w"], du, t["tw"], F2 // 4, "dw_up")
    started = ex.mlp_grads(dict(w_up=g_wup, w_down=g_wd))
    dy_ret, do, g_wout = _out_bwd_call(dx1, f["y_ret"], f["y_mla"], pw["wout"], t["t1"], started)
    started = ex.behind_out_bwd(g_wout)
    drq, dg, do_ret, g_gnw = _ret_bwd_q_call(f["q"], f["k"], f["v"], f["o_ret"], f["g"], dy_ret, small["ret_gn_w"], rc, cos_r, sin_r, t["tr"])
    drk, drv = _ret_bwd_kv_call(f["q"], f["k"], f["v"], do_ret, rc, cos_r, sin_r, t["tr"])
    dmk, dmv, dmq = _flash_bwd_call(f["mqb"], f["mk"], f["mv"], do, t["tb"], started)
    ex.behind_attention(dmk)
    dcq, dckv, dkpe, g_wq, g_wk, g_wv, g_qnw, g_kvnw = _mla_post_call(
        dmq, dmk, dmv, f["cq"], f["ckv"], small["mla_q_norm_w"], small["mla_kv_norm_w"], pw["wq"], pw["wk"], pw["wv"], cos_m, sin_m, t["ts"])
    gx, g_win_ext, g_anw = _in_bwd_call([drq, drk, drv, dg, dcq, dckv, dkpe], x, f["r1"], small["attn_norm_w"], dx1, pw["win"], t["ts"])
    if w["w_in"].ndim == 3:
        g_win = _win_grad_blocks_call(g_win_ext, w["w_in"].shape[0])
    else:
        g_win = _win_grad_blocks_call(g_win_ext, 1)[0]
    g_wuq = g_wq.transpose(1, 0, 2)[:, :, :HEAD + ROPE].reshape(Q_RANK, N_HEADS * (HEAD + ROPE))
    g_wukv = jnp.concatenate([g_wk[:, :, :HEAD], g_wv[:, :, :HEAD]], -1).transpose(1, 0, 2).reshape(KV_RANK, 2 * MLA_W)
    gw = dict(w_in=g_win, w_uq=g_wuq, w_ukv=g_wukv, w_out=g_wout, w_up=g_wup,
              conv_w=g_cw, w_down=g_wd)
    gs = dict(attn_norm_w=g_anw, ret_gn_w=g_gnw, mla_q_norm_w=g_qnw, mla_kv_norm_w=g_kvnw, ffn_norm_w=g_fnw,
              conv_b=g_cb, final_norm_w=g_fw)
    return loss, gx, gw, gs


MESH_ID = pl.DeviceIdType.MESH
ANY = pl.BlockSpec(memory_space=pl.ANY)
VMEM_SPEC = pl.BlockSpec(memory_space=pltpu.VMEM)
N_DEV = 8
GROUP_A = (("w_in", (D_MODEL, IN_W // 4), 1), ("w_uq", (Q_RANK, 192), 1), ("w_ukv", (KV_RANK, 256), 1),
           ("w_out", (D_MODEL // 4, D_MODEL), 0))
GROUP_B = (("w_up", (D_MODEL, F2 // 4), 1), ("w_down", (D_FF // 4, D_MODEL), 0))
HBM_SPEC = pl.BlockSpec(memory_space=pltpu.HBM)
SEM_SPEC = pl.BlockSpec(memory_space=pltpu.SEMAPHORE)


def _mesh_pos():
    return lax.axis_index("x"), lax.axis_index("y"), lax.axis_index("c")


def _other_chips(x, y):
    return [(1 - x, y), (x, 1 - y), (1 - x, 1 - y)]


def _remote(src, dst, send_sems, recv_sems, k, dev):
    return pltpu.make_async_remote_copy(src_ref=src, dst_ref=dst, send_sem=send_sems.at[k], recv_sem=recv_sems.at[k],
                                        device_id=dev, device_id_type=MESH_ID)


def _gather_list_call(parts, tag):
    n = len(parts)

    def body(*refs):
        srcs, outs, (send_sems, recv_sems) = refs[:n], refs[n:2 * n], refs[2 * n:]
        x, y, c = _mesh_pos()
        sm = 2 * x + y
        chips = _other_chips(x, y)
        sib = (x, y, 1 - c)
        rc = lambda k, src, dst, dev: _remote(src, dst, send_sems, recv_sems, k, dev)
        first = [rc(7 * i + j, srcs[i].at[c], outs[i].at[sm, c], (cx, cy, c)) for i in range(n) for j, (cx, cy) in enumerate(chips)]
        own = [rc(7 * i + 6, srcs[i], outs[i].at[sm], sib) for i in range(n)]
        for cp in first + own:
            cp.start()
        passed = []
        for j, (cx, cy) in enumerate(chips):
            for i in range(n):
                land = outs[i].at[2 * cx + cy, c]
                rc(7 * i + j, srcs[i].at[c], land, (cx, cy, c)).wait_recv()
                cp = rc(7 * i + 3 + j, land, land, sib)
                cp.start()
                passed.append(cp)
        for j, (cx, cy) in enumerate(chips):
            for i in range(n):
                rc(7 * i + 3 + j, srcs[i].at[c], outs[i].at[2 * cx + cy, 1 - c], sib).wait_recv()
        for cp in own:
            cp.wait_recv()
        for cp in first + passed + own:
            cp.wait_send()

    return pl.pallas_call(
        body, name="weights_all_gather_" + tag,
        in_specs=[ANY] * n, out_specs=[ANY] * n,
        out_shape=[jax.ShapeDtypeStruct((4,) + p.shape, p.dtype) for p in parts],
        scratch_shapes=[pltpu.SemaphoreType.DMA((7 * n,)), pltpu.SemaphoreType.DMA((7 * n,))],
    )(*parts)


def _direct_gather_copies(srcs, lands, send_sems, recv_sems):
    x, y, c = _mesh_pos()
    sm = 2 * x + y
    sends, recvs = [], []
    for i, (src, land) in enumerate(zip(srcs, lands)):
        for j, (cx, cy) in enumerate(_other_chips(x, y)):
            for t in range(2):
                sends.append(_remote(src.at[c], land.at[sm, c], send_sems, recv_sems, 13 * i + 4 * j + 2 * c + t, (cx, cy, t)))
                recvs.append(_remote(src.at[t], land.at[2 * cx + cy, t], send_sems, recv_sems, 13 * i + 4 * j + 2 * t + c, (cx, cy, t)))
        sends.append(_remote(src, land.at[sm], send_sems, recv_sems, 13 * i + 12, (x, y, 1 - c)))
        recvs.append(_remote(src, land.at[sm], send_sems, recv_sems, 13 * i + 12, (x, y, 1 - c)))
    return sends, recvs


def _sibling_copies(srcs, lands, send_sems, recv_sems):
    x, y, c = _mesh_pos()
    cps = [_remote(src.at[s, 1 - c], land.at[s], send_sems, recv_sems, 4 * i + s, (x, y, 1 - c))
           for i, (src, land) in enumerate(zip(srcs, lands)) for s in range(4)]
    return cps, cps


def _chips_copies(srcs, lands, send_sems, recv_sems):
    x, y, c = _mesh_pos()
    cps = [_remote(src.at[2 * cx + cy], land.at[j], send_sems, recv_sems, 3 * i + j, (cx, cy, c))
           for i, (src, land) in enumerate(zip(srcs, lands)) for j, (cx, cy) in enumerate(_other_chips(x, y))]
    return cps, cps


def _share_copies(srcs, lands, send_sems, recv_sems):
    x, y, c = _mesh_pos()
    cps = [_remote(src, land, send_sems, recv_sems, i, (x, y, 1 - c)) for i, (src, land) in enumerate(zip(srcs, lands))]
    return cps, cps


def _exchange_call(name, copies, srcs, land_shapes, n_sems):
    n = len(srcs)

    def body(*refs):
        sends, recvs = copies(refs[:n], refs[n:2 * n], refs[2 * n], refs[2 * n + 1])
        for cp in sends:
            cp.start()
        for cp in sends:
            cp.wait_send()
        for cp in recvs:
            cp.wait_recv()

    return pl.pallas_call(
        body, name=name, in_specs=[ANY] * n, out_specs=[ANY] * n, out_shape=list(land_shapes),
        scratch_shapes=[pltpu.SemaphoreType.DMA((n_sems,)), pltpu.SemaphoreType.DMA((n_sems,))],
    )(*srcs)


def _exchange_start_call(name, copies, srcs, land_shapes, n_sems, order=None):
    n = len(srcs)
    extra = [] if order is None else [order]
    k = 2 * n + len(extra)

    def body(*refs):
        sends, _ = copies(refs[:n], refs[n:2 * n], refs[k], refs[k + 1])
        for cp in sends:
            cp.start()
        refs[-1][...] = jnp.zeros_like(refs[-1])

    hbm = lambda a: pltpu.with_memory_space_constraint(a, pltpu.HBM)
    lands = [hbm(lax.empty(sd.shape, sd.dtype)) for sd in land_shapes]
    sem = pltpu.SemaphoreType.DMA((n_sems,))
    out = pl.pallas_call(
        body, name=name,
        out_shape=(sem, sem, *[pltpu.HBM(a.shape, a.dtype) for a in list(srcs) + lands], jax.ShapeDtypeStruct((8, LANES), F32)),
        in_specs=[HBM_SPEC] * (2 * n) + [ANY] * len(extra), out_specs=(SEM_SPEC, SEM_SPEC, *[HBM_SPEC] * (2 * n), VMEM_SPEC),
        input_output_aliases={i: 2 + i for i in range(2 * n)},
        compiler_params=pltpu.CompilerParams(has_side_effects=pltpu.SideEffectType.DATAFLOW_SIDE_EFFECTING),
    )(*[hbm(a) for a in srcs], *lands, *extra)
    return out[0], out[1], out[2:2 + n], out[2 + n:2 + 2 * n], out[-1]


def _exchange_wait_call(name, copies, started, after):
    send_sems, recv_sems, srcs, lands, _ = started
    n = len(srcs)

    def body(*refs):
        sends, recvs = copies(refs[:n], refs[n:2 * n], refs[2 * n], refs[2 * n + 1])
        for cp in sends:
            cp.wait_send()
        for cp in recvs:
            cp.wait_recv()

    out = pl.pallas_call(
        body, name=name,
        out_shape=tuple(pltpu.HBM(a.shape, a.dtype) for a in list(srcs) + list(lands)),
        in_specs=[HBM_SPEC] * (2 * n) + [SEM_SPEC, SEM_SPEC, ANY], out_specs=tuple([HBM_SPEC] * (2 * n)),
        input_output_aliases={i: i for i in range(2 * n)},
        compiler_params=pltpu.CompilerParams(has_side_effects=pltpu.SideEffectType.DATAFLOW_SIDE_EFFECTING),
    )(*srcs, *lands, send_sems, recv_sems, after)
    return out[:n], out[n:]


def _rows_tile(rows, width, itemsize=4):
    limit = max(16, (3 << 20) // (width * itemsize))
    if rows <= limit:
        return rows
    return max(t for t in range(16, limit + 1, 16) if rows % t == 0)


def _sum_sibling_call(g, buf, c, name):
    _, _, rh, w = g.shape
    tile = _rows_tile(rh, w)

    def body(c_ref, g_ref, b_ref, p_ref, pb_ref):
        p = g_ref[...] + b_ref[...]
        p_ref[...] = p
        pb_ref[...] = p.astype(BF16)

    blk = pl.BlockSpec((None, tile, w), lambda s, i, c_ref: (s, i, 0))
    return pl.pallas_call(
        body, name=name,
        grid_spec=pltpu.PrefetchScalarGridSpec(
            num_scalar_prefetch=1, grid=(4, rh // tile),
            in_specs=[pl.BlockSpec((None, None, tile, w), lambda s, i, c_ref: (s, c_ref[0], i, 0)), blk],
            out_specs=[blk, blk]),
        out_shape=[jax.ShapeDtypeStruct((4, rh, w), F32), jax.ShapeDtypeStruct((4, rh, w), BF16)],
        compiler_params=_cp("parallel", "parallel"),
    )(c, g, buf)


def _sum_chips_call(p, buf, sm, name):
    _, rh, w = p.shape
    tile = _rows_tile(rh, w)

    def body(sm_ref, p_ref, b_ref, f_ref):
        f_ref[...] = ((p_ref[...] + b_ref[0].astype(F32)) + b_ref[1].astype(F32)) + b_ref[2].astype(F32)

    return pl.pallas_call(
        body, name=name,
        grid_spec=pltpu.PrefetchScalarGridSpec(
            num_scalar_prefetch=1, grid=(rh // tile,),
            in_specs=[pl.BlockSpec((None, tile, w), lambda i, sm_ref: (sm_ref[0], i, 0)),
                      pl.BlockSpec((3, tile, w), lambda i, sm_ref: (0, i, 0))],
            out_specs=pl.BlockSpec((tile, w), lambda i, sm_ref: (i, 0))),
        out_shape=jax.ShapeDtypeStruct((rh, w), F32),
        compiler_params=_cp("parallel"),
    )(sm, p, buf)


def _adamw_halves_call(w, g_mine, g_sib, c, m, v, name, transposed=False, order=None):
    extra = [] if order is None else [order]
    if transposed:
        rows, r = w.shape
        rh = r // 2
        tile = _rows_tile(rows, rh)
        whole = pl.BlockSpec((tile, rh), lambda h, i, c_ref: (i, h))
        half = pl.BlockSpec((tile, rh), lambda h, i, c_ref: (i, 0))
        nt = rows // tile
    else:
        r, wd = w.shape
        rh = r // 2
        tile = _rows_tile(rh, wd)
        nt = rh // tile
        whole = pl.BlockSpec((tile, wd), lambda h, i, c_ref: (h * nt + i, 0))
        half = pl.BlockSpec((tile, wd), lambda h, i, c_ref: (i, 0))

    def body(c_ref, w_ref, gm_ref, gs_ref, m_ref, v_ref, *rest):
        g_ref, d_ref, nm_ref, nv_ref = rest[len(extra):]
        gv = jnp.where(pl.program_id(0) == c_ref[0], gm_ref[...], gs_ref[...])
        g_ref[...] = gv
        nm = ADAM_B1 * m_ref[...] + (1.0 - ADAM_B1) * gv
        nv = ADAM_B2 * v_ref[...] + (1.0 - ADAM_B2) * jnp.square(gv)
        m_hat = nm / (1.0 - ADAM_B1 ** ADAM_STEP)
        v_hat = nv / (1.0 - ADAM_B2 ** ADAM_STEP)
        d_ref[...] = -ADAM_LR * (m_hat / (jnp.sqrt(v_hat) + ADAM_EPS) + ADAM_WD * w_ref[...])
        nm_ref[...] = nm
        nv_ref[...] = nv

    sd = jax.ShapeDtypeStruct(w.shape, F32)
    return pl.pallas_call(
        body, name=name,
        grid_spec=pltpu.PrefetchScalarGridSpec(
            num_scalar_prefetch=1, grid=(2, nt),
            in_specs=[whole, half, half, whole, whole] + [ANY] * len(extra), out_specs=[whole] * 4),
        out_shape=[sd, sd, sd, sd],
        compiler_params=_cp("parallel", "parallel"),
    )(c, w, g_mine, g_sib, m, v, *extra)


def _all_reduce8_call(vec, name):
    rows = vec.shape[0]

    def body(v_ref, out_ref, slots, send_sems, recv_sems):
        x, y, c = _mesh_pos()
        me = 4 * x + 2 * y + c
        slots[me] = v_ref[...]

        def rcopy(k, to_me):
            bx, by, bc = (k >> 2) & 1, (k >> 1) & 1, k & 1
            px, py, pc = (1 - x if bx else x), (1 - y if by else y), (1 - c if bc else c)
            slot = 4 * px + 2 * py + pc if to_me else me
            return pltpu.make_async_remote_copy(src_ref=v_ref, dst_ref=slots.at[slot], send_sem=send_sems.at[k - 1],
                                                recv_sem=recv_sems.at[k - 1], device_id=(px, py, pc), device_id_type=MESH_ID)

        for k in range(1, N_DEV):
            rcopy(k, False).start()
        for k in range(1, N_DEV):
            rcopy(k, True).wait_recv()
        for k in range(1, N_DEV):
            rcopy(k, False).wait_send()
        tot = slots[0]
        for d in range(1, N_DEV):
            tot = tot + slots[d]
        out_ref[...] = tot

    return pl.pallas_call(
        body, name=name,
        in_specs=[VMEM_SPEC], out_specs=VMEM_SPEC,
        out_shape=jax.ShapeDtypeStruct((rows, LANES), F32),
        scratch_shapes=[pltpu.VMEM((N_DEV, rows, LANES), F32),
                        pltpu.SemaphoreType.DMA((N_DEV - 1,)), pltpu.SemaphoreType.DMA((N_DEV - 1,))],
    )(vec)


def _adamw_call(w, g, m, v, name):
    r, c = w.shape
    rb = r if r <= 256 else (256 if r % 256 == 0 else 352)
    assert r % rb == 0

    def body(w_ref, g_ref, m_ref, v_ref, d_ref, nm_ref, nv_ref):
        gv = g_ref[...]
        nm = ADAM_B1 * m_ref[...] + (1.0 - ADAM_B1) * gv
        nv = ADAM_B2 * v_ref[...] + (1.0 - ADAM_B2) * jnp.square(gv)
        m_hat = nm / (1.0 - ADAM_B1 ** ADAM_STEP)
        v_hat = nv / (1.0 - ADAM_B2 ** ADAM_STEP)
        d_ref[...] = -ADAM_LR * (m_hat / (jnp.sqrt(v_hat) + ADAM_EPS) + ADAM_WD * w_ref[...])
        nm_ref[...] = nm
        nv_ref[...] = nv

    spec = pl.BlockSpec((rb, c), lambda i: (i, 0))
    sd = jax.ShapeDtypeStruct((r, c), F32)
    return pl.pallas_call(
        body, name=name, grid=(r // rb,),
        in_specs=[spec] * 4, out_specs=[spec] * 3, out_shape=[sd, sd, sd],
        compiler_params=_cp("parallel"),
    )(w, g, m, v)


SMALL = (("attn_norm_w", D_MODEL), ("ret_gn_w", RET_W), ("mla_q_norm_w", Q_RANK), ("mla_kv_norm_w", KV_RANK),
         ("ffn_norm_w", D_MODEL), ("conv_b", F2), ("final_norm_w", D_MODEL))
WEIGHT_ORDER = ("attn_norm_w", "w_in", "ret_gn_w", "mla_q_norm_w", "w_uq", "mla_kv_norm_w", "w_ukv", "w_out",
                "ffn_norm_w", "w_up", "conv_w", "conv_b", "w_down", "final_norm_w")


def _pad_rows(flat, rows):
    return jnp.concatenate([flat, jnp.zeros((rows * LANES - flat.shape[0],), flat.dtype)]).reshape(rows, LANES)


def kernel(x, positions, attn_norm_w, w_in, ret_gn_w, mla_q_norm_w, w_uq, mla_kv_norm_w, w_ukv, w_out, ffn_norm_w, w_up, conv_w, conv_b, w_down, final_norm_w, loss_target, m_attn_norm_w, m_w_in, m_ret_gn_w, m_mla_q_norm_w, m_w_uq, m_mla_kv_norm_w, m_w_ukv, m_w_out, m_ffn_norm_w, m_w_up, m_conv_w, m_conv_b, m_w_down, m_final_norm_w, v_attn_norm_w, v_w_in, v_ret_gn_w, v_mla_q_norm_w, v_w_uq, v_mla_kv_norm_w, v_w_ukv, v_w_out, v_ffn_norm_w, v_w_up, v_conv_w, v_conv_b, v_w_down, v_final_norm_w):
    args = dict(locals())
    cx, cy, cc = _mesh_pos()
    sm = 2 * cx + cy

    c_arr, sm_arr = cc.reshape(1).astype(jnp.int32), sm.reshape(1).astype(jnp.int32)
    sds = jax.ShapeDtypeStruct

    def my_shards(group):
        return [args[n][0].astype(BF16).reshape(2, r // 2, c) for n, (r, c), _ in group]

    def full_weights(gathered, group):
        full = {}
        for (n, (r, c), axis), got in zip(group, gathered):
            piece = got.reshape(4, r, c)
            full[n] = piece if n in ("w_up", "w_in") else (piece.transpose(1, 0, 2).reshape(r, 4 * c) if axis == 1 else piece.reshape(4 * r, c))
        return full

    def by_owner(gw, group):
        out = []
        for n, (r, c), axis in group:
            g = gw[n]
            if axis == 1 and g.ndim == 2:
                g = g.reshape(r, 4, c).transpose(1, 0, 2)
            out.append(g.reshape(4, 2, r // 2, c))
        return out

    def sibling_shapes(gs):
        return [sds((4,) + g.shape[2:], F32) for g in gs]

    def chip_sums(gs, bufs, group):
        res = [_sum_sibling_call(g, b, c_arr, "grads_sum_sibling_" + n) for g, b, (n, _, _) in zip(gs, bufs, group)]
        return [p for p, _ in res], [pb for _, pb in res]

    def chips_shapes(pbs):
        return [sds((3,) + pb.shape[1:], BF16) for pb in pbs]

    def totals(ps, lands, group, tag):
        fins = [_sum_chips_call(p, l, sm_arr, "grads_sum_chips_" + n) for p, l, (n, _, _) in zip(ps, lands, group)]
        sibs = _exchange_call("grads_rs_share_" + tag, _share_copies, fins, [sds(f.shape, F32) for f in fins], len(fins))
        return {n: (f, s) for (n, _, _), f, s in zip(group, fins, sibs)}

    class StepExchanges(_Exchanges):
        def __init__(self, order):
            shards = my_shards(GROUP_B)
            self.gather = _exchange_start_call("weights_gather_start_b", _direct_gather_copies, shards,
                                               [sds((4,) + s.shape, BF16) for s in shards], 13 * len(shards), order)
            self.red = None

        def token(self):
            return self.gather[4][0:1, 0:1]

        def mlp_weights(self, after):
            return full_weights(_exchange_wait_call("weights_gather_wait_b", _direct_gather_copies, self.gather, after)[1], GROUP_B)

        def mlp_grads(self, gw):
            gs = by_owner(gw, GROUP_B)
            self.step1 = _exchange_start_call("grads_rs_sibling_start_b", _sibling_copies, gs, sibling_shapes(gs), 4 * len(gs))
            return self.step1[4]

        def behind_out_bwd(self, after):
            gs, bufs = _exchange_wait_call("grads_rs_sibling_wait_b", _sibling_copies, self.step1, after)
            self.ps, pbs = chip_sums(gs, bufs, GROUP_B)
            self.step2 = _exchange_start_call("grads_rs_chips_start_b", _chips_copies, pbs, chips_shapes(pbs), 3 * len(pbs))
            return self.step2[4]

        def behind_attention(self, after):
            _, lands = _exchange_wait_call("grads_rs_chips_wait_b", _chips_copies, self.step2, after)
            self.red = totals(self.ps, lands, GROUP_B, "b")

    gathered = _gather_list_call(my_shards(GROUP_A) + [conv_w[0].reshape(2, 1, 3 * F2 // 8)], "a")
    full = full_weights(gathered[:-1], GROUP_A)
    ex = StepExchanges(gathered[-1])
    full["conv_w"] = gathered[-1].reshape(4, 3, F2 // 4).transpose(1, 0, 2).reshape(3, F2)
    small = {n: args[n].reshape(1, d) for n, d in SMALL}
    small["attn_norm_w"] = small["attn_norm_w"] + ex.token()

    loss, gx, gw, gs = _local_step(x[0], positions[0], loss_target[0], full, small, ex)

    ga = by_owner(gw, GROUP_A)
    bufs = _exchange_call("grads_rs_sibling_a", _sibling_copies, ga, sibling_shapes(ga), 4 * len(ga))
    ps, pbs = chip_sums(ga, bufs, GROUP_A)
    step2 = _exchange_start_call("grads_rs_chips_start_a", _chips_copies, pbs, chips_shapes(pbs), 3 * len(pbs))
    early, last = {}, step2[4]
    for n, _, _ in GROUP_B:
        wmv = [args[k + n][0] for k in ("", "m_", "v_")]
        early[n] = _adamw_halves_call(wmv[0], *ex.red[n], c_arr, wmv[1], wmv[2], "adamw_" + n, order=last)
        last = early[n][1]
    _, lands = _exchange_wait_call("grads_rs_chips_wait_a", _chips_copies, step2, last)
    halves = totals(ps, lands, GROUP_A, "a")

    vec = jnp.concatenate([gs[n].reshape(-1) for n, _ in SMALL] + [gw["conv_w"].reshape(-1), loss.reshape(-1)])
    tot = _all_reduce8_call(_pad_rows(vec, 216), "small_all_reduce").reshape(-1)
    red, off = {}, 0
    for n, d in SMALL:
        red[n] = tot[off:off + d].reshape(1, d)
        off += d
    red["conv_w"] = lax.dynamic_slice(tot[off:off + 3 * F2].reshape(3, F2), (0, sm * (F2 // 4)), (3, F2 // 4))
    loss_tot = tot[off + 3 * F2]

    grads, deltas, new_m, new_v = [], [], [], []
    for n in WEIGHT_ORDER:
        shape = args[n].shape
        two_d = (1, shape[0]) if len(shape) == 1 else shape[-2:]
        wmv = [args[k + n].reshape(two_d) for k in ("", "m_", "v_")]
        if n in early:
            g, d, nm, nv = early[n]
        elif n in halves and two_d[1] % LANES:
            tr = lambda a: a.T
            g, d, nm, nv = map(tr, _adamw_halves_call(tr(wmv[0]), *map(tr, halves[n]), c_arr, tr(wmv[1]), tr(wmv[2]),
                                                      "adamw_" + n, transposed=True))
        elif n in halves:
            g, d, nm, nv = _adamw_halves_call(wmv[0], *halves[n], c_arr, wmv[1], wmv[2], "adamw_" + n)
        else:
            g = red[n].reshape(two_d)
            d, nm, nv = _adamw_call(wmv[0], g, wmv[1], wmv[2], "adamw_" + n)
        grads.append(g.reshape(shape))
        deltas.append(d.reshape(shape))
        new_m.append(nm.reshape(shape))
        new_v.append(nv.reshape(shape))
    return (loss_tot, gx[None], *grads, *deltas, *new_m, *new_v)
```

```python
import math

import numpy as np
import jax
import jax.numpy as jnp
from jax import lax
from jax.experimental import pallas as pl
from jax.experimental.pallas import tpu as pltpu

F32 = jnp.float32
BF16 = jnp.bfloat16

D_MODEL = 1024
N_HEADS = 8
HEAD = 64
RET_W = N_HEADS * HEAD
MLA_W = N_HEADS * HEAD
ROPE = 32
Q_RANK = 256
KV_RANK = 128
D_FF = 2816
F2 = 2 * D_FF
IN_W = 4 * RET_W + Q_RANK + KV_RANK + ROPE
IN_EXT = 4 * RET_W + Q_RANK + KV_RANK + 128
KPE_LO = 64
ROPE_BASE = 10000.0
EPS = 1e-6
RET_CHUNK = 256
SM_SCALE = (HEAD + ROPE) ** -0.5
LOG2E = math.log2(math.e)
LN2 = math.log(2.0)
NEG = -1e30
LANES = 128
VMEM_LIMIT = 56 * 1024 * 1024

ADAM_LR = 0.001
ADAM_B1 = 0.9
ADAM_B2 = 0.999
ADAM_EPS = 1e-08
ADAM_WD = 0.01
ADAM_STEP = 10


VMEM_LIMIT_MLP = 60 * 1024 * 1024


def _cp(*sem, vmem=VMEM_LIMIT):
    return pltpu.CompilerParams(dimension_semantics=sem, vmem_limit_bytes=vmem)


def _full(shape):
    n = len(shape)
    return pl.BlockSpec(tuple(shape), lambda *_: (0,) * n)


def _row(ts, c):
    return pl.BlockSpec((ts, c), lambda i: (i, 0))


def _hrow(h, ts, c):
    return pl.BlockSpec((h, ts, c), lambda i: (0, i, 0))


def _dot(a, b):
    return jnp.dot(a, b, preferred_element_type=F32)


def _dot_nt(a, b):
    return lax.dot_general(a, b, (((1,), (1,)), ((), ())), preferred_element_type=F32)


def _dot_tn(a, b):
    return lax.dot_general(a, b, (((0,), (0,)), ((), ())), preferred_element_type=F32)


def _dot_hi(a, b):
    hi = a.astype(BF16)
    lo = (a - hi.astype(F32)).astype(BF16)
    bb = b.astype(BF16)
    return _dot(hi, bb) + _dot(lo, bb)


def _rot_half(x, half):
    w = x.shape[-1]
    lane = lax.broadcasted_iota(jnp.int32, x.shape, x.ndim - 1)
    first = (lane % (2 * half)) < half
    return jnp.where(first, -pltpu.roll(x, w - half, x.ndim - 1), pltpu.roll(x, half, x.ndim - 1))


def _rope(x, cos, sin, half):
    return x * cos + _rot_half(x, half) * sin


def _unrope(dy, cos, sin, half):
    return dy * cos - _rot_half(dy, half) * sin


def _sigmoid(g):
    return 0.5 * jnp.tanh(0.5 * g) + 0.5


def _silu(g):
    return g * _sigmoid(g)


def _rstd(x):
    return lax.rsqrt(jnp.mean(x * x, axis=-1, keepdims=True) + EPS)


def _rope_tables(positions):
    s = positions.shape[0]
    hr, hm = HEAD // 2, ROPE // 2
    pos = positions.astype(F32)[None, :]
    inv_r = ROPE_BASE ** (-jnp.arange(0, HEAD, 2, dtype=F32) / HEAD)
    inv_m = ROPE_BASE ** (-jnp.arange(0, ROPE, 2, dtype=F32) / ROPE)
    ang = jnp.concatenate([inv_r, inv_m])[:, None] * pos
    packed = jnp.concatenate([jnp.cos(ang), jnp.sin(ang), jnp.zeros((LANES - 2 * (hr + hm), s), F32)], 0)
    tx = min(s, 1024)

    def spread(t, lane, pieces, fill):
        out = jnp.full(t.shape, fill, F32)
        for lo, src, width in pieces:
            moved = t if lo == src else pltpu.roll(t, (lo - src) % LANES, 1)
            out = jnp.where((lane >= lo) & (lane < lo + width), moved, out)
        return out

    def body(p_ref, cr_ref, sr_ref, cm_ref, sm_ref):
        t = p_ref[...].T
        lane = lax.broadcasted_iota(jnp.int32, t.shape, 1)
        cr_ref[...] = spread(t, lane, [(j * hr, 0, hr) for j in range(LANES // hr)], 0.0)
        sr_ref[...] = spread(t, lane, [(j * hr, hr + hm, hr) for j in range(LANES // hr)], 0.0)
        cm_ref[...] = spread(t, lane, [(KPE_LO, hr, hm), (KPE_LO + hm, hr, hm)], 1.0)
        sm_ref[...] = spread(t, lane, [(KPE_LO, 2 * hr + hm, hm), (KPE_LO + hm, 2 * hr + hm, hm)], 0.0)

    tab = jax.ShapeDtypeStruct((s, LANES), F32)
    return pl.pallas_call(
        body, name="rope_tables", grid=(s // tx,),
        in_specs=[pl.BlockSpec((LANES, tx), lambda i: (0, i))],
        out_specs=[_row(tx, LANES)] * 4, out_shape=[tab] * 4,
        compiler_params=_cp("parallel"),
    )(packed)


def _ret_consts():
    c = RET_CHUNK
    lg = np.log1p(-np.power(2.0, -5.0 - np.arange(N_HEADS, dtype=np.float64)))
    idx = np.arange(c, dtype=np.float64)
    diff = idx[:, None] - idx[None, :]
    lane_head = np.arange(LANES) // HEAD
    dmask = np.zeros((4, 2, c, c))
    zeta = np.zeros((4, c, LANES))
    xi = np.zeros((4, c, LANES))
    cd = np.zeros((4, LANES, LANES))
    bd = (lane_head[:, None] == lane_head[None, :]).astype(np.float64)
    for j in range(4):
        for hh in range(2):
            dmask[j, hh] = np.where(diff >= 0, np.exp(lg[2 * j + hh] * np.maximum(diff, 0.0)), 0.0)
        lgl = lg[2 * j + lane_head]
        zeta[j] = np.exp(lgl[None, :] * (c - 1.0 - idx[:, None]))
        xi[j] = np.exp(lgl[None, :] * (idx[:, None] + 1.0))
        cd[j] = np.exp(lgl * c)[:, None] * bd
    f = lambda a: jnp.asarray(a, F32)
    side = lambda d: np.concatenate([d[:, 0], d[:, 1]], axis=-1)
    return dict(dmask=f(side(dmask)), dmask_t=f(side(np.swapaxes(dmask, 2, 3))), zeta=f(zeta), xi=f(xi), cd=f(cd), bd=f(bd))


def _f1_call(x, anw, win, qnw, kvnw, wq, wk, wv, cos_r, sin_r, cos_m, sin_m, ts):
    s = x.shape[0]

    def body(x_ref, anw_ref, w_ref, qnw_ref, kvnw_ref, wq_ref, wk_ref, wv_ref, cr_ref, sr_ref, cm_ref, sm_ref,
             q_ref, k_ref, v_ref, g_ref, cq_ref, ckv_ref, mq_ref, mk_ref, mv_ref, r_ref):
        xv = x_ref[...]
        r = _rstd(xv)
        r_ref[...] = r
        h = (xv * r * anw_ref[...]).astype(BF16)
        cr, sr = cr_ref[...], sr_ref[...]
        qk = _dot(h, w_ref[:, 0:2 * RET_W])
        for j in range(4):
            sl = slice(j * LANES, (j + 1) * LANES)
            q_ref[:, sl] = _rope(qk[:, sl], cr, sr, HEAD // 2).astype(BF16)
            kk = qk[:, RET_W + j * LANES:RET_W + (j + 1) * LANES]
            k_ref[:, sl] = (_rope(kk, cr, sr, HEAD // 2) * (HEAD ** -0.5)).astype(BF16)
        v_ref[...] = _dot(h, w_ref[:, 2 * RET_W:3 * RET_W]).astype(BF16)
        g_ref[...] = _dot(h, w_ref[:, 3 * RET_W:4 * RET_W])
        o = 4 * RET_W
        cqv = _dot(h, w_ref[:, o:o + Q_RANK])
        ckvv = _dot(h, w_ref[:, o + Q_RANK:o + Q_RANK + KV_RANK])
        cq_ref[...] = cqv
        ckv_ref[...] = ckvv
        cm, sm = cm_ref[...], sm_ref[...]
        kp = _rope(_dot(h, w_ref[:, o + Q_RANK + KV_RANK:IN_EXT]), cm, sm, ROPE // 2)
        kp = _lane_pair((ts, LANES), QK_AUX, -1.0, -1.0, kp)
        cqn = (cqv * _rstd(cqv) * qnw_ref[...]).astype(BF16)
        ckvn = (ckvv * _rstd(ckvv) * kvnw_ref[...]).astype(BF16)
        for hd in range(N_HEADS):
            qh = _rope(_dot(cqn, wq_ref[hd]), cm, sm, ROPE // 2)
            mq_ref[hd] = (qh * (SM_SCALE * LOG2E)).astype(BF16)
            mk_ref[hd] = (_dot(ckvn, wk_ref[hd]) + kp).astype(BF16)
            mv_ref[hd] = _lane_pair((ts, LANES), V_AUX, 1.0, 1.0, _dot(ckvn, wv_ref[hd])).astype(BF16)

    sd = jax.ShapeDtypeStruct
    hm = sd((N_HEADS, s, LANES), BF16)
    return pl.pallas_call(
        body, name="f1_in_proj", grid=(s // ts,),
        in_specs=[_row(ts, D_MODEL), _full((1, D_MODEL)), _full((D_MODEL, IN_EXT)), _full((1, Q_RANK)), _full((1, KV_RANK)),
                  _full((N_HEADS, Q_RANK, LANES)), _full((N_HEADS, KV_RANK, LANES)), _full((N_HEADS, KV_RANK, LANES)),
                  _row(ts, LANES), _row(ts, LANES), _row(ts, LANES), _row(ts, LANES)],
        out_specs=[_row(ts, RET_W), _row(ts, RET_W), _row(ts, RET_W), _row(ts, RET_W),
                   _row(ts, Q_RANK), _row(ts, KV_RANK)] + [_hrow(N_HEADS, ts, LANES)] * 3 + [_row(ts, 1)],
        out_shape=[sd((s, RET_W), BF16), sd((s, RET_W), BF16), sd((s, RET_W), BF16), sd((s, RET_W), F32),
                   sd((s, Q_RANK), F32), sd((s, KV_RANK), F32), hm, hm, hm, sd((s, 1), F32)],
        compiler_params=_cp("parallel"),
    )(x, anw, win, qnw, kvnw, wq, wk, wv, cos_r, sin_r, cos_m, sin_m)


def _stack_heads(a):
    lo = lax.broadcasted_iota(jnp.int32, a.shape, 1) < HEAD
    zero = jnp.zeros_like(a)
    return jnp.concatenate([jnp.where(lo, a, zero), jnp.where(lo, zero, a)], axis=0)


def _pair_product(a, b2, decay2, w2):
    return _dot((_dot_nt(a, b2) * decay2).astype(BF16), w2)


RET_SLABS = 2


def _ret_specs(tr, tile_of):
    c, ns = RET_CHUNK, RET_SLABS
    return dict(
        slab=pl.BlockSpec((tr, ns * LANES), lambda j, i: (tile_of(i), j)),
        tab=pl.BlockSpec((tr, LANES), lambda j, i: (tile_of(i), 0)),
        vec=pl.BlockSpec((1, ns * LANES), lambda j, i: (0, j)),
        dmask=pl.BlockSpec((ns, c, 2 * c), lambda j, i: (j, 0, 0)),
        rows=pl.BlockSpec((ns, c, LANES), lambda j, i: (j, 0, 0)),
        state=pl.BlockSpec((ns, LANES, LANES), lambda j, i: (j, 0, 0)),
        bd=pl.BlockSpec((LANES, LANES), lambda j, i: (0, 0)))


def _ret_states(a_ref, b_ref, scale_ref, cd_ref, bd, st_ref, chunks, lanes, reverse):
    nc = len(chunks)
    contrib = [[_dot_tn((a_ref[rows, ln].astype(F32) * scale_ref[sl]).astype(BF16), b_ref[rows, ln]) * bd for rows in chunks]
               for sl, ln in enumerate(lanes)]
    states = []
    for sl in range(len(lanes)):
        st, seen = st_ref[sl], [None] * nc
        for ci in (reversed(range(nc)) if reverse else range(nc)):
            seen[ci] = st.astype(BF16)
            st = st * cd_ref[sl] + contrib[sl][ci]
        st_ref[sl] = st
        states.append(seen)
    return states


def _ret_fwd_call(q, k, v, g, gnw, rc, tr, order=None):
    s = q.shape[0]
    c = RET_CHUNK
    nc = tr // c
    ns = RET_SLABS

    def body(q_ref, k_ref, v_ref, g_ref, gnw_ref, dm_ref, zeta_ref, xi_ref, cd_ref, bd_ref, o_ref, y_ref, st_ref):
        @pl.when(pl.program_id(1) == 0)
        def _():
            st_ref[...] = jnp.zeros_like(st_ref)

        bd = bd_ref[...]
        chunks = [slice(ci * c, (ci + 1) * c) for ci in range(nc)]
        lanes = [slice(sl * LANES, (sl + 1) * LANES) for sl in range(ns)]
        states = _ret_states(k_ref, v_ref, zeta_ref, cd_ref, bd, st_ref, chunks, lanes, False)
        for ci, rows in enumerate(chunks):
            for sl, ln in enumerate(lanes):
                qc = q_ref[rows, ln]
                o_ref[rows, ln] = (_dot(qc, states[sl][ci]) * xi_ref[sl]
                                   + _pair_product(qc, _stack_heads(k_ref[rows, ln]), dm_ref[sl], _stack_heads(v_ref[rows, ln])))
        avg = bd * (1.0 / HEAD)
        for ln in lanes:
            o = o_ref[:, ln]
            ctr = o - _dot_hi(o, avg)
            var = _dot_hi(ctr * ctr, avg)
            y_ref[:, ln] = (_silu(g_ref[:, ln]) * (ctr * lax.rsqrt(var + EPS) * gnw_ref[:, ln])).astype(BF16)

    specs = _ret_specs(tr, lambda i: i)
    sd = jax.ShapeDtypeStruct
    body, first_specs, first = _ordered_after(body, order)
    return pl.pallas_call(
        body, name="ret_fwd", grid=(4 // ns, s // tr),
        in_specs=first_specs + [specs["slab"]] * 4
        + [specs["vec"], specs["dmask"], specs["rows"], specs["rows"], specs["state"], specs["bd"]],
        out_specs=[specs["slab"]] * 2,
        out_shape=[sd((s, RET_W), F32), sd((s, RET_W), BF16)],
        scratch_shapes=[pltpu.VMEM((ns, LANES, LANES), F32)],
        compiler_params=_cp("parallel", "arbitrary"),
    )(*first, q, k, v, g, gnw, rc["dmask"], rc["zeta"], rc["xi"], rc["cd"], rc["bd"])


QK_AUX = HEAD + ROPE
V_AUX = HEAD


def _lane_pair(shape, lo, a, b, rest):
    lane = lax.broadcasted_iota(jnp.int32, shape, len(shape) - 1)
    return jnp.where(lane == lo, a, jnp.where(lane == lo + 1, b, rest))


def _hi_lo(v):
    hi = v.astype(BF16).astype(F32)
    return hi, v - hi


def _flash_fwd_call(q, k, v, tb):
    s = q.shape[1]
    nb = s // tb
    pairs = [(a, b) for a in range(nb) for b in range(a + 1)]
    qi_of, ki_of = (jnp.asarray(np.array(col, np.int32)) for col in zip(*pairs))

    def body(qi_ref, ki_ref, q_ref, k_ref, v_ref, o_ref, qb_ref, m_ref, acc_ref):
        qi, ki = qi_ref[pl.program_id(0)], ki_ref[pl.program_id(0)]

        @pl.when(ki == 0)
        def _():
            m_ref[...] = jnp.full_like(m_ref, NEG)
            acc_ref[...] = jnp.zeros_like(acc_ref)

        def step(masked):
            if masked:
                keep = lax.broadcasted_iota(jnp.int32, (tb, tb), 1) <= lax.broadcasted_iota(jnp.int32, (tb, tb), 0)
            def finish(h, pe, alpha):
                acc_ref[h] = acc_ref[h] * alpha + _dot(pe, v_ref[h])

            nxt, pending = _dot_nt(q_ref[0], k_ref[0]), None
            for h in range(N_HEADS):
                sc = nxt
                if h + 1 < N_HEADS:
                    nxt = _dot_nt(q_ref[h + 1], k_ref[h + 1])
                if masked:
                    sc = jnp.where(keep, sc, NEG)
                m_prev = m_ref[h]
                m_new = jnp.maximum(m_prev, jnp.max(sc, axis=1, keepdims=True))
                pe = jnp.exp2(sc - jnp.tile(m_new, (1, tb // LANES))).astype(BF16)
                m_ref[h] = m_new
                if pending is not None:
                    finish(*pending)
                pending = (h, pe, jnp.exp2(m_prev - m_new))
            finish(*pending)

        @pl.when(ki < qi)
        def _():
            step(False)

        @pl.when(ki == qi)
        def _():
            step(True)
            lane = lax.broadcasted_iota(jnp.int32, (tb, LANES), 1)
            for p in range(N_HEADS // 2):
                outs = []
                for h in (2 * p, 2 * p + 1):
                    acc = acc_ref[h]
                    l = acc[:, V_AUX:V_AUX + 1]
                    outs.append(acc * (1.0 / l))
                    hi, lo = _hi_lo(m_ref[h][:, 0:1] + jnp.log(l) * LOG2E)
                    qb_ref[h] = _lane_pair((tb, LANES), QK_AUX, hi, lo, q_ref[h].astype(F32)).astype(BF16)
                o_ref[:, p * LANES:(p + 1) * LANES] = jnp.where(lane < HEAD, outs[0], pltpu.roll(outs[1], HEAD, 1)).astype(BF16)

    sd = jax.ShapeDtypeStruct
    qspec = pl.BlockSpec((N_HEADS, tb, LANES), lambda p, qi_ref, ki_ref: (0, qi_ref[p], 0))
    kspec = pl.BlockSpec((N_HEADS, tb, LANES), lambda p, qi_ref, ki_ref: (0, ki_ref[p], 0))
    return pl.pallas_call(
        body, name="mla_flash_fwd",
        grid_spec=pltpu.PrefetchScalarGridSpec(
            num_scalar_prefetch=2, grid=(len(pairs),),
            in_specs=[qspec, kspec, kspec],
            out_specs=[pl.BlockSpec((tb, MLA_W), lambda p, qi_ref, ki_ref: (qi_ref[p], 0)), qspec],
            scratch_shapes=[pltpu.VMEM((N_HEADS, tb, LANES), F32), pltpu.VMEM((N_HEADS, tb, LANES), F32)]),
        out_shape=[sd((s, MLA_W), BF16), sd((N_HEADS, s, LANES), BF16)],
        compiler_params=_cp("arbitrary"),
    )(qi_of, ki_of, q, k, v)


def _out_proj_call(x, yret, ymla, wout, ts):
    s = x.shape[0]

    def body(x_ref, yr_ref, ym_ref, w_ref, x1_ref, r_ref):
        x1 = x_ref[...] + _dot(yr_ref[...], w_ref[0:RET_W, :]) + _dot(ym_ref[...], w_ref[RET_W:, :])
        x1_ref[...] = x1
        r_ref[...] = _rstd(x1)

    sd = jax.ShapeDtypeStruct
    return pl.pallas_call(
        body, name="out_proj", grid=(s // ts,),
        in_specs=[_row(ts, D_MODEL), _row(ts, RET_W), _row(ts, MLA_W), _full((D_MODEL, D_MODEL))],
        out_specs=[_row(ts, D_MODEL), _row(ts, 1)],
        out_shape=[sd((s, D_MODEL), F32), sd((s, 1), F32)],
        compiler_params=_cp("parallel"),
    )(x, yret, ymla, wout)


W_UP_SHARD = F2 // 4


def _ffn_fwd_call(x1, r2, fnw, wup4, cw, cb, wdown, tgt, fw, ts):
    s = x1.shape[0]
    wsh = W_UP_SHARD

    def body(x_ref, r_ref, fnw_ref, wup_ref, cw_ref, cb_ref, wd_ref, t_ref, fw_ref,
             u_ref, uc_ref, dx2_ref, loss_ref, gfw_ref, carry_ref):
        _zero_first(pl.program_id(0) == 0, carry_ref, loss_ref, gfw_ref)
        xv = x_ref[...]
        h = (xv * r_ref[...] * fnw_ref[...]).astype(BF16)
        conv = []
        for j in range(4):
            cols = slice(j * wsh, (j + 1) * wsh)
            ub = _dot(h, wup_ref[j]).astype(BF16)
            u_ref[:, cols] = ub
            u = ub.astype(F32)
            u1, u2 = _shifted(u, carry_ref[:, cols])
            w = cw_ref[:, cols]
            cb16 = (cb_ref[:, cols] + w[0:1, :] * u2 + w[1:2, :] * u1 + w[2:3, :] * u).astype(BF16)
            uc_ref[:, cols] = cb16
            conv.append(cb16.astype(F32))
            carry_ref[:, cols] = u[ts - 8:, :]
        acc = xv
        for j in range(2):
            a = (_silu(conv[j]) * conv[j + 2]).astype(BF16)
            acc = acc + _dot(a, wd_ref[j * wsh:(j + 1) * wsh, :])
        r = _rstd(acc)
        xh = acc * r
        fwv = fw_ref[...]
        e = xh * fwv - t_ref[...]
        loss_ref[...] += (0.5 / D_MODEL) * _colsum(jnp.sum(e * e, axis=1, keepdims=True))
        dy = e * (1.0 / D_MODEL)
        gfw_ref[...] += _colsum(dy * xh)
        dx2_ref[...] = _norm_bwd(dy, xh, r, fwv)

    sd = jax.ShapeDtypeStruct
    once = lambda shape: pl.BlockSpec(shape, lambda i: (0,) * len(shape), pipeline_mode=pl.Buffered(1))
    return pl.pallas_call(
        body, name="ffn_fwd_loss", grid=(s // ts,),
        in_specs=[_row(ts, D_MODEL), _row(ts, 1), once((1, D_MODEL)), once((4, D_MODEL, wsh)),
                  once((3, F2)), once((1, F2)), once((D_FF, D_MODEL)), _row(ts, D_MODEL), once((1, D_MODEL))],
        out_specs=[_row(ts, F2), _row(ts, F2), _row(ts, D_MODEL), _full((1, 1)), _full((1, D_MODEL))],
        out_shape=[sd((s, F2), BF16), sd((s, F2), BF16), sd((s, D_MODEL), F32), sd((1, 1), F32), sd((1, D_MODEL), F32)],
        scratch_shapes=[pltpu.VMEM((8, F2), F32)],
        compiler_params=_cp("arbitrary", vmem=VMEM_LIMIT_MLP),
    )(x1, r2, fnw, wup4, cw, cb, wdown, tgt, fw)


def _shifted(u, hal):
    row = lax.broadcasted_iota(jnp.int32, hal.shape, 0)
    r1, r2 = pltpu.roll(u, 1, 0), pltpu.roll(u, 2, 0)
    top1 = jnp.where(row == 0, hal[7:8, :], r1[0:8, :])
    top2 = jnp.where(row == 0, hal[6:7, :], jnp.where(row == 1, hal[7:8, :], r2[0:8, :]))
    return jnp.concatenate([top1, r1[8:, :]], axis=0), jnp.concatenate([top2, r2[8:, :]], axis=0)


def _win_ext_call(win):
    blocks = win[None] if win.ndim == 2 else win
    nb, r, wb = blocks.shape
    tr = min(r, 256)

    def body(b_ref, o_ref):
        left, right, at = [], [], IN_W - ROPE
        for j in range(nb):
            blk = b_ref[j]
            cut = min(max(at - j * wb, 0), wb)
            left += [blk[:, :cut]] if cut else []
            right += [blk[:, cut:]] if cut < wb else []
        pad = lambda n: jnp.zeros((tr, n), o_ref.dtype)
        o_ref[...] = jnp.concatenate(left + [pad(KPE_LO)] + right + [pad(LANES - KPE_LO - ROPE)], -1)

    return pl.pallas_call(
        body, name="w_in_layout", grid=(r // tr,),
        in_specs=[pl.BlockSpec((nb, tr, wb), lambda i: (0, i, 0))], out_specs=_row(tr, IN_EXT),
        out_shape=jax.ShapeDtypeStruct((r, IN_EXT), win.dtype),
        compiler_params=_cp("parallel"),
    )(blocks)


def _win_grad_blocks_call(g_ext, nb):
    r = g_ext.shape[0]
    wb = IN_W // nb
    tr = min(r, 256)
    lo = IN_W - ROPE

    def body(g_ref, o_ref):
        g = g_ref[...]
        for j in range(nb):
            a, b = j * wb, (j + 1) * wb
            parts = ([g[:, a:min(b, lo)]] if a < lo else []) + ([g[:, max(a, lo) + KPE_LO:b + KPE_LO]] if b > lo else [])
            o_ref[j] = jnp.concatenate(parts, -1)

    return pl.pallas_call(
        body, name="w_in_grad_blocks", grid=(r // tr,),
        in_specs=[_row(tr, IN_EXT)], out_specs=pl.BlockSpec((nb, tr, wb), lambda i: (0, i, 0)),
        out_shape=jax.ShapeDtypeStruct((nb, r, wb), g_ext.dtype),
        compiler_params=_cp("parallel"),
    )(g_ext)


def _prep_weights(w):
    win_ext = _win_ext_call(w["w_in"])
    wuq = w["w_uq"].reshape(Q_RANK, N_HEADS, HEAD + ROPE)
    wq = jnp.concatenate([wuq, jnp.zeros((Q_RANK, N_HEADS, LANES - HEAD - ROPE), wuq.dtype)], -1).transpose(1, 0, 2)
    wukv = w["w_ukv"].reshape(KV_RANK, N_HEADS, 2 * HEAD)
    zk = jnp.zeros((KV_RANK, N_HEADS, HEAD), wukv.dtype)
    wk = jnp.concatenate([wukv[:, :, :HEAD], zk], -1).transpose(1, 0, 2)
    wv = jnp.concatenate([wukv[:, :, HEAD:], zk], -1).transpose(1, 0, 2)
    c = lambda a: a.astype(BF16)
    return dict(win=c(win_ext), wq=c(wq), wk=c(wk), wv=c(wv), wout=c(w["w_out"]))


def _prep_mlp_weights(w):
    wup = w["w_up"]
    if wup.ndim == 2:
        wup = wup.reshape(D_MODEL, 4, W_UP_SHARD).transpose(1, 0, 2)
    return dict(wup=wup.astype(BF16), wdown=w["w_down"].astype(BF16))


def _tiles(s):
    return dict(ts=min(s, 512), tr=min(s, 2048), tbf=min(s, 1024), tb=min(s, 512), t2=min(s, 256),
                tw=min(s, 2048), t1=min(s, 1024))


class _Exchanges:
    def __init__(self, w):
        self.w = w

    def mlp_weights(self, after):
        return self.w

    def mlp_grads(self, gw):
        pass

    def behind_out_bwd(self, after):
        pass

    def behind_attention(self, after):
        pass


def _forward(x, positions, tgt, w, small, ex):
    s = x.shape[0]
    t = _tiles(s)
    pw = _prep_weights(w)
    cos_r, sin_r, cos_m, sin_m = _rope_tables(positions)
    rc = _ret_consts()
    q, k, v, g, cq, ckv, mq, mk, mv, r1 = _f1_call(
        x, small["attn_norm_w"], pw["win"], small["mla_q_norm_w"], small["mla_kv_norm_w"], pw["wq"], pw["wk"], pw["wv"],
        cos_r, sin_r, cos_m, sin_m, t["ts"])
    y_mla, mqb = _flash_fwd_call(mq, mk, mv, t["tbf"])
    o_ret, y_ret = _ret_fwd_call(q, k, v, g, small["ret_gn_w"], rc, t["tr"], order=y_mla)
    x1, r2 = _out_proj_call(x, y_ret, y_mla, pw["wout"], t["ts"])
    pw.update(_prep_mlp_weights(ex.mlp_weights(r2)))
    u, uc, dx2, loss, g_fw = _ffn_fwd_call(x1, r2, small["ffn_norm_w"], pw["wup"], w["conv_w"], small["conv_b"], pw["wdown"],
                                           tgt, small["final_norm_w"], t["ts"])
    return dict(pw=pw, tabs=(cos_r, sin_r, cos_m, sin_m), rc=rc, q=q, k=k, v=v, g=g, cq=cq, ckv=ckv, r1=r1,
                o_ret=o_ret, y_ret=y_ret, mqb=mqb, mk=mk, mv=mv, y_mla=y_mla, x1=x1, r2=r2, u=u, uc=uc,
                dx2=dx2, loss=loss, g_fw=g_fw)


def _norm_bwd(dh, xh, r, nw):
    dxn = dh * nw
    return r * (dxn - xh * jnp.mean(dxn * xh, axis=-1, keepdims=True))


def _ordered_after(body, order):
    if order is None:
        return body, [], []
    return (lambda order_ref, *refs: body(*refs)), [pl.BlockSpec(memory_space=pl.ANY)], [order]


def _zero_first(first, *refs):
    @pl.when(first)
    def _():
        for ref in refs:
            ref[...] = jnp.zeros_like(ref)


def _colsum(v):
    return jnp.sum(v, axis=0, keepdims=True)


def _dsilu(g, sg):
    return sg * (1.0 + g * (1.0 - sg))


def _ffn_bwd_call(dx2, u, uc, cw, wdown, wup4, x1, r2, fnw, ts):
    s = dx2.shape[0]
    nt = s // ts
    wsh = W_UP_SHARD
    rev = lambda i: nt - 1 - i

    def body(dx2_ref, u_ref, uc_ref, cw_ref, wd_ref, wup_ref, x_ref, r_ref, fnw_ref,
             du_ref, dx1_ref, dcw_ref, dcb_ref, dfnw_ref, dwd_hbm, carry_ref, dwd_ref, sem):
        i = pl.program_id(0)
        _zero_first(i == 0, carry_ref, dwd_ref, dcw_ref, dcb_ref, dfnw_ref)
        dxb = dx2_ref[...].astype(BF16)
        dh = jnp.zeros((ts, D_MODEL), F32)
        for j in range(2):
            gcols = slice(j * wsh, (j + 1) * wsh)
            vcols = slice(D_FF + j * wsh, D_FF + (j + 1) * wsh)
            gate, val = uc_ref[:, gcols].astype(F32), uc_ref[:, vcols].astype(F32)
            da = _dot_nt(dxb, wd_ref[gcols, :])
            sg = _sigmoid(gate)
            sl = gate * sg
            dwd_ref[gcols, :] += _dot_tn((sl * val).astype(BF16), dxb)
            for d, cols, shard in ((da * val * _dsilu(gate, sg), gcols, j), (da * sl, vcols, 2 + j)):
                d1, d2 = _shifted_up(d, carry_ref[:, cols])
                uv = u_ref[:, cols].astype(F32)
                for t, dt in enumerate((d2, d1, d)):
                    dcw_ref[t:t + 1, cols] += _colsum(dt * uv)
                dcb_ref[:, cols] += _colsum(d)
                w = cw_ref[:, cols]
                du = (w[2:3, :] * d + w[1:2, :] * d1 + w[0:1, :] * d2).astype(BF16)
                du_ref[:, cols] = du
                dh = dh + _dot_nt(du, wup_ref[shard])
                carry_ref[:, cols] = d[0:8, :]
        r = r_ref[...]
        xh = x_ref[...] * r
        dfnw_ref[...] += _colsum(dh * xh)
        dx1_ref[...] = dx2_ref[...] + _norm_bwd(dh, xh, r, fnw_ref[...])

        @pl.when(i == nt - 1)
        def _():
            cp = pltpu.make_async_copy(dwd_ref, dwd_hbm, sem)
            cp.start()
            cp.wait()

    sd = jax.ShapeDtypeStruct
    row = lambda c: pl.BlockSpec((ts, c), lambda i: (rev(i), 0))
    once = lambda shape: pl.BlockSpec(shape, lambda i: (0,) * len(shape), pipeline_mode=pl.Buffered(1))
    return pl.pallas_call(
        body, name="ffn_bwd", grid=(nt,),
        in_specs=[row(D_MODEL), row(F2), row(F2), once((3, F2)), once((D_FF, D_MODEL)), once((4, D_MODEL, wsh)),
                  row(D_MODEL), row(1), once((1, D_MODEL))],
        out_specs=[row(F2), row(D_MODEL), _full((3, F2)), _full((1, F2)), _full((1, D_MODEL)), pl.BlockSpec(memory_space=pl.ANY)],
        out_shape=[sd((s, F2), BF16), sd((s, D_MODEL), F32), sd((3, F2), F32), sd((1, F2), F32), sd((1, D_MODEL), F32),
                   sd((D_FF, D_MODEL), F32)],
        scratch_shapes=[pltpu.VMEM((8, F2), F32), pltpu.VMEM((D_FF, D_MODEL), F32), pltpu.SemaphoreType.DMA],
        compiler_params=_cp("arbitrary", vmem=VMEM_LIMIT_MLP),
    )(dx2, u, uc, cw, wdown, wup4, x1, r2, fnw)


def _shifted_up(d, hal):
    n = d.shape[0]
    row = lax.broadcasted_iota(jnp.int32, hal.shape, 0)
    r1, r2 = pltpu.roll(d, n - 1, 0), pltpu.roll(d, n - 2, 0)
    end1 = jnp.where(row == 7, hal[0:1, :], r1[n - 8:, :])
    end2 = jnp.where(row == 6, hal[0:1, :], jnp.where(row == 7, hal[1:2, :], r2[n - 8:, :]))
    return jnp.concatenate([r1[:n - 8, :], end1], axis=0), jnp.concatenate([r2[:n - 8, :], end2], axis=0)


def _dw_norm_call(x, r, nw, b, ts, tn, name):
    s, n = b.shape
    k = x.shape[1]

    def body(x_ref, r_ref, nw_ref, b_ref, dw_ref):
        _zero_first(pl.program_id(1) == 0, dw_ref)
        h = (x_ref[...] * r_ref[...] * nw_ref[...]).astype(BF16)
        dw_ref[...] += _dot_tn(h, b_ref[...])

    return pl.pallas_call(
        body, name=name, grid=(n // tn, s // ts),
        in_specs=[pl.BlockSpec((ts, k), lambda j, i: (i, 0)), pl.BlockSpec((ts, 1), lambda j, i: (i, 0)),
                  pl.BlockSpec((1, k), lambda j, i: (0, 0)), pl.BlockSpec((ts, tn), lambda j, i: (i, j))],
        out_specs=pl.BlockSpec((None, k, tn), lambda j, i: (j, 0, 0)),
        out_shape=jax.ShapeDtypeStruct((n // tn, k, tn), F32),
        compiler_params=_cp("parallel", "arbitrary"),
    )(x, r, nw, b)


def _out_bwd_call(dx1, yret, ymla, wout, ts, order=None):
    s = dx1.shape[0]

    def body(dx_ref, yr_ref, ym_ref, w_ref, dyr_ref, do_ref, dwo_ref):
        _zero_first(pl.program_id(0) == 0, dwo_ref)
        dxb = dx_ref[...].astype(BF16)
        dmix = _dot_nt(dxb, w_ref[...])
        dyr_ref[...] = dmix[:, :RET_W]
        ym = ym_ref[...]
        lane = lax.broadcasted_iota(jnp.int32, (ts, LANES), 1)
        for p in range(N_HEADS // 2):
            dom = dmix[:, RET_W + p * LANES:RET_W + (p + 1) * LANES]
            prod = dom * ym[:, p * LANES:(p + 1) * LANES].astype(F32)
            for hh in range(2):
                mine = (lane >= HEAD) if hh else (lane < HEAD)
                hi, lo = _hi_lo(jnp.sum(jnp.where(mine, prod, 0.0), axis=1, keepdims=True))
                base = jnp.where(lane < HEAD, pltpu.roll(dom, HEAD, 1) if hh else dom, 0.0)
                do_ref[2 * p + hh] = _lane_pair((ts, LANES), V_AUX, -hi, -lo, base).astype(BF16)
        dwo_ref[0:RET_W, :] += _dot_tn(yr_ref[...], dxb)
        dwo_ref[RET_W:, :] += _dot_tn(ym, dxb)

    sd = jax.ShapeDtypeStruct
    body, first_specs, first = _ordered_after(body, order)
    return pl.pallas_call(
        body, name="out_proj_bwd", grid=(s // ts,),
        in_specs=first_specs + [_row(ts, D_MODEL), _row(ts, RET_W), _row(ts, MLA_W), _full((D_MODEL, D_MODEL))],
        out_specs=[_row(ts, RET_W), _hrow(N_HEADS, ts, LANES), _full((D_MODEL, D_MODEL))],
        out_shape=[sd((s, RET_W), F32), sd((N_HEADS, s, LANES), BF16), sd((D_MODEL, D_MODEL), F32)],
        compiler_params=_cp("arbitrary"),
    )(*first, dx1, yret, ymla, wout)


def _ret_bwd_q_call(q, k, v, o, g, dy, gnw, rc, cos_r, sin_r, tr, order=None):
    s = q.shape[0]
    c = RET_CHUNK
    nc = tr // c
    ns = RET_SLABS

    def body(q_ref, k_ref, v_ref, o_ref, g_ref, dy_ref, gnw_ref, dm_ref, zeta_ref, xi_ref, cd_ref, bd_ref, cr_ref, sr_ref,
             dq_ref, dg_ref, do_ref, dgnw_ref, st_ref):
        _zero_first(pl.program_id(1) == 0, st_ref, dgnw_ref)
        bd = bd_ref[...]
        avg = bd * (1.0 / HEAD)
        chunks = [slice(ci * c, (ci + 1) * c) for ci in range(nc)]
        lanes = [slice(sl * LANES, (sl + 1) * LANES) for sl in range(ns)]
        dov = []
        for ln in lanes:
            ov = o_ref[:, ln]
            ctr = ov - _dot_hi(ov, avg)
            rs = lax.rsqrt(_dot_hi(ctr * ctr, avg) + EPS)
            oh = ctr * rs
            gg, dyv, gnw_v = g_ref[:, ln], dy_ref[:, ln], gnw_ref[:, ln]
            sg = _sigmoid(gg)
            sl = gg * sg
            dg_ref[:, ln] = (dyv * oh * gnw_v * _dsilu(gg, sg)).astype(BF16)
            dgnw_ref[:, ln] += _colsum(dyv * sl * oh)
            doh = dyv * sl * gnw_v
            dov.append((rs * (doh - _dot_hi(doh, avg) - oh * _dot_hi(doh * oh, avg))).astype(BF16))
            do_ref[:, ln] = dov[-1]
        states = _ret_states(k_ref, v_ref, zeta_ref, cd_ref, bd, st_ref, chunks, lanes, False)
        for ci, rows in enumerate(chunks):
            for sl, ln in enumerate(lanes):
                doc = dov[sl][rows, :]
                dq = (_dot_nt(doc, states[sl][ci]) * xi_ref[sl]
                      + _pair_product(doc, _stack_heads(v_ref[rows, ln]), dm_ref[sl], _stack_heads(k_ref[rows, ln])))
                dq_ref[rows, ln] = _unrope(dq, cr_ref[rows, :], sr_ref[rows, :], HEAD // 2).astype(BF16)

    specs = _ret_specs(tr, lambda i: i)
    sd = jax.ShapeDtypeStruct
    body, first_specs, first = _ordered_after(body, order)
    return pl.pallas_call(
        body, name="ret_bwd_q", grid=(4 // ns, s // tr),
        in_specs=first_specs + [specs["slab"]] * 6
        + [specs["vec"], specs["dmask"], specs["rows"], specs["rows"], specs["state"], specs["bd"], specs["tab"], specs["tab"]],
        out_specs=[specs["slab"]] * 3 + [specs["vec"]],
        out_shape=[sd((s, RET_W), BF16), sd((s, RET_W), BF16), sd((s, RET_W), BF16), sd((1, RET_W), F32)],
        scratch_shapes=[pltpu.VMEM((ns, LANES, LANES), F32)],
        compiler_params=_cp("parallel", "arbitrary"),
    )(*first, q, k, v, o, g, dy, gnw, rc["dmask"], rc["zeta"], rc["xi"], rc["cd"], rc["bd"], cos_r, sin_r)


def _ret_bwd_kv_call(q, k, v, do, rc, cos_r, sin_r, tr):
    s = q.shape[0]
    c = RET_CHUNK
    nc = tr // c
    nt = s // tr
    ns = RET_SLABS

    def body(q_ref, k_ref, v_ref, do_ref, dm_ref, zeta_ref, xi_ref, cd_ref, bd_ref, cr_ref, sr_ref, dk_ref, dv_ref, gs_ref):
        _zero_first(pl.program_id(1) == 0, gs_ref)
        bd = bd_ref[...]
        chunks = [slice(ci * c, (ci + 1) * c) for ci in range(nc)]
        lanes = [slice(sl * LANES, (sl + 1) * LANES) for sl in range(ns)]
        states = _ret_states(q_ref, do_ref, xi_ref, cd_ref, bd, gs_ref, chunks, lanes, True)
        for ci, rows in enumerate(chunks):
            for sl, ln in enumerate(lanes):
                kc, vc = k_ref[rows, ln], v_ref[rows, ln]
                q2, do2 = _stack_heads(q_ref[rows, ln]), _stack_heads(do_ref[rows, ln])
                gb = states[sl][ci]
                dk = _dot_nt(vc, gb) * zeta_ref[sl] + _pair_product(vc, do2, dm_ref[sl], q2)
                dv = _dot(kc, gb) * zeta_ref[sl] + _pair_product(kc, q2, dm_ref[sl], do2)
                dk_ref[rows, ln] = (_unrope(dk, cr_ref[rows, :], sr_ref[rows, :], HEAD // 2) * (HEAD ** -0.5)).astype(BF16)
                dv_ref[rows, ln] = dv.astype(BF16)

    specs = _ret_specs(tr, lambda i: nt - 1 - i)
    sd = jax.ShapeDtypeStruct
    return pl.pallas_call(
        body, name="ret_bwd_kv", grid=(4 // ns, nt),
        in_specs=[specs["slab"]] * 4 + [specs["dmask"], specs["rows"], specs["rows"], specs["state"], specs["bd"],
                                        specs["tab"], specs["tab"]],
        out_specs=[specs["slab"]] * 2,
        out_shape=[sd((s, RET_W), BF16), sd((s, RET_W), BF16)],
        scratch_shapes=[pltpu.VMEM((ns, LANES, LANES), F32)],
        compiler_params=_cp("parallel", "arbitrary"),
    )(q, k, v, do, rc["dmask_t"], rc["zeta"], rc["xi"], rc["cd"], rc["bd"], cos_r, sin_r)


FLASH_BWD_HEADS = 8


def _flash_bwd_call(qb, k, v, do, tb, order=None):
    s = qb.shape[1]
    nb = s // tb
    hg = FLASH_BWD_HEADS
    pairs = [(a, b) for a in range(nb) for b in range(a, nb)]
    ki_of, qi_of = (jnp.asarray(np.array(col, np.int32)) for col in zip(*pairs))
    extra = [] if order is None else [order]

    def body(ki_ref, qi_ref, *refs):
        q_ref, k_ref, v_ref, do_ref, dk_ref, dv_ref, dq_hbm, dka_ref, dva_ref, dq_ref, sem = refs[len(extra):]
        g, p = pl.program_id(0), pl.program_id(1)
        ki, qi = ki_ref[p], qi_ref[p]
        _zero_first(p == 0, dq_ref)
        _zero_first(qi == ki, dka_ref, dva_ref)
        rows = pl.ds(pl.multiple_of(qi * tb, tb), tb)

        def step(masked):
            if masked:
                keep = lax.broadcasted_iota(jnp.int32, (tb, tb), 0) <= lax.broadcasted_iota(jnp.int32, (tb, tb), 1)
            for h in range(hg):
                st = _dot_nt(k_ref[h], q_ref[h])
                if masked:
                    st = jnp.where(keep, st, NEG)
                pt = jnp.exp2(st)
                dob = do_ref[h]
                dva_ref[h] += _dot(pt.astype(BF16), dob)
                dst = (pt * _dot_nt(v_ref[h], dob)).astype(BF16)
                dka_ref[h] += _dot(dst, q_ref[h])
                dq_ref[h, rows, :] += _dot_tn(dst, k_ref[h])

        @pl.when(qi > ki)
        def _():
            step(False)

        @pl.when(qi == ki)
        def _():
            step(True)

        @pl.when(qi == nb - 1)
        def _():
            dk_ref[...] = (dka_ref[...] * LN2).astype(BF16)
            dv_ref[...] = dva_ref[...].astype(BF16)

        @pl.when(p == len(pairs) - 1)
        def _():
            cp = pltpu.make_async_copy(dq_ref, dq_hbm.at[pl.ds(g * hg, hg)], sem)
            cp.start()
            cp.wait()

    kspec = pl.BlockSpec((hg, tb, LANES), lambda g, p, ki_ref, qi_ref: (g, ki_ref[p], 0))
    qspec = pl.BlockSpec((hg, tb, LANES), lambda g, p, ki_ref, qi_ref: (g, qi_ref[p], 0))
    hm = jax.ShapeDtypeStruct((N_HEADS, s, LANES), BF16)
    return pl.pallas_call(
        body, name="mla_flash_bwd",
        grid_spec=pltpu.PrefetchScalarGridSpec(
            num_scalar_prefetch=2, grid=(N_HEADS // hg, len(pairs)),
            in_specs=[ANY] * len(extra) + [qspec, kspec, kspec, qspec],
            out_specs=[kspec, kspec, ANY],
            scratch_shapes=[pltpu.VMEM((hg, tb, LANES), F32), pltpu.VMEM((hg, tb, LANES), F32),
                            pltpu.VMEM((hg, s, LANES), F32), pltpu.SemaphoreType.DMA]),
        out_shape=[hm, hm, jax.ShapeDtypeStruct((N_HEADS, s, LANES), F32)],
        compiler_params=_cp("arbitrary", "arbitrary"),
    )(ki_of, qi_of, *extra, qb, k, v, do)


def _mla_post_call(dq, dk, dv, cq, ckv, qnw, kvnw, wq, wk, wv, cos_m, sin_m, ts):
    s = cq.shape[0]

    def body(dq_ref, dk_ref, dv_ref, cq_ref, ckv_ref, qnw_ref, kvnw_ref, wq_ref, wk_ref, wv_ref, cm_ref, sm_ref,
             dcq_ref, dckv_ref, dkpe_ref, dwq_ref, dwk_ref, dwv_ref, dqnw_ref, dkvnw_ref):
        _zero_first(pl.program_id(0) == 0, dwq_ref, dwk_ref, dwv_ref, dqnw_ref, dkvnw_ref)
        cqv, ckvv = cq_ref[...], ckv_ref[...]
        rq, rkv = _rstd(cqv), _rstd(ckvv)
        qh_, kvh_ = cqv * rq, ckvv * rkv
        qnw_v, kvnw_v = qnw_ref[...], kvnw_ref[...]
        cqn = (qh_ * qnw_v).astype(BF16)
        ckvn = (kvh_ * kvnw_v).astype(BF16)
        cm, sm = cm_ref[...], sm_ref[...]
        dcqn = jnp.zeros((ts, Q_RANK), F32)
        dckvn = jnp.zeros((ts, KV_RANK), F32)
        dkpe = jnp.zeros((ts, LANES), F32)
        for h in range(N_HEADS):
            dqu = _unrope(dq_ref[h] * SM_SCALE, cm, sm, ROPE // 2).astype(BF16)
            dwq_ref[h] += _dot_tn(cqn, dqu)
            dcqn = dcqn + _dot_nt(dqu, wq_ref[h])
            dkb, dvb = dk_ref[h], dv_ref[h]
            dkpe = dkpe + dkb.astype(F32)
            dwk_ref[h] += _dot_tn(ckvn, dkb)
            dwv_ref[h] += _dot_tn(ckvn, dvb)
            dckvn = dckvn + _dot_nt(dkb, wk_ref[h]) + _dot_nt(dvb, wv_ref[h])
        lane = lax.broadcasted_iota(jnp.int32, (ts, LANES), 1)
        dkpe = jnp.where((lane >= KPE_LO) & (lane < KPE_LO + ROPE), dkpe, 0.0)
        dkpe_ref[...] = _unrope(dkpe, cm, sm, ROPE // 2).astype(BF16)
        dqnw_ref[...] += _colsum(dcqn * qh_)
        dkvnw_ref[...] += _colsum(dckvn * kvh_)
        dcq_ref[...] = _norm_bwd(dcqn, qh_, rq, qnw_v).astype(BF16)
        dckv_ref[...] = _norm_bwd(dckvn, kvh_, rkv, kvnw_v).astype(BF16)

    sd = jax.ShapeDtypeStruct
    hm = _hrow(N_HEADS, ts, LANES)
    return pl.pallas_call(
        body, name="mla_post", grid=(s // ts,),
        in_specs=[hm, hm, hm, _row(ts, Q_RANK), _row(ts, KV_RANK), _full((1, Q_RANK)), _full((1, KV_RANK)),
                  _full((N_HEADS, Q_RANK, LANES)), _full((N_HEADS, KV_RANK, LANES)), _full((N_HEADS, KV_RANK, LANES)),
                  _row(ts, LANES), _row(ts, LANES)],
        out_specs=[_row(ts, Q_RANK), _row(ts, KV_RANK), _row(ts, LANES),
                   _full((N_HEADS, Q_RANK, LANES)), _full((N_HEADS, KV_RANK, LANES)), _full((N_HEADS, KV_RANK, LANES)),
                   _full((1, Q_RANK)), _full((1, KV_RANK))],
        out_shape=[sd((s, Q_RANK), BF16), sd((s, KV_RANK), BF16), sd((s, LANES), BF16),
                   sd((N_HEADS, Q_RANK, LANES), F32), sd((N_HEADS, KV_RANK, LANES), F32), sd((N_HEADS, KV_RANK, LANES), F32),
                   sd((1, Q_RANK), F32), sd((1, KV_RANK), F32)],
        compiler_params=_cp("arbitrary"),
    )(dq, dk, dv, cq, ckv, qnw, kvnw, wq, wk, wv, cos_m, sin_m)


def _in_bwd_call(parts, x, r1, anw, dx1, win, ts):
    s = x.shape[0]
    widths = [p.shape[1] for p in parts]
    np_ = len(parts)

    def body(*refs):
        p_refs = refs[:np_]
        x_ref, r_ref, anw_ref, dx1_ref, w_ref, dx_ref, dw_ref, danw_ref = refs[np_:]
        _zero_first(pl.program_id(0) == 0, dw_ref, danw_ref)
        dproj = jnp.concatenate([p[...] for p in p_refs], axis=-1)
        r, anw_v = r_ref[...], anw_ref[...]
        xh = x_ref[...] * r
        dw_ref[...] += _dot_tn((xh * anw_v).astype(BF16), dproj)
        dh = _dot_nt(dproj, w_ref[...])
        danw_ref[...] += _colsum(dh * xh)
        dx_ref[...] = dx1_ref[...] + _norm_bwd(dh, xh, r, anw_v)

    sd = jax.ShapeDtypeStruct
    return pl.pallas_call(
        body, name="in_proj_bwd", grid=(s // ts,),
        in_specs=[_row(ts, w) for w in widths]
        + [_row(ts, D_MODEL), _row(ts, 1), _full((1, D_MODEL)), _row(ts, D_MODEL), _full((D_MODEL, IN_EXT))],
        out_specs=[_row(ts, D_MODEL), _full((D_MODEL, IN_EXT)), _full((1, D_MODEL))],
        out_shape=[sd((s, D_MODEL), F32), sd((D_MODEL, IN_EXT), F32), sd((1, D_MODEL), F32)],
        compiler_params=_cp("arbitrary"),
    )(*parts, x, r1, anw, dx1, win)


def _local_step(x, positions, tgt, w, small, ex=None):
    s = x.shape[0]
    t = _tiles(s)
    ex = _Exchanges(w) if ex is None else ex
    f = _forward(x, positions, tgt, w, small, ex)
    pw, rc = f["pw"], f["rc"]
    cos_r, sin_r, cos_m, sin_m = f["tabs"]
    dx2, loss, g_fw = f["dx2"], f["loss"], f["g_fw"]
    du, dx1, g_cw, g_cb, g_fnw, g_wd = _ffn_bwd_call(dx2, f["u"], f["uc"], w["conv_w"], pw["wdown"], pw["wup"],
                                                     f["x1"], f["r2"], small["ffn_norm_w"], t["t2"])
    g_wup = _dw_norm_call(f["x1"], f["r2"], small["ffn_norm_w"], du, t["tw"], F2 // 4, "dw_up")
    started = ex.mlp_grads(dict(w_up=g_wup, w_down=g_wd))
    dy_ret, do, g_wout = _out_bwd_call(dx1, f["y_ret"], f["y_mla"], pw["wout"], t["t1"], started)
    started = ex.behind_out_bwd(g_wout)
    dmk, dmv, dmq = _flash_bwd_call(f["mqb"], f["mk"], f["mv"], do, t["tb"], started)
    ex.behind_attention(dmk)
    drq, dg, do_ret, g_gnw = _ret_bwd_q_call(f["q"], f["k"], f["v"], f["o_ret"], f["g"], dy_ret, small["ret_gn_w"], rc, cos_r, sin_r,
                                             t["tr"], order=dmk)
    drk, drv = _ret_bwd_kv_call(f["q"], f["k"], f["v"], do_ret, rc, cos_r, sin_r, t["tr"])
    dcq, dckv, dkpe, g_wq, g_wk, g_wv, g_qnw, g_kvnw = _mla_post_call(
        dmq, dmk, dmv, f["cq"], f["ckv"], small["mla_q_norm_w"], small["mla_kv_norm_w"], pw["wq"], pw["wk"], pw["wv"], cos_m, sin_m, t["ts"])
    gx, g_win_ext, g_anw = _in_bwd_call([drq, drk, drv, dg, dcq, dckv, dkpe], x, f["r1"], small["attn_norm_w"], dx1, pw["win"], t["ts"])
    if w["w_in"].ndim == 3:
        g_win = _win_grad_blocks_call(g_win_ext, w["w_in"].shape[0])
    else:
        g_win = _win_grad_blocks_call(g_win_ext, 1)[0]
    g_wuq = g_wq.transpose(1, 0, 2)[:, :, :HEAD + ROPE].reshape(Q_RANK, N_HEADS * (HEAD + ROPE))
    g_wukv = jnp.concatenate([g_wk[:, :, :HEAD], g_wv[:, :, :HEAD]], -1).transpose(1, 0, 2).reshape(KV_RANK, 2 * MLA_W)
    gw = dict(w_in=g_win, w_uq=g_wuq, w_ukv=g_wukv, w_out=g_wout, w_up=g_wup,
              conv_w=g_cw, w_down=g_wd)
    gs = dict(attn_norm_w=g_anw, ret_gn_w=g_gnw, mla_q_norm_w=g_qnw, mla_kv_norm_w=g_kvnw, ffn_norm_w=g_fnw,
              conv_b=g_cb, final_norm_w=g_fw)
    return loss, gx, gw, gs


MESH_ID = pl.DeviceIdType.MESH
ANY = pl.BlockSpec(memory_space=pl.ANY)
VMEM_SPEC = pl.BlockSpec(memory_space=pltpu.VMEM)
N_DEV = 8
GROUP_A = (("w_in", (D_MODEL, IN_W // 4), 1), ("w_uq", (Q_RANK, 192), 1), ("w_ukv", (KV_RANK, 256), 1),
           ("w_out", (D_MODEL // 4, D_MODEL), 0))
GROUP_B = (("w_up", (D_MODEL, F2 // 4), 1), ("w_down", (D_FF // 4, D_MODEL), 0))
HBM_SPEC = pl.BlockSpec(memory_space=pltpu.HBM)
SEM_SPEC = pl.BlockSpec(memory_space=pltpu.SEMAPHORE)


def _mesh_pos():
    return lax.axis_index("x"), lax.axis_index("y"), lax.axis_index("c")


def _other_chips(x, y):
    return [(1 - x, y), (x, 1 - y), (1 - x, 1 - y)]


def _remote(src, dst, send_sems, recv_sems, k, dev):
    return pltpu.make_async_remote_copy(src_ref=src, dst_ref=dst, send_sem=send_sems.at[k], recv_sem=recv_sems.at[k],
                                        device_id=dev, device_id_type=MESH_ID)


def _gather_list_call(parts, tag):
    n = len(parts)

    def body(*refs):
        srcs, outs, (send_sems, recv_sems) = refs[:n], refs[n:2 * n], refs[2 * n:]
        x, y, c = _mesh_pos()
        sm = 2 * x + y
        chips = _other_chips(x, y)
        sib = (x, y, 1 - c)
        rc = lambda k, src, dst, dev: _remote(src, dst, send_sems, recv_sems, k, dev)
        first = [rc(7 * i + j, srcs[i].at[c], outs[i].at[sm, c], (cx, cy, c)) for i in range(n) for j, (cx, cy) in enumerate(chips)]
        own = [rc(7 * i + 6, srcs[i], outs[i].at[sm], sib) for i in range(n)]
        for cp in first + own:
            cp.start()
        passed = []
        for j, (cx, cy) in enumerate(chips):
            for i in range(n):
                land = outs[i].at[2 * cx + cy, c]
                rc(7 * i + j, srcs[i].at[c], land, (cx, cy, c)).wait_recv()
                cp = rc(7 * i + 3 + j, land, land, sib)
                cp.start()
                passed.append(cp)
        for j, (cx, cy) in enumerate(chips):
            for i in range(n):
                rc(7 * i + 3 + j, srcs[i].at[c], outs[i].at[2 * cx + cy, 1 - c], sib).wait_recv()
        for cp in own:
            cp.wait_recv()
        for cp in first + passed + own:
            cp.wait_send()

    return pl.pallas_call(
        body, name="weights_all_gather_" + tag,
        in_specs=[ANY] * n, out_specs=[ANY] * n,
        out_shape=[jax.ShapeDtypeStruct((4,) + p.shape, p.dtype) for p in parts],
        scratch_shapes=[pltpu.SemaphoreType.DMA((7 * n,)), pltpu.SemaphoreType.DMA((7 * n,))],
    )(*parts)


def _direct_gather_copies(srcs, lands, send_sems, recv_sems):
    x, y, c = _mesh_pos()
    sm = 2 * x + y
    sends, recvs = [], []
    for i, (src, land) in enumerate(zip(srcs, lands)):
        for j, (cx, cy) in enumerate(_other_chips(x, y)):
            for t in range(2):
                sends.append(_remote(src.at[c], land.at[sm, c], send_sems, recv_sems, 13 * i + 4 * j + 2 * c + t, (cx, cy, t)))
                recvs.append(_remote(src.at[t], land.at[2 * cx + cy, t], send_sems, recv_sems, 13 * i + 4 * j + 2 * t + c, (cx, cy, t)))
        sends.append(_remote(src, land.at[sm], send_sems, recv_sems, 13 * i + 12, (x, y, 1 - c)))
        recvs.append(_remote(src, land.at[sm], send_sems, recv_sems, 13 * i + 12, (x, y, 1 - c)))
    return sends, recvs


def _sibling_copies(srcs, lands, send_sems, recv_sems):
    x, y, c = _mesh_pos()
    cps = [_remote(src.at[s, 1 - c], land.at[s], send_sems, recv_sems, 4 * i + s, (x, y, 1 - c))
           for i, (src, land) in enumerate(zip(srcs, lands)) for s in range(4)]
    return cps, cps


def _chips_copies(srcs, lands, send_sems, recv_sems):
    x, y, c = _mesh_pos()
    cps = [_remote(src.at[2 * cx + cy], land.at[j], send_sems, recv_sems, 3 * i + j, (cx, cy, c))
           for i, (src, land) in enumerate(zip(srcs, lands)) for j, (cx, cy) in enumerate(_other_chips(x, y))]
    return cps, cps


def _share_copies(srcs, lands, send_sems, recv_sems):
    x, y, c = _mesh_pos()
    cps = [_remote(src, land, send_sems, recv_sems, i, (x, y, 1 - c)) for i, (src, land) in enumerate(zip(srcs, lands))]
    return cps, cps


def _exchange_call(name, copies, srcs, land_shapes, n_sems):
    n = len(srcs)

    def body(*refs):
        sends, recvs = copies(refs[:n], refs[n:2 * n], refs[2 * n], refs[2 * n + 1])
        for cp in sends:
            cp.start()
        for cp in sends:
            cp.wait_send()
        for cp in recvs:
            cp.wait_recv()

    return pl.pallas_call(
        body, name=name, in_specs=[ANY] * n, out_specs=[ANY] * n, out_shape=list(land_shapes),
        scratch_shapes=[pltpu.SemaphoreType.DMA((n_sems,)), pltpu.SemaphoreType.DMA((n_sems,))],
    )(*srcs)


def _exchange_start_call(name, copies, srcs, land_shapes, n_sems, order=None):
    n = len(srcs)
    extra = [] if order is None else [order]
    k = 2 * n + len(extra)

    def body(*refs):
        sends, _ = copies(refs[:n], refs[n:2 * n], refs[k], refs[k + 1])
        for cp in sends:
            cp.start()
        refs[-1][...] = jnp.zeros_like(refs[-1])

    hbm = lambda a: pltpu.with_memory_space_constraint(a, pltpu.HBM)
    lands = [hbm(lax.empty(sd.shape, sd.dtype)) for sd in land_shapes]
    sem = pltpu.SemaphoreType.DMA((n_sems,))
    out = pl.pallas_call(
        body, name=name,
        out_shape=(sem, sem, *[pltpu.HBM(a.shape, a.dtype) for a in list(srcs) + lands], jax.ShapeDtypeStruct((8, LANES), F32)),
        in_specs=[HBM_SPEC] * (2 * n) + [ANY] * len(extra), out_specs=(SEM_SPEC, SEM_SPEC, *[HBM_SPEC] * (2 * n), VMEM_SPEC),
        input_output_aliases={i: 2 + i for i in range(2 * n)},
        compiler_params=pltpu.CompilerParams(has_side_effects=pltpu.SideEffectType.DATAFLOW_SIDE_EFFECTING),
    )(*[hbm(a) for a in srcs], *lands, *extra)
    return out[0], out[1], out[2:2 + n], out[2 + n:2 + 2 * n], out[-1]


def _exchange_wait_call(name, copies, started, after):
    send_sems, recv_sems, srcs, lands, _ = started
    n = len(srcs)

    def body(*refs):
        sends, recvs = copies(refs[:n], refs[n:2 * n], refs[2 * n], refs[2 * n + 1])
        for cp in sends:
            cp.wait_send()
        for cp in recvs:
            cp.wait_recv()

    out = pl.pallas_call(
        body, name=name,
        out_shape=tuple(pltpu.HBM(a.shape, a.dtype) for a in list(srcs) + list(lands)),
        in_specs=[HBM_SPEC] * (2 * n) + [SEM_SPEC, SEM_SPEC, ANY], out_specs=tuple([HBM_SPEC] * (2 * n)),
        input_output_aliases={i: i for i in range(2 * n)},
        compiler_params=pltpu.CompilerParams(has_side_effects=pltpu.SideEffectType.DATAFLOW_SIDE_EFFECTING),
    )(*srcs, *lands, send_sems, recv_sems, after)
    return out[:n], out[n:]


def _rows_tile(rows, width, itemsize=4):
    limit = max(16, (3 << 20) // (width * itemsize))
    if rows <= limit:
        return rows
    return max(t for t in range(16, limit + 1, 16) if rows % t == 0)


def _sum_sibling_call(g, buf, c, name):
    _, _, rh, w = g.shape
    tile = _rows_tile(rh, w)

    def body(c_ref, g_ref, b_ref, p_ref, pb_ref):
        p = g_ref[...] + b_ref[...]
        p_ref[...] = p
        pb_ref[...] = p.astype(BF16)

    blk = pl.BlockSpec((None, tile, w), lambda s, i, c_ref: (s, i, 0))
    return pl.pallas_call(
        body, name=name,
        grid_spec=pltpu.PrefetchScalarGridSpec(
            num_scalar_prefetch=1, grid=(4, rh // tile),
            in_specs=[pl.BlockSpec((None, None, tile, w), lambda s, i, c_ref: (s, c_ref[0], i, 0)), blk],
            out_specs=[blk, blk]),
        out_shape=[jax.ShapeDtypeStruct((4, rh, w), F32), jax.ShapeDtypeStruct((4, rh, w), BF16)],
        compiler_params=_cp("parallel", "parallel"),
    )(c, g, buf)


def _sum_chips_call(p, buf, sm, name):
    _, rh, w = p.shape
    tile = _rows_tile(rh, w)

    def body(sm_ref, p_ref, b_ref, f_ref):
        f_ref[...] = ((p_ref[...] + b_ref[0].astype(F32)) + b_ref[1].astype(F32)) + b_ref[2].astype(F32)

    return pl.pallas_call(
        body, name=name,
        grid_spec=pltpu.PrefetchScalarGridSpec(
            num_scalar_prefetch=1, grid=(rh // tile,),
            in_specs=[pl.BlockSpec((None, tile, w), lambda i, sm_ref: (sm_ref[0], i, 0)),
                      pl.BlockSpec((3, tile, w), lambda i, sm_ref: (0, i, 0))],
            out_specs=pl.BlockSpec((tile, w), lambda i, sm_ref: (i, 0))),
        out_shape=jax.ShapeDtypeStruct((rh, w), F32),
        compiler_params=_cp("parallel"),
    )(sm, p, buf)


def _adamw_halves_call(w, g_mine, g_sib, c, m, v, name, transposed=False, order=None):
    extra = [] if order is None else [order]
    if transposed:
        rows, r = w.shape
        rh = r // 2
        tile = _rows_tile(rows, rh)
        whole = pl.BlockSpec((tile, rh), lambda h, i, c_ref: (i, h))
        half = pl.BlockSpec((tile, rh), lambda h, i, c_ref: (i, 0))
        nt = rows // tile
    else:
        r, wd = w.shape
        rh = r // 2
        tile = _rows_tile(rh, wd)
        nt = rh // tile
        whole = pl.BlockSpec((tile, wd), lambda h, i, c_ref: (h * nt + i, 0))
        half = pl.BlockSpec((tile, wd), lambda h, i, c_ref: (i, 0))

    def body(c_ref, w_ref, gm_ref, gs_ref, m_ref, v_ref, *rest):
        g_ref, d_ref, nm_ref, nv_ref = rest[len(extra):]
        gv = jnp.where(pl.program_id(0) == c_ref[0], gm_ref[...], gs_ref[...])
        g_ref[...] = gv
        nm = ADAM_B1 * m_ref[...] + (1.0 - ADAM_B1) * gv
        nv = ADAM_B2 * v_ref[...] + (1.0 - ADAM_B2) * jnp.square(gv)
        m_hat = nm / (1.0 - ADAM_B1 ** ADAM_STEP)
        v_hat = nv / (1.0 - ADAM_B2 ** ADAM_STEP)
        d_ref[...] = -ADAM_LR * (m_hat / (jnp.sqrt(v_hat) + ADAM_EPS) + ADAM_WD * w_ref[...])
        nm_ref[...] = nm
        nv_ref[...] = nv

    sd = jax.ShapeDtypeStruct(w.shape, F32)
    return pl.pallas_call(
        body, name=name,
        grid_spec=pltpu.PrefetchScalarGridSpec(
            num_scalar_prefetch=1, grid=(2, nt),
            in_specs=[whole, half, half, whole, whole] + [ANY] * len(extra), out_specs=[whole] * 4),
        out_shape=[sd, sd, sd, sd],
        compiler_params=_cp("parallel", "parallel"),
    )(c, w, g_mine, g_sib, m, v, *extra)


def _all_reduce8_call(vec, name):
    rows = vec.shape[0]

    def body(v_ref, out_ref, slots, send_sems, recv_sems):
        x, y, c = _mesh_pos()
        me = 4 * x + 2 * y + c
        slots[me] = v_ref[...]

        def rcopy(k, to_me):
            bx, by, bc = (k >> 2) & 1, (k >> 1) & 1, k & 1
            px, py, pc = (1 - x if bx else x), (1 - y if by else y), (1 - c if bc else c)
            slot = 4 * px + 2 * py + pc if to_me else me
            return pltpu.make_async_remote_copy(src_ref=v_ref, dst_ref=slots.at[slot], send_sem=send_sems.at[k - 1],
                                                recv_sem=recv_sems.at[k - 1], device_id=(px, py, pc), device_id_type=MESH_ID)

        for k in range(1, N_DEV):
            rcopy(k, False).start()
        for k in range(1, N_DEV):
            rcopy(k, True).wait_recv()
        for k in range(1, N_DEV):
            rcopy(k, False).wait_send()
        tot = slots[0]
        for d in range(1, N_DEV):
            tot = tot + slots[d]
        out_ref[...] = tot

    return pl.pallas_call(
        body, name=name,
        in_specs=[VMEM_SPEC], out_specs=VMEM_SPEC,
        out_shape=jax.ShapeDtypeStruct((rows, LANES), F32),
        scratch_shapes=[pltpu.VMEM((N_DEV, rows, LANES), F32),
                        pltpu.SemaphoreType.DMA((N_DEV - 1,)), pltpu.SemaphoreType.DMA((N_DEV - 1,))],
    )(vec)


def _adamw_call(w, g, m, v, name):
    r, c = w.shape
    rb = r if r <= 256 else (256 if r % 256 == 0 else 352)
    assert r % rb == 0

    def body(w_ref, g_ref, m_ref, v_ref, d_ref, nm_ref, nv_ref):
        gv = g_ref[...]
        nm = ADAM_B1 * m_ref[...] + (1.0 - ADAM_B1) * gv
        nv = ADAM_B2 * v_ref[...] + (1.0 - ADAM_B2) * jnp.square(gv)
        m_hat = nm / (1.0 - ADAM_B1 ** ADAM_STEP)
        v_hat = nv / (1.0 - ADAM_B2 ** ADAM_STEP)
        d_ref[...] = -ADAM_LR * (m_hat / (jnp.sqrt(v_hat) + ADAM_EPS) + ADAM_WD * w_ref[...])
        nm_ref[...] = nm
        nv_ref[...] = nv

    spec = pl.BlockSpec((rb, c), lambda i: (i, 0))
    sd = jax.ShapeDtypeStruct((r, c), F32)
    return pl.pallas_call(
        body, name=name, grid=(r // rb,),
        in_specs=[spec] * 4, out_specs=[spec] * 3, out_shape=[sd, sd, sd],
        compiler_params=_cp("parallel"),
    )(w, g, m, v)


SMALL = (("attn_norm_w", D_MODEL), ("ret_gn_w", RET_W), ("mla_q_norm_w", Q_RANK), ("mla_kv_norm_w", KV_RANK),
         ("ffn_norm_w", D_MODEL), ("conv_b", F2), ("final_norm_w", D_MODEL))
WEIGHT_ORDER = ("attn_norm_w", "w_in", "ret_gn_w", "mla_q_norm_w", "w_uq", "mla_kv_norm_w", "w_ukv", "w_out",
                "ffn_norm_w", "w_up", "conv_w", "conv_b", "w_down", "final_norm_w")


def _pad_rows(flat, rows):
    return jnp.concatenate([flat, jnp.zeros((rows * LANES - flat.shape[0],), flat.dtype)]).reshape(rows, LANES)


def kernel(x, positions, attn_norm_w, w_in, ret_gn_w, mla_q_norm_w, w_uq, mla_kv_norm_w, w_ukv, w_out, ffn_norm_w, w_up, conv_w, conv_b, w_down, final_norm_w, loss_target, m_attn_norm_w, m_w_in, m_ret_gn_w, m_mla_q_norm_w, m_w_uq, m_mla_kv_norm_w, m_w_ukv, m_w_out, m_ffn_norm_w, m_w_up, m_conv_w, m_conv_b, m_w_down, m_final_norm_w, v_attn_norm_w, v_w_in, v_ret_gn_w, v_mla_q_norm_w, v_w_uq, v_mla_kv_norm_w, v_w_ukv, v_w_out, v_ffn_norm_w, v_w_up, v_conv_w, v_conv_b, v_w_down, v_final_norm_w):
    args = dict(locals())
    cx, cy, cc = _mesh_pos()
    sm = 2 * cx + cy

    c_arr, sm_arr = cc.reshape(1).astype(jnp.int32), sm.reshape(1).astype(jnp.int32)
    sds = jax.ShapeDtypeStruct

    def my_shards(group):
        return [args[n][0].astype(BF16).reshape(2, r // 2, c) for n, (r, c), _ in group]

    def full_weights(gathered, group):
        full = {}
        for (n, (r, c), axis), got in zip(group, gathered):
            piece = got.reshape(4, r, c)
            full[n] = piece if n in ("w_up", "w_in") else (piece.transpose(1, 0, 2).reshape(r, 4 * c) if axis == 1 else piece.reshape(4 * r, c))
        return full

    def by_owner(gw, group):
        out = []
        for n, (r, c), axis in group:
            g = gw[n]
            if axis == 1 and g.ndim == 2:
                g = g.reshape(r, 4, c).transpose(1, 0, 2)
            out.append(g.reshape(4, 2, r // 2, c))
        return out

    def sibling_shapes(gs):
        return [sds((4,) + g.shape[2:], F32) for g in gs]

    def chip_sums(gs, bufs, group):
        res = [_sum_sibling_call(g, b, c_arr, "grads_sum_sibling_" + n) for g, b, (n, _, _) in zip(gs, bufs, group)]
        return [p for p, _ in res], [pb for _, pb in res]

    def chips_shapes(pbs):
        return [sds((3,) + pb.shape[1:], BF16) for pb in pbs]

    def totals(ps, lands, group, tag):
        fins = [_sum_chips_call(p, l, sm_arr, "grads_sum_chips_" + n) for p, l, (n, _, _) in zip(ps, lands, group)]
        sibs = _exchange_call("grads_rs_share_" + tag, _share_copies, fins, [sds(f.shape, F32) for f in fins], len(fins))
        return {n: (f, s) for (n, _, _), f, s in zip(group, fins, sibs)}

    class StepExchanges(_Exchanges):
        def __init__(self, order):
            shards = my_shards(GROUP_B)
            self.gather = _exchange_start_call("weights_gather_start_b", _direct_gather_copies, shards,
                                               [sds((4,) + s.shape, BF16) for s in shards], 13 * len(shards), order)
            self.red = None

        def token(self):
            return self.gather[4][0:1, 0:1]

        def mlp_weights(self, after):
            return full_weights(_exchange_wait_call("weights_gather_wait_b", _direct_gather_copies, self.gather, after)[1], GROUP_B)

        def mlp_grads(self, gw):
            gs = by_owner(gw, GROUP_B)
            self.step1 = _exchange_start_call("grads_rs_sibling_start_b", _sibling_copies, gs, sibling_shapes(gs), 4 * len(gs))
            return self.step1[4]

        def behind_out_bwd(self, after):
            gs, bufs = _exchange_wait_call("grads_rs_sibling_wait_b", _sibling_copies, self.step1, after)
            self.ps, pbs = chip_sums(gs, bufs, GROUP_B)
            self.step2 = _exchange_start_call("grads_rs_chips_start_b", _chips_copies, pbs, chips_shapes(pbs), 3 * len(pbs))
            return self.step2[4]

        def behind_attention(self, after):
            _, lands = _exchange_wait_call("grads_rs_chips_wait_b", _chips_copies, self.step2, after)
            self.red = totals(self.ps, lands, GROUP_B, "b")

    gathered = _gather_list_call(my_shards(GROUP_A) + [conv_w[0].reshape(2, 1, 3 * F2 // 8)], "a")
    full = full_weights(gathered[:-1], GROUP_A)
    ex = StepExchanges(gathered[-1])
    full["conv_w"] = gathered[-1].reshape(4, 3, F2 // 4).transpose(1, 0, 2).reshape(3, F2)
    small = {n: args[n].reshape(1, d) for n, d in SMALL}
    small["attn_norm_w"] = small["attn_norm_w"] + ex.token()

    loss, gx, gw, gs = _local_step(x[0], positions[0], loss_target[0], full, small, ex)

    ga = by_owner(gw, GROUP_A)
    bufs = _exchange_call("grads_rs_sibling_a", _sibling_copies, ga, sibling_shapes(ga), 4 * len(ga))
    ps, pbs = chip_sums(ga, bufs, GROUP_A)
    step2 = _exchange_start_call("grads_rs_chips_start_a", _chips_copies, pbs, chips_shapes(pbs), 3 * len(pbs))
    early, last = {}, step2[4]
    for n, _, _ in GROUP_B:
        wmv = [args[k + n][0] for k in ("", "m_", "v_")]
        early[n] = _adamw_halves_call(wmv[0], *ex.red[n], c_arr, wmv[1], wmv[2], "adamw_" + n, order=last)
        last = early[n][1]
    _, lands = _exchange_wait_call("grads_rs_chips_wait_a", _chips_copies, step2, last)
    halves = totals(ps, lands, GROUP_A, "a")

    vec = jnp.concatenate([gs[n].reshape(-1) for n, _ in SMALL] + [gw["conv_w"].reshape(-1), loss.reshape(-1)])
    tot = _all_reduce8_call(_pad_rows(vec, 216), "small_all_reduce").reshape(-1)
    red, off = {}, 0
    for n, d in SMALL:
        red[n] = tot[off:off + d].reshape(1, d)
        off += d
    red["conv_w"] = lax.dynamic_slice(tot[off:off + 3 * F2].reshape(3, F2), (0, sm * (F2 // 4)), (3, F2 // 4))
    loss_tot = tot[off + 3 * F2]

    grads, deltas, new_m, new_v = [], [], [], []
    for n in WEIGHT_ORDER:
        shape = args[n].shape
        two_d = (1, shape[0]) if len(shape) == 1 else shape[-2:]
        wmv = [args[k + n].reshape(two_d) for k in ("", "m_", "v_")]
        if n in early:
            g, d, nm, nv = early[n]
        elif n in halves and two_d[1] % LANES:
            tr = lambda a: a.T
            g, d, nm, nv = map(tr, _adamw_halves_call(tr(wmv[0]), *map(tr, halves[n]), c_arr, tr(wmv[1]), tr(wmv[2]),
                                                      "adamw_" + n, transposed=True))
        elif n in halves:
            g, d, nm, nv = _adamw_halves_call(wmv[0], *halves[n], c_arr, wmv[1], wmv[2], "adamw_" + n)
        else:
            g = red[n].reshape(two_d)
            d, nm, nv = _adamw_call(wmv[0], g, wmv[1], wmv[2], "adamw_" + n)
        grads.append(g.reshape(shape))
        deltas.append(d.reshape(shape))
        new_m.append(nm.reshape(shape))
        new_v.append(nv.reshape(shape))
    return (loss_tot, gx[None], *grads, *deltas, *new_m, *new_v)
```
